```python
import jax, jax.numpy as jnp
from jax import lax
import numpy as np

D_MODEL = 1024
BATCH = 8
SEQ = 4096
DEPTH = 2

POOL_GROUPS = 4
POOL_WINDOWS = (2, 4, 8, 16)
POOL_WIDTH = 3 * D_MODEL // 8
POOL_GROUP_DIM = POOL_WIDTH // POOL_GROUPS
CONV_WIDTH = 3 * D_MODEL // 8
HEAD_DIM = 64
DIL_CONFIGS = ((128, 1), (512, 4), (2048, 16))
HEADS_PER_GROUP = 4
ATTN_HEADS = HEADS_PER_GROUP * len(DIL_CONFIGS)
ATTN_WIDTH = ATTN_HEADS * HEAD_DIM
ATTN_OUT_WIDTH = HEADS_PER_GROUP * HEAD_DIM
ROT_DIM = HEAD_DIM // 4
ROPE_THETA = 500000.0
QUERY_BLOCK = 128
N_BRANCHES = 3
IN_SPLITS = (POOL_WIDTH, CONV_WIDTH, CONV_WIDTH, CONV_WIDTH, ATTN_WIDTH, ATTN_WIDTH, ATTN_WIDTH, N_BRANCHES * D_MODEL)
IN_WIDTH = POOL_WIDTH + 3 * CONV_WIDTH + 3 * ATTN_WIDTH + N_BRANCHES * D_MODEL
MEM_LEN = 256
MEM_HEADS = 4
MEM_HEAD_DIM = D_MODEL // 8
MEM_WIDTH = MEM_HEADS * MEM_HEAD_DIM
D_FF = ((8 * D_MODEL // 3 + 127) // 128) * 128
CONV_K = 3
RMS_EPS = 1e-6

kernel_name = "hybrid_gated_pool_conv_dilattn_block"


def rmsnorm(x, g):
    xf = x.astype(jnp.float32)
    y = xf * lax.rsqrt(jnp.mean(xf * xf, axis=-1, keepdims=True) + RMS_EPS)
    return (y * g.astype(jnp.float32)).astype(x.dtype)


def split_cols(u, sizes):
    outs, start = [], 0
    for s in sizes:
        outs.append(u[..., start:start + s])
        start += s
    return outs


def causal_dwconv(u, w):
    k, c = w.shape
    return lax.conv_general_dilated(
        u, w[:, None, :].astype(u.dtype), window_strides=(1,), padding=((k - 1, 0),),
        dimension_numbers=("NWC", "WIO", "NWC"), feature_group_count=c)


def rope_tables(positions):
    inv = ROPE_THETA ** (-jnp.arange(0, ROT_DIM, 2, dtype=jnp.float32) / ROT_DIM)
    ang = positions.astype(jnp.float32)[..., None] * inv
    return jnp.cos(ang)[:, :, None, :], jnp.sin(ang)[:, :, None, :]


def apply_partial_rope(u, cos, sin):
    half = ROT_DIM // 2
    uf = u[..., :ROT_DIM].astype(jnp.float32)
    u1, u2 = uf[..., :half], uf[..., half:]
    rot = jnp.concatenate([u1 * cos - u2 * sin, u2 * cos + u1 * sin], axis=-1).astype(u.dtype)
    return jnp.concatenate([rot, u[..., ROT_DIM:]], axis=-1)


def multiscale_pool(u, pool_w, pool_scale):
    b, s, _ = u.shape
    ug = u.reshape(b, s, POOL_GROUPS, POOL_GROUP_DIM).astype(jnp.float32)
    cs = jnp.cumsum(ug, axis=1)
    t = jnp.arange(s)
    outs = []
    for g, w in enumerate(POOL_WINDOWS):
        c = cs[:, :, g]
        prev = jnp.pad(c, ((0, 0), (w, 0), (0, 0)))[:, :s]
        cnt = jnp.minimum(t + 1, w).astype(jnp.float32)[None, :, None]
        outs.append((c - prev) / cnt - ug[:, :, g])
    pooled = jnp.stack(outs, axis=2)
    mixed = jnp.einsum("bsgc,gcd->bsgd", pooled, pool_w.astype(jnp.float32))
    return (mixed.reshape(b, s, POOL_WIDTH) * pool_scale.astype(jnp.float32)).astype(u.dtype)


def dilated_attention(q, k, v):
    b, s, _, hd = q.shape
    ng = len(DIL_CONFIGS)
    def grp(u):
        return u.reshape(b, s, ng, HEADS_PER_GROUP, hd).transpose(2, 0, 3, 1, 4)
    qg, kg, vg = grp(q), grp(k), grp(v)
    scale = hd ** -0.5

    def block(bi):
        start = bi * QUERY_BLOCK
        t = start + jnp.arange(QUERY_BLOCK)
        qb = lax.dynamic_slice_in_dim(qg, start, QUERY_BLOCK, axis=3)
        outs, lses = [], []
        for g, (window, dil) in enumerate(DIL_CONFIGS):
            offs = jnp.arange(window // dil + 1) * dil
            idx = t[:, None] - offs[None, :]
            valid = idx >= 0
            idx = jnp.maximum(idx, 0)
            kk = jnp.take(kg[g], idx, axis=2)
            vv = jnp.take(vg[g], idx, axis=2)
            sc = jnp.einsum("bhqd,bhqkd->bhqk", qb[g], kk).astype(jnp.float32) * scale
            sc = jnp.where(valid, sc, -jnp.inf)
            m = jnp.max(sc, axis=-1, keepdims=True)
            p = jnp.exp(sc - m)
            den = jnp.sum(p, axis=-1, keepdims=True)
            o = jnp.einsum("bhqk,bhqkd->bhqd", p, vv.astype(jnp.float32)) / den
            outs.append(o)
            lses.append(m + jnp.log(den))
        wts = jax.nn.softmax(jnp.stack(lses, axis=0), axis=0)
        return jnp.sum(wts * jnp.stack(outs, axis=0), axis=0).astype(q.dtype)

    out = lax.map(block, jnp.arange(s // QUERY_BLOCK))
    return out.transpose(1, 0, 3, 2, 4).reshape(b, s, ATTN_OUT_WIDTH)


def memory_cross_attention(h, mem_n, w_q, w_kv, w_o):
    b, s, _ = h.shape
    q = (h @ w_q).reshape(b, s, MEM_HEADS, MEM_HEAD_DIM)
    kv = mem_n @ w_kv
    k = kv[..., :MEM_WIDTH].reshape(b, -1, MEM_HEADS, MEM_HEAD_DIM)
    v = kv[..., MEM_WIDTH:].reshape(b, -1, MEM_HEADS, MEM_HEAD_DIM)
    sc = jnp.einsum("bshd,bmhd->bhsm", q, k).astype(jnp.float32) * (MEM_HEAD_DIM ** -0.5)
    p = jax.nn.softmax(sc, axis=-1)
    o = jnp.einsum("bhsm,bmhd->bshd", p, v.astype(jnp.float32)).astype(h.dtype)
    return o.reshape(b, s, MEM_WIDTH) @ w_o


def _fwd_setup_inputs(seed: int = 0) -> dict:
    key = jax.random.key(seed)
    ks = jax.random.split(key, 32)
    f32 = jnp.float32

    def nrm(k, shape, fan_in):
        return jax.random.normal(k, shape, f32) * (fan_in ** -0.5)

    def gain(k, n):
        return 1.0 + 0.05 * jax.random.normal(k, (DEPTH, n), f32)

    return {
        "x": jax.random.normal(ks[0], (BATCH, SEQ, D_MODEL), f32),
        "mem": jax.random.normal(ks[1], (BATCH, MEM_LEN, D_MODEL), f32),
        "positions": jnp.broadcast_to(jnp.arange(SEQ, dtype=jnp.int32), (BATCH, SEQ)),
        "norm_mix_pre": gain(ks[2], D_MODEL),
        "norm_mix_post": gain(ks[3], D_MODEL),
        "w_in": nrm(ks[4], (DEPTH, D_MODEL, IN_WIDTH), D_MODEL),
        "pool_w": nrm(ks[5], (DEPTH, POOL_GROUPS, POOL_GROUP_DIM, POOL_GROUP_DIM), POOL_GROUP_DIM),
        "pool_scale": 1.0 + 0.1 * jax.random.normal(ks[6], (DEPTH, POOL_WIDTH), f32),
        "conv_b_w": nrm(ks[7], (DEPTH, CONV_K, CONV_WIDTH), CONV_K),
        "w_branch_a": nrm(ks[8], (DEPTH, POOL_WIDTH, D_MODEL), POOL_WIDTH),
        "w_branch_b": nrm(ks[9], (DEPTH, CONV_WIDTH, D_MODEL), CONV_WIDTH),
        "w_branch_c": nrm(ks[10], (DEPTH, ATTN_OUT_WIDTH, D_MODEL), ATTN_OUT_WIDTH),
        "w_out": nrm(ks[11], (DEPTH, D_MODEL, D_MODEL), D_MODEL),
        "norm_mem_pre": gain(ks[12], D_MODEL),
        "norm_mem_post": gain(ks[13], D_MODEL),
        "norm_memkv": gain(ks[14], D_MODEL),
        "w_mq": nrm(ks[15], (DEPTH, D_MODEL, MEM_WIDTH), D_MODEL),
        "w_mkv": nrm(ks[16], (DEPTH, D_MODEL, 2 * MEM_WIDTH), D_MODEL),
        "w_mo": nrm(ks[17], (DEPTH, MEM_WIDTH, D_MODEL), MEM_WIDTH),
        "norm_ffn_pre": gain(ks[18], D_MODEL),
        "norm_ffn_post": gain(ks[19], D_MODEL),
        "w_up": nrm(ks[20], (DEPTH, D_MODEL, 2 * D_FF), D_MODEL),
        "conv_ffn_w": nrm(ks[21], (DEPTH, CONV_K, D_FF), CONV_K),
        "w_down": nrm(ks[22], (DEPTH, D_FF, D_MODEL), D_FF),
    }


def _fwd_reference(x, mem, positions, norm_mix_pre, norm_mix_post, w_in, pool_w, pool_scale, conv_b_w,
              w_branch_a, w_branch_b, w_branch_c, w_out, norm_mem_pre, norm_mem_post, norm_memkv,
              w_mq, w_mkv, w_mo, norm_ffn_pre, norm_ffn_post, w_up, conv_ffn_w, w_down):
    b, s, d = x.shape
    cos, sin = rope_tables(positions)
    cos, sin = cos.astype(x.dtype), sin.astype(x.dtype)
    for l in range(DEPTH):
        h = rmsnorm(x, norm_mix_pre[l])
        a_in, b_x, b_b, b_c, q, k, v, gate_in = split_cols(h @ w_in[l], IN_SPLITS)
        br_a = multiscale_pool(a_in, pool_w[l], pool_scale[l]) @ w_branch_a[l]
        br_b = (b_b * causal_dwconv(b_c * b_x, conv_b_w[l])) @ w_branch_b[l]
        q = apply_partial_rope(q.reshape(b, s, ATTN_HEADS, HEAD_DIM), cos, sin)
        k = apply_partial_rope(k.reshape(b, s, ATTN_HEADS, HEAD_DIM), cos, sin)
        v = v.reshape(b, s, ATTN_HEADS, HEAD_DIM)
        br_c = dilated_attention(q, k, v) @ w_branch_c[l]
        gates = jax.nn.sigmoid(gate_in.astype(jnp.float32)).astype(x.dtype).reshape(b, s, N_BRANCHES, d)
        merged = gates[:, :, 0] * br_a + gates[:, :, 1] * br_b + gates[:, :, 2] * br_c
        x = x + rmsnorm(merged @ w_out[l], norm_mix_post[l])
        h = rmsnorm(x, norm_mem_pre[l])
        mem_n = rmsnorm(mem, norm_memkv[l])
        x = x + rmsnorm(memory_cross_attention(h, mem_n, w_mq[l], w_mkv[l], w_mo[l]), norm_mem_post[l])
        h = rmsnorm(x, norm_ffn_pre[l])
        u = h @ w_up[l]
        ua, ub = u[..., :D_FF], u[..., D_FF:]
        y = (jax.nn.silu(causal_dwconv(ua, conv_ffn_w[l])) * ub) @ w_down[l]
        x = x + rmsnorm(y, norm_ffn_post[l])
    return x


import jax as _jax
import jax.numpy as _jnp

TWIN_FORMAT = 'train_step'
FWD_PARAMS = ['x', 'mem', 'positions', 'norm_mix_pre', 'norm_mix_post', 'w_in', 'pool_w', 'pool_scale', 'conv_b_w', 'w_branch_a', 'w_branch_b', 'w_branch_c', 'w_out', 'norm_mem_pre', 'norm_mem_post', 'norm_memkv', 'w_mq', 'w_mkv', 'w_mo', 'norm_ffn_pre', 'norm_ffn_post', 'w_up', 'conv_ffn_w', 'w_down']
TWIN_WEIGHTS = ['norm_mix_pre', 'norm_mix_post', 'w_in', 'pool_w', 'pool_scale', 'conv_b_w', 'w_branch_a', 'w_branch_b', 'w_branch_c', 'w_out', 'norm_mem_pre', 'norm_mem_post', 'norm_memkv', 'w_mq', 'w_mkv', 'w_mo', 'norm_ffn_pre', 'norm_ffn_post', 'w_up', 'conv_ffn_w', 'w_down']
TWIN_DIFF_INPUT = 'x'
TWIN_INPUTS = ['x', 'mem', 'positions', 'norm_mix_pre', 'norm_mix_post', 'w_in', 'pool_w', 'pool_scale', 'conv_b_w', 'w_branch_a', 'w_branch_b', 'w_branch_c', 'w_out', 'norm_mem_pre', 'norm_mem_post', 'norm_memkv', 'w_mq', 'w_mkv', 'w_mo', 'norm_ffn_pre', 'norm_ffn_post', 'w_up', 'conv_ffn_w', 'w_down', 'loss_target', 'm_norm_mix_pre', 'm_norm_mix_post', 'm_w_in', 'm_pool_w', 'm_pool_scale', 'm_conv_b_w', 'm_w_branch_a', 'm_w_branch_b', 'm_w_branch_c', 'm_w_out', 'm_norm_mem_pre', 'm_norm_mem_post', 'm_norm_memkv', 'm_w_mq', 'm_w_mkv', 'm_w_mo', 'm_norm_ffn_pre', 'm_norm_ffn_post', 'm_w_up', 'm_conv_ffn_w', 'm_w_down', 'v_norm_mix_pre', 'v_norm_mix_post', 'v_w_in', 'v_pool_w', 'v_pool_scale', 'v_conv_b_w', 'v_w_branch_a', 'v_w_branch_b', 'v_w_branch_c', 'v_w_out', 'v_norm_mem_pre', 'v_norm_mem_post', 'v_norm_memkv', 'v_w_mq', 'v_w_mkv', 'v_w_mo', 'v_norm_ffn_pre', 'v_norm_ffn_post', 'v_w_up', 'v_conv_ffn_w', 'v_w_down']
TWIN_OUTPUTS = ['loss', 'grad_x', 'grad_norm_mix_pre', 'grad_norm_mix_post', 'grad_w_in', 'grad_pool_w', 'grad_pool_scale', 'grad_conv_b_w', 'grad_w_branch_a', 'grad_w_branch_b', 'grad_w_branch_c', 'grad_w_out', 'grad_norm_mem_pre', 'grad_norm_mem_post', 'grad_norm_memkv', 'grad_w_mq', 'grad_w_mkv', 'grad_w_mo', 'grad_norm_ffn_pre', 'grad_norm_ffn_post', 'grad_w_up', 'grad_conv_ffn_w', 'grad_w_down', 'delta_norm_mix_pre', 'delta_norm_mix_post', 'delta_w_in', 'delta_pool_w', 'delta_pool_scale', 'delta_conv_b_w', 'delta_w_branch_a', 'delta_w_branch_b', 'delta_w_branch_c', 'delta_w_out', 'delta_norm_mem_pre', 'delta_norm_mem_post', 'delta_norm_memkv', 'delta_w_mq', 'delta_w_mkv', 'delta_w_mo', 'delta_norm_ffn_pre', 'delta_norm_ffn_post', 'delta_w_up', 'delta_conv_ffn_w', 'delta_w_down', 'new_m_norm_mix_pre', 'new_m_norm_mix_post', 'new_m_w_in', 'new_m_pool_w', 'new_m_pool_scale', 'new_m_conv_b_w', 'new_m_w_branch_a', 'new_m_w_branch_b', 'new_m_w_branch_c', 'new_m_w_out', 'new_m_norm_mem_pre', 'new_m_norm_mem_post', 'new_m_norm_memkv', 'new_m_w_mq', 'new_m_w_mkv', 'new_m_w_mo', 'new_m_norm_ffn_pre', 'new_m_norm_ffn_post', 'new_m_w_up', 'new_m_conv_ffn_w', 'new_m_w_down', 'new_v_norm_mix_pre', 'new_v_norm_mix_post', 'new_v_w_in', 'new_v_pool_w', 'new_v_pool_scale', 'new_v_conv_b_w', 'new_v_w_branch_a', 'new_v_w_branch_b', 'new_v_w_branch_c', 'new_v_w_out', 'new_v_norm_mem_pre', 'new_v_norm_mem_post', 'new_v_norm_memkv', 'new_v_w_mq', 'new_v_w_mkv', 'new_v_w_mo', 'new_v_norm_ffn_pre', 'new_v_norm_ffn_post', 'new_v_w_up', 'new_v_conv_ffn_w', 'new_v_w_down']
TWIN_LEAF_KINDS = {'loss': 'loss', 'grad_x': 'grad_x', 'grad_norm_mix_pre': 'grad_w', 'grad_norm_mix_post': 'grad_w', 'grad_w_in': 'grad_w', 'grad_pool_w': 'grad_w', 'grad_pool_scale': 'grad_w', 'grad_conv_b_w': 'grad_w', 'grad_w_branch_a': 'grad_w', 'grad_w_branch_b': 'grad_w', 'grad_w_branch_c': 'grad_w', 'grad_w_out': 'grad_w', 'grad_norm_mem_pre': 'grad_w', 'grad_norm_mem_post': 'grad_w', 'grad_norm_memkv': 'grad_w', 'grad_w_mq': 'grad_w', 'grad_w_mkv': 'grad_w', 'grad_w_mo': 'grad_w', 'grad_norm_ffn_pre': 'grad_w', 'grad_norm_ffn_post': 'grad_w', 'grad_w_up': 'grad_w', 'grad_conv_ffn_w': 'grad_w', 'grad_w_down': 'grad_w', 'delta_norm_mix_pre': 'delta_w', 'delta_norm_mix_post': 'delta_w', 'delta_w_in': 'delta_w', 'delta_pool_w': 'delta_w', 'delta_pool_scale': 'delta_w', 'delta_conv_b_w': 'delta_w', 'delta_w_branch_a': 'delta_w', 'delta_w_branch_b': 'delta_w', 'delta_w_branch_c': 'delta_w', 'delta_w_out': 'delta_w', 'delta_norm_mem_pre': 'delta_w', 'delta_norm_mem_post': 'delta_w', 'delta_norm_memkv': 'delta_w', 'delta_w_mq': 'delta_w', 'delta_w_mkv': 'delta_w', 'delta_w_mo': 'delta_w', 'delta_norm_ffn_pre': 'delta_w', 'delta_norm_ffn_post': 'delta_w', 'delta_w_up': 'delta_w', 'delta_conv_ffn_w': 'delta_w', 'delta_w_down': 'delta_w', 'new_m_norm_mix_pre': 'new_m', 'new_m_norm_mix_post': 'new_m', 'new_m_w_in': 'new_m', 'new_m_pool_w': 'new_m', 'new_m_pool_scale': 'new_m', 'new_m_conv_b_w': 'new_m', 'new_m_w_branch_a': 'new_m', 'new_m_w_branch_b': 'new_m', 'new_m_w_branch_c': 'new_m', 'new_m_w_out': 'new_m', 'new_m_norm_mem_pre': 'new_m', 'new_m_norm_mem_post': 'new_m', 'new_m_norm_memkv': 'new_m', 'new_m_w_mq': 'new_m', 'new_m_w_mkv': 'new_m', 'new_m_w_mo': 'new_m', 'new_m_norm_ffn_pre': 'new_m', 'new_m_norm_ffn_post': 'new_m', 'new_m_w_up': 'new_m', 'new_m_conv_ffn_w': 'new_m', 'new_m_w_down': 'new_m', 'new_v_norm_mix_pre': 'new_v', 'new_v_norm_mix_post': 'new_v', 'new_v_w_in': 'new_v', 'new_v_pool_w': 'new_v', 'new_v_pool_scale': 'new_v', 'new_v_conv_b_w': 'new_v', 'new_v_w_branch_a': 'new_v', 'new_v_w_branch_b': 'new_v', 'new_v_w_branch_c': 'new_v', 'new_v_w_out': 'new_v', 'new_v_norm_mem_pre': 'new_v', 'new_v_norm_mem_post': 'new_v', 'new_v_norm_memkv': 'new_v', 'new_v_w_mq': 'new_v', 'new_v_w_mkv': 'new_v', 'new_v_w_mo': 'new_v', 'new_v_norm_ffn_pre': 'new_v', 'new_v_norm_ffn_post': 'new_v', 'new_v_w_up': 'new_v', 'new_v_conv_ffn_w': 'new_v', 'new_v_w_down': 'new_v'}


def _forward(args):
    return _fwd_reference(*[args[k] for k in FWD_PARAMS])


def _output_shape():
    out = _jax.eval_shape(lambda: _forward(_fwd_setup_inputs(0)))
    return out.shape, out.dtype

N_MICROBATCH = 1
ADAM_LR = 0.001
ADAM_B1 = 0.9
ADAM_B2 = 0.999
ADAM_EPS = 1e-08
ADAM_WD = 0.01
ADAM_STEP = 10
PER_EXAMPLE_BATCH_AXIS = {'x': 0, 'mem': 0, 'positions': 0, 'loss_target': 0}
SHARED_INPUTS = []
_WEIGHT_DTYPES = {'norm_mix_pre': _jnp.float32, 'norm_mix_post': _jnp.float32, 'w_in': _jnp.float32, 'pool_w': _jnp.float32, 'pool_scale': _jnp.float32, 'conv_b_w': _jnp.float32, 'w_branch_a': _jnp.float32, 'w_branch_b': _jnp.float32, 'w_branch_c': _jnp.float32, 'w_out': _jnp.float32, 'norm_mem_pre': _jnp.float32, 'norm_mem_post': _jnp.float32, 'norm_memkv': _jnp.float32, 'w_mq': _jnp.float32, 'w_mkv': _jnp.float32, 'w_mo': _jnp.float32, 'norm_ffn_pre': _jnp.float32, 'norm_ffn_post': _jnp.float32, 'w_up': _jnp.float32, 'conv_ffn_w': _jnp.float32, 'w_down': _jnp.float32}
MOMENT_SCALE = {'norm_mix_pre': 1.675697e+00, 'norm_mix_post': 3.167433e+01, 'w_in': 6.699979e-01, 'pool_w': 1.524380e+00, 'pool_scale': 1.710631e+00, 'conv_b_w': 1.285636e+00, 'w_branch_a': 9.867390e-01, 'w_branch_b': 8.070292e-01, 'w_branch_c': 9.296083e-01, 'w_out': 1.567125e+00, 'norm_mem_pre': 1.307435e+00, 'norm_mem_post': 3.387587e+01, 'norm_memkv': 6.296362e+00, 'w_mq': 1.941046e+00, 'w_mkv': 6.494498e+00, 'w_mo': 6.728586e+00, 'norm_ffn_pre': 2.476644e+00, 'norm_ffn_post': 3.182850e+01, 'w_up': 9.948100e-01, 'conv_ffn_w': 1.083537e+00, 'w_down': 2.117764e+00}


def _to_microbatches(a, axis):
    t = _jnp.moveaxis(a, axis, 0)
    t = t.reshape((N_MICROBATCH, t.shape[0] // N_MICROBATCH) + t.shape[1:])
    return _jnp.moveaxis(t, 1, axis + 1)


def setup_inputs(seed: int = 0) -> dict:
    inp = _fwd_setup_inputs(seed)
    key = _jax.random.fold_in(_jax.random.key(seed), 7919)
    shape, _ = _output_shape()
    out = dict(inp)
    out["loss_target"] = _jax.random.normal(_jax.random.fold_in(key, 0), shape, _jnp.float32)
    for i, name in enumerate(TWIN_WEIGHTS):
        w = inp[name].astype(_jnp.float32)
        if MOMENT_SCALE is None:
            s = _jnp.sqrt(_jnp.mean(_jnp.square(w)) + 1e-30)
        else:
            s = MOMENT_SCALE[name]
        km, kv = _jax.random.split(_jax.random.fold_in(key, i + 1))
        out[name] = w
        out["m_" + name] = s * _jax.random.normal(km, w.shape, _jnp.float32)
        out["v_" + name] = (s * s) * _jax.random.uniform(kv, w.shape, _jnp.float32, 0.5, 1.5)
    if N_MICROBATCH > 1:
        for name, axis in PER_EXAMPLE_BATCH_AXIS.items():
            out[name] = _to_microbatches(out[name], axis)
    return {'x': out['x'], 'mem': out['mem'], 'positions': out['positions'], 'norm_mix_pre': out['norm_mix_pre'], 'norm_mix_post': out['norm_mix_post'], 'w_in': out['w_in'], 'pool_w': out['pool_w'], 'pool_scale': out['pool_scale'], 'conv_b_w': out['conv_b_w'], 'w_branch_a': out['w_branch_a'], 'w_branch_b': out['w_branch_b'], 'w_branch_c': out['w_branch_c'], 'w_out': out['w_out'], 'norm_mem_pre': out['norm_mem_pre'], 'norm_mem_post': out['norm_mem_post'], 'norm_memkv': out['norm_memkv'], 'w_mq': out['w_mq'], 'w_mkv': out['w_mkv'], 'w_mo': out['w_mo'], 'norm_ffn_pre': out['norm_ffn_pre'], 'norm_ffn_post': out['norm_ffn_post'], 'w_up': out['w_up'], 'conv_ffn_w': out['conv_ffn_w'], 'w_down': out['w_down'], 'loss_target': out['loss_target'], 'm_norm_mix_pre': out['m_norm_mix_pre'], 'm_norm_mix_post': out['m_norm_mix_post'], 'm_w_in': out['m_w_in'], 'm_pool_w': out['m_pool_w'], 'm_pool_scale': out['m_pool_scale'], 'm_conv_b_w': out['m_conv_b_w'], 'm_w_branch_a': out['m_w_branch_a'], 'm_w_branch_b': out['m_w_branch_b'], 'm_w_branch_c': out['m_w_branch_c'], 'm_w_out': out['m_w_out'], 'm_norm_mem_pre': out['m_norm_mem_pre'], 'm_norm_mem_post': out['m_norm_mem_post'], 'm_norm_memkv': out['m_norm_memkv'], 'm_w_mq': out['m_w_mq'], 'm_w_mkv': out['m_w_mkv'], 'm_w_mo': out['m_w_mo'], 'm_norm_ffn_pre': out['m_norm_ffn_pre'], 'm_norm_ffn_post': out['m_norm_ffn_post'], 'm_w_up': out['m_w_up'], 'm_conv_ffn_w': out['m_conv_ffn_w'], 'm_w_down': out['m_w_down'], 'v_norm_mix_pre': out['v_norm_mix_pre'], 'v_norm_mix_post': out['v_norm_mix_post'], 'v_w_in': out['v_w_in'], 'v_pool_w': out['v_pool_w'], 'v_pool_scale': out['v_pool_scale'], 'v_conv_b_w': out['v_conv_b_w'], 'v_w_branch_a': out['v_w_branch_a'], 'v_w_branch_b': out['v_w_branch_b'], 'v_w_branch_c': out['v_w_branch_c'], 'v_w_out': out['v_w_out'], 'v_norm_mem_pre': out['v_norm_mem_pre'], 'v_norm_mem_post': out['v_norm_mem_post'], 'v_norm_memkv': out['v_norm_memkv'], 'v_w_mq': out['v_w_mq'], 'v_w_mkv': out['v_w_mkv'], 'v_w_mo': out['v_w_mo'], 'v_norm_ffn_pre': out['v_norm_ffn_pre'], 'v_norm_ffn_post': out['v_norm_ffn_post'], 'v_w_up': out['v_w_up'], 'v_conv_ffn_w': out['v_conv_ffn_w'], 'v_w_down': out['v_w_down']}


def _loss(weights, diff, rest, loss_target):
    with _jax.named_scope("forward"):
        args = {**rest, TWIN_DIFF_INPUT: diff, **{k: w.astype(_WEIGHT_DTYPES[k]) for k, w in weights.items()}}
        y = _forward(args)
    with _jax.named_scope("loss_head"):
        err = _jnp.square(y.astype(_jnp.float32) - loss_target)
        return 0.5 * _jnp.sum(_jnp.mean(err, axis=-1)) if err.ndim else 0.5 * err


def _adamw(w, g, m, v):
    m = ADAM_B1 * m + (1.0 - ADAM_B1) * g
    v = ADAM_B2 * v + (1.0 - ADAM_B2) * _jnp.square(g)
    m_hat = m / (1.0 - ADAM_B1 ** ADAM_STEP)
    v_hat = v / (1.0 - ADAM_B2 ** ADAM_STEP)
    delta = -ADAM_LR * (m_hat / (_jnp.sqrt(v_hat) + ADAM_EPS) + ADAM_WD * w)
    return delta, m, v


def reference(x, mem, positions, norm_mix_pre, norm_mix_post, w_in, pool_w, pool_scale, conv_b_w, w_branch_a, w_branch_b, w_branch_c, w_out, norm_mem_pre, norm_mem_post, norm_memkv, w_mq, w_mkv, w_mo, norm_ffn_pre, norm_ffn_post, w_up, conv_ffn_w, w_down, loss_target, m_norm_mix_pre, m_norm_mix_post, m_w_in, m_pool_w, m_pool_scale, m_conv_b_w, m_w_branch_a, m_w_branch_b, m_w_branch_c, m_w_out, m_norm_mem_pre, m_norm_mem_post, m_norm_memkv, m_w_mq, m_w_mkv, m_w_mo, m_norm_ffn_pre, m_norm_ffn_post, m_w_up, m_conv_ffn_w, m_w_down, v_norm_mix_pre, v_norm_mix_post, v_w_in, v_pool_w, v_pool_scale, v_conv_b_w, v_w_branch_a, v_w_branch_b, v_w_branch_c, v_w_out, v_norm_mem_pre, v_norm_mem_post, v_norm_memkv, v_w_mq, v_w_mkv, v_w_mo, v_norm_ffn_pre, v_norm_ffn_post, v_w_up, v_conv_ffn_w, v_w_down):
    given = dict(x=x, mem=mem, positions=positions, norm_mix_pre=norm_mix_pre, norm_mix_post=norm_mix_post, w_in=w_in, pool_w=pool_w, pool_scale=pool_scale, conv_b_w=conv_b_w, w_branch_a=w_branch_a, w_branch_b=w_branch_b, w_branch_c=w_branch_c, w_out=w_out, norm_mem_pre=norm_mem_pre, norm_mem_post=norm_mem_post, norm_memkv=norm_memkv, w_mq=w_mq, w_mkv=w_mkv, w_mo=w_mo, norm_ffn_pre=norm_ffn_pre, norm_ffn_post=norm_ffn_post, w_up=w_up, conv_ffn_w=conv_ffn_w, w_down=w_down, loss_target=loss_target, m_norm_mix_pre=m_norm_mix_pre, m_norm_mix_post=m_norm_mix_post, m_w_in=m_w_in, m_pool_w=m_pool_w, m_pool_scale=m_pool_scale, m_conv_b_w=m_conv_b_w, m_w_branch_a=m_w_branch_a, m_w_branch_b=m_w_branch_b, m_w_branch_c=m_w_branch_c, m_w_out=m_w_out, m_norm_mem_pre=m_norm_mem_pre, m_norm_mem_post=m_norm_mem_post, m_norm_memkv=m_norm_memkv, m_w_mq=m_w_mq, m_w_mkv=m_w_mkv, m_w_mo=m_w_mo, m_norm_ffn_pre=m_norm_ffn_pre, m_norm_ffn_post=m_norm_ffn_post, m_w_up=m_w_up, m_conv_ffn_w=m_conv_ffn_w, m_w_down=m_w_down, v_norm_mix_pre=v_norm_mix_pre, v_norm_mix_post=v_norm_mix_post, v_w_in=v_w_in, v_pool_w=v_pool_w, v_pool_scale=v_pool_scale, v_conv_b_w=v_conv_b_w, v_w_branch_a=v_w_branch_a, v_w_branch_b=v_w_branch_b, v_w_branch_c=v_w_branch_c, v_w_out=v_w_out, v_norm_mem_pre=v_norm_mem_pre, v_norm_mem_post=v_norm_mem_post, v_norm_memkv=v_norm_memkv, v_w_mq=v_w_mq, v_w_mkv=v_w_mkv, v_w_mo=v_w_mo, v_norm_ffn_pre=v_norm_ffn_pre, v_norm_ffn_post=v_norm_ffn_post, v_w_up=v_w_up, v_conv_ffn_w=v_conv_ffn_w, v_w_down=v_w_down)
    weights = {n: given[n] for n in TWIN_WEIGHTS}
    shared = {n: given[n] for n in SHARED_INPUTS}
    per_example = {n: given[n] for n in ['x', 'mem', 'positions']}
    grad_fn = _jax.value_and_grad(_loss, argnums=(0, 1))

    def one_microbatch(ex, loss_target):
        ex = dict(ex)
        diff = ex.pop(TWIN_DIFF_INPUT)
        return grad_fn(weights, diff, {**shared, **ex}, loss_target)

    if N_MICROBATCH == 1:
        loss, (grad_w, grad_x) = one_microbatch(per_example, given["loss_target"])
    else:
        def body(carry, xs):
            loss_sum, grad_sum = carry
            l_k, (gw_k, gx_k) = one_microbatch(xs[0], xs[1])
            with _jax.named_scope("update"):
                return (loss_sum + l_k, _jax.tree.map(_jnp.add, grad_sum, gw_k)), gx_k

        init = (_jnp.zeros((), _jnp.float32), _jax.tree.map(_jnp.zeros_like, weights))
        (loss, grad_w), grad_x = _jax.lax.scan(body, init, (per_example, given["loss_target"]))
    with _jax.named_scope("update"):
        delta_w, new_m, new_v = {}, {}, {}
        for n in TWIN_WEIGHTS:
            delta_w[n], new_m[n], new_v[n] = _adamw(weights[n], grad_w[n], given["m_" + n], given["v_" + n])
    return (loss, grad_x, *[grad_w[n] for n in TWIN_WEIGHTS], *[delta_w[n] for n in TWIN_WEIGHTS],
            *[new_m[n] for n in TWIN_WEIGHTS], *[new_v[n] for n in TWIN_WEIGHTS])
```

```python
import jax
import jax.numpy as jnp
from jax import lax
from jax.experimental import pallas as pl
from jax.experimental.pallas import tpu as pltpu

F32 = jnp.float32
MXU = jnp.bfloat16
HI = lax.Precision.HIGHEST

D = 1024
DEPTH = 2
POOLW = 384
ATT_W = 768
ATT_O = 256
GATE_W = 3 * D
IN_W = 6912
MEM_W = 512
D_FF = 2816
EPS = 1e-6
ROPE_THETA = 500000.0
QB = 128
DILS = (1, 4, 16)
NEG = -1e30
MEM_SCALE = 128 ** -0.5
ATT_SCALE = 0.125

ADAM_LR, ADAM_B1, ADAM_B2, ADAM_EPS, ADAM_WD, ADAM_STEP = 0.001, 0.9, 0.999, 1e-08, 0.01, 10

N_DEV = 8
MESH_AXES = ("x", "y", "c")
LANES = 128
FLAT_ALIGN = 2048
ROW_TILE = 1024

WEIGHTS = ['norm_mix_pre', 'norm_mix_post', 'w_in', 'pool_w', 'pool_scale', 'conv_b_w', 'w_branch_a', 'w_branch_b',
           'w_branch_c', 'w_out', 'norm_mem_pre', 'norm_mem_post', 'norm_memkv', 'w_mq', 'w_mkv', 'w_mo',
           'norm_ffn_pre', 'norm_ffn_post', 'w_up', 'conv_ffn_w', 'w_down']
SHARD_AXIS = {'w_in': 2, 'conv_b_w': 2, 'w_branch_a': 2, 'w_branch_b': 2, 'w_branch_c': 2, 'w_out': 1, 'w_mq': 1,
              'w_mkv': 1, 'w_mo': 2, 'w_up': 2, 'conv_ffn_w': 2, 'w_down': 1}
F32_GATHERED = ('conv_b_w', 'conv_ffn_w')
BIG = [n for n in WEIGHTS if n in SHARD_AXIS and n not in F32_GATHERED]


def _params(sem, vmem_mb):
    return pltpu.CompilerParams(dimension_semantics=sem, vmem_limit_bytes=vmem_mb << 20)


def _dot(a, b, prec=None):
    return lax.dot_general(a, b, (((1,), (0,)), ((), ())), preferred_element_type=F32, precision=prec)


def _dot_nt(a, b, prec=None):
    return lax.dot_general(a, b, (((1,), (1,)), ((), ())), preferred_element_type=F32, precision=prec)


def _dot_tn(a, b, prec=None):
    return lax.dot_general(a, b, (((0,), (0,)), ((), ())), preferred_element_type=F32, precision=prec)


def _tile(n, cap):
    if n <= cap:
        return n
    best = None
    for t in range(LANES, cap + 1, LANES):
        if n % t == 0:
            best = t
    assert best is not None, (n, cap)
    return best


def _rms(x, g):
    r = lax.rsqrt(jnp.mean(x * x, axis=-1, keepdims=True) + EPS)
    return x * r * g, r


def _rms_bwd(w, y):
    r = lax.rsqrt(jnp.mean(y * y, axis=-1, keepdims=True) + EPS)
    return r * w - y * (r * r * r) * jnp.mean(w * y, axis=-1, keepdims=True), r


def _rows_call(name, body, n_rows, ts, ins, outs, scratch=(), reverse=False, vmem_mb=48):
    nt = n_rows // ts
    assert nt * ts == n_rows

    def tile_of(g):
        return (nt - 1 - g) if reverse else g

    in_specs, args = [], []
    for op in ins:
        if op[0] == "t":
            _, a, cw, cb = op
            in_specs.append(pl.BlockSpec((ts, cw), lambda g, cb=cb: (tile_of(g), cb)))
        elif op[0] == "h":
            _, a, hr, cw, cb = op
            in_specs.append(pl.BlockSpec((hr, cw), lambda g, cb=cb, k=ts // hr: (jnp.maximum(tile_of(g) * k - 1, 0), cb)))
        else:
            _, a = op
            in_specs.append(pl.BlockSpec(a.shape, lambda g, n=a.ndim: (0,) * n))
        args.append(a)
    out_specs, out_shape = [], []
    for op in outs:
        if op[0] == "t":
            _, cols, dt = op
            out_specs.append(pl.BlockSpec((ts, cols), lambda g: (tile_of(g), 0)))
            out_shape.append(jax.ShapeDtypeStruct((n_rows, cols), dt))
        else:
            _, shp, dt = op
            out_specs.append(pl.BlockSpec(shp, lambda g, n=len(shp): (0,) * n))
            out_shape.append(jax.ShapeDtypeStruct(shp, dt))

    def kern(*refs):
        g = pl.program_id(0)
        body(tile_of(g), g, *refs)

    return pl.pallas_call(kern, grid=(nt,), in_specs=in_specs, out_specs=out_specs, out_shape=out_shape,
                          scratch_shapes=list(scratch), compiler_params=_params(("arbitrary",), vmem_mb), name=name)(*args)


def _acc(ref, g, val):
    @pl.when(g == 0)
    def _():
        ref[...] = val

    @pl.when(g != 0)
    def _():
        ref[...] += val


def _norm_mm(name, x, g, w, ts, tn, out_dtype=F32, wt=False):
    S, K = x.shape
    N = w.shape[0] if wt else w.shape[1]

    def body(x_ref, g_ref, w_ref, o_ref, h_ref, hs):
        @pl.when(pl.program_id(1) == 0)
        def _():
            h, _ = _rms(x_ref[...], g_ref[...])
            hs[...] = h.astype(MXU)
            h_ref[...] = h.astype(MXU)

        o_ref[...] = (_dot_nt if wt else _dot)(hs[...], w_ref[...]).astype(out_dtype)

    w_spec = pl.BlockSpec((tn, K), lambda i, j: (j, 0)) if wt else pl.BlockSpec((K, tn), lambda i, j: (0, j))
    return pl.pallas_call(
        body, grid=(S // ts, N // tn),
        in_specs=[pl.BlockSpec((ts, K), lambda i, j: (i, 0)), pl.BlockSpec((1, K), lambda i, j: (0, 0)), w_spec],
        out_specs=[pl.BlockSpec((ts, tn), lambda i, j: (i, j)), pl.BlockSpec((ts, K), lambda i, j: (i, 0))],
        out_shape=[jax.ShapeDtypeStruct((S, N), out_dtype), jax.ShapeDtypeStruct((S, K), MXU)],
        scratch_shapes=[pltpu.VMEM((ts, K), MXU)],
        compiler_params=_params(("arbitrary", "arbitrary"), 48), name=name)(x, g, w)


def _mm_nt(name, a, b, ts, tn, out_dtype=F32):
    M, K = a.shape
    N = b.shape[0]

    def body(a_ref, b_ref, o_ref):
        o_ref[...] = _dot_nt(a_ref[...], b_ref[...]).astype(out_dtype)

    return pl.pallas_call(
        body, grid=(M // ts, N // tn),
        in_specs=[pl.BlockSpec((ts, K), lambda i, j: (i, 0)), pl.BlockSpec((tn, K), lambda i, j: (j, 0))],
        out_specs=pl.BlockSpec((ts, tn), lambda i, j: (i, j)), out_shape=jax.ShapeDtypeStruct((M, N), out_dtype),
        compiler_params=_params(("arbitrary", "arbitrary"), 48), name=name)(a, b)


def _mm_nn(name, a, b, ts, tn, out_dtype=F32):
    M, K = a.shape
    N = b.shape[1]

    def body(a_ref, b_ref, o_ref):
        o_ref[...] = _dot(a_ref[...], b_ref[...]).astype(out_dtype)

    return pl.pallas_call(
        body, grid=(M // ts, N // tn),
        in_specs=[pl.BlockSpec((ts, K), lambda i, j: (i, 0)), pl.BlockSpec((K, tn), lambda i, j: (0, j))],
        out_specs=pl.BlockSpec((ts, tn), lambda i, j: (i, j)), out_shape=jax.ShapeDtypeStruct((M, N), out_dtype),
        compiler_params=_params(("arbitrary", "arbitrary"), 48), name=name)(a, b)


def _mm_tn(name, a, b, cap_k=512, cap_n=512):
    S, K = a.shape
    N = b.shape[1]
    tk, tn = _tile(K, cap_k), _tile(N, cap_n)

    def body(a_ref, b_ref, o_ref):
        o_ref[...] = _dot_tn(a_ref[...], b_ref[...])

    return pl.pallas_call(
        body, grid=(K // tk, N // tn),
        in_specs=[pl.BlockSpec((S, tk), lambda i, j: (0, i)), pl.BlockSpec((S, tn), lambda i, j: (0, j))],
        out_specs=pl.BlockSpec((tk, tn), lambda i, j: (i, j)), out_shape=jax.ShapeDtypeStruct((K, N), F32),
        compiler_params=_params(("arbitrary", "arbitrary"), 48), name=name)(a, b)


def _pool_cols(shape):
    col = lax.broadcasted_iota(jnp.int32, shape, 1)
    return col < 96, col < 192, col < 288


def _pool_select(s2, s4, s8, s16):
    c1, c2, c3 = _pool_cols(s2.shape)
    return jnp.where(c1, s2, jnp.where(c2, s4, jnp.where(c3, s8, s16)))


def _pool_cnt(t0, ts):
    c1, c2, c3 = _pool_cols((ts, POOLW))
    win = jnp.where(c1, 2, jnp.where(c2, 4, jnp.where(c3, 8, 16)))
    t = t0 + lax.broadcasted_iota(jnp.int32, (ts, POOLW), 0)
    return jnp.minimum(t + 1, win).astype(F32)


def _pooled(a, prev, t0):
    ts = a.shape[0]
    ext = jnp.concatenate([prev, a], axis=0)
    s2 = ext + pltpu.roll(ext, 1, axis=0)
    s4 = s2 + pltpu.roll(s2, 2, axis=0)
    s8 = s4 + pltpu.roll(s4, 4, axis=0)
    s16 = s8 + pltpu.roll(s8, 8, axis=0)
    sums = _pool_select(s2, s4, s8, s16)[16:]
    return sums / _pool_cnt(t0, ts) - a


def _conv3(z, prev8, w):
    ext = jnp.concatenate([prev8, z], axis=0)
    z1 = pltpu.roll(ext, 1, axis=0)[8:]
    z2 = pltpu.roll(ext, 2, axis=0)[8:]
    return w[0:1] * z2 + w[1:2] * z1 + w[2:3] * z, z1, z2


def _conv3_t(dc, next8, w):
    ts = dc.shape[0]
    ext = jnp.concatenate([dc, next8], axis=0)
    n = ts + 8
    u1 = pltpu.roll(ext, n - 1, axis=0)[:ts]
    u2 = pltpu.roll(ext, n - 2, axis=0)[:ts]
    return w[2:3] * dc + w[1:2] * u1 + w[0:1] * u2


def _poolconv_fwd(u, wblk, pool_scale, conv_b, ts=256):
    S = u.shape[0]

    def body(i, g, a_ref, bx_ref, bb_ref, bc_ref, wblk_ref, ps_ref, cw_ref, a2_ref, yb_ref, ca, cz):
        @pl.when(g == 0)
        def _():
            ca[...] = jnp.zeros_like(ca)
            cz[...] = jnp.zeros_like(cz)

        a = a_ref[...]
        p = _pooled(a, ca[...], i * ts)
        mixed = _dot(p.astype(MXU), wblk_ref[...])
        a2_ref[...] = (mixed * ps_ref[...]).astype(MXU)
        z = bc_ref[...] * bx_ref[...]
        conv, _, _ = _conv3(z, cz[...], cw_ref[...])
        yb_ref[...] = (bb_ref[...] * conv).astype(MXU)
        ca[...] = a[ts - 16:]
        cz[...] = z[ts - 8:]

    ins = [("t", u, POOLW, 8), ("t", u, POOLW, 9), ("t", u, POOLW, 10), ("t", u, POOLW, 11), ("w", wblk), ("w", pool_scale),
           ("w", conv_b)]
    return _rows_call("poolconv_fwd", body, S, ts, ins, [("t", POOLW, MXU), ("t", POOLW, MXU)],
                      scratch=[pltpu.VMEM((16, POOLW), F32), pltpu.VMEM((8, POOLW), F32)])


def _poolconv_bwd(u, d_a2, d_yb, wblk, pool_scale, conv_b, ts=256):
    S = u.shape[0]

    def body(i, g, a_ref, bx_ref, bb_ref, bc_ref, ap_ref, bxp_ref, bcp_ref, da2_ref, dyb_ref, wblk_ref, ps_ref, cw_ref,
             o_ref, dps_ref, dwb_ref, dcw_ref, ce, cdz):
        @pl.when(g == 0)
        def _():
            ce[...] = jnp.zeros_like(ce)
            cdz[...] = jnp.zeros_like(cdz)

        first = (i > 0).astype(F32)
        a = a_ref[...]
        p = _pooled(a, ap_ref[...] * first, i * ts)
        pb = p.astype(MXU)
        mixed = _dot(pb, wblk_ref[...])
        da2 = da2_ref[...]
        dmixed = (da2 * ps_ref[...]).astype(MXU)
        dp = _dot_nt(dmixed, wblk_ref[...])
        _acc(dps_ref, g, jnp.sum(da2 * mixed, axis=0, keepdims=True))
        _acc(dwb_ref, g, _dot_tn(pb, dmixed))
        e = dp / _pool_cnt(i * ts, ts)
        ext = jnp.concatenate([e, ce[...]], axis=0)
        n = ts + 16
        f2 = ext + pltpu.roll(ext, n - 1, axis=0)
        f4 = f2 + pltpu.roll(f2, n - 2, axis=0)
        f8 = f4 + pltpu.roll(f4, n - 4, axis=0)
        f16 = f8 + pltpu.roll(f8, n - 8, axis=0)
        o_ref[:, 0:POOLW] = (_pool_select(f2, f4, f8, f16)[:ts] - dp).astype(o_ref.dtype)
        ce[...] = e[:16]

        bx, bb, bc = bx_ref[...], bb_ref[...], bc_ref[...]
        z = bc * bx
        w = cw_ref[...]
        conv, z1, z2 = _conv3(z, bxp_ref[...] * bcp_ref[...] * first, w)
        dyb = dyb_ref[...]
        dconv = dyb * bb
        dz = _conv3_t(dconv, cdz[...], w)
        o_ref[:, POOLW:2 * POOLW] = (dz * bc).astype(o_ref.dtype)
        o_ref[:, 2 * POOLW:3 * POOLW] = (dyb * conv).astype(o_ref.dtype)
        o_ref[:, 3 * POOLW:4 * POOLW] = (dz * bx).astype(o_ref.dtype)
        dw = jnp.concatenate([jnp.sum(dconv * z2, axis=0, keepdims=True), jnp.sum(dconv * z1, axis=0, keepdims=True),
                              jnp.sum(dconv * z, axis=0, keepdims=True)], axis=0)
        _acc(dcw_ref, g, dw)
        cdz[...] = dconv[:8]

    ins = [("t", u, POOLW, 8), ("t", u, POOLW, 9), ("t", u, POOLW, 10), ("t", u, POOLW, 11),
           ("h", u, 16, POOLW, 8), ("h", u, 8, POOLW, 9), ("h", u, 8, POOLW, 11),
           ("t", d_a2, POOLW, 0), ("t", d_yb, POOLW, 0), ("w", wblk), ("w", pool_scale), ("w", conv_b)]
    outs = [("t", 4 * POOLW, MXU), ("a", (1, POOLW), F32), ("a", (POOLW, POOLW), F32), ("a", (3, POOLW), F32)]
    return _rows_call("poolconv_bwd", body, S, ts, ins, outs,
                      scratch=[pltpu.VMEM((16, POOLW), F32), pltpu.VMEM((8, POOLW), F32)], reverse=True)


def _rope_tables(positions):
    S = positions.shape[0]
    inv = ROPE_THETA ** (-jnp.arange(0, 16, 2, dtype=F32) / 16)
    ang = positions.astype(F32)[:, None] * inv
    cos, sin = jnp.cos(ang), jnp.sin(ang)
    c64 = jnp.concatenate([cos, cos, jnp.ones((S, 48), F32)], axis=1)
    s64 = jnp.concatenate([-sin, sin, jnp.zeros((S, 48), F32)], axis=1)
    return jnp.concatenate([c64, c64], axis=1), jnp.concatenate([s64, s64], axis=1)


def _partner(x):
    lane = lax.broadcasted_iota(jnp.int32, x.shape, 1) % 64
    return jnp.where(lane < 8, pltpu.roll(x, LANES - 8, axis=1), jnp.where(lane < 16, pltpu.roll(x, 8, axis=1), 0.0))


def _rope_fwd(u, ctab, stab, ts=256):
    S = u.shape[0]

    def body(i, g, q_ref, k_ref, c_ref, s_ref, qo_ref, ko_ref):
        c, s = c_ref[...], s_ref[...]
        for src, dst in ((q_ref, qo_ref), (k_ref, ko_ref)):
            for j in range(ATT_W // LANES):
                x = src[:, j * LANES:(j + 1) * LANES]
                dst[:, j * LANES:(j + 1) * LANES] = x * c + _partner(x) * s

    ins = [("t", u, ATT_W, 6), ("t", u, ATT_W, 7), ("t", ctab, LANES, 0), ("t", stab, LANES, 0)]
    return _rows_call("rope_fwd", body, S, ts, ins, [("t", ATT_W, F32), ("t", ATT_W, F32)])


def _rope_bwd(dq, dk, dv, ctab, stab, ts=256):
    S = dq.shape[0]

    def body(i, g, dq_ref, dk_ref, dv_ref, c_ref, s_ref, o_ref):
        c, s = c_ref[...], s_ref[...]
        for n, src in enumerate((dq_ref, dk_ref)):
            for j in range(ATT_W // LANES):
                x = src[:, j * LANES:(j + 1) * LANES]
                lo = n * ATT_W + j * LANES
                o_ref[:, lo:lo + LANES] = (x * c + _partner(x * s)).astype(o_ref.dtype)
        o_ref[:, 2 * ATT_W:3 * ATT_W] = dv_ref[...].astype(o_ref.dtype)

    ins = [("t", dq, ATT_W, 0), ("t", dk, ATT_W, 0), ("t", dv, ATT_W, 0), ("t", ctab, LANES, 0), ("t", stab, LANES, 0)]
    return _rows_call("rope_bwd", body, S, ts, ins, [("t", 3 * ATT_W, MXU)])[0]


def _perm_one(xg, d):
    S = xg.shape[0]
    return xg.reshape(S // d, d, 4, 64).transpose(2, 1, 0, 3).reshape(4, S, 64)


def _unperm_one(y, d):
    S = y.shape[1]
    return y.reshape(4, d, S // d, 64).transpose(2, 1, 0, 3).reshape(S, ATT_O)


def _perm(x):
    return jnp.stack([_perm_one(x[:, g * ATT_O:(g + 1) * ATT_O], d) for g, d in enumerate(DILS)])


def _perm_shared(x):
    return jnp.stack([_perm_one(x, d) for d in DILS])


def _unperm(y):
    return jnp.concatenate([_unperm_one(y[g], d) for g, d in enumerate(DILS)], axis=1)


def _unperm_groups(y):
    return jnp.stack([_unperm_one(y[g], d) for g, d in enumerate(DILS)])


def _band_masks(has_prev, has_next=None):
    r = lax.broadcasted_iota(jnp.int32, (QB, QB), 0)
    c = lax.broadcasted_iota(jnp.int32, (QB, QB), 1)
    return c <= r, (c >= r) & has_prev


def _blk4(fn):
    return pl.BlockSpec((None, 4, QB, 64), fn)


def _attn_fwd(qp, kp, vp):
    S = qp.shape[2]
    nb = S // QB

    def body(q_ref, kc_ref, kp_ref, vc_ref, vp_ref, o_ref, m_ref, l_ref):
        g, b = pl.program_id(0), pl.program_id(1)
        nblk = jnp.right_shift(nb, 2 * g)
        ok_c, ok_p = _band_masks((b & (nblk - 1)) > 0)
        for h in range(4):
            q = q_ref[h]
            sc = jnp.where(ok_c, _dot_nt(q, kc_ref[h], HI) * ATT_SCALE, NEG)
            sp = jnp.where(ok_p, _dot_nt(q, kp_ref[h], HI) * ATT_SCALE, NEG)
            m = jnp.maximum(jnp.max(sc, axis=1, keepdims=True), jnp.max(sp, axis=1, keepdims=True))
            pc, pp = jnp.exp(sc - m), jnp.exp(sp - m)
            l = jnp.sum(pc, axis=1, keepdims=True) + jnp.sum(pp, axis=1, keepdims=True)
            o_ref[h] = _dot(pc, vc_ref[h], HI) + _dot(pp, vp_ref[h], HI)
            m_ref[h] = jnp.broadcast_to(m, (QB, 64))
            l_ref[h] = jnp.broadcast_to(l, (QB, 64))

    cur = _blk4(lambda g, b: (g, 0, b, 0))
    prev = _blk4(lambda g, b: (g, 0, jnp.maximum(b - 1, 0), 0))
    shp = jax.ShapeDtypeStruct(qp.shape, F32)
    return pl.pallas_call(body, grid=(3, nb), in_specs=[cur, cur, prev, cur, prev], out_specs=[cur, cur, cur],
                          out_shape=[shp, shp, shp], compiler_params=_params(("arbitrary", "arbitrary"), 32),
                          name="attn_fwd")(qp, kp, kp, vp, vp)


def _attn_combine(o3, m3, l3, ts=256):
    S = o3.shape[1]

    def body(o_ref, m_ref, l_ref, att_ref, out_ref, lse_ref):
        m = jnp.maximum(jnp.maximum(m_ref[0], m_ref[1]), m_ref[2])
        w0, w1, w2 = jnp.exp(m_ref[0] - m), jnp.exp(m_ref[1] - m), jnp.exp(m_ref[2] - m)
        l = w0 * l_ref[0] + w1 * l_ref[1] + w2 * l_ref[2]
        o = (w0 * o_ref[0] + w1 * o_ref[1] + w2 * o_ref[2]) / l
        out_ref[...] = o
        att_ref[...] = o.astype(MXU)
        lse_ref[...] = m + jnp.log(l)

    blk3 = pl.BlockSpec((3, ts, ATT_O), lambda i: (0, i, 0))
    blk = pl.BlockSpec((ts, ATT_O), lambda i: (i, 0))
    return pl.pallas_call(body, grid=(S // ts,), in_specs=[blk3, blk3, blk3], out_specs=[blk, blk, blk],
                          out_shape=[jax.ShapeDtypeStruct((S, ATT_O), MXU), jax.ShapeDtypeStruct((S, ATT_O), F32),
                                     jax.ShapeDtypeStruct((S, ATT_O), F32)],
                          compiler_params=_params(("arbitrary",), 32), name="attn_combine")(o3, m3, l3)


def _attn_probs(q, k, lse, ok):
    return jnp.where(ok, jnp.exp(_dot_nt(q, k, HI) * ATT_SCALE - lse), 0.0)


def _attn_dq(qp, kp, vp, dop, op, lsep):
    S = qp.shape[2]
    nb = S // QB

    def body(q_ref, kc_ref, kp_ref, vc_ref, vp_ref, do_ref, o_ref, lse_ref, dq_ref):
        g, b = pl.program_id(0), pl.program_id(1)
        nblk = jnp.right_shift(nb, 2 * g)
        ok_c, ok_p = _band_masks((b & (nblk - 1)) > 0)
        for h in range(4):
            q, do = q_ref[h], do_ref[h]
            lse = lse_ref[h][:, 0:1]
            delta = jnp.sum(do * o_ref[h], axis=1, keepdims=True)
            ds_c = _attn_probs(q, kc_ref[h], lse, ok_c) * (_dot_nt(do, vc_ref[h], HI) - delta)
            ds_p = _attn_probs(q, kp_ref[h], lse, ok_p) * (_dot_nt(do, vp_ref[h], HI) - delta)
            dq_ref[h] = (_dot(ds_c, kc_ref[h], HI) + _dot(ds_p, kp_ref[h], HI)) * ATT_SCALE

    cur = _blk4(lambda g, b: (g, 0, b, 0))
    prev = _blk4(lambda g, b: (g, 0, jnp.maximum(b - 1, 0), 0))
    return pl.pallas_call(body, grid=(3, nb), in_specs=[cur, cur, prev, cur, prev, cur, cur, cur], out_specs=cur,
                          out_shape=jax.ShapeDtypeStruct(qp.shape, F32),
                          compiler_params=_params(("arbitrary", "arbitrary"), 32), name="attn_dq")(qp, kp, kp, vp, vp, dop, op, lsep)


def _attn_dkv(qp, kp, vp, dop, op, lsep):
    S = qp.shape[2]
    nb = S // QB

    def body(k_ref, v_ref, qc_ref, qn_ref, doc_ref, don_ref, oc_ref, on_ref, lc_ref, ln_ref, dk_ref, dv_ref):
        g, b = pl.program_id(0), pl.program_id(1)
        nblk = jnp.right_shift(nb, 2 * g)
        ok_c, ok_n = _band_masks(((b + 1) & (nblk - 1)) > 0)
        for h in range(4):
            k, v = k_ref[h], v_ref[h]
            dk = jnp.zeros((QB, 64), F32)
            dv = jnp.zeros((QB, 64), F32)
            for q_ref, do_ref, o_ref, l_ref, ok in ((qc_ref, doc_ref, oc_ref, lc_ref, ok_c), (qn_ref, don_ref, on_ref, ln_ref, ok_n)):
                q, do = q_ref[h], do_ref[h]
                p = _attn_probs(q, k, l_ref[h][:, 0:1], ok)
                ds = p * (_dot_nt(do, v, HI) - jnp.sum(do * o_ref[h], axis=1, keepdims=True))
                dv = dv + _dot_tn(p, do, HI)
                dk = dk + _dot_tn(ds, q, HI)
            dk_ref[h] = dk * ATT_SCALE
            dv_ref[h] = dv

    cur = _blk4(lambda g, b: (g, 0, b, 0))
    nxt = _blk4(lambda g, b: (g, 0, jnp.minimum(b + 1, nb - 1), 0))
    shp = jax.ShapeDtypeStruct(qp.shape, F32)
    return pl.pallas_call(body, grid=(3, nb), in_specs=[cur, cur, cur, nxt, cur, nxt, cur, nxt, cur, nxt], out_specs=[cur, cur],
                          out_shape=[shp, shp], compiler_params=_params(("arbitrary", "arbitrary"), 32),
                          name="attn_dkv")(kp, vp, qp, qp, dop, dop, op, op, lsep, lsep)


def _merge_fwd(x0, u, a2, yb, att, wa, wb, wc, w_out, g_post, ts=256):
    S = x0.shape[0]

    def body(i, g, x_ref, gate_ref, a2_ref, yb_ref, att_ref, wa_ref, wb_ref, wc_ref, wo_ref, gp_ref, mg_ref, y_ref, xo_ref):
        merged = jax.nn.sigmoid(gate_ref[:, 0:D]) * _dot_nt(a2_ref[...], wa_ref[...])
        merged = merged + jax.nn.sigmoid(gate_ref[:, D:2 * D]) * _dot_nt(yb_ref[...], wb_ref[...])
        merged = merged + jax.nn.sigmoid(gate_ref[:, 2 * D:3 * D]) * _dot_nt(att_ref[...], wc_ref[...])
        mb = merged.astype(MXU)
        mg_ref[...] = mb
        y = _dot(mb, wo_ref[...])
        y_ref[...] = y
        xo_ref[...] = x_ref[...] + _rms(y, gp_ref[...])[0]

    ins = [("t", x0, D, 0), ("t", u, GATE_W, 0), ("t", a2, POOLW, 0), ("t", yb, POOLW, 0), ("t", att, ATT_O, 0),
           ("w", wa), ("w", wb), ("w", wc), ("w", w_out), ("w", g_post)]
    return _rows_call("merge_fwd", body, S, ts, ins, [("t", D, MXU), ("t", D, F32), ("t", D, F32)])


def _merge_bwd(dx, y1, u, a2, yb, att, wa, wb, wc, w_out, g_post, ts=256):
    S = dx.shape[0]

    def body(i, g, dx_ref, y_ref, gate_ref, a2_ref, yb_ref, att_ref, wa_ref, wb_ref, wc_ref, wo_ref, gp_ref,
             dy_ref, dgate_ref, dbra_ref, dbrb_ref, dbrc_ref, da2_ref, dyb_ref, datt_ref, dgp_ref):
        dxv, y = dx_ref[...], y_ref[...]
        dy, r = _rms_bwd(dxv * gp_ref[...], y)
        _acc(dgp_ref, g, jnp.sum(dxv * (y * r), axis=0, keepdims=True))
        dyb16 = dy.astype(MXU)
        dy_ref[...] = dyb16
        dm = _dot_nt(dyb16, wo_ref[...])
        for n, (src, w_ref, dbr_ref, din_ref) in enumerate(((a2_ref, wa_ref, dbra_ref, da2_ref), (yb_ref, wb_ref, dbrb_ref, dyb_ref),
                                                           (att_ref, wc_ref, dbrc_ref, datt_ref))):
            gt = jax.nn.sigmoid(gate_ref[:, n * D:(n + 1) * D])
            br = _dot_nt(src[...], w_ref[...])
            dgate_ref[:, n * D:(n + 1) * D] = (dm * br * gt * (1.0 - gt)).astype(dgate_ref.dtype)
            dbr = (dm * gt).astype(MXU)
            dbr_ref[...] = dbr
            din_ref[...] = _dot(dbr, w_ref[...])

    ins = [("t", dx, D, 0), ("t", y1, D, 0), ("t", u, GATE_W, 0), ("t", a2, POOLW, 0), ("t", yb, POOLW, 0), ("t", att, ATT_O, 0),
           ("w", wa), ("w", wb), ("w", wc), ("w", w_out), ("w", g_post)]
    outs = [("t", D, MXU), ("t", GATE_W, MXU), ("t", D, MXU), ("t", D, MXU), ("t", D, MXU), ("t", POOLW, F32), ("t", POOLW, F32),
            ("t", ATT_O, F32), ("a", (1, D), F32)]
    return _rows_call("merge_bwd", body, S, ts, ins, outs)


def _prenorm_bwd(name, dx_res, dh, x, g_pre, ts=512):
    S = x.shape[0]

    def body(i, g, dx_ref, dh_ref, x_ref, g_ref, o_ref, dg_ref):
        dhv, xv = dh_ref[...], x_ref[...]
        dxn, r = _rms_bwd(dhv * g_ref[...], xv)
        o_ref[...] = dx_ref[...] + dxn
        _acc(dg_ref, g, jnp.sum(dhv * (xv * r), axis=0, keepdims=True))

    ins = [("t", dx_res, D, 0), ("t", dh, D, 0), ("t", x, D, 0), ("w", g_pre)]
    return _rows_call(name, body, S, ts, ins, [("t", D, F32), ("a", (1, D), F32)])


def _mem_heads(qm, kv_ref):
    out = []
    for h in range(4):
        q = qm[:, h * 128:(h + 1) * 128].astype(MXU)
        k = kv_ref[:, h * 128:(h + 1) * 128]
        v = kv_ref[:, MEM_W + h * 128:MEM_W + (h + 1) * 128]
        sc = _dot_nt(q, k) * MEM_SCALE
        e = jnp.exp(sc - jnp.max(sc, axis=1, keepdims=True))
        out.append((e / jnp.sum(e, axis=1, keepdims=True), q, k, v))
    return out


def _mem_fwd(x1, kv, g_pre, w_mq, w_mo, g_post, ts=256):
    S = x1.shape[0]

    def body(i, g, x_ref, kv_ref, gq_ref, wq_ref, wo_ref, gp_ref, om_ref, h_ref, y_ref, xo_ref):
        x = x_ref[...]
        hb = _rms(x, gq_ref[...])[0].astype(MXU)
        h_ref[...] = hb
        qm = _dot(hb, wq_ref[...])
        om = jnp.concatenate([_dot(p.astype(MXU), v) for p, _, _, v in _mem_heads(qm, kv_ref)], axis=1).astype(MXU)
        om_ref[...] = om
        y = _dot_nt(om, wo_ref[...])
        y_ref[...] = y
        xo_ref[...] = x + _rms(y, gp_ref[...])[0]

    ins = [("t", x1, D, 0), ("w", kv), ("w", g_pre), ("w", w_mq), ("w", w_mo), ("w", g_post)]
    return _rows_call("mem_fwd", body, S, ts, ins, [("t", MEM_W, MXU), ("t", D, MXU), ("t", D, F32), ("t", D, F32)])


def _mem_bwd(dx2, ym, x1, kv, g_pre, w_mq, w_mo, g_post, ts=256):
    S = x1.shape[0]

    def body(i, g, dx_ref, y_ref, x_ref, kv_ref, gq_ref, wq_ref, wo_ref, gp_ref, dy_ref, dq_ref, dxo_ref, dgp_ref, dgq_ref, dkv_ref):
        dxv, y, x = dx_ref[...], y_ref[...], x_ref[...]
        dy, r = _rms_bwd(dxv * gp_ref[...], y)
        _acc(dgp_ref, g, jnp.sum(dxv * (y * r), axis=0, keepdims=True))
        dyb = dy.astype(MXU)
        dy_ref[...] = dyb
        dom = _dot(dyb, wo_ref[...])
        h, r1 = _rms(x, gq_ref[...])
        qm = _dot(h.astype(MXU), wq_ref[...])
        dqs = []

        @pl.when(g == 0)
        def _():
            dkv_ref[...] = jnp.zeros_like(dkv_ref)

        for hh, (p, q, k, v) in enumerate(_mem_heads(qm, kv_ref)):
            doh = dom[:, hh * 128:(hh + 1) * 128].astype(MXU)
            dp = _dot_nt(doh, v)
            dsc = (p * (dp - jnp.sum(dp * p, axis=1, keepdims=True)) * MEM_SCALE).astype(MXU)
            dqs.append(_dot(dsc, k))
            dkv_ref[:, hh * 128:(hh + 1) * 128] += _dot_tn(dsc, q)
            dkv_ref[:, MEM_W + hh * 128:MEM_W + (hh + 1) * 128] += _dot_tn(p.astype(MXU), doh)
        dq = jnp.concatenate(dqs, axis=1).astype(MXU)
        dq_ref[...] = dq
        dh = _dot_nt(dq, wq_ref[...])
        _acc(dgq_ref, g, jnp.sum(dh * (x * r1), axis=0, keepdims=True))
        dxo_ref[...] = dxv + _rms_bwd(dh * gq_ref[...], x)[0]

    ins = [("t", dx2, D, 0), ("t", ym, D, 0), ("t", x1, D, 0), ("w", kv), ("w", g_pre), ("w", w_mq), ("w", w_mo), ("w", g_post)]
    outs = [("t", D, MXU), ("t", MEM_W, MXU), ("t", D, F32), ("a", (1, D), F32), ("a", (1, D), F32), ("a", (256, D), F32)]
    return _rows_call("mem_bwd", body, S, ts, ins, outs)


def _gain_grad(name, dn, x):
    n = x.shape[0]

    def body(i, g, dn_ref, x_ref, o_ref):
        xv = x_ref[...]
        r = lax.rsqrt(jnp.mean(xv * xv, axis=-1, keepdims=True) + EPS)
        o_ref[...] = jnp.sum(dn_ref[...] * (xv * r), axis=0, keepdims=True)

    return _rows_call(name, body, n, n, [("t", dn, D, 0), ("t", x, D, 0)], [("a", (1, D), F32)])[0]


def _ffn_fwd(x2, u3, conv_f, w_down, g_post, ts=256):
    S = x2.shape[0]

    def body(i, g, x_ref, ua_ref, ub_ref, cw_ref, wd_ref, gp_ref, act_ref, y_ref, xo_ref, cu):
        @pl.when(g == 0)
        def _():
            cu[...] = jnp.zeros_like(cu)

        ua = ua_ref[...]
        c, _, _ = _conv3(ua, cu[...], cw_ref[...])
        act = (c * jax.nn.sigmoid(c) * ub_ref[...]).astype(MXU)
        act_ref[...] = act
        y = _dot(act, wd_ref[...])
        y_ref[...] = y
        xo_ref[...] = x_ref[...] + _rms(y, gp_ref[...])[0]
        cu[...] = ua[ts - 8:]

    ins = [("t", x2, D, 0), ("t", u3, D_FF, 0), ("t", u3, D_FF, 1), ("w", conv_f), ("w", w_down), ("w", g_post)]
    return _rows_call("ffn_fwd", body, S, ts, ins, [("t", D_FF, MXU), ("t", D, F32), ("t", D, F32)],
                      scratch=[pltpu.VMEM((8, D_FF), F32)], vmem_mb=56)


def _ffn_bwd(dx3, y3, u3, conv_f, w_down, g_post, ts=128):
    S = dx3.shape[0]

    def body(i, g, dx_ref, y_ref, ua_ref, ub_ref, uap_ref, cw_ref, wd_ref, gp_ref, dy_ref, du_ref, dgp_ref, dcw_ref, cdc):
        @pl.when(g == 0)
        def _():
            cdc[...] = jnp.zeros_like(cdc)

        dxv, y = dx_ref[...], y_ref[...]
        dy, r = _rms_bwd(dxv * gp_ref[...], y)
        _acc(dgp_ref, g, jnp.sum(dxv * (y * r), axis=0, keepdims=True))
        dyb = dy.astype(MXU)
        dy_ref[...] = dyb
        dact = _dot_nt(dyb, wd_ref[...])
        ua, w = ua_ref[...], cw_ref[...]
        c, u1, u2 = _conv3(ua, uap_ref[...] * (i > 0).astype(F32), w)
        sg = jax.nn.sigmoid(c)
        du_ref[:, D_FF:2 * D_FF] = (dact * (c * sg)).astype(du_ref.dtype)
        dc = dact * ub_ref[...] * (sg * (1.0 + c * (1.0 - sg)))
        du_ref[:, 0:D_FF] = _conv3_t(dc, cdc[...], w).astype(du_ref.dtype)
        dw = jnp.concatenate([jnp.sum(dc * u2, axis=0, keepdims=True), jnp.sum(dc * u1, axis=0, keepdims=True),
                              jnp.sum(dc * ua, axis=0, keepdims=True)], axis=0)
        _acc(dcw_ref, g, dw)
        cdc[...] = dc[:8]

    ins = [("t", dx3, D, 0), ("t", y3, D, 0), ("t", u3, D_FF, 0), ("t", u3, D_FF, 1), ("h", u3, 8, D_FF, 0), ("w", conv_f),
           ("w", w_down), ("w", g_post)]
    outs = [("t", D, MXU), ("t", 2 * D_FF, MXU), ("a", (1, D), F32), ("a", (3, D_FF), F32)]
    return _rows_call("ffn_bwd", body, S, ts, ins, outs, scratch=[pltpu.VMEM((8, D_FF), F32)], reverse=True, vmem_mb=56)


def _loss_head(x, target, ts=512):
    S = x.shape[0]

    def body(i, g, x_ref, t_ref, dx_ref, acc_ref):
        diff = x_ref[...] - t_ref[...]
        dx_ref[...] = diff * (1.0 / D)
        col = jnp.sum(diff * diff, axis=0, keepdims=True)
        part = col[:, 0:LANES]
        for j in range(1, D // LANES):
            part = part + col[:, j * LANES:(j + 1) * LANES]
        row = lax.broadcasted_iota(jnp.int32, (8, LANES), 0)
        _acc(acc_ref, g, jnp.where(row == 0, jnp.broadcast_to(part, (8, LANES)), 0.0))

    return _rows_call("loss_head", body, S, ts, [("t", x, D, 0), ("t", target, D, 0)], [("t", D, F32), ("a", (8, LANES), F32)])


def _layer_weights(big, small, l):
    w_in = big['w_in']
    pool_w = small['pool_w'][l].astype(MXU)
    wblk = jnp.zeros((POOLW, POOLW), MXU)
    for g in range(4):
        wblk = lax.dynamic_update_slice(wblk, pool_w[g], (g * 96, g * 96))
    vec = lambda n: small[n][l].reshape(1, -1)
    return dict(
        w_in=jnp.concatenate([w_in[IN_W - GATE_W:], w_in[:IN_W - GATE_W]], axis=0),
        wblk=wblk, pool_scale=vec('pool_scale'), conv_b=small['conv_b_w'][l], wa=big['w_branch_a'], wb=big['w_branch_b'],
        wc=big['w_branch_c'], w_out=big['w_out'], w_mq=big['w_mq'], w_mkv=big['w_mkv'], w_mo=big['w_mo'],
        w_up=big['w_up'], conv_f=small['conv_ffn_w'][l], w_down=big['w_down'],
        g_mix_pre=vec('norm_mix_pre'), g_mix_post=vec('norm_mix_post'), g_mem_pre=vec('norm_mem_pre'),
        g_mem_post=vec('norm_mem_post'), g_memkv=vec('norm_memkv'), g_ffn_pre=vec('norm_ffn_pre'), g_ffn_post=vec('norm_ffn_post'))


def _layer_fwd(x0, mem, W, ctab, stab):
    sv = dict(x0=x0)
    sv['u'], sv['h1'] = _norm_mm("in_proj", x0, W['g_mix_pre'], W['w_in'], ts=512, tn=768, wt=True)
    sv['a2'], sv['yb'] = _poolconv_fwd(sv['u'], W['wblk'], W['pool_scale'], W['conv_b'])
    qr, kr = _rope_fwd(sv['u'], ctab, stab)
    sv['qp'], sv['kp'] = _perm(qr), _perm(kr)
    sv['vp'] = _perm(sv['u'][:, IN_W - ATT_W:])
    o3, m3, l3 = _attn_fwd(sv['qp'], sv['kp'], sv['vp'])
    sv['att'], sv['o'], sv['lse'] = _attn_combine(_unperm_groups(o3), _unperm_groups(m3), _unperm_groups(l3))
    sv['merged'], sv['y1'], sv['x1'] = _merge_fwd(x0, sv['u'], sv['a2'], sv['yb'], sv['att'], W['wa'], W['wb'], W['wc'],
                                                  W['w_out'], W['g_mix_post'])
    sv['kv'], sv['memn'] = _norm_mm("mem_kv", mem, W['g_memkv'], W['w_mkv'], ts=256, tn=D, out_dtype=MXU)
    sv['om'], sv['h2'], sv['ym'], sv['x2'] = _mem_fwd(sv['x1'], sv['kv'], W['g_mem_pre'], W['w_mq'], W['w_mo'], W['g_mem_post'])
    sv['u3'], sv['h3'] = _norm_mm("up_proj", sv['x2'], W['g_ffn_pre'], W['w_up'], ts=512, tn=512, wt=True)
    sv['act'], sv['y3'], x3 = _ffn_fwd(sv['x2'], sv['u3'], W['conv_f'], W['w_down'], W['g_ffn_post'])
    return x3, sv


def _layer_bwd(dx3, mem, W, sv, ctab, stab):
    g = {}
    dy3, du3, g['norm_ffn_post'], g['conv_ffn_w'] = _ffn_bwd(dx3, sv['y3'], sv['u3'], W['conv_f'], W['w_down'], W['g_ffn_post'])
    g['w_down'] = _mm_tn("dw_down", sv['act'], dy3, cap_k=256)
    g['w_up'] = _mm_tn("dw_up", du3, sv['h3'])
    dh3 = _mm_nn("dh_up", du3, W['w_up'], ts=512, tn=512)
    dx2, g['norm_ffn_pre'] = _prenorm_bwd("ffn_pre_bwd", dx3, dh3, sv['x2'], W['g_ffn_pre'])
    dym, dqm, dx1, g['norm_mem_post'], g['norm_mem_pre'], dkv = _mem_bwd(dx2, sv['ym'], sv['x1'], sv['kv'], W['g_mem_pre'],
                                                                       W['w_mq'], W['w_mo'], W['g_mem_post'])
    g['w_mo'] = _mm_tn("dw_mo", dym, sv['om'])
    g['w_mq'] = _mm_tn("dw_mq", sv['h2'], dqm)
    dkvb = dkv.astype(MXU)
    g['w_mkv'] = _mm_tn("dw_mkv", sv['memn'], dkvb)
    g['norm_memkv'] = _gain_grad("memkv_gain", _mm_nt("d_memn", dkvb, W['w_mkv'], ts=256, tn=512), mem)
    dy1, dgate, dbra, dbrb, dbrc, da2, dyb, datt, g['norm_mix_post'] = _merge_bwd(
        dx1, sv['y1'], sv['u'], sv['a2'], sv['yb'], sv['att'], W['wa'], W['wb'], W['wc'], W['w_out'], W['g_mix_post'])
    g['w_out'] = _mm_tn("dw_out", sv['merged'], dy1)
    g['w_branch_a'] = _mm_tn("dw_a", dbra, sv['a2'])
    g['w_branch_b'] = _mm_tn("dw_b", dbrb, sv['yb'])
    g['w_branch_c'] = _mm_tn("dw_c", dbrc, sv['att'])
    dabc, g['pool_scale'], dwblk, g['conv_b_w'] = _poolconv_bwd(sv['u'], da2, dyb, W['wblk'], W['pool_scale'], W['conv_b'])
    g['pool_w'] = jnp.stack([dwblk[k * 96:(k + 1) * 96, k * 96:(k + 1) * 96] for k in range(4)])
    dop, op, lsep = _perm_shared(datt), _perm_shared(sv['o']), _perm_shared(sv['lse'])
    dqp = _attn_dq(sv['qp'], sv['kp'], sv['vp'], dop, op, lsep)
    dkp, dvp = _attn_dkv(sv['qp'], sv['kp'], sv['vp'], dop, op, lsep)
    dqkv = _rope_bwd(_unperm(dqp), _unperm(dkp), _unperm(dvp), ctab, stab)
    du = jnp.concatenate([dgate, dabc, dqkv], axis=1)
    dw_in = _mm_tn("dw_in", du, sv['h1'], cap_k=768)
    g['w_in'] = jnp.concatenate([dw_in[GATE_W:], dw_in[:GATE_W]], axis=0)
    dh1 = _mm_nn("dh_in", du, W['w_in'], ts=256, tn=512)
    dx0, g['norm_mix_pre'] = _prenorm_bwd("mix_pre_bwd", dx1, dh1, sv['x0'], W['g_mix_pre'])
    return dx0, g


def _local_step(x, mem, positions, target, big, small):
    ctab, stab = _rope_tables(positions)
    Ws = [_layer_weights(big[l], small, l) for l in range(DEPTH)]
    saved = []
    for l in range(DEPTH):
        x, sv = _layer_fwd(x, mem, Ws[l], ctab, stab)
        saved.append(sv)
    dx, acc = _loss_head(x, target)
    loss = jnp.sum(acc) * (0.5 / D)
    grads = [None] * DEPTH
    for l in reversed(range(DEPTH)):
        dx, grads[l] = _layer_bwd(dx, mem, Ws[l], saved[l], ctab, stab)
    return loss, dx, grads


_HBM = pl.BlockSpec(memory_space=pl.ANY)
MESH_ID = pl.DeviceIdType.MESH


def _all_gather(name, xs):
    n = len(xs)

    def body(*refs):
        x_refs, out_refs = refs[:n], refs[n:2 * n]
        send_sems, recv_sems, local_sems = refs[2 * n:]
        x, y, c = lax.axis_index("x"), lax.axis_index("y"), lax.axis_index("c")
        me, sibling = (x, y, c), (x, y, 1 - c)
        chips = [(1 - x, y), (x, 1 - y), (1 - x, 1 - y)]

        def slot(a, p):
            return out_refs[a].at[4 * p[0] + 2 * p[1] + p[2]]

        def copy(a, k, block, to, src=None):
            return pltpu.make_async_remote_copy(src_ref=slot(a, block) if src is None else src, dst_ref=slot(a, block),
                                                send_sem=send_sems.at[a, k], recv_sem=recv_sems.at[a, k], device_id=to,
                                                device_id_type=MESH_ID)

        started = []
        for a in range(n):
            mine = pltpu.make_async_copy(x_refs[a], slot(a, me), local_sems.at[a])
            mine.start()
            started.append(mine)
        first = []
        for a in range(n):
            first.append(copy(a, 0, me, sibling, src=x_refs[a]))
            first += [copy(a, 1 + j, me, (*chip, c), src=x_refs[a]) for j, chip in enumerate(chips)]
        for cp in first:
            cp.start()
        passed = []
        for j, chip in enumerate(chips):
            for a in range(n):
                copy(a, 1 + j, (*chip, c), me).wait_recv()
                fw = copy(a, 4 + j, (*chip, c), sibling)
                fw.start()
                passed.append(fw)
        for a in range(n):
            copy(a, 0, sibling, me).wait_recv()
            for j, chip in enumerate(chips):
                copy(a, 4 + j, (*chip, 1 - c), me).wait_recv()
        for cp in first + passed:
            cp.wait_send()
        for mine in started:
            mine.wait()

    return pl.pallas_call(
        body, out_shape=[jax.ShapeDtypeStruct((N_DEV,) + x.shape, x.dtype) for x in xs], in_specs=[_HBM] * n, out_specs=[_HBM] * n,
        scratch_shapes=[pltpu.SemaphoreType.DMA((n, 7)), pltpu.SemaphoreType.DMA((n, 7)), pltpu.SemaphoreType.DMA((n,))],
        name=name)(*xs)


def _exchange(name, gs):
    n = len(gs)

    def body(*refs):
        g_refs, out_refs = refs[:n], refs[n:2 * n]
        send_sems, recv_sems, local_sems = refs[2 * n:]
        x, y, c = lax.axis_index("x"), lax.axis_index("y"), lax.axis_index("c")
        me = 4 * x + 2 * y + c
        copies = []
        for a in range(n):
            mine = pltpu.make_async_copy(g_refs[a].at[me], out_refs[a].at[me], local_sems.at[a])
            mine.start()
            copies.append(mine)
        for r in range(1, N_DEV):
            px, py, pc = x ^ ((r >> 2) & 1), y ^ ((r >> 1) & 1), c ^ (r & 1)
            for a in range(n):
                cp = pltpu.make_async_remote_copy(src_ref=g_refs[a].at[4 * px + 2 * py + pc], dst_ref=out_refs[a].at[me],
                                                  send_sem=send_sems.at[a, r - 1], recv_sem=recv_sems.at[a, r - 1],
                                                  device_id=(px, py, pc), device_id_type=MESH_ID)
                cp.start()
                copies.append(cp)
        for cp in copies:
            cp.wait()

    return pl.pallas_call(
        body, out_shape=[jax.ShapeDtypeStruct(g.shape, g.dtype) for g in gs], in_specs=[_HBM] * n, out_specs=[_HBM] * n,
        scratch_shapes=[pltpu.SemaphoreType.DMA((n, N_DEV - 1)), pltpu.SemaphoreType.DMA((n, N_DEV - 1)), pltpu.SemaphoreType.DMA((n,))],
        name=name)(*gs)


def _row_tile(rows, cols, budget):
    if rows * cols * 4 <= budget or rows % 8:
        return rows
    best = 8
    for t in range(8, rows + 1, 8):
        if rows % t == 0 and t * cols * 4 <= budget:
            best = t
    return best


def _sum_slots(name, recv):
    _, R, C = recv.shape
    tr = _row_tile(R, C, 1 << 20)

    def body(r_ref, o_ref):
        g = r_ref[0]
        for k in range(1, N_DEV):
            g = g + r_ref[k]
        o_ref[...] = g

    return pl.pallas_call(body, grid=(R // tr,), in_specs=[pl.BlockSpec((N_DEV, tr, C), lambda i: (0, i, 0))],
                          out_specs=pl.BlockSpec((tr, C), lambda i: (i, 0)), out_shape=jax.ShapeDtypeStruct((R, C), F32),
                          compiler_params=_params(("arbitrary",), 32), name=name)(recv)


def _adamw(name, g, w, m, v):
    R, C = w.shape
    tr = _row_tile(R, C, 1 << 20)
    c1 = 1.0 - ADAM_B1 ** ADAM_STEP
    c2 = 1.0 - ADAM_B2 ** ADAM_STEP

    def body(g_ref, w_ref, m_ref, v_ref, d_ref, mo_ref, vo_ref):
        gv = g_ref[...]
        mn = ADAM_B1 * m_ref[...] + (1.0 - ADAM_B1) * gv
        vn = ADAM_B2 * v_ref[...] + (1.0 - ADAM_B2) * (gv * gv)
        mo_ref[...] = mn
        vo_ref[...] = vn
        d_ref[...] = -ADAM_LR * ((mn / c1) / (jnp.sqrt(vn / c2) + ADAM_EPS) + ADAM_WD * w_ref[...])

    blk = pl.BlockSpec((tr, C), lambda i: (i, 0))
    shp = jax.ShapeDtypeStruct((R, C), F32)
    return pl.pallas_call(body, grid=(R // tr,), in_specs=[blk, blk, blk, blk], out_specs=[blk, blk, blk], out_shape=[shp, shp, shp],
                          compiler_params=_params(("arbitrary",), 32), name=name)(g, w, m, v)


def _pad_flat(a, n):
    a = a.reshape(-1)
    return jnp.pad(a, (0, n - a.shape[0]))


def _seg(n):
    return -(-n // FLAT_ALIGN) * FLAT_ALIGN


def _to_blocks(full, axis):
    shp = full.shape
    return jnp.moveaxis(full.reshape(shp[:axis] + (N_DEV, shp[axis] // N_DEV) + shp[axis + 1:]), axis, 0)


def _from_blocks(blocks, axis):
    b = jnp.moveaxis(blocks, 0, axis)
    shp = b.shape
    return b.reshape(shp[:axis] + (shp[axis] * shp[axis + 1],) + shp[axis + 2:])


def _as_rows(shard, n):
    return shard.T if SHARD_AXIS[n] == 2 else shard


def _gather_weights(w):
    conv = jnp.concatenate([_pad_flat(w[n], _seg(w[n].size)) for n in F32_GATHERED]).reshape(-1, LANES)
    big = []
    for l in range(DEPTH):
        got = _all_gather(f"weights_all_gather_{l}", [_as_rows(w[n][l], n).astype(MXU) for n in BIG] + ([conv] if l == 0 else []))
        big.append({n: o.reshape(-1, o.shape[-1]) for n, o in zip(BIG, got)})
        if l == 0:
            conv_all = got[-1].reshape(N_DEV, -1)
    full, off = {}, 0
    for n in F32_GATHERED:
        full[n] = _from_blocks(conv_all[:, off:off + w[n].size].reshape((N_DEV,) + w[n].shape), 2)
        off += _seg(w[n].size)
    return big, full


def kernel(x, mem, positions, norm_mix_pre, norm_mix_post, w_in, pool_w, pool_scale, conv_b_w, w_branch_a, w_branch_b, w_branch_c, w_out, norm_mem_pre, norm_mem_post, norm_memkv, w_mq, w_mkv, w_mo, norm_ffn_pre, norm_ffn_post, w_up, conv_ffn_w, w_down, loss_target, m_norm_mix_pre, m_norm_mix_post, m_w_in, m_pool_w, m_pool_scale, m_conv_b_w, m_w_branch_a, m_w_branch_b, m_w_branch_c, m_w_out, m_norm_mem_pre, m_norm_mem_post, m_norm_memkv, m_w_mq, m_w_mkv, m_w_mo, m_norm_ffn_pre, m_norm_ffn_post, m_w_up, m_conv_ffn_w, m_w_down, v_norm_mix_pre, v_norm_mix_post, v_w_in, v_pool_w, v_pool_scale, v_conv_b_w, v_w_branch_a, v_w_branch_b, v_w_branch_c, v_w_out, v_norm_mem_pre, v_norm_mem_post, v_norm_memkv, v_w_mq, v_w_mkv, v_w_mo, v_norm_ffn_pre, v_norm_ffn_post, v_w_up, v_conv_ffn_w, v_w_down):
    w = dict(norm_mix_pre=norm_mix_pre, norm_mix_post=norm_mix_post, w_in=w_in, pool_w=pool_w, pool_scale=pool_scale, conv_b_w=conv_b_w, w_branch_a=w_branch_a, w_branch_b=w_branch_b, w_branch_c=w_branch_c, w_out=w_out, norm_mem_pre=norm_mem_pre, norm_mem_post=norm_mem_post, norm_memkv=norm_memkv, w_mq=w_mq, w_mkv=w_mkv, w_mo=w_mo, norm_ffn_pre=norm_ffn_pre, norm_ffn_post=norm_ffn_post, w_up=w_up, conv_ffn_w=conv_ffn_w, w_down=w_down)
    m = dict(norm_mix_pre=m_norm_mix_pre, norm_mix_post=m_norm_mix_post, w_in=m_w_in, pool_w=m_pool_w, pool_scale=m_pool_scale, conv_b_w=m_conv_b_w, w_branch_a=m_w_branch_a, w_branch_b=m_w_branch_b, w_branch_c=m_w_branch_c, w_out=m_w_out, norm_mem_pre=m_norm_mem_pre, norm_mem_post=m_norm_mem_post, norm_memkv=m_norm_memkv, w_mq=m_w_mq, w_mkv=m_w_mkv, w_mo=m_w_mo, norm_ffn_pre=m_norm_ffn_pre, norm_ffn_post=m_norm_ffn_post, w_up=m_w_up, conv_ffn_w=m_conv_ffn_w, w_down=m_w_down)
    v = dict(norm_mix_pre=v_norm_mix_pre, norm_mix_post=v_norm_mix_post, w_in=v_w_in, pool_w=v_pool_w, pool_scale=v_pool_scale, conv_b_w=v_conv_b_w, w_branch_a=v_w_branch_a, w_branch_b=v_w_branch_b, w_branch_c=v_w_branch_c, w_out=v_w_out, norm_mem_pre=v_norm_mem_pre, norm_mem_post=v_norm_mem_post, norm_memkv=v_norm_memkv, w_mq=v_w_mq, w_mkv=v_w_mkv, w_mo=v_w_mo, norm_ffn_pre=v_norm_ffn_pre, norm_ffn_post=v_norm_ffn_post, w_up=v_w_up, conv_ffn_w=v_conv_ffn_w, w_down=v_w_down)

    big, conv_full = _gather_weights(w)
    small = {n: w[n] for n in WEIGHTS if n not in SHARD_AXIS}
    small.update(conv_full)
    loss, dx, grads = _local_step(x[0], mem[0], positions[0], loss_target[0], big, small)
    loss = lax.psum(loss, MESH_AXES)

    misc_names = [n for n in WEIGHTS if n not in BIG]
    stacked = {n: jnp.stack([grads[l][n].reshape(small[n].shape[1:]) for l in range(DEPTH)]) for n in misc_names}
    rows = [(_to_blocks(stacked[n], 2) if n in SHARD_AXIS else jnp.broadcast_to(stacked[n][None], (N_DEV,) + stacked[n].shape))
            for n in misc_names]
    segs = [_seg(w[n].size) for n in misc_names]
    misc = jnp.concatenate([jnp.pad(r.reshape(N_DEV, -1), ((0, 0), (0, s - r[0].size))) for r, s in zip(rows, segs)],
                           axis=1).reshape(N_DEV, -1, LANES)
    g_out, per_layer = {}, {}
    for l in reversed(range(DEPTH)):
        recv = _exchange(f"grad_exchange_{l}", [grads[l][n].reshape(N_DEV, -1, grads[l][n].shape[-1]) for n in BIG] + ([misc] if l == 0 else []))
        for n, r in zip(BIG, recv):
            per_layer[n, l] = _as_rows(_sum_slots(f"sum_{n}_{l}", r), n)
    misc_sum = _sum_slots("sum_misc", recv[-1]).reshape(-1)
    off = 0
    for n, s in zip(misc_names, segs):
        g_out[n] = misc_sum[off:off + w[n].size].reshape(w[n].shape)
        off += s
    for n in BIG:
        g_out[n] = jnp.stack([per_layer[n, l] for l in range(DEPTH)])

    res = [[], [], [], []]
    for n in WEIGHTS:
        shp = w[n].shape
        d, mn, vn = _adamw(f"adamw_{n}", *[a.reshape(-1, shp[-1]) for a in (g_out[n], w[n], m[n], v[n])])
        for k, a in enumerate((g_out[n], d, mn, vn)):
            res[k].append(a.reshape(shp))
    return (loss, dx[None], *res[0], *res[1], *res[2], *res[3])
```

```python
import jax
import jax.numpy as jnp
from jax import lax
from jax.experimental import pallas as pl
from jax.experimental.pallas import tpu as pltpu

F32 = jnp.float32
MXU = jnp.bfloat16
HI = lax.Precision.HIGHEST

D = 1024
DEPTH = 2
POOLW = 384
ATT_W = 768
ATT_O = 256
GATE_W = 3 * D
IN_W = 6912
MEM_W = 512
D_FF = 2816
EPS = 1e-6
ROPE_THETA = 500000.0
QB = 128
DILS = (1, 4, 16)
NEG = -1e30
MEM_SCALE = 128 ** -0.5
ATT_SCALE = 0.125

ADAM_LR, ADAM_B1, ADAM_B2, ADAM_EPS, ADAM_WD, ADAM_STEP = 0.001, 0.9, 0.999, 1e-08, 0.01, 10

N_DEV = 8
MESH_AXES = ("x", "y", "c")
LANES = 128
FLAT_ALIGN = 2048
ROW_TILE = 1024

WEIGHTS = ['norm_mix_pre', 'norm_mix_post', 'w_in', 'pool_w', 'pool_scale', 'conv_b_w', 'w_branch_a', 'w_branch_b',
           'w_branch_c', 'w_out', 'norm_mem_pre', 'norm_mem_post', 'norm_memkv', 'w_mq', 'w_mkv', 'w_mo',
           'norm_ffn_pre', 'norm_ffn_post', 'w_up', 'conv_ffn_w', 'w_down']
SHARD_AXIS = {'w_in': 2, 'conv_b_w': 2, 'w_branch_a': 2, 'w_branch_b': 2, 'w_branch_c': 2, 'w_out': 1, 'w_mq': 1,
              'w_mkv': 1, 'w_mo': 2, 'w_up': 2, 'conv_ffn_w': 2, 'w_down': 1}
F32_GATHERED = ('conv_b_w', 'conv_ffn_w')
BIG = [n for n in WEIGHTS if n in SHARD_AXIS and n not in F32_GATHERED]


def _params(sem, vmem_mb):
    return pltpu.CompilerParams(dimension_semantics=sem, vmem_limit_bytes=vmem_mb << 20)


def _dot(a, b, prec=None):
    return lax.dot_general(a, b, (((1,), (0,)), ((), ())), preferred_element_type=F32, precision=prec)


def _dot_nt(a, b, prec=None):
    return lax.dot_general(a, b, (((1,), (1,)), ((), ())), preferred_element_type=F32, precision=prec)


def _dot_tn(a, b, prec=None):
    return lax.dot_general(a, b, (((0,), (0,)), ((), ())), preferred_element_type=F32, precision=prec)


def _tile(n, cap):
    if n <= cap:
        return n
    best = None
    for t in range(LANES, cap + 1, LANES):
        if n % t == 0:
            best = t
    assert best is not None, (n, cap)
    return best


def _rms(x, g):
    r = lax.rsqrt(jnp.mean(x * x, axis=-1, keepdims=True) + EPS)
    return x * r * g, r


def _rms_bwd(w, y):
    r = lax.rsqrt(jnp.mean(y * y, axis=-1, keepdims=True) + EPS)
    return r * w - y * (r * r * r) * jnp.mean(w * y, axis=-1, keepdims=True), r


def _rows_call(name, body, n_rows, ts, ins, outs, scratch=(), reverse=False, vmem_mb=48):
    nt = n_rows // ts
    assert nt * ts == n_rows

    def tile_of(g):
        return (nt - 1 - g) if reverse else g

    in_specs, args = [], []
    for op in ins:
        if op[0] == "t":
            _, a, cw, cb = op
            in_specs.append(pl.BlockSpec((ts, cw), lambda g, cb=cb: (tile_of(g), cb)))
        elif op[0] == "h":
            _, a, hr, cw, cb = op
            in_specs.append(pl.BlockSpec((hr, cw), lambda g, cb=cb, k=ts // hr: (jnp.maximum(tile_of(g) * k - 1, 0), cb)))
        else:
            _, a = op
            in_specs.append(pl.BlockSpec(a.shape, lambda g, n=a.ndim: (0,) * n))
        args.append(a)
    out_specs, out_shape = [], []
    for op in outs:
        if op[0] == "t":
            _, cols, dt = op
            out_specs.append(pl.BlockSpec((ts, cols), lambda g: (tile_of(g), 0)))
            out_shape.append(jax.ShapeDtypeStruct((n_rows, cols), dt))
        else:
            _, shp, dt = op
            out_specs.append(pl.BlockSpec(shp, lambda g, n=len(shp): (0,) * n))
            out_shape.append(jax.ShapeDtypeStruct(shp, dt))

    def kern(*refs):
        g = pl.program_id(0)
        body(tile_of(g), g, *refs)

    return pl.pallas_call(kern, grid=(nt,), in_specs=in_specs, out_specs=out_specs, out_shape=out_shape,
                          scratch_shapes=list(scratch), compiler_params=_params(("arbitrary",), vmem_mb), name=name)(*args)


def _acc(ref, g, val):
    @pl.when(g == 0)
    def _():
        ref[...] = val

    @pl.when(g != 0)
    def _():
        ref[...] += val


def _norm_mm(name, x, g, w, ts, tn, out_dtype=F32, wt=False):
    S, K = x.shape
    N = w.shape[0] if wt else w.shape[1]

    def body(x_ref, g_ref, w_ref, o_ref, h_ref, hs):
        @pl.when(pl.program_id(1) == 0)
        def _():
            h, _ = _rms(x_ref[...], g_ref[...])
            hs[...] = h.astype(MXU)
            h_ref[...] = h.astype(MXU)

        o_ref[...] = (_dot_nt if wt else _dot)(hs[...], w_ref[...]).astype(out_dtype)

    w_spec = pl.BlockSpec((tn, K), lambda i, j: (j, 0)) if wt else pl.BlockSpec((K, tn), lambda i, j: (0, j))
    return pl.pallas_call(
        body, grid=(S // ts, N // tn),
        in_specs=[pl.BlockSpec((ts, K), lambda i, j: (i, 0)), pl.BlockSpec((1, K), lambda i, j: (0, 0)), w_spec],
        out_specs=[pl.BlockSpec((ts, tn), lambda i, j: (i, j)), pl.BlockSpec((ts, K), lambda i, j: (i, 0))],
        out_shape=[jax.ShapeDtypeStruct((S, N), out_dtype), jax.ShapeDtypeStruct((S, K), MXU)],
        scratch_shapes=[pltpu.VMEM((ts, K), MXU)],
        compiler_params=_params(("arbitrary", "arbitrary"), 48), name=name)(x, g, w)


def _mm_nt(name, a, b, ts, tn, out_dtype=F32):
    M, K = a.shape
    N = b.shape[0]

    def body(a_ref, b_ref, o_ref):
        o_ref[...] = _dot_nt(a_ref[...], b_ref[...]).astype(out_dtype)

    return pl.pallas_call(
        body, grid=(M // ts, N // tn),
        in_specs=[pl.BlockSpec((ts, K), lambda i, j: (i, 0)), pl.BlockSpec((tn, K), lambda i, j: (j, 0))],
        out_specs=pl.BlockSpec((ts, tn), lambda i, j: (i, j)), out_shape=jax.ShapeDtypeStruct((M, N), out_dtype),
        compiler_params=_params(("arbitrary", "arbitrary"), 48), name=name)(a, b)


def _mm_nn(name, a, b, ts, tn, out_dtype=F32):
    M, K = a.shape
    N = b.shape[1]

    def body(a_ref, b_ref, o_ref):
        o_ref[...] = _dot(a_ref[...], b_ref[...]).astype(out_dtype)

    return pl.pallas_call(
        body, grid=(M // ts, N // tn),
        in_specs=[pl.BlockSpec((ts, K), lambda i, j: (i, 0)), pl.BlockSpec((K, tn), lambda i, j: (0, j))],
        out_specs=pl.BlockSpec((ts, tn), lambda i, j: (i, j)), out_shape=jax.ShapeDtypeStruct((M, N), out_dtype),
        compiler_params=_params(("arbitrary", "arbitrary"), 48), name=name)(a, b)


def _mm_tn(name, a, b, cap_k=512, cap_n=512, out_dtype=MXU):
    S, K = a.shape
    N = b.shape[1]
    tk, tn = _tile(K, cap_k), _tile(N, cap_n)

    def body(a_ref, b_ref, o_ref):
        o_ref[...] = _dot_tn(a_ref[...], b_ref[...]).astype(out_dtype)

    return pl.pallas_call(
        body, grid=(K // tk, N // tn),
        in_specs=[pl.BlockSpec((S, tk), lambda i, j: (0, i)), pl.BlockSpec((S, tn), lambda i, j: (0, j))],
        out_specs=pl.BlockSpec((tk, tn), lambda i, j: (i, j)), out_shape=jax.ShapeDtypeStruct((K, N), out_dtype),
        compiler_params=_params(("arbitrary", "arbitrary"), 48), name=name)(a, b)


def _pool_cols(shape):
    col = lax.broadcasted_iota(jnp.int32, shape, 1)
    return col < 96, col < 192, col < 288


def _pool_select(s2, s4, s8, s16):
    c1, c2, c3 = _pool_cols(s2.shape)
    return jnp.where(c1, s2, jnp.where(c2, s4, jnp.where(c3, s8, s16)))


def _pool_cnt(t0, ts):
    c1, c2, c3 = _pool_cols((ts, POOLW))
    win = jnp.where(c1, 2, jnp.where(c2, 4, jnp.where(c3, 8, 16)))
    t = t0 + lax.broadcasted_iota(jnp.int32, (ts, POOLW), 0)
    return jnp.minimum(t + 1, win).astype(F32)


def _pooled(a, prev, t0):
    ts = a.shape[0]
    ext = jnp.concatenate([prev, a], axis=0)
    s2 = ext + pltpu.roll(ext, 1, axis=0)
    s4 = s2 + pltpu.roll(s2, 2, axis=0)
    s8 = s4 + pltpu.roll(s4, 4, axis=0)
    s16 = s8 + pltpu.roll(s8, 8, axis=0)
    sums = _pool_select(s2, s4, s8, s16)[16:]
    return sums / _pool_cnt(t0, ts) - a


def _conv3(z, prev8, w):
    ext = jnp.concatenate([prev8, z], axis=0)
    z1 = pltpu.roll(ext, 1, axis=0)[8:]
    z2 = pltpu.roll(ext, 2, axis=0)[8:]
    return w[0:1] * z2 + w[1:2] * z1 + w[2:3] * z, z1, z2


def _conv3_t(dc, next8, w):
    ts = dc.shape[0]
    ext = jnp.concatenate([dc, next8], axis=0)
    n = ts + 8
    u1 = pltpu.roll(ext, n - 1, axis=0)[:ts]
    u2 = pltpu.roll(ext, n - 2, axis=0)[:ts]
    return w[2:3] * dc + w[1:2] * u1 + w[0:1] * u2


def _poolconv_fwd(u, wblk, pool_scale, conv_b, ts=256):
    S = u.shape[0]

    def body(i, g, a_ref, bx_ref, bb_ref, bc_ref, wblk_ref, ps_ref, cw_ref, a2_ref, yb_ref, ca, cz):
        @pl.when(g == 0)
        def _():
            ca[...] = jnp.zeros_like(ca)
            cz[...] = jnp.zeros_like(cz)

        a = a_ref[...]
        p = _pooled(a, ca[...], i * ts)
        mixed = _dot(p.astype(MXU), wblk_ref[...])
        a2_ref[...] = (mixed * ps_ref[...]).astype(MXU)
        z = bc_ref[...] * bx_ref[...]
        conv, _, _ = _conv3(z, cz[...], cw_ref[...])
        yb_ref[...] = (bb_ref[...] * conv).astype(MXU)
        ca[...] = a[ts - 16:]
        cz[...] = z[ts - 8:]

    ins = [("t", u, POOLW, 8), ("t", u, POOLW, 9), ("t", u, POOLW, 10), ("t", u, POOLW, 11), ("w", wblk), ("w", pool_scale),
           ("w", conv_b)]
    return _rows_call("poolconv_fwd", body, S, ts, ins, [("t", POOLW, MXU), ("t", POOLW, MXU)],
                      scratch=[pltpu.VMEM((16, POOLW), F32), pltpu.VMEM((8, POOLW), F32)])


def _poolconv_bwd(u, d_a2, d_yb, wblk, pool_scale, conv_b, ts=256):
    S = u.shape[0]

    def body(i, g, a_ref, bx_ref, bb_ref, bc_ref, ap_ref, bxp_ref, bcp_ref, da2_ref, dyb_ref, wblk_ref, ps_ref, cw_ref,
             o_ref, dps_ref, dwb_ref, dcw_ref, ce, cdz):
        @pl.when(g == 0)
        def _():
            ce[...] = jnp.zeros_like(ce)
            cdz[...] = jnp.zeros_like(cdz)

        first = (i > 0).astype(F32)
        a = a_ref[...]
        p = _pooled(a, ap_ref[...] * first, i * ts)
        pb = p.astype(MXU)
        mixed = _dot(pb, wblk_ref[...])
        da2 = da2_ref[...]
        dmixed = (da2 * ps_ref[...]).astype(MXU)
        dp = _dot_nt(dmixed, wblk_ref[...])
        _acc(dps_ref, g, jnp.sum(da2 * mixed, axis=0, keepdims=True))
        _acc(dwb_ref, g, _dot_tn(pb, dmixed))
        e = dp / _pool_cnt(i * ts, ts)
        ext = jnp.concatenate([e, ce[...]], axis=0)
        n = ts + 16
        f2 = ext + pltpu.roll(ext, n - 1, axis=0)
        f4 = f2 + pltpu.roll(f2, n - 2, axis=0)
        f8 = f4 + pltpu.roll(f4, n - 4, axis=0)
        f16 = f8 + pltpu.roll(f8, n - 8, axis=0)
        o_ref[:, 0:POOLW] = (_pool_select(f2, f4, f8, f16)[:ts] - dp).astype(o_ref.dtype)
        ce[...] = e[:16]

        bx, bb, bc = bx_ref[...], bb_ref[...], bc_ref[...]
        z = bc * bx
        w = cw_ref[...]
        conv, z1, z2 = _conv3(z, bxp_ref[...] * bcp_ref[...] * first, w)
        dyb = dyb_ref[...]
        dconv = dyb * bb
        dz = _conv3_t(dconv, cdz[...], w)
        o_ref[:, POOLW:2 * POOLW] = (dz * bc).astype(o_ref.dtype)
        o_ref[:, 2 * POOLW:3 * POOLW] = (dyb * conv).astype(o_ref.dtype)
        o_ref[:, 3 * POOLW:4 * POOLW] = (dz * bx).astype(o_ref.dtype)
        dw = jnp.concatenate([jnp.sum(dconv * z2, axis=0, keepdims=True), jnp.sum(dconv * z1, axis=0, keepdims=True),
                              jnp.sum(dconv * z, axis=0, keepdims=True)], axis=0)
        _acc(dcw_ref, g, dw)
        cdz[...] = dconv[:8]

    ins = [("t", u, POOLW, 8), ("t", u, POOLW, 9), ("t", u, POOLW, 10), ("t", u, POOLW, 11),
           ("h", u, 16, POOLW, 8), ("h", u, 8, POOLW, 9), ("h", u, 8, POOLW, 11),
           ("t", d_a2, POOLW, 0), ("t", d_yb, POOLW, 0), ("w", wblk), ("w", pool_scale), ("w", conv_b)]
    outs = [("t", 4 * POOLW, MXU), ("a", (1, POOLW), F32), ("a", (POOLW, POOLW), F32), ("a", (3, POOLW), F32)]
    return _rows_call("poolconv_bwd", body, S, ts, ins, outs,
                      scratch=[pltpu.VMEM((16, POOLW), F32), pltpu.VMEM((8, POOLW), F32)], reverse=True)


def _rope_tables(positions):
    S = positions.shape[0]
    inv = ROPE_THETA ** (-jnp.arange(0, 16, 2, dtype=F32) / 16)
    ang = positions.astype(F32)[:, None] * inv
    cos, sin = jnp.cos(ang), jnp.sin(ang)
    c64 = jnp.concatenate([cos, cos, jnp.ones((S, 48), F32)], axis=1)
    s64 = jnp.concatenate([-sin, sin, jnp.zeros((S, 48), F32)], axis=1)
    return jnp.concatenate([c64, c64], axis=1), jnp.concatenate([s64, s64], axis=1)


def _partner(x):
    lane = lax.broadcasted_iota(jnp.int32, x.shape, 1) % 64
    return jnp.where(lane < 8, pltpu.roll(x, LANES - 8, axis=1), jnp.where(lane < 16, pltpu.roll(x, 8, axis=1), 0.0))


def _rope(x, c, s):
    return x * c + _partner(x) * s


def _rope_t(x, c, s):
    return x * c + _partner(x * s)


def _rows_of(r, n, d):
    return pl.ds(r, n, stride=d) if d > 1 else pl.ds(0, n)


def _head_masks(shape):
    lane = lax.broadcasted_iota(jnp.int32, shape, 1) // 64
    return [lane == h for h in range(4)]


def _only(mask, x):
    return jnp.where(mask, x, jnp.zeros_like(x))


def _rope_perm(u, ctab, stab, ts=256):
    S = u.shape[0]
    nch = ATT_W // LANES

    def body(*refs):
        chunks, (c_ref, s_ref), outs = refs[:3 * nch], refs[3 * nch:3 * nch + 2], refs[3 * nch + 2:]
        for g, d in enumerate(DILS):
            n = ts // d
            for r in range(d):
                rows = _rows_of(r, n, d)
                c, s = c_ref[rows, :], s_ref[rows, :]
                for which in range(3):
                    parts = [chunks[which * nch + j][rows, :] for j in (2 * g, 2 * g + 1)]
                    if which < 2:
                        parts = [_rope(x, c, s) for x in parts]
                    outs[which * 3 + g][r] = jnp.concatenate(parts, axis=1).astype(MXU)

    base = (IN_W - 3 * ATT_W) // LANES
    in_specs = [pl.BlockSpec((ts, LANES), lambda i, cb=base + k: (i, cb)) for k in range(3 * nch)]
    in_specs += [pl.BlockSpec((ts, LANES), lambda i: (i, 0))] * 2
    out_specs = [pl.BlockSpec((d, ts // d, ATT_O), lambda i: (0, i, 0)) for _ in range(3) for d in DILS]
    out_shape = [jax.ShapeDtypeStruct((d, S // d, ATT_O), MXU) for _ in range(3) for d in DILS]
    res = pl.pallas_call(body, grid=(S // ts,), in_specs=in_specs, out_specs=out_specs, out_shape=out_shape,
                         compiler_params=_params(("arbitrary",), 32), name="rope_perm")(*([u] * (3 * nch)), ctab, stab)
    return [[res[which * 3 + g].reshape(S, ATT_O) for g in range(3)] for which in range(3)]


def _rope_unperm_bwd(dqkv, ctab, stab, ts=256):
    S = dqkv[0][0].shape[0]
    nch = ATT_W // LANES

    def body(*refs):
        ins, (c_ref, s_ref, o_ref, scr) = refs[:9], refs[9:]
        for g, d in enumerate(DILS):
            n = ts // d
            for r in range(d):
                rows = _rows_of(r, n, d)
                c, s = c_ref[rows, :], s_ref[rows, :]
                for which in range(3):
                    v = ins[which * 3 + g][r]
                    for half in range(2):
                        x = v[:, half * LANES:(half + 1) * LANES]
                        scr.at[which * nch + 2 * g + half][rows, :] = _rope_t(x, c, s) if which < 2 else x
        for j in range(3 * nch):
            o_ref[:, j * LANES:(j + 1) * LANES] = scr[j].astype(o_ref.dtype)

    in_specs = [pl.BlockSpec((d, ts // d, ATT_O), lambda i: (0, i, 0)) for _ in range(3) for d in DILS]
    in_specs += [pl.BlockSpec((ts, LANES), lambda i: (i, 0))] * 2
    args = [dqkv[which][g].reshape(d, S // d, ATT_O) for which in range(3) for g, d in enumerate(DILS)]
    return pl.pallas_call(body, grid=(S // ts,), in_specs=in_specs, out_specs=pl.BlockSpec((ts, 3 * ATT_W), lambda i: (i, 0)),
                          out_shape=jax.ShapeDtypeStruct((S, 3 * ATT_W), MXU), scratch_shapes=[pltpu.VMEM((3 * nch, ts, LANES), F32)],
                          compiler_params=_params(("arbitrary",), 32), name="rope_unperm_bwd")(*args, ctab, stab)


def _band_mask_keys(has_prev):
    r = lax.broadcasted_iota(jnp.int32, (QB, 2 * QB), 0)
    c = lax.broadcasted_iota(jnp.int32, (QB, 2 * QB), 1)
    return ((c < QB) & (c >= r) & has_prev) | ((c >= QB) & (c - QB <= r))


def _band_mask_queries(has_next):
    r = lax.broadcasted_iota(jnp.int32, (2 * QB, QB), 0)
    c = lax.broadcasted_iota(jnp.int32, (2 * QB, QB), 1)
    return ((r < QB) & (c <= r)) | ((r >= QB) & (c >= r - QB) & has_next)


def _blk(fn):
    return pl.BlockSpec((QB, ATT_O), fn)


_CUR = lambda b: (b, 0)
_PREV = lambda b: (jnp.maximum(b - 1, 0), 0)


def _attn_fwd(g, q, k, v):
    S = q.shape[0]
    nblk = S // QB // DILS[g]

    def body(q_ref, kc_ref, kp_ref, vc_ref, vp_ref, o_ref, m_ref, l_ref):
        ok = _band_mask_keys((pl.program_id(0) & (nblk - 1)) > 0)
        k2 = jnp.concatenate([kp_ref[...], kc_ref[...]], axis=0)
        v2 = jnp.concatenate([vp_ref[...], vc_ref[...]], axis=0)
        qv = q_ref[...]
        hm_kv, hm_o = _head_masks((2 * QB, ATT_O)), _head_masks((QB, ATT_O))
        o_acc = jnp.zeros((QB, ATT_O), F32)
        m_acc = jnp.zeros((QB, ATT_O), F32)
        l_acc = jnp.zeros((QB, ATT_O), F32)
        for h in range(4):
            s = jnp.where(ok, _dot_nt(qv, _only(hm_kv[h], k2)) * ATT_SCALE, NEG)
            m = jnp.max(s, axis=1, keepdims=True)
            p = jnp.exp(s - m)
            o_acc = o_acc + _dot(p.astype(MXU), _only(hm_kv[h], v2))
            m_acc = jnp.where(hm_o[h], m, m_acc)
            l_acc = jnp.where(hm_o[h], jnp.sum(p, axis=1, keepdims=True), l_acc)
        o_ref[...] = o_acc
        m_ref[...] = m_acc
        l_ref[...] = l_acc

    shp = jax.ShapeDtypeStruct((S, ATT_O), F32)
    return pl.pallas_call(body, grid=(S // QB,), in_specs=[_blk(_CUR), _blk(_CUR), _blk(_PREV), _blk(_CUR), _blk(_PREV)],
                          out_specs=[_blk(_CUR)] * 3, out_shape=[shp, shp, shp], compiler_params=_params(("arbitrary",), 32),
                          name=f"attn_fwd_{g}")(q, k, k, v, v)


def _natural(ref, d, scr, ts):
    if d == 1:
        return ref[0]
    n = ts // d
    for r in range(d):
        v = ref[r]
        scr.at[0][pl.ds(r, n, stride=d), :] = v[:, 0:LANES]
        scr.at[1][pl.ds(r, n, stride=d), :] = v[:, LANES:2 * LANES]
    return jnp.concatenate([scr[0], scr[1]], axis=1)


def _attn_combine(oml, ts=256):
    S = oml[0][0].shape[0]

    def body(*refs):
        ins, (att_ref, out_ref, lse_ref, scr) = refs[:9], refs[9:]
        o, m, l = [[_natural(ins[3 * g + k], d, scr, ts) for g, d in enumerate(DILS)] for k in range(3)]
        mx = jnp.maximum(jnp.maximum(m[0], m[1]), m[2])
        w = [jnp.exp(m[g] - mx) for g in range(3)]
        den = w[0] * l[0] + w[1] * l[1] + w[2] * l[2]
        out = (w[0] * o[0] + w[1] * o[1] + w[2] * o[2]) / den
        out_ref[...] = out
        att_ref[...] = out.astype(MXU)
        lse_ref[...] = mx + jnp.log(den)

    in_specs = [pl.BlockSpec((d, ts // d, ATT_O), lambda i: (0, i, 0)) for d in DILS for _ in range(3)]
    args = [a.reshape(d, S // d, ATT_O) for d, grp in zip(DILS, oml) for a in grp]
    blk = pl.BlockSpec((ts, ATT_O), lambda i: (i, 0))
    return pl.pallas_call(body, grid=(S // ts,), in_specs=in_specs, out_specs=[blk, blk, blk],
                          out_shape=[jax.ShapeDtypeStruct((S, ATT_O), MXU), jax.ShapeDtypeStruct((S, ATT_O), F32),
                                     jax.ShapeDtypeStruct((S, ATT_O), F32)],
                          scratch_shapes=[pltpu.VMEM((2, ts, LANES), F32)], compiler_params=_params(("arbitrary",), 32),
                          name="attn_combine")(*args)


def _attn_bwd_prep(datt, o, lse, ts=256):
    S = datt.shape[0]

    def body(da0, da1, o_ref, l0, l1, *rest):
        outs, dl = rest[:9], rest[9]
        prod = jnp.concatenate([da0[...], da1[...]], axis=1) * o_ref[...]
        delta = jnp.zeros((ts, ATT_O), F32)
        for hm in _head_masks((ts, ATT_O)):
            delta = jnp.where(hm, jnp.sum(_only(hm, prod), axis=1, keepdims=True), delta)
        dl[0] = delta[:, 0:LANES]
        dl[1] = delta[:, LANES:2 * LANES]
        for g, d in enumerate(DILS):
            n = ts // d
            for r in range(d):
                rows = _rows_of(r, n, d)
                outs[g][r] = jnp.concatenate([da0[rows, :], da1[rows, :]], axis=1).astype(MXU)
                outs[3 + g][r] = jnp.concatenate([dl.at[0][rows, :], dl.at[1][rows, :]], axis=1)
                outs[6 + g][r] = jnp.concatenate([l0[rows, :], l1[rows, :]], axis=1)

    half = lambda j: pl.BlockSpec((ts, LANES), lambda i: (i, j))
    out_specs = [pl.BlockSpec((d, ts // d, ATT_O), lambda i: (0, i, 0)) for _ in range(3) for d in DILS]
    out_shape = [jax.ShapeDtypeStruct((d, S // d, ATT_O), dt) for dt in (MXU, F32, F32) for d in DILS]
    res = pl.pallas_call(body, grid=(S // ts,), in_specs=[half(0), half(1), pl.BlockSpec((ts, ATT_O), lambda i: (i, 0)), half(0), half(1)],
                         out_specs=out_specs, out_shape=out_shape, scratch_shapes=[pltpu.VMEM((2, ts, LANES), F32)],
                         compiler_params=_params(("arbitrary",), 32), name="attn_bwd_prep")(datt, datt, o, lse, lse)
    return [[res[k * 3 + g].reshape(S, ATT_O) for g in range(3)] for k in range(3)]


def _head_col(x, h):
    return x[:, h * 64:h * 64 + 1]


def _attn_dq(g, q, k, v, do, delta, lse):
    S = q.shape[0]
    nblk = S // QB // DILS[g]

    def body(q_ref, kc_ref, kp_ref, vc_ref, vp_ref, do_ref, dl_ref, lse_ref, dq_ref):
        ok = _band_mask_keys((pl.program_id(0) & (nblk - 1)) > 0)
        k2 = jnp.concatenate([kp_ref[...], kc_ref[...]], axis=0)
        v2 = jnp.concatenate([vp_ref[...], vc_ref[...]], axis=0)
        qv, dov, dl, lse_v = q_ref[...], do_ref[...], dl_ref[...], lse_ref[...]
        dq = jnp.zeros((QB, ATT_O), F32)
        for h, hm in enumerate(_head_masks((2 * QB, ATT_O))):
            kh = _only(hm, k2)
            p = jnp.where(ok, jnp.exp(_dot_nt(qv, kh) * ATT_SCALE - _head_col(lse_v, h)), 0.0)
            ds = p * (_dot_nt(dov, _only(hm, v2)) - _head_col(dl, h))
            dq = dq + _dot(ds.astype(MXU), kh)
        dq_ref[...] = dq * ATT_SCALE

    specs = [_blk(_CUR), _blk(_CUR), _blk(_PREV), _blk(_CUR), _blk(_PREV), _blk(_CUR), _blk(_CUR), _blk(_CUR)]
    return pl.pallas_call(body, grid=(S // QB,), in_specs=specs, out_specs=_blk(_CUR), out_shape=jax.ShapeDtypeStruct((S, ATT_O), F32),
                          compiler_params=_params(("arbitrary",), 32), name=f"attn_dq_{g}")(q, k, k, v, v, do, delta, lse)


def _attn_dkv(g, q, k, v, do, delta, lse):
    S = q.shape[0]
    nb = S // QB
    nblk = nb // DILS[g]

    def body(k_ref, v_ref, qc_ref, qn_ref, doc_ref, don_ref, dlc_ref, dln_ref, lc_ref, ln_ref, dk_ref, dv_ref):
        ok = _band_mask_queries(((pl.program_id(0) + 1) & (nblk - 1)) > 0)
        q2 = jnp.concatenate([qc_ref[...], qn_ref[...]], axis=0)
        do2 = jnp.concatenate([doc_ref[...], don_ref[...]], axis=0)
        dl2 = jnp.concatenate([dlc_ref[...], dln_ref[...]], axis=0)
        lse2 = jnp.concatenate([lc_ref[...], ln_ref[...]], axis=0)
        kv, vv = k_ref[...], v_ref[...]
        dk = jnp.zeros((QB, ATT_O), F32)
        dv = jnp.zeros((QB, ATT_O), F32)
        for h, hm in enumerate(_head_masks((2 * QB, ATT_O))):
            qh, doh = _only(hm, q2), _only(hm, do2)
            p = jnp.where(ok, jnp.exp(_dot_nt(qh, kv) * ATT_SCALE - _head_col(lse2, h)), 0.0)
            ds = p * (_dot_nt(doh, vv) - _head_col(dl2, h))
            dv = dv + _dot_tn(p.astype(MXU), doh)
            dk = dk + _dot_tn(ds.astype(MXU), qh)
        dk_ref[...] = dk * ATT_SCALE
        dv_ref[...] = dv

    nxt = _blk(lambda b: (jnp.minimum(b + 1, nb - 1), 0))
    cur = _blk(_CUR)
    shp = jax.ShapeDtypeStruct((S, ATT_O), F32)
    return pl.pallas_call(body, grid=(nb,), in_specs=[cur, cur, cur, nxt, cur, nxt, cur, nxt, cur, nxt], out_specs=[cur, cur],
                          out_shape=[shp, shp], compiler_params=_params(("arbitrary",), 32),
                          name=f"attn_dkv_{g}")(k, v, q, q, do, do, delta, delta, lse, lse)


def _merge_fwd(x0, u, a2, yb, att, wa, wb, wc, w_out, g_post, ts=256):
    S = x0.shape[0]

    def body(i, g, x_ref, gate_ref, a2_ref, yb_ref, att_ref, wa_ref, wb_ref, wc_ref, wo_ref, gp_ref, mg_ref, y_ref, xo_ref):
        merged = jax.nn.sigmoid(gate_ref[:, 0:D]) * _dot_nt(a2_ref[...], wa_ref[...])
        merged = merged + jax.nn.sigmoid(gate_ref[:, D:2 * D]) * _dot_nt(yb_ref[...], wb_ref[...])
        merged = merged + jax.nn.sigmoid(gate_ref[:, 2 * D:3 * D]) * _dot_nt(att_ref[...], wc_ref[...])
        mb = merged.astype(MXU)
        mg_ref[...] = mb
        y = _dot(mb, wo_ref[...])
        y_ref[...] = y
        xo_ref[...] = x_ref[...] + _rms(y, gp_ref[...])[0]

    ins = [("t", x0, D, 0), ("t", u, GATE_W, 0), ("t", a2, POOLW, 0), ("t", yb, POOLW, 0), ("t", att, ATT_O, 0),
           ("w", wa), ("w", wb), ("w", wc), ("w", w_out), ("w", g_post)]
    return _rows_call("merge_fwd", body, S, ts, ins, [("t", D, MXU), ("t", D, F32), ("t", D, F32)])


def _merge_bwd(dx, y1, u, a2, yb, att, wa, wb, wc, w_out, g_post, ts=256):
    S = dx.shape[0]

    def body(i, g, dx_ref, y_ref, gate_ref, a2_ref, yb_ref, att_ref, wa_ref, wb_ref, wc_ref, wo_ref, gp_ref,
             dy_ref, dgate_ref, dbra_ref, dbrb_ref, dbrc_ref, da2_ref, dyb_ref, datt_ref, dgp_ref):
        dxv, y = dx_ref[...], y_ref[...]
        dy, r = _rms_bwd(dxv * gp_ref[...], y)
        _acc(dgp_ref, g, jnp.sum(dxv * (y * r), axis=0, keepdims=True))
        dyb16 = dy.astype(MXU)
        dy_ref[...] = dyb16
        dm = _dot_nt(dyb16, wo_ref[...])
        for n, (src, w_ref, dbr_ref, din_ref) in enumerate(((a2_ref, wa_ref, dbra_ref, da2_ref), (yb_ref, wb_ref, dbrb_ref, dyb_ref),
                                                           (att_ref, wc_ref, dbrc_ref, datt_ref))):
            gt = jax.nn.sigmoid(gate_ref[:, n * D:(n + 1) * D])
            br = _dot_nt(src[...], w_ref[...])
            dgate_ref[:, n * D:(n + 1) * D] = (dm * br * gt * (1.0 - gt)).astype(dgate_ref.dtype)
            dbr = (dm * gt).astype(MXU)
            dbr_ref[...] = dbr
            din_ref[...] = _dot(dbr, w_ref[...])

    ins = [("t", dx, D, 0), ("t", y1, D, 0), ("t", u, GATE_W, 0), ("t", a2, POOLW, 0), ("t", yb, POOLW, 0), ("t", att, ATT_O, 0),
           ("w", wa), ("w", wb), ("w", wc), ("w", w_out), ("w", g_post)]
    outs = [("t", D, MXU), ("t", GATE_W, MXU), ("t", D, MXU), ("t", D, MXU), ("t", D, MXU), ("t", POOLW, F32), ("t", POOLW, F32),
            ("t", ATT_O, F32), ("a", (1, D), F32)]
    return _rows_call("merge_bwd", body, S, ts, ins, outs)


def _prenorm_bwd(name, dx_res, dh, x, g_pre, ts=512):
    S = x.shape[0]

    def body(i, g, dx_ref, dh_ref, x_ref, g_ref, o_ref, dg_ref):
        dhv, xv = dh_ref[...], x_ref[...]
        dxn, r = _rms_bwd(dhv * g_ref[...], xv)
        o_ref[...] = dx_ref[...] + dxn
        _acc(dg_ref, g, jnp.sum(dhv * (xv * r), axis=0, keepdims=True))

    ins = [("t", dx_res, D, 0), ("t", dh, D, 0), ("t", x, D, 0), ("w", g_pre)]
    return _rows_call(name, body, S, ts, ins, [("t", D, F32), ("a", (1, D), F32)])


def _mem_heads(qm, kv_ref):
    out = []
    for h in range(4):
        q = qm[:, h * 128:(h + 1) * 128].astype(MXU)
        k = kv_ref[:, h * 128:(h + 1) * 128]
        v = kv_ref[:, MEM_W + h * 128:MEM_W + (h + 1) * 128]
        sc = _dot_nt(q, k) * MEM_SCALE
        e = jnp.exp(sc - jnp.max(sc, axis=1, keepdims=True))
        out.append((e / jnp.sum(e, axis=1, keepdims=True), q, k, v))
    return out


def _mem_fwd(x1, kv, g_pre, w_mq, w_mo, g_post, ts=256):
    S = x1.shape[0]

    def body(i, g, x_ref, kv_ref, gq_ref, wq_ref, wo_ref, gp_ref, om_ref, h_ref, y_ref, xo_ref):
        x = x_ref[...]
        hb = _rms(x, gq_ref[...])[0].astype(MXU)
        h_ref[...] = hb
        qm = _dot(hb, wq_ref[...])
        om = jnp.concatenate([_dot(p.astype(MXU), v) for p, _, _, v in _mem_heads(qm, kv_ref)], axis=1).astype(MXU)
        om_ref[...] = om
        y = _dot_nt(om, wo_ref[...])
        y_ref[...] = y
        xo_ref[...] = x + _rms(y, gp_ref[...])[0]

    ins = [("t", x1, D, 0), ("w", kv), ("w", g_pre), ("w", w_mq), ("w", w_mo), ("w", g_post)]
    return _rows_call("mem_fwd", body, S, ts, ins, [("t", MEM_W, MXU), ("t", D, MXU), ("t", D, F32), ("t", D, F32)])


def _mem_bwd(dx2, ym, x1, kv, g_pre, w_mq, w_mo, g_post, ts=256):
    S = x1.shape[0]

    def body(i, g, dx_ref, y_ref, x_ref, kv_ref, gq_ref, wq_ref, wo_ref, gp_ref, dy_ref, dq_ref, dxo_ref, dgp_ref, dgq_ref, dkv_ref):
        dxv, y, x = dx_ref[...], y_ref[...], x_ref[...]
        dy, r = _rms_bwd(dxv * gp_ref[...], y)
        _acc(dgp_ref, g, jnp.sum(dxv * (y * r), axis=0, keepdims=True))
        dyb = dy.astype(MXU)
        dy_ref[...] = dyb
        dom = _dot(dyb, wo_ref[...])
        h, r1 = _rms(x, gq_ref[...])
        qm = _dot(h.astype(MXU), wq_ref[...])
        dqs = []

        @pl.when(g == 0)
        def _():
            dkv_ref[...] = jnp.zeros_like(dkv_ref)

        for hh, (p, q, k, v) in enumerate(_mem_heads(qm, kv_ref)):
            doh = dom[:, hh * 128:(hh + 1) * 128].astype(MXU)
            dp = _dot_nt(doh, v)
            dsc = (p * (dp - jnp.sum(dp * p, axis=1, keepdims=True)) * MEM_SCALE).astype(MXU)
            dqs.append(_dot(dsc, k))
            dkv_ref[:, hh * 128:(hh + 1) * 128] += _dot_tn(dsc, q)
            dkv_ref[:, MEM_W + hh * 128:MEM_W + (hh + 1) * 128] += _dot_tn(p.astype(MXU), doh)
        dq = jnp.concatenate(dqs, axis=1).astype(MXU)
        dq_ref[...] = dq
        dh = _dot_nt(dq, wq_ref[...])
        _acc(dgq_ref, g, jnp.sum(dh * (x * r1), axis=0, keepdims=True))
        dxo_ref[...] = dxv + _rms_bwd(dh * gq_ref[...], x)[0]

    ins = [("t", dx2, D, 0), ("t", ym, D, 0), ("t", x1, D, 0), ("w", kv), ("w", g_pre), ("w", w_mq), ("w", w_mo), ("w", g_post)]
    outs = [("t", D, MXU), ("t", MEM_W, MXU), ("t", D, F32), ("a", (1, D), F32), ("a", (1, D), F32), ("a", (256, D), F32)]
    return _rows_call("mem_bwd", body, S, ts, ins, outs)


def _gain_grad(name, dn, x):
    n = x.shape[0]

    def body(i, g, dn_ref, x_ref, o_ref):
        xv = x_ref[...]
        r = lax.rsqrt(jnp.mean(xv * xv, axis=-1, keepdims=True) + EPS)
        o_ref[...] = jnp.sum(dn_ref[...] * (xv * r), axis=0, keepdims=True)

    return _rows_call(name, body, n, n, [("t", dn, D, 0), ("t", x, D, 0)], [("a", (1, D), F32)])[0]


def _ffn_fwd(x2, u3, conv_f, w_down, g_post, ts=256):
    S = x2.shape[0]

    def body(i, g, x_ref, ua_ref, ub_ref, cw_ref, wd_ref, gp_ref, act_ref, y_ref, xo_ref, cu):
        @pl.when(g == 0)
        def _():
            cu[...] = jnp.zeros_like(cu)

        ua = ua_ref[...]
        c, _, _ = _conv3(ua, cu[...], cw_ref[...])
        act = (c * jax.nn.sigmoid(c) * ub_ref[...]).astype(MXU)
        act_ref[...] = act
        y = _dot(act, wd_ref[...])
        y_ref[...] = y
        xo_ref[...] = x_ref[...] + _rms(y, gp_ref[...])[0]
        cu[...] = ua[ts - 8:]

    ins = [("t", x2, D, 0), ("t", u3, D_FF, 0), ("t", u3, D_FF, 1), ("w", conv_f), ("w", w_down), ("w", g_post)]
    return _rows_call("ffn_fwd", body, S, ts, ins, [("t", D_FF, MXU), ("t", D, F32), ("t", D, F32)],
                      scratch=[pltpu.VMEM((8, D_FF), F32)], vmem_mb=56)


def _ffn_bwd(dx3, y3, u3, conv_f, w_down, g_post, ts=128):
    S = dx3.shape[0]

    def body(i, g, dx_ref, y_ref, ua_ref, ub_ref, uap_ref, cw_ref, wd_ref, gp_ref, dy_ref, du_ref, dgp_ref, dcw_ref, cdc):
        @pl.when(g == 0)
        def _():
            cdc[...] = jnp.zeros_like(cdc)

        dxv, y = dx_ref[...], y_ref[...]
        dy, r = _rms_bwd(dxv * gp_ref[...], y)
        _acc(dgp_ref, g, jnp.sum(dxv * (y * r), axis=0, keepdims=True))
        dyb = dy.astype(MXU)
        dy_ref[...] = dyb
        dact = _dot_nt(dyb, wd_ref[...])
        ua, w = ua_ref[...], cw_ref[...]
        c, u1, u2 = _conv3(ua, uap_ref[...] * (i > 0).astype(F32), w)
        sg = jax.nn.sigmoid(c)
        du_ref[:, D_FF:2 * D_FF] = (dact * (c * sg)).astype(du_ref.dtype)
        dc = dact * ub_ref[...] * (sg * (1.0 + c * (1.0 - sg)))
        du_ref[:, 0:D_FF] = _conv3_t(dc, cdc[...], w).astype(du_ref.dtype)
        dw = jnp.concatenate([jnp.sum(dc * u2, axis=0, keepdims=True), jnp.sum(dc * u1, axis=0, keepdims=True),
                              jnp.sum(dc * ua, axis=0, keepdims=True)], axis=0)
        _acc(dcw_ref, g, dw)
        cdc[...] = dc[:8]

    ins = [("t", dx3, D, 0), ("t", y3, D, 0), ("t", u3, D_FF, 0), ("t", u3, D_FF, 1), ("h", u3, 8, D_FF, 0), ("w", conv_f),
           ("w", w_down), ("w", g_post)]
    outs = [("t", D, MXU), ("t", 2 * D_FF, MXU), ("a", (1, D), F32), ("a", (3, D_FF), F32)]
    return _rows_call("ffn_bwd", body, S, ts, ins, outs, scratch=[pltpu.VMEM((8, D_FF), F32)], reverse=True, vmem_mb=56)


def _loss_head(x, target, ts=512):
    S = x.shape[0]

    def body(i, g, x_ref, t_ref, dx_ref, acc_ref):
        diff = x_ref[...] - t_ref[...]
        dx_ref[...] = diff * (1.0 / D)
        col = jnp.sum(diff * diff, axis=0, keepdims=True)
        part = col[:, 0:LANES]
        for j in range(1, D // LANES):
            part = part + col[:, j * LANES:(j + 1) * LANES]
        row = lax.broadcasted_iota(jnp.int32, (8, LANES), 0)
        _acc(acc_ref, g, jnp.where(row == 0, jnp.broadcast_to(part, (8, LANES)), 0.0))

    return _rows_call("loss_head", body, S, ts, [("t", x, D, 0), ("t", target, D, 0)], [("t", D, F32), ("a", (8, LANES), F32)])


def _layer_weights(big, small, l):
    w_in = big['w_in']
    pool_w = small['pool_w'][l].astype(MXU)
    wblk = jnp.zeros((POOLW, POOLW), MXU)
    for g in range(4):
        wblk = lax.dynamic_update_slice(wblk, pool_w[g], (g * 96, g * 96))
    vec = lambda n: small[n][l].reshape(1, -1)
    return dict(
        w_in=jnp.concatenate([w_in[IN_W - GATE_W:], w_in[:IN_W - GATE_W]], axis=0),
        wblk=wblk, pool_scale=vec('pool_scale'), conv_b=small['conv_b_w'][l], wa=big['w_branch_a'], wb=big['w_branch_b'],
        wc=big['w_branch_c'], w_out=big['w_out'], w_mq=big['w_mq'], w_mkv=big['w_mkv'], w_mo=big['w_mo'],
        w_up=big['w_up'], conv_f=small['conv_ffn_w'][l], w_down=big['w_down'],
        g_mix_pre=vec('norm_mix_pre'), g_mix_post=vec('norm_mix_post'), g_mem_pre=vec('norm_mem_pre'),
        g_mem_post=vec('norm_mem_post'), g_memkv=vec('norm_memkv'), g_ffn_pre=vec('norm_ffn_pre'), g_ffn_post=vec('norm_ffn_post'))


def _layer_fwd(x0, mem, W, ctab, stab):
    sv = dict(x0=x0)
    sv['u'], sv['h1'] = _norm_mm("in_proj", x0, W['g_mix_pre'], W['w_in'], ts=512, tn=768, wt=True)
    sv['a2'], sv['yb'] = _poolconv_fwd(sv['u'], W['wblk'], W['pool_scale'], W['conv_b'])
    sv['qkv'] = q3, k3, v3 = _rope_perm(sv['u'], ctab, stab)
    sv['att'], sv['o'], sv['lse'] = _attn_combine([_attn_fwd(g, q3[g], k3[g], v3[g]) for g in range(3)])
    sv['merged'], sv['y1'], sv['x1'] = _merge_fwd(x0, sv['u'], sv['a2'], sv['yb'], sv['att'], W['wa'], W['wb'], W['wc'],
                                                  W['w_out'], W['g_mix_post'])
    sv['kv'], sv['memn'] = _norm_mm("mem_kv", mem, W['g_memkv'], W['w_mkv'], ts=256, tn=D, out_dtype=MXU)
    sv['om'], sv['h2'], sv['ym'], sv['x2'] = _mem_fwd(sv['x1'], sv['kv'], W['g_mem_pre'], W['w_mq'], W['w_mo'], W['g_mem_post'])
    sv['u3'], sv['h3'] = _norm_mm("up_proj", sv['x2'], W['g_ffn_pre'], W['w_up'], ts=512, tn=512, wt=True)
    sv['act'], sv['y3'], x3 = _ffn_fwd(sv['x2'], sv['u3'], W['conv_f'], W['w_down'], W['g_ffn_post'])
    return x3, sv


def _layer_bwd(dx3, mem, W, sv, ctab, stab):
    g = {}
    dy3, du3, g['norm_ffn_post'], g['conv_ffn_w'] = _ffn_bwd(dx3, sv['y3'], sv['u3'], W['conv_f'], W['w_down'], W['g_ffn_post'])
    g['w_down'] = _mm_tn("dw_down", sv['act'], dy3, cap_k=256)
    g['w_up'] = _mm_tn("dw_up", du3, sv['h3'])
    dh3 = _mm_nn("dh_up", du3, W['w_up'], ts=512, tn=512)
    dx2, g['norm_ffn_pre'] = _prenorm_bwd("ffn_pre_bwd", dx3, dh3, sv['x2'], W['g_ffn_pre'])
    dym, dqm, dx1, g['norm_mem_post'], g['norm_mem_pre'], dkv = _mem_bwd(dx2, sv['ym'], sv['x1'], sv['kv'], W['g_mem_pre'],
                                                                       W['w_mq'], W['w_mo'], W['g_mem_post'])
    g['w_mo'] = _mm_tn("dw_mo", dym, sv['om'])
    g['w_mq'] = _mm_tn("dw_mq", sv['h2'], dqm)
    dkvb = dkv.astype(MXU)
    g['w_mkv'] = _mm_tn("dw_mkv", sv['memn'], dkvb)
    g['norm_memkv'] = _gain_grad("memkv_gain", _mm_nt("d_memn", dkvb, W['w_mkv'], ts=256, tn=512), mem)
    dy1, dgate, dbra, dbrb, dbrc, da2, dyb, datt, g['norm_mix_post'] = _merge_bwd(
        dx1, sv['y1'], sv['u'], sv['a2'], sv['yb'], sv['att'], W['wa'], W['wb'], W['wc'], W['w_out'], W['g_mix_post'])
    g['w_out'] = _mm_tn("dw_out", sv['merged'], dy1)
    g['w_branch_a'] = _mm_tn("dw_a", dbra, sv['a2'])
    g['w_branch_b'] = _mm_tn("dw_b", dbrb, sv['yb'])
    g['w_branch_c'] = _mm_tn("dw_c", dbrc, sv['att'])
    dabc, g['pool_scale'], dwblk, g['conv_b_w'] = _poolconv_bwd(sv['u'], da2, dyb, W['wblk'], W['pool_scale'], W['conv_b'])
    g['pool_w'] = jnp.stack([dwblk[k * 96:(k + 1) * 96, k * 96:(k + 1) * 96] for k in range(4)])
    q3, k3, v3 = sv['qkv']
    do3, dl3, lse3 = _attn_bwd_prep(datt, sv['o'], sv['lse'])
    dq3 = [_attn_dq(i, q3[i], k3[i], v3[i], do3[i], dl3[i], lse3[i]) for i in range(3)]
    dkv3 = [_attn_dkv(i, q3[i], k3[i], v3[i], do3[i], dl3[i], lse3[i]) for i in range(3)]
    dqkv = _rope_unperm_bwd([dq3, [a for a, _ in dkv3], [b for _, b in dkv3]], ctab, stab)
    du = jnp.concatenate([dgate, dabc, dqkv], axis=1)
    dw_in = _mm_tn("dw_in", du, sv['h1'], cap_k=768)
    g['w_in'] = jnp.concatenate([dw_in[GATE_W:], dw_in[:GATE_W]], axis=0)
    dh1 = _mm_nn("dh_in", du, W['w_in'], ts=256, tn=512)
    dx0, g['norm_mix_pre'] = _prenorm_bwd("mix_pre_bwd", dx1, dh1, sv['x0'], W['g_mix_pre'])
    return dx0, g


def _local_step(x, mem, positions, target, big, small):
    ctab, stab = _rope_tables(positions)
    Ws = [_layer_weights(big[l], small, l) for l in range(DEPTH)]
    saved = []
    for l in range(DEPTH):
        x, sv = _layer_fwd(x, mem, Ws[l], ctab, stab)
        saved.append(sv)
    dx, acc = _loss_head(x, target)
    loss = jnp.sum(acc) * (0.5 / D)
    grads = [None] * DEPTH
    for l in reversed(range(DEPTH)):
        dx, grads[l] = _layer_bwd(dx, mem, Ws[l], saved[l], ctab, stab)
    return loss, dx, grads


_HBM = pl.BlockSpec(memory_space=pl.ANY)
MESH_ID = pl.DeviceIdType.MESH


def _all_gather(name, xs):
    n = len(xs)

    def body(*refs):
        x_refs, out_refs = refs[:n], refs[n:2 * n]
        send_sems, recv_sems, local_sems = refs[2 * n:]
        x, y, c = lax.axis_index("x"), lax.axis_index("y"), lax.axis_index("c")
        me, sibling = (x, y, c), (x, y, 1 - c)
        chips = [(1 - x, y), (x, 1 - y), (1 - x, 1 - y)]

        def slot(a, p):
            return out_refs[a].at[4 * p[0] + 2 * p[1] + p[2]]

        def copy(a, k, block, to, src=None):
            return pltpu.make_async_remote_copy(src_ref=slot(a, block) if src is None else src, dst_ref=slot(a, block),
                                                send_sem=send_sems.at[a, k], recv_sem=recv_sems.at[a, k], device_id=to,
                                                device_id_type=MESH_ID)

        started = []
        for a in range(n):
            mine = pltpu.make_async_copy(x_refs[a], slot(a, me), local_sems.at[a])
            mine.start()
            started.append(mine)
        first = []
        for a in range(n):
            first.append(copy(a, 0, me, sibling, src=x_refs[a]))
            first += [copy(a, 1 + j, me, (*chip, c), src=x_refs[a]) for j, chip in enumerate(chips)]
        for cp in first:
            cp.start()
        passed = []
        for j, chip in enumerate(chips):
            for a in range(n):
                copy(a, 1 + j, (*chip, c), me).wait_recv()
                fw = copy(a, 4 + j, (*chip, c), sibling)
                fw.start()
                passed.append(fw)
        for a in range(n):
            copy(a, 0, sibling, me).wait_recv()
            for j, chip in enumerate(chips):
                copy(a, 4 + j, (*chip, 1 - c), me).wait_recv()
        for cp in first + passed:
            cp.wait_send()
        for mine in started:
            mine.wait()

    return pl.pallas_call(
        body, out_shape=[jax.ShapeDtypeStruct((N_DEV,) + x.shape, x.dtype) for x in xs], in_specs=[_HBM] * n, out_specs=[_HBM] * n,
        scratch_shapes=[pltpu.SemaphoreType.DMA((n, 7)), pltpu.SemaphoreType.DMA((n, 7)), pltpu.SemaphoreType.DMA((n,))],
        name=name)(*xs)


def _exchange(name, gs):
    n = len(gs)

    def body(*refs):
        g_refs, out_refs = refs[:n], refs[n:2 * n]
        send_sems, recv_sems, local_sems = refs[2 * n:]
        x, y, c = lax.axis_index("x"), lax.axis_index("y"), lax.axis_index("c")
        me = 4 * x + 2 * y + c
        copies = []
        for a in range(n):
            mine = pltpu.make_async_copy(g_refs[a].at[me], out_refs[a].at[me], local_sems.at[a])
            mine.start()
            copies.append(mine)
        for r in range(1, N_DEV):
            px, py, pc = x ^ ((r >> 2) & 1), y ^ ((r >> 1) & 1), c ^ (r & 1)
            for a in range(n):
                cp = pltpu.make_async_remote_copy(src_ref=g_refs[a].at[4 * px + 2 * py + pc], dst_ref=out_refs[a].at[me],
                                                  send_sem=send_sems.at[a, r - 1], recv_sem=recv_sems.at[a, r - 1],
                                                  device_id=(px, py, pc), device_id_type=MESH_ID)
                cp.start()
                copies.append(cp)
        for cp in copies:
            cp.wait()

    return pl.pallas_call(
        body, out_shape=[jax.ShapeDtypeStruct(g.shape, g.dtype) for g in gs], in_specs=[_HBM] * n, out_specs=[_HBM] * n,
        scratch_shapes=[pltpu.SemaphoreType.DMA((n, N_DEV - 1)), pltpu.SemaphoreType.DMA((n, N_DEV - 1)), pltpu.SemaphoreType.DMA((n,))],
        name=name)(*gs)


def _row_tile(rows, cols, budget):
    if rows * cols * 4 <= budget or rows % 16:
        return rows
    best = 16
    for t in range(16, rows + 1, 16):
        if rows % t == 0 and t * cols * 4 <= budget:
            best = t
    return best


def _sum_slots(name, recv):
    _, R, C = recv.shape
    tr = _row_tile(R, C, 1 << 20)

    def body(r_ref, o_ref):
        g = r_ref[0].astype(F32)
        for k in range(1, N_DEV):
            g = g + r_ref[k].astype(F32)
        o_ref[...] = g

    return pl.pallas_call(body, grid=(R // tr,), in_specs=[pl.BlockSpec((N_DEV, tr, C), lambda i: (0, i, 0))],
                          out_specs=pl.BlockSpec((tr, C), lambda i: (i, 0)), out_shape=jax.ShapeDtypeStruct((R, C), F32),
                          compiler_params=_params(("arbitrary",), 32), name=name)(recv)


def _adamw(name, g, w, m, v):
    R, C = w.shape
    tr = _row_tile(R, C, 1 << 20)
    c1 = 1.0 - ADAM_B1 ** ADAM_STEP
    c2 = 1.0 - ADAM_B2 ** ADAM_STEP

    def body(g_ref, w_ref, m_ref, v_ref, d_ref, mo_ref, vo_ref):
        gv = g_ref[...]
        mn = ADAM_B1 * m_ref[...] + (1.0 - ADAM_B1) * gv
        vn = ADAM_B2 * v_ref[...] + (1.0 - ADAM_B2) * (gv * gv)
        mo_ref[...] = mn
        vo_ref[...] = vn
        d_ref[...] = -ADAM_LR * ((mn / c1) / (jnp.sqrt(vn / c2) + ADAM_EPS) + ADAM_WD * w_ref[...])

    blk = pl.BlockSpec((tr, C), lambda i: (i, 0))
    shp = jax.ShapeDtypeStruct((R, C), F32)
    return pl.pallas_call(body, grid=(R // tr,), in_specs=[blk, blk, blk, blk], out_specs=[blk, blk, blk], out_shape=[shp, shp, shp],
                          compiler_params=_params(("arbitrary",), 32), name=name)(g, w, m, v)


def _pad_flat(a, n):
    a = a.reshape(-1)
    return jnp.pad(a, (0, n - a.shape[0]))


def _seg(n):
    return -(-n // FLAT_ALIGN) * FLAT_ALIGN


def _to_blocks(full, axis):
    shp = full.shape
    return jnp.moveaxis(full.reshape(shp[:axis] + (N_DEV, shp[axis] // N_DEV) + shp[axis + 1:]), axis, 0)


def _from_blocks(blocks, axis):
    b = jnp.moveaxis(blocks, 0, axis)
    shp = b.shape
    return b.reshape(shp[:axis] + (shp[axis] * shp[axis + 1],) + shp[axis + 2:])


def _as_rows(shard, n):
    return shard.T if SHARD_AXIS[n] == 2 else shard


def _gather_weights(w):
    conv = jnp.concatenate([_pad_flat(w[n], _seg(w[n].size)) for n in F32_GATHERED]).reshape(-1, LANES)
    big = []
    for l in range(DEPTH):
        got = _all_gather(f"weights_all_gather_{l}", [_as_rows(w[n][l], n).astype(MXU) for n in BIG] + ([conv] if l == 0 else []))
        big.append({n: o.reshape(-1, o.shape[-1]) for n, o in zip(BIG, got)})
        if l == 0:
            conv_all = got[-1].reshape(N_DEV, -1)
    full, off = {}, 0
    for n in F32_GATHERED:
        full[n] = _from_blocks(conv_all[:, off:off + w[n].size].reshape((N_DEV,) + w[n].shape), 2)
        off += _seg(w[n].size)
    return big, full


def kernel(x, mem, positions, norm_mix_pre, norm_mix_post, w_in, pool_w, pool_scale, conv_b_w, w_branch_a, w_branch_b, w_branch_c, w_out, norm_mem_pre, norm_mem_post, norm_memkv, w_mq, w_mkv, w_mo, norm_ffn_pre, norm_ffn_post, w_up, conv_ffn_w, w_down, loss_target, m_norm_mix_pre, m_norm_mix_post, m_w_in, m_pool_w, m_pool_scale, m_conv_b_w, m_w_branch_a, m_w_branch_b, m_w_branch_c, m_w_out, m_norm_mem_pre, m_norm_mem_post, m_norm_memkv, m_w_mq, m_w_mkv, m_w_mo, m_norm_ffn_pre, m_norm_ffn_post, m_w_up, m_conv_ffn_w, m_w_down, v_norm_mix_pre, v_norm_mix_post, v_w_in, v_pool_w, v_pool_scale, v_conv_b_w, v_w_branch_a, v_w_branch_b, v_w_branch_c, v_w_out, v_norm_mem_pre, v_norm_mem_post, v_norm_memkv, v_w_mq, v_w_mkv, v_w_mo, v_norm_ffn_pre, v_norm_ffn_post, v_w_up, v_conv_ffn_w, v_w_down):
    w = dict(norm_mix_pre=norm_mix_pre, norm_mix_post=norm_mix_post, w_in=w_in, pool_w=pool_w, pool_scale=pool_scale, conv_b_w=conv_b_w, w_branch_a=w_branch_a, w_branch_b=w_branch_b, w_branch_c=w_branch_c, w_out=w_out, norm_mem_pre=norm_mem_pre, norm_mem_post=norm_mem_post, norm_memkv=norm_memkv, w_mq=w_mq, w_mkv=w_mkv, w_mo=w_mo, norm_ffn_pre=norm_ffn_pre, norm_ffn_post=norm_ffn_post, w_up=w_up, conv_ffn_w=conv_ffn_w, w_down=w_down)
    m = dict(norm_mix_pre=m_norm_mix_pre, norm_mix_post=m_norm_mix_post, w_in=m_w_in, pool_w=m_pool_w, pool_scale=m_pool_scale, conv_b_w=m_conv_b_w, w_branch_a=m_w_branch_a, w_branch_b=m_w_branch_b, w_branch_c=m_w_branch_c, w_out=m_w_out, norm_mem_pre=m_norm_mem_pre, norm_mem_post=m_norm_mem_post, norm_memkv=m_norm_memkv, w_mq=m_w_mq, w_mkv=m_w_mkv, w_mo=m_w_mo, norm_ffn_pre=m_norm_ffn_pre, norm_ffn_post=m_norm_ffn_post, w_up=m_w_up, conv_ffn_w=m_conv_ffn_w, w_down=m_w_down)
    v = dict(norm_mix_pre=v_norm_mix_pre, norm_mix_post=v_norm_mix_post, w_in=v_w_in, pool_w=v_pool_w, pool_scale=v_pool_scale, conv_b_w=v_conv_b_w, w_branch_a=v_w_branch_a, w_branch_b=v_w_branch_b, w_branch_c=v_w_branch_c, w_out=v_w_out, norm_mem_pre=v_norm_mem_pre, norm_mem_post=v_norm_mem_post, norm_memkv=v_norm_memkv, w_mq=v_w_mq, w_mkv=v_w_mkv, w_mo=v_w_mo, norm_ffn_pre=v_norm_ffn_pre, norm_ffn_post=v_norm_ffn_post, w_up=v_w_up, conv_ffn_w=v_conv_ffn_w, w_down=v_w_down)

    big, conv_full = _gather_weights(w)
    small = {n: w[n] for n in WEIGHTS if n not in SHARD_AXIS}
    small.update(conv_full)
    loss, dx, grads = _local_step(x[0], mem[0], positions[0], loss_target[0], big, small)
    loss = lax.psum(loss, MESH_AXES)

    misc_names = [n for n in WEIGHTS if n not in BIG]
    stacked = {n: jnp.stack([grads[l][n].reshape(small[n].shape[1:]) for l in range(DEPTH)]) for n in misc_names}
    rows = [(_to_blocks(stacked[n], 2) if n in SHARD_AXIS else jnp.broadcast_to(stacked[n][None], (N_DEV,) + stacked[n].shape))
            for n in misc_names]
    segs = [_seg(w[n].size) for n in misc_names]
    misc = jnp.concatenate([jnp.pad(r.reshape(N_DEV, -1), ((0, 0), (0, s - r[0].size))) for r, s in zip(rows, segs)],
                           axis=1).reshape(N_DEV, -1, LANES)
    g_out, per_layer = {}, {}
    for l in reversed(range(DEPTH)):
        recv = _exchange(f"grad_exchange_{l}", [grads[l][n].reshape(N_DEV, -1, grads[l][n].shape[-1]) for n in BIG] + ([misc] if l == 0 else []))
        for n, r in zip(BIG, recv):
            per_layer[n, l] = _as_rows(_sum_slots(f"sum_{n}_{l}", r), n)
    misc_sum = _sum_slots("sum_misc", recv[-1]).reshape(-1)
    off = 0
    for n, s in zip(misc_names, segs):
        g_out[n] = misc_sum[off:off + w[n].size].reshape(w[n].shape)
        off += s
    for n in BIG:
        g_out[n] = jnp.stack([per_layer[n, l] for l in range(DEPTH)])

    res = [[], [], [], []]
    for n in WEIGHTS:
        shp = w[n].shape
        d, mn, vn = _adamw(f"adamw_{n}", *[a.reshape(-1, shp[-1]) for a in (g_out[n], w[n], m[n], v[n])])
        for k, a in enumerate((g_out[n], d, mn, vn)):
            res[k].append(a.reshape(shp))
    return (loss, dx[None], *res[0], *res[1], *res[2], *res[3])
```

```python
import jax
import jax.numpy as jnp
from jax import lax
from jax.experimental import pallas as pl
from jax.experimental.pallas import tpu as pltpu

F32 = jnp.float32
MXU = jnp.bfloat16
HI = lax.Precision.HIGHEST

D = 1024
DEPTH = 2
POOLW = 384
ATT_W = 768
ATT_O = 256
GATE_W = 3 * D
IN_W = 6912
MEM_W = 512
D_FF = 2816
EPS = 1e-6
ROPE_THETA = 500000.0
QB = 128
DILS = (1, 4, 16)
NEG = -1e30
MEM_SCALE = 128 ** -0.5
ATT_SCALE = 0.125

ADAM_LR, ADAM_B1, ADAM_B2, ADAM_EPS, ADAM_WD, ADAM_STEP = 0.001, 0.9, 0.999, 1e-08, 0.01, 10

N_DEV = 8
MESH_AXES = ("x", "y", "c")
LANES = 128
FLAT_ALIGN = 2048
ROW_TILE = 1024

WEIGHTS = ['norm_mix_pre', 'norm_mix_post', 'w_in', 'pool_w', 'pool_scale', 'conv_b_w', 'w_branch_a', 'w_branch_b',
           'w_branch_c', 'w_out', 'norm_mem_pre', 'norm_mem_post', 'norm_memkv', 'w_mq', 'w_mkv', 'w_mo',
           'norm_ffn_pre', 'norm_ffn_post', 'w_up', 'conv_ffn_w', 'w_down']
SHARD_AXIS = {'w_in': 2, 'conv_b_w': 2, 'w_branch_a': 2, 'w_branch_b': 2, 'w_branch_c': 2, 'w_out': 1, 'w_mq': 1,
              'w_mkv': 1, 'w_mo': 2, 'w_up': 2, 'conv_ffn_w': 2, 'w_down': 1}
F32_GATHERED = ('conv_b_w', 'conv_ffn_w')
BIG = [n for n in WEIGHTS if n in SHARD_AXIS and n not in F32_GATHERED]


def _params(sem, vmem_mb):
    return pltpu.CompilerParams(dimension_semantics=sem, vmem_limit_bytes=vmem_mb << 20)


def _dot(a, b, prec=None):
    return lax.dot_general(a, b, (((1,), (0,)), ((), ())), preferred_element_type=F32, precision=prec)


def _dot_nt(a, b, prec=None):
    return lax.dot_general(a, b, (((1,), (1,)), ((), ())), preferred_element_type=F32, precision=prec)


def _dot_tn(a, b, prec=None):
    return lax.dot_general(a, b, (((0,), (0,)), ((), ())), preferred_element_type=F32, precision=prec)


def _tile(n, cap):
    if n <= cap:
        return n
    best = None
    for t in range(LANES, cap + 1, LANES):
        if n % t == 0:
            best = t
    assert best is not None, (n, cap)
    return best


def _rms(x, g):
    r = lax.rsqrt(jnp.mean(x * x, axis=-1, keepdims=True) + EPS)
    return x * r * g, r


def _rms_bwd(w, y):
    r = lax.rsqrt(jnp.mean(y * y, axis=-1, keepdims=True) + EPS)
    return r * w - y * (r * r * r) * jnp.mean(w * y, axis=-1, keepdims=True), r


def _rows_call(name, body, n_rows, ts, ins, outs, scratch=(), reverse=False, vmem_mb=48):
    nt = n_rows // ts
    assert nt * ts == n_rows

    def tile_of(g):
        return (nt - 1 - g) if reverse else g

    in_specs, args = [], []
    for op in ins:
        if op[0] == "t":
            _, a, cw, cb = op
            in_specs.append(pl.BlockSpec((ts, cw), lambda g, cb=cb: (tile_of(g), cb)))
        elif op[0] == "h":
            _, a, hr, cw, cb = op
            in_specs.append(pl.BlockSpec((hr, cw), lambda g, cb=cb, k=ts // hr: (jnp.maximum(tile_of(g) * k - 1, 0), cb)))
        else:
            _, a = op
            in_specs.append(pl.BlockSpec(a.shape, lambda g, n=a.ndim: (0,) * n))
        args.append(a)
    out_specs, out_shape = [], []
    for op in outs:
        if op[0] == "t":
            _, cols, dt = op
            out_specs.append(pl.BlockSpec((ts, cols), lambda g: (tile_of(g), 0)))
            out_shape.append(jax.ShapeDtypeStruct((n_rows, cols), dt))
        else:
            _, shp, dt = op
            out_specs.append(pl.BlockSpec(shp, lambda g, n=len(shp): (0,) * n))
            out_shape.append(jax.ShapeDtypeStruct(shp, dt))

    def kern(*refs):
        g = pl.program_id(0)
        body(tile_of(g), g, *refs)

    return pl.pallas_call(kern, grid=(nt,), in_specs=in_specs, out_specs=out_specs, out_shape=out_shape,
                          scratch_shapes=list(scratch), compiler_params=_params(("arbitrary",), vmem_mb), name=name)(*args)


def _acc(ref, g, val):
    @pl.when(g == 0)
    def _():
        ref[...] = val

    @pl.when(g != 0)
    def _():
        ref[...] += val


def _norm_mm(name, x, g, w, ts, tn, out_dtype=F32, wt=False):
    S, K = x.shape
    N = w.shape[0] if wt else w.shape[1]

    def body(x_ref, g_ref, w_ref, o_ref, h_ref, hs):
        @pl.when(pl.program_id(1) == 0)
        def _():
            h, _ = _rms(x_ref[...], g_ref[...])
            hs[...] = h.astype(MXU)
            h_ref[...] = h.astype(MXU)

        o_ref[...] = (_dot_nt if wt else _dot)(hs[...], w_ref[...]).astype(out_dtype)

    w_spec = pl.BlockSpec((tn, K), lambda i, j: (j, 0)) if wt else pl.BlockSpec((K, tn), lambda i, j: (0, j))
    return pl.pallas_call(
        body, grid=(S // ts, N // tn),
        in_specs=[pl.BlockSpec((ts, K), lambda i, j: (i, 0)), pl.BlockSpec((1, K), lambda i, j: (0, 0)), w_spec],
        out_specs=[pl.BlockSpec((ts, tn), lambda i, j: (i, j)), pl.BlockSpec((ts, K), lambda i, j: (i, 0))],
        out_shape=[jax.ShapeDtypeStruct((S, N), out_dtype), jax.ShapeDtypeStruct((S, K), MXU)],
        scratch_shapes=[pltpu.VMEM((ts, K), MXU)],
        compiler_params=_params(("arbitrary", "arbitrary"), 48), name=name)(x, g, w)


def _mm_nt(name, a, b, ts, tn, out_dtype=F32):
    M, K = a.shape
    N = b.shape[0]

    def body(a_ref, b_ref, o_ref):
        o_ref[...] = _dot_nt(a_ref[...], b_ref[...]).astype(out_dtype)

    return pl.pallas_call(
        body, grid=(M // ts, N // tn),
        in_specs=[pl.BlockSpec((ts, K), lambda i, j: (i, 0)), pl.BlockSpec((tn, K), lambda i, j: (j, 0))],
        out_specs=pl.BlockSpec((ts, tn), lambda i, j: (i, j)), out_shape=jax.ShapeDtypeStruct((M, N), out_dtype),
        compiler_params=_params(("arbitrary", "arbitrary"), 48), name=name)(a, b)


def _mm_tn(name, a, b, cap_k=512, cap_n=512, out_dtype=MXU):
    S, K = a.shape
    N = b.shape[1]
    tk, tn = _tile(K, cap_k), _tile(N, cap_n)

    def body(a_ref, b_ref, o_ref):
        o_ref[...] = _dot_tn(a_ref[...], b_ref[...]).astype(out_dtype)

    return pl.pallas_call(
        body, grid=(K // tk, N // tn),
        in_specs=[pl.BlockSpec((S, tk), lambda i, j: (0, i)), pl.BlockSpec((S, tn), lambda i, j: (0, j))],
        out_specs=pl.BlockSpec((tk, tn), lambda i, j: (i, j)), out_shape=jax.ShapeDtypeStruct((K, N), out_dtype),
        compiler_params=_params(("arbitrary", "arbitrary"), 48), name=name)(a, b)


def _pool_cols(shape):
    col = lax.broadcasted_iota(jnp.int32, shape, 1)
    return col < 96, col < 192, col < 288


def _pool_select(s2, s4, s8, s16):
    c1, c2, c3 = _pool_cols(s2.shape)
    return jnp.where(c1, s2, jnp.where(c2, s4, jnp.where(c3, s8, s16)))


def _pool_cnt(t0, ts):
    c1, c2, c3 = _pool_cols((ts, POOLW))
    win = jnp.where(c1, 2, jnp.where(c2, 4, jnp.where(c3, 8, 16)))
    t = t0 + lax.broadcasted_iota(jnp.int32, (ts, POOLW), 0)
    return jnp.minimum(t + 1, win).astype(F32)


def _pooled(a, prev, t0):
    ts = a.shape[0]
    ext = jnp.concatenate([prev, a], axis=0)
    s2 = ext + pltpu.roll(ext, 1, axis=0)
    s4 = s2 + pltpu.roll(s2, 2, axis=0)
    s8 = s4 + pltpu.roll(s4, 4, axis=0)
    s16 = s8 + pltpu.roll(s8, 8, axis=0)
    sums = _pool_select(s2, s4, s8, s16)[16:]
    return sums / _pool_cnt(t0, ts) - a


def _conv3(z, prev8, w):
    ext = jnp.concatenate([prev8, z], axis=0)
    z1 = pltpu.roll(ext, 1, axis=0)[8:]
    z2 = pltpu.roll(ext, 2, axis=0)[8:]
    return w[0:1] * z2 + w[1:2] * z1 + w[2:3] * z, z1, z2


def _conv3_t(dc, next8, w):
    ts = dc.shape[0]
    ext = jnp.concatenate([dc, next8], axis=0)
    n = ts + 8
    u1 = pltpu.roll(ext, n - 1, axis=0)[:ts]
    u2 = pltpu.roll(ext, n - 2, axis=0)[:ts]
    return w[2:3] * dc + w[1:2] * u1 + w[0:1] * u2


def _poolconv_fwd(u, wblk, pool_scale, conv_b, ts=256):
    S = u.shape[0]

    def body(i, g, a_ref, bx_ref, bb_ref, bc_ref, wblk_ref, ps_ref, cw_ref, a2_ref, yb_ref, ca, cz):
        @pl.when(g == 0)
        def _():
            ca[...] = jnp.zeros_like(ca)
            cz[...] = jnp.zeros_like(cz)

        a = a_ref[...]
        p = _pooled(a, ca[...], i * ts)
        mixed = _dot(p.astype(MXU), wblk_ref[...])
        a2_ref[...] = (mixed * ps_ref[...]).astype(MXU)
        z = bc_ref[...] * bx_ref[...]
        conv, _, _ = _conv3(z, cz[...], cw_ref[...])
        yb_ref[...] = (bb_ref[...] * conv).astype(MXU)
        ca[...] = a[ts - 16:]
        cz[...] = z[ts - 8:]

    ins = [("t", u, POOLW, 8), ("t", u, POOLW, 9), ("t", u, POOLW, 10), ("t", u, POOLW, 11), ("w", wblk), ("w", pool_scale),
           ("w", conv_b)]
    return _rows_call("poolconv_fwd", body, S, ts, ins, [("t", POOLW, MXU), ("t", POOLW, MXU)],
                      scratch=[pltpu.VMEM((16, POOLW), F32), pltpu.VMEM((8, POOLW), F32)])


def _poolconv_bwd(u, d_a2, d_yb, wblk, pool_scale, conv_b, ts=256):
    S = u.shape[0]

    def body(i, g, a_ref, bx_ref, bb_ref, bc_ref, ap_ref, bxp_ref, bcp_ref, da2_ref, dyb_ref, wblk_ref, ps_ref, cw_ref,
             o_ref, dps_ref, dwb_ref, dcw_ref, ce, cdz):
        @pl.when(g == 0)
        def _():
            ce[...] = jnp.zeros_like(ce)
            cdz[...] = jnp.zeros_like(cdz)

        first = (i > 0).astype(F32)
        a = a_ref[...]
        p = _pooled(a, ap_ref[...] * first, i * ts)
        pb = p.astype(MXU)
        mixed = _dot(pb, wblk_ref[...])
        da2 = da2_ref[...]
        dmixed = (da2 * ps_ref[...]).astype(MXU)
        dp = _dot_nt(dmixed, wblk_ref[...])
        _acc(dps_ref, g, jnp.sum(da2 * mixed, axis=0, keepdims=True))
        _acc(dwb_ref, g, _dot_tn(pb, dmixed))
        e = dp / _pool_cnt(i * ts, ts)
        ext = jnp.concatenate([e, ce[...]], axis=0)
        n = ts + 16
        f2 = ext + pltpu.roll(ext, n - 1, axis=0)
        f4 = f2 + pltpu.roll(f2, n - 2, axis=0)
        f8 = f4 + pltpu.roll(f4, n - 4, axis=0)
        f16 = f8 + pltpu.roll(f8, n - 8, axis=0)
        o_ref[:, 0:POOLW] = (_pool_select(f2, f4, f8, f16)[:ts] - dp).astype(o_ref.dtype)
        ce[...] = e[:16]

        bx, bb, bc = bx_ref[...], bb_ref[...], bc_ref[...]
        z = bc * bx
        w = cw_ref[...]
        conv, z1, z2 = _conv3(z, bxp_ref[...] * bcp_ref[...] * first, w)
        dyb = dyb_ref[...]
        dconv = dyb * bb
        dz = _conv3_t(dconv, cdz[...], w)
        o_ref[:, POOLW:2 * POOLW] = (dz * bc).astype(o_ref.dtype)
        o_ref[:, 2 * POOLW:3 * POOLW] = (dyb * conv).astype(o_ref.dtype)
        o_ref[:, 3 * POOLW:4 * POOLW] = (dz * bx).astype(o_ref.dtype)
        dw = jnp.concatenate([jnp.sum(dconv * z2, axis=0, keepdims=True), jnp.sum(dconv * z1, axis=0, keepdims=True),
                              jnp.sum(dconv * z, axis=0, keepdims=True)], axis=0)
        _acc(dcw_ref, g, dw)
        cdz[...] = dconv[:8]

    ins = [("t", u, POOLW, 8), ("t", u, POOLW, 9), ("t", u, POOLW, 10), ("t", u, POOLW, 11),
           ("h", u, 16, POOLW, 8), ("h", u, 8, POOLW, 9), ("h", u, 8, POOLW, 11),
           ("t", d_a2, POOLW, 0), ("t", d_yb, POOLW, 0), ("w", wblk), ("w", pool_scale), ("w", conv_b)]
    outs = [("t", 4 * POOLW, MXU), ("a", (1, POOLW), F32), ("a", (POOLW, POOLW), F32), ("a", (3, POOLW), F32)]
    return _rows_call("poolconv_bwd", body, S, ts, ins, outs,
                      scratch=[pltpu.VMEM((16, POOLW), F32), pltpu.VMEM((8, POOLW), F32)], reverse=True)


def _rope_tables(positions):
    S = positions.shape[0]
    inv = ROPE_THETA ** (-jnp.arange(0, 16, 2, dtype=F32) / 16)
    ang = positions.astype(F32)[:, None] * inv
    cos, sin = jnp.cos(ang), jnp.sin(ang)
    c64 = jnp.concatenate([cos, cos, jnp.ones((S, 48), F32)], axis=1)
    s64 = jnp.concatenate([-sin, sin, jnp.zeros((S, 48), F32)], axis=1)
    return jnp.concatenate([c64, c64], axis=1), jnp.concatenate([s64, s64], axis=1)


def _partner(x):
    lane = lax.broadcasted_iota(jnp.int32, x.shape, 1) % 64
    return jnp.where(lane < 8, pltpu.roll(x, LANES - 8, axis=1), jnp.where(lane < 16, pltpu.roll(x, 8, axis=1), 0.0))


def _rope(x, c, s):
    return x * c + _partner(x) * s


def _rope_t(x, c, s):
    return x * c + _partner(x * s)


def _rows_of(r, n, d):
    return pl.ds(r, n, stride=d) if d > 1 else pl.ds(0, n)


def _head_masks(shape):
    lane = lax.broadcasted_iota(jnp.int32, shape, 1) // 64
    return [lane == h for h in range(4)]


def _only(mask, x):
    return jnp.where(mask, x, jnp.zeros_like(x))


def _rope_perm(u, ctab, stab, ts=256):
    S = u.shape[0]
    nch = ATT_W // LANES

    def body(*refs):
        chunks, (c_ref, s_ref), outs = refs[:3 * nch], refs[3 * nch:3 * nch + 2], refs[3 * nch + 2:]
        for g, d in enumerate(DILS):
            n = ts // d
            for r in range(d):
                rows = _rows_of(r, n, d)
                c, s = c_ref[rows, :], s_ref[rows, :]
                for which in range(3):
                    parts = [chunks[which * nch + j][rows, :] for j in (2 * g, 2 * g + 1)]
                    if which < 2:
                        parts = [_rope(x, c, s) for x in parts]
                    outs[which * 3 + g][r] = jnp.concatenate(parts, axis=1).astype(MXU)

    base = (IN_W - 3 * ATT_W) // LANES
    in_specs = [pl.BlockSpec((ts, LANES), lambda i, cb=base + k: (i, cb)) for k in range(3 * nch)]
    in_specs += [pl.BlockSpec((ts, LANES), lambda i: (i, 0))] * 2
    out_specs = [pl.BlockSpec((d, ts // d, ATT_O), lambda i: (0, i, 0)) for _ in range(3) for d in DILS]
    out_shape = [jax.ShapeDtypeStruct((d, S // d, ATT_O), MXU) for _ in range(3) for d in DILS]
    res = pl.pallas_call(body, grid=(S // ts,), in_specs=in_specs, out_specs=out_specs, out_shape=out_shape,
                         compiler_params=_params(("arbitrary",), 32), name="rope_perm")(*([u] * (3 * nch)), ctab, stab)
    return [[res[which * 3 + g].reshape(S, ATT_O) for g in range(3)] for which in range(3)]


def _rope_unperm_bwd(dqkv, ctab, stab, ts=256):
    S = dqkv[0][0].shape[0]
    nch = ATT_W // LANES

    def body(*refs):
        ins, (c_ref, s_ref, o_ref, scr) = refs[:9], refs[9:]
        for g, d in enumerate(DILS):
            n = ts // d
            for r in range(d):
                rows = _rows_of(r, n, d)
                c, s = c_ref[rows, :], s_ref[rows, :]
                for which in range(3):
                    v = ins[which * 3 + g][r]
                    for half in range(2):
                        x = v[:, half * LANES:(half + 1) * LANES]
                        scr.at[which * nch + 2 * g + half][rows, :] = _rope_t(x, c, s) if which < 2 else x
        for j in range(3 * nch):
            o_ref[:, j * LANES:(j + 1) * LANES] = scr[j].astype(o_ref.dtype)

    in_specs = [pl.BlockSpec((d, ts // d, ATT_O), lambda i: (0, i, 0)) for _ in range(3) for d in DILS]
    in_specs += [pl.BlockSpec((ts, LANES), lambda i: (i, 0))] * 2
    args = [dqkv[which][g].reshape(d, S // d, ATT_O) for which in range(3) for g, d in enumerate(DILS)]
    return pl.pallas_call(body, grid=(S // ts,), in_specs=in_specs, out_specs=pl.BlockSpec((ts, 3 * ATT_W), lambda i: (i, 0)),
                          out_shape=jax.ShapeDtypeStruct((S, 3 * ATT_W), MXU), scratch_shapes=[pltpu.VMEM((3 * nch, ts, LANES), F32)],
                          compiler_params=_params(("arbitrary",), 32), name="rope_unperm_bwd")(*args, ctab, stab)


def _band_mask_keys(has_prev):
    r = lax.broadcasted_iota(jnp.int32, (QB, 2 * QB), 0)
    c = lax.broadcasted_iota(jnp.int32, (QB, 2 * QB), 1)
    return ((c < QB) & (c >= r) & has_prev) | ((c >= QB) & (c - QB <= r))


def _band_mask_queries(has_next):
    r = lax.broadcasted_iota(jnp.int32, (2 * QB, QB), 0)
    c = lax.broadcasted_iota(jnp.int32, (2 * QB, QB), 1)
    return ((r < QB) & (c <= r)) | ((r >= QB) & (c >= r - QB) & has_next)


def _blk(fn):
    return pl.BlockSpec((QB, ATT_O), fn)


_CUR = lambda b: (b, 0)
_PREV = lambda b: (jnp.maximum(b - 1, 0), 0)


def _attn_fwd(g, q, k, v):
    S = q.shape[0]
    nblk = S // QB // DILS[g]

    def body(q_ref, kc_ref, kp_ref, vc_ref, vp_ref, o_ref, m_ref, l_ref):
        ok = _band_mask_keys((pl.program_id(0) & (nblk - 1)) > 0)
        k2 = jnp.concatenate([kp_ref[...], kc_ref[...]], axis=0)
        v2 = jnp.concatenate([vp_ref[...], vc_ref[...]], axis=0)
        qv = q_ref[...]
        hm_kv, hm_o = _head_masks((2 * QB, ATT_O)), _head_masks((QB, ATT_O))
        o_acc = jnp.zeros((QB, ATT_O), F32)
        m_acc = jnp.zeros((QB, ATT_O), F32)
        l_acc = jnp.zeros((QB, ATT_O), F32)
        for h in range(4):
            s = jnp.where(ok, _dot_nt(qv, _only(hm_kv[h], k2)) * ATT_SCALE, NEG)
            m = jnp.max(s, axis=1, keepdims=True)
            p = jnp.exp(s - m)
            o_acc = o_acc + _dot(p.astype(MXU), _only(hm_kv[h], v2))
            m_acc = jnp.where(hm_o[h], m, m_acc)
            l_acc = jnp.where(hm_o[h], jnp.sum(p, axis=1, keepdims=True), l_acc)
        o_ref[...] = o_acc
        m_ref[...] = m_acc
        l_ref[...] = l_acc

    shp = jax.ShapeDtypeStruct((S, ATT_O), F32)
    return pl.pallas_call(body, grid=(S // QB,), in_specs=[_blk(_CUR), _blk(_CUR), _blk(_PREV), _blk(_CUR), _blk(_PREV)],
                          out_specs=[_blk(_CUR)] * 3, out_shape=[shp, shp, shp], compiler_params=_params(("arbitrary",), 32),
                          name=f"attn_fwd_{g}")(q, k, k, v, v)


def _natural(ref, d, scr, ts):
    if d == 1:
        return ref[0]
    n = ts // d
    for r in range(d):
        v = ref[r]
        scr.at[0][pl.ds(r, n, stride=d), :] = v[:, 0:LANES]
        scr.at[1][pl.ds(r, n, stride=d), :] = v[:, LANES:2 * LANES]
    return jnp.concatenate([scr[0], scr[1]], axis=1)


def _attn_combine(oml, ts=256):
    S = oml[0][0].shape[0]

    def body(*refs):
        ins, (att_ref, out_ref, lse_ref, scr) = refs[:9], refs[9:]
        o, m, l = [[_natural(ins[3 * g + k], d, scr, ts) for g, d in enumerate(DILS)] for k in range(3)]
        mx = jnp.maximum(jnp.maximum(m[0], m[1]), m[2])
        w = [jnp.exp(m[g] - mx) for g in range(3)]
        den = w[0] * l[0] + w[1] * l[1] + w[2] * l[2]
        out = (w[0] * o[0] + w[1] * o[1] + w[2] * o[2]) / den
        out_ref[...] = out
        att_ref[...] = out.astype(MXU)
        lse_ref[...] = mx + jnp.log(den)

    in_specs = [pl.BlockSpec((d, ts // d, ATT_O), lambda i: (0, i, 0)) for d in DILS for _ in range(3)]
    args = [a.reshape(d, S // d, ATT_O) for d, grp in zip(DILS, oml) for a in grp]
    blk = pl.BlockSpec((ts, ATT_O), lambda i: (i, 0))
    return pl.pallas_call(body, grid=(S // ts,), in_specs=in_specs, out_specs=[blk, blk, blk],
                          out_shape=[jax.ShapeDtypeStruct((S, ATT_O), MXU), jax.ShapeDtypeStruct((S, ATT_O), F32),
                                     jax.ShapeDtypeStruct((S, ATT_O), F32)],
                          scratch_shapes=[pltpu.VMEM((2, ts, LANES), F32)], compiler_params=_params(("arbitrary",), 32),
                          name="attn_combine")(*args)


def _attn_bwd_prep(datt, o, lse, ts=256):
    S = datt.shape[0]

    def body(da0, da1, o_ref, l0, l1, *rest):
        outs, dl = rest[:9], rest[9]
        prod = jnp.concatenate([da0[...], da1[...]], axis=1) * o_ref[...]
        delta = jnp.zeros((ts, ATT_O), F32)
        for hm in _head_masks((ts, ATT_O)):
            delta = jnp.where(hm, jnp.sum(_only(hm, prod), axis=1, keepdims=True), delta)
        dl[0] = delta[:, 0:LANES]
        dl[1] = delta[:, LANES:2 * LANES]
        for g, d in enumerate(DILS):
            n = ts // d
            for r in range(d):
                rows = _rows_of(r, n, d)
                outs[g][r] = jnp.concatenate([da0[rows, :], da1[rows, :]], axis=1).astype(MXU)
                outs[3 + g][r] = jnp.concatenate([dl.at[0][rows, :], dl.at[1][rows, :]], axis=1)
                outs[6 + g][r] = jnp.concatenate([l0[rows, :], l1[rows, :]], axis=1)

    half = lambda j: pl.BlockSpec((ts, LANES), lambda i: (i, j))
    out_specs = [pl.BlockSpec((d, ts // d, ATT_O), lambda i: (0, i, 0)) for _ in range(3) for d in DILS]
    out_shape = [jax.ShapeDtypeStruct((d, S // d, ATT_O), dt) for dt in (MXU, F32, F32) for d in DILS]
    res = pl.pallas_call(body, grid=(S // ts,), in_specs=[half(0), half(1), pl.BlockSpec((ts, ATT_O), lambda i: (i, 0)), half(0), half(1)],
                         out_specs=out_specs, out_shape=out_shape, scratch_shapes=[pltpu.VMEM((2, ts, LANES), F32)],
                         compiler_params=_params(("arbitrary",), 32), name="attn_bwd_prep")(datt, datt, o, lse, lse)
    return [[res[k * 3 + g].reshape(S, ATT_O) for g in range(3)] for k in range(3)]


def _head_col(x, h):
    return x[:, h * 64:h * 64 + 1]


def _attn_dq(g, q, k, v, do, delta, lse):
    S = q.shape[0]
    nblk = S // QB // DILS[g]

    def body(q_ref, kc_ref, kp_ref, vc_ref, vp_ref, do_ref, dl_ref, lse_ref, dq_ref):
        ok = _band_mask_keys((pl.program_id(0) & (nblk - 1)) > 0)
        k2 = jnp.concatenate([kp_ref[...], kc_ref[...]], axis=0)
        v2 = jnp.concatenate([vp_ref[...], vc_ref[...]], axis=0)
        qv, dov, dl, lse_v = q_ref[...], do_ref[...], dl_ref[...], lse_ref[...]
        dq = jnp.zeros((QB, ATT_O), F32)
        for h, hm in enumerate(_head_masks((2 * QB, ATT_O))):
            kh = _only(hm, k2)
            p = jnp.where(ok, jnp.exp(_dot_nt(qv, kh) * ATT_SCALE - _head_col(lse_v, h)), 0.0)
            ds = p * (_dot_nt(dov, _only(hm, v2)) - _head_col(dl, h))
            dq = dq + _dot(ds.astype(MXU), kh)
        dq_ref[...] = dq * ATT_SCALE

    specs = [_blk(_CUR), _blk(_CUR), _blk(_PREV), _blk(_CUR), _blk(_PREV), _blk(_CUR), _blk(_CUR), _blk(_CUR)]
    return pl.pallas_call(body, grid=(S // QB,), in_specs=specs, out_specs=_blk(_CUR), out_shape=jax.ShapeDtypeStruct((S, ATT_O), F32),
                          compiler_params=_params(("arbitrary",), 32), name=f"attn_dq_{g}")(q, k, k, v, v, do, delta, lse)


def _attn_dkv(g, q, k, v, do, delta, lse):
    S = q.shape[0]
    nb = S // QB
    nblk = nb // DILS[g]

    def body(k_ref, v_ref, qc_ref, qn_ref, doc_ref, don_ref, dlc_ref, dln_ref, lc_ref, ln_ref, dk_ref, dv_ref):
        ok = _band_mask_queries(((pl.program_id(0) + 1) & (nblk - 1)) > 0)
        q2 = jnp.concatenate([qc_ref[...], qn_ref[...]], axis=0)
        do2 = jnp.concatenate([doc_ref[...], don_ref[...]], axis=0)
        dl2 = jnp.concatenate([dlc_ref[...], dln_ref[...]], axis=0)
        lse2 = jnp.concatenate([lc_ref[...], ln_ref[...]], axis=0)
        kv, vv = k_ref[...], v_ref[...]
        dk = jnp.zeros((QB, ATT_O), F32)
        dv = jnp.zeros((QB, ATT_O), F32)
        for h, hm in enumerate(_head_masks((2 * QB, ATT_O))):
            qh, doh = _only(hm, q2), _only(hm, do2)
            p = jnp.where(ok, jnp.exp(_dot_nt(qh, kv) * ATT_SCALE - _head_col(lse2, h)), 0.0)
            ds = p * (_dot_nt(doh, vv) - _head_col(dl2, h))
            dv = dv + _dot_tn(p.astype(MXU), doh)
            dk = dk + _dot_tn(ds.astype(MXU), qh)
        dk_ref[...] = dk * ATT_SCALE
        dv_ref[...] = dv

    nxt = _blk(lambda b: (jnp.minimum(b + 1, nb - 1), 0))
    cur = _blk(_CUR)
    shp = jax.ShapeDtypeStruct((S, ATT_O), F32)
    return pl.pallas_call(body, grid=(nb,), in_specs=[cur, cur, cur, nxt, cur, nxt, cur, nxt, cur, nxt], out_specs=[cur, cur],
                          out_shape=[shp, shp], compiler_params=_params(("arbitrary",), 32),
                          name=f"attn_dkv_{g}")(k, v, q, q, do, do, delta, delta, lse, lse)


def _merge_fwd(x0, u, a2, yb, att, wa, wb, wc, w_out, g_post, ts=256):
    S = x0.shape[0]

    def body(i, g, x_ref, gate_ref, a2_ref, yb_ref, att_ref, wa_ref, wb_ref, wc_ref, wo_ref, gp_ref, mg_ref, y_ref, xo_ref):
        merged = jax.nn.sigmoid(gate_ref[:, 0:D]) * _dot_nt(a2_ref[...], wa_ref[...])
        merged = merged + jax.nn.sigmoid(gate_ref[:, D:2 * D]) * _dot_nt(yb_ref[...], wb_ref[...])
        merged = merged + jax.nn.sigmoid(gate_ref[:, 2 * D:3 * D]) * _dot_nt(att_ref[...], wc_ref[...])
        mb = merged.astype(MXU)
        mg_ref[...] = mb
        y = _dot(mb, wo_ref[...])
        y_ref[...] = y
        xo_ref[...] = x_ref[...] + _rms(y, gp_ref[...])[0]

    ins = [("t", x0, D, 0), ("t", u, GATE_W, 0), ("t", a2, POOLW, 0), ("t", yb, POOLW, 0), ("t", att, ATT_O, 0),
           ("w", wa), ("w", wb), ("w", wc), ("w", w_out), ("w", g_post)]
    return _rows_call("merge_fwd", body, S, ts, ins, [("t", D, MXU), ("t", D, F32), ("t", D, F32)])


def _merge_bwd(dx, y1, u, a2, yb, att, wa, wb, wc, w_out, g_post, ts=256):
    S = dx.shape[0]

    def body(i, g, dx_ref, y_ref, gate_ref, a2_ref, yb_ref, att_ref, wa_ref, wb_ref, wc_ref, wo_ref, gp_ref,
             dy_ref, dgate_ref, dbra_ref, dbrb_ref, dbrc_ref, da2_ref, dyb_ref, datt_ref, dgp_ref):
        dxv, y = dx_ref[...], y_ref[...]
        dy, r = _rms_bwd(dxv * gp_ref[...], y)
        _acc(dgp_ref, g, jnp.sum(dxv * (y * r), axis=0, keepdims=True))
        dyb16 = dy.astype(MXU)
        dy_ref[...] = dyb16
        dm = _dot_nt(dyb16, wo_ref[...])
        for n, (src, w_ref, dbr_ref, din_ref) in enumerate(((a2_ref, wa_ref, dbra_ref, da2_ref), (yb_ref, wb_ref, dbrb_ref, dyb_ref),
                                                           (att_ref, wc_ref, dbrc_ref, datt_ref))):
            gt = jax.nn.sigmoid(gate_ref[:, n * D:(n + 1) * D])
            br = _dot_nt(src[...], w_ref[...])
            dgate_ref[:, n * D:(n + 1) * D] = (dm * br * gt * (1.0 - gt)).astype(dgate_ref.dtype)
            dbr = (dm * gt).astype(MXU)
            dbr_ref[...] = dbr
            din_ref[...] = _dot(dbr, w_ref[...])

    ins = [("t", dx, D, 0), ("t", y1, D, 0), ("t", u, GATE_W, 0), ("t", a2, POOLW, 0), ("t", yb, POOLW, 0), ("t", att, ATT_O, 0),
           ("w", wa), ("w", wb), ("w", wc), ("w", w_out), ("w", g_post)]
    outs = [("t", D, MXU), ("t", GATE_W, MXU), ("t", D, MXU), ("t", D, MXU), ("t", D, MXU), ("t", POOLW, F32), ("t", POOLW, F32),
            ("t", ATT_O, F32), ("a", (1, D), F32)]
    return _rows_call("merge_bwd", body, S, ts, ins, outs)


def _prenorm_bwd(name, dx_res, du, wt, x, g_pre, ts=256):
    S = x.shape[0]
    N = du.shape[1]

    def body(i, g, dx_ref, du_ref, wt_ref, x_ref, g_ref, o_ref, dg_ref):
        dhv, xv = _dot(du_ref[...], wt_ref[...]), x_ref[...]
        dxn, r = _rms_bwd(dhv * g_ref[...], xv)
        o_ref[...] = dx_ref[...] + dxn
        _acc(dg_ref, g, jnp.sum(dhv * (xv * r), axis=0, keepdims=True))

    ins = [("t", dx_res, D, 0), ("t", du, N, 0), ("w", wt), ("t", x, D, 0), ("w", g_pre)]
    return _rows_call(name, body, S, ts, ins, [("t", D, F32), ("a", (1, D), F32)], vmem_mb=52)


def _mem_heads(qm, kv_ref):
    out = []
    for h in range(4):
        q = qm[:, h * 128:(h + 1) * 128].astype(MXU)
        k = kv_ref[:, h * 128:(h + 1) * 128]
        v = kv_ref[:, MEM_W + h * 128:MEM_W + (h + 1) * 128]
        sc = _dot_nt(q, k) * MEM_SCALE
        e = jnp.exp(sc - jnp.max(sc, axis=1, keepdims=True))
        out.append((e / jnp.sum(e, axis=1, keepdims=True), q, k, v))
    return out


def _mem_fwd(x1, kv, g_pre, w_mq, w_mo, g_post, ts=256):
    S = x1.shape[0]

    def body(i, g, x_ref, kv_ref, gq_ref, wq_ref, wo_ref, gp_ref, om_ref, h_ref, y_ref, xo_ref):
        x = x_ref[...]
        hb = _rms(x, gq_ref[...])[0].astype(MXU)
        h_ref[...] = hb
        qm = _dot(hb, wq_ref[...])
        om = jnp.concatenate([_dot(p.astype(MXU), v) for p, _, _, v in _mem_heads(qm, kv_ref)], axis=1).astype(MXU)
        om_ref[...] = om
        y = _dot_nt(om, wo_ref[...])
        y_ref[...] = y
        xo_ref[...] = x + _rms(y, gp_ref[...])[0]

    ins = [("t", x1, D, 0), ("w", kv), ("w", g_pre), ("w", w_mq), ("w", w_mo), ("w", g_post)]
    return _rows_call("mem_fwd", body, S, ts, ins, [("t", MEM_W, MXU), ("t", D, MXU), ("t", D, F32), ("t", D, F32)])


def _mem_bwd(dx2, ym, x1, kv, g_pre, w_mq, w_mo, g_post, ts=256):
    S = x1.shape[0]

    def body(i, g, dx_ref, y_ref, x_ref, kv_ref, gq_ref, wq_ref, wo_ref, gp_ref, dy_ref, dq_ref, dxo_ref, dgp_ref, dgq_ref, dkv_ref):
        dxv, y, x = dx_ref[...], y_ref[...], x_ref[...]
        dy, r = _rms_bwd(dxv * gp_ref[...], y)
        _acc(dgp_ref, g, jnp.sum(dxv * (y * r), axis=0, keepdims=True))
        dyb = dy.astype(MXU)
        dy_ref[...] = dyb
        dom = _dot(dyb, wo_ref[...])
        h, r1 = _rms(x, gq_ref[...])
        qm = _dot(h.astype(MXU), wq_ref[...])
        dqs = []

        @pl.when(g == 0)
        def _():
            dkv_ref[...] = jnp.zeros_like(dkv_ref)

        for hh, (p, q, k, v) in enumerate(_mem_heads(qm, kv_ref)):
            doh = dom[:, hh * 128:(hh + 1) * 128].astype(MXU)
            dp = _dot_nt(doh, v)
            dsc = (p * (dp - jnp.sum(dp * p, axis=1, keepdims=True)) * MEM_SCALE).astype(MXU)
            dqs.append(_dot(dsc, k))
            dkv_ref[:, hh * 128:(hh + 1) * 128] += _dot_tn(dsc, q)
            dkv_ref[:, MEM_W + hh * 128:MEM_W + (hh + 1) * 128] += _dot_tn(p.astype(MXU), doh)
        dq = jnp.concatenate(dqs, axis=1).astype(MXU)
        dq_ref[...] = dq
        dh = _dot_nt(dq, wq_ref[...])
        _acc(dgq_ref, g, jnp.sum(dh * (x * r1), axis=0, keepdims=True))
        dxo_ref[...] = dxv + _rms_bwd(dh * gq_ref[...], x)[0]

    ins = [("t", dx2, D, 0), ("t", ym, D, 0), ("t", x1, D, 0), ("w", kv), ("w", g_pre), ("w", w_mq), ("w", w_mo), ("w", g_post)]
    outs = [("t", D, MXU), ("t", MEM_W, MXU), ("t", D, F32), ("a", (1, D), F32), ("a", (1, D), F32), ("a", (256, D), F32)]
    return _rows_call("mem_bwd", body, S, ts, ins, outs)


def _gain_grad(name, dn, x):
    n = x.shape[0]

    def body(i, g, dn_ref, x_ref, o_ref):
        xv = x_ref[...]
        r = lax.rsqrt(jnp.mean(xv * xv, axis=-1, keepdims=True) + EPS)
        o_ref[...] = jnp.sum(dn_ref[...] * (xv * r), axis=0, keepdims=True)

    return _rows_call(name, body, n, n, [("t", dn, D, 0), ("t", x, D, 0)], [("a", (1, D), F32)])[0]


def _ffn_fwd(x2, u3, conv_f, w_down, g_post, ts=256):
    S = x2.shape[0]

    def body(i, g, x_ref, ua_ref, ub_ref, cw_ref, wd_ref, gp_ref, act_ref, y_ref, xo_ref, cu):
        @pl.when(g == 0)
        def _():
            cu[...] = jnp.zeros_like(cu)

        ua = ua_ref[...]
        c, _, _ = _conv3(ua, cu[...], cw_ref[...])
        act = (c * jax.nn.sigmoid(c) * ub_ref[...]).astype(MXU)
        act_ref[...] = act
        y = _dot(act, wd_ref[...])
        y_ref[...] = y
        xo_ref[...] = x_ref[...] + _rms(y, gp_ref[...])[0]
        cu[...] = ua[ts - 8:]

    ins = [("t", x2, D, 0), ("t", u3, D_FF, 0), ("t", u3, D_FF, 1), ("w", conv_f), ("w", w_down), ("w", g_post)]
    return _rows_call("ffn_fwd", body, S, ts, ins, [("t", D_FF, MXU), ("t", D, F32), ("t", D, F32)],
                      scratch=[pltpu.VMEM((8, D_FF), F32)], vmem_mb=56)


def _ffn_bwd(dx3, y3, u3, conv_f, w_down, g_post, ts=128):
    S = dx3.shape[0]

    def body(i, g, dx_ref, y_ref, ua_ref, ub_ref, uap_ref, cw_ref, wd_ref, gp_ref, dy_ref, du_ref, dgp_ref, dcw_ref, cdc):
        @pl.when(g == 0)
        def _():
            cdc[...] = jnp.zeros_like(cdc)

        dxv, y = dx_ref[...], y_ref[...]
        dy, r = _rms_bwd(dxv * gp_ref[...], y)
        _acc(dgp_ref, g, jnp.sum(dxv * (y * r), axis=0, keepdims=True))
        dyb = dy.astype(MXU)
        dy_ref[...] = dyb
        dact = _dot_nt(dyb, wd_ref[...])
        ua, w = ua_ref[...], cw_ref[...]
        c, u1, u2 = _conv3(ua, uap_ref[...] * (i > 0).astype(F32), w)
        sg = jax.nn.sigmoid(c)
        du_ref[:, D_FF:2 * D_FF] = (dact * (c * sg)).astype(du_ref.dtype)
        dc = dact * ub_ref[...] * (sg * (1.0 + c * (1.0 - sg)))
        du_ref[:, 0:D_FF] = _conv3_t(dc, cdc[...], w).astype(du_ref.dtype)
        dw = jnp.concatenate([jnp.sum(dc * u2, axis=0, keepdims=True), jnp.sum(dc * u1, axis=0, keepdims=True),
                              jnp.sum(dc * ua, axis=0, keepdims=True)], axis=0)
        _acc(dcw_ref, g, dw)
        cdc[...] = dc[:8]

    ins = [("t", dx3, D, 0), ("t", y3, D, 0), ("t", u3, D_FF, 0), ("t", u3, D_FF, 1), ("h", u3, 8, D_FF, 0), ("w", conv_f),
           ("w", w_down), ("w", g_post)]
    outs = [("t", D, MXU), ("t", 2 * D_FF, MXU), ("a", (1, D), F32), ("a", (3, D_FF), F32)]
    return _rows_call("ffn_bwd", body, S, ts, ins, outs, scratch=[pltpu.VMEM((8, D_FF), F32)], reverse=True, vmem_mb=56)


def _loss_head(x, target, ts=512):
    S = x.shape[0]

    def body(i, g, x_ref, t_ref, dx_ref, acc_ref):
        diff = x_ref[...] - t_ref[...]
        dx_ref[...] = diff * (1.0 / D)
        col = jnp.sum(diff * diff, axis=0, keepdims=True)
        part = col[:, 0:LANES]
        for j in range(1, D // LANES):
            part = part + col[:, j * LANES:(j + 1) * LANES]
        row = lax.broadcasted_iota(jnp.int32, (8, LANES), 0)
        _acc(acc_ref, g, jnp.where(row == 0, jnp.broadcast_to(part, (8, LANES)), 0.0))

    return _rows_call("loss_head", body, S, ts, [("t", x, D, 0), ("t", target, D, 0)], [("t", D, F32), ("a", (8, LANES), F32)])


def _layer_weights(big, small, l):
    w_in = big['w_in']
    pool_w = small['pool_w'][l].astype(MXU)
    wblk = jnp.zeros((POOLW, POOLW), MXU)
    for g in range(4):
        wblk = lax.dynamic_update_slice(wblk, pool_w[g], (g * 96, g * 96))
    vec = lambda n: small[n][l].reshape(1, -1)
    return dict(
        w_in=jnp.concatenate([w_in[IN_W - GATE_W:], w_in[:IN_W - GATE_W]], axis=0),
        wblk=wblk, pool_scale=vec('pool_scale'), conv_b=small['conv_b_w'][l], wa=big['w_branch_a'], wb=big['w_branch_b'],
        wc=big['w_branch_c'], w_out=big['w_out'], w_mq=big['w_mq'], w_mkv=big['w_mkv'], w_mo=big['w_mo'],
        w_up=big['w_up'], conv_f=small['conv_ffn_w'][l], w_down=big['w_down'],
        g_mix_pre=vec('norm_mix_pre'), g_mix_post=vec('norm_mix_post'), g_mem_pre=vec('norm_mem_pre'),
        g_mem_post=vec('norm_mem_post'), g_memkv=vec('norm_memkv'), g_ffn_pre=vec('norm_ffn_pre'), g_ffn_post=vec('norm_ffn_post'))


def _layer_fwd(x0, mem, W, ctab, stab):
    sv = dict(x0=x0)
    sv['u'], sv['h1'] = _norm_mm("in_proj", x0, W['g_mix_pre'], W['w_in'], ts=1024, tn=768, wt=True)
    sv['a2'], sv['yb'] = _poolconv_fwd(sv['u'], W['wblk'], W['pool_scale'], W['conv_b'])
    sv['qkv'] = q3, k3, v3 = _rope_perm(sv['u'], ctab, stab)
    sv['att'], sv['o'], sv['lse'] = _attn_combine([_attn_fwd(g, q3[g], k3[g], v3[g]) for g in range(3)])
    sv['merged'], sv['y1'], sv['x1'] = _merge_fwd(x0, sv['u'], sv['a2'], sv['yb'], sv['att'], W['wa'], W['wb'], W['wc'],
                                                  W['w_out'], W['g_mix_post'])
    sv['kv'], sv['memn'] = _norm_mm("mem_kv", mem, W['g_memkv'], W['w_mkv'], ts=256, tn=D, out_dtype=MXU)
    sv['om'], sv['h2'], sv['ym'], sv['x2'] = _mem_fwd(sv['x1'], sv['kv'], W['g_mem_pre'], W['w_mq'], W['w_mo'], W['g_mem_post'])
    sv['u3'], sv['h3'] = _norm_mm("up_proj", sv['x2'], W['g_ffn_pre'], W['w_up'], ts=1024, tn=512, wt=True)
    sv['act'], sv['y3'], x3 = _ffn_fwd(sv['x2'], sv['u3'], W['conv_f'], W['w_down'], W['g_ffn_post'])
    return x3, sv


def _layer_bwd(dx3, mem, W, sv, ctab, stab):
    g = {}
    dy3, du3, g['norm_ffn_post'], g['conv_ffn_w'] = _ffn_bwd(dx3, sv['y3'], sv['u3'], W['conv_f'], W['w_down'], W['g_ffn_post'])
    g['w_down'] = _mm_tn("dw_down", sv['act'], dy3, cap_k=256)
    g['w_up'] = _mm_tn("dw_up", du3, sv['h3'])
    dx2, g['norm_ffn_pre'] = _prenorm_bwd("ffn_pre_bwd", dx3, du3, W['w_up'], sv['x2'], W['g_ffn_pre'])
    dym, dqm, dx1, g['norm_mem_post'], g['norm_mem_pre'], dkv = _mem_bwd(dx2, sv['ym'], sv['x1'], sv['kv'], W['g_mem_pre'],
                                                                       W['w_mq'], W['w_mo'], W['g_mem_post'])
    g['w_mo'] = _mm_tn("dw_mo", dym, sv['om'])
    g['w_mq'] = _mm_tn("dw_mq", sv['h2'], dqm)
    dkvb = dkv.astype(MXU)
    g['w_mkv'] = _mm_tn("dw_mkv", sv['memn'], dkvb)
    g['norm_memkv'] = _gain_grad("memkv_gain", _mm_nt("d_memn", dkvb, W['w_mkv'], ts=256, tn=512), mem)
    dy1, dgate, dbra, dbrb, dbrc, da2, dyb, datt, g['norm_mix_post'] = _merge_bwd(
        dx1, sv['y1'], sv['u'], sv['a2'], sv['yb'], sv['att'], W['wa'], W['wb'], W['wc'], W['w_out'], W['g_mix_post'])
    g['w_out'] = _mm_tn("dw_out", sv['merged'], dy1)
    g['w_branch_a'] = _mm_tn("dw_a", dbra, sv['a2'])
    g['w_branch_b'] = _mm_tn("dw_b", dbrb, sv['yb'])
    g['w_branch_c'] = _mm_tn("dw_c", dbrc, sv['att'])
    dabc, g['pool_scale'], dwblk, g['conv_b_w'] = _poolconv_bwd(sv['u'], da2, dyb, W['wblk'], W['pool_scale'], W['conv_b'])
    g['pool_w'] = jnp.stack([dwblk[k * 96:(k + 1) * 96, k * 96:(k + 1) * 96] for k in range(4)])
    q3, k3, v3 = sv['qkv']
    do3, dl3, lse3 = _attn_bwd_prep(datt, sv['o'], sv['lse'])
    dq3 = [_attn_dq(i, q3[i], k3[i], v3[i], do3[i], dl3[i], lse3[i]) for i in range(3)]
    dkv3 = [_attn_dkv(i, q3[i], k3[i], v3[i], do3[i], dl3[i], lse3[i]) for i in range(3)]
    dqkv = _rope_unperm_bwd([dq3, [a for a, _ in dkv3], [b for _, b in dkv3]], ctab, stab)
    du = jnp.concatenate([dgate, dabc, dqkv], axis=1)
    dw_in = _mm_tn("dw_in", du, sv['h1'], cap_k=768)
    g['w_in'] = jnp.concatenate([dw_in[GATE_W:], dw_in[:GATE_W]], axis=0)
    dx0, g['norm_mix_pre'] = _prenorm_bwd("mix_pre_bwd", dx1, du, W['w_in'], sv['x0'], W['g_mix_pre'])
    return dx0, g


def _local_step(x, mem, positions, target, big, small):
    ctab, stab = _rope_tables(positions)
    Ws = [_layer_weights(big[l], small, l) for l in range(DEPTH)]
    saved = []
    for l in range(DEPTH):
        x, sv = _layer_fwd(x, mem, Ws[l], ctab, stab)
        saved.append(sv)
    dx, acc = _loss_head(x, target)
    loss = jnp.sum(acc) * (0.5 / D)
    grads = [None] * DEPTH
    for l in reversed(range(DEPTH)):
        dx, grads[l] = _layer_bwd(dx, mem, Ws[l], saved[l], ctab, stab)
    return loss, dx, grads


_HBM = pl.BlockSpec(memory_space=pl.ANY)
MESH_ID = pl.DeviceIdType.MESH


def _all_gather(name, xs):
    n = len(xs)

    def body(*refs):
        x_refs, out_refs = refs[:n], refs[n:2 * n]
        send_sems, recv_sems, local_sems = refs[2 * n:]
        x, y, c = lax.axis_index("x"), lax.axis_index("y"), lax.axis_index("c")
        me, sibling = (x, y, c), (x, y, 1 - c)
        chips = [(1 - x, y), (x, 1 - y), (1 - x, 1 - y)]

        def slot(a, p):
            return out_refs[a].at[4 * p[0] + 2 * p[1] + p[2]]

        def copy(a, k, block, to, src=None):
            return pltpu.make_async_remote_copy(src_ref=slot(a, block) if src is None else src, dst_ref=slot(a, block),
                                                send_sem=send_sems.at[a, k], recv_sem=recv_sems.at[a, k], device_id=to,
                                                device_id_type=MESH_ID)

        started = []
        for a in range(n):
            mine = pltpu.make_async_copy(x_refs[a], slot(a, me), local_sems.at[a])
            mine.start()
            started.append(mine)
        first = []
        for a in range(n):
            first.append(copy(a, 0, me, sibling, src=x_refs[a]))
            first += [copy(a, 1 + j, me, (*chip, c), src=x_refs[a]) for j, chip in enumerate(chips)]
        for cp in first:
            cp.start()
        passed = []
        for j, chip in enumerate(chips):
            for a in range(n):
                copy(a, 1 + j, (*chip, c), me).wait_recv()
                fw = copy(a, 4 + j, (*chip, c), sibling)
                fw.start()
                passed.append(fw)
        for a in range(n):
            copy(a, 0, sibling, me).wait_recv()
            for j, chip in enumerate(chips):
                copy(a, 4 + j, (*chip, 1 - c), me).wait_recv()
        for cp in first + passed:
            cp.wait_send()
        for mine in started:
            mine.wait()

    return pl.pallas_call(
        body, out_shape=[jax.ShapeDtypeStruct((N_DEV,) + x.shape, x.dtype) for x in xs], in_specs=[_HBM] * n, out_specs=[_HBM] * n,
        scratch_shapes=[pltpu.SemaphoreType.DMA((n, 7)), pltpu.SemaphoreType.DMA((n, 7)), pltpu.SemaphoreType.DMA((n,))],
        name=name)(*xs)


def _exchange(name, gs):
    n = len(gs)

    def body(*refs):
        g_refs, out_refs = refs[:n], refs[n:2 * n]
        send_sems, recv_sems, local_sems = refs[2 * n:]
        x, y, c = lax.axis_index("x"), lax.axis_index("y"), lax.axis_index("c")
        me = 4 * x + 2 * y + c
        copies = []
        for a in range(n):
            mine = pltpu.make_async_copy(g_refs[a].at[me], out_refs[a].at[me], local_sems.at[a])
            mine.start()
            copies.append(mine)
        for r in range(1, N_DEV):
            px, py, pc = x ^ ((r >> 2) & 1), y ^ ((r >> 1) & 1), c ^ (r & 1)
            for a in range(n):
                cp = pltpu.make_async_remote_copy(src_ref=g_refs[a].at[4 * px + 2 * py + pc], dst_ref=out_refs[a].at[me],
                                                  send_sem=send_sems.at[a, r - 1], recv_sem=recv_sems.at[a, r - 1],
                                                  device_id=(px, py, pc), device_id_type=MESH_ID)
                cp.start()
                copies.append(cp)
        for cp in copies:
            cp.wait()

    return pl.pallas_call(
        body, out_shape=[jax.ShapeDtypeStruct(g.shape, g.dtype) for g in gs], in_specs=[_HBM] * n, out_specs=[_HBM] * n,
        scratch_shapes=[pltpu.SemaphoreType.DMA((n, N_DEV - 1)), pltpu.SemaphoreType.DMA((n, N_DEV - 1)), pltpu.SemaphoreType.DMA((n,))],
        name=name)(*gs)


def _row_tile(rows, cols, budget):
    if rows * cols * 4 <= budget or rows % 16:
        return rows
    best = 16
    for t in range(16, rows + 1, 16):
        if rows % t == 0 and t * cols * 4 <= budget:
            best = t
    return best


def _sum_slots(name, recv):
    _, R, C = recv.shape
    tr = _row_tile(R, C, 1 << 20)

    def body(r_ref, o_ref):
        g = r_ref[0].astype(F32)
        for k in range(1, N_DEV):
            g = g + r_ref[k].astype(F32)
        o_ref[...] = g

    return pl.pallas_call(body, grid=(R // tr,), in_specs=[pl.BlockSpec((N_DEV, tr, C), lambda i: (0, i, 0))],
                          out_specs=pl.BlockSpec((tr, C), lambda i: (i, 0)), out_shape=jax.ShapeDtypeStruct((R, C), F32),
                          compiler_params=_params(("arbitrary",), 32), name=name)(recv)


def _adamw(name, g, w, m, v):
    R, C = w.shape
    tr = _row_tile(R, C, 1 << 20)
    c1 = 1.0 - ADAM_B1 ** ADAM_STEP
    c2 = 1.0 - ADAM_B2 ** ADAM_STEP

    def body(g_ref, w_ref, m_ref, v_ref, d_ref, mo_ref, vo_ref):
        gv = g_ref[...]
        mn = ADAM_B1 * m_ref[...] + (1.0 - ADAM_B1) * gv
        vn = ADAM_B2 * v_ref[...] + (1.0 - ADAM_B2) * (gv * gv)
        mo_ref[...] = mn
        vo_ref[...] = vn
        d_ref[...] = -ADAM_LR * ((mn / c1) / (jnp.sqrt(vn / c2) + ADAM_EPS) + ADAM_WD * w_ref[...])

    blk = pl.BlockSpec((tr, C), lambda i: (i, 0))
    shp = jax.ShapeDtypeStruct((R, C), F32)
    return pl.pallas_call(body, grid=(R // tr,), in_specs=[blk, blk, blk, blk], out_specs=[blk, blk, blk], out_shape=[shp, shp, shp],
                          compiler_params=_params(("arbitrary",), 32), name=name)(g, w, m, v)


def _pad_flat(a, n):
    a = a.reshape(-1)
    return jnp.pad(a, (0, n - a.shape[0]))


def _seg(n):
    return -(-n // FLAT_ALIGN) * FLAT_ALIGN


def _to_blocks(full, axis):
    shp = full.shape
    return jnp.moveaxis(full.reshape(shp[:axis] + (N_DEV, shp[axis] // N_DEV) + shp[axis + 1:]), axis, 0)


def _from_blocks(blocks, axis):
    b = jnp.moveaxis(blocks, 0, axis)
    shp = b.shape
    return b.reshape(shp[:axis] + (shp[axis] * shp[axis + 1],) + shp[axis + 2:])


def _as_rows(shard, n):
    return shard.T if SHARD_AXIS[n] == 2 else shard


def _gather_weights(w):
    conv = jnp.concatenate([_pad_flat(w[n], _seg(w[n].size)) for n in F32_GATHERED]).reshape(-1, LANES)
    big = []
    for l in range(DEPTH):
        got = _all_gather(f"weights_all_gather_{l}", [_as_rows(w[n][l], n).astype(MXU) for n in BIG] + ([conv] if l == 0 else []))
        big.append({n: o.reshape(-1, o.shape[-1]) for n, o in zip(BIG, got)})
        if l == 0:
            conv_all = got[-1].reshape(N_DEV, -1)
    full, off = {}, 0
    for n in F32_GATHERED:
        full[n] = _from_blocks(conv_all[:, off:off + w[n].size].reshape((N_DEV,) + w[n].shape), 2)
        off += _seg(w[n].size)
    return big, full


def kernel(x, mem, positions, norm_mix_pre, norm_mix_post, w_in, pool_w, pool_scale, conv_b_w, w_branch_a, w_branch_b, w_branch_c, w_out, norm_mem_pre, norm_mem_post, norm_memkv, w_mq, w_mkv, w_mo, norm_ffn_pre, norm_ffn_post, w_up, conv_ffn_w, w_down, loss_target, m_norm_mix_pre, m_norm_mix_post, m_w_in, m_pool_w, m_pool_scale, m_conv_b_w, m_w_branch_a, m_w_branch_b, m_w_branch_c, m_w_out, m_norm_mem_pre, m_norm_mem_post, m_norm_memkv, m_w_mq, m_w_mkv, m_w_mo, m_norm_ffn_pre, m_norm_ffn_post, m_w_up, m_conv_ffn_w, m_w_down, v_norm_mix_pre, v_norm_mix_post, v_w_in, v_pool_w, v_pool_scale, v_conv_b_w, v_w_branch_a, v_w_branch_b, v_w_branch_c, v_w_out, v_norm_mem_pre, v_norm_mem_post, v_norm_memkv, v_w_mq, v_w_mkv, v_w_mo, v_norm_ffn_pre, v_norm_ffn_post, v_w_up, v_conv_ffn_w, v_w_down):
    w = dict(norm_mix_pre=norm_mix_pre, norm_mix_post=norm_mix_post, w_in=w_in, pool_w=pool_w, pool_scale=pool_scale, conv_b_w=conv_b_w, w_branch_a=w_branch_a, w_branch_b=w_branch_b, w_branch_c=w_branch_c, w_out=w_out, norm_mem_pre=norm_mem_pre, norm_mem_post=norm_mem_post, norm_memkv=norm_memkv, w_mq=w_mq, w_mkv=w_mkv, w_mo=w_mo, norm_ffn_pre=norm_ffn_pre, norm_ffn_post=norm_ffn_post, w_up=w_up, conv_ffn_w=conv_ffn_w, w_down=w_down)
    m = dict(norm_mix_pre=m_norm_mix_pre, norm_mix_post=m_norm_mix_post, w_in=m_w_in, pool_w=m_pool_w, pool_scale=m_pool_scale, conv_b_w=m_conv_b_w, w_branch_a=m_w_branch_a, w_branch_b=m_w_branch_b, w_branch_c=m_w_branch_c, w_out=m_w_out, norm_mem_pre=m_norm_mem_pre, norm_mem_post=m_norm_mem_post, norm_memkv=m_norm_memkv, w_mq=m_w_mq, w_mkv=m_w_mkv, w_mo=m_w_mo, norm_ffn_pre=m_norm_ffn_pre, norm_ffn_post=m_norm_ffn_post, w_up=m_w_up, conv_ffn_w=m_conv_ffn_w, w_down=m_w_down)
    v = dict(norm_mix_pre=v_norm_mix_pre, norm_mix_post=v_norm_mix_post, w_in=v_w_in, pool_w=v_pool_w, pool_scale=v_pool_scale, conv_b_w=v_conv_b_w, w_branch_a=v_w_branch_a, w_branch_b=v_w_branch_b, w_branch_c=v_w_branch_c, w_out=v_w_out, norm_mem_pre=v_norm_mem_pre, norm_mem_post=v_norm_mem_post, norm_memkv=v_norm_memkv, w_mq=v_w_mq, w_mkv=v_w_mkv, w_mo=v_w_mo, norm_ffn_pre=v_norm_ffn_pre, norm_ffn_post=v_norm_ffn_post, w_up=v_w_up, conv_ffn_w=v_conv_ffn_w, w_down=v_w_down)

    big, conv_full = _gather_weights(w)
    small = {n: w[n] for n in WEIGHTS if n not in SHARD_AXIS}
    small.update(conv_full)
    loss, dx, grads = _local_step(x[0], mem[0], positions[0], loss_target[0], big, small)
    loss = lax.psum(loss, MESH_AXES)

    misc_names = [n for n in WEIGHTS if n not in BIG]
    stacked = {n: jnp.stack([grads[l][n].reshape(small[n].shape[1:]) for l in range(DEPTH)]) for n in misc_names}
    rows = [(_to_blocks(stacked[n], 2) if n in SHARD_AXIS else jnp.broadcast_to(stacked[n][None], (N_DEV,) + stacked[n].shape))
            for n in misc_names]
    segs = [_seg(w[n].size) for n in misc_names]
    misc = jnp.concatenate([jnp.pad(r.reshape(N_DEV, -1), ((0, 0), (0, s - r[0].size))) for r, s in zip(rows, segs)],
                           axis=1).reshape(N_DEV, -1, LANES)
    g_out, per_layer = {}, {}
    for l in reversed(range(DEPTH)):
        recv = _exchange(f"grad_exchange_{l}", [grads[l][n].reshape(N_DEV, -1, grads[l][n].shape[-1]) for n in BIG] + ([misc] if l == 0 else []))
        for n, r in zip(BIG, recv):
            per_layer[n, l] = _as_rows(_sum_slots(f"sum_{n}_{l}", r), n)
    misc_sum = _sum_slots("sum_misc", recv[-1]).reshape(-1)
    off = 0
    for n, s in zip(misc_names, segs):
        g_out[n] = misc_sum[off:off + w[n].size].reshape(w[n].shape)
        off += s
    for n in BIG:
        g_out[n] = jnp.stack([per_layer[n, l] for l in range(DEPTH)])

    res = [[], [], [], []]
    for n in WEIGHTS:
        shp = w[n].shape
        d, mn, vn = _adamw(f"adamw_{n}", *[a.reshape(-1, shp[-1]) for a in (g_out[n], w[n], m[n], v[n])])
        for k, a in enumerate((g_out[n], d, mn, vn)):
            res[k].append(a.reshape(shp))
    return (loss, dx[None], *res[0], *res[1], *res[2], *res[3])
```

```python
import jax
import jax.numpy as jnp
from jax import lax
from jax.experimental import pallas as pl
from jax.experimental.pallas import tpu as pltpu

F32 = jnp.float32
MXU = jnp.bfloat16
HI = lax.Precision.HIGHEST

D = 1024
DEPTH = 2
POOLW = 384
ATT_W = 768
ATT_O = 256
GATE_W = 3 * D
IN_W = 6912
MEM_W = 512
D_FF = 2816
EPS = 1e-6
ROPE_THETA = 500000.0
QB = 128
DILS = (1, 4, 16)
NEG = -1e30
MEM_SCALE = 128 ** -0.5
ATT_SCALE = 0.125

ADAM_LR, ADAM_B1, ADAM_B2, ADAM_EPS, ADAM_WD, ADAM_STEP = 0.001, 0.9, 0.999, 1e-08, 0.01, 10

N_DEV = 8
MESH_AXES = ("x", "y", "c")
LANES = 128
FLAT_ALIGN = 2048
ROW_TILE = 1024

WEIGHTS = ['norm_mix_pre', 'norm_mix_post', 'w_in', 'pool_w', 'pool_scale', 'conv_b_w', 'w_branch_a', 'w_branch_b',
           'w_branch_c', 'w_out', 'norm_mem_pre', 'norm_mem_post', 'norm_memkv', 'w_mq', 'w_mkv', 'w_mo',
           'norm_ffn_pre', 'norm_ffn_post', 'w_up', 'conv_ffn_w', 'w_down']
SHARD_AXIS = {'w_in': 2, 'conv_b_w': 2, 'w_branch_a': 2, 'w_branch_b': 2, 'w_branch_c': 2, 'w_out': 1, 'w_mq': 1,
              'w_mkv': 1, 'w_mo': 2, 'w_up': 2, 'conv_ffn_w': 2, 'w_down': 1}
F32_GATHERED = ('conv_b_w', 'conv_ffn_w')
BIG = [n for n in WEIGHTS if n in SHARD_AXIS and n not in F32_GATHERED]


def _params(sem, vmem_mb):
    return pltpu.CompilerParams(dimension_semantics=sem, vmem_limit_bytes=vmem_mb << 20)


def _dot(a, b, prec=None):
    return lax.dot_general(a, b, (((1,), (0,)), ((), ())), preferred_element_type=F32, precision=prec)


def _dot_nt(a, b, prec=None):
    return lax.dot_general(a, b, (((1,), (1,)), ((), ())), preferred_element_type=F32, precision=prec)


def _dot_tn(a, b, prec=None):
    return lax.dot_general(a, b, (((0,), (0,)), ((), ())), preferred_element_type=F32, precision=prec)


def _tile(n, cap):
    if n <= cap:
        return n
    best = None
    for t in range(LANES, cap + 1, LANES):
        if n % t == 0:
            best = t
    assert best is not None, (n, cap)
    return best


def _rms(x, g):
    r = lax.rsqrt(jnp.mean(x * x, axis=-1, keepdims=True) + EPS)
    return x * r * g, r


def _rms_bwd(w, y):
    r = lax.rsqrt(jnp.mean(y * y, axis=-1, keepdims=True) + EPS)
    return r * w - y * (r * r * r) * jnp.mean(w * y, axis=-1, keepdims=True), r


def _rows_call(name, body, n_rows, ts, ins, outs, scratch=(), reverse=False, vmem_mb=48):
    nt = n_rows // ts
    assert nt * ts == n_rows

    def tile_of(g):
        return (nt - 1 - g) if reverse else g

    in_specs, args = [], []
    for op in ins:
        if op[0] == "t":
            _, a, cw, cb = op
            in_specs.append(pl.BlockSpec((ts, cw), lambda g, cb=cb: (tile_of(g), cb)))
        elif op[0] == "h":
            _, a, hr, cw, cb = op
            in_specs.append(pl.BlockSpec((hr, cw), lambda g, cb=cb, k=ts // hr: (jnp.maximum(tile_of(g) * k - 1, 0), cb)))
        else:
            _, a = op
            in_specs.append(pl.BlockSpec(a.shape, lambda g, n=a.ndim: (0,) * n))
        args.append(a)
    out_specs, out_shape = [], []
    for op in outs:
        if op[0] == "t":
            _, cols, dt = op
            out_specs.append(pl.BlockSpec((ts, cols), lambda g: (tile_of(g), 0)))
            out_shape.append(jax.ShapeDtypeStruct((n_rows, cols), dt))
        else:
            _, shp, dt = op
            out_specs.append(pl.BlockSpec(shp, lambda g, n=len(shp): (0,) * n))
            out_shape.append(jax.ShapeDtypeStruct(shp, dt))

    def kern(*refs):
        g = pl.program_id(0)
        body(tile_of(g), g, *refs)

    return pl.pallas_call(kern, grid=(nt,), in_specs=in_specs, out_specs=out_specs, out_shape=out_shape,
                          scratch_shapes=list(scratch), compiler_params=_params(("arbitrary",), vmem_mb), name=name)(*args)


def _acc(ref, g, val):
    @pl.when(g == 0)
    def _():
        ref[...] = val

    @pl.when(g != 0)
    def _():
        ref[...] += val


def _norm_mm(name, x, g, w, ts, tn, out_dtype=F32, wt=False):
    S, K = x.shape
    N = w.shape[0] if wt else w.shape[1]

    def body(x_ref, g_ref, w_ref, o_ref, h_ref, hs):
        @pl.when(pl.program_id(1) == 0)
        def _():
            h, _ = _rms(x_ref[...], g_ref[...])
            hs[...] = h.astype(MXU)
            h_ref[...] = h.astype(MXU)

        o_ref[...] = (_dot_nt if wt else _dot)(hs[...], w_ref[...]).astype(out_dtype)

    w_spec = pl.BlockSpec((tn, K), lambda i, j: (j, 0)) if wt else pl.BlockSpec((K, tn), lambda i, j: (0, j))
    return pl.pallas_call(
        body, grid=(S // ts, N // tn),
        in_specs=[pl.BlockSpec((ts, K), lambda i, j: (i, 0)), pl.BlockSpec((1, K), lambda i, j: (0, 0)), w_spec],
        out_specs=[pl.BlockSpec((ts, tn), lambda i, j: (i, j)), pl.BlockSpec((ts, K), lambda i, j: (i, 0))],
        out_shape=[jax.ShapeDtypeStruct((S, N), out_dtype), jax.ShapeDtypeStruct((S, K), MXU)],
        scratch_shapes=[pltpu.VMEM((ts, K), MXU)],
        compiler_params=_params(("arbitrary", "arbitrary"), 48), name=name)(x, g, w)


def _mm_nt(name, a, b, ts, tn, out_dtype=F32):
    M, K = a.shape
    N = b.shape[0]

    def body(a_ref, b_ref, o_ref):
        o_ref[...] = _dot_nt(a_ref[...], b_ref[...]).astype(out_dtype)

    return pl.pallas_call(
        body, grid=(M // ts, N // tn),
        in_specs=[pl.BlockSpec((ts, K), lambda i, j: (i, 0)), pl.BlockSpec((tn, K), lambda i, j: (j, 0))],
        out_specs=pl.BlockSpec((ts, tn), lambda i, j: (i, j)), out_shape=jax.ShapeDtypeStruct((M, N), out_dtype),
        compiler_params=_params(("arbitrary", "arbitrary"), 48), name=name)(a, b)


def _mm_tn(name, a, b, cap_k=512, cap_n=512, out_dtype=MXU):
    S, K = a.shape
    N = b.shape[1]
    tk, tn = _tile(K, cap_k), _tile(N, cap_n)

    def body(a_ref, b_ref, o_ref):
        o_ref[...] = _dot_tn(a_ref[...], b_ref[...]).astype(out_dtype)

    return pl.pallas_call(
        body, grid=(K // tk, N // tn),
        in_specs=[pl.BlockSpec((S, tk), lambda i, j: (0, i)), pl.BlockSpec((S, tn), lambda i, j: (0, j))],
        out_specs=pl.BlockSpec((tk, tn), lambda i, j: (i, j)), out_shape=jax.ShapeDtypeStruct((K, N), out_dtype),
        compiler_params=_params(("arbitrary", "arbitrary"), 48), name=name)(a, b)


def _pool_cols(shape):
    col = lax.broadcasted_iota(jnp.int32, shape, 1)
    return col < 96, col < 192, col < 288


def _pool_select(s2, s4, s8, s16):
    c1, c2, c3 = _pool_cols(s2.shape)
    return jnp.where(c1, s2, jnp.where(c2, s4, jnp.where(c3, s8, s16)))


def _pool_cnt(t0, ts):
    c1, c2, c3 = _pool_cols((ts, POOLW))
    win = jnp.where(c1, 2, jnp.where(c2, 4, jnp.where(c3, 8, 16)))
    t = t0 + lax.broadcasted_iota(jnp.int32, (ts, POOLW), 0)
    return jnp.minimum(t + 1, win).astype(F32)


def _pooled(a, prev, t0):
    ts = a.shape[0]
    ext = jnp.concatenate([prev, a], axis=0)
    s2 = ext + pltpu.roll(ext, 1, axis=0)
    s4 = s2 + pltpu.roll(s2, 2, axis=0)
    s8 = s4 + pltpu.roll(s4, 4, axis=0)
    s16 = s8 + pltpu.roll(s8, 8, axis=0)
    sums = _pool_select(s2, s4, s8, s16)[16:]
    return sums / _pool_cnt(t0, ts) - a


def _conv3(z, prev8, w):
    ext = jnp.concatenate([prev8, z], axis=0)
    z1 = pltpu.roll(ext, 1, axis=0)[8:]
    z2 = pltpu.roll(ext, 2, axis=0)[8:]
    return w[0:1] * z2 + w[1:2] * z1 + w[2:3] * z, z1, z2


def _conv3_t(dc, next8, w):
    ts = dc.shape[0]
    ext = jnp.concatenate([dc, next8], axis=0)
    n = ts + 8
    u1 = pltpu.roll(ext, n - 1, axis=0)[:ts]
    u2 = pltpu.roll(ext, n - 2, axis=0)[:ts]
    return w[2:3] * dc + w[1:2] * u1 + w[0:1] * u2


def _poolconv_fwd(u, wblk, pool_scale, conv_b, ts=256):
    S = u.shape[0]

    def body(i, g, a_ref, bx_ref, bb_ref, bc_ref, wblk_ref, ps_ref, cw_ref, a2_ref, yb_ref, ca, cz):
        @pl.when(g == 0)
        def _():
            ca[...] = jnp.zeros_like(ca)
            cz[...] = jnp.zeros_like(cz)

        a = a_ref[...]
        p = _pooled(a, ca[...], i * ts)
        mixed = _dot(p.astype(MXU), wblk_ref[...])
        a2_ref[...] = (mixed * ps_ref[...]).astype(MXU)
        z = bc_ref[...] * bx_ref[...]
        conv, _, _ = _conv3(z, cz[...], cw_ref[...])
        yb_ref[...] = (bb_ref[...] * conv).astype(MXU)
        ca[...] = a[ts - 16:]
        cz[...] = z[ts - 8:]

    ins = [("t", u, POOLW, 8), ("t", u, POOLW, 9), ("t", u, POOLW, 10), ("t", u, POOLW, 11), ("w", wblk), ("w", pool_scale),
           ("w", conv_b)]
    return _rows_call("poolconv_fwd", body, S, ts, ins, [("t", POOLW, MXU), ("t", POOLW, MXU)],
                      scratch=[pltpu.VMEM((16, POOLW), F32), pltpu.VMEM((8, POOLW), F32)])


def _poolconv_bwd(u, d_a2, d_yb, wblk, pool_scale, conv_b, ts=256):
    S = u.shape[0]

    def body(i, g, a_ref, bx_ref, bb_ref, bc_ref, ap_ref, bxp_ref, bcp_ref, da2_ref, dyb_ref, wblk_ref, ps_ref, cw_ref,
             o_ref, dps_ref, dwb_ref, dcw_ref, ce, cdz):
        @pl.when(g == 0)
        def _():
            ce[...] = jnp.zeros_like(ce)
            cdz[...] = jnp.zeros_like(cdz)

        first = (i > 0).astype(F32)
        a = a_ref[...]
        p = _pooled(a, ap_ref[...] * first, i * ts)
        pb = p.astype(MXU)
        mixed = _dot(pb, wblk_ref[...])
        da2 = da2_ref[...]
        dmixed = (da2 * ps_ref[...]).astype(MXU)
        dp = _dot_nt(dmixed, wblk_ref[...])
        _acc(dps_ref, g, jnp.sum(da2 * mixed, axis=0, keepdims=True))
        _acc(dwb_ref, g, _dot_tn(pb, dmixed))
        e = dp / _pool_cnt(i * ts, ts)
        ext = jnp.concatenate([e, ce[...]], axis=0)
        n = ts + 16
        f2 = ext + pltpu.roll(ext, n - 1, axis=0)
        f4 = f2 + pltpu.roll(f2, n - 2, axis=0)
        f8 = f4 + pltpu.roll(f4, n - 4, axis=0)
        f16 = f8 + pltpu.roll(f8, n - 8, axis=0)
        o_ref[:, 0:POOLW] = (_pool_select(f2, f4, f8, f16)[:ts] - dp).astype(o_ref.dtype)
        ce[...] = e[:16]

        bx, bb, bc = bx_ref[...], bb_ref[...], bc_ref[...]
        z = bc * bx
        w = cw_ref[...]
        conv, z1, z2 = _conv3(z, bxp_ref[...] * bcp_ref[...] * first, w)
        dyb = dyb_ref[...]
        dconv = dyb * bb
        dz = _conv3_t(dconv, cdz[...], w)
        o_ref[:, POOLW:2 * POOLW] = (dz * bc).astype(o_ref.dtype)
        o_ref[:, 2 * POOLW:3 * POOLW] = (dyb * conv).astype(o_ref.dtype)
        o_ref[:, 3 * POOLW:4 * POOLW] = (dz * bx).astype(o_ref.dtype)
        dw = jnp.concatenate([jnp.sum(dconv * z2, axis=0, keepdims=True), jnp.sum(dconv * z1, axis=0, keepdims=True),
                              jnp.sum(dconv * z, axis=0, keepdims=True)], axis=0)
        _acc(dcw_ref, g, dw)
        cdz[...] = dconv[:8]

    ins = [("t", u, POOLW, 8), ("t", u, POOLW, 9), ("t", u, POOLW, 10), ("t", u, POOLW, 11),
           ("h", u, 16, POOLW, 8), ("h", u, 8, POOLW, 9), ("h", u, 8, POOLW, 11),
           ("t", d_a2, POOLW, 0), ("t", d_yb, POOLW, 0), ("w", wblk), ("w", pool_scale), ("w", conv_b)]
    outs = [("t", 4 * POOLW, MXU), ("a", (1, POOLW), F32), ("a", (POOLW, POOLW), F32), ("a", (3, POOLW), F32)]
    return _rows_call("poolconv_bwd", body, S, ts, ins, outs,
                      scratch=[pltpu.VMEM((16, POOLW), F32), pltpu.VMEM((8, POOLW), F32)], reverse=True)


def _rope_tables(positions):
    S = positions.shape[0]
    inv = ROPE_THETA ** (-jnp.arange(0, 16, 2, dtype=F32) / 16)
    ang = positions.astype(F32)[:, None] * inv
    cos, sin = jnp.cos(ang), jnp.sin(ang)
    c64 = jnp.concatenate([cos, cos, jnp.ones((S, 48), F32)], axis=1)
    s64 = jnp.concatenate([-sin, sin, jnp.zeros((S, 48), F32)], axis=1)
    return jnp.concatenate([c64, c64], axis=1), jnp.concatenate([s64, s64], axis=1)


def _partner(x):
    lane = lax.broadcasted_iota(jnp.int32, x.shape, 1) % 64
    return jnp.where(lane < 8, pltpu.roll(x, LANES - 8, axis=1), jnp.where(lane < 16, pltpu.roll(x, 8, axis=1), 0.0))


def _rope(x, c, s):
    return x * c + _partner(x) * s


def _rope_t(x, c, s):
    return x * c + _partner(x * s)


def _rows_of(r, n, d):
    return pl.ds(r, n, stride=d) if d > 1 else pl.ds(0, n)


def _head_masks(shape):
    lane = lax.broadcasted_iota(jnp.int32, shape, 1) // 64
    return [lane == h for h in range(4)]


def _only(mask, x):
    return jnp.where(mask, x, jnp.zeros_like(x))


def _rope_perm(u, ctab, stab, ts=256):
    S = u.shape[0]
    nch = ATT_W // LANES

    def body(*refs):
        chunks, (c_ref, s_ref), outs = refs[:3 * nch], refs[3 * nch:3 * nch + 2], refs[3 * nch + 2:]
        for g, d in enumerate(DILS):
            n = ts // d
            for r in range(d):
                rows = _rows_of(r, n, d)
                c, s = c_ref[rows, :], s_ref[rows, :]
                for which in range(3):
                    parts = [chunks[which * nch + j][rows, :] for j in (2 * g, 2 * g + 1)]
                    if which < 2:
                        parts = [_rope(x, c, s) for x in parts]
                    outs[which * 3 + g][r] = jnp.concatenate(parts, axis=1).astype(MXU)

    base = (IN_W - 3 * ATT_W) // LANES
    in_specs = [pl.BlockSpec((ts, LANES), lambda i, cb=base + k: (i, cb)) for k in range(3 * nch)]
    in_specs += [pl.BlockSpec((ts, LANES), lambda i: (i, 0))] * 2
    out_specs = [pl.BlockSpec((d, ts // d, ATT_O), lambda i: (0, i, 0)) for _ in range(3) for d in DILS]
    out_shape = [jax.ShapeDtypeStruct((d, S // d, ATT_O), MXU) for _ in range(3) for d in DILS]
    res = pl.pallas_call(body, grid=(S // ts,), in_specs=in_specs, out_specs=out_specs, out_shape=out_shape,
                         compiler_params=_params(("arbitrary",), 32), name="rope_perm")(*([u] * (3 * nch)), ctab, stab)
    return [[res[which * 3 + g].reshape(S, ATT_O) for g in range(3)] for which in range(3)]


def _rope_unperm_bwd(dqkv, ctab, stab, ts=256):
    S = dqkv[0][0].shape[0]
    nch = ATT_W // LANES

    def body(*refs):
        ins, (c_ref, s_ref, o_ref, scr) = refs[:9], refs[9:]
        for g, d in enumerate(DILS):
            n = ts // d
            for r in range(d):
                rows = _rows_of(r, n, d)
                c, s = c_ref[rows, :], s_ref[rows, :]
                for which in range(3):
                    v = ins[which * 3 + g][r]
                    for half in range(2):
                        x = v[:, half * LANES:(half + 1) * LANES]
                        scr.at[which * nch + 2 * g + half][rows, :] = _rope_t(x, c, s) if which < 2 else x
        for j in range(3 * nch):
            o_ref[:, j * LANES:(j + 1) * LANES] = scr[j].astype(o_ref.dtype)

    in_specs = [pl.BlockSpec((d, ts // d, ATT_O), lambda i: (0, i, 0)) for _ in range(3) for d in DILS]
    in_specs += [pl.BlockSpec((ts, LANES), lambda i: (i, 0))] * 2
    args = [dqkv[which][g].reshape(d, S // d, ATT_O) for which in range(3) for g, d in enumerate(DILS)]
    return pl.pallas_call(body, grid=(S // ts,), in_specs=in_specs, out_specs=pl.BlockSpec((ts, 3 * ATT_W), lambda i: (i, 0)),
                          out_shape=jax.ShapeDtypeStruct((S, 3 * ATT_W), MXU), scratch_shapes=[pltpu.VMEM((3 * nch, ts, LANES), F32)],
                          compiler_params=_params(("arbitrary",), 32), name="rope_unperm_bwd")(*args, ctab, stab)


def _band_mask_keys(has_prev):
    r = lax.broadcasted_iota(jnp.int32, (QB, 2 * QB), 0)
    c = lax.broadcasted_iota(jnp.int32, (QB, 2 * QB), 1)
    return ((c < QB) & (c >= r) & has_prev) | ((c >= QB) & (c - QB <= r))


def _band_mask_queries(has_next):
    r = lax.broadcasted_iota(jnp.int32, (2 * QB, QB), 0)
    c = lax.broadcasted_iota(jnp.int32, (2 * QB, QB), 1)
    return ((r < QB) & (c <= r)) | ((r >= QB) & (c >= r - QB) & has_next)


def _blk(fn):
    return pl.BlockSpec((QB, ATT_O), fn)


_CUR = lambda b: (b, 0)
_PREV = lambda b: (jnp.maximum(b - 1, 0), 0)


def _attn_fwd(g, q, k, v):
    S = q.shape[0]
    nblk = S // QB // DILS[g]

    def body(q_ref, kc_ref, kp_ref, vc_ref, vp_ref, o_ref, m_ref, l_ref):
        ok = _band_mask_keys((pl.program_id(0) & (nblk - 1)) > 0)
        k2 = jnp.concatenate([kp_ref[...], kc_ref[...]], axis=0)
        v2 = jnp.concatenate([vp_ref[...], vc_ref[...]], axis=0)
        qv = q_ref[...]
        hm_kv, hm_o = _head_masks((2 * QB, ATT_O)), _head_masks((QB, ATT_O))
        o_acc = jnp.zeros((QB, ATT_O), F32)
        m_acc = jnp.zeros((QB, ATT_O), F32)
        l_acc = jnp.zeros((QB, ATT_O), F32)
        for h in range(4):
            s = jnp.where(ok, _dot_nt(qv, _only(hm_kv[h], k2)) * ATT_SCALE, NEG)
            m = jnp.max(s, axis=1, keepdims=True)
            p = jnp.exp(s - m)
            o_acc = o_acc + _dot(p.astype(MXU), _only(hm_kv[h], v2))
            m_acc = jnp.where(hm_o[h], m, m_acc)
            l_acc = jnp.where(hm_o[h], jnp.sum(p, axis=1, keepdims=True), l_acc)
        o_ref[...] = o_acc
        m_ref[...] = m_acc
        l_ref[...] = l_acc

    shp = jax.ShapeDtypeStruct((S, ATT_O), F32)
    return pl.pallas_call(body, grid=(S // QB,), in_specs=[_blk(_CUR), _blk(_CUR), _blk(_PREV), _blk(_CUR), _blk(_PREV)],
                          out_specs=[_blk(_CUR)] * 3, out_shape=[shp, shp, shp], compiler_params=_params(("arbitrary",), 32),
                          name=f"attn_fwd_{g}")(q, k, k, v, v)


def _natural(ref, d, scr, ts):
    if d == 1:
        return ref[0]
    n = ts // d
    for r in range(d):
        v = ref[r]
        scr.at[0][pl.ds(r, n, stride=d), :] = v[:, 0:LANES]
        scr.at[1][pl.ds(r, n, stride=d), :] = v[:, LANES:2 * LANES]
    return jnp.concatenate([scr[0], scr[1]], axis=1)


def _attn_combine(oml, ts=256):
    S = oml[0][0].shape[0]

    def body(*refs):
        ins, (att_ref, out_ref, lse_ref, scr) = refs[:9], refs[9:]
        o, m, l = [[_natural(ins[3 * g + k], d, scr, ts) for g, d in enumerate(DILS)] for k in range(3)]
        mx = jnp.maximum(jnp.maximum(m[0], m[1]), m[2])
        w = [jnp.exp(m[g] - mx) for g in range(3)]
        den = w[0] * l[0] + w[1] * l[1] + w[2] * l[2]
        out = (w[0] * o[0] + w[1] * o[1] + w[2] * o[2]) / den
        out_ref[...] = out
        att_ref[...] = out.astype(MXU)
        lse_ref[...] = mx + jnp.log(den)

    in_specs = [pl.BlockSpec((d, ts // d, ATT_O), lambda i: (0, i, 0)) for d in DILS for _ in range(3)]
    args = [a.reshape(d, S // d, ATT_O) for d, grp in zip(DILS, oml) for a in grp]
    blk = pl.BlockSpec((ts, ATT_O), lambda i: (i, 0))
    return pl.pallas_call(body, grid=(S // ts,), in_specs=in_specs, out_specs=[blk, blk, blk],
                          out_shape=[jax.ShapeDtypeStruct((S, ATT_O), MXU), jax.ShapeDtypeStruct((S, ATT_O), F32),
                                     jax.ShapeDtypeStruct((S, ATT_O), F32)],
                          scratch_shapes=[pltpu.VMEM((2, ts, LANES), F32)], compiler_params=_params(("arbitrary",), 32),
                          name="attn_combine")(*args)


def _attn_bwd_prep(datt, o, lse, ts=256):
    S = datt.shape[0]

    def body(da0, da1, o_ref, l0, l1, *rest):
        outs, dl = rest[:9], rest[9]
        prod = jnp.concatenate([da0[...], da1[...]], axis=1) * o_ref[...]
        delta = jnp.zeros((ts, ATT_O), F32)
        for hm in _head_masks((ts, ATT_O)):
            delta = jnp.where(hm, jnp.sum(_only(hm, prod), axis=1, keepdims=True), delta)
        dl[0] = delta[:, 0:LANES]
        dl[1] = delta[:, LANES:2 * LANES]
        for g, d in enumerate(DILS):
            n = ts // d
            for r in range(d):
                rows = _rows_of(r, n, d)
                outs[g][r] = jnp.concatenate([da0[rows, :], da1[rows, :]], axis=1).astype(MXU)
                outs[3 + g][r] = jnp.concatenate([dl.at[0][rows, :], dl.at[1][rows, :]], axis=1)
                outs[6 + g][r] = jnp.concatenate([l0[rows, :], l1[rows, :]], axis=1)

    half = lambda j: pl.BlockSpec((ts, LANES), lambda i: (i, j))
    out_specs = [pl.BlockSpec((d, ts // d, ATT_O), lambda i: (0, i, 0)) for _ in range(3) for d in DILS]
    out_shape = [jax.ShapeDtypeStruct((d, S // d, ATT_O), dt) for dt in (MXU, F32, F32) for d in DILS]
    res = pl.pallas_call(body, grid=(S // ts,), in_specs=[half(0), half(1), pl.BlockSpec((ts, ATT_O), lambda i: (i, 0)), half(0), half(1)],
                         out_specs=out_specs, out_shape=out_shape, scratch_shapes=[pltpu.VMEM((2, ts, LANES), F32)],
                         compiler_params=_params(("arbitrary",), 32), name="attn_bwd_prep")(datt, datt, o, lse, lse)
    return [[res[k * 3 + g].reshape(S, ATT_O) for g in range(3)] for k in range(3)]


def _head_col(x, h):
    return x[:, h * 64:h * 64 + 1]


def _attn_dq(g, q, k, v, do, delta, lse):
    S = q.shape[0]
    nblk = S // QB // DILS[g]

    def body(q_ref, kc_ref, kp_ref, vc_ref, vp_ref, do_ref, dl_ref, lse_ref, dq_ref):
        ok = _band_mask_keys((pl.program_id(0) & (nblk - 1)) > 0)
        k2 = jnp.concatenate([kp_ref[...], kc_ref[...]], axis=0)
        v2 = jnp.concatenate([vp_ref[...], vc_ref[...]], axis=0)
        qv, dov, dl, lse_v = q_ref[...], do_ref[...], dl_ref[...], lse_ref[...]
        dq = jnp.zeros((QB, ATT_O), F32)
        for h, hm in enumerate(_head_masks((2 * QB, ATT_O))):
            kh = _only(hm, k2)
            p = jnp.where(ok, jnp.exp(_dot_nt(qv, kh) * ATT_SCALE - _head_col(lse_v, h)), 0.0)
            ds = p * (_dot_nt(dov, _only(hm, v2)) - _head_col(dl, h))
            dq = dq + _dot(ds.astype(MXU), kh)
        dq_ref[...] = dq * ATT_SCALE

    specs = [_blk(_CUR), _blk(_CUR), _blk(_PREV), _blk(_CUR), _blk(_PREV), _blk(_CUR), _blk(_CUR), _blk(_CUR)]
    return pl.pallas_call(body, grid=(S // QB,), in_specs=specs, out_specs=_blk(_CUR), out_shape=jax.ShapeDtypeStruct((S, ATT_O), F32),
                          compiler_params=_params(("arbitrary",), 32), name=f"attn_dq_{g}")(q, k, k, v, v, do, delta, lse)


def _attn_dkv(g, q, k, v, do, delta, lse):
    S = q.shape[0]
    nb = S // QB
    nblk = nb // DILS[g]

    def body(k_ref, v_ref, qc_ref, qn_ref, doc_ref, don_ref, dlc_ref, dln_ref, lc_ref, ln_ref, dk_ref, dv_ref):
        ok = _band_mask_queries(((pl.program_id(0) + 1) & (nblk - 1)) > 0)
        q2 = jnp.concatenate([qc_ref[...], qn_ref[...]], axis=0)
        do2 = jnp.concatenate([doc_ref[...], don_ref[...]], axis=0)
        dl2 = jnp.concatenate([dlc_ref[...], dln_ref[...]], axis=0)
        lse2 = jnp.concatenate([lc_ref[...], ln_ref[...]], axis=0)
        kv, vv = k_ref[...], v_ref[...]
        dk = jnp.zeros((QB, ATT_O), F32)
        dv = jnp.zeros((QB, ATT_O), F32)
        for h, hm in enumerate(_head_masks((2 * QB, ATT_O))):
            qh, doh = _only(hm, q2), _only(hm, do2)
            p = jnp.where(ok, jnp.exp(_dot_nt(qh, kv) * ATT_SCALE - _head_col(lse2, h)), 0.0)
            ds = p * (_dot_nt(doh, vv) - _head_col(dl2, h))
            dv = dv + _dot_tn(p.astype(MXU), doh)
            dk = dk + _dot_tn(ds.astype(MXU), qh)
        dk_ref[...] = dk * ATT_SCALE
        dv_ref[...] = dv

    nxt = _blk(lambda b: (jnp.minimum(b + 1, nb - 1), 0))
    cur = _blk(_CUR)
    shp = jax.ShapeDtypeStruct((S, ATT_O), F32)
    return pl.pallas_call(body, grid=(nb,), in_specs=[cur, cur, cur, nxt, cur, nxt, cur, nxt, cur, nxt], out_specs=[cur, cur],
                          out_shape=[shp, shp], compiler_params=_params(("arbitrary",), 32),
                          name=f"attn_dkv_{g}")(k, v, q, q, do, do, delta, delta, lse, lse)


def _merge_fwd(x0, u, a2, yb, att, wa, wb, wc, w_out, g_post, ts=256):
    S = x0.shape[0]

    def body(i, g, x_ref, gate_ref, a2_ref, yb_ref, att_ref, wa_ref, wb_ref, wc_ref, wo_ref, gp_ref, mg_ref, y_ref, xo_ref):
        merged = jax.nn.sigmoid(gate_ref[:, 0:D]) * _dot_nt(a2_ref[...], wa_ref[...])
        merged = merged + jax.nn.sigmoid(gate_ref[:, D:2 * D]) * _dot_nt(yb_ref[...], wb_ref[...])
        merged = merged + jax.nn.sigmoid(gate_ref[:, 2 * D:3 * D]) * _dot_nt(att_ref[...], wc_ref[...])
        mb = merged.astype(MXU)
        mg_ref[...] = mb
        y = _dot(mb, wo_ref[...])
        y_ref[...] = y
        xo_ref[...] = x_ref[...] + _rms(y, gp_ref[...])[0]

    ins = [("t", x0, D, 0), ("t", u, GATE_W, 0), ("t", a2, POOLW, 0), ("t", yb, POOLW, 0), ("t", att, ATT_O, 0),
           ("w", wa), ("w", wb), ("w", wc), ("w", w_out), ("w", g_post)]
    return _rows_call("merge_fwd", body, S, ts, ins, [("t", D, MXU), ("t", D, F32), ("t", D, F32)])


def _merge_bwd(dx, y1, u, a2, yb, att, wa, wb, wc, w_out, g_post, ts=256):
    S = dx.shape[0]

    def body(i, g, dx_ref, y_ref, gate_ref, a2_ref, yb_ref, att_ref, wa_ref, wb_ref, wc_ref, wo_ref, gp_ref,
             dy_ref, dgate_ref, dbra_ref, dbrb_ref, dbrc_ref, da2_ref, dyb_ref, datt_ref, dgp_ref):
        dxv, y = dx_ref[...], y_ref[...]
        dy, r = _rms_bwd(dxv * gp_ref[...], y)
        _acc(dgp_ref, g, jnp.sum(dxv * (y * r), axis=0, keepdims=True))
        dyb16 = dy.astype(MXU)
        dy_ref[...] = dyb16
        dm = _dot_nt(dyb16, wo_ref[...])
        for n, (src, w_ref, dbr_ref, din_ref) in enumerate(((a2_ref, wa_ref, dbra_ref, da2_ref), (yb_ref, wb_ref, dbrb_ref, dyb_ref),
                                                           (att_ref, wc_ref, dbrc_ref, datt_ref))):
            gt = jax.nn.sigmoid(gate_ref[:, n * D:(n + 1) * D])
            br = _dot_nt(src[...], w_ref[...])
            dgate_ref[:, n * D:(n + 1) * D] = (dm * br * gt * (1.0 - gt)).astype(dgate_ref.dtype)
            dbr = (dm * gt).astype(MXU)
            dbr_ref[...] = dbr
            din_ref[...] = _dot(dbr, w_ref[...])

    ins = [("t", dx, D, 0), ("t", y1, D, 0), ("t", u, GATE_W, 0), ("t", a2, POOLW, 0), ("t", yb, POOLW, 0), ("t", att, ATT_O, 0),
           ("w", wa), ("w", wb), ("w", wc), ("w", w_out), ("w", g_post)]
    outs = [("t", D, MXU), ("t", GATE_W, MXU), ("t", D, MXU), ("t", D, MXU), ("t", D, MXU), ("t", POOLW, F32), ("t", POOLW, F32),
            ("t", ATT_O, F32), ("a", (1, D), F32)]
    return _rows_call("merge_bwd", body, S, ts, ins, outs)


def _prenorm_bwd(name, dx_res, du, wt, x, g_pre, ts=256):
    S = x.shape[0]
    N = du.shape[1]

    def body(i, g, dx_ref, du_ref, wt_ref, x_ref, g_ref, o_ref, dg_ref):
        dhv, xv = _dot(du_ref[...], wt_ref[...]), x_ref[...]
        dxn, r = _rms_bwd(dhv * g_ref[...], xv)
        o_ref[...] = dx_ref[...] + dxn
        _acc(dg_ref, g, jnp.sum(dhv * (xv * r), axis=0, keepdims=True))

    ins = [("t", dx_res, D, 0), ("t", du, N, 0), ("w", wt), ("t", x, D, 0), ("w", g_pre)]
    return _rows_call(name, body, S, ts, ins, [("t", D, F32), ("a", (1, D), F32)], vmem_mb=52)


def _mem_heads(qm, kv_ref):
    out = []
    for h in range(4):
        q = qm[:, h * 128:(h + 1) * 128].astype(MXU)
        k = kv_ref[:, h * 128:(h + 1) * 128]
        v = kv_ref[:, MEM_W + h * 128:MEM_W + (h + 1) * 128]
        sc = _dot_nt(q, k) * MEM_SCALE
        e = jnp.exp(sc - jnp.max(sc, axis=1, keepdims=True))
        out.append((e / jnp.sum(e, axis=1, keepdims=True), q, k, v))
    return out


def _mem_fwd(x1, kv, g_pre, w_mq, w_mo, g_post, ts=256):
    S = x1.shape[0]

    def body(i, g, x_ref, kv_ref, gq_ref, wq_ref, wo_ref, gp_ref, om_ref, h_ref, y_ref, xo_ref):
        x = x_ref[...]
        hb = _rms(x, gq_ref[...])[0].astype(MXU)
        h_ref[...] = hb
        qm = _dot(hb, wq_ref[...])
        om = jnp.concatenate([_dot(p.astype(MXU), v) for p, _, _, v in _mem_heads(qm, kv_ref)], axis=1).astype(MXU)
        om_ref[...] = om
        y = _dot_nt(om, wo_ref[...])
        y_ref[...] = y
        xo_ref[...] = x + _rms(y, gp_ref[...])[0]

    ins = [("t", x1, D, 0), ("w", kv), ("w", g_pre), ("w", w_mq), ("w", w_mo), ("w", g_post)]
    return _rows_call("mem_fwd", body, S, ts, ins, [("t", MEM_W, MXU), ("t", D, MXU), ("t", D, F32), ("t", D, F32)])


def _mem_bwd(dx2, ym, x1, kv, g_pre, w_mq, w_mo, g_post, ts=256):
    S = x1.shape[0]

    def body(i, g, dx_ref, y_ref, x_ref, kv_ref, gq_ref, wq_ref, wo_ref, gp_ref, dy_ref, dq_ref, dxo_ref, dgp_ref, dgq_ref, dkv_ref):
        dxv, y, x = dx_ref[...], y_ref[...], x_ref[...]
        dy, r = _rms_bwd(dxv * gp_ref[...], y)
        _acc(dgp_ref, g, jnp.sum(dxv * (y * r), axis=0, keepdims=True))
        dyb = dy.astype(MXU)
        dy_ref[...] = dyb
        dom = _dot(dyb, wo_ref[...])
        h, r1 = _rms(x, gq_ref[...])
        qm = _dot(h.astype(MXU), wq_ref[...])
        dqs = []

        @pl.when(g == 0)
        def _():
            dkv_ref[...] = jnp.zeros_like(dkv_ref)

        for hh, (p, q, k, v) in enumerate(_mem_heads(qm, kv_ref)):
            doh = dom[:, hh * 128:(hh + 1) * 128].astype(MXU)
            dp = _dot_nt(doh, v)
            dsc = (p * (dp - jnp.sum(dp * p, axis=1, keepdims=True)) * MEM_SCALE).astype(MXU)
            dqs.append(_dot(dsc, k))
            dkv_ref[:, hh * 128:(hh + 1) * 128] += _dot_tn(dsc, q)
            dkv_ref[:, MEM_W + hh * 128:MEM_W + (hh + 1) * 128] += _dot_tn(p.astype(MXU), doh)
        dq = jnp.concatenate(dqs, axis=1).astype(MXU)
        dq_ref[...] = dq
        dh = _dot_nt(dq, wq_ref[...])
        _acc(dgq_ref, g, jnp.sum(dh * (x * r1), axis=0, keepdims=True))
        dxo_ref[...] = dxv + _rms_bwd(dh * gq_ref[...], x)[0]

    ins = [("t", dx2, D, 0), ("t", ym, D, 0), ("t", x1, D, 0), ("w", kv), ("w", g_pre), ("w", w_mq), ("w", w_mo), ("w", g_post)]
    outs = [("t", D, MXU), ("t", MEM_W, MXU), ("t", D, F32), ("a", (1, D), F32), ("a", (1, D), F32), ("a", (256, D), F32)]
    return _rows_call("mem_bwd", body, S, ts, ins, outs)


def _gain_grad(name, dn, x):
    n = x.shape[0]

    def body(i, g, dn_ref, x_ref, o_ref):
        xv = x_ref[...]
        r = lax.rsqrt(jnp.mean(xv * xv, axis=-1, keepdims=True) + EPS)
        o_ref[...] = jnp.sum(dn_ref[...] * (xv * r), axis=0, keepdims=True)

    return _rows_call(name, body, n, n, [("t", dn, D, 0), ("t", x, D, 0)], [("a", (1, D), F32)])[0]


def _ffn_fwd(x2, u3, conv_f, w_down, g_post, ts=256):
    S = x2.shape[0]

    def body(i, g, x_ref, ua_ref, ub_ref, cw_ref, wd_ref, gp_ref, act_ref, y_ref, xo_ref, cu):
        @pl.when(g == 0)
        def _():
            cu[...] = jnp.zeros_like(cu)

        ua = ua_ref[...]
        c, _, _ = _conv3(ua, cu[...], cw_ref[...])
        act = (c * jax.nn.sigmoid(c) * ub_ref[...]).astype(MXU)
        act_ref[...] = act
        y = _dot(act, wd_ref[...])
        y_ref[...] = y
        xo_ref[...] = x_ref[...] + _rms(y, gp_ref[...])[0]
        cu[...] = ua[ts - 8:]

    ins = [("t", x2, D, 0), ("t", u3, D_FF, 0), ("t", u3, D_FF, 1), ("w", conv_f), ("w", w_down), ("w", g_post)]
    return _rows_call("ffn_fwd", body, S, ts, ins, [("t", D_FF, MXU), ("t", D, F32), ("t", D, F32)],
                      scratch=[pltpu.VMEM((8, D_FF), F32)], vmem_mb=56)


def _ffn_bwd(dx3, y3, u3, conv_f, w_down, g_post, ts=128):
    S = dx3.shape[0]

    def body(i, g, dx_ref, y_ref, ua_ref, ub_ref, uap_ref, cw_ref, wd_ref, gp_ref, dy_ref, du_ref, dgp_ref, dcw_ref, cdc):
        @pl.when(g == 0)
        def _():
            cdc[...] = jnp.zeros_like(cdc)

        dxv, y = dx_ref[...], y_ref[...]
        dy, r = _rms_bwd(dxv * gp_ref[...], y)
        _acc(dgp_ref, g, jnp.sum(dxv * (y * r), axis=0, keepdims=True))
        dyb = dy.astype(MXU)
        dy_ref[...] = dyb
        dact = _dot_nt(dyb, wd_ref[...])
        ua, w = ua_ref[...], cw_ref[...]
        c, u1, u2 = _conv3(ua, uap_ref[...] * (i > 0).astype(F32), w)
        sg = jax.nn.sigmoid(c)
        du_ref[:, D_FF:2 * D_FF] = (dact * (c * sg)).astype(du_ref.dtype)
        dc = dact * ub_ref[...] * (sg * (1.0 + c * (1.0 - sg)))
        du_ref[:, 0:D_FF] = _conv3_t(dc, cdc[...], w).astype(du_ref.dtype)
        dw = jnp.concatenate([jnp.sum(dc * u2, axis=0, keepdims=True), jnp.sum(dc * u1, axis=0, keepdims=True),
                              jnp.sum(dc * ua, axis=0, keepdims=True)], axis=0)
        _acc(dcw_ref, g, dw)
        cdc[...] = dc[:8]

    ins = [("t", dx3, D, 0), ("t", y3, D, 0), ("t", u3, D_FF, 0), ("t", u3, D_FF, 1), ("h", u3, 8, D_FF, 0), ("w", conv_f),
           ("w", w_down), ("w", g_post)]
    outs = [("t", D, MXU), ("t", 2 * D_FF, MXU), ("a", (1, D), F32), ("a", (3, D_FF), F32)]
    return _rows_call("ffn_bwd", body, S, ts, ins, outs, scratch=[pltpu.VMEM((8, D_FF), F32)], reverse=True, vmem_mb=56)


def _loss_head(x, target, ts=512):
    S = x.shape[0]

    def body(i, g, x_ref, t_ref, dx_ref, acc_ref):
        diff = x_ref[...] - t_ref[...]
        dx_ref[...] = diff * (1.0 / D)
        col = jnp.sum(diff * diff, axis=0, keepdims=True)
        part = col[:, 0:LANES]
        for j in range(1, D // LANES):
            part = part + col[:, j * LANES:(j + 1) * LANES]
        row = lax.broadcasted_iota(jnp.int32, (8, LANES), 0)
        _acc(acc_ref, g, jnp.where(row == 0, jnp.broadcast_to(part, (8, LANES)), 0.0))

    return _rows_call("loss_head", body, S, ts, [("t", x, D, 0), ("t", target, D, 0)], [("t", D, F32), ("a", (8, LANES), F32)])


def _layer_weights(big, small, l):
    w_in = big['w_in']
    pool_w = small['pool_w'][l].astype(MXU)
    wblk = jnp.zeros((POOLW, POOLW), MXU)
    for g in range(4):
        wblk = lax.dynamic_update_slice(wblk, pool_w[g], (g * 96, g * 96))
    vec = lambda n: small[n][l].reshape(1, -1)
    return dict(
        w_in=jnp.concatenate([w_in[IN_W - GATE_W:], w_in[:IN_W - GATE_W]], axis=0),
        wblk=wblk, pool_scale=vec('pool_scale'), conv_b=small['conv_b_w'][l], wa=big['w_branch_a'], wb=big['w_branch_b'],
        wc=big['w_branch_c'], w_out=big['w_out'], w_mq=big['w_mq'], w_mkv=big['w_mkv'], w_mo=big['w_mo'],
        w_up=big['w_up'], conv_f=small['conv_ffn_w'][l], w_down=big['w_down'],
        g_mix_pre=vec('norm_mix_pre'), g_mix_post=vec('norm_mix_post'), g_mem_pre=vec('norm_mem_pre'),
        g_mem_post=vec('norm_mem_post'), g_memkv=vec('norm_memkv'), g_ffn_pre=vec('norm_ffn_pre'), g_ffn_post=vec('norm_ffn_post'))


def _layer_fwd(x0, mem, W, ctab, stab):
    sv = dict(x0=x0)
    sv['u'], sv['h1'] = _norm_mm("in_proj", x0, W['g_mix_pre'], W['w_in'], ts=1024, tn=768, wt=True)
    sv['a2'], sv['yb'] = _poolconv_fwd(sv['u'], W['wblk'], W['pool_scale'], W['conv_b'])
    sv['qkv'] = q3, k3, v3 = _rope_perm(sv['u'], ctab, stab)
    sv['att'], sv['o'], sv['lse'] = _attn_combine([_attn_fwd(g, q3[g], k3[g], v3[g]) for g in range(3)])
    sv['merged'], sv['y1'], sv['x1'] = _merge_fwd(x0, sv['u'], sv['a2'], sv['yb'], sv['att'], W['wa'], W['wb'], W['wc'],
                                                  W['w_out'], W['g_mix_post'])
    sv['kv'], sv['memn'] = _norm_mm("mem_kv", mem, W['g_memkv'], W['w_mkv'], ts=256, tn=D, out_dtype=MXU)
    sv['om'], sv['h2'], sv['ym'], sv['x2'] = _mem_fwd(sv['x1'], sv['kv'], W['g_mem_pre'], W['w_mq'], W['w_mo'], W['g_mem_post'])
    sv['u3'], sv['h3'] = _norm_mm("up_proj", sv['x2'], W['g_ffn_pre'], W['w_up'], ts=1024, tn=512, wt=True)
    sv['act'], sv['y3'], x3 = _ffn_fwd(sv['x2'], sv['u3'], W['conv_f'], W['w_down'], W['g_ffn_post'])
    return x3, sv


def _layer_bwd(dx3, mem, W, sv, ctab, stab):
    g = {}
    dy3, du3, g['norm_ffn_post'], g['conv_ffn_w'] = _ffn_bwd(dx3, sv['y3'], sv['u3'], W['conv_f'], W['w_down'], W['g_ffn_post'])
    g['w_down'] = _mm_tn("dw_down", sv['act'], dy3, cap_k=256)
    g['w_up'] = _mm_tn("dw_up", du3, sv['h3'])
    dx2, g['norm_ffn_pre'] = _prenorm_bwd("ffn_pre_bwd", dx3, du3, W['w_up'], sv['x2'], W['g_ffn_pre'])
    dym, dqm, dx1, g['norm_mem_post'], g['norm_mem_pre'], dkv = _mem_bwd(dx2, sv['ym'], sv['x1'], sv['kv'], W['g_mem_pre'],
                                                                       W['w_mq'], W['w_mo'], W['g_mem_post'])
    g['w_mo'] = _mm_tn("dw_mo", dym, sv['om'])
    g['w_mq'] = _mm_tn("dw_mq", sv['h2'], dqm)
    dkvb = dkv.astype(MXU)
    g['w_mkv'] = _mm_tn("dw_mkv", sv['memn'], dkvb)
    g['norm_memkv'] = _gain_grad("memkv_gain", _mm_nt("d_memn", dkvb, W['w_mkv'], ts=256, tn=512), mem)
    dy1, dgate, dbra, dbrb, dbrc, da2, dyb, datt, g['norm_mix_post'] = _merge_bwd(
        dx1, sv['y1'], sv['u'], sv['a2'], sv['yb'], sv['att'], W['wa'], W['wb'], W['wc'], W['w_out'], W['g_mix_post'])
    g['w_out'] = _mm_tn("dw_out", sv['merged'], dy1)
    g['w_branch_a'] = _mm_tn("dw_a", dbra, sv['a2'])
    g['w_branch_b'] = _mm_tn("dw_b", dbrb, sv['yb'])
    g['w_branch_c'] = _mm_tn("dw_c", dbrc, sv['att'])
    dabc, g['pool_scale'], dwblk, g['conv_b_w'] = _poolconv_bwd(sv['u'], da2, dyb, W['wblk'], W['pool_scale'], W['conv_b'])
    g['pool_w'] = jnp.stack([dwblk[k * 96:(k + 1) * 96, k * 96:(k + 1) * 96] for k in range(4)])
    q3, k3, v3 = sv['qkv']
    do3, dl3, lse3 = _attn_bwd_prep(datt, sv['o'], sv['lse'])
    dq3 = [_attn_dq(i, q3[i], k3[i], v3[i], do3[i], dl3[i], lse3[i]) for i in range(3)]
    dkv3 = [_attn_dkv(i, q3[i], k3[i], v3[i], do3[i], dl3[i], lse3[i]) for i in range(3)]
    dqkv = _rope_unperm_bwd([dq3, [a for a, _ in dkv3], [b for _, b in dkv3]], ctab, stab)
    du = jnp.concatenate([dgate, dabc, dqkv], axis=1)
    dw_in = _mm_tn("dw_in", du, sv['h1'], cap_k=768)
    g['w_in'] = jnp.concatenate([dw_in[GATE_W:], dw_in[:GATE_W]], axis=0)
    dx0, g['norm_mix_pre'] = _prenorm_bwd("mix_pre_bwd", dx1, du, W['w_in'], sv['x0'], W['g_mix_pre'])
    return dx0, g


def _local_step(x, mem, positions, target, big, small):
    ctab, stab = _rope_tables(positions)
    Ws = [_layer_weights(big[l], small, l) for l in range(DEPTH)]
    saved = []
    for l in range(DEPTH):
        x, sv = _layer_fwd(x, mem, Ws[l], ctab, stab)
        saved.append(sv)
    dx, acc = _loss_head(x, target)
    loss = jnp.sum(acc) * (0.5 / D)
    grads = [None] * DEPTH
    for l in reversed(range(DEPTH)):
        dx, grads[l] = _layer_bwd(dx, mem, Ws[l], saved[l], ctab, stab)
    return loss, dx, grads


_HBM = pl.BlockSpec(memory_space=pl.ANY)
MESH_ID = pl.DeviceIdType.MESH


def _all_gather(name, xs):
    n = len(xs)

    def body(*refs):
        x_refs, out_refs = refs[:n], refs[n:2 * n]
        send_sems, recv_sems, local_sems = refs[2 * n:]
        x, y, c = lax.axis_index("x"), lax.axis_index("y"), lax.axis_index("c")
        me, sibling = (x, y, c), (x, y, 1 - c)
        chips = [(1 - x, y), (x, 1 - y), (1 - x, 1 - y)]

        def slot(a, p):
            return out_refs[a].at[4 * p[0] + 2 * p[1] + p[2]]

        def copy(a, k, block, to, src=None):
            return pltpu.make_async_remote_copy(src_ref=slot(a, block) if src is None else src, dst_ref=slot(a, block),
                                                send_sem=send_sems.at[a, k], recv_sem=recv_sems.at[a, k], device_id=to,
                                                device_id_type=MESH_ID)

        started = []
        for a in range(n):
            mine = pltpu.make_async_copy(x_refs[a], slot(a, me), local_sems.at[a])
            mine.start()
            started.append(mine)
        first = []
        for a in range(n):
            first.append(copy(a, 0, me, sibling, src=x_refs[a]))
            first += [copy(a, 1 + j, me, (*chip, c), src=x_refs[a]) for j, chip in enumerate(chips)]
        for cp in first:
            cp.start()
        passed = []
        for j, chip in enumerate(chips):
            for a in range(n):
                copy(a, 1 + j, (*chip, c), me).wait_recv()
                fw = copy(a, 4 + j, (*chip, c), sibling)
                fw.start()
                passed.append(fw)
        for a in range(n):
            copy(a, 0, sibling, me).wait_recv()
            for j, chip in enumerate(chips):
                copy(a, 4 + j, (*chip, 1 - c), me).wait_recv()
        for cp in first + passed:
            cp.wait_send()
        for mine in started:
            mine.wait()

    return pl.pallas_call(
        body, out_shape=[jax.ShapeDtypeStruct((N_DEV,) + x.shape, x.dtype) for x in xs], in_specs=[_HBM] * n, out_specs=[_HBM] * n,
        scratch_shapes=[pltpu.SemaphoreType.DMA((n, 7)), pltpu.SemaphoreType.DMA((n, 7)), pltpu.SemaphoreType.DMA((n,))],
        name=name)(*xs)


def _exchange(name, gs):
    n = len(gs)

    def body(*refs):
        g_refs, out_refs = refs[:n], refs[n:2 * n]
        send_sems, recv_sems, local_sems = refs[2 * n:]
        x, y, c = lax.axis_index("x"), lax.axis_index("y"), lax.axis_index("c")
        me = 4 * x + 2 * y + c
        copies = []
        for a in range(n):
            mine = pltpu.make_async_copy(g_refs[a].at[me], out_refs[a].at[me], local_sems.at[a])
            mine.start()
            copies.append(mine)
        for r in range(1, N_DEV):
            px, py, pc = x ^ ((r >> 2) & 1), y ^ ((r >> 1) & 1), c ^ (r & 1)
            for a in range(n):
                cp = pltpu.make_async_remote_copy(src_ref=g_refs[a].at[4 * px + 2 * py + pc], dst_ref=out_refs[a].at[me],
                                                  send_sem=send_sems.at[a, r - 1], recv_sem=recv_sems.at[a, r - 1],
                                                  device_id=(px, py, pc), device_id_type=MESH_ID)
                cp.start()
                copies.append(cp)
        for cp in copies:
            cp.wait()

    return pl.pallas_call(
        body, out_shape=[jax.ShapeDtypeStruct(g.shape, g.dtype) for g in gs], in_specs=[_HBM] * n, out_specs=[_HBM] * n,
        scratch_shapes=[pltpu.SemaphoreType.DMA((n, N_DEV - 1)), pltpu.SemaphoreType.DMA((n, N_DEV - 1)), pltpu.SemaphoreType.DMA((n,))],
        name=name)(*gs)


_SEM = pl.BlockSpec(memory_space=pltpu.SEMAPHORE)
_IN_HBM = pl.BlockSpec(memory_space=pltpu.HBM)
_SIDE_EFFECT = pltpu.SideEffectType.DATAFLOW_SIDE_EFFECTING


def _push_copies(src_refs, land_refs, send_sems, recv_sems, per_peer):
    x, y, c = lax.axis_index("x"), lax.axis_index("y"), lax.axis_index("c")
    me = 4 * x + 2 * y + c
    copies = []
    for r in range(1, N_DEV):
        px, py, pc = x ^ ((r >> 2) & 1), y ^ ((r >> 1) & 1), c ^ (r & 1)
        for a, (s, d) in enumerate(zip(src_refs, land_refs)):
            k = a * (N_DEV - 1) + r - 1
            copies.append(pltpu.make_async_remote_copy(src_ref=s.at[4 * px + 2 * py + pc] if per_peer else s, dst_ref=d.at[me],
                                                       send_sem=send_sems.at[k], recv_sem=recv_sems.at[k],
                                                       device_id=(px, py, pc), device_id_type=MESH_ID))
    return copies


def _push_start(name, srcs, per_peer, after):
    n = len(srcs)
    lands = [lax.empty((N_DEV,) + (s.shape[1:] if per_peer else s.shape), s.dtype) for s in srcs]

    def body(*refs):
        for cp in _push_copies(refs[:n], refs[n:2 * n], refs[2 * n + 1], refs[2 * n + 2], per_peer):
            cp.start()
        refs[-1][...] = jnp.zeros_like(refs[-1])

    hbm = [pltpu.HBM(a.shape, a.dtype) for a in (*srcs, *lands)]
    sems = pltpu.SemaphoreType.DMA((n * (N_DEV - 1),))
    out = pl.pallas_call(
        body, name=name, out_shape=(sems, sems, *hbm, jax.ShapeDtypeStruct((8, LANES), F32)),
        in_specs=[_IN_HBM] * (2 * n) + [pl.BlockSpec(memory_space=pl.ANY)],
        out_specs=(_SEM, _SEM, *[_IN_HBM] * (2 * n), pl.BlockSpec(memory_space=pltpu.VMEM)),
        input_output_aliases={a: 2 + a for a in range(2 * n)},
        compiler_params=pltpu.CompilerParams(has_side_effects=_SIDE_EFFECT),
    )(*[pltpu.with_memory_space_constraint(a, pltpu.HBM) for a in (*srcs, *lands)], after)
    return out[0], out[1], out[2:2 + n], out[2 + n:2 + 2 * n], out[-1]


def _push_wait(name, started, per_peer, after):
    send_sems, recv_sems, srcs, lands, _ = started
    n = len(srcs)

    def body(*refs):
        for cp in _push_copies(refs[:n], refs[n:2 * n], refs[2 * n], refs[2 * n + 1], per_peer):
            cp.wait_send()
            cp.wait_recv()

    out = pl.pallas_call(
        body, name=name, out_shape=[pltpu.HBM(a.shape, a.dtype) for a in (*srcs, *lands)],
        in_specs=[_IN_HBM] * (2 * n) + [_SEM, _SEM, pl.BlockSpec(memory_space=pl.ANY)], out_specs=[_IN_HBM] * (2 * n),
        input_output_aliases={a: a for a in range(2 * n)},
        compiler_params=pltpu.CompilerParams(has_side_effects=_SIDE_EFFECT),
    )(*srcs, *lands, send_sems, recv_sems, after)
    return out[n:]


def _my_slot():
    return 4 * lax.axis_index("x") + 2 * lax.axis_index("y") + lax.axis_index("c")


def _row_tile(rows, cols, budget):
    if rows * cols * 4 <= budget or rows % 16:
        return rows
    best = 16
    for t in range(16, rows + 1, 16):
        if rows % t == 0 and t * cols * 4 <= budget:
            best = t
    return best


def _sum_slots(name, recv):
    _, R, C = recv.shape
    tr = _row_tile(R, C, 1 << 20)

    def body(r_ref, o_ref):
        g = r_ref[0].astype(F32)
        for k in range(1, N_DEV):
            g = g + r_ref[k].astype(F32)
        o_ref[...] = g

    return pl.pallas_call(body, grid=(R // tr,), in_specs=[pl.BlockSpec((N_DEV, tr, C), lambda i: (0, i, 0))],
                          out_specs=pl.BlockSpec((tr, C), lambda i: (i, 0)), out_shape=jax.ShapeDtypeStruct((R, C), F32),
                          compiler_params=_params(("arbitrary",), 32), name=name)(recv)


def _adamw(name, g, w, m, v):
    R, C = w.shape
    tr = _row_tile(R, C, 1 << 20)
    c1 = 1.0 - ADAM_B1 ** ADAM_STEP
    c2 = 1.0 - ADAM_B2 ** ADAM_STEP

    def body(g_ref, w_ref, m_ref, v_ref, d_ref, mo_ref, vo_ref):
        gv = g_ref[...]
        mn = ADAM_B1 * m_ref[...] + (1.0 - ADAM_B1) * gv
        vn = ADAM_B2 * v_ref[...] + (1.0 - ADAM_B2) * (gv * gv)
        mo_ref[...] = mn
        vo_ref[...] = vn
        d_ref[...] = -ADAM_LR * ((mn / c1) / (jnp.sqrt(vn / c2) + ADAM_EPS) + ADAM_WD * w_ref[...])

    blk = pl.BlockSpec((tr, C), lambda i: (i, 0))
    shp = jax.ShapeDtypeStruct((R, C), F32)
    return pl.pallas_call(body, grid=(R // tr,), in_specs=[blk, blk, blk, blk], out_specs=[blk, blk, blk], out_shape=[shp, shp, shp],
                          compiler_params=_params(("arbitrary",), 32), name=name)(g, w, m, v)


def _pad_flat(a, n):
    a = a.reshape(-1)
    return jnp.pad(a, (0, n - a.shape[0]))


def _seg(n):
    return -(-n // FLAT_ALIGN) * FLAT_ALIGN


def _to_blocks(full, axis):
    shp = full.shape
    return jnp.moveaxis(full.reshape(shp[:axis] + (N_DEV, shp[axis] // N_DEV) + shp[axis + 1:]), axis, 0)


def _from_blocks(blocks, axis):
    b = jnp.moveaxis(blocks, 0, axis)
    shp = b.shape
    return b.reshape(shp[:axis] + (shp[axis] * shp[axis + 1],) + shp[axis + 2:])


def _as_rows(shard, n):
    return shard.T if SHARD_AXIS[n] == 2 else shard


def _with_own(lands, own, me):
    return [lax.dynamic_update_slice(land, o[None], (me, 0, 0)) for land, o in zip(lands, own)]


def kernel(x, mem, positions, norm_mix_pre, norm_mix_post, w_in, pool_w, pool_scale, conv_b_w, w_branch_a, w_branch_b, w_branch_c, w_out, norm_mem_pre, norm_mem_post, norm_memkv, w_mq, w_mkv, w_mo, norm_ffn_pre, norm_ffn_post, w_up, conv_ffn_w, w_down, loss_target, m_norm_mix_pre, m_norm_mix_post, m_w_in, m_pool_w, m_pool_scale, m_conv_b_w, m_w_branch_a, m_w_branch_b, m_w_branch_c, m_w_out, m_norm_mem_pre, m_norm_mem_post, m_norm_memkv, m_w_mq, m_w_mkv, m_w_mo, m_norm_ffn_pre, m_norm_ffn_post, m_w_up, m_conv_ffn_w, m_w_down, v_norm_mix_pre, v_norm_mix_post, v_w_in, v_pool_w, v_pool_scale, v_conv_b_w, v_w_branch_a, v_w_branch_b, v_w_branch_c, v_w_out, v_norm_mem_pre, v_norm_mem_post, v_norm_memkv, v_w_mq, v_w_mkv, v_w_mo, v_norm_ffn_pre, v_norm_ffn_post, v_w_up, v_conv_ffn_w, v_w_down):
    w = dict(norm_mix_pre=norm_mix_pre, norm_mix_post=norm_mix_post, w_in=w_in, pool_w=pool_w, pool_scale=pool_scale, conv_b_w=conv_b_w, w_branch_a=w_branch_a, w_branch_b=w_branch_b, w_branch_c=w_branch_c, w_out=w_out, norm_mem_pre=norm_mem_pre, norm_mem_post=norm_mem_post, norm_memkv=norm_memkv, w_mq=w_mq, w_mkv=w_mkv, w_mo=w_mo, norm_ffn_pre=norm_ffn_pre, norm_ffn_post=norm_ffn_post, w_up=w_up, conv_ffn_w=conv_ffn_w, w_down=w_down)
    m = dict(norm_mix_pre=m_norm_mix_pre, norm_mix_post=m_norm_mix_post, w_in=m_w_in, pool_w=m_pool_w, pool_scale=m_pool_scale, conv_b_w=m_conv_b_w, w_branch_a=m_w_branch_a, w_branch_b=m_w_branch_b, w_branch_c=m_w_branch_c, w_out=m_w_out, norm_mem_pre=m_norm_mem_pre, norm_mem_post=m_norm_mem_post, norm_memkv=m_norm_memkv, w_mq=m_w_mq, w_mkv=m_w_mkv, w_mo=m_w_mo, norm_ffn_pre=m_norm_ffn_pre, norm_ffn_post=m_norm_ffn_post, w_up=m_w_up, conv_ffn_w=m_conv_ffn_w, w_down=m_w_down)
    v = dict(norm_mix_pre=v_norm_mix_pre, norm_mix_post=v_norm_mix_post, w_in=v_w_in, pool_w=v_pool_w, pool_scale=v_pool_scale, conv_b_w=v_conv_b_w, w_branch_a=v_w_branch_a, w_branch_b=v_w_branch_b, w_branch_c=v_w_branch_c, w_out=v_w_out, norm_mem_pre=v_norm_mem_pre, norm_mem_post=v_norm_mem_post, norm_memkv=v_norm_memkv, w_mq=v_w_mq, w_mkv=v_w_mkv, w_mo=v_w_mo, norm_ffn_pre=v_norm_ffn_pre, norm_ffn_post=v_norm_ffn_post, w_up=v_w_up, conv_ffn_w=v_conv_ffn_w, w_down=v_w_down)

    me = _my_slot()
    blocks = [[_as_rows(w[n][l], n).astype(MXU) for n in BIG] for l in range(DEPTH)]
    conv = jnp.concatenate([_pad_flat(w[n], _seg(w[n].size)) for n in F32_GATHERED]).reshape(-1, LANES)
    got0 = _all_gather("weights_all_gather_0", blocks[0] + [conv])
    conv_all = got0[-1].reshape(N_DEV, -1)
    small, off = {n: w[n] for n in WEIGHTS if n not in SHARD_AXIS}, 0
    for n in F32_GATHERED:
        small[n] = _from_blocks(conv_all[:, off:off + w[n].size].reshape((N_DEV,) + w[n].shape), 2)
        off += _seg(w[n].size)
    whole = lambda got: {n: o.reshape(-1, o.shape[-1]) for n, o in zip(BIG, got)}

    ctab, stab = _rope_tables(positions[0])
    push_w = _push_start("weights_push_start_1", blocks[1], False, got0[0])
    W0 = _layer_weights(whole(got0), small, 0)
    x1, sv0 = _layer_fwd(x[0], mem[0], dict(W0, g_mix_pre=W0['g_mix_pre'] + push_w[4][0, 0]), ctab, stab)
    W1 = _layer_weights(whole(_with_own(_push_wait("weights_push_wait_1", push_w, False, x1), blocks[1], me)), small, 1)
    x2, sv1 = _layer_fwd(x1, mem[0], W1, ctab, stab)
    dx, acc = _loss_head(x2, loss_target[0])
    loss = lax.psum(jnp.sum(acc) * (0.5 / D), MESH_AXES)
    grads = [None] * DEPTH
    dx, grads[1] = _layer_bwd(dx, mem[0], W1, sv1, ctab, stab)
    sent = [None, [grads[1][n].reshape(N_DEV, -1, grads[1][n].shape[-1]) for n in BIG]]
    push_g = _push_start("grads_push_start_1", sent[1], True, dx)
    dx, grads[0] = _layer_bwd(dx, mem[0], dict(W0, g_ffn_post=W0['g_ffn_post'] + push_g[4][0, 0]), sv0, ctab, stab)
    recv1 = _with_own(_push_wait("grads_push_wait_1", push_g, True, dx), [lax.dynamic_index_in_dim(s, me, 0, keepdims=False) for s in sent[1]], me)
    sent[0] = [grads[0][n].reshape(N_DEV, -1, grads[0][n].shape[-1]) for n in BIG]

    misc_names = [n for n in WEIGHTS if n not in BIG]
    stacked = {n: jnp.stack([grads[l][n].reshape(small[n].shape[1:]) for l in range(DEPTH)]) for n in misc_names}
    rows = [(_to_blocks(stacked[n], 2) if n in SHARD_AXIS else jnp.broadcast_to(stacked[n][None], (N_DEV,) + stacked[n].shape))
            for n in misc_names]
    segs = [_seg(w[n].size) for n in misc_names]
    misc = jnp.concatenate([jnp.pad(r.reshape(N_DEV, -1), ((0, 0), (0, s - r[0].size))) for r, s in zip(rows, segs)],
                           axis=1).reshape(N_DEV, -1, LANES)
    recv0 = _exchange("grad_exchange_0", sent[0] + [misc])
    g_out, per_layer = {}, {}
    for l, recv in enumerate((recv0, recv1)):
        for n, r in zip(BIG, recv):
            per_layer[n, l] = _as_rows(_sum_slots(f"sum_{n}_{l}", r), n)
    misc_sum = _sum_slots("sum_misc", recv0[-1]).reshape(-1)
    off = 0
    for n, s in zip(misc_names, segs):
        g_out[n] = misc_sum[off:off + w[n].size].reshape(w[n].shape)
        off += s
    for n in BIG:
        g_out[n] = jnp.stack([per_layer[n, l] for l in range(DEPTH)])

    res = [[], [], [], []]
    for n in WEIGHTS:
        shp = w[n].shape
        d, mn, vn = _adamw(f"adamw_{n}", *[a.reshape(-1, shp[-1]) for a in (g_out[n], w[n], m[n], v[n])])
        for k, a in enumerate((g_out[n], d, mn, vn)):
            res[k].append(a.reshape(shp))
    return (loss, dx[None], *res[0], *res[1], *res[2], *res[3])
```

```python
import jax
import jax.numpy as jnp
from jax import lax
from jax.experimental import pallas as pl
from jax.experimental.pallas import tpu as pltpu

F32 = jnp.float32
MXU = jnp.bfloat16
HI = lax.Precision.HIGHEST

D = 1024
DEPTH = 2
POOLW = 384
ATT_W = 768
ATT_O = 256
GATE_W = 3 * D
IN_W = 6912
MEM_W = 512
D_FF = 2816
EPS = 1e-6
ROPE_THETA = 500000.0
QB = 128
DILS = (1, 4, 16)
NEG = -1e30
MEM_SCALE = 128 ** -0.5
ATT_SCALE = 0.125

ADAM_LR, ADAM_B1, ADAM_B2, ADAM_EPS, ADAM_WD, ADAM_STEP = 0.001, 0.9, 0.999, 1e-08, 0.01, 10

N_DEV = 8
MESH_AXES = ("x", "y", "c")
LANES = 128
FLAT_ALIGN = 2048
ROW_TILE = 1024

WEIGHTS = ['norm_mix_pre', 'norm_mix_post', 'w_in', 'pool_w', 'pool_scale', 'conv_b_w', 'w_branch_a', 'w_branch_b',
           'w_branch_c', 'w_out', 'norm_mem_pre', 'norm_mem_post', 'norm_memkv', 'w_mq', 'w_mkv', 'w_mo',
           'norm_ffn_pre', 'norm_ffn_post', 'w_up', 'conv_ffn_w', 'w_down']
SHARD_AXIS = {'w_in': 2, 'conv_b_w': 2, 'w_branch_a': 2, 'w_branch_b': 2, 'w_branch_c': 2, 'w_out': 1, 'w_mq': 1,
              'w_mkv': 1, 'w_mo': 2, 'w_up': 2, 'conv_ffn_w': 2, 'w_down': 1}
F32_GATHERED = ('conv_b_w', 'conv_ffn_w')
BIG = [n for n in WEIGHTS if n in SHARD_AXIS and n not in F32_GATHERED]
LATE_BIG = ['w_mq', 'w_mkv', 'w_mo', 'w_up', 'w_down']


def _params(sem, vmem_mb):
    return pltpu.CompilerParams(dimension_semantics=sem, vmem_limit_bytes=vmem_mb << 20)


def _dot(a, b, prec=None):
    return lax.dot_general(a, b, (((1,), (0,)), ((), ())), preferred_element_type=F32, precision=prec)


def _dot_nt(a, b, prec=None):
    return lax.dot_general(a, b, (((1,), (1,)), ((), ())), preferred_element_type=F32, precision=prec)


def _dot_tn(a, b, prec=None):
    return lax.dot_general(a, b, (((0,), (0,)), ((), ())), preferred_element_type=F32, precision=prec)


def _tile(n, cap):
    if n <= cap:
        return n
    best = None
    for t in range(LANES, cap + 1, LANES):
        if n % t == 0:
            best = t
    assert best is not None, (n, cap)
    return best


def _rms(x, g):
    r = lax.rsqrt(jnp.mean(x * x, axis=-1, keepdims=True) + EPS)
    return x * r * g, r


def _rms_bwd(w, y):
    r = lax.rsqrt(jnp.mean(y * y, axis=-1, keepdims=True) + EPS)
    return r * w - y * (r * r * r) * jnp.mean(w * y, axis=-1, keepdims=True), r


def _rows_call(name, body, n_rows, ts, ins, outs, scratch=(), reverse=False, vmem_mb=48):
    nt = n_rows // ts
    assert nt * ts == n_rows

    def tile_of(g):
        return (nt - 1 - g) if reverse else g

    in_specs, args = [], []
    for op in ins:
        if op[0] == "t":
            _, a, cw, cb = op
            in_specs.append(pl.BlockSpec((ts, cw), lambda g, cb=cb: (tile_of(g), cb)))
        elif op[0] == "h":
            _, a, hr, cw, cb = op
            in_specs.append(pl.BlockSpec((hr, cw), lambda g, cb=cb, k=ts // hr: (jnp.maximum(tile_of(g) * k - 1, 0), cb)))
        else:
            _, a = op
            in_specs.append(pl.BlockSpec(a.shape, lambda g, n=a.ndim: (0,) * n))
        args.append(a)
    out_specs, out_shape = [], []
    for op in outs:
        if op[0] == "t":
            _, cols, dt = op
            out_specs.append(pl.BlockSpec((ts, cols), lambda g: (tile_of(g), 0)))
            out_shape.append(jax.ShapeDtypeStruct((n_rows, cols), dt))
        else:
            _, shp, dt = op
            out_specs.append(pl.BlockSpec(shp, lambda g, n=len(shp): (0,) * n))
            out_shape.append(jax.ShapeDtypeStruct(shp, dt))

    def kern(*refs):
        g = pl.program_id(0)
        body(tile_of(g), g, *refs)

    return pl.pallas_call(kern, grid=(nt,), in_specs=in_specs, out_specs=out_specs, out_shape=out_shape,
                          scratch_shapes=list(scratch), compiler_params=_params(("arbitrary",), vmem_mb), name=name)(*args)


def _acc(ref, g, val):
    @pl.when(g == 0)
    def _():
        ref[...] = val

    @pl.when(g != 0)
    def _():
        ref[...] += val


def _norm_mm(name, x, g, w, ts, tn, out_dtype=F32, wt=False):
    S, K = x.shape
    N = w.shape[0] if wt else w.shape[1]

    def body(x_ref, g_ref, w_ref, o_ref, h_ref, hs):
        @pl.when(pl.program_id(1) == 0)
        def _():
            h, _ = _rms(x_ref[...], g_ref[...])
            hs[...] = h.astype(MXU)
            h_ref[...] = h.astype(MXU)

        o_ref[...] = (_dot_nt if wt else _dot)(hs[...], w_ref[...]).astype(out_dtype)

    w_spec = pl.BlockSpec((tn, K), lambda i, j: (j, 0)) if wt else pl.BlockSpec((K, tn), lambda i, j: (0, j))
    return pl.pallas_call(
        body, grid=(S // ts, N // tn),
        in_specs=[pl.BlockSpec((ts, K), lambda i, j: (i, 0)), pl.BlockSpec((1, K), lambda i, j: (0, 0)), w_spec],
        out_specs=[pl.BlockSpec((ts, tn), lambda i, j: (i, j)), pl.BlockSpec((ts, K), lambda i, j: (i, 0))],
        out_shape=[jax.ShapeDtypeStruct((S, N), out_dtype), jax.ShapeDtypeStruct((S, K), MXU)],
        scratch_shapes=[pltpu.VMEM((ts, K), MXU)],
        compiler_params=_params(("arbitrary", "arbitrary"), 48), name=name)(x, g, w)


def _mm_nt(name, a, b, ts, tn, out_dtype=F32):
    M, K = a.shape
    N = b.shape[0]

    def body(a_ref, b_ref, o_ref):
        o_ref[...] = _dot_nt(a_ref[...], b_ref[...]).astype(out_dtype)

    return pl.pallas_call(
        body, grid=(M // ts, N // tn),
        in_specs=[pl.BlockSpec((ts, K), lambda i, j: (i, 0)), pl.BlockSpec((tn, K), lambda i, j: (j, 0))],
        out_specs=pl.BlockSpec((ts, tn), lambda i, j: (i, j)), out_shape=jax.ShapeDtypeStruct((M, N), out_dtype),
        compiler_params=_params(("arbitrary", "arbitrary"), 48), name=name)(a, b)


def _mm_tn(name, a, b, cap_k=512, cap_n=512, out_dtype=MXU):
    S, K = a.shape
    N = b.shape[1]
    tk, tn = _tile(K, cap_k), _tile(N, cap_n)

    def body(a_ref, b_ref, o_ref):
        o_ref[...] = _dot_tn(a_ref[...], b_ref[...]).astype(out_dtype)

    return pl.pallas_call(
        body, grid=(K // tk, N // tn),
        in_specs=[pl.BlockSpec((S, tk), lambda i, j: (0, i)), pl.BlockSpec((S, tn), lambda i, j: (0, j))],
        out_specs=pl.BlockSpec((tk, tn), lambda i, j: (i, j)), out_shape=jax.ShapeDtypeStruct((K, N), out_dtype),
        compiler_params=_params(("arbitrary", "arbitrary"), 48), name=name)(a, b)


def _pool_cols(shape):
    col = lax.broadcasted_iota(jnp.int32, shape, 1)
    return col < 96, col < 192, col < 288


def _pool_select(s2, s4, s8, s16):
    c1, c2, c3 = _pool_cols(s2.shape)
    return jnp.where(c1, s2, jnp.where(c2, s4, jnp.where(c3, s8, s16)))


def _pool_cnt(t0, ts):
    c1, c2, c3 = _pool_cols((ts, POOLW))
    win = jnp.where(c1, 2, jnp.where(c2, 4, jnp.where(c3, 8, 16)))
    t = t0 + lax.broadcasted_iota(jnp.int32, (ts, POOLW), 0)
    return jnp.minimum(t + 1, win).astype(F32)


def _pooled(a, prev, t0):
    ts = a.shape[0]
    ext = jnp.concatenate([prev, a], axis=0)
    s2 = ext + pltpu.roll(ext, 1, axis=0)
    s4 = s2 + pltpu.roll(s2, 2, axis=0)
    s8 = s4 + pltpu.roll(s4, 4, axis=0)
    s16 = s8 + pltpu.roll(s8, 8, axis=0)
    sums = _pool_select(s2, s4, s8, s16)[16:]
    return sums / _pool_cnt(t0, ts) - a


def _conv3(z, prev8, w):
    ext = jnp.concatenate([prev8, z], axis=0)
    z1 = pltpu.roll(ext, 1, axis=0)[8:]
    z2 = pltpu.roll(ext, 2, axis=0)[8:]
    return w[0:1] * z2 + w[1:2] * z1 + w[2:3] * z, z1, z2


def _conv3_t(dc, next8, w):
    ts = dc.shape[0]
    ext = jnp.concatenate([dc, next8], axis=0)
    n = ts + 8
    u1 = pltpu.roll(ext, n - 1, axis=0)[:ts]
    u2 = pltpu.roll(ext, n - 2, axis=0)[:ts]
    return w[2:3] * dc + w[1:2] * u1 + w[0:1] * u2


def _poolconv_fwd(u, wblk, pool_scale, conv_b, ts=256):
    S = u.shape[0]

    def body(i, g, a_ref, bx_ref, bb_ref, bc_ref, wblk_ref, ps_ref, cw_ref, a2_ref, yb_ref, ca, cz):
        @pl.when(g == 0)
        def _():
            ca[...] = jnp.zeros_like(ca)
            cz[...] = jnp.zeros_like(cz)

        a = a_ref[...]
        p = _pooled(a, ca[...], i * ts)
        mixed = _dot(p.astype(MXU), wblk_ref[...])
        a2_ref[...] = (mixed * ps_ref[...]).astype(MXU)
        z = bc_ref[...] * bx_ref[...]
        conv, _, _ = _conv3(z, cz[...], cw_ref[...])
        yb_ref[...] = (bb_ref[...] * conv).astype(MXU)
        ca[...] = a[ts - 16:]
        cz[...] = z[ts - 8:]

    ins = [("t", u, POOLW, 8), ("t", u, POOLW, 9), ("t", u, POOLW, 10), ("t", u, POOLW, 11), ("w", wblk), ("w", pool_scale),
           ("w", conv_b)]
    return _rows_call("poolconv_fwd", body, S, ts, ins, [("t", POOLW, MXU), ("t", POOLW, MXU)],
                      scratch=[pltpu.VMEM((16, POOLW), F32), pltpu.VMEM((8, POOLW), F32)])


def _poolconv_bwd(u, d_a2, d_yb, wblk, pool_scale, conv_b, ts=256):
    S = u.shape[0]

    def body(i, g, a_ref, bx_ref, bb_ref, bc_ref, ap_ref, bxp_ref, bcp_ref, da2_ref, dyb_ref, wblk_ref, ps_ref, cw_ref,
             o_ref, dps_ref, dwb_ref, dcw_ref, ce, cdz):
        @pl.when(g == 0)
        def _():
            ce[...] = jnp.zeros_like(ce)
            cdz[...] = jnp.zeros_like(cdz)

        first = (i > 0).astype(F32)
        a = a_ref[...]
        p = _pooled(a, ap_ref[...] * first, i * ts)
        pb = p.astype(MXU)
        mixed = _dot(pb, wblk_ref[...])
        da2 = da2_ref[...]
        dmixed = (da2 * ps_ref[...]).astype(MXU)
        dp = _dot_nt(dmixed, wblk_ref[...])
        _acc(dps_ref, g, jnp.sum(da2 * mixed, axis=0, keepdims=True))
        _acc(dwb_ref, g, _dot_tn(pb, dmixed))
        e = dp / _pool_cnt(i * ts, ts)
        ext = jnp.concatenate([e, ce[...]], axis=0)
        n = ts + 16
        f2 = ext + pltpu.roll(ext, n - 1, axis=0)
        f4 = f2 + pltpu.roll(f2, n - 2, axis=0)
        f8 = f4 + pltpu.roll(f4, n - 4, axis=0)
        f16 = f8 + pltpu.roll(f8, n - 8, axis=0)
        o_ref[:, 0:POOLW] = (_pool_select(f2, f4, f8, f16)[:ts] - dp).astype(o_ref.dtype)
        ce[...] = e[:16]

        bx, bb, bc = bx_ref[...], bb_ref[...], bc_ref[...]
        z = bc * bx
        w = cw_ref[...]
        conv, z1, z2 = _conv3(z, bxp_ref[...] * bcp_ref[...] * first, w)
        dyb = dyb_ref[...]
        dconv = dyb * bb
        dz = _conv3_t(dconv, cdz[...], w)
        o_ref[:, POOLW:2 * POOLW] = (dz * bc).astype(o_ref.dtype)
        o_ref[:, 2 * POOLW:3 * POOLW] = (dyb * conv).astype(o_ref.dtype)
        o_ref[:, 3 * POOLW:4 * POOLW] = (dz * bx).astype(o_ref.dtype)
        dw = jnp.concatenate([jnp.sum(dconv * z2, axis=0, keepdims=True), jnp.sum(dconv * z1, axis=0, keepdims=True),
                              jnp.sum(dconv * z, axis=0, keepdims=True)], axis=0)
        _acc(dcw_ref, g, dw)
        cdz[...] = dconv[:8]

    ins = [("t", u, POOLW, 8), ("t", u, POOLW, 9), ("t", u, POOLW, 10), ("t", u, POOLW, 11),
           ("h", u, 16, POOLW, 8), ("h", u, 8, POOLW, 9), ("h", u, 8, POOLW, 11),
           ("t", d_a2, POOLW, 0), ("t", d_yb, POOLW, 0), ("w", wblk), ("w", pool_scale), ("w", conv_b)]
    outs = [("t", 4 * POOLW, MXU), ("a", (1, POOLW), F32), ("a", (POOLW, POOLW), F32), ("a", (3, POOLW), F32)]
    return _rows_call("poolconv_bwd", body, S, ts, ins, outs,
                      scratch=[pltpu.VMEM((16, POOLW), F32), pltpu.VMEM((8, POOLW), F32)], reverse=True)


def _rope_tables(positions):
    S = positions.shape[0]
    inv = ROPE_THETA ** (-jnp.arange(0, 16, 2, dtype=F32) / 16)
    ang = positions.astype(F32)[:, None] * inv
    cos, sin = jnp.cos(ang), jnp.sin(ang)
    c64 = jnp.concatenate([cos, cos, jnp.ones((S, 48), F32)], axis=1)
    s64 = jnp.concatenate([-sin, sin, jnp.zeros((S, 48), F32)], axis=1)
    return jnp.concatenate([c64, c64], axis=1), jnp.concatenate([s64, s64], axis=1)


def _partner(x):
    lane = lax.broadcasted_iota(jnp.int32, x.shape, 1) % 64
    return jnp.where(lane < 8, pltpu.roll(x, LANES - 8, axis=1), jnp.where(lane < 16, pltpu.roll(x, 8, axis=1), 0.0))


def _rope(x, c, s):
    return x * c + _partner(x) * s


def _rope_t(x, c, s):
    return x * c + _partner(x * s)


def _rows_of(r, n, d):
    return pl.ds(r, n, stride=d) if d > 1 else pl.ds(0, n)


def _head_masks(shape):
    lane = lax.broadcasted_iota(jnp.int32, shape, 1) // 64
    return [lane == h for h in range(4)]


def _only(mask, x):
    return jnp.where(mask, x, jnp.zeros_like(x))


def _rope_perm(u, ctab, stab, ts=256):
    S = u.shape[0]
    nch = ATT_W // LANES

    def body(*refs):
        chunks, (c_ref, s_ref), outs = refs[:3 * nch], refs[3 * nch:3 * nch + 2], refs[3 * nch + 2:]
        for g, d in enumerate(DILS):
            n = ts // d
            for r in range(d):
                rows = _rows_of(r, n, d)
                c, s = c_ref[rows, :], s_ref[rows, :]
                for which in range(3):
                    parts = [chunks[which * nch + j][rows, :] for j in (2 * g, 2 * g + 1)]
                    if which < 2:
                        parts = [_rope(x, c, s) for x in parts]
                    outs[which * 3 + g][r] = jnp.concatenate(parts, axis=1).astype(MXU)

    base = (IN_W - 3 * ATT_W) // LANES
    in_specs = [pl.BlockSpec((ts, LANES), lambda i, cb=base + k: (i, cb)) for k in range(3 * nch)]
    in_specs += [pl.BlockSpec((ts, LANES), lambda i: (i, 0))] * 2
    out_specs = [pl.BlockSpec((d, ts // d, ATT_O), lambda i: (0, i, 0)) for _ in range(3) for d in DILS]
    out_shape = [jax.ShapeDtypeStruct((d, S // d, ATT_O), MXU) for _ in range(3) for d in DILS]
    res = pl.pallas_call(body, grid=(S // ts,), in_specs=in_specs, out_specs=out_specs, out_shape=out_shape,
                         compiler_params=_params(("arbitrary",), 32), name="rope_perm")(*([u] * (3 * nch)), ctab, stab)
    return [[res[which * 3 + g].reshape(S, ATT_O) for g in range(3)] for which in range(3)]


def _rope_unperm_bwd(dqkv, ctab, stab, ts=256):
    S = dqkv[0][0].shape[0]
    nch = ATT_W // LANES

    def body(*refs):
        ins, (c_ref, s_ref, o_ref, scr) = refs[:9], refs[9:]
        for g, d in enumerate(DILS):
            n = ts // d
            for r in range(d):
                rows = _rows_of(r, n, d)
                c, s = c_ref[rows, :], s_ref[rows, :]
                for which in range(3):
                    v = ins[which * 3 + g][r]
                    for half in range(2):
                        x = v[:, half * LANES:(half + 1) * LANES]
                        scr.at[which * nch + 2 * g + half][rows, :] = _rope_t(x, c, s) if which < 2 else x
        for j in range(3 * nch):
            o_ref[:, j * LANES:(j + 1) * LANES] = scr[j].astype(o_ref.dtype)

    in_specs = [pl.BlockSpec((d, ts // d, ATT_O), lambda i: (0, i, 0)) for _ in range(3) for d in DILS]
    in_specs += [pl.BlockSpec((ts, LANES), lambda i: (i, 0))] * 2
    args = [dqkv[which][g].reshape(d, S // d, ATT_O) for which in range(3) for g, d in enumerate(DILS)]
    return pl.pallas_call(body, grid=(S // ts,), in_specs=in_specs, out_specs=pl.BlockSpec((ts, 3 * ATT_W), lambda i: (i, 0)),
                          out_shape=jax.ShapeDtypeStruct((S, 3 * ATT_W), MXU), scratch_shapes=[pltpu.VMEM((3 * nch, ts, LANES), F32)],
                          compiler_params=_params(("arbitrary",), 32), name="rope_unperm_bwd")(*args, ctab, stab)


def _band_mask_keys(has_prev):
    r = lax.broadcasted_iota(jnp.int32, (QB, 2 * QB), 0)
    c = lax.broadcasted_iota(jnp.int32, (QB, 2 * QB), 1)
    return ((c < QB) & (c >= r) & has_prev) | ((c >= QB) & (c - QB <= r))


def _band_mask_queries(has_next):
    r = lax.broadcasted_iota(jnp.int32, (2 * QB, QB), 0)
    c = lax.broadcasted_iota(jnp.int32, (2 * QB, QB), 1)
    return ((r < QB) & (c <= r)) | ((r >= QB) & (c >= r - QB) & has_next)


def _blk(fn):
    return pl.BlockSpec((QB, ATT_O), fn)


_CUR = lambda b: (b, 0)
_PREV = lambda b: (jnp.maximum(b - 1, 0), 0)


def _attn_fwd(g, q, k, v):
    S = q.shape[0]
    nblk = S // QB // DILS[g]

    def body(q_ref, kc_ref, kp_ref, vc_ref, vp_ref, o_ref, m_ref, l_ref):
        ok = _band_mask_keys((pl.program_id(0) & (nblk - 1)) > 0)
        k2 = jnp.concatenate([kp_ref[...], kc_ref[...]], axis=0)
        v2 = jnp.concatenate([vp_ref[...], vc_ref[...]], axis=0)
        qv = q_ref[...]
        hm_kv, hm_o = _head_masks((2 * QB, ATT_O)), _head_masks((QB, ATT_O))
        o_acc = jnp.zeros((QB, ATT_O), F32)
        m_acc = jnp.zeros((QB, ATT_O), F32)
        l_acc = jnp.zeros((QB, ATT_O), F32)
        for h in range(4):
            s = jnp.where(ok, _dot_nt(qv, _only(hm_kv[h], k2)) * ATT_SCALE, NEG)
            m = jnp.max(s, axis=1, keepdims=True)
            p = jnp.exp(s - m)
            o_acc = o_acc + _dot(p.astype(MXU), _only(hm_kv[h], v2))
            m_acc = jnp.where(hm_o[h], m, m_acc)
            l_acc = jnp.where(hm_o[h], jnp.sum(p, axis=1, keepdims=True), l_acc)
        o_ref[...] = o_acc
        m_ref[...] = m_acc
        l_ref[...] = l_acc

    shp = jax.ShapeDtypeStruct((S, ATT_O), F32)
    return pl.pallas_call(body, grid=(S // QB,), in_specs=[_blk(_CUR), _blk(_CUR), _blk(_PREV), _blk(_CUR), _blk(_PREV)],
                          out_specs=[_blk(_CUR)] * 3, out_shape=[shp, shp, shp], compiler_params=_params(("arbitrary",), 32),
                          name=f"attn_fwd_{g}")(q, k, k, v, v)


def _natural(ref, d, scr, ts):
    if d == 1:
        return ref[0]
    n = ts // d
    for r in range(d):
        v = ref[r]
        scr.at[0][pl.ds(r, n, stride=d), :] = v[:, 0:LANES]
        scr.at[1][pl.ds(r, n, stride=d), :] = v[:, LANES:2 * LANES]
    return jnp.concatenate([scr[0], scr[1]], axis=1)


def _attn_combine(oml, ts=256):
    S = oml[0][0].shape[0]

    def body(*refs):
        ins, (att_ref, out_ref, lse_ref, scr) = refs[:9], refs[9:]
        o, m, l = [[_natural(ins[3 * g + k], d, scr, ts) for g, d in enumerate(DILS)] for k in range(3)]
        mx = jnp.maximum(jnp.maximum(m[0], m[1]), m[2])
        w = [jnp.exp(m[g] - mx) for g in range(3)]
        den = w[0] * l[0] + w[1] * l[1] + w[2] * l[2]
        out = (w[0] * o[0] + w[1] * o[1] + w[2] * o[2]) / den
        out_ref[...] = out
        att_ref[...] = out.astype(MXU)
        lse_ref[...] = mx + jnp.log(den)

    in_specs = [pl.BlockSpec((d, ts // d, ATT_O), lambda i: (0, i, 0)) for d in DILS for _ in range(3)]
    args = [a.reshape(d, S // d, ATT_O) for d, grp in zip(DILS, oml) for a in grp]
    blk = pl.BlockSpec((ts, ATT_O), lambda i: (i, 0))
    return pl.pallas_call(body, grid=(S // ts,), in_specs=in_specs, out_specs=[blk, blk, blk],
                          out_shape=[jax.ShapeDtypeStruct((S, ATT_O), MXU), jax.ShapeDtypeStruct((S, ATT_O), F32),
                                     jax.ShapeDtypeStruct((S, ATT_O), F32)],
                          scratch_shapes=[pltpu.VMEM((2, ts, LANES), F32)], compiler_params=_params(("arbitrary",), 32),
                          name="attn_combine")(*args)


def _attn_bwd_prep(datt, o, lse, ts=256):
    S = datt.shape[0]

    def body(da0, da1, o_ref, l0, l1, *rest):
        outs, dl = rest[:9], rest[9]
        prod = jnp.concatenate([da0[...], da1[...]], axis=1) * o_ref[...]
        delta = jnp.zeros((ts, ATT_O), F32)
        for hm in _head_masks((ts, ATT_O)):
            delta = jnp.where(hm, jnp.sum(_only(hm, prod), axis=1, keepdims=True), delta)
        dl[0] = delta[:, 0:LANES]
        dl[1] = delta[:, LANES:2 * LANES]
        for g, d in enumerate(DILS):
            n = ts // d
            for r in range(d):
                rows = _rows_of(r, n, d)
                outs[g][r] = jnp.concatenate([da0[rows, :], da1[rows, :]], axis=1).astype(MXU)
                outs[3 + g][r] = jnp.concatenate([dl.at[0][rows, :], dl.at[1][rows, :]], axis=1)
                outs[6 + g][r] = jnp.concatenate([l0[rows, :], l1[rows, :]], axis=1)

    half = lambda j: pl.BlockSpec((ts, LANES), lambda i: (i, j))
    out_specs = [pl.BlockSpec((d, ts // d, ATT_O), lambda i: (0, i, 0)) for _ in range(3) for d in DILS]
    out_shape = [jax.ShapeDtypeStruct((d, S // d, ATT_O), dt) for dt in (MXU, F32, F32) for d in DILS]
    res = pl.pallas_call(body, grid=(S // ts,), in_specs=[half(0), half(1), pl.BlockSpec((ts, ATT_O), lambda i: (i, 0)), half(0), half(1)],
                         out_specs=out_specs, out_shape=out_shape, scratch_shapes=[pltpu.VMEM((2, ts, LANES), F32)],
                         compiler_params=_params(("arbitrary",), 32), name="attn_bwd_prep")(datt, datt, o, lse, lse)
    return [[res[k * 3 + g].reshape(S, ATT_O) for g in range(3)] for k in range(3)]


def _head_col(x, h):
    return x[:, h * 64:h * 64 + 1]


def _attn_dq(g, q, k, v, do, delta, lse):
    S = q.shape[0]
    nblk = S // QB // DILS[g]

    def body(q_ref, kc_ref, kp_ref, vc_ref, vp_ref, do_ref, dl_ref, lse_ref, dq_ref):
        ok = _band_mask_keys((pl.program_id(0) & (nblk - 1)) > 0)
        k2 = jnp.concatenate([kp_ref[...], kc_ref[...]], axis=0)
        v2 = jnp.concatenate([vp_ref[...], vc_ref[...]], axis=0)
        qv, dov, dl, lse_v = q_ref[...], do_ref[...], dl_ref[...], lse_ref[...]
        dq = jnp.zeros((QB, ATT_O), F32)
        for h, hm in enumerate(_head_masks((2 * QB, ATT_O))):
            kh = _only(hm, k2)
            p = jnp.where(ok, jnp.exp(_dot_nt(qv, kh) * ATT_SCALE - _head_col(lse_v, h)), 0.0)
            ds = p * (_dot_nt(dov, _only(hm, v2)) - _head_col(dl, h))
            dq = dq + _dot(ds.astype(MXU), kh)
        dq_ref[...] = dq * ATT_SCALE

    specs = [_blk(_CUR), _blk(_CUR), _blk(_PREV), _blk(_CUR), _blk(_PREV), _blk(_CUR), _blk(_CUR), _blk(_CUR)]
    return pl.pallas_call(body, grid=(S // QB,), in_specs=specs, out_specs=_blk(_CUR), out_shape=jax.ShapeDtypeStruct((S, ATT_O), F32),
                          compiler_params=_params(("arbitrary",), 32), name=f"attn_dq_{g}")(q, k, k, v, v, do, delta, lse)


def _attn_dkv(g, q, k, v, do, delta, lse):
    S = q.shape[0]
    nb = S // QB
    nblk = nb // DILS[g]

    def body(k_ref, v_ref, qc_ref, qn_ref, doc_ref, don_ref, dlc_ref, dln_ref, lc_ref, ln_ref, dk_ref, dv_ref):
        ok = _band_mask_queries(((pl.program_id(0) + 1) & (nblk - 1)) > 0)
        q2 = jnp.concatenate([qc_ref[...], qn_ref[...]], axis=0)
        do2 = jnp.concatenate([doc_ref[...], don_ref[...]], axis=0)
        dl2 = jnp.concatenate([dlc_ref[...], dln_ref[...]], axis=0)
        lse2 = jnp.concatenate([lc_ref[...], ln_ref[...]], axis=0)
        kv, vv = k_ref[...], v_ref[...]
        dk = jnp.zeros((QB, ATT_O), F32)
        dv = jnp.zeros((QB, ATT_O), F32)
        for h, hm in enumerate(_head_masks((2 * QB, ATT_O))):
            qh, doh = _only(hm, q2), _only(hm, do2)
            p = jnp.where(ok, jnp.exp(_dot_nt(qh, kv) * ATT_SCALE - _head_col(lse2, h)), 0.0)
            ds = p * (_dot_nt(doh, vv) - _head_col(dl2, h))
            dv = dv + _dot_tn(p.astype(MXU), doh)
            dk = dk + _dot_tn(ds.astype(MXU), qh)
        dk_ref[...] = dk * ATT_SCALE
        dv_ref[...] = dv

    nxt = _blk(lambda b: (jnp.minimum(b + 1, nb - 1), 0))
    cur = _blk(_CUR)
    shp = jax.ShapeDtypeStruct((S, ATT_O), F32)
    return pl.pallas_call(body, grid=(nb,), in_specs=[cur, cur, cur, nxt, cur, nxt, cur, nxt, cur, nxt], out_specs=[cur, cur],
                          out_shape=[shp, shp], compiler_params=_params(("arbitrary",), 32),
                          name=f"attn_dkv_{g}")(k, v, q, q, do, do, delta, delta, lse, lse)


def _merge_fwd(x0, u, a2, yb, att, wa, wb, wc, w_out, g_post, ts=256):
    S = x0.shape[0]

    def body(i, g, x_ref, gate_ref, a2_ref, yb_ref, att_ref, wa_ref, wb_ref, wc_ref, wo_ref, gp_ref, mg_ref, y_ref, xo_ref):
        merged = jax.nn.sigmoid(gate_ref[:, 0:D]) * _dot_nt(a2_ref[...], wa_ref[...])
        merged = merged + jax.nn.sigmoid(gate_ref[:, D:2 * D]) * _dot_nt(yb_ref[...], wb_ref[...])
        merged = merged + jax.nn.sigmoid(gate_ref[:, 2 * D:3 * D]) * _dot_nt(att_ref[...], wc_ref[...])
        mb = merged.astype(MXU)
        mg_ref[...] = mb
        y = _dot(mb, wo_ref[...])
        y_ref[...] = y
        xo_ref[...] = x_ref[...] + _rms(y, gp_ref[...])[0]

    ins = [("t", x0, D, 0), ("t", u, GATE_W, 0), ("t", a2, POOLW, 0), ("t", yb, POOLW, 0), ("t", att, ATT_O, 0),
           ("w", wa), ("w", wb), ("w", wc), ("w", w_out), ("w", g_post)]
    return _rows_call("merge_fwd", body, S, ts, ins, [("t", D, MXU), ("t", D, F32), ("t", D, F32)])


def _merge_bwd(dx, y1, u, a2, yb, att, wa, wb, wc, w_out, g_post, ts=256):
    S = dx.shape[0]

    def body(i, g, dx_ref, y_ref, gate_ref, a2_ref, yb_ref, att_ref, wa_ref, wb_ref, wc_ref, wo_ref, gp_ref,
             dy_ref, dgate_ref, dbra_ref, dbrb_ref, dbrc_ref, da2_ref, dyb_ref, datt_ref, dgp_ref):
        dxv, y = dx_ref[...], y_ref[...]
        dy, r = _rms_bwd(dxv * gp_ref[...], y)
        _acc(dgp_ref, g, jnp.sum(dxv * (y * r), axis=0, keepdims=True))
        dyb16 = dy.astype(MXU)
        dy_ref[...] = dyb16
        dm = _dot_nt(dyb16, wo_ref[...])
        for n, (src, w_ref, dbr_ref, din_ref) in enumerate(((a2_ref, wa_ref, dbra_ref, da2_ref), (yb_ref, wb_ref, dbrb_ref, dyb_ref),
                                                           (att_ref, wc_ref, dbrc_ref, datt_ref))):
            gt = jax.nn.sigmoid(gate_ref[:, n * D:(n + 1) * D])
            br = _dot_nt(src[...], w_ref[...])
            dgate_ref[:, n * D:(n + 1) * D] = (dm * br * gt * (1.0 - gt)).astype(dgate_ref.dtype)
            dbr = (dm * gt).astype(MXU)
            dbr_ref[...] = dbr
            din_ref[...] = _dot(dbr, w_ref[...])

    ins = [("t", dx, D, 0), ("t", y1, D, 0), ("t", u, GATE_W, 0), ("t", a2, POOLW, 0), ("t", yb, POOLW, 0), ("t", att, ATT_O, 0),
           ("w", wa), ("w", wb), ("w", wc), ("w", w_out), ("w", g_post)]
    outs = [("t", D, MXU), ("t", GATE_W, MXU), ("t", D, MXU), ("t", D, MXU), ("t", D, MXU), ("t", POOLW, F32), ("t", POOLW, F32),
            ("t", ATT_O, F32), ("a", (1, D), F32)]
    return _rows_call("merge_bwd", body, S, ts, ins, outs)


def _prenorm_bwd(name, dx_res, du, wt, x, g_pre, ts=256):
    S = x.shape[0]
    N = du.shape[1]

    def body(i, g, dx_ref, du_ref, wt_ref, x_ref, g_ref, o_ref, dg_ref):
        dhv, xv = _dot(du_ref[...], wt_ref[...]), x_ref[...]
        dxn, r = _rms_bwd(dhv * g_ref[...], xv)
        o_ref[...] = dx_ref[...] + dxn
        _acc(dg_ref, g, jnp.sum(dhv * (xv * r), axis=0, keepdims=True))

    ins = [("t", dx_res, D, 0), ("t", du, N, 0), ("w", wt), ("t", x, D, 0), ("w", g_pre)]
    return _rows_call(name, body, S, ts, ins, [("t", D, F32), ("a", (1, D), F32)], vmem_mb=52)


def _mem_heads(qm, kv_ref):
    out = []
    for h in range(4):
        q = qm[:, h * 128:(h + 1) * 128].astype(MXU)
        k = kv_ref[:, h * 128:(h + 1) * 128]
        v = kv_ref[:, MEM_W + h * 128:MEM_W + (h + 1) * 128]
        sc = _dot_nt(q, k) * MEM_SCALE
        e = jnp.exp(sc - jnp.max(sc, axis=1, keepdims=True))
        out.append((e / jnp.sum(e, axis=1, keepdims=True), q, k, v))
    return out


def _mem_fwd(x1, kv, g_pre, w_mq, w_mo, g_post, ts=256):
    S = x1.shape[0]

    def body(i, g, x_ref, kv_ref, gq_ref, wq_ref, wo_ref, gp_ref, om_ref, h_ref, y_ref, xo_ref):
        x = x_ref[...]
        hb = _rms(x, gq_ref[...])[0].astype(MXU)
        h_ref[...] = hb
        qm = _dot(hb, wq_ref[...])
        om = jnp.concatenate([_dot(p.astype(MXU), v) for p, _, _, v in _mem_heads(qm, kv_ref)], axis=1).astype(MXU)
        om_ref[...] = om
        y = _dot_nt(om, wo_ref[...])
        y_ref[...] = y
        xo_ref[...] = x + _rms(y, gp_ref[...])[0]

    ins = [("t", x1, D, 0), ("w", kv), ("w", g_pre), ("w", w_mq), ("w", w_mo), ("w", g_post)]
    return _rows_call("mem_fwd", body, S, ts, ins, [("t", MEM_W, MXU), ("t", D, MXU), ("t", D, F32), ("t", D, F32)])


def _mem_bwd(dx2, ym, x1, kv, g_pre, w_mq, w_mo, g_post, ts=256):
    S = x1.shape[0]

    def body(i, g, dx_ref, y_ref, x_ref, kv_ref, gq_ref, wq_ref, wo_ref, gp_ref, dy_ref, dq_ref, dxo_ref, dgp_ref, dgq_ref, dkv_ref):
        dxv, y, x = dx_ref[...], y_ref[...], x_ref[...]
        dy, r = _rms_bwd(dxv * gp_ref[...], y)
        _acc(dgp_ref, g, jnp.sum(dxv * (y * r), axis=0, keepdims=True))
        dyb = dy.astype(MXU)
        dy_ref[...] = dyb
        dom = _dot(dyb, wo_ref[...])
        h, r1 = _rms(x, gq_ref[...])
        qm = _dot(h.astype(MXU), wq_ref[...])
        dqs = []

        @pl.when(g == 0)
        def _():
            dkv_ref[...] = jnp.zeros_like(dkv_ref)

        for hh, (p, q, k, v) in enumerate(_mem_heads(qm, kv_ref)):
            doh = dom[:, hh * 128:(hh + 1) * 128].astype(MXU)
            dp = _dot_nt(doh, v)
            dsc = (p * (dp - jnp.sum(dp * p, axis=1, keepdims=True)) * MEM_SCALE).astype(MXU)
            dqs.append(_dot(dsc, k))
            dkv_ref[:, hh * 128:(hh + 1) * 128] += _dot_tn(dsc, q)
            dkv_ref[:, MEM_W + hh * 128:MEM_W + (hh + 1) * 128] += _dot_tn(p.astype(MXU), doh)
        dq = jnp.concatenate(dqs, axis=1).astype(MXU)
        dq_ref[...] = dq
        dh = _dot_nt(dq, wq_ref[...])
        _acc(dgq_ref, g, jnp.sum(dh * (x * r1), axis=0, keepdims=True))
        dxo_ref[...] = dxv + _rms_bwd(dh * gq_ref[...], x)[0]

    ins = [("t", dx2, D, 0), ("t", ym, D, 0), ("t", x1, D, 0), ("w", kv), ("w", g_pre), ("w", w_mq), ("w", w_mo), ("w", g_post)]
    outs = [("t", D, MXU), ("t", MEM_W, MXU), ("t", D, F32), ("a", (1, D), F32), ("a", (1, D), F32), ("a", (256, D), F32)]
    return _rows_call("mem_bwd", body, S, ts, ins, outs)


def _gain_grad(name, dn, x):
    n = x.shape[0]

    def body(i, g, dn_ref, x_ref, o_ref):
        xv = x_ref[...]
        r = lax.rsqrt(jnp.mean(xv * xv, axis=-1, keepdims=True) + EPS)
        o_ref[...] = jnp.sum(dn_ref[...] * (xv * r), axis=0, keepdims=True)

    return _rows_call(name, body, n, n, [("t", dn, D, 0), ("t", x, D, 0)], [("a", (1, D), F32)])[0]


def _ffn_fwd(x2, u3, conv_f, w_down, g_post, ts=256):
    S = x2.shape[0]

    def body(i, g, x_ref, ua_ref, ub_ref, cw_ref, wd_ref, gp_ref, act_ref, y_ref, xo_ref, cu):
        @pl.when(g == 0)
        def _():
            cu[...] = jnp.zeros_like(cu)

        ua = ua_ref[...]
        c, _, _ = _conv3(ua, cu[...], cw_ref[...])
        act = (c * jax.nn.sigmoid(c) * ub_ref[...]).astype(MXU)
        act_ref[...] = act
        y = _dot(act, wd_ref[...])
        y_ref[...] = y
        xo_ref[...] = x_ref[...] + _rms(y, gp_ref[...])[0]
        cu[...] = ua[ts - 8:]

    ins = [("t", x2, D, 0), ("t", u3, D_FF, 0), ("t", u3, D_FF, 1), ("w", conv_f), ("w", w_down), ("w", g_post)]
    return _rows_call("ffn_fwd", body, S, ts, ins, [("t", D_FF, MXU), ("t", D, F32), ("t", D, F32)],
                      scratch=[pltpu.VMEM((8, D_FF), F32)], vmem_mb=56)


def _ffn_bwd(dx3, y3, u3, conv_f, w_down, g_post, ts=128):
    S = dx3.shape[0]

    def body(i, g, dx_ref, y_ref, ua_ref, ub_ref, uap_ref, cw_ref, wd_ref, gp_ref, dy_ref, du_ref, dgp_ref, dcw_ref, cdc):
        @pl.when(g == 0)
        def _():
            cdc[...] = jnp.zeros_like(cdc)

        dxv, y = dx_ref[...], y_ref[...]
        dy, r = _rms_bwd(dxv * gp_ref[...], y)
        _acc(dgp_ref, g, jnp.sum(dxv * (y * r), axis=0, keepdims=True))
        dyb = dy.astype(MXU)
        dy_ref[...] = dyb
        dact = _dot_nt(dyb, wd_ref[...])
        ua, w = ua_ref[...], cw_ref[...]
        c, u1, u2 = _conv3(ua, uap_ref[...] * (i > 0).astype(F32), w)
        sg = jax.nn.sigmoid(c)
        du_ref[:, D_FF:2 * D_FF] = (dact * (c * sg)).astype(du_ref.dtype)
        dc = dact * ub_ref[...] * (sg * (1.0 + c * (1.0 - sg)))
        du_ref[:, 0:D_FF] = _conv3_t(dc, cdc[...], w).astype(du_ref.dtype)
        dw = jnp.concatenate([jnp.sum(dc * u2, axis=0, keepdims=True), jnp.sum(dc * u1, axis=0, keepdims=True),
                              jnp.sum(dc * ua, axis=0, keepdims=True)], axis=0)
        _acc(dcw_ref, g, dw)
        cdc[...] = dc[:8]

    ins = [("t", dx3, D, 0), ("t", y3, D, 0), ("t", u3, D_FF, 0), ("t", u3, D_FF, 1), ("h", u3, 8, D_FF, 0), ("w", conv_f),
           ("w", w_down), ("w", g_post)]
    outs = [("t", D, MXU), ("t", 2 * D_FF, MXU), ("a", (1, D), F32), ("a", (3, D_FF), F32)]
    return _rows_call("ffn_bwd", body, S, ts, ins, outs, scratch=[pltpu.VMEM((8, D_FF), F32)], reverse=True, vmem_mb=56)


def _loss_head(x, target, ts=512):
    S = x.shape[0]

    def body(i, g, x_ref, t_ref, dx_ref, acc_ref):
        diff = x_ref[...] - t_ref[...]
        dx_ref[...] = diff * (1.0 / D)
        col = jnp.sum(diff * diff, axis=0, keepdims=True)
        part = col[:, 0:LANES]
        for j in range(1, D // LANES):
            part = part + col[:, j * LANES:(j + 1) * LANES]
        row = lax.broadcasted_iota(jnp.int32, (8, LANES), 0)
        _acc(acc_ref, g, jnp.where(row == 0, jnp.broadcast_to(part, (8, LANES)), 0.0))

    return _rows_call("loss_head", body, S, ts, [("t", x, D, 0), ("t", target, D, 0)], [("t", D, F32), ("a", (8, LANES), F32)])


def _layer_weights(big, small, l):
    w_in = big['w_in']
    pool_w = small['pool_w'][l].astype(MXU)
    wblk = jnp.zeros((POOLW, POOLW), MXU)
    for g in range(4):
        wblk = lax.dynamic_update_slice(wblk, pool_w[g], (g * 96, g * 96))
    vec = lambda n: small[n][l].reshape(1, -1)
    return dict(
        w_in=jnp.concatenate([w_in[IN_W - GATE_W:], w_in[:IN_W - GATE_W]], axis=0),
        wblk=wblk, pool_scale=vec('pool_scale'), conv_b=small['conv_b_w'][l], wa=big['w_branch_a'], wb=big['w_branch_b'],
        wc=big['w_branch_c'], w_out=big['w_out'], w_mq=big['w_mq'], w_mkv=big['w_mkv'], w_mo=big['w_mo'],
        w_up=big['w_up'], conv_f=small['conv_ffn_w'][l], w_down=big['w_down'],
        g_mix_pre=vec('norm_mix_pre'), g_mix_post=vec('norm_mix_post'), g_mem_pre=vec('norm_mem_pre'),
        g_mem_post=vec('norm_mem_post'), g_memkv=vec('norm_memkv'), g_ffn_pre=vec('norm_ffn_pre'), g_ffn_post=vec('norm_ffn_post'))


def _layer_fwd(x0, mem, W, ctab, stab):
    sv = dict(x0=x0)
    sv['u'], sv['h1'] = _norm_mm("in_proj", x0, W['g_mix_pre'], W['w_in'], ts=1024, tn=768, wt=True)
    sv['a2'], sv['yb'] = _poolconv_fwd(sv['u'], W['wblk'], W['pool_scale'], W['conv_b'])
    sv['qkv'] = q3, k3, v3 = _rope_perm(sv['u'], ctab, stab)
    sv['att'], sv['o'], sv['lse'] = _attn_combine([_attn_fwd(g, q3[g], k3[g], v3[g]) for g in range(3)])
    sv['merged'], sv['y1'], sv['x1'] = _merge_fwd(x0, sv['u'], sv['a2'], sv['yb'], sv['att'], W['wa'], W['wb'], W['wc'],
                                                  W['w_out'], W['g_mix_post'])
    sv['kv'], sv['memn'] = _norm_mm("mem_kv", mem, W['g_memkv'], W['w_mkv'], ts=256, tn=D, out_dtype=MXU)
    sv['om'], sv['h2'], sv['ym'], sv['x2'] = _mem_fwd(sv['x1'], sv['kv'], W['g_mem_pre'], W['w_mq'], W['w_mo'], W['g_mem_post'])
    sv['u3'], sv['h3'] = _norm_mm("up_proj", sv['x2'], W['g_ffn_pre'], W['w_up'], ts=1024, tn=512, wt=True)
    sv['act'], sv['y3'], x3 = _ffn_fwd(sv['x2'], sv['u3'], W['conv_f'], W['w_down'], W['g_ffn_post'])
    return x3, sv


def _layer_bwd(dx3, mem, W, sv, ctab, stab):
    dx1, g = _layer_bwd_late(dx3, mem, W, sv)
    dx0, g_mix = _layer_bwd_mix(dx1, W, sv, ctab, stab)
    return dx0, {**g, **g_mix}


def _layer_bwd_late(dx3, mem, W, sv):
    g = {}
    dy3, du3, g['norm_ffn_post'], g['conv_ffn_w'] = _ffn_bwd(dx3, sv['y3'], sv['u3'], W['conv_f'], W['w_down'], W['g_ffn_post'])
    g['w_down'] = _mm_tn("dw_down", sv['act'], dy3, cap_k=256)
    g['w_up'] = _mm_tn("dw_up", du3, sv['h3'])
    dx2, g['norm_ffn_pre'] = _prenorm_bwd("ffn_pre_bwd", dx3, du3, W['w_up'], sv['x2'], W['g_ffn_pre'])
    dym, dqm, dx1, g['norm_mem_post'], g['norm_mem_pre'], dkv = _mem_bwd(dx2, sv['ym'], sv['x1'], sv['kv'], W['g_mem_pre'],
                                                                       W['w_mq'], W['w_mo'], W['g_mem_post'])
    g['w_mo'] = _mm_tn("dw_mo", dym, sv['om'])
    g['w_mq'] = _mm_tn("dw_mq", sv['h2'], dqm)
    dkvb = dkv.astype(MXU)
    g['w_mkv'] = _mm_tn("dw_mkv", sv['memn'], dkvb)
    g['norm_memkv'] = _gain_grad("memkv_gain", _mm_nt("d_memn", dkvb, W['w_mkv'], ts=256, tn=512), mem)
    return dx1, g


def _layer_bwd_mix(dx1, W, sv, ctab, stab):
    g = {}
    dy1, dgate, dbra, dbrb, dbrc, da2, dyb, datt, g['norm_mix_post'] = _merge_bwd(
        dx1, sv['y1'], sv['u'], sv['a2'], sv['yb'], sv['att'], W['wa'], W['wb'], W['wc'], W['w_out'], W['g_mix_post'])
    g['w_out'] = _mm_tn("dw_out", sv['merged'], dy1)
    g['w_branch_a'] = _mm_tn("dw_a", dbra, sv['a2'])
    g['w_branch_b'] = _mm_tn("dw_b", dbrb, sv['yb'])
    g['w_branch_c'] = _mm_tn("dw_c", dbrc, sv['att'])
    dabc, g['pool_scale'], dwblk, g['conv_b_w'] = _poolconv_bwd(sv['u'], da2, dyb, W['wblk'], W['pool_scale'], W['conv_b'])
    g['pool_w'] = jnp.stack([dwblk[k * 96:(k + 1) * 96, k * 96:(k + 1) * 96] for k in range(4)])
    q3, k3, v3 = sv['qkv']
    do3, dl3, lse3 = _attn_bwd_prep(datt, sv['o'], sv['lse'])
    dq3 = [_attn_dq(i, q3[i], k3[i], v3[i], do3[i], dl3[i], lse3[i]) for i in range(3)]
    dkv3 = [_attn_dkv(i, q3[i], k3[i], v3[i], do3[i], dl3[i], lse3[i]) for i in range(3)]
    dqkv = _rope_unperm_bwd([dq3, [a for a, _ in dkv3], [b for _, b in dkv3]], ctab, stab)
    du = jnp.concatenate([dgate, dabc, dqkv], axis=1)
    dw_in = _mm_tn("dw_in", du, sv['h1'], cap_k=768)
    g['w_in'] = jnp.concatenate([dw_in[GATE_W:], dw_in[:GATE_W]], axis=0)
    dx0, g['norm_mix_pre'] = _prenorm_bwd("mix_pre_bwd", dx1, du, W['w_in'], sv['x0'], W['g_mix_pre'])
    return dx0, g


def _local_step(x, mem, positions, target, big, small):
    ctab, stab = _rope_tables(positions)
    Ws = [_layer_weights(big[l], small, l) for l in range(DEPTH)]
    saved = []
    for l in range(DEPTH):
        x, sv = _layer_fwd(x, mem, Ws[l], ctab, stab)
        saved.append(sv)
    dx, acc = _loss_head(x, target)
    loss = jnp.sum(acc) * (0.5 / D)
    grads = [None] * DEPTH
    for l in reversed(range(DEPTH)):
        dx, grads[l] = _layer_bwd(dx, mem, Ws[l], saved[l], ctab, stab)
    return loss, dx, grads


_HBM = pl.BlockSpec(memory_space=pl.ANY)
MESH_ID = pl.DeviceIdType.MESH


def _all_gather(name, xs):
    n = len(xs)

    def body(*refs):
        x_refs, out_refs = refs[:n], refs[n:2 * n]
        send_sems, recv_sems, local_sems = refs[2 * n:]
        x, y, c = lax.axis_index("x"), lax.axis_index("y"), lax.axis_index("c")
        me, sibling = (x, y, c), (x, y, 1 - c)
        chips = [(1 - x, y), (x, 1 - y), (1 - x, 1 - y)]

        def slot(a, p):
            return out_refs[a].at[4 * p[0] + 2 * p[1] + p[2]]

        def copy(a, k, block, to, src=None):
            return pltpu.make_async_remote_copy(src_ref=slot(a, block) if src is None else src, dst_ref=slot(a, block),
                                                send_sem=send_sems.at[a, k], recv_sem=recv_sems.at[a, k], device_id=to,
                                                device_id_type=MESH_ID)

        started = []
        for a in range(n):
            mine = pltpu.make_async_copy(x_refs[a], slot(a, me), local_sems.at[a])
            mine.start()
            started.append(mine)
        first = []
        for a in range(n):
            first.append(copy(a, 0, me, sibling, src=x_refs[a]))
            first += [copy(a, 1 + j, me, (*chip, c), src=x_refs[a]) for j, chip in enumerate(chips)]
        for cp in first:
            cp.start()
        passed = []
        for j, chip in enumerate(chips):
            for a in range(n):
                copy(a, 1 + j, (*chip, c), me).wait_recv()
                fw = copy(a, 4 + j, (*chip, c), sibling)
                fw.start()
                passed.append(fw)
        for a in range(n):
            copy(a, 0, sibling, me).wait_recv()
            for j, chip in enumerate(chips):
                copy(a, 4 + j, (*chip, 1 - c), me).wait_recv()
        for cp in first + passed:
            cp.wait_send()
        for mine in started:
            mine.wait()

    return pl.pallas_call(
        body, out_shape=[jax.ShapeDtypeStruct((N_DEV,) + x.shape, x.dtype) for x in xs], in_specs=[_HBM] * n, out_specs=[_HBM] * n,
        scratch_shapes=[pltpu.SemaphoreType.DMA((n, 7)), pltpu.SemaphoreType.DMA((n, 7)), pltpu.SemaphoreType.DMA((n,))],
        name=name)(*xs)


def _exchange(name, gs):
    n = len(gs)

    def body(*refs):
        g_refs, out_refs = refs[:n], refs[n:2 * n]
        send_sems, recv_sems, local_sems = refs[2 * n:]
        x, y, c = lax.axis_index("x"), lax.axis_index("y"), lax.axis_index("c")
        me = 4 * x + 2 * y + c
        copies = []
        for a in range(n):
            mine = pltpu.make_async_copy(g_refs[a].at[me], out_refs[a].at[me], local_sems.at[a])
            mine.start()
            copies.append(mine)
        for r in range(1, N_DEV):
            px, py, pc = x ^ ((r >> 2) & 1), y ^ ((r >> 1) & 1), c ^ (r & 1)
            for a in range(n):
                cp = pltpu.make_async_remote_copy(src_ref=g_refs[a].at[4 * px + 2 * py + pc], dst_ref=out_refs[a].at[me],
                                                  send_sem=send_sems.at[a, r - 1], recv_sem=recv_sems.at[a, r - 1],
                                                  device_id=(px, py, pc), device_id_type=MESH_ID)
                cp.start()
                copies.append(cp)
        for cp in copies:
            cp.wait()

    return pl.pallas_call(
        body, out_shape=[jax.ShapeDtypeStruct(g.shape, g.dtype) for g in gs], in_specs=[_HBM] * n, out_specs=[_HBM] * n,
        scratch_shapes=[pltpu.SemaphoreType.DMA((n, N_DEV - 1)), pltpu.SemaphoreType.DMA((n, N_DEV - 1)), pltpu.SemaphoreType.DMA((n,))],
        name=name)(*gs)


_SEM = pl.BlockSpec(memory_space=pltpu.SEMAPHORE)
_IN_HBM = pl.BlockSpec(memory_space=pltpu.HBM)
_SIDE_EFFECT = pltpu.SideEffectType.DATAFLOW_SIDE_EFFECTING


def _push_copies(src_refs, land_refs, send_sems, recv_sems, per_peer):
    x, y, c = lax.axis_index("x"), lax.axis_index("y"), lax.axis_index("c")
    me = 4 * x + 2 * y + c
    copies = []
    for r in range(1, N_DEV):
        px, py, pc = x ^ ((r >> 2) & 1), y ^ ((r >> 1) & 1), c ^ (r & 1)
        for a, (s, d) in enumerate(zip(src_refs, land_refs)):
            k = a * (N_DEV - 1) + r - 1
            copies.append(pltpu.make_async_remote_copy(src_ref=s.at[4 * px + 2 * py + pc] if per_peer else s, dst_ref=d.at[me],
                                                       send_sem=send_sems.at[k], recv_sem=recv_sems.at[k],
                                                       device_id=(px, py, pc), device_id_type=MESH_ID))
    return copies


def _push_start(name, srcs, per_peer, after):
    n = len(srcs)
    lands = [lax.empty((N_DEV,) + (s.shape[1:] if per_peer else s.shape), s.dtype) for s in srcs]

    def body(*refs):
        for cp in _push_copies(refs[:n], refs[n:2 * n], refs[2 * n + 1], refs[2 * n + 2], per_peer):
            cp.start()
        refs[-1][...] = jnp.zeros_like(refs[-1])

    hbm = [pltpu.HBM(a.shape, a.dtype) for a in (*srcs, *lands)]
    sems = pltpu.SemaphoreType.DMA((n * (N_DEV - 1),))
    out = pl.pallas_call(
        body, name=name, out_shape=(sems, sems, *hbm, jax.ShapeDtypeStruct((8, LANES), F32)),
        in_specs=[_IN_HBM] * (2 * n) + [pl.BlockSpec(memory_space=pl.ANY)],
        out_specs=(_SEM, _SEM, *[_IN_HBM] * (2 * n), pl.BlockSpec(memory_space=pltpu.VMEM)),
        input_output_aliases={a: 2 + a for a in range(2 * n)},
        compiler_params=pltpu.CompilerParams(has_side_effects=_SIDE_EFFECT),
    )(*[pltpu.with_memory_space_constraint(a, pltpu.HBM) for a in (*srcs, *lands)], after)
    return out[0], out[1], out[2:2 + n], out[2 + n:2 + 2 * n], out[-1]


def _push_wait(name, started, per_peer, after):
    send_sems, recv_sems, srcs, lands, _ = started
    n = len(srcs)

    def body(*refs):
        for cp in _push_copies(refs[:n], refs[n:2 * n], refs[2 * n], refs[2 * n + 1], per_peer):
            cp.wait_send()
            cp.wait_recv()

    out = pl.pallas_call(
        body, name=name, out_shape=[pltpu.HBM(a.shape, a.dtype) for a in (*srcs, *lands)],
        in_specs=[_IN_HBM] * (2 * n) + [_SEM, _SEM, pl.BlockSpec(memory_space=pl.ANY)], out_specs=[_IN_HBM] * (2 * n),
        input_output_aliases={a: a for a in range(2 * n)},
        compiler_params=pltpu.CompilerParams(has_side_effects=_SIDE_EFFECT),
    )(*srcs, *lands, send_sems, recv_sems, after)
    return out[n:]


def _my_slot():
    return 4 * lax.axis_index("x") + 2 * lax.axis_index("y") + lax.axis_index("c")


def _row_tile(rows, cols, budget):
    if rows * cols * 4 <= budget or rows % 16:
        return rows
    best = 16
    for t in range(16, rows + 1, 16):
        if rows % t == 0 and t * cols * 4 <= budget:
            best = t
    return best


def _sum_slots(name, recv):
    _, R, C = recv.shape
    tr = _row_tile(R, C, 1 << 20)

    def body(r_ref, o_ref):
        g = r_ref[0].astype(F32)
        for k in range(1, N_DEV):
            g = g + r_ref[k].astype(F32)
        o_ref[...] = g

    return pl.pallas_call(body, grid=(R // tr,), in_specs=[pl.BlockSpec((N_DEV, tr, C), lambda i: (0, i, 0))],
                          out_specs=pl.BlockSpec((tr, C), lambda i: (i, 0)), out_shape=jax.ShapeDtypeStruct((R, C), F32),
                          compiler_params=_params(("arbitrary",), 32), name=name)(recv)


def _adamw(name, g, w, m, v):
    R, C = w.shape
    tr = _row_tile(R, C, 1 << 20)
    c1 = 1.0 - ADAM_B1 ** ADAM_STEP
    c2 = 1.0 - ADAM_B2 ** ADAM_STEP

    def body(g_ref, w_ref, m_ref, v_ref, d_ref, mo_ref, vo_ref):
        gv = g_ref[...]
        mn = ADAM_B1 * m_ref[...] + (1.0 - ADAM_B1) * gv
        vn = ADAM_B2 * v_ref[...] + (1.0 - ADAM_B2) * (gv * gv)
        mo_ref[...] = mn
        vo_ref[...] = vn
        d_ref[...] = -ADAM_LR * ((mn / c1) / (jnp.sqrt(vn / c2) + ADAM_EPS) + ADAM_WD * w_ref[...])

    blk = pl.BlockSpec((tr, C), lambda i: (i, 0))
    shp = jax.ShapeDtypeStruct((R, C), F32)
    return pl.pallas_call(body, grid=(R // tr,), in_specs=[blk, blk, blk, blk], out_specs=[blk, blk, blk], out_shape=[shp, shp, shp],
                          compiler_params=_params(("arbitrary",), 32), name=name)(g, w, m, v)


def _pad_flat(a, n):
    a = a.reshape(-1)
    return jnp.pad(a, (0, n - a.shape[0]))


def _seg(n):
    return -(-n // FLAT_ALIGN) * FLAT_ALIGN


def _to_blocks(full, axis):
    shp = full.shape
    return jnp.moveaxis(full.reshape(shp[:axis] + (N_DEV, shp[axis] // N_DEV) + shp[axis + 1:]), axis, 0)


def _from_blocks(blocks, axis):
    b = jnp.moveaxis(blocks, 0, axis)
    shp = b.shape
    return b.reshape(shp[:axis] + (shp[axis] * shp[axis + 1],) + shp[axis + 2:])


def _as_rows(shard, n):
    return shard.T if SHARD_AXIS[n] == 2 else shard


def _with_own(lands, own, me):
    return [lax.dynamic_update_slice(land, o[None], (me, 0, 0)) for land, o in zip(lands, own)]


def kernel(x, mem, positions, norm_mix_pre, norm_mix_post, w_in, pool_w, pool_scale, conv_b_w, w_branch_a, w_branch_b, w_branch_c, w_out, norm_mem_pre, norm_mem_post, norm_memkv, w_mq, w_mkv, w_mo, norm_ffn_pre, norm_ffn_post, w_up, conv_ffn_w, w_down, loss_target, m_norm_mix_pre, m_norm_mix_post, m_w_in, m_pool_w, m_pool_scale, m_conv_b_w, m_w_branch_a, m_w_branch_b, m_w_branch_c, m_w_out, m_norm_mem_pre, m_norm_mem_post, m_norm_memkv, m_w_mq, m_w_mkv, m_w_mo, m_norm_ffn_pre, m_norm_ffn_post, m_w_up, m_conv_ffn_w, m_w_down, v_norm_mix_pre, v_norm_mix_post, v_w_in, v_pool_w, v_pool_scale, v_conv_b_w, v_w_branch_a, v_w_branch_b, v_w_branch_c, v_w_out, v_norm_mem_pre, v_norm_mem_post, v_norm_memkv, v_w_mq, v_w_mkv, v_w_mo, v_norm_ffn_pre, v_norm_ffn_post, v_w_up, v_conv_ffn_w, v_w_down):
    w = dict(norm_mix_pre=norm_mix_pre, norm_mix_post=norm_mix_post, w_in=w_in, pool_w=pool_w, pool_scale=pool_scale, conv_b_w=conv_b_w, w_branch_a=w_branch_a, w_branch_b=w_branch_b, w_branch_c=w_branch_c, w_out=w_out, norm_mem_pre=norm_mem_pre, norm_mem_post=norm_mem_post, norm_memkv=norm_memkv, w_mq=w_mq, w_mkv=w_mkv, w_mo=w_mo, norm_ffn_pre=norm_ffn_pre, norm_ffn_post=norm_ffn_post, w_up=w_up, conv_ffn_w=conv_ffn_w, w_down=w_down)
    m = dict(norm_mix_pre=m_norm_mix_pre, norm_mix_post=m_norm_mix_post, w_in=m_w_in, pool_w=m_pool_w, pool_scale=m_pool_scale, conv_b_w=m_conv_b_w, w_branch_a=m_w_branch_a, w_branch_b=m_w_branch_b, w_branch_c=m_w_branch_c, w_out=m_w_out, norm_mem_pre=m_norm_mem_pre, norm_mem_post=m_norm_mem_post, norm_memkv=m_norm_memkv, w_mq=m_w_mq, w_mkv=m_w_mkv, w_mo=m_w_mo, norm_ffn_pre=m_norm_ffn_pre, norm_ffn_post=m_norm_ffn_post, w_up=m_w_up, conv_ffn_w=m_conv_ffn_w, w_down=m_w_down)
    v = dict(norm_mix_pre=v_norm_mix_pre, norm_mix_post=v_norm_mix_post, w_in=v_w_in, pool_w=v_pool_w, pool_scale=v_pool_scale, conv_b_w=v_conv_b_w, w_branch_a=v_w_branch_a, w_branch_b=v_w_branch_b, w_branch_c=v_w_branch_c, w_out=v_w_out, norm_mem_pre=v_norm_mem_pre, norm_mem_post=v_norm_mem_post, norm_memkv=v_norm_memkv, w_mq=v_w_mq, w_mkv=v_w_mkv, w_mo=v_w_mo, norm_ffn_pre=v_norm_ffn_pre, norm_ffn_post=v_norm_ffn_post, w_up=v_w_up, conv_ffn_w=v_conv_ffn_w, w_down=v_w_down)

    me = _my_slot()
    blocks = [[_as_rows(w[n][l], n).astype(MXU) for n in BIG] for l in range(DEPTH)]
    conv = jnp.concatenate([_pad_flat(w[n], _seg(w[n].size)) for n in F32_GATHERED]).reshape(-1, LANES)
    got0 = _all_gather("weights_all_gather_0", blocks[0] + [conv])
    conv_all = got0[-1].reshape(N_DEV, -1)
    small, off = {n: w[n] for n in WEIGHTS if n not in SHARD_AXIS}, 0
    for n in F32_GATHERED:
        small[n] = _from_blocks(conv_all[:, off:off + w[n].size].reshape((N_DEV,) + w[n].shape), 2)
        off += _seg(w[n].size)
    whole = lambda got: {n: o.reshape(-1, o.shape[-1]) for n, o in zip(BIG, got)}

    ctab, stab = _rope_tables(positions[0])
    push_w = _push_start("weights_push_start_1", blocks[1], False, got0[0])
    W0 = _layer_weights(whole(got0), small, 0)
    x1, sv0 = _layer_fwd(x[0], mem[0], dict(W0, g_mix_pre=W0['g_mix_pre'] + push_w[4][0, 0]), ctab, stab)
    W1 = _layer_weights(whole(_with_own(_push_wait("weights_push_wait_1", push_w, False, x1), blocks[1], me)), small, 1)
    x2, sv1 = _layer_fwd(x1, mem[0], W1, ctab, stab)
    dx, acc = _loss_head(x2, loss_target[0])
    loss = lax.psum(jnp.sum(acc) * (0.5 / D), MESH_AXES)
    grads = [None] * DEPTH
    dx, grads[1] = _layer_bwd(dx, mem[0], W1, sv1, ctab, stab)
    sent = [None, [grads[1][n].reshape(N_DEV, -1, grads[1][n].shape[-1]) for n in BIG]]
    push_g = _push_start("grads_push_start_1", sent[1], True, dx)
    dx, g_late = _layer_bwd_late(dx, mem[0], dict(W0, g_ffn_post=W0['g_ffn_post'] + push_g[4][0, 0]), sv0)
    sent_late = [g_late[n].reshape(N_DEV, -1, g_late[n].shape[-1]) for n in LATE_BIG]
    push_l = _push_start("grads_push_start_0", sent_late, True, dx)
    dx, g_mix = _layer_bwd_mix(dx, dict(W0, g_mix_post=W0['g_mix_post'] + push_l[4][0, 0]), sv0, ctab, stab)
    grads[0] = {**g_late, **g_mix}
    own = lambda s: [lax.dynamic_index_in_dim(a, me, 0, keepdims=False) for a in s]
    recv1 = _with_own(_push_wait("grads_push_wait_1", push_g, True, dx), own(sent[1]), me)
    recv_late = _with_own(_push_wait("grads_push_wait_0", push_l, True, dx), own(sent_late), me)
    mix_big = [n for n in BIG if n not in LATE_BIG]

    misc_names = [n for n in WEIGHTS if n not in BIG]
    stacked = {n: jnp.stack([grads[l][n].reshape(small[n].shape[1:]) for l in range(DEPTH)]) for n in misc_names}
    rows = [(_to_blocks(stacked[n], 2) if n in SHARD_AXIS else jnp.broadcast_to(stacked[n][None], (N_DEV,) + stacked[n].shape))
            for n in misc_names]
    segs = [_seg(w[n].size) for n in misc_names]
    misc = jnp.concatenate([jnp.pad(r.reshape(N_DEV, -1), ((0, 0), (0, s - r[0].size))) for r, s in zip(rows, segs)],
                           axis=1).reshape(N_DEV, -1, LANES)
    recv_mix = _exchange("grad_exchange_0", [g_mix[n].reshape(N_DEV, -1, g_mix[n].shape[-1]) for n in mix_big] + [misc])
    g_out, per_layer = {}, {}
    for l, names, recv in ((1, BIG, recv1), (0, LATE_BIG, recv_late), (0, mix_big, recv_mix)):
        for n, r in zip(names, recv):
            per_layer[n, l] = _as_rows(_sum_slots(f"sum_{n}_{l}", r), n)
    misc_sum = _sum_slots("sum_misc", recv_mix[-1]).reshape(-1)
    off = 0
    for n, s in zip(misc_names, segs):
        g_out[n] = misc_sum[off:off + w[n].size].reshape(w[n].shape)
        off += s
    for n in BIG:
        g_out[n] = jnp.stack([per_layer[n, l] for l in range(DEPTH)])

    res = [[], [], [], []]
    for n in WEIGHTS:
        shp = w[n].shape
        d, mn, vn = _adamw(f"adamw_{n}", *[a.reshape(-1, shp[-1]) for a in (g_out[n], w[n], m[n], v[n])])
        for k, a in enumerate((g_out[n], d, mn, vn)):
            res[k].append(a.reshape(shp))
    return (loss, dx[None], *res[0], *res[1], *res[2], *res[3])
```

```python
import jax
import jax.numpy as jnp
from jax import lax
from jax.experimental import pallas as pl
from jax.experimental.pallas import tpu as pltpu

F32 = jnp.float32
MXU = jnp.bfloat16
HI = lax.Precision.HIGHEST

D = 1024
DEPTH = 2
POOLW = 384
ATT_W = 768
ATT_O = 256
GATE_W = 3 * D
IN_W = 6912
IN_TILE = 768
IN_ROT = (IN_W - GATE_W) // IN_TILE
MEM_W = 512
D_FF = 2816
EPS = 1e-6
ROPE_THETA = 500000.0
QB = 128
DILS = (1, 4, 16)
NEG = -1e30
MEM_SCALE = 128 ** -0.5
ATT_SCALE = 0.125

ADAM_LR, ADAM_B1, ADAM_B2, ADAM_EPS, ADAM_WD, ADAM_STEP = 0.001, 0.9, 0.999, 1e-08, 0.01, 10

N_DEV = 8
MESH_AXES = ("x", "y", "c")
LANES = 128
FLAT_ALIGN = 2048
ROW_TILE = 1024

WEIGHTS = ['norm_mix_pre', 'norm_mix_post', 'w_in', 'pool_w', 'pool_scale', 'conv_b_w', 'w_branch_a', 'w_branch_b',
           'w_branch_c', 'w_out', 'norm_mem_pre', 'norm_mem_post', 'norm_memkv', 'w_mq', 'w_mkv', 'w_mo',
           'norm_ffn_pre', 'norm_ffn_post', 'w_up', 'conv_ffn_w', 'w_down']
SHARD_AXIS = {'w_in': 2, 'conv_b_w': 2, 'w_branch_a': 2, 'w_branch_b': 2, 'w_branch_c': 2, 'w_out': 1, 'w_mq': 1,
              'w_mkv': 1, 'w_mo': 2, 'w_up': 2, 'conv_ffn_w': 2, 'w_down': 1}
F32_GATHERED = ('conv_b_w', 'conv_ffn_w')
BIG = [n for n in WEIGHTS if n in SHARD_AXIS and n not in F32_GATHERED]
LATE_BIG = ['w_mq', 'w_mkv', 'w_mo', 'w_up', 'w_down']


def _params(sem, vmem_mb):
    return pltpu.CompilerParams(dimension_semantics=sem, vmem_limit_bytes=vmem_mb << 20)


def _dot(a, b, prec=None):
    return lax.dot_general(a, b, (((1,), (0,)), ((), ())), preferred_element_type=F32, precision=prec)


def _dot_nt(a, b, prec=None):
    return lax.dot_general(a, b, (((1,), (1,)), ((), ())), preferred_element_type=F32, precision=prec)


def _dot_tn(a, b, prec=None):
    return lax.dot_general(a, b, (((0,), (0,)), ((), ())), preferred_element_type=F32, precision=prec)


def _tile(n, cap):
    if n <= cap:
        return n
    best = None
    for t in range(LANES, cap + 1, LANES):
        if n % t == 0:
            best = t
    assert best is not None, (n, cap)
    return best


def _rms(x, g):
    r = lax.rsqrt(jnp.mean(x * x, axis=-1, keepdims=True) + EPS)
    return x * r * g, r


def _rms_bwd(w, y):
    r = lax.rsqrt(jnp.mean(y * y, axis=-1, keepdims=True) + EPS)
    return r * w - y * (r * r * r) * jnp.mean(w * y, axis=-1, keepdims=True), r


def _rows_call(name, body, n_rows, ts, ins, outs, scratch=(), reverse=False, vmem_mb=48):
    nt = n_rows // ts
    assert nt * ts == n_rows

    def tile_of(g):
        return (nt - 1 - g) if reverse else g

    in_specs, args = [], []
    for op in ins:
        if op[0] == "t":
            _, a, cw, cb = op
            in_specs.append(pl.BlockSpec((ts, cw), lambda g, cb=cb: (tile_of(g), cb)))
        elif op[0] == "h":
            _, a, hr, cw, cb = op
            in_specs.append(pl.BlockSpec((hr, cw), lambda g, cb=cb, k=ts // hr: (jnp.maximum(tile_of(g) * k - 1, 0), cb)))
        else:
            _, a = op
            in_specs.append(pl.BlockSpec(a.shape, lambda g, n=a.ndim: (0,) * n))
        args.append(a)
    out_specs, out_shape = [], []
    for op in outs:
        if op[0] == "t":
            _, cols, dt = op
            out_specs.append(pl.BlockSpec((ts, cols), lambda g: (tile_of(g), 0)))
            out_shape.append(jax.ShapeDtypeStruct((n_rows, cols), dt))
        else:
            _, shp, dt = op
            out_specs.append(pl.BlockSpec(shp, lambda g, n=len(shp): (0,) * n))
            out_shape.append(jax.ShapeDtypeStruct(shp, dt))

    def kern(*refs):
        g = pl.program_id(0)
        body(tile_of(g), g, *refs)

    return pl.pallas_call(kern, grid=(nt,), in_specs=in_specs, out_specs=out_specs, out_shape=out_shape,
                          scratch_shapes=list(scratch), compiler_params=_params(("arbitrary",), vmem_mb), name=name)(*args)


def _acc(ref, g, val):
    @pl.when(g == 0)
    def _():
        ref[...] = val

    @pl.when(g != 0)
    def _():
        ref[...] += val


def _norm_mm(name, x, g, w, ts, tn, out_dtype=F32, wt=False, rot=0):
    S, K = x.shape
    N = w.shape[0] if wt else w.shape[1]
    assert wt or not rot

    def body(x_ref, g_ref, w_ref, o_ref, h_ref, hs):
        @pl.when(pl.program_id(1) == 0)
        def _():
            h, _ = _rms(x_ref[...], g_ref[...])
            hs[...] = h.astype(MXU)
            h_ref[...] = h.astype(MXU)

        o_ref[...] = (_dot_nt if wt else _dot)(hs[...], w_ref[...]).astype(out_dtype)

    w_spec = pl.BlockSpec((tn, K), lambda i, j: ((j + rot) % (N // tn), 0)) if wt else pl.BlockSpec((K, tn), lambda i, j: (0, j))
    return pl.pallas_call(
        body, grid=(S // ts, N // tn),
        in_specs=[pl.BlockSpec((ts, K), lambda i, j: (i, 0)), pl.BlockSpec((1, K), lambda i, j: (0, 0)), w_spec],
        out_specs=[pl.BlockSpec((ts, tn), lambda i, j: (i, j)), pl.BlockSpec((ts, K), lambda i, j: (i, 0))],
        out_shape=[jax.ShapeDtypeStruct((S, N), out_dtype), jax.ShapeDtypeStruct((S, K), MXU)],
        scratch_shapes=[pltpu.VMEM((ts, K), MXU)],
        compiler_params=_params(("arbitrary", "arbitrary"), 48), name=name)(x, g, w)


def _mm_nt(name, a, b, ts, tn, out_dtype=F32):
    M, K = a.shape
    N = b.shape[0]

    def body(a_ref, b_ref, o_ref):
        o_ref[...] = _dot_nt(a_ref[...], b_ref[...]).astype(out_dtype)

    return pl.pallas_call(
        body, grid=(M // ts, N // tn),
        in_specs=[pl.BlockSpec((ts, K), lambda i, j: (i, 0)), pl.BlockSpec((tn, K), lambda i, j: (j, 0))],
        out_specs=pl.BlockSpec((ts, tn), lambda i, j: (i, j)), out_shape=jax.ShapeDtypeStruct((M, N), out_dtype),
        compiler_params=_params(("arbitrary", "arbitrary"), 48), name=name)(a, b)


def _mm_tn(name, a, b, cap_k=512, cap_n=1024, out_dtype=MXU, rot=0):
    S, K = a.shape
    N = b.shape[1]
    tk, tn = _tile(K, cap_k), _tile(N, cap_n)

    def body(a_ref, b_ref, o_ref):
        o_ref[...] = _dot_tn(a_ref[...], b_ref[...]).astype(out_dtype)

    return pl.pallas_call(
        body, grid=(K // tk, N // tn),
        in_specs=[pl.BlockSpec((S, tk), lambda i, j: (0, i)), pl.BlockSpec((S, tn), lambda i, j: (0, j))],
        out_specs=pl.BlockSpec((tk, tn), lambda i, j: ((i + rot) % (K // tk), j)), out_shape=jax.ShapeDtypeStruct((K, N), out_dtype),
        compiler_params=_params(("arbitrary", "arbitrary"), 48), name=name)(a, b)


def _pool_cols(shape):
    col = lax.broadcasted_iota(jnp.int32, shape, 1)
    return col < 96, col < 192, col < 288


def _pool_select(s2, s4, s8, s16):
    c1, c2, c3 = _pool_cols(s2.shape)
    return jnp.where(c1, s2, jnp.where(c2, s4, jnp.where(c3, s8, s16)))


def _pool_cnt(t0, ts):
    c1, c2, c3 = _pool_cols((ts, POOLW))
    win = jnp.where(c1, 2, jnp.where(c2, 4, jnp.where(c3, 8, 16)))
    t = t0 + lax.broadcasted_iota(jnp.int32, (ts, POOLW), 0)
    return jnp.minimum(t + 1, win).astype(F32)


def _pooled(a, prev, t0):
    ts = a.shape[0]
    ext = jnp.concatenate([prev, a], axis=0)
    s2 = ext + pltpu.roll(ext, 1, axis=0)
    s4 = s2 + pltpu.roll(s2, 2, axis=0)
    s8 = s4 + pltpu.roll(s4, 4, axis=0)
    s16 = s8 + pltpu.roll(s8, 8, axis=0)
    sums = _pool_select(s2, s4, s8, s16)[16:]
    return sums / _pool_cnt(t0, ts) - a


def _conv3(z, prev8, w):
    ext = jnp.concatenate([prev8, z], axis=0)
    z1 = pltpu.roll(ext, 1, axis=0)[8:]
    z2 = pltpu.roll(ext, 2, axis=0)[8:]
    return w[0:1] * z2 + w[1:2] * z1 + w[2:3] * z, z1, z2


def _conv3_t(dc, next8, w):
    ts = dc.shape[0]
    ext = jnp.concatenate([dc, next8], axis=0)
    n = ts + 8
    u1 = pltpu.roll(ext, n - 1, axis=0)[:ts]
    u2 = pltpu.roll(ext, n - 2, axis=0)[:ts]
    return w[2:3] * dc + w[1:2] * u1 + w[0:1] * u2


def _poolconv_fwd(u, wblk, pool_scale, conv_b, ts=256):
    S = u.shape[0]

    def body(i, g, a_ref, bx_ref, bb_ref, bc_ref, wblk_ref, ps_ref, cw_ref, a2_ref, yb_ref, ca, cz):
        @pl.when(g == 0)
        def _():
            ca[...] = jnp.zeros_like(ca)
            cz[...] = jnp.zeros_like(cz)

        a = a_ref[...]
        p = _pooled(a, ca[...], i * ts)
        mixed = _dot(p.astype(MXU), wblk_ref[...])
        a2_ref[...] = (mixed * ps_ref[...]).astype(MXU)
        z = bc_ref[...] * bx_ref[...]
        conv, _, _ = _conv3(z, cz[...], cw_ref[...])
        yb_ref[...] = (bb_ref[...] * conv).astype(MXU)
        ca[...] = a[ts - 16:]
        cz[...] = z[ts - 8:]

    ins = [("t", u, POOLW, 8), ("t", u, POOLW, 9), ("t", u, POOLW, 10), ("t", u, POOLW, 11), ("w", wblk), ("w", pool_scale),
           ("w", conv_b)]
    return _rows_call("poolconv_fwd", body, S, ts, ins, [("t", POOLW, MXU), ("t", POOLW, MXU)],
                      scratch=[pltpu.VMEM((16, POOLW), F32), pltpu.VMEM((8, POOLW), F32)])


def _poolconv_bwd(u, d_a2, d_yb, wblk, pool_scale, conv_b, ts=256):
    S = u.shape[0]

    def body(i, g, a_ref, bx_ref, bb_ref, bc_ref, ap_ref, bxp_ref, bcp_ref, da2_ref, dyb_ref, wblk_ref, ps_ref, cw_ref,
             o_ref, dps_ref, dwb_ref, dcw_ref, ce, cdz):
        @pl.when(g == 0)
        def _():
            ce[...] = jnp.zeros_like(ce)
            cdz[...] = jnp.zeros_like(cdz)

        first = (i > 0).astype(F32)
        a = a_ref[...]
        p = _pooled(a, ap_ref[...] * first, i * ts)
        pb = p.astype(MXU)
        mixed = _dot(pb, wblk_ref[...])
        da2 = da2_ref[...]
        dmixed = (da2 * ps_ref[...]).astype(MXU)
        dp = _dot_nt(dmixed, wblk_ref[...])
        _acc(dps_ref, g, jnp.sum(da2 * mixed, axis=0, keepdims=True))
        _acc(dwb_ref, g, _dot_tn(pb, dmixed))
        e = dp / _pool_cnt(i * ts, ts)
        ext = jnp.concatenate([e, ce[...]], axis=0)
        n = ts + 16
        f2 = ext + pltpu.roll(ext, n - 1, axis=0)
        f4 = f2 + pltpu.roll(f2, n - 2, axis=0)
        f8 = f4 + pltpu.roll(f4, n - 4, axis=0)
        f16 = f8 + pltpu.roll(f8, n - 8, axis=0)
        o_ref[:, 0:POOLW] = (_pool_select(f2, f4, f8, f16)[:ts] - dp).astype(o_ref.dtype)
        ce[...] = e[:16]

        bx, bb, bc = bx_ref[...], bb_ref[...], bc_ref[...]
        z = bc * bx
        w = cw_ref[...]
        conv, z1, z2 = _conv3(z, bxp_ref[...] * bcp_ref[...] * first, w)
        dyb = dyb_ref[...]
        dconv = dyb * bb
        dz = _conv3_t(dconv, cdz[...], w)
        o_ref[:, POOLW:2 * POOLW] = (dz * bc).astype(o_ref.dtype)
        o_ref[:, 2 * POOLW:3 * POOLW] = (dyb * conv).astype(o_ref.dtype)
        o_ref[:, 3 * POOLW:4 * POOLW] = (dz * bx).astype(o_ref.dtype)
        dw = jnp.concatenate([jnp.sum(dconv * z2, axis=0, keepdims=True), jnp.sum(dconv * z1, axis=0, keepdims=True),
                              jnp.sum(dconv * z, axis=0, keepdims=True)], axis=0)
        _acc(dcw_ref, g, dw)
        cdz[...] = dconv[:8]

    ins = [("t", u, POOLW, 8), ("t", u, POOLW, 9), ("t", u, POOLW, 10), ("t", u, POOLW, 11),
           ("h", u, 16, POOLW, 8), ("h", u, 8, POOLW, 9), ("h", u, 8, POOLW, 11),
           ("t", d_a2, POOLW, 0), ("t", d_yb, POOLW, 0), ("w", wblk), ("w", pool_scale), ("w", conv_b)]
    outs = [("t", 4 * POOLW, MXU), ("a", (1, POOLW), F32), ("a", (POOLW, POOLW), F32), ("a", (3, POOLW), F32)]
    return _rows_call("poolconv_bwd", body, S, ts, ins, outs,
                      scratch=[pltpu.VMEM((16, POOLW), F32), pltpu.VMEM((8, POOLW), F32)], reverse=True)


def _rope_tables(positions):
    S = positions.shape[0]
    inv = ROPE_THETA ** (-jnp.arange(0, 16, 2, dtype=F32) / 16)
    ang = positions.astype(F32)[:, None] * inv
    cos, sin = jnp.cos(ang), jnp.sin(ang)
    c64 = jnp.concatenate([cos, cos, jnp.ones((S, 48), F32)], axis=1)
    s64 = jnp.concatenate([-sin, sin, jnp.zeros((S, 48), F32)], axis=1)
    return jnp.concatenate([c64, c64], axis=1), jnp.concatenate([s64, s64], axis=1)


def _partner(x):
    lane = lax.broadcasted_iota(jnp.int32, x.shape, 1) % 64
    return jnp.where(lane < 8, pltpu.roll(x, LANES - 8, axis=1), jnp.where(lane < 16, pltpu.roll(x, 8, axis=1), 0.0))


def _rope(x, c, s):
    return x * c + _partner(x) * s


def _rope_t(x, c, s):
    return x * c + _partner(x * s)


def _rows_of(r, n, d):
    return pl.ds(r, n, stride=d) if d > 1 else pl.ds(0, n)


def _head_masks(shape):
    lane = lax.broadcasted_iota(jnp.int32, shape, 1) // 64
    return [lane == h for h in range(4)]


def _only(mask, x):
    return jnp.where(mask, x, jnp.zeros_like(x))


def _rope_perm(u, ctab, stab, ts=256):
    S = u.shape[0]
    nch = ATT_W // LANES

    def body(*refs):
        chunks, (c_ref, s_ref), outs = refs[:3 * nch], refs[3 * nch:3 * nch + 2], refs[3 * nch + 2:]
        for g, d in enumerate(DILS):
            n = ts // d
            for r in range(d):
                rows = _rows_of(r, n, d)
                c, s = c_ref[rows, :], s_ref[rows, :]
                for which in range(3):
                    parts = [chunks[which * nch + j][rows, :] for j in (2 * g, 2 * g + 1)]
                    if which < 2:
                        parts = [_rope(x, c, s) for x in parts]
                    outs[which * 3 + g][r] = jnp.concatenate(parts, axis=1).astype(MXU)

    base = (IN_W - 3 * ATT_W) // LANES
    in_specs = [pl.BlockSpec((ts, LANES), lambda i, cb=base + k: (i, cb)) for k in range(3 * nch)]
    in_specs += [pl.BlockSpec((ts, LANES), lambda i: (i, 0))] * 2
    out_specs = [pl.BlockSpec((d, ts // d, ATT_O), lambda i: (0, i, 0)) for _ in range(3) for d in DILS]
    out_shape = [jax.ShapeDtypeStruct((d, S // d, ATT_O), MXU) for _ in range(3) for d in DILS]
    res = pl.pallas_call(body, grid=(S // ts,), in_specs=in_specs, out_specs=out_specs, out_shape=out_shape,
                         compiler_params=_params(("arbitrary",), 32), name="rope_perm")(*([u] * (3 * nch)), ctab, stab)
    return [[res[which * 3 + g].reshape(S, ATT_O) for g in range(3)] for which in range(3)]


def _rope_unperm_bwd(dqkv, ctab, stab, ts=256):
    S = dqkv[0][0].shape[0]
    nch = ATT_W // LANES

    def body(*refs):
        ins, (c_ref, s_ref, o_ref, scr) = refs[:9], refs[9:]
        for g, d in enumerate(DILS):
            n = ts // d
            for r in range(d):
                rows = _rows_of(r, n, d)
                c, s = c_ref[rows, :], s_ref[rows, :]
                for which in range(3):
                    v = ins[which * 3 + g][r]
                    for half in range(2):
                        x = v[:, half * LANES:(half + 1) * LANES]
                        scr.at[which * nch + 2 * g + half][rows, :] = _rope_t(x, c, s) if which < 2 else x
        for j in range(3 * nch):
            o_ref[:, j * LANES:(j + 1) * LANES] = scr[j].astype(o_ref.dtype)

    in_specs = [pl.BlockSpec((d, ts // d, ATT_O), lambda i: (0, i, 0)) for _ in range(3) for d in DILS]
    in_specs += [pl.BlockSpec((ts, LANES), lambda i: (i, 0))] * 2
    args = [dqkv[which][g].reshape(d, S // d, ATT_O) for which in range(3) for g, d in enumerate(DILS)]
    return pl.pallas_call(body, grid=(S // ts,), in_specs=in_specs, out_specs=pl.BlockSpec((ts, 3 * ATT_W), lambda i: (i, 0)),
                          out_shape=jax.ShapeDtypeStruct((S, 3 * ATT_W), MXU), scratch_shapes=[pltpu.VMEM((3 * nch, ts, LANES), F32)],
                          compiler_params=_params(("arbitrary",), 32), name="rope_unperm_bwd")(*args, ctab, stab)


def _band_mask_keys(has_prev):
    r = lax.broadcasted_iota(jnp.int32, (QB, 2 * QB), 0)
    c = lax.broadcasted_iota(jnp.int32, (QB, 2 * QB), 1)
    return ((c < QB) & (c >= r) & has_prev) | ((c >= QB) & (c - QB <= r))


def _band_mask_queries(has_next):
    r = lax.broadcasted_iota(jnp.int32, (2 * QB, QB), 0)
    c = lax.broadcasted_iota(jnp.int32, (2 * QB, QB), 1)
    return ((r < QB) & (c <= r)) | ((r >= QB) & (c >= r - QB) & has_next)


ASUB = 4
_BIG = pl.BlockSpec((ASUB * QB, ATT_O), lambda b: (b, 0))
_PREV = pl.BlockSpec((QB, ATT_O), lambda b: (jnp.maximum(b * ASUB - 1, 0), 0))


def _sub(ref, j):
    return ref[j * QB:(j + 1) * QB]


def _attn_fwd(g, q, k, v):
    S = q.shape[0]
    nb = S // QB
    nblk = nb // DILS[g]

    def body(q_ref, kc_ref, kp_ref, vc_ref, vp_ref, o_ref, m_ref, l_ref):
        hm_kv, hm_o = _head_masks((2 * QB, ATT_O)), _head_masks((QB, ATT_O))
        for j in range(ASUB):
            ok = _band_mask_keys(((pl.program_id(0) * ASUB + j) & (nblk - 1)) > 0)
            k2 = jnp.concatenate([kp_ref[...] if j == 0 else _sub(kc_ref, j - 1), _sub(kc_ref, j)], axis=0)
            v2 = jnp.concatenate([vp_ref[...] if j == 0 else _sub(vc_ref, j - 1), _sub(vc_ref, j)], axis=0)
            qv = _sub(q_ref, j)
            o_acc = jnp.zeros((QB, ATT_O), F32)
            m_acc = jnp.zeros((QB, ATT_O), F32)
            l_acc = jnp.zeros((QB, ATT_O), F32)
            for h in range(4):
                s = jnp.where(ok, _dot_nt(qv, _only(hm_kv[h], k2)) * ATT_SCALE, NEG)
                m = jnp.max(s, axis=1, keepdims=True)
                p = jnp.exp(s - m)
                o_acc = o_acc + _dot(p.astype(MXU), _only(hm_kv[h], v2))
                m_acc = jnp.where(hm_o[h], m, m_acc)
                l_acc = jnp.where(hm_o[h], jnp.sum(p, axis=1, keepdims=True), l_acc)
            o_ref[j * QB:(j + 1) * QB] = o_acc
            m_ref[j * QB:(j + 1) * QB] = m_acc
            l_ref[j * QB:(j + 1) * QB] = l_acc

    shp = jax.ShapeDtypeStruct((S, ATT_O), F32)
    return pl.pallas_call(body, grid=(nb // ASUB,), in_specs=[_BIG, _BIG, _PREV, _BIG, _PREV],
                          out_specs=[_BIG] * 3, out_shape=[shp, shp, shp], compiler_params=_params(("arbitrary",), 32),
                          name=f"attn_fwd_{g}")(q, k, k, v, v)


def _natural(ref, d, scr, ts):
    if d == 1:
        return ref[0]
    n = ts // d
    for r in range(d):
        v = ref[r]
        scr.at[0][pl.ds(r, n, stride=d), :] = v[:, 0:LANES]
        scr.at[1][pl.ds(r, n, stride=d), :] = v[:, LANES:2 * LANES]
    return jnp.concatenate([scr[0], scr[1]], axis=1)


def _attn_combine(oml, ts=256):
    S = oml[0][0].shape[0]

    def body(*refs):
        ins, (att_ref, out_ref, lse_ref, scr) = refs[:9], refs[9:]
        o, m, l = [[_natural(ins[3 * g + k], d, scr, ts) for g, d in enumerate(DILS)] for k in range(3)]
        mx = jnp.maximum(jnp.maximum(m[0], m[1]), m[2])
        w = [jnp.exp(m[g] - mx) for g in range(3)]
        den = w[0] * l[0] + w[1] * l[1] + w[2] * l[2]
        out = (w[0] * o[0] + w[1] * o[1] + w[2] * o[2]) / den
        out_ref[...] = out
        att_ref[...] = out.astype(MXU)
        lse_ref[...] = mx + jnp.log(den)

    in_specs = [pl.BlockSpec((d, ts // d, ATT_O), lambda i: (0, i, 0)) for d in DILS for _ in range(3)]
    args = [a.reshape(d, S // d, ATT_O) for d, grp in zip(DILS, oml) for a in grp]
    blk = pl.BlockSpec((ts, ATT_O), lambda i: (i, 0))
    return pl.pallas_call(body, grid=(S // ts,), in_specs=in_specs, out_specs=[blk, blk, blk],
                          out_shape=[jax.ShapeDtypeStruct((S, ATT_O), MXU), jax.ShapeDtypeStruct((S, ATT_O), F32),
                                     jax.ShapeDtypeStruct((S, ATT_O), F32)],
                          scratch_shapes=[pltpu.VMEM((2, ts, LANES), F32)], compiler_params=_params(("arbitrary",), 32),
                          name="attn_combine")(*args)


def _attn_bwd_prep(datt, o, lse, ts=256):
    S = datt.shape[0]

    def body(da0, da1, o_ref, l0, l1, *rest):
        outs, dl = rest[:9], rest[9]
        prod = jnp.concatenate([da0[...], da1[...]], axis=1) * o_ref[...]
        delta = jnp.zeros((ts, ATT_O), F32)
        for hm in _head_masks((ts, ATT_O)):
            delta = jnp.where(hm, jnp.sum(_only(hm, prod), axis=1, keepdims=True), delta)
        dl[0] = delta[:, 0:LANES]
        dl[1] = delta[:, LANES:2 * LANES]
        for g, d in enumerate(DILS):
            n = ts // d
            for r in range(d):
                rows = _rows_of(r, n, d)
                outs[g][r] = jnp.concatenate([da0[rows, :], da1[rows, :]], axis=1).astype(MXU)
                outs[3 + g][r] = jnp.concatenate([dl.at[0][rows, :], dl.at[1][rows, :]], axis=1)
                outs[6 + g][r] = jnp.concatenate([l0[rows, :], l1[rows, :]], axis=1)

    half = lambda j: pl.BlockSpec((ts, LANES), lambda i: (i, j))
    out_specs = [pl.BlockSpec((d, ts // d, ATT_O), lambda i: (0, i, 0)) for _ in range(3) for d in DILS]
    out_shape = [jax.ShapeDtypeStruct((d, S // d, ATT_O), dt) for dt in (MXU, F32, F32) for d in DILS]
    res = pl.pallas_call(body, grid=(S // ts,), in_specs=[half(0), half(1), pl.BlockSpec((ts, ATT_O), lambda i: (i, 0)), half(0), half(1)],
                         out_specs=out_specs, out_shape=out_shape, scratch_shapes=[pltpu.VMEM((2, ts, LANES), F32)],
                         compiler_params=_params(("arbitrary",), 32), name="attn_bwd_prep")(datt, datt, o, lse, lse)
    return [[res[k * 3 + g].reshape(S, ATT_O) for g in range(3)] for k in range(3)]


def _head_col(x, h):
    return x[:, h * 64:h * 64 + 1]


def _attn_dq(g, q, k, v, do, delta, lse):
    S = q.shape[0]
    nb = S // QB
    nblk = nb // DILS[g]

    def body(q_ref, kc_ref, kp_ref, vc_ref, vp_ref, do_ref, dl_ref, lse_ref, dq_ref):
        hms = _head_masks((2 * QB, ATT_O))
        for j in range(ASUB):
            ok = _band_mask_keys(((pl.program_id(0) * ASUB + j) & (nblk - 1)) > 0)
            k2 = jnp.concatenate([kp_ref[...] if j == 0 else _sub(kc_ref, j - 1), _sub(kc_ref, j)], axis=0)
            v2 = jnp.concatenate([vp_ref[...] if j == 0 else _sub(vc_ref, j - 1), _sub(vc_ref, j)], axis=0)
            qv, dov, dl, lse_v = _sub(q_ref, j), _sub(do_ref, j), _sub(dl_ref, j), _sub(lse_ref, j)
            dq = jnp.zeros((QB, ATT_O), F32)
            for h, hm in enumerate(hms):
                kh = _only(hm, k2)
                p = jnp.where(ok, jnp.exp(_dot_nt(qv, kh) * ATT_SCALE - _head_col(lse_v, h)), 0.0)
                ds = p * (_dot_nt(dov, _only(hm, v2)) - _head_col(dl, h))
                dq = dq + _dot(ds.astype(MXU), kh)
            dq_ref[j * QB:(j + 1) * QB] = dq * ATT_SCALE

    return pl.pallas_call(body, grid=(nb // ASUB,), in_specs=[_BIG, _BIG, _PREV, _BIG, _PREV, _BIG, _BIG, _BIG], out_specs=_BIG,
                          out_shape=jax.ShapeDtypeStruct((S, ATT_O), F32),
                          compiler_params=_params(("arbitrary",), 32), name=f"attn_dq_{g}")(q, k, k, v, v, do, delta, lse)


def _attn_dkv(g, q, k, v, do, delta, lse):
    S = q.shape[0]
    nb = S // QB
    nblk = nb // DILS[g]

    def body(k_ref, v_ref, qc_ref, qn_ref, doc_ref, don_ref, dlc_ref, dln_ref, lc_ref, ln_ref, dk_ref, dv_ref):
        hms = _head_masks((2 * QB, ATT_O))

        def both(cur_ref, nxt_ref, j):
            return jnp.concatenate([_sub(cur_ref, j), nxt_ref[...] if j == ASUB - 1 else _sub(cur_ref, j + 1)], axis=0)

        for j in range(ASUB):
            ok = _band_mask_queries(((pl.program_id(0) * ASUB + j + 1) & (nblk - 1)) > 0)
            q2, do2, dl2, lse2 = both(qc_ref, qn_ref, j), both(doc_ref, don_ref, j), both(dlc_ref, dln_ref, j), both(lc_ref, ln_ref, j)
            kv, vv = _sub(k_ref, j), _sub(v_ref, j)
            dk = jnp.zeros((QB, ATT_O), F32)
            dv = jnp.zeros((QB, ATT_O), F32)
            for h, hm in enumerate(hms):
                qh, doh = _only(hm, q2), _only(hm, do2)
                p = jnp.where(ok, jnp.exp(_dot_nt(qh, kv) * ATT_SCALE - _head_col(lse2, h)), 0.0)
                ds = p * (_dot_nt(doh, vv) - _head_col(dl2, h))
                dv = dv + _dot_tn(p.astype(MXU), doh)
                dk = dk + _dot_tn(ds.astype(MXU), qh)
            dk_ref[j * QB:(j + 1) * QB] = dk * ATT_SCALE
            dv_ref[j * QB:(j + 1) * QB] = dv

    nxt = pl.BlockSpec((QB, ATT_O), lambda b: (jnp.minimum((b + 1) * ASUB, nb - 1), 0))
    shp = jax.ShapeDtypeStruct((S, ATT_O), F32)
    return pl.pallas_call(body, grid=(nb // ASUB,), in_specs=[_BIG, _BIG, _BIG, nxt, _BIG, nxt, _BIG, nxt, _BIG, nxt], out_specs=[_BIG, _BIG],
                          out_shape=[shp, shp], compiler_params=_params(("arbitrary",), 32),
                          name=f"attn_dkv_{g}")(k, v, q, q, do, do, delta, delta, lse, lse)


def _merge_fwd(x0, u, a2, yb, att, wa, wb, wc, w_out, g_post, ts=256):
    S = x0.shape[0]

    def body(i, g, x_ref, gate_ref, a2_ref, yb_ref, att_ref, wa_ref, wb_ref, wc_ref, wo_ref, gp_ref, mg_ref, y_ref, xo_ref):
        merged = jax.nn.sigmoid(gate_ref[:, 0:D]) * _dot_nt(a2_ref[...], wa_ref[...])
        merged = merged + jax.nn.sigmoid(gate_ref[:, D:2 * D]) * _dot_nt(yb_ref[...], wb_ref[...])
        merged = merged + jax.nn.sigmoid(gate_ref[:, 2 * D:3 * D]) * _dot_nt(att_ref[...], wc_ref[...])
        mb = merged.astype(MXU)
        mg_ref[...] = mb
        y = _dot(mb, wo_ref[...])
        y_ref[...] = y
        xo_ref[...] = x_ref[...] + _rms(y, gp_ref[...])[0]

    ins = [("t", x0, D, 0), ("t", u, GATE_W, 0), ("t", a2, POOLW, 0), ("t", yb, POOLW, 0), ("t", att, ATT_O, 0),
           ("w", wa), ("w", wb), ("w", wc), ("w", w_out), ("w", g_post)]
    return _rows_call("merge_fwd", body, S, ts, ins, [("t", D, MXU), ("t", D, F32), ("t", D, F32)])


def _merge_bwd(dx, y1, u, a2, yb, att, wa, wb, wc, w_out, g_post, ts=256):
    S = dx.shape[0]

    def body(i, g, dx_ref, y_ref, gate_ref, a2_ref, yb_ref, att_ref, wa_ref, wb_ref, wc_ref, wo_ref, gp_ref,
             dy_ref, dgate_ref, dbra_ref, dbrb_ref, dbrc_ref, da2_ref, dyb_ref, datt_ref, dgp_ref):
        dxv, y = dx_ref[...], y_ref[...]
        dy, r = _rms_bwd(dxv * gp_ref[...], y)
        _acc(dgp_ref, g, jnp.sum(dxv * (y * r), axis=0, keepdims=True))
        dyb16 = dy.astype(MXU)
        dy_ref[...] = dyb16
        dm = _dot_nt(dyb16, wo_ref[...])
        for n, (src, w_ref, dbr_ref, din_ref) in enumerate(((a2_ref, wa_ref, dbra_ref, da2_ref), (yb_ref, wb_ref, dbrb_ref, dyb_ref),
                                                           (att_ref, wc_ref, dbrc_ref, datt_ref))):
            gt = jax.nn.sigmoid(gate_ref[:, n * D:(n + 1) * D])
            br = _dot_nt(src[...], w_ref[...])
            dgate_ref[:, n * D:(n + 1) * D] = (dm * br * gt * (1.0 - gt)).astype(dgate_ref.dtype)
            dbr = (dm * gt).astype(MXU)
            dbr_ref[...] = dbr
            din_ref[...] = _dot(dbr, w_ref[...])

    ins = [("t", dx, D, 0), ("t", y1, D, 0), ("t", u, GATE_W, 0), ("t", a2, POOLW, 0), ("t", yb, POOLW, 0), ("t", att, ATT_O, 0),
           ("w", wa), ("w", wb), ("w", wc), ("w", w_out), ("w", g_post)]
    outs = [("t", D, MXU), ("t", GATE_W, MXU), ("t", D, MXU), ("t", D, MXU), ("t", D, MXU), ("t", POOLW, F32), ("t", POOLW, F32),
            ("t", ATT_O, F32), ("a", (1, D), F32)]
    return _rows_call("merge_bwd", body, S, ts, ins, outs)


def _prenorm_bwd(name, dx_res, du, wt, x, g_pre, ts=256, lead=0):
    S = x.shape[0]
    N = du.shape[1]

    def body(i, g, dx_ref, du_ref, wt_ref, x_ref, g_ref, o_ref, dg_ref):
        if lead:
            dhv = _dot(du_ref[:, 0:lead], wt_ref[N - lead:N, :]) + _dot(du_ref[:, lead:N], wt_ref[0:N - lead, :])
        else:
            dhv = _dot(du_ref[...], wt_ref[...])
        xv = x_ref[...]
        dxn, r = _rms_bwd(dhv * g_ref[...], xv)
        o_ref[...] = dx_ref[...] + dxn
        _acc(dg_ref, g, jnp.sum(dhv * (xv * r), axis=0, keepdims=True))

    ins = [("t", dx_res, D, 0), ("t", du, N, 0), ("w", wt), ("t", x, D, 0), ("w", g_pre)]
    return _rows_call(name, body, S, ts, ins, [("t", D, F32), ("a", (1, D), F32)], vmem_mb=52)


def _mem_heads(qm, kv_ref):
    out = []
    for h in range(4):
        q = qm[:, h * 128:(h + 1) * 128].astype(MXU)
        k = kv_ref[:, h * 128:(h + 1) * 128]
        v = kv_ref[:, MEM_W + h * 128:MEM_W + (h + 1) * 128]
        sc = _dot_nt(q, k) * MEM_SCALE
        e = jnp.exp(sc - jnp.max(sc, axis=1, keepdims=True))
        out.append((e / jnp.sum(e, axis=1, keepdims=True), q, k, v))
    return out


def _mem_fwd(x1, kv, g_pre, w_mq, w_mo, g_post, ts=256):
    S = x1.shape[0]

    def body(i, g, x_ref, kv_ref, gq_ref, wq_ref, wo_ref, gp_ref, om_ref, h_ref, y_ref, xo_ref):
        x = x_ref[...]
        hb = _rms(x, gq_ref[...])[0].astype(MXU)
        h_ref[...] = hb
        qm = _dot(hb, wq_ref[...])
        om = jnp.concatenate([_dot(p.astype(MXU), v) for p, _, _, v in _mem_heads(qm, kv_ref)], axis=1).astype(MXU)
        om_ref[...] = om
        y = _dot_nt(om, wo_ref[...])
        y_ref[...] = y
        xo_ref[...] = x + _rms(y, gp_ref[...])[0]

    ins = [("t", x1, D, 0), ("w", kv), ("w", g_pre), ("w", w_mq), ("w", w_mo), ("w", g_post)]
    return _rows_call("mem_fwd", body, S, ts, ins, [("t", MEM_W, MXU), ("t", D, MXU), ("t", D, F32), ("t", D, F32)])


def _mem_bwd(dx2, ym, x1, kv, g_pre, w_mq, w_mo, g_post, ts=256):
    S = x1.shape[0]

    def body(i, g, dx_ref, y_ref, x_ref, kv_ref, gq_ref, wq_ref, wo_ref, gp_ref, dy_ref, dq_ref, dxo_ref, dgp_ref, dgq_ref, dkv_ref):
        dxv, y, x = dx_ref[...], y_ref[...], x_ref[...]
        dy, r = _rms_bwd(dxv * gp_ref[...], y)
        _acc(dgp_ref, g, jnp.sum(dxv * (y * r), axis=0, keepdims=True))
        dyb = dy.astype(MXU)
        dy_ref[...] = dyb
        dom = _dot(dyb, wo_ref[...])
        h, r1 = _rms(x, gq_ref[...])
        qm = _dot(h.astype(MXU), wq_ref[...])
        dqs = []

        @pl.when(g == 0)
        def _():
            dkv_ref[...] = jnp.zeros_like(dkv_ref)

        for hh, (p, q, k, v) in enumerate(_mem_heads(qm, kv_ref)):
            doh = dom[:, hh * 128:(hh + 1) * 128].astype(MXU)
            dp = _dot_nt(doh, v)
            dsc = (p * (dp - jnp.sum(dp * p, axis=1, keepdims=True)) * MEM_SCALE).astype(MXU)
            dqs.append(_dot(dsc, k))
            dkv_ref[:, hh * 128:(hh + 1) * 128] += _dot_tn(dsc, q)
            dkv_ref[:, MEM_W + hh * 128:MEM_W + (hh + 1) * 128] += _dot_tn(p.astype(MXU), doh)
        dq = jnp.concatenate(dqs, axis=1).astype(MXU)
        dq_ref[...] = dq
        dh = _dot_nt(dq, wq_ref[...])
        _acc(dgq_ref, g, jnp.sum(dh * (x * r1), axis=0, keepdims=True))
        dxo_ref[...] = dxv + _rms_bwd(dh * gq_ref[...], x)[0]

    ins = [("t", dx2, D, 0), ("t", ym, D, 0), ("t", x1, D, 0), ("w", kv), ("w", g_pre), ("w", w_mq), ("w", w_mo), ("w", g_post)]
    outs = [("t", D, MXU), ("t", MEM_W, MXU), ("t", D, F32), ("a", (1, D), F32), ("a", (1, D), F32), ("a", (256, D), F32)]
    return _rows_call("mem_bwd", body, S, ts, ins, outs)


def _gain_grad(name, dn, x):
    n = x.shape[0]

    def body(i, g, dn_ref, x_ref, o_ref):
        xv = x_ref[...]
        r = lax.rsqrt(jnp.mean(xv * xv, axis=-1, keepdims=True) + EPS)
        o_ref[...] = jnp.sum(dn_ref[...] * (xv * r), axis=0, keepdims=True)

    return _rows_call(name, body, n, n, [("t", dn, D, 0), ("t", x, D, 0)], [("a", (1, D), F32)])[0]


def _ffn_fwd(x2, u3, conv_f, w_down, g_post, ts=256):
    S = x2.shape[0]

    def body(i, g, x_ref, ua_ref, ub_ref, cw_ref, wd_ref, gp_ref, act_ref, y_ref, xo_ref, cu):
        @pl.when(g == 0)
        def _():
            cu[...] = jnp.zeros_like(cu)

        ua = ua_ref[...]
        c, _, _ = _conv3(ua, cu[...], cw_ref[...])
        act = (c * jax.nn.sigmoid(c) * ub_ref[...]).astype(MXU)
        act_ref[...] = act
        y = _dot(act, wd_ref[...])
        y_ref[...] = y
        xo_ref[...] = x_ref[...] + _rms(y, gp_ref[...])[0]
        cu[...] = ua[ts - 8:]

    ins = [("t", x2, D, 0), ("t", u3, D_FF, 0), ("t", u3, D_FF, 1), ("w", conv_f), ("w", w_down), ("w", g_post)]
    return _rows_call("ffn_fwd", body, S, ts, ins, [("t", D_FF, MXU), ("t", D, F32), ("t", D, F32)],
                      scratch=[pltpu.VMEM((8, D_FF), F32)], vmem_mb=56)


def _ffn_bwd(dx3, y3, u3, conv_f, w_down, g_post, ts=128):
    S = dx3.shape[0]

    def body(i, g, dx_ref, y_ref, ua_ref, ub_ref, uap_ref, cw_ref, wd_ref, gp_ref, dy_ref, du_ref, dgp_ref, dcw_ref, cdc):
        @pl.when(g == 0)
        def _():
            cdc[...] = jnp.zeros_like(cdc)

        dxv, y = dx_ref[...], y_ref[...]
        dy, r = _rms_bwd(dxv * gp_ref[...], y)
        _acc(dgp_ref, g, jnp.sum(dxv * (y * r), axis=0, keepdims=True))
        dyb = dy.astype(MXU)
        dy_ref[...] = dyb
        dact = _dot_nt(dyb, wd_ref[...])
        ua, w = ua_ref[...], cw_ref[...]
        c, u1, u2 = _conv3(ua, uap_ref[...] * (i > 0).astype(F32), w)
        sg = jax.nn.sigmoid(c)
        du_ref[:, D_FF:2 * D_FF] = (dact * (c * sg)).astype(du_ref.dtype)
        dc = dact * ub_ref[...] * (sg * (1.0 + c * (1.0 - sg)))
        du_ref[:, 0:D_FF] = _conv3_t(dc, cdc[...], w).astype(du_ref.dtype)
        dw = jnp.concatenate([jnp.sum(dc * u2, axis=0, keepdims=True), jnp.sum(dc * u1, axis=0, keepdims=True),
                              jnp.sum(dc * ua, axis=0, keepdims=True)], axis=0)
        _acc(dcw_ref, g, dw)
        cdc[...] = dc[:8]

    ins = [("t", dx3, D, 0), ("t", y3, D, 0), ("t", u3, D_FF, 0), ("t", u3, D_FF, 1), ("h", u3, 8, D_FF, 0), ("w", conv_f),
           ("w", w_down), ("w", g_post)]
    outs = [("t", D, MXU), ("t", 2 * D_FF, MXU), ("a", (1, D), F32), ("a", (3, D_FF), F32)]
    return _rows_call("ffn_bwd", body, S, ts, ins, outs, scratch=[pltpu.VMEM((8, D_FF), F32)], reverse=True, vmem_mb=56)


def _loss_head(x, target, ts=512):
    S = x.shape[0]

    def body(i, g, x_ref, t_ref, dx_ref, acc_ref):
        diff = x_ref[...] - t_ref[...]
        dx_ref[...] = diff * (1.0 / D)
        col = jnp.sum(diff * diff, axis=0, keepdims=True)
        part = col[:, 0:LANES]
        for j in range(1, D // LANES):
            part = part + col[:, j * LANES:(j + 1) * LANES]
        row = lax.broadcasted_iota(jnp.int32, (8, LANES), 0)
        _acc(acc_ref, g, jnp.where(row == 0, jnp.broadcast_to(part, (8, LANES)), 0.0))

    return _rows_call("loss_head", body, S, ts, [("t", x, D, 0), ("t", target, D, 0)], [("t", D, F32), ("a", (8, LANES), F32)])


def _layer_weights(big, small, l):
    w_in = big['w_in']
    pool_w = small['pool_w'][l].astype(MXU)
    wblk = jnp.zeros((POOLW, POOLW), MXU)
    for g in range(4):
        wblk = lax.dynamic_update_slice(wblk, pool_w[g], (g * 96, g * 96))
    vec = lambda n: small[n][l].reshape(1, -1)
    return dict(
        w_in=w_in,
        wblk=wblk, pool_scale=vec('pool_scale'), conv_b=small['conv_b_w'][l], wa=big['w_branch_a'], wb=big['w_branch_b'],
        wc=big['w_branch_c'], w_out=big['w_out'], w_mq=big['w_mq'], w_mkv=big['w_mkv'], w_mo=big['w_mo'],
        w_up=big['w_up'], conv_f=small['conv_ffn_w'][l], w_down=big['w_down'],
        g_mix_pre=vec('norm_mix_pre'), g_mix_post=vec('norm_mix_post'), g_mem_pre=vec('norm_mem_pre'),
        g_mem_post=vec('norm_mem_post'), g_memkv=vec('norm_memkv'), g_ffn_pre=vec('norm_ffn_pre'), g_ffn_post=vec('norm_ffn_post'))


def _layer_fwd(x0, mem, W, ctab, stab):
    sv = dict(x0=x0)
    sv['u'], sv['h1'] = _norm_mm("in_proj", x0, W['g_mix_pre'], W['w_in'], ts=1024, tn=IN_TILE, wt=True, rot=IN_ROT)
    sv['a2'], sv['yb'] = _poolconv_fwd(sv['u'], W['wblk'], W['pool_scale'], W['conv_b'])
    sv['qkv'] = q3, k3, v3 = _rope_perm(sv['u'], ctab, stab)
    sv['att'], sv['o'], sv['lse'] = _attn_combine([_attn_fwd(g, q3[g], k3[g], v3[g]) for g in range(3)])
    sv['merged'], sv['y1'], sv['x1'] = _merge_fwd(x0, sv['u'], sv['a2'], sv['yb'], sv['att'], W['wa'], W['wb'], W['wc'],
                                                  W['w_out'], W['g_mix_post'])
    sv['kv'], sv['memn'] = _norm_mm("mem_kv", mem, W['g_memkv'], W['w_mkv'], ts=256, tn=D, out_dtype=MXU)
    sv['om'], sv['h2'], sv['ym'], sv['x2'] = _mem_fwd(sv['x1'], sv['kv'], W['g_mem_pre'], W['w_mq'], W['w_mo'], W['g_mem_post'])
    sv['u3'], sv['h3'] = _norm_mm("up_proj", sv['x2'], W['g_ffn_pre'], W['w_up'], ts=1024, tn=1408, wt=True)
    sv['act'], sv['y3'], x3 = _ffn_fwd(sv['x2'], sv['u3'], W['conv_f'], W['w_down'], W['g_ffn_post'])
    return x3, sv


def _layer_bwd(dx3, mem, W, sv, ctab, stab):
    dx1, g = _layer_bwd_late(dx3, mem, W, sv)
    dx0, g_mix = _layer_bwd_mix(dx1, W, sv, ctab, stab)
    return dx0, {**g, **g_mix}


def _layer_bwd_late(dx3, mem, W, sv):
    g = {}
    dy3, du3, g['norm_ffn_post'], g['conv_ffn_w'] = _ffn_bwd(dx3, sv['y3'], sv['u3'], W['conv_f'], W['w_down'], W['g_ffn_post'])
    g['w_down'] = _mm_tn("dw_down", sv['act'], dy3, cap_k=256)
    g['w_up'] = _mm_tn("dw_up", du3, sv['h3'])
    dx2, g['norm_ffn_pre'] = _prenorm_bwd("ffn_pre_bwd", dx3, du3, W['w_up'], sv['x2'], W['g_ffn_pre'])
    dym, dqm, dx1, g['norm_mem_post'], g['norm_mem_pre'], dkv = _mem_bwd(dx2, sv['ym'], sv['x1'], sv['kv'], W['g_mem_pre'],
                                                                       W['w_mq'], W['w_mo'], W['g_mem_post'])
    g['w_mo'] = _mm_tn("dw_mo", dym, sv['om'])
    g['w_mq'] = _mm_tn("dw_mq", sv['h2'], dqm)
    dkvb = dkv.astype(MXU)
    g['w_mkv'] = _mm_tn("dw_mkv", sv['memn'], dkvb)
    g['norm_memkv'] = _gain_grad("memkv_gain", _mm_nt("d_memn", dkvb, W['w_mkv'], ts=256, tn=512), mem)
    return dx1, g


def _layer_bwd_mix(dx1, W, sv, ctab, stab):
    g = {}
    dy1, dgate, dbra, dbrb, dbrc, da2, dyb, datt, g['norm_mix_post'] = _merge_bwd(
        dx1, sv['y1'], sv['u'], sv['a2'], sv['yb'], sv['att'], W['wa'], W['wb'], W['wc'], W['w_out'], W['g_mix_post'])
    g['w_out'] = _mm_tn("dw_out", sv['merged'], dy1)
    g['w_branch_a'] = _mm_tn("dw_a", dbra, sv['a2'])
    g['w_branch_b'] = _mm_tn("dw_b", dbrb, sv['yb'])
    g['w_branch_c'] = _mm_tn("dw_c", dbrc, sv['att'])
    dabc, g['pool_scale'], dwblk, g['conv_b_w'] = _poolconv_bwd(sv['u'], da2, dyb, W['wblk'], W['pool_scale'], W['conv_b'])
    g['pool_w'] = jnp.stack([dwblk[k * 96:(k + 1) * 96, k * 96:(k + 1) * 96] for k in range(4)])
    q3, k3, v3 = sv['qkv']
    do3, dl3, lse3 = _attn_bwd_prep(datt, sv['o'], sv['lse'])
    dq3 = [_attn_dq(i, q3[i], k3[i], v3[i], do3[i], dl3[i], lse3[i]) for i in range(3)]
    dkv3 = [_attn_dkv(i, q3[i], k3[i], v3[i], do3[i], dl3[i], lse3[i]) for i in range(3)]
    dqkv = _rope_unperm_bwd([dq3, [a for a, _ in dkv3], [b for _, b in dkv3]], ctab, stab)
    du = jnp.concatenate([dgate, dabc, dqkv], axis=1)
    g['w_in'] = _mm_tn("dw_in", du, sv['h1'], cap_k=IN_TILE, rot=IN_ROT)
    dx0, g['norm_mix_pre'] = _prenorm_bwd("mix_pre_bwd", dx1, du, W['w_in'], sv['x0'], W['g_mix_pre'], lead=GATE_W)
    return dx0, g


def _local_step(x, mem, positions, target, big, small):
    ctab, stab = _rope_tables(positions)
    Ws = [_layer_weights(big[l], small, l) for l in range(DEPTH)]
    saved = []
    for l in range(DEPTH):
        x, sv = _layer_fwd(x, mem, Ws[l], ctab, stab)
        saved.append(sv)
    dx, acc = _loss_head(x, target)
    loss = jnp.sum(acc) * (0.5 / D)
    grads = [None] * DEPTH
    for l in reversed(range(DEPTH)):
        dx, grads[l] = _layer_bwd(dx, mem, Ws[l], saved[l], ctab, stab)
    return loss, dx, grads


_HBM = pl.BlockSpec(memory_space=pl.ANY)
MESH_ID = pl.DeviceIdType.MESH


def _all_gather(name, xs):
    n = len(xs)

    def body(*refs):
        x_refs, out_refs = refs[:n], refs[n:2 * n]
        send_sems, recv_sems, local_sems = refs[2 * n:]
        x, y, c = lax.axis_index("x"), lax.axis_index("y"), lax.axis_index("c")
        me, sibling = (x, y, c), (x, y, 1 - c)
        chips = [(1 - x, y), (x, 1 - y), (1 - x, 1 - y)]

        def slot(a, p):
            return out_refs[a].at[4 * p[0] + 2 * p[1] + p[2]]

        def copy(a, k, block, to, src=None):
            return pltpu.make_async_remote_copy(src_ref=slot(a, block) if src is None else src, dst_ref=slot(a, block),
                                                send_sem=send_sems.at[a, k], recv_sem=recv_sems.at[a, k], device_id=to,
                                                device_id_type=MESH_ID)

        started = []
        for a in range(n):
            mine = pltpu.make_async_copy(x_refs[a], slot(a, me), local_sems.at[a])
            mine.start()
            started.append(mine)
        first = []
        for a in range(n):
            first.append(copy(a, 0, me, sibling, src=x_refs[a]))
            first += [copy(a, 1 + j, me, (*chip, c), src=x_refs[a]) for j, chip in enumerate(chips)]
        for cp in first:
            cp.start()
        passed = []
        for j, chip in enumerate(chips):
            for a in range(n):
                copy(a, 1 + j, (*chip, c), me).wait_recv()
                fw = copy(a, 4 + j, (*chip, c), sibling)
                fw.start()
                passed.append(fw)
        for a in range(n):
            copy(a, 0, sibling, me).wait_recv()
            for j, chip in enumerate(chips):
                copy(a, 4 + j, (*chip, 1 - c), me).wait_recv()
        for cp in first + passed:
            cp.wait_send()
        for mine in started:
            mine.wait()

    return pl.pallas_call(
        body, out_shape=[jax.ShapeDtypeStruct((N_DEV,) + x.shape, x.dtype) for x in xs], in_specs=[_HBM] * n, out_specs=[_HBM] * n,
        scratch_shapes=[pltpu.SemaphoreType.DMA((n, 7)), pltpu.SemaphoreType.DMA((n, 7)), pltpu.SemaphoreType.DMA((n,))],
        name=name)(*xs)


def _exchange(name, gs):
    n = len(gs)

    def body(*refs):
        g_refs, out_refs = refs[:n], refs[n:2 * n]
        send_sems, recv_sems, local_sems = refs[2 * n:]
        x, y, c = lax.axis_index("x"), lax.axis_index("y"), lax.axis_index("c")
        me = 4 * x + 2 * y + c
        copies = []
        for a in range(n):
            mine = pltpu.make_async_copy(g_refs[a].at[me], out_refs[a].at[me], local_sems.at[a])
            mine.start()
            copies.append(mine)
        for r in range(1, N_DEV):
            px, py, pc = x ^ ((r >> 2) & 1), y ^ ((r >> 1) & 1), c ^ (r & 1)
            for a in range(n):
                cp = pltpu.make_async_remote_copy(src_ref=g_refs[a].at[4 * px + 2 * py + pc], dst_ref=out_refs[a].at[me],
                                                  send_sem=send_sems.at[a, r - 1], recv_sem=recv_sems.at[a, r - 1],
                                                  device_id=(px, py, pc), device_id_type=MESH_ID)
                cp.start()
                copies.append(cp)
        for cp in copies:
            cp.wait()

    return pl.pallas_call(
        body, out_shape=[jax.ShapeDtypeStruct(g.shape, g.dtype) for g in gs], in_specs=[_HBM] * n, out_specs=[_HBM] * n,
        scratch_shapes=[pltpu.SemaphoreType.DMA((n, N_DEV - 1)), pltpu.SemaphoreType.DMA((n, N_DEV - 1)), pltpu.SemaphoreType.DMA((n,))],
        name=name)(*gs)


_SEM = pl.BlockSpec(memory_space=pltpu.SEMAPHORE)
_IN_HBM = pl.BlockSpec(memory_space=pltpu.HBM)
_SIDE_EFFECT = pltpu.SideEffectType.DATAFLOW_SIDE_EFFECTING


def _push_copies(src_refs, land_refs, send_sems, recv_sems, per_peer):
    x, y, c = lax.axis_index("x"), lax.axis_index("y"), lax.axis_index("c")
    me = 4 * x + 2 * y + c
    copies = []
    for r in range(1, N_DEV):
        px, py, pc = x ^ ((r >> 2) & 1), y ^ ((r >> 1) & 1), c ^ (r & 1)
        for a, (s, d) in enumerate(zip(src_refs, land_refs)):
            k = a * (N_DEV - 1) + r - 1
            copies.append(pltpu.make_async_remote_copy(src_ref=s.at[4 * px + 2 * py + pc] if per_peer else s, dst_ref=d.at[me],
                                                       send_sem=send_sems.at[k], recv_sem=recv_sems.at[k],
                                                       device_id=(px, py, pc), device_id_type=MESH_ID))
    return copies


def _push_start(name, srcs, per_peer, after):
    n = len(srcs)
    lands = [lax.empty((N_DEV,) + (s.shape[1:] if per_peer else s.shape), s.dtype) for s in srcs]

    def body(*refs):
        for cp in _push_copies(refs[:n], refs[n:2 * n], refs[2 * n + 1], refs[2 * n + 2], per_peer):
            cp.start()
        refs[-1][...] = jnp.zeros_like(refs[-1])

    hbm = [pltpu.HBM(a.shape, a.dtype) for a in (*srcs, *lands)]
    sems = pltpu.SemaphoreType.DMA((n * (N_DEV - 1),))
    out = pl.pallas_call(
        body, name=name, out_shape=(sems, sems, *hbm, jax.ShapeDtypeStruct((8, LANES), F32)),
        in_specs=[_IN_HBM] * (2 * n) + [pl.BlockSpec(memory_space=pl.ANY)],
        out_specs=(_SEM, _SEM, *[_IN_HBM] * (2 * n), pl.BlockSpec(memory_space=pltpu.VMEM)),
        input_output_aliases={a: 2 + a for a in range(2 * n)},
        compiler_params=pltpu.CompilerParams(has_side_effects=_SIDE_EFFECT),
    )(*[pltpu.with_memory_space_constraint(a, pltpu.HBM) for a in (*srcs, *lands)], after)
    return out[0], out[1], out[2:2 + n], out[2 + n:2 + 2 * n], out[-1]


def _push_wait(name, started, per_peer, after):
    send_sems, recv_sems, srcs, lands, _ = started
    n = len(srcs)

    def body(*refs):
        for cp in _push_copies(refs[:n], refs[n:2 * n], refs[2 * n], refs[2 * n + 1], per_peer):
            cp.wait_send()
            cp.wait_recv()

    out = pl.pallas_call(
        body, name=name, out_shape=[pltpu.HBM(a.shape, a.dtype) for a in (*srcs, *lands)],
        in_specs=[_IN_HBM] * (2 * n) + [_SEM, _SEM, pl.BlockSpec(memory_space=pl.ANY)], out_specs=[_IN_HBM] * (2 * n),
        input_output_aliases={a: a for a in range(2 * n)},
        compiler_params=pltpu.CompilerParams(has_side_effects=_SIDE_EFFECT),
    )(*srcs, *lands, send_sems, recv_sems, after)
    return out[n:]


def _my_slot():
    return 4 * lax.axis_index("x") + 2 * lax.axis_index("y") + lax.axis_index("c")


def _row_tile(rows, cols, budget):
    if rows * cols * 4 <= budget or rows % 16:
        return rows
    best = 16
    for t in range(16, rows + 1, 16):
        if rows % t == 0 and t * cols * 4 <= budget:
            best = t
    return best


def _sum_slots(name, recv):
    _, R, C = recv.shape
    tr = _row_tile(R, C, 1 << 20)

    def body(r_ref, o_ref):
        g = r_ref[0].astype(F32)
        for k in range(1, N_DEV):
            g = g + r_ref[k].astype(F32)
        o_ref[...] = g

    return pl.pallas_call(body, grid=(R // tr,), in_specs=[pl.BlockSpec((N_DEV, tr, C), lambda i: (0, i, 0))],
                          out_specs=pl.BlockSpec((tr, C), lambda i: (i, 0)), out_shape=jax.ShapeDtypeStruct((R, C), F32),
                          compiler_params=_params(("arbitrary",), 32), name=name)(recv)


def _adamw(name, g, w, m, v):
    R, C = w.shape
    tr = _row_tile(R, C, 1 << 20)
    c1 = 1.0 - ADAM_B1 ** ADAM_STEP
    c2 = 1.0 - ADAM_B2 ** ADAM_STEP

    def body(g_ref, w_ref, m_ref, v_ref, d_ref, mo_ref, vo_ref):
        gv = g_ref[...]
        mn = ADAM_B1 * m_ref[...] + (1.0 - ADAM_B1) * gv
        vn = ADAM_B2 * v_ref[...] + (1.0 - ADAM_B2) * (gv * gv)
        mo_ref[...] = mn
        vo_ref[...] = vn
        d_ref[...] = -ADAM_LR * ((mn / c1) / (jnp.sqrt(vn / c2) + ADAM_EPS) + ADAM_WD * w_ref[...])

    blk = pl.BlockSpec((tr, C), lambda i: (i, 0))
    shp = jax.ShapeDtypeStruct((R, C), F32)
    return pl.pallas_call(body, grid=(R // tr,), in_specs=[blk, blk, blk, blk], out_specs=[blk, blk, blk], out_shape=[shp, shp, shp],
                          compiler_params=_params(("arbitrary",), 32), name=name)(g, w, m, v)


def _pad_flat(a, n):
    a = a.reshape(-1)
    return jnp.pad(a, (0, n - a.shape[0]))


def _seg(n):
    return -(-n // FLAT_ALIGN) * FLAT_ALIGN


def _to_blocks(full, axis):
    shp = full.shape
    return jnp.moveaxis(full.reshape(shp[:axis] + (N_DEV, shp[axis] // N_DEV) + shp[axis + 1:]), axis, 0)


def _from_blocks(blocks, axis):
    b = jnp.moveaxis(blocks, 0, axis)
    shp = b.shape
    return b.reshape(shp[:axis] + (shp[axis] * shp[axis + 1],) + shp[axis + 2:])


def _as_rows(shard, n):
    return shard.T if SHARD_AXIS[n] == 2 else shard


def _with_own(lands, own, me):
    return [lax.dynamic_update_slice(land, o[None], (me, 0, 0)) for land, o in zip(lands, own)]


def kernel(x, mem, positions, norm_mix_pre, norm_mix_post, w_in, pool_w, pool_scale, conv_b_w, w_branch_a, w_branch_b, w_branch_c, w_out, norm_mem_pre, norm_mem_post, norm_memkv, w_mq, w_mkv, w_mo, norm_ffn_pre, norm_ffn_post, w_up, conv_ffn_w, w_down, loss_target, m_norm_mix_pre, m_norm_mix_post, m_w_in, m_pool_w, m_pool_scale, m_conv_b_w, m_w_branch_a, m_w_branch_b, m_w_branch_c, m_w_out, m_norm_mem_pre, m_norm_mem_post, m_norm_memkv, m_w_mq, m_w_mkv, m_w_mo, m_norm_ffn_pre, m_norm_ffn_post, m_w_up, m_conv_ffn_w, m_w_down, v_norm_mix_pre, v_norm_mix_post, v_w_in, v_pool_w, v_pool_scale, v_conv_b_w, v_w_branch_a, v_w_branch_b, v_w_branch_c, v_w_out, v_norm_mem_pre, v_norm_mem_post, v_norm_memkv, v_w_mq, v_w_mkv, v_w_mo, v_norm_ffn_pre, v_norm_ffn_post, v_w_up, v_conv_ffn_w, v_w_down):
    w = dict(norm_mix_pre=norm_mix_pre, norm_mix_post=norm_mix_post, w_in=w_in, pool_w=pool_w, pool_scale=pool_scale, conv_b_w=conv_b_w, w_branch_a=w_branch_a, w_branch_b=w_branch_b, w_branch_c=w_branch_c, w_out=w_out, norm_mem_pre=norm_mem_pre, norm_mem_post=norm_mem_post, norm_memkv=norm_memkv, w_mq=w_mq, w_mkv=w_mkv, w_mo=w_mo, norm_ffn_pre=norm_ffn_pre, norm_ffn_post=norm_ffn_post, w_up=w_up, conv_ffn_w=conv_ffn_w, w_down=w_down)
    m = dict(norm_mix_pre=m_norm_mix_pre, norm_mix_post=m_norm_mix_post, w_in=m_w_in, pool_w=m_pool_w, pool_scale=m_pool_scale, conv_b_w=m_conv_b_w, w_branch_a=m_w_branch_a, w_branch_b=m_w_branch_b, w_branch_c=m_w_branch_c, w_out=m_w_out, norm_mem_pre=m_norm_mem_pre, norm_mem_post=m_norm_mem_post, norm_memkv=m_norm_memkv, w_mq=m_w_mq, w_mkv=m_w_mkv, w_mo=m_w_mo, norm_ffn_pre=m_norm_ffn_pre, norm_ffn_post=m_norm_ffn_post, w_up=m_w_up, conv_ffn_w=m_conv_ffn_w, w_down=m_w_down)
    v = dict(norm_mix_pre=v_norm_mix_pre, norm_mix_post=v_norm_mix_post, w_in=v_w_in, pool_w=v_pool_w, pool_scale=v_pool_scale, conv_b_w=v_conv_b_w, w_branch_a=v_w_branch_a, w_branch_b=v_w_branch_b, w_branch_c=v_w_branch_c, w_out=v_w_out, norm_mem_pre=v_norm_mem_pre, norm_mem_post=v_norm_mem_post, norm_memkv=v_norm_memkv, w_mq=v_w_mq, w_mkv=v_w_mkv, w_mo=v_w_mo, norm_ffn_pre=v_norm_ffn_pre, norm_ffn_post=v_norm_ffn_post, w_up=v_w_up, conv_ffn_w=v_conv_ffn_w, w_down=v_w_down)

    me = _my_slot()
    blocks = [[_as_rows(w[n][l], n).astype(MXU) for n in BIG] for l in range(DEPTH)]
    conv = jnp.concatenate([_pad_flat(w[n], _seg(w[n].size)) for n in F32_GATHERED]).reshape(-1, LANES)
    got0 = _all_gather("weights_all_gather_0", blocks[0] + [conv])
    conv_all = got0[-1].reshape(N_DEV, -1)
    small, off = {n: w[n] for n in WEIGHTS if n not in SHARD_AXIS}, 0
    for n in F32_GATHERED:
        small[n] = _from_blocks(conv_all[:, off:off + w[n].size].reshape((N_DEV,) + w[n].shape), 2)
        off += _seg(w[n].size)
    whole = lambda got: {n: o.reshape(-1, o.shape[-1]) for n, o in zip(BIG, got)}

    ctab, stab = _rope_tables(positions[0])
    push_w = _push_start("weights_push_start_1", blocks[1], False, got0[0])
    W0 = _layer_weights(whole(got0), small, 0)
    x1, sv0 = _layer_fwd(x[0], mem[0], dict(W0, g_mix_pre=W0['g_mix_pre'] + push_w[4][0, 0]), ctab, stab)
    W1 = _layer_weights(whole(_with_own(_push_wait("weights_push_wait_1", push_w, False, x1), blocks[1], me)), small, 1)
    x2, sv1 = _layer_fwd(x1, mem[0], W1, ctab, stab)
    dx, acc = _loss_head(x2, loss_target[0])
    loss = lax.psum(jnp.sum(acc) * (0.5 / D), MESH_AXES)
    grads = [None] * DEPTH
    dx, grads[1] = _layer_bwd(dx, mem[0], W1, sv1, ctab, stab)
    sent = [None, [grads[1][n].reshape(N_DEV, -1, grads[1][n].shape[-1]) for n in BIG]]
    push_g = _push_start("grads_push_start_1", sent[1], True, dx)
    dx, g_late = _layer_bwd_late(dx, mem[0], dict(W0, g_ffn_post=W0['g_ffn_post'] + push_g[4][0, 0]), sv0)
    sent_late = [g_late[n].reshape(N_DEV, -1, g_late[n].shape[-1]) for n in LATE_BIG]
    push_l = _push_start("grads_push_start_0", sent_late, True, dx)
    dx, g_mix = _layer_bwd_mix(dx, dict(W0, g_mix_post=W0['g_mix_post'] + push_l[4][0, 0]), sv0, ctab, stab)
    grads[0] = {**g_late, **g_mix}
    own = lambda s: [lax.dynamic_index_in_dim(a, me, 0, keepdims=False) for a in s]
    recv1 = _with_own(_push_wait("grads_push_wait_1", push_g, True, dx), own(sent[1]), me)
    recv_late = _with_own(_push_wait("grads_push_wait_0", push_l, True, dx), own(sent_late), me)
    mix_big = [n for n in BIG if n not in LATE_BIG]

    misc_names = [n for n in WEIGHTS if n not in BIG]
    stacked = {n: jnp.stack([grads[l][n].reshape(small[n].shape[1:]) for l in range(DEPTH)]) for n in misc_names}
    rows = [(_to_blocks(stacked[n], 2) if n in SHARD_AXIS else jnp.broadcast_to(stacked[n][None], (N_DEV,) + stacked[n].shape))
            for n in misc_names]
    segs = [_seg(w[n].size) for n in misc_names]
    misc = jnp.concatenate([jnp.pad(r.reshape(N_DEV, -1), ((0, 0), (0, s - r[0].size))) for r, s in zip(rows, segs)],
                           axis=1).reshape(N_DEV, -1, LANES)
    recv_mix = _exchange("grad_exchange_0", [g_mix[n].reshape(N_DEV, -1, g_mix[n].shape[-1]) for n in mix_big] + [misc])
    g_out, per_layer = {}, {}
    for l, names, recv in ((1, BIG, recv1), (0, LATE_BIG, recv_late), (0, mix_big, recv_mix)):
        for n, r in zip(names, recv):
            per_layer[n, l] = _as_rows(_sum_slots(f"sum_{n}_{l}", r), n)
    misc_sum = _sum_slots("sum_misc", recv_mix[-1]).reshape(-1)
    off = 0
    for n, s in zip(misc_names, segs):
        g_out[n] = misc_sum[off:off + w[n].size].reshape(w[n].shape)
        off += s
    for n in BIG:
        g_out[n] = jnp.stack([per_layer[n, l] for l in range(DEPTH)])

    res = [[], [], [], []]
    for n in WEIGHTS:
        shp = w[n].shape
        d, mn, vn = _adamw(f"adamw_{n}", *[a.reshape(-1, shp[-1]) for a in (g_out[n], w[n], m[n], v[n])])
        for k, a in enumerate((g_out[n], d, mn, vn)):
            res[k].append(a.reshape(shp))
    return (loss, dx[None], *res[0], *res[1], *res[2], *res[3])
```

```python
import jax
import jax.numpy as jnp
from jax import lax
from jax.experimental import pallas as pl
from jax.experimental.pallas import tpu as pltpu

F32 = jnp.float32
MXU = jnp.bfloat16
HI = lax.Precision.HIGHEST

D = 1024
DEPTH = 2
POOLW = 384
ATT_W = 768
ATT_O = 256
GATE_W = 3 * D
IN_W = 6912
IN_TILE = 768
IN_ROT = (IN_W - GATE_W) // IN_TILE
MEM_W = 512
D_FF = 2816
EPS = 1e-6
ROPE_THETA = 500000.0
QB = 128
DILS = (1, 4, 16)
NEG = -1e30
MEM_SCALE = 128 ** -0.5
ATT_SCALE = 0.125

ADAM_LR, ADAM_B1, ADAM_B2, ADAM_EPS, ADAM_WD, ADAM_STEP = 0.001, 0.9, 0.999, 1e-08, 0.01, 10

N_DEV = 8
MESH_AXES = ("x", "y", "c")
LANES = 128
FLAT_ALIGN = 2048
ROW_TILE = 1024

WEIGHTS = ['norm_mix_pre', 'norm_mix_post', 'w_in', 'pool_w', 'pool_scale', 'conv_b_w', 'w_branch_a', 'w_branch_b',
           'w_branch_c', 'w_out', 'norm_mem_pre', 'norm_mem_post', 'norm_memkv', 'w_mq', 'w_mkv', 'w_mo',
           'norm_ffn_pre', 'norm_ffn_post', 'w_up', 'conv_ffn_w', 'w_down']
SHARD_AXIS = {'w_in': 2, 'conv_b_w': 2, 'w_branch_a': 2, 'w_branch_b': 2, 'w_branch_c': 2, 'w_out': 1, 'w_mq': 1,
              'w_mkv': 1, 'w_mo': 2, 'w_up': 2, 'conv_ffn_w': 2, 'w_down': 1}
F32_GATHERED = ('conv_b_w', 'conv_ffn_w')
BIG = [n for n in WEIGHTS if n in SHARD_AXIS and n not in F32_GATHERED]
LATE_BIG = ['w_mq', 'w_mkv', 'w_mo', 'w_up', 'w_down']


VMEM_LIMIT_MB = 60


def _params(sem, vmem_mb):
    del vmem_mb
    return pltpu.CompilerParams(dimension_semantics=sem, vmem_limit_bytes=VMEM_LIMIT_MB << 20)


def _dot(a, b, prec=None):
    return lax.dot_general(a, b, (((1,), (0,)), ((), ())), preferred_element_type=F32, precision=prec)


def _dot_nt(a, b, prec=None):
    return lax.dot_general(a, b, (((1,), (1,)), ((), ())), preferred_element_type=F32, precision=prec)


def _dot_tn(a, b, prec=None):
    return lax.dot_general(a, b, (((0,), (0,)), ((), ())), preferred_element_type=F32, precision=prec)


def _tile(n, cap):
    if n <= cap:
        return n
    best = None
    for t in range(LANES, cap + 1, LANES):
        if n % t == 0:
            best = t
    assert best is not None, (n, cap)
    return best


def _rms(x, g):
    r = lax.rsqrt(jnp.mean(x * x, axis=-1, keepdims=True) + EPS)
    return x * r * g, r


def _rms_bwd(w, y):
    r = lax.rsqrt(jnp.mean(y * y, axis=-1, keepdims=True) + EPS)
    return r * w - y * (r * r * r) * jnp.mean(w * y, axis=-1, keepdims=True), r


def _rows_call(name, body, n_rows, ts, ins, outs, scratch=(), reverse=False, vmem_mb=48):
    nt = n_rows // ts
    assert nt * ts == n_rows

    def tile_of(g):
        return (nt - 1 - g) if reverse else g

    in_specs, args = [], []
    for op in ins:
        if op[0] == "t":
            _, a, cw, cb = op
            in_specs.append(pl.BlockSpec((ts, cw), lambda g, cb=cb: (tile_of(g), cb)))
        elif op[0] == "h":
            _, a, hr, cw, cb = op
            in_specs.append(pl.BlockSpec((hr, cw), lambda g, cb=cb, k=ts // hr: (jnp.maximum(tile_of(g) * k - 1, 0), cb)))
        else:
            _, a = op
            in_specs.append(pl.BlockSpec(a.shape, lambda g, n=a.ndim: (0,) * n))
        args.append(a)
    out_specs, out_shape = [], []
    for op in outs:
        if op[0] == "t":
            _, cols, dt = op
            out_specs.append(pl.BlockSpec((ts, cols), lambda g: (tile_of(g), 0)))
            out_shape.append(jax.ShapeDtypeStruct((n_rows, cols), dt))
        else:
            _, shp, dt = op
            out_specs.append(pl.BlockSpec(shp, lambda g, n=len(shp): (0,) * n))
            out_shape.append(jax.ShapeDtypeStruct(shp, dt))

    def kern(*refs):
        g = pl.program_id(0)
        body(tile_of(g), g, *refs)

    return pl.pallas_call(kern, grid=(nt,), in_specs=in_specs, out_specs=out_specs, out_shape=out_shape,
                          scratch_shapes=list(scratch), compiler_params=_params(("arbitrary",), vmem_mb), name=name)(*args)


def _acc(ref, g, val):
    @pl.when(g == 0)
    def _():
        ref[...] = val

    @pl.when(g != 0)
    def _():
        ref[...] += val


def _norm_mm(name, x, g, w, ts, tn, out_dtype=F32, wt=False, rot=0):
    S, K = x.shape
    N = w.shape[0] if wt else w.shape[1]
    assert wt or not rot

    def body(x_ref, g_ref, w_ref, o_ref, h_ref, hs):
        @pl.when(pl.program_id(1) == 0)
        def _():
            h, _ = _rms(x_ref[...], g_ref[...])
            hs[...] = h.astype(MXU)
            h_ref[...] = h.astype(MXU)

        o_ref[...] = (_dot_nt if wt else _dot)(hs[...], w_ref[...]).astype(out_dtype)

    w_spec = pl.BlockSpec((tn, K), lambda i, j: ((j + rot) % (N // tn), 0)) if wt else pl.BlockSpec((K, tn), lambda i, j: (0, j))
    return pl.pallas_call(
        body, grid=(S // ts, N // tn),
        in_specs=[pl.BlockSpec((ts, K), lambda i, j: (i, 0)), pl.BlockSpec((1, K), lambda i, j: (0, 0)), w_spec],
        out_specs=[pl.BlockSpec((ts, tn), lambda i, j: (i, j)), pl.BlockSpec((ts, K), lambda i, j: (i, 0))],
        out_shape=[jax.ShapeDtypeStruct((S, N), out_dtype), jax.ShapeDtypeStruct((S, K), MXU)],
        scratch_shapes=[pltpu.VMEM((ts, K), MXU)],
        compiler_params=_params(("arbitrary", "arbitrary"), 48), name=name)(x, g, w)


def _mm_nt(name, a, b, ts, tn, out_dtype=F32):
    M, K = a.shape
    N = b.shape[0]

    def body(a_ref, b_ref, o_ref):
        o_ref[...] = _dot_nt(a_ref[...], b_ref[...]).astype(out_dtype)

    return pl.pallas_call(
        body, grid=(M // ts, N // tn),
        in_specs=[pl.BlockSpec((ts, K), lambda i, j: (i, 0)), pl.BlockSpec((tn, K), lambda i, j: (j, 0))],
        out_specs=pl.BlockSpec((ts, tn), lambda i, j: (i, j)), out_shape=jax.ShapeDtypeStruct((M, N), out_dtype),
        compiler_params=_params(("arbitrary", "arbitrary"), 48), name=name)(a, b)


def _mm_tn(name, a, b, cap_k=512, cap_n=1024, out_dtype=MXU, rot=0):
    S, K = a.shape
    N = b.shape[1]
    tk, tn = _tile(K, cap_k), _tile(N, cap_n)

    def body(a_ref, b_ref, o_ref):
        o_ref[...] = _dot_tn(a_ref[...], b_ref[...]).astype(out_dtype)

    return pl.pallas_call(
        body, grid=(K // tk, N // tn),
        in_specs=[pl.BlockSpec((S, tk), lambda i, j: (0, i)), pl.BlockSpec((S, tn), lambda i, j: (0, j))],
        out_specs=pl.BlockSpec((tk, tn), lambda i, j: ((i + rot) % (K // tk), j)), out_shape=jax.ShapeDtypeStruct((K, N), out_dtype),
        compiler_params=_params(("arbitrary", "arbitrary"), 48), name=name)(a, b)


def _pool_cols(shape):
    col = lax.broadcasted_iota(jnp.int32, shape, 1)
    return col < 96, col < 192, col < 288


def _pool_select(s2, s4, s8, s16):
    c1, c2, c3 = _pool_cols(s2.shape)
    return jnp.where(c1, s2, jnp.where(c2, s4, jnp.where(c3, s8, s16)))


def _pool_cnt(t0, ts):
    c1, c2, c3 = _pool_cols((ts, POOLW))
    win = jnp.where(c1, 2, jnp.where(c2, 4, jnp.where(c3, 8, 16)))
    t = t0 + lax.broadcasted_iota(jnp.int32, (ts, POOLW), 0)
    return jnp.minimum(t + 1, win).astype(F32)


def _pooled(a, prev, t0):
    ts = a.shape[0]
    ext = jnp.concatenate([prev, a], axis=0)
    s2 = ext + pltpu.roll(ext, 1, axis=0)
    s4 = s2 + pltpu.roll(s2, 2, axis=0)
    s8 = s4 + pltpu.roll(s4, 4, axis=0)
    s16 = s8 + pltpu.roll(s8, 8, axis=0)
    sums = _pool_select(s2, s4, s8, s16)[16:]
    return sums / _pool_cnt(t0, ts) - a


def _conv3(z, prev8, w):
    ext = jnp.concatenate([prev8, z], axis=0)
    z1 = pltpu.roll(ext, 1, axis=0)[8:]
    z2 = pltpu.roll(ext, 2, axis=0)[8:]
    return w[0:1] * z2 + w[1:2] * z1 + w[2:3] * z, z1, z2


def _conv3_t(dc, next8, w):
    ts = dc.shape[0]
    ext = jnp.concatenate([dc, next8], axis=0)
    n = ts + 8
    u1 = pltpu.roll(ext, n - 1, axis=0)[:ts]
    u2 = pltpu.roll(ext, n - 2, axis=0)[:ts]
    return w[2:3] * dc + w[1:2] * u1 + w[0:1] * u2


def _poolconv_fwd(u, wblk, pool_scale, conv_b, ts=256):
    S = u.shape[0]

    def body(i, g, a_ref, bx_ref, bb_ref, bc_ref, wblk_ref, ps_ref, cw_ref, a2_ref, yb_ref, ca, cz):
        @pl.when(g == 0)
        def _():
            ca[...] = jnp.zeros_like(ca)
            cz[...] = jnp.zeros_like(cz)

        a = a_ref[...]
        p = _pooled(a, ca[...], i * ts)
        mixed = _dot(p.astype(MXU), wblk_ref[...])
        a2_ref[...] = (mixed * ps_ref[...]).astype(MXU)
        z = bc_ref[...] * bx_ref[...]
        conv, _, _ = _conv3(z, cz[...], cw_ref[...])
        yb_ref[...] = (bb_ref[...] * conv).astype(MXU)
        ca[...] = a[ts - 16:]
        cz[...] = z[ts - 8:]

    ins = [("t", u, POOLW, 8), ("t", u, POOLW, 9), ("t", u, POOLW, 10), ("t", u, POOLW, 11), ("w", wblk), ("w", pool_scale),
           ("w", conv_b)]
    return _rows_call("poolconv_fwd", body, S, ts, ins, [("t", POOLW, MXU), ("t", POOLW, MXU)],
                      scratch=[pltpu.VMEM((16, POOLW), F32), pltpu.VMEM((8, POOLW), F32)])


def _poolconv_bwd(u, d_a2, d_yb, wblk, pool_scale, conv_b, ts=256):
    S = u.shape[0]

    def body(i, g, a_ref, bx_ref, bb_ref, bc_ref, ap_ref, bxp_ref, bcp_ref, da2_ref, dyb_ref, wblk_ref, ps_ref, cw_ref,
             o_ref, dps_ref, dwb_ref, dcw_ref, ce, cdz):
        @pl.when(g == 0)
        def _():
            ce[...] = jnp.zeros_like(ce)
            cdz[...] = jnp.zeros_like(cdz)

        first = (i > 0).astype(F32)
        a = a_ref[...]
        p = _pooled(a, ap_ref[...] * first, i * ts)
        pb = p.astype(MXU)
        mixed = _dot(pb, wblk_ref[...])
        da2 = da2_ref[...]
        dmixed = (da2 * ps_ref[...]).astype(MXU)
        dp = _dot_nt(dmixed, wblk_ref[...])
        _acc(dps_ref, g, jnp.sum(da2 * mixed, axis=0, keepdims=True))
        _acc(dwb_ref, g, _dot_tn(pb, dmixed))
        e = dp / _pool_cnt(i * ts, ts)
        ext = jnp.concatenate([e, ce[...]], axis=0)
        n = ts + 16
        f2 = ext + pltpu.roll(ext, n - 1, axis=0)
        f4 = f2 + pltpu.roll(f2, n - 2, axis=0)
        f8 = f4 + pltpu.roll(f4, n - 4, axis=0)
        f16 = f8 + pltpu.roll(f8, n - 8, axis=0)
        o_ref[:, 0:POOLW] = (_pool_select(f2, f4, f8, f16)[:ts] - dp).astype(o_ref.dtype)
        ce[...] = e[:16]

        bx, bb, bc = bx_ref[...], bb_ref[...], bc_ref[...]
        z = bc * bx
        w = cw_ref[...]
        conv, z1, z2 = _conv3(z, bxp_ref[...] * bcp_ref[...] * first, w)
        dyb = dyb_ref[...]
        dconv = dyb * bb
        dz = _conv3_t(dconv, cdz[...], w)
        o_ref[:, POOLW:2 * POOLW] = (dz * bc).astype(o_ref.dtype)
        o_ref[:, 2 * POOLW:3 * POOLW] = (dyb * conv).astype(o_ref.dtype)
        o_ref[:, 3 * POOLW:4 * POOLW] = (dz * bx).astype(o_ref.dtype)
        dw = jnp.concatenate([jnp.sum(dconv * z2, axis=0, keepdims=True), jnp.sum(dconv * z1, axis=0, keepdims=True),
                              jnp.sum(dconv * z, axis=0, keepdims=True)], axis=0)
        _acc(dcw_ref, g, dw)
        cdz[...] = dconv[:8]

    ins = [("t", u, POOLW, 8), ("t", u, POOLW, 9), ("t", u, POOLW, 10), ("t", u, POOLW, 11),
           ("h", u, 16, POOLW, 8), ("h", u, 8, POOLW, 9), ("h", u, 8, POOLW, 11),
           ("t", d_a2, POOLW, 0), ("t", d_yb, POOLW, 0), ("w", wblk), ("w", pool_scale), ("w", conv_b)]
    outs = [("t", 4 * POOLW, MXU), ("a", (1, POOLW), F32), ("a", (POOLW, POOLW), F32), ("a", (3, POOLW), F32)]
    return _rows_call("poolconv_bwd", body, S, ts, ins, outs,
                      scratch=[pltpu.VMEM((16, POOLW), F32), pltpu.VMEM((8, POOLW), F32)], reverse=True)


def _rope_tables(positions):
    S = positions.shape[0]
    inv = ROPE_THETA ** (-jnp.arange(0, 16, 2, dtype=F32) / 16)
    ang = positions.astype(F32)[:, None] * inv
    cos, sin = jnp.cos(ang), jnp.sin(ang)
    c64 = jnp.concatenate([cos, cos, jnp.ones((S, 48), F32)], axis=1)
    s64 = jnp.concatenate([-sin, sin, jnp.zeros((S, 48), F32)], axis=1)
    return jnp.concatenate([c64, c64], axis=1), jnp.concatenate([s64, s64], axis=1)


def _partner(x):
    lane = lax.broadcasted_iota(jnp.int32, x.shape, 1) % 64
    return jnp.where(lane < 8, pltpu.roll(x, LANES - 8, axis=1), jnp.where(lane < 16, pltpu.roll(x, 8, axis=1), 0.0))


def _rope(x, c, s):
    return x * c + _partner(x) * s


def _rope_t(x, c, s):
    return x * c + _partner(x * s)


def _rows_of(r, n, d):
    return pl.ds(r, n, stride=d) if d > 1 else pl.ds(0, n)


def _head_masks(shape):
    lane = lax.broadcasted_iota(jnp.int32, shape, 1) // 64
    return [lane == h for h in range(4)]


def _only(mask, x):
    return jnp.where(mask, x, jnp.zeros_like(x))


def _rope_perm(u, ctab, stab, ts=256):
    S = u.shape[0]
    nch = ATT_W // LANES

    def body(*refs):
        chunks, (c_ref, s_ref), outs = refs[:3 * nch], refs[3 * nch:3 * nch + 2], refs[3 * nch + 2:]
        for g, d in enumerate(DILS):
            n = ts // d
            for r in range(d):
                rows = _rows_of(r, n, d)
                c, s = c_ref[rows, :], s_ref[rows, :]
                for which in range(3):
                    parts = [chunks[which * nch + j][rows, :] for j in (2 * g, 2 * g + 1)]
                    if which < 2:
                        parts = [_rope(x, c, s) for x in parts]
                    outs[which * 3 + g][r] = jnp.concatenate(parts, axis=1).astype(MXU)

    base = (IN_W - 3 * ATT_W) // LANES
    in_specs = [pl.BlockSpec((ts, LANES), lambda i, cb=base + k: (i, cb)) for k in range(3 * nch)]
    in_specs += [pl.BlockSpec((ts, LANES), lambda i: (i, 0))] * 2
    out_specs = [pl.BlockSpec((d, ts // d, ATT_O), lambda i: (0, i, 0)) for _ in range(3) for d in DILS]
    out_shape = [jax.ShapeDtypeStruct((d, S // d, ATT_O), MXU) for _ in range(3) for d in DILS]
    res = pl.pallas_call(body, grid=(S // ts,), in_specs=in_specs, out_specs=out_specs, out_shape=out_shape,
                         compiler_params=_params(("arbitrary",), 32), name="rope_perm")(*([u] * (3 * nch)), ctab, stab)
    return [[res[which * 3 + g].reshape(S, ATT_O) for g in range(3)] for which in range(3)]


def _rope_unperm_bwd(dqkv, ctab, stab, ts=256):
    S = dqkv[0][0].shape[0]
    nch = ATT_W // LANES

    def body(*refs):
        ins, (c_ref, s_ref, o_ref, scr) = refs[:9], refs[9:]
        for g, d in enumerate(DILS):
            n = ts // d
            for r in range(d):
                rows = _rows_of(r, n, d)
                c, s = c_ref[rows, :], s_ref[rows, :]
                for which in range(3):
                    v = ins[which * 3 + g][r]
                    for half in range(2):
                        x = v[:, half * LANES:(half + 1) * LANES]
                        scr.at[which * nch + 2 * g + half][rows, :] = _rope_t(x, c, s) if which < 2 else x
        for j in range(3 * nch):
            o_ref[:, j * LANES:(j + 1) * LANES] = scr[j].astype(o_ref.dtype)

    in_specs = [pl.BlockSpec((d, ts // d, ATT_O), lambda i: (0, i, 0)) for _ in range(3) for d in DILS]
    in_specs += [pl.BlockSpec((ts, LANES), lambda i: (i, 0))] * 2
    args = [dqkv[which][g].reshape(d, S // d, ATT_O) for which in range(3) for g, d in enumerate(DILS)]
    return pl.pallas_call(body, grid=(S // ts,), in_specs=in_specs, out_specs=pl.BlockSpec((ts, 3 * ATT_W), lambda i: (i, 0)),
                          out_shape=jax.ShapeDtypeStruct((S, 3 * ATT_W), MXU), scratch_shapes=[pltpu.VMEM((3 * nch, ts, LANES), F32)],
                          compiler_params=_params(("arbitrary",), 32), name="rope_unperm_bwd")(*args, ctab, stab)


def _band_mask_keys(has_prev):
    r = lax.broadcasted_iota(jnp.int32, (QB, 2 * QB), 0)
    c = lax.broadcasted_iota(jnp.int32, (QB, 2 * QB), 1)
    return ((c < QB) & (c >= r) & has_prev) | ((c >= QB) & (c - QB <= r))


def _band_mask_queries(has_next):
    r = lax.broadcasted_iota(jnp.int32, (2 * QB, QB), 0)
    c = lax.broadcasted_iota(jnp.int32, (2 * QB, QB), 1)
    return ((r < QB) & (c <= r)) | ((r >= QB) & (c >= r - QB) & has_next)


ASUB = 4
_BIG = pl.BlockSpec((ASUB * QB, ATT_O), lambda b: (b, 0))
_PREV = pl.BlockSpec((QB, ATT_O), lambda b: (jnp.maximum(b * ASUB - 1, 0), 0))


def _sub(ref, j):
    return ref[j * QB:(j + 1) * QB]


def _attn_fwd(g, q, k, v):
    S = q.shape[0]
    nb = S // QB
    nblk = nb // DILS[g]

    def body(q_ref, kc_ref, kp_ref, vc_ref, vp_ref, o_ref, m_ref, l_ref):
        hm_kv, hm_o = _head_masks((2 * QB, ATT_O)), _head_masks((QB, ATT_O))
        for j in range(ASUB):
            ok = _band_mask_keys(((pl.program_id(0) * ASUB + j) & (nblk - 1)) > 0)
            k2 = jnp.concatenate([kp_ref[...] if j == 0 else _sub(kc_ref, j - 1), _sub(kc_ref, j)], axis=0)
            v2 = jnp.concatenate([vp_ref[...] if j == 0 else _sub(vc_ref, j - 1), _sub(vc_ref, j)], axis=0)
            qv = _sub(q_ref, j)
            o_acc = jnp.zeros((QB, ATT_O), F32)
            m_acc = jnp.zeros((QB, ATT_O), F32)
            l_acc = jnp.zeros((QB, ATT_O), F32)
            for h in range(4):
                s = jnp.where(ok, _dot_nt(qv, _only(hm_kv[h], k2)) * ATT_SCALE, NEG)
                m = jnp.max(s, axis=1, keepdims=True)
                p = jnp.exp(s - m)
                o_acc = o_acc + _dot(p.astype(MXU), _only(hm_kv[h], v2))
                m_acc = jnp.where(hm_o[h], m, m_acc)
                l_acc = jnp.where(hm_o[h], jnp.sum(p, axis=1, keepdims=True), l_acc)
            o_ref[j * QB:(j + 1) * QB] = o_acc
            m_ref[j * QB:(j + 1) * QB] = m_acc
            l_ref[j * QB:(j + 1) * QB] = l_acc

    shp = jax.ShapeDtypeStruct((S, ATT_O), F32)
    return pl.pallas_call(body, grid=(nb // ASUB,), in_specs=[_BIG, _BIG, _PREV, _BIG, _PREV],
                          out_specs=[_BIG] * 3, out_shape=[shp, shp, shp], compiler_params=_params(("arbitrary",), 32),
                          name=f"attn_fwd_{g}")(q, k, k, v, v)


def _natural(ref, d, scr, ts):
    if d == 1:
        return ref[0]
    n = ts // d
    for r in range(d):
        v = ref[r]
        scr.at[0][pl.ds(r, n, stride=d), :] = v[:, 0:LANES]
        scr.at[1][pl.ds(r, n, stride=d), :] = v[:, LANES:2 * LANES]
    return jnp.concatenate([scr[0], scr[1]], axis=1)


def _attn_combine(oml, ts=256):
    S = oml[0][0].shape[0]

    def body(*refs):
        ins, (att_ref, out_ref, lse_ref, scr) = refs[:9], refs[9:]
        o, m, l = [[_natural(ins[3 * g + k], d, scr, ts) for g, d in enumerate(DILS)] for k in range(3)]
        mx = jnp.maximum(jnp.maximum(m[0], m[1]), m[2])
        w = [jnp.exp(m[g] - mx) for g in range(3)]
        den = w[0] * l[0] + w[1] * l[1] + w[2] * l[2]
        out = (w[0] * o[0] + w[1] * o[1] + w[2] * o[2]) / den
        out_ref[...] = out
        att_ref[...] = out.astype(MXU)
        lse_ref[...] = mx + jnp.log(den)

    in_specs = [pl.BlockSpec((d, ts // d, ATT_O), lambda i: (0, i, 0)) for d in DILS for _ in range(3)]
    args = [a.reshape(d, S // d, ATT_O) for d, grp in zip(DILS, oml) for a in grp]
    blk = pl.BlockSpec((ts, ATT_O), lambda i: (i, 0))
    return pl.pallas_call(body, grid=(S // ts,), in_specs=in_specs, out_specs=[blk, blk, blk],
                          out_shape=[jax.ShapeDtypeStruct((S, ATT_O), MXU), jax.ShapeDtypeStruct((S, ATT_O), F32),
                                     jax.ShapeDtypeStruct((S, ATT_O), F32)],
                          scratch_shapes=[pltpu.VMEM((2, ts, LANES), F32)], compiler_params=_params(("arbitrary",), 32),
                          name="attn_combine")(*args)


def _attn_bwd_prep(datt, o, lse, ts=256):
    S = datt.shape[0]

    def body(da0, da1, o_ref, l0, l1, *rest):
        outs, dl = rest[:9], rest[9]
        prod = jnp.concatenate([da0[...], da1[...]], axis=1) * o_ref[...]
        delta = jnp.zeros((ts, ATT_O), F32)
        for hm in _head_masks((ts, ATT_O)):
            delta = jnp.where(hm, jnp.sum(_only(hm, prod), axis=1, keepdims=True), delta)
        dl[0] = delta[:, 0:LANES]
        dl[1] = delta[:, LANES:2 * LANES]
        for g, d in enumerate(DILS):
            n = ts // d
            for r in range(d):
                rows = _rows_of(r, n, d)
                outs[g][r] = jnp.concatenate([da0[rows, :], da1[rows, :]], axis=1).astype(MXU)
                outs[3 + g][r] = jnp.concatenate([dl.at[0][rows, :], dl.at[1][rows, :]], axis=1)
                outs[6 + g][r] = jnp.concatenate([l0[rows, :], l1[rows, :]], axis=1)

    half = lambda j: pl.BlockSpec((ts, LANES), lambda i: (i, j))
    out_specs = [pl.BlockSpec((d, ts // d, ATT_O), lambda i: (0, i, 0)) for _ in range(3) for d in DILS]
    out_shape = [jax.ShapeDtypeStruct((d, S // d, ATT_O), dt) for dt in (MXU, F32, F32) for d in DILS]
    res = pl.pallas_call(body, grid=(S // ts,), in_specs=[half(0), half(1), pl.BlockSpec((ts, ATT_O), lambda i: (i, 0)), half(0), half(1)],
                         out_specs=out_specs, out_shape=out_shape, scratch_shapes=[pltpu.VMEM((2, ts, LANES), F32)],
                         compiler_params=_params(("arbitrary",), 32), name="attn_bwd_prep")(datt, datt, o, lse, lse)
    return [[res[k * 3 + g].reshape(S, ATT_O) for g in range(3)] for k in range(3)]


def _head_col(x, h):
    return x[:, h * 64:h * 64 + 1]


def _attn_dq(g, q, k, v, do, delta, lse):
    S = q.shape[0]
    nb = S // QB
    nblk = nb // DILS[g]

    def body(q_ref, kc_ref, kp_ref, vc_ref, vp_ref, do_ref, dl_ref, lse_ref, dq_ref):
        hms = _head_masks((2 * QB, ATT_O))
        for j in range(ASUB):
            ok = _band_mask_keys(((pl.program_id(0) * ASUB + j) & (nblk - 1)) > 0)
            k2 = jnp.concatenate([kp_ref[...] if j == 0 else _sub(kc_ref, j - 1), _sub(kc_ref, j)], axis=0)
            v2 = jnp.concatenate([vp_ref[...] if j == 0 else _sub(vc_ref, j - 1), _sub(vc_ref, j)], axis=0)
            qv, dov, dl, lse_v = _sub(q_ref, j), _sub(do_ref, j), _sub(dl_ref, j), _sub(lse_ref, j)
            dq = jnp.zeros((QB, ATT_O), F32)
            for h, hm in enumerate(hms):
                kh = _only(hm, k2)
                p = jnp.where(ok, jnp.exp(_dot_nt(qv, kh) * ATT_SCALE - _head_col(lse_v, h)), 0.0)
                ds = p * (_dot_nt(dov, _only(hm, v2)) - _head_col(dl, h))
                dq = dq + _dot(ds.astype(MXU), kh)
            dq_ref[j * QB:(j + 1) * QB] = dq * ATT_SCALE

    return pl.pallas_call(body, grid=(nb // ASUB,), in_specs=[_BIG, _BIG, _PREV, _BIG, _PREV, _BIG, _BIG, _BIG], out_specs=_BIG,
                          out_shape=jax.ShapeDtypeStruct((S, ATT_O), F32),
                          compiler_params=_params(("arbitrary",), 32), name=f"attn_dq_{g}")(q, k, k, v, v, do, delta, lse)


def _attn_dkv(g, q, k, v, do, delta, lse):
    S = q.shape[0]
    nb = S // QB
    nblk = nb // DILS[g]

    def body(k_ref, v_ref, qc_ref, qn_ref, doc_ref, don_ref, dlc_ref, dln_ref, lc_ref, ln_ref, dk_ref, dv_ref):
        hms = _head_masks((2 * QB, ATT_O))

        def both(cur_ref, nxt_ref, j):
            return jnp.concatenate([_sub(cur_ref, j), nxt_ref[...] if j == ASUB - 1 else _sub(cur_ref, j + 1)], axis=0)

        for j in range(ASUB):
            ok = _band_mask_queries(((pl.program_id(0) * ASUB + j + 1) & (nblk - 1)) > 0)
            q2, do2, dl2, lse2 = both(qc_ref, qn_ref, j), both(doc_ref, don_ref, j), both(dlc_ref, dln_ref, j), both(lc_ref, ln_ref, j)
            kv, vv = _sub(k_ref, j), _sub(v_ref, j)
            dk = jnp.zeros((QB, ATT_O), F32)
            dv = jnp.zeros((QB, ATT_O), F32)
            for h, hm in enumerate(hms):
                qh, doh = _only(hm, q2), _only(hm, do2)
                p = jnp.where(ok, jnp.exp(_dot_nt(qh, kv) * ATT_SCALE - _head_col(lse2, h)), 0.0)
                ds = p * (_dot_nt(doh, vv) - _head_col(dl2, h))
                dv = dv + _dot_tn(p.astype(MXU), doh)
                dk = dk + _dot_tn(ds.astype(MXU), qh)
            dk_ref[j * QB:(j + 1) * QB] = dk * ATT_SCALE
            dv_ref[j * QB:(j + 1) * QB] = dv

    nxt = pl.BlockSpec((QB, ATT_O), lambda b: (jnp.minimum((b + 1) * ASUB, nb - 1), 0))
    shp = jax.ShapeDtypeStruct((S, ATT_O), F32)
    return pl.pallas_call(body, grid=(nb // ASUB,), in_specs=[_BIG, _BIG, _BIG, nxt, _BIG, nxt, _BIG, nxt, _BIG, nxt], out_specs=[_BIG, _BIG],
                          out_shape=[shp, shp], compiler_params=_params(("arbitrary",), 32),
                          name=f"attn_dkv_{g}")(k, v, q, q, do, do, delta, delta, lse, lse)


def _merge_fwd(x0, u, a2, yb, att, wa, wb, wc, w_out, g_post, ts=256):
    S = x0.shape[0]

    def body(i, g, x_ref, gate_ref, a2_ref, yb_ref, att_ref, wa_ref, wb_ref, wc_ref, wo_ref, gp_ref, mg_ref, y_ref, xo_ref):
        merged = jax.nn.sigmoid(gate_ref[:, 0:D]) * _dot_nt(a2_ref[...], wa_ref[...])
        merged = merged + jax.nn.sigmoid(gate_ref[:, D:2 * D]) * _dot_nt(yb_ref[...], wb_ref[...])
        merged = merged + jax.nn.sigmoid(gate_ref[:, 2 * D:3 * D]) * _dot_nt(att_ref[...], wc_ref[...])
        mb = merged.astype(MXU)
        mg_ref[...] = mb
        y = _dot(mb, wo_ref[...])
        y_ref[...] = y
        xo_ref[...] = x_ref[...] + _rms(y, gp_ref[...])[0]

    ins = [("t", x0, D, 0), ("t", u, GATE_W, 0), ("t", a2, POOLW, 0), ("t", yb, POOLW, 0), ("t", att, ATT_O, 0),
           ("w", wa), ("w", wb), ("w", wc), ("w", w_out), ("w", g_post)]
    return _rows_call("merge_fwd", body, S, ts, ins, [("t", D, MXU), ("t", D, F32), ("t", D, F32)])


def _merge_bwd(dx, y1, u, a2, yb, att, wa, wb, wc, w_out, g_post, ts=256):
    S = dx.shape[0]

    def body(i, g, dx_ref, y_ref, gate_ref, a2_ref, yb_ref, att_ref, wa_ref, wb_ref, wc_ref, wo_ref, gp_ref,
             dy_ref, dgate_ref, dbra_ref, dbrb_ref, dbrc_ref, da2_ref, dyb_ref, datt_ref, dgp_ref):
        dxv, y = dx_ref[...], y_ref[...]
        dy, r = _rms_bwd(dxv * gp_ref[...], y)
        _acc(dgp_ref, g, jnp.sum(dxv * (y * r), axis=0, keepdims=True))
        dyb16 = dy.astype(MXU)
        dy_ref[...] = dyb16
        dm = _dot_nt(dyb16, wo_ref[...])
        for n, (src, w_ref, dbr_ref, din_ref) in enumerate(((a2_ref, wa_ref, dbra_ref, da2_ref), (yb_ref, wb_ref, dbrb_ref, dyb_ref),
                                                           (att_ref, wc_ref, dbrc_ref, datt_ref))):
            gt = jax.nn.sigmoid(gate_ref[:, n * D:(n + 1) * D])
            br = _dot_nt(src[...], w_ref[...])
            dgate_ref[:, n * D:(n + 1) * D] = (dm * br * gt * (1.0 - gt)).astype(dgate_ref.dtype)
            dbr = (dm * gt).astype(MXU)
            dbr_ref[...] = dbr
            din_ref[...] = _dot(dbr, w_ref[...])

    ins = [("t", dx, D, 0), ("t", y1, D, 0), ("t", u, GATE_W, 0), ("t", a2, POOLW, 0), ("t", yb, POOLW, 0), ("t", att, ATT_O, 0),
           ("w", wa), ("w", wb), ("w", wc), ("w", w_out), ("w", g_post)]
    outs = [("t", D, MXU), ("t", GATE_W, MXU), ("t", D, MXU), ("t", D, MXU), ("t", D, MXU), ("t", POOLW, F32), ("t", POOLW, F32),
            ("t", ATT_O, F32), ("a", (1, D), F32)]
    return _rows_call("merge_bwd", body, S, ts, ins, outs)


def _prenorm_bwd(name, dx_res, du, wt, x, g_pre, ts=256, lead=0):
    S = x.shape[0]
    N = du.shape[1]

    def body(i, g, dx_ref, du_ref, wt_ref, x_ref, g_ref, o_ref, dg_ref):
        if lead:
            dhv = _dot(du_ref[:, 0:lead], wt_ref[N - lead:N, :]) + _dot(du_ref[:, lead:N], wt_ref[0:N - lead, :])
        else:
            dhv = _dot(du_ref[...], wt_ref[...])
        xv = x_ref[...]
        dxn, r = _rms_bwd(dhv * g_ref[...], xv)
        o_ref[...] = dx_ref[...] + dxn
        _acc(dg_ref, g, jnp.sum(dhv * (xv * r), axis=0, keepdims=True))

    ins = [("t", dx_res, D, 0), ("t", du, N, 0), ("w", wt), ("t", x, D, 0), ("w", g_pre)]
    return _rows_call(name, body, S, ts, ins, [("t", D, F32), ("a", (1, D), F32)], vmem_mb=52)


def _mem_heads(qm, kv_ref):
    out = []
    for h in range(4):
        q = qm[:, h * 128:(h + 1) * 128].astype(MXU)
        k = kv_ref[:, h * 128:(h + 1) * 128]
        v = kv_ref[:, MEM_W + h * 128:MEM_W + (h + 1) * 128]
        sc = _dot_nt(q, k) * MEM_SCALE
        e = jnp.exp(sc - jnp.max(sc, axis=1, keepdims=True))
        out.append((e / jnp.sum(e, axis=1, keepdims=True), q, k, v))
    return out


def _mem_fwd(x1, kv, g_pre, w_mq, w_mo, g_post, ts=256):
    S = x1.shape[0]

    def body(i, g, x_ref, kv_ref, gq_ref, wq_ref, wo_ref, gp_ref, om_ref, h_ref, y_ref, xo_ref):
        x = x_ref[...]
        hb = _rms(x, gq_ref[...])[0].astype(MXU)
        h_ref[...] = hb
        qm = _dot(hb, wq_ref[...])
        om = jnp.concatenate([_dot(p.astype(MXU), v) for p, _, _, v in _mem_heads(qm, kv_ref)], axis=1).astype(MXU)
        om_ref[...] = om
        y = _dot_nt(om, wo_ref[...])
        y_ref[...] = y
        xo_ref[...] = x + _rms(y, gp_ref[...])[0]

    ins = [("t", x1, D, 0), ("w", kv), ("w", g_pre), ("w", w_mq), ("w", w_mo), ("w", g_post)]
    return _rows_call("mem_fwd", body, S, ts, ins, [("t", MEM_W, MXU), ("t", D, MXU), ("t", D, F32), ("t", D, F32)])


def _mem_bwd(dx2, ym, x1, kv, g_pre, w_mq, w_mo, g_post, ts=256):
    S = x1.shape[0]

    def body(i, g, dx_ref, y_ref, x_ref, kv_ref, gq_ref, wq_ref, wo_ref, gp_ref, dy_ref, dq_ref, dxo_ref, dgp_ref, dgq_ref, dkv_ref):
        dxv, y, x = dx_ref[...], y_ref[...], x_ref[...]
        dy, r = _rms_bwd(dxv * gp_ref[...], y)
        _acc(dgp_ref, g, jnp.sum(dxv * (y * r), axis=0, keepdims=True))
        dyb = dy.astype(MXU)
        dy_ref[...] = dyb
        dom = _dot(dyb, wo_ref[...])
        h, r1 = _rms(x, gq_ref[...])
        qm = _dot(h.astype(MXU), wq_ref[...])
        dqs = []

        @pl.when(g == 0)
        def _():
            dkv_ref[...] = jnp.zeros_like(dkv_ref)

        for hh, (p, q, k, v) in enumerate(_mem_heads(qm, kv_ref)):
            doh = dom[:, hh * 128:(hh + 1) * 128].astype(MXU)
            dp = _dot_nt(doh, v)
            dsc = (p * (dp - jnp.sum(dp * p, axis=1, keepdims=True)) * MEM_SCALE).astype(MXU)
            dqs.append(_dot(dsc, k))
            dkv_ref[:, hh * 128:(hh + 1) * 128] += _dot_tn(dsc, q)
            dkv_ref[:, MEM_W + hh * 128:MEM_W + (hh + 1) * 128] += _dot_tn(p.astype(MXU), doh)
        dq = jnp.concatenate(dqs, axis=1).astype(MXU)
        dq_ref[...] = dq
        dh = _dot_nt(dq, wq_ref[...])
        _acc(dgq_ref, g, jnp.sum(dh * (x * r1), axis=0, keepdims=True))
        dxo_ref[...] = dxv + _rms_bwd(dh * gq_ref[...], x)[0]

    ins = [("t", dx2, D, 0), ("t", ym, D, 0), ("t", x1, D, 0), ("w", kv), ("w", g_pre), ("w", w_mq), ("w", w_mo), ("w", g_post)]
    outs = [("t", D, MXU), ("t", MEM_W, MXU), ("t", D, F32), ("a", (1, D), F32), ("a", (1, D), F32), ("a", (256, D), F32)]
    return _rows_call("mem_bwd", body, S, ts, ins, outs)


def _gain_grad(name, dn, x):
    n = x.shape[0]

    def body(i, g, dn_ref, x_ref, o_ref):
        xv = x_ref[...]
        r = lax.rsqrt(jnp.mean(xv * xv, axis=-1, keepdims=True) + EPS)
        o_ref[...] = jnp.sum(dn_ref[...] * (xv * r), axis=0, keepdims=True)

    return _rows_call(name, body, n, n, [("t", dn, D, 0), ("t", x, D, 0)], [("a", (1, D), F32)])[0]


def _ffn_fwd(x2, u3, conv_f, w_down, g_post, ts=256):
    S = x2.shape[0]

    def body(i, g, x_ref, ua_ref, ub_ref, cw_ref, wd_ref, gp_ref, act_ref, y_ref, xo_ref, cu):
        @pl.when(g == 0)
        def _():
            cu[...] = jnp.zeros_like(cu)

        ua = ua_ref[...]
        c, _, _ = _conv3(ua, cu[...], cw_ref[...])
        act = (c * jax.nn.sigmoid(c) * ub_ref[...]).astype(MXU)
        act_ref[...] = act
        y = _dot(act, wd_ref[...])
        y_ref[...] = y
        xo_ref[...] = x_ref[...] + _rms(y, gp_ref[...])[0]
        cu[...] = ua[ts - 8:]

    ins = [("t", x2, D, 0), ("t", u3, D_FF, 0), ("t", u3, D_FF, 1), ("w", conv_f), ("w", w_down), ("w", g_post)]
    return _rows_call("ffn_fwd", body, S, ts, ins, [("t", D_FF, MXU), ("t", D, F32), ("t", D, F32)],
                      scratch=[pltpu.VMEM((8, D_FF), F32)], vmem_mb=56)


def _ffn_bwd(dx3, y3, u3, conv_f, w_down, g_post, ts=128):
    S = dx3.shape[0]

    def body(i, g, dx_ref, y_ref, ua_ref, ub_ref, uap_ref, cw_ref, wd_ref, gp_ref, dy_ref, du_ref, dgp_ref, dcw_ref, cdc):
        @pl.when(g == 0)
        def _():
            cdc[...] = jnp.zeros_like(cdc)

        dxv, y = dx_ref[...], y_ref[...]
        dy, r = _rms_bwd(dxv * gp_ref[...], y)
        _acc(dgp_ref, g, jnp.sum(dxv * (y * r), axis=0, keepdims=True))
        dyb = dy.astype(MXU)
        dy_ref[...] = dyb
        dact = _dot_nt(dyb, wd_ref[...])
        ua, w = ua_ref[...], cw_ref[...]
        c, u1, u2 = _conv3(ua, uap_ref[...] * (i > 0).astype(F32), w)
        sg = jax.nn.sigmoid(c)
        du_ref[:, D_FF:2 * D_FF] = (dact * (c * sg)).astype(du_ref.dtype)
        dc = dact * ub_ref[...] * (sg * (1.0 + c * (1.0 - sg)))
        du_ref[:, 0:D_FF] = _conv3_t(dc, cdc[...], w).astype(du_ref.dtype)
        dw = jnp.concatenate([jnp.sum(dc * u2, axis=0, keepdims=True), jnp.sum(dc * u1, axis=0, keepdims=True),
                              jnp.sum(dc * ua, axis=0, keepdims=True)], axis=0)
        _acc(dcw_ref, g, dw)
        cdc[...] = dc[:8]

    ins = [("t", dx3, D, 0), ("t", y3, D, 0), ("t", u3, D_FF, 0), ("t", u3, D_FF, 1), ("h", u3, 8, D_FF, 0), ("w", conv_f),
           ("w", w_down), ("w", g_post)]
    outs = [("t", D, MXU), ("t", 2 * D_FF, MXU), ("a", (1, D), F32), ("a", (3, D_FF), F32)]
    return _rows_call("ffn_bwd", body, S, ts, ins, outs, scratch=[pltpu.VMEM((8, D_FF), F32)], reverse=True, vmem_mb=56)


def _loss_head(x, target, ts=512):
    S = x.shape[0]

    def body(i, g, x_ref, t_ref, dx_ref, acc_ref):
        diff = x_ref[...] - t_ref[...]
        dx_ref[...] = diff * (1.0 / D)
        col = jnp.sum(diff * diff, axis=0, keepdims=True)
        part = col[:, 0:LANES]
        for j in range(1, D // LANES):
            part = part + col[:, j * LANES:(j + 1) * LANES]
        row = lax.broadcasted_iota(jnp.int32, (8, LANES), 0)
        _acc(acc_ref, g, jnp.where(row == 0, jnp.broadcast_to(part, (8, LANES)), 0.0))

    return _rows_call("loss_head", body, S, ts, [("t", x, D, 0), ("t", target, D, 0)], [("t", D, F32), ("a", (8, LANES), F32)])


def _layer_weights(big, small, l):
    w_in = big['w_in']
    pool_w = small['pool_w'][l].astype(MXU)
    wblk = jnp.zeros((POOLW, POOLW), MXU)
    for g in range(4):
        wblk = lax.dynamic_update_slice(wblk, pool_w[g], (g * 96, g * 96))
    vec = lambda n: small[n][l].reshape(1, -1)
    return dict(
        w_in=w_in,
        wblk=wblk, pool_scale=vec('pool_scale'), conv_b=small['conv_b_w'][l], wa=big['w_branch_a'], wb=big['w_branch_b'],
        wc=big['w_branch_c'], w_out=big['w_out'], w_mq=big['w_mq'], w_mkv=big['w_mkv'], w_mo=big['w_mo'],
        w_up=big['w_up'], conv_f=small['conv_ffn_w'][l], w_down=big['w_down'],
        g_mix_pre=vec('norm_mix_pre'), g_mix_post=vec('norm_mix_post'), g_mem_pre=vec('norm_mem_pre'),
        g_mem_post=vec('norm_mem_post'), g_memkv=vec('norm_memkv'), g_ffn_pre=vec('norm_ffn_pre'), g_ffn_post=vec('norm_ffn_post'))


def _layer_fwd(x0, mem, W, ctab, stab):
    sv = dict(x0=x0)
    sv['u'], sv['h1'] = _norm_mm("in_proj", x0, W['g_mix_pre'], W['w_in'], ts=1024, tn=IN_TILE, wt=True, rot=IN_ROT)
    sv['a2'], sv['yb'] = _poolconv_fwd(sv['u'], W['wblk'], W['pool_scale'], W['conv_b'])
    sv['qkv'] = q3, k3, v3 = _rope_perm(sv['u'], ctab, stab)
    sv['att'], sv['o'], sv['lse'] = _attn_combine([_attn_fwd(g, q3[g], k3[g], v3[g]) for g in range(3)])
    sv['merged'], sv['y1'], sv['x1'] = _merge_fwd(x0, sv['u'], sv['a2'], sv['yb'], sv['att'], W['wa'], W['wb'], W['wc'],
                                                  W['w_out'], W['g_mix_post'])
    sv['kv'], sv['memn'] = _norm_mm("mem_kv", mem, W['g_memkv'], W['w_mkv'], ts=256, tn=D, out_dtype=MXU)
    sv['om'], sv['h2'], sv['ym'], sv['x2'] = _mem_fwd(sv['x1'], sv['kv'], W['g_mem_pre'], W['w_mq'], W['w_mo'], W['g_mem_post'])
    sv['u3'], sv['h3'] = _norm_mm("up_proj", sv['x2'], W['g_ffn_pre'], W['w_up'], ts=1024, tn=1408, wt=True)
    sv['act'], sv['y3'], x3 = _ffn_fwd(sv['x2'], sv['u3'], W['conv_f'], W['w_down'], W['g_ffn_post'])
    return x3, sv


def _layer_bwd(dx3, mem, W, sv, ctab, stab):
    dx1, g = _layer_bwd_late(dx3, mem, W, sv)
    dx0, g_mix = _layer_bwd_mix(dx1, W, sv, ctab, stab)
    return dx0, {**g, **g_mix}


def _layer_bwd_late(dx3, mem, W, sv):
    g = {}
    dy3, du3, g['norm_ffn_post'], g['conv_ffn_w'] = _ffn_bwd(dx3, sv['y3'], sv['u3'], W['conv_f'], W['w_down'], W['g_ffn_post'])
    g['w_down'] = _mm_tn("dw_down", sv['act'], dy3, cap_k=256)
    g['w_up'] = _mm_tn("dw_up", du3, sv['h3'])
    dx2, g['norm_ffn_pre'] = _prenorm_bwd("ffn_pre_bwd", dx3, du3, W['w_up'], sv['x2'], W['g_ffn_pre'])
    dym, dqm, dx1, g['norm_mem_post'], g['norm_mem_pre'], dkv = _mem_bwd(dx2, sv['ym'], sv['x1'], sv['kv'], W['g_mem_pre'],
                                                                       W['w_mq'], W['w_mo'], W['g_mem_post'])
    g['w_mo'] = _mm_tn("dw_mo", dym, sv['om'])
    g['w_mq'] = _mm_tn("dw_mq", sv['h2'], dqm)
    dkvb = dkv.astype(MXU)
    g['w_mkv'] = _mm_tn("dw_mkv", sv['memn'], dkvb)
    g['norm_memkv'] = _gain_grad("memkv_gain", _mm_nt("d_memn", dkvb, W['w_mkv'], ts=256, tn=512), mem)
    return dx1, g


def _layer_bwd_mix(dx1, W, sv, ctab, stab):
    g = {}
    dy1, dgate, dbra, dbrb, dbrc, da2, dyb, datt, g['norm_mix_post'] = _merge_bwd(
        dx1, sv['y1'], sv['u'], sv['a2'], sv['yb'], sv['att'], W['wa'], W['wb'], W['wc'], W['w_out'], W['g_mix_post'])
    g['w_out'] = _mm_tn("dw_out", sv['merged'], dy1)
    g['w_branch_a'] = _mm_tn("dw_a", dbra, sv['a2'])
    g['w_branch_b'] = _mm_tn("dw_b", dbrb, sv['yb'])
    g['w_branch_c'] = _mm_tn("dw_c", dbrc, sv['att'])
    dabc, g['pool_scale'], dwblk, g['conv_b_w'] = _poolconv_bwd(sv['u'], da2, dyb, W['wblk'], W['pool_scale'], W['conv_b'])
    g['pool_w'] = jnp.stack([dwblk[k * 96:(k + 1) * 96, k * 96:(k + 1) * 96] for k in range(4)])
    q3, k3, v3 = sv['qkv']
    do3, dl3, lse3 = _attn_bwd_prep(datt, sv['o'], sv['lse'])
    dq3 = [_attn_dq(i, q3[i], k3[i], v3[i], do3[i], dl3[i], lse3[i]) for i in range(3)]
    dkv3 = [_attn_dkv(i, q3[i], k3[i], v3[i], do3[i], dl3[i], lse3[i]) for i in range(3)]
    dqkv = _rope_unperm_bwd([dq3, [a for a, _ in dkv3], [b for _, b in dkv3]], ctab, stab)
    du = jnp.concatenate([dgate, dabc, dqkv], axis=1)
    g['w_in'] = _mm_tn("dw_in", du, sv['h1'], cap_k=IN_TILE, rot=IN_ROT)
    dx0, g['norm_mix_pre'] = _prenorm_bwd("mix_pre_bwd", dx1, du, W['w_in'], sv['x0'], W['g_mix_pre'], lead=GATE_W)
    return dx0, g


def _local_step(x, mem, positions, target, big, small):
    ctab, stab = _rope_tables(positions)
    Ws = [_layer_weights(big[l], small, l) for l in range(DEPTH)]
    saved = []
    for l in range(DEPTH):
        x, sv = _layer_fwd(x, mem, Ws[l], ctab, stab)
        saved.append(sv)
    dx, acc = _loss_head(x, target)
    loss = jnp.sum(acc) * (0.5 / D)
    grads = [None] * DEPTH
    for l in reversed(range(DEPTH)):
        dx, grads[l] = _layer_bwd(dx, mem, Ws[l], saved[l], ctab, stab)
    return loss, dx, grads


_HBM = pl.BlockSpec(memory_space=pl.ANY)
MESH_ID = pl.DeviceIdType.MESH


def _all_gather(name, xs):
    n = len(xs)

    def body(*refs):
        x_refs, out_refs = refs[:n], refs[n:2 * n]
        send_sems, recv_sems, local_sems = refs[2 * n:]
        x, y, c = lax.axis_index("x"), lax.axis_index("y"), lax.axis_index("c")
        me, sibling = (x, y, c), (x, y, 1 - c)
        chips = [(1 - x, y), (x, 1 - y), (1 - x, 1 - y)]

        def slot(a, p):
            return out_refs[a].at[4 * p[0] + 2 * p[1] + p[2]]

        def copy(a, k, block, to, src=None):
            return pltpu.make_async_remote_copy(src_ref=slot(a, block) if src is None else src, dst_ref=slot(a, block),
                                                send_sem=send_sems.at[a, k], recv_sem=recv_sems.at[a, k], device_id=to,
                                                device_id_type=MESH_ID)

        started = []
        for a in range(n):
            mine = pltpu.make_async_copy(x_refs[a], slot(a, me), local_sems.at[a])
            mine.start()
            started.append(mine)
        first = []
        for a in range(n):
            first.append(copy(a, 0, me, sibling, src=x_refs[a]))
            first += [copy(a, 1 + j, me, (*chip, c), src=x_refs[a]) for j, chip in enumerate(chips)]
        for cp in first:
            cp.start()
        passed = []
        for j, chip in enumerate(chips):
            for a in range(n):
                copy(a, 1 + j, (*chip, c), me).wait_recv()
                fw = copy(a, 4 + j, (*chip, c), sibling)
                fw.start()
                passed.append(fw)
        for a in range(n):
            copy(a, 0, sibling, me).wait_recv()
            for j, chip in enumerate(chips):
                copy(a, 4 + j, (*chip, 1 - c), me).wait_recv()
        for cp in first + passed:
            cp.wait_send()
        for mine in started:
            mine.wait()

    return pl.pallas_call(
        body, out_shape=[jax.ShapeDtypeStruct((N_DEV,) + x.shape, x.dtype) for x in xs], in_specs=[_HBM] * n, out_specs=[_HBM] * n,
        scratch_shapes=[pltpu.SemaphoreType.DMA((n, 7)), pltpu.SemaphoreType.DMA((n, 7)), pltpu.SemaphoreType.DMA((n,))],
        name=name)(*xs)


def _exchange(name, gs):
    n = len(gs)

    def body(*refs):
        g_refs, out_refs = refs[:n], refs[n:2 * n]
        send_sems, recv_sems, local_sems = refs[2 * n:]
        x, y, c = lax.axis_index("x"), lax.axis_index("y"), lax.axis_index("c")
        me = 4 * x + 2 * y + c
        copies = []
        for a in range(n):
            mine = pltpu.make_async_copy(g_refs[a].at[me], out_refs[a].at[me], local_sems.at[a])
            mine.start()
            copies.append(mine)
        for r in range(1, N_DEV):
            px, py, pc = x ^ ((r >> 2) & 1), y ^ ((r >> 1) & 1), c ^ (r & 1)
            for a in range(n):
                cp = pltpu.make_async_remote_copy(src_ref=g_refs[a].at[4 * px + 2 * py + pc], dst_ref=out_refs[a].at[me],
                                                  send_sem=send_sems.at[a, r - 1], recv_sem=recv_sems.at[a, r - 1],
                                                  device_id=(px, py, pc), device_id_type=MESH_ID)
                cp.start()
                copies.append(cp)
        for cp in copies:
            cp.wait()

    return pl.pallas_call(
        body, out_shape=[jax.ShapeDtypeStruct(g.shape, g.dtype) for g in gs], in_specs=[_HBM] * n, out_specs=[_HBM] * n,
        scratch_shapes=[pltpu.SemaphoreType.DMA((n, N_DEV - 1)), pltpu.SemaphoreType.DMA((n, N_DEV - 1)), pltpu.SemaphoreType.DMA((n,))],
        name=name)(*gs)


_SEM = pl.BlockSpec(memory_space=pltpu.SEMAPHORE)
_IN_HBM = pl.BlockSpec(memory_space=pltpu.HBM)
_SIDE_EFFECT = pltpu.SideEffectType.DATAFLOW_SIDE_EFFECTING


def _push_copies(src_refs, land_refs, send_sems, recv_sems, per_peer):
    x, y, c = lax.axis_index("x"), lax.axis_index("y"), lax.axis_index("c")
    me = 4 * x + 2 * y + c
    copies = []
    for r in range(1, N_DEV):
        px, py, pc = x ^ ((r >> 2) & 1), y ^ ((r >> 1) & 1), c ^ (r & 1)
        for a, (s, d) in enumerate(zip(src_refs, land_refs)):
            k = a * (N_DEV - 1) + r - 1
            copies.append(pltpu.make_async_remote_copy(src_ref=s.at[4 * px + 2 * py + pc] if per_peer else s, dst_ref=d.at[me],
                                                       send_sem=send_sems.at[k], recv_sem=recv_sems.at[k],
                                                       device_id=(px, py, pc), device_id_type=MESH_ID))
    return copies


def _push_start(name, srcs, per_peer, after):
    n = len(srcs)
    lands = [lax.empty((N_DEV,) + (s.shape[1:] if per_peer else s.shape), s.dtype) for s in srcs]

    def body(*refs):
        for cp in _push_copies(refs[:n], refs[n:2 * n], refs[2 * n + 1], refs[2 * n + 2], per_peer):
            cp.start()
        refs[-1][...] = jnp.zeros_like(refs[-1])

    hbm = [pltpu.HBM(a.shape, a.dtype) for a in (*srcs, *lands)]
    sems = pltpu.SemaphoreType.DMA((n * (N_DEV - 1),))
    out = pl.pallas_call(
        body, name=name, out_shape=(sems, sems, *hbm, jax.ShapeDtypeStruct((8, LANES), F32)),
        in_specs=[_IN_HBM] * (2 * n) + [pl.BlockSpec(memory_space=pl.ANY)],
        out_specs=(_SEM, _SEM, *[_IN_HBM] * (2 * n), pl.BlockSpec(memory_space=pltpu.VMEM)),
        input_output_aliases={a: 2 + a for a in range(2 * n)},
        compiler_params=pltpu.CompilerParams(has_side_effects=_SIDE_EFFECT),
    )(*[pltpu.with_memory_space_constraint(a, pltpu.HBM) for a in (*srcs, *lands)], after)
    return out[0], out[1], out[2:2 + n], out[2 + n:2 + 2 * n], out[-1]


def _push_wait(name, started, per_peer, after):
    send_sems, recv_sems, srcs, lands, _ = started
    n = len(srcs)

    def body(*refs):
        for cp in _push_copies(refs[:n], refs[n:2 * n], refs[2 * n], refs[2 * n + 1], per_peer):
            cp.wait_send()
            cp.wait_recv()

    out = pl.pallas_call(
        body, name=name, out_shape=[pltpu.HBM(a.shape, a.dtype) for a in (*srcs, *lands)],
        in_specs=[_IN_HBM] * (2 * n) + [_SEM, _SEM, pl.BlockSpec(memory_space=pl.ANY)], out_specs=[_IN_HBM] * (2 * n),
        input_output_aliases={a: a for a in range(2 * n)},
        compiler_params=pltpu.CompilerParams(has_side_effects=_SIDE_EFFECT),
    )(*srcs, *lands, send_sems, recv_sems, after)
    return out[n:]


def _my_slot():
    return 4 * lax.axis_index("x") + 2 * lax.axis_index("y") + lax.axis_index("c")


def _row_tile(rows, cols, budget):
    if rows * cols * 4 <= budget or rows % 16:
        return rows
    best = 16
    for t in range(16, rows + 1, 16):
        if rows % t == 0 and t * cols * 4 <= budget:
            best = t
    return best


def _sum_slots(name, recv):
    _, R, C = recv.shape
    tr = _row_tile(R, C, 1 << 20)

    def body(r_ref, o_ref):
        g = r_ref[0].astype(F32)
        for k in range(1, N_DEV):
            g = g + r_ref[k].astype(F32)
        o_ref[...] = g

    return pl.pallas_call(body, grid=(R // tr,), in_specs=[pl.BlockSpec((N_DEV, tr, C), lambda i: (0, i, 0))],
                          out_specs=pl.BlockSpec((tr, C), lambda i: (i, 0)), out_shape=jax.ShapeDtypeStruct((R, C), F32),
                          compiler_params=_params(("arbitrary",), 32), name=name)(recv)


def _adamw(name, g, w, m, v):
    R, C = w.shape
    tr = _row_tile(R, C, 1 << 20)
    c1 = 1.0 - ADAM_B1 ** ADAM_STEP
    c2 = 1.0 - ADAM_B2 ** ADAM_STEP

    def body(g_ref, w_ref, m_ref, v_ref, d_ref, mo_ref, vo_ref):
        gv = g_ref[...]
        mn = ADAM_B1 * m_ref[...] + (1.0 - ADAM_B1) * gv
        vn = ADAM_B2 * v_ref[...] + (1.0 - ADAM_B2) * (gv * gv)
        mo_ref[...] = mn
        vo_ref[...] = vn
        d_ref[...] = -ADAM_LR * ((mn / c1) / (jnp.sqrt(vn / c2) + ADAM_EPS) + ADAM_WD * w_ref[...])

    blk = pl.BlockSpec((tr, C), lambda i: (i, 0))
    shp = jax.ShapeDtypeStruct((R, C), F32)
    return pl.pallas_call(body, grid=(R // tr,), in_specs=[blk, blk, blk, blk], out_specs=[blk, blk, blk], out_shape=[shp, shp, shp],
                          compiler_params=_params(("arbitrary",), 32), name=name)(g, w, m, v)


def _pad_flat(a, n):
    a = a.reshape(-1)
    return jnp.pad(a, (0, n - a.shape[0]))


def _seg(n):
    return -(-n // FLAT_ALIGN) * FLAT_ALIGN


def _to_blocks(full, axis):
    shp = full.shape
    return jnp.moveaxis(full.reshape(shp[:axis] + (N_DEV, shp[axis] // N_DEV) + shp[axis + 1:]), axis, 0)


def _from_blocks(blocks, axis):
    b = jnp.moveaxis(blocks, 0, axis)
    shp = b.shape
    return b.reshape(shp[:axis] + (shp[axis] * shp[axis + 1],) + shp[axis + 2:])


def _as_rows(shard, n):
    return shard.T if SHARD_AXIS[n] == 2 else shard


def _with_own(lands, own, me):
    return [lax.dynamic_update_slice(land, o[None], (me, 0, 0)) for land, o in zip(lands, own)]


def kernel(x, mem, positions, norm_mix_pre, norm_mix_post, w_in, pool_w, pool_scale, conv_b_w, w_branch_a, w_branch_b, w_branch_c, w_out, norm_mem_pre, norm_mem_post, norm_memkv, w_mq, w_mkv, w_mo, norm_ffn_pre, norm_ffn_post, w_up, conv_ffn_w, w_down, loss_target, m_norm_mix_pre, m_norm_mix_post, m_w_in, m_pool_w, m_pool_scale, m_conv_b_w, m_w_branch_a, m_w_branch_b, m_w_branch_c, m_w_out, m_norm_mem_pre, m_norm_mem_post, m_norm_memkv, m_w_mq, m_w_mkv, m_w_mo, m_norm_ffn_pre, m_norm_ffn_post, m_w_up, m_conv_ffn_w, m_w_down, v_norm_mix_pre, v_norm_mix_post, v_w_in, v_pool_w, v_pool_scale, v_conv_b_w, v_w_branch_a, v_w_branch_b, v_w_branch_c, v_w_out, v_norm_mem_pre, v_norm_mem_post, v_norm_memkv, v_w_mq, v_w_mkv, v_w_mo, v_norm_ffn_pre, v_norm_ffn_post, v_w_up, v_conv_ffn_w, v_w_down):
    w = dict(norm_mix_pre=norm_mix_pre, norm_mix_post=norm_mix_post, w_in=w_in, pool_w=pool_w, pool_scale=pool_scale, conv_b_w=conv_b_w, w_branch_a=w_branch_a, w_branch_b=w_branch_b, w_branch_c=w_branch_c, w_out=w_out, norm_mem_pre=norm_mem_pre, norm_mem_post=norm_mem_post, norm_memkv=norm_memkv, w_mq=w_mq, w_mkv=w_mkv, w_mo=w_mo, norm_ffn_pre=norm_ffn_pre, norm_ffn_post=norm_ffn_post, w_up=w_up, conv_ffn_w=conv_ffn_w, w_down=w_down)
    m = dict(norm_mix_pre=m_norm_mix_pre, norm_mix_post=m_norm_mix_post, w_in=m_w_in, pool_w=m_pool_w, pool_scale=m_pool_scale, conv_b_w=m_conv_b_w, w_branch_a=m_w_branch_a, w_branch_b=m_w_branch_b, w_branch_c=m_w_branch_c, w_out=m_w_out, norm_mem_pre=m_norm_mem_pre, norm_mem_post=m_norm_mem_post, norm_memkv=m_norm_memkv, w_mq=m_w_mq, w_mkv=m_w_mkv, w_mo=m_w_mo, norm_ffn_pre=m_norm_ffn_pre, norm_ffn_post=m_norm_ffn_post, w_up=m_w_up, conv_ffn_w=m_conv_ffn_w, w_down=m_w_down)
    v = dict(norm_mix_pre=v_norm_mix_pre, norm_mix_post=v_norm_mix_post, w_in=v_w_in, pool_w=v_pool_w, pool_scale=v_pool_scale, conv_b_w=v_conv_b_w, w_branch_a=v_w_branch_a, w_branch_b=v_w_branch_b, w_branch_c=v_w_branch_c, w_out=v_w_out, norm_mem_pre=v_norm_mem_pre, norm_mem_post=v_norm_mem_post, norm_memkv=v_norm_memkv, w_mq=v_w_mq, w_mkv=v_w_mkv, w_mo=v_w_mo, norm_ffn_pre=v_norm_ffn_pre, norm_ffn_post=v_norm_ffn_post, w_up=v_w_up, conv_ffn_w=v_conv_ffn_w, w_down=v_w_down)

    me = _my_slot()
    blocks = [[_as_rows(w[n][l], n).astype(MXU) for n in BIG] for l in range(DEPTH)]
    conv = jnp.concatenate([_pad_flat(w[n], _seg(w[n].size)) for n in F32_GATHERED]).reshape(-1, LANES)
    got0 = _all_gather("weights_all_gather_0", blocks[0] + [conv])
    conv_all = got0[-1].reshape(N_DEV, -1)
    small, off = {n: w[n] for n in WEIGHTS if n not in SHARD_AXIS}, 0
    for n in F32_GATHERED:
        small[n] = _from_blocks(conv_all[:, off:off + w[n].size].reshape((N_DEV,) + w[n].shape), 2)
        off += _seg(w[n].size)
    whole = lambda got: {n: o.reshape(-1, o.shape[-1]) for n, o in zip(BIG, got)}

    ctab, stab = _rope_tables(positions[0])
    push_w = _push_start("weights_push_start_1", blocks[1], False, got0[0])
    W0 = _layer_weights(whole(got0), small, 0)
    x1, sv0 = _layer_fwd(x[0], mem[0], dict(W0, g_mix_pre=W0['g_mix_pre'] + push_w[4][0, 0]), ctab, stab)
    W1 = _layer_weights(whole(_with_own(_push_wait("weights_push_wait_1", push_w, False, x1), blocks[1], me)), small, 1)
    x2, sv1 = _layer_fwd(x1, mem[0], W1, ctab, stab)
    dx, acc = _loss_head(x2, loss_target[0])
    loss = lax.psum(jnp.sum(acc) * (0.5 / D), MESH_AXES)
    grads = [None] * DEPTH
    dx, grads[1] = _layer_bwd(dx, mem[0], W1, sv1, ctab, stab)
    sent = [None, [grads[1][n].reshape(N_DEV, -1, grads[1][n].shape[-1]) for n in BIG]]
    push_g = _push_start("grads_push_start_1", sent[1], True, dx)
    dx, g_late = _layer_bwd_late(dx, mem[0], dict(W0, g_ffn_post=W0['g_ffn_post'] + push_g[4][0, 0]), sv0)
    sent_late = [g_late[n].reshape(N_DEV, -1, g_late[n].shape[-1]) for n in LATE_BIG]
    push_l = _push_start("grads_push_start_0", sent_late, True, dx)
    dx, g_mix = _layer_bwd_mix(dx, dict(W0, g_mix_post=W0['g_mix_post'] + push_l[4][0, 0]), sv0, ctab, stab)
    grads[0] = {**g_late, **g_mix}
    own = lambda s: [lax.dynamic_index_in_dim(a, me, 0, keepdims=False) for a in s]
    recv1 = _with_own(_push_wait("grads_push_wait_1", push_g, True, dx), own(sent[1]), me)
    recv_late = _with_own(_push_wait("grads_push_wait_0", push_l, True, dx), own(sent_late), me)
    mix_big = [n for n in BIG if n not in LATE_BIG]

    misc_names = [n for n in WEIGHTS if n not in BIG]
    stacked = {n: jnp.stack([grads[l][n].reshape(small[n].shape[1:]) for l in range(DEPTH)]) for n in misc_names}
    rows = [(_to_blocks(stacked[n], 2) if n in SHARD_AXIS else jnp.broadcast_to(stacked[n][None], (N_DEV,) + stacked[n].shape))
            for n in misc_names]
    segs = [_seg(w[n].size) for n in misc_names]
    misc = jnp.concatenate([jnp.pad(r.reshape(N_DEV, -1), ((0, 0), (0, s - r[0].size))) for r, s in zip(rows, segs)],
                           axis=1).reshape(N_DEV, -1, LANES)
    recv_mix = _exchange("grad_exchange_0", [g_mix[n].reshape(N_DEV, -1, g_mix[n].shape[-1]) for n in mix_big] + [misc])
    g_out, per_layer = {}, {}
    for l, names, recv in ((1, BIG, recv1), (0, LATE_BIG, recv_late), (0, mix_big, recv_mix)):
        for n, r in zip(names, recv):
            per_layer[n, l] = _as_rows(_sum_slots(f"sum_{n}_{l}", r), n)
    misc_sum = _sum_slots("sum_misc", recv_mix[-1]).reshape(-1)
    off = 0
    for n, s in zip(misc_names, segs):
        g_out[n] = misc_sum[off:off + w[n].size].reshape(w[n].shape)
        off += s
    for n in BIG:
        g_out[n] = jnp.stack([per_layer[n, l] for l in range(DEPTH)])

    res = [[], [], [], []]
    for n in WEIGHTS:
        shp = w[n].shape
        d, mn, vn = _adamw(f"adamw_{n}", *[a.reshape(-1, shp[-1]) for a in (g_out[n], w[n], m[n], v[n])])
        for k, a in enumerate((g_out[n], d, mn, vn)):
            res[k].append(a.reshape(shp))
    return (loss, dx[None], *res[0], *res[1], *res[2], *res[3])
```

```python
import jax
import jax.numpy as jnp
from jax import lax
from jax.experimental import pallas as pl
from jax.experimental.pallas import tpu as pltpu

F32 = jnp.float32
MXU = jnp.bfloat16
HI = lax.Precision.HIGHEST

D = 1024
DEPTH = 2
POOLW = 384
ATT_W = 768
ATT_O = 256
GATE_W = 3 * D
IN_W = 6912
IN_TILE = 768
IN_ROT = (IN_W - GATE_W) // IN_TILE
MEM_W = 512
D_FF = 2816
EPS = 1e-6
ROPE_THETA = 500000.0
QB = 128
DILS = (1, 4, 16)
NEG = -1e30
MEM_SCALE = 128 ** -0.5
ATT_SCALE = 0.125

ADAM_LR, ADAM_B1, ADAM_B2, ADAM_EPS, ADAM_WD, ADAM_STEP = 0.001, 0.9, 0.999, 1e-08, 0.01, 10

N_DEV = 8
MESH_AXES = ("x", "y", "c")
LANES = 128
FLAT_ALIGN = 2048
ROW_TILE = 1024

WEIGHTS = ['norm_mix_pre', 'norm_mix_post', 'w_in', 'pool_w', 'pool_scale', 'conv_b_w', 'w_branch_a', 'w_branch_b',
           'w_branch_c', 'w_out', 'norm_mem_pre', 'norm_mem_post', 'norm_memkv', 'w_mq', 'w_mkv', 'w_mo',
           'norm_ffn_pre', 'norm_ffn_post', 'w_up', 'conv_ffn_w', 'w_down']
SHARD_AXIS = {'w_in': 2, 'conv_b_w': 2, 'w_branch_a': 2, 'w_branch_b': 2, 'w_branch_c': 2, 'w_out': 1, 'w_mq': 1,
              'w_mkv': 1, 'w_mo': 2, 'w_up': 2, 'conv_ffn_w': 2, 'w_down': 1}
F32_GATHERED = ('conv_b_w', 'conv_ffn_w')
BIG = [n for n in WEIGHTS if n in SHARD_AXIS and n not in F32_GATHERED]
LATE_BIG = ['w_mq', 'w_mkv', 'w_mo', 'w_up', 'w_down']


VMEM_LIMIT_MB = 60


def _params(sem, vmem_mb):
    del vmem_mb
    return pltpu.CompilerParams(dimension_semantics=sem, vmem_limit_bytes=VMEM_LIMIT_MB << 20)


def _dot(a, b, prec=None):
    return lax.dot_general(a, b, (((1,), (0,)), ((), ())), preferred_element_type=F32, precision=prec)


def _dot_nt(a, b, prec=None):
    return lax.dot_general(a, b, (((1,), (1,)), ((), ())), preferred_element_type=F32, precision=prec)


def _dot_tn(a, b, prec=None):
    return lax.dot_general(a, b, (((0,), (0,)), ((), ())), preferred_element_type=F32, precision=prec)


def _tile(n, cap):
    if n <= cap:
        return n
    best = None
    for t in range(LANES, cap + 1, LANES):
        if n % t == 0:
            best = t
    assert best is not None, (n, cap)
    return best


def _rms(x, g):
    r = lax.rsqrt(jnp.mean(x * x, axis=-1, keepdims=True) + EPS)
    return x * r * g, r


def _rms_bwd(w, y):
    r = lax.rsqrt(jnp.mean(y * y, axis=-1, keepdims=True) + EPS)
    return r * w - y * (r * r * r) * jnp.mean(w * y, axis=-1, keepdims=True), r


def _rows_call(name, body, n_rows, ts, ins, outs, scratch=(), reverse=False, vmem_mb=48, aliases=None):
    nt = n_rows // ts
    assert nt * ts == n_rows

    def tile_of(g):
        return (nt - 1 - g) if reverse else g

    in_specs, args = [], []
    for op in ins:
        if op[0] == "t":
            _, a, cw, cb = op
            in_specs.append(pl.BlockSpec((ts, cw), lambda g, cb=cb: (tile_of(g), cb)))
        elif op[0] == "h":
            _, a, hr, cw, cb = op
            in_specs.append(pl.BlockSpec((hr, cw), lambda g, cb=cb, k=ts // hr: (jnp.maximum(tile_of(g) * k - 1, 0), cb)))
        elif op[0] == "x":
            _, a = op
            in_specs.append(pl.BlockSpec(memory_space=pl.ANY))
        else:
            _, a = op
            in_specs.append(pl.BlockSpec(a.shape, lambda g, n=a.ndim: (0,) * n))
        args.append(a)
    out_specs, out_shape = [], []
    for op in outs:
        if op[0] == "t":
            _, cols, dt = op
            out_specs.append(pl.BlockSpec((ts, cols), lambda g: (tile_of(g), 0)))
            out_shape.append(jax.ShapeDtypeStruct((n_rows, cols), dt))
        elif op[0] == "c":
            _, total, cols, cb, dt = op
            out_specs.append(pl.BlockSpec((ts, cols), lambda g, cb=cb: (tile_of(g), cb)))
            out_shape.append(jax.ShapeDtypeStruct((n_rows, total), dt))
        else:
            _, shp, dt = op
            out_specs.append(pl.BlockSpec(shp, lambda g, n=len(shp): (0,) * n))
            out_shape.append(jax.ShapeDtypeStruct(shp, dt))

    def kern(*refs):
        g = pl.program_id(0)
        body(tile_of(g), g, *refs)

    return pl.pallas_call(kern, grid=(nt,), in_specs=in_specs, out_specs=out_specs, out_shape=out_shape,
                          scratch_shapes=list(scratch), input_output_aliases=aliases or {},
                          compiler_params=_params(("arbitrary",), vmem_mb), name=name)(*args)


def _acc(ref, g, val):
    @pl.when(g == 0)
    def _():
        ref[...] = val

    @pl.when(g != 0)
    def _():
        ref[...] += val


def _norm_mm(name, x, g, w, ts, tn, out_dtype=F32, wt=False, rot=0):
    S, K = x.shape
    N = w.shape[0] if wt else w.shape[1]
    assert wt or not rot

    def body(x_ref, g_ref, w_ref, o_ref, h_ref, hs):
        @pl.when(pl.program_id(1) == 0)
        def _():
            h, _ = _rms(x_ref[...], g_ref[...])
            hs[...] = h.astype(MXU)
            h_ref[...] = h.astype(MXU)

        o_ref[...] = (_dot_nt if wt else _dot)(hs[...], w_ref[...]).astype(out_dtype)

    w_spec = pl.BlockSpec((tn, K), lambda i, j: ((j + rot) % (N // tn), 0)) if wt else pl.BlockSpec((K, tn), lambda i, j: (0, j))
    return pl.pallas_call(
        body, grid=(S // ts, N // tn),
        in_specs=[pl.BlockSpec((ts, K), lambda i, j: (i, 0)), pl.BlockSpec((1, K), lambda i, j: (0, 0)), w_spec],
        out_specs=[pl.BlockSpec((ts, tn), lambda i, j: (i, j)), pl.BlockSpec((ts, K), lambda i, j: (i, 0))],
        out_shape=[jax.ShapeDtypeStruct((S, N), out_dtype), jax.ShapeDtypeStruct((S, K), MXU)],
        scratch_shapes=[pltpu.VMEM((ts, K), MXU)],
        compiler_params=_params(("arbitrary", "arbitrary"), 48), name=name)(x, g, w)


def _mm_nt(name, a, b, ts, tn, out_dtype=F32):
    M, K = a.shape
    N = b.shape[0]

    def body(a_ref, b_ref, o_ref):
        o_ref[...] = _dot_nt(a_ref[...], b_ref[...]).astype(out_dtype)

    return pl.pallas_call(
        body, grid=(M // ts, N // tn),
        in_specs=[pl.BlockSpec((ts, K), lambda i, j: (i, 0)), pl.BlockSpec((tn, K), lambda i, j: (j, 0))],
        out_specs=pl.BlockSpec((ts, tn), lambda i, j: (i, j)), out_shape=jax.ShapeDtypeStruct((M, N), out_dtype),
        compiler_params=_params(("arbitrary", "arbitrary"), 48), name=name)(a, b)


def _mm_tn(name, a, b, cap_k=512, cap_n=1024, out_dtype=MXU, rot=0):
    S, K = a.shape
    N = b.shape[1]
    tk, tn = _tile(K, cap_k), _tile(N, cap_n)

    def body(a_ref, b_ref, o_ref):
        o_ref[...] = _dot_tn(a_ref[...], b_ref[...]).astype(out_dtype)

    return pl.pallas_call(
        body, grid=(K // tk, N // tn),
        in_specs=[pl.BlockSpec((S, tk), lambda i, j: (0, i)), pl.BlockSpec((S, tn), lambda i, j: (0, j))],
        out_specs=pl.BlockSpec((tk, tn), lambda i, j: ((i + rot) % (K // tk), j)), out_shape=jax.ShapeDtypeStruct((K, N), out_dtype),
        compiler_params=_params(("arbitrary", "arbitrary"), 48), name=name)(a, b)


def _pool_cols(shape):
    col = lax.broadcasted_iota(jnp.int32, shape, 1)
    return col < 96, col < 192, col < 288


def _pool_select(s2, s4, s8, s16):
    c1, c2, c3 = _pool_cols(s2.shape)
    return jnp.where(c1, s2, jnp.where(c2, s4, jnp.where(c3, s8, s16)))


def _pool_cnt(t0, ts):
    c1, c2, c3 = _pool_cols((ts, POOLW))
    win = jnp.where(c1, 2, jnp.where(c2, 4, jnp.where(c3, 8, 16)))
    t = t0 + lax.broadcasted_iota(jnp.int32, (ts, POOLW), 0)
    return jnp.minimum(t + 1, win).astype(F32)


def _pooled(a, prev, t0):
    ts = a.shape[0]
    ext = jnp.concatenate([prev, a], axis=0)
    s2 = ext + pltpu.roll(ext, 1, axis=0)
    s4 = s2 + pltpu.roll(s2, 2, axis=0)
    s8 = s4 + pltpu.roll(s4, 4, axis=0)
    s16 = s8 + pltpu.roll(s8, 8, axis=0)
    sums = _pool_select(s2, s4, s8, s16)[16:]
    return sums / _pool_cnt(t0, ts) - a


def _conv3(z, prev8, w):
    ext = jnp.concatenate([prev8, z], axis=0)
    z1 = pltpu.roll(ext, 1, axis=0)[8:]
    z2 = pltpu.roll(ext, 2, axis=0)[8:]
    return w[0:1] * z2 + w[1:2] * z1 + w[2:3] * z, z1, z2


def _conv3_t(dc, next8, w):
    ts = dc.shape[0]
    ext = jnp.concatenate([dc, next8], axis=0)
    n = ts + 8
    u1 = pltpu.roll(ext, n - 1, axis=0)[:ts]
    u2 = pltpu.roll(ext, n - 2, axis=0)[:ts]
    return w[2:3] * dc + w[1:2] * u1 + w[0:1] * u2


def _poolconv_fwd(u, wblk, pool_scale, conv_b, ts=256):
    S = u.shape[0]

    def body(i, g, a_ref, bx_ref, bb_ref, bc_ref, wblk_ref, ps_ref, cw_ref, a2_ref, yb_ref, ca, cz):
        @pl.when(g == 0)
        def _():
            ca[...] = jnp.zeros_like(ca)
            cz[...] = jnp.zeros_like(cz)

        a = a_ref[...]
        p = _pooled(a, ca[...], i * ts)
        mixed = _dot(p.astype(MXU), wblk_ref[...])
        a2_ref[...] = (mixed * ps_ref[...]).astype(MXU)
        z = bc_ref[...] * bx_ref[...]
        conv, _, _ = _conv3(z, cz[...], cw_ref[...])
        yb_ref[...] = (bb_ref[...] * conv).astype(MXU)
        ca[...] = a[ts - 16:]
        cz[...] = z[ts - 8:]

    ins = [("t", u, POOLW, 8), ("t", u, POOLW, 9), ("t", u, POOLW, 10), ("t", u, POOLW, 11), ("w", wblk), ("w", pool_scale),
           ("w", conv_b)]
    return _rows_call("poolconv_fwd", body, S, ts, ins, [("t", POOLW, MXU), ("t", POOLW, MXU)],
                      scratch=[pltpu.VMEM((16, POOLW), F32), pltpu.VMEM((8, POOLW), F32)])


def _poolconv_bwd(u, d_a2, d_yb, du, wblk, pool_scale, conv_b, ts=256):
    S = u.shape[0]

    def body(i, g, a_ref, bx_ref, bb_ref, bc_ref, ap_ref, bxp_ref, bcp_ref, da2_ref, dyb_ref, wblk_ref, ps_ref, cw_ref, _,
             o_ref, dps_ref, dwb_ref, dcw_ref, ce, cdz):
        @pl.when(g == 0)
        def _():
            ce[...] = jnp.zeros_like(ce)
            cdz[...] = jnp.zeros_like(cdz)

        first = (i > 0).astype(F32)
        a = a_ref[...]
        p = _pooled(a, ap_ref[...] * first, i * ts)
        pb = p.astype(MXU)
        mixed = _dot(pb, wblk_ref[...])
        da2 = da2_ref[...]
        dmixed = (da2 * ps_ref[...]).astype(MXU)
        dp = _dot_nt(dmixed, wblk_ref[...])
        _acc(dps_ref, g, jnp.sum(da2 * mixed, axis=0, keepdims=True))
        _acc(dwb_ref, g, _dot_tn(pb, dmixed))
        e = dp / _pool_cnt(i * ts, ts)
        ext = jnp.concatenate([e, ce[...]], axis=0)
        n = ts + 16
        f2 = ext + pltpu.roll(ext, n - 1, axis=0)
        f4 = f2 + pltpu.roll(f2, n - 2, axis=0)
        f8 = f4 + pltpu.roll(f4, n - 4, axis=0)
        f16 = f8 + pltpu.roll(f8, n - 8, axis=0)
        o_ref[:, 0:POOLW] = (_pool_select(f2, f4, f8, f16)[:ts] - dp).astype(o_ref.dtype)
        ce[...] = e[:16]

        bx, bb, bc = bx_ref[...], bb_ref[...], bc_ref[...]
        z = bc * bx
        w = cw_ref[...]
        conv, z1, z2 = _conv3(z, bxp_ref[...] * bcp_ref[...] * first, w)
        dyb = dyb_ref[...]
        dconv = dyb * bb
        dz = _conv3_t(dconv, cdz[...], w)
        o_ref[:, POOLW:2 * POOLW] = (dz * bc).astype(o_ref.dtype)
        o_ref[:, 2 * POOLW:3 * POOLW] = (dyb * conv).astype(o_ref.dtype)
        o_ref[:, 3 * POOLW:4 * POOLW] = (dz * bx).astype(o_ref.dtype)
        dw = jnp.concatenate([jnp.sum(dconv * z2, axis=0, keepdims=True), jnp.sum(dconv * z1, axis=0, keepdims=True),
                              jnp.sum(dconv * z, axis=0, keepdims=True)], axis=0)
        _acc(dcw_ref, g, dw)
        cdz[...] = dconv[:8]

    ins = [("t", u, POOLW, 8), ("t", u, POOLW, 9), ("t", u, POOLW, 10), ("t", u, POOLW, 11),
           ("h", u, 16, POOLW, 8), ("h", u, 8, POOLW, 9), ("h", u, 8, POOLW, 11),
           ("t", d_a2, POOLW, 0), ("t", d_yb, POOLW, 0), ("w", wblk), ("w", pool_scale), ("w", conv_b), ("x", du)]
    outs = [("c", IN_W, 4 * POOLW, GATE_W // (4 * POOLW), MXU), ("a", (1, POOLW), F32), ("a", (POOLW, POOLW), F32), ("a", (3, POOLW), F32)]
    return _rows_call("poolconv_bwd", body, S, ts, ins, outs, aliases={len(ins) - 1: 0},
                      scratch=[pltpu.VMEM((16, POOLW), F32), pltpu.VMEM((8, POOLW), F32)], reverse=True)


def _rope_tables(positions):
    S = positions.shape[0]
    inv = ROPE_THETA ** (-jnp.arange(0, 16, 2, dtype=F32) / 16)
    ang = positions.astype(F32)[:, None] * inv
    cos, sin = jnp.cos(ang), jnp.sin(ang)
    c64 = jnp.concatenate([cos, cos, jnp.ones((S, 48), F32)], axis=1)
    s64 = jnp.concatenate([-sin, sin, jnp.zeros((S, 48), F32)], axis=1)
    return jnp.concatenate([c64, c64], axis=1), jnp.concatenate([s64, s64], axis=1)


def _partner(x):
    lane = lax.broadcasted_iota(jnp.int32, x.shape, 1) % 64
    return jnp.where(lane < 8, pltpu.roll(x, LANES - 8, axis=1), jnp.where(lane < 16, pltpu.roll(x, 8, axis=1), 0.0))


def _rope(x, c, s):
    return x * c + _partner(x) * s


def _rope_t(x, c, s):
    return x * c + _partner(x * s)


def _rows_of(r, n, d):
    return pl.ds(r, n, stride=d) if d > 1 else pl.ds(0, n)


def _head_masks(shape):
    lane = lax.broadcasted_iota(jnp.int32, shape, 1) // 64
    return [lane == h for h in range(4)]


def _only(mask, x):
    return jnp.where(mask, x, jnp.zeros_like(x))


def _rope_perm(u, ctab, stab, ts=256):
    S = u.shape[0]
    nch = ATT_W // LANES

    def body(*refs):
        chunks, (c_ref, s_ref), outs = refs[:3 * nch], refs[3 * nch:3 * nch + 2], refs[3 * nch + 2:]
        for g, d in enumerate(DILS):
            n = ts // d
            for r in range(d):
                rows = _rows_of(r, n, d)
                c, s = c_ref[rows, :], s_ref[rows, :]
                for which in range(3):
                    parts = [chunks[which * nch + j][rows, :] for j in (2 * g, 2 * g + 1)]
                    if which < 2:
                        parts = [_rope(x, c, s) for x in parts]
                    outs[which * 3 + g][r] = jnp.concatenate(parts, axis=1).astype(MXU)

    base = (IN_W - 3 * ATT_W) // LANES
    in_specs = [pl.BlockSpec((ts, LANES), lambda i, cb=base + k: (i, cb)) for k in range(3 * nch)]
    in_specs += [pl.BlockSpec((ts, LANES), lambda i: (i, 0))] * 2
    out_specs = [pl.BlockSpec((d, ts // d, ATT_O), lambda i: (0, i, 0)) for _ in range(3) for d in DILS]
    out_shape = [jax.ShapeDtypeStruct((d, S // d, ATT_O), MXU) for _ in range(3) for d in DILS]
    res = pl.pallas_call(body, grid=(S // ts,), in_specs=in_specs, out_specs=out_specs, out_shape=out_shape,
                         compiler_params=_params(("arbitrary",), 32), name="rope_perm")(*([u] * (3 * nch)), ctab, stab)
    return [[res[which * 3 + g].reshape(S, ATT_O) for g in range(3)] for which in range(3)]


def _rope_unperm_bwd(dqkv, du, ctab, stab, ts=256):
    S = dqkv[0][0].shape[0]
    nch = ATT_W // LANES

    def body(*refs):
        ins, (c_ref, s_ref, _, o_ref, scr) = refs[:9], refs[9:]
        for g, d in enumerate(DILS):
            n = ts // d
            for r in range(d):
                rows = _rows_of(r, n, d)
                c, s = c_ref[rows, :], s_ref[rows, :]
                for which in range(3):
                    v = ins[which * 3 + g][r]
                    for half in range(2):
                        x = v[:, half * LANES:(half + 1) * LANES]
                        scr.at[which * nch + 2 * g + half][rows, :] = _rope_t(x, c, s) if which < 2 else x
        for j in range(3 * nch):
            o_ref[:, j * LANES:(j + 1) * LANES] = scr[j].astype(o_ref.dtype)

    in_specs = [pl.BlockSpec((d, ts // d, ATT_O), lambda i: (0, i, 0)) for _ in range(3) for d in DILS]
    in_specs += [pl.BlockSpec((ts, LANES), lambda i: (i, 0))] * 2 + [pl.BlockSpec(memory_space=pl.ANY)]
    args = [dqkv[which][g].reshape(d, S // d, ATT_O) for which in range(3) for g, d in enumerate(DILS)]
    last = (IN_W - 3 * ATT_W) // (3 * ATT_W)
    return pl.pallas_call(body, grid=(S // ts,), in_specs=in_specs, out_specs=pl.BlockSpec((ts, 3 * ATT_W), lambda i: (i, last)),
                          out_shape=jax.ShapeDtypeStruct((S, IN_W), MXU), scratch_shapes=[pltpu.VMEM((3 * nch, ts, LANES), F32)],
                          input_output_aliases={len(in_specs) - 1: 0},
                          compiler_params=_params(("arbitrary",), 32), name="rope_unperm_bwd")(*args, ctab, stab, du)


def _band_mask_keys(has_prev):
    r = lax.broadcasted_iota(jnp.int32, (QB, 2 * QB), 0)
    c = lax.broadcasted_iota(jnp.int32, (QB, 2 * QB), 1)
    return ((c < QB) & (c >= r) & has_prev) | ((c >= QB) & (c - QB <= r))


def _band_mask_queries(has_next):
    r = lax.broadcasted_iota(jnp.int32, (2 * QB, QB), 0)
    c = lax.broadcasted_iota(jnp.int32, (2 * QB, QB), 1)
    return ((r < QB) & (c <= r)) | ((r >= QB) & (c >= r - QB) & has_next)


ASUB = 4
_BIG = pl.BlockSpec((ASUB * QB, ATT_O), lambda b: (b, 0))
_PREV = pl.BlockSpec((QB, ATT_O), lambda b: (jnp.maximum(b * ASUB - 1, 0), 0))


def _sub(ref, j):
    return ref[j * QB:(j + 1) * QB]


def _attn_fwd(g, q, k, v):
    S = q.shape[0]
    nb = S // QB
    nblk = nb // DILS[g]

    def body(q_ref, kc_ref, kp_ref, vc_ref, vp_ref, o_ref, m_ref, l_ref):
        hm_kv, hm_o = _head_masks((2 * QB, ATT_O)), _head_masks((QB, ATT_O))
        for j in range(ASUB):
            ok = _band_mask_keys(((pl.program_id(0) * ASUB + j) & (nblk - 1)) > 0)
            k2 = jnp.concatenate([kp_ref[...] if j == 0 else _sub(kc_ref, j - 1), _sub(kc_ref, j)], axis=0)
            v2 = jnp.concatenate([vp_ref[...] if j == 0 else _sub(vc_ref, j - 1), _sub(vc_ref, j)], axis=0)
            qv = _sub(q_ref, j)
            o_acc = jnp.zeros((QB, ATT_O), F32)
            m_acc = jnp.zeros((QB, ATT_O), F32)
            l_acc = jnp.zeros((QB, ATT_O), F32)
            for h in range(4):
                s = jnp.where(ok, _dot_nt(qv, _only(hm_kv[h], k2)) * ATT_SCALE, NEG)
                m = jnp.max(s, axis=1, keepdims=True)
                p = jnp.exp(s - m)
                o_acc = o_acc + _dot(p.astype(MXU), _only(hm_kv[h], v2))
                m_acc = jnp.where(hm_o[h], m, m_acc)
                l_acc = jnp.where(hm_o[h], jnp.sum(p, axis=1, keepdims=True), l_acc)
            o_ref[j * QB:(j + 1) * QB] = o_acc
            m_ref[j * QB:(j + 1) * QB] = m_acc
            l_ref[j * QB:(j + 1) * QB] = l_acc

    shp = jax.ShapeDtypeStruct((S, ATT_O), F32)
    return pl.pallas_call(body, grid=(nb // ASUB,), in_specs=[_BIG, _BIG, _PREV, _BIG, _PREV],
                          out_specs=[_BIG] * 3, out_shape=[shp, shp, shp], compiler_params=_params(("arbitrary",), 32),
                          name=f"attn_fwd_{g}")(q, k, k, v, v)


def _natural(ref, d, scr, ts):
    if d == 1:
        return ref[0]
    n = ts // d
    for r in range(d):
        v = ref[r]
        scr.at[0][pl.ds(r, n, stride=d), :] = v[:, 0:LANES]
        scr.at[1][pl.ds(r, n, stride=d), :] = v[:, LANES:2 * LANES]
    return jnp.concatenate([scr[0], scr[1]], axis=1)


def _attn_combine(oml, ts=256):
    S = oml[0][0].shape[0]

    def body(*refs):
        ins, (att_ref, out_ref, lse_ref, scr) = refs[:9], refs[9:]
        o, m, l = [[_natural(ins[3 * g + k], d, scr, ts) for g, d in enumerate(DILS)] for k in range(3)]
        mx = jnp.maximum(jnp.maximum(m[0], m[1]), m[2])
        w = [jnp.exp(m[g] - mx) for g in range(3)]
        den = w[0] * l[0] + w[1] * l[1] + w[2] * l[2]
        out = (w[0] * o[0] + w[1] * o[1] + w[2] * o[2]) / den
        out_ref[...] = out
        att_ref[...] = out.astype(MXU)
        lse_ref[...] = mx + jnp.log(den)

    in_specs = [pl.BlockSpec((d, ts // d, ATT_O), lambda i: (0, i, 0)) for d in DILS for _ in range(3)]
    args = [a.reshape(d, S // d, ATT_O) for d, grp in zip(DILS, oml) for a in grp]
    blk = pl.BlockSpec((ts, ATT_O), lambda i: (i, 0))
    return pl.pallas_call(body, grid=(S // ts,), in_specs=in_specs, out_specs=[blk, blk, blk],
                          out_shape=[jax.ShapeDtypeStruct((S, ATT_O), MXU), jax.ShapeDtypeStruct((S, ATT_O), F32),
                                     jax.ShapeDtypeStruct((S, ATT_O), F32)],
                          scratch_shapes=[pltpu.VMEM((2, ts, LANES), F32)], compiler_params=_params(("arbitrary",), 32),
                          name="attn_combine")(*args)


def _attn_bwd_prep(datt, o, lse, ts=256):
    S = datt.shape[0]

    def body(da0, da1, o_ref, l0, l1, *rest):
        outs, dl = rest[:9], rest[9]
        prod = jnp.concatenate([da0[...], da1[...]], axis=1) * o_ref[...]
        delta = jnp.zeros((ts, ATT_O), F32)
        for hm in _head_masks((ts, ATT_O)):
            delta = jnp.where(hm, jnp.sum(_only(hm, prod), axis=1, keepdims=True), delta)
        dl[0] = delta[:, 0:LANES]
        dl[1] = delta[:, LANES:2 * LANES]
        for g, d in enumerate(DILS):
            n = ts // d
            for r in range(d):
                rows = _rows_of(r, n, d)
                outs[g][r] = jnp.concatenate([da0[rows, :], da1[rows, :]], axis=1).astype(MXU)
                outs[3 + g][r] = jnp.concatenate([dl.at[0][rows, :], dl.at[1][rows, :]], axis=1)
                outs[6 + g][r] = jnp.concatenate([l0[rows, :], l1[rows, :]], axis=1)

    half = lambda j: pl.BlockSpec((ts, LANES), lambda i: (i, j))
    out_specs = [pl.BlockSpec((d, ts // d, ATT_O), lambda i: (0, i, 0)) for _ in range(3) for d in DILS]
    out_shape = [jax.ShapeDtypeStruct((d, S // d, ATT_O), dt) for dt in (MXU, F32, F32) for d in DILS]
    res = pl.pallas_call(body, grid=(S // ts,), in_specs=[half(0), half(1), pl.BlockSpec((ts, ATT_O), lambda i: (i, 0)), half(0), half(1)],
                         out_specs=out_specs, out_shape=out_shape, scratch_shapes=[pltpu.VMEM((2, ts, LANES), F32)],
                         compiler_params=_params(("arbitrary",), 32), name="attn_bwd_prep")(datt, datt, o, lse, lse)
    return [[res[k * 3 + g].reshape(S, ATT_O) for g in range(3)] for k in range(3)]


def _head_col(x, h):
    return x[:, h * 64:h * 64 + 1]


def _attn_dq(g, q, k, v, do, delta, lse):
    S = q.shape[0]
    nb = S // QB
    nblk = nb // DILS[g]

    def body(q_ref, kc_ref, kp_ref, vc_ref, vp_ref, do_ref, dl_ref, lse_ref, dq_ref):
        hms = _head_masks((2 * QB, ATT_O))
        for j in range(ASUB):
            ok = _band_mask_keys(((pl.program_id(0) * ASUB + j) & (nblk - 1)) > 0)
            k2 = jnp.concatenate([kp_ref[...] if j == 0 else _sub(kc_ref, j - 1), _sub(kc_ref, j)], axis=0)
            v2 = jnp.concatenate([vp_ref[...] if j == 0 else _sub(vc_ref, j - 1), _sub(vc_ref, j)], axis=0)
            qv, dov, dl, lse_v = _sub(q_ref, j), _sub(do_ref, j), _sub(dl_ref, j), _sub(lse_ref, j)
            dq = jnp.zeros((QB, ATT_O), F32)
            for h, hm in enumerate(hms):
                kh = _only(hm, k2)
                p = jnp.where(ok, jnp.exp(_dot_nt(qv, kh) * ATT_SCALE - _head_col(lse_v, h)), 0.0)
                ds = p * (_dot_nt(dov, _only(hm, v2)) - _head_col(dl, h))
                dq = dq + _dot(ds.astype(MXU), kh)
            dq_ref[j * QB:(j + 1) * QB] = dq * ATT_SCALE

    return pl.pallas_call(body, grid=(nb // ASUB,), in_specs=[_BIG, _BIG, _PREV, _BIG, _PREV, _BIG, _BIG, _BIG], out_specs=_BIG,
                          out_shape=jax.ShapeDtypeStruct((S, ATT_O), F32),
                          compiler_params=_params(("arbitrary",), 32), name=f"attn_dq_{g}")(q, k, k, v, v, do, delta, lse)


def _attn_dkv(g, q, k, v, do, delta, lse):
    S = q.shape[0]
    nb = S // QB
    nblk = nb // DILS[g]

    def body(k_ref, v_ref, qc_ref, qn_ref, doc_ref, don_ref, dlc_ref, dln_ref, lc_ref, ln_ref, dk_ref, dv_ref):
        hms = _head_masks((2 * QB, ATT_O))

        def both(cur_ref, nxt_ref, j):
            return jnp.concatenate([_sub(cur_ref, j), nxt_ref[...] if j == ASUB - 1 else _sub(cur_ref, j + 1)], axis=0)

        for j in range(ASUB):
            ok = _band_mask_queries(((pl.program_id(0) * ASUB + j + 1) & (nblk - 1)) > 0)
            q2, do2, dl2, lse2 = both(qc_ref, qn_ref, j), both(doc_ref, don_ref, j), both(dlc_ref, dln_ref, j), both(lc_ref, ln_ref, j)
            kv, vv = _sub(k_ref, j), _sub(v_ref, j)
            dk = jnp.zeros((QB, ATT_O), F32)
            dv = jnp.zeros((QB, ATT_O), F32)
            for h, hm in enumerate(hms):
                qh, doh = _only(hm, q2), _only(hm, do2)
                p = jnp.where(ok, jnp.exp(_dot_nt(qh, kv) * ATT_SCALE - _head_col(lse2, h)), 0.0)
                ds = p * (_dot_nt(doh, vv) - _head_col(dl2, h))
                dv = dv + _dot_tn(p.astype(MXU), doh)
                dk = dk + _dot_tn(ds.astype(MXU), qh)
            dk_ref[j * QB:(j + 1) * QB] = dk * ATT_SCALE
            dv_ref[j * QB:(j + 1) * QB] = dv

    nxt = pl.BlockSpec((QB, ATT_O), lambda b: (jnp.minimum((b + 1) * ASUB, nb - 1), 0))
    shp = jax.ShapeDtypeStruct((S, ATT_O), F32)
    return pl.pallas_call(body, grid=(nb // ASUB,), in_specs=[_BIG, _BIG, _BIG, nxt, _BIG, nxt, _BIG, nxt, _BIG, nxt], out_specs=[_BIG, _BIG],
                          out_shape=[shp, shp], compiler_params=_params(("arbitrary",), 32),
                          name=f"attn_dkv_{g}")(k, v, q, q, do, do, delta, delta, lse, lse)


def _merge_fwd(x0, u, a2, yb, att, wa, wb, wc, w_out, g_post, ts=256):
    S = x0.shape[0]

    def body(i, g, x_ref, gate_ref, a2_ref, yb_ref, att_ref, wa_ref, wb_ref, wc_ref, wo_ref, gp_ref, mg_ref, y_ref, xo_ref):
        merged = jax.nn.sigmoid(gate_ref[:, 0:D]) * _dot_nt(a2_ref[...], wa_ref[...])
        merged = merged + jax.nn.sigmoid(gate_ref[:, D:2 * D]) * _dot_nt(yb_ref[...], wb_ref[...])
        merged = merged + jax.nn.sigmoid(gate_ref[:, 2 * D:3 * D]) * _dot_nt(att_ref[...], wc_ref[...])
        mb = merged.astype(MXU)
        mg_ref[...] = mb
        y = _dot(mb, wo_ref[...])
        y_ref[...] = y
        xo_ref[...] = x_ref[...] + _rms(y, gp_ref[...])[0]

    ins = [("t", x0, D, 0), ("t", u, GATE_W, 0), ("t", a2, POOLW, 0), ("t", yb, POOLW, 0), ("t", att, ATT_O, 0),
           ("w", wa), ("w", wb), ("w", wc), ("w", w_out), ("w", g_post)]
    return _rows_call("merge_fwd", body, S, ts, ins, [("t", D, MXU), ("t", D, F32), ("t", D, F32)])


def _merge_bwd(dx, y1, u, a2, yb, att, wa, wb, wc, w_out, g_post, ts=256):
    S = dx.shape[0]

    def body(i, g, dx_ref, y_ref, gate_ref, a2_ref, yb_ref, att_ref, wa_ref, wb_ref, wc_ref, wo_ref, gp_ref,
             dy_ref, dgate_ref, dbra_ref, dbrb_ref, dbrc_ref, da2_ref, dyb_ref, datt_ref, dgp_ref):
        dxv, y = dx_ref[...], y_ref[...]
        dy, r = _rms_bwd(dxv * gp_ref[...], y)
        _acc(dgp_ref, g, jnp.sum(dxv * (y * r), axis=0, keepdims=True))
        dyb16 = dy.astype(MXU)
        dy_ref[...] = dyb16
        dm = _dot_nt(dyb16, wo_ref[...])
        for n, (src, w_ref, dbr_ref, din_ref) in enumerate(((a2_ref, wa_ref, dbra_ref, da2_ref), (yb_ref, wb_ref, dbrb_ref, dyb_ref),
                                                           (att_ref, wc_ref, dbrc_ref, datt_ref))):
            gt = jax.nn.sigmoid(gate_ref[:, n * D:(n + 1) * D])
            br = _dot_nt(src[...], w_ref[...])
            dgate_ref[:, n * D:(n + 1) * D] = (dm * br * gt * (1.0 - gt)).astype(dgate_ref.dtype)
            dbr = (dm * gt).astype(MXU)
            dbr_ref[...] = dbr
            din_ref[...] = _dot(dbr, w_ref[...])

    ins = [("t", dx, D, 0), ("t", y1, D, 0), ("t", u, GATE_W, 0), ("t", a2, POOLW, 0), ("t", yb, POOLW, 0), ("t", att, ATT_O, 0),
           ("w", wa), ("w", wb), ("w", wc), ("w", w_out), ("w", g_post)]
    outs = [("t", D, MXU), ("c", IN_W, GATE_W, 0, MXU), ("t", D, MXU), ("t", D, MXU), ("t", D, MXU), ("t", POOLW, F32), ("t", POOLW, F32),
            ("t", ATT_O, F32), ("a", (1, D), F32)]
    return _rows_call("merge_bwd", body, S, ts, ins, outs)


def _prenorm_bwd(name, dx_res, du, wt, x, g_pre, ts=256, lead=0):
    S = x.shape[0]
    N = du.shape[1]

    def body(i, g, dx_ref, du_ref, wt_ref, x_ref, g_ref, o_ref, dg_ref):
        if lead:
            dhv = _dot(du_ref[:, 0:lead], wt_ref[N - lead:N, :]) + _dot(du_ref[:, lead:N], wt_ref[0:N - lead, :])
        else:
            dhv = _dot(du_ref[...], wt_ref[...])
        xv = x_ref[...]
        dxn, r = _rms_bwd(dhv * g_ref[...], xv)
        o_ref[...] = dx_ref[...] + dxn
        _acc(dg_ref, g, jnp.sum(dhv * (xv * r), axis=0, keepdims=True))

    ins = [("t", dx_res, D, 0), ("t", du, N, 0), ("w", wt), ("t", x, D, 0), ("w", g_pre)]
    return _rows_call(name, body, S, ts, ins, [("t", D, F32), ("a", (1, D), F32)], vmem_mb=52)


def _mem_heads(qm, kv_ref):
    out = []
    for h in range(4):
        q = qm[:, h * 128:(h + 1) * 128].astype(MXU)
        k = kv_ref[:, h * 128:(h + 1) * 128]
        v = kv_ref[:, MEM_W + h * 128:MEM_W + (h + 1) * 128]
        sc = _dot_nt(q, k) * MEM_SCALE
        e = jnp.exp(sc - jnp.max(sc, axis=1, keepdims=True))
        out.append((e / jnp.sum(e, axis=1, keepdims=True), q, k, v))
    return out


def _mem_fwd(x1, kv, g_pre, w_mq, w_mo, g_post, ts=256):
    S = x1.shape[0]

    def body(i, g, x_ref, kv_ref, gq_ref, wq_ref, wo_ref, gp_ref, om_ref, h_ref, y_ref, xo_ref):
        x = x_ref[...]
        hb = _rms(x, gq_ref[...])[0].astype(MXU)
        h_ref[...] = hb
        qm = _dot(hb, wq_ref[...])
        om = jnp.concatenate([_dot(p.astype(MXU), v) for p, _, _, v in _mem_heads(qm, kv_ref)], axis=1).astype(MXU)
        om_ref[...] = om
        y = _dot_nt(om, wo_ref[...])
        y_ref[...] = y
        xo_ref[...] = x + _rms(y, gp_ref[...])[0]

    ins = [("t", x1, D, 0), ("w", kv), ("w", g_pre), ("w", w_mq), ("w", w_mo), ("w", g_post)]
    return _rows_call("mem_fwd", body, S, ts, ins, [("t", MEM_W, MXU), ("t", D, MXU), ("t", D, F32), ("t", D, F32)])


def _mem_bwd(dx2, ym, x1, kv, g_pre, w_mq, w_mo, g_post, ts=256):
    S = x1.shape[0]

    def body(i, g, dx_ref, y_ref, x_ref, kv_ref, gq_ref, wq_ref, wo_ref, gp_ref, dy_ref, dq_ref, dxo_ref, dgp_ref, dgq_ref, dkv_ref):
        dxv, y, x = dx_ref[...], y_ref[...], x_ref[...]
        dy, r = _rms_bwd(dxv * gp_ref[...], y)
        _acc(dgp_ref, g, jnp.sum(dxv * (y * r), axis=0, keepdims=True))
        dyb = dy.astype(MXU)
        dy_ref[...] = dyb
        dom = _dot(dyb, wo_ref[...])
        h, r1 = _rms(x, gq_ref[...])
        qm = _dot(h.astype(MXU), wq_ref[...])
        dqs = []

        @pl.when(g == 0)
        def _():
            dkv_ref[...] = jnp.zeros_like(dkv_ref)

        for hh, (p, q, k, v) in enumerate(_mem_heads(qm, kv_ref)):
            doh = dom[:, hh * 128:(hh + 1) * 128].astype(MXU)
            dp = _dot_nt(doh, v)
            dsc = (p * (dp - jnp.sum(dp * p, axis=1, keepdims=True)) * MEM_SCALE).astype(MXU)
            dqs.append(_dot(dsc, k))
            dkv_ref[:, hh * 128:(hh + 1) * 128] += _dot_tn(dsc, q)
            dkv_ref[:, MEM_W + hh * 128:MEM_W + (hh + 1) * 128] += _dot_tn(p.astype(MXU), doh)
        dq = jnp.concatenate(dqs, axis=1).astype(MXU)
        dq_ref[...] = dq
        dh = _dot_nt(dq, wq_ref[...])
        _acc(dgq_ref, g, jnp.sum(dh * (x * r1), axis=0, keepdims=True))
        dxo_ref[...] = dxv + _rms_bwd(dh * gq_ref[...], x)[0]

    ins = [("t", dx2, D, 0), ("t", ym, D, 0), ("t", x1, D, 0), ("w", kv), ("w", g_pre), ("w", w_mq), ("w", w_mo), ("w", g_post)]
    outs = [("t", D, MXU), ("t", MEM_W, MXU), ("t", D, F32), ("a", (1, D), F32), ("a", (1, D), F32), ("a", (256, D), F32)]
    return _rows_call("mem_bwd", body, S, ts, ins, outs)


def _gain_grad(name, dn, x):
    n = x.shape[0]

    def body(i, g, dn_ref, x_ref, o_ref):
        xv = x_ref[...]
        r = lax.rsqrt(jnp.mean(xv * xv, axis=-1, keepdims=True) + EPS)
        o_ref[...] = jnp.sum(dn_ref[...] * (xv * r), axis=0, keepdims=True)

    return _rows_call(name, body, n, n, [("t", dn, D, 0), ("t", x, D, 0)], [("a", (1, D), F32)])[0]


def _ffn_fwd(x2, u3, conv_f, w_down, g_post, ts=256):
    S = x2.shape[0]

    def body(i, g, x_ref, ua_ref, ub_ref, cw_ref, wd_ref, gp_ref, act_ref, y_ref, xo_ref, cu):
        @pl.when(g == 0)
        def _():
            cu[...] = jnp.zeros_like(cu)

        ua = ua_ref[...]
        c, _, _ = _conv3(ua, cu[...], cw_ref[...])
        act = (c * jax.nn.sigmoid(c) * ub_ref[...]).astype(MXU)
        act_ref[...] = act
        y = _dot(act, wd_ref[...])
        y_ref[...] = y
        xo_ref[...] = x_ref[...] + _rms(y, gp_ref[...])[0]
        cu[...] = ua[ts - 8:]

    ins = [("t", x2, D, 0), ("t", u3, D_FF, 0), ("t", u3, D_FF, 1), ("w", conv_f), ("w", w_down), ("w", g_post)]
    return _rows_call("ffn_fwd", body, S, ts, ins, [("t", D_FF, MXU), ("t", D, F32), ("t", D, F32)],
                      scratch=[pltpu.VMEM((8, D_FF), F32)], vmem_mb=56)


def _ffn_bwd(dx3, y3, u3, conv_f, w_down, g_post, ts=128):
    S = dx3.shape[0]

    def body(i, g, dx_ref, y_ref, ua_ref, ub_ref, uap_ref, cw_ref, wd_ref, gp_ref, dy_ref, du_ref, dgp_ref, dcw_ref, cdc):
        @pl.when(g == 0)
        def _():
            cdc[...] = jnp.zeros_like(cdc)

        dxv, y = dx_ref[...], y_ref[...]
        dy, r = _rms_bwd(dxv * gp_ref[...], y)
        _acc(dgp_ref, g, jnp.sum(dxv * (y * r), axis=0, keepdims=True))
        dyb = dy.astype(MXU)
        dy_ref[...] = dyb
        dact = _dot_nt(dyb, wd_ref[...])
        ua, w = ua_ref[...], cw_ref[...]
        c, u1, u2 = _conv3(ua, uap_ref[...] * (i > 0).astype(F32), w)
        sg = jax.nn.sigmoid(c)
        du_ref[:, D_FF:2 * D_FF] = (dact * (c * sg)).astype(du_ref.dtype)
        dc = dact * ub_ref[...] * (sg * (1.0 + c * (1.0 - sg)))
        du_ref[:, 0:D_FF] = _conv3_t(dc, cdc[...], w).astype(du_ref.dtype)
        dw = jnp.concatenate([jnp.sum(dc * u2, axis=0, keepdims=True), jnp.sum(dc * u1, axis=0, keepdims=True),
                              jnp.sum(dc * ua, axis=0, keepdims=True)], axis=0)
        _acc(dcw_ref, g, dw)
        cdc[...] = dc[:8]

    ins = [("t", dx3, D, 0), ("t", y3, D, 0), ("t", u3, D_FF, 0), ("t", u3, D_FF, 1), ("h", u3, 8, D_FF, 0), ("w", conv_f),
           ("w", w_down), ("w", g_post)]
    outs = [("t", D, MXU), ("t", 2 * D_FF, MXU), ("a", (1, D), F32), ("a", (3, D_FF), F32)]
    return _rows_call("ffn_bwd", body, S, ts, ins, outs, scratch=[pltpu.VMEM((8, D_FF), F32)], reverse=True, vmem_mb=56)


def _loss_head(x, target, ts=512):
    S = x.shape[0]

    def body(i, g, x_ref, t_ref, dx_ref, acc_ref):
        diff = x_ref[...] - t_ref[...]
        dx_ref[...] = diff * (1.0 / D)
        col = jnp.sum(diff * diff, axis=0, keepdims=True)
        part = col[:, 0:LANES]
        for j in range(1, D // LANES):
            part = part + col[:, j * LANES:(j + 1) * LANES]
        row = lax.broadcasted_iota(jnp.int32, (8, LANES), 0)
        _acc(acc_ref, g, jnp.where(row == 0, jnp.broadcast_to(part, (8, LANES)), 0.0))

    return _rows_call("loss_head", body, S, ts, [("t", x, D, 0), ("t", target, D, 0)], [("t", D, F32), ("a", (8, LANES), F32)])


def _layer_weights(big, small, l):
    w_in = big['w_in']
    pool_w = small['pool_w'][l].astype(MXU)
    wblk = jnp.zeros((POOLW, POOLW), MXU)
    for g in range(4):
        wblk = lax.dynamic_update_slice(wblk, pool_w[g], (g * 96, g * 96))
    vec = lambda n: small[n][l].reshape(1, -1)
    return dict(
        w_in=w_in,
        wblk=wblk, pool_scale=vec('pool_scale'), conv_b=small['conv_b_w'][l], wa=big['w_branch_a'], wb=big['w_branch_b'],
        wc=big['w_branch_c'], w_out=big['w_out'], w_mq=big['w_mq'], w_mkv=big['w_mkv'], w_mo=big['w_mo'],
        w_up=big['w_up'], conv_f=small['conv_ffn_w'][l], w_down=big['w_down'],
        g_mix_pre=vec('norm_mix_pre'), g_mix_post=vec('norm_mix_post'), g_mem_pre=vec('norm_mem_pre'),
        g_mem_post=vec('norm_mem_post'), g_memkv=vec('norm_memkv'), g_ffn_pre=vec('norm_ffn_pre'), g_ffn_post=vec('norm_ffn_post'))


def _layer_fwd(x0, mem, W, ctab, stab):
    sv = dict(x0=x0)
    sv['u'], sv['h1'] = _norm_mm("in_proj", x0, W['g_mix_pre'], W['w_in'], ts=1024, tn=IN_TILE, wt=True, rot=IN_ROT)
    sv['a2'], sv['yb'] = _poolconv_fwd(sv['u'], W['wblk'], W['pool_scale'], W['conv_b'])
    sv['qkv'] = q3, k3, v3 = _rope_perm(sv['u'], ctab, stab)
    sv['att'], sv['o'], sv['lse'] = _attn_combine([_attn_fwd(g, q3[g], k3[g], v3[g]) for g in range(3)])
    sv['merged'], sv['y1'], sv['x1'] = _merge_fwd(x0, sv['u'], sv['a2'], sv['yb'], sv['att'], W['wa'], W['wb'], W['wc'],
                                                  W['w_out'], W['g_mix_post'])
    sv['kv'], sv['memn'] = _norm_mm("mem_kv", mem, W['g_memkv'], W['w_mkv'], ts=256, tn=D, out_dtype=MXU)
    sv['om'], sv['h2'], sv['ym'], sv['x2'] = _mem_fwd(sv['x1'], sv['kv'], W['g_mem_pre'], W['w_mq'], W['w_mo'], W['g_mem_post'])
    sv['u3'], sv['h3'] = _norm_mm("up_proj", sv['x2'], W['g_ffn_pre'], W['w_up'], ts=1024, tn=1408, wt=True)
    sv['act'], sv['y3'], x3 = _ffn_fwd(sv['x2'], sv['u3'], W['conv_f'], W['w_down'], W['g_ffn_post'])
    return x3, sv


def _layer_bwd(dx3, mem, W, sv, ctab, stab):
    dx1, g = _layer_bwd_late(dx3, mem, W, sv)
    dx0, g_mix = _layer_bwd_mix(dx1, W, sv, ctab, stab)
    return dx0, {**g, **g_mix}


def _layer_bwd_late(dx3, mem, W, sv):
    g = {}
    dy3, du3, g['norm_ffn_post'], g['conv_ffn_w'] = _ffn_bwd(dx3, sv['y3'], sv['u3'], W['conv_f'], W['w_down'], W['g_ffn_post'])
    g['w_down'] = _mm_tn("dw_down", sv['act'], dy3, cap_k=256)
    g['w_up'] = _mm_tn("dw_up", du3, sv['h3'])
    dx2, g['norm_ffn_pre'] = _prenorm_bwd("ffn_pre_bwd", dx3, du3, W['w_up'], sv['x2'], W['g_ffn_pre'])
    dym, dqm, dx1, g['norm_mem_post'], g['norm_mem_pre'], dkv = _mem_bwd(dx2, sv['ym'], sv['x1'], sv['kv'], W['g_mem_pre'],
                                                                       W['w_mq'], W['w_mo'], W['g_mem_post'])
    g['w_mo'] = _mm_tn("dw_mo", dym, sv['om'])
    g['w_mq'] = _mm_tn("dw_mq", sv['h2'], dqm)
    dkvb = dkv.astype(MXU)
    g['w_mkv'] = _mm_tn("dw_mkv", sv['memn'], dkvb)
    g['norm_memkv'] = _gain_grad("memkv_gain", _mm_nt("d_memn", dkvb, W['w_mkv'], ts=256, tn=512), mem)
    return dx1, g


def _layer_bwd_mix(dx1, W, sv, ctab, stab):
    du, g = _layer_bwd_mixers(dx1, W, sv, ctab, stab)
    g['w_in'] = _dw_in(du, sv)
    dx0, g['norm_mix_pre'] = _mix_pre_bwd(dx1, du, W, sv)
    return dx0, g


def _dw_in(du, sv):
    return _mm_tn("dw_in", du, sv['h1'], cap_k=IN_TILE, rot=IN_ROT)


def _mix_pre_bwd(dx1, du, W, sv):
    return _prenorm_bwd("mix_pre_bwd", dx1, du, W['w_in'], sv['x0'], W['g_mix_pre'], lead=GATE_W)


def _layer_bwd_mixers(dx1, W, sv, ctab, stab):
    g = {}
    dy1, du, dbra, dbrb, dbrc, da2, dyb, datt, g['norm_mix_post'] = _merge_bwd(
        dx1, sv['y1'], sv['u'], sv['a2'], sv['yb'], sv['att'], W['wa'], W['wb'], W['wc'], W['w_out'], W['g_mix_post'])
    g['w_out'] = _mm_tn("dw_out", sv['merged'], dy1)
    g['w_branch_a'] = _mm_tn("dw_a", dbra, sv['a2'])
    g['w_branch_b'] = _mm_tn("dw_b", dbrb, sv['yb'])
    g['w_branch_c'] = _mm_tn("dw_c", dbrc, sv['att'])
    du, g['pool_scale'], dwblk, g['conv_b_w'] = _poolconv_bwd(sv['u'], da2, dyb, du, W['wblk'], W['pool_scale'], W['conv_b'])
    g['pool_w'] = jnp.stack([dwblk[k * 96:(k + 1) * 96, k * 96:(k + 1) * 96] for k in range(4)])
    q3, k3, v3 = sv['qkv']
    do3, dl3, lse3 = _attn_bwd_prep(datt, sv['o'], sv['lse'])
    dq3 = [_attn_dq(i, q3[i], k3[i], v3[i], do3[i], dl3[i], lse3[i]) for i in range(3)]
    dkv3 = [_attn_dkv(i, q3[i], k3[i], v3[i], do3[i], dl3[i], lse3[i]) for i in range(3)]
    du = _rope_unperm_bwd([dq3, [a for a, _ in dkv3], [b for _, b in dkv3]], du, ctab, stab)
    return du, g


def _local_step(x, mem, positions, target, big, small):
    ctab, stab = _rope_tables(positions)
    Ws = [_layer_weights(big[l], small, l) for l in range(DEPTH)]
    saved = []
    for l in range(DEPTH):
        x, sv = _layer_fwd(x, mem, Ws[l], ctab, stab)
        saved.append(sv)
    dx, acc = _loss_head(x, target)
    loss = jnp.sum(acc) * (0.5 / D)
    grads = [None] * DEPTH
    for l in reversed(range(DEPTH)):
        dx, grads[l] = _layer_bwd(dx, mem, Ws[l], saved[l], ctab, stab)
    return loss, dx, grads


_HBM = pl.BlockSpec(memory_space=pl.ANY)
MESH_ID = pl.DeviceIdType.MESH


def _all_gather(name, xs):
    n = len(xs)

    def body(*refs):
        x_refs, out_refs = refs[:n], refs[n:2 * n]
        send_sems, recv_sems, local_sems = refs[2 * n:]
        x, y, c = lax.axis_index("x"), lax.axis_index("y"), lax.axis_index("c")
        me, sibling = (x, y, c), (x, y, 1 - c)
        chips = [(1 - x, y), (x, 1 - y), (1 - x, 1 - y)]

        def slot(a, p):
            return out_refs[a].at[4 * p[0] + 2 * p[1] + p[2]]

        def copy(a, k, block, to, src=None):
            return pltpu.make_async_remote_copy(src_ref=slot(a, block) if src is None else src, dst_ref=slot(a, block),
                                                send_sem=send_sems.at[a, k], recv_sem=recv_sems.at[a, k], device_id=to,
                                                device_id_type=MESH_ID)

        started = []
        for a in range(n):
            mine = pltpu.make_async_copy(x_refs[a], slot(a, me), local_sems.at[a])
            mine.start()
            started.append(mine)
        first = []
        for a in range(n):
            first.append(copy(a, 0, me, sibling, src=x_refs[a]))
            first += [copy(a, 1 + j, me, (*chip, c), src=x_refs[a]) for j, chip in enumerate(chips)]
        for cp in first:
            cp.start()
        passed = []
        for j, chip in enumerate(chips):
            for a in range(n):
                copy(a, 1 + j, (*chip, c), me).wait_recv()
                fw = copy(a, 4 + j, (*chip, c), sibling)
                fw.start()
                passed.append(fw)
        for a in range(n):
            copy(a, 0, sibling, me).wait_recv()
            for j, chip in enumerate(chips):
                copy(a, 4 + j, (*chip, 1 - c), me).wait_recv()
        for cp in first + passed:
            cp.wait_send()
        for mine in started:
            mine.wait()

    return pl.pallas_call(
        body, out_shape=[jax.ShapeDtypeStruct((N_DEV,) + x.shape, x.dtype) for x in xs], in_specs=[_HBM] * n, out_specs=[_HBM] * n,
        scratch_shapes=[pltpu.SemaphoreType.DMA((n, 7)), pltpu.SemaphoreType.DMA((n, 7)), pltpu.SemaphoreType.DMA((n,))],
        name=name)(*xs)


def _exchange(name, gs):
    n = len(gs)

    def body(*refs):
        g_refs, out_refs = refs[:n], refs[n:2 * n]
        send_sems, recv_sems, local_sems = refs[2 * n:]
        x, y, c = lax.axis_index("x"), lax.axis_index("y"), lax.axis_index("c")
        me = 4 * x + 2 * y + c
        copies = []
        for a in range(n):
            mine = pltpu.make_async_copy(g_refs[a].at[me], out_refs[a].at[me], local_sems.at[a])
            mine.start()
            copies.append(mine)
        for r in range(1, N_DEV):
            px, py, pc = x ^ ((r >> 2) & 1), y ^ ((r >> 1) & 1), c ^ (r & 1)
            for a in range(n):
                cp = pltpu.make_async_remote_copy(src_ref=g_refs[a].at[4 * px + 2 * py + pc], dst_ref=out_refs[a].at[me],
                                                  send_sem=send_sems.at[a, r - 1], recv_sem=recv_sems.at[a, r - 1],
                                                  device_id=(px, py, pc), device_id_type=MESH_ID)
                cp.start()
                copies.append(cp)
        for cp in copies:
            cp.wait()

    return pl.pallas_call(
        body, out_shape=[jax.ShapeDtypeStruct(g.shape, g.dtype) for g in gs], in_specs=[_HBM] * n, out_specs=[_HBM] * n,
        scratch_shapes=[pltpu.SemaphoreType.DMA((n, N_DEV - 1)), pltpu.SemaphoreType.DMA((n, N_DEV - 1)), pltpu.SemaphoreType.DMA((n,))],
        name=name)(*gs)


_SEM = pl.BlockSpec(memory_space=pltpu.SEMAPHORE)
_IN_HBM = pl.BlockSpec(memory_space=pltpu.HBM)
_SIDE_EFFECT = pltpu.SideEffectType.DATAFLOW_SIDE_EFFECTING


def _push_copies(src_refs, land_refs, send_sems, recv_sems, per_peer):
    x, y, c = lax.axis_index("x"), lax.axis_index("y"), lax.axis_index("c")
    me = 4 * x + 2 * y + c
    copies = []
    for r in range(1, N_DEV):
        px, py, pc = x ^ ((r >> 2) & 1), y ^ ((r >> 1) & 1), c ^ (r & 1)
        for a, (s, d) in enumerate(zip(src_refs, land_refs)):
            k = a * (N_DEV - 1) + r - 1
            copies.append(pltpu.make_async_remote_copy(src_ref=s.at[4 * px + 2 * py + pc] if per_peer else s, dst_ref=d.at[me],
                                                       send_sem=send_sems.at[k], recv_sem=recv_sems.at[k],
                                                       device_id=(px, py, pc), device_id_type=MESH_ID))
    return copies


def _push_start(name, srcs, per_peer, after):
    n = len(srcs)
    lands = [lax.empty((N_DEV,) + (s.shape[1:] if per_peer else s.shape), s.dtype) for s in srcs]

    def body(*refs):
        for cp in _push_copies(refs[:n], refs[n:2 * n], refs[2 * n + 1], refs[2 * n + 2], per_peer):
            cp.start()
        refs[-1][...] = jnp.zeros_like(refs[-1])

    hbm = [pltpu.HBM(a.shape, a.dtype) for a in (*srcs, *lands)]
    sems = pltpu.SemaphoreType.DMA((n * (N_DEV - 1),))
    out = pl.pallas_call(
        body, name=name, out_shape=(sems, sems, *hbm, jax.ShapeDtypeStruct((8, LANES), F32)),
        in_specs=[_IN_HBM] * (2 * n) + [pl.BlockSpec(memory_space=pl.ANY)],
        out_specs=(_SEM, _SEM, *[_IN_HBM] * (2 * n), pl.BlockSpec(memory_space=pltpu.VMEM)),
        input_output_aliases={a: 2 + a for a in range(2 * n)},
        compiler_params=pltpu.CompilerParams(has_side_effects=_SIDE_EFFECT),
    )(*[pltpu.with_memory_space_constraint(a, pltpu.HBM) for a in (*srcs, *lands)], after)
    return out[0], out[1], out[2:2 + n], out[2 + n:2 + 2 * n], out[-1]


def _push_wait(name, started, per_peer, after):
    send_sems, recv_sems, srcs, lands, _ = started
    n = len(srcs)

    def body(*refs):
        for cp in _push_copies(refs[:n], refs[n:2 * n], refs[2 * n], refs[2 * n + 1], per_peer):
            cp.wait_send()
            cp.wait_recv()

    out = pl.pallas_call(
        body, name=name, out_shape=[pltpu.HBM(a.shape, a.dtype) for a in (*srcs, *lands)],
        in_specs=[_IN_HBM] * (2 * n) + [_SEM, _SEM, pl.BlockSpec(memory_space=pl.ANY)], out_specs=[_IN_HBM] * (2 * n),
        input_output_aliases={a: a for a in range(2 * n)},
        compiler_params=pltpu.CompilerParams(has_side_effects=_SIDE_EFFECT),
    )(*srcs, *lands, send_sems, recv_sems, after)
    return out[n:]


def _my_slot():
    return 4 * lax.axis_index("x") + 2 * lax.axis_index("y") + lax.axis_index("c")


def _row_tile(rows, cols, budget):
    if rows * cols * 4 <= budget or rows % 16:
        return rows
    best = 16
    for t in range(16, rows + 1, 16):
        if rows % t == 0 and t * cols * 4 <= budget:
            best = t
    return best


def _sum_slots(name, recv):
    _, R, C = recv.shape
    tr = _row_tile(R, C, 1 << 20)

    def body(r_ref, o_ref):
        g = r_ref[0].astype(F32)
        for k in range(1, N_DEV):
            g = g + r_ref[k].astype(F32)
        o_ref[...] = g

    return pl.pallas_call(body, grid=(R // tr,), in_specs=[pl.BlockSpec((N_DEV, tr, C), lambda i: (0, i, 0))],
                          out_specs=pl.BlockSpec((tr, C), lambda i: (i, 0)), out_shape=jax.ShapeDtypeStruct((R, C), F32),
                          compiler_params=_params(("arbitrary",), 32), name=name)(recv)


def _adamw(name, g, w, m, v):
    R, C = w.shape
    tr = _row_tile(R, C, 1 << 20)
    c1 = 1.0 - ADAM_B1 ** ADAM_STEP
    c2 = 1.0 - ADAM_B2 ** ADAM_STEP

    def body(g_ref, w_ref, m_ref, v_ref, d_ref, mo_ref, vo_ref):
        gv = g_ref[...]
        mn = ADAM_B1 * m_ref[...] + (1.0 - ADAM_B1) * gv
        vn = ADAM_B2 * v_ref[...] + (1.0 - ADAM_B2) * (gv * gv)
        mo_ref[...] = mn
        vo_ref[...] = vn
        d_ref[...] = -ADAM_LR * ((mn / c1) / (jnp.sqrt(vn / c2) + ADAM_EPS) + ADAM_WD * w_ref[...])

    blk = pl.BlockSpec((tr, C), lambda i: (i, 0))
    shp = jax.ShapeDtypeStruct((R, C), F32)
    return pl.pallas_call(body, grid=(R // tr,), in_specs=[blk, blk, blk, blk], out_specs=[blk, blk, blk], out_shape=[shp, shp, shp],
                          compiler_params=_params(("arbitrary",), 32), name=name)(g, w, m, v)


def _pad_flat(a, n):
    a = a.reshape(-1)
    return jnp.pad(a, (0, n - a.shape[0]))


def _seg(n):
    return -(-n // FLAT_ALIGN) * FLAT_ALIGN


def _to_blocks(full, axis):
    shp = full.shape
    return jnp.moveaxis(full.reshape(shp[:axis] + (N_DEV, shp[axis] // N_DEV) + shp[axis + 1:]), axis, 0)


def _from_blocks(blocks, axis):
    b = jnp.moveaxis(blocks, 0, axis)
    shp = b.shape
    return b.reshape(shp[:axis] + (shp[axis] * shp[axis + 1],) + shp[axis + 2:])


def _as_rows(shard, n):
    return shard.T if SHARD_AXIS[n] == 2 else shard


def _with_own(lands, own, me):
    return [lax.dynamic_update_slice(land, o[None], (me, 0, 0)) for land, o in zip(lands, own)]


def kernel(x, mem, positions, norm_mix_pre, norm_mix_post, w_in, pool_w, pool_scale, conv_b_w, w_branch_a, w_branch_b, w_branch_c, w_out, norm_mem_pre, norm_mem_post, norm_memkv, w_mq, w_mkv, w_mo, norm_ffn_pre, norm_ffn_post, w_up, conv_ffn_w, w_down, loss_target, m_norm_mix_pre, m_norm_mix_post, m_w_in, m_pool_w, m_pool_scale, m_conv_b_w, m_w_branch_a, m_w_branch_b, m_w_branch_c, m_w_out, m_norm_mem_pre, m_norm_mem_post, m_norm_memkv, m_w_mq, m_w_mkv, m_w_mo, m_norm_ffn_pre, m_norm_ffn_post, m_w_up, m_conv_ffn_w, m_w_down, v_norm_mix_pre, v_norm_mix_post, v_w_in, v_pool_w, v_pool_scale, v_conv_b_w, v_w_branch_a, v_w_branch_b, v_w_branch_c, v_w_out, v_norm_mem_pre, v_norm_mem_post, v_norm_memkv, v_w_mq, v_w_mkv, v_w_mo, v_norm_ffn_pre, v_norm_ffn_post, v_w_up, v_conv_ffn_w, v_w_down):
    w = dict(norm_mix_pre=norm_mix_pre, norm_mix_post=norm_mix_post, w_in=w_in, pool_w=pool_w, pool_scale=pool_scale, conv_b_w=conv_b_w, w_branch_a=w_branch_a, w_branch_b=w_branch_b, w_branch_c=w_branch_c, w_out=w_out, norm_mem_pre=norm_mem_pre, norm_mem_post=norm_mem_post, norm_memkv=norm_memkv, w_mq=w_mq, w_mkv=w_mkv, w_mo=w_mo, norm_ffn_pre=norm_ffn_pre, norm_ffn_post=norm_ffn_post, w_up=w_up, conv_ffn_w=conv_ffn_w, w_down=w_down)
    m = dict(norm_mix_pre=m_norm_mix_pre, norm_mix_post=m_norm_mix_post, w_in=m_w_in, pool_w=m_pool_w, pool_scale=m_pool_scale, conv_b_w=m_conv_b_w, w_branch_a=m_w_branch_a, w_branch_b=m_w_branch_b, w_branch_c=m_w_branch_c, w_out=m_w_out, norm_mem_pre=m_norm_mem_pre, norm_mem_post=m_norm_mem_post, norm_memkv=m_norm_memkv, w_mq=m_w_mq, w_mkv=m_w_mkv, w_mo=m_w_mo, norm_ffn_pre=m_norm_ffn_pre, norm_ffn_post=m_norm_ffn_post, w_up=m_w_up, conv_ffn_w=m_conv_ffn_w, w_down=m_w_down)
    v = dict(norm_mix_pre=v_norm_mix_pre, norm_mix_post=v_norm_mix_post, w_in=v_w_in, pool_w=v_pool_w, pool_scale=v_pool_scale, conv_b_w=v_conv_b_w, w_branch_a=v_w_branch_a, w_branch_b=v_w_branch_b, w_branch_c=v_w_branch_c, w_out=v_w_out, norm_mem_pre=v_norm_mem_pre, norm_mem_post=v_norm_mem_post, norm_memkv=v_norm_memkv, w_mq=v_w_mq, w_mkv=v_w_mkv, w_mo=v_w_mo, norm_ffn_pre=v_norm_ffn_pre, norm_ffn_post=v_norm_ffn_post, w_up=v_w_up, conv_ffn_w=v_conv_ffn_w, w_down=v_w_down)

    me = _my_slot()
    blocks = [[_as_rows(w[n][l], n).astype(MXU) for n in BIG] for l in range(DEPTH)]
    conv = jnp.concatenate([_pad_flat(w[n], _seg(w[n].size)) for n in F32_GATHERED]).reshape(-1, LANES)
    got0 = _all_gather("weights_all_gather_0", blocks[0] + [conv])
    conv_all = got0[-1].reshape(N_DEV, -1)
    small, off = {n: w[n] for n in WEIGHTS if n not in SHARD_AXIS}, 0
    for n in F32_GATHERED:
        small[n] = _from_blocks(conv_all[:, off:off + w[n].size].reshape((N_DEV,) + w[n].shape), 2)
        off += _seg(w[n].size)
    whole = lambda got: {n: o.reshape(-1, o.shape[-1]) for n, o in zip(BIG, got)}

    ctab, stab = _rope_tables(positions[0])
    push_w = _push_start("weights_push_start_1", blocks[1], False, got0[0])
    W0 = _layer_weights(whole(got0), small, 0)
    x1, sv0 = _layer_fwd(x[0], mem[0], dict(W0, g_mix_pre=W0['g_mix_pre'] + push_w[4][0, 0]), ctab, stab)
    W1 = _layer_weights(whole(_with_own(_push_wait("weights_push_wait_1", push_w, False, x1), blocks[1], me)), small, 1)
    x2, sv1 = _layer_fwd(x1, mem[0], W1, ctab, stab)
    dx, acc = _loss_head(x2, loss_target[0])
    loss = lax.psum(jnp.sum(acc) * (0.5 / D), MESH_AXES)
    grads = [None] * DEPTH
    dx, grads[1] = _layer_bwd(dx, mem[0], W1, sv1, ctab, stab)
    sent = [None, [grads[1][n].reshape(N_DEV, -1, grads[1][n].shape[-1]) for n in BIG]]
    push_g = _push_start("grads_push_start_1", sent[1], True, dx)
    dx, g_late = _layer_bwd_late(dx, mem[0], dict(W0, g_ffn_post=W0['g_ffn_post'] + push_g[4][0, 0]), sv0)
    sent_late = [g_late[n].reshape(N_DEV, -1, g_late[n].shape[-1]) for n in LATE_BIG]
    push_l = _push_start("grads_push_start_0", sent_late, True, dx)
    du, g_mix = _layer_bwd_mixers(dx, dict(W0, g_mix_post=W0['g_mix_post'] + push_l[4][0, 0]), sv0, ctab, stab)
    g_mix['w_in'] = _dw_in(du, sv0)
    sent_in = [g_mix['w_in'].reshape(N_DEV, -1, D)]
    push_i = _push_start("grads_push_start_in", sent_in, True, du)
    dx, g_mix['norm_mix_pre'] = _mix_pre_bwd(dx, du, dict(W0, g_mix_pre=W0['g_mix_pre'] + push_i[4][0, 0]), sv0)
    grads[0] = {**g_late, **g_mix}
    own = lambda s: [lax.dynamic_index_in_dim(a, me, 0, keepdims=False) for a in s]
    recv1 = _with_own(_push_wait("grads_push_wait_1", push_g, True, dx), own(sent[1]), me)
    recv_late = _with_own(_push_wait("grads_push_wait_0", push_l, True, dx), own(sent_late), me)
    mix_big = [n for n in BIG if n not in LATE_BIG and n != 'w_in']

    misc_names = [n for n in WEIGHTS if n not in BIG]
    stacked = {n: jnp.stack([grads[l][n].reshape(small[n].shape[1:]) for l in range(DEPTH)]) for n in misc_names}
    rows = [(_to_blocks(stacked[n], 2) if n in SHARD_AXIS else jnp.broadcast_to(stacked[n][None], (N_DEV,) + stacked[n].shape))
            for n in misc_names]
    segs = [_seg(w[n].size) for n in misc_names]
    misc = jnp.concatenate([jnp.pad(r.reshape(N_DEV, -1), ((0, 0), (0, s - r[0].size))) for r, s in zip(rows, segs)],
                           axis=1).reshape(N_DEV, -1, LANES)
    recv_mix = _exchange("grad_exchange_0", [g_mix[n].reshape(N_DEV, -1, g_mix[n].shape[-1]) for n in mix_big] + [misc])
    g_out, per_layer = {}, {}
    for l, names, recv in ((1, BIG, recv1), (0, LATE_BIG, recv_late), (0, mix_big, recv_mix)):
        for n, r in zip(names, recv):
            per_layer[n, l] = _as_rows(_sum_slots(f"sum_{n}_{l}", r), n)
    misc_sum = _sum_slots("sum_misc", recv_mix[-1]).reshape(-1)
    off = 0
    for n, s in zip(misc_names, segs):
        g_out[n] = misc_sum[off:off + w[n].size].reshape(w[n].shape)
        off += s
    def update(n):
        shp = w[n].shape
        g_out[n] = jnp.stack([per_layer[n, l] for l in range(DEPTH)]) if n in BIG else g_out[n]
        d, mn, vn = _adamw(f"adamw_{n}", *[a.reshape(-1, shp[-1]) for a in (g_out[n], w[n], m[n], v[n])])
        return [a.reshape(shp) for a in (g_out[n], d, mn, vn)]

    done = {n: update(n) for n in WEIGHTS if n != 'w_in'}
    recv_in = _with_own(_push_wait("grads_push_wait_in", push_i, True, done[WEIGHTS[-1]][1]), own(sent_in), me)
    per_layer['w_in', 0] = _as_rows(_sum_slots("sum_w_in_0", recv_in[0]), 'w_in')
    done['w_in'] = update('w_in')
    return (loss, dx[None], *[done[n][k] for k in range(4) for n in WEIGHTS])
```

```python
import jax
import jax.numpy as jnp
from jax import lax
from jax.experimental import pallas as pl
from jax.experimental.pallas import tpu as pltpu

F32 = jnp.float32
MXU = jnp.bfloat16
HI = lax.Precision.HIGHEST

D = 1024
DEPTH = 2
POOLW = 384
ATT_W = 768
ATT_O = 256
GATE_W = 3 * D
IN_W = 6912
IN_TILE = 768
IN_ROT = (IN_W - GATE_W) // IN_TILE
MEM_W = 512
D_FF = 2816
EPS = 1e-6
ROPE_THETA = 500000.0
QB = 128
DILS = (1, 4, 16)
NEG = -1e30
MEM_SCALE = 128 ** -0.5
ATT_SCALE = 0.125

ADAM_LR, ADAM_B1, ADAM_B2, ADAM_EPS, ADAM_WD, ADAM_STEP = 0.001, 0.9, 0.999, 1e-08, 0.01, 10

N_DEV = 8
MESH_AXES = ("x", "y", "c")
LANES = 128
FLAT_ALIGN = 2048
ROW_TILE = 1024

WEIGHTS = ['norm_mix_pre', 'norm_mix_post', 'w_in', 'pool_w', 'pool_scale', 'conv_b_w', 'w_branch_a', 'w_branch_b',
           'w_branch_c', 'w_out', 'norm_mem_pre', 'norm_mem_post', 'norm_memkv', 'w_mq', 'w_mkv', 'w_mo',
           'norm_ffn_pre', 'norm_ffn_post', 'w_up', 'conv_ffn_w', 'w_down']
SHARD_AXIS = {'w_in': 2, 'conv_b_w': 2, 'w_branch_a': 2, 'w_branch_b': 2, 'w_branch_c': 2, 'w_out': 1, 'w_mq': 1,
              'w_mkv': 1, 'w_mo': 2, 'w_up': 2, 'conv_ffn_w': 2, 'w_down': 1}
F32_GATHERED = ('conv_b_w', 'conv_ffn_w')
BIG = [n for n in WEIGHTS if n in SHARD_AXIS and n not in F32_GATHERED]
LATE_BIG = ['w_mq', 'w_mkv', 'w_mo', 'w_up', 'w_down']


VMEM_LIMIT_MB = 60


def _params(sem, vmem_mb):
    del vmem_mb
    return pltpu.CompilerParams(dimension_semantics=sem, vmem_limit_bytes=VMEM_LIMIT_MB << 20)


def _dot(a, b, prec=None):
    return lax.dot_general(a, b, (((1,), (0,)), ((), ())), preferred_element_type=F32, precision=prec)


def _dot_nt(a, b, prec=None):
    return lax.dot_general(a, b, (((1,), (1,)), ((), ())), preferred_element_type=F32, precision=prec)


def _dot_tn(a, b, prec=None):
    return lax.dot_general(a, b, (((0,), (0,)), ((), ())), preferred_element_type=F32, precision=prec)


def _tile(n, cap):
    if n <= cap:
        return n
    best = None
    for t in range(LANES, cap + 1, LANES):
        if n % t == 0:
            best = t
    assert best is not None, (n, cap)
    return best


def _rms(x, g):
    r = lax.rsqrt(jnp.mean(x * x, axis=-1, keepdims=True) + EPS)
    return x * r * g, r


def _rms_bwd(w, y):
    r = lax.rsqrt(jnp.mean(y * y, axis=-1, keepdims=True) + EPS)
    return r * w - y * (r * r * r) * jnp.mean(w * y, axis=-1, keepdims=True), r


def _rows_call(name, body, n_rows, ts, ins, outs, scratch=(), reverse=False, vmem_mb=48, aliases=None):
    nt = n_rows // ts
    assert nt * ts == n_rows

    def tile_of(g):
        return (nt - 1 - g) if reverse else g

    in_specs, args = [], []
    for op in ins:
        if op[0] == "t":
            _, a, cw, cb = op
            in_specs.append(pl.BlockSpec((ts, cw), lambda g, cb=cb: (tile_of(g), cb)))
        elif op[0] == "h":
            _, a, hr, cw, cb = op
            in_specs.append(pl.BlockSpec((hr, cw), lambda g, cb=cb, k=ts // hr: (jnp.maximum(tile_of(g) * k - 1, 0), cb)))
        elif op[0] == "x":
            _, a = op
            in_specs.append(pl.BlockSpec(memory_space=pl.ANY))
        else:
            _, a = op
            in_specs.append(pl.BlockSpec(a.shape, lambda g, n=a.ndim: (0,) * n))
        args.append(a)
    out_specs, out_shape = [], []
    for op in outs:
        if op[0] == "t":
            _, cols, dt = op
            out_specs.append(pl.BlockSpec((ts, cols), lambda g: (tile_of(g), 0)))
            out_shape.append(jax.ShapeDtypeStruct((n_rows, cols), dt))
        elif op[0] == "c":
            _, total, cols, cb, dt = op
            out_specs.append(pl.BlockSpec((ts, cols), lambda g, cb=cb: (tile_of(g), cb)))
            out_shape.append(jax.ShapeDtypeStruct((n_rows, total), dt))
        else:
            _, shp, dt = op
            out_specs.append(pl.BlockSpec(shp, lambda g, n=len(shp): (0,) * n))
            out_shape.append(jax.ShapeDtypeStruct(shp, dt))

    def kern(*refs):
        g = pl.program_id(0)
        body(tile_of(g), g, *refs)

    return pl.pallas_call(kern, grid=(nt,), in_specs=in_specs, out_specs=out_specs, out_shape=out_shape,
                          scratch_shapes=list(scratch), input_output_aliases=aliases or {},
                          compiler_params=_params(("arbitrary",), vmem_mb), name=name)(*args)


def _acc(ref, g, val):
    @pl.when(g == 0)
    def _():
        ref[...] = val

    @pl.when(g != 0)
    def _():
        ref[...] += val


def _norm_mm(name, x, g, w, ts, tn, out_dtype=F32, wt=False, rot=0):
    S, K = x.shape
    N = w.shape[0] if wt else w.shape[1]
    assert wt or not rot

    def body(x_ref, g_ref, w_ref, o_ref, h_ref, hs):
        @pl.when(pl.program_id(1) == 0)
        def _():
            h, _ = _rms(x_ref[...], g_ref[...])
            hs[...] = h.astype(MXU)
            h_ref[...] = h.astype(MXU)

        o_ref[...] = (_dot_nt if wt else _dot)(hs[...], w_ref[...]).astype(out_dtype)

    w_spec = pl.BlockSpec((tn, K), lambda i, j: ((j + rot) % (N // tn), 0)) if wt else pl.BlockSpec((K, tn), lambda i, j: (0, j))
    return pl.pallas_call(
        body, grid=(S // ts, N // tn),
        in_specs=[pl.BlockSpec((ts, K), lambda i, j: (i, 0)), pl.BlockSpec((1, K), lambda i, j: (0, 0)), w_spec],
        out_specs=[pl.BlockSpec((ts, tn), lambda i, j: (i, j)), pl.BlockSpec((ts, K), lambda i, j: (i, 0))],
        out_shape=[jax.ShapeDtypeStruct((S, N), out_dtype), jax.ShapeDtypeStruct((S, K), MXU)],
        scratch_shapes=[pltpu.VMEM((ts, K), MXU)],
        compiler_params=_params(("arbitrary", "arbitrary"), 48), name=name)(x, g, w)


def _mm_nt(name, a, b, ts, tn, out_dtype=F32):
    M, K = a.shape
    N = b.shape[0]

    def body(a_ref, b_ref, o_ref):
        o_ref[...] = _dot_nt(a_ref[...], b_ref[...]).astype(out_dtype)

    return pl.pallas_call(
        body, grid=(M // ts, N // tn),
        in_specs=[pl.BlockSpec((ts, K), lambda i, j: (i, 0)), pl.BlockSpec((tn, K), lambda i, j: (j, 0))],
        out_specs=pl.BlockSpec((ts, tn), lambda i, j: (i, j)), out_shape=jax.ShapeDtypeStruct((M, N), out_dtype),
        compiler_params=_params(("arbitrary", "arbitrary"), 48), name=name)(a, b)


def _mm_tn(name, a, b, cap_k=512, cap_n=1024, out_dtype=MXU, rot=0):
    S, K = a.shape
    N = b.shape[1]
    tk, tn = _tile(K, cap_k), _tile(N, cap_n)

    def body(a_ref, b_ref, o_ref):
        o_ref[...] = _dot_tn(a_ref[...], b_ref[...]).astype(out_dtype)

    return pl.pallas_call(
        body, grid=(K // tk, N // tn),
        in_specs=[pl.BlockSpec((S, tk), lambda i, j: (0, i)), pl.BlockSpec((S, tn), lambda i, j: (0, j))],
        out_specs=pl.BlockSpec((tk, tn), lambda i, j: ((i + rot) % (K // tk), j)), out_shape=jax.ShapeDtypeStruct((K, N), out_dtype),
        compiler_params=_params(("arbitrary", "arbitrary"), 48), name=name)(a, b)


def _pool_cols(shape):
    col = lax.broadcasted_iota(jnp.int32, shape, 1)
    return col < 96, col < 192, col < 288


def _pool_select(s2, s4, s8, s16):
    c1, c2, c3 = _pool_cols(s2.shape)
    return jnp.where(c1, s2, jnp.where(c2, s4, jnp.where(c3, s8, s16)))


def _pool_cnt(t0, ts):
    c1, c2, c3 = _pool_cols((ts, POOLW))
    win = jnp.where(c1, 2, jnp.where(c2, 4, jnp.where(c3, 8, 16)))
    t = t0 + lax.broadcasted_iota(jnp.int32, (ts, POOLW), 0)
    return jnp.minimum(t + 1, win).astype(F32)


def _pooled(a, prev, t0):
    ts = a.shape[0]
    ext = jnp.concatenate([prev, a], axis=0)
    s2 = ext + pltpu.roll(ext, 1, axis=0)
    s4 = s2 + pltpu.roll(s2, 2, axis=0)
    s8 = s4 + pltpu.roll(s4, 4, axis=0)
    s16 = s8 + pltpu.roll(s8, 8, axis=0)
    sums = _pool_select(s2, s4, s8, s16)[16:]
    return sums / _pool_cnt(t0, ts) - a


def _conv3(z, prev8, w):
    ext = jnp.concatenate([prev8, z], axis=0)
    z1 = pltpu.roll(ext, 1, axis=0)[8:]
    z2 = pltpu.roll(ext, 2, axis=0)[8:]
    return w[0:1] * z2 + w[1:2] * z1 + w[2:3] * z, z1, z2


def _conv3_t(dc, next8, w):
    ts = dc.shape[0]
    ext = jnp.concatenate([dc, next8], axis=0)
    n = ts + 8
    u1 = pltpu.roll(ext, n - 1, axis=0)[:ts]
    u2 = pltpu.roll(ext, n - 2, axis=0)[:ts]
    return w[2:3] * dc + w[1:2] * u1 + w[0:1] * u2


def _poolconv_fwd(u, wblk, pool_scale, conv_b, ts=256):
    S = u.shape[0]

    def body(i, g, a_ref, bx_ref, bb_ref, bc_ref, wblk_ref, ps_ref, cw_ref, a2_ref, yb_ref, ca, cz):
        @pl.when(g == 0)
        def _():
            ca[...] = jnp.zeros_like(ca)
            cz[...] = jnp.zeros_like(cz)

        a = a_ref[...]
        p = _pooled(a, ca[...], i * ts)
        mixed = _dot(p.astype(MXU), wblk_ref[...])
        a2_ref[...] = (mixed * ps_ref[...]).astype(MXU)
        z = bc_ref[...] * bx_ref[...]
        conv, _, _ = _conv3(z, cz[...], cw_ref[...])
        yb_ref[...] = (bb_ref[...] * conv).astype(MXU)
        ca[...] = a[ts - 16:]
        cz[...] = z[ts - 8:]

    ins = [("t", u, POOLW, 8), ("t", u, POOLW, 9), ("t", u, POOLW, 10), ("t", u, POOLW, 11), ("w", wblk), ("w", pool_scale),
           ("w", conv_b)]
    return _rows_call("poolconv_fwd", body, S, ts, ins, [("t", POOLW, MXU), ("t", POOLW, MXU)],
                      scratch=[pltpu.VMEM((16, POOLW), F32), pltpu.VMEM((8, POOLW), F32)])


def _poolconv_bwd(u, d_a2, d_yb, du, wblk, pool_scale, conv_b, ts=256):
    S = u.shape[0]

    def body(i, g, a_ref, bx_ref, bb_ref, bc_ref, ap_ref, bxp_ref, bcp_ref, da2_ref, dyb_ref, wblk_ref, ps_ref, cw_ref, _,
             o_ref, dps_ref, dwb_ref, dcw_ref, ce, cdz):
        @pl.when(g == 0)
        def _():
            ce[...] = jnp.zeros_like(ce)
            cdz[...] = jnp.zeros_like(cdz)

        first = (i > 0).astype(F32)
        a = a_ref[...]
        p = _pooled(a, ap_ref[...] * first, i * ts)
        pb = p.astype(MXU)
        mixed = _dot(pb, wblk_ref[...])
        da2 = da2_ref[...]
        dmixed = (da2 * ps_ref[...]).astype(MXU)
        dp = _dot_nt(dmixed, wblk_ref[...])
        _acc(dps_ref, g, jnp.sum(da2 * mixed, axis=0, keepdims=True))
        _acc(dwb_ref, g, _dot_tn(pb, dmixed))
        e = dp / _pool_cnt(i * ts, ts)
        ext = jnp.concatenate([e, ce[...]], axis=0)
        n = ts + 16
        f2 = ext + pltpu.roll(ext, n - 1, axis=0)
        f4 = f2 + pltpu.roll(f2, n - 2, axis=0)
        f8 = f4 + pltpu.roll(f4, n - 4, axis=0)
        f16 = f8 + pltpu.roll(f8, n - 8, axis=0)
        o_ref[:, 0:POOLW] = (_pool_select(f2, f4, f8, f16)[:ts] - dp).astype(o_ref.dtype)
        ce[...] = e[:16]

        bx, bb, bc = bx_ref[...], bb_ref[...], bc_ref[...]
        z = bc * bx
        w = cw_ref[...]
        conv, z1, z2 = _conv3(z, bxp_ref[...] * bcp_ref[...] * first, w)
        dyb = dyb_ref[...]
        dconv = dyb * bb
        dz = _conv3_t(dconv, cdz[...], w)
        o_ref[:, POOLW:2 * POOLW] = (dz * bc).astype(o_ref.dtype)
        o_ref[:, 2 * POOLW:3 * POOLW] = (dyb * conv).astype(o_ref.dtype)
        o_ref[:, 3 * POOLW:4 * POOLW] = (dz * bx).astype(o_ref.dtype)
        dw = jnp.concatenate([jnp.sum(dconv * z2, axis=0, keepdims=True), jnp.sum(dconv * z1, axis=0, keepdims=True),
                              jnp.sum(dconv * z, axis=0, keepdims=True)], axis=0)
        _acc(dcw_ref, g, dw)
        cdz[...] = dconv[:8]

    ins = [("t", u, POOLW, 8), ("t", u, POOLW, 9), ("t", u, POOLW, 10), ("t", u, POOLW, 11),
           ("h", u, 16, POOLW, 8), ("h", u, 8, POOLW, 9), ("h", u, 8, POOLW, 11),
           ("t", d_a2, POOLW, 0), ("t", d_yb, POOLW, 0), ("w", wblk), ("w", pool_scale), ("w", conv_b), ("x", du)]
    outs = [("c", IN_W, 4 * POOLW, GATE_W // (4 * POOLW), MXU), ("a", (1, POOLW), F32), ("a", (POOLW, POOLW), F32), ("a", (3, POOLW), F32)]
    return _rows_call("poolconv_bwd", body, S, ts, ins, outs, aliases={len(ins) - 1: 0},
                      scratch=[pltpu.VMEM((16, POOLW), F32), pltpu.VMEM((8, POOLW), F32)], reverse=True)


def _rope_tables(positions):
    S = positions.shape[0]
    inv = ROPE_THETA ** (-jnp.arange(0, 16, 2, dtype=F32) / 16)
    ang = positions.astype(F32)[:, None] * inv
    cos, sin = jnp.cos(ang), jnp.sin(ang)
    c64 = jnp.concatenate([cos, cos, jnp.ones((S, 48), F32)], axis=1)
    s64 = jnp.concatenate([-sin, sin, jnp.zeros((S, 48), F32)], axis=1)
    return jnp.concatenate([c64, c64], axis=1), jnp.concatenate([s64, s64], axis=1)


def _partner(x):
    lane = lax.broadcasted_iota(jnp.int32, x.shape, 1) % 64
    return jnp.where(lane < 8, pltpu.roll(x, LANES - 8, axis=1), jnp.where(lane < 16, pltpu.roll(x, 8, axis=1), 0.0))


def _rope(x, c, s):
    return x * c + _partner(x) * s


def _rope_t(x, c, s):
    return x * c + _partner(x * s)


def _rows_of(r, n, d):
    return pl.ds(r, n, stride=d) if d > 1 else pl.ds(0, n)


def _head_masks(shape):
    lane = lax.broadcasted_iota(jnp.int32, shape, 1) // 64
    return [lane == h for h in range(4)]


def _only(mask, x):
    return jnp.where(mask, x, jnp.zeros_like(x))


def _rope_perm(u, ctab, stab, ts=256):
    S = u.shape[0]
    nch = ATT_W // LANES

    def body(*refs):
        chunks, (c_ref, s_ref), outs = refs[:3 * nch], refs[3 * nch:3 * nch + 2], refs[3 * nch + 2:]
        for g, d in enumerate(DILS):
            n = ts // d
            for r in range(d):
                rows = _rows_of(r, n, d)
                c, s = c_ref[rows, :], s_ref[rows, :]
                for which in range(3):
                    parts = [chunks[which * nch + j][rows, :] for j in (2 * g, 2 * g + 1)]
                    if which < 2:
                        parts = [_rope(x, c, s) for x in parts]
                    outs[which * 3 + g][r] = jnp.concatenate(parts, axis=1).astype(MXU)

    base = (IN_W - 3 * ATT_W) // LANES
    in_specs = [pl.BlockSpec((ts, LANES), lambda i, cb=base + k: (i, cb)) for k in range(3 * nch)]
    in_specs += [pl.BlockSpec((ts, LANES), lambda i: (i, 0))] * 2
    out_specs = [pl.BlockSpec((d, ts // d, ATT_O), lambda i: (0, i, 0)) for _ in range(3) for d in DILS]
    out_shape = [jax.ShapeDtypeStruct((d, S // d, ATT_O), MXU) for _ in range(3) for d in DILS]
    res = pl.pallas_call(body, grid=(S // ts,), in_specs=in_specs, out_specs=out_specs, out_shape=out_shape,
                         compiler_params=_params(("arbitrary",), 32), name="rope_perm")(*([u] * (3 * nch)), ctab, stab)
    return [[res[which * 3 + g].reshape(S, ATT_O) for g in range(3)] for which in range(3)]


def _rope_unperm_bwd(dqkv, du, ctab, stab, ts=256):
    S = dqkv[0][0].shape[0]
    nch = ATT_W // LANES

    def body(*refs):
        ins, (c_ref, s_ref, _, o_ref, scr) = refs[:9], refs[9:]
        for g, d in enumerate(DILS):
            n = ts // d
            for r in range(d):
                rows = _rows_of(r, n, d)
                c, s = c_ref[rows, :], s_ref[rows, :]
                for which in range(3):
                    v = ins[which * 3 + g][r]
                    for half in range(2):
                        x = v[:, half * LANES:(half + 1) * LANES]
                        scr.at[which * nch + 2 * g + half][rows, :] = _rope_t(x, c, s) if which < 2 else x
        for j in range(3 * nch):
            o_ref[:, j * LANES:(j + 1) * LANES] = scr[j].astype(o_ref.dtype)

    in_specs = [pl.BlockSpec((d, ts // d, ATT_O), lambda i: (0, i, 0)) for _ in range(3) for d in DILS]
    in_specs += [pl.BlockSpec((ts, LANES), lambda i: (i, 0))] * 2 + [pl.BlockSpec(memory_space=pl.ANY)]
    args = [dqkv[which][g].reshape(d, S // d, ATT_O) for which in range(3) for g, d in enumerate(DILS)]
    last = (IN_W - 3 * ATT_W) // (3 * ATT_W)
    return pl.pallas_call(body, grid=(S // ts,), in_specs=in_specs, out_specs=pl.BlockSpec((ts, 3 * ATT_W), lambda i: (i, last)),
                          out_shape=jax.ShapeDtypeStruct((S, IN_W), MXU), scratch_shapes=[pltpu.VMEM((3 * nch, ts, LANES), F32)],
                          input_output_aliases={len(in_specs) - 1: 0},
                          compiler_params=_params(("arbitrary",), 32), name="rope_unperm_bwd")(*args, ctab, stab, du)


def _band_mask_keys(has_prev):
    r = lax.broadcasted_iota(jnp.int32, (QB, 2 * QB), 0)
    c = lax.broadcasted_iota(jnp.int32, (QB, 2 * QB), 1)
    return ((c < QB) & (c >= r) & has_prev) | ((c >= QB) & (c - QB <= r))


def _band_mask_queries(has_next):
    r = lax.broadcasted_iota(jnp.int32, (2 * QB, QB), 0)
    c = lax.broadcasted_iota(jnp.int32, (2 * QB, QB), 1)
    return ((r < QB) & (c <= r)) | ((r >= QB) & (c >= r - QB) & has_next)


ASUB = 4
_BIG = pl.BlockSpec((ASUB * QB, ATT_O), lambda b: (b, 0))
_PREV = pl.BlockSpec((QB, ATT_O), lambda b: (jnp.maximum(b * ASUB - 1, 0), 0))


def _sub(ref, j):
    return ref[j * QB:(j + 1) * QB]


def _attn_fwd(g, q, k, v):
    S = q.shape[0]
    nb = S // QB
    nblk = nb // DILS[g]

    def body(q_ref, kc_ref, kp_ref, vc_ref, vp_ref, o_ref, m_ref, l_ref):
        hm_kv, hm_o = _head_masks((2 * QB, ATT_O)), _head_masks((QB, ATT_O))
        for j in range(ASUB):
            ok = _band_mask_keys(((pl.program_id(0) * ASUB + j) & (nblk - 1)) > 0)
            k2 = jnp.concatenate([kp_ref[...] if j == 0 else _sub(kc_ref, j - 1), _sub(kc_ref, j)], axis=0)
            v2 = jnp.concatenate([vp_ref[...] if j == 0 else _sub(vc_ref, j - 1), _sub(vc_ref, j)], axis=0)
            qv = _sub(q_ref, j)
            o_acc = jnp.zeros((QB, ATT_O), F32)
            m_acc = jnp.zeros((QB, ATT_O), F32)
            l_acc = jnp.zeros((QB, ATT_O), F32)
            for h in range(4):
                s = jnp.where(ok, _dot_nt(qv, _only(hm_kv[h], k2)) * ATT_SCALE, NEG)
                m = jnp.max(s, axis=1, keepdims=True)
                p = jnp.exp(s - m)
                o_acc = o_acc + _dot(p.astype(MXU), _only(hm_kv[h], v2))
                m_acc = jnp.where(hm_o[h], m, m_acc)
                l_acc = jnp.where(hm_o[h], jnp.sum(p, axis=1, keepdims=True), l_acc)
            o_ref[j * QB:(j + 1) * QB] = o_acc
            m_ref[j * QB:(j + 1) * QB] = m_acc
            l_ref[j * QB:(j + 1) * QB] = l_acc

    shp = jax.ShapeDtypeStruct((S, ATT_O), F32)
    return pl.pallas_call(body, grid=(nb // ASUB,), in_specs=[_BIG, _BIG, _PREV, _BIG, _PREV],
                          out_specs=[_BIG] * 3, out_shape=[shp, shp, shp], compiler_params=_params(("arbitrary",), 32),
                          name=f"attn_fwd_{g}")(q, k, k, v, v)


def _natural(ref, d, scr, ts):
    if d == 1:
        return ref[0]
    n = ts // d
    for r in range(d):
        v = ref[r]
        scr.at[0][pl.ds(r, n, stride=d), :] = v[:, 0:LANES]
        scr.at[1][pl.ds(r, n, stride=d), :] = v[:, LANES:2 * LANES]
    return jnp.concatenate([scr[0], scr[1]], axis=1)


def _attn_combine(oml, ts=256):
    S = oml[0][0].shape[0]

    def body(*refs):
        ins, (att_ref, out_ref, lse_ref, scr) = refs[:9], refs[9:]
        o, m, l = [[_natural(ins[3 * g + k], d, scr, ts) for g, d in enumerate(DILS)] for k in range(3)]
        mx = jnp.maximum(jnp.maximum(m[0], m[1]), m[2])
        w = [jnp.exp(m[g] - mx) for g in range(3)]
        den = w[0] * l[0] + w[1] * l[1] + w[2] * l[2]
        out = (w[0] * o[0] + w[1] * o[1] + w[2] * o[2]) / den
        out_ref[...] = out
        att_ref[...] = out.astype(MXU)
        lse_ref[...] = mx + jnp.log(den)

    in_specs = [pl.BlockSpec((d, ts // d, ATT_O), lambda i: (0, i, 0)) for d in DILS for _ in range(3)]
    args = [a.reshape(d, S // d, ATT_O) for d, grp in zip(DILS, oml) for a in grp]
    blk = pl.BlockSpec((ts, ATT_O), lambda i: (i, 0))
    return pl.pallas_call(body, grid=(S // ts,), in_specs=in_specs, out_specs=[blk, blk, blk],
                          out_shape=[jax.ShapeDtypeStruct((S, ATT_O), MXU), jax.ShapeDtypeStruct((S, ATT_O), F32),
                                     jax.ShapeDtypeStruct((S, ATT_O), F32)],
                          scratch_shapes=[pltpu.VMEM((2, ts, LANES), F32)], compiler_params=_params(("arbitrary",), 32),
                          name="attn_combine")(*args)


def _attn_bwd_prep(datt, o, lse, ts=256):
    S = datt.shape[0]

    def body(da0, da1, o_ref, l0, l1, *rest):
        outs, dl = rest[:9], rest[9]
        prod = jnp.concatenate([da0[...], da1[...]], axis=1) * o_ref[...]
        delta = jnp.zeros((ts, ATT_O), F32)
        for hm in _head_masks((ts, ATT_O)):
            delta = jnp.where(hm, jnp.sum(_only(hm, prod), axis=1, keepdims=True), delta)
        dl[0] = delta[:, 0:LANES]
        dl[1] = delta[:, LANES:2 * LANES]
        for g, d in enumerate(DILS):
            n = ts // d
            for r in range(d):
                rows = _rows_of(r, n, d)
                outs[g][r] = jnp.concatenate([da0[rows, :], da1[rows, :]], axis=1).astype(MXU)
                outs[3 + g][r] = jnp.concatenate([dl.at[0][rows, :], dl.at[1][rows, :]], axis=1)
                outs[6 + g][r] = jnp.concatenate([l0[rows, :], l1[rows, :]], axis=1)

    half = lambda j: pl.BlockSpec((ts, LANES), lambda i: (i, j))
    out_specs = [pl.BlockSpec((d, ts // d, ATT_O), lambda i: (0, i, 0)) for _ in range(3) for d in DILS]
    out_shape = [jax.ShapeDtypeStruct((d, S // d, ATT_O), dt) for dt in (MXU, F32, F32) for d in DILS]
    res = pl.pallas_call(body, grid=(S // ts,), in_specs=[half(0), half(1), pl.BlockSpec((ts, ATT_O), lambda i: (i, 0)), half(0), half(1)],
                         out_specs=out_specs, out_shape=out_shape, scratch_shapes=[pltpu.VMEM((2, ts, LANES), F32)],
                         compiler_params=_params(("arbitrary",), 32), name="attn_bwd_prep")(datt, datt, o, lse, lse)
    return [[res[k * 3 + g].reshape(S, ATT_O) for g in range(3)] for k in range(3)]


def _head_col(x, h):
    return x[:, h * 64:h * 64 + 1]


def _attn_dq(g, q, k, v, do, delta, lse):
    S = q.shape[0]
    nb = S // QB
    nblk = nb // DILS[g]

    def body(q_ref, kc_ref, kp_ref, vc_ref, vp_ref, do_ref, dl_ref, lse_ref, dq_ref):
        hms = _head_masks((2 * QB, ATT_O))
        for j in range(ASUB):
            ok = _band_mask_keys(((pl.program_id(0) * ASUB + j) & (nblk - 1)) > 0)
            k2 = jnp.concatenate([kp_ref[...] if j == 0 else _sub(kc_ref, j - 1), _sub(kc_ref, j)], axis=0)
            v2 = jnp.concatenate([vp_ref[...] if j == 0 else _sub(vc_ref, j - 1), _sub(vc_ref, j)], axis=0)
            qv, dov, dl, lse_v = _sub(q_ref, j), _sub(do_ref, j), _sub(dl_ref, j), _sub(lse_ref, j)
            dq = jnp.zeros((QB, ATT_O), F32)
            for h, hm in enumerate(hms):
                kh = _only(hm, k2)
                p = jnp.where(ok, jnp.exp(_dot_nt(qv, kh) * ATT_SCALE - _head_col(lse_v, h)), 0.0)
                ds = p * (_dot_nt(dov, _only(hm, v2)) - _head_col(dl, h))
                dq = dq + _dot(ds.astype(MXU), kh)
            dq_ref[j * QB:(j + 1) * QB] = dq * ATT_SCALE

    return pl.pallas_call(body, grid=(nb // ASUB,), in_specs=[_BIG, _BIG, _PREV, _BIG, _PREV, _BIG, _BIG, _BIG], out_specs=_BIG,
                          out_shape=jax.ShapeDtypeStruct((S, ATT_O), F32),
                          compiler_params=_params(("arbitrary",), 32), name=f"attn_dq_{g}")(q, k, k, v, v, do, delta, lse)


def _attn_dkv(g, q, k, v, do, delta, lse):
    S = q.shape[0]
    nb = S // QB
    nblk = nb // DILS[g]

    def body(k_ref, v_ref, qc_ref, qn_ref, doc_ref, don_ref, dlc_ref, dln_ref, lc_ref, ln_ref, dk_ref, dv_ref):
        hms = _head_masks((2 * QB, ATT_O))

        def both(cur_ref, nxt_ref, j):
            return jnp.concatenate([_sub(cur_ref, j), nxt_ref[...] if j == ASUB - 1 else _sub(cur_ref, j + 1)], axis=0)

        for j in range(ASUB):
            ok = _band_mask_queries(((pl.program_id(0) * ASUB + j + 1) & (nblk - 1)) > 0)
            q2, do2, dl2, lse2 = both(qc_ref, qn_ref, j), both(doc_ref, don_ref, j), both(dlc_ref, dln_ref, j), both(lc_ref, ln_ref, j)
            kv, vv = _sub(k_ref, j), _sub(v_ref, j)
            dk = jnp.zeros((QB, ATT_O), F32)
            dv = jnp.zeros((QB, ATT_O), F32)
            for h, hm in enumerate(hms):
                qh, doh = _only(hm, q2), _only(hm, do2)
                p = jnp.where(ok, jnp.exp(_dot_nt(qh, kv) * ATT_SCALE - _head_col(lse2, h)), 0.0)
                ds = p * (_dot_nt(doh, vv) - _head_col(dl2, h))
                dv = dv + _dot_tn(p.astype(MXU), doh)
                dk = dk + _dot_tn(ds.astype(MXU), qh)
            dk_ref[j * QB:(j + 1) * QB] = dk * ATT_SCALE
            dv_ref[j * QB:(j + 1) * QB] = dv

    nxt = pl.BlockSpec((QB, ATT_O), lambda b: (jnp.minimum((b + 1) * ASUB, nb - 1), 0))
    shp = jax.ShapeDtypeStruct((S, ATT_O), F32)
    return pl.pallas_call(body, grid=(nb // ASUB,), in_specs=[_BIG, _BIG, _BIG, nxt, _BIG, nxt, _BIG, nxt, _BIG, nxt], out_specs=[_BIG, _BIG],
                          out_shape=[shp, shp], compiler_params=_params(("arbitrary",), 32),
                          name=f"attn_dkv_{g}")(k, v, q, q, do, do, delta, delta, lse, lse)


def _merge_fwd(x0, u, a2, yb, att, wa, wb, wc, w_out, g_post, ts=256):
    S = x0.shape[0]

    def body(i, g, x_ref, gate_ref, a2_ref, yb_ref, att_ref, wa_ref, wb_ref, wc_ref, wo_ref, gp_ref, mg_ref, y_ref, xo_ref):
        merged = jax.nn.sigmoid(gate_ref[:, 0:D]) * _dot_nt(a2_ref[...], wa_ref[...])
        merged = merged + jax.nn.sigmoid(gate_ref[:, D:2 * D]) * _dot_nt(yb_ref[...], wb_ref[...])
        merged = merged + jax.nn.sigmoid(gate_ref[:, 2 * D:3 * D]) * _dot_nt(att_ref[...], wc_ref[...])
        mb = merged.astype(MXU)
        mg_ref[...] = mb
        y = _dot(mb, wo_ref[...])
        y_ref[...] = y
        xo_ref[...] = x_ref[...] + _rms(y, gp_ref[...])[0]

    ins = [("t", x0, D, 0), ("t", u, GATE_W, 0), ("t", a2, POOLW, 0), ("t", yb, POOLW, 0), ("t", att, ATT_O, 0),
           ("w", wa), ("w", wb), ("w", wc), ("w", w_out), ("w", g_post)]
    return _rows_call("merge_fwd", body, S, ts, ins, [("t", D, MXU), ("t", D, F32), ("t", D, F32)])


def _merge_bwd(dx, y1, u, a2, yb, att, wa, wb, wc, w_out, g_post, ts=256):
    S = dx.shape[0]

    def body(i, g, dx_ref, y_ref, gate_ref, a2_ref, yb_ref, att_ref, wa_ref, wb_ref, wc_ref, wo_ref, gp_ref,
             dy_ref, dgate_ref, dbra_ref, dbrb_ref, dbrc_ref, da2_ref, dyb_ref, datt_ref, dgp_ref):
        dxv, y = dx_ref[...], y_ref[...]
        dy, r = _rms_bwd(dxv * gp_ref[...], y)
        _acc(dgp_ref, g, jnp.sum(dxv * (y * r), axis=0, keepdims=True))
        dyb16 = dy.astype(MXU)
        dy_ref[...] = dyb16
        dm = _dot_nt(dyb16, wo_ref[...])
        for n, (src, w_ref, dbr_ref, din_ref) in enumerate(((a2_ref, wa_ref, dbra_ref, da2_ref), (yb_ref, wb_ref, dbrb_ref, dyb_ref),
                                                           (att_ref, wc_ref, dbrc_ref, datt_ref))):
            gt = jax.nn.sigmoid(gate_ref[:, n * D:(n + 1) * D])
            br = _dot_nt(src[...], w_ref[...])
            dgate_ref[:, n * D:(n + 1) * D] = (dm * br * gt * (1.0 - gt)).astype(dgate_ref.dtype)
            dbr = (dm * gt).astype(MXU)
            dbr_ref[...] = dbr
            din_ref[...] = _dot(dbr, w_ref[...])

    ins = [("t", dx, D, 0), ("t", y1, D, 0), ("t", u, GATE_W, 0), ("t", a2, POOLW, 0), ("t", yb, POOLW, 0), ("t", att, ATT_O, 0),
           ("w", wa), ("w", wb), ("w", wc), ("w", w_out), ("w", g_post)]
    outs = [("t", D, MXU), ("c", IN_W, GATE_W, 0, MXU), ("t", D, MXU), ("t", D, MXU), ("t", D, MXU), ("t", POOLW, F32), ("t", POOLW, F32),
            ("t", ATT_O, F32), ("a", (1, D), F32)]
    return _rows_call("merge_bwd", body, S, ts, ins, outs)


def _prenorm_bwd(name, dx_res, du, wt, x, g_pre, ts=256, lead=0):
    S = x.shape[0]
    N = du.shape[1]

    def body(i, g, dx_ref, du_ref, wt_ref, x_ref, g_ref, o_ref, dg_ref):
        if lead:
            dhv = _dot(du_ref[:, 0:lead], wt_ref[N - lead:N, :]) + _dot(du_ref[:, lead:N], wt_ref[0:N - lead, :])
        else:
            dhv = _dot(du_ref[...], wt_ref[...])
        xv = x_ref[...]
        dxn, r = _rms_bwd(dhv * g_ref[...], xv)
        o_ref[...] = dx_ref[...] + dxn
        _acc(dg_ref, g, jnp.sum(dhv * (xv * r), axis=0, keepdims=True))

    ins = [("t", dx_res, D, 0), ("t", du, N, 0), ("w", wt), ("t", x, D, 0), ("w", g_pre)]
    return _rows_call(name, body, S, ts, ins, [("t", D, F32), ("a", (1, D), F32)], vmem_mb=52)


def _mem_heads(qm, kv_ref):
    out = []
    for h in range(4):
        q = qm[:, h * 128:(h + 1) * 128].astype(MXU)
        k = kv_ref[:, h * 128:(h + 1) * 128]
        v = kv_ref[:, MEM_W + h * 128:MEM_W + (h + 1) * 128]
        sc = _dot_nt(q, k) * MEM_SCALE
        e = jnp.exp(sc - jnp.max(sc, axis=1, keepdims=True))
        out.append((e / jnp.sum(e, axis=1, keepdims=True), q, k, v))
    return out


def _mem_fwd(x1, kv, g_pre, w_mq, w_mo, g_post, ts=256):
    S = x1.shape[0]

    def body(i, g, x_ref, kv_ref, gq_ref, wq_ref, wo_ref, gp_ref, om_ref, h_ref, y_ref, xo_ref):
        x = x_ref[...]
        hb = _rms(x, gq_ref[...])[0].astype(MXU)
        h_ref[...] = hb
        qm = _dot(hb, wq_ref[...])
        om = jnp.concatenate([_dot(p.astype(MXU), v) for p, _, _, v in _mem_heads(qm, kv_ref)], axis=1).astype(MXU)
        om_ref[...] = om
        y = _dot_nt(om, wo_ref[...])
        y_ref[...] = y
        xo_ref[...] = x + _rms(y, gp_ref[...])[0]

    ins = [("t", x1, D, 0), ("w", kv), ("w", g_pre), ("w", w_mq), ("w", w_mo), ("w", g_post)]
    return _rows_call("mem_fwd", body, S, ts, ins, [("t", MEM_W, MXU), ("t", D, MXU), ("t", D, F32), ("t", D, F32)])


def _mem_bwd(dx2, ym, x1, kv, g_pre, w_mq, w_mo, g_post, ts=256):
    S = x1.shape[0]

    def body(i, g, dx_ref, y_ref, x_ref, kv_ref, gq_ref, wq_ref, wo_ref, gp_ref, dy_ref, dq_ref, dxo_ref, dgp_ref, dgq_ref, dkv_ref):
        dxv, y, x = dx_ref[...], y_ref[...], x_ref[...]
        dy, r = _rms_bwd(dxv * gp_ref[...], y)
        _acc(dgp_ref, g, jnp.sum(dxv * (y * r), axis=0, keepdims=True))
        dyb = dy.astype(MXU)
        dy_ref[...] = dyb
        dom = _dot(dyb, wo_ref[...])
        h, r1 = _rms(x, gq_ref[...])
        qm = _dot(h.astype(MXU), wq_ref[...])
        dqs = []

        @pl.when(g == 0)
        def _():
            dkv_ref[...] = jnp.zeros_like(dkv_ref)

        for hh, (p, q, k, v) in enumerate(_mem_heads(qm, kv_ref)):
            doh = dom[:, hh * 128:(hh + 1) * 128].astype(MXU)
            dp = _dot_nt(doh, v)
            dsc = (p * (dp - jnp.sum(dp * p, axis=1, keepdims=True)) * MEM_SCALE).astype(MXU)
            dqs.append(_dot(dsc, k))
            dkv_ref[:, hh * 128:(hh + 1) * 128] += _dot_tn(dsc, q)
            dkv_ref[:, MEM_W + hh * 128:MEM_W + (hh + 1) * 128] += _dot_tn(p.astype(MXU), doh)
        dq = jnp.concatenate(dqs, axis=1).astype(MXU)
        dq_ref[...] = dq
        dh = _dot_nt(dq, wq_ref[...])
        _acc(dgq_ref, g, jnp.sum(dh * (x * r1), axis=0, keepdims=True))
        dxo_ref[...] = dxv + _rms_bwd(dh * gq_ref[...], x)[0]

    ins = [("t", dx2, D, 0), ("t", ym, D, 0), ("t", x1, D, 0), ("w", kv), ("w", g_pre), ("w", w_mq), ("w", w_mo), ("w", g_post)]
    outs = [("t", D, MXU), ("t", MEM_W, MXU), ("t", D, F32), ("a", (1, D), F32), ("a", (1, D), F32), ("a", (256, D), F32)]
    return _rows_call("mem_bwd", body, S, ts, ins, outs)


def _gain_grad(name, dn, x):
    n = x.shape[0]

    def body(i, g, dn_ref, x_ref, o_ref):
        xv = x_ref[...]
        r = lax.rsqrt(jnp.mean(xv * xv, axis=-1, keepdims=True) + EPS)
        o_ref[...] = jnp.sum(dn_ref[...] * (xv * r), axis=0, keepdims=True)

    return _rows_call(name, body, n, n, [("t", dn, D, 0), ("t", x, D, 0)], [("a", (1, D), F32)])[0]


def _ffn_fwd(x2, u3, conv_f, w_down, g_post, ts=256):
    S = x2.shape[0]

    def body(i, g, x_ref, ua_ref, ub_ref, cw_ref, wd_ref, gp_ref, act_ref, y_ref, xo_ref, cu):
        @pl.when(g == 0)
        def _():
            cu[...] = jnp.zeros_like(cu)

        ua = ua_ref[...]
        c, _, _ = _conv3(ua, cu[...], cw_ref[...])
        act = (c * jax.nn.sigmoid(c) * ub_ref[...]).astype(MXU)
        act_ref[...] = act
        y = _dot(act, wd_ref[...])
        y_ref[...] = y
        xo_ref[...] = x_ref[...] + _rms(y, gp_ref[...])[0]
        cu[...] = ua[ts - 8:]

    ins = [("t", x2, D, 0), ("t", u3, D_FF, 0), ("t", u3, D_FF, 1), ("w", conv_f), ("w", w_down), ("w", g_post)]
    return _rows_call("ffn_fwd", body, S, ts, ins, [("t", D_FF, MXU), ("t", D, F32), ("t", D, F32)],
                      scratch=[pltpu.VMEM((8, D_FF), F32)], vmem_mb=56)


def _ffn_bwd(dx3, y3, u3, conv_f, w_down, g_post, ts=128):
    S = dx3.shape[0]

    def body(i, g, dx_ref, y_ref, ua_ref, ub_ref, uap_ref, cw_ref, wd_ref, gp_ref, dy_ref, du_ref, dgp_ref, dcw_ref, cdc):
        @pl.when(g == 0)
        def _():
            cdc[...] = jnp.zeros_like(cdc)

        dxv, y = dx_ref[...], y_ref[...]
        dy, r = _rms_bwd(dxv * gp_ref[...], y)
        _acc(dgp_ref, g, jnp.sum(dxv * (y * r), axis=0, keepdims=True))
        dyb = dy.astype(MXU)
        dy_ref[...] = dyb
        dact = _dot_nt(dyb, wd_ref[...])
        ua, w = ua_ref[...], cw_ref[...]
        c, u1, u2 = _conv3(ua, uap_ref[...] * (i > 0).astype(F32), w)
        sg = jax.nn.sigmoid(c)
        du_ref[:, D_FF:2 * D_FF] = (dact * (c * sg)).astype(du_ref.dtype)
        dc = dact * ub_ref[...] * (sg * (1.0 + c * (1.0 - sg)))
        du_ref[:, 0:D_FF] = _conv3_t(dc, cdc[...], w).astype(du_ref.dtype)
        dw = jnp.concatenate([jnp.sum(dc * u2, axis=0, keepdims=True), jnp.sum(dc * u1, axis=0, keepdims=True),
                              jnp.sum(dc * ua, axis=0, keepdims=True)], axis=0)
        _acc(dcw_ref, g, dw)
        cdc[...] = dc[:8]

    ins = [("t", dx3, D, 0), ("t", y3, D, 0), ("t", u3, D_FF, 0), ("t", u3, D_FF, 1), ("h", u3, 8, D_FF, 0), ("w", conv_f),
           ("w", w_down), ("w", g_post)]
    outs = [("t", D, MXU), ("t", 2 * D_FF, MXU), ("a", (1, D), F32), ("a", (3, D_FF), F32)]
    return _rows_call("ffn_bwd", body, S, ts, ins, outs, scratch=[pltpu.VMEM((8, D_FF), F32)], reverse=True, vmem_mb=56)


def _loss_head(x, target, ts=512):
    S = x.shape[0]

    def body(i, g, x_ref, t_ref, dx_ref, acc_ref):
        diff = x_ref[...] - t_ref[...]
        dx_ref[...] = diff * (1.0 / D)
        col = jnp.sum(diff * diff, axis=0, keepdims=True)
        part = col[:, 0:LANES]
        for j in range(1, D // LANES):
            part = part + col[:, j * LANES:(j + 1) * LANES]
        row = lax.broadcasted_iota(jnp.int32, (8, LANES), 0)
        _acc(acc_ref, g, jnp.where(row == 0, jnp.broadcast_to(part, (8, LANES)), 0.0))

    return _rows_call("loss_head", body, S, ts, [("t", x, D, 0), ("t", target, D, 0)], [("t", D, F32), ("a", (8, LANES), F32)])


_OPERAND_NAME = dict(w_in='w_in', w_branch_a='wa', w_branch_b='wb', w_branch_c='wc', w_out='w_out', w_mq='w_mq', w_mkv='w_mkv',
                     w_mo='w_mo', w_up='w_up', w_down='w_down')


def _big_operands(big):
    return {_OPERAND_NAME[n]: a for n, a in big.items()}


def _layer_weights(big, small, l):
    pool_w = small['pool_w'][l].astype(MXU)
    wblk = jnp.zeros((POOLW, POOLW), MXU)
    for g in range(4):
        wblk = lax.dynamic_update_slice(wblk, pool_w[g], (g * 96, g * 96))
    vec = lambda n: small[n][l].reshape(1, -1)
    return dict(
        _big_operands(big),
        wblk=wblk, pool_scale=vec('pool_scale'), conv_b=small['conv_b_w'][l], conv_f=small['conv_ffn_w'][l],
        g_mix_pre=vec('norm_mix_pre'), g_mix_post=vec('norm_mix_post'), g_mem_pre=vec('norm_mem_pre'),
        g_mem_post=vec('norm_mem_post'), g_memkv=vec('norm_memkv'), g_ffn_pre=vec('norm_ffn_pre'), g_ffn_post=vec('norm_ffn_post'))


def _layer_fwd(x0, mem, W, ctab, stab):
    sv = _layer_fwd_mix(x0, W, ctab, stab)
    return _layer_fwd_late(mem, W, sv), sv


def _layer_fwd_mix(x0, W, ctab, stab):
    sv = dict(x0=x0)
    sv['u'], sv['h1'] = _norm_mm("in_proj", x0, W['g_mix_pre'], W['w_in'], ts=1024, tn=IN_TILE, wt=True, rot=IN_ROT)
    sv['a2'], sv['yb'] = _poolconv_fwd(sv['u'], W['wblk'], W['pool_scale'], W['conv_b'])
    sv['qkv'] = q3, k3, v3 = _rope_perm(sv['u'], ctab, stab)
    sv['att'], sv['o'], sv['lse'] = _attn_combine([_attn_fwd(g, q3[g], k3[g], v3[g]) for g in range(3)])
    sv['merged'], sv['y1'], sv['x1'] = _merge_fwd(x0, sv['u'], sv['a2'], sv['yb'], sv['att'], W['wa'], W['wb'], W['wc'],
                                                  W['w_out'], W['g_mix_post'])
    return sv


def _layer_fwd_late(mem, W, sv):
    sv['kv'], sv['memn'] = _norm_mm("mem_kv", mem, W['g_memkv'], W['w_mkv'], ts=256, tn=D, out_dtype=MXU)
    sv['om'], sv['h2'], sv['ym'], sv['x2'] = _mem_fwd(sv['x1'], sv['kv'], W['g_mem_pre'], W['w_mq'], W['w_mo'], W['g_mem_post'])
    sv['u3'], sv['h3'] = _norm_mm("up_proj", sv['x2'], W['g_ffn_pre'], W['w_up'], ts=1024, tn=1408, wt=True)
    sv['act'], sv['y3'], x3 = _ffn_fwd(sv['x2'], sv['u3'], W['conv_f'], W['w_down'], W['g_ffn_post'])
    return x3


def _layer_bwd(dx3, mem, W, sv, ctab, stab):
    dx1, g = _layer_bwd_late(dx3, mem, W, sv)
    dx0, g_mix = _layer_bwd_mix(dx1, W, sv, ctab, stab)
    return dx0, {**g, **g_mix}


def _layer_bwd_late(dx3, mem, W, sv):
    g = {}
    dy3, du3, g['norm_ffn_post'], g['conv_ffn_w'] = _ffn_bwd(dx3, sv['y3'], sv['u3'], W['conv_f'], W['w_down'], W['g_ffn_post'])
    g['w_down'] = _mm_tn("dw_down", sv['act'], dy3, cap_k=256)
    g['w_up'] = _mm_tn("dw_up", du3, sv['h3'])
    dx2, g['norm_ffn_pre'] = _prenorm_bwd("ffn_pre_bwd", dx3, du3, W['w_up'], sv['x2'], W['g_ffn_pre'])
    dym, dqm, dx1, g['norm_mem_post'], g['norm_mem_pre'], dkv = _mem_bwd(dx2, sv['ym'], sv['x1'], sv['kv'], W['g_mem_pre'],
                                                                       W['w_mq'], W['w_mo'], W['g_mem_post'])
    g['w_mo'] = _mm_tn("dw_mo", dym, sv['om'])
    g['w_mq'] = _mm_tn("dw_mq", sv['h2'], dqm)
    dkvb = dkv.astype(MXU)
    g['w_mkv'] = _mm_tn("dw_mkv", sv['memn'], dkvb)
    g['norm_memkv'] = _gain_grad("memkv_gain", _mm_nt("d_memn", dkvb, W['w_mkv'], ts=256, tn=512), mem)
    return dx1, g


def _layer_bwd_mix(dx1, W, sv, ctab, stab):
    du, g = _layer_bwd_mixers(dx1, W, sv, ctab, stab)
    g['w_in'] = _dw_in(du, sv)
    dx0, g['norm_mix_pre'] = _mix_pre_bwd(dx1, du, W, sv)
    return dx0, g


def _dw_in(du, sv):
    return _mm_tn("dw_in", du, sv['h1'], cap_k=IN_TILE, rot=IN_ROT)


def _mix_pre_bwd(dx1, du, W, sv):
    return _prenorm_bwd("mix_pre_bwd", dx1, du, W['w_in'], sv['x0'], W['g_mix_pre'], lead=GATE_W)


def _layer_bwd_mixers(dx1, W, sv, ctab, stab):
    g = {}
    dy1, du, dbra, dbrb, dbrc, da2, dyb, datt, g['norm_mix_post'] = _merge_bwd(
        dx1, sv['y1'], sv['u'], sv['a2'], sv['yb'], sv['att'], W['wa'], W['wb'], W['wc'], W['w_out'], W['g_mix_post'])
    g['w_out'] = _mm_tn("dw_out", sv['merged'], dy1)
    g['w_branch_a'] = _mm_tn("dw_a", dbra, sv['a2'])
    g['w_branch_b'] = _mm_tn("dw_b", dbrb, sv['yb'])
    g['w_branch_c'] = _mm_tn("dw_c", dbrc, sv['att'])
    du, g['pool_scale'], dwblk, g['conv_b_w'] = _poolconv_bwd(sv['u'], da2, dyb, du, W['wblk'], W['pool_scale'], W['conv_b'])
    g['pool_w'] = jnp.stack([dwblk[k * 96:(k + 1) * 96, k * 96:(k + 1) * 96] for k in range(4)])
    q3, k3, v3 = sv['qkv']
    do3, dl3, lse3 = _attn_bwd_prep(datt, sv['o'], sv['lse'])
    dq3 = [_attn_dq(i, q3[i], k3[i], v3[i], do3[i], dl3[i], lse3[i]) for i in range(3)]
    dkv3 = [_attn_dkv(i, q3[i], k3[i], v3[i], do3[i], dl3[i], lse3[i]) for i in range(3)]
    du = _rope_unperm_bwd([dq3, [a for a, _ in dkv3], [b for _, b in dkv3]], du, ctab, stab)
    return du, g


def _local_step(x, mem, positions, target, big, small):
    ctab, stab = _rope_tables(positions)
    Ws = [_layer_weights(big[l], small, l) for l in range(DEPTH)]
    saved = []
    for l in range(DEPTH):
        x, sv = _layer_fwd(x, mem, Ws[l], ctab, stab)
        saved.append(sv)
    dx, acc = _loss_head(x, target)
    loss = jnp.sum(acc) * (0.5 / D)
    grads = [None] * DEPTH
    for l in reversed(range(DEPTH)):
        dx, grads[l] = _layer_bwd(dx, mem, Ws[l], saved[l], ctab, stab)
    return loss, dx, grads


_HBM = pl.BlockSpec(memory_space=pl.ANY)
MESH_ID = pl.DeviceIdType.MESH


def _all_gather(name, xs):
    n = len(xs)

    def body(*refs):
        x_refs, out_refs = refs[:n], refs[n:2 * n]
        send_sems, recv_sems, local_sems = refs[2 * n:]
        x, y, c = lax.axis_index("x"), lax.axis_index("y"), lax.axis_index("c")
        me, sibling = (x, y, c), (x, y, 1 - c)
        chips = [(1 - x, y), (x, 1 - y), (1 - x, 1 - y)]

        def slot(a, p):
            return out_refs[a].at[4 * p[0] + 2 * p[1] + p[2]]

        def copy(a, k, block, to, src=None):
            return pltpu.make_async_remote_copy(src_ref=slot(a, block) if src is None else src, dst_ref=slot(a, block),
                                                send_sem=send_sems.at[a, k], recv_sem=recv_sems.at[a, k], device_id=to,
                                                device_id_type=MESH_ID)

        started = []
        for a in range(n):
            mine = pltpu.make_async_copy(x_refs[a], slot(a, me), local_sems.at[a])
            mine.start()
            started.append(mine)
        first = []
        for a in range(n):
            first.append(copy(a, 0, me, sibling, src=x_refs[a]))
            first += [copy(a, 1 + j, me, (*chip, c), src=x_refs[a]) for j, chip in enumerate(chips)]
        for cp in first:
            cp.start()
        passed = []
        for j, chip in enumerate(chips):
            for a in range(n):
                copy(a, 1 + j, (*chip, c), me).wait_recv()
                fw = copy(a, 4 + j, (*chip, c), sibling)
                fw.start()
                passed.append(fw)
        for a in range(n):
            copy(a, 0, sibling, me).wait_recv()
            for j, chip in enumerate(chips):
                copy(a, 4 + j, (*chip, 1 - c), me).wait_recv()
        for cp in first + passed:
            cp.wait_send()
        for mine in started:
            mine.wait()

    return pl.pallas_call(
        body, out_shape=[jax.ShapeDtypeStruct((N_DEV,) + x.shape, x.dtype) for x in xs], in_specs=[_HBM] * n, out_specs=[_HBM] * n,
        scratch_shapes=[pltpu.SemaphoreType.DMA((n, 7)), pltpu.SemaphoreType.DMA((n, 7)), pltpu.SemaphoreType.DMA((n,))],
        name=name)(*xs)


def _exchange(name, gs):
    n = len(gs)

    def body(*refs):
        g_refs, out_refs = refs[:n], refs[n:2 * n]
        send_sems, recv_sems, local_sems = refs[2 * n:]
        x, y, c = lax.axis_index("x"), lax.axis_index("y"), lax.axis_index("c")
        me = 4 * x + 2 * y + c
        copies = []
        for a in range(n):
            mine = pltpu.make_async_copy(g_refs[a].at[me], out_refs[a].at[me], local_sems.at[a])
            mine.start()
            copies.append(mine)
        for r in range(1, N_DEV):
            px, py, pc = x ^ ((r >> 2) & 1), y ^ ((r >> 1) & 1), c ^ (r & 1)
            for a in range(n):
                cp = pltpu.make_async_remote_copy(src_ref=g_refs[a].at[4 * px + 2 * py + pc], dst_ref=out_refs[a].at[me],
                                                  send_sem=send_sems.at[a, r - 1], recv_sem=recv_sems.at[a, r - 1],
                                                  device_id=(px, py, pc), device_id_type=MESH_ID)
                cp.start()
                copies.append(cp)
        for cp in copies:
            cp.wait()

    return pl.pallas_call(
        body, out_shape=[jax.ShapeDtypeStruct(g.shape, g.dtype) for g in gs], in_specs=[_HBM] * n, out_specs=[_HBM] * n,
        scratch_shapes=[pltpu.SemaphoreType.DMA((n, N_DEV - 1)), pltpu.SemaphoreType.DMA((n, N_DEV - 1)), pltpu.SemaphoreType.DMA((n,))],
        name=name)(*gs)


_SEM = pl.BlockSpec(memory_space=pltpu.SEMAPHORE)
_IN_HBM = pl.BlockSpec(memory_space=pltpu.HBM)
_SIDE_EFFECT = pltpu.SideEffectType.DATAFLOW_SIDE_EFFECTING


def _push_copies(src_refs, land_refs, send_sems, recv_sems, per_peer):
    x, y, c = lax.axis_index("x"), lax.axis_index("y"), lax.axis_index("c")
    me = 4 * x + 2 * y + c
    copies = []
    for r in range(1, N_DEV):
        px, py, pc = x ^ ((r >> 2) & 1), y ^ ((r >> 1) & 1), c ^ (r & 1)
        for a, (s, d) in enumerate(zip(src_refs, land_refs)):
            k = a * (N_DEV - 1) + r - 1
            copies.append(pltpu.make_async_remote_copy(src_ref=s.at[4 * px + 2 * py + pc] if per_peer else s, dst_ref=d.at[me],
                                                       send_sem=send_sems.at[k], recv_sem=recv_sems.at[k],
                                                       device_id=(px, py, pc), device_id_type=MESH_ID))
    return copies


def _push_start(name, srcs, per_peer, after):
    n = len(srcs)
    lands = [lax.empty((N_DEV,) + (s.shape[1:] if per_peer else s.shape), s.dtype) for s in srcs]

    def body(*refs):
        for cp in _push_copies(refs[:n], refs[n:2 * n], refs[2 * n + 1], refs[2 * n + 2], per_peer):
            cp.start()
        refs[-1][...] = jnp.zeros_like(refs[-1])

    hbm = [pltpu.HBM(a.shape, a.dtype) for a in (*srcs, *lands)]
    sems = pltpu.SemaphoreType.DMA((n * (N_DEV - 1),))
    out = pl.pallas_call(
        body, name=name, out_shape=(sems, sems, *hbm, jax.ShapeDtypeStruct((8, LANES), F32)),
        in_specs=[_IN_HBM] * (2 * n) + [pl.BlockSpec(memory_space=pl.ANY)],
        out_specs=(_SEM, _SEM, *[_IN_HBM] * (2 * n), pl.BlockSpec(memory_space=pltpu.VMEM)),
        input_output_aliases={a: 2 + a for a in range(2 * n)},
        compiler_params=pltpu.CompilerParams(has_side_effects=_SIDE_EFFECT),
    )(*[pltpu.with_memory_space_constraint(a, pltpu.HBM) for a in (*srcs, *lands)], after)
    return out[0], out[1], out[2:2 + n], out[2 + n:2 + 2 * n], out[-1]


def _push_wait(name, started, per_peer, after):
    send_sems, recv_sems, srcs, lands, _ = started
    n = len(srcs)

    def body(*refs):
        for cp in _push_copies(refs[:n], refs[n:2 * n], refs[2 * n], refs[2 * n + 1], per_peer):
            cp.wait_send()
            cp.wait_recv()

    out = pl.pallas_call(
        body, name=name, out_shape=[pltpu.HBM(a.shape, a.dtype) for a in (*srcs, *lands)],
        in_specs=[_IN_HBM] * (2 * n) + [_SEM, _SEM, pl.BlockSpec(memory_space=pl.ANY)], out_specs=[_IN_HBM] * (2 * n),
        input_output_aliases={a: a for a in range(2 * n)},
        compiler_params=pltpu.CompilerParams(has_side_effects=_SIDE_EFFECT),
    )(*srcs, *lands, send_sems, recv_sems, after)
    return out[n:]


def _my_slot():
    return 4 * lax.axis_index("x") + 2 * lax.axis_index("y") + lax.axis_index("c")


def _row_tile(rows, cols, budget):
    if rows * cols * 4 <= budget or rows % 16:
        return rows
    best = 16
    for t in range(16, rows + 1, 16):
        if rows % t == 0 and t * cols * 4 <= budget:
            best = t
    return best


def _sum_slots(name, recv):
    _, R, C = recv.shape
    tr = _row_tile(R, C, 1 << 20)

    def body(r_ref, o_ref):
        g = r_ref[0].astype(F32)
        for k in range(1, N_DEV):
            g = g + r_ref[k].astype(F32)
        o_ref[...] = g

    return pl.pallas_call(body, grid=(R // tr,), in_specs=[pl.BlockSpec((N_DEV, tr, C), lambda i: (0, i, 0))],
                          out_specs=pl.BlockSpec((tr, C), lambda i: (i, 0)), out_shape=jax.ShapeDtypeStruct((R, C), F32),
                          compiler_params=_params(("arbitrary",), 32), name=name)(recv)


def _adamw(name, g, w, m, v):
    R, C = w.shape
    tr = _row_tile(R, C, 1 << 20)
    c1 = 1.0 - ADAM_B1 ** ADAM_STEP
    c2 = 1.0 - ADAM_B2 ** ADAM_STEP

    def body(g_ref, w_ref, m_ref, v_ref, d_ref, mo_ref, vo_ref):
        gv = g_ref[...]
        mn = ADAM_B1 * m_ref[...] + (1.0 - ADAM_B1) * gv
        vn = ADAM_B2 * v_ref[...] + (1.0 - ADAM_B2) * (gv * gv)
        mo_ref[...] = mn
        vo_ref[...] = vn
        d_ref[...] = -ADAM_LR * ((mn / c1) / (jnp.sqrt(vn / c2) + ADAM_EPS) + ADAM_WD * w_ref[...])

    blk = pl.BlockSpec((tr, C), lambda i: (i, 0))
    shp = jax.ShapeDtypeStruct((R, C), F32)
    return pl.pallas_call(body, grid=(R // tr,), in_specs=[blk, blk, blk, blk], out_specs=[blk, blk, blk], out_shape=[shp, shp, shp],
                          compiler_params=_params(("arbitrary",), 32), name=name)(g, w, m, v)


def _pad_flat(a, n):
    a = a.reshape(-1)
    return jnp.pad(a, (0, n - a.shape[0]))


def _seg(n):
    return -(-n // FLAT_ALIGN) * FLAT_ALIGN


def _to_blocks(full, axis):
    shp = full.shape
    return jnp.moveaxis(full.reshape(shp[:axis] + (N_DEV, shp[axis] // N_DEV) + shp[axis + 1:]), axis, 0)


def _from_blocks(blocks, axis):
    b = jnp.moveaxis(blocks, 0, axis)
    shp = b.shape
    return b.reshape(shp[:axis] + (shp[axis] * shp[axis + 1],) + shp[axis + 2:])


def _as_rows(shard, n):
    return shard.T if SHARD_AXIS[n] == 2 else shard


def _with_own(lands, own, me):
    return [lax.dynamic_update_slice(land, o[None], (me, 0, 0)) for land, o in zip(lands, own)]


def kernel(x, mem, positions, norm_mix_pre, norm_mix_post, w_in, pool_w, pool_scale, conv_b_w, w_branch_a, w_branch_b, w_branch_c, w_out, norm_mem_pre, norm_mem_post, norm_memkv, w_mq, w_mkv, w_mo, norm_ffn_pre, norm_ffn_post, w_up, conv_ffn_w, w_down, loss_target, m_norm_mix_pre, m_norm_mix_post, m_w_in, m_pool_w, m_pool_scale, m_conv_b_w, m_w_branch_a, m_w_branch_b, m_w_branch_c, m_w_out, m_norm_mem_pre, m_norm_mem_post, m_norm_memkv, m_w_mq, m_w_mkv, m_w_mo, m_norm_ffn_pre, m_norm_ffn_post, m_w_up, m_conv_ffn_w, m_w_down, v_norm_mix_pre, v_norm_mix_post, v_w_in, v_pool_w, v_pool_scale, v_conv_b_w, v_w_branch_a, v_w_branch_b, v_w_branch_c, v_w_out, v_norm_mem_pre, v_norm_mem_post, v_norm_memkv, v_w_mq, v_w_mkv, v_w_mo, v_norm_ffn_pre, v_norm_ffn_post, v_w_up, v_conv_ffn_w, v_w_down):
    w = dict(norm_mix_pre=norm_mix_pre, norm_mix_post=norm_mix_post, w_in=w_in, pool_w=pool_w, pool_scale=pool_scale, conv_b_w=conv_b_w, w_branch_a=w_branch_a, w_branch_b=w_branch_b, w_branch_c=w_branch_c, w_out=w_out, norm_mem_pre=norm_mem_pre, norm_mem_post=norm_mem_post, norm_memkv=norm_memkv, w_mq=w_mq, w_mkv=w_mkv, w_mo=w_mo, norm_ffn_pre=norm_ffn_pre, norm_ffn_post=norm_ffn_post, w_up=w_up, conv_ffn_w=conv_ffn_w, w_down=w_down)
    m = dict(norm_mix_pre=m_norm_mix_pre, norm_mix_post=m_norm_mix_post, w_in=m_w_in, pool_w=m_pool_w, pool_scale=m_pool_scale, conv_b_w=m_conv_b_w, w_branch_a=m_w_branch_a, w_branch_b=m_w_branch_b, w_branch_c=m_w_branch_c, w_out=m_w_out, norm_mem_pre=m_norm_mem_pre, norm_mem_post=m_norm_mem_post, norm_memkv=m_norm_memkv, w_mq=m_w_mq, w_mkv=m_w_mkv, w_mo=m_w_mo, norm_ffn_pre=m_norm_ffn_pre, norm_ffn_post=m_norm_ffn_post, w_up=m_w_up, conv_ffn_w=m_conv_ffn_w, w_down=m_w_down)
    v = dict(norm_mix_pre=v_norm_mix_pre, norm_mix_post=v_norm_mix_post, w_in=v_w_in, pool_w=v_pool_w, pool_scale=v_pool_scale, conv_b_w=v_conv_b_w, w_branch_a=v_w_branch_a, w_branch_b=v_w_branch_b, w_branch_c=v_w_branch_c, w_out=v_w_out, norm_mem_pre=v_norm_mem_pre, norm_mem_post=v_norm_mem_post, norm_memkv=v_norm_memkv, w_mq=v_w_mq, w_mkv=v_w_mkv, w_mo=v_w_mo, norm_ffn_pre=v_norm_ffn_pre, norm_ffn_post=v_norm_ffn_post, w_up=v_w_up, conv_ffn_w=v_conv_ffn_w, w_down=v_w_down)

    me = _my_slot()
    mix_big = [n for n in BIG if n not in LATE_BIG]
    block = lambda names, l: [_as_rows(w[n][l], n).astype(MXU) for n in names]
    conv = jnp.concatenate([_pad_flat(w[n], _seg(w[n].size)) for n in F32_GATHERED]).reshape(-1, LANES)
    got0 = _all_gather("weights_all_gather_0", block(mix_big, 0) + [conv])
    conv_all = got0[-1].reshape(N_DEV, -1)
    small, off = {n: w[n] for n in WEIGHTS if n not in SHARD_AXIS}, 0
    for n in F32_GATHERED:
        small[n] = _from_blocks(conv_all[:, off:off + w[n].size].reshape((N_DEV,) + w[n].shape), 2)
        off += _seg(w[n].size)
    whole = lambda names, got: {n: o.reshape(-1, o.shape[-1]) for n, o in zip(names, got)}
    pushes, after = {}, got0[0]
    for names, l in ((LATE_BIG, 0), (mix_big, 1), (LATE_BIG, 1)):
        pushes[names is LATE_BIG, l] = _push_start(f"weights_push_start_{l}{'b' if names is LATE_BIG else 'a'}", block(names, l), False, after)
        after = pushes[names is LATE_BIG, l][4]

    def arrived(late, l, done):
        names = LATE_BIG if late else mix_big
        lands = _push_wait(f"weights_push_wait_{l}{'b' if late else 'a'}", pushes[late, l], False, done)
        return _big_operands(whole(names, _with_own(lands, block(names, l), me)))

    ctab, stab = _rope_tables(positions[0])
    W0 = _layer_weights(whole(mix_big, got0), small, 0)
    sv0 = _layer_fwd_mix(x[0], dict(W0, g_mix_pre=W0['g_mix_pre'] + after[0, 0]), ctab, stab)
    W0.update(arrived(True, 0, sv0['x1']))
    x1 = _layer_fwd_late(mem[0], W0, sv0)
    W1 = _layer_weights({}, small, 1)
    W1.update(arrived(False, 1, x1))
    sv1 = _layer_fwd_mix(x1, W1, ctab, stab)
    W1.update(arrived(True, 1, sv1['x1']))
    x2 = _layer_fwd_late(mem[0], W1, sv1)
    dx, acc = _loss_head(x2, loss_target[0])
    loss = lax.psum(jnp.sum(acc) * (0.5 / D), MESH_AXES)
    grads = [None] * DEPTH
    dx, grads[1] = _layer_bwd(dx, mem[0], W1, sv1, ctab, stab)
    sent = [None, [grads[1][n].reshape(N_DEV, -1, grads[1][n].shape[-1]) for n in BIG]]
    push_g = _push_start("grads_push_start_1", sent[1], True, dx)
    dx, g_late = _layer_bwd_late(dx, mem[0], dict(W0, g_ffn_post=W0['g_ffn_post'] + push_g[4][0, 0]), sv0)
    sent_late = [g_late[n].reshape(N_DEV, -1, g_late[n].shape[-1]) for n in LATE_BIG]
    push_l = _push_start("grads_push_start_0", sent_late, True, dx)
    du, g_mix = _layer_bwd_mixers(dx, dict(W0, g_mix_post=W0['g_mix_post'] + push_l[4][0, 0]), sv0, ctab, stab)
    g_mix['w_in'] = _dw_in(du, sv0)
    sent_in = [g_mix['w_in'].reshape(N_DEV, -1, D)]
    push_i = _push_start("grads_push_start_in", sent_in, True, du)
    dx, g_mix['norm_mix_pre'] = _mix_pre_bwd(dx, du, dict(W0, g_mix_pre=W0['g_mix_pre'] + push_i[4][0, 0]), sv0)
    grads[0] = {**g_late, **g_mix}
    own = lambda s: [lax.dynamic_index_in_dim(a, me, 0, keepdims=False) for a in s]
    recv1 = _with_own(_push_wait("grads_push_wait_1", push_g, True, dx), own(sent[1]), me)
    recv_late = _with_own(_push_wait("grads_push_wait_0", push_l, True, dx), own(sent_late), me)
    mix_rest = [n for n in mix_big if n != 'w_in']

    misc_names = [n for n in WEIGHTS if n not in BIG]
    stacked = {n: jnp.stack([grads[l][n].reshape(small[n].shape[1:]) for l in range(DEPTH)]) for n in misc_names}
    rows = [(_to_blocks(stacked[n], 2) if n in SHARD_AXIS else jnp.broadcast_to(stacked[n][None], (N_DEV,) + stacked[n].shape))
            for n in misc_names]
    segs = [_seg(w[n].size) for n in misc_names]
    misc = jnp.concatenate([jnp.pad(r.reshape(N_DEV, -1), ((0, 0), (0, s - r[0].size))) for r, s in zip(rows, segs)],
                           axis=1).reshape(N_DEV, -1, LANES)
    recv_mix = _exchange("grad_exchange_0", [g_mix[n].reshape(N_DEV, -1, g_mix[n].shape[-1]) for n in mix_rest] + [misc])
    g_out, per_layer = {}, {}
    for l, names, recv in ((1, BIG, recv1), (0, LATE_BIG, recv_late), (0, mix_rest, recv_mix)):
        for n, r in zip(names, recv):
            per_layer[n, l] = _as_rows(_sum_slots(f"sum_{n}_{l}", r), n)
    misc_sum = _sum_slots("sum_misc", recv_mix[-1]).reshape(-1)
    off = 0
    for n, s in zip(misc_names, segs):
        g_out[n] = misc_sum[off:off + w[n].size].reshape(w[n].shape)
        off += s
    def update(n):
        shp = w[n].shape
        g_out[n] = jnp.stack([per_layer[n, l] for l in range(DEPTH)]) if n in BIG else g_out[n]
        d, mn, vn = _adamw(f"adamw_{n}", *[a.reshape(-1, shp[-1]) for a in (g_out[n], w[n], m[n], v[n])])
        return [a.reshape(shp) for a in (g_out[n], d, mn, vn)]

    done = {n: update(n) for n in WEIGHTS if n != 'w_in'}
    recv_in = _with_own(_push_wait("grads_push_wait_in", push_i, True, done[WEIGHTS[-1]][1]), own(sent_in), me)
    per_layer['w_in', 0] = _as_rows(_sum_slots("sum_w_in_0", recv_in[0]), 'w_in')
    done['w_in'] = update('w_in')
    return (loss, dx[None], *[done[n][k] for k in range(4) for n in WEIGHTS])
```

```python
import jax
import jax.numpy as jnp
from jax import lax
from jax.experimental import pallas as pl
from jax.experimental.pallas import tpu as pltpu

F32 = jnp.float32
MXU = jnp.bfloat16
HI = lax.Precision.HIGHEST

D = 1024
DEPTH = 2
POOLW = 384
ATT_W = 768
ATT_O = 256
GATE_W = 3 * D
IN_W = 6912
IN_TILE = 768
IN_ROT = (IN_W - GATE_W) // IN_TILE
MEM_W = 512
D_FF = 2816
EPS = 1e-6
ROPE_THETA = 500000.0
QB = 128
DILS = (1, 4, 16)
NEG = -1e30
MEM_SCALE = 128 ** -0.5
ATT_SCALE = 0.125

ADAM_LR, ADAM_B1, ADAM_B2, ADAM_EPS, ADAM_WD, ADAM_STEP = 0.001, 0.9, 0.999, 1e-08, 0.01, 10

N_DEV = 8
MESH_AXES = ("x", "y", "c")
LANES = 128
FLAT_ALIGN = 2048
ROW_TILE = 1024

WEIGHTS = ['norm_mix_pre', 'norm_mix_post', 'w_in', 'pool_w', 'pool_scale', 'conv_b_w', 'w_branch_a', 'w_branch_b',
           'w_branch_c', 'w_out', 'norm_mem_pre', 'norm_mem_post', 'norm_memkv', 'w_mq', 'w_mkv', 'w_mo',
           'norm_ffn_pre', 'norm_ffn_post', 'w_up', 'conv_ffn_w', 'w_down']
SHARD_AXIS = {'w_in': 2, 'conv_b_w': 2, 'w_branch_a': 2, 'w_branch_b': 2, 'w_branch_c': 2, 'w_out': 1, 'w_mq': 1,
              'w_mkv': 1, 'w_mo': 2, 'w_up': 2, 'conv_ffn_w': 2, 'w_down': 1}
F32_GATHERED = ('conv_b_w', 'conv_ffn_w')
BIG = [n for n in WEIGHTS if n in SHARD_AXIS and n not in F32_GATHERED]
LATE_BIG = ['w_mq', 'w_mkv', 'w_mo', 'w_up', 'w_down']


VMEM_LIMIT_MB = 60


def _params(sem, vmem_mb):
    del vmem_mb
    return pltpu.CompilerParams(dimension_semantics=sem, vmem_limit_bytes=VMEM_LIMIT_MB << 20)


def _dot(a, b, prec=None):
    return lax.dot_general(a, b, (((1,), (0,)), ((), ())), preferred_element_type=F32, precision=prec)


def _dot_nt(a, b, prec=None):
    return lax.dot_general(a, b, (((1,), (1,)), ((), ())), preferred_element_type=F32, precision=prec)


def _dot_tn(a, b, prec=None):
    return lax.dot_general(a, b, (((0,), (0,)), ((), ())), preferred_element_type=F32, precision=prec)


def _tile(n, cap):
    if n <= cap:
        return n
    best = None
    for t in range(LANES, cap + 1, LANES):
        if n % t == 0:
            best = t
    assert best is not None, (n, cap)
    return best


def _rms(x, g):
    r = lax.rsqrt(jnp.mean(x * x, axis=-1, keepdims=True) + EPS)
    return x * r * g, r


def _rms_bwd(w, y):
    r = lax.rsqrt(jnp.mean(y * y, axis=-1, keepdims=True) + EPS)
    return r * w - y * (r * r * r) * jnp.mean(w * y, axis=-1, keepdims=True), r


def _rows_call(name, body, n_rows, ts, ins, outs, scratch=(), reverse=False, vmem_mb=48, aliases=None):
    nt = n_rows // ts
    assert nt * ts == n_rows

    def tile_of(g):
        return (nt - 1 - g) if reverse else g

    in_specs, args = [], []
    for op in ins:
        if op[0] == "t":
            _, a, cw, cb = op
            in_specs.append(pl.BlockSpec((ts, cw), lambda g, cb=cb: (tile_of(g), cb)))
        elif op[0] == "h":
            _, a, hr, cw, cb = op
            in_specs.append(pl.BlockSpec((hr, cw), lambda g, cb=cb, k=ts // hr: (jnp.maximum(tile_of(g) * k - 1, 0), cb)))
        elif op[0] == "x":
            _, a = op
            in_specs.append(pl.BlockSpec(memory_space=pl.ANY))
        else:
            _, a = op
            in_specs.append(pl.BlockSpec(a.shape, lambda g, n=a.ndim: (0,) * n))
        args.append(a)
    out_specs, out_shape = [], []
    for op in outs:
        if op[0] == "t":
            _, cols, dt = op
            out_specs.append(pl.BlockSpec((ts, cols), lambda g: (tile_of(g), 0)))
            out_shape.append(jax.ShapeDtypeStruct((n_rows, cols), dt))
        elif op[0] == "c":
            _, total, cols, cb, dt = op
            out_specs.append(pl.BlockSpec((ts, cols), lambda g, cb=cb: (tile_of(g), cb)))
            out_shape.append(jax.ShapeDtypeStruct((n_rows, total), dt))
        else:
            _, shp, dt = op
            out_specs.append(pl.BlockSpec(shp, lambda g, n=len(shp): (0,) * n))
            out_shape.append(jax.ShapeDtypeStruct(shp, dt))

    def kern(*refs):
        g = pl.program_id(0)
        body(tile_of(g), g, *refs)

    return pl.pallas_call(kern, grid=(nt,), in_specs=in_specs, out_specs=out_specs, out_shape=out_shape,
                          scratch_shapes=list(scratch), input_output_aliases=aliases or {},
                          compiler_params=_params(("arbitrary",), vmem_mb), name=name)(*args)


def _acc(ref, g, val):
    @pl.when(g == 0)
    def _():
        ref[...] = val

    @pl.when(g != 0)
    def _():
        ref[...] += val


def _norm_mm(name, x, g, w, ts, tn, out_dtype=F32, wt=False, rot=0):
    S, K = x.shape
    N = w.shape[0] if wt else w.shape[1]
    assert wt or not rot

    def body(x_ref, g_ref, w_ref, o_ref, h_ref, hs):
        @pl.when(pl.program_id(1) == 0)
        def _():
            h, _ = _rms(x_ref[...], g_ref[...])
            hs[...] = h.astype(MXU)
            h_ref[...] = h.astype(MXU)

        o_ref[...] = (_dot_nt if wt else _dot)(hs[...], w_ref[...]).astype(out_dtype)

    w_spec = pl.BlockSpec((tn, K), lambda i, j: ((j + rot) % (N // tn), 0)) if wt else pl.BlockSpec((K, tn), lambda i, j: (0, j))
    return pl.pallas_call(
        body, grid=(S // ts, N // tn),
        in_specs=[pl.BlockSpec((ts, K), lambda i, j: (i, 0)), pl.BlockSpec((1, K), lambda i, j: (0, 0)), w_spec],
        out_specs=[pl.BlockSpec((ts, tn), lambda i, j: (i, j)), pl.BlockSpec((ts, K), lambda i, j: (i, 0))],
        out_shape=[jax.ShapeDtypeStruct((S, N), out_dtype), jax.ShapeDtypeStruct((S, K), MXU)],
        scratch_shapes=[pltpu.VMEM((ts, K), MXU)],
        compiler_params=_params(("arbitrary", "arbitrary"), 48), name=name)(x, g, w)


def _mm_nt(name, a, b, ts, tn, out_dtype=F32):
    M, K = a.shape
    N = b.shape[0]

    def body(a_ref, b_ref, o_ref):
        o_ref[...] = _dot_nt(a_ref[...], b_ref[...]).astype(out_dtype)

    return pl.pallas_call(
        body, grid=(M // ts, N // tn),
        in_specs=[pl.BlockSpec((ts, K), lambda i, j: (i, 0)), pl.BlockSpec((tn, K), lambda i, j: (j, 0))],
        out_specs=pl.BlockSpec((ts, tn), lambda i, j: (i, j)), out_shape=jax.ShapeDtypeStruct((M, N), out_dtype),
        compiler_params=_params(("arbitrary", "arbitrary"), 48), name=name)(a, b)


def _mm_tn(name, a, b, cap_k=512, cap_n=1024, out_dtype=MXU, rot=0):
    S, K = a.shape
    N = b.shape[1]
    tk, tn = _tile(K, cap_k), _tile(N, cap_n)

    def body(a_ref, b_ref, o_ref):
        o_ref[...] = _dot_tn(a_ref[...], b_ref[...]).astype(out_dtype)

    return pl.pallas_call(
        body, grid=(K // tk, N // tn),
        in_specs=[pl.BlockSpec((S, tk), lambda i, j: (0, i)), pl.BlockSpec((S, tn), lambda i, j: (0, j))],
        out_specs=pl.BlockSpec((tk, tn), lambda i, j: ((i + rot) % (K // tk), j)), out_shape=jax.ShapeDtypeStruct((K, N), out_dtype),
        compiler_params=_params(("arbitrary", "arbitrary"), 48), name=name)(a, b)


def _pool_cols(shape):
    col = lax.broadcasted_iota(jnp.int32, shape, 1)
    return col < 96, col < 192, col < 288


def _pool_select(s2, s4, s8, s16):
    c1, c2, c3 = _pool_cols(s2.shape)
    return jnp.where(c1, s2, jnp.where(c2, s4, jnp.where(c3, s8, s16)))


def _pool_cnt(t0, ts):
    c1, c2, c3 = _pool_cols((ts, POOLW))
    win = jnp.where(c1, 2, jnp.where(c2, 4, jnp.where(c3, 8, 16)))
    t = t0 + lax.broadcasted_iota(jnp.int32, (ts, POOLW), 0)
    return jnp.minimum(t + 1, win).astype(F32)


def _pooled(a, prev, t0):
    ts = a.shape[0]
    ext = jnp.concatenate([prev, a], axis=0)
    s2 = ext + pltpu.roll(ext, 1, axis=0)
    s4 = s2 + pltpu.roll(s2, 2, axis=0)
    s8 = s4 + pltpu.roll(s4, 4, axis=0)
    s16 = s8 + pltpu.roll(s8, 8, axis=0)
    sums = _pool_select(s2, s4, s8, s16)[16:]
    return sums / _pool_cnt(t0, ts) - a


def _conv3(z, prev8, w):
    ext = jnp.concatenate([prev8, z], axis=0)
    z1 = pltpu.roll(ext, 1, axis=0)[8:]
    z2 = pltpu.roll(ext, 2, axis=0)[8:]
    return w[0:1] * z2 + w[1:2] * z1 + w[2:3] * z, z1, z2


def _conv3_t(dc, next8, w):
    ts = dc.shape[0]
    ext = jnp.concatenate([dc, next8], axis=0)
    n = ts + 8
    u1 = pltpu.roll(ext, n - 1, axis=0)[:ts]
    u2 = pltpu.roll(ext, n - 2, axis=0)[:ts]
    return w[2:3] * dc + w[1:2] * u1 + w[0:1] * u2


def _poolconv_fwd(u, wblk, pool_scale, conv_b, ts=256):
    S = u.shape[0]

    def body(i, g, a_ref, bx_ref, bb_ref, bc_ref, wblk_ref, ps_ref, cw_ref, a2_ref, yb_ref, ca, cz):
        @pl.when(g == 0)
        def _():
            ca[...] = jnp.zeros_like(ca)
            cz[...] = jnp.zeros_like(cz)

        a = a_ref[...]
        p = _pooled(a, ca[...], i * ts)
        mixed = _dot(p.astype(MXU), wblk_ref[...])
        a2_ref[...] = (mixed * ps_ref[...]).astype(MXU)
        z = bc_ref[...] * bx_ref[...]
        conv, _, _ = _conv3(z, cz[...], cw_ref[...])
        yb_ref[...] = (bb_ref[...] * conv).astype(MXU)
        ca[...] = a[ts - 16:]
        cz[...] = z[ts - 8:]

    ins = [("t", u, POOLW, 8), ("t", u, POOLW, 9), ("t", u, POOLW, 10), ("t", u, POOLW, 11), ("w", wblk), ("w", pool_scale),
           ("w", conv_b)]
    return _rows_call("poolconv_fwd", body, S, ts, ins, [("t", POOLW, MXU), ("t", POOLW, MXU)],
                      scratch=[pltpu.VMEM((16, POOLW), F32), pltpu.VMEM((8, POOLW), F32)])


def _poolconv_bwd(u, d_a2, d_yb, du, wblk, pool_scale, conv_b, ts=256):
    S = u.shape[0]

    def body(i, g, a_ref, bx_ref, bb_ref, bc_ref, ap_ref, bxp_ref, bcp_ref, da2_ref, dyb_ref, wblk_ref, ps_ref, cw_ref, _,
             o_ref, dps_ref, dwb_ref, dcw_ref, ce, cdz):
        @pl.when(g == 0)
        def _():
            ce[...] = jnp.zeros_like(ce)
            cdz[...] = jnp.zeros_like(cdz)

        first = (i > 0).astype(F32)
        a = a_ref[...]
        p = _pooled(a, ap_ref[...] * first, i * ts)
        pb = p.astype(MXU)
        mixed = _dot(pb, wblk_ref[...])
        da2 = da2_ref[...]
        dmixed = (da2 * ps_ref[...]).astype(MXU)
        dp = _dot_nt(dmixed, wblk_ref[...])
        _acc(dps_ref, g, jnp.sum(da2 * mixed, axis=0, keepdims=True))
        _acc(dwb_ref, g, _dot_tn(pb, dmixed))
        e = dp / _pool_cnt(i * ts, ts)
        ext = jnp.concatenate([e, ce[...]], axis=0)
        n = ts + 16
        f2 = ext + pltpu.roll(ext, n - 1, axis=0)
        f4 = f2 + pltpu.roll(f2, n - 2, axis=0)
        f8 = f4 + pltpu.roll(f4, n - 4, axis=0)
        f16 = f8 + pltpu.roll(f8, n - 8, axis=0)
        o_ref[:, 0:POOLW] = (_pool_select(f2, f4, f8, f16)[:ts] - dp).astype(o_ref.dtype)
        ce[...] = e[:16]

        bx, bb, bc = bx_ref[...], bb_ref[...], bc_ref[...]
        z = bc * bx
        w = cw_ref[...]
        conv, z1, z2 = _conv3(z, bxp_ref[...] * bcp_ref[...] * first, w)
        dyb = dyb_ref[...]
        dconv = dyb * bb
        dz = _conv3_t(dconv, cdz[...], w)
        o_ref[:, POOLW:2 * POOLW] = (dz * bc).astype(o_ref.dtype)
        o_ref[:, 2 * POOLW:3 * POOLW] = (dyb * conv).astype(o_ref.dtype)
        o_ref[:, 3 * POOLW:4 * POOLW] = (dz * bx).astype(o_ref.dtype)
        dw = jnp.concatenate([jnp.sum(dconv * z2, axis=0, keepdims=True), jnp.sum(dconv * z1, axis=0, keepdims=True),
                              jnp.sum(dconv * z, axis=0, keepdims=True)], axis=0)
        _acc(dcw_ref, g, dw)
        cdz[...] = dconv[:8]

    ins = [("t", u, POOLW, 8), ("t", u, POOLW, 9), ("t", u, POOLW, 10), ("t", u, POOLW, 11),
           ("h", u, 16, POOLW, 8), ("h", u, 8, POOLW, 9), ("h", u, 8, POOLW, 11),
           ("t", d_a2, POOLW, 0), ("t", d_yb, POOLW, 0), ("w", wblk), ("w", pool_scale), ("w", conv_b), ("x", du)]
    outs = [("c", IN_W, 4 * POOLW, GATE_W // (4 * POOLW), MXU), ("a", (1, POOLW), F32), ("a", (POOLW, POOLW), F32), ("a", (3, POOLW), F32)]
    return _rows_call("poolconv_bwd", body, S, ts, ins, outs, aliases={len(ins) - 1: 0},
                      scratch=[pltpu.VMEM((16, POOLW), F32), pltpu.VMEM((8, POOLW), F32)], reverse=True)


def _rope_tables(positions):
    S = positions.shape[0]
    inv = ROPE_THETA ** (-jnp.arange(0, 16, 2, dtype=F32) / 16)
    ang = positions.astype(F32)[:, None] * inv
    cos, sin = jnp.cos(ang), jnp.sin(ang)
    c64 = jnp.concatenate([cos, cos, jnp.ones((S, 48), F32)], axis=1)
    s64 = jnp.concatenate([-sin, sin, jnp.zeros((S, 48), F32)], axis=1)
    return jnp.concatenate([c64, c64], axis=1), jnp.concatenate([s64, s64], axis=1)


def _partner(x):
    lane = lax.broadcasted_iota(jnp.int32, x.shape, 1) % 64
    return jnp.where(lane < 8, pltpu.roll(x, LANES - 8, axis=1), jnp.where(lane < 16, pltpu.roll(x, 8, axis=1), 0.0))


def _rope(x, c, s):
    return x * c + _partner(x) * s


def _rope_t(x, c, s):
    return x * c + _partner(x * s)


def _rows_of(r, n, d):
    return pl.ds(r, n, stride=d) if d > 1 else pl.ds(0, n)


def _head_masks(shape):
    lane = lax.broadcasted_iota(jnp.int32, shape, 1) // 64
    return [lane == h for h in range(4)]


def _only(mask, x):
    return jnp.where(mask, x, jnp.zeros_like(x))


def _rope_perm(u, ctab, stab, ts=256):
    S = u.shape[0]
    nch = ATT_W // LANES

    def body(*refs):
        chunks, (c_ref, s_ref), outs = refs[:3 * nch], refs[3 * nch:3 * nch + 2], refs[3 * nch + 2:]
        for g, d in enumerate(DILS):
            n = ts // d
            for r in range(d):
                rows = _rows_of(r, n, d)
                c, s = c_ref[rows, :], s_ref[rows, :]
                for which in range(3):
                    parts = [chunks[which * nch + j][rows, :] for j in (2 * g, 2 * g + 1)]
                    if which < 2:
                        parts = [_rope(x, c, s) for x in parts]
                    outs[which * 3 + g][r] = jnp.concatenate(parts, axis=1).astype(MXU)

    base = (IN_W - 3 * ATT_W) // LANES
    in_specs = [pl.BlockSpec((ts, LANES), lambda i, cb=base + k: (i, cb)) for k in range(3 * nch)]
    in_specs += [pl.BlockSpec((ts, LANES), lambda i: (i, 0))] * 2
    out_specs = [pl.BlockSpec((d, ts // d, ATT_O), lambda i: (0, i, 0)) for _ in range(3) for d in DILS]
    out_shape = [jax.ShapeDtypeStruct((d, S // d, ATT_O), MXU) for _ in range(3) for d in DILS]
    res = pl.pallas_call(body, grid=(S // ts,), in_specs=in_specs, out_specs=out_specs, out_shape=out_shape,
                         compiler_params=_params(("arbitrary",), 32), name="rope_perm")(*([u] * (3 * nch)), ctab, stab)
    return [[res[which * 3 + g].reshape(S, ATT_O) for g in range(3)] for which in range(3)]


def _rope_unperm_bwd(dqkv, du, ctab, stab, ts=256):
    S = dqkv[0][0].shape[0]
    nch = ATT_W // LANES

    def body(*refs):
        ins, (c_ref, s_ref, _, o_ref, scr) = refs[:9], refs[9:]
        for g, d in enumerate(DILS):
            n = ts // d
            for r in range(d):
                rows = _rows_of(r, n, d)
                c, s = c_ref[rows, :], s_ref[rows, :]
                for which in range(3):
                    v = ins[which * 3 + g][r]
                    for half in range(2):
                        x = v[:, half * LANES:(half + 1) * LANES]
                        scr.at[which * nch + 2 * g + half][rows, :] = _rope_t(x, c, s) if which < 2 else x
        for j in range(3 * nch):
            o_ref[:, j * LANES:(j + 1) * LANES] = scr[j].astype(o_ref.dtype)

    in_specs = [pl.BlockSpec((d, ts // d, ATT_O), lambda i: (0, i, 0)) for _ in range(3) for d in DILS]
    in_specs += [pl.BlockSpec((ts, LANES), lambda i: (i, 0))] * 2 + [pl.BlockSpec(memory_space=pl.ANY)]
    args = [dqkv[which][g].reshape(d, S // d, ATT_O) for which in range(3) for g, d in enumerate(DILS)]
    last = (IN_W - 3 * ATT_W) // (3 * ATT_W)
    return pl.pallas_call(body, grid=(S // ts,), in_specs=in_specs, out_specs=pl.BlockSpec((ts, 3 * ATT_W), lambda i: (i, last)),
                          out_shape=jax.ShapeDtypeStruct((S, IN_W), MXU), scratch_shapes=[pltpu.VMEM((3 * nch, ts, LANES), F32)],
                          input_output_aliases={len(in_specs) - 1: 0},
                          compiler_params=_params(("arbitrary",), 32), name="rope_unperm_bwd")(*args, ctab, stab, du)


def _band_mask_keys(has_prev):
    r = lax.broadcasted_iota(jnp.int32, (QB, 2 * QB), 0)
    c = lax.broadcasted_iota(jnp.int32, (QB, 2 * QB), 1)
    return ((c < QB) & (c >= r) & has_prev) | ((c >= QB) & (c - QB <= r))


def _band_mask_queries(has_next):
    r = lax.broadcasted_iota(jnp.int32, (2 * QB, QB), 0)
    c = lax.broadcasted_iota(jnp.int32, (2 * QB, QB), 1)
    return ((r < QB) & (c <= r)) | ((r >= QB) & (c >= r - QB) & has_next)


ASUB = 4
_BIG = pl.BlockSpec((ASUB * QB, ATT_O), lambda b: (b, 0))
_PREV = pl.BlockSpec((QB, ATT_O), lambda b: (jnp.maximum(b * ASUB - 1, 0), 0))


def _sub(ref, j):
    return ref[j * QB:(j + 1) * QB]


def _attn_fwd(g, q, k, v):
    S = q.shape[0]
    nb = S // QB
    nblk = nb // DILS[g]

    def body(q_ref, kc_ref, kp_ref, vc_ref, vp_ref, o_ref, m_ref, l_ref):
        hm_kv, hm_o = _head_masks((2 * QB, ATT_O)), _head_masks((QB, ATT_O))
        for j in range(ASUB):
            ok = _band_mask_keys(((pl.program_id(0) * ASUB + j) & (nblk - 1)) > 0)
            k2 = jnp.concatenate([kp_ref[...] if j == 0 else _sub(kc_ref, j - 1), _sub(kc_ref, j)], axis=0)
            v2 = jnp.concatenate([vp_ref[...] if j == 0 else _sub(vc_ref, j - 1), _sub(vc_ref, j)], axis=0)
            qv = _sub(q_ref, j)
            o_acc = jnp.zeros((QB, ATT_O), F32)
            m_acc = jnp.zeros((QB, ATT_O), F32)
            l_acc = jnp.zeros((QB, ATT_O), F32)
            for h in range(4):
                s = jnp.where(ok, _dot_nt(qv, _only(hm_kv[h], k2)) * ATT_SCALE, NEG)
                m = jnp.max(s, axis=1, keepdims=True)
                p = jnp.exp(s - m)
                o_acc = o_acc + _dot(p.astype(MXU), _only(hm_kv[h], v2))
                m_acc = jnp.where(hm_o[h], m, m_acc)
                l_acc = jnp.where(hm_o[h], jnp.sum(p, axis=1, keepdims=True), l_acc)
            o_ref[j * QB:(j + 1) * QB] = o_acc
            m_ref[j * QB:(j + 1) * QB] = m_acc
            l_ref[j * QB:(j + 1) * QB] = l_acc

    shp = jax.ShapeDtypeStruct((S, ATT_O), F32)
    return pl.pallas_call(body, grid=(nb // ASUB,), in_specs=[_BIG, _BIG, _PREV, _BIG, _PREV],
                          out_specs=[_BIG] * 3, out_shape=[shp, shp, shp], compiler_params=_params(("arbitrary",), 32),
                          name=f"attn_fwd_{g}")(q, k, k, v, v)


def _natural(ref, d, scr, ts):
    if d == 1:
        return ref[0]
    n = ts // d
    for r in range(d):
        v = ref[r]
        scr.at[0][pl.ds(r, n, stride=d), :] = v[:, 0:LANES]
        scr.at[1][pl.ds(r, n, stride=d), :] = v[:, LANES:2 * LANES]
    return jnp.concatenate([scr[0], scr[1]], axis=1)


def _attn_combine(oml, ts=256):
    S = oml[0][0].shape[0]

    def body(*refs):
        ins, (att_ref, out_ref, lse_ref, scr) = refs[:9], refs[9:]
        o, m, l = [[_natural(ins[3 * g + k], d, scr, ts) for g, d in enumerate(DILS)] for k in range(3)]
        mx = jnp.maximum(jnp.maximum(m[0], m[1]), m[2])
        w = [jnp.exp(m[g] - mx) for g in range(3)]
        den = w[0] * l[0] + w[1] * l[1] + w[2] * l[2]
        out = (w[0] * o[0] + w[1] * o[1] + w[2] * o[2]) / den
        out_ref[...] = out
        att_ref[...] = out.astype(MXU)
        lse_ref[...] = mx + jnp.log(den)

    in_specs = [pl.BlockSpec((d, ts // d, ATT_O), lambda i: (0, i, 0)) for d in DILS for _ in range(3)]
    args = [a.reshape(d, S // d, ATT_O) for d, grp in zip(DILS, oml) for a in grp]
    blk = pl.BlockSpec((ts, ATT_O), lambda i: (i, 0))
    return pl.pallas_call(body, grid=(S // ts,), in_specs=in_specs, out_specs=[blk, blk, blk],
                          out_shape=[jax.ShapeDtypeStruct((S, ATT_O), MXU), jax.ShapeDtypeStruct((S, ATT_O), F32),
                                     jax.ShapeDtypeStruct((S, ATT_O), F32)],
                          scratch_shapes=[pltpu.VMEM((2, ts, LANES), F32)], compiler_params=_params(("arbitrary",), 32),
                          name="attn_combine")(*args)


def _attn_bwd_prep(datt, o, lse, ts=256):
    S = datt.shape[0]

    def body(da0, da1, o_ref, l0, l1, *rest):
        outs, dl = rest[:9], rest[9]
        prod = jnp.concatenate([da0[...], da1[...]], axis=1) * o_ref[...]
        delta = jnp.zeros((ts, ATT_O), F32)
        for hm in _head_masks((ts, ATT_O)):
            delta = jnp.where(hm, jnp.sum(_only(hm, prod), axis=1, keepdims=True), delta)
        dl[0] = delta[:, 0:LANES]
        dl[1] = delta[:, LANES:2 * LANES]
        for g, d in enumerate(DILS):
            n = ts // d
            for r in range(d):
                rows = _rows_of(r, n, d)
                outs[g][r] = jnp.concatenate([da0[rows, :], da1[rows, :]], axis=1).astype(MXU)
                outs[3 + g][r] = jnp.concatenate([dl.at[0][rows, :], dl.at[1][rows, :]], axis=1)
                outs[6 + g][r] = jnp.concatenate([l0[rows, :], l1[rows, :]], axis=1)

    half = lambda j: pl.BlockSpec((ts, LANES), lambda i: (i, j))
    out_specs = [pl.BlockSpec((d, ts // d, ATT_O), lambda i: (0, i, 0)) for _ in range(3) for d in DILS]
    out_shape = [jax.ShapeDtypeStruct((d, S // d, ATT_O), dt) for dt in (MXU, F32, F32) for d in DILS]
    res = pl.pallas_call(body, grid=(S // ts,), in_specs=[half(0), half(1), pl.BlockSpec((ts, ATT_O), lambda i: (i, 0)), half(0), half(1)],
                         out_specs=out_specs, out_shape=out_shape, scratch_shapes=[pltpu.VMEM((2, ts, LANES), F32)],
                         compiler_params=_params(("arbitrary",), 32), name="attn_bwd_prep")(datt, datt, o, lse, lse)
    return [[res[k * 3 + g].reshape(S, ATT_O) for g in range(3)] for k in range(3)]


def _head_col(x, h):
    return x[:, h * 64:h * 64 + 1]


def _attn_bwd(g, q, k, v, do, delta, lse):
    S = q.shape[0]
    nb = S // QB
    nblk = nb // DILS[g]

    def body(k_ref, v_ref, qc_ref, qn_ref, doc_ref, don_ref, dlc_ref, dln_ref, lc_ref, ln_ref, dq_ref, dk_ref, dv_ref, dq_scr):
        hms, hmk = _head_masks((2 * QB, ATT_O)), _head_masks((QB, ATT_O))
        first = pl.program_id(0) == 0

        @pl.when(first)
        def _():
            dq_scr[0:QB] = jnp.zeros((QB, ATT_O), F32)

        @pl.when(jnp.logical_not(first))
        def _():
            dq_scr[0:QB] = dq_scr[ASUB * QB:(ASUB + 1) * QB]

        dq_scr[QB:(ASUB + 1) * QB] = jnp.zeros((ASUB * QB, ATT_O), F32)

        def both(cur_ref, nxt_ref, j):
            return jnp.concatenate([_sub(cur_ref, j), nxt_ref[...] if j == ASUB - 1 else _sub(cur_ref, j + 1)], axis=0)

        for j in range(ASUB):
            ok = _band_mask_queries(((pl.program_id(0) * ASUB + j + 1) & (nblk - 1)) > 0)
            q2, do2, dl2, lse2 = both(qc_ref, qn_ref, j), both(doc_ref, don_ref, j), both(dlc_ref, dln_ref, j), both(lc_ref, ln_ref, j)
            kv, vv = _sub(k_ref, j), _sub(v_ref, j)
            dk = jnp.zeros((QB, ATT_O), F32)
            dv = jnp.zeros((QB, ATT_O), F32)
            dq2 = jnp.zeros((2 * QB, ATT_O), F32)
            for h, hm in enumerate(hms):
                qh, doh = _only(hm, q2), _only(hm, do2)
                p = jnp.where(ok, jnp.exp(_dot_nt(qh, kv) * ATT_SCALE - _head_col(lse2, h)), 0.0)
                ds = (p * (_dot_nt(doh, vv) - _head_col(dl2, h))).astype(MXU)
                dv = dv + _dot_tn(p.astype(MXU), doh)
                dk = dk + _dot_tn(ds, qh)
                dq2 = dq2 + _dot(ds, _only(hmk[h], kv))
            dk_ref[j * QB:(j + 1) * QB] = dk * ATT_SCALE
            dv_ref[j * QB:(j + 1) * QB] = dv
            dq_scr[j * QB:(j + 2) * QB] += dq2
        dq_ref[...] = dq_scr[0:ASUB * QB] * ATT_SCALE

    nxt = pl.BlockSpec((QB, ATT_O), lambda b: (jnp.minimum((b + 1) * ASUB, nb - 1), 0))
    shp = jax.ShapeDtypeStruct((S, ATT_O), F32)
    return pl.pallas_call(body, grid=(nb // ASUB,), in_specs=[_BIG, _BIG, _BIG, nxt, _BIG, nxt, _BIG, nxt, _BIG, nxt], out_specs=[_BIG] * 3,
                          out_shape=[shp, shp, shp], scratch_shapes=[pltpu.VMEM(((ASUB + 1) * QB, ATT_O), F32)],
                          compiler_params=_params(("arbitrary",), 32), name=f"attn_bwd_{g}")(k, v, q, q, do, do, delta, delta, lse, lse)


def _merge_fwd(x0, u, a2, yb, att, wa, wb, wc, w_out, g_post, ts=256):
    S = x0.shape[0]

    def body(i, g, x_ref, gate_ref, a2_ref, yb_ref, att_ref, wa_ref, wb_ref, wc_ref, wo_ref, gp_ref, mg_ref, y_ref, xo_ref):
        merged = jax.nn.sigmoid(gate_ref[:, 0:D]) * _dot_nt(a2_ref[...], wa_ref[...])
        merged = merged + jax.nn.sigmoid(gate_ref[:, D:2 * D]) * _dot_nt(yb_ref[...], wb_ref[...])
        merged = merged + jax.nn.sigmoid(gate_ref[:, 2 * D:3 * D]) * _dot_nt(att_ref[...], wc_ref[...])
        mb = merged.astype(MXU)
        mg_ref[...] = mb
        y = _dot(mb, wo_ref[...])
        y_ref[...] = y
        xo_ref[...] = x_ref[...] + _rms(y, gp_ref[...])[0]

    ins = [("t", x0, D, 0), ("t", u, GATE_W, 0), ("t", a2, POOLW, 0), ("t", yb, POOLW, 0), ("t", att, ATT_O, 0),
           ("w", wa), ("w", wb), ("w", wc), ("w", w_out), ("w", g_post)]
    return _rows_call("merge_fwd", body, S, ts, ins, [("t", D, MXU), ("t", D, F32), ("t", D, F32)])


def _merge_bwd(dx, y1, u, a2, yb, att, wa, wb, wc, w_out, g_post, ts=256):
    S = dx.shape[0]

    def body(i, g, dx_ref, y_ref, gate_ref, a2_ref, yb_ref, att_ref, wa_ref, wb_ref, wc_ref, wo_ref, gp_ref,
             dy_ref, dgate_ref, dbra_ref, dbrb_ref, dbrc_ref, da2_ref, dyb_ref, datt_ref, dgp_ref):
        dxv, y = dx_ref[...], y_ref[...]
        dy, r = _rms_bwd(dxv * gp_ref[...], y)
        _acc(dgp_ref, g, jnp.sum(dxv * (y * r), axis=0, keepdims=True))
        dyb16 = dy.astype(MXU)
        dy_ref[...] = dyb16
        dm = _dot_nt(dyb16, wo_ref[...])
        for n, (src, w_ref, dbr_ref, din_ref) in enumerate(((a2_ref, wa_ref, dbra_ref, da2_ref), (yb_ref, wb_ref, dbrb_ref, dyb_ref),
                                                           (att_ref, wc_ref, dbrc_ref, datt_ref))):
            gt = jax.nn.sigmoid(gate_ref[:, n * D:(n + 1) * D])
            br = _dot_nt(src[...], w_ref[...])
            dgate_ref[:, n * D:(n + 1) * D] = (dm * br * gt * (1.0 - gt)).astype(dgate_ref.dtype)
            dbr = (dm * gt).astype(MXU)
            dbr_ref[...] = dbr
            din_ref[...] = _dot(dbr, w_ref[...])

    ins = [("t", dx, D, 0), ("t", y1, D, 0), ("t", u, GATE_W, 0), ("t", a2, POOLW, 0), ("t", yb, POOLW, 0), ("t", att, ATT_O, 0),
           ("w", wa), ("w", wb), ("w", wc), ("w", w_out), ("w", g_post)]
    outs = [("t", D, MXU), ("c", IN_W, GATE_W, 0, MXU), ("t", D, MXU), ("t", D, MXU), ("t", D, MXU), ("t", POOLW, F32), ("t", POOLW, F32),
            ("t", ATT_O, F32), ("a", (1, D), F32)]
    return _rows_call("merge_bwd", body, S, ts, ins, outs)


def _prenorm_bwd(name, dx_res, du, wt, x, g_pre, ts=256, lead=0):
    S = x.shape[0]
    N = du.shape[1]

    def body(i, g, dx_ref, du_ref, wt_ref, x_ref, g_ref, o_ref, dg_ref):
        if lead:
            dhv = _dot(du_ref[:, 0:lead], wt_ref[N - lead:N, :]) + _dot(du_ref[:, lead:N], wt_ref[0:N - lead, :])
        else:
            dhv = _dot(du_ref[...], wt_ref[...])
        xv = x_ref[...]
        dxn, r = _rms_bwd(dhv * g_ref[...], xv)
        o_ref[...] = dx_ref[...] + dxn
        _acc(dg_ref, g, jnp.sum(dhv * (xv * r), axis=0, keepdims=True))

    ins = [("t", dx_res, D, 0), ("t", du, N, 0), ("w", wt), ("t", x, D, 0), ("w", g_pre)]
    return _rows_call(name, body, S, ts, ins, [("t", D, F32), ("a", (1, D), F32)], vmem_mb=52)


def _mem_heads(qm, kv_ref):
    out = []
    for h in range(4):
        q = qm[:, h * 128:(h + 1) * 128].astype(MXU)
        k = kv_ref[:, h * 128:(h + 1) * 128]
        v = kv_ref[:, MEM_W + h * 128:MEM_W + (h + 1) * 128]
        sc = _dot_nt(q, k) * MEM_SCALE
        e = jnp.exp(sc - jnp.max(sc, axis=1, keepdims=True))
        out.append((e / jnp.sum(e, axis=1, keepdims=True), q, k, v))
    return out


def _mem_fwd(x1, kv, g_pre, w_mq, w_mo, g_post, ts=256):
    S = x1.shape[0]

    def body(i, g, x_ref, kv_ref, gq_ref, wq_ref, wo_ref, gp_ref, om_ref, h_ref, y_ref, xo_ref):
        x = x_ref[...]
        hb = _rms(x, gq_ref[...])[0].astype(MXU)
        h_ref[...] = hb
        qm = _dot(hb, wq_ref[...])
        om = jnp.concatenate([_dot(p.astype(MXU), v) for p, _, _, v in _mem_heads(qm, kv_ref)], axis=1).astype(MXU)
        om_ref[...] = om
        y = _dot_nt(om, wo_ref[...])
        y_ref[...] = y
        xo_ref[...] = x + _rms(y, gp_ref[...])[0]

    ins = [("t", x1, D, 0), ("w", kv), ("w", g_pre), ("w", w_mq), ("w", w_mo), ("w", g_post)]
    return _rows_call("mem_fwd", body, S, ts, ins, [("t", MEM_W, MXU), ("t", D, MXU), ("t", D, F32), ("t", D, F32)])


def _mem_bwd(dx2, ym, x1, kv, g_pre, w_mq, w_mo, g_post, ts=256):
    S = x1.shape[0]

    def body(i, g, dx_ref, y_ref, x_ref, kv_ref, gq_ref, wq_ref, wo_ref, gp_ref, dy_ref, dq_ref, dxo_ref, dgp_ref, dgq_ref, dkv_ref):
        dxv, y, x = dx_ref[...], y_ref[...], x_ref[...]
        dy, r = _rms_bwd(dxv * gp_ref[...], y)
        _acc(dgp_ref, g, jnp.sum(dxv * (y * r), axis=0, keepdims=True))
        dyb = dy.astype(MXU)
        dy_ref[...] = dyb
        dom = _dot(dyb, wo_ref[...])
        h, r1 = _rms(x, gq_ref[...])
        qm = _dot(h.astype(MXU), wq_ref[...])
        dqs = []

        @pl.when(g == 0)
        def _():
            dkv_ref[...] = jnp.zeros_like(dkv_ref)

        for hh, (p, q, k, v) in enumerate(_mem_heads(qm, kv_ref)):
            doh = dom[:, hh * 128:(hh + 1) * 128].astype(MXU)
            dp = _dot_nt(doh, v)
            dsc = (p * (dp - jnp.sum(dp * p, axis=1, keepdims=True)) * MEM_SCALE).astype(MXU)
            dqs.append(_dot(dsc, k))
            dkv_ref[:, hh * 128:(hh + 1) * 128] += _dot_tn(dsc, q)
            dkv_ref[:, MEM_W + hh * 128:MEM_W + (hh + 1) * 128] += _dot_tn(p.astype(MXU), doh)
        dq = jnp.concatenate(dqs, axis=1).astype(MXU)
        dq_ref[...] = dq
        dh = _dot_nt(dq, wq_ref[...])
        _acc(dgq_ref, g, jnp.sum(dh * (x * r1), axis=0, keepdims=True))
        dxo_ref[...] = dxv + _rms_bwd(dh * gq_ref[...], x)[0]

    ins = [("t", dx2, D, 0), ("t", ym, D, 0), ("t", x1, D, 0), ("w", kv), ("w", g_pre), ("w", w_mq), ("w", w_mo), ("w", g_post)]
    outs = [("t", D, MXU), ("t", MEM_W, MXU), ("t", D, F32), ("a", (1, D), F32), ("a", (1, D), F32), ("a", (256, D), F32)]
    return _rows_call("mem_bwd", body, S, ts, ins, outs)


def _gain_grad(name, dn, x):
    n = x.shape[0]

    def body(i, g, dn_ref, x_ref, o_ref):
        xv = x_ref[...]
        r = lax.rsqrt(jnp.mean(xv * xv, axis=-1, keepdims=True) + EPS)
        o_ref[...] = jnp.sum(dn_ref[...] * (xv * r), axis=0, keepdims=True)

    return _rows_call(name, body, n, n, [("t", dn, D, 0), ("t", x, D, 0)], [("a", (1, D), F32)])[0]


def _ffn_fwd(x2, u3, conv_f, w_down, g_post, ts=256):
    S = x2.shape[0]

    def body(i, g, x_ref, ua_ref, ub_ref, cw_ref, wd_ref, gp_ref, act_ref, y_ref, xo_ref, cu):
        @pl.when(g == 0)
        def _():
            cu[...] = jnp.zeros_like(cu)

        ua = ua_ref[...]
        c, _, _ = _conv3(ua, cu[...], cw_ref[...])
        act = (c * jax.nn.sigmoid(c) * ub_ref[...]).astype(MXU)
        act_ref[...] = act
        y = _dot(act, wd_ref[...])
        y_ref[...] = y
        xo_ref[...] = x_ref[...] + _rms(y, gp_ref[...])[0]
        cu[...] = ua[ts - 8:]

    ins = [("t", x2, D, 0), ("t", u3, D_FF, 0), ("t", u3, D_FF, 1), ("w", conv_f), ("w", w_down), ("w", g_post)]
    return _rows_call("ffn_fwd", body, S, ts, ins, [("t", D_FF, MXU), ("t", D, F32), ("t", D, F32)],
                      scratch=[pltpu.VMEM((8, D_FF), F32)], vmem_mb=56)


def _ffn_bwd(dx3, y3, u3, conv_f, w_down, g_post, ts=128):
    S = dx3.shape[0]

    def body(i, g, dx_ref, y_ref, ua_ref, ub_ref, uap_ref, cw_ref, wd_ref, gp_ref, dy_ref, du_ref, dgp_ref, dcw_ref, cdc):
        @pl.when(g == 0)
        def _():
            cdc[...] = jnp.zeros_like(cdc)

        dxv, y = dx_ref[...], y_ref[...]
        dy, r = _rms_bwd(dxv * gp_ref[...], y)
        _acc(dgp_ref, g, jnp.sum(dxv * (y * r), axis=0, keepdims=True))
        dyb = dy.astype(MXU)
        dy_ref[...] = dyb
        dact = _dot_nt(dyb, wd_ref[...])
        ua, w = ua_ref[...], cw_ref[...]
        c, u1, u2 = _conv3(ua, uap_ref[...] * (i > 0).astype(F32), w)
        sg = jax.nn.sigmoid(c)
        du_ref[:, D_FF:2 * D_FF] = (dact * (c * sg)).astype(du_ref.dtype)
        dc = dact * ub_ref[...] * (sg * (1.0 + c * (1.0 - sg)))
        du_ref[:, 0:D_FF] = _conv3_t(dc, cdc[...], w).astype(du_ref.dtype)
        dw = jnp.concatenate([jnp.sum(dc * u2, axis=0, keepdims=True), jnp.sum(dc * u1, axis=0, keepdims=True),
                              jnp.sum(dc * ua, axis=0, keepdims=True)], axis=0)
        _acc(dcw_ref, g, dw)
        cdc[...] = dc[:8]

    ins = [("t", dx3, D, 0), ("t", y3, D, 0), ("t", u3, D_FF, 0), ("t", u3, D_FF, 1), ("h", u3, 8, D_FF, 0), ("w", conv_f),
           ("w", w_down), ("w", g_post)]
    outs = [("t", D, MXU), ("t", 2 * D_FF, MXU), ("a", (1, D), F32), ("a", (3, D_FF), F32)]
    return _rows_call("ffn_bwd", body, S, ts, ins, outs, scratch=[pltpu.VMEM((8, D_FF), F32)], reverse=True, vmem_mb=56)


def _loss_head(x, target, ts=512):
    S = x.shape[0]

    def body(i, g, x_ref, t_ref, dx_ref, acc_ref):
        diff = x_ref[...] - t_ref[...]
        dx_ref[...] = diff * (1.0 / D)
        col = jnp.sum(diff * diff, axis=0, keepdims=True)
        part = col[:, 0:LANES]
        for j in range(1, D // LANES):
            part = part + col[:, j * LANES:(j + 1) * LANES]
        row = lax.broadcasted_iota(jnp.int32, (8, LANES), 0)
        _acc(acc_ref, g, jnp.where(row == 0, jnp.broadcast_to(part, (8, LANES)), 0.0))

    return _rows_call("loss_head", body, S, ts, [("t", x, D, 0), ("t", target, D, 0)], [("t", D, F32), ("a", (8, LANES), F32)])


_OPERAND_NAME = dict(w_in='w_in', w_branch_a='wa', w_branch_b='wb', w_branch_c='wc', w_out='w_out', w_mq='w_mq', w_mkv='w_mkv',
                     w_mo='w_mo', w_up='w_up', w_down='w_down')


def _big_operands(big):
    return {_OPERAND_NAME[n]: a for n, a in big.items()}


def _layer_weights(big, small, l):
    pool_w = small['pool_w'][l].astype(MXU)
    wblk = jnp.zeros((POOLW, POOLW), MXU)
    for g in range(4):
        wblk = lax.dynamic_update_slice(wblk, pool_w[g], (g * 96, g * 96))
    vec = lambda n: small[n][l].reshape(1, -1)
    return dict(
        _big_operands(big),
        wblk=wblk, pool_scale=vec('pool_scale'), conv_b=small['conv_b_w'][l], conv_f=small['conv_ffn_w'][l],
        g_mix_pre=vec('norm_mix_pre'), g_mix_post=vec('norm_mix_post'), g_mem_pre=vec('norm_mem_pre'),
        g_mem_post=vec('norm_mem_post'), g_memkv=vec('norm_memkv'), g_ffn_pre=vec('norm_ffn_pre'), g_ffn_post=vec('norm_ffn_post'))


def _layer_fwd(x0, mem, W, ctab, stab):
    sv = _layer_fwd_mix(x0, W, ctab, stab)
    return _layer_fwd_late(mem, W, sv), sv


def _layer_fwd_mix(x0, W, ctab, stab):
    sv = dict(x0=x0)
    sv['u'], sv['h1'] = _norm_mm("in_proj", x0, W['g_mix_pre'], W['w_in'], ts=1024, tn=IN_TILE, wt=True, rot=IN_ROT)
    sv['a2'], sv['yb'] = _poolconv_fwd(sv['u'], W['wblk'], W['pool_scale'], W['conv_b'])
    sv['qkv'] = q3, k3, v3 = _rope_perm(sv['u'], ctab, stab)
    sv['att'], sv['o'], sv['lse'] = _attn_combine([_attn_fwd(g, q3[g], k3[g], v3[g]) for g in range(3)])
    sv['merged'], sv['y1'], sv['x1'] = _merge_fwd(x0, sv['u'], sv['a2'], sv['yb'], sv['att'], W['wa'], W['wb'], W['wc'],
                                                  W['w_out'], W['g_mix_post'])
    return sv


def _layer_fwd_late(mem, W, sv):
    sv['kv'], sv['memn'] = _norm_mm("mem_kv", mem, W['g_memkv'], W['w_mkv'], ts=256, tn=D, out_dtype=MXU)
    sv['om'], sv['h2'], sv['ym'], sv['x2'] = _mem_fwd(sv['x1'], sv['kv'], W['g_mem_pre'], W['w_mq'], W['w_mo'], W['g_mem_post'])
    sv['u3'], sv['h3'] = _norm_mm("up_proj", sv['x2'], W['g_ffn_pre'], W['w_up'], ts=1024, tn=1408, wt=True)
    sv['act'], sv['y3'], x3 = _ffn_fwd(sv['x2'], sv['u3'], W['conv_f'], W['w_down'], W['g_ffn_post'])
    return x3


def _layer_bwd(dx3, mem, W, sv, ctab, stab):
    dx1, g = _layer_bwd_late(dx3, mem, W, sv)
    dx0, g_mix = _layer_bwd_mix(dx1, W, sv, ctab, stab)
    return dx0, {**g, **g_mix}


def _layer_bwd_late(dx3, mem, W, sv):
    g = {}
    dy3, du3, g['norm_ffn_post'], g['conv_ffn_w'] = _ffn_bwd(dx3, sv['y3'], sv['u3'], W['conv_f'], W['w_down'], W['g_ffn_post'])
    g['w_down'] = _mm_tn("dw_down", sv['act'], dy3, cap_k=256)
    g['w_up'] = _mm_tn("dw_up", du3, sv['h3'])
    dx2, g['norm_ffn_pre'] = _prenorm_bwd("ffn_pre_bwd", dx3, du3, W['w_up'], sv['x2'], W['g_ffn_pre'])
    dym, dqm, dx1, g['norm_mem_post'], g['norm_mem_pre'], dkv = _mem_bwd(dx2, sv['ym'], sv['x1'], sv['kv'], W['g_mem_pre'],
                                                                       W['w_mq'], W['w_mo'], W['g_mem_post'])
    g['w_mo'] = _mm_tn("dw_mo", dym, sv['om'])
    g['w_mq'] = _mm_tn("dw_mq", sv['h2'], dqm)
    dkvb = dkv.astype(MXU)
    g['w_mkv'] = _mm_tn("dw_mkv", sv['memn'], dkvb)
    g['norm_memkv'] = _gain_grad("memkv_gain", _mm_nt("d_memn", dkvb, W['w_mkv'], ts=256, tn=512), mem)
    return dx1, g


def _layer_bwd_mix(dx1, W, sv, ctab, stab):
    du, g = _layer_bwd_mixers(dx1, W, sv, ctab, stab)
    g['w_in'] = _dw_in(du, sv)
    dx0, g['norm_mix_pre'] = _mix_pre_bwd(dx1, du, W, sv)
    return dx0, g


def _dw_in(du, sv):
    return _mm_tn("dw_in", du, sv['h1'], cap_k=IN_TILE, rot=IN_ROT)


def _mix_pre_bwd(dx1, du, W, sv):
    return _prenorm_bwd("mix_pre_bwd", dx1, du, W['w_in'], sv['x0'], W['g_mix_pre'], lead=GATE_W)


def _layer_bwd_mixers(dx1, W, sv, ctab, stab):
    g = {}
    dy1, du, dbra, dbrb, dbrc, da2, dyb, datt, g['norm_mix_post'] = _merge_bwd(
        dx1, sv['y1'], sv['u'], sv['a2'], sv['yb'], sv['att'], W['wa'], W['wb'], W['wc'], W['w_out'], W['g_mix_post'])
    g['w_out'] = _mm_tn("dw_out", sv['merged'], dy1)
    g['w_branch_a'] = _mm_tn("dw_a", dbra, sv['a2'])
    g['w_branch_b'] = _mm_tn("dw_b", dbrb, sv['yb'])
    g['w_branch_c'] = _mm_tn("dw_c", dbrc, sv['att'])
    du, g['pool_scale'], dwblk, g['conv_b_w'] = _poolconv_bwd(sv['u'], da2, dyb, du, W['wblk'], W['pool_scale'], W['conv_b'])
    g['pool_w'] = jnp.stack([dwblk[k * 96:(k + 1) * 96, k * 96:(k + 1) * 96] for k in range(4)])
    q3, k3, v3 = sv['qkv']
    do3, dl3, lse3 = _attn_bwd_prep(datt, sv['o'], sv['lse'])
    dqkv3 = [_attn_bwd(i, q3[i], k3[i], v3[i], do3[i], dl3[i], lse3[i]) for i in range(3)]
    du = _rope_unperm_bwd([[t[which] for t in dqkv3] for which in range(3)], du, ctab, stab)
    return du, g


def _local_step(x, mem, positions, target, big, small):
    ctab, stab = _rope_tables(positions)
    Ws = [_layer_weights(big[l], small, l) for l in range(DEPTH)]
    saved = []
    for l in range(DEPTH):
        x, sv = _layer_fwd(x, mem, Ws[l], ctab, stab)
        saved.append(sv)
    dx, acc = _loss_head(x, target)
    loss = jnp.sum(acc) * (0.5 / D)
    grads = [None] * DEPTH
    for l in reversed(range(DEPTH)):
        dx, grads[l] = _layer_bwd(dx, mem, Ws[l], saved[l], ctab, stab)
    return loss, dx, grads


_HBM = pl.BlockSpec(memory_space=pl.ANY)
MESH_ID = pl.DeviceIdType.MESH


def _all_gather(name, xs):
    n = len(xs)

    def body(*refs):
        x_refs, out_refs = refs[:n], refs[n:2 * n]
        send_sems, recv_sems, local_sems = refs[2 * n:]
        x, y, c = lax.axis_index("x"), lax.axis_index("y"), lax.axis_index("c")
        me, sibling = (x, y, c), (x, y, 1 - c)
        chips = [(1 - x, y), (x, 1 - y), (1 - x, 1 - y)]

        def slot(a, p):
            return out_refs[a].at[4 * p[0] + 2 * p[1] + p[2]]

        def copy(a, k, block, to, src=None):
            return pltpu.make_async_remote_copy(src_ref=slot(a, block) if src is None else src, dst_ref=slot(a, block),
                                                send_sem=send_sems.at[a, k], recv_sem=recv_sems.at[a, k], device_id=to,
                                                device_id_type=MESH_ID)

        started = []
        for a in range(n):
            mine = pltpu.make_async_copy(x_refs[a], slot(a, me), local_sems.at[a])
            mine.start()
            started.append(mine)
        first = []
        for a in range(n):
            first.append(copy(a, 0, me, sibling, src=x_refs[a]))
            first += [copy(a, 1 + j, me, (*chip, c), src=x_refs[a]) for j, chip in enumerate(chips)]
        for cp in first:
            cp.start()
        passed = []
        for j, chip in enumerate(chips):
            for a in range(n):
                copy(a, 1 + j, (*chip, c), me).wait_recv()
                fw = copy(a, 4 + j, (*chip, c), sibling)
                fw.start()
                passed.append(fw)
        for a in range(n):
            copy(a, 0, sibling, me).wait_recv()
            for j, chip in enumerate(chips):
                copy(a, 4 + j, (*chip, 1 - c), me).wait_recv()
        for cp in first + passed:
            cp.wait_send()
        for mine in started:
            mine.wait()

    return pl.pallas_call(
        body, out_shape=[jax.ShapeDtypeStruct((N_DEV,) + x.shape, x.dtype) for x in xs], in_specs=[_HBM] * n, out_specs=[_HBM] * n,
        scratch_shapes=[pltpu.SemaphoreType.DMA((n, 7)), pltpu.SemaphoreType.DMA((n, 7)), pltpu.SemaphoreType.DMA((n,))],
        name=name)(*xs)


def _exchange(name, gs):
    n = len(gs)

    def body(*refs):
        g_refs, out_refs = refs[:n], refs[n:2 * n]
        send_sems, recv_sems, local_sems = refs[2 * n:]
        x, y, c = lax.axis_index("x"), lax.axis_index("y"), lax.axis_index("c")
        me = 4 * x + 2 * y + c
        copies = []
        for a in range(n):
            mine = pltpu.make_async_copy(g_refs[a].at[me], out_refs[a].at[me], local_sems.at[a])
            mine.start()
            copies.append(mine)
        for r in range(1, N_DEV):
            px, py, pc = x ^ ((r >> 2) & 1), y ^ ((r >> 1) & 1), c ^ (r & 1)
            for a in range(n):
                cp = pltpu.make_async_remote_copy(src_ref=g_refs[a].at[4 * px + 2 * py + pc], dst_ref=out_refs[a].at[me],
                                                  send_sem=send_sems.at[a, r - 1], recv_sem=recv_sems.at[a, r - 1],
                                                  device_id=(px, py, pc), device_id_type=MESH_ID)
                cp.start()
                copies.append(cp)
        for cp in copies:
            cp.wait()

    return pl.pallas_call(
        body, out_shape=[jax.ShapeDtypeStruct(g.shape, g.dtype) for g in gs], in_specs=[_HBM] * n, out_specs=[_HBM] * n,
        scratch_shapes=[pltpu.SemaphoreType.DMA((n, N_DEV - 1)), pltpu.SemaphoreType.DMA((n, N_DEV - 1)), pltpu.SemaphoreType.DMA((n,))],
        name=name)(*gs)


_SEM = pl.BlockSpec(memory_space=pltpu.SEMAPHORE)
_IN_HBM = pl.BlockSpec(memory_space=pltpu.HBM)
_SIDE_EFFECT = pltpu.SideEffectType.DATAFLOW_SIDE_EFFECTING


def _push_copies(src_refs, land_refs, send_sems, recv_sems, per_peer):
    x, y, c = lax.axis_index("x"), lax.axis_index("y"), lax.axis_index("c")
    me = 4 * x + 2 * y + c
    copies = []
    for r in range(1, N_DEV):
        px, py, pc = x ^ ((r >> 2) & 1), y ^ ((r >> 1) & 1), c ^ (r & 1)
        for a, (s, d) in enumerate(zip(src_refs, land_refs)):
            k = a * (N_DEV - 1) + r - 1
            copies.append(pltpu.make_async_remote_copy(src_ref=s.at[4 * px + 2 * py + pc] if per_peer else s, dst_ref=d.at[me],
                                                       send_sem=send_sems.at[k], recv_sem=recv_sems.at[k],
                                                       device_id=(px, py, pc), device_id_type=MESH_ID))
    return copies


def _push_start(name, srcs, per_peer, after):
    n = len(srcs)
    lands = [lax.empty((N_DEV,) + (s.shape[1:] if per_peer else s.shape), s.dtype) for s in srcs]

    def body(*refs):
        for cp in _push_copies(refs[:n], refs[n:2 * n], refs[2 * n + 1], refs[2 * n + 2], per_peer):
            cp.start()
        refs[-1][...] = jnp.zeros_like(refs[-1])

    hbm = [pltpu.HBM(a.shape, a.dtype) for a in (*srcs, *lands)]
    sems = pltpu.SemaphoreType.DMA((n * (N_DEV - 1),))
    out = pl.pallas_call(
        body, name=name, out_shape=(sems, sems, *hbm, jax.ShapeDtypeStruct((8, LANES), F32)),
        in_specs=[_IN_HBM] * (2 * n) + [pl.BlockSpec(memory_space=pl.ANY)],
        out_specs=(_SEM, _SEM, *[_IN_HBM] * (2 * n), pl.BlockSpec(memory_space=pltpu.VMEM)),
        input_output_aliases={a: 2 + a for a in range(2 * n)},
        compiler_params=pltpu.CompilerParams(has_side_effects=_SIDE_EFFECT),
    )(*[pltpu.with_memory_space_constraint(a, pltpu.HBM) for a in (*srcs, *lands)], after)
    return out[0], out[1], out[2:2 + n], out[2 + n:2 + 2 * n], out[-1]


def _push_wait(name, started, per_peer, after):
    send_sems, recv_sems, srcs, lands, _ = started
    n = len(srcs)

    def body(*refs):
        for cp in _push_copies(refs[:n], refs[n:2 * n], refs[2 * n], refs[2 * n + 1], per_peer):
            cp.wait_send()
            cp.wait_recv()

    out = pl.pallas_call(
        body, name=name, out_shape=[pltpu.HBM(a.shape, a.dtype) for a in (*srcs, *lands)],
        in_specs=[_IN_HBM] * (2 * n) + [_SEM, _SEM, pl.BlockSpec(memory_space=pl.ANY)], out_specs=[_IN_HBM] * (2 * n),
        input_output_aliases={a: a for a in range(2 * n)},
        compiler_params=pltpu.CompilerParams(has_side_effects=_SIDE_EFFECT),
    )(*srcs, *lands, send_sems, recv_sems, after)
    return out[n:]


def _my_slot():
    return 4 * lax.axis_index("x") + 2 * lax.axis_index("y") + lax.axis_index("c")


def _row_tile(rows, cols, budget):
    if rows * cols * 4 <= budget or rows % 16:
        return rows
    best = 16
    for t in range(16, rows + 1, 16):
        if rows % t == 0 and t * cols * 4 <= budget:
            best = t
    return best


def _sum_slots(name, recv):
    _, R, C = recv.shape
    tr = _row_tile(R, C, 1 << 20)

    def body(r_ref, o_ref):
        g = r_ref[0].astype(F32)
        for k in range(1, N_DEV):
            g = g + r_ref[k].astype(F32)
        o_ref[...] = g

    return pl.pallas_call(body, grid=(R // tr,), in_specs=[pl.BlockSpec((N_DEV, tr, C), lambda i: (0, i, 0))],
                          out_specs=pl.BlockSpec((tr, C), lambda i: (i, 0)), out_shape=jax.ShapeDtypeStruct((R, C), F32),
                          compiler_params=_params(("arbitrary",), 32), name=name)(recv)


def _adamw(name, g, w, m, v):
    shape = w.shape
    R, C = shape[-2], shape[-1]
    view = (-1, R, C)
    L = w.size // (R * C)
    tr = _row_tile(R, C, 1 << 20)
    c1 = 1.0 - ADAM_B1 ** ADAM_STEP
    c2 = 1.0 - ADAM_B2 ** ADAM_STEP

    def body(g_ref, w_ref, m_ref, v_ref, d_ref, mo_ref, vo_ref):
        gv = g_ref[...]
        mn = ADAM_B1 * m_ref[...] + (1.0 - ADAM_B1) * gv
        vn = ADAM_B2 * v_ref[...] + (1.0 - ADAM_B2) * (gv * gv)
        mo_ref[...] = mn
        vo_ref[...] = vn
        d_ref[...] = -ADAM_LR * ((mn / c1) / (jnp.sqrt(vn / c2) + ADAM_EPS) + ADAM_WD * w_ref[...])

    blk = pl.BlockSpec((None, tr, C), lambda l, i: (l, i, 0))
    shp = jax.ShapeDtypeStruct((L, R, C), F32)
    outs = pl.pallas_call(body, grid=(L, R // tr), in_specs=[blk, blk, blk, blk], out_specs=[blk, blk, blk], out_shape=[shp, shp, shp],
                          compiler_params=_params(("arbitrary", "arbitrary"), 32), name=name)(*[a.reshape(view) for a in (g, w, m, v)])
    return [o.reshape(shape) for o in outs]


def _pad_flat(a, n):
    a = a.reshape(-1)
    return jnp.pad(a, (0, n - a.shape[0]))


def _seg(n):
    return -(-n // FLAT_ALIGN) * FLAT_ALIGN


def _to_blocks(full, axis):
    shp = full.shape
    return jnp.moveaxis(full.reshape(shp[:axis] + (N_DEV, shp[axis] // N_DEV) + shp[axis + 1:]), axis, 0)


def _from_blocks(blocks, axis):
    b = jnp.moveaxis(blocks, 0, axis)
    shp = b.shape
    return b.reshape(shp[:axis] + (shp[axis] * shp[axis + 1],) + shp[axis + 2:])


def _as_rows(shard, n):
    return shard.T if SHARD_AXIS[n] == 2 else shard


def _with_own(lands, own, me):
    return [lax.dynamic_update_slice(land, o[None], (me, 0, 0)) for land, o in zip(lands, own)]


def kernel(x, mem, positions, norm_mix_pre, norm_mix_post, w_in, pool_w, pool_scale, conv_b_w, w_branch_a, w_branch_b, w_branch_c, w_out, norm_mem_pre, norm_mem_post, norm_memkv, w_mq, w_mkv, w_mo, norm_ffn_pre, norm_ffn_post, w_up, conv_ffn_w, w_down, loss_target, m_norm_mix_pre, m_norm_mix_post, m_w_in, m_pool_w, m_pool_scale, m_conv_b_w, m_w_branch_a, m_w_branch_b, m_w_branch_c, m_w_out, m_norm_mem_pre, m_norm_mem_post, m_norm_memkv, m_w_mq, m_w_mkv, m_w_mo, m_norm_ffn_pre, m_norm_ffn_post, m_w_up, m_conv_ffn_w, m_w_down, v_norm_mix_pre, v_norm_mix_post, v_w_in, v_pool_w, v_pool_scale, v_conv_b_w, v_w_branch_a, v_w_branch_b, v_w_branch_c, v_w_out, v_norm_mem_pre, v_norm_mem_post, v_norm_memkv, v_w_mq, v_w_mkv, v_w_mo, v_norm_ffn_pre, v_norm_ffn_post, v_w_up, v_conv_ffn_w, v_w_down):
    w = dict(norm_mix_pre=norm_mix_pre, norm_mix_post=norm_mix_post, w_in=w_in, pool_w=pool_w, pool_scale=pool_scale, conv_b_w=conv_b_w, w_branch_a=w_branch_a, w_branch_b=w_branch_b, w_branch_c=w_branch_c, w_out=w_out, norm_mem_pre=norm_mem_pre, norm_mem_post=norm_mem_post, norm_memkv=norm_memkv, w_mq=w_mq, w_mkv=w_mkv, w_mo=w_mo, norm_ffn_pre=norm_ffn_pre, norm_ffn_post=norm_ffn_post, w_up=w_up, conv_ffn_w=conv_ffn_w, w_down=w_down)
    m = dict(norm_mix_pre=m_norm_mix_pre, norm_mix_post=m_norm_mix_post, w_in=m_w_in, pool_w=m_pool_w, pool_scale=m_pool_scale, conv_b_w=m_conv_b_w, w_branch_a=m_w_branch_a, w_branch_b=m_w_branch_b, w_branch_c=m_w_branch_c, w_out=m_w_out, norm_mem_pre=m_norm_mem_pre, norm_mem_post=m_norm_mem_post, norm_memkv=m_norm_memkv, w_mq=m_w_mq, w_mkv=m_w_mkv, w_mo=m_w_mo, norm_ffn_pre=m_norm_ffn_pre, norm_ffn_post=m_norm_ffn_post, w_up=m_w_up, conv_ffn_w=m_conv_ffn_w, w_down=m_w_down)
    v = dict(norm_mix_pre=v_norm_mix_pre, norm_mix_post=v_norm_mix_post, w_in=v_w_in, pool_w=v_pool_w, pool_scale=v_pool_scale, conv_b_w=v_conv_b_w, w_branch_a=v_w_branch_a, w_branch_b=v_w_branch_b, w_branch_c=v_w_branch_c, w_out=v_w_out, norm_mem_pre=v_norm_mem_pre, norm_mem_post=v_norm_mem_post, norm_memkv=v_norm_memkv, w_mq=v_w_mq, w_mkv=v_w_mkv, w_mo=v_w_mo, norm_ffn_pre=v_norm_ffn_pre, norm_ffn_post=v_norm_ffn_post, w_up=v_w_up, conv_ffn_w=v_conv_ffn_w, w_down=v_w_down)

    me = _my_slot()
    mix_big = [n for n in BIG if n not in LATE_BIG]
    block = lambda names, l: [_as_rows(w[n][l], n).astype(MXU) for n in names]
    conv = jnp.concatenate([_pad_flat(w[n], _seg(w[n].size)) for n in F32_GATHERED]).reshape(-1, LANES)
    got0 = _all_gather("weights_all_gather_0", block(mix_big, 0) + [conv])
    conv_all = got0[-1].reshape(N_DEV, -1)
    small, off = {n: w[n] for n in WEIGHTS if n not in SHARD_AXIS}, 0
    for n in F32_GATHERED:
        small[n] = _from_blocks(conv_all[:, off:off + w[n].size].reshape((N_DEV,) + w[n].shape), 2)
        off += _seg(w[n].size)
    whole = lambda names, got: {n: o.reshape(-1, o.shape[-1]) for n, o in zip(names, got)}
    pushes, after = {}, got0[0]
    for names, l in ((LATE_BIG, 0), (mix_big, 1), (LATE_BIG, 1)):
        pushes[names is LATE_BIG, l] = _push_start(f"weights_push_start_{l}{'b' if names is LATE_BIG else 'a'}", block(names, l), False, after)
        after = pushes[names is LATE_BIG, l][4]

    def arrived(late, l, done):
        names = LATE_BIG if late else mix_big
        lands = _push_wait(f"weights_push_wait_{l}{'b' if late else 'a'}", pushes[late, l], False, done)
        return _big_operands(whole(names, _with_own(lands, block(names, l), me)))

    ctab, stab = _rope_tables(positions[0])
    W0 = _layer_weights(whole(mix_big, got0), small, 0)
    sv0 = _layer_fwd_mix(x[0], dict(W0, g_mix_pre=W0['g_mix_pre'] + after[0, 0]), ctab, stab)
    W0.update(arrived(True, 0, sv0['x1']))
    x1 = _layer_fwd_late(mem[0], W0, sv0)
    W1 = _layer_weights({}, small, 1)
    W1.update(arrived(False, 1, x1))
    sv1 = _layer_fwd_mix(x1, W1, ctab, stab)
    W1.update(arrived(True, 1, sv1['x1']))
    x2 = _layer_fwd_late(mem[0], W1, sv1)
    dx, acc = _loss_head(x2, loss_target[0])
    loss = lax.psum(jnp.sum(acc) * (0.5 / D), MESH_AXES)
    grads = [None] * DEPTH
    dx, grads[1] = _layer_bwd(dx, mem[0], W1, sv1, ctab, stab)
    sent = [None, [grads[1][n].reshape(N_DEV, -1, grads[1][n].shape[-1]) for n in BIG]]
    push_g = _push_start("grads_push_start_1", sent[1], True, dx)
    dx, g_late = _layer_bwd_late(dx, mem[0], dict(W0, g_ffn_post=W0['g_ffn_post'] + push_g[4][0, 0]), sv0)
    sent_late = [g_late[n].reshape(N_DEV, -1, g_late[n].shape[-1]) for n in LATE_BIG]
    push_l = _push_start("grads_push_start_0", sent_late, True, dx)
    du, g_mix = _layer_bwd_mixers(dx, dict(W0, g_mix_post=W0['g_mix_post'] + push_l[4][0, 0]), sv0, ctab, stab)
    g_mix['w_in'] = _dw_in(du, sv0)
    sent_in = [g_mix['w_in'].reshape(N_DEV, -1, D)]
    push_i = _push_start("grads_push_start_in", sent_in, True, du)
    dx, g_mix['norm_mix_pre'] = _mix_pre_bwd(dx, du, dict(W0, g_mix_pre=W0['g_mix_pre'] + push_i[4][0, 0]), sv0)
    grads[0] = {**g_late, **g_mix}
    own = lambda s: [lax.dynamic_index_in_dim(a, me, 0, keepdims=False) for a in s]
    recv1 = _with_own(_push_wait("grads_push_wait_1", push_g, True, dx), own(sent[1]), me)
    recv_late = _with_own(_push_wait("grads_push_wait_0", push_l, True, dx), own(sent_late), me)
    mix_rest = [n for n in mix_big if n != 'w_in']

    misc_names = [n for n in WEIGHTS if n not in BIG]
    stacked = {n: jnp.stack([grads[l][n].reshape(small[n].shape[1:]) for l in range(DEPTH)]) for n in misc_names}
    rows = [(_to_blocks(stacked[n], 2) if n in SHARD_AXIS else jnp.broadcast_to(stacked[n][None], (N_DEV,) + stacked[n].shape))
            for n in misc_names]
    segs = [_seg(w[n].size) for n in misc_names]
    misc = jnp.concatenate([jnp.pad(r.reshape(N_DEV, -1), ((0, 0), (0, s - r[0].size))) for r, s in zip(rows, segs)],
                           axis=1).reshape(N_DEV, -1, LANES)
    recv_mix = _exchange("grad_exchange_0", [g_mix[n].reshape(N_DEV, -1, g_mix[n].shape[-1]) for n in mix_rest] + [misc])
    g_out, per_layer = {}, {}
    for l, names, recv in ((1, BIG, recv1), (0, LATE_BIG, recv_late), (0, mix_rest, recv_mix)):
        for n, r in zip(names, recv):
            per_layer[n, l] = _as_rows(_sum_slots(f"sum_{n}_{l}", r), n)
    misc_sum = _sum_slots("sum_misc", recv_mix[-1]).reshape(-1)
    off = 0
    for n, s in zip(misc_names, segs):
        g_out[n] = misc_sum[off:off + w[n].size].reshape(w[n].shape)
        off += s
    def update(n):
        g_out[n] = jnp.stack([per_layer[n, l] for l in range(DEPTH)]) if n in BIG else g_out[n]
        return [g_out[n], *_adamw(f"adamw_{n}", g_out[n], w[n], m[n], v[n])]

    done = {n: update(n) for n in WEIGHTS if n != 'w_in'}
    recv_in = _with_own(_push_wait("grads_push_wait_in", push_i, True, done[WEIGHTS[-1]][1]), own(sent_in), me)
    per_layer['w_in', 0] = _as_rows(_sum_slots("sum_w_in_0", recv_in[0]), 'w_in')
    done['w_in'] = update('w_in')
    return (loss, dx[None], *[done[n][k] for k in range(4) for n in WEIGHTS])
```

```python
import jax
import jax.numpy as jnp
from jax import lax
from jax.experimental import pallas as pl
from jax.experimental.pallas import tpu as pltpu

F32 = jnp.float32
MXU = jnp.bfloat16
HI = lax.Precision.HIGHEST

D = 1024
DEPTH = 2
POOLW = 384
ATT_W = 768
ATT_O = 256
GATE_W = 3 * D
IN_W = 6912
IN_TILE = 768
IN_ROT = (IN_W - GATE_W) // IN_TILE
MEM_W = 512
D_FF = 2816
EPS = 1e-6
ROPE_THETA = 500000.0
QB = 128
DILS = (1, 4, 16)
NEG = -1e30
MEM_SCALE = 128 ** -0.5
ATT_SCALE = 0.125

ADAM_LR, ADAM_B1, ADAM_B2, ADAM_EPS, ADAM_WD, ADAM_STEP = 0.001, 0.9, 0.999, 1e-08, 0.01, 10

N_DEV = 8
MESH_AXES = ("x", "y", "c")
LANES = 128
FLAT_ALIGN = 2048
ROW_TILE = 1024

WEIGHTS = ['norm_mix_pre', 'norm_mix_post', 'w_in', 'pool_w', 'pool_scale', 'conv_b_w', 'w_branch_a', 'w_branch_b',
           'w_branch_c', 'w_out', 'norm_mem_pre', 'norm_mem_post', 'norm_memkv', 'w_mq', 'w_mkv', 'w_mo',
           'norm_ffn_pre', 'norm_ffn_post', 'w_up', 'conv_ffn_w', 'w_down']
SHARD_AXIS = {'w_in': 2, 'conv_b_w': 2, 'w_branch_a': 2, 'w_branch_b': 2, 'w_branch_c': 2, 'w_out': 1, 'w_mq': 1,
              'w_mkv': 1, 'w_mo': 2, 'w_up': 2, 'conv_ffn_w': 2, 'w_down': 1}
F32_GATHERED = ('conv_b_w', 'conv_ffn_w')
BIG = [n for n in WEIGHTS if n in SHARD_AXIS and n not in F32_GATHERED]
LATE_BIG = ['w_mq', 'w_mkv', 'w_mo', 'w_up', 'w_down']


VMEM_LIMIT_MB = 60


def _params(sem, vmem_mb):
    del vmem_mb
    return pltpu.CompilerParams(dimension_semantics=sem, vmem_limit_bytes=VMEM_LIMIT_MB << 20)


def _dot(a, b, prec=None):
    return lax.dot_general(a, b, (((1,), (0,)), ((), ())), preferred_element_type=F32, precision=prec)


def _dot_nt(a, b, prec=None):
    return lax.dot_general(a, b, (((1,), (1,)), ((), ())), preferred_element_type=F32, precision=prec)


def _dot_tn(a, b, prec=None):
    return lax.dot_general(a, b, (((0,), (0,)), ((), ())), preferred_element_type=F32, precision=prec)


def _tile(n, cap):
    if n <= cap:
        return n
    best = None
    for t in range(LANES, cap + 1, LANES):
        if n % t == 0:
            best = t
    assert best is not None, (n, cap)
    return best


def _rms(x, g):
    r = lax.rsqrt(jnp.mean(x * x, axis=-1, keepdims=True) + EPS)
    return x * r * g, r


def _rms_bwd(w, y):
    r = lax.rsqrt(jnp.mean(y * y, axis=-1, keepdims=True) + EPS)
    return r * w - y * (r * r * r) * jnp.mean(w * y, axis=-1, keepdims=True), r


def _rows_call(name, body, n_rows, ts, ins, outs, scratch=(), reverse=False, vmem_mb=48, aliases=None):
    nt = n_rows // ts
    assert nt * ts == n_rows

    def tile_of(g):
        return (nt - 1 - g) if reverse else g

    in_specs, args = [], []
    for op in ins:
        if op[0] == "t":
            _, a, cw, cb = op
            in_specs.append(pl.BlockSpec((ts, cw), lambda g, cb=cb: (tile_of(g), cb)))
        elif op[0] == "h":
            _, a, hr, cw, cb = op
            in_specs.append(pl.BlockSpec((hr, cw), lambda g, cb=cb, k=ts // hr: (jnp.maximum(tile_of(g) * k - 1, 0), cb)))
        elif op[0] == "x":
            _, a = op
            in_specs.append(pl.BlockSpec(memory_space=pl.ANY))
        else:
            _, a = op
            in_specs.append(pl.BlockSpec(a.shape, lambda g, n=a.ndim: (0,) * n))
        args.append(a)
    out_specs, out_shape = [], []
    for op in outs:
        if op[0] == "t":
            _, cols, dt = op
            out_specs.append(pl.BlockSpec((ts, cols), lambda g: (tile_of(g), 0)))
            out_shape.append(jax.ShapeDtypeStruct((n_rows, cols), dt))
        elif op[0] == "c":
            _, total, cols, cb, dt = op
            out_specs.append(pl.BlockSpec((ts, cols), lambda g, cb=cb: (tile_of(g), cb)))
            out_shape.append(jax.ShapeDtypeStruct((n_rows, total), dt))
        else:
            _, shp, dt = op
            out_specs.append(pl.BlockSpec(shp, lambda g, n=len(shp): (0,) * n))
            out_shape.append(jax.ShapeDtypeStruct(shp, dt))

    def kern(*refs):
        g = pl.program_id(0)
        body(tile_of(g), g, *refs)

    return pl.pallas_call(kern, grid=(nt,), in_specs=in_specs, out_specs=out_specs, out_shape=out_shape,
                          scratch_shapes=list(scratch), input_output_aliases=aliases or {},
                          compiler_params=_params(("arbitrary",), vmem_mb), name=name)(*args)


def _acc(ref, g, val):
    @pl.when(g == 0)
    def _():
        ref[...] = val

    @pl.when(g != 0)
    def _():
        ref[...] += val


def _norm_mm(name, x, g, w, ts, tn, out_dtype=F32, wt=False, rot=0):
    S, K = x.shape
    N = w.shape[0] if wt else w.shape[1]
    assert wt or not rot

    def body(x_ref, g_ref, w_ref, o_ref, h_ref, hs):
        @pl.when(pl.program_id(1) == 0)
        def _():
            h, _ = _rms(x_ref[...], g_ref[...])
            hs[...] = h.astype(MXU)
            h_ref[...] = h.astype(MXU)

        o_ref[...] = (_dot_nt if wt else _dot)(hs[...], w_ref[...]).astype(out_dtype)

    w_spec = pl.BlockSpec((tn, K), lambda i, j: ((j + rot) % (N // tn), 0)) if wt else pl.BlockSpec((K, tn), lambda i, j: (0, j))
    return pl.pallas_call(
        body, grid=(S // ts, N // tn),
        in_specs=[pl.BlockSpec((ts, K), lambda i, j: (i, 0)), pl.BlockSpec((1, K), lambda i, j: (0, 0)), w_spec],
        out_specs=[pl.BlockSpec((ts, tn), lambda i, j: (i, j)), pl.BlockSpec((ts, K), lambda i, j: (i, 0))],
        out_shape=[jax.ShapeDtypeStruct((S, N), out_dtype), jax.ShapeDtypeStruct((S, K), MXU)],
        scratch_shapes=[pltpu.VMEM((ts, K), MXU)],
        compiler_params=_params(("arbitrary", "arbitrary"), 48), name=name)(x, g, w)


def _mm_nt(name, a, b, ts, tn, out_dtype=F32):
    M, K = a.shape
    N = b.shape[0]

    def body(a_ref, b_ref, o_ref):
        o_ref[...] = _dot_nt(a_ref[...], b_ref[...]).astype(out_dtype)

    return pl.pallas_call(
        body, grid=(M // ts, N // tn),
        in_specs=[pl.BlockSpec((ts, K), lambda i, j: (i, 0)), pl.BlockSpec((tn, K), lambda i, j: (j, 0))],
        out_specs=pl.BlockSpec((ts, tn), lambda i, j: (i, j)), out_shape=jax.ShapeDtypeStruct((M, N), out_dtype),
        compiler_params=_params(("arbitrary", "arbitrary"), 48), name=name)(a, b)


def _mm_tn(name, a, b, cap_k=512, cap_n=1024, out_dtype=MXU, rot=0):
    S, K = a.shape
    N = b.shape[1]
    tk, tn = _tile(K, cap_k), _tile(N, cap_n)

    def body(a_ref, b_ref, o_ref):
        o_ref[...] = _dot_tn(a_ref[...], b_ref[...]).astype(out_dtype)

    return pl.pallas_call(
        body, grid=(K // tk, N // tn),
        in_specs=[pl.BlockSpec((S, tk), lambda i, j: (0, i)), pl.BlockSpec((S, tn), lambda i, j: (0, j))],
        out_specs=pl.BlockSpec((tk, tn), lambda i, j: ((i + rot) % (K // tk), j)), out_shape=jax.ShapeDtypeStruct((K, N), out_dtype),
        compiler_params=_params(("arbitrary", "arbitrary"), 48), name=name)(a, b)


def _pool_cols(shape):
    col = lax.broadcasted_iota(jnp.int32, shape, 1)
    return col < 96, col < 192, col < 288


def _pool_select(s2, s4, s8, s16):
    c1, c2, c3 = _pool_cols(s2.shape)
    return jnp.where(c1, s2, jnp.where(c2, s4, jnp.where(c3, s8, s16)))


def _pool_cnt(t0, ts):
    c1, c2, c3 = _pool_cols((ts, POOLW))
    win = jnp.where(c1, 2, jnp.where(c2, 4, jnp.where(c3, 8, 16)))
    t = t0 + lax.broadcasted_iota(jnp.int32, (ts, POOLW), 0)
    return jnp.minimum(t + 1, win).astype(F32)


def _pooled(a, prev, t0):
    ts = a.shape[0]
    ext = jnp.concatenate([prev, a], axis=0)
    s2 = ext + pltpu.roll(ext, 1, axis=0)
    s4 = s2 + pltpu.roll(s2, 2, axis=0)
    s8 = s4 + pltpu.roll(s4, 4, axis=0)
    s16 = s8 + pltpu.roll(s8, 8, axis=0)
    sums = _pool_select(s2, s4, s8, s16)[16:]
    return sums / _pool_cnt(t0, ts) - a


def _conv3(z, prev8, w):
    ext = jnp.concatenate([prev8, z], axis=0)
    z1 = pltpu.roll(ext, 1, axis=0)[8:]
    z2 = pltpu.roll(ext, 2, axis=0)[8:]
    return w[0:1] * z2 + w[1:2] * z1 + w[2:3] * z, z1, z2


def _conv3_t(dc, next8, w):
    ts = dc.shape[0]
    ext = jnp.concatenate([dc, next8], axis=0)
    n = ts + 8
    u1 = pltpu.roll(ext, n - 1, axis=0)[:ts]
    u2 = pltpu.roll(ext, n - 2, axis=0)[:ts]
    return w[2:3] * dc + w[1:2] * u1 + w[0:1] * u2


def _poolconv_fwd(u, wblk, pool_scale, conv_b, ts=256):
    S = u.shape[0]

    def body(i, g, a_ref, bx_ref, bb_ref, bc_ref, wblk_ref, ps_ref, cw_ref, a2_ref, yb_ref, ca, cz):
        @pl.when(g == 0)
        def _():
            ca[...] = jnp.zeros_like(ca)
            cz[...] = jnp.zeros_like(cz)

        a = a_ref[...]
        p = _pooled(a, ca[...], i * ts)
        mixed = _dot(p.astype(MXU), wblk_ref[...])
        a2_ref[...] = (mixed * ps_ref[...]).astype(MXU)
        z = bc_ref[...] * bx_ref[...]
        conv, _, _ = _conv3(z, cz[...], cw_ref[...])
        yb_ref[...] = (bb_ref[...] * conv).astype(MXU)
        ca[...] = a[ts - 16:]
        cz[...] = z[ts - 8:]

    ins = [("t", u, POOLW, 8), ("t", u, POOLW, 9), ("t", u, POOLW, 10), ("t", u, POOLW, 11), ("w", wblk), ("w", pool_scale),
           ("w", conv_b)]
    return _rows_call("poolconv_fwd", body, S, ts, ins, [("t", POOLW, MXU), ("t", POOLW, MXU)],
                      scratch=[pltpu.VMEM((16, POOLW), F32), pltpu.VMEM((8, POOLW), F32)])


def _poolconv_bwd(u, d_a2, d_yb, du, wblk, pool_scale, conv_b, ts=256):
    S = u.shape[0]

    def body(i, g, a_ref, bx_ref, bb_ref, bc_ref, ap_ref, bxp_ref, bcp_ref, da2_ref, dyb_ref, wblk_ref, ps_ref, cw_ref, _,
             o_ref, dps_ref, dwb_ref, dcw_ref, ce, cdz):
        @pl.when(g == 0)
        def _():
            ce[...] = jnp.zeros_like(ce)
            cdz[...] = jnp.zeros_like(cdz)

        first = (i > 0).astype(F32)
        a = a_ref[...]
        p = _pooled(a, ap_ref[...] * first, i * ts)
        pb = p.astype(MXU)
        mixed = _dot(pb, wblk_ref[...])
        da2 = da2_ref[...]
        dmixed = (da2 * ps_ref[...]).astype(MXU)
        dp = _dot_nt(dmixed, wblk_ref[...])
        _acc(dps_ref, g, jnp.sum(da2 * mixed, axis=0, keepdims=True))
        _acc(dwb_ref, g, _dot_tn(pb, dmixed))
        e = dp / _pool_cnt(i * ts, ts)
        ext = jnp.concatenate([e, ce[...]], axis=0)
        n = ts + 16
        f2 = ext + pltpu.roll(ext, n - 1, axis=0)
        f4 = f2 + pltpu.roll(f2, n - 2, axis=0)
        f8 = f4 + pltpu.roll(f4, n - 4, axis=0)
        f16 = f8 + pltpu.roll(f8, n - 8, axis=0)
        o_ref[:, 0:POOLW] = (_pool_select(f2, f4, f8, f16)[:ts] - dp).astype(o_ref.dtype)
        ce[...] = e[:16]

        bx, bb, bc = bx_ref[...], bb_ref[...], bc_ref[...]
        z = bc * bx
        w = cw_ref[...]
        conv, z1, z2 = _conv3(z, bxp_ref[...] * bcp_ref[...] * first, w)
        dyb = dyb_ref[...]
        dconv = dyb * bb
        dz = _conv3_t(dconv, cdz[...], w)
        o_ref[:, POOLW:2 * POOLW] = (dz * bc).astype(o_ref.dtype)
        o_ref[:, 2 * POOLW:3 * POOLW] = (dyb * conv).astype(o_ref.dtype)
        o_ref[:, 3 * POOLW:4 * POOLW] = (dz * bx).astype(o_ref.dtype)
        dw = jnp.concatenate([jnp.sum(dconv * z2, axis=0, keepdims=True), jnp.sum(dconv * z1, axis=0, keepdims=True),
                              jnp.sum(dconv * z, axis=0, keepdims=True)], axis=0)
        _acc(dcw_ref, g, dw)
        cdz[...] = dconv[:8]

    ins = [("t", u, POOLW, 8), ("t", u, POOLW, 9), ("t", u, POOLW, 10), ("t", u, POOLW, 11),
           ("h", u, 16, POOLW, 8), ("h", u, 8, POOLW, 9), ("h", u, 8, POOLW, 11),
           ("t", d_a2, POOLW, 0), ("t", d_yb, POOLW, 0), ("w", wblk), ("w", pool_scale), ("w", conv_b), ("x", du)]
    outs = [("c", IN_W, 4 * POOLW, GATE_W // (4 * POOLW), MXU), ("a", (1, POOLW), F32), ("a", (POOLW, POOLW), F32), ("a", (3, POOLW), F32)]
    return _rows_call("poolconv_bwd", body, S, ts, ins, outs, aliases={len(ins) - 1: 0},
                      scratch=[pltpu.VMEM((16, POOLW), F32), pltpu.VMEM((8, POOLW), F32)], reverse=True)


def _rope_tables(positions):
    S = positions.shape[0]
    inv = ROPE_THETA ** (-jnp.arange(0, 16, 2, dtype=F32) / 16)
    ang = positions.astype(F32)[:, None] * inv
    cos, sin = jnp.cos(ang), jnp.sin(ang)
    c64 = jnp.concatenate([cos, cos, jnp.ones((S, 48), F32)], axis=1)
    s64 = jnp.concatenate([-sin, sin, jnp.zeros((S, 48), F32)], axis=1)
    return jnp.concatenate([c64, c64], axis=1), jnp.concatenate([s64, s64], axis=1)


def _partner(x):
    lane = lax.broadcasted_iota(jnp.int32, x.shape, 1) % 64
    return jnp.where(lane < 8, pltpu.roll(x, LANES - 8, axis=1), jnp.where(lane < 16, pltpu.roll(x, 8, axis=1), 0.0))


def _rope(x, c, s):
    return x * c + _partner(x) * s


def _rope_t(x, c, s):
    return x * c + _partner(x * s)


def _rows_of(r, n, d):
    return pl.ds(r, n, stride=d) if d > 1 else pl.ds(0, n)


def _head_masks(shape):
    lane = lax.broadcasted_iota(jnp.int32, shape, 1) // 64
    return [lane == h for h in range(4)]


def _only(mask, x):
    return jnp.where(mask, x, jnp.zeros_like(x))


def _rope_perm(u, ctab, stab, ts=256):
    S = u.shape[0]
    nch = ATT_W // LANES

    def body(*refs):
        chunks, (c_ref, s_ref), outs = refs[:3 * nch], refs[3 * nch:3 * nch + 2], refs[3 * nch + 2:]
        for g, d in enumerate(DILS):
            n = ts // d
            for r in range(d):
                rows = _rows_of(r, n, d)
                c, s = c_ref[rows, :], s_ref[rows, :]
                for which in range(3):
                    parts = [chunks[which * nch + j][rows, :] for j in (2 * g, 2 * g + 1)]
                    if which < 2:
                        parts = [_rope(x, c, s) for x in parts]
                    outs[which * 3 + g][r] = jnp.concatenate(parts, axis=1).astype(MXU)

    base = (IN_W - 3 * ATT_W) // LANES
    in_specs = [pl.BlockSpec((ts, LANES), lambda i, cb=base + k: (i, cb)) for k in range(3 * nch)]
    in_specs += [pl.BlockSpec((ts, LANES), lambda i: (i, 0))] * 2
    out_specs = [pl.BlockSpec((d, ts // d, ATT_O), lambda i: (0, i, 0)) for _ in range(3) for d in DILS]
    out_shape = [jax.ShapeDtypeStruct((d, S // d, ATT_O), MXU) for _ in range(3) for d in DILS]
    res = pl.pallas_call(body, grid=(S // ts,), in_specs=in_specs, out_specs=out_specs, out_shape=out_shape,
                         compiler_params=_params(("arbitrary",), 32), name="rope_perm")(*([u] * (3 * nch)), ctab, stab)
    return [[res[which * 3 + g].reshape(S, ATT_O) for g in range(3)] for which in range(3)]


def _rope_unperm_bwd(dqkv, du, ctab, stab, ts=256):
    S = dqkv[0][0].shape[0]
    nch = ATT_W // LANES

    def body(*refs):
        ins, (c_ref, s_ref, _, o_ref, scr) = refs[:9], refs[9:]
        for g, d in enumerate(DILS):
            n = ts // d
            for r in range(d):
                rows = _rows_of(r, n, d)
                c, s = c_ref[rows, :], s_ref[rows, :]
                for which in range(3):
                    v = ins[which * 3 + g][r]
                    for half in range(2):
                        x = v[:, half * LANES:(half + 1) * LANES]
                        scr.at[which * nch + 2 * g + half][rows, :] = _rope_t(x, c, s) if which < 2 else x
        for j in range(3 * nch):
            o_ref[:, j * LANES:(j + 1) * LANES] = scr[j].astype(o_ref.dtype)

    in_specs = [pl.BlockSpec((d, ts // d, ATT_O), lambda i: (0, i, 0)) for _ in range(3) for d in DILS]
    in_specs += [pl.BlockSpec((ts, LANES), lambda i: (i, 0))] * 2 + [pl.BlockSpec(memory_space=pl.ANY)]
    args = [dqkv[which][g].reshape(d, S // d, ATT_O) for which in range(3) for g, d in enumerate(DILS)]
    last = (IN_W - 3 * ATT_W) // (3 * ATT_W)
    return pl.pallas_call(body, grid=(S // ts,), in_specs=in_specs, out_specs=pl.BlockSpec((ts, 3 * ATT_W), lambda i: (i, last)),
                          out_shape=jax.ShapeDtypeStruct((S, IN_W), MXU), scratch_shapes=[pltpu.VMEM((3 * nch, ts, LANES), F32)],
                          input_output_aliases={len(in_specs) - 1: 0},
                          compiler_params=_params(("arbitrary",), 32), name="rope_unperm_bwd")(*args, ctab, stab, du)


def _band_mask_keys(has_prev):
    r = lax.broadcasted_iota(jnp.int32, (QB, 2 * QB), 0)
    c = lax.broadcasted_iota(jnp.int32, (QB, 2 * QB), 1)
    return ((c < QB) & (c >= r) & has_prev) | ((c >= QB) & (c - QB <= r))


def _band_mask_queries(has_next):
    r = lax.broadcasted_iota(jnp.int32, (2 * QB, QB), 0)
    c = lax.broadcasted_iota(jnp.int32, (2 * QB, QB), 1)
    return ((r < QB) & (c <= r)) | ((r >= QB) & (c >= r - QB) & has_next)


ASUB = 4
_BIG = pl.BlockSpec((ASUB * QB, ATT_O), lambda b: (b, 0))
_PREV = pl.BlockSpec((QB, ATT_O), lambda b: (jnp.maximum(b * ASUB - 1, 0), 0))


def _sub(ref, j):
    return ref[j * QB:(j + 1) * QB]


def _attn_fwd(g, q, k, v):
    S = q.shape[0]
    nb = S // QB
    nblk = nb // DILS[g]

    def body(q_ref, kc_ref, kp_ref, vc_ref, vp_ref, o_ref, m_ref, l_ref):
        hm_kv, hm_o = _head_masks((2 * QB, ATT_O)), _head_masks((QB, ATT_O))
        for j in range(ASUB):
            ok = _band_mask_keys(((pl.program_id(0) * ASUB + j) & (nblk - 1)) > 0)
            k2 = jnp.concatenate([kp_ref[...] if j == 0 else _sub(kc_ref, j - 1), _sub(kc_ref, j)], axis=0)
            v2 = jnp.concatenate([vp_ref[...] if j == 0 else _sub(vc_ref, j - 1), _sub(vc_ref, j)], axis=0)
            qv = _sub(q_ref, j)
            o_acc = jnp.zeros((QB, ATT_O), F32)
            m_acc = jnp.zeros((QB, ATT_O), F32)
            l_acc = jnp.zeros((QB, ATT_O), F32)
            for h in range(4):
                s = jnp.where(ok, _dot_nt(qv, _only(hm_kv[h], k2)) * ATT_SCALE, NEG)
                m = jnp.max(s, axis=1, keepdims=True)
                p = jnp.exp(s - m)
                o_acc = o_acc + _dot(p.astype(MXU), _only(hm_kv[h], v2))
                m_acc = jnp.where(hm_o[h], m, m_acc)
                l_acc = jnp.where(hm_o[h], jnp.sum(p, axis=1, keepdims=True), l_acc)
            o_ref[j * QB:(j + 1) * QB] = o_acc
            m_ref[j * QB:(j + 1) * QB] = m_acc
            l_ref[j * QB:(j + 1) * QB] = l_acc

    shp = jax.ShapeDtypeStruct((S, ATT_O), F32)
    return pl.pallas_call(body, grid=(nb // ASUB,), in_specs=[_BIG, _BIG, _PREV, _BIG, _PREV],
                          out_specs=[_BIG] * 3, out_shape=[shp, shp, shp], compiler_params=_params(("arbitrary",), 32),
                          name=f"attn_fwd_{g}")(q, k, k, v, v)


def _natural(ref, d, scr, ts):
    if d == 1:
        return ref[0]
    n = ts // d
    for r in range(d):
        v = ref[r]
        scr.at[0][pl.ds(r, n, stride=d), :] = v[:, 0:LANES]
        scr.at[1][pl.ds(r, n, stride=d), :] = v[:, LANES:2 * LANES]
    return jnp.concatenate([scr[0], scr[1]], axis=1)


def _attn_combine(oml, ts=256):
    S = oml[0][0].shape[0]

    def body(*refs):
        ins, (att_ref, out_ref, lse_ref, scr) = refs[:9], refs[9:]
        o, m, l = [[_natural(ins[3 * g + k], d, scr, ts) for g, d in enumerate(DILS)] for k in range(3)]
        mx = jnp.maximum(jnp.maximum(m[0], m[1]), m[2])
        w = [jnp.exp(m[g] - mx) for g in range(3)]
        den = w[0] * l[0] + w[1] * l[1] + w[2] * l[2]
        out = (w[0] * o[0] + w[1] * o[1] + w[2] * o[2]) / den
        out_ref[...] = out
        att_ref[...] = out.astype(MXU)
        lse_ref[...] = mx + jnp.log(den)

    in_specs = [pl.BlockSpec((d, ts // d, ATT_O), lambda i: (0, i, 0)) for d in DILS for _ in range(3)]
    args = [a.reshape(d, S // d, ATT_O) for d, grp in zip(DILS, oml) for a in grp]
    blk = pl.BlockSpec((ts, ATT_O), lambda i: (i, 0))
    return pl.pallas_call(body, grid=(S // ts,), in_specs=in_specs, out_specs=[blk, blk, blk],
                          out_shape=[jax.ShapeDtypeStruct((S, ATT_O), MXU), jax.ShapeDtypeStruct((S, ATT_O), F32),
                                     jax.ShapeDtypeStruct((S, ATT_O), F32)],
                          scratch_shapes=[pltpu.VMEM((2, ts, LANES), F32)], compiler_params=_params(("arbitrary",), 32),
                          name="attn_combine")(*args)


def _attn_bwd_prep(datt, o, lse, ts=256):
    S = datt.shape[0]

    def body(da0, da1, o_ref, l0, l1, *rest):
        outs, dl = rest[:9], rest[9]
        prod = jnp.concatenate([da0[...], da1[...]], axis=1) * o_ref[...]
        delta = jnp.zeros((ts, ATT_O), F32)
        for hm in _head_masks((ts, ATT_O)):
            delta = jnp.where(hm, jnp.sum(_only(hm, prod), axis=1, keepdims=True), delta)
        dl[0] = delta[:, 0:LANES]
        dl[1] = delta[:, LANES:2 * LANES]
        for g, d in enumerate(DILS):
            n = ts // d
            for r in range(d):
                rows = _rows_of(r, n, d)
                outs[g][r] = jnp.concatenate([da0[rows, :], da1[rows, :]], axis=1).astype(MXU)
                outs[3 + g][r] = jnp.concatenate([dl.at[0][rows, :], dl.at[1][rows, :]], axis=1)
                outs[6 + g][r] = jnp.concatenate([l0[rows, :], l1[rows, :]], axis=1)

    half = lambda j: pl.BlockSpec((ts, LANES), lambda i: (i, j))
    out_specs = [pl.BlockSpec((d, ts // d, ATT_O), lambda i: (0, i, 0)) for _ in range(3) for d in DILS]
    out_shape = [jax.ShapeDtypeStruct((d, S // d, ATT_O), dt) for dt in (MXU, F32, F32) for d in DILS]
    res = pl.pallas_call(body, grid=(S // ts,), in_specs=[half(0), half(1), pl.BlockSpec((ts, ATT_O), lambda i: (i, 0)), half(0), half(1)],
                         out_specs=out_specs, out_shape=out_shape, scratch_shapes=[pltpu.VMEM((2, ts, LANES), F32)],
                         compiler_params=_params(("arbitrary",), 32), name="attn_bwd_prep")(datt, datt, o, lse, lse)
    return [[res[k * 3 + g].reshape(S, ATT_O) for g in range(3)] for k in range(3)]


def _head_col(x, h):
    return x[:, h * 64:h * 64 + 1]


def _attn_bwd(g, q, k, v, do, delta, lse):
    S = q.shape[0]
    nb = S // QB
    nblk = nb // DILS[g]

    def body(k_ref, v_ref, qc_ref, qn_ref, doc_ref, don_ref, dlc_ref, dln_ref, lc_ref, ln_ref, dq_ref, dk_ref, dv_ref, dq_scr):
        hms, hmk = _head_masks((2 * QB, ATT_O)), _head_masks((QB, ATT_O))
        first = pl.program_id(0) == 0

        @pl.when(first)
        def _():
            dq_scr[0:QB] = jnp.zeros((QB, ATT_O), F32)

        @pl.when(jnp.logical_not(first))
        def _():
            dq_scr[0:QB] = dq_scr[ASUB * QB:(ASUB + 1) * QB]

        dq_scr[QB:(ASUB + 1) * QB] = jnp.zeros((ASUB * QB, ATT_O), F32)

        def both(cur_ref, nxt_ref, j):
            return jnp.concatenate([_sub(cur_ref, j), nxt_ref[...] if j == ASUB - 1 else _sub(cur_ref, j + 1)], axis=0)

        for j in range(ASUB):
            ok = _band_mask_queries(((pl.program_id(0) * ASUB + j + 1) & (nblk - 1)) > 0)
            q2, do2, dl2, lse2 = both(qc_ref, qn_ref, j), both(doc_ref, don_ref, j), both(dlc_ref, dln_ref, j), both(lc_ref, ln_ref, j)
            kv, vv = _sub(k_ref, j), _sub(v_ref, j)
            dk = jnp.zeros((QB, ATT_O), F32)
            dv = jnp.zeros((QB, ATT_O), F32)
            dq2 = jnp.zeros((2 * QB, ATT_O), F32)
            for h, hm in enumerate(hms):
                qh, doh = _only(hm, q2), _only(hm, do2)
                p = jnp.where(ok, jnp.exp(_dot_nt(qh, kv) * ATT_SCALE - _head_col(lse2, h)), 0.0)
                ds = (p * (_dot_nt(doh, vv) - _head_col(dl2, h))).astype(MXU)
                dv = dv + _dot_tn(p.astype(MXU), doh)
                dk = dk + _dot_tn(ds, qh)
                dq2 = dq2 + _dot(ds, _only(hmk[h], kv))
            dk_ref[j * QB:(j + 1) * QB] = dk * ATT_SCALE
            dv_ref[j * QB:(j + 1) * QB] = dv
            dq_scr[j * QB:(j + 2) * QB] += dq2
        dq_ref[...] = dq_scr[0:ASUB * QB] * ATT_SCALE

    nxt = pl.BlockSpec((QB, ATT_O), lambda b: (jnp.minimum((b + 1) * ASUB, nb - 1), 0))
    shp = jax.ShapeDtypeStruct((S, ATT_O), F32)
    return pl.pallas_call(body, grid=(nb // ASUB,), in_specs=[_BIG, _BIG, _BIG, nxt, _BIG, nxt, _BIG, nxt, _BIG, nxt], out_specs=[_BIG] * 3,
                          out_shape=[shp, shp, shp], scratch_shapes=[pltpu.VMEM(((ASUB + 1) * QB, ATT_O), F32)],
                          compiler_params=_params(("arbitrary",), 32), name=f"attn_bwd_{g}")(k, v, q, q, do, do, delta, delta, lse, lse)


def _merge_fwd(x0, u, a2, yb, att, wa, wb, wc, w_out, g_post, ts=256):
    S = x0.shape[0]

    def body(i, g, x_ref, gate_ref, a2_ref, yb_ref, att_ref, wa_ref, wb_ref, wc_ref, wo_ref, gp_ref, mg_ref, y_ref, xo_ref):
        merged = jax.nn.sigmoid(gate_ref[:, 0:D]) * _dot_nt(a2_ref[...], wa_ref[...])
        merged = merged + jax.nn.sigmoid(gate_ref[:, D:2 * D]) * _dot_nt(yb_ref[...], wb_ref[...])
        merged = merged + jax.nn.sigmoid(gate_ref[:, 2 * D:3 * D]) * _dot_nt(att_ref[...], wc_ref[...])
        mb = merged.astype(MXU)
        mg_ref[...] = mb
        y = _dot(mb, wo_ref[...])
        y_ref[...] = y
        xo_ref[...] = x_ref[...] + _rms(y, gp_ref[...])[0]

    ins = [("t", x0, D, 0), ("t", u, GATE_W, 0), ("t", a2, POOLW, 0), ("t", yb, POOLW, 0), ("t", att, ATT_O, 0),
           ("w", wa), ("w", wb), ("w", wc), ("w", w_out), ("w", g_post)]
    return _rows_call("merge_fwd", body, S, ts, ins, [("t", D, MXU), ("t", D, F32), ("t", D, F32)])


def _merge_bwd(dx, y1, u, a2, yb, att, wa, wb, wc, w_out, g_post, ts=256):
    S = dx.shape[0]

    def body(i, g, dx_ref, y_ref, gate_ref, a2_ref, yb_ref, att_ref, wa_ref, wb_ref, wc_ref, wo_ref, gp_ref,
             dy_ref, dgate_ref, dbra_ref, dbrb_ref, dbrc_ref, da2_ref, dyb_ref, datt_ref, dgp_ref):
        dxv, y = dx_ref[...], y_ref[...]
        dy, r = _rms_bwd(dxv * gp_ref[...], y)
        _acc(dgp_ref, g, jnp.sum(dxv * (y * r), axis=0, keepdims=True))
        dyb16 = dy.astype(MXU)
        dy_ref[...] = dyb16
        dm = _dot_nt(dyb16, wo_ref[...])
        for n, (src, w_ref, dbr_ref, din_ref) in enumerate(((a2_ref, wa_ref, dbra_ref, da2_ref), (yb_ref, wb_ref, dbrb_ref, dyb_ref),
                                                           (att_ref, wc_ref, dbrc_ref, datt_ref))):
            gt = jax.nn.sigmoid(gate_ref[:, n * D:(n + 1) * D])
            br = _dot_nt(src[...], w_ref[...])
            dgate_ref[:, n * D:(n + 1) * D] = (dm * br * gt * (1.0 - gt)).astype(dgate_ref.dtype)
            dbr = (dm * gt).astype(MXU)
            dbr_ref[...] = dbr
            din_ref[...] = _dot(dbr, w_ref[...])

    ins = [("t", dx, D, 0), ("t", y1, D, 0), ("t", u, GATE_W, 0), ("t", a2, POOLW, 0), ("t", yb, POOLW, 0), ("t", att, ATT_O, 0),
           ("w", wa), ("w", wb), ("w", wc), ("w", w_out), ("w", g_post)]
    outs = [("t", D, MXU), ("c", IN_W, GATE_W, 0, MXU), ("t", D, MXU), ("t", D, MXU), ("t", D, MXU), ("t", POOLW, F32), ("t", POOLW, F32),
            ("t", ATT_O, F32), ("a", (1, D), F32)]
    return _rows_call("merge_bwd", body, S, ts, ins, outs)


def _prenorm_bwd(name, dx_res, du, wt, x, g_pre, ts=256, lead=0):
    S = x.shape[0]
    N = du.shape[1]

    def body(i, g, dx_ref, du_ref, wt_ref, x_ref, g_ref, o_ref, dg_ref):
        if lead:
            dhv = _dot(du_ref[:, 0:lead], wt_ref[N - lead:N, :]) + _dot(du_ref[:, lead:N], wt_ref[0:N - lead, :])
        else:
            dhv = _dot(du_ref[...], wt_ref[...])
        xv = x_ref[...]
        dxn, r = _rms_bwd(dhv * g_ref[...], xv)
        o_ref[...] = dx_ref[...] + dxn
        _acc(dg_ref, g, jnp.sum(dhv * (xv * r), axis=0, keepdims=True))

    ins = [("t", dx_res, D, 0), ("t", du, N, 0), ("w", wt), ("t", x, D, 0), ("w", g_pre)]
    return _rows_call(name, body, S, ts, ins, [("t", D, F32), ("a", (1, D), F32)], vmem_mb=52)


def _mem_heads(qm, kv_ref):
    out = []
    for h in range(4):
        q = qm[:, h * 128:(h + 1) * 128].astype(MXU)
        k = kv_ref[:, h * 128:(h + 1) * 128]
        v = kv_ref[:, MEM_W + h * 128:MEM_W + (h + 1) * 128]
        sc = _dot_nt(q, k) * MEM_SCALE
        e = jnp.exp(sc - jnp.max(sc, axis=1, keepdims=True))
        out.append((e / jnp.sum(e, axis=1, keepdims=True), q, k, v))
    return out


def _mem_fwd(x1, kv, g_pre, w_mq, w_mo, g_post, ts=256):
    S = x1.shape[0]

    def body(i, g, x_ref, kv_ref, gq_ref, wq_ref, wo_ref, gp_ref, om_ref, h_ref, y_ref, xo_ref):
        x = x_ref[...]
        hb = _rms(x, gq_ref[...])[0].astype(MXU)
        h_ref[...] = hb
        qm = _dot(hb, wq_ref[...])
        om = jnp.concatenate([_dot(p.astype(MXU), v) for p, _, _, v in _mem_heads(qm, kv_ref)], axis=1).astype(MXU)
        om_ref[...] = om
        y = _dot_nt(om, wo_ref[...])
        y_ref[...] = y
        xo_ref[...] = x + _rms(y, gp_ref[...])[0]

    ins = [("t", x1, D, 0), ("w", kv), ("w", g_pre), ("w", w_mq), ("w", w_mo), ("w", g_post)]
    return _rows_call("mem_fwd", body, S, ts, ins, [("t", MEM_W, MXU), ("t", D, MXU), ("t", D, F32), ("t", D, F32)])


def _mem_bwd(dx2, ym, x1, kv, g_pre, w_mq, w_mo, g_post, ts=256):
    S = x1.shape[0]

    def body(i, g, dx_ref, y_ref, x_ref, kv_ref, gq_ref, wq_ref, wo_ref, gp_ref, dy_ref, dq_ref, dxo_ref, dgp_ref, dgq_ref, dkv_ref):
        dxv, y, x = dx_ref[...], y_ref[...], x_ref[...]
        dy, r = _rms_bwd(dxv * gp_ref[...], y)
        _acc(dgp_ref, g, jnp.sum(dxv * (y * r), axis=0, keepdims=True))
        dyb = dy.astype(MXU)
        dy_ref[...] = dyb
        dom = _dot(dyb, wo_ref[...])
        h, r1 = _rms(x, gq_ref[...])
        qm = _dot(h.astype(MXU), wq_ref[...])
        dqs = []

        @pl.when(g == 0)
        def _():
            dkv_ref[...] = jnp.zeros_like(dkv_ref)

        for hh, (p, q, k, v) in enumerate(_mem_heads(qm, kv_ref)):
            doh = dom[:, hh * 128:(hh + 1) * 128].astype(MXU)
            dp = _dot_nt(doh, v)
            dsc = (p * (dp - jnp.sum(dp * p, axis=1, keepdims=True)) * MEM_SCALE).astype(MXU)
            dqs.append(_dot(dsc, k))
            dkv_ref[:, hh * 128:(hh + 1) * 128] += _dot_tn(dsc, q)
            dkv_ref[:, MEM_W + hh * 128:MEM_W + (hh + 1) * 128] += _dot_tn(p.astype(MXU), doh)
        dq = jnp.concatenate(dqs, axis=1).astype(MXU)
        dq_ref[...] = dq
        dh = _dot_nt(dq, wq_ref[...])
        _acc(dgq_ref, g, jnp.sum(dh * (x * r1), axis=0, keepdims=True))
        dxo_ref[...] = dxv + _rms_bwd(dh * gq_ref[...], x)[0]

    ins = [("t", dx2, D, 0), ("t", ym, D, 0), ("t", x1, D, 0), ("w", kv), ("w", g_pre), ("w", w_mq), ("w", w_mo), ("w", g_post)]
    outs = [("t", D, MXU), ("t", MEM_W, MXU), ("t", D, F32), ("a", (1, D), F32), ("a", (1, D), F32), ("a", (256, D), F32)]
    return _rows_call("mem_bwd", body, S, ts, ins, outs)


def _gain_grad(name, dn, x):
    n = x.shape[0]

    def body(i, g, dn_ref, x_ref, o_ref):
        xv = x_ref[...]
        r = lax.rsqrt(jnp.mean(xv * xv, axis=-1, keepdims=True) + EPS)
        o_ref[...] = jnp.sum(dn_ref[...] * (xv * r), axis=0, keepdims=True)

    return _rows_call(name, body, n, n, [("t", dn, D, 0), ("t", x, D, 0)], [("a", (1, D), F32)])[0]


def _ffn_fwd(x2, u3, conv_f, w_down, g_post, ts=256):
    S = x2.shape[0]

    def body(i, g, x_ref, ua_ref, ub_ref, cw_ref, wd_ref, gp_ref, act_ref, y_ref, xo_ref, cu):
        @pl.when(g == 0)
        def _():
            cu[...] = jnp.zeros_like(cu)

        ua = ua_ref[...]
        c, _, _ = _conv3(ua, cu[...], cw_ref[...])
        act = (c * jax.nn.sigmoid(c) * ub_ref[...]).astype(MXU)
        act_ref[...] = act
        y = _dot(act, wd_ref[...])
        y_ref[...] = y
        xo_ref[...] = x_ref[...] + _rms(y, gp_ref[...])[0]
        cu[...] = ua[ts - 8:]

    ins = [("t", x2, D, 0), ("t", u3, D_FF, 0), ("t", u3, D_FF, 1), ("w", conv_f), ("w", w_down), ("w", g_post)]
    return _rows_call("ffn_fwd", body, S, ts, ins, [("t", D_FF, MXU), ("t", D, F32), ("t", D, F32)],
                      scratch=[pltpu.VMEM((8, D_FF), F32)], vmem_mb=56)


def _ffn_bwd(dx3, y3, u3, conv_f, w_down, g_post, ts=128):
    S = dx3.shape[0]

    def body(i, g, dx_ref, y_ref, ua_ref, ub_ref, uap_ref, cw_ref, wd_ref, gp_ref, dy_ref, du_ref, dgp_ref, dcw_ref, cdc):
        @pl.when(g == 0)
        def _():
            cdc[...] = jnp.zeros_like(cdc)

        dxv, y = dx_ref[...], y_ref[...]
        dy, r = _rms_bwd(dxv * gp_ref[...], y)
        _acc(dgp_ref, g, jnp.sum(dxv * (y * r), axis=0, keepdims=True))
        dyb = dy.astype(MXU)
        dy_ref[...] = dyb
        dact = _dot_nt(dyb, wd_ref[...])
        ua, w = ua_ref[...], cw_ref[...]
        c, u1, u2 = _conv3(ua, uap_ref[...] * (i > 0).astype(F32), w)
        sg = jax.nn.sigmoid(c)
        du_ref[:, D_FF:2 * D_FF] = (dact * (c * sg)).astype(du_ref.dtype)
        dc = dact * ub_ref[...] * (sg * (1.0 + c * (1.0 - sg)))
        du_ref[:, 0:D_FF] = _conv3_t(dc, cdc[...], w).astype(du_ref.dtype)
        dw = jnp.concatenate([jnp.sum(dc * u2, axis=0, keepdims=True), jnp.sum(dc * u1, axis=0, keepdims=True),
                              jnp.sum(dc * ua, axis=0, keepdims=True)], axis=0)
        _acc(dcw_ref, g, dw)
        cdc[...] = dc[:8]

    ins = [("t", dx3, D, 0), ("t", y3, D, 0), ("t", u3, D_FF, 0), ("t", u3, D_FF, 1), ("h", u3, 8, D_FF, 0), ("w", conv_f),
           ("w", w_down), ("w", g_post)]
    outs = [("t", D, MXU), ("t", 2 * D_FF, MXU), ("a", (1, D), F32), ("a", (3, D_FF), F32)]
    return _rows_call("ffn_bwd", body, S, ts, ins, outs, scratch=[pltpu.VMEM((8, D_FF), F32)], reverse=True, vmem_mb=56)


def _loss_head(x, target, ts=512):
    S = x.shape[0]

    def body(i, g, x_ref, t_ref, dx_ref, acc_ref):
        diff = x_ref[...] - t_ref[...]
        dx_ref[...] = diff * (1.0 / D)
        col = jnp.sum(diff * diff, axis=0, keepdims=True)
        part = col[:, 0:LANES]
        for j in range(1, D // LANES):
            part = part + col[:, j * LANES:(j + 1) * LANES]
        row = lax.broadcasted_iota(jnp.int32, (8, LANES), 0)
        _acc(acc_ref, g, jnp.where(row == 0, jnp.broadcast_to(part, (8, LANES)), 0.0))

    return _rows_call("loss_head", body, S, ts, [("t", x, D, 0), ("t", target, D, 0)], [("t", D, F32), ("a", (8, LANES), F32)])


_OPERAND_NAME = dict(w_in='w_in', w_branch_a='wa', w_branch_b='wb', w_branch_c='wc', w_out='w_out', w_mq='w_mq', w_mkv='w_mkv',
                     w_mo='w_mo', w_up='w_up', w_down='w_down')


def _big_operands(big):
    return {_OPERAND_NAME[n]: a for n, a in big.items()}


def _layer_weights(big, small, l):
    pool_w = small['pool_w'][l].astype(MXU)
    wblk = jnp.zeros((POOLW, POOLW), MXU)
    for g in range(4):
        wblk = lax.dynamic_update_slice(wblk, pool_w[g], (g * 96, g * 96))
    vec = lambda n: small[n][l].reshape(1, -1)
    return dict(
        _big_operands(big),
        wblk=wblk, pool_scale=vec('pool_scale'), conv_b=small['conv_b_w'][l], conv_f=small['conv_ffn_w'][l],
        g_mix_pre=vec('norm_mix_pre'), g_mix_post=vec('norm_mix_post'), g_mem_pre=vec('norm_mem_pre'),
        g_mem_post=vec('norm_mem_post'), g_memkv=vec('norm_memkv'), g_ffn_pre=vec('norm_ffn_pre'), g_ffn_post=vec('norm_ffn_post'))


def _layer_fwd(x0, mem, W, ctab, stab):
    sv = _layer_fwd_mix(x0, W, ctab, stab)
    return _layer_fwd_late(mem, W, sv), sv


def _layer_fwd_mix(x0, W, ctab, stab):
    sv = dict(x0=x0)
    sv['u'], sv['h1'] = _norm_mm("in_proj", x0, W['g_mix_pre'], W['w_in'], ts=1024, tn=IN_TILE, wt=True, rot=IN_ROT)
    sv['a2'], sv['yb'] = _poolconv_fwd(sv['u'], W['wblk'], W['pool_scale'], W['conv_b'])
    sv['qkv'] = q3, k3, v3 = _rope_perm(sv['u'], ctab, stab)
    sv['att'], sv['o'], sv['lse'] = _attn_combine([_attn_fwd(g, q3[g], k3[g], v3[g]) for g in range(3)])
    sv['merged'], sv['y1'], sv['x1'] = _merge_fwd(x0, sv['u'], sv['a2'], sv['yb'], sv['att'], W['wa'], W['wb'], W['wc'],
                                                  W['w_out'], W['g_mix_post'])
    return sv


def _layer_fwd_late(mem, W, sv):
    sv['kv'], sv['memn'] = _norm_mm("mem_kv", mem, W['g_memkv'], W['w_mkv'], ts=256, tn=D, out_dtype=MXU)
    sv['om'], sv['h2'], sv['ym'], sv['x2'] = _mem_fwd(sv['x1'], sv['kv'], W['g_mem_pre'], W['w_mq'], W['w_mo'], W['g_mem_post'])
    sv['u3'], sv['h3'] = _norm_mm("up_proj", sv['x2'], W['g_ffn_pre'], W['w_up'], ts=1024, tn=1408, wt=True)
    sv['act'], sv['y3'], x3 = _ffn_fwd(sv['x2'], sv['u3'], W['conv_f'], W['w_down'], W['g_ffn_post'])
    return x3


def _layer_bwd(dx3, mem, W, sv, ctab, stab):
    dx1, g = _layer_bwd_late(dx3, mem, W, sv)
    dx0, g_mix = _layer_bwd_mix(dx1, W, sv, ctab, stab)
    return dx0, {**g, **g_mix}


def _layer_bwd_late(dx3, mem, W, sv):
    g = {}
    dy3, du3, g['norm_ffn_post'], g['conv_ffn_w'] = _ffn_bwd(dx3, sv['y3'], sv['u3'], W['conv_f'], W['w_down'], W['g_ffn_post'])
    g['w_down'] = _mm_tn("dw_down", sv['act'], dy3, cap_k=256)
    g['w_up'] = _mm_tn("dw_up", du3, sv['h3'])
    dx2, g['norm_ffn_pre'] = _prenorm_bwd("ffn_pre_bwd", dx3, du3, W['w_up'], sv['x2'], W['g_ffn_pre'])
    dym, dqm, dx1, g['norm_mem_post'], g['norm_mem_pre'], dkv = _mem_bwd(dx2, sv['ym'], sv['x1'], sv['kv'], W['g_mem_pre'],
                                                                       W['w_mq'], W['w_mo'], W['g_mem_post'])
    g['w_mo'] = _mm_tn("dw_mo", dym, sv['om'])
    g['w_mq'] = _mm_tn("dw_mq", sv['h2'], dqm)
    dkvb = dkv.astype(MXU)
    g['w_mkv'] = _mm_tn("dw_mkv", sv['memn'], dkvb)
    g['norm_memkv'] = _gain_grad("memkv_gain", _mm_nt("d_memn", dkvb, W['w_mkv'], ts=256, tn=512), mem)
    return dx1, g


def _layer_bwd_mix(dx1, W, sv, ctab, stab):
    du, g = _layer_bwd_mixers(dx1, W, sv, ctab, stab)
    g['w_in'] = _dw_in(du, sv)
    dx0, g['norm_mix_pre'] = _mix_pre_bwd(dx1, du, W, sv)
    return dx0, g


def _dw_in(du, sv):
    return _mm_tn("dw_in", du, sv['h1'], cap_k=IN_TILE, rot=IN_ROT)


def _mix_pre_bwd(dx1, du, W, sv):
    return _prenorm_bwd("mix_pre_bwd", dx1, du, W['w_in'], sv['x0'], W['g_mix_pre'], lead=GATE_W)


def _layer_bwd_mixers(dx1, W, sv, ctab, stab):
    g = {}
    dy1, du, dbra, dbrb, dbrc, da2, dyb, datt, g['norm_mix_post'] = _merge_bwd(
        dx1, sv['y1'], sv['u'], sv['a2'], sv['yb'], sv['att'], W['wa'], W['wb'], W['wc'], W['w_out'], W['g_mix_post'])
    g['w_out'] = _mm_tn("dw_out", sv['merged'], dy1)
    g['w_branch_a'] = _mm_tn("dw_a", dbra, sv['a2'])
    g['w_branch_b'] = _mm_tn("dw_b", dbrb, sv['yb'])
    g['w_branch_c'] = _mm_tn("dw_c", dbrc, sv['att'])
    du, g['pool_scale'], dwblk, g['conv_b_w'] = _poolconv_bwd(sv['u'], da2, dyb, du, W['wblk'], W['pool_scale'], W['conv_b'])
    g['pool_w'] = jnp.stack([dwblk[k * 96:(k + 1) * 96, k * 96:(k + 1) * 96] for k in range(4)])
    q3, k3, v3 = sv['qkv']
    do3, dl3, lse3 = _attn_bwd_prep(datt, sv['o'], sv['lse'])
    dqkv3 = [_attn_bwd(i, q3[i], k3[i], v3[i], do3[i], dl3[i], lse3[i]) for i in range(3)]
    du = _rope_unperm_bwd([[t[which] for t in dqkv3] for which in range(3)], du, ctab, stab)
    return du, g


def _local_step(x, mem, positions, target, big, small):
    ctab, stab = _rope_tables(positions)
    Ws = [_layer_weights(big[l], small, l) for l in range(DEPTH)]
    saved = []
    for l in range(DEPTH):
        x, sv = _layer_fwd(x, mem, Ws[l], ctab, stab)
        saved.append(sv)
    dx, acc = _loss_head(x, target)
    loss = jnp.sum(acc) * (0.5 / D)
    grads = [None] * DEPTH
    for l in reversed(range(DEPTH)):
        dx, grads[l] = _layer_bwd(dx, mem, Ws[l], saved[l], ctab, stab)
    return loss, dx, grads


_HBM = pl.BlockSpec(memory_space=pl.ANY)
MESH_ID = pl.DeviceIdType.MESH


def _all_gather(name, xs):
    n = len(xs)

    def body(*refs):
        x_refs, out_refs = refs[:n], refs[n:2 * n]
        send_sems, recv_sems, local_sems = refs[2 * n:]
        x, y, c = lax.axis_index("x"), lax.axis_index("y"), lax.axis_index("c")
        me, sibling = (x, y, c), (x, y, 1 - c)
        chips = [(1 - x, y), (x, 1 - y), (1 - x, 1 - y)]

        def slot(a, p):
            return out_refs[a].at[4 * p[0] + 2 * p[1] + p[2]]

        def copy(a, k, block, to, src=None):
            return pltpu.make_async_remote_copy(src_ref=slot(a, block) if src is None else src, dst_ref=slot(a, block),
                                                send_sem=send_sems.at[a, k], recv_sem=recv_sems.at[a, k], device_id=to,
                                                device_id_type=MESH_ID)

        started = []
        for a in range(n):
            mine = pltpu.make_async_copy(x_refs[a], slot(a, me), local_sems.at[a])
            mine.start()
            started.append(mine)
        first = []
        for a in range(n):
            first.append(copy(a, 0, me, sibling, src=x_refs[a]))
            first += [copy(a, 1 + j, me, (*chip, c), src=x_refs[a]) for j, chip in enumerate(chips)]
        for cp in first:
            cp.start()
        passed = []
        for j, chip in enumerate(chips):
            for a in range(n):
                copy(a, 1 + j, (*chip, c), me).wait_recv()
                fw = copy(a, 4 + j, (*chip, c), sibling)
                fw.start()
                passed.append(fw)
        for a in range(n):
            copy(a, 0, sibling, me).wait_recv()
            for j, chip in enumerate(chips):
                copy(a, 4 + j, (*chip, 1 - c), me).wait_recv()
        for cp in first + passed:
            cp.wait_send()
        for mine in started:
            mine.wait()

    return pl.pallas_call(
        body, out_shape=[jax.ShapeDtypeStruct((N_DEV,) + x.shape, x.dtype) for x in xs], in_specs=[_HBM] * n, out_specs=[_HBM] * n,
        scratch_shapes=[pltpu.SemaphoreType.DMA((n, 7)), pltpu.SemaphoreType.DMA((n, 7)), pltpu.SemaphoreType.DMA((n,))],
        name=name)(*xs)


def _exchange(name, gs):
    n = len(gs)

    def body(*refs):
        g_refs, out_refs = refs[:n], refs[n:2 * n]
        send_sems, recv_sems, local_sems = refs[2 * n:]
        x, y, c = lax.axis_index("x"), lax.axis_index("y"), lax.axis_index("c")
        me = 4 * x + 2 * y + c
        copies = []
        for a in range(n):
            mine = pltpu.make_async_copy(g_refs[a].at[me], out_refs[a].at[me], local_sems.at[a])
            mine.start()
            copies.append(mine)
        for r in range(1, N_DEV):
            px, py, pc = x ^ ((r >> 2) & 1), y ^ ((r >> 1) & 1), c ^ (r & 1)
            for a in range(n):
                cp = pltpu.make_async_remote_copy(src_ref=g_refs[a].at[4 * px + 2 * py + pc], dst_ref=out_refs[a].at[me],
                                                  send_sem=send_sems.at[a, r - 1], recv_sem=recv_sems.at[a, r - 1],
                                                  device_id=(px, py, pc), device_id_type=MESH_ID)
                cp.start()
                copies.append(cp)
        for cp in copies:
            cp.wait()

    return pl.pallas_call(
        body, out_shape=[jax.ShapeDtypeStruct(g.shape, g.dtype) for g in gs], in_specs=[_HBM] * n, out_specs=[_HBM] * n,
        scratch_shapes=[pltpu.SemaphoreType.DMA((n, N_DEV - 1)), pltpu.SemaphoreType.DMA((n, N_DEV - 1)), pltpu.SemaphoreType.DMA((n,))],
        name=name)(*gs)


_SEM = pl.BlockSpec(memory_space=pltpu.SEMAPHORE)
_IN_HBM = pl.BlockSpec(memory_space=pltpu.HBM)
_SIDE_EFFECT = pltpu.SideEffectType.DATAFLOW_SIDE_EFFECTING


def _push_copies(src_refs, land_refs, send_sems, recv_sems, per_peer):
    x, y, c = lax.axis_index("x"), lax.axis_index("y"), lax.axis_index("c")
    me = 4 * x + 2 * y + c
    copies = []
    for r in range(1, N_DEV):
        px, py, pc = x ^ ((r >> 2) & 1), y ^ ((r >> 1) & 1), c ^ (r & 1)
        for a, (s, d) in enumerate(zip(src_refs, land_refs)):
            k = a * (N_DEV - 1) + r - 1
            copies.append(pltpu.make_async_remote_copy(src_ref=s.at[4 * px + 2 * py + pc] if per_peer else s, dst_ref=d.at[me],
                                                       send_sem=send_sems.at[k], recv_sem=recv_sems.at[k],
                                                       device_id=(px, py, pc), device_id_type=MESH_ID))
    return copies


def _push_start(name, srcs, per_peer, after):
    n = len(srcs)
    lands = [lax.empty((N_DEV,) + (s.shape[1:] if per_peer else s.shape), s.dtype) for s in srcs]

    def body(*refs):
        for cp in _push_copies(refs[:n], refs[n:2 * n], refs[2 * n + 1], refs[2 * n + 2], per_peer):
            cp.start()
        refs[-1][...] = jnp.zeros_like(refs[-1])

    hbm = [pltpu.HBM(a.shape, a.dtype) for a in (*srcs, *lands)]
    sems = pltpu.SemaphoreType.DMA((n * (N_DEV - 1),))
    out = pl.pallas_call(
        body, name=name, out_shape=(sems, sems, *hbm, jax.ShapeDtypeStruct((8, LANES), F32)),
        in_specs=[_IN_HBM] * (2 * n) + [pl.BlockSpec(memory_space=pl.ANY)],
        out_specs=(_SEM, _SEM, *[_IN_HBM] * (2 * n), pl.BlockSpec(memory_space=pltpu.VMEM)),
        input_output_aliases={a: 2 + a for a in range(2 * n)},
        compiler_params=pltpu.CompilerParams(has_side_effects=_SIDE_EFFECT),
    )(*[pltpu.with_memory_space_constraint(a, pltpu.HBM) for a in (*srcs, *lands)], after)
    return out[0], out[1], out[2:2 + n], out[2 + n:2 + 2 * n], out[-1]


def _push_wait(name, started, per_peer, after):
    send_sems, recv_sems, srcs, lands, _ = started
    n = len(srcs)

    def body(*refs):
        for cp in _push_copies(refs[:n], refs[n:2 * n], refs[2 * n], refs[2 * n + 1], per_peer):
            cp.wait_send()
            cp.wait_recv()

    out = pl.pallas_call(
        body, name=name, out_shape=[pltpu.HBM(a.shape, a.dtype) for a in (*srcs, *lands)],
        in_specs=[_IN_HBM] * (2 * n) + [_SEM, _SEM, pl.BlockSpec(memory_space=pl.ANY)], out_specs=[_IN_HBM] * (2 * n),
        input_output_aliases={a: a for a in range(2 * n)},
        compiler_params=pltpu.CompilerParams(has_side_effects=_SIDE_EFFECT),
    )(*srcs, *lands, send_sems, recv_sems, after)
    return out[n:]


def _my_slot():
    return 4 * lax.axis_index("x") + 2 * lax.axis_index("y") + lax.axis_index("c")


def _row_tile(rows, cols, budget):
    if rows * cols * 4 <= budget or rows % 16:
        return rows
    best = 16
    for t in range(16, rows + 1, 16):
        if rows % t == 0 and t * cols * 4 <= budget:
            best = t
    return best


def _sum_slots(name, recv):
    _, R, C = recv.shape
    tr = _row_tile(R, C, 1 << 20)

    def body(r_ref, o_ref):
        g = r_ref[0].astype(F32)
        for k in range(1, N_DEV):
            g = g + r_ref[k].astype(F32)
        o_ref[...] = g

    return pl.pallas_call(body, grid=(R // tr,), in_specs=[pl.BlockSpec((N_DEV, tr, C), lambda i: (0, i, 0))],
                          out_specs=pl.BlockSpec((tr, C), lambda i: (i, 0)), out_shape=jax.ShapeDtypeStruct((R, C), F32),
                          compiler_params=_params(("arbitrary",), 32), name=name)(recv)


def _adamw(name, g, w, m, v):
    shape = w.shape
    R, C = shape[-2], shape[-1]
    view = (-1, R, C)
    L = w.size // (R * C)
    tr = _row_tile(R, C, 1 << 20)
    c1 = 1.0 - ADAM_B1 ** ADAM_STEP
    c2 = 1.0 - ADAM_B2 ** ADAM_STEP

    def body(g_ref, w_ref, m_ref, v_ref, d_ref, mo_ref, vo_ref):
        gv = g_ref[...]
        mn = ADAM_B1 * m_ref[...] + (1.0 - ADAM_B1) * gv
        vn = ADAM_B2 * v_ref[...] + (1.0 - ADAM_B2) * (gv * gv)
        mo_ref[...] = mn
        vo_ref[...] = vn
        d_ref[...] = -ADAM_LR * ((mn / c1) / (jnp.sqrt(vn / c2) + ADAM_EPS) + ADAM_WD * w_ref[...])

    blk = pl.BlockSpec((None, tr, C), lambda l, i: (l, i, 0))
    shp = jax.ShapeDtypeStruct((L, R, C), F32)
    outs = pl.pallas_call(body, grid=(L, R // tr), in_specs=[blk, blk, blk, blk], out_specs=[blk, blk, blk], out_shape=[shp, shp, shp],
                          compiler_params=_params(("arbitrary", "arbitrary"), 32), name=name)(*[a.reshape(view) for a in (g, w, m, v)])
    return [o.reshape(shape) for o in outs]


def _pad_flat(a, n):
    a = a.reshape(-1)
    return jnp.pad(a, (0, n - a.shape[0]))


def _seg(n):
    return -(-n // FLAT_ALIGN) * FLAT_ALIGN


def _to_blocks(full, axis):
    shp = full.shape
    return jnp.moveaxis(full.reshape(shp[:axis] + (N_DEV, shp[axis] // N_DEV) + shp[axis + 1:]), axis, 0)


def _from_blocks(blocks, axis):
    b = jnp.moveaxis(blocks, 0, axis)
    shp = b.shape
    return b.reshape(shp[:axis] + (shp[axis] * shp[axis + 1],) + shp[axis + 2:])


def _as_rows(shard, n):
    return shard.T if SHARD_AXIS[n] == 2 else shard


def _with_own(lands, own, me):
    return [lax.dynamic_update_slice(land, o[None], (me, 0, 0)) for land, o in zip(lands, own)]


def kernel(x, mem, positions, norm_mix_pre, norm_mix_post, w_in, pool_w, pool_scale, conv_b_w, w_branch_a, w_branch_b, w_branch_c, w_out, norm_mem_pre, norm_mem_post, norm_memkv, w_mq, w_mkv, w_mo, norm_ffn_pre, norm_ffn_post, w_up, conv_ffn_w, w_down, loss_target, m_norm_mix_pre, m_norm_mix_post, m_w_in, m_pool_w, m_pool_scale, m_conv_b_w, m_w_branch_a, m_w_branch_b, m_w_branch_c, m_w_out, m_norm_mem_pre, m_norm_mem_post, m_norm_memkv, m_w_mq, m_w_mkv, m_w_mo, m_norm_ffn_pre, m_norm_ffn_post, m_w_up, m_conv_ffn_w, m_w_down, v_norm_mix_pre, v_norm_mix_post, v_w_in, v_pool_w, v_pool_scale, v_conv_b_w, v_w_branch_a, v_w_branch_b, v_w_branch_c, v_w_out, v_norm_mem_pre, v_norm_mem_post, v_norm_memkv, v_w_mq, v_w_mkv, v_w_mo, v_norm_ffn_pre, v_norm_ffn_post, v_w_up, v_conv_ffn_w, v_w_down):
    w = dict(norm_mix_pre=norm_mix_pre, norm_mix_post=norm_mix_post, w_in=w_in, pool_w=pool_w, pool_scale=pool_scale, conv_b_w=conv_b_w, w_branch_a=w_branch_a, w_branch_b=w_branch_b, w_branch_c=w_branch_c, w_out=w_out, norm_mem_pre=norm_mem_pre, norm_mem_post=norm_mem_post, norm_memkv=norm_memkv, w_mq=w_mq, w_mkv=w_mkv, w_mo=w_mo, norm_ffn_pre=norm_ffn_pre, norm_ffn_post=norm_ffn_post, w_up=w_up, conv_ffn_w=conv_ffn_w, w_down=w_down)
    m = dict(norm_mix_pre=m_norm_mix_pre, norm_mix_post=m_norm_mix_post, w_in=m_w_in, pool_w=m_pool_w, pool_scale=m_pool_scale, conv_b_w=m_conv_b_w, w_branch_a=m_w_branch_a, w_branch_b=m_w_branch_b, w_branch_c=m_w_branch_c, w_out=m_w_out, norm_mem_pre=m_norm_mem_pre, norm_mem_post=m_norm_mem_post, norm_memkv=m_norm_memkv, w_mq=m_w_mq, w_mkv=m_w_mkv, w_mo=m_w_mo, norm_ffn_pre=m_norm_ffn_pre, norm_ffn_post=m_norm_ffn_post, w_up=m_w_up, conv_ffn_w=m_conv_ffn_w, w_down=m_w_down)
    v = dict(norm_mix_pre=v_norm_mix_pre, norm_mix_post=v_norm_mix_post, w_in=v_w_in, pool_w=v_pool_w, pool_scale=v_pool_scale, conv_b_w=v_conv_b_w, w_branch_a=v_w_branch_a, w_branch_b=v_w_branch_b, w_branch_c=v_w_branch_c, w_out=v_w_out, norm_mem_pre=v_norm_mem_pre, norm_mem_post=v_norm_mem_post, norm_memkv=v_norm_memkv, w_mq=v_w_mq, w_mkv=v_w_mkv, w_mo=v_w_mo, norm_ffn_pre=v_norm_ffn_pre, norm_ffn_post=v_norm_ffn_post, w_up=v_w_up, conv_ffn_w=v_conv_ffn_w, w_down=v_w_down)

    me = _my_slot()
    mix_big = [n for n in BIG if n not in LATE_BIG]
    block = lambda names, l: [_as_rows(w[n][l], n).astype(MXU) for n in names]
    conv = jnp.concatenate([_pad_flat(w[n], _seg(w[n].size)) for n in F32_GATHERED]).reshape(-1, LANES)
    got0 = _all_gather("weights_all_gather_0", block(mix_big, 0) + [conv])
    conv_all = got0[-1].reshape(N_DEV, -1)
    small, off = {n: w[n] for n in WEIGHTS if n not in SHARD_AXIS}, 0
    for n in F32_GATHERED:
        small[n] = _from_blocks(conv_all[:, off:off + w[n].size].reshape((N_DEV,) + w[n].shape), 2)
        off += _seg(w[n].size)
    whole = lambda names, got: {n: o.reshape(-1, o.shape[-1]) for n, o in zip(names, got)}
    pushes, after = {}, got0[0]
    for names, l in ((LATE_BIG, 0), (mix_big, 1), (LATE_BIG, 1)):
        pushes[names is LATE_BIG, l] = _push_start(f"weights_push_start_{l}{'b' if names is LATE_BIG else 'a'}", block(names, l), False, after)
        after = pushes[names is LATE_BIG, l][4]

    def arrived(late, l, done):
        names = LATE_BIG if late else mix_big
        lands = _push_wait(f"weights_push_wait_{l}{'b' if late else 'a'}", pushes[late, l], False, done)
        return _big_operands(whole(names, _with_own(lands, block(names, l), me)))

    ctab, stab = _rope_tables(positions[0])
    W0 = _layer_weights(whole(mix_big, got0), small, 0)
    sv0 = _layer_fwd_mix(x[0], dict(W0, g_mix_pre=W0['g_mix_pre'] + after[0, 0]), ctab, stab)
    W0.update(arrived(True, 0, sv0['x1']))
    x1 = _layer_fwd_late(mem[0], W0, sv0)
    W1 = _layer_weights({}, small, 1)
    W1.update(arrived(False, 1, x1))
    sv1 = _layer_fwd_mix(x1, W1, ctab, stab)
    W1.update(arrived(True, 1, sv1['x1']))
    x2 = _layer_fwd_late(mem[0], W1, sv1)
    dx, acc = _loss_head(x2, loss_target[0])
    loss = lax.psum(jnp.sum(acc) * (0.5 / D), MESH_AXES)
    grads = [None] * DEPTH
    dx, grads[1] = _layer_bwd(dx, mem[0], W1, sv1, ctab, stab)
    sent = [None, [grads[1][n].reshape(N_DEV, -1, grads[1][n].shape[-1]) for n in BIG]]
    push_g = _push_start("grads_push_start_1", sent[1], True, dx)
    dx, g_late = _layer_bwd_late(dx, mem[0], dict(W0, g_ffn_post=W0['g_ffn_post'] + push_g[4][0, 0]), sv0)
    sent_late = [g_late[n].reshape(N_DEV, -1, g_late[n].shape[-1]) for n in LATE_BIG]
    push_l = _push_start("grads_push_start_0", sent_late, True, dx)
    du, g_mix = _layer_bwd_mixers(dx, dict(W0, g_mix_post=W0['g_mix_post'] + push_l[4][0, 0]), sv0, ctab, stab)
    g_mix['w_in'] = _dw_in(du, sv0)
    sent_in = [g_mix['w_in'].reshape(N_DEV, -1, D)]
    push_i = _push_start("grads_push_start_in", sent_in, True, du)
    dx, g_mix['norm_mix_pre'] = _mix_pre_bwd(dx, du, dict(W0, g_mix_pre=W0['g_mix_pre'] + push_i[4][0, 0]), sv0)
    grads[0] = {**g_late, **g_mix}
    own = lambda s: [lax.dynamic_index_in_dim(a, me, 0, keepdims=False) for a in s]
    recv1 = _with_own(_push_wait("grads_push_wait_1", push_g, True, dx), own(sent[1]), me)
    recv_late = _with_own(_push_wait("grads_push_wait_0", push_l, True, dx), own(sent_late), me)
    mix_rest = [n for n in mix_big if n != 'w_in']

    misc_names = [n for n in WEIGHTS if n not in BIG]
    stacked = {n: jnp.stack([grads[l][n].reshape(small[n].shape[1:]) for l in range(DEPTH)]) for n in misc_names}
    rows = [(_to_blocks(stacked[n], 2) if n in SHARD_AXIS else jnp.broadcast_to(stacked[n][None], (N_DEV,) + stacked[n].shape))
            for n in misc_names]
    segs = [_seg(w[n].size) for n in misc_names]
    misc = jnp.concatenate([jnp.pad(r.reshape(N_DEV, -1), ((0, 0), (0, s - r[0].size))) for r, s in zip(rows, segs)],
                           axis=1).reshape(N_DEV, -1, LANES)
    recv_mix = _exchange("grad_exchange_0", [g_mix[n].reshape(N_DEV, -1, g_mix[n].shape[-1]) for n in mix_rest] + [misc])
    g_out, per_layer = {}, {}
    for l, names, recv in ((1, BIG, recv1), (0, LATE_BIG, recv_late), (0, mix_rest, recv_mix)):
        for n, r in zip(names, recv):
            per_layer[n, l] = _sum_slots(f"sum_{n}_{l}", r)
    misc_sum = _sum_slots("sum_misc", recv_mix[-1]).reshape(-1)
    off = 0
    for n, s in zip(misc_names, segs):
        g_out[n] = misc_sum[off:off + w[n].size].reshape(w[n].shape)
        off += s

    swap = lambda a: jnp.swapaxes(a, 1, 2)

    def update(n):
        if n not in BIG:
            return [g_out[n], *_adamw(f"adamw_{n}", g_out[n], w[n], m[n], v[n])]
        g = jnp.stack([per_layer[n, l] for l in range(DEPTH)])
        if SHARD_AXIS[n] == 2 and w[n].shape[2] % LANES:
            return [swap(a) for a in (g, *_adamw(f"adamw_{n}", g, swap(w[n]), swap(m[n]), swap(v[n])))]
        g = swap(g) if SHARD_AXIS[n] == 2 else g
        return [g, *_adamw(f"adamw_{n}", g, w[n], m[n], v[n])]

    done = {n: update(n) for n in WEIGHTS if n != 'w_in'}
    recv_in = _with_own(_push_wait("grads_push_wait_in", push_i, True, done[WEIGHTS[-1]][1]), own(sent_in), me)
    per_layer['w_in', 0] = _sum_slots("sum_w_in_0", recv_in[0])
    done['w_in'] = update('w_in')
    return (loss, dx[None], *[done[n][k] for k in range(4) for n in WEIGHTS])
```

```python
import jax
import jax.numpy as jnp
from jax import lax
from jax.experimental import pallas as pl
from jax.experimental.pallas import tpu as pltpu

F32 = jnp.float32
MXU = jnp.bfloat16
HI = lax.Precision.HIGHEST

D = 1024
DEPTH = 2
POOLW = 384
ATT_W = 768
ATT_O = 256
GATE_W = 3 * D
IN_W = 6912
IN_TILE = 768
IN_ROT = (IN_W - GATE_W) // IN_TILE
MEM_W = 512
D_FF = 2816
EPS = 1e-6
ROPE_THETA = 500000.0
QB = 128
DILS = (1, 4, 16)
NEG = -1e30
MEM_SCALE = 128 ** -0.5
ATT_SCALE = 0.125

ADAM_LR, ADAM_B1, ADAM_B2, ADAM_EPS, ADAM_WD, ADAM_STEP = 0.001, 0.9, 0.999, 1e-08, 0.01, 10

N_DEV = 8
MESH_AXES = ("x", "y", "c")
LANES = 128
FLAT_ALIGN = 2048
ROW_TILE = 1024

WEIGHTS = ['norm_mix_pre', 'norm_mix_post', 'w_in', 'pool_w', 'pool_scale', 'conv_b_w', 'w_branch_a', 'w_branch_b',
           'w_branch_c', 'w_out', 'norm_mem_pre', 'norm_mem_post', 'norm_memkv', 'w_mq', 'w_mkv', 'w_mo',
           'norm_ffn_pre', 'norm_ffn_post', 'w_up', 'conv_ffn_w', 'w_down']
SHARD_AXIS = {'w_in': 2, 'conv_b_w': 2, 'w_branch_a': 2, 'w_branch_b': 2, 'w_branch_c': 2, 'w_out': 1, 'w_mq': 1,
              'w_mkv': 1, 'w_mo': 2, 'w_up': 2, 'conv_ffn_w': 2, 'w_down': 1}
F32_GATHERED = ('conv_b_w', 'conv_ffn_w')
BIG = [n for n in WEIGHTS if n in SHARD_AXIS and n not in F32_GATHERED]
LATE_BIG = ['w_mq', 'w_mkv', 'w_mo', 'w_up', 'w_down']


VMEM_LIMIT_MB = 60


def _params(sem, vmem_mb):
    del vmem_mb
    return pltpu.CompilerParams(dimension_semantics=sem, vmem_limit_bytes=VMEM_LIMIT_MB << 20)


def _dot(a, b, prec=None):
    return lax.dot_general(a, b, (((1,), (0,)), ((), ())), preferred_element_type=F32, precision=prec)


def _dot_nt(a, b, prec=None):
    return lax.dot_general(a, b, (((1,), (1,)), ((), ())), preferred_element_type=F32, precision=prec)


def _dot_tn(a, b, prec=None):
    return lax.dot_general(a, b, (((0,), (0,)), ((), ())), preferred_element_type=F32, precision=prec)


def _tile(n, cap):
    if n <= cap:
        return n
    best = None
    for t in range(LANES, cap + 1, LANES):
        if n % t == 0:
            best = t
    assert best is not None, (n, cap)
    return best


def _rms(x, g):
    r = lax.rsqrt(jnp.mean(x * x, axis=-1, keepdims=True) + EPS)
    return x * r * g, r


def _rms_bwd(w, y):
    r = lax.rsqrt(jnp.mean(y * y, axis=-1, keepdims=True) + EPS)
    return r * w - y * (r * r * r) * jnp.mean(w * y, axis=-1, keepdims=True), r


def _rows_call(name, body, n_rows, ts, ins, outs, scratch=(), reverse=False, vmem_mb=48, aliases=None):
    nt = n_rows // ts
    assert nt * ts == n_rows

    def tile_of(g):
        return (nt - 1 - g) if reverse else g

    in_specs, args = [], []
    for op in ins:
        if op[0] == "t":
            _, a, cw, cb = op
            in_specs.append(pl.BlockSpec((ts, cw), lambda g, cb=cb: (tile_of(g), cb)))
        elif op[0] == "h":
            _, a, hr, cw, cb = op
            in_specs.append(pl.BlockSpec((hr, cw), lambda g, cb=cb, k=ts // hr: (jnp.maximum(tile_of(g) * k - 1, 0), cb)))
        elif op[0] == "x":
            _, a = op
            in_specs.append(pl.BlockSpec(memory_space=pl.ANY))
        else:
            _, a = op
            in_specs.append(pl.BlockSpec(a.shape, lambda g, n=a.ndim: (0,) * n))
        args.append(a)
    out_specs, out_shape = [], []
    for op in outs:
        if op[0] == "t":
            _, cols, dt = op
            out_specs.append(pl.BlockSpec((ts, cols), lambda g: (tile_of(g), 0)))
            out_shape.append(jax.ShapeDtypeStruct((n_rows, cols), dt))
        elif op[0] == "c":
            _, total, cols, cb, dt = op
            out_specs.append(pl.BlockSpec((ts, cols), lambda g, cb=cb: (tile_of(g), cb)))
            out_shape.append(jax.ShapeDtypeStruct((n_rows, total), dt))
        else:
            _, shp, dt = op
            out_specs.append(pl.BlockSpec(shp, lambda g, n=len(shp): (0,) * n))
            out_shape.append(jax.ShapeDtypeStruct(shp, dt))

    def kern(*refs):
        g = pl.program_id(0)
        body(tile_of(g), g, *refs)

    return pl.pallas_call(kern, grid=(nt,), in_specs=in_specs, out_specs=out_specs, out_shape=out_shape,
                          scratch_shapes=list(scratch), input_output_aliases=aliases or {},
                          compiler_params=_params(("arbitrary",), vmem_mb), name=name)(*args)


def _acc(ref, g, val):
    @pl.when(g == 0)
    def _():
        ref[...] = val

    @pl.when(g != 0)
    def _():
        ref[...] += val


def _norm_mm(name, x, g, w, ts, tn, out_dtype=F32, wt=False, rot=0):
    S, K = x.shape
    N = w.shape[0] if wt else w.shape[1]
    assert wt or not rot

    def body(x_ref, g_ref, w_ref, o_ref, h_ref, hs):
        @pl.when(pl.program_id(1) == 0)
        def _():
            h, _ = _rms(x_ref[...], g_ref[...])
            hs[...] = h.astype(MXU)
            h_ref[...] = h.astype(MXU)

        o_ref[...] = (_dot_nt if wt else _dot)(hs[...], w_ref[...]).astype(out_dtype)

    w_spec = pl.BlockSpec((tn, K), lambda i, j: ((j + rot) % (N // tn), 0)) if wt else pl.BlockSpec((K, tn), lambda i, j: (0, j))
    return pl.pallas_call(
        body, grid=(S // ts, N // tn),
        in_specs=[pl.BlockSpec((ts, K), lambda i, j: (i, 0)), pl.BlockSpec((1, K), lambda i, j: (0, 0)), w_spec],
        out_specs=[pl.BlockSpec((ts, tn), lambda i, j: (i, j)), pl.BlockSpec((ts, K), lambda i, j: (i, 0))],
        out_shape=[jax.ShapeDtypeStruct((S, N), out_dtype), jax.ShapeDtypeStruct((S, K), MXU)],
        scratch_shapes=[pltpu.VMEM((ts, K), MXU)],
        compiler_params=_params(("arbitrary", "arbitrary"), 48), name=name)(x, g, w)


def _mm_nt(name, a, b, ts, tn, out_dtype=F32):
    M, K = a.shape
    N = b.shape[0]

    def body(a_ref, b_ref, o_ref):
        o_ref[...] = _dot_nt(a_ref[...], b_ref[...]).astype(out_dtype)

    return pl.pallas_call(
        body, grid=(M // ts, N // tn),
        in_specs=[pl.BlockSpec((ts, K), lambda i, j: (i, 0)), pl.BlockSpec((tn, K), lambda i, j: (j, 0))],
        out_specs=pl.BlockSpec((ts, tn), lambda i, j: (i, j)), out_shape=jax.ShapeDtypeStruct((M, N), out_dtype),
        compiler_params=_params(("arbitrary", "arbitrary"), 48), name=name)(a, b)


def _mm_tn(name, a, b, cap_k=512, cap_n=1024, out_dtype=MXU, rot=0):
    S, K = a.shape
    N = b.shape[1]
    tk, tn = _tile(K, cap_k), _tile(N, cap_n)

    def body(a_ref, b_ref, o_ref):
        o_ref[...] = _dot_tn(a_ref[...], b_ref[...]).astype(out_dtype)

    return pl.pallas_call(
        body, grid=(K // tk, N // tn),
        in_specs=[pl.BlockSpec((S, tk), lambda i, j: (0, i)), pl.BlockSpec((S, tn), lambda i, j: (0, j))],
        out_specs=pl.BlockSpec((tk, tn), lambda i, j: ((i + rot) % (K // tk), j)), out_shape=jax.ShapeDtypeStruct((K, N), out_dtype),
        compiler_params=_params(("arbitrary", "arbitrary"), 48), name=name)(a, b)


def _pool_cols(shape):
    col = lax.broadcasted_iota(jnp.int32, shape, 1)
    return col < 96, col < 192, col < 288


def _pool_select(s2, s4, s8, s16):
    c1, c2, c3 = _pool_cols(s2.shape)
    return jnp.where(c1, s2, jnp.where(c2, s4, jnp.where(c3, s8, s16)))


def _pool_cnt(t0, ts):
    c1, c2, c3 = _pool_cols((ts, POOLW))
    win = jnp.where(c1, 2, jnp.where(c2, 4, jnp.where(c3, 8, 16)))
    t = t0 + lax.broadcasted_iota(jnp.int32, (ts, POOLW), 0)
    return jnp.minimum(t + 1, win).astype(F32)


def _pooled(a, prev, t0):
    ts = a.shape[0]
    ext = jnp.concatenate([prev, a], axis=0)
    s2 = ext + pltpu.roll(ext, 1, axis=0)
    s4 = s2 + pltpu.roll(s2, 2, axis=0)
    s8 = s4 + pltpu.roll(s4, 4, axis=0)
    s16 = s8 + pltpu.roll(s8, 8, axis=0)
    sums = _pool_select(s2, s4, s8, s16)[16:]
    return sums / _pool_cnt(t0, ts) - a


def _conv3(z, prev8, w):
    ext = jnp.concatenate([prev8, z], axis=0)
    z1 = pltpu.roll(ext, 1, axis=0)[8:]
    z2 = pltpu.roll(ext, 2, axis=0)[8:]
    return w[0:1] * z2 + w[1:2] * z1 + w[2:3] * z, z1, z2


def _conv3_t(dc, next8, w):
    ts = dc.shape[0]
    ext = jnp.concatenate([dc, next8], axis=0)
    n = ts + 8
    u1 = pltpu.roll(ext, n - 1, axis=0)[:ts]
    u2 = pltpu.roll(ext, n - 2, axis=0)[:ts]
    return w[2:3] * dc + w[1:2] * u1 + w[0:1] * u2


def _poolconv_fwd(u, wblk, pool_scale, conv_b, ts=256):
    S = u.shape[0]

    def body(i, g, a_ref, bx_ref, bb_ref, bc_ref, wblk_ref, ps_ref, cw_ref, a2_ref, yb_ref, ca, cz):
        @pl.when(g == 0)
        def _():
            ca[...] = jnp.zeros_like(ca)
            cz[...] = jnp.zeros_like(cz)

        a = a_ref[...].astype(F32)
        p = _pooled(a, ca[...], i * ts)
        mixed = _dot(p.astype(MXU), wblk_ref[...])
        a2_ref[...] = (mixed * ps_ref[...]).astype(MXU)
        z = bc_ref[...].astype(F32) * bx_ref[...].astype(F32)
        conv, _, _ = _conv3(z, cz[...], cw_ref[...])
        yb_ref[...] = (bb_ref[...].astype(F32) * conv).astype(MXU)
        ca[...] = a[ts - 16:]
        cz[...] = z[ts - 8:]

    ins = [("t", u, POOLW, 8), ("t", u, POOLW, 9), ("t", u, POOLW, 10), ("t", u, POOLW, 11), ("w", wblk), ("w", pool_scale),
           ("w", conv_b)]
    return _rows_call("poolconv_fwd", body, S, ts, ins, [("t", POOLW, MXU), ("t", POOLW, MXU)],
                      scratch=[pltpu.VMEM((16, POOLW), F32), pltpu.VMEM((8, POOLW), F32)])


def _poolconv_bwd(u, d_a2, d_yb, du, wblk, pool_scale, conv_b, ts=256):
    S = u.shape[0]

    def body(i, g, a_ref, bx_ref, bb_ref, bc_ref, ap_ref, bxp_ref, bcp_ref, da2_ref, dyb_ref, wblk_ref, ps_ref, cw_ref, _,
             o_ref, dps_ref, dwb_ref, dcw_ref, ce, cdz):
        @pl.when(g == 0)
        def _():
            ce[...] = jnp.zeros_like(ce)
            cdz[...] = jnp.zeros_like(cdz)

        first = (i > 0).astype(F32)
        a = a_ref[...].astype(F32)
        p = _pooled(a, ap_ref[...].astype(F32) * first, i * ts)
        pb = p.astype(MXU)
        mixed = _dot(pb, wblk_ref[...])
        da2 = da2_ref[...]
        dmixed = (da2 * ps_ref[...]).astype(MXU)
        dp = _dot_nt(dmixed, wblk_ref[...])
        _acc(dps_ref, g, jnp.sum(da2 * mixed, axis=0, keepdims=True))
        _acc(dwb_ref, g, _dot_tn(pb, dmixed))
        e = dp / _pool_cnt(i * ts, ts)
        ext = jnp.concatenate([e, ce[...]], axis=0)
        n = ts + 16
        f2 = ext + pltpu.roll(ext, n - 1, axis=0)
        f4 = f2 + pltpu.roll(f2, n - 2, axis=0)
        f8 = f4 + pltpu.roll(f4, n - 4, axis=0)
        f16 = f8 + pltpu.roll(f8, n - 8, axis=0)
        o_ref[:, 0:POOLW] = (_pool_select(f2, f4, f8, f16)[:ts] - dp).astype(o_ref.dtype)
        ce[...] = e[:16]

        bx, bb, bc = bx_ref[...].astype(F32), bb_ref[...].astype(F32), bc_ref[...].astype(F32)
        z = bc * bx
        w = cw_ref[...]
        conv, z1, z2 = _conv3(z, (bxp_ref[...].astype(F32) * bcp_ref[...].astype(F32))[8:16] * first, w)
        dyb = dyb_ref[...]
        dconv = dyb * bb
        dz = _conv3_t(dconv, cdz[...], w)
        o_ref[:, POOLW:2 * POOLW] = (dz * bc).astype(o_ref.dtype)
        o_ref[:, 2 * POOLW:3 * POOLW] = (dyb * conv).astype(o_ref.dtype)
        o_ref[:, 3 * POOLW:4 * POOLW] = (dz * bx).astype(o_ref.dtype)
        dw = jnp.concatenate([jnp.sum(dconv * z2, axis=0, keepdims=True), jnp.sum(dconv * z1, axis=0, keepdims=True),
                              jnp.sum(dconv * z, axis=0, keepdims=True)], axis=0)
        _acc(dcw_ref, g, dw)
        cdz[...] = dconv[:8]

    ins = [("t", u, POOLW, 8), ("t", u, POOLW, 9), ("t", u, POOLW, 10), ("t", u, POOLW, 11),
           ("h", u, 16, POOLW, 8), ("h", u, 16, POOLW, 9), ("h", u, 16, POOLW, 11),
           ("t", d_a2, POOLW, 0), ("t", d_yb, POOLW, 0), ("w", wblk), ("w", pool_scale), ("w", conv_b), ("x", du)]
    outs = [("c", IN_W, 4 * POOLW, GATE_W // (4 * POOLW), MXU), ("a", (1, POOLW), F32), ("a", (POOLW, POOLW), F32), ("a", (3, POOLW), F32)]
    return _rows_call("poolconv_bwd", body, S, ts, ins, outs, aliases={len(ins) - 1: 0},
                      scratch=[pltpu.VMEM((16, POOLW), F32), pltpu.VMEM((8, POOLW), F32)], reverse=True)


def _rope_tables(positions):
    S = positions.shape[0]
    inv = ROPE_THETA ** (-jnp.arange(0, 16, 2, dtype=F32) / 16)
    ang = positions.astype(F32)[:, None] * inv
    cos, sin = jnp.cos(ang), jnp.sin(ang)
    c64 = jnp.concatenate([cos, cos, jnp.ones((S, 48), F32)], axis=1)
    s64 = jnp.concatenate([-sin, sin, jnp.zeros((S, 48), F32)], axis=1)
    return jnp.concatenate([c64, c64], axis=1), jnp.concatenate([s64, s64], axis=1)


def _partner(x):
    lane = lax.broadcasted_iota(jnp.int32, x.shape, 1) % 64
    return jnp.where(lane < 8, pltpu.roll(x, LANES - 8, axis=1), jnp.where(lane < 16, pltpu.roll(x, 8, axis=1), 0.0))


def _rope(x, c, s):
    return x * c + _partner(x) * s


def _rope_t(x, c, s):
    return x * c + _partner(x * s)


def _rows_of(r, n, d):
    return pl.ds(r, n, stride=d) if d > 1 else pl.ds(0, n)


def _head_masks(shape):
    lane = lax.broadcasted_iota(jnp.int32, shape, 1) // 64
    return [lane == h for h in range(4)]


def _only(mask, x):
    return jnp.where(mask, x, jnp.zeros_like(x))


def _rope_perm(u, ctab, stab, ts=256):
    S = u.shape[0]
    nch = ATT_W // LANES

    def body(*refs):
        chunks, (c_ref, s_ref), outs, scr = refs[:3 * nch], refs[3 * nch:3 * nch + 2], refs[3 * nch + 2:-1], refs[-1]
        for k in range(3 * nch):
            scr[k] = chunks[k][...].astype(F32)
        for g, d in enumerate(DILS):
            n = ts // d
            for r in range(d):
                rows = _rows_of(r, n, d)
                c, s = c_ref[rows, :], s_ref[rows, :]
                for which in range(3):
                    parts = [scr.at[which * nch + j][rows, :] for j in (2 * g, 2 * g + 1)]
                    if which < 2:
                        parts = [_rope(x, c, s) for x in parts]
                    outs[which * 3 + g][r] = jnp.concatenate(parts, axis=1).astype(MXU)

    base = (IN_W - 3 * ATT_W) // LANES
    in_specs = [pl.BlockSpec((ts, LANES), lambda i, cb=base + k: (i, cb)) for k in range(3 * nch)]
    in_specs += [pl.BlockSpec((ts, LANES), lambda i: (i, 0))] * 2
    out_specs = [pl.BlockSpec((d, ts // d, ATT_O), lambda i: (0, i, 0)) for _ in range(3) for d in DILS]
    out_shape = [jax.ShapeDtypeStruct((d, S // d, ATT_O), MXU) for _ in range(3) for d in DILS]
    res = pl.pallas_call(body, grid=(S // ts,), in_specs=in_specs, out_specs=out_specs, out_shape=out_shape,
                         scratch_shapes=[pltpu.VMEM((3 * nch, ts, LANES), F32)],
                         compiler_params=_params(("arbitrary",), 32), name="rope_perm")(*([u] * (3 * nch)), ctab, stab)
    return [[res[which * 3 + g].reshape(S, ATT_O) for g in range(3)] for which in range(3)]


def _rope_unperm_bwd(dqkv, du, ctab, stab, ts=256):
    S = dqkv[0][0].shape[0]
    nch = ATT_W // LANES

    def body(*refs):
        ins, (c_ref, s_ref, _, o_ref, scr) = refs[:9], refs[9:]
        for g, d in enumerate(DILS):
            n = ts // d
            for r in range(d):
                rows = _rows_of(r, n, d)
                c, s = c_ref[rows, :], s_ref[rows, :]
                for which in range(3):
                    v = ins[which * 3 + g][r]
                    for half in range(2):
                        x = v[:, half * LANES:(half + 1) * LANES]
                        scr.at[which * nch + 2 * g + half][rows, :] = _rope_t(x, c, s) if which < 2 else x
        for j in range(3 * nch):
            o_ref[:, j * LANES:(j + 1) * LANES] = scr[j].astype(o_ref.dtype)

    in_specs = [pl.BlockSpec((d, ts // d, ATT_O), lambda i: (0, i, 0)) for _ in range(3) for d in DILS]
    in_specs += [pl.BlockSpec((ts, LANES), lambda i: (i, 0))] * 2 + [pl.BlockSpec(memory_space=pl.ANY)]
    args = [dqkv[which][g].reshape(d, S // d, ATT_O) for which in range(3) for g, d in enumerate(DILS)]
    last = (IN_W - 3 * ATT_W) // (3 * ATT_W)
    return pl.pallas_call(body, grid=(S // ts,), in_specs=in_specs, out_specs=pl.BlockSpec((ts, 3 * ATT_W), lambda i: (i, last)),
                          out_shape=jax.ShapeDtypeStruct((S, IN_W), MXU), scratch_shapes=[pltpu.VMEM((3 * nch, ts, LANES), F32)],
                          input_output_aliases={len(in_specs) - 1: 0},
                          compiler_params=_params(("arbitrary",), 32), name="rope_unperm_bwd")(*args, ctab, stab, du)


def _band_mask_keys(has_prev):
    r = lax.broadcasted_iota(jnp.int32, (QB, 2 * QB), 0)
    c = lax.broadcasted_iota(jnp.int32, (QB, 2 * QB), 1)
    return ((c < QB) & (c >= r) & has_prev) | ((c >= QB) & (c - QB <= r))


def _band_mask_queries(has_next):
    r = lax.broadcasted_iota(jnp.int32, (2 * QB, QB), 0)
    c = lax.broadcasted_iota(jnp.int32, (2 * QB, QB), 1)
    return ((r < QB) & (c <= r)) | ((r >= QB) & (c >= r - QB) & has_next)


ASUB = 4
_BIG = pl.BlockSpec((ASUB * QB, ATT_O), lambda b: (b, 0))
_PREV = pl.BlockSpec((QB, ATT_O), lambda b: (jnp.maximum(b * ASUB - 1, 0), 0))


def _sub(ref, j):
    return ref[j * QB:(j + 1) * QB]


def _attn_fwd(g, q, k, v):
    S = q.shape[0]
    nb = S // QB
    nblk = nb // DILS[g]

    def body(q_ref, kc_ref, kp_ref, vc_ref, vp_ref, o_ref, m_ref, l_ref):
        hm_kv, hm_o = _head_masks((2 * QB, ATT_O)), _head_masks((QB, ATT_O))
        for j in range(ASUB):
            ok = _band_mask_keys(((pl.program_id(0) * ASUB + j) & (nblk - 1)) > 0)
            k2 = jnp.concatenate([kp_ref[...] if j == 0 else _sub(kc_ref, j - 1), _sub(kc_ref, j)], axis=0)
            v2 = jnp.concatenate([vp_ref[...] if j == 0 else _sub(vc_ref, j - 1), _sub(vc_ref, j)], axis=0)
            qv = _sub(q_ref, j)
            o_acc = jnp.zeros((QB, ATT_O), F32)
            m_acc = jnp.zeros((QB, ATT_O), F32)
            l_acc = jnp.zeros((QB, ATT_O), F32)
            for h in range(4):
                s = jnp.where(ok, _dot_nt(qv, _only(hm_kv[h], k2)) * ATT_SCALE, NEG)
                m = jnp.max(s, axis=1, keepdims=True)
                p = jnp.exp(s - m)
                o_acc = o_acc + _dot(p.astype(MXU), _only(hm_kv[h], v2))
                m_acc = jnp.where(hm_o[h], m, m_acc)
                l_acc = jnp.where(hm_o[h], jnp.sum(p, axis=1, keepdims=True), l_acc)
            o_ref[j * QB:(j + 1) * QB] = o_acc
            m_ref[j * QB:(j + 1) * QB] = m_acc
            l_ref[j * QB:(j + 1) * QB] = l_acc

    shp = jax.ShapeDtypeStruct((S, ATT_O), F32)
    return pl.pallas_call(body, grid=(nb // ASUB,), in_specs=[_BIG, _BIG, _PREV, _BIG, _PREV],
                          out_specs=[_BIG] * 3, out_shape=[shp, shp, shp], compiler_params=_params(("arbitrary",), 32),
                          name=f"attn_fwd_{g}")(q, k, k, v, v)


def _natural(ref, d, scr, ts):
    if d == 1:
        return ref[0]
    n = ts // d
    for r in range(d):
        v = ref[r]
        scr.at[0][pl.ds(r, n, stride=d), :] = v[:, 0:LANES]
        scr.at[1][pl.ds(r, n, stride=d), :] = v[:, LANES:2 * LANES]
    return jnp.concatenate([scr[0], scr[1]], axis=1)


def _attn_combine(oml, ts=256):
    S = oml[0][0].shape[0]

    def body(*refs):
        ins, (att_ref, out_ref, lse_ref, scr) = refs[:9], refs[9:]
        o, m, l = [[_natural(ins[3 * g + k], d, scr, ts) for g, d in enumerate(DILS)] for k in range(3)]
        mx = jnp.maximum(jnp.maximum(m[0], m[1]), m[2])
        w = [jnp.exp(m[g] - mx) for g in range(3)]
        den = w[0] * l[0] + w[1] * l[1] + w[2] * l[2]
        out = (w[0] * o[0] + w[1] * o[1] + w[2] * o[2]) / den
        out_ref[...] = out
        att_ref[...] = out.astype(MXU)
        lse_ref[...] = mx + jnp.log(den)

    in_specs = [pl.BlockSpec((d, ts // d, ATT_O), lambda i: (0, i, 0)) for d in DILS for _ in range(3)]
    args = [a.reshape(d, S // d, ATT_O) for d, grp in zip(DILS, oml) for a in grp]
    blk = pl.BlockSpec((ts, ATT_O), lambda i: (i, 0))
    return pl.pallas_call(body, grid=(S // ts,), in_specs=in_specs, out_specs=[blk, blk, blk],
                          out_shape=[jax.ShapeDtypeStruct((S, ATT_O), MXU), jax.ShapeDtypeStruct((S, ATT_O), F32),
                                     jax.ShapeDtypeStruct((S, ATT_O), F32)],
                          scratch_shapes=[pltpu.VMEM((2, ts, LANES), F32)], compiler_params=_params(("arbitrary",), 32),
                          name="attn_combine")(*args)


def _attn_bwd_prep(datt, o, lse, ts=256):
    S = datt.shape[0]

    def body(da0, da1, o_ref, l0, l1, *rest):
        outs, dl = rest[:9], rest[9]
        prod = jnp.concatenate([da0[...], da1[...]], axis=1) * o_ref[...]
        delta = jnp.zeros((ts, ATT_O), F32)
        for hm in _head_masks((ts, ATT_O)):
            delta = jnp.where(hm, jnp.sum(_only(hm, prod), axis=1, keepdims=True), delta)
        dl[0] = delta[:, 0:LANES]
        dl[1] = delta[:, LANES:2 * LANES]
        for g, d in enumerate(DILS):
            n = ts // d
            for r in range(d):
                rows = _rows_of(r, n, d)
                outs[g][r] = jnp.concatenate([da0[rows, :], da1[rows, :]], axis=1).astype(MXU)
                outs[3 + g][r] = jnp.concatenate([dl.at[0][rows, :], dl.at[1][rows, :]], axis=1)
                outs[6 + g][r] = jnp.concatenate([l0[rows, :], l1[rows, :]], axis=1)

    half = lambda j: pl.BlockSpec((ts, LANES), lambda i: (i, j))
    out_specs = [pl.BlockSpec((d, ts // d, ATT_O), lambda i: (0, i, 0)) for _ in range(3) for d in DILS]
    out_shape = [jax.ShapeDtypeStruct((d, S // d, ATT_O), dt) for dt in (MXU, F32, F32) for d in DILS]
    res = pl.pallas_call(body, grid=(S // ts,), in_specs=[half(0), half(1), pl.BlockSpec((ts, ATT_O), lambda i: (i, 0)), half(0), half(1)],
                         out_specs=out_specs, out_shape=out_shape, scratch_shapes=[pltpu.VMEM((2, ts, LANES), F32)],
                         compiler_params=_params(("arbitrary",), 32), name="attn_bwd_prep")(datt, datt, o, lse, lse)
    return [[res[k * 3 + g].reshape(S, ATT_O) for g in range(3)] for k in range(3)]


def _head_col(x, h):
    return x[:, h * 64:h * 64 + 1]


def _attn_bwd(g, q, k, v, do, delta, lse):
    S = q.shape[0]
    nb = S // QB
    nblk = nb // DILS[g]

    def body(k_ref, v_ref, qc_ref, qn_ref, doc_ref, don_ref, dlc_ref, dln_ref, lc_ref, ln_ref, dq_ref, dk_ref, dv_ref, dq_scr):
        hms, hmk = _head_masks((2 * QB, ATT_O)), _head_masks((QB, ATT_O))
        first = pl.program_id(0) == 0

        @pl.when(first)
        def _():
            dq_scr[0:QB] = jnp.zeros((QB, ATT_O), F32)

        @pl.when(jnp.logical_not(first))
        def _():
            dq_scr[0:QB] = dq_scr[ASUB * QB:(ASUB + 1) * QB]

        dq_scr[QB:(ASUB + 1) * QB] = jnp.zeros((ASUB * QB, ATT_O), F32)

        def both(cur_ref, nxt_ref, j):
            return jnp.concatenate([_sub(cur_ref, j), nxt_ref[...] if j == ASUB - 1 else _sub(cur_ref, j + 1)], axis=0)

        for j in range(ASUB):
            ok = _band_mask_queries(((pl.program_id(0) * ASUB + j + 1) & (nblk - 1)) > 0)
            q2, do2, dl2, lse2 = both(qc_ref, qn_ref, j), both(doc_ref, don_ref, j), both(dlc_ref, dln_ref, j), both(lc_ref, ln_ref, j)
            kv, vv = _sub(k_ref, j), _sub(v_ref, j)
            dk = jnp.zeros((QB, ATT_O), F32)
            dv = jnp.zeros((QB, ATT_O), F32)
            dq2 = jnp.zeros((2 * QB, ATT_O), F32)
            for h, hm in enumerate(hms):
                qh, doh = _only(hm, q2), _only(hm, do2)
                p = jnp.where(ok, jnp.exp(_dot_nt(qh, kv) * ATT_SCALE - _head_col(lse2, h)), 0.0)
                ds = (p * (_dot_nt(doh, vv) - _head_col(dl2, h))).astype(MXU)
                dv = dv + _dot_tn(p.astype(MXU), doh)
                dk = dk + _dot_tn(ds, qh)
                dq2 = dq2 + _dot(ds, _only(hmk[h], kv))
            dk_ref[j * QB:(j + 1) * QB] = dk * ATT_SCALE
            dv_ref[j * QB:(j + 1) * QB] = dv
            dq_scr[j * QB:(j + 2) * QB] += dq2
        dq_ref[...] = dq_scr[0:ASUB * QB] * ATT_SCALE

    nxt = pl.BlockSpec((QB, ATT_O), lambda b: (jnp.minimum((b + 1) * ASUB, nb - 1), 0))
    shp = jax.ShapeDtypeStruct((S, ATT_O), F32)
    return pl.pallas_call(body, grid=(nb // ASUB,), in_specs=[_BIG, _BIG, _BIG, nxt, _BIG, nxt, _BIG, nxt, _BIG, nxt], out_specs=[_BIG] * 3,
                          out_shape=[shp, shp, shp], scratch_shapes=[pltpu.VMEM(((ASUB + 1) * QB, ATT_O), F32)],
                          compiler_params=_params(("arbitrary",), 32), name=f"attn_bwd_{g}")(k, v, q, q, do, do, delta, delta, lse, lse)


def _merge_fwd(x0, u, a2, yb, att, wa, wb, wc, w_out, g_post, ts=256):
    S = x0.shape[0]

    def body(i, g, x_ref, gate_ref, a2_ref, yb_ref, att_ref, wa_ref, wb_ref, wc_ref, wo_ref, gp_ref, mg_ref, y_ref, xo_ref):
        gate = lambda n: jax.nn.sigmoid(gate_ref[:, n * D:(n + 1) * D].astype(F32))
        merged = gate(0) * _dot_nt(a2_ref[...], wa_ref[...])
        merged = merged + gate(1) * _dot_nt(yb_ref[...], wb_ref[...])
        merged = merged + gate(2) * _dot_nt(att_ref[...], wc_ref[...])
        mb = merged.astype(MXU)
        mg_ref[...] = mb
        y = _dot(mb, wo_ref[...])
        y_ref[...] = y
        xo_ref[...] = x_ref[...] + _rms(y, gp_ref[...])[0]

    ins = [("t", x0, D, 0), ("t", u, GATE_W, 0), ("t", a2, POOLW, 0), ("t", yb, POOLW, 0), ("t", att, ATT_O, 0),
           ("w", wa), ("w", wb), ("w", wc), ("w", w_out), ("w", g_post)]
    return _rows_call("merge_fwd", body, S, ts, ins, [("t", D, MXU), ("t", D, F32), ("t", D, F32)])


def _merge_bwd(dx, y1, u, a2, yb, att, wa, wb, wc, w_out, g_post, ts=256):
    S = dx.shape[0]

    def body(i, g, dx_ref, y_ref, gate_ref, a2_ref, yb_ref, att_ref, wa_ref, wb_ref, wc_ref, wo_ref, gp_ref,
             dy_ref, dgate_ref, dbra_ref, dbrb_ref, dbrc_ref, da2_ref, dyb_ref, datt_ref, dgp_ref):
        dxv, y = dx_ref[...], y_ref[...]
        dy, r = _rms_bwd(dxv * gp_ref[...], y)
        _acc(dgp_ref, g, jnp.sum(dxv * (y * r), axis=0, keepdims=True))
        dyb16 = dy.astype(MXU)
        dy_ref[...] = dyb16
        dm = _dot_nt(dyb16, wo_ref[...])
        for n, (src, w_ref, dbr_ref, din_ref) in enumerate(((a2_ref, wa_ref, dbra_ref, da2_ref), (yb_ref, wb_ref, dbrb_ref, dyb_ref),
                                                           (att_ref, wc_ref, dbrc_ref, datt_ref))):
            gt = jax.nn.sigmoid(gate_ref[:, n * D:(n + 1) * D].astype(F32))
            br = _dot_nt(src[...], w_ref[...])
            dgate_ref[:, n * D:(n + 1) * D] = (dm * br * gt * (1.0 - gt)).astype(dgate_ref.dtype)
            dbr = (dm * gt).astype(MXU)
            dbr_ref[...] = dbr
            din_ref[...] = _dot(dbr, w_ref[...])

    ins = [("t", dx, D, 0), ("t", y1, D, 0), ("t", u, GATE_W, 0), ("t", a2, POOLW, 0), ("t", yb, POOLW, 0), ("t", att, ATT_O, 0),
           ("w", wa), ("w", wb), ("w", wc), ("w", w_out), ("w", g_post)]
    outs = [("t", D, MXU), ("c", IN_W, GATE_W, 0, MXU), ("t", D, MXU), ("t", D, MXU), ("t", D, MXU), ("t", POOLW, F32), ("t", POOLW, F32),
            ("t", ATT_O, F32), ("a", (1, D), F32)]
    return _rows_call("merge_bwd", body, S, ts, ins, outs)


def _prenorm_bwd(name, dx_res, du, wt, x, g_pre, ts=256, lead=0):
    S = x.shape[0]
    N = du.shape[1]

    def body(i, g, dx_ref, du_ref, wt_ref, x_ref, g_ref, o_ref, dg_ref):
        if lead:
            dhv = _dot(du_ref[:, 0:lead], wt_ref[N - lead:N, :]) + _dot(du_ref[:, lead:N], wt_ref[0:N - lead, :])
        else:
            dhv = _dot(du_ref[...], wt_ref[...])
        xv = x_ref[...]
        dxn, r = _rms_bwd(dhv * g_ref[...], xv)
        o_ref[...] = dx_ref[...] + dxn
        _acc(dg_ref, g, jnp.sum(dhv * (xv * r), axis=0, keepdims=True))

    ins = [("t", dx_res, D, 0), ("t", du, N, 0), ("w", wt), ("t", x, D, 0), ("w", g_pre)]
    return _rows_call(name, body, S, ts, ins, [("t", D, F32), ("a", (1, D), F32)], vmem_mb=52)


def _mem_heads(qm, kv_ref):
    out = []
    for h in range(4):
        q = qm[:, h * 128:(h + 1) * 128].astype(MXU)
        k = kv_ref[:, h * 128:(h + 1) * 128]
        v = kv_ref[:, MEM_W + h * 128:MEM_W + (h + 1) * 128]
        sc = _dot_nt(q, k) * MEM_SCALE
        e = jnp.exp(sc - jnp.max(sc, axis=1, keepdims=True))
        out.append((e / jnp.sum(e, axis=1, keepdims=True), q, k, v))
    return out


def _mem_fwd(x1, kv, g_pre, w_mq, w_mo, g_post, ts=256):
    S = x1.shape[0]

    def body(i, g, x_ref, kv_ref, gq_ref, wq_ref, wo_ref, gp_ref, om_ref, h_ref, y_ref, xo_ref):
        x = x_ref[...]
        hb = _rms(x, gq_ref[...])[0].astype(MXU)
        h_ref[...] = hb
        qm = _dot(hb, wq_ref[...])
        om = jnp.concatenate([_dot(p.astype(MXU), v) for p, _, _, v in _mem_heads(qm, kv_ref)], axis=1).astype(MXU)
        om_ref[...] = om
        y = _dot_nt(om, wo_ref[...])
        y_ref[...] = y
        xo_ref[...] = x + _rms(y, gp_ref[...])[0]

    ins = [("t", x1, D, 0), ("w", kv), ("w", g_pre), ("w", w_mq), ("w", w_mo), ("w", g_post)]
    return _rows_call("mem_fwd", body, S, ts, ins, [("t", MEM_W, MXU), ("t", D, MXU), ("t", D, F32), ("t", D, F32)])


def _mem_bwd(dx2, ym, x1, kv, g_pre, w_mq, w_mo, g_post, ts=256):
    S = x1.shape[0]

    def body(i, g, dx_ref, y_ref, x_ref, kv_ref, gq_ref, wq_ref, wo_ref, gp_ref, dy_ref, dq_ref, dxo_ref, dgp_ref, dgq_ref, dkv_ref):
        dxv, y, x = dx_ref[...], y_ref[...], x_ref[...]
        dy, r = _rms_bwd(dxv * gp_ref[...], y)
        _acc(dgp_ref, g, jnp.sum(dxv * (y * r), axis=0, keepdims=True))
        dyb = dy.astype(MXU)
        dy_ref[...] = dyb
        dom = _dot(dyb, wo_ref[...])
        h, r1 = _rms(x, gq_ref[...])
        qm = _dot(h.astype(MXU), wq_ref[...])
        dqs = []

        @pl.when(g == 0)
        def _():
            dkv_ref[...] = jnp.zeros_like(dkv_ref)

        for hh, (p, q, k, v) in enumerate(_mem_heads(qm, kv_ref)):
            doh = dom[:, hh * 128:(hh + 1) * 128].astype(MXU)
            dp = _dot_nt(doh, v)
            dsc = (p * (dp - jnp.sum(dp * p, axis=1, keepdims=True)) * MEM_SCALE).astype(MXU)
            dqs.append(_dot(dsc, k))
            dkv_ref[:, hh * 128:(hh + 1) * 128] += _dot_tn(dsc, q)
            dkv_ref[:, MEM_W + hh * 128:MEM_W + (hh + 1) * 128] += _dot_tn(p.astype(MXU), doh)
        dq = jnp.concatenate(dqs, axis=1).astype(MXU)
        dq_ref[...] = dq
        dh = _dot_nt(dq, wq_ref[...])
        _acc(dgq_ref, g, jnp.sum(dh * (x * r1), axis=0, keepdims=True))
        dxo_ref[...] = dxv + _rms_bwd(dh * gq_ref[...], x)[0]

    ins = [("t", dx2, D, 0), ("t", ym, D, 0), ("t", x1, D, 0), ("w", kv), ("w", g_pre), ("w", w_mq), ("w", w_mo), ("w", g_post)]
    outs = [("t", D, MXU), ("t", MEM_W, MXU), ("t", D, F32), ("a", (1, D), F32), ("a", (1, D), F32), ("a", (256, D), F32)]
    return _rows_call("mem_bwd", body, S, ts, ins, outs)


def _gain_grad(name, dn, x):
    n = x.shape[0]

    def body(i, g, dn_ref, x_ref, o_ref):
        xv = x_ref[...]
        r = lax.rsqrt(jnp.mean(xv * xv, axis=-1, keepdims=True) + EPS)
        o_ref[...] = jnp.sum(dn_ref[...] * (xv * r), axis=0, keepdims=True)

    return _rows_call(name, body, n, n, [("t", dn, D, 0), ("t", x, D, 0)], [("a", (1, D), F32)])[0]


def _ffn_fwd(x2, u3, conv_f, w_down, g_post, ts=256):
    S = x2.shape[0]

    def body(i, g, x_ref, ua_ref, ub_ref, cw_ref, wd_ref, gp_ref, act_ref, y_ref, xo_ref, cu):
        @pl.when(g == 0)
        def _():
            cu[...] = jnp.zeros_like(cu)

        ua = ua_ref[...].astype(F32)
        c, _, _ = _conv3(ua, cu[...], cw_ref[...])
        act = (c * jax.nn.sigmoid(c) * ub_ref[...].astype(F32)).astype(MXU)
        act_ref[...] = act
        y = _dot(act, wd_ref[...])
        y_ref[...] = y
        xo_ref[...] = x_ref[...] + _rms(y, gp_ref[...])[0]
        cu[...] = ua[ts - 8:]

    ins = [("t", x2, D, 0), ("t", u3, D_FF, 0), ("t", u3, D_FF, 1), ("w", conv_f), ("w", w_down), ("w", g_post)]
    return _rows_call("ffn_fwd", body, S, ts, ins, [("t", D_FF, MXU), ("t", D, F32), ("t", D, F32)],
                      scratch=[pltpu.VMEM((8, D_FF), F32)], vmem_mb=56)


def _ffn_bwd(dx3, y3, u3, conv_f, w_down, g_post, ts=128):
    S = dx3.shape[0]

    def body(i, g, dx_ref, y_ref, ua_ref, ub_ref, uap_ref, cw_ref, wd_ref, gp_ref, dy_ref, du_ref, dgp_ref, dcw_ref, cdc):
        @pl.when(g == 0)
        def _():
            cdc[...] = jnp.zeros_like(cdc)

        dxv, y = dx_ref[...], y_ref[...]
        dy, r = _rms_bwd(dxv * gp_ref[...], y)
        _acc(dgp_ref, g, jnp.sum(dxv * (y * r), axis=0, keepdims=True))
        dyb = dy.astype(MXU)
        dy_ref[...] = dyb
        dact = _dot_nt(dyb, wd_ref[...])
        ua, w = ua_ref[...].astype(F32), cw_ref[...]
        c, u1, u2 = _conv3(ua, uap_ref[...].astype(F32)[8:16] * (i > 0).astype(F32), w)
        sg = jax.nn.sigmoid(c)
        du_ref[:, D_FF:2 * D_FF] = (dact * (c * sg)).astype(du_ref.dtype)
        dc = dact * ub_ref[...].astype(F32) * (sg * (1.0 + c * (1.0 - sg)))
        du_ref[:, 0:D_FF] = _conv3_t(dc, cdc[...], w).astype(du_ref.dtype)
        dw = jnp.concatenate([jnp.sum(dc * u2, axis=0, keepdims=True), jnp.sum(dc * u1, axis=0, keepdims=True),
                              jnp.sum(dc * ua, axis=0, keepdims=True)], axis=0)
        _acc(dcw_ref, g, dw)
        cdc[...] = dc[:8]

    ins = [("t", dx3, D, 0), ("t", y3, D, 0), ("t", u3, D_FF, 0), ("t", u3, D_FF, 1), ("h", u3, 16, D_FF, 0), ("w", conv_f),
           ("w", w_down), ("w", g_post)]
    outs = [("t", D, MXU), ("t", 2 * D_FF, MXU), ("a", (1, D), F32), ("a", (3, D_FF), F32)]
    return _rows_call("ffn_bwd", body, S, ts, ins, outs, scratch=[pltpu.VMEM((8, D_FF), F32)], reverse=True, vmem_mb=56)


def _loss_head(x, target, ts=512):
    S = x.shape[0]

    def body(i, g, x_ref, t_ref, dx_ref, acc_ref):
        diff = x_ref[...] - t_ref[...]
        dx_ref[...] = diff * (1.0 / D)
        col = jnp.sum(diff * diff, axis=0, keepdims=True)
        part = col[:, 0:LANES]
        for j in range(1, D // LANES):
            part = part + col[:, j * LANES:(j + 1) * LANES]
        row = lax.broadcasted_iota(jnp.int32, (8, LANES), 0)
        _acc(acc_ref, g, jnp.where(row == 0, jnp.broadcast_to(part, (8, LANES)), 0.0))

    return _rows_call("loss_head", body, S, ts, [("t", x, D, 0), ("t", target, D, 0)], [("t", D, F32), ("a", (8, LANES), F32)])


_OPERAND_NAME = dict(w_in='w_in', w_branch_a='wa', w_branch_b='wb', w_branch_c='wc', w_out='w_out', w_mq='w_mq', w_mkv='w_mkv',
                     w_mo='w_mo', w_up='w_up', w_down='w_down')


def _big_operands(big):
    return {_OPERAND_NAME[n]: a for n, a in big.items()}


def _layer_weights(big, small, l):
    pool_w = small['pool_w'][l].astype(MXU)
    wblk = jnp.zeros((POOLW, POOLW), MXU)
    for g in range(4):
        wblk = lax.dynamic_update_slice(wblk, pool_w[g], (g * 96, g * 96))
    vec = lambda n: small[n][l].reshape(1, -1)
    return dict(
        _big_operands(big),
        wblk=wblk, pool_scale=vec('pool_scale'), conv_b=small['conv_b_w'][l], conv_f=small['conv_ffn_w'][l],
        g_mix_pre=vec('norm_mix_pre'), g_mix_post=vec('norm_mix_post'), g_mem_pre=vec('norm_mem_pre'),
        g_mem_post=vec('norm_mem_post'), g_memkv=vec('norm_memkv'), g_ffn_pre=vec('norm_ffn_pre'), g_ffn_post=vec('norm_ffn_post'))


def _layer_fwd(x0, mem, W, ctab, stab):
    sv = _layer_fwd_mix(x0, W, ctab, stab)
    return _layer_fwd_late(mem, W, sv), sv


def _layer_fwd_mix(x0, W, ctab, stab):
    sv = dict(x0=x0)
    sv['u'], sv['h1'] = _norm_mm("in_proj", x0, W['g_mix_pre'], W['w_in'], ts=1024, tn=IN_TILE, wt=True, rot=IN_ROT, out_dtype=MXU)
    sv['a2'], sv['yb'] = _poolconv_fwd(sv['u'], W['wblk'], W['pool_scale'], W['conv_b'])
    sv['qkv'] = q3, k3, v3 = _rope_perm(sv['u'], ctab, stab)
    sv['att'], sv['o'], sv['lse'] = _attn_combine([_attn_fwd(g, q3[g], k3[g], v3[g]) for g in range(3)])
    sv['merged'], sv['y1'], sv['x1'] = _merge_fwd(x0, sv['u'], sv['a2'], sv['yb'], sv['att'], W['wa'], W['wb'], W['wc'],
                                                  W['w_out'], W['g_mix_post'])
    return sv


def _layer_fwd_late(mem, W, sv):
    sv['kv'], sv['memn'] = _norm_mm("mem_kv", mem, W['g_memkv'], W['w_mkv'], ts=256, tn=D, out_dtype=MXU)
    sv['om'], sv['h2'], sv['ym'], sv['x2'] = _mem_fwd(sv['x1'], sv['kv'], W['g_mem_pre'], W['w_mq'], W['w_mo'], W['g_mem_post'])
    sv['u3'], sv['h3'] = _norm_mm("up_proj", sv['x2'], W['g_ffn_pre'], W['w_up'], ts=1024, tn=1408, wt=True, out_dtype=MXU)
    sv['act'], sv['y3'], x3 = _ffn_fwd(sv['x2'], sv['u3'], W['conv_f'], W['w_down'], W['g_ffn_post'])
    return x3


def _layer_bwd(dx3, mem, W, sv, ctab, stab):
    dx1, g = _layer_bwd_late(dx3, mem, W, sv)
    dx0, g_mix = _layer_bwd_mix(dx1, W, sv, ctab, stab)
    return dx0, {**g, **g_mix}


def _layer_bwd_late(dx3, mem, W, sv):
    g = {}
    dy3, du3, g['norm_ffn_post'], g['conv_ffn_w'] = _ffn_bwd(dx3, sv['y3'], sv['u3'], W['conv_f'], W['w_down'], W['g_ffn_post'])
    g['w_down'] = _mm_tn("dw_down", sv['act'], dy3, cap_k=256)
    g['w_up'] = _mm_tn("dw_up", du3, sv['h3'])
    dx2, g['norm_ffn_pre'] = _prenorm_bwd("ffn_pre_bwd", dx3, du3, W['w_up'], sv['x2'], W['g_ffn_pre'])
    dym, dqm, dx1, g['norm_mem_post'], g['norm_mem_pre'], dkv = _mem_bwd(dx2, sv['ym'], sv['x1'], sv['kv'], W['g_mem_pre'],
                                                                       W['w_mq'], W['w_mo'], W['g_mem_post'])
    g['w_mo'] = _mm_tn("dw_mo", dym, sv['om'])
    g['w_mq'] = _mm_tn("dw_mq", sv['h2'], dqm)
    dkvb = dkv.astype(MXU)
    g['w_mkv'] = _mm_tn("dw_mkv", sv['memn'], dkvb)
    g['norm_memkv'] = _gain_grad("memkv_gain", _mm_nt("d_memn", dkvb, W['w_mkv'], ts=256, tn=512), mem)
    return dx1, g


def _layer_bwd_mix(dx1, W, sv, ctab, stab):
    du, g = _layer_bwd_mixers(dx1, W, sv, ctab, stab)
    g['w_in'] = _dw_in(du, sv)
    dx0, g['norm_mix_pre'] = _mix_pre_bwd(dx1, du, W, sv)
    return dx0, g


def _dw_in(du, sv):
    return _mm_tn("dw_in", du, sv['h1'], cap_k=IN_TILE, rot=IN_ROT)


def _mix_pre_bwd(dx1, du, W, sv):
    return _prenorm_bwd("mix_pre_bwd", dx1, du, W['w_in'], sv['x0'], W['g_mix_pre'], lead=GATE_W)


def _layer_bwd_mixers(dx1, W, sv, ctab, stab):
    g = {}
    dy1, du, dbra, dbrb, dbrc, da2, dyb, datt, g['norm_mix_post'] = _merge_bwd(
        dx1, sv['y1'], sv['u'], sv['a2'], sv['yb'], sv['att'], W['wa'], W['wb'], W['wc'], W['w_out'], W['g_mix_post'])
    g['w_out'] = _mm_tn("dw_out", sv['merged'], dy1)
    g['w_branch_a'] = _mm_tn("dw_a", dbra, sv['a2'])
    g['w_branch_b'] = _mm_tn("dw_b", dbrb, sv['yb'])
    g['w_branch_c'] = _mm_tn("dw_c", dbrc, sv['att'])
    du, g['pool_scale'], dwblk, g['conv_b_w'] = _poolconv_bwd(sv['u'], da2, dyb, du, W['wblk'], W['pool_scale'], W['conv_b'])
    g['pool_w'] = jnp.stack([dwblk[k * 96:(k + 1) * 96, k * 96:(k + 1) * 96] for k in range(4)])
    q3, k3, v3 = sv['qkv']
    do3, dl3, lse3 = _attn_bwd_prep(datt, sv['o'], sv['lse'])
    dqkv3 = [_attn_bwd(i, q3[i], k3[i], v3[i], do3[i], dl3[i], lse3[i]) for i in range(3)]
    du = _rope_unperm_bwd([[t[which] for t in dqkv3] for which in range(3)], du, ctab, stab)
    return du, g


def _local_step(x, mem, positions, target, big, small):
    ctab, stab = _rope_tables(positions)
    Ws = [_layer_weights(big[l], small, l) for l in range(DEPTH)]
    saved = []
    for l in range(DEPTH):
        x, sv = _layer_fwd(x, mem, Ws[l], ctab, stab)
        saved.append(sv)
    dx, acc = _loss_head(x, target)
    loss = jnp.sum(acc) * (0.5 / D)
    grads = [None] * DEPTH
    for l in reversed(range(DEPTH)):
        dx, grads[l] = _layer_bwd(dx, mem, Ws[l], saved[l], ctab, stab)
    return loss, dx, grads


_HBM = pl.BlockSpec(memory_space=pl.ANY)
MESH_ID = pl.DeviceIdType.MESH


def _all_gather(name, xs):
    n = len(xs)

    def body(*refs):
        x_refs, out_refs = refs[:n], refs[n:2 * n]
        send_sems, recv_sems, local_sems = refs[2 * n:]
        x, y, c = lax.axis_index("x"), lax.axis_index("y"), lax.axis_index("c")
        me, sibling = (x, y, c), (x, y, 1 - c)
        chips = [(1 - x, y), (x, 1 - y), (1 - x, 1 - y)]

        def slot(a, p):
            return out_refs[a].at[4 * p[0] + 2 * p[1] + p[2]]

        def copy(a, k, block, to, src=None):
            return pltpu.make_async_remote_copy(src_ref=slot(a, block) if src is None else src, dst_ref=slot(a, block),
                                                send_sem=send_sems.at[a, k], recv_sem=recv_sems.at[a, k], device_id=to,
                                                device_id_type=MESH_ID)

        started = []
        for a in range(n):
            mine = pltpu.make_async_copy(x_refs[a], slot(a, me), local_sems.at[a])
            mine.start()
            started.append(mine)
        first = []
        for a in range(n):
            first.append(copy(a, 0, me, sibling, src=x_refs[a]))
            first += [copy(a, 1 + j, me, (*chip, c), src=x_refs[a]) for j, chip in enumerate(chips)]
        for cp in first:
            cp.start()
        passed = []
        for j, chip in enumerate(chips):
            for a in range(n):
                copy(a, 1 + j, (*chip, c), me).wait_recv()
                fw = copy(a, 4 + j, (*chip, c), sibling)
                fw.start()
                passed.append(fw)
        for a in range(n):
            copy(a, 0, sibling, me).wait_recv()
            for j, chip in enumerate(chips):
                copy(a, 4 + j, (*chip, 1 - c), me).wait_recv()
        for cp in first + passed:
            cp.wait_send()
        for mine in started:
            mine.wait()

    return pl.pallas_call(
        body, out_shape=[jax.ShapeDtypeStruct((N_DEV,) + x.shape, x.dtype) for x in xs], in_specs=[_HBM] * n, out_specs=[_HBM] * n,
        scratch_shapes=[pltpu.SemaphoreType.DMA((n, 7)), pltpu.SemaphoreType.DMA((n, 7)), pltpu.SemaphoreType.DMA((n,))],
        name=name)(*xs)


def _exchange(name, gs):
    n = len(gs)

    def body(*refs):
        g_refs, out_refs = refs[:n], refs[n:2 * n]
        send_sems, recv_sems, local_sems = refs[2 * n:]
        x, y, c = lax.axis_index("x"), lax.axis_index("y"), lax.axis_index("c")
        me = 4 * x + 2 * y + c
        copies = []
        for a in range(n):
            mine = pltpu.make_async_copy(g_refs[a].at[me], out_refs[a].at[me], local_sems.at[a])
            mine.start()
            copies.append(mine)
        for r in range(1, N_DEV):
            px, py, pc = x ^ ((r >> 2) & 1), y ^ ((r >> 1) & 1), c ^ (r & 1)
            for a in range(n):
                cp = pltpu.make_async_remote_copy(src_ref=g_refs[a].at[4 * px + 2 * py + pc], dst_ref=out_refs[a].at[me],
                                                  send_sem=send_sems.at[a, r - 1], recv_sem=recv_sems.at[a, r - 1],
                                                  device_id=(px, py, pc), device_id_type=MESH_ID)
                cp.start()
                copies.append(cp)
        for cp in copies:
            cp.wait()

    return pl.pallas_call(
        body, out_shape=[jax.ShapeDtypeStruct(g.shape, g.dtype) for g in gs], in_specs=[_HBM] * n, out_specs=[_HBM] * n,
        scratch_shapes=[pltpu.SemaphoreType.DMA((n, N_DEV - 1)), pltpu.SemaphoreType.DMA((n, N_DEV - 1)), pltpu.SemaphoreType.DMA((n,))],
        name=name)(*gs)


_SEM = pl.BlockSpec(memory_space=pltpu.SEMAPHORE)
_IN_HBM = pl.BlockSpec(memory_space=pltpu.HBM)
_SIDE_EFFECT = pltpu.SideEffectType.DATAFLOW_SIDE_EFFECTING


def _push_copies(src_refs, land_refs, send_sems, recv_sems, per_peer):
    x, y, c = lax.axis_index("x"), lax.axis_index("y"), lax.axis_index("c")
    me = 4 * x + 2 * y + c
    copies = []
    for r in range(1, N_DEV):
        px, py, pc = x ^ ((r >> 2) & 1), y ^ ((r >> 1) & 1), c ^ (r & 1)
        for a, (s, d) in enumerate(zip(src_refs, land_refs)):
            k = a * (N_DEV - 1) + r - 1
            copies.append(pltpu.make_async_remote_copy(src_ref=s.at[4 * px + 2 * py + pc] if per_peer else s, dst_ref=d.at[me],
                                                       send_sem=send_sems.at[k], recv_sem=recv_sems.at[k],
                                                       device_id=(px, py, pc), device_id_type=MESH_ID))
    return copies


def _push_start(name, srcs, per_peer, after):
    n = len(srcs)
    lands = [lax.empty((N_DEV,) + (s.shape[1:] if per_peer else s.shape), s.dtype) for s in srcs]

    def body(*refs):
        for cp in _push_copies(refs[:n], refs[n:2 * n], refs[2 * n + 1], refs[2 * n + 2], per_peer):
            cp.start()
        refs[-1][...] = jnp.zeros_like(refs[-1])

    hbm = [pltpu.HBM(a.shape, a.dtype) for a in (*srcs, *lands)]
    sems = pltpu.SemaphoreType.DMA((n * (N_DEV - 1),))
    out = pl.pallas_call(
        body, name=name, out_shape=(sems, sems, *hbm, jax.ShapeDtypeStruct((8, LANES), F32)),
        in_specs=[_IN_HBM] * (2 * n) + [pl.BlockSpec(memory_space=pl.ANY)],
        out_specs=(_SEM, _SEM, *[_IN_HBM] * (2 * n), pl.BlockSpec(memory_space=pltpu.VMEM)),
        input_output_aliases={a: 2 + a for a in range(2 * n)},
        compiler_params=pltpu.CompilerParams(has_side_effects=_SIDE_EFFECT),
    )(*[pltpu.with_memory_space_constraint(a, pltpu.HBM) for a in (*srcs, *lands)], after)
    return out[0], out[1], out[2:2 + n], out[2 + n:2 + 2 * n], out[-1]


def _push_wait(name, started, per_peer, after):
    send_sems, recv_sems, srcs, lands, _ = started
    n = len(srcs)

    def body(*refs):
        for cp in _push_copies(refs[:n], refs[n:2 * n], refs[2 * n], refs[2 * n + 1], per_peer):
            cp.wait_send()
            cp.wait_recv()

    out = pl.pallas_call(
        body, name=name, out_shape=[pltpu.HBM(a.shape, a.dtype) for a in (*srcs, *lands)],
        in_specs=[_IN_HBM] * (2 * n) + [_SEM, _SEM, pl.BlockSpec(memory_space=pl.ANY)], out_specs=[_IN_HBM] * (2 * n),
        input_output_aliases={a: a for a in range(2 * n)},
        compiler_params=pltpu.CompilerParams(has_side_effects=_SIDE_EFFECT),
    )(*srcs, *lands, send_sems, recv_sems, after)
    me = _my_slot()
    own = [lax.dynamic_index_in_dim(s, me, 0, keepdims=False) if per_peer else s for s in out[:n]]
    return _with_own(out[n:], own, me)


def _my_slot():
    return 4 * lax.axis_index("x") + 2 * lax.axis_index("y") + lax.axis_index("c")


def _row_tile(rows, cols, budget):
    if rows * cols * 4 <= budget or rows % 16:
        return rows
    best = 16
    for t in range(16, rows + 1, 16):
        if rows % t == 0 and t * cols * 4 <= budget:
            best = t
    return best


def _sum_slots(name, recv):
    _, R, C = recv.shape
    tr = _row_tile(R, C, 1 << 20)

    def body(r_ref, o_ref):
        g = r_ref[0].astype(F32)
        for k in range(1, N_DEV):
            g = g + r_ref[k].astype(F32)
        o_ref[...] = g

    return pl.pallas_call(body, grid=(R // tr,), in_specs=[pl.BlockSpec((N_DEV, tr, C), lambda i: (0, i, 0))],
                          out_specs=pl.BlockSpec((tr, C), lambda i: (i, 0)), out_shape=jax.ShapeDtypeStruct((R, C), F32),
                          compiler_params=_params(("arbitrary",), 32), name=name)(recv)


def _adamw(name, g, w, m, v):
    shape = w.shape
    R, C = shape[-2], shape[-1]
    view = (-1, R, C)
    L = w.size // (R * C)
    tr = _row_tile(R, C, 1 << 20)
    c1 = 1.0 - ADAM_B1 ** ADAM_STEP
    c2 = 1.0 - ADAM_B2 ** ADAM_STEP

    def body(g_ref, w_ref, m_ref, v_ref, d_ref, mo_ref, vo_ref):
        gv = g_ref[...]
        mn = ADAM_B1 * m_ref[...] + (1.0 - ADAM_B1) * gv
        vn = ADAM_B2 * v_ref[...] + (1.0 - ADAM_B2) * (gv * gv)
        mo_ref[...] = mn
        vo_ref[...] = vn
        d_ref[...] = -ADAM_LR * ((mn / c1) / (jnp.sqrt(vn / c2) + ADAM_EPS) + ADAM_WD * w_ref[...])

    blk = pl.BlockSpec((None, tr, C), lambda l, i: (l, i, 0))
    shp = jax.ShapeDtypeStruct((L, R, C), F32)
    outs = pl.pallas_call(body, grid=(L, R // tr), in_specs=[blk, blk, blk, blk], out_specs=[blk, blk, blk], out_shape=[shp, shp, shp],
                          compiler_params=_params(("arbitrary", "arbitrary"), 32), name=name)(*[a.reshape(view) for a in (g, w, m, v)])
    return [o.reshape(shape) for o in outs]


def _pad_flat(a, n):
    a = a.reshape(-1)
    return jnp.pad(a, (0, n - a.shape[0]))


def _seg(n):
    return -(-n // FLAT_ALIGN) * FLAT_ALIGN


def _to_blocks(full, axis):
    shp = full.shape
    return jnp.moveaxis(full.reshape(shp[:axis] + (N_DEV, shp[axis] // N_DEV) + shp[axis + 1:]), axis, 0)


def _from_blocks(blocks, axis):
    b = jnp.moveaxis(blocks, 0, axis)
    shp = b.shape
    return b.reshape(shp[:axis] + (shp[axis] * shp[axis + 1],) + shp[axis + 2:])


def _as_rows(shard, n):
    return shard.T if SHARD_AXIS[n] == 2 else shard


def _with_own(lands, own, me):
    return [lax.dynamic_update_slice(land, o[None], (me, 0, 0)) for land, o in zip(lands, own)]


def kernel(x, mem, positions, norm_mix_pre, norm_mix_post, w_in, pool_w, pool_scale, conv_b_w, w_branch_a, w_branch_b, w_branch_c, w_out, norm_mem_pre, norm_mem_post, norm_memkv, w_mq, w_mkv, w_mo, norm_ffn_pre, norm_ffn_post, w_up, conv_ffn_w, w_down, loss_target, m_norm_mix_pre, m_norm_mix_post, m_w_in, m_pool_w, m_pool_scale, m_conv_b_w, m_w_branch_a, m_w_branch_b, m_w_branch_c, m_w_out, m_norm_mem_pre, m_norm_mem_post, m_norm_memkv, m_w_mq, m_w_mkv, m_w_mo, m_norm_ffn_pre, m_norm_ffn_post, m_w_up, m_conv_ffn_w, m_w_down, v_norm_mix_pre, v_norm_mix_post, v_w_in, v_pool_w, v_pool_scale, v_conv_b_w, v_w_branch_a, v_w_branch_b, v_w_branch_c, v_w_out, v_norm_mem_pre, v_norm_mem_post, v_norm_memkv, v_w_mq, v_w_mkv, v_w_mo, v_norm_ffn_pre, v_norm_ffn_post, v_w_up, v_conv_ffn_w, v_w_down):
    w = dict(norm_mix_pre=norm_mix_pre, norm_mix_post=norm_mix_post, w_in=w_in, pool_w=pool_w, pool_scale=pool_scale, conv_b_w=conv_b_w, w_branch_a=w_branch_a, w_branch_b=w_branch_b, w_branch_c=w_branch_c, w_out=w_out, norm_mem_pre=norm_mem_pre, norm_mem_post=norm_mem_post, norm_memkv=norm_memkv, w_mq=w_mq, w_mkv=w_mkv, w_mo=w_mo, norm_ffn_pre=norm_ffn_pre, norm_ffn_post=norm_ffn_post, w_up=w_up, conv_ffn_w=conv_ffn_w, w_down=w_down)
    m = dict(norm_mix_pre=m_norm_mix_pre, norm_mix_post=m_norm_mix_post, w_in=m_w_in, pool_w=m_pool_w, pool_scale=m_pool_scale, conv_b_w=m_conv_b_w, w_branch_a=m_w_branch_a, w_branch_b=m_w_branch_b, w_branch_c=m_w_branch_c, w_out=m_w_out, norm_mem_pre=m_norm_mem_pre, norm_mem_post=m_norm_mem_post, norm_memkv=m_norm_memkv, w_mq=m_w_mq, w_mkv=m_w_mkv, w_mo=m_w_mo, norm_ffn_pre=m_norm_ffn_pre, norm_ffn_post=m_norm_ffn_post, w_up=m_w_up, conv_ffn_w=m_conv_ffn_w, w_down=m_w_down)
    v = dict(norm_mix_pre=v_norm_mix_pre, norm_mix_post=v_norm_mix_post, w_in=v_w_in, pool_w=v_pool_w, pool_scale=v_pool_scale, conv_b_w=v_conv_b_w, w_branch_a=v_w_branch_a, w_branch_b=v_w_branch_b, w_branch_c=v_w_branch_c, w_out=v_w_out, norm_mem_pre=v_norm_mem_pre, norm_mem_post=v_norm_mem_post, norm_memkv=v_norm_memkv, w_mq=v_w_mq, w_mkv=v_w_mkv, w_mo=v_w_mo, norm_ffn_pre=v_norm_ffn_pre, norm_ffn_post=v_norm_ffn_post, w_up=v_w_up, conv_ffn_w=v_conv_ffn_w, w_down=v_w_down)

    me = _my_slot()
    mix_big = [n for n in BIG if n not in LATE_BIG]
    block = lambda names, l: [_as_rows(w[n][l], n).astype(MXU) for n in names]
    conv = jnp.concatenate([_pad_flat(w[n], _seg(w[n].size)) for n in F32_GATHERED]).reshape(-1, LANES)
    got0 = _all_gather("weights_all_gather_0", block(mix_big, 0) + [conv])
    conv_all = got0[-1].reshape(N_DEV, -1)
    small, off = {n: w[n] for n in WEIGHTS if n not in SHARD_AXIS}, 0
    for n in F32_GATHERED:
        small[n] = _from_blocks(conv_all[:, off:off + w[n].size].reshape((N_DEV,) + w[n].shape), 2)
        off += _seg(w[n].size)
    whole = lambda names, got: {n: o.reshape(-1, o.shape[-1]) for n, o in zip(names, got)}
    pushes, after = {}, got0[0]
    for names, l in ((LATE_BIG, 0), (mix_big, 1), (LATE_BIG, 1)):
        pushes[names is LATE_BIG, l] = _push_start(f"weights_push_start_{l}{'b' if names is LATE_BIG else 'a'}", block(names, l), False, after)
        after = pushes[names is LATE_BIG, l][4]

    def arrived(late, l, done):
        names = LATE_BIG if late else mix_big
        return _big_operands(whole(names, _push_wait(f"weights_push_wait_{l}{'b' if late else 'a'}", pushes[late, l], False, done)))

    ctab, stab = _rope_tables(positions[0])
    W0 = _layer_weights(whole(mix_big, got0), small, 0)
    sv0 = _layer_fwd_mix(x[0], dict(W0, g_mix_pre=W0['g_mix_pre'] + after[0, 0]), ctab, stab)
    W0.update(arrived(True, 0, sv0['x1']))
    x1 = _layer_fwd_late(mem[0], W0, sv0)
    W1 = _layer_weights({}, small, 1)
    W1.update(arrived(False, 1, x1))
    sv1 = _layer_fwd_mix(x1, W1, ctab, stab)
    W1.update(arrived(True, 1, sv1['x1']))
    x2 = _layer_fwd_late(mem[0], W1, sv1)
    dx, acc = _loss_head(x2, loss_target[0])
    loss = lax.psum(jnp.sum(acc) * (0.5 / D), MESH_AXES)
    grads = [None] * DEPTH
    dx, grads[1] = _layer_bwd(dx, mem[0], W1, sv1, ctab, stab)
    sent = [None, [grads[1][n].reshape(N_DEV, -1, grads[1][n].shape[-1]) for n in BIG]]
    push_g = _push_start("grads_push_start_1", sent[1], True, dx)
    dx, g_late = _layer_bwd_late(dx, mem[0], dict(W0, g_ffn_post=W0['g_ffn_post'] + push_g[4][0, 0]), sv0)
    sent_late = [g_late[n].reshape(N_DEV, -1, g_late[n].shape[-1]) for n in LATE_BIG]
    push_l = _push_start("grads_push_start_0", sent_late, True, dx)
    du, g_mix = _layer_bwd_mixers(dx, dict(W0, g_mix_post=W0['g_mix_post'] + push_l[4][0, 0]), sv0, ctab, stab)
    g_mix['w_in'] = _dw_in(du, sv0)
    sent_in = [g_mix['w_in'].reshape(N_DEV, -1, D)]
    push_i = _push_start("grads_push_start_in", sent_in, True, du)
    dx, g_mix['norm_mix_pre'] = _mix_pre_bwd(dx, du, dict(W0, g_mix_pre=W0['g_mix_pre'] + push_i[4][0, 0]), sv0)
    grads[0] = {**g_late, **g_mix}
    own = lambda s: [lax.dynamic_index_in_dim(a, me, 0, keepdims=False) for a in s]
    recv1 = _push_wait("grads_push_wait_1", push_g, True, dx)
    recv_late = _push_wait("grads_push_wait_0", push_l, True, dx)
    mix_rest = [n for n in mix_big if n != 'w_in']

    misc_names = [n for n in WEIGHTS if n not in BIG]
    stacked = {n: jnp.stack([grads[l][n].reshape(small[n].shape[1:]) for l in range(DEPTH)]) for n in misc_names}
    rows = [(_to_blocks(stacked[n], 2) if n in SHARD_AXIS else jnp.broadcast_to(stacked[n][None], (N_DEV,) + stacked[n].shape))
            for n in misc_names]
    segs = [_seg(w[n].size) for n in misc_names]
    misc = jnp.concatenate([jnp.pad(r.reshape(N_DEV, -1), ((0, 0), (0, s - r[0].size))) for r, s in zip(rows, segs)],
                           axis=1).reshape(N_DEV, -1, LANES)
    recv_mix = _exchange("grad_exchange_0", [g_mix[n].reshape(N_DEV, -1, g_mix[n].shape[-1]) for n in mix_rest] + [misc])
    g_out, per_layer = {}, {}
    for l, names, recv in ((1, BIG, recv1), (0, LATE_BIG, recv_late), (0, mix_rest, recv_mix)):
        for n, r in zip(names, recv):
            per_layer[n, l] = _sum_slots(f"sum_{n}_{l}", r)
    misc_sum = _sum_slots("sum_misc", recv_mix[-1]).reshape(-1)
    off = 0
    for n, s in zip(misc_names, segs):
        g_out[n] = misc_sum[off:off + w[n].size].reshape(w[n].shape)
        off += s

    swap = lambda a: jnp.swapaxes(a, 1, 2)

    def update(n):
        if n not in BIG:
            return [g_out[n], *_adamw(f"adamw_{n}", g_out[n], w[n], m[n], v[n])]
        g = jnp.stack([per_layer[n, l] for l in range(DEPTH)])
        if SHARD_AXIS[n] == 2 and w[n].shape[2] % LANES:
            return [swap(a) for a in (g, *_adamw(f"adamw_{n}", g, swap(w[n]), swap(m[n]), swap(v[n])))]
        g = swap(g) if SHARD_AXIS[n] == 2 else g
        return [g, *_adamw(f"adamw_{n}", g, w[n], m[n], v[n])]

    done = {n: update(n) for n in WEIGHTS if n != 'w_in'}
    recv_in = _push_wait("grads_push_wait_in", push_i, True, done[WEIGHTS[-1]][1])
    per_layer['w_in', 0] = _sum_slots("sum_w_in_0", recv_in[0])
    done['w_in'] = update('w_in')
    return (loss, dx[None], *[done[n][k] for k in range(4) for n in WEIGHTS])
```

```python
import jax
import jax.numpy as jnp
from jax import lax
from jax.experimental import pallas as pl
from jax.experimental.pallas import tpu as pltpu

F32 = jnp.float32
MXU = jnp.bfloat16
HI = lax.Precision.HIGHEST

D = 1024
DEPTH = 2
POOLW = 384
ATT_W = 768
ATT_O = 256
GATE_W = 3 * D
IN_W = 6912
IN_TILE = 768
IN_ROT = (IN_W - GATE_W) // IN_TILE
MEM_W = 512
D_FF = 2816
EPS = 1e-6
ROPE_THETA = 500000.0
QB = 128
DILS = (1, 4, 16)
NEG = -1e30
MEM_SCALE = 128 ** -0.5
ATT_SCALE = 0.125

ADAM_LR, ADAM_B1, ADAM_B2, ADAM_EPS, ADAM_WD, ADAM_STEP = 0.001, 0.9, 0.999, 1e-08, 0.01, 10

N_DEV = 8
MESH_AXES = ("x", "y", "c")
LANES = 128
FLAT_ALIGN = 2048
ROW_TILE = 1024

WEIGHTS = ['norm_mix_pre', 'norm_mix_post', 'w_in', 'pool_w', 'pool_scale', 'conv_b_w', 'w_branch_a', 'w_branch_b',
           'w_branch_c', 'w_out', 'norm_mem_pre', 'norm_mem_post', 'norm_memkv', 'w_mq', 'w_mkv', 'w_mo',
           'norm_ffn_pre', 'norm_ffn_post', 'w_up', 'conv_ffn_w', 'w_down']
SHARD_AXIS = {'w_in': 2, 'conv_b_w': 2, 'w_branch_a': 2, 'w_branch_b': 2, 'w_branch_c': 2, 'w_out': 1, 'w_mq': 1,
              'w_mkv': 1, 'w_mo': 2, 'w_up': 2, 'conv_ffn_w': 2, 'w_down': 1}
F32_GATHERED = ('conv_b_w', 'conv_ffn_w')
BIG = [n for n in WEIGHTS if n in SHARD_AXIS and n not in F32_GATHERED]
LATE_BIG = ['w_mq', 'w_mkv', 'w_mo', 'w_up', 'w_down']
MERGE_BIG = ['w_branch_a', 'w_branch_b', 'w_branch_c', 'w_out']


VMEM_LIMIT_MB = 60


def _params(sem, vmem_mb):
    del vmem_mb
    return pltpu.CompilerParams(dimension_semantics=sem, vmem_limit_bytes=VMEM_LIMIT_MB << 20)


def _dot(a, b, prec=None):
    return lax.dot_general(a, b, (((1,), (0,)), ((), ())), preferred_element_type=F32, precision=prec)


def _dot_nt(a, b, prec=None):
    return lax.dot_general(a, b, (((1,), (1,)), ((), ())), preferred_element_type=F32, precision=prec)


def _dot_tn(a, b, prec=None):
    return lax.dot_general(a, b, (((0,), (0,)), ((), ())), preferred_element_type=F32, precision=prec)


def _tile(n, cap):
    if n <= cap:
        return n
    best = None
    for t in range(LANES, cap + 1, LANES):
        if n % t == 0:
            best = t
    assert best is not None, (n, cap)
    return best


def _rms(x, g):
    r = lax.rsqrt(jnp.mean(x * x, axis=-1, keepdims=True) + EPS)
    return x * r * g, r


def _rms_bwd(w, y):
    r = lax.rsqrt(jnp.mean(y * y, axis=-1, keepdims=True) + EPS)
    return r * w - y * (r * r * r) * jnp.mean(w * y, axis=-1, keepdims=True), r


def _rows_call(name, body, n_rows, ts, ins, outs, scratch=(), reverse=False, vmem_mb=48, aliases=None):
    nt = n_rows // ts
    assert nt * ts == n_rows

    def tile_of(g):
        return (nt - 1 - g) if reverse else g

    in_specs, args = [], []
    for op in ins:
        if op[0] == "t":
            _, a, cw, cb = op
            in_specs.append(pl.BlockSpec((ts, cw), lambda g, cb=cb: (tile_of(g), cb)))
        elif op[0] == "h":
            _, a, hr, cw, cb = op
            in_specs.append(pl.BlockSpec((hr, cw), lambda g, cb=cb, k=ts // hr: (jnp.maximum(tile_of(g) * k - 1, 0), cb)))
        elif op[0] == "x":
            _, a = op
            in_specs.append(pl.BlockSpec(memory_space=pl.ANY))
        else:
            _, a = op
            in_specs.append(pl.BlockSpec(a.shape, lambda g, n=a.ndim: (0,) * n))
        args.append(a)
    out_specs, out_shape = [], []
    for op in outs:
        if op[0] == "t":
            _, cols, dt = op
            out_specs.append(pl.BlockSpec((ts, cols), lambda g: (tile_of(g), 0)))
            out_shape.append(jax.ShapeDtypeStruct((n_rows, cols), dt))
        elif op[0] == "c":
            _, total, cols, cb, dt = op
            out_specs.append(pl.BlockSpec((ts, cols), lambda g, cb=cb: (tile_of(g), cb)))
            out_shape.append(jax.ShapeDtypeStruct((n_rows, total), dt))
        else:
            _, shp, dt = op
            out_specs.append(pl.BlockSpec(shp, lambda g, n=len(shp): (0,) * n))
            out_shape.append(jax.ShapeDtypeStruct(shp, dt))

    def kern(*refs):
        g = pl.program_id(0)
        body(tile_of(g), g, *refs)

    return pl.pallas_call(kern, grid=(nt,), in_specs=in_specs, out_specs=out_specs, out_shape=out_shape,
                          scratch_shapes=list(scratch), input_output_aliases=aliases or {},
                          compiler_params=_params(("arbitrary",), vmem_mb), name=name)(*args)


def _acc(ref, g, val):
    @pl.when(g == 0)
    def _():
        ref[...] = val

    @pl.when(g != 0)
    def _():
        ref[...] += val


def _norm_mm(name, x, g, w, ts, tn, out_dtype=F32, wt=False, rot=0):
    S, K = x.shape
    N = w.shape[0] if wt else w.shape[1]
    assert wt or not rot

    def body(x_ref, g_ref, w_ref, o_ref, h_ref, hs):
        @pl.when(pl.program_id(1) == 0)
        def _():
            h, _ = _rms(x_ref[...], g_ref[...])
            hs[...] = h.astype(MXU)
            h_ref[...] = h.astype(MXU)

        o_ref[...] = (_dot_nt if wt else _dot)(hs[...], w_ref[...]).astype(out_dtype)

    w_spec = pl.BlockSpec((tn, K), lambda i, j: ((j + rot) % (N // tn), 0)) if wt else pl.BlockSpec((K, tn), lambda i, j: (0, j))
    return pl.pallas_call(
        body, grid=(S // ts, N // tn),
        in_specs=[pl.BlockSpec((ts, K), lambda i, j: (i, 0)), pl.BlockSpec((1, K), lambda i, j: (0, 0)), w_spec],
        out_specs=[pl.BlockSpec((ts, tn), lambda i, j: (i, j)), pl.BlockSpec((ts, K), lambda i, j: (i, 0))],
        out_shape=[jax.ShapeDtypeStruct((S, N), out_dtype), jax.ShapeDtypeStruct((S, K), MXU)],
        scratch_shapes=[pltpu.VMEM((ts, K), MXU)],
        compiler_params=_params(("arbitrary", "arbitrary"), 48), name=name)(x, g, w)


def _mm_nt(name, a, b, ts, tn, out_dtype=F32):
    M, K = a.shape
    N = b.shape[0]

    def body(a_ref, b_ref, o_ref):
        o_ref[...] = _dot_nt(a_ref[...], b_ref[...]).astype(out_dtype)

    return pl.pallas_call(
        body, grid=(M // ts, N // tn),
        in_specs=[pl.BlockSpec((ts, K), lambda i, j: (i, 0)), pl.BlockSpec((tn, K), lambda i, j: (j, 0))],
        out_specs=pl.BlockSpec((ts, tn), lambda i, j: (i, j)), out_shape=jax.ShapeDtypeStruct((M, N), out_dtype),
        compiler_params=_params(("arbitrary", "arbitrary"), 48), name=name)(a, b)


def _mm_tn(name, a, b, cap_k=512, cap_n=1024, out_dtype=MXU, rot=0):
    S, K = a.shape
    N = b.shape[1]
    tk, tn = _tile(K, cap_k), _tile(N, cap_n)

    def body(a_ref, b_ref, o_ref):
        o_ref[...] = _dot_tn(a_ref[...], b_ref[...]).astype(out_dtype)

    return pl.pallas_call(
        body, grid=(K // tk, N // tn),
        in_specs=[pl.BlockSpec((S, tk), lambda i, j: (0, i)), pl.BlockSpec((S, tn), lambda i, j: (0, j))],
        out_specs=pl.BlockSpec((tk, tn), lambda i, j: ((i + rot) % (K // tk), j)), out_shape=jax.ShapeDtypeStruct((K, N), out_dtype),
        compiler_params=_params(("arbitrary", "arbitrary"), 48), name=name)(a, b)


def _pool_cols(shape):
    col = lax.broadcasted_iota(jnp.int32, shape, 1)
    return col < 96, col < 192, col < 288


def _pool_select(s2, s4, s8, s16):
    c1, c2, c3 = _pool_cols(s2.shape)
    return jnp.where(c1, s2, jnp.where(c2, s4, jnp.where(c3, s8, s16)))


def _pool_cnt(t0, ts):
    c1, c2, c3 = _pool_cols((ts, POOLW))
    win = jnp.where(c1, 2, jnp.where(c2, 4, jnp.where(c3, 8, 16)))
    t = t0 + lax.broadcasted_iota(jnp.int32, (ts, POOLW), 0)
    return jnp.minimum(t + 1, win).astype(F32)


def _pooled(a, prev, t0):
    ts = a.shape[0]
    ext = jnp.concatenate([prev, a], axis=0)
    s2 = ext + pltpu.roll(ext, 1, axis=0)
    s4 = s2 + pltpu.roll(s2, 2, axis=0)
    s8 = s4 + pltpu.roll(s4, 4, axis=0)
    s16 = s8 + pltpu.roll(s8, 8, axis=0)
    sums = _pool_select(s2, s4, s8, s16)[16:]
    return sums / _pool_cnt(t0, ts) - a


def _conv3(z, prev8, w):
    ext = jnp.concatenate([prev8, z], axis=0)
    z1 = pltpu.roll(ext, 1, axis=0)[8:]
    z2 = pltpu.roll(ext, 2, axis=0)[8:]
    return w[0:1] * z2 + w[1:2] * z1 + w[2:3] * z, z1, z2


def _conv3_t(dc, next8, w):
    ts = dc.shape[0]
    ext = jnp.concatenate([dc, next8], axis=0)
    n = ts + 8
    u1 = pltpu.roll(ext, n - 1, axis=0)[:ts]
    u2 = pltpu.roll(ext, n - 2, axis=0)[:ts]
    return w[2:3] * dc + w[1:2] * u1 + w[0:1] * u2


def _poolconv_fwd(u, wblk, pool_scale, conv_b, ts=256):
    S = u.shape[0]

    def body(i, g, a_ref, bx_ref, bb_ref, bc_ref, wblk_ref, ps_ref, cw_ref, a2_ref, yb_ref, ca, cz):
        @pl.when(g == 0)
        def _():
            ca[...] = jnp.zeros_like(ca)
            cz[...] = jnp.zeros_like(cz)

        a = a_ref[...].astype(F32)
        p = _pooled(a, ca[...], i * ts)
        mixed = _dot(p.astype(MXU), wblk_ref[...])
        a2_ref[...] = (mixed * ps_ref[...]).astype(MXU)
        z = bc_ref[...].astype(F32) * bx_ref[...].astype(F32)
        conv, _, _ = _conv3(z, cz[...], cw_ref[...])
        yb_ref[...] = (bb_ref[...].astype(F32) * conv).astype(MXU)
        ca[...] = a[ts - 16:]
        cz[...] = z[ts - 8:]

    ins = [("t", u, POOLW, 8), ("t", u, POOLW, 9), ("t", u, POOLW, 10), ("t", u, POOLW, 11), ("w", wblk), ("w", pool_scale),
           ("w", conv_b)]
    return _rows_call("poolconv_fwd", body, S, ts, ins, [("t", POOLW, MXU), ("t", POOLW, MXU)],
                      scratch=[pltpu.VMEM((16, POOLW), F32), pltpu.VMEM((8, POOLW), F32)])


def _poolconv_bwd(u, d_a2, d_yb, du, wblk, pool_scale, conv_b, ts=256):
    S = u.shape[0]

    def body(i, g, a_ref, bx_ref, bb_ref, bc_ref, ap_ref, bxp_ref, bcp_ref, da2_ref, dyb_ref, wblk_ref, ps_ref, cw_ref, _,
             o_ref, dps_ref, dwb_ref, dcw_ref, ce, cdz):
        @pl.when(g == 0)
        def _():
            ce[...] = jnp.zeros_like(ce)
            cdz[...] = jnp.zeros_like(cdz)

        first = (i > 0).astype(F32)
        a = a_ref[...].astype(F32)
        p = _pooled(a, ap_ref[...].astype(F32) * first, i * ts)
        pb = p.astype(MXU)
        mixed = _dot(pb, wblk_ref[...])
        da2 = da2_ref[...]
        dmixed = (da2 * ps_ref[...]).astype(MXU)
        dp = _dot_nt(dmixed, wblk_ref[...])
        _acc(dps_ref, g, jnp.sum(da2 * mixed, axis=0, keepdims=True))
        _acc(dwb_ref, g, _dot_tn(pb, dmixed))
        e = dp / _pool_cnt(i * ts, ts)
        ext = jnp.concatenate([e, ce[...]], axis=0)
        n = ts + 16
        f2 = ext + pltpu.roll(ext, n - 1, axis=0)
        f4 = f2 + pltpu.roll(f2, n - 2, axis=0)
        f8 = f4 + pltpu.roll(f4, n - 4, axis=0)
        f16 = f8 + pltpu.roll(f8, n - 8, axis=0)
        o_ref[:, 0:POOLW] = (_pool_select(f2, f4, f8, f16)[:ts] - dp).astype(o_ref.dtype)
        ce[...] = e[:16]

        bx, bb, bc = bx_ref[...].astype(F32), bb_ref[...].astype(F32), bc_ref[...].astype(F32)
        z = bc * bx
        w = cw_ref[...]
        conv, z1, z2 = _conv3(z, (bxp_ref[...].astype(F32) * bcp_ref[...].astype(F32))[8:16] * first, w)
        dyb = dyb_ref[...]
        dconv = dyb * bb
        dz = _conv3_t(dconv, cdz[...], w)
        o_ref[:, POOLW:2 * POOLW] = (dz * bc).astype(o_ref.dtype)
        o_ref[:, 2 * POOLW:3 * POOLW] = (dyb * conv).astype(o_ref.dtype)
        o_ref[:, 3 * POOLW:4 * POOLW] = (dz * bx).astype(o_ref.dtype)
        dw = jnp.concatenate([jnp.sum(dconv * z2, axis=0, keepdims=True), jnp.sum(dconv * z1, axis=0, keepdims=True),
                              jnp.sum(dconv * z, axis=0, keepdims=True)], axis=0)
        _acc(dcw_ref, g, dw)
        cdz[...] = dconv[:8]

    ins = [("t", u, POOLW, 8), ("t", u, POOLW, 9), ("t", u, POOLW, 10), ("t", u, POOLW, 11),
           ("h", u, 16, POOLW, 8), ("h", u, 16, POOLW, 9), ("h", u, 16, POOLW, 11),
           ("t", d_a2, POOLW, 0), ("t", d_yb, POOLW, 0), ("w", wblk), ("w", pool_scale), ("w", conv_b), ("x", du)]
    outs = [("c", IN_W, 4 * POOLW, GATE_W // (4 * POOLW), MXU), ("a", (1, POOLW), F32), ("a", (POOLW, POOLW), F32), ("a", (3, POOLW), F32)]
    return _rows_call("poolconv_bwd", body, S, ts, ins, outs, aliases={len(ins) - 1: 0},
                      scratch=[pltpu.VMEM((16, POOLW), F32), pltpu.VMEM((8, POOLW), F32)], reverse=True)


def _rope_tables(positions):
    S = positions.shape[0]
    inv = ROPE_THETA ** (-jnp.arange(0, 16, 2, dtype=F32) / 16)
    ang = positions.astype(F32)[:, None] * inv
    cos, sin = jnp.cos(ang), jnp.sin(ang)
    c64 = jnp.concatenate([cos, cos, jnp.ones((S, 48), F32)], axis=1)
    s64 = jnp.concatenate([-sin, sin, jnp.zeros((S, 48), F32)], axis=1)
    return jnp.concatenate([c64, c64], axis=1), jnp.concatenate([s64, s64], axis=1)


def _partner(x):
    lane = lax.broadcasted_iota(jnp.int32, x.shape, 1) % 64
    return jnp.where(lane < 8, pltpu.roll(x, LANES - 8, axis=1), jnp.where(lane < 16, pltpu.roll(x, 8, axis=1), 0.0))


def _rope(x, c, s):
    return x * c + _partner(x) * s


def _rope_t(x, c, s):
    return x * c + _partner(x * s)


def _rows_of(r, n, d):
    return pl.ds(r, n, stride=d) if d > 1 else pl.ds(0, n)


def _head_masks(shape):
    lane = lax.broadcasted_iota(jnp.int32, shape, 1) // 64
    return [lane == h for h in range(4)]


def _only(mask, x):
    return jnp.where(mask, x, jnp.zeros_like(x))


def _rope_perm(u, ctab, stab, ts=256):
    S = u.shape[0]
    nch = ATT_W // LANES

    def body(*refs):
        chunks, (c_ref, s_ref), outs, scr = refs[:3 * nch], refs[3 * nch:3 * nch + 2], refs[3 * nch + 2:-1], refs[-1]
        for k in range(3 * nch):
            scr[k] = chunks[k][...].astype(F32)
        for g, d in enumerate(DILS):
            n = ts // d
            for r in range(d):
                rows = _rows_of(r, n, d)
                c, s = c_ref[rows, :], s_ref[rows, :]
                for which in range(3):
                    parts = [scr.at[which * nch + j][rows, :] for j in (2 * g, 2 * g + 1)]
                    if which < 2:
                        parts = [_rope(x, c, s) for x in parts]
                    outs[which * 3 + g][r] = jnp.concatenate(parts, axis=1).astype(MXU)

    base = (IN_W - 3 * ATT_W) // LANES
    in_specs = [pl.BlockSpec((ts, LANES), lambda i, cb=base + k: (i, cb)) for k in range(3 * nch)]
    in_specs += [pl.BlockSpec((ts, LANES), lambda i: (i, 0))] * 2
    out_specs = [pl.BlockSpec((d, ts // d, ATT_O), lambda i: (0, i, 0)) for _ in range(3) for d in DILS]
    out_shape = [jax.ShapeDtypeStruct((d, S // d, ATT_O), MXU) for _ in range(3) for d in DILS]
    res = pl.pallas_call(body, grid=(S // ts,), in_specs=in_specs, out_specs=out_specs, out_shape=out_shape,
                         scratch_shapes=[pltpu.VMEM((3 * nch, ts, LANES), F32)],
                         compiler_params=_params(("arbitrary",), 32), name="rope_perm")(*([u] * (3 * nch)), ctab, stab)
    return [[res[which * 3 + g].reshape(S, ATT_O) for g in range(3)] for which in range(3)]


def _rope_unperm_bwd(dqkv, du, ctab, stab, ts=256):
    S = dqkv[0][0].shape[0]
    nch = ATT_W // LANES

    def body(*refs):
        ins, (c_ref, s_ref, _, o_ref, scr) = refs[:9], refs[9:]
        for g, d in enumerate(DILS):
            n = ts // d
            for r in range(d):
                rows = _rows_of(r, n, d)
                c, s = c_ref[rows, :], s_ref[rows, :]
                for which in range(3):
                    v = ins[which * 3 + g][r]
                    for half in range(2):
                        x = v[:, half * LANES:(half + 1) * LANES]
                        scr.at[which * nch + 2 * g + half][rows, :] = _rope_t(x, c, s) if which < 2 else x
        for j in range(3 * nch):
            o_ref[:, j * LANES:(j + 1) * LANES] = scr[j].astype(o_ref.dtype)

    in_specs = [pl.BlockSpec((d, ts // d, ATT_O), lambda i: (0, i, 0)) for _ in range(3) for d in DILS]
    in_specs += [pl.BlockSpec((ts, LANES), lambda i: (i, 0))] * 2 + [pl.BlockSpec(memory_space=pl.ANY)]
    args = [dqkv[which][g].reshape(d, S // d, ATT_O) for which in range(3) for g, d in enumerate(DILS)]
    last = (IN_W - 3 * ATT_W) // (3 * ATT_W)
    return pl.pallas_call(body, grid=(S // ts,), in_specs=in_specs, out_specs=pl.BlockSpec((ts, 3 * ATT_W), lambda i: (i, last)),
                          out_shape=jax.ShapeDtypeStruct((S, IN_W), MXU), scratch_shapes=[pltpu.VMEM((3 * nch, ts, LANES), F32)],
                          input_output_aliases={len(in_specs) - 1: 0},
                          compiler_params=_params(("arbitrary",), 32), name="rope_unperm_bwd")(*args, ctab, stab, du)


def _band_mask_keys(has_prev):
    r = lax.broadcasted_iota(jnp.int32, (QB, 2 * QB), 0)
    c = lax.broadcasted_iota(jnp.int32, (QB, 2 * QB), 1)
    return ((c < QB) & (c >= r) & has_prev) | ((c >= QB) & (c - QB <= r))


def _band_mask_queries(has_next):
    r = lax.broadcasted_iota(jnp.int32, (2 * QB, QB), 0)
    c = lax.broadcasted_iota(jnp.int32, (2 * QB, QB), 1)
    return ((r < QB) & (c <= r)) | ((r >= QB) & (c >= r - QB) & has_next)


ASUB = 4
_BIG = pl.BlockSpec((ASUB * QB, ATT_O), lambda b: (b, 0))
_PREV = pl.BlockSpec((QB, ATT_O), lambda b: (jnp.maximum(b * ASUB - 1, 0), 0))


def _sub(ref, j):
    return ref[j * QB:(j + 1) * QB]


def _attn_fwd(g, q, k, v):
    S = q.shape[0]
    nb = S // QB
    nblk = nb // DILS[g]

    def body(q_ref, kc_ref, kp_ref, vc_ref, vp_ref, o_ref, m_ref, l_ref):
        hm_kv, hm_o = _head_masks((2 * QB, ATT_O)), _head_masks((QB, ATT_O))
        for j in range(ASUB):
            ok = _band_mask_keys(((pl.program_id(0) * ASUB + j) & (nblk - 1)) > 0)
            k2 = jnp.concatenate([kp_ref[...] if j == 0 else _sub(kc_ref, j - 1), _sub(kc_ref, j)], axis=0)
            v2 = jnp.concatenate([vp_ref[...] if j == 0 else _sub(vc_ref, j - 1), _sub(vc_ref, j)], axis=0)
            qv = _sub(q_ref, j)
            o_acc = jnp.zeros((QB, ATT_O), F32)
            m_acc = jnp.zeros((QB, ATT_O), F32)
            l_acc = jnp.zeros((QB, ATT_O), F32)
            for h in range(4):
                s = jnp.where(ok, _dot_nt(qv, _only(hm_kv[h], k2)) * ATT_SCALE, NEG)
                m = jnp.max(s, axis=1, keepdims=True)
                p = jnp.exp(s - m)
                o_acc = o_acc + _dot(p.astype(MXU), _only(hm_kv[h], v2))
                m_acc = jnp.where(hm_o[h], m, m_acc)
                l_acc = jnp.where(hm_o[h], jnp.sum(p, axis=1, keepdims=True), l_acc)
            o_ref[j * QB:(j + 1) * QB] = o_acc
            m_ref[j * QB:(j + 1) * QB] = m_acc
            l_ref[j * QB:(j + 1) * QB] = l_acc

    shp = jax.ShapeDtypeStruct((S, ATT_O), F32)
    return pl.pallas_call(body, grid=(nb // ASUB,), in_specs=[_BIG, _BIG, _PREV, _BIG, _PREV],
                          out_specs=[_BIG] * 3, out_shape=[shp, shp, shp], compiler_params=_params(("arbitrary",), 32),
                          name=f"attn_fwd_{g}")(q, k, k, v, v)


def _natural(ref, d, scr, ts):
    if d == 1:
        return ref[0]
    n = ts // d
    for r in range(d):
        v = ref[r]
        scr.at[0][pl.ds(r, n, stride=d), :] = v[:, 0:LANES]
        scr.at[1][pl.ds(r, n, stride=d), :] = v[:, LANES:2 * LANES]
    return jnp.concatenate([scr[0], scr[1]], axis=1)


def _attn_combine(oml, ts=256):
    S = oml[0][0].shape[0]

    def body(*refs):
        ins, (att_ref, out_ref, lse_ref, scr) = refs[:9], refs[9:]
        o, m, l = [[_natural(ins[3 * g + k], d, scr, ts) for g, d in enumerate(DILS)] for k in range(3)]
        mx = jnp.maximum(jnp.maximum(m[0], m[1]), m[2])
        w = [jnp.exp(m[g] - mx) for g in range(3)]
        den = w[0] * l[0] + w[1] * l[1] + w[2] * l[2]
        out = (w[0] * o[0] + w[1] * o[1] + w[2] * o[2]) / den
        out_ref[...] = out
        att_ref[...] = out.astype(MXU)
        lse_ref[...] = mx + jnp.log(den)

    in_specs = [pl.BlockSpec((d, ts // d, ATT_O), lambda i: (0, i, 0)) for d in DILS for _ in range(3)]
    args = [a.reshape(d, S // d, ATT_O) for d, grp in zip(DILS, oml) for a in grp]
    blk = pl.BlockSpec((ts, ATT_O), lambda i: (i, 0))
    return pl.pallas_call(body, grid=(S // ts,), in_specs=in_specs, out_specs=[blk, blk, blk],
                          out_shape=[jax.ShapeDtypeStruct((S, ATT_O), MXU), jax.ShapeDtypeStruct((S, ATT_O), F32),
                                     jax.ShapeDtypeStruct((S, ATT_O), F32)],
                          scratch_shapes=[pltpu.VMEM((2, ts, LANES), F32)], compiler_params=_params(("arbitrary",), 32),
                          name="attn_combine")(*args)


def _attn_bwd_prep(datt, o, lse, ts=256):
    S = datt.shape[0]

    def body(da0, da1, o_ref, l0, l1, *rest):
        outs, dl = rest[:9], rest[9]
        prod = jnp.concatenate([da0[...], da1[...]], axis=1) * o_ref[...]
        delta = jnp.zeros((ts, ATT_O), F32)
        for hm in _head_masks((ts, ATT_O)):
            delta = jnp.where(hm, jnp.sum(_only(hm, prod), axis=1, keepdims=True), delta)
        dl[0] = delta[:, 0:LANES]
        dl[1] = delta[:, LANES:2 * LANES]
        for g, d in enumerate(DILS):
            n = ts // d
            for r in range(d):
                rows = _rows_of(r, n, d)
                outs[g][r] = jnp.concatenate([da0[rows, :], da1[rows, :]], axis=1).astype(MXU)
                outs[3 + g][r] = jnp.concatenate([dl.at[0][rows, :], dl.at[1][rows, :]], axis=1)
                outs[6 + g][r] = jnp.concatenate([l0[rows, :], l1[rows, :]], axis=1)

    half = lambda j: pl.BlockSpec((ts, LANES), lambda i: (i, j))
    out_specs = [pl.BlockSpec((d, ts // d, ATT_O), lambda i: (0, i, 0)) for _ in range(3) for d in DILS]
    out_shape = [jax.ShapeDtypeStruct((d, S // d, ATT_O), dt) for dt in (MXU, F32, F32) for d in DILS]
    res = pl.pallas_call(body, grid=(S // ts,), in_specs=[half(0), half(1), pl.BlockSpec((ts, ATT_O), lambda i: (i, 0)), half(0), half(1)],
                         out_specs=out_specs, out_shape=out_shape, scratch_shapes=[pltpu.VMEM((2, ts, LANES), F32)],
                         compiler_params=_params(("arbitrary",), 32), name="attn_bwd_prep")(datt, datt, o, lse, lse)
    return [[res[k * 3 + g].reshape(S, ATT_O) for g in range(3)] for k in range(3)]


def _head_col(x, h):
    return x[:, h * 64:h * 64 + 1]


def _attn_bwd(g, q, k, v, do, delta, lse):
    S = q.shape[0]
    nb = S // QB
    nblk = nb // DILS[g]

    def body(k_ref, v_ref, qc_ref, qn_ref, doc_ref, don_ref, dlc_ref, dln_ref, lc_ref, ln_ref, dq_ref, dk_ref, dv_ref, dq_scr):
        hms, hmk = _head_masks((2 * QB, ATT_O)), _head_masks((QB, ATT_O))
        first = pl.program_id(0) == 0

        @pl.when(first)
        def _():
            dq_scr[0:QB] = jnp.zeros((QB, ATT_O), F32)

        @pl.when(jnp.logical_not(first))
        def _():
            dq_scr[0:QB] = dq_scr[ASUB * QB:(ASUB + 1) * QB]

        dq_scr[QB:(ASUB + 1) * QB] = jnp.zeros((ASUB * QB, ATT_O), F32)

        def both(cur_ref, nxt_ref, j):
            return jnp.concatenate([_sub(cur_ref, j), nxt_ref[...] if j == ASUB - 1 else _sub(cur_ref, j + 1)], axis=0)

        for j in range(ASUB):
            ok = _band_mask_queries(((pl.program_id(0) * ASUB + j + 1) & (nblk - 1)) > 0)
            q2, do2, dl2, lse2 = both(qc_ref, qn_ref, j), both(doc_ref, don_ref, j), both(dlc_ref, dln_ref, j), both(lc_ref, ln_ref, j)
            kv, vv = _sub(k_ref, j), _sub(v_ref, j)
            dk = jnp.zeros((QB, ATT_O), F32)
            dv = jnp.zeros((QB, ATT_O), F32)
            dq2 = jnp.zeros((2 * QB, ATT_O), F32)
            for h, hm in enumerate(hms):
                qh, doh = _only(hm, q2), _only(hm, do2)
                p = jnp.where(ok, jnp.exp(_dot_nt(qh, kv) * ATT_SCALE - _head_col(lse2, h)), 0.0)
                ds = (p * (_dot_nt(doh, vv) - _head_col(dl2, h))).astype(MXU)
                dv = dv + _dot_tn(p.astype(MXU), doh)
                dk = dk + _dot_tn(ds, qh)
                dq2 = dq2 + _dot(ds, _only(hmk[h], kv))
            dk_ref[j * QB:(j + 1) * QB] = dk * ATT_SCALE
            dv_ref[j * QB:(j + 1) * QB] = dv
            dq_scr[j * QB:(j + 2) * QB] += dq2
        dq_ref[...] = dq_scr[0:ASUB * QB] * ATT_SCALE

    nxt = pl.BlockSpec((QB, ATT_O), lambda b: (jnp.minimum((b + 1) * ASUB, nb - 1), 0))
    shp = jax.ShapeDtypeStruct((S, ATT_O), F32)
    return pl.pallas_call(body, grid=(nb // ASUB,), in_specs=[_BIG, _BIG, _BIG, nxt, _BIG, nxt, _BIG, nxt, _BIG, nxt], out_specs=[_BIG] * 3,
                          out_shape=[shp, shp, shp], scratch_shapes=[pltpu.VMEM(((ASUB + 1) * QB, ATT_O), F32)],
                          compiler_params=_params(("arbitrary",), 32), name=f"attn_bwd_{g}")(k, v, q, q, do, do, delta, delta, lse, lse)


def _merge_fwd(x0, u, a2, yb, att, wa, wb, wc, w_out, g_post, ts=256):
    S = x0.shape[0]

    def body(i, g, x_ref, gate_ref, a2_ref, yb_ref, att_ref, wa_ref, wb_ref, wc_ref, wo_ref, gp_ref, mg_ref, y_ref, xo_ref):
        gate = lambda n: jax.nn.sigmoid(gate_ref[:, n * D:(n + 1) * D].astype(F32))
        merged = gate(0) * _dot_nt(a2_ref[...], wa_ref[...])
        merged = merged + gate(1) * _dot_nt(yb_ref[...], wb_ref[...])
        merged = merged + gate(2) * _dot_nt(att_ref[...], wc_ref[...])
        mb = merged.astype(MXU)
        mg_ref[...] = mb
        y = _dot(mb, wo_ref[...])
        y_ref[...] = y
        xo_ref[...] = x_ref[...] + _rms(y, gp_ref[...])[0]

    ins = [("t", x0, D, 0), ("t", u, GATE_W, 0), ("t", a2, POOLW, 0), ("t", yb, POOLW, 0), ("t", att, ATT_O, 0),
           ("w", wa), ("w", wb), ("w", wc), ("w", w_out), ("w", g_post)]
    return _rows_call("merge_fwd", body, S, ts, ins, [("t", D, MXU), ("t", D, F32), ("t", D, F32)])


def _merge_bwd(dx, y1, u, a2, yb, att, wa, wb, wc, w_out, g_post, ts=256):
    S = dx.shape[0]

    def body(i, g, dx_ref, y_ref, gate_ref, a2_ref, yb_ref, att_ref, wa_ref, wb_ref, wc_ref, wo_ref, gp_ref,
             dy_ref, dgate_ref, dbra_ref, dbrb_ref, dbrc_ref, da2_ref, dyb_ref, datt_ref, dgp_ref):
        dxv, y = dx_ref[...], y_ref[...]
        dy, r = _rms_bwd(dxv * gp_ref[...], y)
        _acc(dgp_ref, g, jnp.sum(dxv * (y * r), axis=0, keepdims=True))
        dyb16 = dy.astype(MXU)
        dy_ref[...] = dyb16
        dm = _dot_nt(dyb16, wo_ref[...])
        for n, (src, w_ref, dbr_ref, din_ref) in enumerate(((a2_ref, wa_ref, dbra_ref, da2_ref), (yb_ref, wb_ref, dbrb_ref, dyb_ref),
                                                           (att_ref, wc_ref, dbrc_ref, datt_ref))):
            gt = jax.nn.sigmoid(gate_ref[:, n * D:(n + 1) * D].astype(F32))
            br = _dot_nt(src[...], w_ref[...])
            dgate_ref[:, n * D:(n + 1) * D] = (dm * br * gt * (1.0 - gt)).astype(dgate_ref.dtype)
            dbr = (dm * gt).astype(MXU)
            dbr_ref[...] = dbr
            din_ref[...] = _dot(dbr, w_ref[...])

    ins = [("t", dx, D, 0), ("t", y1, D, 0), ("t", u, GATE_W, 0), ("t", a2, POOLW, 0), ("t", yb, POOLW, 0), ("t", att, ATT_O, 0),
           ("w", wa), ("w", wb), ("w", wc), ("w", w_out), ("w", g_post)]
    outs = [("t", D, MXU), ("c", IN_W, GATE_W, 0, MXU), ("t", D, MXU), ("t", D, MXU), ("t", D, MXU), ("t", POOLW, F32), ("t", POOLW, F32),
            ("t", ATT_O, F32), ("a", (1, D), F32)]
    return _rows_call("merge_bwd", body, S, ts, ins, outs)


def _prenorm_bwd(name, dx_res, du, wt, x, g_pre, ts=256, lead=0):
    S = x.shape[0]
    N = du.shape[1]

    def body(i, g, dx_ref, du_ref, wt_ref, x_ref, g_ref, o_ref, dg_ref):
        if lead:
            dhv = _dot(du_ref[:, 0:lead], wt_ref[N - lead:N, :]) + _dot(du_ref[:, lead:N], wt_ref[0:N - lead, :])
        else:
            dhv = _dot(du_ref[...], wt_ref[...])
        xv = x_ref[...]
        dxn, r = _rms_bwd(dhv * g_ref[...], xv)
        o_ref[...] = dx_ref[...] + dxn
        _acc(dg_ref, g, jnp.sum(dhv * (xv * r), axis=0, keepdims=True))

    ins = [("t", dx_res, D, 0), ("t", du, N, 0), ("w", wt), ("t", x, D, 0), ("w", g_pre)]
    return _rows_call(name, body, S, ts, ins, [("t", D, F32), ("a", (1, D), F32)], vmem_mb=52)


def _mem_heads(qm, kv_ref):
    out = []
    for h in range(4):
        q = qm[:, h * 128:(h + 1) * 128].astype(MXU)
        k = kv_ref[:, h * 128:(h + 1) * 128]
        v = kv_ref[:, MEM_W + h * 128:MEM_W + (h + 1) * 128]
        sc = _dot_nt(q, k) * MEM_SCALE
        e = jnp.exp(sc - jnp.max(sc, axis=1, keepdims=True))
        out.append((e / jnp.sum(e, axis=1, keepdims=True), q, k, v))
    return out


def _mem_fwd(x1, kv, g_pre, w_mq, w_mo, g_post, ts=256):
    S = x1.shape[0]

    def body(i, g, x_ref, kv_ref, gq_ref, wq_ref, wo_ref, gp_ref, om_ref, h_ref, y_ref, xo_ref):
        x = x_ref[...]
        hb = _rms(x, gq_ref[...])[0].astype(MXU)
        h_ref[...] = hb
        qm = _dot(hb, wq_ref[...])
        om = jnp.concatenate([_dot(p.astype(MXU), v) for p, _, _, v in _mem_heads(qm, kv_ref)], axis=1).astype(MXU)
        om_ref[...] = om
        y = _dot_nt(om, wo_ref[...])
        y_ref[...] = y
        xo_ref[...] = x + _rms(y, gp_ref[...])[0]

    ins = [("t", x1, D, 0), ("w", kv), ("w", g_pre), ("w", w_mq), ("w", w_mo), ("w", g_post)]
    return _rows_call("mem_fwd", body, S, ts, ins, [("t", MEM_W, MXU), ("t", D, MXU), ("t", D, F32), ("t", D, F32)])


def _mem_bwd(dx2, ym, x1, kv, g_pre, w_mq, w_mo, g_post, ts=256):
    S = x1.shape[0]

    def body(i, g, dx_ref, y_ref, x_ref, kv_ref, gq_ref, wq_ref, wo_ref, gp_ref, dy_ref, dq_ref, dxo_ref, dgp_ref, dgq_ref, dkv_ref):
        dxv, y, x = dx_ref[...], y_ref[...], x_ref[...]
        dy, r = _rms_bwd(dxv * gp_ref[...], y)
        _acc(dgp_ref, g, jnp.sum(dxv * (y * r), axis=0, keepdims=True))
        dyb = dy.astype(MXU)
        dy_ref[...] = dyb
        dom = _dot(dyb, wo_ref[...])
        h, r1 = _rms(x, gq_ref[...])
        qm = _dot(h.astype(MXU), wq_ref[...])
        dqs = []

        @pl.when(g == 0)
        def _():
            dkv_ref[...] = jnp.zeros_like(dkv_ref)

        for hh, (p, q, k, v) in enumerate(_mem_heads(qm, kv_ref)):
            doh = dom[:, hh * 128:(hh + 1) * 128].astype(MXU)
            dp = _dot_nt(doh, v)
            dsc = (p * (dp - jnp.sum(dp * p, axis=1, keepdims=True)) * MEM_SCALE).astype(MXU)
            dqs.append(_dot(dsc, k))
            dkv_ref[:, hh * 128:(hh + 1) * 128] += _dot_tn(dsc, q)
            dkv_ref[:, MEM_W + hh * 128:MEM_W + (hh + 1) * 128] += _dot_tn(p.astype(MXU), doh)
        dq = jnp.concatenate(dqs, axis=1).astype(MXU)
        dq_ref[...] = dq
        dh = _dot_nt(dq, wq_ref[...])
        _acc(dgq_ref, g, jnp.sum(dh * (x * r1), axis=0, keepdims=True))
        dxo_ref[...] = dxv + _rms_bwd(dh * gq_ref[...], x)[0]

    ins = [("t", dx2, D, 0), ("t", ym, D, 0), ("t", x1, D, 0), ("w", kv), ("w", g_pre), ("w", w_mq), ("w", w_mo), ("w", g_post)]
    outs = [("t", D, MXU), ("t", MEM_W, MXU), ("t", D, F32), ("a", (1, D), F32), ("a", (1, D), F32), ("a", (256, D), F32)]
    return _rows_call("mem_bwd", body, S, ts, ins, outs)


def _gain_grad(name, dn, x):
    n = x.shape[0]

    def body(i, g, dn_ref, x_ref, o_ref):
        xv = x_ref[...]
        r = lax.rsqrt(jnp.mean(xv * xv, axis=-1, keepdims=True) + EPS)
        o_ref[...] = jnp.sum(dn_ref[...] * (xv * r), axis=0, keepdims=True)

    return _rows_call(name, body, n, n, [("t", dn, D, 0), ("t", x, D, 0)], [("a", (1, D), F32)])[0]


def _ffn_fwd(x2, u3, conv_f, w_down, g_post, ts=256):
    S = x2.shape[0]

    def body(i, g, x_ref, ua_ref, ub_ref, cw_ref, wd_ref, gp_ref, act_ref, y_ref, xo_ref, cu):
        @pl.when(g == 0)
        def _():
            cu[...] = jnp.zeros_like(cu)

        ua = ua_ref[...].astype(F32)
        c, _, _ = _conv3(ua, cu[...], cw_ref[...])
        act = (c * jax.nn.sigmoid(c) * ub_ref[...].astype(F32)).astype(MXU)
        act_ref[...] = act
        y = _dot(act, wd_ref[...])
        y_ref[...] = y
        xo_ref[...] = x_ref[...] + _rms(y, gp_ref[...])[0]
        cu[...] = ua[ts - 8:]

    ins = [("t", x2, D, 0), ("t", u3, D_FF, 0), ("t", u3, D_FF, 1), ("w", conv_f), ("w", w_down), ("w", g_post)]
    return _rows_call("ffn_fwd", body, S, ts, ins, [("t", D_FF, MXU), ("t", D, F32), ("t", D, F32)],
                      scratch=[pltpu.VMEM((8, D_FF), F32)], vmem_mb=56)


def _ffn_bwd(dx3, y3, u3, conv_f, w_down, g_post, ts=128):
    S = dx3.shape[0]

    def body(i, g, dx_ref, y_ref, ua_ref, ub_ref, uap_ref, cw_ref, wd_ref, gp_ref, dy_ref, du_ref, dgp_ref, dcw_ref, cdc):
        @pl.when(g == 0)
        def _():
            cdc[...] = jnp.zeros_like(cdc)

        dxv, y = dx_ref[...], y_ref[...]
        dy, r = _rms_bwd(dxv * gp_ref[...], y)
        _acc(dgp_ref, g, jnp.sum(dxv * (y * r), axis=0, keepdims=True))
        dyb = dy.astype(MXU)
        dy_ref[...] = dyb
        dact = _dot_nt(dyb, wd_ref[...])
        ua, w = ua_ref[...].astype(F32), cw_ref[...]
        c, u1, u2 = _conv3(ua, uap_ref[...].astype(F32)[8:16] * (i > 0).astype(F32), w)
        sg = jax.nn.sigmoid(c)
        du_ref[:, D_FF:2 * D_FF] = (dact * (c * sg)).astype(du_ref.dtype)
        dc = dact * ub_ref[...].astype(F32) * (sg * (1.0 + c * (1.0 - sg)))
        du_ref[:, 0:D_FF] = _conv3_t(dc, cdc[...], w).astype(du_ref.dtype)
        dw = jnp.concatenate([jnp.sum(dc * u2, axis=0, keepdims=True), jnp.sum(dc * u1, axis=0, keepdims=True),
                              jnp.sum(dc * ua, axis=0, keepdims=True)], axis=0)
        _acc(dcw_ref, g, dw)
        cdc[...] = dc[:8]

    ins = [("t", dx3, D, 0), ("t", y3, D, 0), ("t", u3, D_FF, 0), ("t", u3, D_FF, 1), ("h", u3, 16, D_FF, 0), ("w", conv_f),
           ("w", w_down), ("w", g_post)]
    outs = [("t", D, MXU), ("t", 2 * D_FF, MXU), ("a", (1, D), F32), ("a", (3, D_FF), F32)]
    return _rows_call("ffn_bwd", body, S, ts, ins, outs, scratch=[pltpu.VMEM((8, D_FF), F32)], reverse=True, vmem_mb=56)


def _loss_head(x, target, ts=512):
    S = x.shape[0]

    def body(i, g, x_ref, t_ref, dx_ref, acc_ref):
        diff = x_ref[...] - t_ref[...]
        dx_ref[...] = diff * (1.0 / D)
        col = jnp.sum(diff * diff, axis=0, keepdims=True)
        part = col[:, 0:LANES]
        for j in range(1, D // LANES):
            part = part + col[:, j * LANES:(j + 1) * LANES]
        row = lax.broadcasted_iota(jnp.int32, (8, LANES), 0)
        _acc(acc_ref, g, jnp.where(row == 0, jnp.broadcast_to(part, (8, LANES)), 0.0))

    return _rows_call("loss_head", body, S, ts, [("t", x, D, 0), ("t", target, D, 0)], [("t", D, F32), ("a", (8, LANES), F32)])


_OPERAND_NAME = dict(w_in='w_in', w_branch_a='wa', w_branch_b='wb', w_branch_c='wc', w_out='w_out', w_mq='w_mq', w_mkv='w_mkv',
                     w_mo='w_mo', w_up='w_up', w_down='w_down')


def _big_operands(big):
    return {_OPERAND_NAME[n]: a for n, a in big.items()}


def _layer_weights(big, small, l):
    pool_w = small['pool_w'][l].astype(MXU)
    wblk = jnp.zeros((POOLW, POOLW), MXU)
    for g in range(4):
        wblk = lax.dynamic_update_slice(wblk, pool_w[g], (g * 96, g * 96))
    vec = lambda n: small[n][l].reshape(1, -1)
    return dict(
        _big_operands(big),
        wblk=wblk, pool_scale=vec('pool_scale'), conv_b=small['conv_b_w'][l], conv_f=small['conv_ffn_w'][l],
        g_mix_pre=vec('norm_mix_pre'), g_mix_post=vec('norm_mix_post'), g_mem_pre=vec('norm_mem_pre'),
        g_mem_post=vec('norm_mem_post'), g_memkv=vec('norm_memkv'), g_ffn_pre=vec('norm_ffn_pre'), g_ffn_post=vec('norm_ffn_post'))


def _layer_fwd(x0, mem, W, ctab, stab):
    sv = _layer_fwd_mix(x0, W, ctab, stab)
    return _layer_fwd_late(mem, W, sv), sv


def _layer_fwd_mix(x0, W, ctab, stab):
    return _layer_fwd_merge(W, _layer_fwd_branches(x0, W, ctab, stab))


def _layer_fwd_branches(x0, W, ctab, stab):
    sv = dict(x0=x0)
    sv['u'], sv['h1'] = _norm_mm("in_proj", x0, W['g_mix_pre'], W['w_in'], ts=1024, tn=IN_TILE, wt=True, rot=IN_ROT, out_dtype=MXU)
    sv['a2'], sv['yb'] = _poolconv_fwd(sv['u'], W['wblk'], W['pool_scale'], W['conv_b'])
    sv['qkv'] = q3, k3, v3 = _rope_perm(sv['u'], ctab, stab)
    sv['att'], sv['o'], sv['lse'] = _attn_combine([_attn_fwd(g, q3[g], k3[g], v3[g]) for g in range(3)])
    return sv


def _layer_fwd_merge(W, sv):
    sv['merged'], sv['y1'], sv['x1'] = _merge_fwd(sv['x0'], sv['u'], sv['a2'], sv['yb'], sv['att'], W['wa'], W['wb'], W['wc'],
                                                  W['w_out'], W['g_mix_post'])
    return sv


def _layer_fwd_late(mem, W, sv):
    sv['kv'], sv['memn'] = _norm_mm("mem_kv", mem, W['g_memkv'], W['w_mkv'], ts=256, tn=D, out_dtype=MXU)
    sv['om'], sv['h2'], sv['ym'], sv['x2'] = _mem_fwd(sv['x1'], sv['kv'], W['g_mem_pre'], W['w_mq'], W['w_mo'], W['g_mem_post'])
    sv['u3'], sv['h3'] = _norm_mm("up_proj", sv['x2'], W['g_ffn_pre'], W['w_up'], ts=1024, tn=1408, wt=True, out_dtype=MXU)
    sv['act'], sv['y3'], x3 = _ffn_fwd(sv['x2'], sv['u3'], W['conv_f'], W['w_down'], W['g_ffn_post'])
    return x3


def _layer_bwd(dx3, mem, W, sv, ctab, stab):
    dx1, g = _layer_bwd_late(dx3, mem, W, sv)
    dx0, g_mix = _layer_bwd_mix(dx1, W, sv, ctab, stab)
    return dx0, {**g, **g_mix}


def _layer_bwd_late(dx3, mem, W, sv):
    g = {}
    dy3, du3, g['norm_ffn_post'], g['conv_ffn_w'] = _ffn_bwd(dx3, sv['y3'], sv['u3'], W['conv_f'], W['w_down'], W['g_ffn_post'])
    g['w_down'] = _mm_tn("dw_down", sv['act'], dy3, cap_k=256)
    g['w_up'] = _mm_tn("dw_up", du3, sv['h3'])
    dx2, g['norm_ffn_pre'] = _prenorm_bwd("ffn_pre_bwd", dx3, du3, W['w_up'], sv['x2'], W['g_ffn_pre'])
    dym, dqm, dx1, g['norm_mem_post'], g['norm_mem_pre'], dkv = _mem_bwd(dx2, sv['ym'], sv['x1'], sv['kv'], W['g_mem_pre'],
                                                                       W['w_mq'], W['w_mo'], W['g_mem_post'])
    g['w_mo'] = _mm_tn("dw_mo", dym, sv['om'])
    g['w_mq'] = _mm_tn("dw_mq", sv['h2'], dqm)
    dkvb = dkv.astype(MXU)
    g['w_mkv'] = _mm_tn("dw_mkv", sv['memn'], dkvb)
    g['norm_memkv'] = _gain_grad("memkv_gain", _mm_nt("d_memn", dkvb, W['w_mkv'], ts=256, tn=512), mem)
    return dx1, g


def _layer_bwd_mix(dx1, W, sv, ctab, stab):
    du, g = _layer_bwd_mixers(dx1, W, sv, ctab, stab)
    g['w_in'] = _dw_in(du, sv)
    dx0, g['norm_mix_pre'] = _mix_pre_bwd(dx1, du, W, sv)
    return dx0, g


def _dw_in(du, sv):
    return _mm_tn("dw_in", du, sv['h1'], cap_k=IN_TILE, rot=IN_ROT)


def _mix_pre_bwd(dx1, du, W, sv):
    return _prenorm_bwd("mix_pre_bwd", dx1, du, W['w_in'], sv['x0'], W['g_mix_pre'], lead=GATE_W)


def _layer_bwd_mixers(dx1, W, sv, ctab, stab):
    parts, g = _layer_bwd_merge(dx1, W, sv)
    du, g_br = _layer_bwd_branches(parts, W, sv, ctab, stab)
    return du, {**g, **g_br}


def _layer_bwd_merge(dx1, W, sv):
    g = {}
    dy1, du, dbra, dbrb, dbrc, da2, dyb, datt, g['norm_mix_post'] = _merge_bwd(
        dx1, sv['y1'], sv['u'], sv['a2'], sv['yb'], sv['att'], W['wa'], W['wb'], W['wc'], W['w_out'], W['g_mix_post'])
    g['w_out'] = _mm_tn("dw_out", sv['merged'], dy1)
    g['w_branch_a'] = _mm_tn("dw_a", dbra, sv['a2'])
    g['w_branch_b'] = _mm_tn("dw_b", dbrb, sv['yb'])
    g['w_branch_c'] = _mm_tn("dw_c", dbrc, sv['att'])
    return (du, da2, dyb, datt), g


def _layer_bwd_branches(parts, W, sv, ctab, stab):
    du, da2, dyb, datt = parts
    g = {}
    du, g['pool_scale'], dwblk, g['conv_b_w'] = _poolconv_bwd(sv['u'], da2, dyb, du, W['wblk'], W['pool_scale'], W['conv_b'])
    g['pool_w'] = jnp.stack([dwblk[k * 96:(k + 1) * 96, k * 96:(k + 1) * 96] for k in range(4)])
    q3, k3, v3 = sv['qkv']
    do3, dl3, lse3 = _attn_bwd_prep(datt, sv['o'], sv['lse'])
    dqkv3 = [_attn_bwd(i, q3[i], k3[i], v3[i], do3[i], dl3[i], lse3[i]) for i in range(3)]
    du = _rope_unperm_bwd([[t[which] for t in dqkv3] for which in range(3)], du, ctab, stab)
    return du, g


def _local_step(x, mem, positions, target, big, small):
    ctab, stab = _rope_tables(positions)
    Ws = [_layer_weights(big[l], small, l) for l in range(DEPTH)]
    saved = []
    for l in range(DEPTH):
        x, sv = _layer_fwd(x, mem, Ws[l], ctab, stab)
        saved.append(sv)
    dx, acc = _loss_head(x, target)
    loss = jnp.sum(acc) * (0.5 / D)
    grads = [None] * DEPTH
    for l in reversed(range(DEPTH)):
        dx, grads[l] = _layer_bwd(dx, mem, Ws[l], saved[l], ctab, stab)
    return loss, dx, grads


_HBM = pl.BlockSpec(memory_space=pl.ANY)
MESH_ID = pl.DeviceIdType.MESH


def _all_gather(name, xs):
    n = len(xs)

    def body(*refs):
        x_refs, out_refs = refs[:n], refs[n:2 * n]
        send_sems, recv_sems, local_sems = refs[2 * n:]
        x, y, c = lax.axis_index("x"), lax.axis_index("y"), lax.axis_index("c")
        me, sibling = (x, y, c), (x, y, 1 - c)
        chips = [(1 - x, y), (x, 1 - y), (1 - x, 1 - y)]

        def slot(a, p):
            return out_refs[a].at[4 * p[0] + 2 * p[1] + p[2]]

        def copy(a, k, block, to, src=None):
            return pltpu.make_async_remote_copy(src_ref=slot(a, block) if src is None else src, dst_ref=slot(a, block),
                                                send_sem=send_sems.at[a, k], recv_sem=recv_sems.at[a, k], device_id=to,
                                                device_id_type=MESH_ID)

        started = []
        for a in range(n):
            mine = pltpu.make_async_copy(x_refs[a], slot(a, me), local_sems.at[a])
            mine.start()
            started.append(mine)
        first = []
        for a in range(n):
            first.append(copy(a, 0, me, sibling, src=x_refs[a]))
            first += [copy(a, 1 + j, me, (*chip, c), src=x_refs[a]) for j, chip in enumerate(chips)]
        for cp in first:
            cp.start()
        passed = []
        for j, chip in enumerate(chips):
            for a in range(n):
                copy(a, 1 + j, (*chip, c), me).wait_recv()
                fw = copy(a, 4 + j, (*chip, c), sibling)
                fw.start()
                passed.append(fw)
        for a in range(n):
            copy(a, 0, sibling, me).wait_recv()
            for j, chip in enumerate(chips):
                copy(a, 4 + j, (*chip, 1 - c), me).wait_recv()
        for cp in first + passed:
            cp.wait_send()
        for mine in started:
            mine.wait()

    return pl.pallas_call(
        body, out_shape=[jax.ShapeDtypeStruct((N_DEV,) + x.shape, x.dtype) for x in xs], in_specs=[_HBM] * n, out_specs=[_HBM] * n,
        scratch_shapes=[pltpu.SemaphoreType.DMA((n, 7)), pltpu.SemaphoreType.DMA((n, 7)), pltpu.SemaphoreType.DMA((n,))],
        name=name)(*xs)


def _exchange(name, gs):
    n = len(gs)

    def body(*refs):
        g_refs, out_refs = refs[:n], refs[n:2 * n]
        send_sems, recv_sems, local_sems = refs[2 * n:]
        x, y, c = lax.axis_index("x"), lax.axis_index("y"), lax.axis_index("c")
        me = 4 * x + 2 * y + c
        copies = []
        for a in range(n):
            mine = pltpu.make_async_copy(g_refs[a].at[me], out_refs[a].at[me], local_sems.at[a])
            mine.start()
            copies.append(mine)
        for r in range(1, N_DEV):
            px, py, pc = x ^ ((r >> 2) & 1), y ^ ((r >> 1) & 1), c ^ (r & 1)
            for a in range(n):
                cp = pltpu.make_async_remote_copy(src_ref=g_refs[a].at[4 * px + 2 * py + pc], dst_ref=out_refs[a].at[me],
                                                  send_sem=send_sems.at[a, r - 1], recv_sem=recv_sems.at[a, r - 1],
                                                  device_id=(px, py, pc), device_id_type=MESH_ID)
                cp.start()
                copies.append(cp)
        for cp in copies:
            cp.wait()

    return pl.pallas_call(
        body, out_shape=[jax.ShapeDtypeStruct(g.shape, g.dtype) for g in gs], in_specs=[_HBM] * n, out_specs=[_HBM] * n,
        scratch_shapes=[pltpu.SemaphoreType.DMA((n, N_DEV - 1)), pltpu.SemaphoreType.DMA((n, N_DEV - 1)), pltpu.SemaphoreType.DMA((n,))],
        name=name)(*gs)


_SEM = pl.BlockSpec(memory_space=pltpu.SEMAPHORE)
_IN_HBM = pl.BlockSpec(memory_space=pltpu.HBM)
_SIDE_EFFECT = pltpu.SideEffectType.DATAFLOW_SIDE_EFFECTING


def _push_copies(src_refs, land_refs, send_sems, recv_sems, per_peer):
    x, y, c = lax.axis_index("x"), lax.axis_index("y"), lax.axis_index("c")
    me = 4 * x + 2 * y + c
    copies = []
    for r in range(1, N_DEV):
        px, py, pc = x ^ ((r >> 2) & 1), y ^ ((r >> 1) & 1), c ^ (r & 1)
        for a, (s, d) in enumerate(zip(src_refs, land_refs)):
            k = a * (N_DEV - 1) + r - 1
            copies.append(pltpu.make_async_remote_copy(src_ref=s.at[4 * px + 2 * py + pc] if per_peer else s, dst_ref=d.at[me],
                                                       send_sem=send_sems.at[k], recv_sem=recv_sems.at[k],
                                                       device_id=(px, py, pc), device_id_type=MESH_ID))
    return copies


def _push_start(name, srcs, per_peer, after):
    n = len(srcs)
    lands = [lax.empty((N_DEV,) + (s.shape[1:] if per_peer else s.shape), s.dtype) for s in srcs]

    def body(*refs):
        for cp in _push_copies(refs[:n], refs[n:2 * n], refs[2 * n + 1], refs[2 * n + 2], per_peer):
            cp.start()
        refs[-1][...] = jnp.zeros_like(refs[-1])

    hbm = [pltpu.HBM(a.shape, a.dtype) for a in (*srcs, *lands)]
    sems = pltpu.SemaphoreType.DMA((n * (N_DEV - 1),))
    out = pl.pallas_call(
        body, name=name, out_shape=(sems, sems, *hbm, jax.ShapeDtypeStruct((8, LANES), F32)),
        in_specs=[_IN_HBM] * (2 * n) + [pl.BlockSpec(memory_space=pl.ANY)],
        out_specs=(_SEM, _SEM, *[_IN_HBM] * (2 * n), pl.BlockSpec(memory_space=pltpu.VMEM)),
        input_output_aliases={a: 2 + a for a in range(2 * n)},
        compiler_params=pltpu.CompilerParams(has_side_effects=_SIDE_EFFECT),
    )(*[pltpu.with_memory_space_constraint(a, pltpu.HBM) for a in (*srcs, *lands)], after)
    return out[0], out[1], out[2:2 + n], out[2 + n:2 + 2 * n], out[-1]


def _push_wait(name, started, per_peer, after):
    send_sems, recv_sems, srcs, lands, _ = started
    n = len(srcs)

    def body(*refs):
        for cp in _push_copies(refs[:n], refs[n:2 * n], refs[2 * n], refs[2 * n + 1], per_peer):
            cp.wait_send()
            cp.wait_recv()

    out = pl.pallas_call(
        body, name=name, out_shape=[pltpu.HBM(a.shape, a.dtype) for a in (*srcs, *lands)],
        in_specs=[_IN_HBM] * (2 * n) + [_SEM, _SEM, pl.BlockSpec(memory_space=pl.ANY)], out_specs=[_IN_HBM] * (2 * n),
        input_output_aliases={a: a for a in range(2 * n)},
        compiler_params=pltpu.CompilerParams(has_side_effects=_SIDE_EFFECT),
    )(*srcs, *lands, send_sems, recv_sems, after)
    me = _my_slot()
    own = [lax.dynamic_index_in_dim(s, me, 0, keepdims=False) if per_peer else s for s in out[:n]]
    return _with_own(out[n:], own, me)


def _my_slot():
    return 4 * lax.axis_index("x") + 2 * lax.axis_index("y") + lax.axis_index("c")


def _row_tile(rows, cols, budget):
    if rows * cols * 4 <= budget or rows % 16:
        return rows
    best = 16
    for t in range(16, rows + 1, 16):
        if rows % t == 0 and t * cols * 4 <= budget:
            best = t
    return best


def _sum_slots(name, recv):
    _, R, C = recv.shape
    tr = _row_tile(R, C, 1 << 20)

    def body(r_ref, o_ref):
        g = r_ref[0].astype(F32)
        for k in range(1, N_DEV):
            g = g + r_ref[k].astype(F32)
        o_ref[...] = g

    return pl.pallas_call(body, grid=(R // tr,), in_specs=[pl.BlockSpec((N_DEV, tr, C), lambda i: (0, i, 0))],
                          out_specs=pl.BlockSpec((tr, C), lambda i: (i, 0)), out_shape=jax.ShapeDtypeStruct((R, C), F32),
                          compiler_params=_params(("arbitrary",), 32), name=name)(recv)


def _adamw(name, g, w, m, v):
    shape = w.shape
    R, C = shape[-2], shape[-1]
    view = (-1, R, C)
    L = w.size // (R * C)
    tr = _row_tile(R, C, 1 << 20)
    c1 = 1.0 - ADAM_B1 ** ADAM_STEP
    c2 = 1.0 - ADAM_B2 ** ADAM_STEP

    def body(g_ref, w_ref, m_ref, v_ref, d_ref, mo_ref, vo_ref):
        gv = g_ref[...]
        mn = ADAM_B1 * m_ref[...] + (1.0 - ADAM_B1) * gv
        vn = ADAM_B2 * v_ref[...] + (1.0 - ADAM_B2) * (gv * gv)
        mo_ref[...] = mn
        vo_ref[...] = vn
        d_ref[...] = -ADAM_LR * ((mn / c1) / (jnp.sqrt(vn / c2) + ADAM_EPS) + ADAM_WD * w_ref[...])

    blk = pl.BlockSpec((None, tr, C), lambda l, i: (l, i, 0))
    shp = jax.ShapeDtypeStruct((L, R, C), F32)
    outs = pl.pallas_call(body, grid=(L, R // tr), in_specs=[blk, blk, blk, blk], out_specs=[blk, blk, blk], out_shape=[shp, shp, shp],
                          compiler_params=_params(("arbitrary", "arbitrary"), 32), name=name)(*[a.reshape(view) for a in (g, w, m, v)])
    return [o.reshape(shape) for o in outs]


def _pad_flat(a, n):
    a = a.reshape(-1)
    return jnp.pad(a, (0, n - a.shape[0]))


def _seg(n):
    return -(-n // FLAT_ALIGN) * FLAT_ALIGN


def _to_blocks(full, axis):
    shp = full.shape
    return jnp.moveaxis(full.reshape(shp[:axis] + (N_DEV, shp[axis] // N_DEV) + shp[axis + 1:]), axis, 0)


def _from_blocks(blocks, axis):
    b = jnp.moveaxis(blocks, 0, axis)
    shp = b.shape
    return b.reshape(shp[:axis] + (shp[axis] * shp[axis + 1],) + shp[axis + 2:])


def _as_rows(shard, n):
    return shard.T if SHARD_AXIS[n] == 2 else shard


def _with_own(lands, own, me):
    return [lax.dynamic_update_slice(land, o[None], (me, 0, 0)) for land, o in zip(lands, own)]


def kernel(x, mem, positions, norm_mix_pre, norm_mix_post, w_in, pool_w, pool_scale, conv_b_w, w_branch_a, w_branch_b, w_branch_c, w_out, norm_mem_pre, norm_mem_post, norm_memkv, w_mq, w_mkv, w_mo, norm_ffn_pre, norm_ffn_post, w_up, conv_ffn_w, w_down, loss_target, m_norm_mix_pre, m_norm_mix_post, m_w_in, m_pool_w, m_pool_scale, m_conv_b_w, m_w_branch_a, m_w_branch_b, m_w_branch_c, m_w_out, m_norm_mem_pre, m_norm_mem_post, m_norm_memkv, m_w_mq, m_w_mkv, m_w_mo, m_norm_ffn_pre, m_norm_ffn_post, m_w_up, m_conv_ffn_w, m_w_down, v_norm_mix_pre, v_norm_mix_post, v_w_in, v_pool_w, v_pool_scale, v_conv_b_w, v_w_branch_a, v_w_branch_b, v_w_branch_c, v_w_out, v_norm_mem_pre, v_norm_mem_post, v_norm_memkv, v_w_mq, v_w_mkv, v_w_mo, v_norm_ffn_pre, v_norm_ffn_post, v_w_up, v_conv_ffn_w, v_w_down):
    w = dict(norm_mix_pre=norm_mix_pre, norm_mix_post=norm_mix_post, w_in=w_in, pool_w=pool_w, pool_scale=pool_scale, conv_b_w=conv_b_w, w_branch_a=w_branch_a, w_branch_b=w_branch_b, w_branch_c=w_branch_c, w_out=w_out, norm_mem_pre=norm_mem_pre, norm_mem_post=norm_mem_post, norm_memkv=norm_memkv, w_mq=w_mq, w_mkv=w_mkv, w_mo=w_mo, norm_ffn_pre=norm_ffn_pre, norm_ffn_post=norm_ffn_post, w_up=w_up, conv_ffn_w=conv_ffn_w, w_down=w_down)
    m = dict(norm_mix_pre=m_norm_mix_pre, norm_mix_post=m_norm_mix_post, w_in=m_w_in, pool_w=m_pool_w, pool_scale=m_pool_scale, conv_b_w=m_conv_b_w, w_branch_a=m_w_branch_a, w_branch_b=m_w_branch_b, w_branch_c=m_w_branch_c, w_out=m_w_out, norm_mem_pre=m_norm_mem_pre, norm_mem_post=m_norm_mem_post, norm_memkv=m_norm_memkv, w_mq=m_w_mq, w_mkv=m_w_mkv, w_mo=m_w_mo, norm_ffn_pre=m_norm_ffn_pre, norm_ffn_post=m_norm_ffn_post, w_up=m_w_up, conv_ffn_w=m_conv_ffn_w, w_down=m_w_down)
    v = dict(norm_mix_pre=v_norm_mix_pre, norm_mix_post=v_norm_mix_post, w_in=v_w_in, pool_w=v_pool_w, pool_scale=v_pool_scale, conv_b_w=v_conv_b_w, w_branch_a=v_w_branch_a, w_branch_b=v_w_branch_b, w_branch_c=v_w_branch_c, w_out=v_w_out, norm_mem_pre=v_norm_mem_pre, norm_mem_post=v_norm_mem_post, norm_memkv=v_norm_memkv, w_mq=v_w_mq, w_mkv=v_w_mkv, w_mo=v_w_mo, norm_ffn_pre=v_norm_ffn_pre, norm_ffn_post=v_norm_ffn_post, w_up=v_w_up, conv_ffn_w=v_conv_ffn_w, w_down=v_w_down)

    me = _my_slot()
    mix_big = [n for n in BIG if n not in LATE_BIG]
    block = lambda names, l: [_as_rows(w[n][l], n).astype(MXU) for n in names]
    conv = jnp.concatenate([_pad_flat(w[n], _seg(w[n].size)) for n in F32_GATHERED]).reshape(-1, LANES)
    groups = dict(m=MERGE_BIG, b=LATE_BIG, a=mix_big)
    got0 = _all_gather("weights_all_gather_0", block(['w_in'], 0) + [conv])
    conv_all = got0[-1].reshape(N_DEV, -1)
    small, off = {n: w[n] for n in WEIGHTS if n not in SHARD_AXIS}, 0
    for n in F32_GATHERED:
        small[n] = _from_blocks(conv_all[:, off:off + w[n].size].reshape((N_DEV,) + w[n].shape), 2)
        off += _seg(w[n].size)
    whole = lambda names, got: {n: o.reshape(-1, o.shape[-1]) for n, o in zip(names, got)}
    pushes, after = {}, got0[0]
    for tag, l in (('m', 0), ('b', 0), ('a', 1), ('b', 1)):
        pushes[tag, l] = _push_start(f"weights_push_start_{l}{tag}", block(groups[tag], l), False, after)
        after = pushes[tag, l][4]

    def arrived(tag, l, done):
        return _big_operands(whole(groups[tag], _push_wait(f"weights_push_wait_{l}{tag}", pushes[tag, l], False, done)))

    ctab, stab = _rope_tables(positions[0])
    W0 = _layer_weights(whole(['w_in'], got0), small, 0)
    sv0 = _layer_fwd_branches(x[0], dict(W0, g_mix_pre=W0['g_mix_pre'] + after[0, 0]), ctab, stab)
    W0.update(arrived('m', 0, sv0['att']))
    sv0 = _layer_fwd_merge(W0, sv0)
    W0.update(arrived('b', 0, sv0['x1']))
    x1 = _layer_fwd_late(mem[0], W0, sv0)
    W1 = _layer_weights({}, small, 1)
    W1.update(arrived('a', 1, x1))
    sv1 = _layer_fwd_mix(x1, W1, ctab, stab)
    W1.update(arrived('b', 1, sv1['x1']))
    x2 = _layer_fwd_late(mem[0], W1, sv1)
    dx, acc = _loss_head(x2, loss_target[0])
    loss = lax.psum(jnp.sum(acc) * (0.5 / D), MESH_AXES)
    grads = [None] * DEPTH
    dx, grads[1] = _layer_bwd(dx, mem[0], W1, sv1, ctab, stab)
    sent = [None, [grads[1][n].reshape(N_DEV, -1, grads[1][n].shape[-1]) for n in BIG]]
    push_g = _push_start("grads_push_start_1", sent[1], True, dx)
    dx, g_late = _layer_bwd_late(dx, mem[0], dict(W0, g_ffn_post=W0['g_ffn_post'] + push_g[4][0, 0]), sv0)
    sent_late = [g_late[n].reshape(N_DEV, -1, g_late[n].shape[-1]) for n in LATE_BIG]
    push_l = _push_start("grads_push_start_0", sent_late, True, dx)
    parts, g_mix = _layer_bwd_merge(dx, dict(W0, g_mix_post=W0['g_mix_post'] + push_l[4][0, 0]), sv0)
    sent_merge = [g_mix[n].reshape(N_DEV, -1, g_mix[n].shape[-1]) for n in MERGE_BIG]
    push_m = _push_start("grads_push_start_0m", sent_merge, True, parts[0])
    du, g_br = _layer_bwd_branches(parts, dict(W0, pool_scale=W0['pool_scale'] + push_m[4][0, 0]), sv0, ctab, stab)
    g_mix.update(g_br)
    g_mix['w_in'] = _dw_in(du, sv0)
    sent_in = [g_mix['w_in'].reshape(N_DEV, -1, D)]
    push_i = _push_start("grads_push_start_in", sent_in, True, du)
    dx, g_mix['norm_mix_pre'] = _mix_pre_bwd(dx, du, dict(W0, g_mix_pre=W0['g_mix_pre'] + push_i[4][0, 0]), sv0)
    grads[0] = {**g_late, **g_mix}
    recv1 = _push_wait("grads_push_wait_1", push_g, True, dx)
    recv_late = _push_wait("grads_push_wait_0", push_l, True, dx)
    recv_merge = _push_wait("grads_push_wait_0m", push_m, True, dx)

    misc_names = [n for n in WEIGHTS if n not in BIG]
    stacked = {n: jnp.stack([grads[l][n].reshape(small[n].shape[1:]) for l in range(DEPTH)]) for n in misc_names}
    rows = [(_to_blocks(stacked[n], 2) if n in SHARD_AXIS else jnp.broadcast_to(stacked[n][None], (N_DEV,) + stacked[n].shape))
            for n in misc_names]
    segs = [_seg(w[n].size) for n in misc_names]
    misc = jnp.concatenate([jnp.pad(r.reshape(N_DEV, -1), ((0, 0), (0, s - r[0].size))) for r, s in zip(rows, segs)],
                           axis=1).reshape(N_DEV, -1, LANES)
    recv_misc = _exchange("grad_exchange_0", [misc])
    g_out, per_layer = {}, {}
    for l, names, recv in ((1, BIG, recv1), (0, LATE_BIG, recv_late), (0, MERGE_BIG, recv_merge)):
        for n, r in zip(names, recv):
            per_layer[n, l] = _sum_slots(f"sum_{n}_{l}", r)
    misc_sum = _sum_slots("sum_misc", recv_misc[0]).reshape(-1)
    off = 0
    for n, s in zip(misc_names, segs):
        g_out[n] = misc_sum[off:off + w[n].size].reshape(w[n].shape)
        off += s

    swap = lambda a: jnp.swapaxes(a, 1, 2)

    def update(n):
        if n not in BIG:
            return [g_out[n], *_adamw(f"adamw_{n}", g_out[n], w[n], m[n], v[n])]
        g = jnp.stack([per_layer[n, l] for l in range(DEPTH)])
        if SHARD_AXIS[n] == 2 and w[n].shape[2] % LANES:
            return [swap(a) for a in (g, *_adamw(f"adamw_{n}", g, swap(w[n]), swap(m[n]), swap(v[n])))]
        g = swap(g) if SHARD_AXIS[n] == 2 else g
        return [g, *_adamw(f"adamw_{n}", g, w[n], m[n], v[n])]

    done = {n: update(n) for n in WEIGHTS if n != 'w_in'}
    recv_in = _push_wait("grads_push_wait_in", push_i, True, done[WEIGHTS[-1]][1])
    per_layer['w_in', 0] = _sum_slots("sum_w_in_0", recv_in[0])
    done['w_in'] = update('w_in')
    return (loss, dx[None], *[done[n][k] for k in range(4) for n in WEIGHTS])
```

```python
import jax
import jax.numpy as jnp
from jax import lax
from jax.experimental import pallas as pl
from jax.experimental.pallas import tpu as pltpu

F32 = jnp.float32
MXU = jnp.bfloat16
HI = lax.Precision.HIGHEST

D = 1024
DEPTH = 2
POOLW = 384
ATT_W = 768
ATT_O = 256
GATE_W = 3 * D
IN_W = 6912
IN_TILE = 768
IN_ROT = (IN_W - GATE_W) // IN_TILE
MEM_W = 512
D_FF = 2816
EPS = 1e-6
ROPE_THETA = 500000.0
QB = 128
DILS = (1, 4, 16)
NEG = -1e30
MEM_SCALE = 128 ** -0.5
ATT_SCALE = 0.125

ADAM_LR, ADAM_B1, ADAM_B2, ADAM_EPS, ADAM_WD, ADAM_STEP = 0.001, 0.9, 0.999, 1e-08, 0.01, 10

N_DEV = 8
MESH_AXES = ("x", "y", "c")
LANES = 128
FLAT_ALIGN = 2048
ROW_TILE = 1024

WEIGHTS = ['norm_mix_pre', 'norm_mix_post', 'w_in', 'pool_w', 'pool_scale', 'conv_b_w', 'w_branch_a', 'w_branch_b',
           'w_branch_c', 'w_out', 'norm_mem_pre', 'norm_mem_post', 'norm_memkv', 'w_mq', 'w_mkv', 'w_mo',
           'norm_ffn_pre', 'norm_ffn_post', 'w_up', 'conv_ffn_w', 'w_down']
SHARD_AXIS = {'w_in': 2, 'conv_b_w': 2, 'w_branch_a': 2, 'w_branch_b': 2, 'w_branch_c': 2, 'w_out': 1, 'w_mq': 1,
              'w_mkv': 1, 'w_mo': 2, 'w_up': 2, 'conv_ffn_w': 2, 'w_down': 1}
F32_GATHERED = ('conv_b_w', 'conv_ffn_w')
BIG = [n for n in WEIGHTS if n in SHARD_AXIS and n not in F32_GATHERED]
LATE_BIG = ['w_mq', 'w_mkv', 'w_mo', 'w_up', 'w_down']
MERGE_BIG = ['w_branch_a', 'w_branch_b', 'w_branch_c', 'w_out']


VMEM_LIMIT_MB = 60


def _params(sem, vmem_mb):
    del vmem_mb
    return pltpu.CompilerParams(dimension_semantics=sem, vmem_limit_bytes=VMEM_LIMIT_MB << 20)


def _dot(a, b, prec=None):
    return lax.dot_general(a, b, (((1,), (0,)), ((), ())), preferred_element_type=F32, precision=prec)


def _dot_nt(a, b, prec=None):
    return lax.dot_general(a, b, (((1,), (1,)), ((), ())), preferred_element_type=F32, precision=prec)


def _dot_tn(a, b, prec=None):
    return lax.dot_general(a, b, (((0,), (0,)), ((), ())), preferred_element_type=F32, precision=prec)


def _tile(n, cap):
    if n <= cap:
        return n
    best = None
    for t in range(LANES, cap + 1, LANES):
        if n % t == 0:
            best = t
    assert best is not None, (n, cap)
    return best


def _rms(x, g):
    r = lax.rsqrt(jnp.mean(x * x, axis=-1, keepdims=True) + EPS)
    return x * r * g, r


def _rms_bwd(w, y):
    r = lax.rsqrt(jnp.mean(y * y, axis=-1, keepdims=True) + EPS)
    return r * w - y * (r * r * r) * jnp.mean(w * y, axis=-1, keepdims=True), r


def _rows_call(name, body, n_rows, ts, ins, outs, scratch=(), reverse=False, vmem_mb=48, aliases=None):
    nt = n_rows // ts
    assert nt * ts == n_rows

    def tile_of(g):
        return (nt - 1 - g) if reverse else g

    in_specs, args = [], []
    for op in ins:
        if op[0] == "t":
            _, a, cw, cb = op
            in_specs.append(pl.BlockSpec((ts, cw), lambda g, cb=cb: (tile_of(g), cb)))
        elif op[0] == "h":
            _, a, hr, cw, cb = op
            in_specs.append(pl.BlockSpec((hr, cw), lambda g, cb=cb, k=ts // hr: (jnp.maximum(tile_of(g) * k - 1, 0), cb)))
        elif op[0] == "x":
            _, a = op
            in_specs.append(pl.BlockSpec(memory_space=pl.ANY))
        else:
            _, a = op
            in_specs.append(pl.BlockSpec(a.shape, lambda g, n=a.ndim: (0,) * n))
        args.append(a)
    out_specs, out_shape = [], []
    for op in outs:
        if op[0] == "t":
            _, cols, dt = op
            out_specs.append(pl.BlockSpec((ts, cols), lambda g: (tile_of(g), 0)))
            out_shape.append(jax.ShapeDtypeStruct((n_rows, cols), dt))
        elif op[0] == "c":
            _, total, cols, cb, dt = op
            out_specs.append(pl.BlockSpec((ts, cols), lambda g, cb=cb: (tile_of(g), cb)))
            out_shape.append(jax.ShapeDtypeStruct((n_rows, total), dt))
        else:
            _, shp, dt = op
            out_specs.append(pl.BlockSpec(shp, lambda g, n=len(shp): (0,) * n))
            out_shape.append(jax.ShapeDtypeStruct(shp, dt))

    def kern(*refs):
        g = pl.program_id(0)
        body(tile_of(g), g, *refs)

    return pl.pallas_call(kern, grid=(nt,), in_specs=in_specs, out_specs=out_specs, out_shape=out_shape,
                          scratch_shapes=list(scratch), input_output_aliases=aliases or {},
                          compiler_params=_params(("arbitrary",), vmem_mb), name=name)(*args)


def _acc(ref, g, val):
    @pl.when(g == 0)
    def _():
        ref[...] = val

    @pl.when(g != 0)
    def _():
        ref[...] += val


def _norm_mm(name, x, g, w, ts, tn, out_dtype=F32, wt=False, rot=0):
    S, K = x.shape
    N = w.shape[0] if wt else w.shape[1]
    assert wt or not rot

    def body(x_ref, g_ref, w_ref, o_ref, h_ref, hs):
        @pl.when(pl.program_id(1) == 0)
        def _():
            h, _ = _rms(x_ref[...], g_ref[...])
            hs[...] = h.astype(MXU)
            h_ref[...] = h.astype(MXU)

        o_ref[...] = (_dot_nt if wt else _dot)(hs[...], w_ref[...]).astype(out_dtype)

    w_spec = pl.BlockSpec((tn, K), lambda i, j: ((j + rot) % (N // tn), 0)) if wt else pl.BlockSpec((K, tn), lambda i, j: (0, j))
    return pl.pallas_call(
        body, grid=(S // ts, N // tn),
        in_specs=[pl.BlockSpec((ts, K), lambda i, j: (i, 0)), pl.BlockSpec((1, K), lambda i, j: (0, 0)), w_spec],
        out_specs=[pl.BlockSpec((ts, tn), lambda i, j: (i, j)), pl.BlockSpec((ts, K), lambda i, j: (i, 0))],
        out_shape=[jax.ShapeDtypeStruct((S, N), out_dtype), jax.ShapeDtypeStruct((S, K), MXU)],
        scratch_shapes=[pltpu.VMEM((ts, K), MXU)],
        compiler_params=_params(("arbitrary", "arbitrary"), 48), name=name)(x, g, w)


def _mm_nt(name, a, b, ts, tn, out_dtype=F32):
    M, K = a.shape
    N = b.shape[0]

    def body(a_ref, b_ref, o_ref):
        o_ref[...] = _dot_nt(a_ref[...], b_ref[...]).astype(out_dtype)

    return pl.pallas_call(
        body, grid=(M // ts, N // tn),
        in_specs=[pl.BlockSpec((ts, K), lambda i, j: (i, 0)), pl.BlockSpec((tn, K), lambda i, j: (j, 0))],
        out_specs=pl.BlockSpec((ts, tn), lambda i, j: (i, j)), out_shape=jax.ShapeDtypeStruct((M, N), out_dtype),
        compiler_params=_params(("arbitrary", "arbitrary"), 48), name=name)(a, b)


def _mm_tn(name, a, b, cap_k=512, cap_n=1024, out_dtype=MXU, rot=0):
    S, K = a.shape
    N = b.shape[1]
    tk, tn = _tile(K, cap_k), _tile(N, cap_n)

    def body(a_ref, b_ref, o_ref):
        o_ref[...] = _dot_tn(a_ref[...], b_ref[...]).astype(out_dtype)

    return pl.pallas_call(
        body, grid=(K // tk, N // tn),
        in_specs=[pl.BlockSpec((S, tk), lambda i, j: (0, i)), pl.BlockSpec((S, tn), lambda i, j: (0, j))],
        out_specs=pl.BlockSpec((tk, tn), lambda i, j: ((i + rot) % (K // tk), j)), out_shape=jax.ShapeDtypeStruct((K, N), out_dtype),
        compiler_params=_params(("arbitrary", "arbitrary"), 48), name=name)(a, b)


def _pool_cols(shape):
    col = lax.broadcasted_iota(jnp.int32, shape, 1)
    return col < 96, col < 192, col < 288


def _pool_select(s2, s4, s8, s16):
    c1, c2, c3 = _pool_cols(s2.shape)
    return jnp.where(c1, s2, jnp.where(c2, s4, jnp.where(c3, s8, s16)))


def _pool_cnt(t0, ts):
    c1, c2, c3 = _pool_cols((ts, POOLW))
    win = jnp.where(c1, 2, jnp.where(c2, 4, jnp.where(c3, 8, 16)))
    t = t0 + lax.broadcasted_iota(jnp.int32, (ts, POOLW), 0)
    return jnp.minimum(t + 1, win).astype(F32)


def _pooled(a, prev, t0):
    ts = a.shape[0]
    ext = jnp.concatenate([prev, a], axis=0)
    s2 = ext + pltpu.roll(ext, 1, axis=0)
    s4 = s2 + pltpu.roll(s2, 2, axis=0)
    s8 = s4 + pltpu.roll(s4, 4, axis=0)
    s16 = s8 + pltpu.roll(s8, 8, axis=0)
    sums = _pool_select(s2, s4, s8, s16)[16:]
    return sums / _pool_cnt(t0, ts) - a


def _conv3(z, prev8, w):
    ext = jnp.concatenate([prev8, z], axis=0)
    z1 = pltpu.roll(ext, 1, axis=0)[8:]
    z2 = pltpu.roll(ext, 2, axis=0)[8:]
    return w[0:1] * z2 + w[1:2] * z1 + w[2:3] * z, z1, z2


def _conv3_t(dc, next8, w, shifted=False):
    ts = dc.shape[0]
    ext = jnp.concatenate([dc, next8], axis=0)
    n = ts + 8
    u1 = pltpu.roll(ext, n - 1, axis=0)[:ts]
    u2 = pltpu.roll(ext, n - 2, axis=0)[:ts]
    out = w[2:3] * dc + w[1:2] * u1 + w[0:1] * u2
    return (out, u1, u2) if shifted else out


def _poolconv_fwd(u, wblk, pool_scale, conv_b, ts=256):
    S = u.shape[0]

    def body(i, g, a_ref, bx_ref, bb_ref, bc_ref, wblk_ref, ps_ref, cw_ref, a2_ref, yb_ref, ca, cz):
        @pl.when(g == 0)
        def _():
            ca[...] = jnp.zeros_like(ca)
            cz[...] = jnp.zeros_like(cz)

        a = a_ref[...].astype(F32)
        p = _pooled(a, ca[...], i * ts)
        mixed = _dot(p.astype(MXU), wblk_ref[...])
        a2_ref[...] = (mixed * ps_ref[...]).astype(MXU)
        z = bc_ref[...].astype(F32) * bx_ref[...].astype(F32)
        conv, _, _ = _conv3(z, cz[...], cw_ref[...])
        yb_ref[...] = (bb_ref[...].astype(F32) * conv).astype(MXU)
        ca[...] = a[ts - 16:]
        cz[...] = z[ts - 8:]

    ins = [("t", u, POOLW, 8), ("t", u, POOLW, 9), ("t", u, POOLW, 10), ("t", u, POOLW, 11), ("w", wblk), ("w", pool_scale),
           ("w", conv_b)]
    return _rows_call("poolconv_fwd", body, S, ts, ins, [("t", POOLW, MXU), ("t", POOLW, MXU)],
                      scratch=[pltpu.VMEM((16, POOLW), F32), pltpu.VMEM((8, POOLW), F32)])


def _poolconv_bwd(u, d_a2, d_yb, du, wblk, pool_scale, conv_b, ts=256):
    S = u.shape[0]

    def body(i, g, a_ref, bx_ref, bb_ref, bc_ref, ap_ref, bxp_ref, bcp_ref, da2_ref, dyb_ref, wblk_ref, ps_ref, cw_ref, _,
             o_ref, dps_ref, dwb_ref, dcw_ref, ce, cdz):
        @pl.when(g == 0)
        def _():
            ce[...] = jnp.zeros_like(ce)
            cdz[...] = jnp.zeros_like(cdz)

        first = (i > 0).astype(F32)
        a = a_ref[...].astype(F32)
        p = _pooled(a, ap_ref[...].astype(F32) * first, i * ts)
        pb = p.astype(MXU)
        mixed = _dot(pb, wblk_ref[...])
        da2 = da2_ref[...]
        dmixed = (da2 * ps_ref[...]).astype(MXU)
        dp = _dot_nt(dmixed, wblk_ref[...])
        _acc(dps_ref, g, jnp.sum(da2 * mixed, axis=0, keepdims=True))
        _acc(dwb_ref, g, _dot_tn(pb, dmixed))
        e = dp / _pool_cnt(i * ts, ts)
        ext = jnp.concatenate([e, ce[...]], axis=0)
        n = ts + 16
        f2 = ext + pltpu.roll(ext, n - 1, axis=0)
        f4 = f2 + pltpu.roll(f2, n - 2, axis=0)
        f8 = f4 + pltpu.roll(f4, n - 4, axis=0)
        f16 = f8 + pltpu.roll(f8, n - 8, axis=0)
        o_ref[:, 0:POOLW] = (_pool_select(f2, f4, f8, f16)[:ts] - dp).astype(o_ref.dtype)
        ce[...] = e[:16]

        bx, bb, bc = bx_ref[...].astype(F32), bb_ref[...].astype(F32), bc_ref[...].astype(F32)
        z = bc * bx
        w = cw_ref[...]
        conv, z1, z2 = _conv3(z, (bxp_ref[...].astype(F32) * bcp_ref[...].astype(F32))[8:16] * first, w)
        dyb = dyb_ref[...]
        dconv = dyb * bb
        dz = _conv3_t(dconv, cdz[...], w)
        o_ref[:, POOLW:2 * POOLW] = (dz * bc).astype(o_ref.dtype)
        o_ref[:, 2 * POOLW:3 * POOLW] = (dyb * conv).astype(o_ref.dtype)
        o_ref[:, 3 * POOLW:4 * POOLW] = (dz * bx).astype(o_ref.dtype)
        dw = jnp.concatenate([jnp.sum(dconv * z2, axis=0, keepdims=True), jnp.sum(dconv * z1, axis=0, keepdims=True),
                              jnp.sum(dconv * z, axis=0, keepdims=True)], axis=0)
        _acc(dcw_ref, g, dw)
        cdz[...] = dconv[:8]

    ins = [("t", u, POOLW, 8), ("t", u, POOLW, 9), ("t", u, POOLW, 10), ("t", u, POOLW, 11),
           ("h", u, 16, POOLW, 8), ("h", u, 16, POOLW, 9), ("h", u, 16, POOLW, 11),
           ("t", d_a2, POOLW, 0), ("t", d_yb, POOLW, 0), ("w", wblk), ("w", pool_scale), ("w", conv_b), ("x", du)]
    outs = [("c", IN_W, 4 * POOLW, GATE_W // (4 * POOLW), MXU), ("a", (1, POOLW), F32), ("a", (POOLW, POOLW), F32), ("a", (3, POOLW), F32)]
    return _rows_call("poolconv_bwd", body, S, ts, ins, outs, aliases={len(ins) - 1: 0},
                      scratch=[pltpu.VMEM((16, POOLW), F32), pltpu.VMEM((8, POOLW), F32)], reverse=True)


def _rope_tables(positions):
    S = positions.shape[0]
    inv = ROPE_THETA ** (-jnp.arange(0, 16, 2, dtype=F32) / 16)
    ang = positions.astype(F32)[:, None] * inv
    cos, sin = jnp.cos(ang), jnp.sin(ang)
    c64 = jnp.concatenate([cos, cos, jnp.ones((S, 48), F32)], axis=1)
    s64 = jnp.concatenate([-sin, sin, jnp.zeros((S, 48), F32)], axis=1)
    return jnp.concatenate([c64, c64], axis=1), jnp.concatenate([s64, s64], axis=1)


def _partner(x):
    lane = lax.broadcasted_iota(jnp.int32, x.shape, 1) % 64
    return jnp.where(lane < 8, pltpu.roll(x, LANES - 8, axis=1), jnp.where(lane < 16, pltpu.roll(x, 8, axis=1), 0.0))


def _rope(x, c, s):
    return x * c + _partner(x) * s


def _rope_t(x, c, s):
    return x * c + _partner(x * s)


def _rows_of(r, n, d):
    return pl.ds(r, n, stride=d) if d > 1 else pl.ds(0, n)


def _head_masks(shape):
    lane = lax.broadcasted_iota(jnp.int32, shape, 1) // 64
    return [lane == h for h in range(4)]


def _only(mask, x):
    return jnp.where(mask, x, jnp.zeros_like(x))


def _rope_perm(u, ctab, stab, ts=256):
    S = u.shape[0]
    nch = ATT_W // LANES

    def body(*refs):
        chunks, (c_ref, s_ref), outs, scr = refs[:3 * nch], refs[3 * nch:3 * nch + 2], refs[3 * nch + 2:-1], refs[-1]
        for k in range(3 * nch):
            scr[k] = chunks[k][...].astype(F32)
        for g, d in enumerate(DILS):
            n = ts // d
            for r in range(d):
                rows = _rows_of(r, n, d)
                c, s = c_ref[rows, :], s_ref[rows, :]
                for which in range(3):
                    parts = [scr.at[which * nch + j][rows, :] for j in (2 * g, 2 * g + 1)]
                    if which < 2:
                        parts = [_rope(x, c, s) for x in parts]
                    outs[which * 3 + g][r] = jnp.concatenate(parts, axis=1).astype(MXU)

    base = (IN_W - 3 * ATT_W) // LANES
    in_specs = [pl.BlockSpec((ts, LANES), lambda i, cb=base + k: (i, cb)) for k in range(3 * nch)]
    in_specs += [pl.BlockSpec((ts, LANES), lambda i: (i, 0))] * 2
    out_specs = [pl.BlockSpec((d, ts // d, ATT_O), lambda i: (0, i, 0)) for _ in range(3) for d in DILS]
    out_shape = [jax.ShapeDtypeStruct((d, S // d, ATT_O), MXU) for _ in range(3) for d in DILS]
    res = pl.pallas_call(body, grid=(S // ts,), in_specs=in_specs, out_specs=out_specs, out_shape=out_shape,
                         scratch_shapes=[pltpu.VMEM((3 * nch, ts, LANES), F32)],
                         compiler_params=_params(("arbitrary",), 32), name="rope_perm")(*([u] * (3 * nch)), ctab, stab)
    return [[res[which * 3 + g].reshape(S, ATT_O) for g in range(3)] for which in range(3)]


def _rope_unperm_bwd(dqkv, du, ctab, stab, ts=256):
    S = dqkv[0][0].shape[0]
    nch = ATT_W // LANES

    def body(*refs):
        ins, (c_ref, s_ref, _, o_ref, scr) = refs[:9], refs[9:]
        for g, d in enumerate(DILS):
            n = ts // d
            for r in range(d):
                rows = _rows_of(r, n, d)
                c, s = c_ref[rows, :], s_ref[rows, :]
                for which in range(3):
                    v = ins[which * 3 + g][r]
                    for half in range(2):
                        x = v[:, half * LANES:(half + 1) * LANES]
                        scr.at[which * nch + 2 * g + half][rows, :] = _rope_t(x, c, s) if which < 2 else x
        for j in range(3 * nch):
            o_ref[:, j * LANES:(j + 1) * LANES] = scr[j].astype(o_ref.dtype)

    in_specs = [pl.BlockSpec((d, ts // d, ATT_O), lambda i: (0, i, 0)) for _ in range(3) for d in DILS]
    in_specs += [pl.BlockSpec((ts, LANES), lambda i: (i, 0))] * 2 + [pl.BlockSpec(memory_space=pl.ANY)]
    args = [dqkv[which][g].reshape(d, S // d, ATT_O) for which in range(3) for g, d in enumerate(DILS)]
    last = (IN_W - 3 * ATT_W) // (3 * ATT_W)
    return pl.pallas_call(body, grid=(S // ts,), in_specs=in_specs, out_specs=pl.BlockSpec((ts, 3 * ATT_W), lambda i: (i, last)),
                          out_shape=jax.ShapeDtypeStruct((S, IN_W), MXU), scratch_shapes=[pltpu.VMEM((3 * nch, ts, LANES), F32)],
                          input_output_aliases={len(in_specs) - 1: 0},
                          compiler_params=_params(("arbitrary",), 32), name="rope_unperm_bwd")(*args, ctab, stab, du)


def _band_mask_keys(has_prev):
    r = lax.broadcasted_iota(jnp.int32, (QB, 2 * QB), 0)
    c = lax.broadcasted_iota(jnp.int32, (QB, 2 * QB), 1)
    return ((c < QB) & (c >= r) & has_prev) | ((c >= QB) & (c - QB <= r))


def _band_mask_queries(has_next):
    r = lax.broadcasted_iota(jnp.int32, (2 * QB, QB), 0)
    c = lax.broadcasted_iota(jnp.int32, (2 * QB, QB), 1)
    return ((r < QB) & (c <= r)) | ((r >= QB) & (c >= r - QB) & has_next)


ASUB = 4
_BIG = pl.BlockSpec((ASUB * QB, ATT_O), lambda b: (b, 0))
_PREV = pl.BlockSpec((QB, ATT_O), lambda b: (jnp.maximum(b * ASUB - 1, 0), 0))


def _sub(ref, j):
    return ref[j * QB:(j + 1) * QB]


def _attn_fwd(g, q, k, v):
    S = q.shape[0]
    nb = S // QB
    nblk = nb // DILS[g]

    def body(q_ref, kc_ref, kp_ref, vc_ref, vp_ref, o_ref, m_ref, l_ref):
        hm_kv, hm_o = _head_masks((2 * QB, ATT_O)), _head_masks((QB, ATT_O))
        for j in range(ASUB):
            ok = _band_mask_keys(((pl.program_id(0) * ASUB + j) & (nblk - 1)) > 0)
            k2 = jnp.concatenate([kp_ref[...] if j == 0 else _sub(kc_ref, j - 1), _sub(kc_ref, j)], axis=0)
            v2 = jnp.concatenate([vp_ref[...] if j == 0 else _sub(vc_ref, j - 1), _sub(vc_ref, j)], axis=0)
            qv = _sub(q_ref, j)
            o_acc = jnp.zeros((QB, ATT_O), F32)
            m_acc = jnp.zeros((QB, ATT_O), F32)
            l_acc = jnp.zeros((QB, ATT_O), F32)
            for h in range(4):
                s = jnp.where(ok, _dot_nt(qv, _only(hm_kv[h], k2)) * ATT_SCALE, NEG)
                m = jnp.max(s, axis=1, keepdims=True)
                p = jnp.exp(s - m)
                o_acc = o_acc + _dot(p.astype(MXU), _only(hm_kv[h], v2))
                m_acc = jnp.where(hm_o[h], m, m_acc)
                l_acc = jnp.where(hm_o[h], jnp.sum(p, axis=1, keepdims=True), l_acc)
            o_ref[j * QB:(j + 1) * QB] = o_acc
            m_ref[j * QB:(j + 1) * QB] = m_acc
            l_ref[j * QB:(j + 1) * QB] = l_acc

    shp = jax.ShapeDtypeStruct((S, ATT_O), F32)
    return pl.pallas_call(body, grid=(nb // ASUB,), in_specs=[_BIG, _BIG, _PREV, _BIG, _PREV],
                          out_specs=[_BIG] * 3, out_shape=[shp, shp, shp], compiler_params=_params(("arbitrary",), 32),
                          name=f"attn_fwd_{g}")(q, k, k, v, v)


def _natural(ref, d, scr, ts):
    if d == 1:
        return ref[0]
    n = ts // d
    for r in range(d):
        v = ref[r]
        scr.at[0][pl.ds(r, n, stride=d), :] = v[:, 0:LANES]
        scr.at[1][pl.ds(r, n, stride=d), :] = v[:, LANES:2 * LANES]
    return jnp.concatenate([scr[0], scr[1]], axis=1)


def _attn_combine(oml, ts=256):
    S = oml[0][0].shape[0]

    def body(*refs):
        ins, (att_ref, out_ref, lse_ref, scr) = refs[:9], refs[9:]
        o, m, l = [[_natural(ins[3 * g + k], d, scr, ts) for g, d in enumerate(DILS)] for k in range(3)]
        mx = jnp.maximum(jnp.maximum(m[0], m[1]), m[2])
        w = [jnp.exp(m[g] - mx) for g in range(3)]
        den = w[0] * l[0] + w[1] * l[1] + w[2] * l[2]
        out = (w[0] * o[0] + w[1] * o[1] + w[2] * o[2]) / den
        out_ref[...] = out
        att_ref[...] = out.astype(MXU)
        lse_ref[...] = mx + jnp.log(den)

    in_specs = [pl.BlockSpec((d, ts // d, ATT_O), lambda i: (0, i, 0)) for d in DILS for _ in range(3)]
    args = [a.reshape(d, S // d, ATT_O) for d, grp in zip(DILS, oml) for a in grp]
    blk = pl.BlockSpec((ts, ATT_O), lambda i: (i, 0))
    return pl.pallas_call(body, grid=(S // ts,), in_specs=in_specs, out_specs=[blk, blk, blk],
                          out_shape=[jax.ShapeDtypeStruct((S, ATT_O), MXU), jax.ShapeDtypeStruct((S, ATT_O), F32),
                                     jax.ShapeDtypeStruct((S, ATT_O), F32)],
                          scratch_shapes=[pltpu.VMEM((2, ts, LANES), F32)], compiler_params=_params(("arbitrary",), 32),
                          name="attn_combine")(*args)


def _attn_bwd_prep(datt, o, lse, ts=256):
    S = datt.shape[0]

    def body(da0, da1, o_ref, l0, l1, *rest):
        outs, dl = rest[:9], rest[9]
        prod = jnp.concatenate([da0[...], da1[...]], axis=1) * o_ref[...]
        delta = jnp.zeros((ts, ATT_O), F32)
        for hm in _head_masks((ts, ATT_O)):
            delta = jnp.where(hm, jnp.sum(_only(hm, prod), axis=1, keepdims=True), delta)
        dl[0] = delta[:, 0:LANES]
        dl[1] = delta[:, LANES:2 * LANES]
        for g, d in enumerate(DILS):
            n = ts // d
            for r in range(d):
                rows = _rows_of(r, n, d)
                outs[g][r] = jnp.concatenate([da0[rows, :], da1[rows, :]], axis=1).astype(MXU)
                outs[3 + g][r] = jnp.concatenate([dl.at[0][rows, :], dl.at[1][rows, :]], axis=1)
                outs[6 + g][r] = jnp.concatenate([l0[rows, :], l1[rows, :]], axis=1)

    half = lambda j: pl.BlockSpec((ts, LANES), lambda i: (i, j))
    out_specs = [pl.BlockSpec((d, ts // d, ATT_O), lambda i: (0, i, 0)) for _ in range(3) for d in DILS]
    out_shape = [jax.ShapeDtypeStruct((d, S // d, ATT_O), dt) for dt in (MXU, F32, F32) for d in DILS]
    res = pl.pallas_call(body, grid=(S // ts,), in_specs=[half(0), half(1), pl.BlockSpec((ts, ATT_O), lambda i: (i, 0)), half(0), half(1)],
                         out_specs=out_specs, out_shape=out_shape, scratch_shapes=[pltpu.VMEM((2, ts, LANES), F32)],
                         compiler_params=_params(("arbitrary",), 32), name="attn_bwd_prep")(datt, datt, o, lse, lse)
    return [[res[k * 3 + g].reshape(S, ATT_O) for g in range(3)] for k in range(3)]


def _head_col(x, h):
    return x[:, h * 64:h * 64 + 1]


def _attn_bwd(g, q, k, v, do, delta, lse):
    S = q.shape[0]
    nb = S // QB
    nblk = nb // DILS[g]

    def body(k_ref, v_ref, qc_ref, qn_ref, doc_ref, don_ref, dlc_ref, dln_ref, lc_ref, ln_ref, dq_ref, dk_ref, dv_ref, dq_scr):
        hms, hmk = _head_masks((2 * QB, ATT_O)), _head_masks((QB, ATT_O))
        first = pl.program_id(0) == 0

        @pl.when(first)
        def _():
            dq_scr[0:QB] = jnp.zeros((QB, ATT_O), F32)

        @pl.when(jnp.logical_not(first))
        def _():
            dq_scr[0:QB] = dq_scr[ASUB * QB:(ASUB + 1) * QB]

        dq_scr[QB:(ASUB + 1) * QB] = jnp.zeros((ASUB * QB, ATT_O), F32)

        def both(cur_ref, nxt_ref, j):
            return jnp.concatenate([_sub(cur_ref, j), nxt_ref[...] if j == ASUB - 1 else _sub(cur_ref, j + 1)], axis=0)

        for j in range(ASUB):
            ok = _band_mask_queries(((pl.program_id(0) * ASUB + j + 1) & (nblk - 1)) > 0)
            q2, do2, dl2, lse2 = both(qc_ref, qn_ref, j), both(doc_ref, don_ref, j), both(dlc_ref, dln_ref, j), both(lc_ref, ln_ref, j)
            kv, vv = _sub(k_ref, j), _sub(v_ref, j)
            dk = jnp.zeros((QB, ATT_O), F32)
            dv = jnp.zeros((QB, ATT_O), F32)
            dq2 = jnp.zeros((2 * QB, ATT_O), F32)
            for h, hm in enumerate(hms):
                qh, doh = _only(hm, q2), _only(hm, do2)
                p = jnp.where(ok, jnp.exp(_dot_nt(qh, kv) * ATT_SCALE - _head_col(lse2, h)), 0.0)
                ds = (p * (_dot_nt(doh, vv) - _head_col(dl2, h))).astype(MXU)
                dv = dv + _dot_tn(p.astype(MXU), doh)
                dk = dk + _dot_tn(ds, qh)
                dq2 = dq2 + _dot(ds, _only(hmk[h], kv))
            dk_ref[j * QB:(j + 1) * QB] = dk * ATT_SCALE
            dv_ref[j * QB:(j + 1) * QB] = dv
            dq_scr[j * QB:(j + 2) * QB] += dq2
        dq_ref[...] = dq_scr[0:ASUB * QB] * ATT_SCALE

    nxt = pl.BlockSpec((QB, ATT_O), lambda b: (jnp.minimum((b + 1) * ASUB, nb - 1), 0))
    shp = jax.ShapeDtypeStruct((S, ATT_O), F32)
    return pl.pallas_call(body, grid=(nb // ASUB,), in_specs=[_BIG, _BIG, _BIG, nxt, _BIG, nxt, _BIG, nxt, _BIG, nxt], out_specs=[_BIG] * 3,
                          out_shape=[shp, shp, shp], scratch_shapes=[pltpu.VMEM(((ASUB + 1) * QB, ATT_O), F32)],
                          compiler_params=_params(("arbitrary",), 32), name=f"attn_bwd_{g}")(k, v, q, q, do, do, delta, delta, lse, lse)


def _merge_fwd(x0, u, a2, yb, att, wa, wb, wc, w_out, g_post, ts=256):
    S = x0.shape[0]

    def body(i, g, x_ref, gate_ref, a2_ref, yb_ref, att_ref, wa_ref, wb_ref, wc_ref, wo_ref, gp_ref, mg_ref, y_ref, xo_ref):
        gate = lambda n: jax.nn.sigmoid(gate_ref[:, n * D:(n + 1) * D].astype(F32))
        merged = gate(0) * _dot_nt(a2_ref[...], wa_ref[...])
        merged = merged + gate(1) * _dot_nt(yb_ref[...], wb_ref[...])
        merged = merged + gate(2) * _dot_nt(att_ref[...], wc_ref[...])
        mb = merged.astype(MXU)
        mg_ref[...] = mb
        y = _dot(mb, wo_ref[...])
        y_ref[...] = y
        xo_ref[...] = x_ref[...] + _rms(y, gp_ref[...])[0]

    ins = [("t", x0, D, 0), ("t", u, GATE_W, 0), ("t", a2, POOLW, 0), ("t", yb, POOLW, 0), ("t", att, ATT_O, 0),
           ("w", wa), ("w", wb), ("w", wc), ("w", w_out), ("w", g_post)]
    return _rows_call("merge_fwd", body, S, ts, ins, [("t", D, MXU), ("t", D, F32), ("t", D, F32)])


def _merge_bwd(dx, y1, u, a2, yb, att, wa, wb, wc, w_out, g_post, ts=256):
    S = dx.shape[0]

    def body(i, g, dx_ref, y_ref, gate_ref, a2_ref, yb_ref, att_ref, wa_ref, wb_ref, wc_ref, wo_ref, gp_ref,
             dy_ref, dgate_ref, dbra_ref, dbrb_ref, dbrc_ref, da2_ref, dyb_ref, datt_ref, dgp_ref):
        dxv, y = dx_ref[...], y_ref[...]
        dy, r = _rms_bwd(dxv * gp_ref[...], y)
        _acc(dgp_ref, g, jnp.sum(dxv * (y * r), axis=0, keepdims=True))
        dyb16 = dy.astype(MXU)
        dy_ref[...] = dyb16
        dm = _dot_nt(dyb16, wo_ref[...])
        for n, (src, w_ref, dbr_ref, din_ref) in enumerate(((a2_ref, wa_ref, dbra_ref, da2_ref), (yb_ref, wb_ref, dbrb_ref, dyb_ref),
                                                           (att_ref, wc_ref, dbrc_ref, datt_ref))):
            gt = jax.nn.sigmoid(gate_ref[:, n * D:(n + 1) * D].astype(F32))
            br = _dot_nt(src[...], w_ref[...])
            dgate_ref[:, n * D:(n + 1) * D] = (dm * br * gt * (1.0 - gt)).astype(dgate_ref.dtype)
            dbr = (dm * gt).astype(MXU)
            dbr_ref[...] = dbr
            din_ref[...] = _dot(dbr, w_ref[...])

    ins = [("t", dx, D, 0), ("t", y1, D, 0), ("t", u, GATE_W, 0), ("t", a2, POOLW, 0), ("t", yb, POOLW, 0), ("t", att, ATT_O, 0),
           ("w", wa), ("w", wb), ("w", wc), ("w", w_out), ("w", g_post)]
    outs = [("t", D, MXU), ("c", IN_W, GATE_W, 0, MXU), ("t", D, MXU), ("t", D, MXU), ("t", D, MXU), ("t", POOLW, F32), ("t", POOLW, F32),
            ("t", ATT_O, F32), ("a", (1, D), F32)]
    return _rows_call("merge_bwd", body, S, ts, ins, outs)


def _prenorm_bwd(name, dx_res, du, wt, x, g_pre, ts=256, lead=0):
    S = x.shape[0]
    N = du.shape[1]

    def body(i, g, dx_ref, du_ref, wt_ref, x_ref, g_ref, o_ref, dg_ref):
        if lead:
            dhv = _dot(du_ref[:, 0:lead], wt_ref[N - lead:N, :]) + _dot(du_ref[:, lead:N], wt_ref[0:N - lead, :])
        else:
            dhv = _dot(du_ref[...], wt_ref[...])
        xv = x_ref[...]
        dxn, r = _rms_bwd(dhv * g_ref[...], xv)
        o_ref[...] = dx_ref[...] + dxn
        _acc(dg_ref, g, jnp.sum(dhv * (xv * r), axis=0, keepdims=True))

    ins = [("t", dx_res, D, 0), ("t", du, N, 0), ("w", wt), ("t", x, D, 0), ("w", g_pre)]
    return _rows_call(name, body, S, ts, ins, [("t", D, F32), ("a", (1, D), F32)], vmem_mb=52)


def _mem_heads(qm, kv_ref):
    out = []
    for h in range(4):
        q = qm[:, h * 128:(h + 1) * 128].astype(MXU)
        k = kv_ref[:, h * 128:(h + 1) * 128]
        v = kv_ref[:, MEM_W + h * 128:MEM_W + (h + 1) * 128]
        sc = _dot_nt(q, k) * MEM_SCALE
        e = jnp.exp(sc - jnp.max(sc, axis=1, keepdims=True))
        out.append((e / jnp.sum(e, axis=1, keepdims=True), q, k, v))
    return out


def _mem_fwd(x1, kv, g_pre, w_mq, w_mo, g_post, ts=256):
    S = x1.shape[0]

    def body(i, g, x_ref, kv_ref, gq_ref, wq_ref, wo_ref, gp_ref, om_ref, h_ref, y_ref, xo_ref):
        x = x_ref[...]
        hb = _rms(x, gq_ref[...])[0].astype(MXU)
        h_ref[...] = hb
        qm = _dot(hb, wq_ref[...])
        om = jnp.concatenate([_dot(p.astype(MXU), v) for p, _, _, v in _mem_heads(qm, kv_ref)], axis=1).astype(MXU)
        om_ref[...] = om
        y = _dot_nt(om, wo_ref[...])
        y_ref[...] = y
        xo_ref[...] = x + _rms(y, gp_ref[...])[0]

    ins = [("t", x1, D, 0), ("w", kv), ("w", g_pre), ("w", w_mq), ("w", w_mo), ("w", g_post)]
    return _rows_call("mem_fwd", body, S, ts, ins, [("t", MEM_W, MXU), ("t", D, MXU), ("t", D, F32), ("t", D, F32)])


def _mem_bwd(dx2, ym, x1, kv, g_pre, w_mq, w_mo, g_post, ts=256):
    S = x1.shape[0]

    def body(i, g, dx_ref, y_ref, x_ref, kv_ref, gq_ref, wq_ref, wo_ref, gp_ref, dy_ref, dq_ref, dxo_ref, dgp_ref, dgq_ref, dkv_ref):
        dxv, y, x = dx_ref[...], y_ref[...], x_ref[...]
        dy, r = _rms_bwd(dxv * gp_ref[...], y)
        _acc(dgp_ref, g, jnp.sum(dxv * (y * r), axis=0, keepdims=True))
        dyb = dy.astype(MXU)
        dy_ref[...] = dyb
        dom = _dot(dyb, wo_ref[...])
        h, r1 = _rms(x, gq_ref[...])
        qm = _dot(h.astype(MXU), wq_ref[...])
        dqs = []

        @pl.when(g == 0)
        def _():
            dkv_ref[...] = jnp.zeros_like(dkv_ref)

        for hh, (p, q, k, v) in enumerate(_mem_heads(qm, kv_ref)):
            doh = dom[:, hh * 128:(hh + 1) * 128].astype(MXU)
            dp = _dot_nt(doh, v)
            dsc = (p * (dp - jnp.sum(dp * p, axis=1, keepdims=True)) * MEM_SCALE).astype(MXU)
            dqs.append(_dot(dsc, k))
            dkv_ref[:, hh * 128:(hh + 1) * 128] += _dot_tn(dsc, q)
            dkv_ref[:, MEM_W + hh * 128:MEM_W + (hh + 1) * 128] += _dot_tn(p.astype(MXU), doh)
        dq = jnp.concatenate(dqs, axis=1).astype(MXU)
        dq_ref[...] = dq
        dh = _dot_nt(dq, wq_ref[...])
        _acc(dgq_ref, g, jnp.sum(dh * (x * r1), axis=0, keepdims=True))
        dxo_ref[...] = dxv + _rms_bwd(dh * gq_ref[...], x)[0]

    ins = [("t", dx2, D, 0), ("t", ym, D, 0), ("t", x1, D, 0), ("w", kv), ("w", g_pre), ("w", w_mq), ("w", w_mo), ("w", g_post)]
    outs = [("t", D, MXU), ("t", MEM_W, MXU), ("t", D, F32), ("a", (1, D), F32), ("a", (1, D), F32), ("a", (256, D), F32)]
    return _rows_call("mem_bwd", body, S, ts, ins, outs)


def _gain_grad(name, dn, x):
    n = x.shape[0]

    def body(i, g, dn_ref, x_ref, o_ref):
        xv = x_ref[...]
        r = lax.rsqrt(jnp.mean(xv * xv, axis=-1, keepdims=True) + EPS)
        o_ref[...] = jnp.sum(dn_ref[...] * (xv * r), axis=0, keepdims=True)

    return _rows_call(name, body, n, n, [("t", dn, D, 0), ("t", x, D, 0)], [("a", (1, D), F32)])[0]


def _ffn_fwd(x2, u3, conv_f, w_down, g_post, ts=256):
    S = x2.shape[0]

    def body(i, g, x_ref, ua_ref, ub_ref, cw_ref, wd_ref, gp_ref, act_ref, y_ref, xo_ref, c_ref, cu):
        @pl.when(g == 0)
        def _():
            cu[...] = jnp.zeros_like(cu)

        ua = ua_ref[...].astype(F32)
        c, _, _ = _conv3(ua, cu[...], cw_ref[...])
        c_ref[...] = c.astype(MXU)
        act = (c * jax.nn.sigmoid(c) * ub_ref[...].astype(F32)).astype(MXU)
        act_ref[...] = act
        y = _dot(act, wd_ref[...])
        y_ref[...] = y
        xo_ref[...] = x_ref[...] + _rms(y, gp_ref[...])[0]
        cu[...] = ua[ts - 8:]

    ins = [("t", x2, D, 0), ("t", u3, D_FF, 0), ("t", u3, D_FF, 1), ("w", conv_f), ("w", w_down), ("w", g_post)]
    return _rows_call("ffn_fwd", body, S, ts, ins, [("t", D_FF, MXU), ("t", D, F32), ("t", D, F32), ("t", D_FF, MXU)],
                      scratch=[pltpu.VMEM((8, D_FF), F32)], vmem_mb=56)


def _ffn_bwd(dx3, y3, u3, c, conv_f, w_down, g_post, ts=128):
    S = dx3.shape[0]

    def body(i, g, dx_ref, y_ref, ua_ref, ub_ref, c_ref, cw_ref, wd_ref, gp_ref, dy_ref, du_ref, dgp_ref, dcw_ref, cdc):
        @pl.when(g == 0)
        def _():
            cdc[...] = jnp.zeros_like(cdc)

        dxv, y = dx_ref[...], y_ref[...]
        dy, r = _rms_bwd(dxv * gp_ref[...], y)
        _acc(dgp_ref, g, jnp.sum(dxv * (y * r), axis=0, keepdims=True))
        dyb = dy.astype(MXU)
        dy_ref[...] = dyb
        dact = _dot_nt(dyb, wd_ref[...])
        ua, c, w = ua_ref[...].astype(F32), c_ref[...].astype(F32), cw_ref[...]
        sg = jax.nn.sigmoid(c)
        du_ref[:, D_FF:2 * D_FF] = (dact * (c * sg)).astype(du_ref.dtype)
        dc = dact * ub_ref[...].astype(F32) * (sg * (1.0 + c * (1.0 - sg)))
        dua, dc1, dc2 = _conv3_t(dc, cdc[...], w, shifted=True)
        du_ref[:, 0:D_FF] = dua.astype(du_ref.dtype)
        dw = jnp.concatenate([jnp.sum(ua * dc2, axis=0, keepdims=True), jnp.sum(ua * dc1, axis=0, keepdims=True),
                              jnp.sum(ua * dc, axis=0, keepdims=True)], axis=0)
        _acc(dcw_ref, g, dw)
        cdc[...] = dc[:8]

    ins = [("t", dx3, D, 0), ("t", y3, D, 0), ("t", u3, D_FF, 0), ("t", u3, D_FF, 1), ("t", c, D_FF, 0), ("w", conv_f),
           ("w", w_down), ("w", g_post)]
    outs = [("t", D, MXU), ("t", 2 * D_FF, MXU), ("a", (1, D), F32), ("a", (3, D_FF), F32)]
    return _rows_call("ffn_bwd", body, S, ts, ins, outs, scratch=[pltpu.VMEM((8, D_FF), F32)], reverse=True, vmem_mb=56)


def _loss_head(x, target, ts=512):
    S = x.shape[0]

    def body(i, g, x_ref, t_ref, dx_ref, acc_ref):
        diff = x_ref[...] - t_ref[...]
        dx_ref[...] = diff * (1.0 / D)
        col = jnp.sum(diff * diff, axis=0, keepdims=True)
        part = col[:, 0:LANES]
        for j in range(1, D // LANES):
            part = part + col[:, j * LANES:(j + 1) * LANES]
        row = lax.broadcasted_iota(jnp.int32, (8, LANES), 0)
        _acc(acc_ref, g, jnp.where(row == 0, jnp.broadcast_to(part, (8, LANES)), 0.0))

    return _rows_call("loss_head", body, S, ts, [("t", x, D, 0), ("t", target, D, 0)], [("t", D, F32), ("a", (8, LANES), F32)])


_OPERAND_NAME = dict(w_in='w_in', w_branch_a='wa', w_branch_b='wb', w_branch_c='wc', w_out='w_out', w_mq='w_mq', w_mkv='w_mkv',
                     w_mo='w_mo', w_up='w_up', w_down='w_down')


def _big_operands(big):
    return {_OPERAND_NAME[n]: a for n, a in big.items()}


def _layer_weights(big, small, l):
    pool_w = small['pool_w'][l].astype(MXU)
    wblk = jnp.zeros((POOLW, POOLW), MXU)
    for g in range(4):
        wblk = lax.dynamic_update_slice(wblk, pool_w[g], (g * 96, g * 96))
    vec = lambda n: small[n][l].reshape(1, -1)
    return dict(
        _big_operands(big),
        wblk=wblk, pool_scale=vec('pool_scale'), conv_b=small['conv_b_w'][l], conv_f=small['conv_ffn_w'][l],
        g_mix_pre=vec('norm_mix_pre'), g_mix_post=vec('norm_mix_post'), g_mem_pre=vec('norm_mem_pre'),
        g_mem_post=vec('norm_mem_post'), g_memkv=vec('norm_memkv'), g_ffn_pre=vec('norm_ffn_pre'), g_ffn_post=vec('norm_ffn_post'))


def _layer_fwd(x0, mem, W, ctab, stab):
    sv = _layer_fwd_mix(x0, W, ctab, stab)
    return _layer_fwd_late(mem, W, sv), sv


def _layer_fwd_mix(x0, W, ctab, stab):
    return _layer_fwd_merge(W, _layer_fwd_branches(x0, W, ctab, stab))


def _layer_fwd_branches(x0, W, ctab, stab):
    sv = dict(x0=x0)
    sv['u'], sv['h1'] = _norm_mm("in_proj", x0, W['g_mix_pre'], W['w_in'], ts=1024, tn=IN_TILE, wt=True, rot=IN_ROT, out_dtype=MXU)
    sv['a2'], sv['yb'] = _poolconv_fwd(sv['u'], W['wblk'], W['pool_scale'], W['conv_b'])
    sv['qkv'] = q3, k3, v3 = _rope_perm(sv['u'], ctab, stab)
    sv['att'], sv['o'], sv['lse'] = _attn_combine([_attn_fwd(g, q3[g], k3[g], v3[g]) for g in range(3)])
    return sv


def _layer_fwd_merge(W, sv):
    sv['merged'], sv['y1'], sv['x1'] = _merge_fwd(sv['x0'], sv['u'], sv['a2'], sv['yb'], sv['att'], W['wa'], W['wb'], W['wc'],
                                                  W['w_out'], W['g_mix_post'])
    return sv


def _layer_fwd_late(mem, W, sv):
    sv['kv'], sv['memn'] = _norm_mm("mem_kv", mem, W['g_memkv'], W['w_mkv'], ts=256, tn=D, out_dtype=MXU)
    sv['om'], sv['h2'], sv['ym'], sv['x2'] = _mem_fwd(sv['x1'], sv['kv'], W['g_mem_pre'], W['w_mq'], W['w_mo'], W['g_mem_post'])
    sv['u3'], sv['h3'] = _norm_mm("up_proj", sv['x2'], W['g_ffn_pre'], W['w_up'], ts=1024, tn=1408, wt=True, out_dtype=MXU)
    sv['act'], sv['y3'], x3, sv['c3'] = _ffn_fwd(sv['x2'], sv['u3'], W['conv_f'], W['w_down'], W['g_ffn_post'])
    return x3


def _layer_bwd(dx3, mem, W, sv, ctab, stab):
    dx1, g = _layer_bwd_late(dx3, mem, W, sv)
    dx0, g_mix = _layer_bwd_mix(dx1, W, sv, ctab, stab)
    return dx0, {**g, **g_mix}


def _layer_bwd_late(dx3, mem, W, sv):
    g = {}
    dy3, du3, g['norm_ffn_post'], g['conv_ffn_w'] = _ffn_bwd(dx3, sv['y3'], sv['u3'], sv['c3'], W['conv_f'], W['w_down'], W['g_ffn_post'])
    g['w_down'] = _mm_tn("dw_down", sv['act'], dy3, cap_k=256)
    g['w_up'] = _mm_tn("dw_up", du3, sv['h3'])
    dx2, g['norm_ffn_pre'] = _prenorm_bwd("ffn_pre_bwd", dx3, du3, W['w_up'], sv['x2'], W['g_ffn_pre'])
    dym, dqm, dx1, g['norm_mem_post'], g['norm_mem_pre'], dkv = _mem_bwd(dx2, sv['ym'], sv['x1'], sv['kv'], W['g_mem_pre'],
                                                                       W['w_mq'], W['w_mo'], W['g_mem_post'])
    g['w_mo'] = _mm_tn("dw_mo", dym, sv['om'])
    g['w_mq'] = _mm_tn("dw_mq", sv['h2'], dqm)
    dkvb = dkv.astype(MXU)
    g['w_mkv'] = _mm_tn("dw_mkv", sv['memn'], dkvb)
    g['norm_memkv'] = _gain_grad("memkv_gain", _mm_nt("d_memn", dkvb, W['w_mkv'], ts=256, tn=512), mem)
    return dx1, g


def _layer_bwd_mix(dx1, W, sv, ctab, stab):
    du, g = _layer_bwd_mixers(dx1, W, sv, ctab, stab)
    g['w_in'] = _dw_in(du, sv)
    dx0, g['norm_mix_pre'] = _mix_pre_bwd(dx1, du, W, sv)
    return dx0, g


def _dw_in(du, sv):
    return _mm_tn("dw_in", du, sv['h1'], cap_k=IN_TILE, rot=IN_ROT)


def _mix_pre_bwd(dx1, du, W, sv):
    return _prenorm_bwd("mix_pre_bwd", dx1, du, W['w_in'], sv['x0'], W['g_mix_pre'], lead=GATE_W)


def _layer_bwd_mixers(dx1, W, sv, ctab, stab):
    parts, g = _layer_bwd_merge(dx1, W, sv)
    du, g_br = _layer_bwd_branches(parts, W, sv, ctab, stab)
    return du, {**g, **g_br}


def _layer_bwd_merge(dx1, W, sv):
    g = {}
    dy1, du, dbra, dbrb, dbrc, da2, dyb, datt, g['norm_mix_post'] = _merge_bwd(
        dx1, sv['y1'], sv['u'], sv['a2'], sv['yb'], sv['att'], W['wa'], W['wb'], W['wc'], W['w_out'], W['g_mix_post'])
    g['w_out'] = _mm_tn("dw_out", sv['merged'], dy1)
    g['w_branch_a'] = _mm_tn("dw_a", dbra, sv['a2'])
    g['w_branch_b'] = _mm_tn("dw_b", dbrb, sv['yb'])
    g['w_branch_c'] = _mm_tn("dw_c", dbrc, sv['att'])
    return (du, da2, dyb, datt), g


def _layer_bwd_branches(parts, W, sv, ctab, stab):
    du, da2, dyb, datt = parts
    g = {}
    du, g['pool_scale'], dwblk, g['conv_b_w'] = _poolconv_bwd(sv['u'], da2, dyb, du, W['wblk'], W['pool_scale'], W['conv_b'])
    g['pool_w'] = jnp.stack([dwblk[k * 96:(k + 1) * 96, k * 96:(k + 1) * 96] for k in range(4)])
    q3, k3, v3 = sv['qkv']
    do3, dl3, lse3 = _attn_bwd_prep(datt, sv['o'], sv['lse'])
    dqkv3 = [_attn_bwd(i, q3[i], k3[i], v3[i], do3[i], dl3[i], lse3[i]) for i in range(3)]
    du = _rope_unperm_bwd([[t[which] for t in dqkv3] for which in range(3)], du, ctab, stab)
    return du, g


def _local_step(x, mem, positions, target, big, small):
    ctab, stab = _rope_tables(positions)
    Ws = [_layer_weights(big[l], small, l) for l in range(DEPTH)]
    saved = []
    for l in range(DEPTH):
        x, sv = _layer_fwd(x, mem, Ws[l], ctab, stab)
        saved.append(sv)
    dx, acc = _loss_head(x, target)
    loss = jnp.sum(acc) * (0.5 / D)
    grads = [None] * DEPTH
    for l in reversed(range(DEPTH)):
        dx, grads[l] = _layer_bwd(dx, mem, Ws[l], saved[l], ctab, stab)
    return loss, dx, grads


_HBM = pl.BlockSpec(memory_space=pl.ANY)
MESH_ID = pl.DeviceIdType.MESH


def _all_gather(name, xs):
    n = len(xs)

    def body(*refs):
        x_refs, out_refs = refs[:n], refs[n:2 * n]
        send_sems, recv_sems, local_sems = refs[2 * n:]
        x, y, c = lax.axis_index("x"), lax.axis_index("y"), lax.axis_index("c")
        me, sibling = (x, y, c), (x, y, 1 - c)
        chips = [(1 - x, y), (x, 1 - y), (1 - x, 1 - y)]

        def slot(a, p):
            return out_refs[a].at[4 * p[0] + 2 * p[1] + p[2]]

        def copy(a, k, block, to, src=None):
            return pltpu.make_async_remote_copy(src_ref=slot(a, block) if src is None else src, dst_ref=slot(a, block),
                                                send_sem=send_sems.at[a, k], recv_sem=recv_sems.at[a, k], device_id=to,
                                                device_id_type=MESH_ID)

        started = []
        for a in range(n):
            mine = pltpu.make_async_copy(x_refs[a], slot(a, me), local_sems.at[a])
            mine.start()
            started.append(mine)
        first = []
        for a in range(n):
            first.append(copy(a, 0, me, sibling, src=x_refs[a]))
            first += [copy(a, 1 + j, me, (*chip, c), src=x_refs[a]) for j, chip in enumerate(chips)]
        for cp in first:
            cp.start()
        passed = []
        for j, chip in enumerate(chips):
            for a in range(n):
                copy(a, 1 + j, (*chip, c), me).wait_recv()
                fw = copy(a, 4 + j, (*chip, c), sibling)
                fw.start()
                passed.append(fw)
        for a in range(n):
            copy(a, 0, sibling, me).wait_recv()
            for j, chip in enumerate(chips):
                copy(a, 4 + j, (*chip, 1 - c), me).wait_recv()
        for cp in first + passed:
            cp.wait_send()
        for mine in started:
            mine.wait()

    return pl.pallas_call(
        body, out_shape=[jax.ShapeDtypeStruct((N_DEV,) + x.shape, x.dtype) for x in xs], in_specs=[_HBM] * n, out_specs=[_HBM] * n,
        scratch_shapes=[pltpu.SemaphoreType.DMA((n, 7)), pltpu.SemaphoreType.DMA((n, 7)), pltpu.SemaphoreType.DMA((n,))],
        name=name)(*xs)


def _exchange(name, gs):
    n = len(gs)

    def body(*refs):
        g_refs, out_refs = refs[:n], refs[n:2 * n]
        send_sems, recv_sems, local_sems = refs[2 * n:]
        x, y, c = lax.axis_index("x"), lax.axis_index("y"), lax.axis_index("c")
        me = 4 * x + 2 * y + c
        copies = []
        for a in range(n):
            mine = pltpu.make_async_copy(g_refs[a].at[me], out_refs[a].at[me], local_sems.at[a])
            mine.start()
            copies.append(mine)
        for r in range(1, N_DEV):
            px, py, pc = x ^ ((r >> 2) & 1), y ^ ((r >> 1) & 1), c ^ (r & 1)
            for a in range(n):
                cp = pltpu.make_async_remote_copy(src_ref=g_refs[a].at[4 * px + 2 * py + pc], dst_ref=out_refs[a].at[me],
                                                  send_sem=send_sems.at[a, r - 1], recv_sem=recv_sems.at[a, r - 1],
                                                  device_id=(px, py, pc), device_id_type=MESH_ID)
                cp.start()
                copies.append(cp)
        for cp in copies:
            cp.wait()

    return pl.pallas_call(
        body, out_shape=[jax.ShapeDtypeStruct(g.shape, g.dtype) for g in gs], in_specs=[_HBM] * n, out_specs=[_HBM] * n,
        scratch_shapes=[pltpu.SemaphoreType.DMA((n, N_DEV - 1)), pltpu.SemaphoreType.DMA((n, N_DEV - 1)), pltpu.SemaphoreType.DMA((n,))],
        name=name)(*gs)


_SEM = pl.BlockSpec(memory_space=pltpu.SEMAPHORE)
_IN_HBM = pl.BlockSpec(memory_space=pltpu.HBM)
_SIDE_EFFECT = pltpu.SideEffectType.DATAFLOW_SIDE_EFFECTING


def _push_copies(src_refs, land_refs, send_sems, recv_sems, per_peer):
    x, y, c = lax.axis_index("x"), lax.axis_index("y"), lax.axis_index("c")
    me = 4 * x + 2 * y + c
    copies = []
    for r in range(1, N_DEV):
        px, py, pc = x ^ ((r >> 2) & 1), y ^ ((r >> 1) & 1), c ^ (r & 1)
        for a, (s, d) in enumerate(zip(src_refs, land_refs)):
            k = a * (N_DEV - 1) + r - 1
            copies.append(pltpu.make_async_remote_copy(src_ref=s.at[4 * px + 2 * py + pc] if per_peer else s, dst_ref=d.at[me],
                                                       send_sem=send_sems.at[k], recv_sem=recv_sems.at[k],
                                                       device_id=(px, py, pc), device_id_type=MESH_ID))
    return copies


def _push_start(name, srcs, per_peer, after):
    n = len(srcs)
    lands = [lax.empty((N_DEV,) + (s.shape[1:] if per_peer else s.shape), s.dtype) for s in srcs]

    def body(*refs):
        for cp in _push_copies(refs[:n], refs[n:2 * n], refs[2 * n + 1], refs[2 * n + 2], per_peer):
            cp.start()
        refs[-1][...] = jnp.zeros_like(refs[-1])

    hbm = [pltpu.HBM(a.shape, a.dtype) for a in (*srcs, *lands)]
    sems = pltpu.SemaphoreType.DMA((n * (N_DEV - 1),))
    out = pl.pallas_call(
        body, name=name, out_shape=(sems, sems, *hbm, jax.ShapeDtypeStruct((8, LANES), F32)),
        in_specs=[_IN_HBM] * (2 * n) + [pl.BlockSpec(memory_space=pl.ANY)],
        out_specs=(_SEM, _SEM, *[_IN_HBM] * (2 * n), pl.BlockSpec(memory_space=pltpu.VMEM)),
        input_output_aliases={a: 2 + a for a in range(2 * n)},
        compiler_params=pltpu.CompilerParams(has_side_effects=_SIDE_EFFECT),
    )(*[pltpu.with_memory_space_constraint(a, pltpu.HBM) for a in (*srcs, *lands)], after)
    return out[0], out[1], out[2:2 + n], out[2 + n:2 + 2 * n], out[-1]


def _push_wait(name, started, per_peer, after):
    send_sems, recv_sems, srcs, lands, _ = started
    n = len(srcs)

    def body(*refs):
        for cp in _push_copies(refs[:n], refs[n:2 * n], refs[2 * n], refs[2 * n + 1], per_peer):
            cp.wait_send()
            cp.wait_recv()

    out = pl.pallas_call(
        body, name=name, out_shape=[pltpu.HBM(a.shape, a.dtype) for a in (*srcs, *lands)],
        in_specs=[_IN_HBM] * (2 * n) + [_SEM, _SEM, pl.BlockSpec(memory_space=pl.ANY)], out_specs=[_IN_HBM] * (2 * n),
        input_output_aliases={a: a for a in range(2 * n)},
        compiler_params=pltpu.CompilerParams(has_side_effects=_SIDE_EFFECT),
    )(*srcs, *lands, send_sems, recv_sems, after)
    me = _my_slot()
    own = [lax.dynamic_index_in_dim(s, me, 0, keepdims=False) if per_peer else s for s in out[:n]]
    return _with_own(out[n:], own, me)


def _my_slot():
    return 4 * lax.axis_index("x") + 2 * lax.axis_index("y") + lax.axis_index("c")


def _row_tile(rows, cols, budget):
    if rows * cols * 4 <= budget or rows % 16:
        return rows
    best = 16
    for t in range(16, rows + 1, 16):
        if rows % t == 0 and t * cols * 4 <= budget:
            best = t
    return best


def _sum_slots(name, recv):
    _, R, C = recv.shape
    tr = _row_tile(R, C, 1 << 20)

    def body(r_ref, o_ref):
        g = r_ref[0].astype(F32)
        for k in range(1, N_DEV):
            g = g + r_ref[k].astype(F32)
        o_ref[...] = g

    return pl.pallas_call(body, grid=(R // tr,), in_specs=[pl.BlockSpec((N_DEV, tr, C), lambda i: (0, i, 0))],
                          out_specs=pl.BlockSpec((tr, C), lambda i: (i, 0)), out_shape=jax.ShapeDtypeStruct((R, C), F32),
                          compiler_params=_params(("arbitrary",), 32), name=name)(recv)


def _adamw(name, g, w, m, v):
    shape = w.shape
    R, C = shape[-2], shape[-1]
    view = (-1, R, C)
    L = w.size // (R * C)
    tr = _row_tile(R, C, 1 << 20)
    c1 = 1.0 - ADAM_B1 ** ADAM_STEP
    c2 = 1.0 - ADAM_B2 ** ADAM_STEP

    def body(g_ref, w_ref, m_ref, v_ref, d_ref, mo_ref, vo_ref):
        gv = g_ref[...]
        mn = ADAM_B1 * m_ref[...] + (1.0 - ADAM_B1) * gv
        vn = ADAM_B2 * v_ref[...] + (1.0 - ADAM_B2) * (gv * gv)
        mo_ref[...] = mn
        vo_ref[...] = vn
        d_ref[...] = -ADAM_LR * ((mn / c1) / (jnp.sqrt(vn / c2) + ADAM_EPS) + ADAM_WD * w_ref[...])

    blk = pl.BlockSpec((None, tr, C), lambda l, i: (l, i, 0))
    shp = jax.ShapeDtypeStruct((L, R, C), F32)
    outs = pl.pallas_call(body, grid=(L, R // tr), in_specs=[blk, blk, blk, blk], out_specs=[blk, blk, blk], out_shape=[shp, shp, shp],
                          compiler_params=_params(("arbitrary", "arbitrary"), 32), name=name)(*[a.reshape(view) for a in (g, w, m, v)])
    return [o.reshape(shape) for o in outs]


def _pad_flat(a, n):
    a = a.reshape(-1)
    return jnp.pad(a, (0, n - a.shape[0]))


def _seg(n):
    return -(-n // FLAT_ALIGN) * FLAT_ALIGN


def _to_blocks(full, axis):
    shp = full.shape
    return jnp.moveaxis(full.reshape(shp[:axis] + (N_DEV, shp[axis] // N_DEV) + shp[axis + 1:]), axis, 0)


def _from_blocks(blocks, axis):
    b = jnp.moveaxis(blocks, 0, axis)
    shp = b.shape
    return b.reshape(shp[:axis] + (shp[axis] * shp[axis + 1],) + shp[axis + 2:])


def _as_rows(shard, n):
    return shard.T if SHARD_AXIS[n] == 2 else shard


def _with_own(lands, own, me):
    return [lax.dynamic_update_slice(land, o[None], (me, 0, 0)) for land, o in zip(lands, own)]


def kernel(x, mem, positions, norm_mix_pre, norm_mix_post, w_in, pool_w, pool_scale, conv_b_w, w_branch_a, w_branch_b, w_branch_c, w_out, norm_mem_pre, norm_mem_post, norm_memkv, w_mq, w_mkv, w_mo, norm_ffn_pre, norm_ffn_post, w_up, conv_ffn_w, w_down, loss_target, m_norm_mix_pre, m_norm_mix_post, m_w_in, m_pool_w, m_pool_scale, m_conv_b_w, m_w_branch_a, m_w_branch_b, m_w_branch_c, m_w_out, m_norm_mem_pre, m_norm_mem_post, m_norm_memkv, m_w_mq, m_w_mkv, m_w_mo, m_norm_ffn_pre, m_norm_ffn_post, m_w_up, m_conv_ffn_w, m_w_down, v_norm_mix_pre, v_norm_mix_post, v_w_in, v_pool_w, v_pool_scale, v_conv_b_w, v_w_branch_a, v_w_branch_b, v_w_branch_c, v_w_out, v_norm_mem_pre, v_norm_mem_post, v_norm_memkv, v_w_mq, v_w_mkv, v_w_mo, v_norm_ffn_pre, v_norm_ffn_post, v_w_up, v_conv_ffn_w, v_w_down):
    w = dict(norm_mix_pre=norm_mix_pre, norm_mix_post=norm_mix_post, w_in=w_in, pool_w=pool_w, pool_scale=pool_scale, conv_b_w=conv_b_w, w_branch_a=w_branch_a, w_branch_b=w_branch_b, w_branch_c=w_branch_c, w_out=w_out, norm_mem_pre=norm_mem_pre, norm_mem_post=norm_mem_post, norm_memkv=norm_memkv, w_mq=w_mq, w_mkv=w_mkv, w_mo=w_mo, norm_ffn_pre=norm_ffn_pre, norm_ffn_post=norm_ffn_post, w_up=w_up, conv_ffn_w=conv_ffn_w, w_down=w_down)
    m = dict(norm_mix_pre=m_norm_mix_pre, norm_mix_post=m_norm_mix_post, w_in=m_w_in, pool_w=m_pool_w, pool_scale=m_pool_scale, conv_b_w=m_conv_b_w, w_branch_a=m_w_branch_a, w_branch_b=m_w_branch_b, w_branch_c=m_w_branch_c, w_out=m_w_out, norm_mem_pre=m_norm_mem_pre, norm_mem_post=m_norm_mem_post, norm_memkv=m_norm_memkv, w_mq=m_w_mq, w_mkv=m_w_mkv, w_mo=m_w_mo, norm_ffn_pre=m_norm_ffn_pre, norm_ffn_post=m_norm_ffn_post, w_up=m_w_up, conv_ffn_w=m_conv_ffn_w, w_down=m_w_down)
    v = dict(norm_mix_pre=v_norm_mix_pre, norm_mix_post=v_norm_mix_post, w_in=v_w_in, pool_w=v_pool_w, pool_scale=v_pool_scale, conv_b_w=v_conv_b_w, w_branch_a=v_w_branch_a, w_branch_b=v_w_branch_b, w_branch_c=v_w_branch_c, w_out=v_w_out, norm_mem_pre=v_norm_mem_pre, norm_mem_post=v_norm_mem_post, norm_memkv=v_norm_memkv, w_mq=v_w_mq, w_mkv=v_w_mkv, w_mo=v_w_mo, norm_ffn_pre=v_norm_ffn_pre, norm_ffn_post=v_norm_ffn_post, w_up=v_w_up, conv_ffn_w=v_conv_ffn_w, w_down=v_w_down)

    me = _my_slot()
    mix_big = [n for n in BIG if n not in LATE_BIG]
    block = lambda names, l: [_as_rows(w[n][l], n).astype(MXU) for n in names]
    conv = jnp.concatenate([_pad_flat(w[n], _seg(w[n].size)) for n in F32_GATHERED]).reshape(-1, LANES)
    groups = dict(m=MERGE_BIG, b=LATE_BIG, a=mix_big)
    got0 = _all_gather("weights_all_gather_0", block(['w_in'], 0) + [conv])
    conv_all = got0[-1].reshape(N_DEV, -1)
    small, off = {n: w[n] for n in WEIGHTS if n not in SHARD_AXIS}, 0
    for n in F32_GATHERED:
        small[n] = _from_blocks(conv_all[:, off:off + w[n].size].reshape((N_DEV,) + w[n].shape), 2)
        off += _seg(w[n].size)
    whole = lambda names, got: {n: o.reshape(-1, o.shape[-1]) for n, o in zip(names, got)}
    pushes, after = {}, got0[0]
    for tag, l in (('m', 0), ('b', 0), ('a', 1), ('b', 1)):
        pushes[tag, l] = _push_start(f"weights_push_start_{l}{tag}", block(groups[tag], l), False, after)
        after = pushes[tag, l][4]

    def arrived(tag, l, done):
        return _big_operands(whole(groups[tag], _push_wait(f"weights_push_wait_{l}{tag}", pushes[tag, l], False, done)))

    ctab, stab = _rope_tables(positions[0])
    W0 = _layer_weights(whole(['w_in'], got0), small, 0)
    sv0 = _layer_fwd_branches(x[0], dict(W0, g_mix_pre=W0['g_mix_pre'] + after[0, 0]), ctab, stab)
    W0.update(arrived('m', 0, sv0['att']))
    sv0 = _layer_fwd_merge(W0, sv0)
    W0.update(arrived('b', 0, sv0['x1']))
    x1 = _layer_fwd_late(mem[0], W0, sv0)
    W1 = _layer_weights({}, small, 1)
    W1.update(arrived('a', 1, x1))
    sv1 = _layer_fwd_mix(x1, W1, ctab, stab)
    W1.update(arrived('b', 1, sv1['x1']))
    x2 = _layer_fwd_late(mem[0], W1, sv1)
    dx, acc = _loss_head(x2, loss_target[0])
    loss = lax.psum(jnp.sum(acc) * (0.5 / D), MESH_AXES)
    grads = [None] * DEPTH
    dx, grads[1] = _layer_bwd(dx, mem[0], W1, sv1, ctab, stab)
    sent = [None, [grads[1][n].reshape(N_DEV, -1, grads[1][n].shape[-1]) for n in BIG]]
    push_g = _push_start("grads_push_start_1", sent[1], True, dx)
    dx, g_late = _layer_bwd_late(dx, mem[0], dict(W0, g_ffn_post=W0['g_ffn_post'] + push_g[4][0, 0]), sv0)
    sent_late = [g_late[n].reshape(N_DEV, -1, g_late[n].shape[-1]) for n in LATE_BIG]
    push_l = _push_start("grads_push_start_0", sent_late, True, dx)
    parts, g_mix = _layer_bwd_merge(dx, dict(W0, g_mix_post=W0['g_mix_post'] + push_l[4][0, 0]), sv0)
    sent_merge = [g_mix[n].reshape(N_DEV, -1, g_mix[n].shape[-1]) for n in MERGE_BIG]
    push_m = _push_start("grads_push_start_0m", sent_merge, True, parts[0])
    du, g_br = _layer_bwd_branches(parts, dict(W0, pool_scale=W0['pool_scale'] + push_m[4][0, 0]), sv0, ctab, stab)
    g_mix.update(g_br)
    g_mix['w_in'] = _dw_in(du, sv0)
    sent_in = [g_mix['w_in'].reshape(N_DEV, -1, D)]
    push_i = _push_start("grads_push_start_in", sent_in, True, du)
    dx, g_mix['norm_mix_pre'] = _mix_pre_bwd(dx, du, dict(W0, g_mix_pre=W0['g_mix_pre'] + push_i[4][0, 0]), sv0)
    grads[0] = {**g_late, **g_mix}
    recv1 = _push_wait("grads_push_wait_1", push_g, True, dx)
    recv_late = _push_wait("grads_push_wait_0", push_l, True, dx)
    recv_merge = _push_wait("grads_push_wait_0m", push_m, True, dx)

    misc_names = [n for n in WEIGHTS if n not in BIG]
    stacked = {n: jnp.stack([grads[l][n].reshape(small[n].shape[1:]) for l in range(DEPTH)]) for n in misc_names}
    rows = [(_to_blocks(stacked[n], 2) if n in SHARD_AXIS else jnp.broadcast_to(stacked[n][None], (N_DEV,) + stacked[n].shape))
            for n in misc_names]
    segs = [_seg(w[n].size) for n in misc_names]
    misc = jnp.concatenate([jnp.pad(r.reshape(N_DEV, -1), ((0, 0), (0, s - r[0].size))) for r, s in zip(rows, segs)],
                           axis=1).reshape(N_DEV, -1, LANES)
    recv_misc = _exchange("grad_exchange_0", [misc])
    g_out, per_layer = {}, {}
    for l, names, recv in ((1, BIG, recv1), (0, LATE_BIG, recv_late), (0, MERGE_BIG, recv_merge)):
        for n, r in zip(names, recv):
            per_layer[n, l] = _sum_slots(f"sum_{n}_{l}", r)
    misc_sum = _sum_slots("sum_misc", recv_misc[0]).reshape(-1)
    off = 0
    for n, s in zip(misc_names, segs):
        g_out[n] = misc_sum[off:off + w[n].size].reshape(w[n].shape)
        off += s

    swap = lambda a: jnp.swapaxes(a, 1, 2)

    def update(n):
        if n not in BIG:
            return [g_out[n], *_adamw(f"adamw_{n}", g_out[n], w[n], m[n], v[n])]
        g = jnp.stack([per_layer[n, l] for l in range(DEPTH)])
        if SHARD_AXIS[n] == 2 and w[n].shape[2] % LANES:
            return [swap(a) for a in (g, *_adamw(f"adamw_{n}", g, swap(w[n]), swap(m[n]), swap(v[n])))]
        g = swap(g) if SHARD_AXIS[n] == 2 else g
        return [g, *_adamw(f"adamw_{n}", g, w[n], m[n], v[n])]

    done = {n: update(n) for n in WEIGHTS if n != 'w_in'}
    recv_in = _push_wait("grads_push_wait_in", push_i, True, done[WEIGHTS[-1]][1])
    per_layer['w_in', 0] = _sum_slots("sum_w_in_0", recv_in[0])
    done['w_in'] = update('w_in')
    return (loss, dx[None], *[done[n][k] for k in range(4) for n in WEIGHTS])
```

```python
import jax
import jax.numpy as jnp
from jax import lax
from jax.experimental import pallas as pl
from jax.experimental.pallas import tpu as pltpu

F32 = jnp.float32
MXU = jnp.bfloat16
HI = lax.Precision.HIGHEST

D = 1024
DEPTH = 2
POOLW = 384
ATT_W = 768
ATT_O = 256
GATE_W = 3 * D
IN_W = 6912
IN_TILE = 768
IN_ROT = (IN_W - GATE_W) // IN_TILE
MEM_W = 512
D_FF = 2816
EPS = 1e-6
ROPE_THETA = 500000.0
QB = 128
DILS = (1, 4, 16)
NEG = -1e30
MEM_SCALE = 128 ** -0.5
ATT_SCALE = 0.125

ADAM_LR, ADAM_B1, ADAM_B2, ADAM_EPS, ADAM_WD, ADAM_STEP = 0.001, 0.9, 0.999, 1e-08, 0.01, 10

N_DEV = 8
MESH_AXES = ("x", "y", "c")
LANES = 128
FLAT_ALIGN = 2048
ROW_TILE = 1024

WEIGHTS = ['norm_mix_pre', 'norm_mix_post', 'w_in', 'pool_w', 'pool_scale', 'conv_b_w', 'w_branch_a', 'w_branch_b',
           'w_branch_c', 'w_out', 'norm_mem_pre', 'norm_mem_post', 'norm_memkv', 'w_mq', 'w_mkv', 'w_mo',
           'norm_ffn_pre', 'norm_ffn_post', 'w_up', 'conv_ffn_w', 'w_down']
SHARD_AXIS = {'w_in': 2, 'conv_b_w': 2, 'w_branch_a': 2, 'w_branch_b': 2, 'w_branch_c': 2, 'w_out': 1, 'w_mq': 1,
              'w_mkv': 1, 'w_mo': 2, 'w_up': 2, 'conv_ffn_w': 2, 'w_down': 1}
F32_GATHERED = ('conv_b_w', 'conv_ffn_w')
BIG = [n for n in WEIGHTS if n in SHARD_AXIS and n not in F32_GATHERED]
LATE_BIG = ['w_mq', 'w_mkv', 'w_mo', 'w_up', 'w_down']
MERGE_BIG = ['w_branch_a', 'w_branch_b', 'w_branch_c', 'w_out']


VMEM_LIMIT_MB = 60


def _params(sem, vmem_mb):
    del vmem_mb
    return pltpu.CompilerParams(dimension_semantics=sem, vmem_limit_bytes=VMEM_LIMIT_MB << 20)


def _dot(a, b, prec=None):
    return lax.dot_general(a, b, (((1,), (0,)), ((), ())), preferred_element_type=F32, precision=prec)


def _dot_nt(a, b, prec=None):
    return lax.dot_general(a, b, (((1,), (1,)), ((), ())), preferred_element_type=F32, precision=prec)


def _dot_tn(a, b, prec=None):
    return lax.dot_general(a, b, (((0,), (0,)), ((), ())), preferred_element_type=F32, precision=prec)


def _tile(n, cap):
    if n <= cap:
        return n
    best = None
    for t in range(LANES, cap + 1, LANES):
        if n % t == 0:
            best = t
    assert best is not None, (n, cap)
    return best


def _rms(x, g):
    r = lax.rsqrt(jnp.mean(x * x, axis=-1, keepdims=True) + EPS)
    return x * r * g, r


def _rms_bwd(w, y):
    r = lax.rsqrt(jnp.mean(y * y, axis=-1, keepdims=True) + EPS)
    return r * w - y * (r * r * r) * jnp.mean(w * y, axis=-1, keepdims=True), r


def _rows_call(name, body, n_rows, ts, ins, outs, scratch=(), reverse=False, vmem_mb=48, aliases=None):
    nt = n_rows // ts
    assert nt * ts == n_rows

    def tile_of(g):
        return (nt - 1 - g) if reverse else g

    in_specs, args = [], []
    for op in ins:
        if op[0] == "t":
            _, a, cw, cb = op
            in_specs.append(pl.BlockSpec((ts, cw), lambda g, cb=cb: (tile_of(g), cb)))
        elif op[0] == "h":
            _, a, hr, cw, cb = op
            in_specs.append(pl.BlockSpec((hr, cw), lambda g, cb=cb, k=ts // hr: (jnp.maximum(tile_of(g) * k - 1, 0), cb)))
        elif op[0] == "x":
            _, a = op
            in_specs.append(pl.BlockSpec(memory_space=pl.ANY))
        else:
            _, a = op
            in_specs.append(pl.BlockSpec(a.shape, lambda g, n=a.ndim: (0,) * n))
        args.append(a)
    out_specs, out_shape = [], []
    for op in outs:
        if op[0] == "t":
            _, cols, dt = op
            out_specs.append(pl.BlockSpec((ts, cols), lambda g: (tile_of(g), 0)))
            out_shape.append(jax.ShapeDtypeStruct((n_rows, cols), dt))
        elif op[0] == "c":
            _, total, cols, cb, dt = op
            out_specs.append(pl.BlockSpec((ts, cols), lambda g, cb=cb: (tile_of(g), cb)))
            out_shape.append(jax.ShapeDtypeStruct((n_rows, total), dt))
        else:
            _, shp, dt = op
            out_specs.append(pl.BlockSpec(shp, lambda g, n=len(shp): (0,) * n))
            out_shape.append(jax.ShapeDtypeStruct(shp, dt))

    def kern(*refs):
        g = pl.program_id(0)
        body(tile_of(g), g, *refs)

    return pl.pallas_call(kern, grid=(nt,), in_specs=in_specs, out_specs=out_specs, out_shape=out_shape,
                          scratch_shapes=list(scratch), input_output_aliases=aliases or {},
                          compiler_params=_params(("arbitrary",), vmem_mb), name=name)(*args)


def _acc(ref, g, val):
    @pl.when(g == 0)
    def _():
        ref[...] = val

    @pl.when(g != 0)
    def _():
        ref[...] += val


def _norm_mm(name, x, g, w, ts, tn, out_dtype=F32, wt=False, rot=0):
    S, K = x.shape
    N = w.shape[0] if wt else w.shape[1]
    assert wt or not rot

    def body(x_ref, g_ref, w_ref, o_ref, h_ref, hs):
        @pl.when(pl.program_id(1) == 0)
        def _():
            h, _ = _rms(x_ref[...], g_ref[...])
            hs[...] = h.astype(MXU)
            h_ref[...] = h.astype(MXU)

        o_ref[...] = (_dot_nt if wt else _dot)(hs[...], w_ref[...]).astype(out_dtype)

    w_spec = pl.BlockSpec((tn, K), lambda i, j: ((j + rot) % (N // tn), 0)) if wt else pl.BlockSpec((K, tn), lambda i, j: (0, j))
    return pl.pallas_call(
        body, grid=(S // ts, N // tn),
        in_specs=[pl.BlockSpec((ts, K), lambda i, j: (i, 0)), pl.BlockSpec((1, K), lambda i, j: (0, 0)), w_spec],
        out_specs=[pl.BlockSpec((ts, tn), lambda i, j: (i, j)), pl.BlockSpec((ts, K), lambda i, j: (i, 0))],
        out_shape=[jax.ShapeDtypeStruct((S, N), out_dtype), jax.ShapeDtypeStruct((S, K), MXU)],
        scratch_shapes=[pltpu.VMEM((ts, K), MXU)],
        compiler_params=_params(("arbitrary", "arbitrary"), 48), name=name)(x, g, w)


def _mm_nt(name, a, b, ts, tn, out_dtype=F32):
    M, K = a.shape
    N = b.shape[0]

    def body(a_ref, b_ref, o_ref):
        o_ref[...] = _dot_nt(a_ref[...], b_ref[...]).astype(out_dtype)

    return pl.pallas_call(
        body, grid=(M // ts, N // tn),
        in_specs=[pl.BlockSpec((ts, K), lambda i, j: (i, 0)), pl.BlockSpec((tn, K), lambda i, j: (j, 0))],
        out_specs=pl.BlockSpec((ts, tn), lambda i, j: (i, j)), out_shape=jax.ShapeDtypeStruct((M, N), out_dtype),
        compiler_params=_params(("arbitrary", "arbitrary"), 48), name=name)(a, b)


def _mm_tn(name, a, b, cap_k=512, cap_n=1024, out_dtype=MXU, rot=0):
    S, K = a.shape
    N = b.shape[1]
    tk, tn = _tile(K, cap_k), _tile(N, cap_n)

    def body(a_ref, b_ref, o_ref):
        o_ref[...] = _dot_tn(a_ref[...], b_ref[...]).astype(out_dtype)

    return pl.pallas_call(
        body, grid=(K // tk, N // tn),
        in_specs=[pl.BlockSpec((S, tk), lambda i, j: (0, i)), pl.BlockSpec((S, tn), lambda i, j: (0, j))],
        out_specs=pl.BlockSpec((tk, tn), lambda i, j: ((i + rot) % (K // tk), j)), out_shape=jax.ShapeDtypeStruct((K, N), out_dtype),
        compiler_params=_params(("arbitrary", "arbitrary"), 48), name=name)(a, b)


def _pool_cols(shape):
    col = lax.broadcasted_iota(jnp.int32, shape, 1)
    return col < 96, col < 192, col < 288


def _pool_select(s2, s4, s8, s16):
    c1, c2, c3 = _pool_cols(s2.shape)
    return jnp.where(c1, s2, jnp.where(c2, s4, jnp.where(c3, s8, s16)))


def _pool_cnt(t0, ts):
    c1, c2, c3 = _pool_cols((ts, POOLW))
    win = jnp.where(c1, 2, jnp.where(c2, 4, jnp.where(c3, 8, 16)))
    t = t0 + lax.broadcasted_iota(jnp.int32, (ts, POOLW), 0)
    return jnp.minimum(t + 1, win).astype(F32)


def _pooled(a, prev, t0):
    ts = a.shape[0]
    ext = jnp.concatenate([prev, a], axis=0)
    s2 = ext + pltpu.roll(ext, 1, axis=0)
    s4 = s2 + pltpu.roll(s2, 2, axis=0)
    s8 = s4 + pltpu.roll(s4, 4, axis=0)
    s16 = s8 + pltpu.roll(s8, 8, axis=0)
    sums = _pool_select(s2, s4, s8, s16)[16:]
    return sums / _pool_cnt(t0, ts) - a


def _conv3(z, prev8, w):
    ext = jnp.concatenate([prev8, z], axis=0)
    z1 = pltpu.roll(ext, 1, axis=0)[8:]
    z2 = pltpu.roll(ext, 2, axis=0)[8:]
    return w[0:1] * z2 + w[1:2] * z1 + w[2:3] * z, z1, z2


def _conv3_t(dc, next8, w, shifted=False):
    ts = dc.shape[0]
    ext = jnp.concatenate([dc, next8], axis=0)
    n = ts + 8
    u1 = pltpu.roll(ext, n - 1, axis=0)[:ts]
    u2 = pltpu.roll(ext, n - 2, axis=0)[:ts]
    out = w[2:3] * dc + w[1:2] * u1 + w[0:1] * u2
    return (out, u1, u2) if shifted else out


def _poolconv_fwd(u, wblk, pool_scale, conv_b, ts=256):
    S = u.shape[0]

    def body(i, g, a_ref, bx_ref, bb_ref, bc_ref, wblk_ref, ps_ref, cw_ref, a2_ref, yb_ref, ca, cz):
        @pl.when(g == 0)
        def _():
            ca[...] = jnp.zeros_like(ca)
            cz[...] = jnp.zeros_like(cz)

        a = a_ref[...].astype(F32)
        p = _pooled(a, ca[...], i * ts)
        mixed = _dot(p.astype(MXU), wblk_ref[...])
        a2_ref[...] = (mixed * ps_ref[...]).astype(MXU)
        z = bc_ref[...].astype(F32) * bx_ref[...].astype(F32)
        conv, _, _ = _conv3(z, cz[...], cw_ref[...])
        yb_ref[...] = (bb_ref[...].astype(F32) * conv).astype(MXU)
        ca[...] = a[ts - 16:]
        cz[...] = z[ts - 8:]

    ins = [("t", u, POOLW, 8), ("t", u, POOLW, 9), ("t", u, POOLW, 10), ("t", u, POOLW, 11), ("w", wblk), ("w", pool_scale),
           ("w", conv_b)]
    return _rows_call("poolconv_fwd", body, S, ts, ins, [("t", POOLW, MXU), ("t", POOLW, MXU)],
                      scratch=[pltpu.VMEM((16, POOLW), F32), pltpu.VMEM((8, POOLW), F32)])


def _poolconv_bwd(u, d_a2, d_yb, du, wblk, pool_scale, conv_b, ts=256):
    S = u.shape[0]

    def body(i, g, a_ref, bx_ref, bb_ref, bc_ref, ap_ref, bxp_ref, bcp_ref, da2_ref, dyb_ref, wblk_ref, ps_ref, cw_ref, _,
             o_ref, dps_ref, dwb_ref, dcw_ref, ce, cdz):
        @pl.when(g == 0)
        def _():
            ce[...] = jnp.zeros_like(ce)
            cdz[...] = jnp.zeros_like(cdz)

        first = (i > 0).astype(F32)
        a = a_ref[...].astype(F32)
        p = _pooled(a, ap_ref[...].astype(F32) * first, i * ts)
        pb = p.astype(MXU)
        mixed = _dot(pb, wblk_ref[...])
        da2 = da2_ref[...]
        dmixed = (da2 * ps_ref[...]).astype(MXU)
        dp = _dot_nt(dmixed, wblk_ref[...])
        _acc(dps_ref, g, jnp.sum(da2 * mixed, axis=0, keepdims=True))
        _acc(dwb_ref, g, _dot_tn(pb, dmixed))
        e = dp / _pool_cnt(i * ts, ts)
        ext = jnp.concatenate([e, ce[...]], axis=0)
        n = ts + 16
        f2 = ext + pltpu.roll(ext, n - 1, axis=0)
        f4 = f2 + pltpu.roll(f2, n - 2, axis=0)
        f8 = f4 + pltpu.roll(f4, n - 4, axis=0)
        f16 = f8 + pltpu.roll(f8, n - 8, axis=0)
        o_ref[:, 0:POOLW] = (_pool_select(f2, f4, f8, f16)[:ts] - dp).astype(o_ref.dtype)
        ce[...] = e[:16]

        bx, bb, bc = bx_ref[...].astype(F32), bb_ref[...].astype(F32), bc_ref[...].astype(F32)
        z = bc * bx
        w = cw_ref[...]
        conv, z1, z2 = _conv3(z, (bxp_ref[...].astype(F32) * bcp_ref[...].astype(F32))[8:16] * first, w)
        dyb = dyb_ref[...]
        dconv = dyb * bb
        dz = _conv3_t(dconv, cdz[...], w)
        o_ref[:, POOLW:2 * POOLW] = (dz * bc).astype(o_ref.dtype)
        o_ref[:, 2 * POOLW:3 * POOLW] = (dyb * conv).astype(o_ref.dtype)
        o_ref[:, 3 * POOLW:4 * POOLW] = (dz * bx).astype(o_ref.dtype)
        dw = jnp.concatenate([jnp.sum(dconv * z2, axis=0, keepdims=True), jnp.sum(dconv * z1, axis=0, keepdims=True),
                              jnp.sum(dconv * z, axis=0, keepdims=True)], axis=0)
        _acc(dcw_ref, g, dw)
        cdz[...] = dconv[:8]

    ins = [("t", u, POOLW, 8), ("t", u, POOLW, 9), ("t", u, POOLW, 10), ("t", u, POOLW, 11),
           ("h", u, 16, POOLW, 8), ("h", u, 16, POOLW, 9), ("h", u, 16, POOLW, 11),
           ("t", d_a2, POOLW, 0), ("t", d_yb, POOLW, 0), ("w", wblk), ("w", pool_scale), ("w", conv_b), ("x", du)]
    outs = [("c", IN_W, 4 * POOLW, GATE_W // (4 * POOLW), MXU), ("a", (1, POOLW), F32), ("a", (POOLW, POOLW), F32), ("a", (3, POOLW), F32)]
    return _rows_call("poolconv_bwd", body, S, ts, ins, outs, aliases={len(ins) - 1: 0},
                      scratch=[pltpu.VMEM((16, POOLW), F32), pltpu.VMEM((8, POOLW), F32)], reverse=True)


def _rope_tables(positions):
    S = positions.shape[0]
    inv = ROPE_THETA ** (-jnp.arange(0, 16, 2, dtype=F32) / 16)
    ang = positions.astype(F32)[:, None] * inv
    cos, sin = jnp.cos(ang), jnp.sin(ang)
    c64 = jnp.concatenate([cos, cos, jnp.ones((S, 48), F32)], axis=1)
    s64 = jnp.concatenate([-sin, sin, jnp.zeros((S, 48), F32)], axis=1)
    return jnp.concatenate([c64, c64], axis=1), jnp.concatenate([s64, s64], axis=1)


def _partner(x):
    lane = lax.broadcasted_iota(jnp.int32, x.shape, 1) % 64
    return jnp.where(lane < 8, pltpu.roll(x, LANES - 8, axis=1), jnp.where(lane < 16, pltpu.roll(x, 8, axis=1), 0.0))


def _rope(x, c, s):
    return x * c + _partner(x) * s


def _rope_t(x, c, s):
    return x * c + _partner(x * s)


def _rows_of(r, n, d):
    return pl.ds(r, n, stride=d) if d > 1 else pl.ds(0, n)


def _head_masks(shape):
    lane = lax.broadcasted_iota(jnp.int32, shape, 1) // 64
    return [lane == h for h in range(4)]


def _only(mask, x):
    return jnp.where(mask, x, jnp.zeros_like(x))


def _rope_perm(u, ctab, stab, ts=256):
    S = u.shape[0]
    nch = ATT_W // LANES

    def body(*refs):
        chunks, (c_ref, s_ref), outs, scr = refs[:3 * nch], refs[3 * nch:3 * nch + 2], refs[3 * nch + 2:-1], refs[-1]
        for k in range(3 * nch):
            scr[k] = chunks[k][...].astype(F32)
        for g, d in enumerate(DILS):
            n = ts // d
            for r in range(d):
                rows = _rows_of(r, n, d)
                c, s = c_ref[rows, :], s_ref[rows, :]
                for which in range(3):
                    parts = [scr.at[which * nch + j][rows, :] for j in (2 * g, 2 * g + 1)]
                    if which < 2:
                        parts = [_rope(x, c, s) for x in parts]
                    outs[which * 3 + g][r] = jnp.concatenate(parts, axis=1).astype(MXU)

    base = (IN_W - 3 * ATT_W) // LANES
    in_specs = [pl.BlockSpec((ts, LANES), lambda i, cb=base + k: (i, cb)) for k in range(3 * nch)]
    in_specs += [pl.BlockSpec((ts, LANES), lambda i: (i, 0))] * 2
    out_specs = [pl.BlockSpec((d, ts // d, ATT_O), lambda i: (0, i, 0)) for _ in range(3) for d in DILS]
    out_shape = [jax.ShapeDtypeStruct((d, S // d, ATT_O), MXU) for _ in range(3) for d in DILS]
    res = pl.pallas_call(body, grid=(S // ts,), in_specs=in_specs, out_specs=out_specs, out_shape=out_shape,
                         scratch_shapes=[pltpu.VMEM((3 * nch, ts, LANES), F32)],
                         compiler_params=_params(("arbitrary",), 32), name="rope_perm")(*([u] * (3 * nch)), ctab, stab)
    return [[res[which * 3 + g].reshape(S, ATT_O) for g in range(3)] for which in range(3)]


def _rope_unperm_bwd(dqkv, du, ctab, stab, ts=256):
    S = dqkv[0][0].shape[0]
    nch = ATT_W // LANES

    def body(*refs):
        ins, (c_ref, s_ref, _, o_ref, scr) = refs[:9], refs[9:]
        for g, d in enumerate(DILS):
            n = ts // d
            for r in range(d):
                rows = _rows_of(r, n, d)
                c, s = c_ref[rows, :], s_ref[rows, :]
                for which in range(3):
                    v = ins[which * 3 + g][r]
                    for half in range(2):
                        x = v[:, half * LANES:(half + 1) * LANES]
                        scr.at[which * nch + 2 * g + half][rows, :] = _rope_t(x, c, s) if which < 2 else x
        for j in range(3 * nch):
            o_ref[:, j * LANES:(j + 1) * LANES] = scr[j].astype(o_ref.dtype)

    in_specs = [pl.BlockSpec((d, ts // d, ATT_O), lambda i: (0, i, 0)) for _ in range(3) for d in DILS]
    in_specs += [pl.BlockSpec((ts, LANES), lambda i: (i, 0))] * 2 + [pl.BlockSpec(memory_space=pl.ANY)]
    args = [dqkv[which][g].reshape(d, S // d, ATT_O) for which in range(3) for g, d in enumerate(DILS)]
    last = (IN_W - 3 * ATT_W) // (3 * ATT_W)
    return pl.pallas_call(body, grid=(S // ts,), in_specs=in_specs, out_specs=pl.BlockSpec((ts, 3 * ATT_W), lambda i: (i, last)),
                          out_shape=jax.ShapeDtypeStruct((S, IN_W), MXU), scratch_shapes=[pltpu.VMEM((3 * nch, ts, LANES), F32)],
                          input_output_aliases={len(in_specs) - 1: 0},
                          compiler_params=_params(("arbitrary",), 32), name="rope_unperm_bwd")(*args, ctab, stab, du)


def _band_mask_keys(has_prev):
    r = lax.broadcasted_iota(jnp.int32, (QB, 2 * QB), 0)
    c = lax.broadcasted_iota(jnp.int32, (QB, 2 * QB), 1)
    return ((c < QB) & (c >= r) & has_prev) | ((c >= QB) & (c - QB <= r))


def _band_mask_queries(has_next):
    r = lax.broadcasted_iota(jnp.int32, (2 * QB, QB), 0)
    c = lax.broadcasted_iota(jnp.int32, (2 * QB, QB), 1)
    return ((r < QB) & (c <= r)) | ((r >= QB) & (c >= r - QB) & has_next)


ASUB = 4
_BIG = pl.BlockSpec((ASUB * QB, ATT_O), lambda b: (b, 0))
_PREV = pl.BlockSpec((QB, ATT_O), lambda b: (jnp.maximum(b * ASUB - 1, 0), 0))


def _sub(ref, j):
    return ref[j * QB:(j + 1) * QB]


def _attn_fwd(g, q, k, v):
    S = q.shape[0]
    nb = S // QB
    nblk = nb // DILS[g]

    def body(q_ref, kc_ref, kp_ref, vc_ref, vp_ref, o_ref, m_ref, l_ref):
        hm_kv, hm_o = _head_masks((2 * QB, ATT_O)), _head_masks((QB, ATT_O))
        for j in range(ASUB):
            ok = _band_mask_keys(((pl.program_id(0) * ASUB + j) & (nblk - 1)) > 0)
            k2 = jnp.concatenate([kp_ref[...] if j == 0 else _sub(kc_ref, j - 1), _sub(kc_ref, j)], axis=0)
            v2 = jnp.concatenate([vp_ref[...] if j == 0 else _sub(vc_ref, j - 1), _sub(vc_ref, j)], axis=0)
            qv = _sub(q_ref, j)
            o_acc = jnp.zeros((QB, ATT_O), F32)
            m_acc = jnp.zeros((QB, ATT_O), F32)
            l_acc = jnp.zeros((QB, ATT_O), F32)
            for h in range(4):
                s = jnp.where(ok, _dot_nt(qv, _only(hm_kv[h], k2)) * ATT_SCALE, NEG)
                m = jnp.max(s, axis=1, keepdims=True)
                p = jnp.exp(s - m)
                o_acc = o_acc + _dot(p.astype(MXU), _only(hm_kv[h], v2))
                m_acc = jnp.where(hm_o[h], m, m_acc)
                l_acc = jnp.where(hm_o[h], jnp.sum(p, axis=1, keepdims=True), l_acc)
            o_ref[j * QB:(j + 1) * QB] = o_acc
            m_ref[j * QB:(j + 1) * QB] = m_acc
            l_ref[j * QB:(j + 1) * QB] = l_acc

    shp = jax.ShapeDtypeStruct((S, ATT_O), F32)
    return pl.pallas_call(body, grid=(nb // ASUB,), in_specs=[_BIG, _BIG, _PREV, _BIG, _PREV],
                          out_specs=[_BIG] * 3, out_shape=[shp, shp, shp], compiler_params=_params(("arbitrary",), 32),
                          name=f"attn_fwd_{g}")(q, k, k, v, v)


def _natural(ref, d, scr, ts):
    if d == 1:
        return ref[0]
    n = ts // d
    for r in range(d):
        v = ref[r]
        scr.at[0][pl.ds(r, n, stride=d), :] = v[:, 0:LANES]
        scr.at[1][pl.ds(r, n, stride=d), :] = v[:, LANES:2 * LANES]
    return jnp.concatenate([scr[0], scr[1]], axis=1)


def _attn_combine(oml, ts=256):
    S = oml[0][0].shape[0]

    def body(*refs):
        ins, (att_ref, out_ref, lse_ref, scr) = refs[:9], refs[9:]
        o, m, l = [[_natural(ins[3 * g + k], d, scr, ts) for g, d in enumerate(DILS)] for k in range(3)]
        mx = jnp.maximum(jnp.maximum(m[0], m[1]), m[2])
        w = [jnp.exp(m[g] - mx) for g in range(3)]
        den = w[0] * l[0] + w[1] * l[1] + w[2] * l[2]
        out = (w[0] * o[0] + w[1] * o[1] + w[2] * o[2]) / den
        out_ref[...] = out
        att_ref[...] = out.astype(MXU)
        lse_ref[...] = mx + jnp.log(den)

    in_specs = [pl.BlockSpec((d, ts // d, ATT_O), lambda i: (0, i, 0)) for d in DILS for _ in range(3)]
    args = [a.reshape(d, S // d, ATT_O) for d, grp in zip(DILS, oml) for a in grp]
    blk = pl.BlockSpec((ts, ATT_O), lambda i: (i, 0))
    return pl.pallas_call(body, grid=(S // ts,), in_specs=in_specs, out_specs=[blk, blk, blk],
                          out_shape=[jax.ShapeDtypeStruct((S, ATT_O), MXU), jax.ShapeDtypeStruct((S, ATT_O), F32),
                                     jax.ShapeDtypeStruct((S, ATT_O), F32)],
                          scratch_shapes=[pltpu.VMEM((2, ts, LANES), F32)], compiler_params=_params(("arbitrary",), 32),
                          name="attn_combine")(*args)


def _attn_bwd_prep(datt, o, lse, ts=256):
    S = datt.shape[0]

    def body(da0, da1, o_ref, l0, l1, *rest):
        outs, dl = rest[:9], rest[9]
        prod = jnp.concatenate([da0[...], da1[...]], axis=1) * o_ref[...]
        delta = jnp.zeros((ts, ATT_O), F32)
        for hm in _head_masks((ts, ATT_O)):
            delta = jnp.where(hm, jnp.sum(_only(hm, prod), axis=1, keepdims=True), delta)
        dl[0] = delta[:, 0:LANES]
        dl[1] = delta[:, LANES:2 * LANES]
        for g, d in enumerate(DILS):
            n = ts // d
            for r in range(d):
                rows = _rows_of(r, n, d)
                outs[g][r] = jnp.concatenate([da0[rows, :], da1[rows, :]], axis=1).astype(MXU)
                outs[3 + g][r] = jnp.concatenate([dl.at[0][rows, :], dl.at[1][rows, :]], axis=1)
                outs[6 + g][r] = jnp.concatenate([l0[rows, :], l1[rows, :]], axis=1)

    half = lambda j: pl.BlockSpec((ts, LANES), lambda i: (i, j))
    out_specs = [pl.BlockSpec((d, ts // d, ATT_O), lambda i: (0, i, 0)) for _ in range(3) for d in DILS]
    out_shape = [jax.ShapeDtypeStruct((d, S // d, ATT_O), dt) for dt in (MXU, F32, F32) for d in DILS]
    res = pl.pallas_call(body, grid=(S // ts,), in_specs=[half(0), half(1), pl.BlockSpec((ts, ATT_O), lambda i: (i, 0)), half(0), half(1)],
                         out_specs=out_specs, out_shape=out_shape, scratch_shapes=[pltpu.VMEM((2, ts, LANES), F32)],
                         compiler_params=_params(("arbitrary",), 32), name="attn_bwd_prep")(datt, datt, o, lse, lse)
    return [[res[k * 3 + g].reshape(S, ATT_O) for g in range(3)] for k in range(3)]


def _head_col(x, h):
    return x[:, h * 64:h * 64 + 1]


def _attn_bwd(g, q, k, v, do, delta, lse):
    S = q.shape[0]
    nb = S // QB
    nblk = nb // DILS[g]

    def body(k_ref, v_ref, qc_ref, qn_ref, doc_ref, don_ref, dlc_ref, dln_ref, lc_ref, ln_ref, dq_ref, dk_ref, dv_ref, dq_scr):
        hms, hmk = _head_masks((2 * QB, ATT_O)), _head_masks((QB, ATT_O))
        first = pl.program_id(0) == 0

        @pl.when(first)
        def _():
            dq_scr[0:QB] = jnp.zeros((QB, ATT_O), F32)

        @pl.when(jnp.logical_not(first))
        def _():
            dq_scr[0:QB] = dq_scr[ASUB * QB:(ASUB + 1) * QB]

        dq_scr[QB:(ASUB + 1) * QB] = jnp.zeros((ASUB * QB, ATT_O), F32)

        def both(cur_ref, nxt_ref, j):
            return jnp.concatenate([_sub(cur_ref, j), nxt_ref[...] if j == ASUB - 1 else _sub(cur_ref, j + 1)], axis=0)

        for j in range(ASUB):
            ok = _band_mask_queries(((pl.program_id(0) * ASUB + j + 1) & (nblk - 1)) > 0)
            q2, do2, dl2, lse2 = both(qc_ref, qn_ref, j), both(doc_ref, don_ref, j), both(dlc_ref, dln_ref, j), both(lc_ref, ln_ref, j)
            kv, vv = _sub(k_ref, j), _sub(v_ref, j)
            dk = jnp.zeros((QB, ATT_O), F32)
            dv = jnp.zeros((QB, ATT_O), F32)
            dq2 = jnp.zeros((2 * QB, ATT_O), F32)
            for h, hm in enumerate(hms):
                qh, doh = _only(hm, q2), _only(hm, do2)
                p = jnp.where(ok, jnp.exp(_dot_nt(qh, kv) * ATT_SCALE - _head_col(lse2, h)), 0.0)
                ds = (p * (_dot_nt(doh, vv) - _head_col(dl2, h))).astype(MXU)
                dv = dv + _dot_tn(p.astype(MXU), doh)
                dk = dk + _dot_tn(ds, qh)
                dq2 = dq2 + _dot(ds, _only(hmk[h], kv))
            dk_ref[j * QB:(j + 1) * QB] = dk * ATT_SCALE
            dv_ref[j * QB:(j + 1) * QB] = dv
            dq_scr[j * QB:(j + 2) * QB] += dq2
        dq_ref[...] = dq_scr[0:ASUB * QB] * ATT_SCALE

    nxt = pl.BlockSpec((QB, ATT_O), lambda b: (jnp.minimum((b + 1) * ASUB, nb - 1), 0))
    shp = jax.ShapeDtypeStruct((S, ATT_O), F32)
    return pl.pallas_call(body, grid=(nb // ASUB,), in_specs=[_BIG, _BIG, _BIG, nxt, _BIG, nxt, _BIG, nxt, _BIG, nxt], out_specs=[_BIG] * 3,
                          out_shape=[shp, shp, shp], scratch_shapes=[pltpu.VMEM(((ASUB + 1) * QB, ATT_O), F32)],
                          compiler_params=_params(("arbitrary",), 32), name=f"attn_bwd_{g}")(k, v, q, q, do, do, delta, delta, lse, lse)


def _merge_fwd(x0, u, a2, yb, att, wa, wb, wc, w_out, g_post, ts=256):
    S = x0.shape[0]

    def body(i, g, x_ref, gate_ref, a2_ref, yb_ref, att_ref, wa_ref, wb_ref, wc_ref, wo_ref, gp_ref, mg_ref, y_ref, xo_ref):
        gate = lambda n: jax.nn.sigmoid(gate_ref[:, n * D:(n + 1) * D].astype(F32))
        merged = gate(0) * _dot_nt(a2_ref[...], wa_ref[...])
        merged = merged + gate(1) * _dot_nt(yb_ref[...], wb_ref[...])
        merged = merged + gate(2) * _dot_nt(att_ref[...], wc_ref[...])
        mb = merged.astype(MXU)
        mg_ref[...] = mb
        y = _dot(mb, wo_ref[...])
        y_ref[...] = y
        xo_ref[...] = x_ref[...] + _rms(y, gp_ref[...])[0]

    ins = [("t", x0, D, 0), ("t", u, GATE_W, 0), ("t", a2, POOLW, 0), ("t", yb, POOLW, 0), ("t", att, ATT_O, 0),
           ("w", wa), ("w", wb), ("w", wc), ("w", w_out), ("w", g_post)]
    return _rows_call("merge_fwd", body, S, ts, ins, [("t", D, MXU), ("t", D, F32), ("t", D, F32)])


def _merge_bwd(dx, y1, u, a2, yb, att, wa, wb, wc, w_out, g_post, ts=256):
    S = dx.shape[0]

    def body(i, g, dx_ref, y_ref, gate_ref, a2_ref, yb_ref, att_ref, wa_ref, wb_ref, wc_ref, wo_ref, gp_ref,
             dy_ref, dgate_ref, dbra_ref, dbrb_ref, dbrc_ref, da2_ref, dyb_ref, datt_ref, dgp_ref):
        dxv, y = dx_ref[...], y_ref[...]
        dy, r = _rms_bwd(dxv * gp_ref[...], y)
        _acc(dgp_ref, g, jnp.sum(dxv * (y * r), axis=0, keepdims=True))
        dyb16 = dy.astype(MXU)
        dy_ref[...] = dyb16
        dm = _dot_nt(dyb16, wo_ref[...])
        for n, (src, w_ref, dbr_ref, din_ref) in enumerate(((a2_ref, wa_ref, dbra_ref, da2_ref), (yb_ref, wb_ref, dbrb_ref, dyb_ref),
                                                           (att_ref, wc_ref, dbrc_ref, datt_ref))):
            gt = jax.nn.sigmoid(gate_ref[:, n * D:(n + 1) * D].astype(F32))
            br = _dot_nt(src[...], w_ref[...])
            dgate_ref[:, n * D:(n + 1) * D] = (dm * br * gt * (1.0 - gt)).astype(dgate_ref.dtype)
            dbr = (dm * gt).astype(MXU)
            dbr_ref[...] = dbr
            din_ref[...] = _dot(dbr, w_ref[...])

    ins = [("t", dx, D, 0), ("t", y1, D, 0), ("t", u, GATE_W, 0), ("t", a2, POOLW, 0), ("t", yb, POOLW, 0), ("t", att, ATT_O, 0),
           ("w", wa), ("w", wb), ("w", wc), ("w", w_out), ("w", g_post)]
    outs = [("t", D, MXU), ("c", IN_W, GATE_W, 0, MXU), ("t", D, MXU), ("t", D, MXU), ("t", D, MXU), ("t", POOLW, F32), ("t", POOLW, F32),
            ("t", ATT_O, F32), ("a", (1, D), F32)]
    return _rows_call("merge_bwd", body, S, ts, ins, outs)


def _prenorm_bwd(name, dx_res, du, wt, x, g_pre, ts=256, lead=0):
    S = x.shape[0]
    N = du.shape[1]

    def body(i, g, dx_ref, du_ref, wt_ref, x_ref, g_ref, o_ref, dg_ref):
        if lead:
            dhv = _dot(du_ref[:, 0:lead], wt_ref[N - lead:N, :]) + _dot(du_ref[:, lead:N], wt_ref[0:N - lead, :])
        else:
            dhv = _dot(du_ref[...], wt_ref[...])
        xv = x_ref[...]
        dxn, r = _rms_bwd(dhv * g_ref[...], xv)
        o_ref[...] = dx_ref[...] + dxn
        _acc(dg_ref, g, jnp.sum(dhv * (xv * r), axis=0, keepdims=True))

    ins = [("t", dx_res, D, 0), ("t", du, N, 0), ("w", wt), ("t", x, D, 0), ("w", g_pre)]
    return _rows_call(name, body, S, ts, ins, [("t", D, F32), ("a", (1, D), F32)], vmem_mb=52)


def _mem_heads(qm, kv_ref):
    out = []
    for h in range(4):
        q = qm[:, h * 128:(h + 1) * 128].astype(MXU)
        k = kv_ref[:, h * 128:(h + 1) * 128]
        v = kv_ref[:, MEM_W + h * 128:MEM_W + (h + 1) * 128]
        sc = _dot_nt(q, k) * MEM_SCALE
        e = jnp.exp(sc - jnp.max(sc, axis=1, keepdims=True))
        out.append((e / jnp.sum(e, axis=1, keepdims=True), q, k, v))
    return out


def _mem_fwd(x1, kv, g_pre, w_mq, w_mo, g_post, ts=256):
    S = x1.shape[0]

    def body(i, g, x_ref, kv_ref, gq_ref, wq_ref, wo_ref, gp_ref, om_ref, h_ref, y_ref, xo_ref):
        x = x_ref[...]
        hb = _rms(x, gq_ref[...])[0].astype(MXU)
        h_ref[...] = hb
        qm = _dot(hb, wq_ref[...])
        om = jnp.concatenate([_dot(p.astype(MXU), v) for p, _, _, v in _mem_heads(qm, kv_ref)], axis=1).astype(MXU)
        om_ref[...] = om
        y = _dot_nt(om, wo_ref[...])
        y_ref[...] = y
        xo_ref[...] = x + _rms(y, gp_ref[...])[0]

    ins = [("t", x1, D, 0), ("w", kv), ("w", g_pre), ("w", w_mq), ("w", w_mo), ("w", g_post)]
    return _rows_call("mem_fwd", body, S, ts, ins, [("t", MEM_W, MXU), ("t", D, MXU), ("t", D, F32), ("t", D, F32)])


def _mem_bwd(dx2, ym, x1, kv, g_pre, w_mq, w_mo, g_post, ts=256):
    S = x1.shape[0]

    def body(i, g, dx_ref, y_ref, x_ref, kv_ref, gq_ref, wq_ref, wo_ref, gp_ref, dy_ref, dq_ref, dxo_ref, dgp_ref, dgq_ref, dkv_ref):
        dxv, y, x = dx_ref[...], y_ref[...], x_ref[...]
        dy, r = _rms_bwd(dxv * gp_ref[...], y)
        _acc(dgp_ref, g, jnp.sum(dxv * (y * r), axis=0, keepdims=True))
        dyb = dy.astype(MXU)
        dy_ref[...] = dyb
        dom = _dot(dyb, wo_ref[...])
        h, r1 = _rms(x, gq_ref[...])
        qm = _dot(h.astype(MXU), wq_ref[...])
        dqs = []

        @pl.when(g == 0)
        def _():
            dkv_ref[...] = jnp.zeros_like(dkv_ref)

        for hh, (p, q, k, v) in enumerate(_mem_heads(qm, kv_ref)):
            doh = dom[:, hh * 128:(hh + 1) * 128].astype(MXU)
            dp = _dot_nt(doh, v)
            dsc = (p * (dp - jnp.sum(dp * p, axis=1, keepdims=True)) * MEM_SCALE).astype(MXU)
            dqs.append(_dot(dsc, k))
            dkv_ref[:, hh * 128:(hh + 1) * 128] += _dot_tn(dsc, q)
            dkv_ref[:, MEM_W + hh * 128:MEM_W + (hh + 1) * 128] += _dot_tn(p.astype(MXU), doh)
        dq = jnp.concatenate(dqs, axis=1).astype(MXU)
        dq_ref[...] = dq
        dh = _dot_nt(dq, wq_ref[...])
        _acc(dgq_ref, g, jnp.sum(dh * (x * r1), axis=0, keepdims=True))
        dxo_ref[...] = dxv + _rms_bwd(dh * gq_ref[...], x)[0]

    ins = [("t", dx2, D, 0), ("t", ym, D, 0), ("t", x1, D, 0), ("w", kv), ("w", g_pre), ("w", w_mq), ("w", w_mo), ("w", g_post)]
    outs = [("t", D, MXU), ("t", MEM_W, MXU), ("t", D, F32), ("a", (1, D), F32), ("a", (1, D), F32), ("a", (256, D), F32)]
    return _rows_call("mem_bwd", body, S, ts, ins, outs)


def _gain_grad(name, dn, x):
    n = x.shape[0]

    def body(i, g, dn_ref, x_ref, o_ref):
        xv = x_ref[...]
        r = lax.rsqrt(jnp.mean(xv * xv, axis=-1, keepdims=True) + EPS)
        o_ref[...] = jnp.sum(dn_ref[...] * (xv * r), axis=0, keepdims=True)

    return _rows_call(name, body, n, n, [("t", dn, D, 0), ("t", x, D, 0)], [("a", (1, D), F32)])[0]


def _ffn_fwd(x2, u3, conv_f, w_down, g_post, ts=256):
    S = x2.shape[0]

    def body(i, g, x_ref, ua_ref, ub_ref, cw_ref, wd_ref, gp_ref, act_ref, y_ref, xo_ref, c_ref, cu):
        @pl.when(g == 0)
        def _():
            cu[...] = jnp.zeros_like(cu)

        ua = ua_ref[...].astype(F32)
        c, _, _ = _conv3(ua, cu[...], cw_ref[...])
        c_ref[...] = c.astype(MXU)
        act = (c * jax.nn.sigmoid(c) * ub_ref[...].astype(F32)).astype(MXU)
        act_ref[...] = act
        y = _dot(act, wd_ref[...])
        y_ref[...] = y
        xo_ref[...] = x_ref[...] + _rms(y, gp_ref[...])[0]
        cu[...] = ua[ts - 8:]

    ins = [("t", x2, D, 0), ("t", u3, D_FF, 0), ("t", u3, D_FF, 1), ("w", conv_f), ("w", w_down), ("w", g_post)]
    return _rows_call("ffn_fwd", body, S, ts, ins, [("t", D_FF, MXU), ("t", D, F32), ("t", D, F32), ("t", D_FF, MXU)],
                      scratch=[pltpu.VMEM((8, D_FF), F32)], vmem_mb=56)


def _ffn_bwd(dx3, y3, u3, c, conv_f, w_down, g_post, ts=128):
    S = dx3.shape[0]

    def body(i, g, dx_ref, y_ref, ua_ref, ub_ref, c_ref, cw_ref, wd_ref, gp_ref, dy_ref, du_ref, dgp_ref, dcw_ref, cdc):
        @pl.when(g == 0)
        def _():
            cdc[...] = jnp.zeros_like(cdc)

        dxv, y = dx_ref[...], y_ref[...]
        dy, r = _rms_bwd(dxv * gp_ref[...], y)
        _acc(dgp_ref, g, jnp.sum(dxv * (y * r), axis=0, keepdims=True))
        dyb = dy.astype(MXU)
        dy_ref[...] = dyb
        dact = _dot_nt(dyb, wd_ref[...])
        ua, c, w = ua_ref[...].astype(F32), c_ref[...].astype(F32), cw_ref[...]
        sg = jax.nn.sigmoid(c)
        du_ref[:, D_FF:2 * D_FF] = (dact * (c * sg)).astype(du_ref.dtype)
        dc = dact * ub_ref[...].astype(F32) * (sg * (1.0 + c * (1.0 - sg)))
        dua, dc1, dc2 = _conv3_t(dc, cdc[...], w, shifted=True)
        du_ref[:, 0:D_FF] = dua.astype(du_ref.dtype)
        dw = jnp.concatenate([jnp.sum(ua * dc2, axis=0, keepdims=True), jnp.sum(ua * dc1, axis=0, keepdims=True),
                              jnp.sum(ua * dc, axis=0, keepdims=True)], axis=0)
        _acc(dcw_ref, g, dw)
        cdc[...] = dc[:8]

    ins = [("t", dx3, D, 0), ("t", y3, D, 0), ("t", u3, D_FF, 0), ("t", u3, D_FF, 1), ("t", c, D_FF, 0), ("w", conv_f),
           ("w", w_down), ("w", g_post)]
    outs = [("t", D, MXU), ("t", 2 * D_FF, MXU), ("a", (1, D), F32), ("a", (3, D_FF), F32)]
    return _rows_call("ffn_bwd", body, S, ts, ins, outs, scratch=[pltpu.VMEM((8, D_FF), F32)], reverse=True, vmem_mb=56)


def _loss_head(x, target, ts=512):
    S = x.shape[0]

    def body(i, g, x_ref, t_ref, dx_ref, acc_ref):
        diff = x_ref[...] - t_ref[...]
        dx_ref[...] = diff * (1.0 / D)
        col = jnp.sum(diff * diff, axis=0, keepdims=True)
        part = col[:, 0:LANES]
        for j in range(1, D // LANES):
            part = part + col[:, j * LANES:(j + 1) * LANES]
        row = lax.broadcasted_iota(jnp.int32, (8, LANES), 0)
        _acc(acc_ref, g, jnp.where(row == 0, jnp.broadcast_to(part, (8, LANES)), 0.0))

    return _rows_call("loss_head", body, S, ts, [("t", x, D, 0), ("t", target, D, 0)], [("t", D, F32), ("a", (8, LANES), F32)])


_OPERAND_NAME = dict(w_in='w_in', w_branch_a='wa', w_branch_b='wb', w_branch_c='wc', w_out='w_out', w_mq='w_mq', w_mkv='w_mkv',
                     w_mo='w_mo', w_up='w_up', w_down='w_down')


def _big_operands(big):
    return {_OPERAND_NAME[n]: a for n, a in big.items()}


def _layer_weights(big, small, l):
    pool_w = small['pool_w'][l].astype(MXU)
    wblk = jnp.zeros((POOLW, POOLW), MXU)
    for g in range(4):
        wblk = lax.dynamic_update_slice(wblk, pool_w[g], (g * 96, g * 96))
    vec = lambda n: small[n][l].reshape(1, -1)
    return dict(
        _big_operands(big),
        wblk=wblk, pool_scale=vec('pool_scale'), conv_b=small['conv_b_w'][l], conv_f=small['conv_ffn_w'][l],
        g_mix_pre=vec('norm_mix_pre'), g_mix_post=vec('norm_mix_post'), g_mem_pre=vec('norm_mem_pre'),
        g_mem_post=vec('norm_mem_post'), g_memkv=vec('norm_memkv'), g_ffn_pre=vec('norm_ffn_pre'), g_ffn_post=vec('norm_ffn_post'))


def _layer_fwd(x0, mem, W, ctab, stab):
    sv = _layer_fwd_mix(x0, W, ctab, stab)
    return _layer_fwd_late(mem, W, sv), sv


def _layer_fwd_mix(x0, W, ctab, stab):
    return _layer_fwd_merge(W, _layer_fwd_branches(x0, W, ctab, stab))


def _layer_fwd_branches(x0, W, ctab, stab):
    sv = dict(x0=x0)
    sv['u'], sv['h1'] = _norm_mm("in_proj", x0, W['g_mix_pre'], W['w_in'], ts=2048, tn=IN_TILE, wt=True, rot=IN_ROT, out_dtype=MXU)
    sv['a2'], sv['yb'] = _poolconv_fwd(sv['u'], W['wblk'], W['pool_scale'], W['conv_b'])
    sv['qkv'] = q3, k3, v3 = _rope_perm(sv['u'], ctab, stab)
    sv['att'], sv['o'], sv['lse'] = _attn_combine([_attn_fwd(g, q3[g], k3[g], v3[g]) for g in range(3)])
    return sv


def _layer_fwd_merge(W, sv):
    sv['merged'], sv['y1'], sv['x1'] = _merge_fwd(sv['x0'], sv['u'], sv['a2'], sv['yb'], sv['att'], W['wa'], W['wb'], W['wc'],
                                                  W['w_out'], W['g_mix_post'])
    return sv


def _layer_fwd_late(mem, W, sv):
    sv['kv'], sv['memn'] = _norm_mm("mem_kv", mem, W['g_memkv'], W['w_mkv'], ts=256, tn=D, out_dtype=MXU)
    sv['om'], sv['h2'], sv['ym'], sv['x2'] = _mem_fwd(sv['x1'], sv['kv'], W['g_mem_pre'], W['w_mq'], W['w_mo'], W['g_mem_post'])
    sv['u3'], sv['h3'] = _norm_mm("up_proj", sv['x2'], W['g_ffn_pre'], W['w_up'], ts=2048, tn=1408, wt=True, out_dtype=MXU)
    sv['act'], sv['y3'], x3, sv['c3'] = _ffn_fwd(sv['x2'], sv['u3'], W['conv_f'], W['w_down'], W['g_ffn_post'])
    return x3


def _layer_bwd(dx3, mem, W, sv, ctab, stab):
    dx1, g = _layer_bwd_late(dx3, mem, W, sv)
    dx0, g_mix = _layer_bwd_mix(dx1, W, sv, ctab, stab)
    return dx0, {**g, **g_mix}


def _layer_bwd_late(dx3, mem, W, sv):
    g = {}
    dy3, du3, g['norm_ffn_post'], g['conv_ffn_w'] = _ffn_bwd(dx3, sv['y3'], sv['u3'], sv['c3'], W['conv_f'], W['w_down'], W['g_ffn_post'])
    g['w_down'] = _mm_tn("dw_down", sv['act'], dy3, cap_k=256)
    g['w_up'] = _mm_tn("dw_up", du3, sv['h3'])
    dx2, g['norm_ffn_pre'] = _prenorm_bwd("ffn_pre_bwd", dx3, du3, W['w_up'], sv['x2'], W['g_ffn_pre'])
    dym, dqm, dx1, g['norm_mem_post'], g['norm_mem_pre'], dkv = _mem_bwd(dx2, sv['ym'], sv['x1'], sv['kv'], W['g_mem_pre'],
                                                                       W['w_mq'], W['w_mo'], W['g_mem_post'])
    g['w_mo'] = _mm_tn("dw_mo", dym, sv['om'])
    g['w_mq'] = _mm_tn("dw_mq", sv['h2'], dqm)
    dkvb = dkv.astype(MXU)
    g['w_mkv'] = _mm_tn("dw_mkv", sv['memn'], dkvb)
    g['norm_memkv'] = _gain_grad("memkv_gain", _mm_nt("d_memn", dkvb, W['w_mkv'], ts=256, tn=512), mem)
    return dx1, g


def _layer_bwd_mix(dx1, W, sv, ctab, stab):
    du, g = _layer_bwd_mixers(dx1, W, sv, ctab, stab)
    g['w_in'] = _dw_in(du, sv)
    dx0, g['norm_mix_pre'] = _mix_pre_bwd(dx1, du, W, sv)
    return dx0, g


def _dw_in(du, sv):
    return _mm_tn("dw_in", du, sv['h1'], cap_k=IN_TILE, rot=IN_ROT)


def _mix_pre_bwd(dx1, du, W, sv):
    return _prenorm_bwd("mix_pre_bwd", dx1, du, W['w_in'], sv['x0'], W['g_mix_pre'], lead=GATE_W)


def _layer_bwd_mixers(dx1, W, sv, ctab, stab):
    parts, g = _layer_bwd_merge(dx1, W, sv)
    du, g_br = _layer_bwd_branches(parts, W, sv, ctab, stab)
    return du, {**g, **g_br}


def _layer_bwd_merge(dx1, W, sv):
    g = {}
    dy1, du, dbra, dbrb, dbrc, da2, dyb, datt, g['norm_mix_post'] = _merge_bwd(
        dx1, sv['y1'], sv['u'], sv['a2'], sv['yb'], sv['att'], W['wa'], W['wb'], W['wc'], W['w_out'], W['g_mix_post'])
    g['w_out'] = _mm_tn("dw_out", sv['merged'], dy1)
    g['w_branch_a'] = _mm_tn("dw_a", dbra, sv['a2'])
    g['w_branch_b'] = _mm_tn("dw_b", dbrb, sv['yb'])
    g['w_branch_c'] = _mm_tn("dw_c", dbrc, sv['att'])
    return (du, da2, dyb, datt), g


def _layer_bwd_branches(parts, W, sv, ctab, stab):
    du, da2, dyb, datt = parts
    g = {}
    du, g['pool_scale'], dwblk, g['conv_b_w'] = _poolconv_bwd(sv['u'], da2, dyb, du, W['wblk'], W['pool_scale'], W['conv_b'])
    g['pool_w'] = jnp.stack([dwblk[k * 96:(k + 1) * 96, k * 96:(k + 1) * 96] for k in range(4)])
    q3, k3, v3 = sv['qkv']
    do3, dl3, lse3 = _attn_bwd_prep(datt, sv['o'], sv['lse'])
    dqkv3 = [_attn_bwd(i, q3[i], k3[i], v3[i], do3[i], dl3[i], lse3[i]) for i in range(3)]
    du = _rope_unperm_bwd([[t[which] for t in dqkv3] for which in range(3)], du, ctab, stab)
    return du, g


def _local_step(x, mem, positions, target, big, small):
    ctab, stab = _rope_tables(positions)
    Ws = [_layer_weights(big[l], small, l) for l in range(DEPTH)]
    saved = []
    for l in range(DEPTH):
        x, sv = _layer_fwd(x, mem, Ws[l], ctab, stab)
        saved.append(sv)
    dx, acc = _loss_head(x, target)
    loss = jnp.sum(acc) * (0.5 / D)
    grads = [None] * DEPTH
    for l in reversed(range(DEPTH)):
        dx, grads[l] = _layer_bwd(dx, mem, Ws[l], saved[l], ctab, stab)
    return loss, dx, grads


_HBM = pl.BlockSpec(memory_space=pl.ANY)
MESH_ID = pl.DeviceIdType.MESH


def _all_gather(name, xs):
    n = len(xs)

    def body(*refs):
        x_refs, out_refs = refs[:n], refs[n:2 * n]
        send_sems, recv_sems, local_sems = refs[2 * n:]
        x, y, c = lax.axis_index("x"), lax.axis_index("y"), lax.axis_index("c")
        me, sibling = (x, y, c), (x, y, 1 - c)
        chips = [(1 - x, y), (x, 1 - y), (1 - x, 1 - y)]

        def slot(a, p):
            return out_refs[a].at[4 * p[0] + 2 * p[1] + p[2]]

        def copy(a, k, block, to, src=None):
            return pltpu.make_async_remote_copy(src_ref=slot(a, block) if src is None else src, dst_ref=slot(a, block),
                                                send_sem=send_sems.at[a, k], recv_sem=recv_sems.at[a, k], device_id=to,
                                                device_id_type=MESH_ID)

        started = []
        for a in range(n):
            mine = pltpu.make_async_copy(x_refs[a], slot(a, me), local_sems.at[a])
            mine.start()
            started.append(mine)
        first = []
        for a in range(n):
            first.append(copy(a, 0, me, sibling, src=x_refs[a]))
            first += [copy(a, 1 + j, me, (*chip, c), src=x_refs[a]) for j, chip in enumerate(chips)]
        for cp in first:
            cp.start()
        passed = []
        for j, chip in enumerate(chips):
            for a in range(n):
                copy(a, 1 + j, (*chip, c), me).wait_recv()
                fw = copy(a, 4 + j, (*chip, c), sibling)
                fw.start()
                passed.append(fw)
        for a in range(n):
            copy(a, 0, sibling, me).wait_recv()
            for j, chip in enumerate(chips):
                copy(a, 4 + j, (*chip, 1 - c), me).wait_recv()
        for cp in first + passed:
            cp.wait_send()
        for mine in started:
            mine.wait()

    return pl.pallas_call(
        body, out_shape=[jax.ShapeDtypeStruct((N_DEV,) + x.shape, x.dtype) for x in xs], in_specs=[_HBM] * n, out_specs=[_HBM] * n,
        scratch_shapes=[pltpu.SemaphoreType.DMA((n, 7)), pltpu.SemaphoreType.DMA((n, 7)), pltpu.SemaphoreType.DMA((n,))],
        name=name)(*xs)


_SEM =pl.BlockSpec(memory_space=pltpu.SEMAPHORE)
_IN_HBM = pl.BlockSpec(memory_space=pltpu.HBM)
_SIDE_EFFECT = pltpu.SideEffectType.DATAFLOW_SIDE_EFFECTING


def _push_copies(src_refs, land_refs, send_sems, recv_sems, per_peer):
    x, y, c = lax.axis_index("x"), lax.axis_index("y"), lax.axis_index("c")
    me = 4 * x + 2 * y + c
    copies = []
    for r in range(1, N_DEV):
        px, py, pc = x ^ ((r >> 2) & 1), y ^ ((r >> 1) & 1), c ^ (r & 1)
        for a, (s, d) in enumerate(zip(src_refs, land_refs)):
            k = a * (N_DEV - 1) + r - 1
            copies.append(pltpu.make_async_remote_copy(src_ref=s.at[4 * px + 2 * py + pc] if per_peer else s, dst_ref=d.at[me],
                                                       send_sem=send_sems.at[k], recv_sem=recv_sems.at[k],
                                                       device_id=(px, py, pc), device_id_type=MESH_ID))
    return copies


def _push_start(name, srcs, per_peer, after):
    n = len(srcs)
    lands = [lax.empty((N_DEV,) + (s.shape[1:] if per_peer else s.shape), s.dtype) for s in srcs]

    def body(*refs):
        for cp in _push_copies(refs[:n], refs[n:2 * n], refs[2 * n + 1], refs[2 * n + 2], per_peer):
            cp.start()
        refs[-1][...] = jnp.zeros_like(refs[-1])

    hbm = [pltpu.HBM(a.shape, a.dtype) for a in (*srcs, *lands)]
    sems = pltpu.SemaphoreType.DMA((n * (N_DEV - 1),))
    out = pl.pallas_call(
        body, name=name, out_shape=(sems, sems, *hbm, jax.ShapeDtypeStruct((8, LANES), F32)),
        in_specs=[_IN_HBM] * (2 * n) + [pl.BlockSpec(memory_space=pl.ANY)],
        out_specs=(_SEM, _SEM, *[_IN_HBM] * (2 * n), pl.BlockSpec(memory_space=pltpu.VMEM)),
        input_output_aliases={a: 2 + a for a in range(2 * n)},
        compiler_params=pltpu.CompilerParams(has_side_effects=_SIDE_EFFECT),
    )(*[pltpu.with_memory_space_constraint(a, pltpu.HBM) for a in (*srcs, *lands)], after)
    return out[0], out[1], out[2:2 + n], out[2 + n:2 + 2 * n], out[-1]


def _push_wait(name, started, per_peer, after):
    send_sems, recv_sems, srcs, lands, _ = started
    n = len(srcs)

    def body(*refs):
        for cp in _push_copies(refs[:n], refs[n:2 * n], refs[2 * n], refs[2 * n + 1], per_peer):
            cp.wait_send()
            cp.wait_recv()

    out = pl.pallas_call(
        body, name=name, out_shape=[pltpu.HBM(a.shape, a.dtype) for a in (*srcs, *lands)],
        in_specs=[_IN_HBM] * (2 * n) + [_SEM, _SEM, pl.BlockSpec(memory_space=pl.ANY)], out_specs=[_IN_HBM] * (2 * n),
        input_output_aliases={a: a for a in range(2 * n)},
        compiler_params=pltpu.CompilerParams(has_side_effects=_SIDE_EFFECT),
    )(*srcs, *lands, send_sems, recv_sems, after)
    me = _my_slot()
    own = [lax.dynamic_index_in_dim(s, me, 0, keepdims=False) if per_peer else s for s in out[:n]]
    return _with_own(out[n:], own, me)


def _my_slot():
    return 4 * lax.axis_index("x") + 2 * lax.axis_index("y") + lax.axis_index("c")


def _row_tile(rows, cols, budget):
    if rows * cols * 4 <= budget or rows % 16:
        return rows
    best = 16
    for t in range(16, rows + 1, 16):
        if rows % t == 0 and t * cols * 4 <= budget:
            best = t
    return best


def _sum_slots(name, recv):
    _, R, C = recv.shape
    tr = _row_tile(R, C, 1 << 20)

    def body(r_ref, o_ref):
        g = r_ref[0].astype(F32)
        for k in range(1, N_DEV):
            g = g + r_ref[k].astype(F32)
        o_ref[...] = g

    return pl.pallas_call(body, grid=(R // tr,), in_specs=[pl.BlockSpec((N_DEV, tr, C), lambda i: (0, i, 0))],
                          out_specs=pl.BlockSpec((tr, C), lambda i: (i, 0)), out_shape=jax.ShapeDtypeStruct((R, C), F32),
                          compiler_params=_params(("arbitrary",), 32), name=name)(recv)


def _adamw(name, g, w, m, v):
    shape = w.shape
    R, C = shape[-2], shape[-1]
    view = (-1, R, C)
    L = w.size // (R * C)
    tr = _row_tile(R, C, 1 << 20)
    c1 = 1.0 - ADAM_B1 ** ADAM_STEP
    c2 = 1.0 - ADAM_B2 ** ADAM_STEP

    def body(g_ref, w_ref, m_ref, v_ref, d_ref, mo_ref, vo_ref):
        gv = g_ref[...]
        mn = ADAM_B1 * m_ref[...] + (1.0 - ADAM_B1) * gv
        vn = ADAM_B2 * v_ref[...] + (1.0 - ADAM_B2) * (gv * gv)
        mo_ref[...] = mn
        vo_ref[...] = vn
        d_ref[...] = -ADAM_LR * ((mn / c1) / (jnp.sqrt(vn / c2) + ADAM_EPS) + ADAM_WD * w_ref[...])

    blk = pl.BlockSpec((None, tr, C), lambda l, i: (l, i, 0))
    shp = jax.ShapeDtypeStruct((L, R, C), F32)
    outs = pl.pallas_call(body, grid=(L, R // tr), in_specs=[blk, blk, blk, blk], out_specs=[blk, blk, blk], out_shape=[shp, shp, shp],
                          compiler_params=_params(("arbitrary", "arbitrary"), 32), name=name)(*[a.reshape(view) for a in (g, w, m, v)])
    return [o.reshape(shape) for o in outs]


def _pad_flat(a, n):
    a = a.reshape(-1)
    return jnp.pad(a, (0, n - a.shape[0]))


def _seg(n):
    return -(-n // FLAT_ALIGN) * FLAT_ALIGN


def _to_blocks(full, axis):
    shp = full.shape
    return jnp.moveaxis(full.reshape(shp[:axis] + (N_DEV, shp[axis] // N_DEV) + shp[axis + 1:]), axis, 0)


def _from_blocks(blocks, axis):
    b = jnp.moveaxis(blocks, 0, axis)
    shp = b.shape
    return b.reshape(shp[:axis] + (shp[axis] * shp[axis + 1],) + shp[axis + 2:])


def _as_rows(shard, n):
    return shard.T if SHARD_AXIS[n] == 2 else shard


def _with_own(lands, own, me):
    return [lax.dynamic_update_slice(land, o[None], (me, 0, 0)) for land, o in zip(lands, own)]


def kernel(x, mem, positions, norm_mix_pre, norm_mix_post, w_in, pool_w, pool_scale, conv_b_w, w_branch_a, w_branch_b, w_branch_c, w_out, norm_mem_pre, norm_mem_post, norm_memkv, w_mq, w_mkv, w_mo, norm_ffn_pre, norm_ffn_post, w_up, conv_ffn_w, w_down, loss_target, m_norm_mix_pre, m_norm_mix_post, m_w_in, m_pool_w, m_pool_scale, m_conv_b_w, m_w_branch_a, m_w_branch_b, m_w_branch_c, m_w_out, m_norm_mem_pre, m_norm_mem_post, m_norm_memkv, m_w_mq, m_w_mkv, m_w_mo, m_norm_ffn_pre, m_norm_ffn_post, m_w_up, m_conv_ffn_w, m_w_down, v_norm_mix_pre, v_norm_mix_post, v_w_in, v_pool_w, v_pool_scale, v_conv_b_w, v_w_branch_a, v_w_branch_b, v_w_branch_c, v_w_out, v_norm_mem_pre, v_norm_mem_post, v_norm_memkv, v_w_mq, v_w_mkv, v_w_mo, v_norm_ffn_pre, v_norm_ffn_post, v_w_up, v_conv_ffn_w, v_w_down):
    w = dict(norm_mix_pre=norm_mix_pre, norm_mix_post=norm_mix_post, w_in=w_in, pool_w=pool_w, pool_scale=pool_scale, conv_b_w=conv_b_w, w_branch_a=w_branch_a, w_branch_b=w_branch_b, w_branch_c=w_branch_c, w_out=w_out, norm_mem_pre=norm_mem_pre, norm_mem_post=norm_mem_post, norm_memkv=norm_memkv, w_mq=w_mq, w_mkv=w_mkv, w_mo=w_mo, norm_ffn_pre=norm_ffn_pre, norm_ffn_post=norm_ffn_post, w_up=w_up, conv_ffn_w=conv_ffn_w, w_down=w_down)
    m = dict(norm_mix_pre=m_norm_mix_pre, norm_mix_post=m_norm_mix_post, w_in=m_w_in, pool_w=m_pool_w, pool_scale=m_pool_scale, conv_b_w=m_conv_b_w, w_branch_a=m_w_branch_a, w_branch_b=m_w_branch_b, w_branch_c=m_w_branch_c, w_out=m_w_out, norm_mem_pre=m_norm_mem_pre, norm_mem_post=m_norm_mem_post, norm_memkv=m_norm_memkv, w_mq=m_w_mq, w_mkv=m_w_mkv, w_mo=m_w_mo, norm_ffn_pre=m_norm_ffn_pre, norm_ffn_post=m_norm_ffn_post, w_up=m_w_up, conv_ffn_w=m_conv_ffn_w, w_down=m_w_down)
    v = dict(norm_mix_pre=v_norm_mix_pre, norm_mix_post=v_norm_mix_post, w_in=v_w_in, pool_w=v_pool_w, pool_scale=v_pool_scale, conv_b_w=v_conv_b_w, w_branch_a=v_w_branch_a, w_branch_b=v_w_branch_b, w_branch_c=v_w_branch_c, w_out=v_w_out, norm_mem_pre=v_norm_mem_pre, norm_mem_post=v_norm_mem_post, norm_memkv=v_norm_memkv, w_mq=v_w_mq, w_mkv=v_w_mkv, w_mo=v_w_mo, norm_ffn_pre=v_norm_ffn_pre, norm_ffn_post=v_norm_ffn_post, w_up=v_w_up, conv_ffn_w=v_conv_ffn_w, w_down=v_w_down)

    me = _my_slot()
    mix_big = [n for n in BIG if n not in LATE_BIG]
    block = lambda names, l: [_as_rows(w[n][l], n).astype(MXU) for n in names]
    conv = jnp.concatenate([_pad_flat(w[n], _seg(w[n].size)) for n in F32_GATHERED]).reshape(-1, LANES)
    groups = dict(m=MERGE_BIG, b=LATE_BIG, a=mix_big)
    got0 = _all_gather("weights_all_gather_0", block(['w_in'], 0) + [conv])
    conv_all = got0[-1].reshape(N_DEV, -1)
    small, off = {n: w[n] for n in WEIGHTS if n not in SHARD_AXIS}, 0
    for n in F32_GATHERED:
        small[n] = _from_blocks(conv_all[:, off:off + w[n].size].reshape((N_DEV,) + w[n].shape), 2)
        off += _seg(w[n].size)
    whole = lambda names, got: {n: o.reshape(-1, o.shape[-1]) for n, o in zip(names, got)}
    pushes, after = {}, got0[0]
    for tag, l in (('m', 0), ('b', 0), ('a', 1), ('b', 1)):
        pushes[tag, l] = _push_start(f"weights_push_start_{l}{tag}", block(groups[tag], l), False, after)
        after = pushes[tag, l][4]

    def arrived(tag, l, done):
        return _big_operands(whole(groups[tag], _push_wait(f"weights_push_wait_{l}{tag}", pushes[tag, l], False, done)))

    ctab, stab = _rope_tables(positions[0])
    W0 = _layer_weights(whole(['w_in'], got0), small, 0)
    sv0 = _layer_fwd_branches(x[0], dict(W0, g_mix_pre=W0['g_mix_pre'] + after[0, 0]), ctab, stab)
    W0.update(arrived('m', 0, sv0['att']))
    sv0 = _layer_fwd_merge(W0, sv0)
    W0.update(arrived('b', 0, sv0['x1']))
    x1 = _layer_fwd_late(mem[0], W0, sv0)
    W1 = _layer_weights({}, small, 1)
    W1.update(arrived('a', 1, x1))
    sv1 = _layer_fwd_mix(x1, W1, ctab, stab)
    W1.update(arrived('b', 1, sv1['x1']))
    x2 = _layer_fwd_late(mem[0], W1, sv1)
    dx, acc = _loss_head(x2, loss_target[0])
    loss = lax.psum(jnp.sum(acc) * (0.5 / D), MESH_AXES)
    grads = [None] * DEPTH
    dx, grads[1] = _layer_bwd(dx, mem[0], W1, sv1, ctab, stab)
    sent = [None, [grads[1][n].reshape(N_DEV, -1, grads[1][n].shape[-1]) for n in BIG]]
    push_g = _push_start("grads_push_start_1", sent[1], True, dx)
    dx, g_late = _layer_bwd_late(dx, mem[0], dict(W0, g_ffn_post=W0['g_ffn_post'] + push_g[4][0, 0]), sv0)
    sent_late = [g_late[n].reshape(N_DEV, -1, g_late[n].shape[-1]) for n in LATE_BIG]
    push_l = _push_start("grads_push_start_0", sent_late, True, dx)
    parts, g_mix = _layer_bwd_merge(dx, dict(W0, g_mix_post=W0['g_mix_post'] + push_l[4][0, 0]), sv0)
    sent_merge = [g_mix[n].reshape(N_DEV, -1, g_mix[n].shape[-1]) for n in MERGE_BIG]
    push_m = _push_start("grads_push_start_0m", sent_merge, True, parts[0])
    du, g_br = _layer_bwd_branches(parts, dict(W0, pool_scale=W0['pool_scale'] + push_m[4][0, 0]), sv0, ctab, stab)
    g_mix.update(g_br)
    g_mix['w_in'] = _dw_in(du, sv0)
    sent_in = [g_mix['w_in'].reshape(N_DEV, -1, D)]
    push_i = _push_start("grads_push_start_in", sent_in, True, du)
    dx, g_mix['norm_mix_pre'] = _mix_pre_bwd(dx, du, dict(W0, g_mix_pre=W0['g_mix_pre'] + push_i[4][0, 0]), sv0)
    grads[0] = {**g_late, **g_mix}
    recv1 = _push_wait("grads_push_wait_1", push_g, True, dx)
    recv_late = _push_wait("grads_push_wait_0", push_l, True, dx)
    recv_merge = _push_wait("grads_push_wait_0m", push_m, True, dx)

    misc_names = [n for n in WEIGHTS if n not in BIG]
    stacked = {n: jnp.stack([grads[l][n].reshape(small[n].shape[1:]) for l in range(DEPTH)]) for n in misc_names}
    rows = [(_to_blocks(stacked[n], 2) if n in SHARD_AXIS else jnp.broadcast_to(stacked[n][None], (N_DEV,) + stacked[n].shape))
            for n in misc_names]
    segs = [_seg(w[n].size) for n in misc_names]
    misc = jnp.concatenate([jnp.pad(r.reshape(N_DEV, -1), ((0, 0), (0, s - r[0].size))) for r, s in zip(rows, segs)],
                           axis=1).reshape(N_DEV, -1, LANES)
    push_x = _push_start("grads_push_start_small", [misc], True, dx)
    g_out, per_layer = {}, {}
    for l, names, recv in ((1, BIG, recv1), (0, LATE_BIG, recv_late), (0, MERGE_BIG, recv_merge)):
        for n, r in zip(names, recv):
            per_layer[n, l] = _sum_slots(f"sum_{n}_{l}", r)

    swap = lambda a: jnp.swapaxes(a, 1, 2)

    def update(n):
        if n not in BIG:
            return [g_out[n], *_adamw(f"adamw_{n}", g_out[n], w[n], m[n], v[n])]
        g = jnp.stack([per_layer[n, l] for l in range(DEPTH)])
        if SHARD_AXIS[n] == 2 and w[n].shape[2] % LANES:
            return [swap(a) for a in (g, *_adamw(f"adamw_{n}", g, swap(w[n]), swap(m[n]), swap(v[n])))]
        g = swap(g) if SHARD_AXIS[n] == 2 else g
        return [g, *_adamw(f"adamw_{n}", g, w[n], m[n], v[n])]

    done = {n: update(n) for n in BIG if n != 'w_in'}
    recv_in = _push_wait("grads_push_wait_in", push_i, True, done[BIG[-1]][1])
    per_layer['w_in', 0] = _sum_slots("sum_w_in_0", recv_in[0])
    done['w_in'] = update('w_in')
    misc_sum = _sum_slots("sum_misc", _push_wait("grads_push_wait_small", push_x, True, done['w_in'][1])[0]).reshape(-1)
    off = 0
    for n, s in zip(misc_names, segs):
        g_out[n] = misc_sum[off:off + w[n].size].reshape(w[n].shape)
        done[n] = update(n)
        off += s
    return (loss, dx[None], *[done[n][k] for k in range(4) for n in WEIGHTS])
```

```python
import jax
import jax.numpy as jnp
from jax import lax
from jax.experimental import pallas as pl
from jax.experimental.pallas import tpu as pltpu

F32 = jnp.float32
MXU = jnp.bfloat16
HI = lax.Precision.HIGHEST

D = 1024
DEPTH = 2
POOLW = 384
ATT_W = 768
ATT_O = 256
GATE_W = 3 * D
IN_W = 6912
IN_TILE = 768
IN_ROT = (IN_W - GATE_W) // IN_TILE
MEM_W = 512
D_FF = 2816
EPS = 1e-6
ROPE_THETA = 500000.0
QB = 128
DILS = (1, 4, 16)
NEG = -1e30
MEM_SCALE = 128 ** -0.5
ATT_SCALE = 0.125

ADAM_LR, ADAM_B1, ADAM_B2, ADAM_EPS, ADAM_WD, ADAM_STEP = 0.001, 0.9, 0.999, 1e-08, 0.01, 10

N_DEV = 8
MESH_AXES = ("x", "y", "c")
LANES = 128
FLAT_ALIGN = 2048
ROW_TILE = 1024

WEIGHTS = ['norm_mix_pre', 'norm_mix_post', 'w_in', 'pool_w', 'pool_scale', 'conv_b_w', 'w_branch_a', 'w_branch_b',
           'w_branch_c', 'w_out', 'norm_mem_pre', 'norm_mem_post', 'norm_memkv', 'w_mq', 'w_mkv', 'w_mo',
           'norm_ffn_pre', 'norm_ffn_post', 'w_up', 'conv_ffn_w', 'w_down']
SHARD_AXIS = {'w_in': 2, 'conv_b_w': 2, 'w_branch_a': 2, 'w_branch_b': 2, 'w_branch_c': 2, 'w_out': 1, 'w_mq': 1,
              'w_mkv': 1, 'w_mo': 2, 'w_up': 2, 'conv_ffn_w': 2, 'w_down': 1}
F32_GATHERED = ('conv_b_w', 'conv_ffn_w')
BIG = [n for n in WEIGHTS if n in SHARD_AXIS and n not in F32_GATHERED]
LATE_BIG = ['w_mq', 'w_mkv', 'w_mo', 'w_up', 'w_down']
MERGE_BIG = ['w_branch_a', 'w_branch_b', 'w_branch_c', 'w_out']


VMEM_LIMIT_MB = 60


def _params(sem, vmem_mb):
    del vmem_mb
    return pltpu.CompilerParams(dimension_semantics=sem, vmem_limit_bytes=VMEM_LIMIT_MB << 20)


def _dot(a, b, prec=None):
    return lax.dot_general(a, b, (((1,), (0,)), ((), ())), preferred_element_type=F32, precision=prec)


def _dot_nt(a, b, prec=None):
    return lax.dot_general(a, b, (((1,), (1,)), ((), ())), preferred_element_type=F32, precision=prec)


def _dot_tn(a, b, prec=None):
    return lax.dot_general(a, b, (((0,), (0,)), ((), ())), preferred_element_type=F32, precision=prec)


def _tile(n, cap):
    if n <= cap:
        return n
    best = None
    for t in range(LANES, cap + 1, LANES):
        if n % t == 0:
            best = t
    assert best is not None, (n, cap)
    return best


def _rms(x, g):
    r = lax.rsqrt(jnp.mean(x * x, axis=-1, keepdims=True) + EPS)
    return x * r * g, r


def _rms_bwd(w, y):
    r = lax.rsqrt(jnp.mean(y * y, axis=-1, keepdims=True) + EPS)
    return r * w - y * (r * r * r) * jnp.mean(w * y, axis=-1, keepdims=True), r


def _rows_call(name, body, n_rows, ts, ins, outs, scratch=(), reverse=False, vmem_mb=48, aliases=None):
    nt = n_rows // ts
    assert nt * ts == n_rows

    def tile_of(g):
        return (nt - 1 - g) if reverse else g

    in_specs, args = [], []
    for op in ins:
        if op[0] == "t":
            _, a, cw, cb = op
            in_specs.append(pl.BlockSpec((ts, cw), lambda g, cb=cb: (tile_of(g), cb)))
        elif op[0] == "h":
            _, a, hr, cw, cb = op
            in_specs.append(pl.BlockSpec((hr, cw), lambda g, cb=cb, k=ts // hr: (jnp.maximum(tile_of(g) * k - 1, 0), cb)))
        elif op[0] == "x":
            _, a = op
            in_specs.append(pl.BlockSpec(memory_space=pl.ANY))
        else:
            _, a = op
            in_specs.append(pl.BlockSpec(a.shape, lambda g, n=a.ndim: (0,) * n))
        args.append(a)
    out_specs, out_shape = [], []
    for op in outs:
        if op[0] == "t":
            _, cols, dt = op
            out_specs.append(pl.BlockSpec((ts, cols), lambda g: (tile_of(g), 0)))
            out_shape.append(jax.ShapeDtypeStruct((n_rows, cols), dt))
        elif op[0] == "c":
            _, total, cols, cb, dt = op
            out_specs.append(pl.BlockSpec((ts, cols), lambda g, cb=cb: (tile_of(g), cb)))
            out_shape.append(jax.ShapeDtypeStruct((n_rows, total), dt))
        else:
            _, shp, dt = op
            out_specs.append(pl.BlockSpec(shp, lambda g, n=len(shp): (0,) * n))
            out_shape.append(jax.ShapeDtypeStruct(shp, dt))

    def kern(*refs):
        g = pl.program_id(0)
        body(tile_of(g), g, *refs)

    return pl.pallas_call(kern, grid=(nt,), in_specs=in_specs, out_specs=out_specs, out_shape=out_shape,
                          scratch_shapes=list(scratch), input_output_aliases=aliases or {},
                          compiler_params=_params(("arbitrary",), vmem_mb), name=name)(*args)


def _acc(ref, g, val):
    @pl.when(g == 0)
    def _():
        ref[...] = val

    @pl.when(g != 0)
    def _():
        ref[...] += val


def _norm_mm(name, x, g, w, ts, tn, out_dtype=F32, wt=False, rot=0):
    S, K = x.shape
    N = w.shape[0] if wt else w.shape[1]
    assert wt or not rot

    def body(x_ref, g_ref, w_ref, o_ref, h_ref, hs):
        @pl.when(pl.program_id(1) == 0)
        def _():
            h, _ = _rms(x_ref[...], g_ref[...])
            hs[...] = h.astype(MXU)
            h_ref[...] = h.astype(MXU)

        o_ref[...] = (_dot_nt if wt else _dot)(hs[...], w_ref[...]).astype(out_dtype)

    w_spec = pl.BlockSpec((tn, K), lambda i, j: ((j + rot) % (N // tn), 0)) if wt else pl.BlockSpec((K, tn), lambda i, j: (0, j))
    return pl.pallas_call(
        body, grid=(S // ts, N // tn),
        in_specs=[pl.BlockSpec((ts, K), lambda i, j: (i, 0)), pl.BlockSpec((1, K), lambda i, j: (0, 0)), w_spec],
        out_specs=[pl.BlockSpec((ts, tn), lambda i, j: (i, j)), pl.BlockSpec((ts, K), lambda i, j: (i, 0))],
        out_shape=[jax.ShapeDtypeStruct((S, N), out_dtype), jax.ShapeDtypeStruct((S, K), MXU)],
        scratch_shapes=[pltpu.VMEM((ts, K), MXU)],
        compiler_params=_params(("arbitrary", "arbitrary"), 48), name=name)(x, g, w)


def _mm_nt(name, a, b, ts, tn, out_dtype=F32):
    M, K = a.shape
    N = b.shape[0]

    def body(a_ref, b_ref, o_ref):
        o_ref[...] = _dot_nt(a_ref[...], b_ref[...]).astype(out_dtype)

    return pl.pallas_call(
        body, grid=(M // ts, N // tn),
        in_specs=[pl.BlockSpec((ts, K), lambda i, j: (i, 0)), pl.BlockSpec((tn, K), lambda i, j: (j, 0))],
        out_specs=pl.BlockSpec((ts, tn), lambda i, j: (i, j)), out_shape=jax.ShapeDtypeStruct((M, N), out_dtype),
        compiler_params=_params(("arbitrary", "arbitrary"), 48), name=name)(a, b)


def _mm_tn(name, a, b, cap_k=512, cap_n=1024, out_dtype=MXU, rot=0):
    S, K = a.shape
    N = b.shape[1]
    tk, tn = _tile(K, cap_k), _tile(N, cap_n)

    def body(a_ref, b_ref, o_ref):
        o_ref[...] = _dot_tn(a_ref[...], b_ref[...]).astype(out_dtype)

    return pl.pallas_call(
        body, grid=(K // tk, N // tn),
        in_specs=[pl.BlockSpec((S, tk), lambda i, j: (0, i)), pl.BlockSpec((S, tn), lambda i, j: (0, j))],
        out_specs=pl.BlockSpec((tk, tn), lambda i, j: ((i + rot) % (K // tk), j)), out_shape=jax.ShapeDtypeStruct((K, N), out_dtype),
        compiler_params=_params(("arbitrary", "arbitrary"), 48), name=name)(a, b)


def _pool_cols(shape):
    col = lax.broadcasted_iota(jnp.int32, shape, 1)
    return col < 96, col < 192, col < 288


def _pool_select(s2, s4, s8, s16):
    c1, c2, c3 = _pool_cols(s2.shape)
    return jnp.where(c1, s2, jnp.where(c2, s4, jnp.where(c3, s8, s16)))


def _pool_cnt(t0, ts):
    c1, c2, c3 = _pool_cols((ts, POOLW))
    win = jnp.where(c1, 2, jnp.where(c2, 4, jnp.where(c3, 8, 16)))
    t = t0 + lax.broadcasted_iota(jnp.int32, (ts, POOLW), 0)
    return jnp.minimum(t + 1, win).astype(F32)


def _pooled(a, prev, t0):
    ts = a.shape[0]
    ext = jnp.concatenate([prev, a], axis=0)
    s2 = ext + pltpu.roll(ext, 1, axis=0)
    s4 = s2 + pltpu.roll(s2, 2, axis=0)
    s8 = s4 + pltpu.roll(s4, 4, axis=0)
    s16 = s8 + pltpu.roll(s8, 8, axis=0)
    sums = _pool_select(s2, s4, s8, s16)[16:]
    return sums / _pool_cnt(t0, ts) - a


def _conv3(z, prev8, w):
    ext = jnp.concatenate([prev8, z], axis=0)
    z1 = pltpu.roll(ext, 1, axis=0)[8:]
    z2 = pltpu.roll(ext, 2, axis=0)[8:]
    return w[0:1] * z2 + w[1:2] * z1 + w[2:3] * z, z1, z2


def _conv3_t(dc, next8, w, shifted=False):
    ts = dc.shape[0]
    ext = jnp.concatenate([dc, next8], axis=0)
    n = ts + 8
    u1 = pltpu.roll(ext, n - 1, axis=0)[:ts]
    u2 = pltpu.roll(ext, n - 2, axis=0)[:ts]
    out = w[2:3] * dc + w[1:2] * u1 + w[0:1] * u2
    return (out, u1, u2) if shifted else out


def _poolconv_fwd(u, wblk, pool_scale, conv_b, ts=256):
    S = u.shape[0]

    def body(i, g, a_ref, bx_ref, bb_ref, bc_ref, wblk_ref, ps_ref, cw_ref, a2_ref, yb_ref, ca, cz):
        @pl.when(g == 0)
        def _():
            ca[...] = jnp.zeros_like(ca)
            cz[...] = jnp.zeros_like(cz)

        a = a_ref[...].astype(F32)
        p = _pooled(a, ca[...], i * ts)
        mixed = _dot(p.astype(MXU), wblk_ref[...])
        a2_ref[...] = (mixed * ps_ref[...]).astype(MXU)
        z = bc_ref[...].astype(F32) * bx_ref[...].astype(F32)
        conv, _, _ = _conv3(z, cz[...], cw_ref[...])
        yb_ref[...] = (bb_ref[...].astype(F32) * conv).astype(MXU)
        ca[...] = a[ts - 16:]
        cz[...] = z[ts - 8:]

    ins = [("t", u, POOLW, 8), ("t", u, POOLW, 9), ("t", u, POOLW, 10), ("t", u, POOLW, 11), ("w", wblk), ("w", pool_scale),
           ("w", conv_b)]
    return _rows_call("poolconv_fwd", body, S, ts, ins, [("t", POOLW, MXU), ("t", POOLW, MXU)],
                      scratch=[pltpu.VMEM((16, POOLW), F32), pltpu.VMEM((8, POOLW), F32)])


def _poolconv_bwd(u, d_a2, d_yb, du, wblk, pool_scale, conv_b, ts=256):
    S = u.shape[0]

    def body(i, g, a_ref, bx_ref, bb_ref, bc_ref, ap_ref, bxp_ref, bcp_ref, da2_ref, dyb_ref, wblk_ref, ps_ref, cw_ref, _,
             o_ref, dps_ref, dwb_ref, dcw_ref, ce, cdz):
        @pl.when(g == 0)
        def _():
            ce[...] = jnp.zeros_like(ce)
            cdz[...] = jnp.zeros_like(cdz)

        first = (i > 0).astype(F32)
        a = a_ref[...].astype(F32)
        p = _pooled(a, ap_ref[...].astype(F32) * first, i * ts)
        pb = p.astype(MXU)
        mixed = _dot(pb, wblk_ref[...])
        da2 = da2_ref[...]
        dmixed = (da2 * ps_ref[...]).astype(MXU)
        dp = _dot_nt(dmixed, wblk_ref[...])
        _acc(dps_ref, g, jnp.sum(da2 * mixed, axis=0, keepdims=True))
        _acc(dwb_ref, g, _dot_tn(pb, dmixed))
        e = dp / _pool_cnt(i * ts, ts)
        ext = jnp.concatenate([e, ce[...]], axis=0)
        n = ts + 16
        f2 = ext + pltpu.roll(ext, n - 1, axis=0)
        f4 = f2 + pltpu.roll(f2, n - 2, axis=0)
        f8 = f4 + pltpu.roll(f4, n - 4, axis=0)
        f16 = f8 + pltpu.roll(f8, n - 8, axis=0)
        o_ref[:, 0:POOLW] = (_pool_select(f2, f4, f8, f16)[:ts] - dp).astype(o_ref.dtype)
        ce[...] = e[:16]

        bx, bb, bc = bx_ref[...].astype(F32), bb_ref[...].astype(F32), bc_ref[...].astype(F32)
        z = bc * bx
        w = cw_ref[...]
        conv, z1, z2 = _conv3(z, (bxp_ref[...].astype(F32) * bcp_ref[...].astype(F32))[8:16] * first, w)
        dyb = dyb_ref[...]
        dconv = dyb * bb
        dz = _conv3_t(dconv, cdz[...], w)
        o_ref[:, POOLW:2 * POOLW] = (dz * bc).astype(o_ref.dtype)
        o_ref[:, 2 * POOLW:3 * POOLW] = (dyb * conv).astype(o_ref.dtype)
        o_ref[:, 3 * POOLW:4 * POOLW] = (dz * bx).astype(o_ref.dtype)
        dw = jnp.concatenate([jnp.sum(dconv * z2, axis=0, keepdims=True), jnp.sum(dconv * z1, axis=0, keepdims=True),
                              jnp.sum(dconv * z, axis=0, keepdims=True)], axis=0)
        _acc(dcw_ref, g, dw)
        cdz[...] = dconv[:8]

    ins = [("t", u, POOLW, 8), ("t", u, POOLW, 9), ("t", u, POOLW, 10), ("t", u, POOLW, 11),
           ("h", u, 16, POOLW, 8), ("h", u, 16, POOLW, 9), ("h", u, 16, POOLW, 11),
           ("t", d_a2, POOLW, 0), ("t", d_yb, POOLW, 0), ("w", wblk), ("w", pool_scale), ("w", conv_b), ("x", du)]
    outs = [("c", IN_W, 4 * POOLW, GATE_W // (4 * POOLW), MXU), ("a", (1, POOLW), F32), ("a", (POOLW, POOLW), F32), ("a", (3, POOLW), F32)]
    return _rows_call("poolconv_bwd", body, S, ts, ins, outs, aliases={len(ins) - 1: 0},
                      scratch=[pltpu.VMEM((16, POOLW), F32), pltpu.VMEM((8, POOLW), F32)], reverse=True)


def _rope_tables(positions):
    S = positions.shape[0]
    inv = ROPE_THETA ** (-jnp.arange(0, 16, 2, dtype=F32) / 16)
    ang = positions.astype(F32)[:, None] * inv
    cos, sin = jnp.cos(ang), jnp.sin(ang)
    c64 = jnp.concatenate([cos, cos, jnp.ones((S, 48), F32)], axis=1)
    s64 = jnp.concatenate([-sin, sin, jnp.zeros((S, 48), F32)], axis=1)
    return jnp.concatenate([c64, c64], axis=1), jnp.concatenate([s64, s64], axis=1)


def _partner(x):
    lane = lax.broadcasted_iota(jnp.int32, x.shape, 1) % 64
    return jnp.where(lane < 8, pltpu.roll(x, LANES - 8, axis=1), jnp.where(lane < 16, pltpu.roll(x, 8, axis=1), 0.0))


def _rope(x, c, s):
    return x * c + _partner(x) * s


def _rope_t(x, c, s):
    return x * c + _partner(x * s)


def _rows_of(r, n, d):
    return pl.ds(r, n, stride=d) if d > 1 else pl.ds(0, n)


def _head_masks(shape):
    lane = lax.broadcasted_iota(jnp.int32, shape, 1) // 64
    return [lane == h for h in range(4)]


def _only(mask, x):
    return jnp.where(mask, x, jnp.zeros_like(x))


def _rope_perm(u, ctab, stab, ts=256):
    S = u.shape[0]
    nch = ATT_W // LANES

    def body(*refs):
        chunks, (c_ref, s_ref), outs, scr = refs[:3 * nch], refs[3 * nch:3 * nch + 2], refs[3 * nch + 2:-1], refs[-1]
        for k in range(3 * nch):
            scr[k] = chunks[k][...].astype(F32)
        for g, d in enumerate(DILS):
            n = ts // d
            for r in range(d):
                rows = _rows_of(r, n, d)
                c, s = c_ref[rows, :], s_ref[rows, :]
                for which in range(3):
                    parts = [scr.at[which * nch + j][rows, :] for j in (2 * g, 2 * g + 1)]
                    if which < 2:
                        parts = [_rope(x, c, s) for x in parts]
                    outs[which * 3 + g][r] = jnp.concatenate(parts, axis=1).astype(MXU)

    base = (IN_W - 3 * ATT_W) // LANES
    in_specs = [pl.BlockSpec((ts, LANES), lambda i, cb=base + k: (i, cb)) for k in range(3 * nch)]
    in_specs += [pl.BlockSpec((ts, LANES), lambda i: (i, 0))] * 2
    out_specs = [pl.BlockSpec((d, ts // d, ATT_O), lambda i: (0, i, 0)) for _ in range(3) for d in DILS]
    out_shape = [jax.ShapeDtypeStruct((d, S // d, ATT_O), MXU) for _ in range(3) for d in DILS]
    res = pl.pallas_call(body, grid=(S // ts,), in_specs=in_specs, out_specs=out_specs, out_shape=out_shape,
                         scratch_shapes=[pltpu.VMEM((3 * nch, ts, LANES), F32)],
                         compiler_params=_params(("arbitrary",), 32), name="rope_perm")(*([u] * (3 * nch)), ctab, stab)
    return [[res[which * 3 + g].reshape(S, ATT_O) for g in range(3)] for which in range(3)]


def _rope_unperm_bwd(dqkv, du, ctab, stab, ts=256):
    S = dqkv[0][0].shape[0]
    nch = ATT_W // LANES

    def body(*refs):
        ins, (c_ref, s_ref, _, o_ref, scr) = refs[:9], refs[9:]
        for g, d in enumerate(DILS):
            n = ts // d
            for r in range(d):
                rows = _rows_of(r, n, d)
                c, s = c_ref[rows, :], s_ref[rows, :]
                for which in range(3):
                    v = ins[which * 3 + g][r]
                    for half in range(2):
                        x = v[:, half * LANES:(half + 1) * LANES]
                        scr.at[which * nch + 2 * g + half][rows, :] = _rope_t(x, c, s) if which < 2 else x
        for j in range(3 * nch):
            o_ref[:, j * LANES:(j + 1) * LANES] = scr[j].astype(o_ref.dtype)

    in_specs = [pl.BlockSpec((d, ts // d, ATT_O), lambda i: (0, i, 0)) for _ in range(3) for d in DILS]
    in_specs += [pl.BlockSpec((ts, LANES), lambda i: (i, 0))] * 2 + [pl.BlockSpec(memory_space=pl.ANY)]
    args = [dqkv[which][g].reshape(d, S // d, ATT_O) for which in range(3) for g, d in enumerate(DILS)]
    last = (IN_W - 3 * ATT_W) // (3 * ATT_W)
    return pl.pallas_call(body, grid=(S // ts,), in_specs=in_specs, out_specs=pl.BlockSpec((ts, 3 * ATT_W), lambda i: (i, last)),
                          out_shape=jax.ShapeDtypeStruct((S, IN_W), MXU), scratch_shapes=[pltpu.VMEM((3 * nch, ts, LANES), F32)],
                          input_output_aliases={len(in_specs) - 1: 0},
                          compiler_params=_params(("arbitrary",), 32), name="rope_unperm_bwd")(*args, ctab, stab, du)


def _band_mask_keys(has_prev):
    r = lax.broadcasted_iota(jnp.int32, (QB, 2 * QB), 0)
    c = lax.broadcasted_iota(jnp.int32, (QB, 2 * QB), 1)
    return ((c < QB) & (c >= r) & has_prev) | ((c >= QB) & (c - QB <= r))


def _band_mask_queries(has_next):
    r = lax.broadcasted_iota(jnp.int32, (2 * QB, QB), 0)
    c = lax.broadcasted_iota(jnp.int32, (2 * QB, QB), 1)
    return ((r < QB) & (c <= r)) | ((r >= QB) & (c >= r - QB) & has_next)


ASUB = 4
_BIG = pl.BlockSpec((ASUB * QB, ATT_O), lambda b: (b, 0))
_PREV = pl.BlockSpec((QB, ATT_O), lambda b: (jnp.maximum(b * ASUB - 1, 0), 0))


def _sub(ref, j):
    return ref[j * QB:(j + 1) * QB]


def _attn_fwd(g, q, k, v):
    S = q.shape[0]
    nb = S // QB
    nblk = nb // DILS[g]

    def body(q_ref, kc_ref, kp_ref, vc_ref, vp_ref, o_ref, m_ref, l_ref):
        hm_kv, hm_o = _head_masks((2 * QB, ATT_O)), _head_masks((QB, ATT_O))
        for j in range(ASUB):
            ok = _band_mask_keys(((pl.program_id(0) * ASUB + j) & (nblk - 1)) > 0)
            k2 = jnp.concatenate([kp_ref[...] if j == 0 else _sub(kc_ref, j - 1), _sub(kc_ref, j)], axis=0)
            v2 = jnp.concatenate([vp_ref[...] if j == 0 else _sub(vc_ref, j - 1), _sub(vc_ref, j)], axis=0)
            qv = _sub(q_ref, j)
            o_acc = jnp.zeros((QB, ATT_O), F32)
            m_acc = jnp.zeros((QB, ATT_O), F32)
            l_acc = jnp.zeros((QB, ATT_O), F32)
            for h in range(4):
                s = jnp.where(ok, _dot_nt(qv, _only(hm_kv[h], k2)) * ATT_SCALE, NEG)
                m = jnp.max(s, axis=1, keepdims=True)
                p = jnp.exp(s - m)
                o_acc = o_acc + _dot(p.astype(MXU), _only(hm_kv[h], v2))
                m_acc = jnp.where(hm_o[h], m, m_acc)
                l_acc = jnp.where(hm_o[h], jnp.sum(p, axis=1, keepdims=True), l_acc)
            o_ref[j * QB:(j + 1) * QB] = o_acc
            m_ref[j * QB:(j + 1) * QB] = m_acc
            l_ref[j * QB:(j + 1) * QB] = l_acc

    shp = jax.ShapeDtypeStruct((S, ATT_O), F32)
    return pl.pallas_call(body, grid=(nb // ASUB,), in_specs=[_BIG, _BIG, _PREV, _BIG, _PREV],
                          out_specs=[_BIG] * 3, out_shape=[shp, shp, shp], compiler_params=_params(("arbitrary",), 32),
                          name=f"attn_fwd_{g}")(q, k, k, v, v)


def _natural(ref, d, scr, ts):
    if d == 1:
        return ref[0]
    n = ts // d
    for r in range(d):
        v = ref[r]
        scr.at[0][pl.ds(r, n, stride=d), :] = v[:, 0:LANES]
        scr.at[1][pl.ds(r, n, stride=d), :] = v[:, LANES:2 * LANES]
    return jnp.concatenate([scr[0], scr[1]], axis=1)


def _attn_combine(oml, ts=256):
    S = oml[0][0].shape[0]

    def body(*refs):
        ins, (att_ref, out_ref, lse_ref, scr) = refs[:9], refs[9:]
        o, m, l = [[_natural(ins[3 * g + k], d, scr, ts) for g, d in enumerate(DILS)] for k in range(3)]
        mx = jnp.maximum(jnp.maximum(m[0], m[1]), m[2])
        w = [jnp.exp(m[g] - mx) for g in range(3)]
        den = w[0] * l[0] + w[1] * l[1] + w[2] * l[2]
        out = (w[0] * o[0] + w[1] * o[1] + w[2] * o[2]) / den
        out_ref[...] = out
        att_ref[...] = out.astype(MXU)
        lse_ref[...] = mx + jnp.log(den)

    in_specs = [pl.BlockSpec((d, ts // d, ATT_O), lambda i: (0, i, 0)) for d in DILS for _ in range(3)]
    args = [a.reshape(d, S // d, ATT_O) for d, grp in zip(DILS, oml) for a in grp]
    blk = pl.BlockSpec((ts, ATT_O), lambda i: (i, 0))
    return pl.pallas_call(body, grid=(S // ts,), in_specs=in_specs, out_specs=[blk, blk, blk],
                          out_shape=[jax.ShapeDtypeStruct((S, ATT_O), MXU), jax.ShapeDtypeStruct((S, ATT_O), F32),
                                     jax.ShapeDtypeStruct((S, ATT_O), F32)],
                          scratch_shapes=[pltpu.VMEM((2, ts, LANES), F32)], compiler_params=_params(("arbitrary",), 32),
                          name="attn_combine")(*args)


def _attn_bwd_prep(datt, o, lse, ts=256):
    S = datt.shape[0]

    def body(da0, da1, o_ref, l0, l1, *rest):
        outs, dl = rest[:9], rest[9]
        prod = jnp.concatenate([da0[...], da1[...]], axis=1) * o_ref[...]
        delta = jnp.zeros((ts, ATT_O), F32)
        for hm in _head_masks((ts, ATT_O)):
            delta = jnp.where(hm, jnp.sum(_only(hm, prod), axis=1, keepdims=True), delta)
        dl[0] = delta[:, 0:LANES]
        dl[1] = delta[:, LANES:2 * LANES]
        for g, d in enumerate(DILS):
            n = ts // d
            for r in range(d):
                rows = _rows_of(r, n, d)
                outs[g][r] = jnp.concatenate([da0[rows, :], da1[rows, :]], axis=1).astype(MXU)
                outs[3 + g][r] = jnp.concatenate([dl.at[0][rows, :], dl.at[1][rows, :]], axis=1)
                outs[6 + g][r] = jnp.concatenate([l0[rows, :], l1[rows, :]], axis=1)

    half = lambda j: pl.BlockSpec((ts, LANES), lambda i: (i, j))
    out_specs = [pl.BlockSpec((d, ts // d, ATT_O), lambda i: (0, i, 0)) for _ in range(3) for d in DILS]
    out_shape = [jax.ShapeDtypeStruct((d, S // d, ATT_O), dt) for dt in (MXU, F32, F32) for d in DILS]
    res = pl.pallas_call(body, grid=(S // ts,), in_specs=[half(0), half(1), pl.BlockSpec((ts, ATT_O), lambda i: (i, 0)), half(0), half(1)],
                         out_specs=out_specs, out_shape=out_shape, scratch_shapes=[pltpu.VMEM((2, ts, LANES), F32)],
                         compiler_params=_params(("arbitrary",), 32), name="attn_bwd_prep")(datt, datt, o, lse, lse)
    return [[res[k * 3 + g].reshape(S, ATT_O) for g in range(3)] for k in range(3)]


def _head_col(x, h):
    return x[:, h * 64:h * 64 + 1]


def _attn_bwd(g, q, k, v, do, delta, lse):
    S = q.shape[0]
    nb = S // QB
    nblk = nb // DILS[g]

    def body(k_ref, v_ref, qc_ref, qn_ref, doc_ref, don_ref, dlc_ref, dln_ref, lc_ref, ln_ref, dq_ref, dk_ref, dv_ref, dq_scr):
        hms, hmk = _head_masks((2 * QB, ATT_O)), _head_masks((QB, ATT_O))
        first = pl.program_id(0) == 0

        @pl.when(first)
        def _():
            dq_scr[0:QB] = jnp.zeros((QB, ATT_O), F32)

        @pl.when(jnp.logical_not(first))
        def _():
            dq_scr[0:QB] = dq_scr[ASUB * QB:(ASUB + 1) * QB]

        dq_scr[QB:(ASUB + 1) * QB] = jnp.zeros((ASUB * QB, ATT_O), F32)

        def both(cur_ref, nxt_ref, j):
            return jnp.concatenate([_sub(cur_ref, j), nxt_ref[...] if j == ASUB - 1 else _sub(cur_ref, j + 1)], axis=0)

        for j in range(ASUB):
            ok = _band_mask_queries(((pl.program_id(0) * ASUB + j + 1) & (nblk - 1)) > 0)
            q2, do2, dl2, lse2 = both(qc_ref, qn_ref, j), both(doc_ref, don_ref, j), both(dlc_ref, dln_ref, j), both(lc_ref, ln_ref, j)
            kv, vv = _sub(k_ref, j), _sub(v_ref, j)
            dk = jnp.zeros((QB, ATT_O), F32)
            dv = jnp.zeros((QB, ATT_O), F32)
            dq2 = jnp.zeros((2 * QB, ATT_O), F32)
            for h, hm in enumerate(hms):
                qh, doh = _only(hm, q2), _only(hm, do2)
                p = jnp.where(ok, jnp.exp(_dot_nt(qh, kv) * ATT_SCALE - _head_col(lse2, h)), 0.0)
                ds = (p * (_dot_nt(doh, vv) - _head_col(dl2, h))).astype(MXU)
                dv = dv + _dot_tn(p.astype(MXU), doh)
                dk = dk + _dot_tn(ds, qh)
                dq2 = dq2 + _dot(ds, _only(hmk[h], kv))
            dk_ref[j * QB:(j + 1) * QB] = dk * ATT_SCALE
            dv_ref[j * QB:(j + 1) * QB] = dv
            dq_scr[j * QB:(j + 2) * QB] += dq2
        dq_ref[...] = dq_scr[0:ASUB * QB] * ATT_SCALE

    nxt = pl.BlockSpec((QB, ATT_O), lambda b: (jnp.minimum((b + 1) * ASUB, nb - 1), 0))
    shp = jax.ShapeDtypeStruct((S, ATT_O), F32)
    return pl.pallas_call(body, grid=(nb // ASUB,), in_specs=[_BIG, _BIG, _BIG, nxt, _BIG, nxt, _BIG, nxt, _BIG, nxt], out_specs=[_BIG] * 3,
                          out_shape=[shp, shp, shp], scratch_shapes=[pltpu.VMEM(((ASUB + 1) * QB, ATT_O), F32)],
                          compiler_params=_params(("arbitrary",), 32), name=f"attn_bwd_{g}")(k, v, q, q, do, do, delta, delta, lse, lse)


def _merge_fwd(x0, u, a2, yb, att, wa, wb, wc, w_out, g_post, ts=256):
    S = x0.shape[0]

    def body(i, g, x_ref, gate_ref, a2_ref, yb_ref, att_ref, wa_ref, wb_ref, wc_ref, wo_ref, gp_ref, mg_ref, y_ref, xo_ref):
        gate = lambda n: jax.nn.sigmoid(gate_ref[:, n * D:(n + 1) * D].astype(F32))
        merged = gate(0) * _dot_nt(a2_ref[...], wa_ref[...])
        merged = merged + gate(1) * _dot_nt(yb_ref[...], wb_ref[...])
        merged = merged + gate(2) * _dot_nt(att_ref[...], wc_ref[...])
        mb = merged.astype(MXU)
        mg_ref[...] = mb
        y = _dot(mb, wo_ref[...])
        y_ref[...] = y
        xo_ref[...] = x_ref[...] + _rms(y, gp_ref[...])[0]

    ins = [("t", x0, D, 0), ("t", u, GATE_W, 0), ("t", a2, POOLW, 0), ("t", yb, POOLW, 0), ("t", att, ATT_O, 0),
           ("w", wa), ("w", wb), ("w", wc), ("w", w_out), ("w", g_post)]
    return _rows_call("merge_fwd", body, S, ts, ins, [("t", D, MXU), ("t", D, F32), ("t", D, F32)])


def _merge_bwd(dx, y1, u, a2, yb, att, merged, wa, wb, wc, w_out, g_post, ts=256):
    S = dx.shape[0]
    last = S // ts - 1

    def body(i, g, dx_ref, y_ref, gate_ref, a2_ref, yb_ref, att_ref, mg_ref, wa_ref, wb_ref, wc_ref, wo_ref, gp_ref,
             dgate_ref, da2_ref, dyb_ref, datt_ref, dgp_ref, dwo_ref, dwa_ref, dwb_ref, dwc_ref, acc_o, acc_a, acc_b, acc_c):
        @pl.when(g == 0)
        def _():
            for acc in (acc_o, acc_a, acc_b, acc_c):
                acc[...] = jnp.zeros_like(acc)

        dxv, y = dx_ref[...], y_ref[...]
        dy, r = _rms_bwd(dxv * gp_ref[...], y)
        _acc(dgp_ref, g, jnp.sum(dxv * (y * r), axis=0, keepdims=True))
        dyb16 = dy.astype(MXU)
        acc_o[...] += _dot_tn(mg_ref[...], dyb16)
        dm = _dot_nt(dyb16, wo_ref[...])
        for n, (src, w_ref, din_ref, acc) in enumerate(((a2_ref, wa_ref, da2_ref, acc_a), (yb_ref, wb_ref, dyb_ref, acc_b),
                                                       (att_ref, wc_ref, datt_ref, acc_c))):
            gt = jax.nn.sigmoid(gate_ref[:, n * D:(n + 1) * D].astype(F32))
            br = _dot_nt(src[...], w_ref[...])
            dgate_ref[:, n * D:(n + 1) * D] = (dm * br * gt * (1.0 - gt)).astype(dgate_ref.dtype)
            dbr = (dm * gt).astype(MXU)
            acc[...] += _dot_tn(dbr, src[...])
            din_ref[...] = _dot(dbr, w_ref[...])

        @pl.when(g == last)
        def _():
            for out, acc in ((dwo_ref, acc_o), (dwa_ref, acc_a), (dwb_ref, acc_b), (dwc_ref, acc_c)):
                out[...] = acc[...].astype(MXU)

    ins = [("t", dx, D, 0), ("t", y1, D, 0), ("t", u, GATE_W, 0), ("t", a2, POOLW, 0), ("t", yb, POOLW, 0), ("t", att, ATT_O, 0),
           ("t", merged, D, 0), ("w", wa), ("w", wb), ("w", wc), ("w", w_out), ("w", g_post)]
    wshapes = [(D, D), (D, POOLW), (D, POOLW), (D, ATT_O)]
    outs = [("c", IN_W, GATE_W, 0, MXU), ("t", POOLW, F32), ("t", POOLW, F32), ("t", ATT_O, F32), ("a", (1, D), F32)]
    outs += [("a", s, MXU) for s in wshapes]
    return _rows_call("merge_bwd", body, S, ts, ins, outs, scratch=[pltpu.VMEM(s, F32) for s in wshapes])


def _prenorm_bwd(name, dx_res, du, wt, x, g_pre, ts=256, lead=0):
    S = x.shape[0]
    N = du.shape[1]

    def body(i, g, dx_ref, du_ref, wt_ref, x_ref, g_ref, o_ref, dg_ref):
        if lead:
            dhv = _dot(du_ref[:, 0:lead], wt_ref[N - lead:N, :]) + _dot(du_ref[:, lead:N], wt_ref[0:N - lead, :])
        else:
            dhv = _dot(du_ref[...], wt_ref[...])
        xv = x_ref[...]
        dxn, r = _rms_bwd(dhv * g_ref[...], xv)
        o_ref[...] = dx_ref[...] + dxn
        _acc(dg_ref, g, jnp.sum(dhv * (xv * r), axis=0, keepdims=True))

    ins = [("t", dx_res, D, 0), ("t", du, N, 0), ("w", wt), ("t", x, D, 0), ("w", g_pre)]
    return _rows_call(name, body, S, ts, ins, [("t", D, F32), ("a", (1, D), F32)], vmem_mb=52)


def _mem_heads(qm, kv_ref):
    out = []
    for h in range(4):
        q = qm[:, h * 128:(h + 1) * 128].astype(MXU)
        k = kv_ref[:, h * 128:(h + 1) * 128]
        v = kv_ref[:, MEM_W + h * 128:MEM_W + (h + 1) * 128]
        sc = _dot_nt(q, k) * MEM_SCALE
        e = jnp.exp(sc - jnp.max(sc, axis=1, keepdims=True))
        out.append((e / jnp.sum(e, axis=1, keepdims=True), q, k, v))
    return out


def _mem_fwd(x1, kv, g_pre, w_mq, w_mo, g_post, ts=256):
    S = x1.shape[0]

    def body(i, g, x_ref, kv_ref, gq_ref, wq_ref, wo_ref, gp_ref, om_ref, y_ref, xo_ref):
        x = x_ref[...]
        hb = _rms(x, gq_ref[...])[0].astype(MXU)
        qm = _dot(hb, wq_ref[...])
        om = jnp.concatenate([_dot(p.astype(MXU), v) for p, _, _, v in _mem_heads(qm, kv_ref)], axis=1).astype(MXU)
        om_ref[...] = om
        y = _dot_nt(om, wo_ref[...])
        y_ref[...] = y
        xo_ref[...] = x + _rms(y, gp_ref[...])[0]

    ins = [("t", x1, D, 0), ("w", kv), ("w", g_pre), ("w", w_mq), ("w", w_mo), ("w", g_post)]
    return _rows_call("mem_fwd", body, S, ts, ins, [("t", MEM_W, MXU), ("t", D, F32), ("t", D, F32)])


def _mem_bwd(dx2, ym, x1, om, kv, g_pre, w_mq, w_mo, g_post, ts=256):
    S = x1.shape[0]
    last = S // ts - 1

    def body(i, g, dx_ref, y_ref, x_ref, om_ref, kv_ref, gq_ref, wq_ref, wo_ref, gp_ref, dxo_ref, dgp_ref, dgq_ref, dkv_ref,
             dwo_ref, dwq_ref, acc_o, acc_q):
        dxv, y, x = dx_ref[...], y_ref[...], x_ref[...]
        dy, r = _rms_bwd(dxv * gp_ref[...], y)
        _acc(dgp_ref, g, jnp.sum(dxv * (y * r), axis=0, keepdims=True))
        dyb = dy.astype(MXU)
        dom = _dot(dyb, wo_ref[...])
        h, r1 = _rms(x, gq_ref[...])
        hb = h.astype(MXU)
        qm = _dot(hb, wq_ref[...])
        dqs = []

        @pl.when(g == 0)
        def _():
            dkv_ref[...] = jnp.zeros_like(dkv_ref)
            acc_o[...] = jnp.zeros_like(acc_o)
            acc_q[...] = jnp.zeros_like(acc_q)

        acc_o[...] += _dot_tn(dyb, om_ref[...])

        for hh, (p, q, k, v) in enumerate(_mem_heads(qm, kv_ref)):
            doh = dom[:, hh * 128:(hh + 1) * 128].astype(MXU)
            dp = _dot_nt(doh, v)
            dsc = (p * (dp - jnp.sum(dp * p, axis=1, keepdims=True)) * MEM_SCALE).astype(MXU)
            dqs.append(_dot(dsc, k))
            dkv_ref[:, hh * 128:(hh + 1) * 128] += _dot_tn(dsc, q)
            dkv_ref[:, MEM_W + hh * 128:MEM_W + (hh + 1) * 128] += _dot_tn(p.astype(MXU), doh)
        dq = jnp.concatenate(dqs, axis=1).astype(MXU)
        acc_q[...] += _dot_tn(hb, dq)
        dh = _dot_nt(dq, wq_ref[...])
        _acc(dgq_ref, g, jnp.sum(dh * (x * r1), axis=0, keepdims=True))
        dxo_ref[...] = dxv + _rms_bwd(dh * gq_ref[...], x)[0]

        @pl.when(g == last)
        def _():
            dwo_ref[...] = acc_o[...].astype(MXU)
            dwq_ref[...] = acc_q[...].astype(MXU)

    ins = [("t", dx2, D, 0), ("t", ym, D, 0), ("t", x1, D, 0), ("t", om, MEM_W, 0), ("w", kv), ("w", g_pre), ("w", w_mq), ("w", w_mo),
           ("w", g_post)]
    outs = [("t", D, F32), ("a", (1, D), F32), ("a", (1, D), F32), ("a", (256, D), F32), ("a", (D, MEM_W), MXU), ("a", (D, MEM_W), MXU)]
    return _rows_call("mem_bwd", body, S, ts, ins, outs, scratch=[pltpu.VMEM((D, MEM_W), F32), pltpu.VMEM((D, MEM_W), F32)])


def _gain_grad(name, dn, x):
    n = x.shape[0]

    def body(i, g, dn_ref, x_ref, o_ref):
        xv = x_ref[...]
        r = lax.rsqrt(jnp.mean(xv * xv, axis=-1, keepdims=True) + EPS)
        o_ref[...] = jnp.sum(dn_ref[...] * (xv * r), axis=0, keepdims=True)

    return _rows_call(name, body, n, n, [("t", dn, D, 0), ("t", x, D, 0)], [("a", (1, D), F32)])[0]


def _ffn_fwd(x2, u3, conv_f, w_down, g_post, ts=256):
    S = x2.shape[0]

    def body(i, g, x_ref, ua_ref, ub_ref, cw_ref, wd_ref, gp_ref, act_ref, y_ref, xo_ref, c_ref, cu):
        @pl.when(g == 0)
        def _():
            cu[...] = jnp.zeros_like(cu)

        ua = ua_ref[...].astype(F32)
        c, _, _ = _conv3(ua, cu[...], cw_ref[...])
        c_ref[...] = c.astype(MXU)
        act = (c * jax.nn.sigmoid(c) * ub_ref[...].astype(F32)).astype(MXU)
        act_ref[...] = act
        y = _dot(act, wd_ref[...])
        y_ref[...] = y
        xo_ref[...] = x_ref[...] + _rms(y, gp_ref[...])[0]
        cu[...] = ua[ts - 8:]

    ins = [("t", x2, D, 0), ("t", u3, D_FF, 0), ("t", u3, D_FF, 1), ("w", conv_f), ("w", w_down), ("w", g_post)]
    return _rows_call("ffn_fwd", body, S, ts, ins, [("t", D_FF, MXU), ("t", D, F32), ("t", D, F32), ("t", D_FF, MXU)],
                      scratch=[pltpu.VMEM((8, D_FF), F32)], vmem_mb=56)


def _ffn_bwd(dx3, y3, u3, c, conv_f, w_down, g_post, ts=128):
    S = dx3.shape[0]

    def body(i, g, dx_ref, y_ref, ua_ref, ub_ref, c_ref, cw_ref, wd_ref, gp_ref, dy_ref, du_ref, dgp_ref, dcw_ref, cdc):
        @pl.when(g == 0)
        def _():
            cdc[...] = jnp.zeros_like(cdc)

        dxv, y = dx_ref[...], y_ref[...]
        dy, r = _rms_bwd(dxv * gp_ref[...], y)
        _acc(dgp_ref, g, jnp.sum(dxv * (y * r), axis=0, keepdims=True))
        dyb = dy.astype(MXU)
        dy_ref[...] = dyb
        dact = _dot_nt(dyb, wd_ref[...])
        ua, c, w = ua_ref[...].astype(F32), c_ref[...].astype(F32), cw_ref[...]
        sg = jax.nn.sigmoid(c)
        du_ref[:, D_FF:2 * D_FF] = (dact * (c * sg)).astype(du_ref.dtype)
        dc = dact * ub_ref[...].astype(F32) * (sg * (1.0 + c * (1.0 - sg)))
        dua, dc1, dc2 = _conv3_t(dc, cdc[...], w, shifted=True)
        du_ref[:, 0:D_FF] = dua.astype(du_ref.dtype)
        dw = jnp.concatenate([jnp.sum(ua * dc2, axis=0, keepdims=True), jnp.sum(ua * dc1, axis=0, keepdims=True),
                              jnp.sum(ua * dc, axis=0, keepdims=True)], axis=0)
        _acc(dcw_ref, g, dw)
        cdc[...] = dc[:8]

    ins = [("t", dx3, D, 0), ("t", y3, D, 0), ("t", u3, D_FF, 0), ("t", u3, D_FF, 1), ("t", c, D_FF, 0), ("w", conv_f),
           ("w", w_down), ("w", g_post)]
    outs = [("t", D, MXU), ("t", 2 * D_FF, MXU), ("a", (1, D), F32), ("a", (3, D_FF), F32)]
    return _rows_call("ffn_bwd", body, S, ts, ins, outs, scratch=[pltpu.VMEM((8, D_FF), F32)], reverse=True, vmem_mb=56)


def _loss_head(x, target, ts=512):
    S = x.shape[0]

    def body(i, g, x_ref, t_ref, dx_ref, acc_ref):
        diff = x_ref[...] - t_ref[...]
        dx_ref[...] = diff * (1.0 / D)
        col = jnp.sum(diff * diff, axis=0, keepdims=True)
        part = col[:, 0:LANES]
        for j in range(1, D // LANES):
            part = part + col[:, j * LANES:(j + 1) * LANES]
        row = lax.broadcasted_iota(jnp.int32, (8, LANES), 0)
        _acc(acc_ref, g, jnp.where(row == 0, jnp.broadcast_to(part, (8, LANES)), 0.0))

    return _rows_call("loss_head", body, S, ts, [("t", x, D, 0), ("t", target, D, 0)], [("t", D, F32), ("a", (8, LANES), F32)])


_OPERAND_NAME = dict(w_in='w_in', w_branch_a='wa', w_branch_b='wb', w_branch_c='wc', w_out='w_out', w_mq='w_mq', w_mkv='w_mkv',
                     w_mo='w_mo', w_up='w_up', w_down='w_down')


def _big_operands(big):
    return {_OPERAND_NAME[n]: a for n, a in big.items()}


def _layer_weights(big, small, l):
    pool_w = small['pool_w'][l].astype(MXU)
    wblk = jnp.zeros((POOLW, POOLW), MXU)
    for g in range(4):
        wblk = lax.dynamic_update_slice(wblk, pool_w[g], (g * 96, g * 96))
    vec = lambda n: small[n][l].reshape(1, -1)
    return dict(
        _big_operands(big),
        wblk=wblk, pool_scale=vec('pool_scale'), conv_b=small['conv_b_w'][l], conv_f=small['conv_ffn_w'][l],
        g_mix_pre=vec('norm_mix_pre'), g_mix_post=vec('norm_mix_post'), g_mem_pre=vec('norm_mem_pre'),
        g_mem_post=vec('norm_mem_post'), g_memkv=vec('norm_memkv'), g_ffn_pre=vec('norm_ffn_pre'), g_ffn_post=vec('norm_ffn_post'))


def _layer_fwd(x0, mem, W, ctab, stab):
    sv = _layer_fwd_mix(x0, W, ctab, stab)
    return _layer_fwd_late(mem, W, sv), sv


def _layer_fwd_mix(x0, W, ctab, stab):
    return _layer_fwd_merge(W, _layer_fwd_branches(x0, W, ctab, stab))


def _layer_fwd_branches(x0, W, ctab, stab):
    sv = dict(x0=x0)
    sv['u'], sv['h1'] = _norm_mm("in_proj", x0, W['g_mix_pre'], W['w_in'], ts=2048, tn=IN_TILE, wt=True, rot=IN_ROT, out_dtype=MXU)
    sv['a2'], sv['yb'] = _poolconv_fwd(sv['u'], W['wblk'], W['pool_scale'], W['conv_b'])
    sv['qkv'] = q3, k3, v3 = _rope_perm(sv['u'], ctab, stab)
    sv['att'], sv['o'], sv['lse'] = _attn_combine([_attn_fwd(g, q3[g], k3[g], v3[g]) for g in range(3)])
    return sv


def _layer_fwd_merge(W, sv):
    sv['merged'], sv['y1'], sv['x1'] = _merge_fwd(sv['x0'], sv['u'], sv['a2'], sv['yb'], sv['att'], W['wa'], W['wb'], W['wc'],
                                                  W['w_out'], W['g_mix_post'])
    return sv


def _layer_fwd_late(mem, W, sv):
    sv['kv'], sv['memn'] = _norm_mm("mem_kv", mem, W['g_memkv'], W['w_mkv'], ts=256, tn=D, out_dtype=MXU)
    sv['om'], sv['ym'], sv['x2'] = _mem_fwd(sv['x1'], sv['kv'], W['g_mem_pre'], W['w_mq'], W['w_mo'], W['g_mem_post'])
    sv['u3'], sv['h3'] = _norm_mm("up_proj", sv['x2'], W['g_ffn_pre'], W['w_up'], ts=2048, tn=1408, wt=True, out_dtype=MXU)
    sv['act'], sv['y3'], x3, sv['c3'] = _ffn_fwd(sv['x2'], sv['u3'], W['conv_f'], W['w_down'], W['g_ffn_post'])
    return x3


def _layer_bwd(dx3, mem, W, sv, ctab, stab):
    dx1, g = _layer_bwd_late(dx3, mem, W, sv)
    dx0, g_mix = _layer_bwd_mix(dx1, W, sv, ctab, stab)
    return dx0, {**g, **g_mix}


def _layer_bwd_late(dx3, mem, W, sv):
    g = {}
    dy3, du3, g['norm_ffn_post'], g['conv_ffn_w'] = _ffn_bwd(dx3, sv['y3'], sv['u3'], sv['c3'], W['conv_f'], W['w_down'], W['g_ffn_post'])
    g['w_down'] = _mm_tn("dw_down", sv['act'], dy3, cap_k=256)
    g['w_up'] = _mm_tn("dw_up", du3, sv['h3'])
    dx2, g['norm_ffn_pre'] = _prenorm_bwd("ffn_pre_bwd", dx3, du3, W['w_up'], sv['x2'], W['g_ffn_pre'])
    dx1, g['norm_mem_post'], g['norm_mem_pre'], dkv, g['w_mo'], g['w_mq'] = _mem_bwd(
        dx2, sv['ym'], sv['x1'], sv['om'], sv['kv'], W['g_mem_pre'], W['w_mq'], W['w_mo'], W['g_mem_post'])
    dkvb = dkv.astype(MXU)
    g['w_mkv'] = _mm_tn("dw_mkv", sv['memn'], dkvb)
    g['norm_memkv'] = _gain_grad("memkv_gain", _mm_nt("d_memn", dkvb, W['w_mkv'], ts=256, tn=512), mem)
    return dx1, g


def _layer_bwd_mix(dx1, W, sv, ctab, stab):
    du, g = _layer_bwd_mixers(dx1, W, sv, ctab, stab)
    g['w_in'] = _dw_in(du, sv)
    dx0, g['norm_mix_pre'] = _mix_pre_bwd(dx1, du, W, sv)
    return dx0, g


def _dw_in(du, sv):
    return _mm_tn("dw_in", du, sv['h1'], cap_k=IN_TILE, rot=IN_ROT)


def _mix_pre_bwd(dx1, du, W, sv):
    return _prenorm_bwd("mix_pre_bwd", dx1, du, W['w_in'], sv['x0'], W['g_mix_pre'], lead=GATE_W)


def _layer_bwd_mixers(dx1, W, sv, ctab, stab):
    parts, g = _layer_bwd_merge(dx1, W, sv)
    du, g_br = _layer_bwd_branches(parts, W, sv, ctab, stab)
    return du, {**g, **g_br}


def _layer_bwd_merge(dx1, W, sv):
    g = {}
    du, da2, dyb, datt, g['norm_mix_post'], g['w_out'], g['w_branch_a'], g['w_branch_b'], g['w_branch_c'] = _merge_bwd(
        dx1, sv['y1'], sv['u'], sv['a2'], sv['yb'], sv['att'], sv['merged'], W['wa'], W['wb'], W['wc'], W['w_out'], W['g_mix_post'])
    return (du, da2, dyb, datt), g


def _layer_bwd_branches(parts, W, sv, ctab, stab):
    du, da2, dyb, datt = parts
    g = {}
    du, g['pool_scale'], dwblk, g['conv_b_w'] = _poolconv_bwd(sv['u'], da2, dyb, du, W['wblk'], W['pool_scale'], W['conv_b'])
    g['pool_w'] = jnp.stack([dwblk[k * 96:(k + 1) * 96, k * 96:(k + 1) * 96] for k in range(4)])
    q3, k3, v3 = sv['qkv']
    do3, dl3, lse3 = _attn_bwd_prep(datt, sv['o'], sv['lse'])
    dqkv3 = [_attn_bwd(i, q3[i], k3[i], v3[i], do3[i], dl3[i], lse3[i]) for i in range(3)]
    du = _rope_unperm_bwd([[t[which] for t in dqkv3] for which in range(3)], du, ctab, stab)
    return du, g


def _local_step(x, mem, positions, target, big, small):
    ctab, stab = _rope_tables(positions)
    Ws = [_layer_weights(big[l], small, l) for l in range(DEPTH)]
    saved = []
    for l in range(DEPTH):
        x, sv = _layer_fwd(x, mem, Ws[l], ctab, stab)
        saved.append(sv)
    dx, acc = _loss_head(x, target)
    loss = jnp.sum(acc) * (0.5 / D)
    grads = [None] * DEPTH
    for l in reversed(range(DEPTH)):
        dx, grads[l] = _layer_bwd(dx, mem, Ws[l], saved[l], ctab, stab)
    return loss, dx, grads


_HBM = pl.BlockSpec(memory_space=pl.ANY)
MESH_ID = pl.DeviceIdType.MESH


def _all_gather(name, xs):
    n = len(xs)

    def body(*refs):
        x_refs, out_refs = refs[:n], refs[n:2 * n]
        send_sems, recv_sems, local_sems = refs[2 * n:]
        x, y, c = lax.axis_index("x"), lax.axis_index("y"), lax.axis_index("c")
        me, sibling = (x, y, c), (x, y, 1 - c)
        chips = [(1 - x, y), (x, 1 - y), (1 - x, 1 - y)]

        def slot(a, p):
            return out_refs[a].at[4 * p[0] + 2 * p[1] + p[2]]

        def copy(a, k, block, to, src=None):
            return pltpu.make_async_remote_copy(src_ref=slot(a, block) if src is None else src, dst_ref=slot(a, block),
                                                send_sem=send_sems.at[a, k], recv_sem=recv_sems.at[a, k], device_id=to,
                                                device_id_type=MESH_ID)

        started = []
        for a in range(n):
            mine = pltpu.make_async_copy(x_refs[a], slot(a, me), local_sems.at[a])
            mine.start()
            started.append(mine)
        first = []
        for a in range(n):
            first.append(copy(a, 0, me, sibling, src=x_refs[a]))
            first += [copy(a, 1 + j, me, (*chip, c), src=x_refs[a]) for j, chip in enumerate(chips)]
        for cp in first:
            cp.start()
        passed = []
        for j, chip in enumerate(chips):
            for a in range(n):
                copy(a, 1 + j, (*chip, c), me).wait_recv()
                fw = copy(a, 4 + j, (*chip, c), sibling)
                fw.start()
                passed.append(fw)
        for a in range(n):
            copy(a, 0, sibling, me).wait_recv()
            for j, chip in enumerate(chips):
                copy(a, 4 + j, (*chip, 1 - c), me).wait_recv()
        for cp in first + passed:
            cp.wait_send()
        for mine in started:
            mine.wait()

    return pl.pallas_call(
        body, out_shape=[jax.ShapeDtypeStruct((N_DEV,) + x.shape, x.dtype) for x in xs], in_specs=[_HBM] * n, out_specs=[_HBM] * n,
        scratch_shapes=[pltpu.SemaphoreType.DMA((n, 7)), pltpu.SemaphoreType.DMA((n, 7)), pltpu.SemaphoreType.DMA((n,))],
        name=name)(*xs)


_SEM =pl.BlockSpec(memory_space=pltpu.SEMAPHORE)
_IN_HBM = pl.BlockSpec(memory_space=pltpu.HBM)
_SIDE_EFFECT = pltpu.SideEffectType.DATAFLOW_SIDE_EFFECTING


def _push_copies(src_refs, land_refs, send_sems, recv_sems, per_peer):
    x, y, c = lax.axis_index("x"), lax.axis_index("y"), lax.axis_index("c")
    me = 4 * x + 2 * y + c
    copies = []
    for r in range(1, N_DEV):
        px, py, pc = x ^ ((r >> 2) & 1), y ^ ((r >> 1) & 1), c ^ (r & 1)
        for a, (s, d) in enumerate(zip(src_refs, land_refs)):
            k = a * (N_DEV - 1) + r - 1
            copies.append(pltpu.make_async_remote_copy(src_ref=s.at[4 * px + 2 * py + pc] if per_peer else s, dst_ref=d.at[me],
                                                       send_sem=send_sems.at[k], recv_sem=recv_sems.at[k],
                                                       device_id=(px, py, pc), device_id_type=MESH_ID))
    return copies


def _push_start(name, srcs, per_peer, after):
    n = len(srcs)
    lands = [lax.empty((N_DEV,) + (s.shape[1:] if per_peer else s.shape), s.dtype) for s in srcs]

    def body(*refs):
        for cp in _push_copies(refs[:n], refs[n:2 * n], refs[2 * n + 1], refs[2 * n + 2], per_peer):
            cp.start()
        refs[-1][...] = jnp.zeros_like(refs[-1])

    hbm = [pltpu.HBM(a.shape, a.dtype) for a in (*srcs, *lands)]
    sems = pltpu.SemaphoreType.DMA((n * (N_DEV - 1),))
    out = pl.pallas_call(
        body, name=name, out_shape=(sems, sems, *hbm, jax.ShapeDtypeStruct((8, LANES), F32)),
        in_specs=[_IN_HBM] * (2 * n) + [pl.BlockSpec(memory_space=pl.ANY)],
        out_specs=(_SEM, _SEM, *[_IN_HBM] * (2 * n), pl.BlockSpec(memory_space=pltpu.VMEM)),
        input_output_aliases={a: 2 + a for a in range(2 * n)},
        compiler_params=pltpu.CompilerParams(has_side_effects=_SIDE_EFFECT),
    )(*[pltpu.with_memory_space_constraint(a, pltpu.HBM) for a in (*srcs, *lands)], after)
    return out[0], out[1], out[2:2 + n], out[2 + n:2 + 2 * n], out[-1]


def _push_wait(name, started, per_peer, after):
    send_sems, recv_sems, srcs, lands, _ = started
    n = len(srcs)

    def body(*refs):
        for cp in _push_copies(refs[:n], refs[n:2 * n], refs[2 * n], refs[2 * n + 1], per_peer):
            cp.wait_send()
            cp.wait_recv()

    out = pl.pallas_call(
        body, name=name, out_shape=[pltpu.HBM(a.shape, a.dtype) for a in (*srcs, *lands)],
        in_specs=[_IN_HBM] * (2 * n) + [_SEM, _SEM, pl.BlockSpec(memory_space=pl.ANY)], out_specs=[_IN_HBM] * (2 * n),
        input_output_aliases={a: a for a in range(2 * n)},
        compiler_params=pltpu.CompilerParams(has_side_effects=_SIDE_EFFECT),
    )(*srcs, *lands, send_sems, recv_sems, after)
    me = _my_slot()
    own = [lax.dynamic_index_in_dim(s, me, 0, keepdims=False) if per_peer else s for s in out[:n]]
    return _with_own(out[n:], own, me)


def _my_slot():
    return 4 * lax.axis_index("x") + 2 * lax.axis_index("y") + lax.axis_index("c")


def _row_tile(rows, cols, budget):
    if rows * cols * 4 <= budget or rows % 16:
        return rows
    best = 16
    for t in range(16, rows + 1, 16):
        if rows % t == 0 and t * cols * 4 <= budget:
            best = t
    return best


def _sum_slots(name, recv):
    _, R, C = recv.shape
    tr = _row_tile(R, C, 1 << 20)

    def body(r_ref, o_ref):
        g = r_ref[0].astype(F32)
        for k in range(1, N_DEV):
            g = g + r_ref[k].astype(F32)
        o_ref[...] = g

    return pl.pallas_call(body, grid=(R // tr,), in_specs=[pl.BlockSpec((N_DEV, tr, C), lambda i: (0, i, 0))],
                          out_specs=pl.BlockSpec((tr, C), lambda i: (i, 0)), out_shape=jax.ShapeDtypeStruct((R, C), F32),
                          compiler_params=_params(("arbitrary",), 32), name=name)(recv)


def _adamw(name, g, w, m, v):
    shape = w.shape
    R, C = shape[-2], shape[-1]
    view = (-1, R, C)
    L = w.size // (R * C)
    tr = _row_tile(R, C, 1 << 20)
    c1 = 1.0 - ADAM_B1 ** ADAM_STEP
    c2 = 1.0 - ADAM_B2 ** ADAM_STEP

    def body(g_ref, w_ref, m_ref, v_ref, d_ref, mo_ref, vo_ref):
        gv = g_ref[...]
        mn = ADAM_B1 * m_ref[...] + (1.0 - ADAM_B1) * gv
        vn = ADAM_B2 * v_ref[...] + (1.0 - ADAM_B2) * (gv * gv)
        mo_ref[...] = mn
        vo_ref[...] = vn
        d_ref[...] = -ADAM_LR * ((mn / c1) / (jnp.sqrt(vn / c2) + ADAM_EPS) + ADAM_WD * w_ref[...])

    blk = pl.BlockSpec((None, tr, C), lambda l, i: (l, i, 0))
    shp = jax.ShapeDtypeStruct((L, R, C), F32)
    outs = pl.pallas_call(body, grid=(L, R // tr), in_specs=[blk, blk, blk, blk], out_specs=[blk, blk, blk], out_shape=[shp, shp, shp],
                          compiler_params=_params(("arbitrary", "arbitrary"), 32), name=name)(*[a.reshape(view) for a in (g, w, m, v)])
    return [o.reshape(shape) for o in outs]


def _pad_flat(a, n):
    a = a.reshape(-1)
    return jnp.pad(a, (0, n - a.shape[0]))


def _seg(n):
    return -(-n // FLAT_ALIGN) * FLAT_ALIGN


def _to_blocks(full, axis):
    shp = full.shape
    return jnp.moveaxis(full.reshape(shp[:axis] + (N_DEV, shp[axis] // N_DEV) + shp[axis + 1:]), axis, 0)


def _from_blocks(blocks, axis):
    b = jnp.moveaxis(blocks, 0, axis)
    shp = b.shape
    return b.reshape(shp[:axis] + (shp[axis] * shp[axis + 1],) + shp[axis + 2:])


def _as_rows(shard, n):
    return shard.T if SHARD_AXIS[n] == 2 else shard


def _with_own(lands, own, me):
    return [lax.dynamic_update_slice(land, o[None], (me, 0, 0)) for land, o in zip(lands, own)]


def kernel(x, mem, positions, norm_mix_pre, norm_mix_post, w_in, pool_w, pool_scale, conv_b_w, w_branch_a, w_branch_b, w_branch_c, w_out, norm_mem_pre, norm_mem_post, norm_memkv, w_mq, w_mkv, w_mo, norm_ffn_pre, norm_ffn_post, w_up, conv_ffn_w, w_down, loss_target, m_norm_mix_pre, m_norm_mix_post, m_w_in, m_pool_w, m_pool_scale, m_conv_b_w, m_w_branch_a, m_w_branch_b, m_w_branch_c, m_w_out, m_norm_mem_pre, m_norm_mem_post, m_norm_memkv, m_w_mq, m_w_mkv, m_w_mo, m_norm_ffn_pre, m_norm_ffn_post, m_w_up, m_conv_ffn_w, m_w_down, v_norm_mix_pre, v_norm_mix_post, v_w_in, v_pool_w, v_pool_scale, v_conv_b_w, v_w_branch_a, v_w_branch_b, v_w_branch_c, v_w_out, v_norm_mem_pre, v_norm_mem_post, v_norm_memkv, v_w_mq, v_w_mkv, v_w_mo, v_norm_ffn_pre, v_norm_ffn_post, v_w_up, v_conv_ffn_w, v_w_down):
    w = dict(norm_mix_pre=norm_mix_pre, norm_mix_post=norm_mix_post, w_in=w_in, pool_w=pool_w, pool_scale=pool_scale, conv_b_w=conv_b_w, w_branch_a=w_branch_a, w_branch_b=w_branch_b, w_branch_c=w_branch_c, w_out=w_out, norm_mem_pre=norm_mem_pre, norm_mem_post=norm_mem_post, norm_memkv=norm_memkv, w_mq=w_mq, w_mkv=w_mkv, w_mo=w_mo, norm_ffn_pre=norm_ffn_pre, norm_ffn_post=norm_ffn_post, w_up=w_up, conv_ffn_w=conv_ffn_w, w_down=w_down)
    m = dict(norm_mix_pre=m_norm_mix_pre, norm_mix_post=m_norm_mix_post, w_in=m_w_in, pool_w=m_pool_w, pool_scale=m_pool_scale, conv_b_w=m_conv_b_w, w_branch_a=m_w_branch_a, w_branch_b=m_w_branch_b, w_branch_c=m_w_branch_c, w_out=m_w_out, norm_mem_pre=m_norm_mem_pre, norm_mem_post=m_norm_mem_post, norm_memkv=m_norm_memkv, w_mq=m_w_mq, w_mkv=m_w_mkv, w_mo=m_w_mo, norm_ffn_pre=m_norm_ffn_pre, norm_ffn_post=m_norm_ffn_post, w_up=m_w_up, conv_ffn_w=m_conv_ffn_w, w_down=m_w_down)
    v = dict(norm_mix_pre=v_norm_mix_pre, norm_mix_post=v_norm_mix_post, w_in=v_w_in, pool_w=v_pool_w, pool_scale=v_pool_scale, conv_b_w=v_conv_b_w, w_branch_a=v_w_branch_a, w_branch_b=v_w_branch_b, w_branch_c=v_w_branch_c, w_out=v_w_out, norm_mem_pre=v_norm_mem_pre, norm_mem_post=v_norm_mem_post, norm_memkv=v_norm_memkv, w_mq=v_w_mq, w_mkv=v_w_mkv, w_mo=v_w_mo, norm_ffn_pre=v_norm_ffn_pre, norm_ffn_post=v_norm_ffn_post, w_up=v_w_up, conv_ffn_w=v_conv_ffn_w, w_down=v_w_down)

    me = _my_slot()
    mix_big = [n for n in BIG if n not in LATE_BIG]
    block = lambda names, l: [_as_rows(w[n][l], n).astype(MXU) for n in names]
    conv = jnp.concatenate([_pad_flat(w[n], _seg(w[n].size)) for n in F32_GATHERED]).reshape(-1, LANES)
    groups = dict(m=MERGE_BIG, b=LATE_BIG, a=mix_big)
    got0 = _all_gather("weights_all_gather_0", block(['w_in'], 0) + [conv])
    conv_all = got0[-1].reshape(N_DEV, -1)
    small, off = {n: w[n] for n in WEIGHTS if n not in SHARD_AXIS}, 0
    for n in F32_GATHERED:
        small[n] = _from_blocks(conv_all[:, off:off + w[n].size].reshape((N_DEV,) + w[n].shape), 2)
        off += _seg(w[n].size)
    whole = lambda names, got: {n: o.reshape(-1, o.shape[-1]) for n, o in zip(names, got)}
    pushes, after = {}, got0[0]
    for tag, l in (('m', 0), ('b', 0), ('a', 1), ('b', 1)):
        pushes[tag, l] = _push_start(f"weights_push_start_{l}{tag}", block(groups[tag], l), False, after)
        after = pushes[tag, l][4]

    def arrived(tag, l, done):
        return _big_operands(whole(groups[tag], _push_wait(f"weights_push_wait_{l}{tag}", pushes[tag, l], False, done)))

    ctab, stab = _rope_tables(positions[0])
    W0 = _layer_weights(whole(['w_in'], got0), small, 0)
    sv0 = _layer_fwd_branches(x[0], dict(W0, g_mix_pre=W0['g_mix_pre'] + after[0, 0]), ctab, stab)
    W0.update(arrived('m', 0, sv0['att']))
    sv0 = _layer_fwd_merge(W0, sv0)
    W0.update(arrived('b', 0, sv0['x1']))
    x1 = _layer_fwd_late(mem[0], W0, sv0)
    W1 = _layer_weights({}, small, 1)
    W1.update(arrived('a', 1, x1))
    sv1 = _layer_fwd_mix(x1, W1, ctab, stab)
    W1.update(arrived('b', 1, sv1['x1']))
    x2 = _layer_fwd_late(mem[0], W1, sv1)
    dx, acc = _loss_head(x2, loss_target[0])
    loss = lax.psum(jnp.sum(acc) * (0.5 / D), MESH_AXES)
    grads = [None] * DEPTH
    dx, grads[1] = _layer_bwd(dx, mem[0], W1, sv1, ctab, stab)
    sent = [None, [grads[1][n].reshape(N_DEV, -1, grads[1][n].shape[-1]) for n in BIG]]
    push_g = _push_start("grads_push_start_1", sent[1], True, dx)
    dx, g_late = _layer_bwd_late(dx, mem[0], dict(W0, g_ffn_post=W0['g_ffn_post'] + push_g[4][0, 0]), sv0)
    sent_late = [g_late[n].reshape(N_DEV, -1, g_late[n].shape[-1]) for n in LATE_BIG]
    push_l = _push_start("grads_push_start_0", sent_late, True, dx)
    parts, g_mix = _layer_bwd_merge(dx, dict(W0, g_mix_post=W0['g_mix_post'] + push_l[4][0, 0]), sv0)
    sent_merge = [g_mix[n].reshape(N_DEV, -1, g_mix[n].shape[-1]) for n in MERGE_BIG]
    push_m = _push_start("grads_push_start_0m", sent_merge, True, parts[0])
    du, g_br = _layer_bwd_branches(parts, dict(W0, pool_scale=W0['pool_scale'] + push_m[4][0, 0]), sv0, ctab, stab)
    g_mix.update(g_br)
    g_mix['w_in'] = _dw_in(du, sv0)
    sent_in = [g_mix['w_in'].reshape(N_DEV, -1, D)]
    push_i = _push_start("grads_push_start_in", sent_in, True, du)
    dx, g_mix['norm_mix_pre'] = _mix_pre_bwd(dx, du, dict(W0, g_mix_pre=W0['g_mix_pre'] + push_i[4][0, 0]), sv0)
    grads[0] = {**g_late, **g_mix}
    recv1 = _push_wait("grads_push_wait_1", push_g, True, dx)
    recv_late = _push_wait("grads_push_wait_0", push_l, True, dx)
    recv_merge = _push_wait("grads_push_wait_0m", push_m, True, dx)

    misc_names = [n for n in WEIGHTS if n not in BIG]
    stacked = {n: jnp.stack([grads[l][n].reshape(small[n].shape[1:]) for l in range(DEPTH)]) for n in misc_names}
    rows = [(_to_blocks(stacked[n], 2) if n in SHARD_AXIS else jnp.broadcast_to(stacked[n][None], (N_DEV,) + stacked[n].shape))
            for n in misc_names]
    segs = [_seg(w[n].size) for n in misc_names]
    misc = jnp.concatenate([jnp.pad(r.reshape(N_DEV, -1), ((0, 0), (0, s - r[0].size))) for r, s in zip(rows, segs)],
                           axis=1).reshape(N_DEV, -1, LANES)
    push_x = _push_start("grads_push_start_small", [misc], True, dx)
    g_out, per_layer = {}, {}
    for l, names, recv in ((1, BIG, recv1), (0, LATE_BIG, recv_late), (0, MERGE_BIG, recv_merge)):
        for n, r in zip(names, recv):
            per_layer[n, l] = _sum_slots(f"sum_{n}_{l}", r)

    swap = lambda a: jnp.swapaxes(a, 1, 2)

    def update(n):
        if n not in BIG:
            return [g_out[n], *_adamw(f"adamw_{n}", g_out[n], w[n], m[n], v[n])]
        g = jnp.stack([per_layer[n, l] for l in range(DEPTH)])
        if SHARD_AXIS[n] == 2 and w[n].shape[2] % LANES:
            return [swap(a) for a in (g, *_adamw(f"adamw_{n}", g, swap(w[n]), swap(m[n]), swap(v[n])))]
        g = swap(g) if SHARD_AXIS[n] == 2 else g
        return [g, *_adamw(f"adamw_{n}", g, w[n], m[n], v[n])]

    done = {n: update(n) for n in BIG if n != 'w_in'}
    recv_in = _push_wait("grads_push_wait_in", push_i, True, done[BIG[-1]][1])
    per_layer['w_in', 0] = _sum_slots("sum_w_in_0", recv_in[0])
    done['w_in'] = update('w_in')
    misc_sum = _sum_slots("sum_misc", _push_wait("grads_push_wait_small", push_x, True, done['w_in'][1])[0]).reshape(-1)
    off = 0
    for n, s in zip(misc_names, segs):
        g_out[n] = misc_sum[off:off + w[n].size].reshape(w[n].shape)
        done[n] = update(n)
        off += s
    return (loss, dx[None], *[done[n][k] for k in range(4) for n in WEIGHTS])
```

```python
import jax
import jax.numpy as jnp
from jax import lax
from jax.experimental import pallas as pl
from jax.experimental.pallas import tpu as pltpu

F32 = jnp.float32
MXU = jnp.bfloat16
HI = lax.Precision.HIGHEST

D = 1024
DEPTH = 2
POOLW = 384
ATT_W = 768
ATT_O = 256
GATE_W = 3 * D
IN_W = 6912
IN_TILE = 768
IN_ROT = (IN_W - GATE_W) // IN_TILE
MEM_W = 512
D_FF = 2816
EPS = 1e-6
ROPE_THETA = 500000.0
QB = 128
DILS = (1, 4, 16)
NEG = -1e30
MEM_SCALE = 128 ** -0.5
ATT_SCALE = 0.125

ADAM_LR, ADAM_B1, ADAM_B2, ADAM_EPS, ADAM_WD, ADAM_STEP = 0.001, 0.9, 0.999, 1e-08, 0.01, 10

N_DEV = 8
MESH_AXES = ("x", "y", "c")
LANES = 128
FLAT_ALIGN = 2048
ROW_TILE = 1024

WEIGHTS = ['norm_mix_pre', 'norm_mix_post', 'w_in', 'pool_w', 'pool_scale', 'conv_b_w', 'w_branch_a', 'w_branch_b',
           'w_branch_c', 'w_out', 'norm_mem_pre', 'norm_mem_post', 'norm_memkv', 'w_mq', 'w_mkv', 'w_mo',
           'norm_ffn_pre', 'norm_ffn_post', 'w_up', 'conv_ffn_w', 'w_down']
SHARD_AXIS = {'w_in': 2, 'conv_b_w': 2, 'w_branch_a': 2, 'w_branch_b': 2, 'w_branch_c': 2, 'w_out': 1, 'w_mq': 1,
              'w_mkv': 1, 'w_mo': 2, 'w_up': 2, 'conv_ffn_w': 2, 'w_down': 1}
F32_GATHERED = ('conv_b_w', 'conv_ffn_w')
BIG = [n for n in WEIGHTS if n in SHARD_AXIS and n not in F32_GATHERED]
LATE_BIG = ['w_mq', 'w_mkv', 'w_mo', 'w_up', 'w_down']
MERGE_BIG = ['w_branch_a', 'w_branch_b', 'w_branch_c', 'w_out']


VMEM_LIMIT_MB = 60


def _params(sem, vmem_mb):
    del vmem_mb
    return pltpu.CompilerParams(dimension_semantics=sem, vmem_limit_bytes=VMEM_LIMIT_MB << 20)


def _dot(a, b, prec=None):
    return lax.dot_general(a, b, (((1,), (0,)), ((), ())), preferred_element_type=F32, precision=prec)


def _dot_nt(a, b, prec=None):
    return lax.dot_general(a, b, (((1,), (1,)), ((), ())), preferred_element_type=F32, precision=prec)


def _dot_tn(a, b, prec=None):
    return lax.dot_general(a, b, (((0,), (0,)), ((), ())), preferred_element_type=F32, precision=prec)


def _tile(n, cap):
    if n <= cap:
        return n
    best = None
    for t in range(LANES, cap + 1, LANES):
        if n % t == 0:
            best = t
    assert best is not None, (n, cap)
    return best


def _rms(x, g):
    r = lax.rsqrt(jnp.mean(x * x, axis=-1, keepdims=True) + EPS)
    return x * r * g, r


def _rms_bwd(w, y):
    r = lax.rsqrt(jnp.mean(y * y, axis=-1, keepdims=True) + EPS)
    return r * w - y * (r * r * r) * jnp.mean(w * y, axis=-1, keepdims=True), r


def _rows_call(name, body, n_rows, ts, ins, outs, scratch=(), reverse=False, vmem_mb=48, aliases=None):
    nt = n_rows // ts
    assert nt * ts == n_rows

    def tile_of(g):
        return (nt - 1 - g) if reverse else g

    in_specs, args = [], []
    for op in ins:
        if op[0] == "t":
            _, a, cw, cb = op
            in_specs.append(pl.BlockSpec((ts, cw), lambda g, cb=cb: (tile_of(g), cb)))
        elif op[0] == "h":
            _, a, hr, cw, cb = op
            in_specs.append(pl.BlockSpec((hr, cw), lambda g, cb=cb, k=ts // hr: (jnp.maximum(tile_of(g) * k - 1, 0), cb)))
        elif op[0] == "x":
            _, a = op
            in_specs.append(pl.BlockSpec(memory_space=pl.ANY))
        else:
            _, a = op
            in_specs.append(pl.BlockSpec(a.shape, lambda g, n=a.ndim: (0,) * n))
        args.append(a)
    out_specs, out_shape = [], []
    for op in outs:
        if op[0] == "t":
            _, cols, dt = op
            out_specs.append(pl.BlockSpec((ts, cols), lambda g: (tile_of(g), 0)))
            out_shape.append(jax.ShapeDtypeStruct((n_rows, cols), dt))
        elif op[0] == "c":
            _, total, cols, cb, dt = op
            out_specs.append(pl.BlockSpec((ts, cols), lambda g, cb=cb: (tile_of(g), cb)))
            out_shape.append(jax.ShapeDtypeStruct((n_rows, total), dt))
        else:
            _, shp, dt = op
            out_specs.append(pl.BlockSpec(shp, lambda g, n=len(shp): (0,) * n))
            out_shape.append(jax.ShapeDtypeStruct(shp, dt))

    def kern(*refs):
        g = pl.program_id(0)
        body(tile_of(g), g, *refs)

    return pl.pallas_call(kern, grid=(nt,), in_specs=in_specs, out_specs=out_specs, out_shape=out_shape,
                          scratch_shapes=list(scratch), input_output_aliases=aliases or {},
                          compiler_params=_params(("arbitrary",), vmem_mb), name=name)(*args)


def _acc(ref, g, val):
    @pl.when(g == 0)
    def _():
        ref[...] = val

    @pl.when(g != 0)
    def _():
        ref[...] += val


def _norm_mm(name, x, g, w, ts, tn, out_dtype=F32, wt=False, rot=0):
    S, K = x.shape
    N = w.shape[0] if wt else w.shape[1]
    assert wt or not rot

    def body(x_ref, g_ref, w_ref, o_ref, h_ref, hs):
        @pl.when(pl.program_id(1) == 0)
        def _():
            h, _ = _rms(x_ref[...], g_ref[...])
            hs[...] = h.astype(MXU)
            h_ref[...] = h.astype(MXU)

        o_ref[...] = (_dot_nt if wt else _dot)(hs[...], w_ref[...]).astype(out_dtype)

    w_spec = pl.BlockSpec((tn, K), lambda i, j: ((j + rot) % (N // tn), 0)) if wt else pl.BlockSpec((K, tn), lambda i, j: (0, j))
    return pl.pallas_call(
        body, grid=(S // ts, N // tn),
        in_specs=[pl.BlockSpec((ts, K), lambda i, j: (i, 0)), pl.BlockSpec((1, K), lambda i, j: (0, 0)), w_spec],
        out_specs=[pl.BlockSpec((ts, tn), lambda i, j: (i, j)), pl.BlockSpec((ts, K), lambda i, j: (i, 0))],
        out_shape=[jax.ShapeDtypeStruct((S, N), out_dtype), jax.ShapeDtypeStruct((S, K), MXU)],
        scratch_shapes=[pltpu.VMEM((ts, K), MXU)],
        compiler_params=_params(("arbitrary", "arbitrary"), 48), name=name)(x, g, w)


def _mm_nt(name, a, b, ts, tn, out_dtype=F32):
    M, K = a.shape
    N = b.shape[0]

    def body(a_ref, b_ref, o_ref):
        o_ref[...] = _dot_nt(a_ref[...], b_ref[...]).astype(out_dtype)

    return pl.pallas_call(
        body, grid=(M // ts, N // tn),
        in_specs=[pl.BlockSpec((ts, K), lambda i, j: (i, 0)), pl.BlockSpec((tn, K), lambda i, j: (j, 0))],
        out_specs=pl.BlockSpec((ts, tn), lambda i, j: (i, j)), out_shape=jax.ShapeDtypeStruct((M, N), out_dtype),
        compiler_params=_params(("arbitrary", "arbitrary"), 48), name=name)(a, b)


def _mm_tn(name, a, b, cap_k=512, cap_n=1024, out_dtype=MXU, rot=0):
    S, K = a.shape
    N = b.shape[1]
    tk, tn = _tile(K, cap_k), _tile(N, cap_n)

    def body(a_ref, b_ref, o_ref):
        o_ref[...] = _dot_tn(a_ref[...], b_ref[...]).astype(out_dtype)

    return pl.pallas_call(
        body, grid=(K // tk, N // tn),
        in_specs=[pl.BlockSpec((S, tk), lambda i, j: (0, i)), pl.BlockSpec((S, tn), lambda i, j: (0, j))],
        out_specs=pl.BlockSpec((tk, tn), lambda i, j: ((i + rot) % (K // tk), j)), out_shape=jax.ShapeDtypeStruct((K, N), out_dtype),
        compiler_params=_params(("arbitrary", "arbitrary"), 48), name=name)(a, b)


def _pool_cols(shape):
    col = lax.broadcasted_iota(jnp.int32, shape, 1)
    return col < 96, col < 192, col < 288


def _pool_select(s2, s4, s8, s16):
    c1, c2, c3 = _pool_cols(s2.shape)
    return jnp.where(c1, s2, jnp.where(c2, s4, jnp.where(c3, s8, s16)))


def _pool_cnt(t0, ts):
    c1, c2, c3 = _pool_cols((ts, POOLW))
    win = jnp.where(c1, 2, jnp.where(c2, 4, jnp.where(c3, 8, 16)))
    t = t0 + lax.broadcasted_iota(jnp.int32, (ts, POOLW), 0)
    return jnp.minimum(t + 1, win).astype(F32)


def _pooled(a, prev, t0):
    ts = a.shape[0]
    ext = jnp.concatenate([prev, a], axis=0)
    s2 = ext + pltpu.roll(ext, 1, axis=0)
    s4 = s2 + pltpu.roll(s2, 2, axis=0)
    s8 = s4 + pltpu.roll(s4, 4, axis=0)
    s16 = s8 + pltpu.roll(s8, 8, axis=0)
    sums = _pool_select(s2, s4, s8, s16)[16:]
    return sums / _pool_cnt(t0, ts) - a


def _conv3(z, prev8, w):
    ext = jnp.concatenate([prev8, z], axis=0)
    z1 = pltpu.roll(ext, 1, axis=0)[8:]
    z2 = pltpu.roll(ext, 2, axis=0)[8:]
    return w[0:1] * z2 + w[1:2] * z1 + w[2:3] * z, z1, z2


def _conv3_t(dc, next8, w, shifted=False):
    ts = dc.shape[0]
    ext = jnp.concatenate([dc, next8], axis=0)
    n = ts + 8
    u1 = pltpu.roll(ext, n - 1, axis=0)[:ts]
    u2 = pltpu.roll(ext, n - 2, axis=0)[:ts]
    out = w[2:3] * dc + w[1:2] * u1 + w[0:1] * u2
    return (out, u1, u2) if shifted else out


def _poolconv_fwd(u, wblk, pool_scale, conv_b, ts=256):
    S = u.shape[0]

    def body(i, g, a_ref, bx_ref, bb_ref, bc_ref, wblk_ref, ps_ref, cw_ref, a2_ref, yb_ref, ca, cz):
        @pl.when(g == 0)
        def _():
            ca[...] = jnp.zeros_like(ca)
            cz[...] = jnp.zeros_like(cz)

        a = a_ref[...].astype(F32)
        p = _pooled(a, ca[...], i * ts)
        mixed = _dot(p.astype(MXU), wblk_ref[...])
        a2_ref[...] = (mixed * ps_ref[...]).astype(MXU)
        z = bc_ref[...].astype(F32) * bx_ref[...].astype(F32)
        conv, _, _ = _conv3(z, cz[...], cw_ref[...])
        yb_ref[...] = (bb_ref[...].astype(F32) * conv).astype(MXU)
        ca[...] = a[ts - 16:]
        cz[...] = z[ts - 8:]

    ins = [("t", u, POOLW, 8), ("t", u, POOLW, 9), ("t", u, POOLW, 10), ("t", u, POOLW, 11), ("w", wblk), ("w", pool_scale),
           ("w", conv_b)]
    return _rows_call("poolconv_fwd", body, S, ts, ins, [("t", POOLW, MXU), ("t", POOLW, MXU)],
                      scratch=[pltpu.VMEM((16, POOLW), F32), pltpu.VMEM((8, POOLW), F32)])


def _poolconv_bwd(u, d_a2, d_yb, du, wblk, pool_scale, conv_b, ts=256):
    S = u.shape[0]

    def body(i, g, a_ref, bx_ref, bb_ref, bc_ref, ap_ref, bxp_ref, bcp_ref, da2_ref, dyb_ref, wblk_ref, ps_ref, cw_ref, _,
             o_ref, dps_ref, dwb_ref, dcw_ref, ce, cdz):
        @pl.when(g == 0)
        def _():
            ce[...] = jnp.zeros_like(ce)
            cdz[...] = jnp.zeros_like(cdz)

        first = (i > 0).astype(F32)
        a = a_ref[...].astype(F32)
        p = _pooled(a, ap_ref[...].astype(F32) * first, i * ts)
        pb = p.astype(MXU)
        mixed = _dot(pb, wblk_ref[...])
        da2 = da2_ref[...]
        dmixed = (da2 * ps_ref[...]).astype(MXU)
        dp = _dot_nt(dmixed, wblk_ref[...])
        _acc(dps_ref, g, jnp.sum(da2 * mixed, axis=0, keepdims=True))
        _acc(dwb_ref, g, _dot_tn(pb, dmixed))
        e = dp / _pool_cnt(i * ts, ts)
        ext = jnp.concatenate([e, ce[...]], axis=0)
        n = ts + 16
        f2 = ext + pltpu.roll(ext, n - 1, axis=0)
        f4 = f2 + pltpu.roll(f2, n - 2, axis=0)
        f8 = f4 + pltpu.roll(f4, n - 4, axis=0)
        f16 = f8 + pltpu.roll(f8, n - 8, axis=0)
        o_ref[:, 0:POOLW] = (_pool_select(f2, f4, f8, f16)[:ts] - dp).astype(o_ref.dtype)
        ce[...] = e[:16]

        bx, bb, bc = bx_ref[...].astype(F32), bb_ref[...].astype(F32), bc_ref[...].astype(F32)
        z = bc * bx
        w = cw_ref[...]
        conv, z1, z2 = _conv3(z, (bxp_ref[...].astype(F32) * bcp_ref[...].astype(F32))[8:16] * first, w)
        dyb = dyb_ref[...]
        dconv = dyb * bb
        dz = _conv3_t(dconv, cdz[...], w)
        o_ref[:, POOLW:2 * POOLW] = (dz * bc).astype(o_ref.dtype)
        o_ref[:, 2 * POOLW:3 * POOLW] = (dyb * conv).astype(o_ref.dtype)
        o_ref[:, 3 * POOLW:4 * POOLW] = (dz * bx).astype(o_ref.dtype)
        dw = jnp.concatenate([jnp.sum(dconv * z2, axis=0, keepdims=True), jnp.sum(dconv * z1, axis=0, keepdims=True),
                              jnp.sum(dconv * z, axis=0, keepdims=True)], axis=0)
        _acc(dcw_ref, g, dw)
        cdz[...] = dconv[:8]

    ins = [("t", u, POOLW, 8), ("t", u, POOLW, 9), ("t", u, POOLW, 10), ("t", u, POOLW, 11),
           ("h", u, 16, POOLW, 8), ("h", u, 16, POOLW, 9), ("h", u, 16, POOLW, 11),
           ("t", d_a2, POOLW, 0), ("t", d_yb, POOLW, 0), ("w", wblk), ("w", pool_scale), ("w", conv_b), ("x", du)]
    outs = [("c", IN_W, 4 * POOLW, GATE_W // (4 * POOLW), MXU), ("a", (1, POOLW), F32), ("a", (POOLW, POOLW), F32), ("a", (3, POOLW), F32)]
    return _rows_call("poolconv_bwd", body, S, ts, ins, outs, aliases={len(ins) - 1: 0},
                      scratch=[pltpu.VMEM((16, POOLW), F32), pltpu.VMEM((8, POOLW), F32)], reverse=True)


def _rope_tables(positions):
    S = positions.shape[0]
    inv = ROPE_THETA ** (-jnp.arange(0, 16, 2, dtype=F32) / 16)
    ang = positions.astype(F32)[:, None] * inv
    cos, sin = jnp.cos(ang), jnp.sin(ang)
    c64 = jnp.concatenate([cos, cos, jnp.ones((S, 48), F32)], axis=1)
    s64 = jnp.concatenate([-sin, sin, jnp.zeros((S, 48), F32)], axis=1)
    return jnp.concatenate([c64, c64], axis=1), jnp.concatenate([s64, s64], axis=1)


def _partner(x):
    lane = lax.broadcasted_iota(jnp.int32, x.shape, 1) % 64
    return jnp.where(lane < 8, pltpu.roll(x, LANES - 8, axis=1), jnp.where(lane < 16, pltpu.roll(x, 8, axis=1), 0.0))


def _rope(x, c, s):
    return x * c + _partner(x) * s


def _rope_t(x, c, s):
    return x * c + _partner(x * s)


def _rows_of(r, n, d):
    return pl.ds(r, n, stride=d) if d > 1 else pl.ds(0, n)


def _head_masks(shape):
    lane = lax.broadcasted_iota(jnp.int32, shape, 1) // 64
    return [lane == h for h in range(4)]


def _only(mask, x):
    return jnp.where(mask, x, jnp.zeros_like(x))


def _rope_perm(u, ctab, stab, ts=256):
    S = u.shape[0]
    nch = ATT_W // LANES

    def body(*refs):
        chunks, (c_ref, s_ref), outs, scr = refs[:3 * nch], refs[3 * nch:3 * nch + 2], refs[3 * nch + 2:-1], refs[-1]
        for k in range(3 * nch):
            scr[k] = chunks[k][...].astype(F32)
        for g, d in enumerate(DILS):
            n = ts // d
            for r in range(d):
                rows = _rows_of(r, n, d)
                c, s = c_ref[rows, :], s_ref[rows, :]
                for which in range(3):
                    parts = [scr.at[which * nch + j][rows, :] for j in (2 * g, 2 * g + 1)]
                    if which < 2:
                        parts = [_rope(x, c, s) for x in parts]
                    outs[which * 3 + g][r] = jnp.concatenate(parts, axis=1).astype(MXU)

    base = (IN_W - 3 * ATT_W) // LANES
    in_specs = [pl.BlockSpec((ts, LANES), lambda i, cb=base + k: (i, cb)) for k in range(3 * nch)]
    in_specs += [pl.BlockSpec((ts, LANES), lambda i: (i, 0))] * 2
    out_specs = [pl.BlockSpec((d, ts // d, ATT_O), lambda i: (0, i, 0)) for _ in range(3) for d in DILS]
    out_shape = [jax.ShapeDtypeStruct((d, S // d, ATT_O), MXU) for _ in range(3) for d in DILS]
    res = pl.pallas_call(body, grid=(S // ts,), in_specs=in_specs, out_specs=out_specs, out_shape=out_shape,
                         scratch_shapes=[pltpu.VMEM((3 * nch, ts, LANES), F32)],
                         compiler_params=_params(("arbitrary",), 32), name="rope_perm")(*([u] * (3 * nch)), ctab, stab)
    return [[res[which * 3 + g].reshape(S, ATT_O) for g in range(3)] for which in range(3)]


def _rope_unperm_bwd(dqkv, du, ctab, stab, ts=256):
    S = dqkv[0][0].shape[0]
    nch = ATT_W // LANES

    def body(*refs):
        ins, (c_ref, s_ref, _, o_ref, scr) = refs[:9], refs[9:]
        for g, d in enumerate(DILS):
            n = ts // d
            for r in range(d):
                rows = _rows_of(r, n, d)
                c, s = c_ref[rows, :], s_ref[rows, :]
                for which in range(3):
                    v = ins[which * 3 + g][r]
                    for half in range(2):
                        x = v[:, half * LANES:(half + 1) * LANES]
                        scr.at[which * nch + 2 * g + half][rows, :] = _rope_t(x, c, s) if which < 2 else x
        for j in range(3 * nch):
            o_ref[:, j * LANES:(j + 1) * LANES] = scr[j].astype(o_ref.dtype)

    in_specs = [pl.BlockSpec((d, ts // d, ATT_O), lambda i: (0, i, 0)) for _ in range(3) for d in DILS]
    in_specs += [pl.BlockSpec((ts, LANES), lambda i: (i, 0))] * 2 + [pl.BlockSpec(memory_space=pl.ANY)]
    args = [dqkv[which][g].reshape(d, S // d, ATT_O) for which in range(3) for g, d in enumerate(DILS)]
    last = (IN_W - 3 * ATT_W) // (3 * ATT_W)
    return pl.pallas_call(body, grid=(S // ts,), in_specs=in_specs, out_specs=pl.BlockSpec((ts, 3 * ATT_W), lambda i: (i, last)),
                          out_shape=jax.ShapeDtypeStruct((S, IN_W), MXU), scratch_shapes=[pltpu.VMEM((3 * nch, ts, LANES), F32)],
                          input_output_aliases={len(in_specs) - 1: 0},
                          compiler_params=_params(("arbitrary",), 32), name="rope_unperm_bwd")(*args, ctab, stab, du)


def _band_mask_keys(has_prev):
    r = lax.broadcasted_iota(jnp.int32, (QB, 2 * QB), 0)
    c = lax.broadcasted_iota(jnp.int32, (QB, 2 * QB), 1)
    return ((c < QB) & (c >= r) & has_prev) | ((c >= QB) & (c - QB <= r))


def _band_mask_queries(has_next):
    r = lax.broadcasted_iota(jnp.int32, (2 * QB, QB), 0)
    c = lax.broadcasted_iota(jnp.int32, (2 * QB, QB), 1)
    return ((r < QB) & (c <= r)) | ((r >= QB) & (c >= r - QB) & has_next)


ASUB = 4
_BIG = pl.BlockSpec((ASUB * QB, ATT_O), lambda b: (b, 0))
_PREV = pl.BlockSpec((QB, ATT_O), lambda b: (jnp.maximum(b * ASUB - 1, 0), 0))


def _sub(ref, j):
    return ref[j * QB:(j + 1) * QB]


def _attn_fwd(g, q, k, v):
    S = q.shape[0]
    nb = S // QB
    nblk = nb // DILS[g]

    def body(q_ref, kc_ref, kp_ref, vc_ref, vp_ref, o_ref, m_ref, l_ref):
        hm_kv, hm_o = _head_masks((2 * QB, ATT_O)), _head_masks((QB, ATT_O))
        for j in range(ASUB):
            ok = _band_mask_keys(((pl.program_id(0) * ASUB + j) & (nblk - 1)) > 0)
            k2 = jnp.concatenate([kp_ref[...] if j == 0 else _sub(kc_ref, j - 1), _sub(kc_ref, j)], axis=0)
            v2 = jnp.concatenate([vp_ref[...] if j == 0 else _sub(vc_ref, j - 1), _sub(vc_ref, j)], axis=0)
            qv = _sub(q_ref, j)
            o_acc = jnp.zeros((QB, ATT_O), F32)
            m_acc = jnp.zeros((QB, ATT_O), F32)
            l_acc = jnp.zeros((QB, ATT_O), F32)
            for h in range(4):
                s = jnp.where(ok, _dot_nt(qv, _only(hm_kv[h], k2)) * ATT_SCALE, NEG)
                m = jnp.max(s, axis=1, keepdims=True)
                p = jnp.exp(s - m)
                o_acc = o_acc + _dot(p.astype(MXU), _only(hm_kv[h], v2))
                m_acc = jnp.where(hm_o[h], m, m_acc)
                l_acc = jnp.where(hm_o[h], jnp.sum(p, axis=1, keepdims=True), l_acc)
            o_ref[j * QB:(j + 1) * QB] = o_acc
            m_ref[j * QB:(j + 1) * QB] = m_acc
            l_ref[j * QB:(j + 1) * QB] = l_acc

    shp = jax.ShapeDtypeStruct((S, ATT_O), F32)
    return pl.pallas_call(body, grid=(nb // ASUB,), in_specs=[_BIG, _BIG, _PREV, _BIG, _PREV],
                          out_specs=[_BIG] * 3, out_shape=[shp, shp, shp], compiler_params=_params(("arbitrary",), 32),
                          name=f"attn_fwd_{g}")(q, k, k, v, v)


def _natural(ref, d, scr, ts):
    if d == 1:
        return ref[0]
    n = ts // d
    for r in range(d):
        v = ref[r]
        scr.at[0][pl.ds(r, n, stride=d), :] = v[:, 0:LANES]
        scr.at[1][pl.ds(r, n, stride=d), :] = v[:, LANES:2 * LANES]
    return jnp.concatenate([scr[0], scr[1]], axis=1)


def _attn_combine(oml, ts=256):
    S = oml[0][0].shape[0]

    def body(*refs):
        ins, (att_ref, out_ref, lse_ref, scr) = refs[:9], refs[9:]
        o, m, l = [[_natural(ins[3 * g + k], d, scr, ts) for g, d in enumerate(DILS)] for k in range(3)]
        mx = jnp.maximum(jnp.maximum(m[0], m[1]), m[2])
        w = [jnp.exp(m[g] - mx) for g in range(3)]
        den = w[0] * l[0] + w[1] * l[1] + w[2] * l[2]
        out = (w[0] * o[0] + w[1] * o[1] + w[2] * o[2]) / den
        out_ref[...] = out
        att_ref[...] = out.astype(MXU)
        lse_ref[...] = mx + jnp.log(den)

    in_specs = [pl.BlockSpec((d, ts // d, ATT_O), lambda i: (0, i, 0)) for d in DILS for _ in range(3)]
    args = [a.reshape(d, S // d, ATT_O) for d, grp in zip(DILS, oml) for a in grp]
    blk = pl.BlockSpec((ts, ATT_O), lambda i: (i, 0))
    return pl.pallas_call(body, grid=(S // ts,), in_specs=in_specs, out_specs=[blk, blk, blk],
                          out_shape=[jax.ShapeDtypeStruct((S, ATT_O), MXU), jax.ShapeDtypeStruct((S, ATT_O), F32),
                                     jax.ShapeDtypeStruct((S, ATT_O), F32)],
                          scratch_shapes=[pltpu.VMEM((2, ts, LANES), F32)], compiler_params=_params(("arbitrary",), 32),
                          name="attn_combine")(*args)


def _attn_bwd_prep(datt, o, lse, ts=256):
    S = datt.shape[0]

    def body(da0, da1, o_ref, l0, l1, *rest):
        outs, dl = rest[:9], rest[9]
        prod = jnp.concatenate([da0[...], da1[...]], axis=1) * o_ref[...]
        delta = jnp.zeros((ts, ATT_O), F32)
        for hm in _head_masks((ts, ATT_O)):
            delta = jnp.where(hm, jnp.sum(_only(hm, prod), axis=1, keepdims=True), delta)
        dl[0] = delta[:, 0:LANES]
        dl[1] = delta[:, LANES:2 * LANES]
        for g, d in enumerate(DILS):
            n = ts // d
            for r in range(d):
                rows = _rows_of(r, n, d)
                outs[g][r] = jnp.concatenate([da0[rows, :], da1[rows, :]], axis=1).astype(MXU)
                outs[3 + g][r] = jnp.concatenate([dl.at[0][rows, :], dl.at[1][rows, :]], axis=1)
                outs[6 + g][r] = jnp.concatenate([l0[rows, :], l1[rows, :]], axis=1)

    half = lambda j: pl.BlockSpec((ts, LANES), lambda i: (i, j))
    out_specs = [pl.BlockSpec((d, ts // d, ATT_O), lambda i: (0, i, 0)) for _ in range(3) for d in DILS]
    out_shape = [jax.ShapeDtypeStruct((d, S // d, ATT_O), dt) for dt in (MXU, F32, F32) for d in DILS]
    res = pl.pallas_call(body, grid=(S // ts,), in_specs=[half(0), half(1), pl.BlockSpec((ts, ATT_O), lambda i: (i, 0)), half(0), half(1)],
                         out_specs=out_specs, out_shape=out_shape, scratch_shapes=[pltpu.VMEM((2, ts, LANES), F32)],
                         compiler_params=_params(("arbitrary",), 32), name="attn_bwd_prep")(datt, datt, o, lse, lse)
    return [[res[k * 3 + g].reshape(S, ATT_O) for g in range(3)] for k in range(3)]


def _head_col(x, h):
    return x[:, h * 64:h * 64 + 1]


def _attn_bwd(g, q, k, v, do, delta, lse):
    S = q.shape[0]
    nb = S // QB
    nblk = nb // DILS[g]

    def body(k_ref, v_ref, qc_ref, qn_ref, doc_ref, don_ref, dlc_ref, dln_ref, lc_ref, ln_ref, dq_ref, dk_ref, dv_ref, dq_scr):
        hms, hmk = _head_masks((2 * QB, ATT_O)), _head_masks((QB, ATT_O))
        first = pl.program_id(0) == 0

        @pl.when(first)
        def _():
            dq_scr[0:QB] = jnp.zeros((QB, ATT_O), F32)

        @pl.when(jnp.logical_not(first))
        def _():
            dq_scr[0:QB] = dq_scr[ASUB * QB:(ASUB + 1) * QB]

        dq_scr[QB:(ASUB + 1) * QB] = jnp.zeros((ASUB * QB, ATT_O), F32)

        def both(cur_ref, nxt_ref, j):
            return jnp.concatenate([_sub(cur_ref, j), nxt_ref[...] if j == ASUB - 1 else _sub(cur_ref, j + 1)], axis=0)

        for j in range(ASUB):
            ok = _band_mask_queries(((pl.program_id(0) * ASUB + j + 1) & (nblk - 1)) > 0)
            q2, do2, dl2, lse2 = both(qc_ref, qn_ref, j), both(doc_ref, don_ref, j), both(dlc_ref, dln_ref, j), both(lc_ref, ln_ref, j)
            kv, vv = _sub(k_ref, j), _sub(v_ref, j)
            dk = jnp.zeros((QB, ATT_O), F32)
            dv = jnp.zeros((QB, ATT_O), F32)
            dq2 = jnp.zeros((2 * QB, ATT_O), F32)
            for h, hm in enumerate(hms):
                qh, doh = _only(hm, q2), _only(hm, do2)
                p = jnp.where(ok, jnp.exp(_dot_nt(qh, kv) * ATT_SCALE - _head_col(lse2, h)), 0.0)
                ds = (p * (_dot_nt(doh, vv) - _head_col(dl2, h))).astype(MXU)
                dv = dv + _dot_tn(p.astype(MXU), doh)
                dk = dk + _dot_tn(ds, qh)
                dq2 = dq2 + _dot(ds, _only(hmk[h], kv))
            dk_ref[j * QB:(j + 1) * QB] = dk * ATT_SCALE
            dv_ref[j * QB:(j + 1) * QB] = dv
            dq_scr[j * QB:(j + 2) * QB] += dq2
        dq_ref[...] = dq_scr[0:ASUB * QB] * ATT_SCALE

    nxt = pl.BlockSpec((QB, ATT_O), lambda b: (jnp.minimum((b + 1) * ASUB, nb - 1), 0))
    shp = jax.ShapeDtypeStruct((S, ATT_O), F32)
    return pl.pallas_call(body, grid=(nb // ASUB,), in_specs=[_BIG, _BIG, _BIG, nxt, _BIG, nxt, _BIG, nxt, _BIG, nxt], out_specs=[_BIG] * 3,
                          out_shape=[shp, shp, shp], scratch_shapes=[pltpu.VMEM(((ASUB + 1) * QB, ATT_O), F32)],
                          compiler_params=_params(("arbitrary",), 32), name=f"attn_bwd_{g}")(k, v, q, q, do, do, delta, delta, lse, lse)


def _merge_fwd(x0, u, a2, yb, att, wa, wb, wc, w_out, g_post, ts=256):
    S = x0.shape[0]

    def body(i, g, x_ref, gate_ref, a2_ref, yb_ref, att_ref, wa_ref, wb_ref, wc_ref, wo_ref, gp_ref, mg_ref, y_ref, xo_ref):
        gate = lambda n: jax.nn.sigmoid(gate_ref[:, n * D:(n + 1) * D].astype(F32))
        merged = gate(0) * _dot_nt(a2_ref[...], wa_ref[...])
        merged = merged + gate(1) * _dot_nt(yb_ref[...], wb_ref[...])
        merged = merged + gate(2) * _dot_nt(att_ref[...], wc_ref[...])
        mb = merged.astype(MXU)
        mg_ref[...] = mb
        y = _dot(mb, wo_ref[...])
        y_ref[...] = y
        xo_ref[...] = x_ref[...] + _rms(y, gp_ref[...])[0]

    ins = [("t", x0, D, 0), ("t", u, GATE_W, 0), ("t", a2, POOLW, 0), ("t", yb, POOLW, 0), ("t", att, ATT_O, 0),
           ("w", wa), ("w", wb), ("w", wc), ("w", w_out), ("w", g_post)]
    return _rows_call("merge_fwd", body, S, ts, ins, [("t", D, MXU), ("t", D, F32), ("t", D, F32)])


def _merge_bwd(dx, y1, u, a2, yb, att, merged, wa, wb, wc, w_out, g_post, ts=256):
    S = dx.shape[0]
    last = S // ts - 1

    def body(i, g, dx_ref, y_ref, gate_ref, a2_ref, yb_ref, att_ref, mg_ref, wa_ref, wb_ref, wc_ref, wo_ref, gp_ref,
             dgate_ref, da2_ref, dyb_ref, datt_ref, dgp_ref, dwo_ref, dwa_ref, dwb_ref, dwc_ref, acc_o, acc_a, acc_b, acc_c):
        @pl.when(g == 0)
        def _():
            for acc in (acc_o, acc_a, acc_b, acc_c):
                acc[...] = jnp.zeros_like(acc)

        dxv, y = dx_ref[...], y_ref[...]
        dy, r = _rms_bwd(dxv * gp_ref[...], y)
        _acc(dgp_ref, g, jnp.sum(dxv * (y * r), axis=0, keepdims=True))
        dyb16 = dy.astype(MXU)
        acc_o[...] += _dot_tn(mg_ref[...], dyb16)
        dm = _dot_nt(dyb16, wo_ref[...])
        for n, (src, w_ref, din_ref, acc) in enumerate(((a2_ref, wa_ref, da2_ref, acc_a), (yb_ref, wb_ref, dyb_ref, acc_b),
                                                       (att_ref, wc_ref, datt_ref, acc_c))):
            gt = jax.nn.sigmoid(gate_ref[:, n * D:(n + 1) * D].astype(F32))
            br = _dot_nt(src[...], w_ref[...])
            dgate_ref[:, n * D:(n + 1) * D] = (dm * br * gt * (1.0 - gt)).astype(dgate_ref.dtype)
            dbr = (dm * gt).astype(MXU)
            acc[...] += _dot_tn(dbr, src[...])
            din_ref[...] = _dot(dbr, w_ref[...])

        @pl.when(g == last)
        def _():
            for out, acc in ((dwo_ref, acc_o), (dwa_ref, acc_a), (dwb_ref, acc_b), (dwc_ref, acc_c)):
                out[...] = acc[...].astype(MXU)

    ins = [("t", dx, D, 0), ("t", y1, D, 0), ("t", u, GATE_W, 0), ("t", a2, POOLW, 0), ("t", yb, POOLW, 0), ("t", att, ATT_O, 0),
           ("t", merged, D, 0), ("w", wa), ("w", wb), ("w", wc), ("w", w_out), ("w", g_post)]
    wshapes = [(D, D), (D, POOLW), (D, POOLW), (D, ATT_O)]
    outs = [("c", IN_W, GATE_W, 0, MXU), ("t", POOLW, F32), ("t", POOLW, F32), ("t", ATT_O, F32), ("a", (1, D), F32)]
    outs += [("a", s, MXU) for s in wshapes]
    return _rows_call("merge_bwd", body, S, ts, ins, outs, scratch=[pltpu.VMEM(s, F32) for s in wshapes])


def _prenorm_bwd(name, dx_res, du, wt, x, g_pre, ts=256, lead=0):
    S = x.shape[0]
    N = du.shape[1]

    def body(i, g, dx_ref, du_ref, wt_ref, x_ref, g_ref, o_ref, dg_ref):
        if lead:
            dhv = _dot(du_ref[:, 0:lead], wt_ref[N - lead:N, :]) + _dot(du_ref[:, lead:N], wt_ref[0:N - lead, :])
        else:
            dhv = _dot(du_ref[...], wt_ref[...])
        xv = x_ref[...]
        dxn, r = _rms_bwd(dhv * g_ref[...], xv)
        o_ref[...] = dx_ref[...] + dxn
        _acc(dg_ref, g, jnp.sum(dhv * (xv * r), axis=0, keepdims=True))

    ins = [("t", dx_res, D, 0), ("t", du, N, 0), ("w", wt), ("t", x, D, 0), ("w", g_pre)]
    return _rows_call(name, body, S, ts, ins, [("t", D, F32), ("a", (1, D), F32)], vmem_mb=52)


def _mem_heads(qm, kv_ref):
    out = []
    for h in range(4):
        q = qm[:, h * 128:(h + 1) * 128].astype(MXU)
        k = kv_ref[:, h * 128:(h + 1) * 128]
        v = kv_ref[:, MEM_W + h * 128:MEM_W + (h + 1) * 128]
        sc = _dot_nt(q, k) * MEM_SCALE
        e = jnp.exp(sc - jnp.max(sc, axis=1, keepdims=True))
        out.append((e / jnp.sum(e, axis=1, keepdims=True), q, k, v))
    return out


def _mem_fwd(x1, kv, g_pre, w_mq, w_mo, g_post, ts=256):
    S = x1.shape[0]

    def body(i, g, x_ref, kv_ref, gq_ref, wq_ref, wo_ref, gp_ref, om_ref, y_ref, xo_ref):
        x = x_ref[...]
        hb = _rms(x, gq_ref[...])[0].astype(MXU)
        qm = _dot(hb, wq_ref[...])
        om = jnp.concatenate([_dot(p.astype(MXU), v) for p, _, _, v in _mem_heads(qm, kv_ref)], axis=1).astype(MXU)
        om_ref[...] = om
        y = _dot_nt(om, wo_ref[...])
        y_ref[...] = y
        xo_ref[...] = x + _rms(y, gp_ref[...])[0]

    ins = [("t", x1, D, 0), ("w", kv), ("w", g_pre), ("w", w_mq), ("w", w_mo), ("w", g_post)]
    return _rows_call("mem_fwd", body, S, ts, ins, [("t", MEM_W, MXU), ("t", D, F32), ("t", D, F32)])


def _mem_bwd(dx2, ym, x1, om, kv, g_pre, w_mq, w_mo, g_post, ts=256):
    S = x1.shape[0]
    last = S // ts - 1

    def body(i, g, dx_ref, y_ref, x_ref, om_ref, kv_ref, gq_ref, wq_ref, wo_ref, gp_ref, dxo_ref, dgp_ref, dgq_ref, dkv_ref,
             dwo_ref, dwq_ref, acc_o, acc_q):
        dxv, y, x = dx_ref[...], y_ref[...], x_ref[...]
        dy, r = _rms_bwd(dxv * gp_ref[...], y)
        _acc(dgp_ref, g, jnp.sum(dxv * (y * r), axis=0, keepdims=True))
        dyb = dy.astype(MXU)
        dom = _dot(dyb, wo_ref[...])
        h, r1 = _rms(x, gq_ref[...])
        hb = h.astype(MXU)
        qm = _dot(hb, wq_ref[...])
        dqs = []

        @pl.when(g == 0)
        def _():
            dkv_ref[...] = jnp.zeros_like(dkv_ref)
            acc_o[...] = jnp.zeros_like(acc_o)
            acc_q[...] = jnp.zeros_like(acc_q)

        acc_o[...] += _dot_tn(dyb, om_ref[...])

        for hh, (p, q, k, v) in enumerate(_mem_heads(qm, kv_ref)):
            doh = dom[:, hh * 128:(hh + 1) * 128].astype(MXU)
            dp = _dot_nt(doh, v)
            dsc = (p * (dp - jnp.sum(dp * p, axis=1, keepdims=True)) * MEM_SCALE).astype(MXU)
            dqs.append(_dot(dsc, k))
            dkv_ref[:, hh * 128:(hh + 1) * 128] += _dot_tn(dsc, q)
            dkv_ref[:, MEM_W + hh * 128:MEM_W + (hh + 1) * 128] += _dot_tn(p.astype(MXU), doh)
        dq = jnp.concatenate(dqs, axis=1).astype(MXU)
        acc_q[...] += _dot_tn(hb, dq)
        dh = _dot_nt(dq, wq_ref[...])
        _acc(dgq_ref, g, jnp.sum(dh * (x * r1), axis=0, keepdims=True))
        dxo_ref[...] = dxv + _rms_bwd(dh * gq_ref[...], x)[0]

        @pl.when(g == last)
        def _():
            dwo_ref[...] = acc_o[...].astype(MXU)
            dwq_ref[...] = acc_q[...].astype(MXU)

    ins = [("t", dx2, D, 0), ("t", ym, D, 0), ("t", x1, D, 0), ("t", om, MEM_W, 0), ("w", kv), ("w", g_pre), ("w", w_mq), ("w", w_mo),
           ("w", g_post)]
    outs = [("t", D, F32), ("a", (1, D), F32), ("a", (1, D), F32), ("a", (256, D), F32), ("a", (D, MEM_W), MXU), ("a", (D, MEM_W), MXU)]
    return _rows_call("mem_bwd", body, S, ts, ins, outs, scratch=[pltpu.VMEM((D, MEM_W), F32), pltpu.VMEM((D, MEM_W), F32)])


def _gain_grad(name, dn, x):
    n = x.shape[0]

    def body(i, g, dn_ref, x_ref, o_ref):
        xv = x_ref[...]
        r = lax.rsqrt(jnp.mean(xv * xv, axis=-1, keepdims=True) + EPS)
        o_ref[...] = jnp.sum(dn_ref[...] * (xv * r), axis=0, keepdims=True)

    return _rows_call(name, body, n, n, [("t", dn, D, 0), ("t", x, D, 0)], [("a", (1, D), F32)])[0]


def _ffn_fwd(x2, u3, conv_f, w_down, g_post, ts=256):
    S = x2.shape[0]

    def body(i, g, x_ref, ua_ref, ub_ref, cw_ref, wd_ref, gp_ref, act_ref, y_ref, xo_ref, c_ref, cu):
        @pl.when(g == 0)
        def _():
            cu[...] = jnp.zeros_like(cu)

        ua = ua_ref[...].astype(F32)
        c, _, _ = _conv3(ua, cu[...], cw_ref[...])
        c_ref[...] = c.astype(MXU)
        act = (c * jax.nn.sigmoid(c) * ub_ref[...].astype(F32)).astype(MXU)
        act_ref[...] = act
        y = _dot(act, wd_ref[...])
        y_ref[...] = y
        xo_ref[...] = x_ref[...] + _rms(y, gp_ref[...])[0]
        cu[...] = ua[ts - 8:]

    ins = [("t", x2, D, 0), ("t", u3, D_FF, 0), ("t", u3, D_FF, 1), ("w", conv_f), ("w", w_down), ("w", g_post)]
    return _rows_call("ffn_fwd", body, S, ts, ins, [("t", D_FF, MXU), ("t", D, F32), ("t", D, F32), ("t", D_FF, MXU)],
                      scratch=[pltpu.VMEM((8, D_FF), F32)], vmem_mb=56)


def _ffn_bwd(dx3, y3, u3, c, conv_f, w_down, g_post, ts=128):
    S = dx3.shape[0]

    def body(i, g, dx_ref, y_ref, ua_ref, ub_ref, c_ref, cw_ref, wd_ref, gp_ref, dy_ref, du_ref, dgp_ref, dcw_ref, cdc):
        @pl.when(g == 0)
        def _():
            cdc[...] = jnp.zeros_like(cdc)

        dxv, y = dx_ref[...], y_ref[...]
        dy, r = _rms_bwd(dxv * gp_ref[...], y)
        _acc(dgp_ref, g, jnp.sum(dxv * (y * r), axis=0, keepdims=True))
        dyb = dy.astype(MXU)
        dy_ref[...] = dyb
        dact = _dot_nt(dyb, wd_ref[...])
        ua, c, w = ua_ref[...].astype(F32), c_ref[...].astype(F32), cw_ref[...]
        sg = jax.nn.sigmoid(c)
        du_ref[:, D_FF:2 * D_FF] = (dact * (c * sg)).astype(du_ref.dtype)
        dc = dact * ub_ref[...].astype(F32) * (sg * (1.0 + c * (1.0 - sg)))
        dua, dc1, dc2 = _conv3_t(dc, cdc[...], w, shifted=True)
        du_ref[:, 0:D_FF] = dua.astype(du_ref.dtype)
        dw = jnp.concatenate([jnp.sum(ua * dc2, axis=0, keepdims=True), jnp.sum(ua * dc1, axis=0, keepdims=True),
                              jnp.sum(ua * dc, axis=0, keepdims=True)], axis=0)
        _acc(dcw_ref, g, dw)
        cdc[...] = dc[:8]

    ins = [("t", dx3, D, 0), ("t", y3, D, 0), ("t", u3, D_FF, 0), ("t", u3, D_FF, 1), ("t", c, D_FF, 0), ("w", conv_f),
           ("w", w_down), ("w", g_post)]
    outs = [("t", D, MXU), ("t", 2 * D_FF, MXU), ("a", (1, D), F32), ("a", (3, D_FF), F32)]
    return _rows_call("ffn_bwd", body, S, ts, ins, outs, scratch=[pltpu.VMEM((8, D_FF), F32)], reverse=True, vmem_mb=56)


def _loss_head(x, target, ts=512):
    S = x.shape[0]

    def body(i, g, x_ref, t_ref, dx_ref, acc_ref):
        diff = x_ref[...] - t_ref[...]
        dx_ref[...] = diff * (1.0 / D)
        col = jnp.sum(diff * diff, axis=0, keepdims=True)
        part = col[:, 0:LANES]
        for j in range(1, D // LANES):
            part = part + col[:, j * LANES:(j + 1) * LANES]
        row = lax.broadcasted_iota(jnp.int32, (8, LANES), 0)
        _acc(acc_ref, g, jnp.where(row == 0, jnp.broadcast_to(part, (8, LANES)), 0.0))

    return _rows_call("loss_head", body, S, ts, [("t", x, D, 0), ("t", target, D, 0)], [("t", D, F32), ("a", (8, LANES), F32)])


_OPERAND_NAME = dict(w_in='w_in', w_branch_a='wa', w_branch_b='wb', w_branch_c='wc', w_out='w_out', w_mq='w_mq', w_mkv='w_mkv',
                     w_mo='w_mo', w_up='w_up', w_down='w_down')


def _big_operands(big):
    return {_OPERAND_NAME[n]: a for n, a in big.items()}


def _layer_weights(big, small, l):
    pool_w = small['pool_w'][l].astype(MXU)
    wblk = jnp.zeros((POOLW, POOLW), MXU)
    for g in range(4):
        wblk = lax.dynamic_update_slice(wblk, pool_w[g], (g * 96, g * 96))
    vec = lambda n: small[n][l].reshape(1, -1)
    return dict(
        _big_operands(big),
        wblk=wblk, pool_scale=vec('pool_scale'), conv_b=small['conv_b_w'][l], conv_f=small['conv_ffn_w'][l],
        g_mix_pre=vec('norm_mix_pre'), g_mix_post=vec('norm_mix_post'), g_mem_pre=vec('norm_mem_pre'),
        g_mem_post=vec('norm_mem_post'), g_memkv=vec('norm_memkv'), g_ffn_pre=vec('norm_ffn_pre'), g_ffn_post=vec('norm_ffn_post'))


def _layer_fwd(x0, mem, W, ctab, stab):
    sv = _layer_fwd_mix(x0, W, ctab, stab)
    return _layer_fwd_late(mem, W, sv), sv


def _layer_fwd_mix(x0, W, ctab, stab):
    return _layer_fwd_merge(W, _layer_fwd_branches(x0, W, ctab, stab))


def _layer_fwd_branches(x0, W, ctab, stab):
    sv = dict(x0=x0)
    sv['u'], sv['h1'] = _norm_mm("in_proj", x0, W['g_mix_pre'], W['w_in'], ts=2048, tn=IN_TILE, wt=True, rot=IN_ROT, out_dtype=MXU)
    sv['a2'], sv['yb'] = _poolconv_fwd(sv['u'], W['wblk'], W['pool_scale'], W['conv_b'])
    sv['qkv'] = q3, k3, v3 = _rope_perm(sv['u'], ctab, stab)
    sv['att'], sv['o'], sv['lse'] = _attn_combine([_attn_fwd(g, q3[g], k3[g], v3[g]) for g in range(3)])
    return sv


def _layer_fwd_merge(W, sv):
    sv['merged'], sv['y1'], sv['x1'] = _merge_fwd(sv['x0'], sv['u'], sv['a2'], sv['yb'], sv['att'], W['wa'], W['wb'], W['wc'],
                                                  W['w_out'], W['g_mix_post'])
    return sv


def _layer_fwd_late(mem, W, sv):
    sv['kv'], sv['memn'] = _norm_mm("mem_kv", mem, W['g_memkv'], W['w_mkv'], ts=256, tn=D, out_dtype=MXU)
    sv['om'], sv['ym'], sv['x2'] = _mem_fwd(sv['x1'], sv['kv'], W['g_mem_pre'], W['w_mq'], W['w_mo'], W['g_mem_post'])
    sv['u3'], sv['h3'] = _norm_mm("up_proj", sv['x2'], W['g_ffn_pre'], W['w_up'], ts=2048, tn=1408, wt=True, out_dtype=MXU)
    sv['act'], sv['y3'], x3, sv['c3'] = _ffn_fwd(sv['x2'], sv['u3'], W['conv_f'], W['w_down'], W['g_ffn_post'])
    return x3


def _layer_bwd(dx3, mem, W, sv, ctab, stab):
    dx1, g = _layer_bwd_late(dx3, mem, W, sv)
    dx0, g_mix = _layer_bwd_mix(dx1, W, sv, ctab, stab)
    return dx0, {**g, **g_mix}


def _layer_bwd_late(dx3, mem, W, sv):
    g = {}
    dy3, du3, g['norm_ffn_post'], g['conv_ffn_w'] = _ffn_bwd(dx3, sv['y3'], sv['u3'], sv['c3'], W['conv_f'], W['w_down'], W['g_ffn_post'])
    g['w_down'] = _mm_tn("dw_down", sv['act'], dy3, cap_k=256)
    g['w_up'] = _mm_tn("dw_up", du3, sv['h3'])
    dx2, g['norm_ffn_pre'] = _prenorm_bwd("ffn_pre_bwd", dx3, du3, W['w_up'], sv['x2'], W['g_ffn_pre'])
    dx1, g['norm_mem_post'], g['norm_mem_pre'], dkv, g['w_mo'], g['w_mq'] = _mem_bwd(
        dx2, sv['ym'], sv['x1'], sv['om'], sv['kv'], W['g_mem_pre'], W['w_mq'], W['w_mo'], W['g_mem_post'])
    dkvb = dkv.astype(MXU)
    g['w_mkv'] = _mm_tn("dw_mkv", sv['memn'], dkvb)
    g['norm_memkv'] = _gain_grad("memkv_gain", _mm_nt("d_memn", dkvb, W['w_mkv'], ts=256, tn=512), mem)
    return dx1, g


def _layer_bwd_mix(dx1, W, sv, ctab, stab):
    du, g = _layer_bwd_mixers(dx1, W, sv, ctab, stab)
    g['w_in'] = _dw_in(du, sv)
    dx0, g['norm_mix_pre'] = _mix_pre_bwd(dx1, du, W, sv)
    return dx0, g


def _dw_in(du, sv):
    return _mm_tn("dw_in", du, sv['h1'], cap_k=IN_TILE, rot=IN_ROT)


def _mix_pre_bwd(dx1, du, W, sv):
    return _prenorm_bwd("mix_pre_bwd", dx1, du, W['w_in'], sv['x0'], W['g_mix_pre'], lead=GATE_W)


def _layer_bwd_mixers(dx1, W, sv, ctab, stab):
    parts, g = _layer_bwd_merge(dx1, W, sv)
    du, g_br = _layer_bwd_branches(parts, W, sv, ctab, stab)
    return du, {**g, **g_br}


def _layer_bwd_merge(dx1, W, sv):
    g = {}
    du, da2, dyb, datt, g['norm_mix_post'], g['w_out'], g['w_branch_a'], g['w_branch_b'], g['w_branch_c'] = _merge_bwd(
        dx1, sv['y1'], sv['u'], sv['a2'], sv['yb'], sv['att'], sv['merged'], W['wa'], W['wb'], W['wc'], W['w_out'], W['g_mix_post'])
    return (du, da2, dyb, datt), g


def _layer_bwd_branches(parts, W, sv, ctab, stab):
    du, da2, dyb, datt = parts
    g = {}
    du, g['pool_scale'], dwblk, g['conv_b_w'] = _poolconv_bwd(sv['u'], da2, dyb, du, W['wblk'], W['pool_scale'], W['conv_b'])
    g['pool_w'] = jnp.stack([dwblk[k * 96:(k + 1) * 96, k * 96:(k + 1) * 96] for k in range(4)])
    q3, k3, v3 = sv['qkv']
    do3, dl3, lse3 = _attn_bwd_prep(datt, sv['o'], sv['lse'])
    dqkv3 = [_attn_bwd(i, q3[i], k3[i], v3[i], do3[i], dl3[i], lse3[i]) for i in range(3)]
    du = _rope_unperm_bwd([[t[which] for t in dqkv3] for which in range(3)], du, ctab, stab)
    return du, g


def _local_step(x, mem, positions, target, big, small):
    ctab, stab = _rope_tables(positions)
    Ws = [_layer_weights(big[l], small, l) for l in range(DEPTH)]
    saved = []
    for l in range(DEPTH):
        x, sv = _layer_fwd(x, mem, Ws[l], ctab, stab)
        saved.append(sv)
    dx, acc = _loss_head(x, target)
    loss = jnp.sum(acc) * (0.5 / D)
    grads = [None] * DEPTH
    for l in reversed(range(DEPTH)):
        dx, grads[l] = _layer_bwd(dx, mem, Ws[l], saved[l], ctab, stab)
    return loss, dx, grads


_HBM = pl.BlockSpec(memory_space=pl.ANY)
MESH_ID = pl.DeviceIdType.MESH


def _all_gather(name, xs):
    n = len(xs)

    def body(*refs):
        x_refs, out_refs = refs[:n], refs[n:2 * n]
        send_sems, recv_sems, local_sems = refs[2 * n:]
        x, y, c = lax.axis_index("x"), lax.axis_index("y"), lax.axis_index("c")
        me, sibling = (x, y, c), (x, y, 1 - c)
        chips = [(1 - x, y), (x, 1 - y), (1 - x, 1 - y)]

        def slot(a, p):
            return out_refs[a].at[4 * p[0] + 2 * p[1] + p[2]]

        def copy(a, k, block, to, src=None):
            return pltpu.make_async_remote_copy(src_ref=slot(a, block) if src is None else src, dst_ref=slot(a, block),
                                                send_sem=send_sems.at[a, k], recv_sem=recv_sems.at[a, k], device_id=to,
                                                device_id_type=MESH_ID)

        started = []
        for a in range(n):
            mine = pltpu.make_async_copy(x_refs[a], slot(a, me), local_sems.at[a])
            mine.start()
            started.append(mine)
        first = []
        for a in range(n):
            first.append(copy(a, 0, me, sibling, src=x_refs[a]))
            first += [copy(a, 1 + j, me, (*chip, c), src=x_refs[a]) for j, chip in enumerate(chips)]
        for cp in first:
            cp.start()
        passed = []
        for j, chip in enumerate(chips):
            for a in range(n):
                copy(a, 1 + j, (*chip, c), me).wait_recv()
                fw = copy(a, 4 + j, (*chip, c), sibling)
                fw.start()
                passed.append(fw)
        for a in range(n):
            copy(a, 0, sibling, me).wait_recv()
            for j, chip in enumerate(chips):
                copy(a, 4 + j, (*chip, 1 - c), me).wait_recv()
        for cp in first + passed:
            cp.wait_send()
        for mine in started:
            mine.wait()

    return pl.pallas_call(
        body, out_shape=[jax.ShapeDtypeStruct((N_DEV,) + x.shape, x.dtype) for x in xs], in_specs=[_HBM] * n, out_specs=[_HBM] * n,
        scratch_shapes=[pltpu.SemaphoreType.DMA((n, 7)), pltpu.SemaphoreType.DMA((n, 7)), pltpu.SemaphoreType.DMA((n,))],
        name=name)(*xs)


_SEM =pl.BlockSpec(memory_space=pltpu.SEMAPHORE)
_IN_HBM = pl.BlockSpec(memory_space=pltpu.HBM)
_SIDE_EFFECT = pltpu.SideEffectType.DATAFLOW_SIDE_EFFECTING


def _push_copies(src_refs, land_refs, send_sems, recv_sems, per_peer):
    x, y, c = lax.axis_index("x"), lax.axis_index("y"), lax.axis_index("c")
    me = 4 * x + 2 * y + c
    copies = []
    for r in range(1, N_DEV):
        px, py, pc = x ^ ((r >> 2) & 1), y ^ ((r >> 1) & 1), c ^ (r & 1)
        for a, (s, d) in enumerate(zip(src_refs, land_refs)):
            k = a * (N_DEV - 1) + r - 1
            copies.append(pltpu.make_async_remote_copy(src_ref=s.at[4 * px + 2 * py + pc] if per_peer else s, dst_ref=d.at[me],
                                                       send_sem=send_sems.at[k], recv_sem=recv_sems.at[k],
                                                       device_id=(px, py, pc), device_id_type=MESH_ID))
    return copies


def _push_start(name, srcs, per_peer, after):
    n = len(srcs)
    lands = [lax.empty((N_DEV,) + (s.shape[1:] if per_peer else s.shape), s.dtype) for s in srcs]

    def body(*refs):
        for cp in _push_copies(refs[:n], refs[n:2 * n], refs[2 * n + 1], refs[2 * n + 2], per_peer):
            cp.start()
        refs[-1][...] = jnp.zeros_like(refs[-1])

    hbm = [pltpu.HBM(a.shape, a.dtype) for a in (*srcs, *lands)]
    sems = pltpu.SemaphoreType.DMA((n * (N_DEV - 1),))
    out = pl.pallas_call(
        body, name=name, out_shape=(sems, sems, *hbm, jax.ShapeDtypeStruct((8, LANES), F32)),
        in_specs=[_IN_HBM] * (2 * n) + [pl.BlockSpec(memory_space=pl.ANY)],
        out_specs=(_SEM, _SEM, *[_IN_HBM] * (2 * n), pl.BlockSpec(memory_space=pltpu.VMEM)),
        input_output_aliases={a: 2 + a for a in range(2 * n)},
        compiler_params=pltpu.CompilerParams(has_side_effects=_SIDE_EFFECT),
    )(*[pltpu.with_memory_space_constraint(a, pltpu.HBM) for a in (*srcs, *lands)], after)
    return out[0], out[1], out[2:2 + n], out[2 + n:2 + 2 * n], out[-1]


def _push_wait(name, started, per_peer, after):
    send_sems, recv_sems, srcs, lands, _ = started
    n = len(srcs)

    def body(*refs):
        for cp in _push_copies(refs[:n], refs[n:2 * n], refs[2 * n], refs[2 * n + 1], per_peer):
            cp.wait_send()
            cp.wait_recv()

    out = pl.pallas_call(
        body, name=name, out_shape=[pltpu.HBM(a.shape, a.dtype) for a in (*srcs, *lands)],
        in_specs=[_IN_HBM] * (2 * n) + [_SEM, _SEM, pl.BlockSpec(memory_space=pl.ANY)], out_specs=[_IN_HBM] * (2 * n),
        input_output_aliases={a: a for a in range(2 * n)},
        compiler_params=pltpu.CompilerParams(has_side_effects=_SIDE_EFFECT),
    )(*srcs, *lands, send_sems, recv_sems, after)
    if per_peer:
        return list(zip(out[:n], out[n:]))
    return _with_own(out[n:], out[:n], _my_slot())


def _my_slot():
    return 4 * lax.axis_index("x") + 2 * lax.axis_index("y") + lax.axis_index("c")


def _row_tile(rows, cols, budget):
    if rows * cols * 4 <= budget or rows % 16:
        return rows
    best = 16
    for t in range(16, rows + 1, 16):
        if rows % t == 0 and t * cols * 4 <= budget:
            best = t
    return best


def _sum_slots(name, pushed):
    src, recv = pushed
    _, R, C = recv.shape
    tr = _row_tile(R, C, 1 << 20)

    def body(r_ref, own_ref, o_ref):
        me = _my_slot()
        g = jnp.where(me == 0, own_ref[...], r_ref[0]).astype(F32)
        for k in range(1, N_DEV):
            g = g + jnp.where(me == k, own_ref[...], r_ref[k]).astype(F32)
        o_ref[...] = g

    return pl.pallas_call(body, grid=(R // tr,),
                          in_specs=[pl.BlockSpec((N_DEV, tr, C), lambda i: (0, i, 0)), pl.BlockSpec((None, tr, C), lambda i: (_my_slot(), i, 0))],
                          out_specs=pl.BlockSpec((tr, C), lambda i: (i, 0)), out_shape=jax.ShapeDtypeStruct((R, C), F32),
                          compiler_params=_params(("arbitrary",), 32), name=name)(recv, src)


def _adamw(name, g, w, m, v):
    shape = w.shape
    R, C = shape[-2], shape[-1]
    view = (-1, R, C)
    L = w.size // (R * C)
    tr = _row_tile(R, C, 1 << 20)
    c1 = 1.0 - ADAM_B1 ** ADAM_STEP
    c2 = 1.0 - ADAM_B2 ** ADAM_STEP

    def body(g_ref, w_ref, m_ref, v_ref, d_ref, mo_ref, vo_ref):
        gv = g_ref[...]
        mn = ADAM_B1 * m_ref[...] + (1.0 - ADAM_B1) * gv
        vn = ADAM_B2 * v_ref[...] + (1.0 - ADAM_B2) * (gv * gv)
        mo_ref[...] = mn
        vo_ref[...] = vn
        d_ref[...] = -ADAM_LR * ((mn / c1) / (jnp.sqrt(vn / c2) + ADAM_EPS) + ADAM_WD * w_ref[...])

    blk = pl.BlockSpec((None, tr, C), lambda l, i: (l, i, 0))
    shp = jax.ShapeDtypeStruct((L, R, C), F32)
    outs = pl.pallas_call(body, grid=(L, R // tr), in_specs=[blk, blk, blk, blk], out_specs=[blk, blk, blk], out_shape=[shp, shp, shp],
                          compiler_params=_params(("arbitrary", "arbitrary"), 32), name=name)(*[a.reshape(view) for a in (g, w, m, v)])
    return [o.reshape(shape) for o in outs]


def _pad_flat(a, n):
    a = a.reshape(-1)
    return jnp.pad(a, (0, n - a.shape[0]))


def _seg(n):
    return -(-n // FLAT_ALIGN) * FLAT_ALIGN


def _to_blocks(full, axis):
    shp = full.shape
    return jnp.moveaxis(full.reshape(shp[:axis] + (N_DEV, shp[axis] // N_DEV) + shp[axis + 1:]), axis, 0)


def _from_blocks(blocks, axis):
    b = jnp.moveaxis(blocks, 0, axis)
    shp = b.shape
    return b.reshape(shp[:axis] + (shp[axis] * shp[axis + 1],) + shp[axis + 2:])


def _as_rows(shard, n):
    return shard.T if SHARD_AXIS[n] == 2 else shard


def _with_own(lands, own, me):
    return [lax.dynamic_update_slice(land, o[None], (me, 0, 0)) for land, o in zip(lands, own)]


def kernel(x, mem, positions, norm_mix_pre, norm_mix_post, w_in, pool_w, pool_scale, conv_b_w, w_branch_a, w_branch_b, w_branch_c, w_out, norm_mem_pre, norm_mem_post, norm_memkv, w_mq, w_mkv, w_mo, norm_ffn_pre, norm_ffn_post, w_up, conv_ffn_w, w_down, loss_target, m_norm_mix_pre, m_norm_mix_post, m_w_in, m_pool_w, m_pool_scale, m_conv_b_w, m_w_branch_a, m_w_branch_b, m_w_branch_c, m_w_out, m_norm_mem_pre, m_norm_mem_post, m_norm_memkv, m_w_mq, m_w_mkv, m_w_mo, m_norm_ffn_pre, m_norm_ffn_post, m_w_up, m_conv_ffn_w, m_w_down, v_norm_mix_pre, v_norm_mix_post, v_w_in, v_pool_w, v_pool_scale, v_conv_b_w, v_w_branch_a, v_w_branch_b, v_w_branch_c, v_w_out, v_norm_mem_pre, v_norm_mem_post, v_norm_memkv, v_w_mq, v_w_mkv, v_w_mo, v_norm_ffn_pre, v_norm_ffn_post, v_w_up, v_conv_ffn_w, v_w_down):
    w = dict(norm_mix_pre=norm_mix_pre, norm_mix_post=norm_mix_post, w_in=w_in, pool_w=pool_w, pool_scale=pool_scale, conv_b_w=conv_b_w, w_branch_a=w_branch_a, w_branch_b=w_branch_b, w_branch_c=w_branch_c, w_out=w_out, norm_mem_pre=norm_mem_pre, norm_mem_post=norm_mem_post, norm_memkv=norm_memkv, w_mq=w_mq, w_mkv=w_mkv, w_mo=w_mo, norm_ffn_pre=norm_ffn_pre, norm_ffn_post=norm_ffn_post, w_up=w_up, conv_ffn_w=conv_ffn_w, w_down=w_down)
    m = dict(norm_mix_pre=m_norm_mix_pre, norm_mix_post=m_norm_mix_post, w_in=m_w_in, pool_w=m_pool_w, pool_scale=m_pool_scale, conv_b_w=m_conv_b_w, w_branch_a=m_w_branch_a, w_branch_b=m_w_branch_b, w_branch_c=m_w_branch_c, w_out=m_w_out, norm_mem_pre=m_norm_mem_pre, norm_mem_post=m_norm_mem_post, norm_memkv=m_norm_memkv, w_mq=m_w_mq, w_mkv=m_w_mkv, w_mo=m_w_mo, norm_ffn_pre=m_norm_ffn_pre, norm_ffn_post=m_norm_ffn_post, w_up=m_w_up, conv_ffn_w=m_conv_ffn_w, w_down=m_w_down)
    v = dict(norm_mix_pre=v_norm_mix_pre, norm_mix_post=v_norm_mix_post, w_in=v_w_in, pool_w=v_pool_w, pool_scale=v_pool_scale, conv_b_w=v_conv_b_w, w_branch_a=v_w_branch_a, w_branch_b=v_w_branch_b, w_branch_c=v_w_branch_c, w_out=v_w_out, norm_mem_pre=v_norm_mem_pre, norm_mem_post=v_norm_mem_post, norm_memkv=v_norm_memkv, w_mq=v_w_mq, w_mkv=v_w_mkv, w_mo=v_w_mo, norm_ffn_pre=v_norm_ffn_pre, norm_ffn_post=v_norm_ffn_post, w_up=v_w_up, conv_ffn_w=v_conv_ffn_w, w_down=v_w_down)

    me = _my_slot()
    mix_big = [n for n in BIG if n not in LATE_BIG]
    block = lambda names, l: [_as_rows(w[n][l], n).astype(MXU) for n in names]
    conv = jnp.concatenate([_pad_flat(w[n], _seg(w[n].size)) for n in F32_GATHERED]).reshape(-1, LANES)
    groups = dict(m=MERGE_BIG, b=LATE_BIG, a=mix_big)
    got0 = _all_gather("weights_all_gather_0", block(['w_in'], 0) + [conv])
    conv_all = got0[-1].reshape(N_DEV, -1)
    small, off = {n: w[n] for n in WEIGHTS if n not in SHARD_AXIS}, 0
    for n in F32_GATHERED:
        small[n] = _from_blocks(conv_all[:, off:off + w[n].size].reshape((N_DEV,) + w[n].shape), 2)
        off += _seg(w[n].size)
    whole = lambda names, got: {n: o.reshape(-1, o.shape[-1]) for n, o in zip(names, got)}
    pushes, after = {}, got0[0]
    for tag, l in (('m', 0), ('b', 0), ('a', 1), ('b', 1)):
        pushes[tag, l] = _push_start(f"weights_push_start_{l}{tag}", block(groups[tag], l), False, after)
        after = pushes[tag, l][4]

    def arrived(tag, l, done):
        return _big_operands(whole(groups[tag], _push_wait(f"weights_push_wait_{l}{tag}", pushes[tag, l], False, done)))

    ctab, stab = _rope_tables(positions[0])
    W0 = _layer_weights(whole(['w_in'], got0), small, 0)
    sv0 = _layer_fwd_branches(x[0], dict(W0, g_mix_pre=W0['g_mix_pre'] + after[0, 0]), ctab, stab)
    W0.update(arrived('m', 0, sv0['att']))
    sv0 = _layer_fwd_merge(W0, sv0)
    W0.update(arrived('b', 0, sv0['x1']))
    x1 = _layer_fwd_late(mem[0], W0, sv0)
    W1 = _layer_weights({}, small, 1)
    W1.update(arrived('a', 1, x1))
    sv1 = _layer_fwd_mix(x1, W1, ctab, stab)
    W1.update(arrived('b', 1, sv1['x1']))
    x2 = _layer_fwd_late(mem[0], W1, sv1)
    dx, acc = _loss_head(x2, loss_target[0])
    loss = lax.psum(jnp.sum(acc) * (0.5 / D), MESH_AXES)
    grads = [None] * DEPTH
    dx, grads[1] = _layer_bwd(dx, mem[0], W1, sv1, ctab, stab)
    sent = [None, [grads[1][n].reshape(N_DEV, -1, grads[1][n].shape[-1]) for n in BIG]]
    push_g = _push_start("grads_push_start_1", sent[1], True, dx)
    dx, g_late = _layer_bwd_late(dx, mem[0], dict(W0, g_ffn_post=W0['g_ffn_post'] + push_g[4][0, 0]), sv0)
    sent_late = [g_late[n].reshape(N_DEV, -1, g_late[n].shape[-1]) for n in LATE_BIG]
    push_l = _push_start("grads_push_start_0", sent_late, True, dx)
    parts, g_mix = _layer_bwd_merge(dx, dict(W0, g_mix_post=W0['g_mix_post'] + push_l[4][0, 0]), sv0)
    sent_merge = [g_mix[n].reshape(N_DEV, -1, g_mix[n].shape[-1]) for n in MERGE_BIG]
    push_m = _push_start("grads_push_start_0m", sent_merge, True, parts[0])
    du, g_br = _layer_bwd_branches(parts, dict(W0, pool_scale=W0['pool_scale'] + push_m[4][0, 0]), sv0, ctab, stab)
    g_mix.update(g_br)
    g_mix['w_in'] = _dw_in(du, sv0)
    sent_in = [g_mix['w_in'].reshape(N_DEV, -1, D)]
    push_i = _push_start("grads_push_start_in", sent_in, True, du)
    dx, g_mix['norm_mix_pre'] = _mix_pre_bwd(dx, du, dict(W0, g_mix_pre=W0['g_mix_pre'] + push_i[4][0, 0]), sv0)
    grads[0] = {**g_late, **g_mix}
    recv1 = _push_wait("grads_push_wait_1", push_g, True, dx)
    recv_late = _push_wait("grads_push_wait_0", push_l, True, dx)
    recv_merge = _push_wait("grads_push_wait_0m", push_m, True, dx)

    misc_names = [n for n in WEIGHTS if n not in BIG]
    stacked = {n: jnp.stack([grads[l][n].reshape(small[n].shape[1:]) for l in range(DEPTH)]) for n in misc_names}
    rows = [(_to_blocks(stacked[n], 2) if n in SHARD_AXIS else jnp.broadcast_to(stacked[n][None], (N_DEV,) + stacked[n].shape))
            for n in misc_names]
    segs = [_seg(w[n].size) for n in misc_names]
    misc = jnp.concatenate([jnp.pad(r.reshape(N_DEV, -1), ((0, 0), (0, s - r[0].size))) for r, s in zip(rows, segs)],
                           axis=1).reshape(N_DEV, -1, LANES)
    push_x = _push_start("grads_push_start_small", [misc], True, dx)
    g_out, per_layer = {}, {}
    for l, names, recv in ((1, BIG, recv1), (0, LATE_BIG, recv_late), (0, MERGE_BIG, recv_merge)):
        for n, r in zip(names, recv):
            per_layer[n, l] = _sum_slots(f"sum_{n}_{l}", r)

    swap = lambda a: jnp.swapaxes(a, 1, 2)

    def update(n):
        if n not in BIG:
            return [g_out[n], *_adamw(f"adamw_{n}", g_out[n], w[n], m[n], v[n])]
        g = jnp.stack([per_layer[n, l] for l in range(DEPTH)])
        if SHARD_AXIS[n] == 2 and w[n].shape[2] % LANES:
            return [swap(a) for a in (g, *_adamw(f"adamw_{n}", g, swap(w[n]), swap(m[n]), swap(v[n])))]
        g = swap(g) if SHARD_AXIS[n] == 2 else g
        return [g, *_adamw(f"adamw_{n}", g, w[n], m[n], v[n])]

    done = {n: update(n) for n in BIG if n != 'w_in'}
    recv_in = _push_wait("grads_push_wait_in", push_i, True, done[BIG[-1]][1])
    per_layer['w_in', 0] = _sum_slots("sum_w_in_0", recv_in[0])
    done['w_in'] = update('w_in')
    misc_sum = _sum_slots("sum_misc", _push_wait("grads_push_wait_small", push_x, True, done['w_in'][1])[0]).reshape(-1)
    off = 0
    for n, s in zip(misc_names, segs):
        g_out[n] = misc_sum[off:off + w[n].size].reshape(w[n].shape)
        done[n] = update(n)
        off += s
    return (loss, dx[None], *[done[n][k] for k in range(4) for n in WEIGHTS])
```

```python
import jax
import jax.numpy as jnp
from jax import lax
from jax.experimental import pallas as pl
from jax.experimental.pallas import tpu as pltpu

F32 = jnp.float32
MXU = jnp.bfloat16

D = 1024
DEPTH = 2
POOLW = 384
ATT_W = 768
ATT_O = 256
GATE_W = 3 * D
IN_W = 6912
IN_TILE = 768
IN_ROT = (IN_W - GATE_W) // IN_TILE
MEM_W = 512
D_FF = 2816
EPS = 1e-6
ROPE_THETA = 500000.0
QB = 128
DILS = (1, 4, 16)
NEG = -1e30
MEM_SCALE = 128 ** -0.5
ATT_SCALE = 0.125

ADAM_LR, ADAM_B1, ADAM_B2, ADAM_EPS, ADAM_WD, ADAM_STEP = 0.001, 0.9, 0.999, 1e-08, 0.01, 10

N_DEV = 8
MESH_AXES = ("x", "y", "c")
LANES = 128
FLAT_ALIGN = 2048

WEIGHTS = ['norm_mix_pre', 'norm_mix_post', 'w_in', 'pool_w', 'pool_scale', 'conv_b_w', 'w_branch_a', 'w_branch_b',
           'w_branch_c', 'w_out', 'norm_mem_pre', 'norm_mem_post', 'norm_memkv', 'w_mq', 'w_mkv', 'w_mo',
           'norm_ffn_pre', 'norm_ffn_post', 'w_up', 'conv_ffn_w', 'w_down']
SHARD_AXIS = {'w_in': 2, 'conv_b_w': 2, 'w_branch_a': 2, 'w_branch_b': 2, 'w_branch_c': 2, 'w_out': 1, 'w_mq': 1,
              'w_mkv': 1, 'w_mo': 2, 'w_up': 2, 'conv_ffn_w': 2, 'w_down': 1}
F32_GATHERED = ('conv_b_w', 'conv_ffn_w')
BIG = [n for n in WEIGHTS if n in SHARD_AXIS and n not in F32_GATHERED]
LATE_BIG = ['w_mq', 'w_mkv', 'w_mo', 'w_up', 'w_down']
MERGE_BIG = ['w_branch_a', 'w_branch_b', 'w_branch_c', 'w_out']


VMEM_LIMIT_MB = 60


def _params(sem):
    return pltpu.CompilerParams(dimension_semantics=sem, vmem_limit_bytes=VMEM_LIMIT_MB << 20)


def _dot(a, b, prec=None):
    return lax.dot_general(a, b, (((1,), (0,)), ((), ())), preferred_element_type=F32, precision=prec)


def _dot_nt(a, b, prec=None):
    return lax.dot_general(a, b, (((1,), (1,)), ((), ())), preferred_element_type=F32, precision=prec)


def _dot_tn(a, b, prec=None):
    return lax.dot_general(a, b, (((0,), (0,)), ((), ())), preferred_element_type=F32, precision=prec)


def _tile(n, cap):
    if n <= cap:
        return n
    best = None
    for t in range(LANES, cap + 1, LANES):
        if n % t == 0:
            best = t
    assert best is not None, (n, cap)
    return best


def _rms(x, g):
    r = lax.rsqrt(jnp.mean(x * x, axis=-1, keepdims=True) + EPS)
    return x * r * g, r


def _rms_bwd(w, y):
    r = lax.rsqrt(jnp.mean(y * y, axis=-1, keepdims=True) + EPS)
    return r * w - y * (r * r * r) * jnp.mean(w * y, axis=-1, keepdims=True), r


def _rows_call(name, body, n_rows, ts, ins, outs, scratch=(), reverse=False, aliases=None):
    nt = n_rows // ts
    assert nt * ts == n_rows

    def tile_of(g):
        return (nt - 1 - g) if reverse else g

    in_specs, args = [], []
    for op in ins:
        if op[0] == "t":
            _, a, cw, cb = op
            in_specs.append(pl.BlockSpec((ts, cw), lambda g, cb=cb: (tile_of(g), cb)))
        elif op[0] == "h":
            _, a, hr, cw, cb = op
            in_specs.append(pl.BlockSpec((hr, cw), lambda g, cb=cb, k=ts // hr: (jnp.maximum(tile_of(g) * k - 1, 0), cb)))
        elif op[0] == "x":
            _, a = op
            in_specs.append(pl.BlockSpec(memory_space=pl.ANY))
        else:
            _, a = op
            in_specs.append(pl.BlockSpec(a.shape, lambda g, n=a.ndim: (0,) * n))
        args.append(a)
    out_specs, out_shape = [], []
    for op in outs:
        if op[0] == "t":
            _, cols, dt = op
            out_specs.append(pl.BlockSpec((ts, cols), lambda g: (tile_of(g), 0)))
            out_shape.append(jax.ShapeDtypeStruct((n_rows, cols), dt))
        elif op[0] == "c":
            _, total, cols, cb, dt = op
            out_specs.append(pl.BlockSpec((ts, cols), lambda g, cb=cb: (tile_of(g), cb)))
            out_shape.append(jax.ShapeDtypeStruct((n_rows, total), dt))
        else:
            _, shp, dt = op
            out_specs.append(pl.BlockSpec(shp, lambda g, n=len(shp): (0,) * n))
            out_shape.append(jax.ShapeDtypeStruct(shp, dt))

    def kern(*refs):
        g = pl.program_id(0)
        body(tile_of(g), g, *refs)

    return pl.pallas_call(kern, grid=(nt,), in_specs=in_specs, out_specs=out_specs, out_shape=out_shape,
                          scratch_shapes=list(scratch), input_output_aliases=aliases or {},
                          compiler_params=_params(("arbitrary",)), name=name)(*args)


def _acc(ref, g, val):
    @pl.when(g == 0)
    def _():
        ref[...] = val

    @pl.when(g != 0)
    def _():
        ref[...] += val


def _norm_mm(name, x, g, w, ts, tn, out_dtype=F32, wt=False, rot=0):
    S, K = x.shape
    N = w.shape[0] if wt else w.shape[1]
    assert wt or not rot

    def body(x_ref, g_ref, w_ref, o_ref, h_ref, hs):
        @pl.when(pl.program_id(1) == 0)
        def _():
            h, _ = _rms(x_ref[...], g_ref[...])
            hs[...] = h.astype(MXU)
            h_ref[...] = h.astype(MXU)

        o_ref[...] = (_dot_nt if wt else _dot)(hs[...], w_ref[...]).astype(out_dtype)

    w_spec = pl.BlockSpec((tn, K), lambda i, j: ((j + rot) % (N // tn), 0)) if wt else pl.BlockSpec((K, tn), lambda i, j: (0, j))
    return pl.pallas_call(
        body, grid=(S // ts, N // tn),
        in_specs=[pl.BlockSpec((ts, K), lambda i, j: (i, 0)), pl.BlockSpec((1, K), lambda i, j: (0, 0)), w_spec],
        out_specs=[pl.BlockSpec((ts, tn), lambda i, j: (i, j)), pl.BlockSpec((ts, K), lambda i, j: (i, 0))],
        out_shape=[jax.ShapeDtypeStruct((S, N), out_dtype), jax.ShapeDtypeStruct((S, K), MXU)],
        scratch_shapes=[pltpu.VMEM((ts, K), MXU)],
        compiler_params=_params(("arbitrary", "arbitrary")), name=name)(x, g, w)


def _mm_nt(name, a, b, ts, tn, out_dtype=F32):
    M, K = a.shape
    N = b.shape[0]

    def body(a_ref, b_ref, o_ref):
        o_ref[...] = _dot_nt(a_ref[...], b_ref[...]).astype(out_dtype)

    return pl.pallas_call(
        body, grid=(M // ts, N // tn),
        in_specs=[pl.BlockSpec((ts, K), lambda i, j: (i, 0)), pl.BlockSpec((tn, K), lambda i, j: (j, 0))],
        out_specs=pl.BlockSpec((ts, tn), lambda i, j: (i, j)), out_shape=jax.ShapeDtypeStruct((M, N), out_dtype),
        compiler_params=_params(("arbitrary", "arbitrary")), name=name)(a, b)


def _mm_tn(name, a, b, cap_k=512, cap_n=1024, out_dtype=MXU, rot=0):
    S, K = a.shape
    N = b.shape[1]
    tk, tn = _tile(K, cap_k), _tile(N, cap_n)

    def body(a_ref, b_ref, o_ref):
        o_ref[...] = _dot_tn(a_ref[...], b_ref[...]).astype(out_dtype)

    return pl.pallas_call(
        body, grid=(K // tk, N // tn),
        in_specs=[pl.BlockSpec((S, tk), lambda i, j: (0, i)), pl.BlockSpec((S, tn), lambda i, j: (0, j))],
        out_specs=pl.BlockSpec((tk, tn), lambda i, j: ((i + rot) % (K // tk), j)), out_shape=jax.ShapeDtypeStruct((K, N), out_dtype),
        compiler_params=_params(("arbitrary", "arbitrary")), name=name)(a, b)


def _pool_cols(shape):
    col = lax.broadcasted_iota(jnp.int32, shape, 1)
    return col < 96, col < 192, col < 288


def _pool_select(s2, s4, s8, s16):
    c1, c2, c3 = _pool_cols(s2.shape)
    return jnp.where(c1, s2, jnp.where(c2, s4, jnp.where(c3, s8, s16)))


def _pool_cnt(t0, ts):
    c1, c2, c3 = _pool_cols((ts, POOLW))
    win = jnp.where(c1, 2, jnp.where(c2, 4, jnp.where(c3, 8, 16)))
    t = t0 + lax.broadcasted_iota(jnp.int32, (ts, POOLW), 0)
    return jnp.minimum(t + 1, win).astype(F32)


def _pooled(a, prev, t0):
    ts = a.shape[0]
    ext = jnp.concatenate([prev, a], axis=0)
    s2 = ext + pltpu.roll(ext, 1, axis=0)
    s4 = s2 + pltpu.roll(s2, 2, axis=0)
    s8 = s4 + pltpu.roll(s4, 4, axis=0)
    s16 = s8 + pltpu.roll(s8, 8, axis=0)
    sums = _pool_select(s2, s4, s8, s16)[16:]
    return sums / _pool_cnt(t0, ts) - a


def _conv3(z, prev8, w):
    ext = jnp.concatenate([prev8, z], axis=0)
    z1 = pltpu.roll(ext, 1, axis=0)[8:]
    z2 = pltpu.roll(ext, 2, axis=0)[8:]
    return w[0:1] * z2 + w[1:2] * z1 + w[2:3] * z, z1, z2


def _conv3_t(dc, next8, w, shifted=False):
    ts = dc.shape[0]
    ext = jnp.concatenate([dc, next8], axis=0)
    n = ts + 8
    u1 = pltpu.roll(ext, n - 1, axis=0)[:ts]
    u2 = pltpu.roll(ext, n - 2, axis=0)[:ts]
    out = w[2:3] * dc + w[1:2] * u1 + w[0:1] * u2
    return (out, u1, u2) if shifted else out


def _poolconv_fwd(u, wblk, pool_scale, conv_b, ts=512):
    S = u.shape[0]

    def body(i, g, a_ref, bx_ref, bb_ref, bc_ref, wblk_ref, ps_ref, cw_ref, a2_ref, yb_ref, ca, cz):
        @pl.when(g == 0)
        def _():
            ca[...] = jnp.zeros_like(ca)
            cz[...] = jnp.zeros_like(cz)

        a = a_ref[...].astype(F32)
        p = _pooled(a, ca[...], i * ts)
        mixed = _dot(p.astype(MXU), wblk_ref[...])
        a2_ref[...] = (mixed * ps_ref[...]).astype(MXU)
        z = bc_ref[...].astype(F32) * bx_ref[...].astype(F32)
        conv, _, _ = _conv3(z, cz[...], cw_ref[...])
        yb_ref[...] = (bb_ref[...].astype(F32) * conv).astype(MXU)
        ca[...] = a[ts - 16:]
        cz[...] = z[ts - 8:]

    ins = [("t", u, POOLW, 8), ("t", u, POOLW, 9), ("t", u, POOLW, 10), ("t", u, POOLW, 11), ("w", wblk), ("w", pool_scale),
           ("w", conv_b)]
    return _rows_call("poolconv_fwd", body, S, ts, ins, [("t", POOLW, MXU), ("t", POOLW, MXU)],
                      scratch=[pltpu.VMEM((16, POOLW), F32), pltpu.VMEM((8, POOLW), F32)])


def _poolconv_bwd(u, d_a2, d_yb, du, wblk, pool_scale, conv_b, ts=512):
    S = u.shape[0]

    def body(i, g, a_ref, bx_ref, bb_ref, bc_ref, ap_ref, bxp_ref, bcp_ref, da2_ref, dyb_ref, wblk_ref, ps_ref, cw_ref, _,
             o_ref, dps_ref, dwb_ref, dcw_ref, ce, cdz):
        @pl.when(g == 0)
        def _():
            ce[...] = jnp.zeros_like(ce)
            cdz[...] = jnp.zeros_like(cdz)

        first = (i > 0).astype(F32)
        a = a_ref[...].astype(F32)
        p = _pooled(a, ap_ref[...].astype(F32) * first, i * ts)
        pb = p.astype(MXU)
        mixed = _dot(pb, wblk_ref[...])
        da2 = da2_ref[...]
        dmixed = (da2 * ps_ref[...]).astype(MXU)
        dp = _dot_nt(dmixed, wblk_ref[...])
        _acc(dps_ref, g, jnp.sum(da2 * mixed, axis=0, keepdims=True))
        _acc(dwb_ref, g, _dot_tn(pb, dmixed))
        e = dp / _pool_cnt(i * ts, ts)
        ext = jnp.concatenate([e, ce[...]], axis=0)
        n = ts + 16
        f2 = ext + pltpu.roll(ext, n - 1, axis=0)
        f4 = f2 + pltpu.roll(f2, n - 2, axis=0)
        f8 = f4 + pltpu.roll(f4, n - 4, axis=0)
        f16 = f8 + pltpu.roll(f8, n - 8, axis=0)
        o_ref[:, 0:POOLW] = (_pool_select(f2, f4, f8, f16)[:ts] - dp).astype(o_ref.dtype)
        ce[...] = e[:16]

        bx, bb, bc = bx_ref[...].astype(F32), bb_ref[...].astype(F32), bc_ref[...].astype(F32)
        z = bc * bx
        w = cw_ref[...]
        conv, z1, z2 = _conv3(z, (bxp_ref[...].astype(F32) * bcp_ref[...].astype(F32))[8:16] * first, w)
        dyb = dyb_ref[...]
        dconv = dyb * bb
        dz = _conv3_t(dconv, cdz[...], w)
        o_ref[:, POOLW:2 * POOLW] = (dz * bc).astype(o_ref.dtype)
        o_ref[:, 2 * POOLW:3 * POOLW] = (dyb * conv).astype(o_ref.dtype)
        o_ref[:, 3 * POOLW:4 * POOLW] = (dz * bx).astype(o_ref.dtype)
        dw = jnp.concatenate([jnp.sum(dconv * z2, axis=0, keepdims=True), jnp.sum(dconv * z1, axis=0, keepdims=True),
                              jnp.sum(dconv * z, axis=0, keepdims=True)], axis=0)
        _acc(dcw_ref, g, dw)
        cdz[...] = dconv[:8]

    ins = [("t", u, POOLW, 8), ("t", u, POOLW, 9), ("t", u, POOLW, 10), ("t", u, POOLW, 11),
           ("h", u, 16, POOLW, 8), ("h", u, 16, POOLW, 9), ("h", u, 16, POOLW, 11),
           ("t", d_a2, POOLW, 0), ("t", d_yb, POOLW, 0), ("w", wblk), ("w", pool_scale), ("w", conv_b), ("x", du)]
    outs = [("c", IN_W, 4 * POOLW, GATE_W // (4 * POOLW), MXU), ("a", (1, POOLW), F32), ("a", (POOLW, POOLW), F32), ("a", (3, POOLW), F32)]
    return _rows_call("poolconv_bwd", body, S, ts, ins, outs, aliases={len(ins) - 1: 0},
                      scratch=[pltpu.VMEM((16, POOLW), F32), pltpu.VMEM((8, POOLW), F32)], reverse=True)


def _rope_tables(positions):
    S = positions.shape[0]
    inv = ROPE_THETA ** (-jnp.arange(0, 16, 2, dtype=F32) / 16)
    ang = positions.astype(F32)[:, None] * inv
    cos, sin = jnp.cos(ang), jnp.sin(ang)
    c64 = jnp.concatenate([cos, cos, jnp.ones((S, 48), F32)], axis=1)
    s64 = jnp.concatenate([-sin, sin, jnp.zeros((S, 48), F32)], axis=1)
    return jnp.concatenate([c64, c64], axis=1), jnp.concatenate([s64, s64], axis=1)


def _partner(x):
    lane = lax.broadcasted_iota(jnp.int32, x.shape, 1) % 64
    return jnp.where(lane < 8, pltpu.roll(x, LANES - 8, axis=1), jnp.where(lane < 16, pltpu.roll(x, 8, axis=1), 0.0))


def _rope(x, c, s):
    return x * c + _partner(x) * s


def _rope_t(x, c, s):
    return x * c + _partner(x * s)


def _rows_of(r, n, d):
    return pl.ds(r, n, stride=d) if d > 1 else pl.ds(0, n)


def _head_masks(shape):
    lane = lax.broadcasted_iota(jnp.int32, shape, 1) // 64
    return [lane == h for h in range(4)]


def _only(mask, x):
    return jnp.where(mask, x, jnp.zeros_like(x))


def _rope_perm(u, ctab, stab, ts=512):
    S = u.shape[0]
    nch = ATT_W // LANES

    def body(*refs):
        chunks, (c_ref, s_ref), outs, scr = refs[:3 * nch], refs[3 * nch:3 * nch + 2], refs[3 * nch + 2:-1], refs[-1]
        for k in range(3 * nch):
            scr[k] = chunks[k][...].astype(F32)
        for g, d in enumerate(DILS):
            n = ts // d
            for r in range(d):
                rows = _rows_of(r, n, d)
                c, s = c_ref[rows, :], s_ref[rows, :]
                for which in range(3):
                    parts = [scr.at[which * nch + j][rows, :] for j in (2 * g, 2 * g + 1)]
                    if which < 2:
                        parts = [_rope(x, c, s) for x in parts]
                    outs[which * 3 + g][r] = jnp.concatenate(parts, axis=1).astype(MXU)

    base = (IN_W - 3 * ATT_W) // LANES
    in_specs = [pl.BlockSpec((ts, LANES), lambda i, cb=base + k: (i, cb)) for k in range(3 * nch)]
    in_specs += [pl.BlockSpec((ts, LANES), lambda i: (i, 0))] * 2
    out_specs = [pl.BlockSpec((d, ts // d, ATT_O), lambda i: (0, i, 0)) for _ in range(3) for d in DILS]
    out_shape = [jax.ShapeDtypeStruct((d, S // d, ATT_O), MXU) for _ in range(3) for d in DILS]
    res = pl.pallas_call(body, grid=(S // ts,), in_specs=in_specs, out_specs=out_specs, out_shape=out_shape,
                         scratch_shapes=[pltpu.VMEM((3 * nch, ts, LANES), F32)],
                         compiler_params=_params(("arbitrary",)), name="rope_perm")(*([u] * (3 * nch)), ctab, stab)
    return [[res[which * 3 + g].reshape(S, ATT_O) for g in range(3)] for which in range(3)]


def _rope_unperm_bwd(dqkv, du, ctab, stab, ts=512):
    S = dqkv[0][0].shape[0]
    nch = ATT_W // LANES

    def body(*refs):
        ins, (c_ref, s_ref, _, o_ref, scr) = refs[:9], refs[9:]
        for g, d in enumerate(DILS):
            n = ts // d
            for r in range(d):
                rows = _rows_of(r, n, d)
                c, s = c_ref[rows, :], s_ref[rows, :]
                for which in range(3):
                    v = ins[which * 3 + g][r]
                    for half in range(2):
                        x = v[:, half * LANES:(half + 1) * LANES]
                        scr.at[which * nch + 2 * g + half][rows, :] = _rope_t(x, c, s) if which < 2 else x
        for j in range(3 * nch):
            o_ref[:, j * LANES:(j + 1) * LANES] = scr[j].astype(o_ref.dtype)

    in_specs = [pl.BlockSpec((d, ts // d, ATT_O), lambda i: (0, i, 0)) for _ in range(3) for d in DILS]
    in_specs += [pl.BlockSpec((ts, LANES), lambda i: (i, 0))] * 2 + [pl.BlockSpec(memory_space=pl.ANY)]
    args = [dqkv[which][g].reshape(d, S // d, ATT_O) for which in range(3) for g, d in enumerate(DILS)]
    last = (IN_W - 3 * ATT_W) // (3 * ATT_W)
    return pl.pallas_call(body, grid=(S // ts,), in_specs=in_specs, out_specs=pl.BlockSpec((ts, 3 * ATT_W), lambda i: (i, last)),
                          out_shape=jax.ShapeDtypeStruct((S, IN_W), MXU), scratch_shapes=[pltpu.VMEM((3 * nch, ts, LANES), F32)],
                          input_output_aliases={len(in_specs) - 1: 0},
                          compiler_params=_params(("arbitrary",)), name="rope_unperm_bwd")(*args, ctab, stab, du)


def _band_mask_keys(has_prev):
    r = lax.broadcasted_iota(jnp.int32, (QB, 2 * QB), 0)
    c = lax.broadcasted_iota(jnp.int32, (QB, 2 * QB), 1)
    return ((c < QB) & (c >= r) & has_prev) | ((c >= QB) & (c - QB <= r))


def _band_mask_queries(has_next):
    r = lax.broadcasted_iota(jnp.int32, (2 * QB, QB), 0)
    c = lax.broadcasted_iota(jnp.int32, (2 * QB, QB), 1)
    return ((r < QB) & (c <= r)) | ((r >= QB) & (c >= r - QB) & has_next)


ASUB = 4
_BIG = pl.BlockSpec((ASUB * QB, ATT_O), lambda b: (b, 0))
_PREV = pl.BlockSpec((QB, ATT_O), lambda b: (jnp.maximum(b * ASUB - 1, 0), 0))


def _sub(ref, j):
    return ref[j * QB:(j + 1) * QB]


def _attn_fwd(g, q, k, v):
    S = q.shape[0]
    nb = S // QB
    nblk = nb // DILS[g]

    def body(q_ref, kc_ref, kp_ref, vc_ref, vp_ref, o_ref, m_ref, l_ref):
        hm_kv, hm_o = _head_masks((2 * QB, ATT_O)), _head_masks((QB, ATT_O))
        for j in range(ASUB):
            ok = _band_mask_keys(((pl.program_id(0) * ASUB + j) & (nblk - 1)) > 0)
            k2 = jnp.concatenate([kp_ref[...] if j == 0 else _sub(kc_ref, j - 1), _sub(kc_ref, j)], axis=0)
            v2 = jnp.concatenate([vp_ref[...] if j == 0 else _sub(vc_ref, j - 1), _sub(vc_ref, j)], axis=0)
            qv = _sub(q_ref, j)
            o_acc = jnp.zeros((QB, ATT_O), F32)
            m_acc = jnp.zeros((QB, ATT_O), F32)
            l_acc = jnp.zeros((QB, ATT_O), F32)
            for h in range(4):
                s = jnp.where(ok, _dot_nt(qv, _only(hm_kv[h], k2)) * ATT_SCALE, NEG)
                m = jnp.max(s, axis=1, keepdims=True)
                p = jnp.exp(s - m)
                o_acc = o_acc + _dot(p.astype(MXU), _only(hm_kv[h], v2))
                m_acc = jnp.where(hm_o[h], m, m_acc)
                l_acc = jnp.where(hm_o[h], jnp.sum(p, axis=1, keepdims=True), l_acc)
            o_ref[j * QB:(j + 1) * QB] = o_acc
            m_ref[j * QB:(j + 1) * QB] = m_acc
            l_ref[j * QB:(j + 1) * QB] = l_acc

    shp = jax.ShapeDtypeStruct((S, ATT_O), F32)
    return pl.pallas_call(body, grid=(nb // ASUB,), in_specs=[_BIG, _BIG, _PREV, _BIG, _PREV],
                          out_specs=[_BIG] * 3, out_shape=[shp, shp, shp], compiler_params=_params(("arbitrary",)),
                          name=f"attn_fwd_{g}")(q, k, k, v, v)


def _natural(ref, d, scr, ts):
    if d == 1:
        return ref[0]
    n = ts // d
    for r in range(d):
        v = ref[r]
        scr.at[0][pl.ds(r, n, stride=d), :] = v[:, 0:LANES]
        scr.at[1][pl.ds(r, n, stride=d), :] = v[:, LANES:2 * LANES]
    return jnp.concatenate([scr[0], scr[1]], axis=1)


def _attn_combine(oml, ts=512):
    S = oml[0][0].shape[0]

    def body(*refs):
        ins, (att_ref, out_ref, lse_ref, scr) = refs[:9], refs[9:]
        o, m, l = [[_natural(ins[3 * g + k], d, scr, ts) for g, d in enumerate(DILS)] for k in range(3)]
        mx = jnp.maximum(jnp.maximum(m[0], m[1]), m[2])
        w = [jnp.exp(m[g] - mx) for g in range(3)]
        den = w[0] * l[0] + w[1] * l[1] + w[2] * l[2]
        out = (w[0] * o[0] + w[1] * o[1] + w[2] * o[2]) / den
        out_ref[...] = out
        att_ref[...] = out.astype(MXU)
        lse_ref[...] = mx + jnp.log(den)

    in_specs = [pl.BlockSpec((d, ts // d, ATT_O), lambda i: (0, i, 0)) for d in DILS for _ in range(3)]
    args = [a.reshape(d, S // d, ATT_O) for d, grp in zip(DILS, oml) for a in grp]
    blk = pl.BlockSpec((ts, ATT_O), lambda i: (i, 0))
    return pl.pallas_call(body, grid=(S // ts,), in_specs=in_specs, out_specs=[blk, blk, blk],
                          out_shape=[jax.ShapeDtypeStruct((S, ATT_O), MXU), jax.ShapeDtypeStruct((S, ATT_O), F32),
                                     jax.ShapeDtypeStruct((S, ATT_O), F32)],
                          scratch_shapes=[pltpu.VMEM((2, ts, LANES), F32)], compiler_params=_params(("arbitrary",)),
                          name="attn_combine")(*args)


def _attn_bwd_prep(datt, o, lse, ts=512):
    S = datt.shape[0]

    def body(da0, da1, o_ref, l0, l1, *rest):
        outs, dl = rest[:9], rest[9]
        prod = jnp.concatenate([da0[...], da1[...]], axis=1) * o_ref[...]
        delta = jnp.zeros((ts, ATT_O), F32)
        for hm in _head_masks((ts, ATT_O)):
            delta = jnp.where(hm, jnp.sum(_only(hm, prod), axis=1, keepdims=True), delta)
        dl[0] = delta[:, 0:LANES]
        dl[1] = delta[:, LANES:2 * LANES]
        for g, d in enumerate(DILS):
            n = ts // d
            for r in range(d):
                rows = _rows_of(r, n, d)
                outs[g][r] = jnp.concatenate([da0[rows, :], da1[rows, :]], axis=1).astype(MXU)
                outs[3 + g][r] = jnp.concatenate([dl.at[0][rows, :], dl.at[1][rows, :]], axis=1)
                outs[6 + g][r] = jnp.concatenate([l0[rows, :], l1[rows, :]], axis=1)

    half = lambda j: pl.BlockSpec((ts, LANES), lambda i: (i, j))
    out_specs = [pl.BlockSpec((d, ts // d, ATT_O), lambda i: (0, i, 0)) for _ in range(3) for d in DILS]
    out_shape = [jax.ShapeDtypeStruct((d, S // d, ATT_O), dt) for dt in (MXU, F32, F32) for d in DILS]
    res = pl.pallas_call(body, grid=(S // ts,), in_specs=[half(0), half(1), pl.BlockSpec((ts, ATT_O), lambda i: (i, 0)), half(0), half(1)],
                         out_specs=out_specs, out_shape=out_shape, scratch_shapes=[pltpu.VMEM((2, ts, LANES), F32)],
                         compiler_params=_params(("arbitrary",)), name="attn_bwd_prep")(datt, datt, o, lse, lse)
    return [[res[k * 3 + g].reshape(S, ATT_O) for g in range(3)] for k in range(3)]


def _head_col(x, h):
    return x[:, h * 64:h * 64 + 1]


def _attn_bwd(g, q, k, v, do, delta, lse):
    S = q.shape[0]
    nb = S // QB
    nblk = nb // DILS[g]

    def body(k_ref, v_ref, qc_ref, qn_ref, doc_ref, don_ref, dlc_ref, dln_ref, lc_ref, ln_ref, dq_ref, dk_ref, dv_ref, dq_scr):
        hms, hmk = _head_masks((2 * QB, ATT_O)), _head_masks((QB, ATT_O))
        first = pl.program_id(0) == 0

        @pl.when(first)
        def _():
            dq_scr[0:QB] = jnp.zeros((QB, ATT_O), F32)

        @pl.when(jnp.logical_not(first))
        def _():
            dq_scr[0:QB] = dq_scr[ASUB * QB:(ASUB + 1) * QB]

        dq_scr[QB:(ASUB + 1) * QB] = jnp.zeros((ASUB * QB, ATT_O), F32)

        def both(cur_ref, nxt_ref, j):
            return jnp.concatenate([_sub(cur_ref, j), nxt_ref[...] if j == ASUB - 1 else _sub(cur_ref, j + 1)], axis=0)

        for j in range(ASUB):
            ok = _band_mask_queries(((pl.program_id(0) * ASUB + j + 1) & (nblk - 1)) > 0)
            q2, do2, dl2, lse2 = both(qc_ref, qn_ref, j), both(doc_ref, don_ref, j), both(dlc_ref, dln_ref, j), both(lc_ref, ln_ref, j)
            kv, vv = _sub(k_ref, j), _sub(v_ref, j)
            dk = jnp.zeros((QB, ATT_O), F32)
            dv = jnp.zeros((QB, ATT_O), F32)
            dq2 = jnp.zeros((2 * QB, ATT_O), F32)
            for h, hm in enumerate(hms):
                qh, doh = _only(hm, q2), _only(hm, do2)
                p = jnp.where(ok, jnp.exp(_dot_nt(qh, kv) * ATT_SCALE - _head_col(lse2, h)), 0.0)
                ds = (p * (_dot_nt(doh, vv) - _head_col(dl2, h))).astype(MXU)
                dv = dv + _dot_tn(p.astype(MXU), doh)
                dk = dk + _dot_tn(ds, qh)
                dq2 = dq2 + _dot(ds, _only(hmk[h], kv))
            dk_ref[j * QB:(j + 1) * QB] = dk * ATT_SCALE
            dv_ref[j * QB:(j + 1) * QB] = dv
            dq_scr[j * QB:(j + 2) * QB] += dq2
        dq_ref[...] = dq_scr[0:ASUB * QB] * ATT_SCALE

    nxt = pl.BlockSpec((QB, ATT_O), lambda b: (jnp.minimum((b + 1) * ASUB, nb - 1), 0))
    shp = jax.ShapeDtypeStruct((S, ATT_O), F32)
    return pl.pallas_call(body, grid=(nb // ASUB,), in_specs=[_BIG, _BIG, _BIG, nxt, _BIG, nxt, _BIG, nxt, _BIG, nxt], out_specs=[_BIG] * 3,
                          out_shape=[shp, shp, shp], scratch_shapes=[pltpu.VMEM(((ASUB + 1) * QB, ATT_O), F32)],
                          compiler_params=_params(("arbitrary",)), name=f"attn_bwd_{g}")(k, v, q, q, do, do, delta, delta, lse, lse)


def _merge_fwd(x0, u, a2, yb, att, wa, wb, wc, w_out, g_post, ts=256):
    S = x0.shape[0]

    def body(i, g, x_ref, gate_ref, a2_ref, yb_ref, att_ref, wa_ref, wb_ref, wc_ref, wo_ref, gp_ref, mg_ref, y_ref, xo_ref):
        gate = lambda n: jax.nn.sigmoid(gate_ref[:, n * D:(n + 1) * D].astype(F32))
        merged = gate(0) * _dot_nt(a2_ref[...], wa_ref[...])
        merged = merged + gate(1) * _dot_nt(yb_ref[...], wb_ref[...])
        merged = merged + gate(2) * _dot_nt(att_ref[...], wc_ref[...])
        mb = merged.astype(MXU)
        mg_ref[...] = mb
        y = _dot(mb, wo_ref[...])
        y_ref[...] = y
        xo_ref[...] = x_ref[...] + _rms(y, gp_ref[...])[0]

    ins = [("t", x0, D, 0), ("t", u, GATE_W, 0), ("t", a2, POOLW, 0), ("t", yb, POOLW, 0), ("t", att, ATT_O, 0),
           ("w", wa), ("w", wb), ("w", wc), ("w", w_out), ("w", g_post)]
    return _rows_call("merge_fwd", body, S, ts, ins, [("t", D, MXU), ("t", D, F32), ("t", D, F32)])


def _merge_bwd(dx, y1, u, a2, yb, att, merged, wa, wb, wc, w_out, g_post, ts=256):
    S = dx.shape[0]
    last = S // ts - 1

    def body(i, g, dx_ref, y_ref, gate_ref, a2_ref, yb_ref, att_ref, mg_ref, wa_ref, wb_ref, wc_ref, wo_ref, gp_ref,
             dgate_ref, da2_ref, dyb_ref, datt_ref, dgp_ref, dwo_ref, dwa_ref, dwb_ref, dwc_ref, acc_o, acc_a, acc_b, acc_c):
        @pl.when(g == 0)
        def _():
            for acc in (acc_o, acc_a, acc_b, acc_c):
                acc[...] = jnp.zeros_like(acc)

        dxv, y = dx_ref[...], y_ref[...]
        dy, r = _rms_bwd(dxv * gp_ref[...], y)
        _acc(dgp_ref, g, jnp.sum(dxv * (y * r), axis=0, keepdims=True))
        dyb16 = dy.astype(MXU)
        acc_o[...] += _dot_tn(mg_ref[...], dyb16)
        dm = _dot_nt(dyb16, wo_ref[...])
        for n, (src, w_ref, din_ref, acc) in enumerate(((a2_ref, wa_ref, da2_ref, acc_a), (yb_ref, wb_ref, dyb_ref, acc_b),
                                                       (att_ref, wc_ref, datt_ref, acc_c))):
            gt = jax.nn.sigmoid(gate_ref[:, n * D:(n + 1) * D].astype(F32))
            br = _dot_nt(src[...], w_ref[...])
            dgate_ref[:, n * D:(n + 1) * D] = (dm * br * gt * (1.0 - gt)).astype(dgate_ref.dtype)
            dbr = (dm * gt).astype(MXU)
            acc[...] += _dot_tn(dbr, src[...])
            din_ref[...] = _dot(dbr, w_ref[...])

        @pl.when(g == last)
        def _():
            for out, acc in ((dwo_ref, acc_o), (dwa_ref, acc_a), (dwb_ref, acc_b), (dwc_ref, acc_c)):
                out[...] = acc[...].astype(MXU)

    ins = [("t", dx, D, 0), ("t", y1, D, 0), ("t", u, GATE_W, 0), ("t", a2, POOLW, 0), ("t", yb, POOLW, 0), ("t", att, ATT_O, 0),
           ("t", merged, D, 0), ("w", wa), ("w", wb), ("w", wc), ("w", w_out), ("w", g_post)]
    wshapes = [(D, D), (D, POOLW), (D, POOLW), (D, ATT_O)]
    outs = [("c", IN_W, GATE_W, 0, MXU), ("t", POOLW, F32), ("t", POOLW, F32), ("t", ATT_O, F32), ("a", (1, D), F32)]
    outs += [("a", s, MXU) for s in wshapes]
    return _rows_call("merge_bwd", body, S, ts, ins, outs, scratch=[pltpu.VMEM(s, F32) for s in wshapes])


def _prenorm_bwd(name, dx_res, du, wt, x, g_pre, ts=256, lead=0):
    S = x.shape[0]
    N = du.shape[1]

    def body(i, g, dx_ref, du_ref, wt_ref, x_ref, g_ref, o_ref, dg_ref):
        if lead:
            dhv = _dot(du_ref[:, 0:lead], wt_ref[N - lead:N, :]) + _dot(du_ref[:, lead:N], wt_ref[0:N - lead, :])
        else:
            dhv = _dot(du_ref[...], wt_ref[...])
        xv = x_ref[...]
        dxn, r = _rms_bwd(dhv * g_ref[...], xv)
        o_ref[...] = dx_ref[...] + dxn
        _acc(dg_ref, g, jnp.sum(dhv * (xv * r), axis=0, keepdims=True))

    ins = [("t", dx_res, D, 0), ("t", du, N, 0), ("w", wt), ("t", x, D, 0), ("w", g_pre)]
    return _rows_call(name, body, S, ts, ins, [("t", D, F32), ("a", (1, D), F32)])


def _mem_heads(qm, kv_ref):
    out = []
    for h in range(4):
        q = qm[:, h * 128:(h + 1) * 128].astype(MXU)
        k = kv_ref[:, h * 128:(h + 1) * 128]
        v = kv_ref[:, MEM_W + h * 128:MEM_W + (h + 1) * 128]
        sc = _dot_nt(q, k) * MEM_SCALE
        e = jnp.exp(sc - jnp.max(sc, axis=1, keepdims=True))
        out.append((e / jnp.sum(e, axis=1, keepdims=True), q, k, v))
    return out


def _mem_fwd(x1, kv, g_pre, w_mq, w_mo, g_post, ts=256):
    S = x1.shape[0]

    def body(i, g, x_ref, kv_ref, gq_ref, wq_ref, wo_ref, gp_ref, om_ref, y_ref, xo_ref):
        x = x_ref[...]
        hb = _rms(x, gq_ref[...])[0].astype(MXU)
        qm = _dot(hb, wq_ref[...])
        om = jnp.concatenate([_dot(p.astype(MXU), v) for p, _, _, v in _mem_heads(qm, kv_ref)], axis=1).astype(MXU)
        om_ref[...] = om
        y = _dot_nt(om, wo_ref[...])
        y_ref[...] = y
        xo_ref[...] = x + _rms(y, gp_ref[...])[0]

    ins = [("t", x1, D, 0), ("w", kv), ("w", g_pre), ("w", w_mq), ("w", w_mo), ("w", g_post)]
    return _rows_call("mem_fwd", body, S, ts, ins, [("t", MEM_W, MXU), ("t", D, F32), ("t", D, F32)])


def _mem_bwd(dx2, ym, x1, om, kv, g_pre, w_mq, w_mo, g_post, ts=256):
    S = x1.shape[0]
    last = S // ts - 1

    def body(i, g, dx_ref, y_ref, x_ref, om_ref, kv_ref, gq_ref, wq_ref, wo_ref, gp_ref, dxo_ref, dgp_ref, dgq_ref, dkv_ref,
             dwo_ref, dwq_ref, acc_o, acc_q):
        dxv, y, x = dx_ref[...], y_ref[...], x_ref[...]
        dy, r = _rms_bwd(dxv * gp_ref[...], y)
        _acc(dgp_ref, g, jnp.sum(dxv * (y * r), axis=0, keepdims=True))
        dyb = dy.astype(MXU)
        dom = _dot(dyb, wo_ref[...])
        h, r1 = _rms(x, gq_ref[...])
        hb = h.astype(MXU)
        qm = _dot(hb, wq_ref[...])
        dqs = []

        @pl.when(g == 0)
        def _():
            dkv_ref[...] = jnp.zeros_like(dkv_ref)
            acc_o[...] = jnp.zeros_like(acc_o)
            acc_q[...] = jnp.zeros_like(acc_q)

        acc_o[...] += _dot_tn(dyb, om_ref[...])

        for hh, (p, q, k, v) in enumerate(_mem_heads(qm, kv_ref)):
            doh = dom[:, hh * 128:(hh + 1) * 128].astype(MXU)
            dp = _dot_nt(doh, v)
            dsc = (p * (dp - jnp.sum(dp * p, axis=1, keepdims=True)) * MEM_SCALE).astype(MXU)
            dqs.append(_dot(dsc, k))
            dkv_ref[:, hh * 128:(hh + 1) * 128] += _dot_tn(dsc, q)
            dkv_ref[:, MEM_W + hh * 128:MEM_W + (hh + 1) * 128] += _dot_tn(p.astype(MXU), doh)
        dq = jnp.concatenate(dqs, axis=1).astype(MXU)
        acc_q[...] += _dot_tn(hb, dq)
        dh = _dot_nt(dq, wq_ref[...])
        _acc(dgq_ref, g, jnp.sum(dh * (x * r1), axis=0, keepdims=True))
        dxo_ref[...] = dxv + _rms_bwd(dh * gq_ref[...], x)[0]

        @pl.when(g == last)
        def _():
            dwo_ref[...] = acc_o[...].astype(MXU)
            dwq_ref[...] = acc_q[...].astype(MXU)

    ins = [("t", dx2, D, 0), ("t", ym, D, 0), ("t", x1, D, 0), ("t", om, MEM_W, 0), ("w", kv), ("w", g_pre), ("w", w_mq), ("w", w_mo),
           ("w", g_post)]
    outs = [("t", D, F32), ("a", (1, D), F32), ("a", (1, D), F32), ("a", (256, D), F32), ("a", (D, MEM_W), MXU), ("a", (D, MEM_W), MXU)]
    return _rows_call("mem_bwd", body, S, ts, ins, outs, scratch=[pltpu.VMEM((D, MEM_W), F32), pltpu.VMEM((D, MEM_W), F32)])


def _gain_grad(name, dn, x):
    n = x.shape[0]

    def body(i, g, dn_ref, x_ref, o_ref):
        xv = x_ref[...]
        r = lax.rsqrt(jnp.mean(xv * xv, axis=-1, keepdims=True) + EPS)
        o_ref[...] = jnp.sum(dn_ref[...] * (xv * r), axis=0, keepdims=True)

    return _rows_call(name, body, n, n, [("t", dn, D, 0), ("t", x, D, 0)], [("a", (1, D), F32)])[0]


def _ffn_fwd(x2, u3, conv_f, w_down, g_post, ts=256):
    S = x2.shape[0]

    def body(i, g, x_ref, ua_ref, ub_ref, cw_ref, wd_ref, gp_ref, act_ref, y_ref, xo_ref, c_ref, cu):
        @pl.when(g == 0)
        def _():
            cu[...] = jnp.zeros_like(cu)

        ua = ua_ref[...].astype(F32)
        c, _, _ = _conv3(ua, cu[...], cw_ref[...])
        c_ref[...] = c.astype(MXU)
        act = (c * jax.nn.sigmoid(c) * ub_ref[...].astype(F32)).astype(MXU)
        act_ref[...] = act
        y = _dot(act, wd_ref[...])
        y_ref[...] = y
        xo_ref[...] = x_ref[...] + _rms(y, gp_ref[...])[0]
        cu[...] = ua[ts - 8:]

    ins = [("t", x2, D, 0), ("t", u3, D_FF, 0), ("t", u3, D_FF, 1), ("w", conv_f), ("w", w_down), ("w", g_post)]
    return _rows_call("ffn_fwd", body, S, ts, ins, [("t", D_FF, MXU), ("t", D, F32), ("t", D, F32), ("t", D_FF, MXU)],
                      scratch=[pltpu.VMEM((8, D_FF), F32)])


def _ffn_bwd(dx3, y3, u3, c, conv_f, w_down, g_post, ts=128):
    S = dx3.shape[0]

    def body(i, g, dx_ref, y_ref, ua_ref, ub_ref, c_ref, cw_ref, wd_ref, gp_ref, dy_ref, du_ref, dgp_ref, dcw_ref, cdc):
        @pl.when(g == 0)
        def _():
            cdc[...] = jnp.zeros_like(cdc)

        dxv, y = dx_ref[...], y_ref[...]
        dy, r = _rms_bwd(dxv * gp_ref[...], y)
        _acc(dgp_ref, g, jnp.sum(dxv * (y * r), axis=0, keepdims=True))
        dyb = dy.astype(MXU)
        dy_ref[...] = dyb
        dact = _dot_nt(dyb, wd_ref[...])
        ua, c, w = ua_ref[...].astype(F32), c_ref[...].astype(F32), cw_ref[...]
        sg = jax.nn.sigmoid(c)
        du_ref[:, D_FF:2 * D_FF] = (dact * (c * sg)).astype(du_ref.dtype)
        dc = dact * ub_ref[...].astype(F32) * (sg * (1.0 + c * (1.0 - sg)))
        dua, dc1, dc2 = _conv3_t(dc, cdc[...], w, shifted=True)
        du_ref[:, 0:D_FF] = dua.astype(du_ref.dtype)
        dw = jnp.concatenate([jnp.sum(ua * dc2, axis=0, keepdims=True), jnp.sum(ua * dc1, axis=0, keepdims=True),
                              jnp.sum(ua * dc, axis=0, keepdims=True)], axis=0)
        _acc(dcw_ref, g, dw)
        cdc[...] = dc[:8]

    ins = [("t", dx3, D, 0), ("t", y3, D, 0), ("t", u3, D_FF, 0), ("t", u3, D_FF, 1), ("t", c, D_FF, 0), ("w", conv_f),
           ("w", w_down), ("w", g_post)]
    outs = [("t", D, MXU), ("t", 2 * D_FF, MXU), ("a", (1, D), F32), ("a", (3, D_FF), F32)]
    return _rows_call("ffn_bwd", body, S, ts, ins, outs, scratch=[pltpu.VMEM((8, D_FF), F32)], reverse=True)


def _loss_head(x, target, ts=512):
    S = x.shape[0]

    def body(i, g, x_ref, t_ref, dx_ref, acc_ref):
        diff = x_ref[...] - t_ref[...]
        dx_ref[...] = diff * (1.0 / D)
        col = jnp.sum(diff * diff, axis=0, keepdims=True)
        part = col[:, 0:LANES]
        for j in range(1, D // LANES):
            part = part + col[:, j * LANES:(j + 1) * LANES]
        row = lax.broadcasted_iota(jnp.int32, (8, LANES), 0)
        _acc(acc_ref, g, jnp.where(row == 0, jnp.broadcast_to(part, (8, LANES)), 0.0))

    return _rows_call("loss_head", body, S, ts, [("t", x, D, 0), ("t", target, D, 0)], [("t", D, F32), ("a", (8, LANES), F32)])


_OPERAND_NAME = dict(w_in='w_in', w_branch_a='wa', w_branch_b='wb', w_branch_c='wc', w_out='w_out', w_mq='w_mq', w_mkv='w_mkv',
                     w_mo='w_mo', w_up='w_up', w_down='w_down')


def _big_operands(big):
    return {_OPERAND_NAME[n]: a for n, a in big.items()}


def _layer_weights(big, small, l):
    pool_w = small['pool_w'][l].astype(MXU)
    wblk = jnp.zeros((POOLW, POOLW), MXU)
    for g in range(4):
        wblk = lax.dynamic_update_slice(wblk, pool_w[g], (g * 96, g * 96))
    vec = lambda n: small[n][l].reshape(1, -1)
    return dict(
        _big_operands(big),
        wblk=wblk, pool_scale=vec('pool_scale'), conv_b=small['conv_b_w'][l], conv_f=small['conv_ffn_w'][l],
        g_mix_pre=vec('norm_mix_pre'), g_mix_post=vec('norm_mix_post'), g_mem_pre=vec('norm_mem_pre'),
        g_mem_post=vec('norm_mem_post'), g_memkv=vec('norm_memkv'), g_ffn_pre=vec('norm_ffn_pre'), g_ffn_post=vec('norm_ffn_post'))


def _layer_fwd(x0, mem, W, ctab, stab):
    sv = _layer_fwd_mix(x0, W, ctab, stab)
    return _layer_fwd_late(mem, W, sv), sv


def _layer_fwd_mix(x0, W, ctab, stab):
    return _layer_fwd_merge(W, _layer_fwd_branches(x0, W, ctab, stab))


def _layer_fwd_branches(x0, W, ctab, stab):
    sv = dict(x0=x0)
    sv['u'], sv['h1'] = _norm_mm("in_proj", x0, W['g_mix_pre'], W['w_in'], ts=2048, tn=IN_TILE, wt=True, rot=IN_ROT, out_dtype=MXU)
    sv['a2'], sv['yb'] = _poolconv_fwd(sv['u'], W['wblk'], W['pool_scale'], W['conv_b'])
    sv['qkv'] = q3, k3, v3 = _rope_perm(sv['u'], ctab, stab)
    sv['att'], sv['o'], sv['lse'] = _attn_combine([_attn_fwd(g, q3[g], k3[g], v3[g]) for g in range(3)])
    return sv


def _layer_fwd_merge(W, sv):
    sv['merged'], sv['y1'], sv['x1'] = _merge_fwd(sv['x0'], sv['u'], sv['a2'], sv['yb'], sv['att'], W['wa'], W['wb'], W['wc'],
                                                  W['w_out'], W['g_mix_post'])
    return sv


def _layer_fwd_late(mem, W, sv):
    sv['kv'], sv['memn'] = _norm_mm("mem_kv", mem, W['g_memkv'], W['w_mkv'], ts=256, tn=D, out_dtype=MXU)
    sv['om'], sv['ym'], sv['x2'] = _mem_fwd(sv['x1'], sv['kv'], W['g_mem_pre'], W['w_mq'], W['w_mo'], W['g_mem_post'])
    sv['u3'], sv['h3'] = _norm_mm("up_proj", sv['x2'], W['g_ffn_pre'], W['w_up'], ts=2048, tn=1408, wt=True, out_dtype=MXU)
    sv['act'], sv['y3'], x3, sv['c3'] = _ffn_fwd(sv['x2'], sv['u3'], W['conv_f'], W['w_down'], W['g_ffn_post'])
    return x3


def _layer_bwd(dx3, mem, W, sv, ctab, stab):
    dx1, g = _layer_bwd_late(dx3, mem, W, sv)
    dx0, g_mix = _layer_bwd_mix(dx1, W, sv, ctab, stab)
    return dx0, {**g, **g_mix}


def _layer_bwd_late(dx3, mem, W, sv):
    g = {}
    dy3, du3, g['norm_ffn_post'], g['conv_ffn_w'] = _ffn_bwd(dx3, sv['y3'], sv['u3'], sv['c3'], W['conv_f'], W['w_down'], W['g_ffn_post'])
    g['w_down'] = _mm_tn("dw_down", sv['act'], dy3, cap_k=256)
    g['w_up'] = _mm_tn("dw_up", du3, sv['h3'])
    dx2, g['norm_ffn_pre'] = _prenorm_bwd("ffn_pre_bwd", dx3, du3, W['w_up'], sv['x2'], W['g_ffn_pre'], ts=512)
    dx1, g['norm_mem_post'], g['norm_mem_pre'], dkv, g['w_mo'], g['w_mq'] = _mem_bwd(
        dx2, sv['ym'], sv['x1'], sv['om'], sv['kv'], W['g_mem_pre'], W['w_mq'], W['w_mo'], W['g_mem_post'])
    dkvb = dkv.astype(MXU)
    g['w_mkv'] = _mm_tn("dw_mkv", sv['memn'], dkvb)
    g['norm_memkv'] = _gain_grad("memkv_gain", _mm_nt("d_memn", dkvb, W['w_mkv'], ts=256, tn=512), mem)
    return dx1, g


def _layer_bwd_mix(dx1, W, sv, ctab, stab):
    du, g = _layer_bwd_mixers(dx1, W, sv, ctab, stab)
    g['w_in'] = _dw_in(du, sv)
    dx0, g['norm_mix_pre'] = _mix_pre_bwd(dx1, du, W, sv)
    return dx0, g


def _dw_in(du, sv):
    return _mm_tn("dw_in", du, sv['h1'], cap_k=IN_TILE, rot=IN_ROT)


def _mix_pre_bwd(dx1, du, W, sv):
    return _prenorm_bwd("mix_pre_bwd", dx1, du, W['w_in'], sv['x0'], W['g_mix_pre'], lead=GATE_W)


def _layer_bwd_mixers(dx1, W, sv, ctab, stab):
    parts, g = _layer_bwd_merge(dx1, W, sv)
    du, g_br = _layer_bwd_branches(parts, W, sv, ctab, stab)
    return du, {**g, **g_br}


def _layer_bwd_merge(dx1, W, sv):
    g = {}
    du, da2, dyb, datt, g['norm_mix_post'], g['w_out'], g['w_branch_a'], g['w_branch_b'], g['w_branch_c'] = _merge_bwd(
        dx1, sv['y1'], sv['u'], sv['a2'], sv['yb'], sv['att'], sv['merged'], W['wa'], W['wb'], W['wc'], W['w_out'], W['g_mix_post'])
    return (du, da2, dyb, datt), g


def _layer_bwd_branches(parts, W, sv, ctab, stab):
    du, da2, dyb, datt = parts
    g = {}
    du, g['pool_scale'], dwblk, g['conv_b_w'] = _poolconv_bwd(sv['u'], da2, dyb, du, W['wblk'], W['pool_scale'], W['conv_b'])
    g['pool_w'] = jnp.stack([dwblk[k * 96:(k + 1) * 96, k * 96:(k + 1) * 96] for k in range(4)])
    q3, k3, v3 = sv['qkv']
    do3, dl3, lse3 = _attn_bwd_prep(datt, sv['o'], sv['lse'])
    dqkv3 = [_attn_bwd(i, q3[i], k3[i], v3[i], do3[i], dl3[i], lse3[i]) for i in range(3)]
    du = _rope_unperm_bwd([[t[which] for t in dqkv3] for which in range(3)], du, ctab, stab)
    return du, g


def _local_step(x, mem, positions, target, big, small):
    ctab, stab = _rope_tables(positions)
    Ws = [_layer_weights(big[l], small, l) for l in range(DEPTH)]
    saved = []
    for l in range(DEPTH):
        x, sv = _layer_fwd(x, mem, Ws[l], ctab, stab)
        saved.append(sv)
    dx, acc = _loss_head(x, target)
    loss = jnp.sum(acc) * (0.5 / D)
    grads = [None] * DEPTH
    for l in reversed(range(DEPTH)):
        dx, grads[l] = _layer_bwd(dx, mem, Ws[l], saved[l], ctab, stab)
    return loss, dx, grads


_HBM = pl.BlockSpec(memory_space=pl.ANY)
MESH_ID = pl.DeviceIdType.MESH


def _all_gather(name, xs):
    n = len(xs)

    def body(*refs):
        x_refs, out_refs = refs[:n], refs[n:2 * n]
        send_sems, recv_sems, local_sems = refs[2 * n:]
        x, y, c = lax.axis_index("x"), lax.axis_index("y"), lax.axis_index("c")
        me, sibling = (x, y, c), (x, y, 1 - c)
        chips = [(1 - x, y), (x, 1 - y), (1 - x, 1 - y)]

        def slot(a, p):
            return out_refs[a].at[4 * p[0] + 2 * p[1] + p[2]]

        def copy(a, k, block, to, src=None):
            return pltpu.make_async_remote_copy(src_ref=slot(a, block) if src is None else src, dst_ref=slot(a, block),
                                                send_sem=send_sems.at[a, k], recv_sem=recv_sems.at[a, k], device_id=to,
                                                device_id_type=MESH_ID)

        started = []
        for a in range(n):
            mine = pltpu.make_async_copy(x_refs[a], slot(a, me), local_sems.at[a])
            mine.start()
            started.append(mine)
        first = []
        for a in range(n):
            first.append(copy(a, 0, me, sibling, src=x_refs[a]))
            first += [copy(a, 1 + j, me, (*chip, c), src=x_refs[a]) for j, chip in enumerate(chips)]
        for cp in first:
            cp.start()
        passed = []
        for j, chip in enumerate(chips):
            for a in range(n):
                copy(a, 1 + j, (*chip, c), me).wait_recv()
                fw = copy(a, 4 + j, (*chip, c), sibling)
                fw.start()
                passed.append(fw)
        for a in range(n):
            copy(a, 0, sibling, me).wait_recv()
            for j, chip in enumerate(chips):
                copy(a, 4 + j, (*chip, 1 - c), me).wait_recv()
        for cp in first + passed:
            cp.wait_send()
        for mine in started:
            mine.wait()

    return pl.pallas_call(
        body, out_shape=[jax.ShapeDtypeStruct((N_DEV,) + x.shape, x.dtype) for x in xs], in_specs=[_HBM] * n, out_specs=[_HBM] * n,
        scratch_shapes=[pltpu.SemaphoreType.DMA((n, 7)), pltpu.SemaphoreType.DMA((n, 7)), pltpu.SemaphoreType.DMA((n,))],
        name=name)(*xs)


_SEM =pl.BlockSpec(memory_space=pltpu.SEMAPHORE)
_IN_HBM = pl.BlockSpec(memory_space=pltpu.HBM)
_SIDE_EFFECT = pltpu.SideEffectType.DATAFLOW_SIDE_EFFECTING


def _push_copies(src_refs, land_refs, send_sems, recv_sems, per_peer):
    x, y, c = lax.axis_index("x"), lax.axis_index("y"), lax.axis_index("c")
    me = 4 * x + 2 * y + c
    copies = []
    for r in range(1, N_DEV):
        px, py, pc = x ^ ((r >> 2) & 1), y ^ ((r >> 1) & 1), c ^ (r & 1)
        for a, (s, d) in enumerate(zip(src_refs, land_refs)):
            k = a * (N_DEV - 1) + r - 1
            copies.append(pltpu.make_async_remote_copy(src_ref=s.at[4 * px + 2 * py + pc] if per_peer else s, dst_ref=d.at[me],
                                                       send_sem=send_sems.at[k], recv_sem=recv_sems.at[k],
                                                       device_id=(px, py, pc), device_id_type=MESH_ID))
    return copies


def _push_start(name, srcs, per_peer, after):
    n = len(srcs)
    lands = [lax.empty((N_DEV,) + (s.shape[1:] if per_peer else s.shape), s.dtype) for s in srcs]

    def body(*refs):
        for cp in _push_copies(refs[:n], refs[n:2 * n], refs[2 * n + 1], refs[2 * n + 2], per_peer):
            cp.start()
        refs[-1][...] = jnp.zeros_like(refs[-1])

    hbm = [pltpu.HBM(a.shape, a.dtype) for a in (*srcs, *lands)]
    sems = pltpu.SemaphoreType.DMA((n * (N_DEV - 1),))
    out = pl.pallas_call(
        body, name=name, out_shape=(sems, sems, *hbm, jax.ShapeDtypeStruct((8, LANES), F32)),
        in_specs=[_IN_HBM] * (2 * n) + [pl.BlockSpec(memory_space=pl.ANY)],
        out_specs=(_SEM, _SEM, *[_IN_HBM] * (2 * n), pl.BlockSpec(memory_space=pltpu.VMEM)),
        input_output_aliases={a: 2 + a for a in range(2 * n)},
        compiler_params=pltpu.CompilerParams(has_side_effects=_SIDE_EFFECT),
    )(*[pltpu.with_memory_space_constraint(a, pltpu.HBM) for a in (*srcs, *lands)], after)
    return out[0], out[1], out[2:2 + n], out[2 + n:2 + 2 * n], out[-1]


def _push_wait(name, started, per_peer, after):
    send_sems, recv_sems, srcs, lands, _ = started
    n = len(srcs)

    def body(*refs):
        for cp in _push_copies(refs[:n], refs[n:2 * n], refs[2 * n], refs[2 * n + 1], per_peer):
            cp.wait_send()
            cp.wait_recv()

    out = pl.pallas_call(
        body, name=name, out_shape=[pltpu.HBM(a.shape, a.dtype) for a in (*srcs, *lands)],
        in_specs=[_IN_HBM] * (2 * n) + [_SEM, _SEM, pl.BlockSpec(memory_space=pl.ANY)], out_specs=[_IN_HBM] * (2 * n),
        input_output_aliases={a: a for a in range(2 * n)},
        compiler_params=pltpu.CompilerParams(has_side_effects=_SIDE_EFFECT),
    )(*srcs, *lands, send_sems, recv_sems, after)
    if per_peer:
        return list(zip(out[:n], out[n:]))
    return _with_own(out[n:], out[:n], _my_slot())


def _my_slot():
    return 4 * lax.axis_index("x") + 2 * lax.axis_index("y") + lax.axis_index("c")


def _row_tile(rows, cols, budget):
    if rows * cols * 4 <= budget or rows % 16:
        return rows
    best = 16
    for t in range(16, rows + 1, 16):
        if rows % t == 0 and t * cols * 4 <= budget:
            best = t
    return best


def _sum_slots(name, pushed):
    src, recv = pushed
    _, R, C = recv.shape
    tr = _row_tile(R, C, 1 << 20)

    def body(r_ref, own_ref, o_ref):
        me = _my_slot()
        g = jnp.where(me == 0, own_ref[...], r_ref[0]).astype(F32)
        for k in range(1, N_DEV):
            g = g + jnp.where(me == k, own_ref[...], r_ref[k]).astype(F32)
        o_ref[...] = g

    return pl.pallas_call(body, grid=(R // tr,),
                          in_specs=[pl.BlockSpec((N_DEV, tr, C), lambda i: (0, i, 0)), pl.BlockSpec((None, tr, C), lambda i: (_my_slot(), i, 0))],
                          out_specs=pl.BlockSpec((tr, C), lambda i: (i, 0)), out_shape=jax.ShapeDtypeStruct((R, C), F32),
                          compiler_params=_params(("arbitrary",)), name=name)(recv, src)


def _adamw(name, g, w, m, v):
    shape = w.shape
    R, C = shape[-2], shape[-1]
    view = (-1, R, C)
    L = w.size // (R * C)
    tr = _row_tile(R, C, 1 << 20)
    c1 = 1.0 - ADAM_B1 ** ADAM_STEP
    c2 = 1.0 - ADAM_B2 ** ADAM_STEP

    def body(g_ref, w_ref, m_ref, v_ref, d_ref, mo_ref, vo_ref):
        gv = g_ref[...]
        mn = ADAM_B1 * m_ref[...] + (1.0 - ADAM_B1) * gv
        vn = ADAM_B2 * v_ref[...] + (1.0 - ADAM_B2) * (gv * gv)
        mo_ref[...] = mn
        vo_ref[...] = vn
        d_ref[...] = -ADAM_LR * ((mn / c1) / (jnp.sqrt(vn / c2) + ADAM_EPS) + ADAM_WD * w_ref[...])

    blk = pl.BlockSpec((None, tr, C), lambda l, i: (l, i, 0))
    shp = jax.ShapeDtypeStruct((L, R, C), F32)
    outs = pl.pallas_call(body, grid=(L, R // tr), in_specs=[blk, blk, blk, blk], out_specs=[blk, blk, blk], out_shape=[shp, shp, shp],
                          compiler_params=_params(("arbitrary", "arbitrary")), name=name)(*[a.reshape(view) for a in (g, w, m, v)])
    return [o.reshape(shape) for o in outs]


def _pad_flat(a, n):
    a = a.reshape(-1)
    return jnp.pad(a, (0, n - a.shape[0]))


def _seg(n):
    return -(-n // FLAT_ALIGN) * FLAT_ALIGN


def _to_blocks(full, axis):
    shp = full.shape
    return jnp.moveaxis(full.reshape(shp[:axis] + (N_DEV, shp[axis] // N_DEV) + shp[axis + 1:]), axis, 0)


def _from_blocks(blocks, axis):
    b = jnp.moveaxis(blocks, 0, axis)
    shp = b.shape
    return b.reshape(shp[:axis] + (shp[axis] * shp[axis + 1],) + shp[axis + 2:])


def _as_rows(shard, n):
    return shard.T if SHARD_AXIS[n] == 2 else shard


def _with_own(lands, own, me):
    return [lax.dynamic_update_slice(land, o[None], (me, 0, 0)) for land, o in zip(lands, own)]


def kernel(x, mem, positions, norm_mix_pre, norm_mix_post, w_in, pool_w, pool_scale, conv_b_w, w_branch_a, w_branch_b, w_branch_c, w_out, norm_mem_pre, norm_mem_post, norm_memkv, w_mq, w_mkv, w_mo, norm_ffn_pre, norm_ffn_post, w_up, conv_ffn_w, w_down, loss_target, m_norm_mix_pre, m_norm_mix_post, m_w_in, m_pool_w, m_pool_scale, m_conv_b_w, m_w_branch_a, m_w_branch_b, m_w_branch_c, m_w_out, m_norm_mem_pre, m_norm_mem_post, m_norm_memkv, m_w_mq, m_w_mkv, m_w_mo, m_norm_ffn_pre, m_norm_ffn_post, m_w_up, m_conv_ffn_w, m_w_down, v_norm_mix_pre, v_norm_mix_post, v_w_in, v_pool_w, v_pool_scale, v_conv_b_w, v_w_branch_a, v_w_branch_b, v_w_branch_c, v_w_out, v_norm_mem_pre, v_norm_mem_post, v_norm_memkv, v_w_mq, v_w_mkv, v_w_mo, v_norm_ffn_pre, v_norm_ffn_post, v_w_up, v_conv_ffn_w, v_w_down):
    w = dict(norm_mix_pre=norm_mix_pre, norm_mix_post=norm_mix_post, w_in=w_in, pool_w=pool_w, pool_scale=pool_scale, conv_b_w=conv_b_w, w_branch_a=w_branch_a, w_branch_b=w_branch_b, w_branch_c=w_branch_c, w_out=w_out, norm_mem_pre=norm_mem_pre, norm_mem_post=norm_mem_post, norm_memkv=norm_memkv, w_mq=w_mq, w_mkv=w_mkv, w_mo=w_mo, norm_ffn_pre=norm_ffn_pre, norm_ffn_post=norm_ffn_post, w_up=w_up, conv_ffn_w=conv_ffn_w, w_down=w_down)
    m = dict(norm_mix_pre=m_norm_mix_pre, norm_mix_post=m_norm_mix_post, w_in=m_w_in, pool_w=m_pool_w, pool_scale=m_pool_scale, conv_b_w=m_conv_b_w, w_branch_a=m_w_branch_a, w_branch_b=m_w_branch_b, w_branch_c=m_w_branch_c, w_out=m_w_out, norm_mem_pre=m_norm_mem_pre, norm_mem_post=m_norm_mem_post, norm_memkv=m_norm_memkv, w_mq=m_w_mq, w_mkv=m_w_mkv, w_mo=m_w_mo, norm_ffn_pre=m_norm_ffn_pre, norm_ffn_post=m_norm_ffn_post, w_up=m_w_up, conv_ffn_w=m_conv_ffn_w, w_down=m_w_down)
    v = dict(norm_mix_pre=v_norm_mix_pre, norm_mix_post=v_norm_mix_post, w_in=v_w_in, pool_w=v_pool_w, pool_scale=v_pool_scale, conv_b_w=v_conv_b_w, w_branch_a=v_w_branch_a, w_branch_b=v_w_branch_b, w_branch_c=v_w_branch_c, w_out=v_w_out, norm_mem_pre=v_norm_mem_pre, norm_mem_post=v_norm_mem_post, norm_memkv=v_norm_memkv, w_mq=v_w_mq, w_mkv=v_w_mkv, w_mo=v_w_mo, norm_ffn_pre=v_norm_ffn_pre, norm_ffn_post=v_norm_ffn_post, w_up=v_w_up, conv_ffn_w=v_conv_ffn_w, w_down=v_w_down)

    mix_big = [n for n in BIG if n not in LATE_BIG]
    block = lambda names, l: [_as_rows(w[n][l], n).astype(MXU) for n in names]
    conv = jnp.concatenate([_pad_flat(w[n], _seg(w[n].size)) for n in F32_GATHERED]).reshape(-1, LANES)
    groups = dict(m=MERGE_BIG, b=LATE_BIG, a=mix_big)
    got0 = _all_gather("weights_all_gather_0", block(['w_in'], 0) + [conv])
    conv_all = got0[-1].reshape(N_DEV, -1)
    small, off = {n: w[n] for n in WEIGHTS if n not in SHARD_AXIS}, 0
    for n in F32_GATHERED:
        small[n] = _from_blocks(conv_all[:, off:off + w[n].size].reshape((N_DEV,) + w[n].shape), 2)
        off += _seg(w[n].size)
    whole = lambda names, got: {n: o.reshape(-1, o.shape[-1]) for n, o in zip(names, got)}
    pushes, after = {}, got0[0]
    for tag, l in (('m', 0), ('b', 0), ('a', 1), ('b', 1)):
        pushes[tag, l] = _push_start(f"weights_push_start_{l}{tag}", block(groups[tag], l), False, after)
        after = pushes[tag, l][4]

    def arrived(tag, l, done):
        return _big_operands(whole(groups[tag], _push_wait(f"weights_push_wait_{l}{tag}", pushes[tag, l], False, done)))

    ctab, stab = _rope_tables(positions[0])
    W0 = _layer_weights(whole(['w_in'], got0), small, 0)
    sv0 = _layer_fwd_branches(x[0], dict(W0, g_mix_pre=W0['g_mix_pre'] + after[0, 0]), ctab, stab)
    W0.update(arrived('m', 0, sv0['att']))
    sv0 = _layer_fwd_merge(W0, sv0)
    W0.update(arrived('b', 0, sv0['x1']))
    x1 = _layer_fwd_late(mem[0], W0, sv0)
    W1 = _layer_weights({}, small, 1)
    W1.update(arrived('a', 1, x1))
    sv1 = _layer_fwd_mix(x1, W1, ctab, stab)
    W1.update(arrived('b', 1, sv1['x1']))
    x2 = _layer_fwd_late(mem[0], W1, sv1)
    dx, acc = _loss_head(x2, loss_target[0])
    loss = lax.psum(jnp.sum(acc) * (0.5 / D), MESH_AXES)
    grads = [None] * DEPTH
    dx, grads[1] = _layer_bwd(dx, mem[0], W1, sv1, ctab, stab)
    sent = [None, [grads[1][n].reshape(N_DEV, -1, grads[1][n].shape[-1]) for n in BIG]]
    push_g = _push_start("grads_push_start_1", sent[1], True, dx)
    dx, g_late = _layer_bwd_late(dx, mem[0], dict(W0, g_ffn_post=W0['g_ffn_post'] + push_g[4][0, 0]), sv0)
    sent_late = [g_late[n].reshape(N_DEV, -1, g_late[n].shape[-1]) for n in LATE_BIG]
    push_l = _push_start("grads_push_start_0", sent_late, True, dx)
    parts, g_mix = _layer_bwd_merge(dx, dict(W0, g_mix_post=W0['g_mix_post'] + push_l[4][0, 0]), sv0)
    sent_merge = [g_mix[n].reshape(N_DEV, -1, g_mix[n].shape[-1]) for n in MERGE_BIG]
    push_m = _push_start("grads_push_start_0m", sent_merge, True, parts[0])
    du, g_br = _layer_bwd_branches(parts, dict(W0, pool_scale=W0['pool_scale'] + push_m[4][0, 0]), sv0, ctab, stab)
    g_mix.update(g_br)
    g_mix['w_in'] = _dw_in(du, sv0)
    sent_in = [g_mix['w_in'].reshape(N_DEV, -1, D)]
    push_i = _push_start("grads_push_start_in", sent_in, True, du)
    dx, g_mix['norm_mix_pre'] = _mix_pre_bwd(dx, du, dict(W0, g_mix_pre=W0['g_mix_pre'] + push_i[4][0, 0]), sv0)
    grads[0] = {**g_late, **g_mix}
    recv1 = _push_wait("grads_push_wait_1", push_g, True, dx)
    recv_late = _push_wait("grads_push_wait_0", push_l, True, dx)
    recv_merge = _push_wait("grads_push_wait_0m", push_m, True, dx)

    misc_names = [n for n in WEIGHTS if n not in BIG]
    stacked = {n: jnp.stack([grads[l][n].reshape(small[n].shape[1:]) for l in range(DEPTH)]) for n in misc_names}
    rows = [(_to_blocks(stacked[n], 2) if n in SHARD_AXIS else jnp.broadcast_to(stacked[n][None], (N_DEV,) + stacked[n].shape))
            for n in misc_names]
    segs = [_seg(w[n].size) for n in misc_names]
    misc = jnp.concatenate([jnp.pad(r.reshape(N_DEV, -1), ((0, 0), (0, s - r[0].size))) for r, s in zip(rows, segs)],
                           axis=1).reshape(N_DEV, -1, LANES)
    push_x = _push_start("grads_push_start_small", [misc], True, dx)
    g_out, per_layer = {}, {}
    for l, names, recv in ((1, BIG, recv1), (0, LATE_BIG, recv_late), (0, MERGE_BIG, recv_merge)):
        for n, r in zip(names, recv):
            per_layer[n, l] = _sum_slots(f"sum_{n}_{l}", r)

    swap = lambda a: jnp.swapaxes(a, 1, 2)

    def update(n):
        if n not in BIG:
            return [g_out[n], *_adamw(f"adamw_{n}", g_out[n], w[n], m[n], v[n])]
        g = jnp.stack([per_layer[n, l] for l in range(DEPTH)])
        if SHARD_AXIS[n] == 2 and w[n].shape[2] % LANES:
            return [swap(a) for a in (g, *_adamw(f"adamw_{n}", g, swap(w[n]), swap(m[n]), swap(v[n])))]
        g = swap(g) if SHARD_AXIS[n] == 2 else g
        return [g, *_adamw(f"adamw_{n}", g, w[n], m[n], v[n])]

    done = {n: update(n) for n in BIG if n != 'w_in'}
    recv_in = _push_wait("grads_push_wait_in", push_i, True, done[BIG[-1]][1])
    per_layer['w_in', 0] = _sum_slots("sum_w_in_0", recv_in[0])
    done['w_in'] = update('w_in')
    misc_sum = _sum_slots("sum_misc", _push_wait("grads_push_wait_small", push_x, True, done['w_in'][1])[0]).reshape(-1)
    off = 0
    for n, s in zip(misc_names, segs):
        g_out[n] = misc_sum[off:off + w[n].size].reshape(w[n].shape)
        done[n] = update(n)
        off += s
    return (loss, dx[None], *[done[n][k] for k in range(4) for n in WEIGHTS])
```

```python
import jax
import jax.numpy as jnp
from jax import lax
from jax.experimental import pallas as pl
from jax.experimental.pallas import tpu as pltpu

F32 = jnp.float32
MXU = jnp.bfloat16

D = 1024
DEPTH = 2
POOLW = 384
ATT_W = 768
ATT_O = 256
GATE_W = 3 * D
IN_W = 6912
IN_TILE = 768
IN_ROT = (IN_W - GATE_W) // IN_TILE
MEM_W = 512
D_FF = 2816
EPS = 1e-6
ROPE_THETA = 500000.0
QB = 128
DILS = (1, 4, 16)
NEG = -1e30
MEM_SCALE = 128 ** -0.5
ATT_SCALE = 0.125

ADAM_LR, ADAM_B1, ADAM_B2, ADAM_EPS, ADAM_WD, ADAM_STEP = 0.001, 0.9, 0.999, 1e-08, 0.01, 10

N_DEV = 8
MESH_AXES = ("x", "y", "c")
LANES = 128
FLAT_ALIGN = 2048

WEIGHTS = ['norm_mix_pre', 'norm_mix_post', 'w_in', 'pool_w', 'pool_scale', 'conv_b_w', 'w_branch_a', 'w_branch_b',
           'w_branch_c', 'w_out', 'norm_mem_pre', 'norm_mem_post', 'norm_memkv', 'w_mq', 'w_mkv', 'w_mo',
           'norm_ffn_pre', 'norm_ffn_post', 'w_up', 'conv_ffn_w', 'w_down']
SHARD_AXIS = {'w_in': 2, 'conv_b_w': 2, 'w_branch_a': 2, 'w_branch_b': 2, 'w_branch_c': 2, 'w_out': 1, 'w_mq': 1,
              'w_mkv': 1, 'w_mo': 2, 'w_up': 2, 'conv_ffn_w': 2, 'w_down': 1}
F32_GATHERED = ('conv_b_w', 'conv_ffn_w')
BIG = [n for n in WEIGHTS if n in SHARD_AXIS and n not in F32_GATHERED]
LATE_BIG = ['w_mq', 'w_mkv', 'w_mo', 'w_up', 'w_down']
MERGE_BIG = ['w_branch_a', 'w_branch_b', 'w_branch_c', 'w_out']


VMEM_LIMIT_MB = 60


def _params(sem):
    return pltpu.CompilerParams(dimension_semantics=sem, vmem_limit_bytes=VMEM_LIMIT_MB << 20)


def _dot(a, b, prec=None):
    return lax.dot_general(a, b, (((1,), (0,)), ((), ())), preferred_element_type=F32, precision=prec)


def _dot_nt(a, b, prec=None):
    return lax.dot_general(a, b, (((1,), (1,)), ((), ())), preferred_element_type=F32, precision=prec)


def _dot_tn(a, b, prec=None):
    return lax.dot_general(a, b, (((0,), (0,)), ((), ())), preferred_element_type=F32, precision=prec)


def _tile(n, cap):
    if n <= cap:
        return n
    best = None
    for t in range(LANES, cap + 1, LANES):
        if n % t == 0:
            best = t
    assert best is not None, (n, cap)
    return best


def _rms(x, g):
    r = lax.rsqrt(jnp.mean(x * x, axis=-1, keepdims=True) + EPS)
    return x * r * g, r


def _rms_bwd(w, y):
    r = lax.rsqrt(jnp.mean(y * y, axis=-1, keepdims=True) + EPS)
    return r * w - y * (r * r * r) * jnp.mean(w * y, axis=-1, keepdims=True), r


def _rows_call(name, body, n_rows, ts, ins, outs, scratch=(), reverse=False, aliases=None):
    nt = n_rows // ts
    assert nt * ts == n_rows

    def tile_of(g):
        return (nt - 1 - g) if reverse else g

    in_specs, args = [], []
    for op in ins:
        if op[0] == "t":
            _, a, cw, cb = op
            in_specs.append(pl.BlockSpec((ts, cw), lambda g, cb=cb: (tile_of(g), cb)))
        elif op[0] == "h":
            _, a, hr, cw, cb = op
            in_specs.append(pl.BlockSpec((hr, cw), lambda g, cb=cb, k=ts // hr: (jnp.maximum(tile_of(g) * k - 1, 0), cb)))
        elif op[0] == "x":
            _, a = op
            in_specs.append(pl.BlockSpec(memory_space=pl.ANY))
        else:
            _, a = op
            in_specs.append(pl.BlockSpec(a.shape, lambda g, n=a.ndim: (0,) * n))
        args.append(a)
    out_specs, out_shape = [], []
    for op in outs:
        if op[0] == "t":
            _, cols, dt = op
            out_specs.append(pl.BlockSpec((ts, cols), lambda g: (tile_of(g), 0)))
            out_shape.append(jax.ShapeDtypeStruct((n_rows, cols), dt))
        elif op[0] == "c":
            _, total, cols, cb, dt = op
            out_specs.append(pl.BlockSpec((ts, cols), lambda g, cb=cb: (tile_of(g), cb)))
            out_shape.append(jax.ShapeDtypeStruct((n_rows, total), dt))
        else:
            _, shp, dt = op
            out_specs.append(pl.BlockSpec(shp, lambda g, n=len(shp): (0,) * n))
            out_shape.append(jax.ShapeDtypeStruct(shp, dt))

    def kern(*refs):
        g = pl.program_id(0)
        body(tile_of(g), g, *refs)

    return pl.pallas_call(kern, grid=(nt,), in_specs=in_specs, out_specs=out_specs, out_shape=out_shape,
                          scratch_shapes=list(scratch), input_output_aliases=aliases or {},
                          compiler_params=_params(("arbitrary",)), name=name)(*args)


def _acc(ref, g, val):
    @pl.when(g == 0)
    def _():
        ref[...] = val

    @pl.when(g != 0)
    def _():
        ref[...] += val


def _norm_mm(name, x, g, w, ts, tn, out_dtype=F32, wt=False, rot=0):
    S, K = x.shape
    N = w.shape[0] if wt else w.shape[1]
    assert wt or not rot

    def body(x_ref, g_ref, w_ref, o_ref, h_ref, hs):
        @pl.when(pl.program_id(1) == 0)
        def _():
            h, _ = _rms(x_ref[...], g_ref[...])
            hs[...] = h.astype(MXU)
            h_ref[...] = h.astype(MXU)

        o_ref[...] = (_dot_nt if wt else _dot)(hs[...], w_ref[...]).astype(out_dtype)

    w_spec = pl.BlockSpec((tn, K), lambda i, j: ((j + rot) % (N // tn), 0)) if wt else pl.BlockSpec((K, tn), lambda i, j: (0, j))
    return pl.pallas_call(
        body, grid=(S // ts, N // tn),
        in_specs=[pl.BlockSpec((ts, K), lambda i, j: (i, 0)), pl.BlockSpec((1, K), lambda i, j: (0, 0)), w_spec],
        out_specs=[pl.BlockSpec((ts, tn), lambda i, j: (i, j)), pl.BlockSpec((ts, K), lambda i, j: (i, 0))],
        out_shape=[jax.ShapeDtypeStruct((S, N), out_dtype), jax.ShapeDtypeStruct((S, K), MXU)],
        scratch_shapes=[pltpu.VMEM((ts, K), MXU)],
        compiler_params=_params(("arbitrary", "arbitrary")), name=name)(x, g, w)


def _mm_nt(name, a, b, ts, tn, out_dtype=F32):
    M, K = a.shape
    N = b.shape[0]

    def body(a_ref, b_ref, o_ref):
        o_ref[...] = _dot_nt(a_ref[...], b_ref[...]).astype(out_dtype)

    return pl.pallas_call(
        body, grid=(M // ts, N // tn),
        in_specs=[pl.BlockSpec((ts, K), lambda i, j: (i, 0)), pl.BlockSpec((tn, K), lambda i, j: (j, 0))],
        out_specs=pl.BlockSpec((ts, tn), lambda i, j: (i, j)), out_shape=jax.ShapeDtypeStruct((M, N), out_dtype),
        compiler_params=_params(("arbitrary", "arbitrary")), name=name)(a, b)


def _mm_tn(name, a, b, cap_k=512, cap_n=1024, out_dtype=MXU, rot=0):
    S, K = a.shape
    N = b.shape[1]
    tk, tn = _tile(K, cap_k), _tile(N, cap_n)

    def body(a_ref, b_ref, o_ref):
        o_ref[...] = _dot_tn(a_ref[...], b_ref[...]).astype(out_dtype)

    return pl.pallas_call(
        body, grid=(K // tk, N // tn),
        in_specs=[pl.BlockSpec((S, tk), lambda i, j: (0, i)), pl.BlockSpec((S, tn), lambda i, j: (0, j))],
        out_specs=pl.BlockSpec((tk, tn), lambda i, j: ((i + rot) % (K // tk), j)), out_shape=jax.ShapeDtypeStruct((K, N), out_dtype),
        compiler_params=_params(("arbitrary", "arbitrary")), name=name)(a, b)


def _pool_cols(shape):
    col = lax.broadcasted_iota(jnp.int32, shape, 1)
    return col < 96, col < 192, col < 288


def _pool_select(s2, s4, s8, s16):
    c1, c2, c3 = _pool_cols(s2.shape)
    return jnp.where(c1, s2, jnp.where(c2, s4, jnp.where(c3, s8, s16)))


def _pool_cnt(t0, ts):
    c1, c2, c3 = _pool_cols((ts, POOLW))
    win = jnp.where(c1, 2, jnp.where(c2, 4, jnp.where(c3, 8, 16)))
    t = t0 + lax.broadcasted_iota(jnp.int32, (ts, POOLW), 0)
    return jnp.minimum(t + 1, win).astype(F32)


def _pooled(a, prev, t0):
    ts = a.shape[0]
    ext = jnp.concatenate([prev, a], axis=0)
    s2 = ext + pltpu.roll(ext, 1, axis=0)
    s4 = s2 + pltpu.roll(s2, 2, axis=0)
    s8 = s4 + pltpu.roll(s4, 4, axis=0)
    s16 = s8 + pltpu.roll(s8, 8, axis=0)
    sums = _pool_select(s2, s4, s8, s16)[16:]
    return sums / _pool_cnt(t0, ts) - a


def _conv3(z, prev8, w):
    ext = jnp.concatenate([prev8, z], axis=0)
    z1 = pltpu.roll(ext, 1, axis=0)[8:]
    z2 = pltpu.roll(ext, 2, axis=0)[8:]
    return w[0:1] * z2 + w[1:2] * z1 + w[2:3] * z, z1, z2


def _conv3_t(dc, next8, w, shifted=False):
    ts = dc.shape[0]
    ext = jnp.concatenate([dc, next8], axis=0)
    n = ts + 8
    u1 = pltpu.roll(ext, n - 1, axis=0)[:ts]
    u2 = pltpu.roll(ext, n - 2, axis=0)[:ts]
    out = w[2:3] * dc + w[1:2] * u1 + w[0:1] * u2
    return (out, u1, u2) if shifted else out


def _poolconv_fwd(u, wblk, pool_scale, conv_b, ts=512):
    S = u.shape[0]

    def body(i, g, a_ref, bx_ref, bb_ref, bc_ref, wblk_ref, ps_ref, cw_ref, a2_ref, yb_ref, ca, cz):
        @pl.when(g == 0)
        def _():
            ca[...] = jnp.zeros_like(ca)
            cz[...] = jnp.zeros_like(cz)

        a = a_ref[...].astype(F32)
        p = _pooled(a, ca[...], i * ts)
        mixed = _dot(p.astype(MXU), wblk_ref[...])
        a2_ref[...] = (mixed * ps_ref[...]).astype(MXU)
        z = bc_ref[...].astype(F32) * bx_ref[...].astype(F32)
        conv, _, _ = _conv3(z, cz[...], cw_ref[...])
        yb_ref[...] = (bb_ref[...].astype(F32) * conv).astype(MXU)
        ca[...] = a[ts - 16:]
        cz[...] = z[ts - 8:]

    ins = [("t", u, POOLW, 8), ("t", u, POOLW, 9), ("t", u, POOLW, 10), ("t", u, POOLW, 11), ("w", wblk), ("w", pool_scale),
           ("w", conv_b)]
    return _rows_call("poolconv_fwd", body, S, ts, ins, [("t", POOLW, MXU), ("t", POOLW, MXU)],
                      scratch=[pltpu.VMEM((16, POOLW), F32), pltpu.VMEM((8, POOLW), F32)])


def _poolconv_bwd(u, d_a2, d_yb, du, wblk, pool_scale, conv_b, ts=512):
    S = u.shape[0]

    def body(i, g, a_ref, bx_ref, bb_ref, bc_ref, ap_ref, bxp_ref, bcp_ref, da2_ref, dyb_ref, wblk_ref, ps_ref, cw_ref, _,
             o_ref, dps_ref, dwb_ref, dcw_ref, ce, cdz):
        @pl.when(g == 0)
        def _():
            ce[...] = jnp.zeros_like(ce)
            cdz[...] = jnp.zeros_like(cdz)

        first = (i > 0).astype(F32)
        a = a_ref[...].astype(F32)
        p = _pooled(a, ap_ref[...].astype(F32) * first, i * ts)
        pb = p.astype(MXU)
        mixed = _dot(pb, wblk_ref[...])
        da2 = da2_ref[...]
        dmixed = (da2 * ps_ref[...]).astype(MXU)
        dp = _dot_nt(dmixed, wblk_ref[...])
        _acc(dps_ref, g, jnp.sum(da2 * mixed, axis=0, keepdims=True))
        _acc(dwb_ref, g, _dot_tn(pb, dmixed))
        e = dp / _pool_cnt(i * ts, ts)
        ext = jnp.concatenate([e, ce[...]], axis=0)
        n = ts + 16
        f2 = ext + pltpu.roll(ext, n - 1, axis=0)
        f4 = f2 + pltpu.roll(f2, n - 2, axis=0)
        f8 = f4 + pltpu.roll(f4, n - 4, axis=0)
        f16 = f8 + pltpu.roll(f8, n - 8, axis=0)
        o_ref[:, 0:POOLW] = (_pool_select(f2, f4, f8, f16)[:ts] - dp).astype(o_ref.dtype)
        ce[...] = e[:16]

        bx, bb, bc = bx_ref[...].astype(F32), bb_ref[...].astype(F32), bc_ref[...].astype(F32)
        z = bc * bx
        w = cw_ref[...]
        conv, z1, z2 = _conv3(z, (bxp_ref[...].astype(F32) * bcp_ref[...].astype(F32))[8:16] * first, w)
        dyb = dyb_ref[...]
        dconv = dyb * bb
        dz = _conv3_t(dconv, cdz[...], w)
        o_ref[:, POOLW:2 * POOLW] = (dz * bc).astype(o_ref.dtype)
        o_ref[:, 2 * POOLW:3 * POOLW] = (dyb * conv).astype(o_ref.dtype)
        o_ref[:, 3 * POOLW:4 * POOLW] = (dz * bx).astype(o_ref.dtype)
        dw = jnp.concatenate([jnp.sum(dconv * z2, axis=0, keepdims=True), jnp.sum(dconv * z1, axis=0, keepdims=True),
                              jnp.sum(dconv * z, axis=0, keepdims=True)], axis=0)
        _acc(dcw_ref, g, dw)
        cdz[...] = dconv[:8]

    ins = [("t", u, POOLW, 8), ("t", u, POOLW, 9), ("t", u, POOLW, 10), ("t", u, POOLW, 11),
           ("h", u, 16, POOLW, 8), ("h", u, 16, POOLW, 9), ("h", u, 16, POOLW, 11),
           ("t", d_a2, POOLW, 0), ("t", d_yb, POOLW, 0), ("w", wblk), ("w", pool_scale), ("w", conv_b), ("x", du)]
    outs = [("c", IN_W, 4 * POOLW, GATE_W // (4 * POOLW), MXU), ("a", (1, POOLW), F32), ("a", (POOLW, POOLW), F32), ("a", (3, POOLW), F32)]
    return _rows_call("poolconv_bwd", body, S, ts, ins, outs, aliases={len(ins) - 1: 0},
                      scratch=[pltpu.VMEM((16, POOLW), F32), pltpu.VMEM((8, POOLW), F32)], reverse=True)


def _rope_tables(positions):
    S = positions.shape[0]
    inv = ROPE_THETA ** (-jnp.arange(0, 16, 2, dtype=F32) / 16)
    ang = positions.astype(F32)[:, None] * inv
    cos, sin = jnp.cos(ang), jnp.sin(ang)
    c64 = jnp.concatenate([cos, cos, jnp.ones((S, 48), F32)], axis=1)
    s64 = jnp.concatenate([-sin, sin, jnp.zeros((S, 48), F32)], axis=1)
    return jnp.concatenate([c64, c64], axis=1), jnp.concatenate([s64, s64], axis=1)


def _partner(x):
    lane = lax.broadcasted_iota(jnp.int32, x.shape, 1) % 64
    return jnp.where(lane < 8, pltpu.roll(x, LANES - 8, axis=1), jnp.where(lane < 16, pltpu.roll(x, 8, axis=1), 0.0))


def _rope(x, c, s):
    return x * c + _partner(x) * s


def _rope_t(x, c, s):
    return x * c + _partner(x * s)


def _rows_of(r, n, d):
    return pl.ds(r, n, stride=d) if d > 1 else pl.ds(0, n)


def _head_masks(shape):
    lane = lax.broadcasted_iota(jnp.int32, shape, 1) // 64
    return [lane == h for h in range(4)]


def _only(mask, x):
    return jnp.where(mask, x, jnp.zeros_like(x))


def _rope_perm(u, ctab, stab, ts=512):
    S = u.shape[0]
    nch = ATT_W // LANES

    def body(*refs):
        chunks, (c_ref, s_ref), outs, scr = refs[:3 * nch], refs[3 * nch:3 * nch + 2], refs[3 * nch + 2:-1], refs[-1]
        for k in range(3 * nch):
            scr[k] = chunks[k][...].astype(F32)
        for g, d in enumerate(DILS):
            n = ts // d
            for r in range(d):
                rows = _rows_of(r, n, d)
                c, s = c_ref[rows, :], s_ref[rows, :]
                for which in range(3):
                    parts = [scr.at[which * nch + j][rows, :] for j in (2 * g, 2 * g + 1)]
                    if which < 2:
                        parts = [_rope(x, c, s) for x in parts]
                    outs[which * 3 + g][r] = jnp.concatenate(parts, axis=1).astype(MXU)

    base = (IN_W - 3 * ATT_W) // LANES
    in_specs = [pl.BlockSpec((ts, LANES), lambda i, cb=base + k: (i, cb)) for k in range(3 * nch)]
    in_specs += [pl.BlockSpec((ts, LANES), lambda i: (i, 0))] * 2
    out_specs = [pl.BlockSpec((d, ts // d, ATT_O), lambda i: (0, i, 0)) for _ in range(3) for d in DILS]
    out_shape = [jax.ShapeDtypeStruct((d, S // d, ATT_O), MXU) for _ in range(3) for d in DILS]
    res = pl.pallas_call(body, grid=(S // ts,), in_specs=in_specs, out_specs=out_specs, out_shape=out_shape,
                         scratch_shapes=[pltpu.VMEM((3 * nch, ts, LANES), F32)],
                         compiler_params=_params(("arbitrary",)), name="rope_perm")(*([u] * (3 * nch)), ctab, stab)
    return [[res[which * 3 + g].reshape(S, ATT_O) for g in range(3)] for which in range(3)]


def _rope_unperm_bwd(dqkv, du, ctab, stab, ts=512):
    S = dqkv[0][0].shape[0]
    nch = ATT_W // LANES

    def body(*refs):
        ins, (c_ref, s_ref, _, o_ref, scr) = refs[:9], refs[9:]
        for g, d in enumerate(DILS):
            n = ts // d
            for r in range(d):
                rows = _rows_of(r, n, d)
                c, s = c_ref[rows, :], s_ref[rows, :]
                for which in range(3):
                    v = ins[which * 3 + g][r]
                    for half in range(2):
                        x = v[:, half * LANES:(half + 1) * LANES]
                        scr.at[which * nch + 2 * g + half][rows, :] = _rope_t(x, c, s) if which < 2 else x
        for j in range(3 * nch):
            o_ref[:, j * LANES:(j + 1) * LANES] = scr[j].astype(o_ref.dtype)

    in_specs = [pl.BlockSpec((d, ts // d, ATT_O), lambda i: (0, i, 0)) for _ in range(3) for d in DILS]
    in_specs += [pl.BlockSpec((ts, LANES), lambda i: (i, 0))] * 2 + [pl.BlockSpec(memory_space=pl.ANY)]
    args = [dqkv[which][g].reshape(d, S // d, ATT_O) for which in range(3) for g, d in enumerate(DILS)]
    last = (IN_W - 3 * ATT_W) // (3 * ATT_W)
    return pl.pallas_call(body, grid=(S // ts,), in_specs=in_specs, out_specs=pl.BlockSpec((ts, 3 * ATT_W), lambda i: (i, last)),
                          out_shape=jax.ShapeDtypeStruct((S, IN_W), MXU), scratch_shapes=[pltpu.VMEM((3 * nch, ts, LANES), F32)],
                          input_output_aliases={len(in_specs) - 1: 0},
                          compiler_params=_params(("arbitrary",)), name="rope_unperm_bwd")(*args, ctab, stab, du)


def _band_mask_keys(has_prev):
    r = lax.broadcasted_iota(jnp.int32, (QB, 2 * QB), 0)
    c = lax.broadcasted_iota(jnp.int32, (QB, 2 * QB), 1)
    return ((c < QB) & (c >= r) & has_prev) | ((c >= QB) & (c - QB <= r))


def _band_mask_queries(has_next):
    r = lax.broadcasted_iota(jnp.int32, (2 * QB, QB), 0)
    c = lax.broadcasted_iota(jnp.int32, (2 * QB, QB), 1)
    return ((r < QB) & (c <= r)) | ((r >= QB) & (c >= r - QB) & has_next)


ASUB = 4
_BIG = pl.BlockSpec((ASUB * QB, ATT_O), lambda b: (b, 0))
_PREV = pl.BlockSpec((QB, ATT_O), lambda b: (jnp.maximum(b * ASUB - 1, 0), 0))


def _sub(ref, j):
    return ref[j * QB:(j + 1) * QB]


def _attn_fwd(g, q, k, v):
    S = q.shape[0]
    nb = S // QB
    nblk = nb // DILS[g]

    def body(q_ref, kc_ref, kp_ref, vc_ref, vp_ref, o_ref, m_ref, l_ref):
        hm_kv, hm_o = _head_masks((2 * QB, ATT_O)), _head_masks((QB, ATT_O))
        for j in range(ASUB):
            ok = _band_mask_keys(((pl.program_id(0) * ASUB + j) & (nblk - 1)) > 0)
            k2 = jnp.concatenate([kp_ref[...] if j == 0 else _sub(kc_ref, j - 1), _sub(kc_ref, j)], axis=0)
            v2 = jnp.concatenate([vp_ref[...] if j == 0 else _sub(vc_ref, j - 1), _sub(vc_ref, j)], axis=0)
            qv = _sub(q_ref, j)
            o_acc = jnp.zeros((QB, ATT_O), F32)
            m_acc = jnp.zeros((QB, ATT_O), F32)
            l_acc = jnp.zeros((QB, ATT_O), F32)
            for h in range(4):
                s = jnp.where(ok, _dot_nt(qv, _only(hm_kv[h], k2)) * ATT_SCALE, NEG)
                m = jnp.max(s, axis=1, keepdims=True)
                p = jnp.exp(s - m)
                o_acc = o_acc + _dot(p.astype(MXU), _only(hm_kv[h], v2))
                m_acc = jnp.where(hm_o[h], m, m_acc)
                l_acc = jnp.where(hm_o[h], jnp.sum(p, axis=1, keepdims=True), l_acc)
            o_ref[j * QB:(j + 1) * QB] = o_acc
            m_ref[j * QB:(j + 1) * QB] = m_acc
            l_ref[j * QB:(j + 1) * QB] = l_acc

    shp = jax.ShapeDtypeStruct((S, ATT_O), F32)
    return pl.pallas_call(body, grid=(nb // ASUB,), in_specs=[_BIG, _BIG, _PREV, _BIG, _PREV],
                          out_specs=[_BIG] * 3, out_shape=[shp, shp, shp], compiler_params=_params(("arbitrary",)),
                          name=f"attn_fwd_{g}")(q, k, k, v, v)


def _natural(ref, d, scr, ts):
    if d == 1:
        return ref[0]
    n = ts // d
    for r in range(d):
        v = ref[r]
        scr.at[0][pl.ds(r, n, stride=d), :] = v[:, 0:LANES]
        scr.at[1][pl.ds(r, n, stride=d), :] = v[:, LANES:2 * LANES]
    return jnp.concatenate([scr[0], scr[1]], axis=1)


def _attn_combine(oml, ts=512):
    S = oml[0][0].shape[0]

    def body(*refs):
        ins, (att_ref, out_ref, lse_ref, scr) = refs[:9], refs[9:]
        o, m, l = [[_natural(ins[3 * g + k], d, scr, ts) for g, d in enumerate(DILS)] for k in range(3)]
        mx = jnp.maximum(jnp.maximum(m[0], m[1]), m[2])
        w = [jnp.exp(m[g] - mx) for g in range(3)]
        den = w[0] * l[0] + w[1] * l[1] + w[2] * l[2]
        out = (w[0] * o[0] + w[1] * o[1] + w[2] * o[2]) / den
        out_ref[...] = out
        att_ref[...] = out.astype(MXU)
        lse_ref[...] = mx + jnp.log(den)

    in_specs = [pl.BlockSpec((d, ts // d, ATT_O), lambda i: (0, i, 0)) for d in DILS for _ in range(3)]
    args = [a.reshape(d, S // d, ATT_O) for d, grp in zip(DILS, oml) for a in grp]
    blk = pl.BlockSpec((ts, ATT_O), lambda i: (i, 0))
    return pl.pallas_call(body, grid=(S // ts,), in_specs=in_specs, out_specs=[blk, blk, blk],
                          out_shape=[jax.ShapeDtypeStruct((S, ATT_O), MXU), jax.ShapeDtypeStruct((S, ATT_O), F32),
                                     jax.ShapeDtypeStruct((S, ATT_O), F32)],
                          scratch_shapes=[pltpu.VMEM((2, ts, LANES), F32)], compiler_params=_params(("arbitrary",)),
                          name="attn_combine")(*args)


def _attn_bwd_prep(datt, o, lse, ts=512):
    S = datt.shape[0]

    def body(da0, da1, o_ref, l0, l1, *rest):
        outs, dl = rest[:9], rest[9]
        prod = jnp.concatenate([da0[...], da1[...]], axis=1) * o_ref[...]
        delta = jnp.zeros((ts, ATT_O), F32)
        for hm in _head_masks((ts, ATT_O)):
            delta = jnp.where(hm, jnp.sum(_only(hm, prod), axis=1, keepdims=True), delta)
        dl[0] = delta[:, 0:LANES]
        dl[1] = delta[:, LANES:2 * LANES]
        for g, d in enumerate(DILS):
            n = ts // d
            for r in range(d):
                rows = _rows_of(r, n, d)
                outs[g][r] = jnp.concatenate([da0[rows, :], da1[rows, :]], axis=1).astype(MXU)
                outs[3 + g][r] = jnp.concatenate([dl.at[0][rows, :], dl.at[1][rows, :]], axis=1)
                outs[6 + g][r] = jnp.concatenate([l0[rows, :], l1[rows, :]], axis=1)

    half = lambda j: pl.BlockSpec((ts, LANES), lambda i: (i, j))
    out_specs = [pl.BlockSpec((d, ts // d, ATT_O), lambda i: (0, i, 0)) for _ in range(3) for d in DILS]
    out_shape = [jax.ShapeDtypeStruct((d, S // d, ATT_O), dt) for dt in (MXU, F32, F32) for d in DILS]
    res = pl.pallas_call(body, grid=(S // ts,), in_specs=[half(0), half(1), pl.BlockSpec((ts, ATT_O), lambda i: (i, 0)), half(0), half(1)],
                         out_specs=out_specs, out_shape=out_shape, scratch_shapes=[pltpu.VMEM((2, ts, LANES), F32)],
                         compiler_params=_params(("arbitrary",)), name="attn_bwd_prep")(datt, datt, o, lse, lse)
    return [[res[k * 3 + g].reshape(S, ATT_O) for g in range(3)] for k in range(3)]


def _head_col(x, h):
    return x[:, h * 64:h * 64 + 1]


def _attn_bwd(g, q, k, v, do, delta, lse):
    S = q.shape[0]
    nb = S // QB
    nblk = nb // DILS[g]

    def body(k_ref, v_ref, qc_ref, qn_ref, doc_ref, don_ref, dlc_ref, dln_ref, lc_ref, ln_ref, dq_ref, dk_ref, dv_ref, dq_scr):
        hms, hmk = _head_masks((2 * QB, ATT_O)), _head_masks((QB, ATT_O))
        first = pl.program_id(0) == 0

        @pl.when(first)
        def _():
            dq_scr[0:QB] = jnp.zeros((QB, ATT_O), F32)

        @pl.when(jnp.logical_not(first))
        def _():
            dq_scr[0:QB] = dq_scr[ASUB * QB:(ASUB + 1) * QB]

        dq_scr[QB:(ASUB + 1) * QB] = jnp.zeros((ASUB * QB, ATT_O), F32)

        def both(cur_ref, nxt_ref, j):
            return jnp.concatenate([_sub(cur_ref, j), nxt_ref[...] if j == ASUB - 1 else _sub(cur_ref, j + 1)], axis=0)

        for j in range(ASUB):
            ok = _band_mask_queries(((pl.program_id(0) * ASUB + j + 1) & (nblk - 1)) > 0)
            q2, do2, dl2, lse2 = both(qc_ref, qn_ref, j), both(doc_ref, don_ref, j), both(dlc_ref, dln_ref, j), both(lc_ref, ln_ref, j)
            kv, vv = _sub(k_ref, j), _sub(v_ref, j)
            dk = jnp.zeros((QB, ATT_O), F32)
            dv = jnp.zeros((QB, ATT_O), F32)
            dq2 = jnp.zeros((2 * QB, ATT_O), F32)
            for h, hm in enumerate(hms):
                qh, doh = _only(hm, q2), _only(hm, do2)
                p = jnp.where(ok, jnp.exp(_dot_nt(qh, kv) * ATT_SCALE - _head_col(lse2, h)), 0.0)
                ds = (p * (_dot_nt(doh, vv) - _head_col(dl2, h))).astype(MXU)
                dv = dv + _dot_tn(p.astype(MXU), doh)
                dk = dk + _dot_tn(ds, qh)
                dq2 = dq2 + _dot(ds, _only(hmk[h], kv))
            dk_ref[j * QB:(j + 1) * QB] = dk * ATT_SCALE
            dv_ref[j * QB:(j + 1) * QB] = dv
            dq_scr[j * QB:(j + 2) * QB] += dq2
        dq_ref[...] = dq_scr[0:ASUB * QB] * ATT_SCALE

    nxt = pl.BlockSpec((QB, ATT_O), lambda b: (jnp.minimum((b + 1) * ASUB, nb - 1), 0))
    shp = jax.ShapeDtypeStruct((S, ATT_O), F32)
    return pl.pallas_call(body, grid=(nb // ASUB,), in_specs=[_BIG, _BIG, _BIG, nxt, _BIG, nxt, _BIG, nxt, _BIG, nxt], out_specs=[_BIG] * 3,
                          out_shape=[shp, shp, shp], scratch_shapes=[pltpu.VMEM(((ASUB + 1) * QB, ATT_O), F32)],
                          compiler_params=_params(("arbitrary",)), name=f"attn_bwd_{g}")(k, v, q, q, do, do, delta, delta, lse, lse)


def _merge_fwd(x0, u, a2, yb, att, wa, wb, wc, w_out, g_post, ts=256):
    S = x0.shape[0]

    def body(i, g, x_ref, gate_ref, a2_ref, yb_ref, att_ref, wa_ref, wb_ref, wc_ref, wo_ref, gp_ref, mg_ref, y_ref, xo_ref):
        gate = lambda n: jax.nn.sigmoid(gate_ref[:, n * D:(n + 1) * D].astype(F32))
        merged = gate(0) * _dot_nt(a2_ref[...], wa_ref[...])
        merged = merged + gate(1) * _dot_nt(yb_ref[...], wb_ref[...])
        merged = merged + gate(2) * _dot_nt(att_ref[...], wc_ref[...])
        mb = merged.astype(MXU)
        mg_ref[...] = mb
        y = _dot(mb, wo_ref[...])
        y_ref[...] = y
        xo_ref[...] = x_ref[...] + _rms(y, gp_ref[...])[0]

    ins = [("t", x0, D, 0), ("t", u, GATE_W, 0), ("t", a2, POOLW, 0), ("t", yb, POOLW, 0), ("t", att, ATT_O, 0),
           ("w", wa), ("w", wb), ("w", wc), ("w", w_out), ("w", g_post)]
    return _rows_call("merge_fwd", body, S, ts, ins, [("t", D, MXU), ("t", D, F32), ("t", D, F32)])


def _merge_bwd(dx, y1, u, a2, yb, att, merged, wa, wb, wc, w_out, g_post, ts=256):
    S = dx.shape[0]
    last = S // ts - 1

    def body(i, g, dx_ref, y_ref, gate_ref, a2_ref, yb_ref, att_ref, mg_ref, wa_ref, wb_ref, wc_ref, wo_ref, gp_ref,
             dgate_ref, da2_ref, dyb_ref, datt_ref, dgp_ref, dwo_ref, dwa_ref, dwb_ref, dwc_ref, acc_o, acc_a, acc_b, acc_c):
        @pl.when(g == 0)
        def _():
            for acc in (acc_o, acc_a, acc_b, acc_c):
                acc[...] = jnp.zeros_like(acc)

        dxv, y = dx_ref[...], y_ref[...]
        dy, r = _rms_bwd(dxv * gp_ref[...], y)
        _acc(dgp_ref, g, jnp.sum(dxv * (y * r), axis=0, keepdims=True))
        dyb16 = dy.astype(MXU)
        acc_o[...] += _dot_tn(mg_ref[...], dyb16)
        dm = _dot_nt(dyb16, wo_ref[...])
        for n, (src, w_ref, din_ref, acc) in enumerate(((a2_ref, wa_ref, da2_ref, acc_a), (yb_ref, wb_ref, dyb_ref, acc_b),
                                                       (att_ref, wc_ref, datt_ref, acc_c))):
            gt = jax.nn.sigmoid(gate_ref[:, n * D:(n + 1) * D].astype(F32))
            br = _dot_nt(src[...], w_ref[...])
            dgate_ref[:, n * D:(n + 1) * D] = (dm * br * gt * (1.0 - gt)).astype(dgate_ref.dtype)
            dbr = (dm * gt).astype(MXU)
            acc[...] += _dot_tn(dbr, src[...])
            din_ref[...] = _dot(dbr, w_ref[...])

        @pl.when(g == last)
        def _():
            for out, acc in ((dwo_ref, acc_o), (dwa_ref, acc_a), (dwb_ref, acc_b), (dwc_ref, acc_c)):
                out[...] = acc[...].astype(MXU)

    ins = [("t", dx, D, 0), ("t", y1, D, 0), ("t", u, GATE_W, 0), ("t", a2, POOLW, 0), ("t", yb, POOLW, 0), ("t", att, ATT_O, 0),
           ("t", merged, D, 0), ("w", wa), ("w", wb), ("w", wc), ("w", w_out), ("w", g_post)]
    wshapes = [(D, D), (D, POOLW), (D, POOLW), (D, ATT_O)]
    outs = [("c", IN_W, GATE_W, 0, MXU), ("t", POOLW, F32), ("t", POOLW, F32), ("t", ATT_O, F32), ("a", (1, D), F32)]
    outs += [("a", s, MXU) for s in wshapes]
    return _rows_call("merge_bwd", body, S, ts, ins, outs, scratch=[pltpu.VMEM(s, F32) for s in wshapes])


def _prenorm_bwd(name, dx_res, du, wt, x, g_pre, ts=256, lead=0):
    S = x.shape[0]
    N = du.shape[1]

    def body(i, g, dx_ref, du_ref, wt_ref, x_ref, g_ref, o_ref, dg_ref):
        if lead:
            dhv = _dot(du_ref[:, 0:lead], wt_ref[N - lead:N, :]) + _dot(du_ref[:, lead:N], wt_ref[0:N - lead, :])
        else:
            dhv = _dot(du_ref[...], wt_ref[...])
        xv = x_ref[...]
        dxn, r = _rms_bwd(dhv * g_ref[...], xv)
        o_ref[...] = dx_ref[...] + dxn
        _acc(dg_ref, g, jnp.sum(dhv * (xv * r), axis=0, keepdims=True))

    ins = [("t", dx_res, D, 0), ("t", du, N, 0), ("w", wt), ("t", x, D, 0), ("w", g_pre)]
    return _rows_call(name, body, S, ts, ins, [("t", D, F32), ("a", (1, D), F32)])


def _mem_heads(qm, kv_ref):
    out = []
    for h in range(4):
        q = qm[:, h * 128:(h + 1) * 128].astype(MXU)
        k = kv_ref[:, h * 128:(h + 1) * 128]
        v = kv_ref[:, MEM_W + h * 128:MEM_W + (h + 1) * 128]
        sc = _dot_nt(q, k) * MEM_SCALE
        e = jnp.exp(sc - jnp.max(sc, axis=1, keepdims=True))
        out.append((e / jnp.sum(e, axis=1, keepdims=True), q, k, v))
    return out


def _mem_fwd(x1, kv, g_pre, w_mq, w_mo, g_post, ts=256):
    S = x1.shape[0]

    def body(i, g, x_ref, kv_ref, gq_ref, wq_ref, wo_ref, gp_ref, om_ref, y_ref, xo_ref):
        x = x_ref[...]
        hb = _rms(x, gq_ref[...])[0].astype(MXU)
        qm = _dot(hb, wq_ref[...])
        om = jnp.concatenate([_dot(p.astype(MXU), v) for p, _, _, v in _mem_heads(qm, kv_ref)], axis=1).astype(MXU)
        om_ref[...] = om
        y = _dot_nt(om, wo_ref[...])
        y_ref[...] = y
        xo_ref[...] = x + _rms(y, gp_ref[...])[0]

    ins = [("t", x1, D, 0), ("w", kv), ("w", g_pre), ("w", w_mq), ("w", w_mo), ("w", g_post)]
    return _rows_call("mem_fwd", body, S, ts, ins, [("t", MEM_W, MXU), ("t", D, F32), ("t", D, F32)])


def _mem_bwd(dx2, ym, x1, om, kv, g_pre, w_mq, w_mo, g_post, ts=256):
    S = x1.shape[0]
    last = S // ts - 1

    def body(i, g, dx_ref, y_ref, x_ref, om_ref, kv_ref, gq_ref, wq_ref, wo_ref, gp_ref, dxo_ref, dgp_ref, dgq_ref, dkv_ref,
             dwo_ref, dwq_ref, acc_o, acc_q):
        dxv, y, x = dx_ref[...], y_ref[...], x_ref[...]
        dy, r = _rms_bwd(dxv * gp_ref[...], y)
        _acc(dgp_ref, g, jnp.sum(dxv * (y * r), axis=0, keepdims=True))
        dyb = dy.astype(MXU)
        dom = _dot(dyb, wo_ref[...])
        h, r1 = _rms(x, gq_ref[...])
        hb = h.astype(MXU)
        qm = _dot(hb, wq_ref[...])
        dqs = []

        @pl.when(g == 0)
        def _():
            dkv_ref[...] = jnp.zeros_like(dkv_ref)
            acc_o[...] = jnp.zeros_like(acc_o)
            acc_q[...] = jnp.zeros_like(acc_q)

        acc_o[...] += _dot_tn(dyb, om_ref[...])

        for hh, (p, q, k, v) in enumerate(_mem_heads(qm, kv_ref)):
            doh = dom[:, hh * 128:(hh + 1) * 128].astype(MXU)
            dp = _dot_nt(doh, v)
            dsc = (p * (dp - jnp.sum(dp * p, axis=1, keepdims=True)) * MEM_SCALE).astype(MXU)
            dqs.append(_dot(dsc, k))
            dkv_ref[:, hh * 128:(hh + 1) * 128] += _dot_tn(dsc, q)
            dkv_ref[:, MEM_W + hh * 128:MEM_W + (hh + 1) * 128] += _dot_tn(p.astype(MXU), doh)
        dq = jnp.concatenate(dqs, axis=1).astype(MXU)
        acc_q[...] += _dot_tn(hb, dq)
        dh = _dot_nt(dq, wq_ref[...])
        _acc(dgq_ref, g, jnp.sum(dh * (x * r1), axis=0, keepdims=True))
        dxo_ref[...] = dxv + _rms_bwd(dh * gq_ref[...], x)[0]

        @pl.when(g == last)
        def _():
            dwo_ref[...] = acc_o[...].astype(MXU)
            dwq_ref[...] = acc_q[...].astype(MXU)

    ins = [("t", dx2, D, 0), ("t", ym, D, 0), ("t", x1, D, 0), ("t", om, MEM_W, 0), ("w", kv), ("w", g_pre), ("w", w_mq), ("w", w_mo),
           ("w", g_post)]
    outs = [("t", D, F32), ("a", (1, D), F32), ("a", (1, D), F32), ("a", (256, D), F32), ("a", (D, MEM_W), MXU), ("a", (D, MEM_W), MXU)]
    return _rows_call("mem_bwd", body, S, ts, ins, outs, scratch=[pltpu.VMEM((D, MEM_W), F32), pltpu.VMEM((D, MEM_W), F32)])


def _gain_grad(name, dn, x):
    n = x.shape[0]

    def body(i, g, dn_ref, x_ref, o_ref):
        xv = x_ref[...]
        r = lax.rsqrt(jnp.mean(xv * xv, axis=-1, keepdims=True) + EPS)
        o_ref[...] = jnp.sum(dn_ref[...] * (xv * r), axis=0, keepdims=True)

    return _rows_call(name, body, n, n, [("t", dn, D, 0), ("t", x, D, 0)], [("a", (1, D), F32)])[0]


def _ffn_fwd(x2, u3, conv_f, w_down, g_post, ts=256):
    S = x2.shape[0]

    def body(i, g, x_ref, ua_ref, ub_ref, cw_ref, wd_ref, gp_ref, act_ref, y_ref, xo_ref, c_ref, cu):
        @pl.when(g == 0)
        def _():
            cu[...] = jnp.zeros_like(cu)

        ua = ua_ref[...].astype(F32)
        c, _, _ = _conv3(ua, cu[...], cw_ref[...])
        c_ref[...] = c.astype(MXU)
        act = (c * jax.nn.sigmoid(c) * ub_ref[...].astype(F32)).astype(MXU)
        act_ref[...] = act
        y = _dot(act, wd_ref[...])
        y_ref[...] = y
        xo_ref[...] = x_ref[...] + _rms(y, gp_ref[...])[0]
        cu[...] = ua[ts - 8:]

    ins = [("t", x2, D, 0), ("t", u3, D_FF, 0), ("t", u3, D_FF, 1), ("w", conv_f), ("w", w_down), ("w", g_post)]
    return _rows_call("ffn_fwd", body, S, ts, ins, [("t", D_FF, MXU), ("t", D, F32), ("t", D, F32), ("t", D_FF, MXU)],
                      scratch=[pltpu.VMEM((8, D_FF), F32)])


def _ffn_bwd(dx3, y3, u3, c, conv_f, w_down, g_post, ts=128):
    S = dx3.shape[0]

    def body(i, g, dx_ref, y_ref, ua_ref, ub_ref, c_ref, cw_ref, wd_ref, gp_ref, dy_ref, du_ref, dgp_ref, dcw_ref, cdc):
        @pl.when(g == 0)
        def _():
            cdc[...] = jnp.zeros_like(cdc)

        dxv, y = dx_ref[...], y_ref[...]
        dy, r = _rms_bwd(dxv * gp_ref[...], y)
        _acc(dgp_ref, g, jnp.sum(dxv * (y * r), axis=0, keepdims=True))
        dyb = dy.astype(MXU)
        dy_ref[...] = dyb
        dact = _dot_nt(dyb, wd_ref[...])
        ua, c, w = ua_ref[...].astype(F32), c_ref[...].astype(F32), cw_ref[...]
        sg = jax.nn.sigmoid(c)
        du_ref[:, D_FF:2 * D_FF] = (dact * (c * sg)).astype(du_ref.dtype)
        dc = dact * ub_ref[...].astype(F32) * (sg * (1.0 + c * (1.0 - sg)))
        dua, dc1, dc2 = _conv3_t(dc, cdc[...], w, shifted=True)
        du_ref[:, 0:D_FF] = dua.astype(du_ref.dtype)
        dw = jnp.concatenate([jnp.sum(ua * dc2, axis=0, keepdims=True), jnp.sum(ua * dc1, axis=0, keepdims=True),
                              jnp.sum(ua * dc, axis=0, keepdims=True)], axis=0)
        _acc(dcw_ref, g, dw)
        cdc[...] = dc[:8]

    ins = [("t", dx3, D, 0), ("t", y3, D, 0), ("t", u3, D_FF, 0), ("t", u3, D_FF, 1), ("t", c, D_FF, 0), ("w", conv_f),
           ("w", w_down), ("w", g_post)]
    outs = [("t", D, MXU), ("t", 2 * D_FF, MXU), ("a", (1, D), F32), ("a", (3, D_FF), F32)]
    return _rows_call("ffn_bwd", body, S, ts, ins, outs, scratch=[pltpu.VMEM((8, D_FF), F32)], reverse=True)


def _loss_head(x, target, ts=512):
    S = x.shape[0]

    def body(i, g, x_ref, t_ref, dx_ref, acc_ref):
        diff = x_ref[...] - t_ref[...]
        dx_ref[...] = diff * (1.0 / D)
        col = jnp.sum(diff * diff, axis=0, keepdims=True)
        part = col[:, 0:LANES]
        for j in range(1, D // LANES):
            part = part + col[:, j * LANES:(j + 1) * LANES]
        row = lax.broadcasted_iota(jnp.int32, (8, LANES), 0)
        _acc(acc_ref, g, jnp.where(row == 0, jnp.broadcast_to(part, (8, LANES)), 0.0))

    return _rows_call("loss_head", body, S, ts, [("t", x, D, 0), ("t", target, D, 0)], [("t", D, F32), ("a", (8, LANES), F32)])


_OPERAND_NAME = dict(w_in='w_in', w_branch_a='wa', w_branch_b='wb', w_branch_c='wc', w_out='w_out', w_mq='w_mq', w_mkv='w_mkv',
                     w_mo='w_mo', w_up='w_up', w_down='w_down')


def _big_operands(big):
    return {_OPERAND_NAME[n]: a for n, a in big.items()}


def _layer_weights(big, small, l):
    pool_w = small['pool_w'][l].astype(MXU)
    wblk = jnp.zeros((POOLW, POOLW), MXU)
    for g in range(4):
        wblk = lax.dynamic_update_slice(wblk, pool_w[g], (g * 96, g * 96))
    vec = lambda n: small[n][l].reshape(1, -1)
    return dict(
        _big_operands(big),
        wblk=wblk, pool_scale=vec('pool_scale'), conv_b=small['conv_b_w'][l], conv_f=small['conv_ffn_w'][l],
        g_mix_pre=vec('norm_mix_pre'), g_mix_post=vec('norm_mix_post'), g_mem_pre=vec('norm_mem_pre'),
        g_mem_post=vec('norm_mem_post'), g_memkv=vec('norm_memkv'), g_ffn_pre=vec('norm_ffn_pre'), g_ffn_post=vec('norm_ffn_post'))


def _layer_fwd(x0, mem, W, ctab, stab):
    sv = _layer_fwd_mix(x0, W, ctab, stab)
    return _layer_fwd_late(mem, W, sv), sv


def _layer_fwd_mix(x0, W, ctab, stab):
    return _layer_fwd_merge(W, _layer_fwd_branches(x0, W, ctab, stab))


def _layer_fwd_branches(x0, W, ctab, stab):
    sv = dict(x0=x0)
    sv['u'], sv['h1'] = _norm_mm("in_proj", x0, W['g_mix_pre'], W['w_in'], ts=2048, tn=IN_TILE, wt=True, rot=IN_ROT, out_dtype=MXU)
    sv['a2'], sv['yb'] = _poolconv_fwd(sv['u'], W['wblk'], W['pool_scale'], W['conv_b'])
    sv['qkv'] = q3, k3, v3 = _rope_perm(sv['u'], ctab, stab)
    sv['att'], sv['o'], sv['lse'] = _attn_combine([_attn_fwd(g, q3[g], k3[g], v3[g]) for g in range(3)])
    return sv


def _layer_fwd_merge(W, sv):
    sv['merged'], sv['y1'], sv['x1'] = _merge_fwd(sv['x0'], sv['u'], sv['a2'], sv['yb'], sv['att'], W['wa'], W['wb'], W['wc'],
                                                  W['w_out'], W['g_mix_post'])
    return sv


def _layer_fwd_late(mem, W, sv):
    sv['kv'], sv['memn'] = _norm_mm("mem_kv", mem, W['g_memkv'], W['w_mkv'], ts=256, tn=D, out_dtype=MXU)
    sv['om'], sv['ym'], sv['x2'] = _mem_fwd(sv['x1'], sv['kv'], W['g_mem_pre'], W['w_mq'], W['w_mo'], W['g_mem_post'])
    sv['u3'], sv['h3'] = _norm_mm("up_proj", sv['x2'], W['g_ffn_pre'], W['w_up'], ts=2048, tn=1408, wt=True, out_dtype=MXU)
    sv['act'], sv['y3'], x3, sv['c3'] = _ffn_fwd(sv['x2'], sv['u3'], W['conv_f'], W['w_down'], W['g_ffn_post'])
    return x3


def _layer_bwd(dx3, mem, W, sv, ctab, stab):
    dx1, g = _layer_bwd_late(dx3, mem, W, sv)
    dx0, g_mix = _layer_bwd_mix(dx1, W, sv, ctab, stab)
    return dx0, {**g, **g_mix}


def _layer_bwd_late(dx3, mem, W, sv):
    g = {}
    dy3, du3, g['norm_ffn_post'], g['conv_ffn_w'] = _ffn_bwd(dx3, sv['y3'], sv['u3'], sv['c3'], W['conv_f'], W['w_down'], W['g_ffn_post'])
    g['w_down'] = _mm_tn("dw_down", sv['act'], dy3, cap_k=256)
    g['w_up'] = _mm_tn("dw_up", du3, sv['h3'])
    dx2, g['norm_ffn_pre'] = _prenorm_bwd("ffn_pre_bwd", dx3, du3, W['w_up'], sv['x2'], W['g_ffn_pre'], ts=512)
    dx1, g['norm_mem_post'], g['norm_mem_pre'], dkv, g['w_mo'], g['w_mq'] = _mem_bwd(
        dx2, sv['ym'], sv['x1'], sv['om'], sv['kv'], W['g_mem_pre'], W['w_mq'], W['w_mo'], W['g_mem_post'])
    dkvb = dkv.astype(MXU)
    g['w_mkv'] = _mm_tn("dw_mkv", sv['memn'], dkvb)
    g['norm_memkv'] = _gain_grad("memkv_gain", _mm_nt("d_memn", dkvb, W['w_mkv'], ts=256, tn=512), mem)
    return dx1, g


def _layer_bwd_mix(dx1, W, sv, ctab, stab):
    du, g = _layer_bwd_mixers(dx1, W, sv, ctab, stab)
    g['w_in'] = _dw_in(du, sv)
    dx0, g['norm_mix_pre'] = _mix_pre_bwd(dx1, du, W, sv)
    return dx0, g


def _dw_in(du, sv):
    return _mm_tn("dw_in", du, sv['h1'], cap_k=IN_TILE, rot=IN_ROT)


def _mix_pre_bwd(dx1, du, W, sv):
    return _prenorm_bwd("mix_pre_bwd", dx1, du, W['w_in'], sv['x0'], W['g_mix_pre'], lead=GATE_W)


def _layer_bwd_mixers(dx1, W, sv, ctab, stab):
    parts, g = _layer_bwd_merge(dx1, W, sv)
    du, g_br = _layer_bwd_branches(parts, W, sv, ctab, stab)
    return du, {**g, **g_br}


def _layer_bwd_merge(dx1, W, sv):
    g = {}
    du, da2, dyb, datt, g['norm_mix_post'], g['w_out'], g['w_branch_a'], g['w_branch_b'], g['w_branch_c'] = _merge_bwd(
        dx1, sv['y1'], sv['u'], sv['a2'], sv['yb'], sv['att'], sv['merged'], W['wa'], W['wb'], W['wc'], W['w_out'], W['g_mix_post'])
    return (du, da2, dyb, datt), g


def _layer_bwd_branches(parts, W, sv, ctab, stab):
    du, da2, dyb, datt = parts
    g = {}
    du, g['pool_scale'], dwblk, g['conv_b_w'] = _poolconv_bwd(sv['u'], da2, dyb, du, W['wblk'], W['pool_scale'], W['conv_b'])
    g['pool_w'] = jnp.stack([dwblk[k * 96:(k + 1) * 96, k * 96:(k + 1) * 96] for k in range(4)])
    q3, k3, v3 = sv['qkv']
    do3, dl3, lse3 = _attn_bwd_prep(datt, sv['o'], sv['lse'])
    dqkv3 = [_attn_bwd(i, q3[i], k3[i], v3[i], do3[i], dl3[i], lse3[i]) for i in range(3)]
    du = _rope_unperm_bwd([[t[which] for t in dqkv3] for which in range(3)], du, ctab, stab)
    return du, g


def _local_step(x, mem, positions, target, big, small):
    ctab, stab = _rope_tables(positions)
    Ws = [_layer_weights(big[l], small, l) for l in range(DEPTH)]
    saved = []
    for l in range(DEPTH):
        x, sv = _layer_fwd(x, mem, Ws[l], ctab, stab)
        saved.append(sv)
    dx, acc = _loss_head(x, target)
    loss = jnp.sum(acc) * (0.5 / D)
    grads = [None] * DEPTH
    for l in reversed(range(DEPTH)):
        dx, grads[l] = _layer_bwd(dx, mem, Ws[l], saved[l], ctab, stab)
    return loss, dx, grads


_HBM = pl.BlockSpec(memory_space=pl.ANY)
MESH_ID = pl.DeviceIdType.MESH


def _all_gather(name, xs):
    n = len(xs)

    def body(*refs):
        x_refs, out_refs = refs[:n], refs[n:2 * n]
        send_sems, recv_sems, local_sems = refs[2 * n:]
        x, y, c = lax.axis_index("x"), lax.axis_index("y"), lax.axis_index("c")
        me, sibling = (x, y, c), (x, y, 1 - c)
        chips = [(1 - x, y), (x, 1 - y), (1 - x, 1 - y)]

        def slot(a, p):
            return out_refs[a].at[4 * p[0] + 2 * p[1] + p[2]]

        def copy(a, k, block, to, src=None):
            return pltpu.make_async_remote_copy(src_ref=slot(a, block) if src is None else src, dst_ref=slot(a, block),
                                                send_sem=send_sems.at[a, k], recv_sem=recv_sems.at[a, k], device_id=to,
                                                device_id_type=MESH_ID)

        started = []
        for a in range(n):
            mine = pltpu.make_async_copy(x_refs[a], slot(a, me), local_sems.at[a])
            mine.start()
            started.append(mine)
        first = []
        for a in range(n):
            first.append(copy(a, 0, me, sibling, src=x_refs[a]))
            first += [copy(a, 1 + j, me, (*chip, c), src=x_refs[a]) for j, chip in enumerate(chips)]
        for cp in first:
            cp.start()
        passed = []
        for j, chip in enumerate(chips):
            for a in range(n):
                copy(a, 1 + j, (*chip, c), me).wait_recv()
                fw = copy(a, 4 + j, (*chip, c), sibling)
                fw.start()
                passed.append(fw)
        for a in range(n):
            copy(a, 0, sibling, me).wait_recv()
            for j, chip in enumerate(chips):
                copy(a, 4 + j, (*chip, 1 - c), me).wait_recv()
        for cp in first + passed:
            cp.wait_send()
        for mine in started:
            mine.wait()

    return pl.pallas_call(
        body, out_shape=[jax.ShapeDtypeStruct((N_DEV,) + x.shape, x.dtype) for x in xs], in_specs=[_HBM] * n, out_specs=[_HBM] * n,
        scratch_shapes=[pltpu.SemaphoreType.DMA((n, 7)), pltpu.SemaphoreType.DMA((n, 7)), pltpu.SemaphoreType.DMA((n,))],
        name=name)(*xs)


_SEM =pl.BlockSpec(memory_space=pltpu.SEMAPHORE)
_IN_HBM = pl.BlockSpec(memory_space=pltpu.HBM)
_SIDE_EFFECT = pltpu.SideEffectType.DATAFLOW_SIDE_EFFECTING


def _push_copies(src_refs, land_refs, send_sems, recv_sems, per_peer):
    x, y, c = lax.axis_index("x"), lax.axis_index("y"), lax.axis_index("c")
    me = 4 * x + 2 * y + c
    copies = []
    for r in range(1, N_DEV):
        px, py, pc = x ^ ((r >> 2) & 1), y ^ ((r >> 1) & 1), c ^ (r & 1)
        for a, (s, d) in enumerate(zip(src_refs, land_refs)):
            k = a * (N_DEV - 1) + r - 1
            copies.append(pltpu.make_async_remote_copy(src_ref=s.at[4 * px + 2 * py + pc] if per_peer else s, dst_ref=d.at[me],
                                                       send_sem=send_sems.at[k], recv_sem=recv_sems.at[k],
                                                       device_id=(px, py, pc), device_id_type=MESH_ID))
    return copies


def _push_start(name, srcs, per_peer, after):
    n = len(srcs)
    lands = [lax.empty((N_DEV,) + (s.shape[1:] if per_peer else s.shape), s.dtype) for s in srcs]

    def body(*refs):
        for cp in _push_copies(refs[:n], refs[n:2 * n], refs[2 * n + 1], refs[2 * n + 2], per_peer):
            cp.start()
        refs[-1][...] = jnp.zeros_like(refs[-1])

    hbm = [pltpu.HBM(a.shape, a.dtype) for a in (*srcs, *lands)]
    sems = pltpu.SemaphoreType.DMA((n * (N_DEV - 1),))
    out = pl.pallas_call(
        body, name=name, out_shape=(sems, sems, *hbm, jax.ShapeDtypeStruct((8, LANES), F32)),
        in_specs=[_IN_HBM] * (2 * n) + [pl.BlockSpec(memory_space=pl.ANY)],
        out_specs=(_SEM, _SEM, *[_IN_HBM] * (2 * n), pl.BlockSpec(memory_space=pltpu.VMEM)),
        input_output_aliases={a: 2 + a for a in range(2 * n)},
        compiler_params=pltpu.CompilerParams(has_side_effects=_SIDE_EFFECT),
    )(*[pltpu.with_memory_space_constraint(a, pltpu.HBM) for a in (*srcs, *lands)], after)
    return out[0], out[1], out[2:2 + n], out[2 + n:2 + 2 * n], out[-1]


def _push_wait(name, started, per_peer, after):
    send_sems, recv_sems, srcs, lands, _ = started
    n = len(srcs)

    def body(*refs):
        for cp in _push_copies(refs[:n], refs[n:2 * n], refs[2 * n], refs[2 * n + 1], per_peer):
            cp.wait_send()
            cp.wait_recv()

    out = pl.pallas_call(
        body, name=name, out_shape=[pltpu.HBM(a.shape, a.dtype) for a in (*srcs, *lands)],
        in_specs=[_IN_HBM] * (2 * n) + [_SEM, _SEM, pl.BlockSpec(memory_space=pl.ANY)], out_specs=[_IN_HBM] * (2 * n),
        input_output_aliases={a: a for a in range(2 * n)},
        compiler_params=pltpu.CompilerParams(has_side_effects=_SIDE_EFFECT),
    )(*srcs, *lands, send_sems, recv_sems, after)
    if per_peer:
        return list(zip(out[:n], out[n:]))
    return _with_own(out[n:], out[:n], _my_slot())


def _my_slot():
    return 4 * lax.axis_index("x") + 2 * lax.axis_index("y") + lax.axis_index("c")


def _row_tile(rows, cols, budget):
    if rows * cols * 4 <= budget or rows % 16:
        return rows
    best = 16
    for t in range(16, rows + 1, 16):
        if rows % t == 0 and t * cols * 4 <= budget:
            best = t
    return best


def _slot_total(r_ref, own_ref):
    me = _my_slot()
    g = jnp.where(me == 0, own_ref[...], r_ref[0]).astype(F32)
    for k in range(1, N_DEV):
        g = g + jnp.where(me == k, own_ref[...], r_ref[k]).astype(F32)
    return g


def _sum_slots(name, pushed):
    src, recv = pushed
    _, R, C = recv.shape
    tr = _row_tile(R, C, 1 << 20)

    def body(r_ref, own_ref, o_ref):
        o_ref[...] = _slot_total(r_ref, own_ref)

    return pl.pallas_call(body, grid=(R // tr,),
                          in_specs=[pl.BlockSpec((N_DEV, tr, C), lambda i: (0, i, 0)), pl.BlockSpec((None, tr, C), lambda i: (_my_slot(), i, 0))],
                          out_specs=pl.BlockSpec((tr, C), lambda i: (i, 0)), out_shape=jax.ShapeDtypeStruct((R, C), F32),
                          compiler_params=_params(("arbitrary",)), name=name)(recv, src)


def _adamw_step(gv, w_ref, m_ref, v_ref, d_ref, mo_ref, vo_ref):
    mn = ADAM_B1 * m_ref[...] + (1.0 - ADAM_B1) * gv
    vn = ADAM_B2 * v_ref[...] + (1.0 - ADAM_B2) * (gv * gv)
    mo_ref[...] = mn
    vo_ref[...] = vn
    c1 = 1.0 - ADAM_B1 ** ADAM_STEP
    c2 = 1.0 - ADAM_B2 ** ADAM_STEP
    d_ref[...] = -ADAM_LR * ((mn / c1) / (jnp.sqrt(vn / c2) + ADAM_EPS) + ADAM_WD * w_ref[...])


def _adamw(name, g, w, m, v):
    shape = w.shape
    R, C = shape[-2], shape[-1]
    view = (-1, R, C)
    L = w.size // (R * C)
    tr = _row_tile(R, C, 1 << 20)

    def body(g_ref, w_ref, m_ref, v_ref, d_ref, mo_ref, vo_ref):
        _adamw_step(g_ref[...], w_ref, m_ref, v_ref, d_ref, mo_ref, vo_ref)

    blk = pl.BlockSpec((None, tr, C), lambda l, i: (l, i, 0))
    shp = jax.ShapeDtypeStruct((L, R, C), F32)
    outs = pl.pallas_call(body, grid=(L, R // tr), in_specs=[blk, blk, blk, blk], out_specs=[blk, blk, blk], out_shape=[shp, shp, shp],
                          compiler_params=_params(("arbitrary", "arbitrary")), name=name)(*[a.reshape(view) for a in (g, w, m, v)])
    return [o.reshape(shape) for o in outs]


def _sum_adamw(name, pushed, w, m, v):
    L, R, C = w.shape
    tr = _row_tile(R, C, 1 << 20)
    n_i = R // tr

    def body(*refs):
        shares, (w_ref, m_ref, v_ref, g_ref, d_ref, mo_ref, vo_ref) = refs[:2 * L], refs[2 * L:]
        for k in range(L):
            @pl.when(pl.program_id(0) == k)
            def _(k=k):
                g_ref[...] = _slot_total(shares[2 * k], shares[2 * k + 1])
        _adamw_step(g_ref[...], w_ref, m_ref, v_ref, d_ref, mo_ref, vo_ref)

    def during(k):
        return lambda l, i: jnp.where(l == k, i, jnp.where(l < k, 0, n_i - 1))

    in_specs, operands = [], []
    for k, (src, recv) in enumerate(pushed):
        in_specs += [pl.BlockSpec((N_DEV, tr, C), lambda l, i, at=during(k): (0, at(l, i), 0)),
                     pl.BlockSpec((None, tr, C), lambda l, i, at=during(k): (_my_slot(), at(l, i), 0))]
        operands += [recv, src]
    blk = pl.BlockSpec((None, tr, C), lambda l, i: (l, i, 0))
    shp = jax.ShapeDtypeStruct((L, R, C), F32)
    return pl.pallas_call(body, grid=(L, n_i), in_specs=in_specs + [blk, blk, blk], out_specs=[blk] * 4, out_shape=[shp] * 4,
                          compiler_params=_params(("arbitrary", "arbitrary")), name=name)(*operands, w, m, v)


def _pad_flat(a, n):
    a = a.reshape(-1)
    return jnp.pad(a, (0, n - a.shape[0]))


def _seg(n):
    return -(-n // FLAT_ALIGN) * FLAT_ALIGN


def _to_blocks(full, axis):
    shp = full.shape
    return jnp.moveaxis(full.reshape(shp[:axis] + (N_DEV, shp[axis] // N_DEV) + shp[axis + 1:]), axis, 0)


def _from_blocks(blocks, axis):
    b = jnp.moveaxis(blocks, 0, axis)
    shp = b.shape
    return b.reshape(shp[:axis] + (shp[axis] * shp[axis + 1],) + shp[axis + 2:])


def _as_rows(shard, n):
    return shard.T if SHARD_AXIS[n] == 2 else shard


def _with_own(lands, own, me):
    return [lax.dynamic_update_slice(land, o[None], (me, 0, 0)) for land, o in zip(lands, own)]


def kernel(x, mem, positions, norm_mix_pre, norm_mix_post, w_in, pool_w, pool_scale, conv_b_w, w_branch_a, w_branch_b, w_branch_c, w_out, norm_mem_pre, norm_mem_post, norm_memkv, w_mq, w_mkv, w_mo, norm_ffn_pre, norm_ffn_post, w_up, conv_ffn_w, w_down, loss_target, m_norm_mix_pre, m_norm_mix_post, m_w_in, m_pool_w, m_pool_scale, m_conv_b_w, m_w_branch_a, m_w_branch_b, m_w_branch_c, m_w_out, m_norm_mem_pre, m_norm_mem_post, m_norm_memkv, m_w_mq, m_w_mkv, m_w_mo, m_norm_ffn_pre, m_norm_ffn_post, m_w_up, m_conv_ffn_w, m_w_down, v_norm_mix_pre, v_norm_mix_post, v_w_in, v_pool_w, v_pool_scale, v_conv_b_w, v_w_branch_a, v_w_branch_b, v_w_branch_c, v_w_out, v_norm_mem_pre, v_norm_mem_post, v_norm_memkv, v_w_mq, v_w_mkv, v_w_mo, v_norm_ffn_pre, v_norm_ffn_post, v_w_up, v_conv_ffn_w, v_w_down):
    w = dict(norm_mix_pre=norm_mix_pre, norm_mix_post=norm_mix_post, w_in=w_in, pool_w=pool_w, pool_scale=pool_scale, conv_b_w=conv_b_w, w_branch_a=w_branch_a, w_branch_b=w_branch_b, w_branch_c=w_branch_c, w_out=w_out, norm_mem_pre=norm_mem_pre, norm_mem_post=norm_mem_post, norm_memkv=norm_memkv, w_mq=w_mq, w_mkv=w_mkv, w_mo=w_mo, norm_ffn_pre=norm_ffn_pre, norm_ffn_post=norm_ffn_post, w_up=w_up, conv_ffn_w=conv_ffn_w, w_down=w_down)
    m = dict(norm_mix_pre=m_norm_mix_pre, norm_mix_post=m_norm_mix_post, w_in=m_w_in, pool_w=m_pool_w, pool_scale=m_pool_scale, conv_b_w=m_conv_b_w, w_branch_a=m_w_branch_a, w_branch_b=m_w_branch_b, w_branch_c=m_w_branch_c, w_out=m_w_out, norm_mem_pre=m_norm_mem_pre, norm_mem_post=m_norm_mem_post, norm_memkv=m_norm_memkv, w_mq=m_w_mq, w_mkv=m_w_mkv, w_mo=m_w_mo, norm_ffn_pre=m_norm_ffn_pre, norm_ffn_post=m_norm_ffn_post, w_up=m_w_up, conv_ffn_w=m_conv_ffn_w, w_down=m_w_down)
    v = dict(norm_mix_pre=v_norm_mix_pre, norm_mix_post=v_norm_mix_post, w_in=v_w_in, pool_w=v_pool_w, pool_scale=v_pool_scale, conv_b_w=v_conv_b_w, w_branch_a=v_w_branch_a, w_branch_b=v_w_branch_b, w_branch_c=v_w_branch_c, w_out=v_w_out, norm_mem_pre=v_norm_mem_pre, norm_mem_post=v_norm_mem_post, norm_memkv=v_norm_memkv, w_mq=v_w_mq, w_mkv=v_w_mkv, w_mo=v_w_mo, norm_ffn_pre=v_norm_ffn_pre, norm_ffn_post=v_norm_ffn_post, w_up=v_w_up, conv_ffn_w=v_conv_ffn_w, w_down=v_w_down)

    mix_big = [n for n in BIG if n not in LATE_BIG]
    block = lambda names, l: [_as_rows(w[n][l], n).astype(MXU) for n in names]
    conv = jnp.concatenate([_pad_flat(w[n], _seg(w[n].size)) for n in F32_GATHERED]).reshape(-1, LANES)
    groups = dict(m=MERGE_BIG, b=LATE_BIG, a=mix_big)
    got0 = _all_gather("weights_all_gather_0", block(['w_in'], 0) + [conv])
    conv_all = got0[-1].reshape(N_DEV, -1)
    small, off = {n: w[n] for n in WEIGHTS if n not in SHARD_AXIS}, 0
    for n in F32_GATHERED:
        small[n] = _from_blocks(conv_all[:, off:off + w[n].size].reshape((N_DEV,) + w[n].shape), 2)
        off += _seg(w[n].size)
    whole = lambda names, got: {n: o.reshape(-1, o.shape[-1]) for n, o in zip(names, got)}
    pushes, after = {}, got0[0]
    for tag, l in (('m', 0), ('b', 0), ('a', 1), ('b', 1)):
        pushes[tag, l] = _push_start(f"weights_push_start_{l}{tag}", block(groups[tag], l), False, after)
        after = pushes[tag, l][4]

    def arrived(tag, l, done):
        return _big_operands(whole(groups[tag], _push_wait(f"weights_push_wait_{l}{tag}", pushes[tag, l], False, done)))

    ctab, stab = _rope_tables(positions[0])
    W0 = _layer_weights(whole(['w_in'], got0), small, 0)
    sv0 = _layer_fwd_branches(x[0], dict(W0, g_mix_pre=W0['g_mix_pre'] + after[0, 0]), ctab, stab)
    W0.update(arrived('m', 0, sv0['att']))
    sv0 = _layer_fwd_merge(W0, sv0)
    W0.update(arrived('b', 0, sv0['x1']))
    x1 = _layer_fwd_late(mem[0], W0, sv0)
    W1 = _layer_weights({}, small, 1)
    W1.update(arrived('a', 1, x1))
    sv1 = _layer_fwd_mix(x1, W1, ctab, stab)
    W1.update(arrived('b', 1, sv1['x1']))
    x2 = _layer_fwd_late(mem[0], W1, sv1)
    dx, acc = _loss_head(x2, loss_target[0])
    loss = lax.psum(jnp.sum(acc) * (0.5 / D), MESH_AXES)
    grads = [None] * DEPTH
    dx, grads[1] = _layer_bwd(dx, mem[0], W1, sv1, ctab, stab)
    sent = [None, [grads[1][n].reshape(N_DEV, -1, grads[1][n].shape[-1]) for n in BIG]]
    push_g = _push_start("grads_push_start_1", sent[1], True, dx)
    dx, g_late = _layer_bwd_late(dx, mem[0], dict(W0, g_ffn_post=W0['g_ffn_post'] + push_g[4][0, 0]), sv0)
    sent_late = [g_late[n].reshape(N_DEV, -1, g_late[n].shape[-1]) for n in LATE_BIG]
    push_l = _push_start("grads_push_start_0", sent_late, True, dx)
    parts, g_mix = _layer_bwd_merge(dx, dict(W0, g_mix_post=W0['g_mix_post'] + push_l[4][0, 0]), sv0)
    sent_merge = [g_mix[n].reshape(N_DEV, -1, g_mix[n].shape[-1]) for n in MERGE_BIG]
    push_m = _push_start("grads_push_start_0m", sent_merge, True, parts[0])
    du, g_br = _layer_bwd_branches(parts, dict(W0, pool_scale=W0['pool_scale'] + push_m[4][0, 0]), sv0, ctab, stab)
    g_mix.update(g_br)
    g_mix['w_in'] = _dw_in(du, sv0)
    sent_in = [g_mix['w_in'].reshape(N_DEV, -1, D)]
    push_i = _push_start("grads_push_start_in", sent_in, True, du)
    dx, g_mix['norm_mix_pre'] = _mix_pre_bwd(dx, du, dict(W0, g_mix_pre=W0['g_mix_pre'] + push_i[4][0, 0]), sv0)
    grads[0] = {**g_late, **g_mix}
    recv1 = _push_wait("grads_push_wait_1", push_g, True, dx)
    recv_late = _push_wait("grads_push_wait_0", push_l, True, dx)
    recv_merge = _push_wait("grads_push_wait_0m", push_m, True, dx)

    misc_names = [n for n in WEIGHTS if n not in BIG]
    stacked = {n: jnp.stack([grads[l][n].reshape(small[n].shape[1:]) for l in range(DEPTH)]) for n in misc_names}
    rows = [(_to_blocks(stacked[n], 2) if n in SHARD_AXIS else jnp.broadcast_to(stacked[n][None], (N_DEV,) + stacked[n].shape))
            for n in misc_names]
    segs = [_seg(w[n].size) for n in misc_names]
    misc = jnp.concatenate([jnp.pad(r.reshape(N_DEV, -1), ((0, 0), (0, s - r[0].size))) for r, s in zip(rows, segs)],
                           axis=1).reshape(N_DEV, -1, LANES)
    push_x = _push_start("grads_push_start_small", [misc], True, dx)
    g_out, shares = {}, {}
    for l, names, recv in ((1, BIG, recv1), (0, LATE_BIG, recv_late), (0, MERGE_BIG, recv_merge)):
        for n, r in zip(names, recv):
            shares[n, l] = r

    swap = lambda a: jnp.swapaxes(a, 1, 2)

    def update(n):
        if n not in BIG:
            return [g_out[n], *_adamw(f"adamw_{n}", g_out[n], w[n], m[n], v[n])]
        of_layers = [shares[n, l] for l in range(DEPTH)]
        if SHARD_AXIS[n] == 1:
            return _sum_adamw(f"adamw_{n}", of_layers, w[n], m[n], v[n])
        if w[n].shape[2] % LANES:
            return [swap(a) for a in _sum_adamw(f"adamw_{n}", of_layers, swap(w[n]), swap(m[n]), swap(v[n]))]
        g = swap(jnp.stack([_sum_slots(f"sum_{n}_{l}", s) for l, s in enumerate(of_layers)]))
        return [g, *_adamw(f"adamw_{n}", g, w[n], m[n], v[n])]

    done = {n: update(n) for n in BIG if n != 'w_in'}
    shares['w_in', 0] = _push_wait("grads_push_wait_in", push_i, True, done[BIG[-1]][1])[0]
    done['w_in'] = update('w_in')
    misc_sum = _sum_slots("sum_misc", _push_wait("grads_push_wait_small", push_x, True, done['w_in'][1])[0]).reshape(-1)
    off = 0
    for n, s in zip(misc_names, segs):
        g_out[n] = misc_sum[off:off + w[n].size].reshape(w[n].shape)
        done[n] = update(n)
        off += s
    return (loss, dx[None], *[done[n][k] for k in range(4) for n in WEIGHTS])
```

```python
import jax
import jax.numpy as jnp
from jax import lax
from jax.experimental import pallas as pl
from jax.experimental.pallas import tpu as pltpu

F32 = jnp.float32
MXU = jnp.bfloat16

D = 1024
DEPTH = 2
POOLW = 384
ATT_W = 768
ATT_O = 256
GATE_W = 3 * D
IN_W = 6912
IN_TILE = 768
IN_ROT = (IN_W - GATE_W) // IN_TILE
MEM_W = 512
D_FF = 2816
EPS = 1e-6
ROPE_THETA = 500000.0
QB = 128
DILS = (1, 4, 16)
NEG = -1e30
MEM_SCALE = 128 ** -0.5
ATT_SCALE = 0.125

ADAM_LR, ADAM_B1, ADAM_B2, ADAM_EPS, ADAM_WD, ADAM_STEP = 0.001, 0.9, 0.999, 1e-08, 0.01, 10

N_DEV = 8
MESH_AXES = ("x", "y", "c")
LANES = 128
FLAT_ALIGN = 2048

WEIGHTS = ['norm_mix_pre', 'norm_mix_post', 'w_in', 'pool_w', 'pool_scale', 'conv_b_w', 'w_branch_a', 'w_branch_b',
           'w_branch_c', 'w_out', 'norm_mem_pre', 'norm_mem_post', 'norm_memkv', 'w_mq', 'w_mkv', 'w_mo',
           'norm_ffn_pre', 'norm_ffn_post', 'w_up', 'conv_ffn_w', 'w_down']
SHARD_AXIS = {'w_in': 2, 'conv_b_w': 2, 'w_branch_a': 2, 'w_branch_b': 2, 'w_branch_c': 2, 'w_out': 1, 'w_mq': 1,
              'w_mkv': 1, 'w_mo': 2, 'w_up': 2, 'conv_ffn_w': 2, 'w_down': 1}
F32_GATHERED = ('conv_b_w', 'conv_ffn_w')
BIG = [n for n in WEIGHTS if n in SHARD_AXIS and n not in F32_GATHERED]
LATE_BIG = ['w_mq', 'w_mkv', 'w_mo', 'w_up', 'w_down']
MERGE_BIG = ['w_branch_a', 'w_branch_b', 'w_branch_c', 'w_out']


VMEM_LIMIT_MB = 60


def _params(sem):
    return pltpu.CompilerParams(dimension_semantics=sem, vmem_limit_bytes=VMEM_LIMIT_MB << 20)


def _dot(a, b, prec=None):
    return lax.dot_general(a, b, (((1,), (0,)), ((), ())), preferred_element_type=F32, precision=prec)


def _dot_nt(a, b, prec=None):
    return lax.dot_general(a, b, (((1,), (1,)), ((), ())), preferred_element_type=F32, precision=prec)


def _dot_tn(a, b, prec=None):
    return lax.dot_general(a, b, (((0,), (0,)), ((), ())), preferred_element_type=F32, precision=prec)


def _tile(n, cap):
    if n <= cap:
        return n
    best = None
    for t in range(LANES, cap + 1, LANES):
        if n % t == 0:
            best = t
    assert best is not None, (n, cap)
    return best


def _rms(x, g):
    r = lax.rsqrt(jnp.mean(x * x, axis=-1, keepdims=True) + EPS)
    return x * r * g, r


def _rms_bwd(w, y):
    r = lax.rsqrt(jnp.mean(y * y, axis=-1, keepdims=True) + EPS)
    return r * w - y * (r * r * r) * jnp.mean(w * y, axis=-1, keepdims=True), r


def _rows_call(name, body, n_rows, ts, ins, outs, scratch=(), reverse=False, aliases=None):
    nt = n_rows // ts
    assert nt * ts == n_rows

    def tile_of(g):
        return (nt - 1 - g) if reverse else g

    in_specs, args = [], []
    for op in ins:
        if op[0] == "t":
            _, a, cw, cb = op
            in_specs.append(pl.BlockSpec((ts, cw), lambda g, cb=cb: (tile_of(g), cb)))
        elif op[0] == "h":
            _, a, hr, cw, cb = op
            in_specs.append(pl.BlockSpec((hr, cw), lambda g, cb=cb, k=ts // hr: (jnp.maximum(tile_of(g) * k - 1, 0), cb)))
        elif op[0] == "x":
            _, a = op
            in_specs.append(pl.BlockSpec(memory_space=pl.ANY))
        else:
            _, a = op
            in_specs.append(pl.BlockSpec(a.shape, lambda g, n=a.ndim: (0,) * n))
        args.append(a)
    out_specs, out_shape = [], []
    for op in outs:
        if op[0] == "t":
            _, cols, dt = op
            out_specs.append(pl.BlockSpec((ts, cols), lambda g: (tile_of(g), 0)))
            out_shape.append(jax.ShapeDtypeStruct((n_rows, cols), dt))
        elif op[0] == "c":
            _, total, cols, cb, dt = op
            out_specs.append(pl.BlockSpec((ts, cols), lambda g, cb=cb: (tile_of(g), cb)))
            out_shape.append(jax.ShapeDtypeStruct((n_rows, total), dt))
        else:
            _, shp, dt = op
            out_specs.append(pl.BlockSpec(shp, lambda g, n=len(shp): (0,) * n))
            out_shape.append(jax.ShapeDtypeStruct(shp, dt))

    def kern(*refs):
        g = pl.program_id(0)
        body(tile_of(g), g, *refs)

    return pl.pallas_call(kern, grid=(nt,), in_specs=in_specs, out_specs=out_specs, out_shape=out_shape,
                          scratch_shapes=list(scratch), input_output_aliases=aliases or {},
                          compiler_params=_params(("arbitrary",)), name=name)(*args)


def _acc(ref, g, val):
    @pl.when(g == 0)
    def _():
        ref[...] = val

    @pl.when(g != 0)
    def _():
        ref[...] += val


def _norm_mm(name, x, g, w, ts, tn, out_dtype=F32, wt=False, rot=0):
    S, K = x.shape
    N = w.shape[0] if wt else w.shape[1]
    assert wt or not rot

    def body(x_ref, g_ref, w_ref, o_ref, h_ref, hs):
        @pl.when(pl.program_id(1) == 0)
        def _():
            h, _ = _rms(x_ref[...], g_ref[...])
            hs[...] = h.astype(MXU)
            h_ref[...] = h.astype(MXU)

        o_ref[...] = (_dot_nt if wt else _dot)(hs[...], w_ref[...]).astype(out_dtype)

    w_spec = pl.BlockSpec((tn, K), lambda i, j: ((j + rot) % (N // tn), 0)) if wt else pl.BlockSpec((K, tn), lambda i, j: (0, j))
    return pl.pallas_call(
        body, grid=(S // ts, N // tn),
        in_specs=[pl.BlockSpec((ts, K), lambda i, j: (i, 0)), pl.BlockSpec((1, K), lambda i, j: (0, 0)), w_spec],
        out_specs=[pl.BlockSpec((ts, tn), lambda i, j: (i, j)), pl.BlockSpec((ts, K), lambda i, j: (i, 0))],
        out_shape=[jax.ShapeDtypeStruct((S, N), out_dtype), jax.ShapeDtypeStruct((S, K), MXU)],
        scratch_shapes=[pltpu.VMEM((ts, K), MXU)],
        compiler_params=_params(("arbitrary", "arbitrary")), name=name)(x, g, w)


def _mm_nt(name, a, b, ts, tn, out_dtype=F32):
    M, K = a.shape
    N = b.shape[0]

    def body(a_ref, b_ref, o_ref):
        o_ref[...] = _dot_nt(a_ref[...], b_ref[...]).astype(out_dtype)

    return pl.pallas_call(
        body, grid=(M // ts, N // tn),
        in_specs=[pl.BlockSpec((ts, K), lambda i, j: (i, 0)), pl.BlockSpec((tn, K), lambda i, j: (j, 0))],
        out_specs=pl.BlockSpec((ts, tn), lambda i, j: (i, j)), out_shape=jax.ShapeDtypeStruct((M, N), out_dtype),
        compiler_params=_params(("arbitrary", "arbitrary")), name=name)(a, b)


def _mm_tn(name, a, b, cap_k=512, cap_n=1024, out_dtype=MXU, rot=0):
    S, K = a.shape
    N = b.shape[1]
    tk, tn = _tile(K, cap_k), _tile(N, cap_n)

    def body(a_ref, b_ref, o_ref):
        o_ref[...] = _dot_tn(a_ref[...], b_ref[...]).astype(out_dtype)

    return pl.pallas_call(
        body, grid=(K // tk, N // tn),
        in_specs=[pl.BlockSpec((S, tk), lambda i, j: (0, i)), pl.BlockSpec((S, tn), lambda i, j: (0, j))],
        out_specs=pl.BlockSpec((tk, tn), lambda i, j: ((i + rot) % (K // tk), j)), out_shape=jax.ShapeDtypeStruct((K, N), out_dtype),
        compiler_params=_params(("arbitrary", "arbitrary")), name=name)(a, b)


def _pool_cols(shape):
    col = lax.broadcasted_iota(jnp.int32, shape, 1)
    return col < 96, col < 192, col < 288


def _pool_select(s2, s4, s8, s16):
    c1, c2, c3 = _pool_cols(s2.shape)
    return jnp.where(c1, s2, jnp.where(c2, s4, jnp.where(c3, s8, s16)))


def _pool_cnt(t0, ts):
    c1, c2, c3 = _pool_cols((ts, POOLW))
    win = jnp.where(c1, 2, jnp.where(c2, 4, jnp.where(c3, 8, 16)))
    t = t0 + lax.broadcasted_iota(jnp.int32, (ts, POOLW), 0)
    return jnp.minimum(t + 1, win).astype(F32)


def _pooled(a, prev, t0):
    ts = a.shape[0]
    ext = jnp.concatenate([prev, a], axis=0)
    s2 = ext + pltpu.roll(ext, 1, axis=0)
    s4 = s2 + pltpu.roll(s2, 2, axis=0)
    s8 = s4 + pltpu.roll(s4, 4, axis=0)
    s16 = s8 + pltpu.roll(s8, 8, axis=0)
    sums = _pool_select(s2, s4, s8, s16)[16:]
    return sums / _pool_cnt(t0, ts) - a


def _conv3(z, prev8, w):
    ext = jnp.concatenate([prev8, z], axis=0)
    z1 = pltpu.roll(ext, 1, axis=0)[8:]
    z2 = pltpu.roll(ext, 2, axis=0)[8:]
    return w[0:1] * z2 + w[1:2] * z1 + w[2:3] * z, z1, z2


def _conv3_t(dc, next8, w, shifted=False):
    ts = dc.shape[0]
    ext = jnp.concatenate([dc, next8], axis=0)
    n = ts + 8
    u1 = pltpu.roll(ext, n - 1, axis=0)[:ts]
    u2 = pltpu.roll(ext, n - 2, axis=0)[:ts]
    out = w[2:3] * dc + w[1:2] * u1 + w[0:1] * u2
    return (out, u1, u2) if shifted else out


def _poolconv_fwd(u, wblk, pool_scale, conv_b, ts=512):
    S = u.shape[0]

    def body(i, g, a_ref, bx_ref, bb_ref, bc_ref, wblk_ref, ps_ref, cw_ref, a2_ref, yb_ref, ca, cz):
        @pl.when(g == 0)
        def _():
            ca[...] = jnp.zeros_like(ca)
            cz[...] = jnp.zeros_like(cz)

        a = a_ref[...].astype(F32)
        p = _pooled(a, ca[...], i * ts)
        mixed = _dot(p.astype(MXU), wblk_ref[...])
        a2_ref[...] = (mixed * ps_ref[...]).astype(MXU)
        z = bc_ref[...].astype(F32) * bx_ref[...].astype(F32)
        conv, _, _ = _conv3(z, cz[...], cw_ref[...])
        yb_ref[...] = (bb_ref[...].astype(F32) * conv).astype(MXU)
        ca[...] = a[ts - 16:]
        cz[...] = z[ts - 8:]

    ins = [("t", u, POOLW, 8), ("t", u, POOLW, 9), ("t", u, POOLW, 10), ("t", u, POOLW, 11), ("w", wblk), ("w", pool_scale),
           ("w", conv_b)]
    return _rows_call("poolconv_fwd", body, S, ts, ins, [("t", POOLW, MXU), ("t", POOLW, MXU)],
                      scratch=[pltpu.VMEM((16, POOLW), F32), pltpu.VMEM((8, POOLW), F32)])


def _poolconv_bwd(u, d_a2, d_yb, du, wblk, pool_scale, conv_b, ts=512):
    S = u.shape[0]

    def body(i, g, a_ref, bx_ref, bb_ref, bc_ref, ap_ref, bxp_ref, bcp_ref, da2_ref, dyb_ref, wblk_ref, ps_ref, cw_ref, _,
             o_ref, dps_ref, dwb_ref, dcw_ref, ce, cdz):
        @pl.when(g == 0)
        def _():
            ce[...] = jnp.zeros_like(ce)
            cdz[...] = jnp.zeros_like(cdz)

        first = (i > 0).astype(F32)
        a = a_ref[...].astype(F32)
        p = _pooled(a, ap_ref[...].astype(F32) * first, i * ts)
        pb = p.astype(MXU)
        mixed = _dot(pb, wblk_ref[...])
        da2 = da2_ref[...]
        dmixed = (da2 * ps_ref[...]).astype(MXU)
        dp = _dot_nt(dmixed, wblk_ref[...])
        _acc(dps_ref, g, jnp.sum(da2 * mixed, axis=0, keepdims=True))
        _acc(dwb_ref, g, _dot_tn(pb, dmixed))
        e = dp / _pool_cnt(i * ts, ts)
        ext = jnp.concatenate([e, ce[...]], axis=0)
        n = ts + 16
        f2 = ext + pltpu.roll(ext, n - 1, axis=0)
        f4 = f2 + pltpu.roll(f2, n - 2, axis=0)
        f8 = f4 + pltpu.roll(f4, n - 4, axis=0)
        f16 = f8 + pltpu.roll(f8, n - 8, axis=0)
        o_ref[:, 0:POOLW] = (_pool_select(f2, f4, f8, f16)[:ts] - dp).astype(o_ref.dtype)
        ce[...] = e[:16]

        bx, bb, bc = bx_ref[...].astype(F32), bb_ref[...].astype(F32), bc_ref[...].astype(F32)
        z = bc * bx
        w = cw_ref[...]
        conv, z1, z2 = _conv3(z, (bxp_ref[...].astype(F32) * bcp_ref[...].astype(F32))[8:16] * first, w)
        dyb = dyb_ref[...]
        dconv = dyb * bb
        dz = _conv3_t(dconv, cdz[...], w)
        o_ref[:, POOLW:2 * POOLW] = (dz * bc).astype(o_ref.dtype)
        o_ref[:, 2 * POOLW:3 * POOLW] = (dyb * conv).astype(o_ref.dtype)
        o_ref[:, 3 * POOLW:4 * POOLW] = (dz * bx).astype(o_ref.dtype)
        dw = jnp.concatenate([jnp.sum(dconv * z2, axis=0, keepdims=True), jnp.sum(dconv * z1, axis=0, keepdims=True),
                              jnp.sum(dconv * z, axis=0, keepdims=True)], axis=0)
        _acc(dcw_ref, g, dw)
        cdz[...] = dconv[:8]

    ins = [("t", u, POOLW, 8), ("t", u, POOLW, 9), ("t", u, POOLW, 10), ("t", u, POOLW, 11),
           ("h", u, 16, POOLW, 8), ("h", u, 16, POOLW, 9), ("h", u, 16, POOLW, 11),
           ("t", d_a2, POOLW, 0), ("t", d_yb, POOLW, 0), ("w", wblk), ("w", pool_scale), ("w", conv_b), ("x", du)]
    outs = [("c", IN_W, 4 * POOLW, GATE_W // (4 * POOLW), MXU), ("a", (1, POOLW), F32), ("a", (POOLW, POOLW), F32), ("a", (3, POOLW), F32)]
    return _rows_call("poolconv_bwd", body, S, ts, ins, outs, aliases={len(ins) - 1: 0},
                      scratch=[pltpu.VMEM((16, POOLW), F32), pltpu.VMEM((8, POOLW), F32)], reverse=True)


def _rope_tables(positions):
    inv = ROPE_THETA ** (-jnp.arange(0, 16, 2, dtype=F32) / 16)
    lane_inv = jnp.tile(jnp.concatenate([inv, inv, jnp.zeros(48, F32)]), 2)
    lane_sign = jnp.tile(jnp.concatenate([-jnp.ones(8, F32), jnp.ones(8, F32), jnp.zeros(48, F32)]), 2)
    ang = positions.astype(F32)[:, None] * lane_inv
    return jnp.where(lane_sign == 0, 1.0, jnp.cos(ang)), jnp.sin(ang) * lane_sign


def _partner(x):
    lane = lax.broadcasted_iota(jnp.int32, x.shape, 1) % 64
    return jnp.where(lane < 8, pltpu.roll(x, LANES - 8, axis=1), jnp.where(lane < 16, pltpu.roll(x, 8, axis=1), 0.0))


def _rope(x, c, s):
    return x * c + _partner(x) * s


def _rope_t(x, c, s):
    return x * c + _partner(x * s)


def _rows_of(r, n, d):
    return pl.ds(r, n, stride=d) if d > 1 else pl.ds(0, n)


def _head_masks(shape):
    lane = lax.broadcasted_iota(jnp.int32, shape, 1) // 64
    return [lane == h for h in range(4)]


def _only(mask, x):
    return jnp.where(mask, x, jnp.zeros_like(x))


def _rope_perm(u, ctab, stab, ts=512):
    S = u.shape[0]
    nch = ATT_W // LANES

    def body(*refs):
        chunks, (c_ref, s_ref), outs, scr = refs[:3 * nch], refs[3 * nch:3 * nch + 2], refs[3 * nch + 2:-1], refs[-1]
        for k in range(3 * nch):
            scr[k] = chunks[k][...].astype(F32)
        for g, d in enumerate(DILS):
            n = ts // d
            for r in range(d):
                rows = _rows_of(r, n, d)
                c, s = c_ref[rows, :], s_ref[rows, :]
                for which in range(3):
                    parts = [scr.at[which * nch + j][rows, :] for j in (2 * g, 2 * g + 1)]
                    if which < 2:
                        parts = [_rope(x, c, s) for x in parts]
                    outs[which * 3 + g][r] = jnp.concatenate(parts, axis=1).astype(MXU)

    base = (IN_W - 3 * ATT_W) // LANES
    in_specs = [pl.BlockSpec((ts, LANES), lambda i, cb=base + k: (i, cb)) for k in range(3 * nch)]
    in_specs += [pl.BlockSpec((ts, LANES), lambda i: (i, 0))] * 2
    out_specs = [pl.BlockSpec((d, ts // d, ATT_O), lambda i: (0, i, 0)) for _ in range(3) for d in DILS]
    out_shape = [jax.ShapeDtypeStruct((d, S // d, ATT_O), MXU) for _ in range(3) for d in DILS]
    res = pl.pallas_call(body, grid=(S // ts,), in_specs=in_specs, out_specs=out_specs, out_shape=out_shape,
                         scratch_shapes=[pltpu.VMEM((3 * nch, ts, LANES), F32)],
                         compiler_params=_params(("arbitrary",)), name="rope_perm")(*([u] * (3 * nch)), ctab, stab)
    return [[res[which * 3 + g].reshape(S, ATT_O) for g in range(3)] for which in range(3)]


def _rope_unperm_bwd(dqkv, du, ctab, stab, ts=512):
    S = dqkv[0][0].shape[0]
    nch = ATT_W // LANES

    def body(*refs):
        ins, (c_ref, s_ref, _, o_ref, scr) = refs[:9], refs[9:]
        for g, d in enumerate(DILS):
            n = ts // d
            for r in range(d):
                rows = _rows_of(r, n, d)
                c, s = c_ref[rows, :], s_ref[rows, :]
                for which in range(3):
                    v = ins[which * 3 + g][r]
                    for half in range(2):
                        x = v[:, half * LANES:(half + 1) * LANES]
                        scr.at[which * nch + 2 * g + half][rows, :] = _rope_t(x, c, s) if which < 2 else x
        for j in range(3 * nch):
            o_ref[:, j * LANES:(j + 1) * LANES] = scr[j].astype(o_ref.dtype)

    in_specs = [pl.BlockSpec((d, ts // d, ATT_O), lambda i: (0, i, 0)) for _ in range(3) for d in DILS]
    in_specs += [pl.BlockSpec((ts, LANES), lambda i: (i, 0))] * 2 + [pl.BlockSpec(memory_space=pl.ANY)]
    args = [dqkv[which][g].reshape(d, S // d, ATT_O) for which in range(3) for g, d in enumerate(DILS)]
    last = (IN_W - 3 * ATT_W) // (3 * ATT_W)
    return pl.pallas_call(body, grid=(S // ts,), in_specs=in_specs, out_specs=pl.BlockSpec((ts, 3 * ATT_W), lambda i: (i, last)),
                          out_shape=jax.ShapeDtypeStruct((S, IN_W), MXU), scratch_shapes=[pltpu.VMEM((3 * nch, ts, LANES), F32)],
                          input_output_aliases={len(in_specs) - 1: 0},
                          compiler_params=_params(("arbitrary",)), name="rope_unperm_bwd")(*args, ctab, stab, du)


def _band_mask_keys(has_prev):
    r = lax.broadcasted_iota(jnp.int32, (QB, 2 * QB), 0)
    c = lax.broadcasted_iota(jnp.int32, (QB, 2 * QB), 1)
    return ((c < QB) & (c >= r) & has_prev) | ((c >= QB) & (c - QB <= r))


def _band_mask_queries(has_next):
    r = lax.broadcasted_iota(jnp.int32, (2 * QB, QB), 0)
    c = lax.broadcasted_iota(jnp.int32, (2 * QB, QB), 1)
    return ((r < QB) & (c <= r)) | ((r >= QB) & (c >= r - QB) & has_next)


ASUB = 4
_BIG = pl.BlockSpec((ASUB * QB, ATT_O), lambda b: (b, 0))
_PREV = pl.BlockSpec((QB, ATT_O), lambda b: (jnp.maximum(b * ASUB - 1, 0), 0))


def _sub(ref, j):
    return ref[j * QB:(j + 1) * QB]


def _attn_fwd(g, q, k, v):
    S = q.shape[0]
    nb = S // QB
    nblk = nb // DILS[g]

    def body(q_ref, kc_ref, kp_ref, vc_ref, vp_ref, o_ref, m_ref, l_ref):
        hm_kv, hm_o = _head_masks((2 * QB, ATT_O)), _head_masks((QB, ATT_O))
        for j in range(ASUB):
            ok = _band_mask_keys(((pl.program_id(0) * ASUB + j) & (nblk - 1)) > 0)
            k2 = jnp.concatenate([kp_ref[...] if j == 0 else _sub(kc_ref, j - 1), _sub(kc_ref, j)], axis=0)
            v2 = jnp.concatenate([vp_ref[...] if j == 0 else _sub(vc_ref, j - 1), _sub(vc_ref, j)], axis=0)
            qv = _sub(q_ref, j)
            o_acc = jnp.zeros((QB, ATT_O), F32)
            m_acc = jnp.zeros((QB, ATT_O), F32)
            l_acc = jnp.zeros((QB, ATT_O), F32)
            for h in range(4):
                s = jnp.where(ok, _dot_nt(qv, _only(hm_kv[h], k2)) * ATT_SCALE, NEG)
                m = jnp.max(s, axis=1, keepdims=True)
                p = jnp.exp(s - m)
                o_acc = o_acc + _dot(p.astype(MXU), _only(hm_kv[h], v2))
                m_acc = jnp.where(hm_o[h], m, m_acc)
                l_acc = jnp.where(hm_o[h], jnp.sum(p, axis=1, keepdims=True), l_acc)
            o_ref[j * QB:(j + 1) * QB] = o_acc
            m_ref[j * QB:(j + 1) * QB] = m_acc
            l_ref[j * QB:(j + 1) * QB] = l_acc

    shp = jax.ShapeDtypeStruct((S, ATT_O), F32)
    return pl.pallas_call(body, grid=(nb // ASUB,), in_specs=[_BIG, _BIG, _PREV, _BIG, _PREV],
                          out_specs=[_BIG] * 3, out_shape=[shp, shp, shp], compiler_params=_params(("arbitrary",)),
                          name=f"attn_fwd_{g}")(q, k, k, v, v)


def _natural(ref, d, scr, ts):
    if d == 1:
        return ref[0]
    n = ts // d
    for r in range(d):
        v = ref[r]
        scr.at[0][pl.ds(r, n, stride=d), :] = v[:, 0:LANES]
        scr.at[1][pl.ds(r, n, stride=d), :] = v[:, LANES:2 * LANES]
    return jnp.concatenate([scr[0], scr[1]], axis=1)


def _attn_combine(oml, ts=512):
    S = oml[0][0].shape[0]

    def body(*refs):
        ins, (att_ref, out_ref, lse_ref, scr) = refs[:9], refs[9:]
        o, m, l = [[_natural(ins[3 * g + k], d, scr, ts) for g, d in enumerate(DILS)] for k in range(3)]
        mx = jnp.maximum(jnp.maximum(m[0], m[1]), m[2])
        w = [jnp.exp(m[g] - mx) for g in range(3)]
        den = w[0] * l[0] + w[1] * l[1] + w[2] * l[2]
        out = (w[0] * o[0] + w[1] * o[1] + w[2] * o[2]) / den
        out_ref[...] = out
        att_ref[...] = out.astype(MXU)
        lse_ref[...] = mx + jnp.log(den)

    in_specs = [pl.BlockSpec((d, ts // d, ATT_O), lambda i: (0, i, 0)) for d in DILS for _ in range(3)]
    args = [a.reshape(d, S // d, ATT_O) for d, grp in zip(DILS, oml) for a in grp]
    blk = pl.BlockSpec((ts, ATT_O), lambda i: (i, 0))
    return pl.pallas_call(body, grid=(S // ts,), in_specs=in_specs, out_specs=[blk, blk, blk],
                          out_shape=[jax.ShapeDtypeStruct((S, ATT_O), MXU), jax.ShapeDtypeStruct((S, ATT_O), F32),
                                     jax.ShapeDtypeStruct((S, ATT_O), F32)],
                          scratch_shapes=[pltpu.VMEM((2, ts, LANES), F32)], compiler_params=_params(("arbitrary",)),
                          name="attn_combine")(*args)


def _attn_bwd_prep(datt, o, lse, ts=512):
    S = datt.shape[0]

    def body(da0, da1, o_ref, l0, l1, *rest):
        outs, dl = rest[:9], rest[9]
        prod = jnp.concatenate([da0[...], da1[...]], axis=1) * o_ref[...]
        delta = jnp.zeros((ts, ATT_O), F32)
        for hm in _head_masks((ts, ATT_O)):
            delta = jnp.where(hm, jnp.sum(_only(hm, prod), axis=1, keepdims=True), delta)
        dl[0] = delta[:, 0:LANES]
        dl[1] = delta[:, LANES:2 * LANES]
        for g, d in enumerate(DILS):
            n = ts // d
            for r in range(d):
                rows = _rows_of(r, n, d)
                outs[g][r] = jnp.concatenate([da0[rows, :], da1[rows, :]], axis=1).astype(MXU)
                outs[3 + g][r] = jnp.concatenate([dl.at[0][rows, :], dl.at[1][rows, :]], axis=1)
                outs[6 + g][r] = jnp.concatenate([l0[rows, :], l1[rows, :]], axis=1)

    half = lambda j: pl.BlockSpec((ts, LANES), lambda i: (i, j))
    out_specs = [pl.BlockSpec((d, ts // d, ATT_O), lambda i: (0, i, 0)) for _ in range(3) for d in DILS]
    out_shape = [jax.ShapeDtypeStruct((d, S // d, ATT_O), dt) for dt in (MXU, F32, F32) for d in DILS]
    res = pl.pallas_call(body, grid=(S // ts,), in_specs=[half(0), half(1), pl.BlockSpec((ts, ATT_O), lambda i: (i, 0)), half(0), half(1)],
                         out_specs=out_specs, out_shape=out_shape, scratch_shapes=[pltpu.VMEM((2, ts, LANES), F32)],
                         compiler_params=_params(("arbitrary",)), name="attn_bwd_prep")(datt, datt, o, lse, lse)
    return [[res[k * 3 + g].reshape(S, ATT_O) for g in range(3)] for k in range(3)]


def _head_col(x, h):
    return x[:, h * 64:h * 64 + 1]


def _attn_bwd(g, q, k, v, do, delta, lse):
    S = q.shape[0]
    nb = S // QB
    nblk = nb // DILS[g]

    def body(k_ref, v_ref, qc_ref, qn_ref, doc_ref, don_ref, dlc_ref, dln_ref, lc_ref, ln_ref, dq_ref, dk_ref, dv_ref, dq_scr):
        hms, hmk = _head_masks((2 * QB, ATT_O)), _head_masks((QB, ATT_O))
        first = pl.program_id(0) == 0

        @pl.when(first)
        def _():
            dq_scr[0:QB] = jnp.zeros((QB, ATT_O), F32)

        @pl.when(jnp.logical_not(first))
        def _():
            dq_scr[0:QB] = dq_scr[ASUB * QB:(ASUB + 1) * QB]

        dq_scr[QB:(ASUB + 1) * QB] = jnp.zeros((ASUB * QB, ATT_O), F32)

        def both(cur_ref, nxt_ref, j):
            return jnp.concatenate([_sub(cur_ref, j), nxt_ref[...] if j == ASUB - 1 else _sub(cur_ref, j + 1)], axis=0)

        for j in range(ASUB):
            ok = _band_mask_queries(((pl.program_id(0) * ASUB + j + 1) & (nblk - 1)) > 0)
            q2, do2, dl2, lse2 = both(qc_ref, qn_ref, j), both(doc_ref, don_ref, j), both(dlc_ref, dln_ref, j), both(lc_ref, ln_ref, j)
            kv, vv = _sub(k_ref, j), _sub(v_ref, j)
            dk = jnp.zeros((QB, ATT_O), F32)
            dv = jnp.zeros((QB, ATT_O), F32)
            dq2 = jnp.zeros((2 * QB, ATT_O), F32)
            for h, hm in enumerate(hms):
                qh, doh = _only(hm, q2), _only(hm, do2)
                p = jnp.where(ok, jnp.exp(_dot_nt(qh, kv) * ATT_SCALE - _head_col(lse2, h)), 0.0)
                ds = (p * (_dot_nt(doh, vv) - _head_col(dl2, h))).astype(MXU)
                dv = dv + _dot_tn(p.astype(MXU), doh)
                dk = dk + _dot_tn(ds, qh)
                dq2 = dq2 + _dot(ds, _only(hmk[h], kv))
            dk_ref[j * QB:(j + 1) * QB] = dk * ATT_SCALE
            dv_ref[j * QB:(j + 1) * QB] = dv
            dq_scr[j * QB:(j + 2) * QB] += dq2
        dq_ref[...] = dq_scr[0:ASUB * QB] * ATT_SCALE

    nxt = pl.BlockSpec((QB, ATT_O), lambda b: (jnp.minimum((b + 1) * ASUB, nb - 1), 0))
    shp = jax.ShapeDtypeStruct((S, ATT_O), F32)
    return pl.pallas_call(body, grid=(nb // ASUB,), in_specs=[_BIG, _BIG, _BIG, nxt, _BIG, nxt, _BIG, nxt, _BIG, nxt], out_specs=[_BIG] * 3,
                          out_shape=[shp, shp, shp], scratch_shapes=[pltpu.VMEM(((ASUB + 1) * QB, ATT_O), F32)],
                          compiler_params=_params(("arbitrary",)), name=f"attn_bwd_{g}")(k, v, q, q, do, do, delta, delta, lse, lse)


def _merge_fwd(x0, u, a2, yb, att, wa, wb, wc, w_out, g_post, ts=256):
    S = x0.shape[0]

    def body(i, g, x_ref, gate_ref, a2_ref, yb_ref, att_ref, wa_ref, wb_ref, wc_ref, wo_ref, gp_ref, mg_ref, y_ref, xo_ref):
        gate = lambda n: jax.nn.sigmoid(gate_ref[:, n * D:(n + 1) * D].astype(F32))
        merged = gate(0) * _dot_nt(a2_ref[...], wa_ref[...])
        merged = merged + gate(1) * _dot_nt(yb_ref[...], wb_ref[...])
        merged = merged + gate(2) * _dot_nt(att_ref[...], wc_ref[...])
        mb = merged.astype(MXU)
        mg_ref[...] = mb
        y = _dot(mb, wo_ref[...])
        y_ref[...] = y
        xo_ref[...] = x_ref[...] + _rms(y, gp_ref[...])[0]

    ins = [("t", x0, D, 0), ("t", u, GATE_W, 0), ("t", a2, POOLW, 0), ("t", yb, POOLW, 0), ("t", att, ATT_O, 0),
           ("w", wa), ("w", wb), ("w", wc), ("w", w_out), ("w", g_post)]
    return _rows_call("merge_fwd", body, S, ts, ins, [("t", D, MXU), ("t", D, F32), ("t", D, F32)])


def _merge_bwd(dx, y1, u, a2, yb, att, merged, wa, wb, wc, w_out, g_post, ts=256):
    S = dx.shape[0]
    last = S // ts - 1

    def body(i, g, dx_ref, y_ref, gate_ref, a2_ref, yb_ref, att_ref, mg_ref, wa_ref, wb_ref, wc_ref, wo_ref, gp_ref,
             dgate_ref, da2_ref, dyb_ref, datt_ref, dgp_ref, dwo_ref, dwa_ref, dwb_ref, dwc_ref, acc_o, acc_a, acc_b, acc_c):
        @pl.when(g == 0)
        def _():
            for acc in (acc_o, acc_a, acc_b, acc_c):
                acc[...] = jnp.zeros_like(acc)

        dxv, y = dx_ref[...], y_ref[...]
        dy, r = _rms_bwd(dxv * gp_ref[...], y)
        _acc(dgp_ref, g, jnp.sum(dxv * (y * r), axis=0, keepdims=True))
        dyb16 = dy.astype(MXU)
        acc_o[...] += _dot_tn(mg_ref[...], dyb16)
        dm = _dot_nt(dyb16, wo_ref[...])
        for n, (src, w_ref, din_ref, acc) in enumerate(((a2_ref, wa_ref, da2_ref, acc_a), (yb_ref, wb_ref, dyb_ref, acc_b),
                                                       (att_ref, wc_ref, datt_ref, acc_c))):
            gt = jax.nn.sigmoid(gate_ref[:, n * D:(n + 1) * D].astype(F32))
            br = _dot_nt(src[...], w_ref[...])
            dgate_ref[:, n * D:(n + 1) * D] = (dm * br * gt * (1.0 - gt)).astype(dgate_ref.dtype)
            dbr = (dm * gt).astype(MXU)
            acc[...] += _dot_tn(dbr, src[...])
            din_ref[...] = _dot(dbr, w_ref[...])

        @pl.when(g == last)
        def _():
            for out, acc in ((dwo_ref, acc_o), (dwa_ref, acc_a), (dwb_ref, acc_b), (dwc_ref, acc_c)):
                out[...] = acc[...].astype(MXU)

    ins = [("t", dx, D, 0), ("t", y1, D, 0), ("t", u, GATE_W, 0), ("t", a2, POOLW, 0), ("t", yb, POOLW, 0), ("t", att, ATT_O, 0),
           ("t", merged, D, 0), ("w", wa), ("w", wb), ("w", wc), ("w", w_out), ("w", g_post)]
    wshapes = [(D, D), (D, POOLW), (D, POOLW), (D, ATT_O)]
    outs = [("c", IN_W, GATE_W, 0, MXU), ("t", POOLW, F32), ("t", POOLW, F32), ("t", ATT_O, F32), ("a", (1, D), F32)]
    outs += [("a", s, MXU) for s in wshapes]
    return _rows_call("merge_bwd", body, S, ts, ins, outs, scratch=[pltpu.VMEM(s, F32) for s in wshapes])


def _prenorm_bwd(name, dx_res, du, wt, x, g_pre, ts=256, lead=0):
    S = x.shape[0]
    N = du.shape[1]

    def body(i, g, dx_ref, du_ref, wt_ref, x_ref, g_ref, o_ref, dg_ref):
        if lead:
            dhv = _dot(du_ref[:, 0:lead], wt_ref[N - lead:N, :]) + _dot(du_ref[:, lead:N], wt_ref[0:N - lead, :])
        else:
            dhv = _dot(du_ref[...], wt_ref[...])
        xv = x_ref[...]
        dxn, r = _rms_bwd(dhv * g_ref[...], xv)
        o_ref[...] = dx_ref[...] + dxn
        _acc(dg_ref, g, jnp.sum(dhv * (xv * r), axis=0, keepdims=True))

    ins = [("t", dx_res, D, 0), ("t", du, N, 0), ("w", wt), ("t", x, D, 0), ("w", g_pre)]
    return _rows_call(name, body, S, ts, ins, [("t", D, F32), ("a", (1, D), F32)])


def _mem_heads(qm, kv_ref):
    out = []
    for h in range(4):
        q = qm[:, h * 128:(h + 1) * 128].astype(MXU)
        k = kv_ref[:, h * 128:(h + 1) * 128]
        v = kv_ref[:, MEM_W + h * 128:MEM_W + (h + 1) * 128]
        sc = _dot_nt(q, k) * MEM_SCALE
        e = jnp.exp(sc - jnp.max(sc, axis=1, keepdims=True))
        out.append((e / jnp.sum(e, axis=1, keepdims=True), q, k, v))
    return out


def _mem_fwd(x1, kv, g_pre, w_mq, w_mo, g_post, ts=256):
    S = x1.shape[0]

    def body(i, g, x_ref, kv_ref, gq_ref, wq_ref, wo_ref, gp_ref, om_ref, y_ref, xo_ref):
        x = x_ref[...]
        hb = _rms(x, gq_ref[...])[0].astype(MXU)
        qm = _dot(hb, wq_ref[...])
        om = jnp.concatenate([_dot(p.astype(MXU), v) for p, _, _, v in _mem_heads(qm, kv_ref)], axis=1).astype(MXU)
        om_ref[...] = om
        y = _dot_nt(om, wo_ref[...])
        y_ref[...] = y
        xo_ref[...] = x + _rms(y, gp_ref[...])[0]

    ins = [("t", x1, D, 0), ("w", kv), ("w", g_pre), ("w", w_mq), ("w", w_mo), ("w", g_post)]
    return _rows_call("mem_fwd", body, S, ts, ins, [("t", MEM_W, MXU), ("t", D, F32), ("t", D, F32)])


def _mem_bwd(dx2, ym, x1, om, kv, g_pre, w_mq, w_mo, g_post, ts=256):
    S = x1.shape[0]
    last = S // ts - 1

    def body(i, g, dx_ref, y_ref, x_ref, om_ref, kv_ref, gq_ref, wq_ref, wo_ref, gp_ref, dxo_ref, dgp_ref, dgq_ref, dkv_ref,
             dwo_ref, dwq_ref, acc_o, acc_q):
        dxv, y, x = dx_ref[...], y_ref[...], x_ref[...]
        dy, r = _rms_bwd(dxv * gp_ref[...], y)
        _acc(dgp_ref, g, jnp.sum(dxv * (y * r), axis=0, keepdims=True))
        dyb = dy.astype(MXU)
        dom = _dot(dyb, wo_ref[...])
        h, r1 = _rms(x, gq_ref[...])
        hb = h.astype(MXU)
        qm = _dot(hb, wq_ref[...])
        dqs = []

        @pl.when(g == 0)
        def _():
            dkv_ref[...] = jnp.zeros_like(dkv_ref)
            acc_o[...] = jnp.zeros_like(acc_o)
            acc_q[...] = jnp.zeros_like(acc_q)

        acc_o[...] += _dot_tn(dyb, om_ref[...])

        for hh, (p, q, k, v) in enumerate(_mem_heads(qm, kv_ref)):
            doh = dom[:, hh * 128:(hh + 1) * 128].astype(MXU)
            dp = _dot_nt(doh, v)
            dsc = (p * (dp - jnp.sum(dp * p, axis=1, keepdims=True)) * MEM_SCALE).astype(MXU)
            dqs.append(_dot(dsc, k))
            dkv_ref[:, hh * 128:(hh + 1) * 128] += _dot_tn(dsc, q)
            dkv_ref[:, MEM_W + hh * 128:MEM_W + (hh + 1) * 128] += _dot_tn(p.astype(MXU), doh)
        dq = jnp.concatenate(dqs, axis=1).astype(MXU)
        acc_q[...] += _dot_tn(hb, dq)
        dh = _dot_nt(dq, wq_ref[...])
        _acc(dgq_ref, g, jnp.sum(dh * (x * r1), axis=0, keepdims=True))
        dxo_ref[...] = dxv + _rms_bwd(dh * gq_ref[...], x)[0]

        @pl.when(g == last)
        def _():
            dwo_ref[...] = acc_o[...].astype(MXU)
            dwq_ref[...] = acc_q[...].astype(MXU)

    ins = [("t", dx2, D, 0), ("t", ym, D, 0), ("t", x1, D, 0), ("t", om, MEM_W, 0), ("w", kv), ("w", g_pre), ("w", w_mq), ("w", w_mo),
           ("w", g_post)]
    outs = [("t", D, F32), ("a", (1, D), F32), ("a", (1, D), F32), ("a", (256, D), F32), ("a", (D, MEM_W), MXU), ("a", (D, MEM_W), MXU)]
    return _rows_call("mem_bwd", body, S, ts, ins, outs, scratch=[pltpu.VMEM((D, MEM_W), F32), pltpu.VMEM((D, MEM_W), F32)])


def _gain_grad(name, dn, x):
    n = x.shape[0]

    def body(i, g, dn_ref, x_ref, o_ref):
        xv = x_ref[...]
        r = lax.rsqrt(jnp.mean(xv * xv, axis=-1, keepdims=True) + EPS)
        o_ref[...] = jnp.sum(dn_ref[...] * (xv * r), axis=0, keepdims=True)

    return _rows_call(name, body, n, n, [("t", dn, D, 0), ("t", x, D, 0)], [("a", (1, D), F32)])[0]


def _ffn_fwd(x2, u3, conv_f, w_down, g_post, ts=256):
    S = x2.shape[0]

    def body(i, g, x_ref, ua_ref, ub_ref, cw_ref, wd_ref, gp_ref, act_ref, y_ref, xo_ref, c_ref, cu):
        @pl.when(g == 0)
        def _():
            cu[...] = jnp.zeros_like(cu)

        ua = ua_ref[...].astype(F32)
        c, _, _ = _conv3(ua, cu[...], cw_ref[...])
        c_ref[...] = c.astype(MXU)
        act = (c * jax.nn.sigmoid(c) * ub_ref[...].astype(F32)).astype(MXU)
        act_ref[...] = act
        y = _dot(act, wd_ref[...])
        y_ref[...] = y
        xo_ref[...] = x_ref[...] + _rms(y, gp_ref[...])[0]
        cu[...] = ua[ts - 8:]

    ins = [("t", x2, D, 0), ("t", u3, D_FF, 0), ("t", u3, D_FF, 1), ("w", conv_f), ("w", w_down), ("w", g_post)]
    return _rows_call("ffn_fwd", body, S, ts, ins, [("t", D_FF, MXU), ("t", D, F32), ("t", D, F32), ("t", D_FF, MXU)],
                      scratch=[pltpu.VMEM((8, D_FF), F32)])


def _ffn_bwd(dx3, y3, u3, c, conv_f, w_down, g_post, ts=128):
    S = dx3.shape[0]

    def body(i, g, dx_ref, y_ref, ua_ref, ub_ref, c_ref, cw_ref, wd_ref, gp_ref, dy_ref, du_ref, dgp_ref, dcw_ref, cdc):
        @pl.when(g == 0)
        def _():
            cdc[...] = jnp.zeros_like(cdc)

        dxv, y = dx_ref[...], y_ref[...]
        dy, r = _rms_bwd(dxv * gp_ref[...], y)
        _acc(dgp_ref, g, jnp.sum(dxv * (y * r), axis=0, keepdims=True))
        dyb = dy.astype(MXU)
        dy_ref[...] = dyb
        dact = _dot_nt(dyb, wd_ref[...])
        ua, c, w = ua_ref[...].astype(F32), c_ref[...].astype(F32), cw_ref[...]
        sg = jax.nn.sigmoid(c)
        du_ref[:, D_FF:2 * D_FF] = (dact * (c * sg)).astype(du_ref.dtype)
        dc = dact * ub_ref[...].astype(F32) * (sg * (1.0 + c * (1.0 - sg)))
        dua, dc1, dc2 = _conv3_t(dc, cdc[...], w, shifted=True)
        du_ref[:, 0:D_FF] = dua.astype(du_ref.dtype)
        dw = jnp.concatenate([jnp.sum(ua * dc2, axis=0, keepdims=True), jnp.sum(ua * dc1, axis=0, keepdims=True),
                              jnp.sum(ua * dc, axis=0, keepdims=True)], axis=0)
        _acc(dcw_ref, g, dw)
        cdc[...] = dc[:8]

    ins = [("t", dx3, D, 0), ("t", y3, D, 0), ("t", u3, D_FF, 0), ("t", u3, D_FF, 1), ("t", c, D_FF, 0), ("w", conv_f),
           ("w", w_down), ("w", g_post)]
    outs = [("t", D, MXU), ("t", 2 * D_FF, MXU), ("a", (1, D), F32), ("a", (3, D_FF), F32)]
    return _rows_call("ffn_bwd", body, S, ts, ins, outs, scratch=[pltpu.VMEM((8, D_FF), F32)], reverse=True)


def _loss_head(x, target, ts=512):
    S = x.shape[0]

    def body(i, g, x_ref, t_ref, dx_ref, acc_ref):
        diff = x_ref[...] - t_ref[...]
        dx_ref[...] = diff * (1.0 / D)
        col = jnp.sum(diff * diff, axis=0, keepdims=True)
        part = col[:, 0:LANES]
        for j in range(1, D // LANES):
            part = part + col[:, j * LANES:(j + 1) * LANES]
        row = lax.broadcasted_iota(jnp.int32, (8, LANES), 0)
        _acc(acc_ref, g, jnp.where(row == 0, jnp.broadcast_to(part, (8, LANES)), 0.0))

    return _rows_call("loss_head", body, S, ts, [("t", x, D, 0), ("t", target, D, 0)], [("t", D, F32), ("a", (8, LANES), F32)])


_OPERAND_NAME = dict(w_in='w_in', w_branch_a='wa', w_branch_b='wb', w_branch_c='wc', w_out='w_out', w_mq='w_mq', w_mkv='w_mkv',
                     w_mo='w_mo', w_up='w_up', w_down='w_down')


def _big_operands(big):
    return {_OPERAND_NAME[n]: a for n, a in big.items()}


def _layer_weights(big, small, l):
    pool_w = small['pool_w'][l].astype(MXU)
    wblk = jnp.zeros((POOLW, POOLW), MXU)
    for g in range(4):
        wblk = lax.dynamic_update_slice(wblk, pool_w[g], (g * 96, g * 96))
    vec = lambda n: small[n][l].reshape(1, -1)
    return dict(
        _big_operands(big),
        wblk=wblk, pool_scale=vec('pool_scale'), conv_b=small['conv_b_w'][l], conv_f=small['conv_ffn_w'][l],
        g_mix_pre=vec('norm_mix_pre'), g_mix_post=vec('norm_mix_post'), g_mem_pre=vec('norm_mem_pre'),
        g_mem_post=vec('norm_mem_post'), g_memkv=vec('norm_memkv'), g_ffn_pre=vec('norm_ffn_pre'), g_ffn_post=vec('norm_ffn_post'))


def _layer_fwd(x0, mem, W, ctab, stab):
    sv = _layer_fwd_mix(x0, W, ctab, stab)
    return _layer_fwd_late(mem, W, sv), sv


def _layer_fwd_mix(x0, W, ctab, stab):
    return _layer_fwd_merge(W, _layer_fwd_branches(x0, W, ctab, stab))


def _layer_fwd_branches(x0, W, ctab, stab):
    sv = dict(x0=x0)
    sv['u'], sv['h1'] = _norm_mm("in_proj", x0, W['g_mix_pre'], W['w_in'], ts=2048, tn=IN_TILE, wt=True, rot=IN_ROT, out_dtype=MXU)
    sv['a2'], sv['yb'] = _poolconv_fwd(sv['u'], W['wblk'], W['pool_scale'], W['conv_b'])
    sv['qkv'] = q3, k3, v3 = _rope_perm(sv['u'], ctab, stab)
    sv['att'], sv['o'], sv['lse'] = _attn_combine([_attn_fwd(g, q3[g], k3[g], v3[g]) for g in range(3)])
    return sv


def _layer_fwd_merge(W, sv):
    sv['merged'], sv['y1'], sv['x1'] = _merge_fwd(sv['x0'], sv['u'], sv['a2'], sv['yb'], sv['att'], W['wa'], W['wb'], W['wc'],
                                                  W['w_out'], W['g_mix_post'])
    return sv


def _layer_fwd_late(mem, W, sv):
    sv['kv'], sv['memn'] = _norm_mm("mem_kv", mem, W['g_memkv'], W['w_mkv'], ts=256, tn=D, out_dtype=MXU)
    sv['om'], sv['ym'], sv['x2'] = _mem_fwd(sv['x1'], sv['kv'], W['g_mem_pre'], W['w_mq'], W['w_mo'], W['g_mem_post'])
    sv['u3'], sv['h3'] = _norm_mm("up_proj", sv['x2'], W['g_ffn_pre'], W['w_up'], ts=2048, tn=1408, wt=True, out_dtype=MXU)
    sv['act'], sv['y3'], x3, sv['c3'] = _ffn_fwd(sv['x2'], sv['u3'], W['conv_f'], W['w_down'], W['g_ffn_post'])
    return x3


def _layer_bwd(dx3, mem, W, sv, ctab, stab):
    dx1, g = _layer_bwd_late(dx3, mem, W, sv)
    dx0, g_mix = _layer_bwd_mix(dx1, W, sv, ctab, stab)
    return dx0, {**g, **g_mix}


def _layer_bwd_late(dx3, mem, W, sv):
    g = {}
    dy3, du3, g['norm_ffn_post'], g['conv_ffn_w'] = _ffn_bwd(dx3, sv['y3'], sv['u3'], sv['c3'], W['conv_f'], W['w_down'], W['g_ffn_post'])
    g['w_down'] = _mm_tn("dw_down", sv['act'], dy3, cap_k=256)
    g['w_up'] = _mm_tn("dw_up", du3, sv['h3'])
    dx2, g['norm_ffn_pre'] = _prenorm_bwd("ffn_pre_bwd", dx3, du3, W['w_up'], sv['x2'], W['g_ffn_pre'], ts=512)
    dx1, g['norm_mem_post'], g['norm_mem_pre'], dkv, g['w_mo'], g['w_mq'] = _mem_bwd(
        dx2, sv['ym'], sv['x1'], sv['om'], sv['kv'], W['g_mem_pre'], W['w_mq'], W['w_mo'], W['g_mem_post'])
    dkvb = dkv.astype(MXU)
    g['w_mkv'] = _mm_tn("dw_mkv", sv['memn'], dkvb)
    g['norm_memkv'] = _gain_grad("memkv_gain", _mm_nt("d_memn", dkvb, W['w_mkv'], ts=256, tn=512), mem)
    return dx1, g


def _layer_bwd_mix(dx1, W, sv, ctab, stab):
    du, g = _layer_bwd_mixers(dx1, W, sv, ctab, stab)
    g['w_in'] = _dw_in(du, sv)
    dx0, g['norm_mix_pre'] = _mix_pre_bwd(dx1, du, W, sv)
    return dx0, g


def _dw_in(du, sv):
    return _mm_tn("dw_in", du, sv['h1'], cap_k=IN_TILE, rot=IN_ROT)


def _mix_pre_bwd(dx1, du, W, sv):
    return _prenorm_bwd("mix_pre_bwd", dx1, du, W['w_in'], sv['x0'], W['g_mix_pre'], lead=GATE_W)


def _layer_bwd_mixers(dx1, W, sv, ctab, stab):
    parts, g = _layer_bwd_merge(dx1, W, sv)
    du, g_br = _layer_bwd_branches(parts, W, sv, ctab, stab)
    return du, {**g, **g_br}


def _layer_bwd_merge(dx1, W, sv):
    g = {}
    du, da2, dyb, datt, g['norm_mix_post'], g['w_out'], g['w_branch_a'], g['w_branch_b'], g['w_branch_c'] = _merge_bwd(
        dx1, sv['y1'], sv['u'], sv['a2'], sv['yb'], sv['att'], sv['merged'], W['wa'], W['wb'], W['wc'], W['w_out'], W['g_mix_post'])
    return (du, da2, dyb, datt), g


def _layer_bwd_branches(parts, W, sv, ctab, stab):
    du, da2, dyb, datt = parts
    g = {}
    du, g['pool_scale'], dwblk, g['conv_b_w'] = _poolconv_bwd(sv['u'], da2, dyb, du, W['wblk'], W['pool_scale'], W['conv_b'])
    g['pool_w'] = jnp.stack([dwblk[k * 96:(k + 1) * 96, k * 96:(k + 1) * 96] for k in range(4)])
    q3, k3, v3 = sv['qkv']
    do3, dl3, lse3 = _attn_bwd_prep(datt, sv['o'], sv['lse'])
    dqkv3 = [_attn_bwd(i, q3[i], k3[i], v3[i], do3[i], dl3[i], lse3[i]) for i in range(3)]
    du = _rope_unperm_bwd([[t[which] for t in dqkv3] for which in range(3)], du, ctab, stab)
    return du, g


def _local_step(x, mem, positions, target, big, small):
    ctab, stab = _rope_tables(positions)
    Ws = [_layer_weights(big[l], small, l) for l in range(DEPTH)]
    saved = []
    for l in range(DEPTH):
        x, sv = _layer_fwd(x, mem, Ws[l], ctab, stab)
        saved.append(sv)
    dx, acc = _loss_head(x, target)
    loss = jnp.sum(acc) * (0.5 / D)
    grads = [None] * DEPTH
    for l in reversed(range(DEPTH)):
        dx, grads[l] = _layer_bwd(dx, mem, Ws[l], saved[l], ctab, stab)
    return loss, dx, grads


_HBM = pl.BlockSpec(memory_space=pl.ANY)
MESH_ID = pl.DeviceIdType.MESH


def _all_gather(name, xs):
    n = len(xs)

    def body(*refs):
        x_refs, out_refs = refs[:n], refs[n:2 * n]
        send_sems, recv_sems, local_sems = refs[2 * n:]
        x, y, c = lax.axis_index("x"), lax.axis_index("y"), lax.axis_index("c")
        me, sibling = (x, y, c), (x, y, 1 - c)
        chips = [(1 - x, y), (x, 1 - y), (1 - x, 1 - y)]

        def slot(a, p):
            return out_refs[a].at[4 * p[0] + 2 * p[1] + p[2]]

        def copy(a, k, block, to, src=None):
            return pltpu.make_async_remote_copy(src_ref=slot(a, block) if src is None else src, dst_ref=slot(a, block),
                                                send_sem=send_sems.at[a, k], recv_sem=recv_sems.at[a, k], device_id=to,
                                                device_id_type=MESH_ID)

        started = []
        for a in range(n):
            mine = pltpu.make_async_copy(x_refs[a], slot(a, me), local_sems.at[a])
            mine.start()
            started.append(mine)
        first = []
        for a in range(n):
            first.append(copy(a, 0, me, sibling, src=x_refs[a]))
            first += [copy(a, 1 + j, me, (*chip, c), src=x_refs[a]) for j, chip in enumerate(chips)]
        for cp in first:
            cp.start()
        passed = []
        for j, chip in enumerate(chips):
            for a in range(n):
                copy(a, 1 + j, (*chip, c), me).wait_recv()
                fw = copy(a, 4 + j, (*chip, c), sibling)
                fw.start()
                passed.append(fw)
        for a in range(n):
            copy(a, 0, sibling, me).wait_recv()
            for j, chip in enumerate(chips):
                copy(a, 4 + j, (*chip, 1 - c), me).wait_recv()
        for cp in first + passed:
            cp.wait_send()
        for mine in started:
            mine.wait()

    return pl.pallas_call(
        body, out_shape=[jax.ShapeDtypeStruct((N_DEV,) + x.shape, x.dtype) for x in xs], in_specs=[_HBM] * n, out_specs=[_HBM] * n,
        scratch_shapes=[pltpu.SemaphoreType.DMA((n, 7)), pltpu.SemaphoreType.DMA((n, 7)), pltpu.SemaphoreType.DMA((n,))],
        name=name)(*xs)


_SEM =pl.BlockSpec(memory_space=pltpu.SEMAPHORE)
_IN_HBM = pl.BlockSpec(memory_space=pltpu.HBM)
_SIDE_EFFECT = pltpu.SideEffectType.DATAFLOW_SIDE_EFFECTING


def _push_copies(src_refs, land_refs, send_sems, recv_sems, per_peer):
    x, y, c = lax.axis_index("x"), lax.axis_index("y"), lax.axis_index("c")
    me = 4 * x + 2 * y + c
    copies = []
    for r in range(1, N_DEV):
        px, py, pc = x ^ ((r >> 2) & 1), y ^ ((r >> 1) & 1), c ^ (r & 1)
        for a, (s, d) in enumerate(zip(src_refs, land_refs)):
            k = a * (N_DEV - 1) + r - 1
            copies.append(pltpu.make_async_remote_copy(src_ref=s.at[4 * px + 2 * py + pc] if per_peer else s, dst_ref=d.at[me],
                                                       send_sem=send_sems.at[k], recv_sem=recv_sems.at[k],
                                                       device_id=(px, py, pc), device_id_type=MESH_ID))
    return copies


def _push_start(name, srcs, per_peer, after):
    n = len(srcs)
    lands = [lax.empty((N_DEV,) + (s.shape[1:] if per_peer else s.shape), s.dtype) for s in srcs]

    def body(*refs):
        for cp in _push_copies(refs[:n], refs[n:2 * n], refs[2 * n + 1], refs[2 * n + 2], per_peer):
            cp.start()
        refs[-1][...] = jnp.zeros_like(refs[-1])

    hbm = [pltpu.HBM(a.shape, a.dtype) for a in (*srcs, *lands)]
    sems = pltpu.SemaphoreType.DMA((n * (N_DEV - 1),))
    out = pl.pallas_call(
        body, name=name, out_shape=(sems, sems, *hbm, jax.ShapeDtypeStruct((8, LANES), F32)),
        in_specs=[_IN_HBM] * (2 * n) + [pl.BlockSpec(memory_space=pl.ANY)],
        out_specs=(_SEM, _SEM, *[_IN_HBM] * (2 * n), pl.BlockSpec(memory_space=pltpu.VMEM)),
        input_output_aliases={a: 2 + a for a in range(2 * n)},
        compiler_params=pltpu.CompilerParams(has_side_effects=_SIDE_EFFECT),
    )(*[pltpu.with_memory_space_constraint(a, pltpu.HBM) for a in (*srcs, *lands)], after)
    return out[0], out[1], out[2:2 + n], out[2 + n:2 + 2 * n], out[-1]


def _push_wait(name, started, per_peer, after):
    send_sems, recv_sems, srcs, lands, _ = started
    n = len(srcs)

    def body(*refs):
        for cp in _push_copies(refs[:n], refs[n:2 * n], refs[2 * n], refs[2 * n + 1], per_peer):
            cp.wait_send()
            cp.wait_recv()

    out = pl.pallas_call(
        body, name=name, out_shape=[pltpu.HBM(a.shape, a.dtype) for a in (*srcs, *lands)],
        in_specs=[_IN_HBM] * (2 * n) + [_SEM, _SEM, pl.BlockSpec(memory_space=pl.ANY)], out_specs=[_IN_HBM] * (2 * n),
        input_output_aliases={a: a for a in range(2 * n)},
        compiler_params=pltpu.CompilerParams(has_side_effects=_SIDE_EFFECT),
    )(*srcs, *lands, send_sems, recv_sems, after)
    if per_peer:
        return list(zip(out[:n], out[n:]))
    return _with_own(out[n:], out[:n], _my_slot())


def _my_slot():
    return 4 * lax.axis_index("x") + 2 * lax.axis_index("y") + lax.axis_index("c")


def _row_tile(rows, cols, budget):
    if rows * cols * 4 <= budget or rows % 16:
        return rows
    best = 16
    for t in range(16, rows + 1, 16):
        if rows % t == 0 and t * cols * 4 <= budget:
            best = t
    return best


def _slot_total(r_ref, own_ref):
    me = _my_slot()
    g = jnp.where(me == 0, own_ref[...], r_ref[0]).astype(F32)
    for k in range(1, N_DEV):
        g = g + jnp.where(me == k, own_ref[...], r_ref[k]).astype(F32)
    return g


def _sum_slots(name, pushed):
    src, recv = pushed
    _, R, C = recv.shape
    tr = _row_tile(R, C, 1 << 20)

    def body(r_ref, own_ref, o_ref):
        o_ref[...] = _slot_total(r_ref, own_ref)

    return pl.pallas_call(body, grid=(R // tr,),
                          in_specs=[pl.BlockSpec((N_DEV, tr, C), lambda i: (0, i, 0)), pl.BlockSpec((None, tr, C), lambda i: (_my_slot(), i, 0))],
                          out_specs=pl.BlockSpec((tr, C), lambda i: (i, 0)), out_shape=jax.ShapeDtypeStruct((R, C), F32),
                          compiler_params=_params(("arbitrary",)), name=name)(recv, src)


def _adamw_step(gv, w_ref, m_ref, v_ref, d_ref, mo_ref, vo_ref):
    mn = ADAM_B1 * m_ref[...] + (1.0 - ADAM_B1) * gv
    vn = ADAM_B2 * v_ref[...] + (1.0 - ADAM_B2) * (gv * gv)
    mo_ref[...] = mn
    vo_ref[...] = vn
    c1 = 1.0 - ADAM_B1 ** ADAM_STEP
    c2 = 1.0 - ADAM_B2 ** ADAM_STEP
    d_ref[...] = -ADAM_LR * ((mn / c1) / (jnp.sqrt(vn / c2) + ADAM_EPS) + ADAM_WD * w_ref[...])


def _adamw(name, g, w, m, v):
    shape = w.shape
    R, C = shape[-2], shape[-1]
    view = (-1, R, C)
    L = w.size // (R * C)
    tr = _row_tile(R, C, 1 << 20)

    def body(g_ref, w_ref, m_ref, v_ref, d_ref, mo_ref, vo_ref):
        _adamw_step(g_ref[...], w_ref, m_ref, v_ref, d_ref, mo_ref, vo_ref)

    blk = pl.BlockSpec((None, tr, C), lambda l, i: (l, i, 0))
    shp = jax.ShapeDtypeStruct((L, R, C), F32)
    outs = pl.pallas_call(body, grid=(L, R // tr), in_specs=[blk, blk, blk, blk], out_specs=[blk, blk, blk], out_shape=[shp, shp, shp],
                          compiler_params=_params(("arbitrary", "arbitrary")), name=name)(*[a.reshape(view) for a in (g, w, m, v)])
    return [o.reshape(shape) for o in outs]


def _sum_adamw(name, pushed, w, m, v):
    L, R, C = w.shape
    tr = _row_tile(R, C, 1 << 20)
    n_i = R // tr

    def body(*refs):
        shares, (w_ref, m_ref, v_ref, g_ref, d_ref, mo_ref, vo_ref) = refs[:2 * L], refs[2 * L:]
        for k in range(L):
            @pl.when(pl.program_id(0) == k)
            def _(k=k):
                g_ref[...] = _slot_total(shares[2 * k], shares[2 * k + 1])
        _adamw_step(g_ref[...], w_ref, m_ref, v_ref, d_ref, mo_ref, vo_ref)

    def during(k):
        return lambda l, i: jnp.where(l == k, i, jnp.where(l < k, 0, n_i - 1))

    in_specs, operands = [], []
    for k, (src, recv) in enumerate(pushed):
        in_specs += [pl.BlockSpec((N_DEV, tr, C), lambda l, i, at=during(k): (0, at(l, i), 0)),
                     pl.BlockSpec((None, tr, C), lambda l, i, at=during(k): (_my_slot(), at(l, i), 0))]
        operands += [recv, src]
    blk = pl.BlockSpec((None, tr, C), lambda l, i: (l, i, 0))
    shp = jax.ShapeDtypeStruct((L, R, C), F32)
    return pl.pallas_call(body, grid=(L, n_i), in_specs=in_specs + [blk, blk, blk], out_specs=[blk] * 4, out_shape=[shp] * 4,
                          compiler_params=_params(("arbitrary", "arbitrary")), name=name)(*operands, w, m, v)


def _pad_flat(a, n):
    a = a.reshape(-1)
    return jnp.pad(a, (0, n - a.shape[0]))


def _seg(n):
    return -(-n // FLAT_ALIGN) * FLAT_ALIGN


def _to_blocks(full, axis):
    shp = full.shape
    return jnp.moveaxis(full.reshape(shp[:axis] + (N_DEV, shp[axis] // N_DEV) + shp[axis + 1:]), axis, 0)


def _from_blocks(blocks, axis):
    b = jnp.moveaxis(blocks, 0, axis)
    shp = b.shape
    return b.reshape(shp[:axis] + (shp[axis] * shp[axis + 1],) + shp[axis + 2:])


def _as_rows(shard, n):
    return shard.T if SHARD_AXIS[n] == 2 else shard


def _with_own(lands, own, me):
    return [lax.dynamic_update_slice(land, o[None], (me, 0, 0)) for land, o in zip(lands, own)]


def kernel(x, mem, positions, norm_mix_pre, norm_mix_post, w_in, pool_w, pool_scale, conv_b_w, w_branch_a, w_branch_b, w_branch_c, w_out, norm_mem_pre, norm_mem_post, norm_memkv, w_mq, w_mkv, w_mo, norm_ffn_pre, norm_ffn_post, w_up, conv_ffn_w, w_down, loss_target, m_norm_mix_pre, m_norm_mix_post, m_w_in, m_pool_w, m_pool_scale, m_conv_b_w, m_w_branch_a, m_w_branch_b, m_w_branch_c, m_w_out, m_norm_mem_pre, m_norm_mem_post, m_norm_memkv, m_w_mq, m_w_mkv, m_w_mo, m_norm_ffn_pre, m_norm_ffn_post, m_w_up, m_conv_ffn_w, m_w_down, v_norm_mix_pre, v_norm_mix_post, v_w_in, v_pool_w, v_pool_scale, v_conv_b_w, v_w_branch_a, v_w_branch_b, v_w_branch_c, v_w_out, v_norm_mem_pre, v_norm_mem_post, v_norm_memkv, v_w_mq, v_w_mkv, v_w_mo, v_norm_ffn_pre, v_norm_ffn_post, v_w_up, v_conv_ffn_w, v_w_down):
    w = dict(norm_mix_pre=norm_mix_pre, norm_mix_post=norm_mix_post, w_in=w_in, pool_w=pool_w, pool_scale=pool_scale, conv_b_w=conv_b_w, w_branch_a=w_branch_a, w_branch_b=w_branch_b, w_branch_c=w_branch_c, w_out=w_out, norm_mem_pre=norm_mem_pre, norm_mem_post=norm_mem_post, norm_memkv=norm_memkv, w_mq=w_mq, w_mkv=w_mkv, w_mo=w_mo, norm_ffn_pre=norm_ffn_pre, norm_ffn_post=norm_ffn_post, w_up=w_up, conv_ffn_w=conv_ffn_w, w_down=w_down)
    m = dict(norm_mix_pre=m_norm_mix_pre, norm_mix_post=m_norm_mix_post, w_in=m_w_in, pool_w=m_pool_w, pool_scale=m_pool_scale, conv_b_w=m_conv_b_w, w_branch_a=m_w_branch_a, w_branch_b=m_w_branch_b, w_branch_c=m_w_branch_c, w_out=m_w_out, norm_mem_pre=m_norm_mem_pre, norm_mem_post=m_norm_mem_post, norm_memkv=m_norm_memkv, w_mq=m_w_mq, w_mkv=m_w_mkv, w_mo=m_w_mo, norm_ffn_pre=m_norm_ffn_pre, norm_ffn_post=m_norm_ffn_post, w_up=m_w_up, conv_ffn_w=m_conv_ffn_w, w_down=m_w_down)
    v = dict(norm_mix_pre=v_norm_mix_pre, norm_mix_post=v_norm_mix_post, w_in=v_w_in, pool_w=v_pool_w, pool_scale=v_pool_scale, conv_b_w=v_conv_b_w, w_branch_a=v_w_branch_a, w_branch_b=v_w_branch_b, w_branch_c=v_w_branch_c, w_out=v_w_out, norm_mem_pre=v_norm_mem_pre, norm_mem_post=v_norm_mem_post, norm_memkv=v_norm_memkv, w_mq=v_w_mq, w_mkv=v_w_mkv, w_mo=v_w_mo, norm_ffn_pre=v_norm_ffn_pre, norm_ffn_post=v_norm_ffn_post, w_up=v_w_up, conv_ffn_w=v_conv_ffn_w, w_down=v_w_down)

    mix_big = [n for n in BIG if n not in LATE_BIG]
    block = lambda names, l: [_as_rows(w[n][l], n).astype(MXU) for n in names]
    conv = jnp.concatenate([_pad_flat(w[n], _seg(w[n].size)) for n in F32_GATHERED]).reshape(-1, LANES)
    groups = dict(m=MERGE_BIG, b=LATE_BIG, a=mix_big)
    got0 = _all_gather("weights_all_gather_0", block(['w_in'], 0) + [conv])
    conv_all = got0[-1].reshape(N_DEV, -1)
    small, off = {n: w[n] for n in WEIGHTS if n not in SHARD_AXIS}, 0
    for n in F32_GATHERED:
        small[n] = _from_blocks(conv_all[:, off:off + w[n].size].reshape((N_DEV,) + w[n].shape), 2)
        off += _seg(w[n].size)
    whole = lambda names, got: {n: o.reshape(-1, o.shape[-1]) for n, o in zip(names, got)}
    pushes, after = {}, got0[0]
    for tag, l in (('m', 0), ('b', 0), ('a', 1), ('b', 1)):
        pushes[tag, l] = _push_start(f"weights_push_start_{l}{tag}", block(groups[tag], l), False, after)
        after = pushes[tag, l][4]

    def arrived(tag, l, done):
        return _big_operands(whole(groups[tag], _push_wait(f"weights_push_wait_{l}{tag}", pushes[tag, l], False, done)))

    ctab, stab = _rope_tables(positions[0])
    W0 = _layer_weights(whole(['w_in'], got0), small, 0)
    sv0 = _layer_fwd_branches(x[0], dict(W0, g_mix_pre=W0['g_mix_pre'] + after[0, 0]), ctab, stab)
    W0.update(arrived('m', 0, sv0['att']))
    sv0 = _layer_fwd_merge(W0, sv0)
    W0.update(arrived('b', 0, sv0['x1']))
    x1 = _layer_fwd_late(mem[0], W0, sv0)
    W1 = _layer_weights({}, small, 1)
    W1.update(arrived('a', 1, x1))
    sv1 = _layer_fwd_mix(x1, W1, ctab, stab)
    W1.update(arrived('b', 1, sv1['x1']))
    x2 = _layer_fwd_late(mem[0], W1, sv1)
    dx, acc = _loss_head(x2, loss_target[0])
    loss = lax.psum(jnp.sum(acc) * (0.5 / D), MESH_AXES)
    grads = [None] * DEPTH
    dx, grads[1] = _layer_bwd(dx, mem[0], W1, sv1, ctab, stab)
    sent = [None, [grads[1][n].reshape(N_DEV, -1, grads[1][n].shape[-1]) for n in BIG]]
    push_g = _push_start("grads_push_start_1", sent[1], True, dx)
    dx, g_late = _layer_bwd_late(dx, mem[0], dict(W0, g_ffn_post=W0['g_ffn_post'] + push_g[4][0, 0]), sv0)
    sent_late = [g_late[n].reshape(N_DEV, -1, g_late[n].shape[-1]) for n in LATE_BIG]
    push_l = _push_start("grads_push_start_0", sent_late, True, dx)
    parts, g_mix = _layer_bwd_merge(dx, dict(W0, g_mix_post=W0['g_mix_post'] + push_l[4][0, 0]), sv0)
    sent_merge = [g_mix[n].reshape(N_DEV, -1, g_mix[n].shape[-1]) for n in MERGE_BIG]
    push_m = _push_start("grads_push_start_0m", sent_merge, True, parts[0])
    du, g_br = _layer_bwd_branches(parts, dict(W0, pool_scale=W0['pool_scale'] + push_m[4][0, 0]), sv0, ctab, stab)
    g_mix.update(g_br)
    g_mix['w_in'] = _dw_in(du, sv0)
    sent_in = [g_mix['w_in'].reshape(N_DEV, -1, D)]
    push_i = _push_start("grads_push_start_in", sent_in, True, du)
    dx, g_mix['norm_mix_pre'] = _mix_pre_bwd(dx, du, dict(W0, g_mix_pre=W0['g_mix_pre'] + push_i[4][0, 0]), sv0)
    grads[0] = {**g_late, **g_mix}
    recv1 = _push_wait("grads_push_wait_1", push_g, True, dx)
    recv_late = _push_wait("grads_push_wait_0", push_l, True, dx)
    recv_merge = _push_wait("grads_push_wait_0m", push_m, True, dx)

    misc_names = [n for n in WEIGHTS if n not in BIG]
    stacked = {n: jnp.stack([grads[l][n].reshape(small[n].shape[1:]) for l in range(DEPTH)]) for n in misc_names}
    rows = [(_to_blocks(stacked[n], 2) if n in SHARD_AXIS else jnp.broadcast_to(stacked[n][None], (N_DEV,) + stacked[n].shape))
            for n in misc_names]
    segs = [_seg(w[n].size) for n in misc_names]
    misc = jnp.concatenate([jnp.pad(r.reshape(N_DEV, -1), ((0, 0), (0, s - r[0].size))) for r, s in zip(rows, segs)],
                           axis=1).reshape(N_DEV, -1, LANES)
    push_x = _push_start("grads_push_start_small", [misc], True, dx)
    g_out, shares = {}, {}
    for l, names, recv in ((1, BIG, recv1), (0, LATE_BIG, recv_late), (0, MERGE_BIG, recv_merge)):
        for n, r in zip(names, recv):
            shares[n, l] = r

    swap = lambda a: jnp.swapaxes(a, 1, 2)

    def update(n):
        if n not in BIG:
            return [g_out[n], *_adamw(f"adamw_{n}", g_out[n], w[n], m[n], v[n])]
        of_layers = [shares[n, l] for l in range(DEPTH)]
        if SHARD_AXIS[n] == 1:
            return _sum_adamw(f"adamw_{n}", of_layers, w[n], m[n], v[n])
        if w[n].shape[2] % LANES:
            return [swap(a) for a in _sum_adamw(f"adamw_{n}", of_layers, swap(w[n]), swap(m[n]), swap(v[n]))]
        g = swap(jnp.stack([_sum_slots(f"sum_{n}_{l}", s) for l, s in enumerate(of_layers)]))
        return [g, *_adamw(f"adamw_{n}", g, w[n], m[n], v[n])]

    done = {n: update(n) for n in BIG if n != 'w_in'}
    shares['w_in', 0] = _push_wait("grads_push_wait_in", push_i, True, done[BIG[-1]][1])[0]
    done['w_in'] = update('w_in')
    after_w_in = swap(done['w_in'][1])
    misc_sum = _sum_slots("sum_misc", _push_wait("grads_push_wait_small", push_x, True, after_w_in)[0]).reshape(-1)
    off = 0
    for n, s in zip(misc_names, segs):
        g_out[n] = misc_sum[off:off + w[n].size].reshape(w[n].shape)
        done[n] = update(n)
        off += s
    return (loss, dx[None], *[done[n][k] for k in range(4) for n in WEIGHTS])
```

```python
import jax
import jax.numpy as jnp
from jax import lax
from jax.experimental import pallas as pl
from jax.experimental.pallas import tpu as pltpu

F32 = jnp.float32
MXU = jnp.bfloat16

D = 1024
DEPTH = 2
POOLW = 384
ATT_W = 768
ATT_O = 256
GATE_W = 3 * D
IN_W = 6912
IN_TILE = 768
IN_ROT = (IN_W - GATE_W) // IN_TILE
MEM_W = 512
D_FF = 2816
EPS = 1e-6
ROPE_THETA = 500000.0
QB = 128
DILS = (1, 4, 16)
NEG = -1e30
MEM_SCALE = 128 ** -0.5
ATT_SCALE = 0.125

ADAM_LR, ADAM_B1, ADAM_B2, ADAM_EPS, ADAM_WD, ADAM_STEP = 0.001, 0.9, 0.999, 1e-08, 0.01, 10

N_DEV = 8
MESH_AXES = ("x", "y", "c")
LANES = 128
FLAT_ALIGN = 2048

WEIGHTS = ['norm_mix_pre', 'norm_mix_post', 'w_in', 'pool_w', 'pool_scale', 'conv_b_w', 'w_branch_a', 'w_branch_b',
           'w_branch_c', 'w_out', 'norm_mem_pre', 'norm_mem_post', 'norm_memkv', 'w_mq', 'w_mkv', 'w_mo',
           'norm_ffn_pre', 'norm_ffn_post', 'w_up', 'conv_ffn_w', 'w_down']
SHARD_AXIS = {'w_in': 2, 'conv_b_w': 2, 'w_branch_a': 2, 'w_branch_b': 2, 'w_branch_c': 2, 'w_out': 1, 'w_mq': 1,
              'w_mkv': 1, 'w_mo': 2, 'w_up': 2, 'conv_ffn_w': 2, 'w_down': 1}
F32_GATHERED = ('conv_b_w', 'conv_ffn_w')
BIG = [n for n in WEIGHTS if n in SHARD_AXIS and n not in F32_GATHERED]
LATE_BIG = ['w_mq', 'w_mkv', 'w_mo', 'w_up', 'w_down']
MERGE_BIG = ['w_branch_a', 'w_branch_b', 'w_branch_c', 'w_out']


VMEM_LIMIT_MB = 60


def _params(sem):
    return pltpu.CompilerParams(dimension_semantics=sem, vmem_limit_bytes=VMEM_LIMIT_MB << 20)


def _dot(a, b, prec=None):
    return lax.dot_general(a, b, (((1,), (0,)), ((), ())), preferred_element_type=F32, precision=prec)


def _dot_nt(a, b, prec=None):
    return lax.dot_general(a, b, (((1,), (1,)), ((), ())), preferred_element_type=F32, precision=prec)


def _dot_tn(a, b, prec=None):
    return lax.dot_general(a, b, (((0,), (0,)), ((), ())), preferred_element_type=F32, precision=prec)


def _tile(n, cap):
    if n <= cap:
        return n
    best = None
    for t in range(LANES, cap + 1, LANES):
        if n % t == 0:
            best = t
    assert best is not None, (n, cap)
    return best


def _rms(x, g):
    r = lax.rsqrt(jnp.mean(x * x, axis=-1, keepdims=True) + EPS)
    return x * r * g, r


def _rms_bwd(w, y):
    r = lax.rsqrt(jnp.mean(y * y, axis=-1, keepdims=True) + EPS)
    return r * w - y * (r * r * r) * jnp.mean(w * y, axis=-1, keepdims=True), r


def _rows_call(name, body, n_rows, ts, ins, outs, scratch=(), reverse=False, aliases=None):
    nt = n_rows // ts
    assert nt * ts == n_rows

    def tile_of(g):
        return (nt - 1 - g) if reverse else g

    in_specs, args = [], []
    for op in ins:
        if op[0] == "t":
            _, a, cw, cb = op
            in_specs.append(pl.BlockSpec((ts, cw), lambda g, cb=cb: (tile_of(g), cb)))
        elif op[0] == "h":
            _, a, hr, cw, cb = op
            in_specs.append(pl.BlockSpec((hr, cw), lambda g, cb=cb, k=ts // hr: (jnp.maximum(tile_of(g) * k - 1, 0), cb)))
        elif op[0] == "x":
            _, a = op
            in_specs.append(pl.BlockSpec(memory_space=pl.ANY))
        else:
            _, a = op
            in_specs.append(pl.BlockSpec(a.shape, lambda g, n=a.ndim: (0,) * n))
        args.append(a)
    out_specs, out_shape = [], []
    for op in outs:
        if op[0] == "t":
            _, cols, dt = op
            out_specs.append(pl.BlockSpec((ts, cols), lambda g: (tile_of(g), 0)))
            out_shape.append(jax.ShapeDtypeStruct((n_rows, cols), dt))
        elif op[0] == "c":
            _, total, cols, cb, dt = op
            out_specs.append(pl.BlockSpec((ts, cols), lambda g, cb=cb: (tile_of(g), cb)))
            out_shape.append(jax.ShapeDtypeStruct((n_rows, total), dt))
        else:
            _, shp, dt = op
            out_specs.append(pl.BlockSpec(shp, lambda g, n=len(shp): (0,) * n))
            out_shape.append(jax.ShapeDtypeStruct(shp, dt))

    def kern(*refs):
        g = pl.program_id(0)
        body(tile_of(g), g, *refs)

    return pl.pallas_call(kern, grid=(nt,), in_specs=in_specs, out_specs=out_specs, out_shape=out_shape,
                          scratch_shapes=list(scratch), input_output_aliases=aliases or {},
                          compiler_params=_params(("arbitrary",)), name=name)(*args)


def _acc(ref, g, val):
    @pl.when(g == 0)
    def _():
        ref[...] = val

    @pl.when(g != 0)
    def _():
        ref[...] += val


def _norm_mm(name, x, g, w, ts, tn, out_dtype=F32, wt=False, rot=0):
    S, K = x.shape
    N = w.shape[0] if wt else w.shape[1]
    assert wt or not rot

    def body(x_ref, g_ref, w_ref, o_ref, h_ref, hs):
        @pl.when(pl.program_id(1) == 0)
        def _():
            h, _ = _rms(x_ref[...], g_ref[...])
            hs[...] = h.astype(MXU)
            h_ref[...] = h.astype(MXU)

        o_ref[...] = (_dot_nt if wt else _dot)(hs[...], w_ref[...]).astype(out_dtype)

    w_spec = pl.BlockSpec((tn, K), lambda i, j: ((j + rot) % (N // tn), 0)) if wt else pl.BlockSpec((K, tn), lambda i, j: (0, j))
    return pl.pallas_call(
        body, grid=(S // ts, N // tn),
        in_specs=[pl.BlockSpec((ts, K), lambda i, j: (i, 0)), pl.BlockSpec((1, K), lambda i, j: (0, 0)), w_spec],
        out_specs=[pl.BlockSpec((ts, tn), lambda i, j: (i, j)), pl.BlockSpec((ts, K), lambda i, j: (i, 0))],
        out_shape=[jax.ShapeDtypeStruct((S, N), out_dtype), jax.ShapeDtypeStruct((S, K), MXU)],
        scratch_shapes=[pltpu.VMEM((ts, K), MXU)],
        compiler_params=_params(("arbitrary", "arbitrary")), name=name)(x, g, w)


def _mm_nt(name, a, b, ts, tn, out_dtype=F32):
    M, K = a.shape
    N = b.shape[0]

    def body(a_ref, b_ref, o_ref):
        o_ref[...] = _dot_nt(a_ref[...], b_ref[...]).astype(out_dtype)

    return pl.pallas_call(
        body, grid=(M // ts, N // tn),
        in_specs=[pl.BlockSpec((ts, K), lambda i, j: (i, 0)), pl.BlockSpec((tn, K), lambda i, j: (j, 0))],
        out_specs=pl.BlockSpec((ts, tn), lambda i, j: (i, j)), out_shape=jax.ShapeDtypeStruct((M, N), out_dtype),
        compiler_params=_params(("arbitrary", "arbitrary")), name=name)(a, b)


def _mm_tn(name, a, b, cap_k=512, cap_n=1024, out_dtype=MXU, rot=0):
    S, K = a.shape
    N = b.shape[1]
    tk, tn = _tile(K, cap_k), _tile(N, cap_n)

    def body(a_ref, b_ref, o_ref):
        o_ref[...] = _dot_tn(a_ref[...], b_ref[...]).astype(out_dtype)

    return pl.pallas_call(
        body, grid=(K // tk, N // tn),
        in_specs=[pl.BlockSpec((S, tk), lambda i, j: (0, i)), pl.BlockSpec((S, tn), lambda i, j: (0, j))],
        out_specs=pl.BlockSpec((tk, tn), lambda i, j: ((i + rot) % (K // tk), j)), out_shape=jax.ShapeDtypeStruct((K, N), out_dtype),
        compiler_params=_params(("arbitrary", "arbitrary")), name=name)(a, b)


def _pool_cols(shape):
    col = lax.broadcasted_iota(jnp.int32, shape, 1)
    return col < 96, col < 192, col < 288


def _pool_select(s2, s4, s8, s16):
    c1, c2, c3 = _pool_cols(s2.shape)
    return jnp.where(c1, s2, jnp.where(c2, s4, jnp.where(c3, s8, s16)))


def _pool_cnt(t0, ts):
    c1, c2, c3 = _pool_cols((ts, POOLW))
    win = jnp.where(c1, 2, jnp.where(c2, 4, jnp.where(c3, 8, 16)))
    t = t0 + lax.broadcasted_iota(jnp.int32, (ts, POOLW), 0)
    return jnp.minimum(t + 1, win).astype(F32)


def _pooled(a, prev, t0):
    ts = a.shape[0]
    ext = jnp.concatenate([prev, a], axis=0)
    s2 = ext + pltpu.roll(ext, 1, axis=0)
    s4 = s2 + pltpu.roll(s2, 2, axis=0)
    s8 = s4 + pltpu.roll(s4, 4, axis=0)
    s16 = s8 + pltpu.roll(s8, 8, axis=0)
    sums = _pool_select(s2, s4, s8, s16)[16:]
    return sums / _pool_cnt(t0, ts) - a


def _conv3(z, prev8, w):
    ext = jnp.concatenate([prev8, z], axis=0)
    z1 = pltpu.roll(ext, 1, axis=0)[8:]
    z2 = pltpu.roll(ext, 2, axis=0)[8:]
    return w[0:1] * z2 + w[1:2] * z1 + w[2:3] * z, z1, z2


def _conv3_t(dc, next8, w, shifted=False):
    ts = dc.shape[0]
    ext = jnp.concatenate([dc, next8], axis=0)
    n = ts + 8
    u1 = pltpu.roll(ext, n - 1, axis=0)[:ts]
    u2 = pltpu.roll(ext, n - 2, axis=0)[:ts]
    out = w[2:3] * dc + w[1:2] * u1 + w[0:1] * u2
    return (out, u1, u2) if shifted else out


def _poolconv_fwd(u, wblk, pool_scale, conv_b, ts=512):
    S = u.shape[0]

    def body(i, g, a_ref, bx_ref, bb_ref, bc_ref, wblk_ref, ps_ref, cw_ref, a2_ref, yb_ref, ca, cz):
        @pl.when(g == 0)
        def _():
            ca[...] = jnp.zeros_like(ca)
            cz[...] = jnp.zeros_like(cz)

        a = a_ref[...].astype(F32)
        p = _pooled(a, ca[...], i * ts)
        mixed = _dot(p.astype(MXU), wblk_ref[...])
        a2_ref[...] = (mixed * ps_ref[...]).astype(MXU)
        z = bc_ref[...].astype(F32) * bx_ref[...].astype(F32)
        conv, _, _ = _conv3(z, cz[...], cw_ref[...])
        yb_ref[...] = (bb_ref[...].astype(F32) * conv).astype(MXU)
        ca[...] = a[ts - 16:]
        cz[...] = z[ts - 8:]

    ins = [("t", u, POOLW, 8), ("t", u, POOLW, 9), ("t", u, POOLW, 10), ("t", u, POOLW, 11), ("w", wblk), ("w", pool_scale),
           ("w", conv_b)]
    return _rows_call("poolconv_fwd", body, S, ts, ins, [("t", POOLW, MXU), ("t", POOLW, MXU)],
                      scratch=[pltpu.VMEM((16, POOLW), F32), pltpu.VMEM((8, POOLW), F32)])


def _poolconv_bwd(u, d_a2, d_yb, du, wblk, pool_scale, conv_b, ts=512):
    S = u.shape[0]

    def body(i, g, a_ref, bx_ref, bb_ref, bc_ref, ap_ref, bxp_ref, bcp_ref, da2_ref, dyb_ref, wblk_ref, ps_ref, cw_ref, _,
             o_ref, dps_ref, dwb_ref, dcw_ref, ce, cdz):
        @pl.when(g == 0)
        def _():
            ce[...] = jnp.zeros_like(ce)
            cdz[...] = jnp.zeros_like(cdz)

        first = (i > 0).astype(F32)
        a = a_ref[...].astype(F32)
        p = _pooled(a, ap_ref[...].astype(F32) * first, i * ts)
        pb = p.astype(MXU)
        mixed = _dot(pb, wblk_ref[...])
        da2 = da2_ref[...]
        dmixed = (da2 * ps_ref[...]).astype(MXU)
        dp = _dot_nt(dmixed, wblk_ref[...])
        _acc(dps_ref, g, jnp.sum(da2 * mixed, axis=0, keepdims=True))
        _acc(dwb_ref, g, _dot_tn(pb, dmixed))
        e = dp / _pool_cnt(i * ts, ts)
        ext = jnp.concatenate([e, ce[...]], axis=0)
        n = ts + 16
        f2 = ext + pltpu.roll(ext, n - 1, axis=0)
        f4 = f2 + pltpu.roll(f2, n - 2, axis=0)
        f8 = f4 + pltpu.roll(f4, n - 4, axis=0)
        f16 = f8 + pltpu.roll(f8, n - 8, axis=0)
        o_ref[:, 0:POOLW] = (_pool_select(f2, f4, f8, f16)[:ts] - dp).astype(o_ref.dtype)
        ce[...] = e[:16]

        bx, bb, bc = bx_ref[...].astype(F32), bb_ref[...].astype(F32), bc_ref[...].astype(F32)
        z = bc * bx
        w = cw_ref[...]
        conv, z1, z2 = _conv3(z, (bxp_ref[...].astype(F32) * bcp_ref[...].astype(F32))[8:16] * first, w)
        dyb = dyb_ref[...]
        dconv = dyb * bb
        dz = _conv3_t(dconv, cdz[...], w)
        o_ref[:, POOLW:2 * POOLW] = (dz * bc).astype(o_ref.dtype)
        o_ref[:, 2 * POOLW:3 * POOLW] = (dyb * conv).astype(o_ref.dtype)
        o_ref[:, 3 * POOLW:4 * POOLW] = (dz * bx).astype(o_ref.dtype)
        dw = jnp.concatenate([jnp.sum(dconv * z2, axis=0, keepdims=True), jnp.sum(dconv * z1, axis=0, keepdims=True),
                              jnp.sum(dconv * z, axis=0, keepdims=True)], axis=0)
        _acc(dcw_ref, g, dw)
        cdz[...] = dconv[:8]

    ins = [("t", u, POOLW, 8), ("t", u, POOLW, 9), ("t", u, POOLW, 10), ("t", u, POOLW, 11),
           ("h", u, 16, POOLW, 8), ("h", u, 16, POOLW, 9), ("h", u, 16, POOLW, 11),
           ("t", d_a2, POOLW, 0), ("t", d_yb, POOLW, 0), ("w", wblk), ("w", pool_scale), ("w", conv_b), ("x", du)]
    outs = [("c", IN_W, 4 * POOLW, GATE_W // (4 * POOLW), MXU), ("a", (1, POOLW), F32), ("a", (POOLW, POOLW), F32), ("a", (3, POOLW), F32)]
    return _rows_call("poolconv_bwd", body, S, ts, ins, outs, aliases={len(ins) - 1: 0},
                      scratch=[pltpu.VMEM((16, POOLW), F32), pltpu.VMEM((8, POOLW), F32)], reverse=True)


def _rope_tables(positions):
    inv = ROPE_THETA ** (-jnp.arange(0, 16, 2, dtype=F32) / 16)
    lane_inv = jnp.tile(jnp.concatenate([inv, inv, jnp.zeros(48, F32)]), 2)
    lane_sign = jnp.tile(jnp.concatenate([-jnp.ones(8, F32), jnp.ones(8, F32), jnp.zeros(48, F32)]), 2)
    ang = positions.astype(F32)[:, None] * lane_inv
    return jnp.where(lane_sign == 0, 1.0, jnp.cos(ang)), jnp.sin(ang) * lane_sign


def _partner(x):
    lane = lax.broadcasted_iota(jnp.int32, x.shape, 1) % 64
    return jnp.where(lane < 8, pltpu.roll(x, LANES - 8, axis=1), jnp.where(lane < 16, pltpu.roll(x, 8, axis=1), 0.0))


def _rope(x, c, s):
    return x * c + _partner(x) * s


def _rope_t(x, c, s):
    return x * c + _partner(x * s)


def _rows_of(r, n, d):
    return pl.ds(r, n, stride=d) if d > 1 else pl.ds(0, n)


def _head_masks(shape):
    lane = lax.broadcasted_iota(jnp.int32, shape, 1) // 64
    return [lane == h for h in range(4)]


def _only(mask, x):
    return jnp.where(mask, x, jnp.zeros_like(x))


def _rope_perm(u, ctab, stab, ts=512):
    S = u.shape[0]
    nch = ATT_W // LANES

    def body(*refs):
        chunks, (c_ref, s_ref), outs, scr = refs[:3 * nch], refs[3 * nch:3 * nch + 2], refs[3 * nch + 2:-1], refs[-1]
        for k in range(3 * nch):
            scr[k] = chunks[k][...].astype(F32)
        for g, d in enumerate(DILS):
            n = ts // d
            for r in range(d):
                rows = _rows_of(r, n, d)
                c, s = c_ref[rows, :], s_ref[rows, :]
                for which in range(3):
                    parts = [scr.at[which * nch + j][rows, :] for j in (2 * g, 2 * g + 1)]
                    if which < 2:
                        parts = [_rope(x, c, s) for x in parts]
                    outs[which * 3 + g][r] = jnp.concatenate(parts, axis=1).astype(MXU)

    base = (IN_W - 3 * ATT_W) // LANES
    in_specs = [pl.BlockSpec((ts, LANES), lambda i, cb=base + k: (i, cb)) for k in range(3 * nch)]
    in_specs += [pl.BlockSpec((ts, LANES), lambda i: (i, 0))] * 2
    out_specs = [pl.BlockSpec((d, ts // d, ATT_O), lambda i: (0, i, 0)) for _ in range(3) for d in DILS]
    out_shape = [jax.ShapeDtypeStruct((d, S // d, ATT_O), MXU) for _ in range(3) for d in DILS]
    res = pl.pallas_call(body, grid=(S // ts,), in_specs=in_specs, out_specs=out_specs, out_shape=out_shape,
                         scratch_shapes=[pltpu.VMEM((3 * nch, ts, LANES), F32)],
                         compiler_params=_params(("arbitrary",)), name="rope_perm")(*([u] * (3 * nch)), ctab, stab)
    return [[res[which * 3 + g].reshape(S, ATT_O) for g in range(3)] for which in range(3)]


def _rope_unperm_bwd(dqkv, du, ctab, stab, ts=512):
    S = dqkv[0][0].shape[0]
    nch = ATT_W // LANES

    def body(*refs):
        ins, (c_ref, s_ref, _, o_ref, scr) = refs[:9], refs[9:]
        for g, d in enumerate(DILS):
            n = ts // d
            for r in range(d):
                rows = _rows_of(r, n, d)
                c, s = c_ref[rows, :], s_ref[rows, :]
                for which in range(3):
                    v = ins[which * 3 + g][r]
                    for half in range(2):
                        x = v[:, half * LANES:(half + 1) * LANES]
                        scr.at[which * nch + 2 * g + half][rows, :] = _rope_t(x, c, s) if which < 2 else x
        for j in range(3 * nch):
            o_ref[:, j * LANES:(j + 1) * LANES] = scr[j].astype(o_ref.dtype)

    in_specs = [pl.BlockSpec((d, ts // d, ATT_O), lambda i: (0, i, 0)) for _ in range(3) for d in DILS]
    in_specs += [pl.BlockSpec((ts, LANES), lambda i: (i, 0))] * 2 + [pl.BlockSpec(memory_space=pl.ANY)]
    args = [dqkv[which][g].reshape(d, S // d, ATT_O) for which in range(3) for g, d in enumerate(DILS)]
    last = (IN_W - 3 * ATT_W) // (3 * ATT_W)
    return pl.pallas_call(body, grid=(S // ts,), in_specs=in_specs, out_specs=pl.BlockSpec((ts, 3 * ATT_W), lambda i: (i, last)),
                          out_shape=jax.ShapeDtypeStruct((S, IN_W), MXU), scratch_shapes=[pltpu.VMEM((3 * nch, ts, LANES), F32)],
                          input_output_aliases={len(in_specs) - 1: 0},
                          compiler_params=_params(("arbitrary",)), name="rope_unperm_bwd")(*args, ctab, stab, du)


def _band_mask_keys(has_prev):
    r = lax.broadcasted_iota(jnp.int32, (QB, 2 * QB), 0)
    c = lax.broadcasted_iota(jnp.int32, (QB, 2 * QB), 1)
    return ((c < QB) & (c >= r) & has_prev) | ((c >= QB) & (c - QB <= r))


def _band_mask_queries(has_next):
    r = lax.broadcasted_iota(jnp.int32, (2 * QB, QB), 0)
    c = lax.broadcasted_iota(jnp.int32, (2 * QB, QB), 1)
    return ((r < QB) & (c <= r)) | ((r >= QB) & (c >= r - QB) & has_next)


ASUB = 4
_BIG = pl.BlockSpec((ASUB * QB, ATT_O), lambda b: (b, 0))
_PREV = pl.BlockSpec((QB, ATT_O), lambda b: (jnp.maximum(b * ASUB - 1, 0), 0))


def _sub(ref, j):
    return ref[j * QB:(j + 1) * QB]


def _attn_fwd(g, q, k, v):
    S = q.shape[0]
    nb = S // QB
    nblk = nb // DILS[g]

    def body(q_ref, kc_ref, kp_ref, vc_ref, vp_ref, o_ref, m_ref, l_ref):
        hm_kv, hm_o = _head_masks((2 * QB, ATT_O)), _head_masks((QB, ATT_O))
        for j in range(ASUB):
            ok = _band_mask_keys(((pl.program_id(0) * ASUB + j) & (nblk - 1)) > 0)
            k2 = jnp.concatenate([kp_ref[...] if j == 0 else _sub(kc_ref, j - 1), _sub(kc_ref, j)], axis=0)
            v2 = jnp.concatenate([vp_ref[...] if j == 0 else _sub(vc_ref, j - 1), _sub(vc_ref, j)], axis=0)
            qv = _sub(q_ref, j)
            o_acc = jnp.zeros((QB, ATT_O), F32)
            m_acc = jnp.zeros((QB, ATT_O), F32)
            l_acc = jnp.zeros((QB, ATT_O), F32)
            for h in range(4):
                s = jnp.where(ok, _dot_nt(qv, _only(hm_kv[h], k2)) * ATT_SCALE, NEG)
                m = jnp.max(s, axis=1, keepdims=True)
                p = jnp.exp(s - m)
                o_acc = o_acc + _dot(p.astype(MXU), _only(hm_kv[h], v2))
                m_acc = jnp.where(hm_o[h], m, m_acc)
                l_acc = jnp.where(hm_o[h], jnp.sum(p, axis=1, keepdims=True), l_acc)
            o_ref[j * QB:(j + 1) * QB] = o_acc
            m_ref[j * QB:(j + 1) * QB] = m_acc
            l_ref[j * QB:(j + 1) * QB] = l_acc

    shp = jax.ShapeDtypeStruct((S, ATT_O), F32)
    return pl.pallas_call(body, grid=(nb // ASUB,), in_specs=[_BIG, _BIG, _PREV, _BIG, _PREV],
                          out_specs=[_BIG] * 3, out_shape=[shp, shp, shp], compiler_params=_params(("arbitrary",)),
                          name=f"attn_fwd_{g}")(q, k, k, v, v)


def _natural(ref, d, scr, ts):
    if d == 1:
        return ref[0]
    n = ts // d
    for r in range(d):
        v = ref[r]
        scr.at[0][pl.ds(r, n, stride=d), :] = v[:, 0:LANES]
        scr.at[1][pl.ds(r, n, stride=d), :] = v[:, LANES:2 * LANES]
    return jnp.concatenate([scr[0], scr[1]], axis=1)


def _attn_combine(oml, ts=512):
    S = oml[0][0].shape[0]

    def body(*refs):
        ins, (att_ref, out_ref, lse_ref, scr) = refs[:9], refs[9:]
        o, m, l = [[_natural(ins[3 * g + k], d, scr, ts) for g, d in enumerate(DILS)] for k in range(3)]
        mx = jnp.maximum(jnp.maximum(m[0], m[1]), m[2])
        w = [jnp.exp(m[g] - mx) for g in range(3)]
        den = w[0] * l[0] + w[1] * l[1] + w[2] * l[2]
        out = (w[0] * o[0] + w[1] * o[1] + w[2] * o[2]) / den
        out_ref[...] = out
        att_ref[...] = out.astype(MXU)
        lse_ref[...] = mx + jnp.log(den)

    in_specs = [pl.BlockSpec((d, ts // d, ATT_O), lambda i: (0, i, 0)) for d in DILS for _ in range(3)]
    args = [a.reshape(d, S // d, ATT_O) for d, grp in zip(DILS, oml) for a in grp]
    blk = pl.BlockSpec((ts, ATT_O), lambda i: (i, 0))
    return pl.pallas_call(body, grid=(S // ts,), in_specs=in_specs, out_specs=[blk, blk, blk],
                          out_shape=[jax.ShapeDtypeStruct((S, ATT_O), MXU), jax.ShapeDtypeStruct((S, ATT_O), F32),
                                     jax.ShapeDtypeStruct((S, ATT_O), F32)],
                          scratch_shapes=[pltpu.VMEM((2, ts, LANES), F32)], compiler_params=_params(("arbitrary",)),
                          name="attn_combine")(*args)


def _attn_bwd_prep(datt, o, lse, ts=512):
    S = datt.shape[0]

    def body(da0, da1, o_ref, l0, l1, *rest):
        outs, dl = rest[:9], rest[9]
        prod = jnp.concatenate([da0[...], da1[...]], axis=1) * o_ref[...]
        delta = jnp.zeros((ts, ATT_O), F32)
        for hm in _head_masks((ts, ATT_O)):
            delta = jnp.where(hm, jnp.sum(_only(hm, prod), axis=1, keepdims=True), delta)
        dl[0] = delta[:, 0:LANES]
        dl[1] = delta[:, LANES:2 * LANES]
        for g, d in enumerate(DILS):
            n = ts // d
            for r in range(d):
                rows = _rows_of(r, n, d)
                outs[g][r] = jnp.concatenate([da0[rows, :], da1[rows, :]], axis=1).astype(MXU)
                outs[3 + g][r] = jnp.concatenate([dl.at[0][rows, :], dl.at[1][rows, :]], axis=1)
                outs[6 + g][r] = jnp.concatenate([l0[rows, :], l1[rows, :]], axis=1)

    half = lambda j: pl.BlockSpec((ts, LANES), lambda i: (i, j))
    out_specs = [pl.BlockSpec((d, ts // d, ATT_O), lambda i: (0, i, 0)) for _ in range(3) for d in DILS]
    out_shape = [jax.ShapeDtypeStruct((d, S // d, ATT_O), dt) for dt in (MXU, F32, F32) for d in DILS]
    res = pl.pallas_call(body, grid=(S // ts,), in_specs=[half(0), half(1), pl.BlockSpec((ts, ATT_O), lambda i: (i, 0)), half(0), half(1)],
                         out_specs=out_specs, out_shape=out_shape, scratch_shapes=[pltpu.VMEM((2, ts, LANES), F32)],
                         compiler_params=_params(("arbitrary",)), name="attn_bwd_prep")(datt, datt, o, lse, lse)
    return [[res[k * 3 + g].reshape(S, ATT_O) for g in range(3)] for k in range(3)]


def _head_col(x, h):
    return x[:, h * 64:h * 64 + 1]


def _attn_bwd(g, q, k, v, do, delta, lse):
    S = q.shape[0]
    nb = S // QB
    nblk = nb // DILS[g]

    def body(k_ref, v_ref, qc_ref, qn_ref, doc_ref, don_ref, dlc_ref, dln_ref, lc_ref, ln_ref, dq_ref, dk_ref, dv_ref, dq_scr):
        hms, hmk = _head_masks((2 * QB, ATT_O)), _head_masks((QB, ATT_O))
        first = pl.program_id(0) == 0

        @pl.when(first)
        def _():
            dq_scr[0:QB] = jnp.zeros((QB, ATT_O), F32)

        @pl.when(jnp.logical_not(first))
        def _():
            dq_scr[0:QB] = dq_scr[ASUB * QB:(ASUB + 1) * QB]

        dq_scr[QB:(ASUB + 1) * QB] = jnp.zeros((ASUB * QB, ATT_O), F32)

        def both(cur_ref, nxt_ref, j):
            return jnp.concatenate([_sub(cur_ref, j), nxt_ref[...] if j == ASUB - 1 else _sub(cur_ref, j + 1)], axis=0)

        for j in range(ASUB):
            ok = _band_mask_queries(((pl.program_id(0) * ASUB + j + 1) & (nblk - 1)) > 0)
            q2, do2, dl2, lse2 = both(qc_ref, qn_ref, j), both(doc_ref, don_ref, j), both(dlc_ref, dln_ref, j), both(lc_ref, ln_ref, j)
            kv, vv = _sub(k_ref, j), _sub(v_ref, j)
            dk = jnp.zeros((QB, ATT_O), F32)
            dv = jnp.zeros((QB, ATT_O), F32)
            dq2 = jnp.zeros((2 * QB, ATT_O), F32)
            for h, hm in enumerate(hms):
                qh, doh = _only(hm, q2), _only(hm, do2)
                p = jnp.where(ok, jnp.exp(_dot_nt(qh, kv) * ATT_SCALE - _head_col(lse2, h)), 0.0)
                ds = (p * (_dot_nt(doh, vv) - _head_col(dl2, h))).astype(MXU)
                dv = dv + _dot_tn(p.astype(MXU), doh)
                dk = dk + _dot_tn(ds, qh)
                dq2 = dq2 + _dot(ds, _only(hmk[h], kv))
            dk_ref[j * QB:(j + 1) * QB] = dk * ATT_SCALE
            dv_ref[j * QB:(j + 1) * QB] = dv
            dq_scr[j * QB:(j + 2) * QB] += dq2
        dq_ref[...] = dq_scr[0:ASUB * QB] * ATT_SCALE

    nxt = pl.BlockSpec((QB, ATT_O), lambda b: (jnp.minimum((b + 1) * ASUB, nb - 1), 0))
    shp = jax.ShapeDtypeStruct((S, ATT_O), F32)
    return pl.pallas_call(body, grid=(nb // ASUB,), in_specs=[_BIG, _BIG, _BIG, nxt, _BIG, nxt, _BIG, nxt, _BIG, nxt], out_specs=[_BIG] * 3,
                          out_shape=[shp, shp, shp], scratch_shapes=[pltpu.VMEM(((ASUB + 1) * QB, ATT_O), F32)],
                          compiler_params=_params(("arbitrary",)), name=f"attn_bwd_{g}")(k, v, q, q, do, do, delta, delta, lse, lse)


def _merge_fwd(x0, u, a2, yb, att, wa, wb, wc, w_out, g_post, ts=256):
    S = x0.shape[0]

    def body(i, g, x_ref, gate_ref, a2_ref, yb_ref, att_ref, wa_ref, wb_ref, wc_ref, wo_ref, gp_ref, mg_ref, y_ref, xo_ref):
        gate = lambda n: jax.nn.sigmoid(gate_ref[:, n * D:(n + 1) * D].astype(F32))
        merged = gate(0) * _dot_nt(a2_ref[...], wa_ref[...])
        merged = merged + gate(1) * _dot_nt(yb_ref[...], wb_ref[...])
        merged = merged + gate(2) * _dot_nt(att_ref[...], wc_ref[...])
        mb = merged.astype(MXU)
        mg_ref[...] = mb
        y = _dot(mb, wo_ref[...])
        y_ref[...] = y
        xo_ref[...] = x_ref[...] + _rms(y, gp_ref[...])[0]

    ins = [("t", x0, D, 0), ("t", u, GATE_W, 0), ("t", a2, POOLW, 0), ("t", yb, POOLW, 0), ("t", att, ATT_O, 0),
           ("w", wa), ("w", wb), ("w", wc), ("w", w_out), ("w", g_post)]
    return _rows_call("merge_fwd", body, S, ts, ins, [("t", D, MXU), ("t", D, F32), ("t", D, F32)])


def _merge_bwd(dx, y1, u, a2, yb, att, merged, wa, wb, wc, w_out, g_post, ts=256):
    S = dx.shape[0]
    last = S // ts - 1

    def body(i, g, dx_ref, y_ref, gate_ref, a2_ref, yb_ref, att_ref, mg_ref, wa_ref, wb_ref, wc_ref, wo_ref, gp_ref,
             dgate_ref, da2_ref, dyb_ref, datt_ref, dgp_ref, dwo_ref, dwa_ref, dwb_ref, dwc_ref, acc_o, acc_a, acc_b, acc_c):
        @pl.when(g == 0)
        def _():
            for acc in (acc_o, acc_a, acc_b, acc_c):
                acc[...] = jnp.zeros_like(acc)

        dxv, y = dx_ref[...], y_ref[...]
        dy, r = _rms_bwd(dxv * gp_ref[...], y)
        _acc(dgp_ref, g, jnp.sum(dxv * (y * r), axis=0, keepdims=True))
        dyb16 = dy.astype(MXU)
        acc_o[...] += _dot_tn(mg_ref[...], dyb16)
        dm = _dot_nt(dyb16, wo_ref[...])
        for n, (src, w_ref, din_ref, acc) in enumerate(((a2_ref, wa_ref, da2_ref, acc_a), (yb_ref, wb_ref, dyb_ref, acc_b),
                                                       (att_ref, wc_ref, datt_ref, acc_c))):
            gt = jax.nn.sigmoid(gate_ref[:, n * D:(n + 1) * D].astype(F32))
            br = _dot_nt(src[...], w_ref[...])
            dgate_ref[:, n * D:(n + 1) * D] = (dm * br * gt * (1.0 - gt)).astype(dgate_ref.dtype)
            dbr = (dm * gt).astype(MXU)
            acc[...] += _dot_tn(dbr, src[...])
            din_ref[...] = _dot(dbr, w_ref[...])

        @pl.when(g == last)
        def _():
            for out, acc in ((dwo_ref, acc_o), (dwa_ref, acc_a), (dwb_ref, acc_b), (dwc_ref, acc_c)):
                out[...] = acc[...].astype(MXU)

    ins = [("t", dx, D, 0), ("t", y1, D, 0), ("t", u, GATE_W, 0), ("t", a2, POOLW, 0), ("t", yb, POOLW, 0), ("t", att, ATT_O, 0),
           ("t", merged, D, 0), ("w", wa), ("w", wb), ("w", wc), ("w", w_out), ("w", g_post)]
    wshapes = [(D, D), (D, POOLW), (D, POOLW), (D, ATT_O)]
    outs = [("c", IN_W, GATE_W, 0, MXU), ("t", POOLW, F32), ("t", POOLW, F32), ("t", ATT_O, F32), ("a", (1, D), F32)]
    outs += [("a", s, MXU) for s in wshapes]
    return _rows_call("merge_bwd", body, S, ts, ins, outs, scratch=[pltpu.VMEM(s, F32) for s in wshapes])


def _prenorm_bwd(name, dx_res, du, wt, x, g_pre, ts=256, lead=0):
    S = x.shape[0]
    N = du.shape[1]

    def body(i, g, dx_ref, du_ref, wt_ref, x_ref, g_ref, o_ref, dg_ref):
        if lead:
            dhv = _dot(du_ref[:, 0:lead], wt_ref[N - lead:N, :]) + _dot(du_ref[:, lead:N], wt_ref[0:N - lead, :])
        else:
            dhv = _dot(du_ref[...], wt_ref[...])
        xv = x_ref[...]
        dxn, r = _rms_bwd(dhv * g_ref[...], xv)
        o_ref[...] = dx_ref[...] + dxn
        _acc(dg_ref, g, jnp.sum(dhv * (xv * r), axis=0, keepdims=True))

    ins = [("t", dx_res, D, 0), ("t", du, N, 0), ("w", wt), ("t", x, D, 0), ("w", g_pre)]
    return _rows_call(name, body, S, ts, ins, [("t", D, F32), ("a", (1, D), F32)])


def _mem_heads(qm, kv_ref):
    out = []
    for h in range(4):
        q = qm[:, h * 128:(h + 1) * 128].astype(MXU)
        k = kv_ref[:, h * 128:(h + 1) * 128]
        v = kv_ref[:, MEM_W + h * 128:MEM_W + (h + 1) * 128]
        sc = _dot_nt(q, k) * MEM_SCALE
        e = jnp.exp(sc - jnp.max(sc, axis=1, keepdims=True))
        out.append((e / jnp.sum(e, axis=1, keepdims=True), q, k, v))
    return out


def _mem_fwd(x1, kv, g_pre, w_mq, w_mo, g_post, ts=256):
    S = x1.shape[0]

    def body(i, g, x_ref, kv_ref, gq_ref, wq_ref, wo_ref, gp_ref, om_ref, y_ref, xo_ref):
        x = x_ref[...]
        hb = _rms(x, gq_ref[...])[0].astype(MXU)
        qm = _dot(hb, wq_ref[...])
        om = jnp.concatenate([_dot(p.astype(MXU), v) for p, _, _, v in _mem_heads(qm, kv_ref)], axis=1).astype(MXU)
        om_ref[...] = om
        y = _dot_nt(om, wo_ref[...])
        y_ref[...] = y
        xo_ref[...] = x + _rms(y, gp_ref[...])[0]

    ins = [("t", x1, D, 0), ("w", kv), ("w", g_pre), ("w", w_mq), ("w", w_mo), ("w", g_post)]
    return _rows_call("mem_fwd", body, S, ts, ins, [("t", MEM_W, MXU), ("t", D, F32), ("t", D, F32)])


def _mem_bwd(dx2, ym, x1, om, kv, g_pre, w_mq, w_mo, g_post, ts=256):
    S = x1.shape[0]
    last = S // ts - 1

    def body(i, g, dx_ref, y_ref, x_ref, om_ref, kv_ref, gq_ref, wq_ref, wo_ref, gp_ref, dxo_ref, dgp_ref, dgq_ref, dkv_ref,
             dwo_ref, dwq_ref, acc_o, acc_q):
        dxv, y, x = dx_ref[...], y_ref[...], x_ref[...]
        dy, r = _rms_bwd(dxv * gp_ref[...], y)
        _acc(dgp_ref, g, jnp.sum(dxv * (y * r), axis=0, keepdims=True))
        dyb = dy.astype(MXU)
        dom = _dot(dyb, wo_ref[...])
        h, r1 = _rms(x, gq_ref[...])
        hb = h.astype(MXU)
        qm = _dot(hb, wq_ref[...])
        dqs = []

        @pl.when(g == 0)
        def _():
            dkv_ref[...] = jnp.zeros_like(dkv_ref)
            acc_o[...] = jnp.zeros_like(acc_o)
            acc_q[...] = jnp.zeros_like(acc_q)

        acc_o[...] += _dot_tn(dyb, om_ref[...])

        for hh, (p, q, k, v) in enumerate(_mem_heads(qm, kv_ref)):
            doh = dom[:, hh * 128:(hh + 1) * 128].astype(MXU)
            dp = _dot_nt(doh, v)
            dsc = (p * (dp - jnp.sum(dp * p, axis=1, keepdims=True)) * MEM_SCALE).astype(MXU)
            dqs.append(_dot(dsc, k))
            dkv_ref[:, hh * 128:(hh + 1) * 128] += _dot_tn(dsc, q)
            dkv_ref[:, MEM_W + hh * 128:MEM_W + (hh + 1) * 128] += _dot_tn(p.astype(MXU), doh)
        dq = jnp.concatenate(dqs, axis=1).astype(MXU)
        acc_q[...] += _dot_tn(hb, dq)
        dh = _dot_nt(dq, wq_ref[...])
        _acc(dgq_ref, g, jnp.sum(dh * (x * r1), axis=0, keepdims=True))
        dxo_ref[...] = dxv + _rms_bwd(dh * gq_ref[...], x)[0]

        @pl.when(g == last)
        def _():
            dwo_ref[...] = acc_o[...].astype(MXU)
            dwq_ref[...] = acc_q[...].astype(MXU)

    ins = [("t", dx2, D, 0), ("t", ym, D, 0), ("t", x1, D, 0), ("t", om, MEM_W, 0), ("w", kv), ("w", g_pre), ("w", w_mq), ("w", w_mo),
           ("w", g_post)]
    outs = [("t", D, F32), ("a", (1, D), F32), ("a", (1, D), F32), ("a", (256, D), F32), ("a", (D, MEM_W), MXU), ("a", (D, MEM_W), MXU)]
    return _rows_call("mem_bwd", body, S, ts, ins, outs, scratch=[pltpu.VMEM((D, MEM_W), F32), pltpu.VMEM((D, MEM_W), F32)])


def _gain_grad(name, dn, x):
    n = x.shape[0]

    def body(i, g, dn_ref, x_ref, o_ref):
        xv = x_ref[...]
        r = lax.rsqrt(jnp.mean(xv * xv, axis=-1, keepdims=True) + EPS)
        o_ref[...] = jnp.sum(dn_ref[...] * (xv * r), axis=0, keepdims=True)

    return _rows_call(name, body, n, n, [("t", dn, D, 0), ("t", x, D, 0)], [("a", (1, D), F32)])[0]


def _ffn_fwd(x2, u3, conv_f, w_down, g_post, ts=256):
    S = x2.shape[0]

    def body(i, g, x_ref, ua_ref, ub_ref, cw_ref, wd_ref, gp_ref, act_ref, y_ref, xo_ref, c_ref, cu):
        @pl.when(g == 0)
        def _():
            cu[...] = jnp.zeros_like(cu)

        ua = ua_ref[...].astype(F32)
        c, _, _ = _conv3(ua, cu[...], cw_ref[...])
        c_ref[...] = c.astype(MXU)
        act = (c * jax.nn.sigmoid(c) * ub_ref[...].astype(F32)).astype(MXU)
        act_ref[...] = act
        y = _dot(act, wd_ref[...])
        y_ref[...] = y
        xo_ref[...] = x_ref[...] + _rms(y, gp_ref[...])[0]
        cu[...] = ua[ts - 8:]

    ins = [("t", x2, D, 0), ("t", u3, D_FF, 0), ("t", u3, D_FF, 1), ("w", conv_f), ("w", w_down), ("w", g_post)]
    return _rows_call("ffn_fwd", body, S, ts, ins, [("t", D_FF, MXU), ("t", D, F32), ("t", D, F32), ("t", D_FF, MXU)],
                      scratch=[pltpu.VMEM((8, D_FF), F32)])


def _ffn_bwd(dx3, y3, u3, c, conv_f, w_down, g_post, ts=128):
    S = dx3.shape[0]

    def body(i, g, dx_ref, y_ref, ua_ref, ub_ref, c_ref, cw_ref, wd_ref, gp_ref, dy_ref, du_ref, dgp_ref, dcw_ref, cdc):
        @pl.when(g == 0)
        def _():
            cdc[...] = jnp.zeros_like(cdc)

        dxv, y = dx_ref[...], y_ref[...]
        dy, r = _rms_bwd(dxv * gp_ref[...], y)
        _acc(dgp_ref, g, jnp.sum(dxv * (y * r), axis=0, keepdims=True))
        dyb = dy.astype(MXU)
        dy_ref[...] = dyb
        dact = _dot_nt(dyb, wd_ref[...])
        ua, c, w = ua_ref[...].astype(F32), c_ref[...].astype(F32), cw_ref[...]
        sg = jax.nn.sigmoid(c)
        du_ref[:, D_FF:2 * D_FF] = (dact * (c * sg)).astype(du_ref.dtype)
        dc = dact * ub_ref[...].astype(F32) * (sg * (1.0 + c * (1.0 - sg)))
        dua, dc1, dc2 = _conv3_t(dc, cdc[...], w, shifted=True)
        du_ref[:, 0:D_FF] = dua.astype(du_ref.dtype)
        dw = jnp.concatenate([jnp.sum(ua * dc2, axis=0, keepdims=True), jnp.sum(ua * dc1, axis=0, keepdims=True),
                              jnp.sum(ua * dc, axis=0, keepdims=True)], axis=0)
        _acc(dcw_ref, g, dw)
        cdc[...] = dc[:8]

    ins = [("t", dx3, D, 0), ("t", y3, D, 0), ("t", u3, D_FF, 0), ("t", u3, D_FF, 1), ("t", c, D_FF, 0), ("w", conv_f),
           ("w", w_down), ("w", g_post)]
    outs = [("t", D, MXU), ("t", 2 * D_FF, MXU), ("a", (1, D), F32), ("a", (3, D_FF), F32)]
    return _rows_call("ffn_bwd", body, S, ts, ins, outs, scratch=[pltpu.VMEM((8, D_FF), F32)], reverse=True)


def _loss_head(x, target, ts=512):
    S = x.shape[0]

    def body(i, g, x_ref, t_ref, dx_ref, acc_ref):
        diff = x_ref[...] - t_ref[...]
        dx_ref[...] = diff * (1.0 / D)
        col = jnp.sum(diff * diff, axis=0, keepdims=True)
        part = col[:, 0:LANES]
        for j in range(1, D // LANES):
            part = part + col[:, j * LANES:(j + 1) * LANES]
        row = lax.broadcasted_iota(jnp.int32, (8, LANES), 0)
        _acc(acc_ref, g, jnp.where(row == 0, jnp.broadcast_to(part, (8, LANES)), 0.0))

    return _rows_call("loss_head", body, S, ts, [("t", x, D, 0), ("t", target, D, 0)], [("t", D, F32), ("a", (8, LANES), F32)])


_OPERAND_NAME = dict(w_in='w_in', w_branch_a='wa', w_branch_b='wb', w_branch_c='wc', w_out='w_out', w_mq='w_mq', w_mkv='w_mkv',
                     w_mo='w_mo', w_up='w_up', w_down='w_down')


def _big_operands(big):
    return {_OPERAND_NAME[n]: a for n, a in big.items()}


def _layer_weights(big, small, l):
    pool_w = small['pool_w'][l].astype(MXU)
    wblk = jnp.zeros((POOLW, POOLW), MXU)
    for g in range(4):
        wblk = lax.dynamic_update_slice(wblk, pool_w[g], (g * 96, g * 96))
    vec = lambda n: small[n][l].reshape(1, -1)
    return dict(
        _big_operands(big),
        wblk=wblk, pool_scale=vec('pool_scale'), conv_b=small['conv_b_w'][l], conv_f=small['conv_ffn_w'][l],
        g_mix_pre=vec('norm_mix_pre'), g_mix_post=vec('norm_mix_post'), g_mem_pre=vec('norm_mem_pre'),
        g_mem_post=vec('norm_mem_post'), g_memkv=vec('norm_memkv'), g_ffn_pre=vec('norm_ffn_pre'), g_ffn_post=vec('norm_ffn_post'))


def _layer_fwd(x0, mem, W, ctab, stab):
    sv = _layer_fwd_mix(x0, W, ctab, stab)
    return _layer_fwd_late(mem, W, sv), sv


def _layer_fwd_mix(x0, W, ctab, stab):
    return _layer_fwd_merge(W, _layer_fwd_branches(x0, W, ctab, stab))


def _layer_fwd_branches(x0, W, ctab, stab):
    sv = dict(x0=x0)
    sv['u'], sv['h1'] = _norm_mm("in_proj", x0, W['g_mix_pre'], W['w_in'], ts=2048, tn=IN_TILE, wt=True, rot=IN_ROT, out_dtype=MXU)
    sv['a2'], sv['yb'] = _poolconv_fwd(sv['u'], W['wblk'], W['pool_scale'], W['conv_b'])
    sv['qkv'] = q3, k3, v3 = _rope_perm(sv['u'], ctab, stab)
    sv['att'], sv['o'], sv['lse'] = _attn_combine([_attn_fwd(g, q3[g], k3[g], v3[g]) for g in range(3)])
    return sv


def _layer_fwd_merge(W, sv):
    sv['merged'], sv['y1'], sv['x1'] = _merge_fwd(sv['x0'], sv['u'], sv['a2'], sv['yb'], sv['att'], W['wa'], W['wb'], W['wc'],
                                                  W['w_out'], W['g_mix_post'])
    return sv


def _layer_fwd_late(mem, W, sv):
    sv['kv'], sv['memn'] = _norm_mm("mem_kv", mem, W['g_memkv'], W['w_mkv'], ts=256, tn=D, out_dtype=MXU)
    sv['om'], sv['ym'], sv['x2'] = _mem_fwd(sv['x1'], sv['kv'], W['g_mem_pre'], W['w_mq'], W['w_mo'], W['g_mem_post'])
    sv['u3'], sv['h3'] = _norm_mm("up_proj", sv['x2'], W['g_ffn_pre'], W['w_up'], ts=2048, tn=1408, wt=True, out_dtype=MXU)
    sv['act'], sv['y3'], x3, sv['c3'] = _ffn_fwd(sv['x2'], sv['u3'], W['conv_f'], W['w_down'], W['g_ffn_post'])
    return x3


def _layer_bwd(dx3, mem, W, sv, ctab, stab):
    dx1, g = _layer_bwd_late(dx3, mem, W, sv)
    dx0, g_mix = _layer_bwd_mix(dx1, W, sv, ctab, stab)
    return dx0, {**g, **g_mix}


def _layer_bwd_late(dx3, mem, W, sv):
    g = {}
    dy3, du3, g['norm_ffn_post'], g['conv_ffn_w'] = _ffn_bwd(dx3, sv['y3'], sv['u3'], sv['c3'], W['conv_f'], W['w_down'], W['g_ffn_post'])
    g['w_down'] = _mm_tn("dw_down", sv['act'], dy3, cap_k=256)
    g['w_up'] = _mm_tn("dw_up", du3, sv['h3'])
    dx2, g['norm_ffn_pre'] = _prenorm_bwd("ffn_pre_bwd", dx3, du3, W['w_up'], sv['x2'], W['g_ffn_pre'], ts=512)
    dx1, g['norm_mem_post'], g['norm_mem_pre'], dkv, g['w_mo'], g['w_mq'] = _mem_bwd(
        dx2, sv['ym'], sv['x1'], sv['om'], sv['kv'], W['g_mem_pre'], W['w_mq'], W['w_mo'], W['g_mem_post'])
    dkvb = dkv.astype(MXU)
    g['w_mkv'] = _mm_tn("dw_mkv", sv['memn'], dkvb)
    g['norm_memkv'] = _gain_grad("memkv_gain", _mm_nt("d_memn", dkvb, W['w_mkv'], ts=256, tn=512), mem)
    return dx1, g


def _layer_bwd_mix(dx1, W, sv, ctab, stab):
    du, g = _layer_bwd_mixers(dx1, W, sv, ctab, stab)
    g['w_in'] = _dw_in(du, sv)
    dx0, g['norm_mix_pre'] = _mix_pre_bwd(dx1, du, W, sv)
    return dx0, g


def _dw_in(du, sv):
    return _mm_tn("dw_in", du, sv['h1'], cap_k=IN_TILE, rot=IN_ROT)


def _mix_pre_bwd(dx1, du, W, sv):
    return _prenorm_bwd("mix_pre_bwd", dx1, du, W['w_in'], sv['x0'], W['g_mix_pre'], lead=GATE_W)


def _layer_bwd_mixers(dx1, W, sv, ctab, stab):
    parts, g = _layer_bwd_merge(dx1, W, sv)
    du, g_br = _layer_bwd_branches(parts, W, sv, ctab, stab)
    return du, {**g, **g_br}


def _layer_bwd_merge(dx1, W, sv):
    g = {}
    du, da2, dyb, datt, g['norm_mix_post'], g['w_out'], g['w_branch_a'], g['w_branch_b'], g['w_branch_c'] = _merge_bwd(
        dx1, sv['y1'], sv['u'], sv['a2'], sv['yb'], sv['att'], sv['merged'], W['wa'], W['wb'], W['wc'], W['w_out'], W['g_mix_post'])
    return (du, da2, dyb, datt), g


def _layer_bwd_branches(parts, W, sv, ctab, stab):
    du, da2, dyb, datt = parts
    g = {}
    du, g['pool_scale'], dwblk, g['conv_b_w'] = _poolconv_bwd(sv['u'], da2, dyb, du, W['wblk'], W['pool_scale'], W['conv_b'])
    g['pool_w'] = jnp.stack([dwblk[k * 96:(k + 1) * 96, k * 96:(k + 1) * 96] for k in range(4)])
    q3, k3, v3 = sv['qkv']
    do3, dl3, lse3 = _attn_bwd_prep(datt, sv['o'], sv['lse'])
    dqkv3 = [_attn_bwd(i, q3[i], k3[i], v3[i], do3[i], dl3[i], lse3[i]) for i in range(3)]
    du = _rope_unperm_bwd([[t[which] for t in dqkv3] for which in range(3)], du, ctab, stab)
    return du, g


def _local_step(x, mem, positions, target, big, small):
    ctab, stab = _rope_tables(positions)
    Ws = [_layer_weights(big[l], small, l) for l in range(DEPTH)]
    saved = []
    for l in range(DEPTH):
        x, sv = _layer_fwd(x, mem, Ws[l], ctab, stab)
        saved.append(sv)
    dx, acc = _loss_head(x, target)
    loss = jnp.sum(acc) * (0.5 / D)
    grads = [None] * DEPTH
    for l in reversed(range(DEPTH)):
        dx, grads[l] = _layer_bwd(dx, mem, Ws[l], saved[l], ctab, stab)
    return loss, dx, grads


_HBM = pl.BlockSpec(memory_space=pl.ANY)
MESH_ID = pl.DeviceIdType.MESH


def _all_gather(name, xs):
    n = len(xs)

    def body(*refs):
        x_refs, out_refs = refs[:n], refs[n:2 * n]
        send_sems, recv_sems, local_sems = refs[2 * n:]
        x, y, c = lax.axis_index("x"), lax.axis_index("y"), lax.axis_index("c")
        me, sibling = (x, y, c), (x, y, 1 - c)
        chips = [(1 - x, y), (x, 1 - y), (1 - x, 1 - y)]

        def slot(a, p):
            return out_refs[a].at[4 * p[0] + 2 * p[1] + p[2]]

        def copy(a, k, block, to, src=None):
            return pltpu.make_async_remote_copy(src_ref=slot(a, block) if src is None else src, dst_ref=slot(a, block),
                                                send_sem=send_sems.at[a, k], recv_sem=recv_sems.at[a, k], device_id=to,
                                                device_id_type=MESH_ID)

        started = []
        for a in range(n):
            mine = pltpu.make_async_copy(x_refs[a], slot(a, me), local_sems.at[a])
            mine.start()
            started.append(mine)
        first = []
        for a in range(n):
            first.append(copy(a, 0, me, sibling, src=x_refs[a]))
            first += [copy(a, 1 + j, me, (*chip, c), src=x_refs[a]) for j, chip in enumerate(chips)]
        for cp in first:
            cp.start()
        passed = []
        for j, chip in enumerate(chips):
            for a in range(n):
                copy(a, 1 + j, (*chip, c), me).wait_recv()
                fw = copy(a, 4 + j, (*chip, c), sibling)
                fw.start()
                passed.append(fw)
        for a in range(n):
            copy(a, 0, sibling, me).wait_recv()
            for j, chip in enumerate(chips):
                copy(a, 4 + j, (*chip, 1 - c), me).wait_recv()
        for cp in first + passed:
            cp.wait_send()
        for mine in started:
            mine.wait()

    return pl.pallas_call(
        body, out_shape=[jax.ShapeDtypeStruct((N_DEV,) + x.shape, x.dtype) for x in xs], in_specs=[_HBM] * n, out_specs=[_HBM] * n,
        scratch_shapes=[pltpu.SemaphoreType.DMA((n, 7)), pltpu.SemaphoreType.DMA((n, 7)), pltpu.SemaphoreType.DMA((n,))],
        name=name)(*xs)


_SEM =pl.BlockSpec(memory_space=pltpu.SEMAPHORE)
_IN_HBM = pl.BlockSpec(memory_space=pltpu.HBM)
_SIDE_EFFECT = pltpu.SideEffectType.DATAFLOW_SIDE_EFFECTING


def _push_copies(src_refs, land_refs, send_sems, recv_sems, per_peer):
    x, y, c = lax.axis_index("x"), lax.axis_index("y"), lax.axis_index("c")
    me = 4 * x + 2 * y + c
    copies = []
    for r in range(1, N_DEV):
        px, py, pc = x ^ ((r >> 2) & 1), y ^ ((r >> 1) & 1), c ^ (r & 1)
        for a, (s, d) in enumerate(zip(src_refs, land_refs)):
            k = a * (N_DEV - 1) + r - 1
            copies.append(pltpu.make_async_remote_copy(src_ref=s.at[4 * px + 2 * py + pc] if per_peer else s, dst_ref=d.at[me],
                                                       send_sem=send_sems.at[k], recv_sem=recv_sems.at[k],
                                                       device_id=(px, py, pc), device_id_type=MESH_ID))
    return copies


def _push_start(name, srcs, per_peer, after):
    n = len(srcs)
    lands = [lax.empty((N_DEV,) + (s.shape[1:] if per_peer else s.shape), s.dtype) for s in srcs]

    def body(*refs):
        for cp in _push_copies(refs[:n], refs[n:2 * n], refs[2 * n + 1], refs[2 * n + 2], per_peer):
            cp.start()
        refs[-1][...] = jnp.zeros_like(refs[-1])

    hbm = [pltpu.HBM(a.shape, a.dtype) for a in (*srcs, *lands)]
    sems = pltpu.SemaphoreType.DMA((n * (N_DEV - 1),))
    out = pl.pallas_call(
        body, name=name, out_shape=(sems, sems, *hbm, jax.ShapeDtypeStruct((8, LANES), F32)),
        in_specs=[_IN_HBM] * (2 * n) + [pl.BlockSpec(memory_space=pl.ANY)],
        out_specs=(_SEM, _SEM, *[_IN_HBM] * (2 * n), pl.BlockSpec(memory_space=pltpu.VMEM)),
        input_output_aliases={a: 2 + a for a in range(2 * n)},
        compiler_params=pltpu.CompilerParams(has_side_effects=_SIDE_EFFECT),
    )(*[pltpu.with_memory_space_constraint(a, pltpu.HBM) for a in (*srcs, *lands)], after)
    return out[0], out[1], out[2:2 + n], out[2 + n:2 + 2 * n], out[-1]


def _push_wait(name, started, per_peer, after):
    send_sems, recv_sems, srcs, lands, _ = started
    n = len(srcs)

    def body(*refs):
        for cp in _push_copies(refs[:n], refs[n:2 * n], refs[2 * n], refs[2 * n + 1], per_peer):
            cp.wait_send()
            cp.wait_recv()

    out = pl.pallas_call(
        body, name=name, out_shape=[pltpu.HBM(a.shape, a.dtype) for a in (*srcs, *lands)],
        in_specs=[_IN_HBM] * (2 * n) + [_SEM, _SEM, pl.BlockSpec(memory_space=pl.ANY)], out_specs=[_IN_HBM] * (2 * n),
        input_output_aliases={a: a for a in range(2 * n)},
        compiler_params=pltpu.CompilerParams(has_side_effects=_SIDE_EFFECT),
    )(*srcs, *lands, send_sems, recv_sems, after)
    if per_peer:
        return list(zip(out[:n], out[n:]))
    return _with_own(out[n:], out[:n], _my_slot())


def _my_slot():
    return 4 * lax.axis_index("x") + 2 * lax.axis_index("y") + lax.axis_index("c")


def _row_tile(rows, cols, budget):
    if rows * cols * 4 <= budget or rows % 16:
        return rows
    best = 16
    for t in range(16, rows + 1, 16):
        if rows % t == 0 and t * cols * 4 <= budget:
            best = t
    return best


def _slot_total(r_ref, own_ref):
    me = _my_slot()
    g = jnp.where(me == 0, own_ref[...], r_ref[0]).astype(F32)
    for k in range(1, N_DEV):
        g = g + jnp.where(me == k, own_ref[...], r_ref[k]).astype(F32)
    return g


def _sum_slots(name, pushed):
    src, recv = pushed
    _, R, C = recv.shape
    tr = _row_tile(R, C, 1 << 20)

    def body(r_ref, own_ref, o_ref):
        o_ref[...] = _slot_total(r_ref, own_ref)

    return pl.pallas_call(body, grid=(R // tr,),
                          in_specs=[pl.BlockSpec((N_DEV, tr, C), lambda i: (0, i, 0)), pl.BlockSpec((None, tr, C), lambda i: (_my_slot(), i, 0))],
                          out_specs=pl.BlockSpec((tr, C), lambda i: (i, 0)), out_shape=jax.ShapeDtypeStruct((R, C), F32),
                          compiler_params=_params(("arbitrary",)), name=name)(recv, src)


def _adamw_step(gv, w_ref, m_ref, v_ref, d_ref, mo_ref, vo_ref):
    mn = ADAM_B1 * m_ref[...] + (1.0 - ADAM_B1) * gv
    vn = ADAM_B2 * v_ref[...] + (1.0 - ADAM_B2) * (gv * gv)
    mo_ref[...] = mn
    vo_ref[...] = vn
    c1 = 1.0 - ADAM_B1 ** ADAM_STEP
    c2 = 1.0 - ADAM_B2 ** ADAM_STEP
    d_ref[...] = -ADAM_LR * ((mn / c1) / (jnp.sqrt(vn / c2) + ADAM_EPS) + ADAM_WD * w_ref[...])


def _adamw(name, g, w, m, v):
    shape = w.shape
    R, C = shape[-2], shape[-1]
    view = (-1, R, C)
    L = w.size // (R * C)
    tr = _row_tile(R, C, 1 << 20)

    def body(g_ref, w_ref, m_ref, v_ref, d_ref, mo_ref, vo_ref):
        _adamw_step(g_ref[...], w_ref, m_ref, v_ref, d_ref, mo_ref, vo_ref)

    blk = pl.BlockSpec((None, tr, C), lambda l, i: (l, i, 0))
    shp = jax.ShapeDtypeStruct((L, R, C), F32)
    outs = pl.pallas_call(body, grid=(L, R // tr), in_specs=[blk, blk, blk, blk], out_specs=[blk, blk, blk], out_shape=[shp, shp, shp],
                          compiler_params=_params(("arbitrary", "arbitrary")), name=name)(*[a.reshape(view) for a in (g, w, m, v)])
    return [o.reshape(shape) for o in outs]


def _sum_adamw(name, pushed, w, m, v):
    L, R, C = w.shape
    tr = _row_tile(R, C, 1 << 20)
    n_i = R // tr

    def body(*refs):
        shares, (w_ref, m_ref, v_ref, g_ref, d_ref, mo_ref, vo_ref) = refs[:2 * L], refs[2 * L:]
        for k in range(L):
            @pl.when(pl.program_id(0) == k)
            def _(k=k):
                g_ref[...] = _slot_total(shares[2 * k], shares[2 * k + 1])
        _adamw_step(g_ref[...], w_ref, m_ref, v_ref, d_ref, mo_ref, vo_ref)

    def during(k):
        return lambda l, i: jnp.where(l == k, i, jnp.where(l < k, 0, n_i - 1))

    in_specs, operands = [], []
    for k, (src, recv) in enumerate(pushed):
        in_specs += [pl.BlockSpec((N_DEV, tr, C), lambda l, i, at=during(k): (0, at(l, i), 0)),
                     pl.BlockSpec((None, tr, C), lambda l, i, at=during(k): (_my_slot(), at(l, i), 0))]
        operands += [recv, src]
    blk = pl.BlockSpec((None, tr, C), lambda l, i: (l, i, 0))
    shp = jax.ShapeDtypeStruct((L, R, C), F32)
    return pl.pallas_call(body, grid=(L, n_i), in_specs=in_specs + [blk, blk, blk], out_specs=[blk] * 4, out_shape=[shp] * 4,
                          compiler_params=_params(("arbitrary", "arbitrary")), name=name)(*operands, w, m, v)


def _pad_flat(a, n):
    a = a.reshape(-1)
    return jnp.pad(a, (0, n - a.shape[0]))


def _seg(n):
    return -(-n // FLAT_ALIGN) * FLAT_ALIGN


def _to_blocks(full, axis):
    shp = full.shape
    return jnp.moveaxis(full.reshape(shp[:axis] + (N_DEV, shp[axis] // N_DEV) + shp[axis + 1:]), axis, 0)


def _from_blocks(blocks, axis):
    b = jnp.moveaxis(blocks, 0, axis)
    shp = b.shape
    return b.reshape(shp[:axis] + (shp[axis] * shp[axis + 1],) + shp[axis + 2:])


def _as_rows(shard, n):
    return shard.T if SHARD_AXIS[n] == 2 else shard


def _with_own(lands, own, me):
    return [lax.dynamic_update_slice(land, o[None], (me, 0, 0)) for land, o in zip(lands, own)]


def kernel(x, mem, positions, norm_mix_pre, norm_mix_post, w_in, pool_w, pool_scale, conv_b_w, w_branch_a, w_branch_b, w_branch_c, w_out, norm_mem_pre, norm_mem_post, norm_memkv, w_mq, w_mkv, w_mo, norm_ffn_pre, norm_ffn_post, w_up, conv_ffn_w, w_down, loss_target, m_norm_mix_pre, m_norm_mix_post, m_w_in, m_pool_w, m_pool_scale, m_conv_b_w, m_w_branch_a, m_w_branch_b, m_w_branch_c, m_w_out, m_norm_mem_pre, m_norm_mem_post, m_norm_memkv, m_w_mq, m_w_mkv, m_w_mo, m_norm_ffn_pre, m_norm_ffn_post, m_w_up, m_conv_ffn_w, m_w_down, v_norm_mix_pre, v_norm_mix_post, v_w_in, v_pool_w, v_pool_scale, v_conv_b_w, v_w_branch_a, v_w_branch_b, v_w_branch_c, v_w_out, v_norm_mem_pre, v_norm_mem_post, v_norm_memkv, v_w_mq, v_w_mkv, v_w_mo, v_norm_ffn_pre, v_norm_ffn_post, v_w_up, v_conv_ffn_w, v_w_down):
    w = dict(norm_mix_pre=norm_mix_pre, norm_mix_post=norm_mix_post, w_in=w_in, pool_w=pool_w, pool_scale=pool_scale, conv_b_w=conv_b_w, w_branch_a=w_branch_a, w_branch_b=w_branch_b, w_branch_c=w_branch_c, w_out=w_out, norm_mem_pre=norm_mem_pre, norm_mem_post=norm_mem_post, norm_memkv=norm_memkv, w_mq=w_mq, w_mkv=w_mkv, w_mo=w_mo, norm_ffn_pre=norm_ffn_pre, norm_ffn_post=norm_ffn_post, w_up=w_up, conv_ffn_w=conv_ffn_w, w_down=w_down)
    m = dict(norm_mix_pre=m_norm_mix_pre, norm_mix_post=m_norm_mix_post, w_in=m_w_in, pool_w=m_pool_w, pool_scale=m_pool_scale, conv_b_w=m_conv_b_w, w_branch_a=m_w_branch_a, w_branch_b=m_w_branch_b, w_branch_c=m_w_branch_c, w_out=m_w_out, norm_mem_pre=m_norm_mem_pre, norm_mem_post=m_norm_mem_post, norm_memkv=m_norm_memkv, w_mq=m_w_mq, w_mkv=m_w_mkv, w_mo=m_w_mo, norm_ffn_pre=m_norm_ffn_pre, norm_ffn_post=m_norm_ffn_post, w_up=m_w_up, conv_ffn_w=m_conv_ffn_w, w_down=m_w_down)
    v = dict(norm_mix_pre=v_norm_mix_pre, norm_mix_post=v_norm_mix_post, w_in=v_w_in, pool_w=v_pool_w, pool_scale=v_pool_scale, conv_b_w=v_conv_b_w, w_branch_a=v_w_branch_a, w_branch_b=v_w_branch_b, w_branch_c=v_w_branch_c, w_out=v_w_out, norm_mem_pre=v_norm_mem_pre, norm_mem_post=v_norm_mem_post, norm_memkv=v_norm_memkv, w_mq=v_w_mq, w_mkv=v_w_mkv, w_mo=v_w_mo, norm_ffn_pre=v_norm_ffn_pre, norm_ffn_post=v_norm_ffn_post, w_up=v_w_up, conv_ffn_w=v_conv_ffn_w, w_down=v_w_down)

    mix_big = [n for n in BIG if n not in LATE_BIG]
    block = lambda names, l: [_as_rows(w[n][l], n).astype(MXU) for n in names]
    conv = jnp.concatenate([_pad_flat(w[n], _seg(w[n].size)) for n in F32_GATHERED]).reshape(-1, LANES)
    groups = dict(m=MERGE_BIG, b=LATE_BIG, a=mix_big)
    got0 = _all_gather("weights_all_gather_0", block(['w_in'], 0) + [conv])
    conv_all = got0[-1].reshape(N_DEV, -1)
    small, off = {n: w[n] for n in WEIGHTS if n not in SHARD_AXIS}, 0
    for n in F32_GATHERED:
        small[n] = _from_blocks(conv_all[:, off:off + w[n].size].reshape((N_DEV,) + w[n].shape), 2)
        off += _seg(w[n].size)
    whole = lambda names, got: {n: o.reshape(-1, o.shape[-1]) for n, o in zip(names, got)}
    pushes, after = {}, got0[0]
    for tag, l in (('m', 0), ('b', 0), ('a', 1)):
        pushes[tag, l] = _push_start(f"weights_push_start_{l}{tag}", block(groups[tag], l), False, after)
        after = pushes[tag, l][4]

    def arrived(tag, l, done):
        return _big_operands(whole(groups[tag], _push_wait(f"weights_push_wait_{l}{tag}", pushes[tag, l], False, done)))

    ctab, stab = _rope_tables(positions[0])
    W0 = _layer_weights(whole(['w_in'], got0), small, 0)
    sv0 = _layer_fwd_branches(x[0], dict(W0, g_mix_pre=W0['g_mix_pre'] + after[0, 0]), ctab, stab)
    W0.update(arrived('m', 0, sv0['att']))
    sv0 = _layer_fwd_merge(W0, sv0)
    W0.update(arrived('b', 0, sv0['x1']))
    x1 = _layer_fwd_late(mem[0], W0, sv0)
    pushes['b', 1] = _push_start("weights_push_start_1b", block(groups['b'], 1), False, x1)
    W1 = _layer_weights({}, small, 1)
    W1.update(arrived('a', 1, pushes['b', 1][4]))
    sv1 = _layer_fwd_mix(x1, W1, ctab, stab)
    W1.update(arrived('b', 1, sv1['x1']))
    x2 = _layer_fwd_late(mem[0], W1, sv1)
    dx, acc = _loss_head(x2, loss_target[0])
    loss = lax.psum(jnp.sum(acc) * (0.5 / D), MESH_AXES)
    grads = [None] * DEPTH
    dx, grads[1] = _layer_bwd(dx, mem[0], W1, sv1, ctab, stab)
    sent = [None, [grads[1][n].reshape(N_DEV, -1, grads[1][n].shape[-1]) for n in BIG]]
    push_g = _push_start("grads_push_start_1", sent[1], True, dx)
    dx, g_late = _layer_bwd_late(dx, mem[0], dict(W0, g_ffn_post=W0['g_ffn_post'] + push_g[4][0, 0]), sv0)
    sent_late = [g_late[n].reshape(N_DEV, -1, g_late[n].shape[-1]) for n in LATE_BIG]
    push_l = _push_start("grads_push_start_0", sent_late, True, dx)
    parts, g_mix = _layer_bwd_merge(dx, dict(W0, g_mix_post=W0['g_mix_post'] + push_l[4][0, 0]), sv0)
    sent_merge = [g_mix[n].reshape(N_DEV, -1, g_mix[n].shape[-1]) for n in MERGE_BIG]
    push_m = _push_start("grads_push_start_0m", sent_merge, True, parts[0])
    du, g_br = _layer_bwd_branches(parts, dict(W0, pool_scale=W0['pool_scale'] + push_m[4][0, 0]), sv0, ctab, stab)
    g_mix.update(g_br)
    g_mix['w_in'] = _dw_in(du, sv0)
    sent_in = [g_mix['w_in'].reshape(N_DEV, -1, D)]
    push_i = _push_start("grads_push_start_in", sent_in, True, du)
    dx, g_mix['norm_mix_pre'] = _mix_pre_bwd(dx, du, dict(W0, g_mix_pre=W0['g_mix_pre'] + push_i[4][0, 0]), sv0)
    grads[0] = {**g_late, **g_mix}
    recv1 = _push_wait("grads_push_wait_1", push_g, True, dx)
    recv_late = _push_wait("grads_push_wait_0", push_l, True, dx)
    recv_merge = _push_wait("grads_push_wait_0m", push_m, True, dx)

    misc_names = [n for n in WEIGHTS if n not in BIG]
    stacked = {n: jnp.stack([grads[l][n].reshape(small[n].shape[1:]) for l in range(DEPTH)]) for n in misc_names}
    rows = [(_to_blocks(stacked[n], 2) if n in SHARD_AXIS else jnp.broadcast_to(stacked[n][None], (N_DEV,) + stacked[n].shape))
            for n in misc_names]
    segs = [_seg(w[n].size) for n in misc_names]
    misc = jnp.concatenate([jnp.pad(r.reshape(N_DEV, -1), ((0, 0), (0, s - r[0].size))) for r, s in zip(rows, segs)],
                           axis=1).reshape(N_DEV, -1, LANES)
    push_x = _push_start("grads_push_start_small", [misc], True, dx)
    g_out, shares = {}, {}
    for l, names, recv in ((1, BIG, recv1), (0, LATE_BIG, recv_late), (0, MERGE_BIG, recv_merge)):
        for n, r in zip(names, recv):
            shares[n, l] = r

    swap = lambda a: jnp.swapaxes(a, 1, 2)

    def update(n):
        if n not in BIG:
            return [g_out[n], *_adamw(f"adamw_{n}", g_out[n], w[n], m[n], v[n])]
        of_layers = [shares[n, l] for l in range(DEPTH)]
        if SHARD_AXIS[n] == 1:
            return _sum_adamw(f"adamw_{n}", of_layers, w[n], m[n], v[n])
        if w[n].shape[2] % LANES:
            return [swap(a) for a in _sum_adamw(f"adamw_{n}", of_layers, swap(w[n]), swap(m[n]), swap(v[n]))]
        g = swap(jnp.stack([_sum_slots(f"sum_{n}_{l}", s) for l, s in enumerate(of_layers)]))
        return [g, *_adamw(f"adamw_{n}", g, w[n], m[n], v[n])]

    done = {n: update(n) for n in BIG if n != 'w_in'}
    shares['w_in', 0] = _push_wait("grads_push_wait_in", push_i, True, done[BIG[-1]][1])[0]
    done['w_in'] = update('w_in')
    after_w_in = swap(done['w_in'][1])
    misc_sum = _sum_slots("sum_misc", _push_wait("grads_push_wait_small", push_x, True, after_w_in)[0]).reshape(-1)
    off = 0
    for n, s in zip(misc_names, segs):
        g_out[n] = misc_sum[off:off + w[n].size].reshape(w[n].shape)
        done[n] = update(n)
        off += s
    return (loss, dx[None], *[done[n][k] for k in range(4) for n in WEIGHTS])
```

```python
import jax
import jax.numpy as jnp
from jax import lax
from jax.experimental import pallas as pl
from jax.experimental.pallas import tpu as pltpu

F32 = jnp.float32
MXU = jnp.bfloat16

D = 1024
DEPTH = 2
POOLW = 384
ATT_W = 768
ATT_O = 256
GATE_W = 3 * D
IN_W = 6912
IN_TILE = 768
IN_ROT = (IN_W - GATE_W) // IN_TILE
MEM_W = 512
D_FF = 2816
EPS = 1e-6
ROPE_THETA = 500000.0
QB = 128
DILS = (1, 4, 16)
NEG = -1e30
MEM_SCALE = 128 ** -0.5
ATT_SCALE = 0.125

ADAM_LR, ADAM_B1, ADAM_B2, ADAM_EPS, ADAM_WD, ADAM_STEP = 0.001, 0.9, 0.999, 1e-08, 0.01, 10

N_DEV = 8
MESH_AXES = ("x", "y", "c")
LANES = 128
FLAT_ALIGN = 2048

WEIGHTS = ['norm_mix_pre', 'norm_mix_post', 'w_in', 'pool_w', 'pool_scale', 'conv_b_w', 'w_branch_a', 'w_branch_b',
           'w_branch_c', 'w_out', 'norm_mem_pre', 'norm_mem_post', 'norm_memkv', 'w_mq', 'w_mkv', 'w_mo',
           'norm_ffn_pre', 'norm_ffn_post', 'w_up', 'conv_ffn_w', 'w_down']
SHARD_AXIS = {'w_in': 2, 'conv_b_w': 2, 'w_branch_a': 2, 'w_branch_b': 2, 'w_branch_c': 2, 'w_out': 1, 'w_mq': 1,
              'w_mkv': 1, 'w_mo': 2, 'w_up': 2, 'conv_ffn_w': 2, 'w_down': 1}
F32_GATHERED = ('conv_b_w', 'conv_ffn_w')
BIG = [n for n in WEIGHTS if n in SHARD_AXIS and n not in F32_GATHERED]
LATE_BIG = ['w_mq', 'w_mkv', 'w_mo', 'w_up', 'w_down']
MERGE_BIG = ['w_branch_a', 'w_branch_b', 'w_branch_c', 'w_out']


VMEM_LIMIT_MB = 60


def _params(sem):
    return pltpu.CompilerParams(dimension_semantics=sem, vmem_limit_bytes=VMEM_LIMIT_MB << 20)


def _dot(a, b, prec=None):
    return lax.dot_general(a, b, (((1,), (0,)), ((), ())), preferred_element_type=F32, precision=prec)


def _dot_nt(a, b, prec=None):
    return lax.dot_general(a, b, (((1,), (1,)), ((), ())), preferred_element_type=F32, precision=prec)


def _dot_tn(a, b, prec=None):
    return lax.dot_general(a, b, (((0,), (0,)), ((), ())), preferred_element_type=F32, precision=prec)


def _tile(n, cap):
    if n <= cap:
        return n
    best = None
    for t in range(LANES, cap + 1, LANES):
        if n % t == 0:
            best = t
    assert best is not None, (n, cap)
    return best


def _rms(x, g):
    r = lax.rsqrt(jnp.mean(x * x, axis=-1, keepdims=True) + EPS)
    return x * r * g, r


def _rms_bwd(w, y):
    r = lax.rsqrt(jnp.mean(y * y, axis=-1, keepdims=True) + EPS)
    return r * w - y * (r * r * r) * jnp.mean(w * y, axis=-1, keepdims=True), r


def _rows_call(name, body, n_rows, ts, ins, outs, scratch=(), reverse=False, aliases=None):
    nt = n_rows // ts
    assert nt * ts == n_rows

    def tile_of(g):
        return (nt - 1 - g) if reverse else g

    in_specs, args = [], []
    for op in ins:
        if op[0] == "t":
            _, a, cw, cb = op
            in_specs.append(pl.BlockSpec((ts, cw), lambda g, cb=cb: (tile_of(g), cb)))
        elif op[0] == "h":
            _, a, hr, cw, cb = op
            in_specs.append(pl.BlockSpec((hr, cw), lambda g, cb=cb, k=ts // hr: (jnp.maximum(tile_of(g) * k - 1, 0), cb)))
        elif op[0] == "x":
            _, a = op
            in_specs.append(pl.BlockSpec(memory_space=pl.ANY))
        else:
            _, a = op
            in_specs.append(pl.BlockSpec(a.shape, lambda g, n=a.ndim: (0,) * n))
        args.append(a)
    out_specs, out_shape = [], []
    for op in outs:
        if op[0] == "t":
            _, cols, dt = op
            out_specs.append(pl.BlockSpec((ts, cols), lambda g: (tile_of(g), 0)))
            out_shape.append(jax.ShapeDtypeStruct((n_rows, cols), dt))
        elif op[0] == "c":
            _, total, cols, cb, dt = op
            out_specs.append(pl.BlockSpec((ts, cols), lambda g, cb=cb: (tile_of(g), cb)))
            out_shape.append(jax.ShapeDtypeStruct((n_rows, total), dt))
        else:
            _, shp, dt = op
            out_specs.append(pl.BlockSpec(shp, lambda g, n=len(shp): (0,) * n))
            out_shape.append(jax.ShapeDtypeStruct(shp, dt))

    def kern(*refs):
        g = pl.program_id(0)
        body(tile_of(g), g, *refs)

    return pl.pallas_call(kern, grid=(nt,), in_specs=in_specs, out_specs=out_specs, out_shape=out_shape,
                          scratch_shapes=list(scratch), input_output_aliases=aliases or {},
                          compiler_params=_params(("arbitrary",)), name=name)(*args)


def _acc(ref, g, val):
    @pl.when(g == 0)
    def _():
        ref[...] = val

    @pl.when(g != 0)
    def _():
        ref[...] += val


def _norm_mm(name, x, g, w, ts, tn, out_dtype=F32, wt=False, rot=0):
    S, K = x.shape
    N = w.shape[0] if wt else w.shape[1]
    assert wt or not rot

    def body(x_ref, g_ref, w_ref, o_ref, h_ref, hs):
        @pl.when(pl.program_id(1) == 0)
        def _():
            h, _ = _rms(x_ref[...], g_ref[...])
            hs[...] = h.astype(MXU)
            h_ref[...] = h.astype(MXU)

        o_ref[...] = (_dot_nt if wt else _dot)(hs[...], w_ref[...]).astype(out_dtype)

    w_spec = pl.BlockSpec((tn, K), lambda i, j: ((j + rot) % (N // tn), 0)) if wt else pl.BlockSpec((K, tn), lambda i, j: (0, j))
    return pl.pallas_call(
        body, grid=(S // ts, N // tn),
        in_specs=[pl.BlockSpec((ts, K), lambda i, j: (i, 0)), pl.BlockSpec((1, K), lambda i, j: (0, 0)), w_spec],
        out_specs=[pl.BlockSpec((ts, tn), lambda i, j: (i, j)), pl.BlockSpec((ts, K), lambda i, j: (i, 0))],
        out_shape=[jax.ShapeDtypeStruct((S, N), out_dtype), jax.ShapeDtypeStruct((S, K), MXU)],
        scratch_shapes=[pltpu.VMEM((ts, K), MXU)],
        compiler_params=_params(("arbitrary", "arbitrary")), name=name)(x, g, w)


def _mm_nt(name, a, b, ts, tn, out_dtype=F32):
    M, K = a.shape
    N = b.shape[0]

    def body(a_ref, b_ref, o_ref):
        o_ref[...] = _dot_nt(a_ref[...], b_ref[...]).astype(out_dtype)

    return pl.pallas_call(
        body, grid=(M // ts, N // tn),
        in_specs=[pl.BlockSpec((ts, K), lambda i, j: (i, 0)), pl.BlockSpec((tn, K), lambda i, j: (j, 0))],
        out_specs=pl.BlockSpec((ts, tn), lambda i, j: (i, j)), out_shape=jax.ShapeDtypeStruct((M, N), out_dtype),
        compiler_params=_params(("arbitrary", "arbitrary")), name=name)(a, b)


def _mm_tn(name, a, b, cap_k=512, cap_n=1024, out_dtype=MXU, rot=0):
    S, K = a.shape
    N = b.shape[1]
    tk, tn = _tile(K, cap_k), _tile(N, cap_n)

    def body(a_ref, b_ref, o_ref):
        o_ref[...] = _dot_tn(a_ref[...], b_ref[...]).astype(out_dtype)

    return pl.pallas_call(
        body, grid=(K // tk, N // tn),
        in_specs=[pl.BlockSpec((S, tk), lambda i, j: (0, i)), pl.BlockSpec((S, tn), lambda i, j: (0, j))],
        out_specs=pl.BlockSpec((tk, tn), lambda i, j: ((i + rot) % (K // tk), j)), out_shape=jax.ShapeDtypeStruct((K, N), out_dtype),
        compiler_params=_params(("arbitrary", "arbitrary")), name=name)(a, b)


def _pool_cols(shape):
    col = lax.broadcasted_iota(jnp.int32, shape, 1)
    return col < 96, col < 192, col < 288


def _pool_select(s2, s4, s8, s16):
    c1, c2, c3 = _pool_cols(s2.shape)
    return jnp.where(c1, s2, jnp.where(c2, s4, jnp.where(c3, s8, s16)))


def _pool_cnt(t0, ts):
    c1, c2, c3 = _pool_cols((ts, POOLW))
    win = jnp.where(c1, 2, jnp.where(c2, 4, jnp.where(c3, 8, 16)))
    t = t0 + lax.broadcasted_iota(jnp.int32, (ts, POOLW), 0)
    return jnp.minimum(t + 1, win).astype(F32)


def _pooled(a, prev, t0):
    ts = a.shape[0]
    ext = jnp.concatenate([prev, a], axis=0)
    s2 = ext + pltpu.roll(ext, 1, axis=0)
    s4 = s2 + pltpu.roll(s2, 2, axis=0)
    s8 = s4 + pltpu.roll(s4, 4, axis=0)
    s16 = s8 + pltpu.roll(s8, 8, axis=0)
    sums = _pool_select(s2, s4, s8, s16)[16:]
    return sums / _pool_cnt(t0, ts) - a


def _conv3(z, prev8, w):
    ext = jnp.concatenate([prev8, z], axis=0)
    z1 = pltpu.roll(ext, 1, axis=0)[8:]
    z2 = pltpu.roll(ext, 2, axis=0)[8:]
    return w[0:1] * z2 + w[1:2] * z1 + w[2:3] * z, z1, z2


def _conv3_t(dc, next8, w, shifted=False):
    ts = dc.shape[0]
    ext = jnp.concatenate([dc, next8], axis=0)
    n = ts + 8
    u1 = pltpu.roll(ext, n - 1, axis=0)[:ts]
    u2 = pltpu.roll(ext, n - 2, axis=0)[:ts]
    out = w[2:3] * dc + w[1:2] * u1 + w[0:1] * u2
    return (out, u1, u2) if shifted else out


def _poolconv_fwd(u, wblk, pool_scale, conv_b, ts=512):
    S = u.shape[0]

    def body(i, g, a_ref, bx_ref, bb_ref, bc_ref, wblk_ref, ps_ref, cw_ref, a2_ref, yb_ref, ca, cz):
        @pl.when(g == 0)
        def _():
            ca[...] = jnp.zeros_like(ca)
            cz[...] = jnp.zeros_like(cz)

        a = a_ref[...].astype(F32)
        p = _pooled(a, ca[...], i * ts)
        mixed = _dot(p.astype(MXU), wblk_ref[...])
        a2_ref[...] = (mixed * ps_ref[...]).astype(MXU)
        z = bc_ref[...].astype(F32) * bx_ref[...].astype(F32)
        conv, _, _ = _conv3(z, cz[...], cw_ref[...])
        yb_ref[...] = (bb_ref[...].astype(F32) * conv).astype(MXU)
        ca[...] = a[ts - 16:]
        cz[...] = z[ts - 8:]

    ins = [("t", u, POOLW, 8), ("t", u, POOLW, 9), ("t", u, POOLW, 10), ("t", u, POOLW, 11), ("w", wblk), ("w", pool_scale),
           ("w", conv_b)]
    return _rows_call("poolconv_fwd", body, S, ts, ins, [("t", POOLW, MXU), ("t", POOLW, MXU)],
                      scratch=[pltpu.VMEM((16, POOLW), F32), pltpu.VMEM((8, POOLW), F32)])


def _poolconv_bwd(u, d_a2, d_yb, du, wblk, pool_scale, conv_b, ts=512):
    S = u.shape[0]

    def body(i, g, a_ref, bx_ref, bb_ref, bc_ref, ap_ref, bxp_ref, bcp_ref, da2_ref, dyb_ref, wblk_ref, ps_ref, cw_ref, _,
             o_ref, dps_ref, dwb_ref, dcw_ref, ce, cdz):
        @pl.when(g == 0)
        def _():
            ce[...] = jnp.zeros_like(ce)
            cdz[...] = jnp.zeros_like(cdz)

        first = (i > 0).astype(F32)
        a = a_ref[...].astype(F32)
        p = _pooled(a, ap_ref[...].astype(F32) * first, i * ts)
        pb = p.astype(MXU)
        mixed = _dot(pb, wblk_ref[...])
        da2 = da2_ref[...]
        dmixed = (da2 * ps_ref[...]).astype(MXU)
        dp = _dot_nt(dmixed, wblk_ref[...])
        _acc(dps_ref, g, jnp.sum(da2 * mixed, axis=0, keepdims=True))
        _acc(dwb_ref, g, _dot_tn(pb, dmixed))
        e = dp / _pool_cnt(i * ts, ts)
        ext = jnp.concatenate([e, ce[...]], axis=0)
        n = ts + 16
        f2 = ext + pltpu.roll(ext, n - 1, axis=0)
        f4 = f2 + pltpu.roll(f2, n - 2, axis=0)
        f8 = f4 + pltpu.roll(f4, n - 4, axis=0)
        f16 = f8 + pltpu.roll(f8, n - 8, axis=0)
        o_ref[:, 0:POOLW] = (_pool_select(f2, f4, f8, f16)[:ts] - dp).astype(o_ref.dtype)
        ce[...] = e[:16]

        bx, bb, bc = bx_ref[...].astype(F32), bb_ref[...].astype(F32), bc_ref[...].astype(F32)
        z = bc * bx
        w = cw_ref[...]
        conv, z1, z2 = _conv3(z, (bxp_ref[...].astype(F32) * bcp_ref[...].astype(F32))[8:16] * first, w)
        dyb = dyb_ref[...]
        dconv = dyb * bb
        dz = _conv3_t(dconv, cdz[...], w)
        o_ref[:, POOLW:2 * POOLW] = (dz * bc).astype(o_ref.dtype)
        o_ref[:, 2 * POOLW:3 * POOLW] = (dyb * conv).astype(o_ref.dtype)
        o_ref[:, 3 * POOLW:4 * POOLW] = (dz * bx).astype(o_ref.dtype)
        dw = jnp.concatenate([jnp.sum(dconv * z2, axis=0, keepdims=True), jnp.sum(dconv * z1, axis=0, keepdims=True),
                              jnp.sum(dconv * z, axis=0, keepdims=True)], axis=0)
        _acc(dcw_ref, g, dw)
        cdz[...] = dconv[:8]

    ins = [("t", u, POOLW, 8), ("t", u, POOLW, 9), ("t", u, POOLW, 10), ("t", u, POOLW, 11),
           ("h", u, 16, POOLW, 8), ("h", u, 16, POOLW, 9), ("h", u, 16, POOLW, 11),
           ("t", d_a2, POOLW, 0), ("t", d_yb, POOLW, 0), ("w", wblk), ("w", pool_scale), ("w", conv_b), ("x", du)]
    outs = [("c", IN_W, 4 * POOLW, GATE_W // (4 * POOLW), MXU), ("a", (1, POOLW), F32), ("a", (POOLW, POOLW), F32), ("a", (3, POOLW), F32)]
    return _rows_call("poolconv_bwd", body, S, ts, ins, outs, aliases={len(ins) - 1: 0},
                      scratch=[pltpu.VMEM((16, POOLW), F32), pltpu.VMEM((8, POOLW), F32)], reverse=True)


def _rope_tables(positions):
    inv = ROPE_THETA ** (-jnp.arange(0, 16, 2, dtype=F32) / 16)
    lane_inv = jnp.tile(jnp.concatenate([inv, inv, jnp.zeros(48, F32)]), 2)
    lane_sign = jnp.tile(jnp.concatenate([-jnp.ones(8, F32), jnp.ones(8, F32), jnp.zeros(48, F32)]), 2)
    ang = positions.astype(F32)[:, None] * lane_inv
    return jnp.where(lane_sign == 0, 1.0, jnp.cos(ang)), jnp.sin(ang) * lane_sign


def _partner(x):
    lane = lax.broadcasted_iota(jnp.int32, x.shape, 1) % 64
    return jnp.where(lane < 8, pltpu.roll(x, LANES - 8, axis=1), jnp.where(lane < 16, pltpu.roll(x, 8, axis=1), 0.0))


def _rope(x, c, s):
    return x * c + _partner(x) * s


def _rope_t(x, c, s):
    return x * c + _partner(x * s)


def _rows_of(r, n, d):
    return pl.ds(r, n, stride=d) if d > 1 else pl.ds(0, n)


def _head_masks(shape):
    lane = lax.broadcasted_iota(jnp.int32, shape, 1) // 64
    return [lane == h for h in range(4)]


def _only(mask, x):
    return jnp.where(mask, x, jnp.zeros_like(x))


def _rope_perm(u, ctab, stab, ts=512):
    S = u.shape[0]
    nch = ATT_W // LANES

    def body(*refs):
        chunks, (c_ref, s_ref), outs, scr = refs[:3 * nch], refs[3 * nch:3 * nch + 2], refs[3 * nch + 2:-1], refs[-1]
        for k in range(3 * nch):
            scr[k] = chunks[k][...].astype(F32)
        for g, d in enumerate(DILS):
            n = ts // d
            for r in range(d):
                rows = _rows_of(r, n, d)
                c, s = c_ref[rows, :], s_ref[rows, :]
                for which in range(3):
                    parts = [scr.at[which * nch + j][rows, :] for j in (2 * g, 2 * g + 1)]
                    if which < 2:
                        parts = [_rope(x, c, s) for x in parts]
                    outs[which * 3 + g][r] = jnp.concatenate(parts, axis=1).astype(MXU)

    base = (IN_W - 3 * ATT_W) // LANES
    in_specs = [pl.BlockSpec((ts, LANES), lambda i, cb=base + k: (i, cb)) for k in range(3 * nch)]
    in_specs += [pl.BlockSpec((ts, LANES), lambda i: (i, 0))] * 2
    out_specs = [pl.BlockSpec((d, ts // d, ATT_O), lambda i: (0, i, 0)) for _ in range(3) for d in DILS]
    out_shape = [jax.ShapeDtypeStruct((d, S // d, ATT_O), MXU) for _ in range(3) for d in DILS]
    res = pl.pallas_call(body, grid=(S // ts,), in_specs=in_specs, out_specs=out_specs, out_shape=out_shape,
                         scratch_shapes=[pltpu.VMEM((3 * nch, ts, LANES), F32)],
                         compiler_params=_params(("arbitrary",)), name="rope_perm")(*([u] * (3 * nch)), ctab, stab)
    return [[res[which * 3 + g].reshape(S, ATT_O) for g in range(3)] for which in range(3)]


def _rope_unperm_bwd(dqkv, du, ctab, stab, ts=512):
    S = dqkv[0][0].shape[0]
    nch = ATT_W // LANES

    def body(*refs):
        ins, (c_ref, s_ref, _, o_ref, scr) = refs[:9], refs[9:]
        for g, d in enumerate(DILS):
            n = ts // d
            for r in range(d):
                rows = _rows_of(r, n, d)
                c, s = c_ref[rows, :], s_ref[rows, :]
                for which in range(3):
                    v = ins[which * 3 + g][r]
                    for half in range(2):
                        x = v[:, half * LANES:(half + 1) * LANES]
                        scr.at[which * nch + 2 * g + half][rows, :] = _rope_t(x, c, s) if which < 2 else x
        for j in range(3 * nch):
            o_ref[:, j * LANES:(j + 1) * LANES] = scr[j].astype(o_ref.dtype)

    in_specs = [pl.BlockSpec((d, ts // d, ATT_O), lambda i: (0, i, 0)) for _ in range(3) for d in DILS]
    in_specs += [pl.BlockSpec((ts, LANES), lambda i: (i, 0))] * 2 + [pl.BlockSpec(memory_space=pl.ANY)]
    args = [dqkv[which][g].reshape(d, S // d, ATT_O) for which in range(3) for g, d in enumerate(DILS)]
    last = (IN_W - 3 * ATT_W) // (3 * ATT_W)
    return pl.pallas_call(body, grid=(S // ts,), in_specs=in_specs, out_specs=pl.BlockSpec((ts, 3 * ATT_W), lambda i: (i, last)),
                          out_shape=jax.ShapeDtypeStruct((S, IN_W), MXU), scratch_shapes=[pltpu.VMEM((3 * nch, ts, LANES), F32)],
                          input_output_aliases={len(in_specs) - 1: 0},
                          compiler_params=_params(("arbitrary",)), name="rope_unperm_bwd")(*args, ctab, stab, du)


def _band_mask_keys(has_prev):
    r = lax.broadcasted_iota(jnp.int32, (QB, 2 * QB), 0)
    c = lax.broadcasted_iota(jnp.int32, (QB, 2 * QB), 1)
    return ((c < QB) & (c >= r) & has_prev) | ((c >= QB) & (c - QB <= r))


def _band_mask_queries(has_next):
    r = lax.broadcasted_iota(jnp.int32, (2 * QB, QB), 0)
    c = lax.broadcasted_iota(jnp.int32, (2 * QB, QB), 1)
    return ((r < QB) & (c <= r)) | ((r >= QB) & (c >= r - QB) & has_next)


ASUB = 4
_BIG = pl.BlockSpec((ASUB * QB, ATT_O), lambda b: (b, 0))
_PREV = pl.BlockSpec((QB, ATT_O), lambda b: (jnp.maximum(b * ASUB - 1, 0), 0))


def _sub(ref, j):
    return ref[j * QB:(j + 1) * QB]


def _attn_fwd(g, q, k, v):
    S = q.shape[0]
    nb = S // QB
    nblk = nb // DILS[g]

    def body(q_ref, kc_ref, kp_ref, vc_ref, vp_ref, o_ref, m_ref, l_ref):
        hm_kv, hm_o = _head_masks((2 * QB, ATT_O)), _head_masks((QB, ATT_O))
        for j in range(ASUB):
            ok = _band_mask_keys(((pl.program_id(0) * ASUB + j) & (nblk - 1)) > 0)
            k2 = jnp.concatenate([kp_ref[...] if j == 0 else _sub(kc_ref, j - 1), _sub(kc_ref, j)], axis=0)
            v2 = jnp.concatenate([vp_ref[...] if j == 0 else _sub(vc_ref, j - 1), _sub(vc_ref, j)], axis=0)
            qv = _sub(q_ref, j)
            o_acc = jnp.zeros((QB, ATT_O), F32)
            m_acc = jnp.zeros((QB, ATT_O), F32)
            l_acc = jnp.zeros((QB, ATT_O), F32)
            for h in range(4):
                s = jnp.where(ok, _dot_nt(qv, _only(hm_kv[h], k2)) * ATT_SCALE, NEG)
                m = jnp.max(s, axis=1, keepdims=True)
                p = jnp.exp(s - m)
                o_acc = o_acc + _dot(p.astype(MXU), _only(hm_kv[h], v2))
                m_acc = jnp.where(hm_o[h], m, m_acc)
                l_acc = jnp.where(hm_o[h], jnp.sum(p, axis=1, keepdims=True), l_acc)
            o_ref[j * QB:(j + 1) * QB] = o_acc
            m_ref[j * QB:(j + 1) * QB] = m_acc
            l_ref[j * QB:(j + 1) * QB] = l_acc

    shp = jax.ShapeDtypeStruct((S, ATT_O), F32)
    return pl.pallas_call(body, grid=(nb // ASUB,), in_specs=[_BIG, _BIG, _PREV, _BIG, _PREV],
                          out_specs=[_BIG] * 3, out_shape=[shp, shp, shp], compiler_params=_params(("arbitrary",)),
                          name=f"attn_fwd_{g}")(q, k, k, v, v)


def _natural(ref, d, scr, ts):
    if d == 1:
        return ref[0]
    n = ts // d
    for r in range(d):
        v = ref[r]
        scr.at[0][pl.ds(r, n, stride=d), :] = v[:, 0:LANES]
        scr.at[1][pl.ds(r, n, stride=d), :] = v[:, LANES:2 * LANES]
    return jnp.concatenate([scr[0], scr[1]], axis=1)


def _attn_combine(oml, ts=512):
    S = oml[0][0].shape[0]

    def body(*refs):
        ins, (att_ref, out_ref, lse_ref, scr) = refs[:9], refs[9:]
        o, m, l = [[_natural(ins[3 * g + k], d, scr, ts) for g, d in enumerate(DILS)] for k in range(3)]
        mx = jnp.maximum(jnp.maximum(m[0], m[1]), m[2])
        w = [jnp.exp(m[g] - mx) for g in range(3)]
        den = w[0] * l[0] + w[1] * l[1] + w[2] * l[2]
        out = (w[0] * o[0] + w[1] * o[1] + w[2] * o[2]) / den
        out_ref[...] = out
        att_ref[...] = out.astype(MXU)
        lse_ref[...] = mx + jnp.log(den)

    in_specs = [pl.BlockSpec((d, ts // d, ATT_O), lambda i: (0, i, 0)) for d in DILS for _ in range(3)]
    args = [a.reshape(d, S // d, ATT_O) for d, grp in zip(DILS, oml) for a in grp]
    blk = pl.BlockSpec((ts, ATT_O), lambda i: (i, 0))
    return pl.pallas_call(body, grid=(S // ts,), in_specs=in_specs, out_specs=[blk, blk, blk],
                          out_shape=[jax.ShapeDtypeStruct((S, ATT_O), MXU), jax.ShapeDtypeStruct((S, ATT_O), F32),
                                     jax.ShapeDtypeStruct((S, ATT_O), F32)],
                          scratch_shapes=[pltpu.VMEM((2, ts, LANES), F32)], compiler_params=_params(("arbitrary",)),
                          name="attn_combine")(*args)


def _attn_bwd_prep(datt, o, lse, ts=512):
    S = datt.shape[0]

    def body(da0, da1, o_ref, l0, l1, *rest):
        outs, dl = rest[:9], rest[9]
        prod = jnp.concatenate([da0[...], da1[...]], axis=1) * o_ref[...]
        delta = jnp.zeros((ts, ATT_O), F32)
        for hm in _head_masks((ts, ATT_O)):
            delta = jnp.where(hm, jnp.sum(_only(hm, prod), axis=1, keepdims=True), delta)
        dl[0] = delta[:, 0:LANES]
        dl[1] = delta[:, LANES:2 * LANES]
        for g, d in enumerate(DILS):
            n = ts // d
            for r in range(d):
                rows = _rows_of(r, n, d)
                outs[g][r] = jnp.concatenate([da0[rows, :], da1[rows, :]], axis=1).astype(MXU)
                outs[3 + g][r] = jnp.concatenate([dl.at[0][rows, :], dl.at[1][rows, :]], axis=1)
                outs[6 + g][r] = jnp.concatenate([l0[rows, :], l1[rows, :]], axis=1)

    half = lambda j: pl.BlockSpec((ts, LANES), lambda i: (i, j))
    out_specs = [pl.BlockSpec((d, ts // d, ATT_O), lambda i: (0, i, 0)) for _ in range(3) for d in DILS]
    out_shape = [jax.ShapeDtypeStruct((d, S // d, ATT_O), dt) for dt in (MXU, F32, F32) for d in DILS]
    res = pl.pallas_call(body, grid=(S // ts,), in_specs=[half(0), half(1), pl.BlockSpec((ts, ATT_O), lambda i: (i, 0)), half(0), half(1)],
                         out_specs=out_specs, out_shape=out_shape, scratch_shapes=[pltpu.VMEM((2, ts, LANES), F32)],
                         compiler_params=_params(("arbitrary",)), name="attn_bwd_prep")(datt, datt, o, lse, lse)
    return [[res[k * 3 + g].reshape(S, ATT_O) for g in range(3)] for k in range(3)]


def _head_col(x, h):
    return x[:, h * 64:h * 64 + 1]


def _attn_bwd(g, q, k, v, do, delta, lse):
    S = q.shape[0]
    nb = S // QB
    nblk = nb // DILS[g]

    def body(k_ref, v_ref, qc_ref, qn_ref, doc_ref, don_ref, dlc_ref, dln_ref, lc_ref, ln_ref, dq_ref, dk_ref, dv_ref, dq_scr):
        hms, hmk = _head_masks((2 * QB, ATT_O)), _head_masks((QB, ATT_O))
        first = pl.program_id(0) == 0

        @pl.when(first)
        def _():
            dq_scr[0:QB] = jnp.zeros((QB, ATT_O), F32)

        @pl.when(jnp.logical_not(first))
        def _():
            dq_scr[0:QB] = dq_scr[ASUB * QB:(ASUB + 1) * QB]

        dq_scr[QB:(ASUB + 1) * QB] = jnp.zeros((ASUB * QB, ATT_O), F32)

        def both(cur_ref, nxt_ref, j):
            return jnp.concatenate([_sub(cur_ref, j), nxt_ref[...] if j == ASUB - 1 else _sub(cur_ref, j + 1)], axis=0)

        for j in range(ASUB):
            ok = _band_mask_queries(((pl.program_id(0) * ASUB + j + 1) & (nblk - 1)) > 0)
            q2, do2, dl2, lse2 = both(qc_ref, qn_ref, j), both(doc_ref, don_ref, j), both(dlc_ref, dln_ref, j), both(lc_ref, ln_ref, j)
            kv, vv = _sub(k_ref, j), _sub(v_ref, j)
            dk = jnp.zeros((QB, ATT_O), F32)
            dv = jnp.zeros((QB, ATT_O), F32)
            dq2 = jnp.zeros((2 * QB, ATT_O), F32)
            for h, hm in enumerate(hms):
                qh, doh = _only(hm, q2), _only(hm, do2)
                p = jnp.where(ok, jnp.exp(_dot_nt(qh, kv) * ATT_SCALE - _head_col(lse2, h)), 0.0)
                ds = (p * (_dot_nt(doh, vv) - _head_col(dl2, h))).astype(MXU)
                dv = dv + _dot_tn(p.astype(MXU), doh)
                dk = dk + _dot_tn(ds, qh)
                dq2 = dq2 + _dot(ds, _only(hmk[h], kv))
            dk_ref[j * QB:(j + 1) * QB] = dk * ATT_SCALE
            dv_ref[j * QB:(j + 1) * QB] = dv
            dq_scr[j * QB:(j + 2) * QB] += dq2
        dq_ref[...] = dq_scr[0:ASUB * QB] * ATT_SCALE

    nxt = pl.BlockSpec((QB, ATT_O), lambda b: (jnp.minimum((b + 1) * ASUB, nb - 1), 0))
    shp = jax.ShapeDtypeStruct((S, ATT_O), F32)
    return pl.pallas_call(body, grid=(nb // ASUB,), in_specs=[_BIG, _BIG, _BIG, nxt, _BIG, nxt, _BIG, nxt, _BIG, nxt], out_specs=[_BIG] * 3,
                          out_shape=[shp, shp, shp], scratch_shapes=[pltpu.VMEM(((ASUB + 1) * QB, ATT_O), F32)],
                          compiler_params=_params(("arbitrary",)), name=f"attn_bwd_{g}")(k, v, q, q, do, do, delta, delta, lse, lse)


def _merge_fwd(x0, u, a2, yb, att, wa, wb, wc, w_out, g_post, ts=512):
    S = x0.shape[0]

    def body(i, g, x_ref, gate_ref, a2_ref, yb_ref, att_ref, wa_ref, wb_ref, wc_ref, wo_ref, gp_ref, mg_ref, y_ref, xo_ref):
        gate = lambda n: jax.nn.sigmoid(gate_ref[:, n * D:(n + 1) * D].astype(F32))
        merged = gate(0) * _dot_nt(a2_ref[...], wa_ref[...])
        merged = merged + gate(1) * _dot_nt(yb_ref[...], wb_ref[...])
        merged = merged + gate(2) * _dot_nt(att_ref[...], wc_ref[...])
        mb = merged.astype(MXU)
        mg_ref[...] = mb
        y = _dot(mb, wo_ref[...])
        y_ref[...] = y
        xo_ref[...] = x_ref[...] + _rms(y, gp_ref[...])[0]

    ins = [("t", x0, D, 0), ("t", u, GATE_W, 0), ("t", a2, POOLW, 0), ("t", yb, POOLW, 0), ("t", att, ATT_O, 0),
           ("w", wa), ("w", wb), ("w", wc), ("w", w_out), ("w", g_post)]
    return _rows_call("merge_fwd", body, S, ts, ins, [("t", D, MXU), ("t", D, F32), ("t", D, F32)])


def _merge_bwd(dx, y1, u, a2, yb, att, merged, wa, wb, wc, w_out, g_post, ts=512):
    S = dx.shape[0]
    last = S // ts - 1

    def body(i, g, dx_ref, y_ref, gate_ref, a2_ref, yb_ref, att_ref, mg_ref, wa_ref, wb_ref, wc_ref, wo_ref, gp_ref,
             dgate_ref, da2_ref, dyb_ref, datt_ref, dgp_ref, dwo_ref, dwa_ref, dwb_ref, dwc_ref, acc_o, acc_a, acc_b, acc_c):
        @pl.when(g == 0)
        def _():
            for acc in (acc_o, acc_a, acc_b, acc_c):
                acc[...] = jnp.zeros_like(acc)

        dxv, y = dx_ref[...], y_ref[...]
        dy, r = _rms_bwd(dxv * gp_ref[...], y)
        _acc(dgp_ref, g, jnp.sum(dxv * (y * r), axis=0, keepdims=True))
        dyb16 = dy.astype(MXU)
        acc_o[...] += _dot_tn(mg_ref[...], dyb16)
        dm = _dot_nt(dyb16, wo_ref[...])
        for n, (src, w_ref, din_ref, acc) in enumerate(((a2_ref, wa_ref, da2_ref, acc_a), (yb_ref, wb_ref, dyb_ref, acc_b),
                                                       (att_ref, wc_ref, datt_ref, acc_c))):
            gt = jax.nn.sigmoid(gate_ref[:, n * D:(n + 1) * D].astype(F32))
            br = _dot_nt(src[...], w_ref[...])
            dgate_ref[:, n * D:(n + 1) * D] = (dm * br * gt * (1.0 - gt)).astype(dgate_ref.dtype)
            dbr = (dm * gt).astype(MXU)
            acc[...] += _dot_tn(dbr, src[...])
            din_ref[...] = _dot(dbr, w_ref[...])

        @pl.when(g == last)
        def _():
            for out, acc in ((dwo_ref, acc_o), (dwa_ref, acc_a), (dwb_ref, acc_b), (dwc_ref, acc_c)):
                out[...] = acc[...].astype(MXU)

    ins = [("t", dx, D, 0), ("t", y1, D, 0), ("t", u, GATE_W, 0), ("t", a2, POOLW, 0), ("t", yb, POOLW, 0), ("t", att, ATT_O, 0),
           ("t", merged, D, 0), ("w", wa), ("w", wb), ("w", wc), ("w", w_out), ("w", g_post)]
    wshapes = [(D, D), (D, POOLW), (D, POOLW), (D, ATT_O)]
    outs = [("c", IN_W, GATE_W, 0, MXU), ("t", POOLW, F32), ("t", POOLW, F32), ("t", ATT_O, F32), ("a", (1, D), F32)]
    outs += [("a", s, MXU) for s in wshapes]
    return _rows_call("merge_bwd", body, S, ts, ins, outs, scratch=[pltpu.VMEM(s, F32) for s in wshapes])


def _prenorm_bwd(name, dx_res, du, wt, x, g_pre, ts=256, lead=0):
    S = x.shape[0]
    N = du.shape[1]

    def body(i, g, dx_ref, du_ref, wt_ref, x_ref, g_ref, o_ref, dg_ref):
        if lead:
            dhv = _dot(du_ref[:, 0:lead], wt_ref[N - lead:N, :]) + _dot(du_ref[:, lead:N], wt_ref[0:N - lead, :])
        else:
            dhv = _dot(du_ref[...], wt_ref[...])
        xv = x_ref[...]
        dxn, r = _rms_bwd(dhv * g_ref[...], xv)
        o_ref[...] = dx_ref[...] + dxn
        _acc(dg_ref, g, jnp.sum(dhv * (xv * r), axis=0, keepdims=True))

    ins = [("t", dx_res, D, 0), ("t", du, N, 0), ("w", wt), ("t", x, D, 0), ("w", g_pre)]
    return _rows_call(name, body, S, ts, ins, [("t", D, F32), ("a", (1, D), F32)])


def _mem_heads(qm, kv_ref):
    out = []
    for h in range(4):
        q = qm[:, h * 128:(h + 1) * 128].astype(MXU)
        k = kv_ref[:, h * 128:(h + 1) * 128]
        v = kv_ref[:, MEM_W + h * 128:MEM_W + (h + 1) * 128]
        sc = _dot_nt(q, k) * MEM_SCALE
        e = jnp.exp(sc - jnp.max(sc, axis=1, keepdims=True))
        out.append((e / jnp.sum(e, axis=1, keepdims=True), q, k, v))
    return out


def _mem_fwd(x1, kv, g_pre, w_mq, w_mo, g_post, ts=512):
    S = x1.shape[0]

    def body(i, g, x_ref, kv_ref, gq_ref, wq_ref, wo_ref, gp_ref, om_ref, y_ref, xo_ref):
        x = x_ref[...]
        hb = _rms(x, gq_ref[...])[0].astype(MXU)
        qm = _dot(hb, wq_ref[...])
        om = jnp.concatenate([_dot(p.astype(MXU), v) for p, _, _, v in _mem_heads(qm, kv_ref)], axis=1).astype(MXU)
        om_ref[...] = om
        y = _dot_nt(om, wo_ref[...])
        y_ref[...] = y
        xo_ref[...] = x + _rms(y, gp_ref[...])[0]

    ins = [("t", x1, D, 0), ("w", kv), ("w", g_pre), ("w", w_mq), ("w", w_mo), ("w", g_post)]
    return _rows_call("mem_fwd", body, S, ts, ins, [("t", MEM_W, MXU), ("t", D, F32), ("t", D, F32)])


def _mem_bwd(dx2, ym, x1, om, kv, g_pre, w_mq, w_mo, g_post, ts=512):
    S = x1.shape[0]
    last = S // ts - 1

    def body(i, g, dx_ref, y_ref, x_ref, om_ref, kv_ref, gq_ref, wq_ref, wo_ref, gp_ref, dxo_ref, dgp_ref, dgq_ref, dkv_ref,
             dwo_ref, dwq_ref, acc_o, acc_q):
        dxv, y, x = dx_ref[...], y_ref[...], x_ref[...]
        dy, r = _rms_bwd(dxv * gp_ref[...], y)
        _acc(dgp_ref, g, jnp.sum(dxv * (y * r), axis=0, keepdims=True))
        dyb = dy.astype(MXU)
        dom = _dot(dyb, wo_ref[...])
        h, r1 = _rms(x, gq_ref[...])
        hb = h.astype(MXU)
        qm = _dot(hb, wq_ref[...])
        dqs = []

        @pl.when(g == 0)
        def _():
            dkv_ref[...] = jnp.zeros_like(dkv_ref)
            acc_o[...] = jnp.zeros_like(acc_o)
            acc_q[...] = jnp.zeros_like(acc_q)

        acc_o[...] += _dot_tn(dyb, om_ref[...])

        for hh, (p, q, k, v) in enumerate(_mem_heads(qm, kv_ref)):
            doh = dom[:, hh * 128:(hh + 1) * 128].astype(MXU)
            dp = _dot_nt(doh, v)
            dsc = (p * (dp - jnp.sum(dp * p, axis=1, keepdims=True)) * MEM_SCALE).astype(MXU)
            dqs.append(_dot(dsc, k))
            dkv_ref[:, hh * 128:(hh + 1) * 128] += _dot_tn(dsc, q)
            dkv_ref[:, MEM_W + hh * 128:MEM_W + (hh + 1) * 128] += _dot_tn(p.astype(MXU), doh)
        dq = jnp.concatenate(dqs, axis=1).astype(MXU)
        acc_q[...] += _dot_tn(hb, dq)
        dh = _dot_nt(dq, wq_ref[...])
        _acc(dgq_ref, g, jnp.sum(dh * (x * r1), axis=0, keepdims=True))
        dxo_ref[...] = dxv + _rms_bwd(dh * gq_ref[...], x)[0]

        @pl.when(g == last)
        def _():
            dwo_ref[...] = acc_o[...].astype(MXU)
            dwq_ref[...] = acc_q[...].astype(MXU)

    ins = [("t", dx2, D, 0), ("t", ym, D, 0), ("t", x1, D, 0), ("t", om, MEM_W, 0), ("w", kv), ("w", g_pre), ("w", w_mq), ("w", w_mo),
           ("w", g_post)]
    outs = [("t", D, F32), ("a", (1, D), F32), ("a", (1, D), F32), ("a", (256, D), F32), ("a", (D, MEM_W), MXU), ("a", (D, MEM_W), MXU)]
    return _rows_call("mem_bwd", body, S, ts, ins, outs, scratch=[pltpu.VMEM((D, MEM_W), F32), pltpu.VMEM((D, MEM_W), F32)])


def _gain_grad(name, dn, x):
    n = x.shape[0]

    def body(i, g, dn_ref, x_ref, o_ref):
        xv = x_ref[...]
        r = lax.rsqrt(jnp.mean(xv * xv, axis=-1, keepdims=True) + EPS)
        o_ref[...] = jnp.sum(dn_ref[...] * (xv * r), axis=0, keepdims=True)

    return _rows_call(name, body, n, n, [("t", dn, D, 0), ("t", x, D, 0)], [("a", (1, D), F32)])[0]


def _ffn_fwd(x2, u3, conv_f, w_down, g_post, ts=512):
    S = x2.shape[0]

    def body(i, g, x_ref, ua_ref, ub_ref, cw_ref, wd_ref, gp_ref, act_ref, y_ref, xo_ref, c_ref, cu):
        @pl.when(g == 0)
        def _():
            cu[...] = jnp.zeros_like(cu)

        ua = ua_ref[...].astype(F32)
        c, _, _ = _conv3(ua, cu[...], cw_ref[...])
        c_ref[...] = c.astype(MXU)
        act = (c * jax.nn.sigmoid(c) * ub_ref[...].astype(F32)).astype(MXU)
        act_ref[...] = act
        y = _dot(act, wd_ref[...])
        y_ref[...] = y
        xo_ref[...] = x_ref[...] + _rms(y, gp_ref[...])[0]
        cu[...] = ua[ts - 8:]

    ins = [("t", x2, D, 0), ("t", u3, D_FF, 0), ("t", u3, D_FF, 1), ("w", conv_f), ("w", w_down), ("w", g_post)]
    return _rows_call("ffn_fwd", body, S, ts, ins, [("t", D_FF, MXU), ("t", D, F32), ("t", D, F32), ("t", D_FF, MXU)],
                      scratch=[pltpu.VMEM((8, D_FF), F32)])


def _ffn_bwd(dx3, y3, u3, c, conv_f, w_down, g_post, ts=256):
    S = dx3.shape[0]

    def body(i, g, dx_ref, y_ref, ua_ref, ub_ref, c_ref, cw_ref, wd_ref, gp_ref, dy_ref, du_ref, dgp_ref, dcw_ref, cdc):
        @pl.when(g == 0)
        def _():
            cdc[...] = jnp.zeros_like(cdc)

        dxv, y = dx_ref[...], y_ref[...]
        dy, r = _rms_bwd(dxv * gp_ref[...], y)
        _acc(dgp_ref, g, jnp.sum(dxv * (y * r), axis=0, keepdims=True))
        dyb = dy.astype(MXU)
        dy_ref[...] = dyb
        dact = _dot_nt(dyb, wd_ref[...])
        ua, c, w = ua_ref[...].astype(F32), c_ref[...].astype(F32), cw_ref[...]
        sg = jax.nn.sigmoid(c)
        du_ref[:, D_FF:2 * D_FF] = (dact * (c * sg)).astype(du_ref.dtype)
        dc = dact * ub_ref[...].astype(F32) * (sg * (1.0 + c * (1.0 - sg)))
        dua, dc1, dc2 = _conv3_t(dc, cdc[...], w, shifted=True)
        du_ref[:, 0:D_FF] = dua.astype(du_ref.dtype)
        dw = jnp.concatenate([jnp.sum(ua * dc2, axis=0, keepdims=True), jnp.sum(ua * dc1, axis=0, keepdims=True),
                              jnp.sum(ua * dc, axis=0, keepdims=True)], axis=0)
        _acc(dcw_ref, g, dw)
        cdc[...] = dc[:8]

    ins = [("t", dx3, D, 0), ("t", y3, D, 0), ("t", u3, D_FF, 0), ("t", u3, D_FF, 1), ("t", c, D_FF, 0), ("w", conv_f),
           ("w", w_down), ("w", g_post)]
    outs = [("t", D, MXU), ("t", 2 * D_FF, MXU), ("a", (1, D), F32), ("a", (3, D_FF), F32)]
    return _rows_call("ffn_bwd", body, S, ts, ins, outs, scratch=[pltpu.VMEM((8, D_FF), F32)], reverse=True)


def _loss_head(x, target, ts=512):
    S = x.shape[0]

    def body(i, g, x_ref, t_ref, dx_ref, acc_ref):
        diff = x_ref[...] - t_ref[...]
        dx_ref[...] = diff * (1.0 / D)
        col = jnp.sum(diff * diff, axis=0, keepdims=True)
        part = col[:, 0:LANES]
        for j in range(1, D // LANES):
            part = part + col[:, j * LANES:(j + 1) * LANES]
        row = lax.broadcasted_iota(jnp.int32, (8, LANES), 0)
        _acc(acc_ref, g, jnp.where(row == 0, jnp.broadcast_to(part, (8, LANES)), 0.0))

    return _rows_call("loss_head", body, S, ts, [("t", x, D, 0), ("t", target, D, 0)], [("t", D, F32), ("a", (8, LANES), F32)])


_OPERAND_NAME = dict(w_in='w_in', w_branch_a='wa', w_branch_b='wb', w_branch_c='wc', w_out='w_out', w_mq='w_mq', w_mkv='w_mkv',
                     w_mo='w_mo', w_up='w_up', w_down='w_down')


def _big_operands(big):
    return {_OPERAND_NAME[n]: a for n, a in big.items()}


def _layer_weights(big, small, l):
    pool_w = small['pool_w'][l].astype(MXU)
    wblk = jnp.zeros((POOLW, POOLW), MXU)
    for g in range(4):
        wblk = lax.dynamic_update_slice(wblk, pool_w[g], (g * 96, g * 96))
    vec = lambda n: small[n][l].reshape(1, -1)
    return dict(
        _big_operands(big),
        wblk=wblk, pool_scale=vec('pool_scale'), conv_b=small['conv_b_w'][l], conv_f=small['conv_ffn_w'][l],
        g_mix_pre=vec('norm_mix_pre'), g_mix_post=vec('norm_mix_post'), g_mem_pre=vec('norm_mem_pre'),
        g_mem_post=vec('norm_mem_post'), g_memkv=vec('norm_memkv'), g_ffn_pre=vec('norm_ffn_pre'), g_ffn_post=vec('norm_ffn_post'))


def _layer_fwd(x0, mem, W, ctab, stab):
    sv = _layer_fwd_mix(x0, W, ctab, stab)
    return _layer_fwd_late(mem, W, sv), sv


def _layer_fwd_mix(x0, W, ctab, stab):
    return _layer_fwd_merge(W, _layer_fwd_branches(x0, W, ctab, stab))


def _layer_fwd_branches(x0, W, ctab, stab):
    sv = dict(x0=x0)
    sv['u'], sv['h1'] = _norm_mm("in_proj", x0, W['g_mix_pre'], W['w_in'], ts=2048, tn=IN_TILE, wt=True, rot=IN_ROT, out_dtype=MXU)
    sv['a2'], sv['yb'] = _poolconv_fwd(sv['u'], W['wblk'], W['pool_scale'], W['conv_b'])
    sv['qkv'] = q3, k3, v3 = _rope_perm(sv['u'], ctab, stab)
    sv['att'], sv['o'], sv['lse'] = _attn_combine([_attn_fwd(g, q3[g], k3[g], v3[g]) for g in range(3)])
    return sv


def _layer_fwd_merge(W, sv):
    sv['merged'], sv['y1'], sv['x1'] = _merge_fwd(sv['x0'], sv['u'], sv['a2'], sv['yb'], sv['att'], W['wa'], W['wb'], W['wc'],
                                                  W['w_out'], W['g_mix_post'])
    return sv


def _layer_fwd_late(mem, W, sv):
    sv['kv'], sv['memn'] = _norm_mm("mem_kv", mem, W['g_memkv'], W['w_mkv'], ts=256, tn=D, out_dtype=MXU)
    sv['om'], sv['ym'], sv['x2'] = _mem_fwd(sv['x1'], sv['kv'], W['g_mem_pre'], W['w_mq'], W['w_mo'], W['g_mem_post'])
    sv['u3'], sv['h3'] = _norm_mm("up_proj", sv['x2'], W['g_ffn_pre'], W['w_up'], ts=2048, tn=1408, wt=True, out_dtype=MXU)
    sv['act'], sv['y3'], x3, sv['c3'] = _ffn_fwd(sv['x2'], sv['u3'], W['conv_f'], W['w_down'], W['g_ffn_post'])
    return x3


def _layer_bwd(dx3, mem, W, sv, ctab, stab):
    dx1, g = _layer_bwd_late(dx3, mem, W, sv)
    dx0, g_mix = _layer_bwd_mix(dx1, W, sv, ctab, stab)
    return dx0, {**g, **g_mix}


def _layer_bwd_late(dx3, mem, W, sv):
    g = {}
    dy3, du3, g['norm_ffn_post'], g['conv_ffn_w'] = _ffn_bwd(dx3, sv['y3'], sv['u3'], sv['c3'], W['conv_f'], W['w_down'], W['g_ffn_post'])
    g['w_down'] = _mm_tn("dw_down", sv['act'], dy3, cap_k=256)
    g['w_up'] = _mm_tn("dw_up", du3, sv['h3'])
    dx2, g['norm_ffn_pre'] = _prenorm_bwd("ffn_pre_bwd", dx3, du3, W['w_up'], sv['x2'], W['g_ffn_pre'], ts=512)
    dx1, g['norm_mem_post'], g['norm_mem_pre'], dkv, g['w_mo'], g['w_mq'] = _mem_bwd(
        dx2, sv['ym'], sv['x1'], sv['om'], sv['kv'], W['g_mem_pre'], W['w_mq'], W['w_mo'], W['g_mem_post'])
    dkvb = dkv.astype(MXU)
    g['w_mkv'] = _mm_tn("dw_mkv", sv['memn'], dkvb)
    g['norm_memkv'] = _gain_grad("memkv_gain", _mm_nt("d_memn", dkvb, W['w_mkv'], ts=256, tn=512), mem)
    return dx1, g


def _layer_bwd_mix(dx1, W, sv, ctab, stab):
    du, g = _layer_bwd_mixers(dx1, W, sv, ctab, stab)
    g['w_in'] = _dw_in(du, sv)
    dx0, g['norm_mix_pre'] = _mix_pre_bwd(dx1, du, W, sv)
    return dx0, g


def _dw_in(du, sv):
    return _mm_tn("dw_in", du, sv['h1'], cap_k=IN_TILE, rot=IN_ROT)


def _mix_pre_bwd(dx1, du, W, sv):
    return _prenorm_bwd("mix_pre_bwd", dx1, du, W['w_in'], sv['x0'], W['g_mix_pre'], lead=GATE_W)


def _layer_bwd_mixers(dx1, W, sv, ctab, stab):
    parts, g = _layer_bwd_merge(dx1, W, sv)
    du, g_br = _layer_bwd_branches(parts, W, sv, ctab, stab)
    return du, {**g, **g_br}


def _layer_bwd_merge(dx1, W, sv):
    g = {}
    du, da2, dyb, datt, g['norm_mix_post'], g['w_out'], g['w_branch_a'], g['w_branch_b'], g['w_branch_c'] = _merge_bwd(
        dx1, sv['y1'], sv['u'], sv['a2'], sv['yb'], sv['att'], sv['merged'], W['wa'], W['wb'], W['wc'], W['w_out'], W['g_mix_post'])
    return (du, da2, dyb, datt), g


def _layer_bwd_branches(parts, W, sv, ctab, stab):
    du, da2, dyb, datt = parts
    g = {}
    du, g['pool_scale'], dwblk, g['conv_b_w'] = _poolconv_bwd(sv['u'], da2, dyb, du, W['wblk'], W['pool_scale'], W['conv_b'])
    g['pool_w'] = jnp.stack([dwblk[k * 96:(k + 1) * 96, k * 96:(k + 1) * 96] for k in range(4)])
    q3, k3, v3 = sv['qkv']
    do3, dl3, lse3 = _attn_bwd_prep(datt, sv['o'], sv['lse'])
    dqkv3 = [_attn_bwd(i, q3[i], k3[i], v3[i], do3[i], dl3[i], lse3[i]) for i in range(3)]
    du = _rope_unperm_bwd([[t[which] for t in dqkv3] for which in range(3)], du, ctab, stab)
    return du, g


def _local_step(x, mem, positions, target, big, small):
    ctab, stab = _rope_tables(positions)
    Ws = [_layer_weights(big[l], small, l) for l in range(DEPTH)]
    saved = []
    for l in range(DEPTH):
        x, sv = _layer_fwd(x, mem, Ws[l], ctab, stab)
        saved.append(sv)
    dx, acc = _loss_head(x, target)
    loss = jnp.sum(acc) * (0.5 / D)
    grads = [None] * DEPTH
    for l in reversed(range(DEPTH)):
        dx, grads[l] = _layer_bwd(dx, mem, Ws[l], saved[l], ctab, stab)
    return loss, dx, grads


_HBM = pl.BlockSpec(memory_space=pl.ANY)
MESH_ID = pl.DeviceIdType.MESH


def _all_gather(name, xs):
    n = len(xs)

    def body(*refs):
        x_refs, out_refs = refs[:n], refs[n:2 * n]
        send_sems, recv_sems, local_sems = refs[2 * n:]
        x, y, c = lax.axis_index("x"), lax.axis_index("y"), lax.axis_index("c")
        me, sibling = (x, y, c), (x, y, 1 - c)
        chips = [(1 - x, y), (x, 1 - y), (1 - x, 1 - y)]

        def slot(a, p):
            return out_refs[a].at[4 * p[0] + 2 * p[1] + p[2]]

        def copy(a, k, block, to, src=None):
            return pltpu.make_async_remote_copy(src_ref=slot(a, block) if src is None else src, dst_ref=slot(a, block),
                                                send_sem=send_sems.at[a, k], recv_sem=recv_sems.at[a, k], device_id=to,
                                                device_id_type=MESH_ID)

        started = []
        for a in range(n):
            mine = pltpu.make_async_copy(x_refs[a], slot(a, me), local_sems.at[a])
            mine.start()
            started.append(mine)
        first = []
        for a in range(n):
            first.append(copy(a, 0, me, sibling, src=x_refs[a]))
            first += [copy(a, 1 + j, me, (*chip, c), src=x_refs[a]) for j, chip in enumerate(chips)]
        for cp in first:
            cp.start()
        passed = []
        for j, chip in enumerate(chips):
            for a in range(n):
                copy(a, 1 + j, (*chip, c), me).wait_recv()
                fw = copy(a, 4 + j, (*chip, c), sibling)
                fw.start()
                passed.append(fw)
        for a in range(n):
            copy(a, 0, sibling, me).wait_recv()
            for j, chip in enumerate(chips):
                copy(a, 4 + j, (*chip, 1 - c), me).wait_recv()
        for cp in first + passed:
            cp.wait_send()
        for mine in started:
            mine.wait()

    return pl.pallas_call(
        body, out_shape=[jax.ShapeDtypeStruct((N_DEV,) + x.shape, x.dtype) for x in xs], in_specs=[_HBM] * n, out_specs=[_HBM] * n,
        scratch_shapes=[pltpu.SemaphoreType.DMA((n, 7)), pltpu.SemaphoreType.DMA((n, 7)), pltpu.SemaphoreType.DMA((n,))],
        name=name)(*xs)


_SEM =pl.BlockSpec(memory_space=pltpu.SEMAPHORE)
_IN_HBM = pl.BlockSpec(memory_space=pltpu.HBM)
_SIDE_EFFECT = pltpu.SideEffectType.DATAFLOW_SIDE_EFFECTING


def _push_copies(src_refs, land_refs, send_sems, recv_sems, per_peer):
    x, y, c = lax.axis_index("x"), lax.axis_index("y"), lax.axis_index("c")
    me = 4 * x + 2 * y + c
    copies = []
    for r in range(1, N_DEV):
        px, py, pc = x ^ ((r >> 2) & 1), y ^ ((r >> 1) & 1), c ^ (r & 1)
        for a, (s, d) in enumerate(zip(src_refs, land_refs)):
            k = a * (N_DEV - 1) + r - 1
            copies.append(pltpu.make_async_remote_copy(src_ref=s.at[4 * px + 2 * py + pc] if per_peer else s, dst_ref=d.at[me],
                                                       send_sem=send_sems.at[k], recv_sem=recv_sems.at[k],
                                                       device_id=(px, py, pc), device_id_type=MESH_ID))
    return copies


def _push_start(name, srcs, per_peer, after):
    n = len(srcs)
    lands = [lax.empty((N_DEV,) + (s.shape[1:] if per_peer else s.shape), s.dtype) for s in srcs]

    def body(*refs):
        for cp in _push_copies(refs[:n], refs[n:2 * n], refs[2 * n + 1], refs[2 * n + 2], per_peer):
            cp.start()
        refs[-1][...] = jnp.zeros_like(refs[-1])

    hbm = [pltpu.HBM(a.shape, a.dtype) for a in (*srcs, *lands)]
    sems = pltpu.SemaphoreType.DMA((n * (N_DEV - 1),))
    out = pl.pallas_call(
        body, name=name, out_shape=(sems, sems, *hbm, jax.ShapeDtypeStruct((8, LANES), F32)),
        in_specs=[_IN_HBM] * (2 * n) + [pl.BlockSpec(memory_space=pl.ANY)],
        out_specs=(_SEM, _SEM, *[_IN_HBM] * (2 * n), pl.BlockSpec(memory_space=pltpu.VMEM)),
        input_output_aliases={a: 2 + a for a in range(2 * n)},
        compiler_params=pltpu.CompilerParams(has_side_effects=_SIDE_EFFECT),
    )(*[pltpu.with_memory_space_constraint(a, pltpu.HBM) for a in (*srcs, *lands)], after)
    return out[0], out[1], out[2:2 + n], out[2 + n:2 + 2 * n], out[-1]


def _push_wait(name, started, per_peer, after):
    send_sems, recv_sems, srcs, lands, _ = started
    n = len(srcs)

    def body(*refs):
        for cp in _push_copies(refs[:n], refs[n:2 * n], refs[2 * n], refs[2 * n + 1], per_peer):
            cp.wait_send()
            cp.wait_recv()

    out = pl.pallas_call(
        body, name=name, out_shape=[pltpu.HBM(a.shape, a.dtype) for a in (*srcs, *lands)],
        in_specs=[_IN_HBM] * (2 * n) + [_SEM, _SEM, pl.BlockSpec(memory_space=pl.ANY)], out_specs=[_IN_HBM] * (2 * n),
        input_output_aliases={a: a for a in range(2 * n)},
        compiler_params=pltpu.CompilerParams(has_side_effects=_SIDE_EFFECT),
    )(*srcs, *lands, send_sems, recv_sems, after)
    if per_peer:
        return list(zip(out[:n], out[n:]))
    return _with_own(out[n:], out[:n], _my_slot())


def _my_slot():
    return 4 * lax.axis_index("x") + 2 * lax.axis_index("y") + lax.axis_index("c")


def _row_tile(rows, cols, budget):
    if rows * cols * 4 <= budget or rows % 16:
        return rows
    best = 16
    for t in range(16, rows + 1, 16):
        if rows % t == 0 and t * cols * 4 <= budget:
            best = t
    return best


def _slot_total(r_ref, own_ref):
    me = _my_slot()
    g = jnp.where(me == 0, own_ref[...], r_ref[0]).astype(F32)
    for k in range(1, N_DEV):
        g = g + jnp.where(me == k, own_ref[...], r_ref[k]).astype(F32)
    return g


def _sum_slots(name, pushed):
    src, recv = pushed
    _, R, C = recv.shape
    tr = _row_tile(R, C, 1 << 20)

    def body(r_ref, own_ref, o_ref):
        o_ref[...] = _slot_total(r_ref, own_ref)

    return pl.pallas_call(body, grid=(R // tr,),
                          in_specs=[pl.BlockSpec((N_DEV, tr, C), lambda i: (0, i, 0)), pl.BlockSpec((None, tr, C), lambda i: (_my_slot(), i, 0))],
                          out_specs=pl.BlockSpec((tr, C), lambda i: (i, 0)), out_shape=jax.ShapeDtypeStruct((R, C), F32),
                          compiler_params=_params(("arbitrary",)), name=name)(recv, src)


def _adamw_step(gv, w_ref, m_ref, v_ref, d_ref, mo_ref, vo_ref):
    mn = ADAM_B1 * m_ref[...] + (1.0 - ADAM_B1) * gv
    vn = ADAM_B2 * v_ref[...] + (1.0 - ADAM_B2) * (gv * gv)
    mo_ref[...] = mn
    vo_ref[...] = vn
    c1 = 1.0 - ADAM_B1 ** ADAM_STEP
    c2 = 1.0 - ADAM_B2 ** ADAM_STEP
    d_ref[...] = -ADAM_LR * ((mn / c1) / (jnp.sqrt(vn / c2) + ADAM_EPS) + ADAM_WD * w_ref[...])


def _adamw(name, g, w, m, v):
    shape = w.shape
    R, C = shape[-2], shape[-1]
    view = (-1, R, C)
    L = w.size // (R * C)
    tr = _row_tile(R, C, 1 << 20)

    def body(g_ref, w_ref, m_ref, v_ref, d_ref, mo_ref, vo_ref):
        _adamw_step(g_ref[...], w_ref, m_ref, v_ref, d_ref, mo_ref, vo_ref)

    blk = pl.BlockSpec((None, tr, C), lambda l, i: (l, i, 0))
    shp = jax.ShapeDtypeStruct((L, R, C), F32)
    outs = pl.pallas_call(body, grid=(L, R // tr), in_specs=[blk, blk, blk, blk], out_specs=[blk, blk, blk], out_shape=[shp, shp, shp],
                          compiler_params=_params(("arbitrary", "arbitrary")), name=name)(*[a.reshape(view) for a in (g, w, m, v)])
    return [o.reshape(shape) for o in outs]


def _sum_adamw(name, pushed, w, m, v):
    L, R, C = w.shape
    tr = _row_tile(R, C, 1 << 20)
    n_i = R // tr

    def body(*refs):
        shares, (w_ref, m_ref, v_ref, g_ref, d_ref, mo_ref, vo_ref) = refs[:2 * L], refs[2 * L:]
        for k in range(L):
            @pl.when(pl.program_id(0) == k)
            def _(k=k):
                g_ref[...] = _slot_total(shares[2 * k], shares[2 * k + 1])
        _adamw_step(g_ref[...], w_ref, m_ref, v_ref, d_ref, mo_ref, vo_ref)

    def during(k):
        return lambda l, i: jnp.where(l == k, i, jnp.where(l < k, 0, n_i - 1))

    in_specs, operands = [], []
    for k, (src, recv) in enumerate(pushed):
        in_specs += [pl.BlockSpec((N_DEV, tr, C), lambda l, i, at=during(k): (0, at(l, i), 0)),
                     pl.BlockSpec((None, tr, C), lambda l, i, at=during(k): (_my_slot(), at(l, i), 0))]
        operands += [recv, src]
    blk = pl.BlockSpec((None, tr, C), lambda l, i: (l, i, 0))
    shp = jax.ShapeDtypeStruct((L, R, C), F32)
    return pl.pallas_call(body, grid=(L, n_i), in_specs=in_specs + [blk, blk, blk], out_specs=[blk] * 4, out_shape=[shp] * 4,
                          compiler_params=_params(("arbitrary", "arbitrary")), name=name)(*operands, w, m, v)


def _pad_flat(a, n):
    a = a.reshape(-1)
    return jnp.pad(a, (0, n - a.shape[0]))


def _seg(n):
    return -(-n // FLAT_ALIGN) * FLAT_ALIGN


def _to_blocks(full, axis):
    shp = full.shape
    return jnp.moveaxis(full.reshape(shp[:axis] + (N_DEV, shp[axis] // N_DEV) + shp[axis + 1:]), axis, 0)


def _from_blocks(blocks, axis):
    b = jnp.moveaxis(blocks, 0, axis)
    shp = b.shape
    return b.reshape(shp[:axis] + (shp[axis] * shp[axis + 1],) + shp[axis + 2:])


def _as_rows(shard, n):
    return shard.T if SHARD_AXIS[n] == 2 else shard


def _with_own(lands, own, me):
    return [lax.dynamic_update_slice(land, o[None], (me, 0, 0)) for land, o in zip(lands, own)]


def kernel(x, mem, positions, norm_mix_pre, norm_mix_post, w_in, pool_w, pool_scale, conv_b_w, w_branch_a, w_branch_b, w_branch_c, w_out, norm_mem_pre, norm_mem_post, norm_memkv, w_mq, w_mkv, w_mo, norm_ffn_pre, norm_ffn_post, w_up, conv_ffn_w, w_down, loss_target, m_norm_mix_pre, m_norm_mix_post, m_w_in, m_pool_w, m_pool_scale, m_conv_b_w, m_w_branch_a, m_w_branch_b, m_w_branch_c, m_w_out, m_norm_mem_pre, m_norm_mem_post, m_norm_memkv, m_w_mq, m_w_mkv, m_w_mo, m_norm_ffn_pre, m_norm_ffn_post, m_w_up, m_conv_ffn_w, m_w_down, v_norm_mix_pre, v_norm_mix_post, v_w_in, v_pool_w, v_pool_scale, v_conv_b_w, v_w_branch_a, v_w_branch_b, v_w_branch_c, v_w_out, v_norm_mem_pre, v_norm_mem_post, v_norm_memkv, v_w_mq, v_w_mkv, v_w_mo, v_norm_ffn_pre, v_norm_ffn_post, v_w_up, v_conv_ffn_w, v_w_down):
    w = dict(norm_mix_pre=norm_mix_pre, norm_mix_post=norm_mix_post, w_in=w_in, pool_w=pool_w, pool_scale=pool_scale, conv_b_w=conv_b_w, w_branch_a=w_branch_a, w_branch_b=w_branch_b, w_branch_c=w_branch_c, w_out=w_out, norm_mem_pre=norm_mem_pre, norm_mem_post=norm_mem_post, norm_memkv=norm_memkv, w_mq=w_mq, w_mkv=w_mkv, w_mo=w_mo, norm_ffn_pre=norm_ffn_pre, norm_ffn_post=norm_ffn_post, w_up=w_up, conv_ffn_w=conv_ffn_w, w_down=w_down)
    m = dict(norm_mix_pre=m_norm_mix_pre, norm_mix_post=m_norm_mix_post, w_in=m_w_in, pool_w=m_pool_w, pool_scale=m_pool_scale, conv_b_w=m_conv_b_w, w_branch_a=m_w_branch_a, w_branch_b=m_w_branch_b, w_branch_c=m_w_branch_c, w_out=m_w_out, norm_mem_pre=m_norm_mem_pre, norm_mem_post=m_norm_mem_post, norm_memkv=m_norm_memkv, w_mq=m_w_mq, w_mkv=m_w_mkv, w_mo=m_w_mo, norm_ffn_pre=m_norm_ffn_pre, norm_ffn_post=m_norm_ffn_post, w_up=m_w_up, conv_ffn_w=m_conv_ffn_w, w_down=m_w_down)
    v = dict(norm_mix_pre=v_norm_mix_pre, norm_mix_post=v_norm_mix_post, w_in=v_w_in, pool_w=v_pool_w, pool_scale=v_pool_scale, conv_b_w=v_conv_b_w, w_branch_a=v_w_branch_a, w_branch_b=v_w_branch_b, w_branch_c=v_w_branch_c, w_out=v_w_out, norm_mem_pre=v_norm_mem_pre, norm_mem_post=v_norm_mem_post, norm_memkv=v_norm_memkv, w_mq=v_w_mq, w_mkv=v_w_mkv, w_mo=v_w_mo, norm_ffn_pre=v_norm_ffn_pre, norm_ffn_post=v_norm_ffn_post, w_up=v_w_up, conv_ffn_w=v_conv_ffn_w, w_down=v_w_down)

    mix_big = [n for n in BIG if n not in LATE_BIG]
    block = lambda names, l: [_as_rows(w[n][l], n).astype(MXU) for n in names]
    conv = jnp.concatenate([_pad_flat(w[n], _seg(w[n].size)) for n in F32_GATHERED]).reshape(-1, LANES)
    groups = dict(m=MERGE_BIG, b=LATE_BIG, a=mix_big)
    got0 = _all_gather("weights_all_gather_0", block(['w_in'], 0) + [conv])
    conv_all = got0[-1].reshape(N_DEV, -1)
    small, off = {n: w[n] for n in WEIGHTS if n not in SHARD_AXIS}, 0
    for n in F32_GATHERED:
        small[n] = _from_blocks(conv_all[:, off:off + w[n].size].reshape((N_DEV,) + w[n].shape), 2)
        off += _seg(w[n].size)
    whole = lambda names, got: {n: o.reshape(-1, o.shape[-1]) for n, o in zip(names, got)}
    pushes, after = {}, got0[0]
    for tag, l in (('m', 0), ('b', 0), ('a', 1)):
        pushes[tag, l] = _push_start(f"weights_push_start_{l}{tag}", block(groups[tag], l), False, after)
        after = pushes[tag, l][4]

    def arrived(tag, l, done):
        return _big_operands(whole(groups[tag], _push_wait(f"weights_push_wait_{l}{tag}", pushes[tag, l], False, done)))

    ctab, stab = _rope_tables(positions[0])
    W0 = _layer_weights(whole(['w_in'], got0), small, 0)
    sv0 = _layer_fwd_branches(x[0], dict(W0, g_mix_pre=W0['g_mix_pre'] + after[0, 0]), ctab, stab)
    W0.update(arrived('m', 0, sv0['att']))
    sv0 = _layer_fwd_merge(W0, sv0)
    W0.update(arrived('b', 0, sv0['x1']))
    x1 = _layer_fwd_late(mem[0], W0, sv0)
    pushes['b', 1] = _push_start("weights_push_start_1b", block(groups['b'], 1), False, x1)
    W1 = _layer_weights({}, small, 1)
    W1.update(arrived('a', 1, pushes['b', 1][4]))
    sv1 = _layer_fwd_mix(x1, W1, ctab, stab)
    W1.update(arrived('b', 1, sv1['x1']))
    x2 = _layer_fwd_late(mem[0], W1, sv1)
    dx, acc = _loss_head(x2, loss_target[0])
    loss = lax.psum(jnp.sum(acc) * (0.5 / D), MESH_AXES)
    grads = [None] * DEPTH
    dx, grads[1] = _layer_bwd(dx, mem[0], W1, sv1, ctab, stab)
    sent = [None, [grads[1][n].reshape(N_DEV, -1, grads[1][n].shape[-1]) for n in BIG]]
    push_g = _push_start("grads_push_start_1", sent[1], True, dx)
    dx, g_late = _layer_bwd_late(dx, mem[0], dict(W0, g_ffn_post=W0['g_ffn_post'] + push_g[4][0, 0]), sv0)
    sent_late = [g_late[n].reshape(N_DEV, -1, g_late[n].shape[-1]) for n in LATE_BIG]
    push_l = _push_start("grads_push_start_0", sent_late, True, dx)
    parts, g_mix = _layer_bwd_merge(dx, dict(W0, g_mix_post=W0['g_mix_post'] + push_l[4][0, 0]), sv0)
    sent_merge = [g_mix[n].reshape(N_DEV, -1, g_mix[n].shape[-1]) for n in MERGE_BIG]
    push_m = _push_start("grads_push_start_0m", sent_merge, True, parts[0])
    du, g_br = _layer_bwd_branches(parts, dict(W0, pool_scale=W0['pool_scale'] + push_m[4][0, 0]), sv0, ctab, stab)
    g_mix.update(g_br)
    g_mix['w_in'] = _dw_in(du, sv0)
    sent_in = [g_mix['w_in'].reshape(N_DEV, -1, D)]
    push_i = _push_start("grads_push_start_in", sent_in, True, du)
    dx, g_mix['norm_mix_pre'] = _mix_pre_bwd(dx, du, dict(W0, g_mix_pre=W0['g_mix_pre'] + push_i[4][0, 0]), sv0)
    grads[0] = {**g_late, **g_mix}
    recv1 = _push_wait("grads_push_wait_1", push_g, True, dx)
    recv_late = _push_wait("grads_push_wait_0", push_l, True, dx)
    recv_merge = _push_wait("grads_push_wait_0m", push_m, True, dx)

    misc_names = [n for n in WEIGHTS if n not in BIG]
    stacked = {n: jnp.stack([grads[l][n].reshape(small[n].shape[1:]) for l in range(DEPTH)]) for n in misc_names}
    rows = [(_to_blocks(stacked[n], 2) if n in SHARD_AXIS else jnp.broadcast_to(stacked[n][None], (N_DEV,) + stacked[n].shape))
            for n in misc_names]
    segs = [_seg(w[n].size) for n in misc_names]
    misc = jnp.concatenate([jnp.pad(r.reshape(N_DEV, -1), ((0, 0), (0, s - r[0].size))) for r, s in zip(rows, segs)],
                           axis=1).reshape(N_DEV, -1, LANES)
    push_x = _push_start("grads_push_start_small", [misc], True, dx)
    g_out, shares = {}, {}
    for l, names, recv in ((1, BIG, recv1), (0, LATE_BIG, recv_late), (0, MERGE_BIG, recv_merge)):
        for n, r in zip(names, recv):
            shares[n, l] = r

    swap = lambda a: jnp.swapaxes(a, 1, 2)

    def update(n):
        if n not in BIG:
            return [g_out[n], *_adamw(f"adamw_{n}", g_out[n], w[n], m[n], v[n])]
        of_layers = [shares[n, l] for l in range(DEPTH)]
        if SHARD_AXIS[n] == 1:
            return _sum_adamw(f"adamw_{n}", of_layers, w[n], m[n], v[n])
        if w[n].shape[2] % LANES:
            return [swap(a) for a in _sum_adamw(f"adamw_{n}", of_layers, swap(w[n]), swap(m[n]), swap(v[n]))]
        g = swap(jnp.stack([_sum_slots(f"sum_{n}_{l}", s) for l, s in enumerate(of_layers)]))
        return [g, *_adamw(f"adamw_{n}", g, w[n], m[n], v[n])]

    done = {n: update(n) for n in BIG if n != 'w_in'}
    shares['w_in', 0] = _push_wait("grads_push_wait_in", push_i, True, done[BIG[-1]][1])[0]
    done['w_in'] = update('w_in')
    after_w_in = swap(done['w_in'][1])
    misc_sum = _sum_slots("sum_misc", _push_wait("grads_push_wait_small", push_x, True, after_w_in)[0]).reshape(-1)
    off = 0
    for n, s in zip(misc_names, segs):
        g_out[n] = misc_sum[off:off + w[n].size].reshape(w[n].shape)
        done[n] = update(n)
        off += s
    return (loss, dx[None], *[done[n][k] for k in range(4) for n in WEIGHTS])
```

```python
import jax
import jax.numpy as jnp
from jax import lax
from jax.experimental import pallas as pl
from jax.experimental.pallas import tpu as pltpu

F32 = jnp.float32
MXU = jnp.bfloat16

D = 1024
DEPTH = 2
POOLW = 384
ATT_W = 768
ATT_O = 256
GATE_W = 3 * D
IN_W = 6912
IN_TILE = 768
IN_ROT = (IN_W - GATE_W) // IN_TILE
MEM_W = 512
D_FF = 2816
EPS = 1e-6
ROPE_THETA = 500000.0
QB = 128
DILS = (1, 4, 16)
NEG = -1e30
MEM_SCALE = 128 ** -0.5
ATT_SCALE = 0.125

ADAM_LR, ADAM_B1, ADAM_B2, ADAM_EPS, ADAM_WD, ADAM_STEP = 0.001, 0.9, 0.999, 1e-08, 0.01, 10

N_DEV = 8
MESH_AXES = ("x", "y", "c")
LANES = 128
FLAT_ALIGN = 2048

WEIGHTS = ['norm_mix_pre', 'norm_mix_post', 'w_in', 'pool_w', 'pool_scale', 'conv_b_w', 'w_branch_a', 'w_branch_b',
           'w_branch_c', 'w_out', 'norm_mem_pre', 'norm_mem_post', 'norm_memkv', 'w_mq', 'w_mkv', 'w_mo',
           'norm_ffn_pre', 'norm_ffn_post', 'w_up', 'conv_ffn_w', 'w_down']
SHARD_AXIS = {'w_in': 2, 'conv_b_w': 2, 'w_branch_a': 2, 'w_branch_b': 2, 'w_branch_c': 2, 'w_out': 1, 'w_mq': 1,
              'w_mkv': 1, 'w_mo': 2, 'w_up': 2, 'conv_ffn_w': 2, 'w_down': 1}
F32_GATHERED = ('conv_b_w', 'conv_ffn_w')
BIG = [n for n in WEIGHTS if n in SHARD_AXIS and n not in F32_GATHERED]
LATE_BIG = ['w_mq', 'w_mkv', 'w_mo', 'w_up', 'w_down']
MERGE_BIG = ['w_branch_a', 'w_branch_b', 'w_branch_c', 'w_out']


VMEM_LIMIT_MB = 60


def _params(sem):
    return pltpu.CompilerParams(dimension_semantics=sem, vmem_limit_bytes=VMEM_LIMIT_MB << 20)


def _dot(a, b, prec=None):
    return lax.dot_general(a, b, (((1,), (0,)), ((), ())), preferred_element_type=F32, precision=prec)


def _dot_nt(a, b, prec=None):
    return lax.dot_general(a, b, (((1,), (1,)), ((), ())), preferred_element_type=F32, precision=prec)


def _dot_tn(a, b, prec=None):
    return lax.dot_general(a, b, (((0,), (0,)), ((), ())), preferred_element_type=F32, precision=prec)


def _tile(n, cap):
    if n <= cap:
        return n
    best = None
    for t in range(LANES, cap + 1, LANES):
        if n % t == 0:
            best = t
    assert best is not None, (n, cap)
    return best


def _rms(x, g):
    r = lax.rsqrt(jnp.mean(x * x, axis=-1, keepdims=True) + EPS)
    return x * r * g, r


def _rms_bwd(w, y):
    r = lax.rsqrt(jnp.mean(y * y, axis=-1, keepdims=True) + EPS)
    return r * w - y * (r * r * r) * jnp.mean(w * y, axis=-1, keepdims=True), r


def _rows_call(name, body, n_rows, ts, ins, outs, scratch=(), reverse=False, aliases=None):
    nt = n_rows // ts
    assert nt * ts == n_rows

    def tile_of(g):
        return (nt - 1 - g) if reverse else g

    in_specs, args = [], []
    for op in ins:
        if op[0] == "t":
            _, a, cw, cb = op
            in_specs.append(pl.BlockSpec((ts, cw), lambda g, cb=cb: (tile_of(g), cb)))
        elif op[0] == "h":
            _, a, hr, cw, cb = op
            in_specs.append(pl.BlockSpec((hr, cw), lambda g, cb=cb, k=ts // hr: (jnp.maximum(tile_of(g) * k - 1, 0), cb)))
        elif op[0] == "x":
            _, a = op
            in_specs.append(pl.BlockSpec(memory_space=pl.ANY))
        else:
            _, a = op
            in_specs.append(pl.BlockSpec(a.shape, lambda g, n=a.ndim: (0,) * n))
        args.append(a)
    out_specs, out_shape = [], []
    for op in outs:
        if op[0] == "t":
            _, cols, dt = op
            out_specs.append(pl.BlockSpec((ts, cols), lambda g: (tile_of(g), 0)))
            out_shape.append(jax.ShapeDtypeStruct((n_rows, cols), dt))
        elif op[0] == "c":
            _, total, cols, cb, dt = op
            out_specs.append(pl.BlockSpec((ts, cols), lambda g, cb=cb: (tile_of(g), cb)))
            out_shape.append(jax.ShapeDtypeStruct((n_rows, total), dt))
        else:
            _, shp, dt = op
            out_specs.append(pl.BlockSpec(shp, lambda g, n=len(shp): (0,) * n))
            out_shape.append(jax.ShapeDtypeStruct(shp, dt))

    def kern(*refs):
        g = pl.program_id(0)
        body(tile_of(g), g, *refs)

    return pl.pallas_call(kern, grid=(nt,), in_specs=in_specs, out_specs=out_specs, out_shape=out_shape,
                          scratch_shapes=list(scratch), input_output_aliases=aliases or {},
                          compiler_params=_params(("arbitrary",)), name=name)(*args)


def _acc(ref, g, val):
    @pl.when(g == 0)
    def _():
        ref[...] = val

    @pl.when(g != 0)
    def _():
        ref[...] += val


def _norm_mm(name, x, g, w, ts, tn, out_dtype=F32, wt=False, rot=0):
    S, K = x.shape
    N = w.shape[0] if wt else w.shape[1]
    assert wt or not rot

    def body(x_ref, g_ref, w_ref, o_ref, h_ref, hs):
        @pl.when(pl.program_id(1) == 0)
        def _():
            h, _ = _rms(x_ref[...], g_ref[...])
            hs[...] = h.astype(MXU)
            h_ref[...] = h.astype(MXU)

        o_ref[...] = (_dot_nt if wt else _dot)(hs[...], w_ref[...]).astype(out_dtype)

    w_spec = pl.BlockSpec((tn, K), lambda i, j: ((j + rot) % (N // tn), 0)) if wt else pl.BlockSpec((K, tn), lambda i, j: (0, j))
    return pl.pallas_call(
        body, grid=(S // ts, N // tn),
        in_specs=[pl.BlockSpec((ts, K), lambda i, j: (i, 0)), pl.BlockSpec((1, K), lambda i, j: (0, 0)), w_spec],
        out_specs=[pl.BlockSpec((ts, tn), lambda i, j: (i, j)), pl.BlockSpec((ts, K), lambda i, j: (i, 0))],
        out_shape=[jax.ShapeDtypeStruct((S, N), out_dtype), jax.ShapeDtypeStruct((S, K), MXU)],
        scratch_shapes=[pltpu.VMEM((ts, K), MXU)],
        compiler_params=_params(("arbitrary", "arbitrary")), name=name)(x, g, w)


def _mm_nt(name, a, b, ts, tn, out_dtype=F32):
    M, K = a.shape
    N = b.shape[0]

    def body(a_ref, b_ref, o_ref):
        o_ref[...] = _dot_nt(a_ref[...], b_ref[...]).astype(out_dtype)

    return pl.pallas_call(
        body, grid=(M // ts, N // tn),
        in_specs=[pl.BlockSpec((ts, K), lambda i, j: (i, 0)), pl.BlockSpec((tn, K), lambda i, j: (j, 0))],
        out_specs=pl.BlockSpec((ts, tn), lambda i, j: (i, j)), out_shape=jax.ShapeDtypeStruct((M, N), out_dtype),
        compiler_params=_params(("arbitrary", "arbitrary")), name=name)(a, b)


def _mm_tn(name, a, b, cap_k=512, cap_n=1024, out_dtype=MXU, rot=0):
    S, K = a.shape
    N = b.shape[1]
    tk, tn = _tile(K, cap_k), _tile(N, cap_n)

    def body(a_ref, b_ref, o_ref):
        o_ref[...] = _dot_tn(a_ref[...], b_ref[...]).astype(out_dtype)

    return pl.pallas_call(
        body, grid=(K // tk, N // tn),
        in_specs=[pl.BlockSpec((S, tk), lambda i, j: (0, i)), pl.BlockSpec((S, tn), lambda i, j: (0, j))],
        out_specs=pl.BlockSpec((tk, tn), lambda i, j: ((i + rot) % (K // tk), j)), out_shape=jax.ShapeDtypeStruct((K, N), out_dtype),
        compiler_params=_params(("arbitrary", "arbitrary")), name=name)(a, b)


def _pool_cols(shape):
    col = lax.broadcasted_iota(jnp.int32, shape, 1)
    return col < 96, col < 192, col < 288


def _pool_select(s2, s4, s8, s16):
    c1, c2, c3 = _pool_cols(s2.shape)
    return jnp.where(c1, s2, jnp.where(c2, s4, jnp.where(c3, s8, s16)))


def _pool_cnt(t0, ts):
    c1, c2, c3 = _pool_cols((ts, POOLW))
    win = jnp.where(c1, 2, jnp.where(c2, 4, jnp.where(c3, 8, 16)))
    t = t0 + lax.broadcasted_iota(jnp.int32, (ts, POOLW), 0)
    return jnp.minimum(t + 1, win).astype(F32)


def _pooled(a, prev, t0):
    ts = a.shape[0]
    ext = jnp.concatenate([prev, a], axis=0)
    s2 = ext + pltpu.roll(ext, 1, axis=0)
    s4 = s2 + pltpu.roll(s2, 2, axis=0)
    s8 = s4 + pltpu.roll(s4, 4, axis=0)
    s16 = s8 + pltpu.roll(s8, 8, axis=0)
    sums = _pool_select(s2, s4, s8, s16)[16:]
    return sums / _pool_cnt(t0, ts) - a


def _conv3(z, prev8, w):
    ext = jnp.concatenate([prev8, z], axis=0)
    z1 = pltpu.roll(ext, 1, axis=0)[8:]
    z2 = pltpu.roll(ext, 2, axis=0)[8:]
    return w[0:1] * z2 + w[1:2] * z1 + w[2:3] * z, z1, z2


def _conv3_t(dc, next8, w, shifted=False):
    ts = dc.shape[0]
    ext = jnp.concatenate([dc, next8], axis=0)
    n = ts + 8
    u1 = pltpu.roll(ext, n - 1, axis=0)[:ts]
    u2 = pltpu.roll(ext, n - 2, axis=0)[:ts]
    out = w[2:3] * dc + w[1:2] * u1 + w[0:1] * u2
    return (out, u1, u2) if shifted else out


def _poolconv_fwd(u, wblk, pool_scale, conv_b, ts=512):
    S = u.shape[0]

    def body(i, g, a_ref, bx_ref, bb_ref, bc_ref, wblk_ref, ps_ref, cw_ref, a2_ref, yb_ref, ca, cz):
        @pl.when(g == 0)
        def _():
            ca[...] = jnp.zeros_like(ca)
            cz[...] = jnp.zeros_like(cz)

        a = a_ref[...].astype(F32)
        p = _pooled(a, ca[...], i * ts)
        mixed = _dot(p.astype(MXU), wblk_ref[...])
        a2_ref[...] = (mixed * ps_ref[...]).astype(MXU)
        z = bc_ref[...].astype(F32) * bx_ref[...].astype(F32)
        conv, _, _ = _conv3(z, cz[...], cw_ref[...])
        yb_ref[...] = (bb_ref[...].astype(F32) * conv).astype(MXU)
        ca[...] = a[ts - 16:]
        cz[...] = z[ts - 8:]

    ins = [("t", u, POOLW, 8), ("t", u, POOLW, 9), ("t", u, POOLW, 10), ("t", u, POOLW, 11), ("w", wblk), ("w", pool_scale),
           ("w", conv_b)]
    return _rows_call("poolconv_fwd", body, S, ts, ins, [("t", POOLW, MXU), ("t", POOLW, MXU)],
                      scratch=[pltpu.VMEM((16, POOLW), F32), pltpu.VMEM((8, POOLW), F32)])


def _poolconv_bwd(u, d_a2, d_yb, du, wblk, pool_scale, conv_b, ts=512):
    S = u.shape[0]

    def body(i, g, a_ref, bx_ref, bb_ref, bc_ref, ap_ref, bxp_ref, bcp_ref, da2_ref, dyb_ref, wblk_ref, ps_ref, cw_ref, _,
             o_ref, dps_ref, dwb_ref, dcw_ref, ce, cdz):
        @pl.when(g == 0)
        def _():
            ce[...] = jnp.zeros_like(ce)
            cdz[...] = jnp.zeros_like(cdz)

        first = (i > 0).astype(F32)
        a = a_ref[...].astype(F32)
        p = _pooled(a, ap_ref[...].astype(F32) * first, i * ts)
        pb = p.astype(MXU)
        mixed = _dot(pb, wblk_ref[...])
        da2 = da2_ref[...]
        dmixed = (da2 * ps_ref[...]).astype(MXU)
        dp = _dot_nt(dmixed, wblk_ref[...])
        _acc(dps_ref, g, jnp.sum(da2 * mixed, axis=0, keepdims=True))
        _acc(dwb_ref, g, _dot_tn(pb, dmixed))
        e = dp / _pool_cnt(i * ts, ts)
        ext = jnp.concatenate([e, ce[...]], axis=0)
        n = ts + 16
        f2 = ext + pltpu.roll(ext, n - 1, axis=0)
        f4 = f2 + pltpu.roll(f2, n - 2, axis=0)
        f8 = f4 + pltpu.roll(f4, n - 4, axis=0)
        f16 = f8 + pltpu.roll(f8, n - 8, axis=0)
        o_ref[:, 0:POOLW] = (_pool_select(f2, f4, f8, f16)[:ts] - dp).astype(o_ref.dtype)
        ce[...] = e[:16]

        bx, bb, bc = bx_ref[...].astype(F32), bb_ref[...].astype(F32), bc_ref[...].astype(F32)
        z = bc * bx
        w = cw_ref[...]
        conv, z1, z2 = _conv3(z, (bxp_ref[...].astype(F32) * bcp_ref[...].astype(F32))[8:16] * first, w)
        dyb = dyb_ref[...]
        dconv = dyb * bb
        dz = _conv3_t(dconv, cdz[...], w)
        o_ref[:, POOLW:2 * POOLW] = (dz * bc).astype(o_ref.dtype)
        o_ref[:, 2 * POOLW:3 * POOLW] = (dyb * conv).astype(o_ref.dtype)
        o_ref[:, 3 * POOLW:4 * POOLW] = (dz * bx).astype(o_ref.dtype)
        dw = jnp.concatenate([jnp.sum(dconv * z2, axis=0, keepdims=True), jnp.sum(dconv * z1, axis=0, keepdims=True),
                              jnp.sum(dconv * z, axis=0, keepdims=True)], axis=0)
        _acc(dcw_ref, g, dw)
        cdz[...] = dconv[:8]

    ins = [("t", u, POOLW, 8), ("t", u, POOLW, 9), ("t", u, POOLW, 10), ("t", u, POOLW, 11),
           ("h", u, 16, POOLW, 8), ("h", u, 16, POOLW, 9), ("h", u, 16, POOLW, 11),
           ("t", d_a2, POOLW, 0), ("t", d_yb, POOLW, 0), ("w", wblk), ("w", pool_scale), ("w", conv_b), ("x", du)]
    outs = [("c", IN_W, 4 * POOLW, GATE_W // (4 * POOLW), MXU), ("a", (1, POOLW), F32), ("a", (POOLW, POOLW), F32), ("a", (3, POOLW), F32)]
    return _rows_call("poolconv_bwd", body, S, ts, ins, outs, aliases={len(ins) - 1: 0},
                      scratch=[pltpu.VMEM((16, POOLW), F32), pltpu.VMEM((8, POOLW), F32)], reverse=True)


def _rope_tables(positions):
    inv = ROPE_THETA ** (-jnp.arange(0, 16, 2, dtype=F32) / 16)
    lane_inv = jnp.tile(jnp.concatenate([inv, inv, jnp.zeros(48, F32)]), 2)
    lane_sign = jnp.tile(jnp.concatenate([-jnp.ones(8, F32), jnp.ones(8, F32), jnp.zeros(48, F32)]), 2)
    ang = positions.astype(F32)[:, None] * lane_inv
    return jnp.where(lane_sign == 0, 1.0, jnp.cos(ang)), jnp.sin(ang) * lane_sign


def _partner(x):
    lane = lax.broadcasted_iota(jnp.int32, x.shape, 1) % 64
    return jnp.where(lane < 8, pltpu.roll(x, LANES - 8, axis=1), jnp.where(lane < 16, pltpu.roll(x, 8, axis=1), 0.0))


def _rope(x, c, s):
    return x * c + _partner(x) * s


def _rope_t(x, c, s):
    return x * c + _partner(x * s)


def _rows_of(r, n, d):
    return pl.ds(r, n, stride=d) if d > 1 else pl.ds(0, n)


def _head_masks(shape):
    lane = lax.broadcasted_iota(jnp.int32, shape, 1) // 64
    return [lane == h for h in range(4)]


def _only(mask, x):
    return jnp.where(mask, x, jnp.zeros_like(x))


def _rope_perm(u, ctab, stab, ts=512):
    S = u.shape[0]
    nch = ATT_W // LANES

    def body(*refs):
        chunks, (c_ref, s_ref), outs, scr = refs[:3 * nch], refs[3 * nch:3 * nch + 2], refs[3 * nch + 2:-1], refs[-1]
        for k in range(3 * nch):
            scr[k] = chunks[k][...].astype(F32)
        for g, d in enumerate(DILS):
            n = ts // d
            for r in range(d):
                rows = _rows_of(r, n, d)
                c, s = c_ref[rows, :], s_ref[rows, :]
                for which in range(3):
                    parts = [scr.at[which * nch + j][rows, :] for j in (2 * g, 2 * g + 1)]
                    if which < 2:
                        parts = [_rope(x, c, s) for x in parts]
                    outs[which * 3 + g][r] = jnp.concatenate(parts, axis=1).astype(MXU)

    base = (IN_W - 3 * ATT_W) // LANES
    in_specs = [pl.BlockSpec((ts, LANES), lambda i, cb=base + k: (i, cb)) for k in range(3 * nch)]
    in_specs += [pl.BlockSpec((ts, LANES), lambda i: (i, 0))] * 2
    out_specs = [pl.BlockSpec((d, ts // d, ATT_O), lambda i: (0, i, 0)) for _ in range(3) for d in DILS]
    out_shape = [jax.ShapeDtypeStruct((d, S // d, ATT_O), MXU) for _ in range(3) for d in DILS]
    res = pl.pallas_call(body, grid=(S // ts,), in_specs=in_specs, out_specs=out_specs, out_shape=out_shape,
                         scratch_shapes=[pltpu.VMEM((3 * nch, ts, LANES), F32)],
                         compiler_params=_params(("arbitrary",)), name="rope_perm")(*([u] * (3 * nch)), ctab, stab)
    return [[res[which * 3 + g].reshape(S, ATT_O) for g in range(3)] for which in range(3)]


def _rope_unperm_bwd(dqkv, du, ctab, stab, ts=512):
    S = dqkv[0][0].shape[0]
    nch = ATT_W // LANES

    def body(*refs):
        ins, (c_ref, s_ref, _, o_ref, scr) = refs[:9], refs[9:]
        for g, d in enumerate(DILS):
            n = ts // d
            for r in range(d):
                rows = _rows_of(r, n, d)
                c, s = c_ref[rows, :], s_ref[rows, :]
                for which in range(3):
                    v = ins[which * 3 + g][r]
                    for half in range(2):
                        x = v[:, half * LANES:(half + 1) * LANES]
                        scr.at[which * nch + 2 * g + half][rows, :] = _rope_t(x, c, s) if which < 2 else x
        for j in range(3 * nch):
            o_ref[:, j * LANES:(j + 1) * LANES] = scr[j].astype(o_ref.dtype)

    in_specs = [pl.BlockSpec((d, ts // d, ATT_O), lambda i: (0, i, 0)) for _ in range(3) for d in DILS]
    in_specs += [pl.BlockSpec((ts, LANES), lambda i: (i, 0))] * 2 + [pl.BlockSpec(memory_space=pl.ANY)]
    args = [dqkv[which][g].reshape(d, S // d, ATT_O) for which in range(3) for g, d in enumerate(DILS)]
    last = (IN_W - 3 * ATT_W) // (3 * ATT_W)
    return pl.pallas_call(body, grid=(S // ts,), in_specs=in_specs, out_specs=pl.BlockSpec((ts, 3 * ATT_W), lambda i: (i, last)),
                          out_shape=jax.ShapeDtypeStruct((S, IN_W), MXU), scratch_shapes=[pltpu.VMEM((3 * nch, ts, LANES), F32)],
                          input_output_aliases={len(in_specs) - 1: 0},
                          compiler_params=_params(("arbitrary",)), name="rope_unperm_bwd")(*args, ctab, stab, du)


def _band_mask_keys(has_prev):
    r = lax.broadcasted_iota(jnp.int32, (QB, 2 * QB), 0)
    c = lax.broadcasted_iota(jnp.int32, (QB, 2 * QB), 1)
    return ((c < QB) & (c >= r) & has_prev) | ((c >= QB) & (c - QB <= r))


def _band_mask_queries(has_next):
    r = lax.broadcasted_iota(jnp.int32, (2 * QB, 2 * QB), 0)
    c = lax.broadcasted_iota(jnp.int32, (2 * QB, 2 * QB), 1) & (QB - 1)
    return ((r < QB) & (c <= r)) | ((r >= QB) & (c >= r - QB) & has_next)


ASUB = 4
_BIG = pl.BlockSpec((ASUB * QB, ATT_O), lambda b: (b, 0))
_PREV = pl.BlockSpec((QB, ATT_O), lambda b: (jnp.maximum(b * ASUB - 1, 0), 0))


def _sub(ref, j):
    return ref[j * QB:(j + 1) * QB]


def _attn_fwd(g, q, k, v):
    S = q.shape[0]
    nb = S // QB
    nblk = nb // DILS[g]

    def body(q_ref, kc_ref, kp_ref, vc_ref, vp_ref, o_ref, m_ref, l_ref):
        hm_kv, hm_o = _head_masks((2 * QB, ATT_O)), _head_masks((QB, ATT_O))
        for j in range(ASUB):
            ok = _band_mask_keys(((pl.program_id(0) * ASUB + j) & (nblk - 1)) > 0)
            k2 = jnp.concatenate([kp_ref[...] if j == 0 else _sub(kc_ref, j - 1), _sub(kc_ref, j)], axis=0)
            v2 = jnp.concatenate([vp_ref[...] if j == 0 else _sub(vc_ref, j - 1), _sub(vc_ref, j)], axis=0)
            qv = _sub(q_ref, j)
            o_acc = jnp.zeros((QB, ATT_O), F32)
            m_acc = jnp.zeros((QB, ATT_O), F32)
            l_acc = jnp.zeros((QB, ATT_O), F32)
            for h in range(4):
                s = jnp.where(ok, _dot_nt(qv, _only(hm_kv[h], k2)) * ATT_SCALE, NEG)
                m = jnp.max(s, axis=1, keepdims=True)
                p = jnp.exp(s - m)
                o_acc = o_acc + _dot(p.astype(MXU), _only(hm_kv[h], v2))
                m_acc = jnp.where(hm_o[h], m, m_acc)
                l_acc = jnp.where(hm_o[h], jnp.sum(p, axis=1, keepdims=True), l_acc)
            o_ref[j * QB:(j + 1) * QB] = o_acc
            m_ref[j * QB:(j + 1) * QB] = m_acc
            l_ref[j * QB:(j + 1) * QB] = l_acc

    shp = jax.ShapeDtypeStruct((S, ATT_O), F32)
    return pl.pallas_call(body, grid=(nb // ASUB,), in_specs=[_BIG, _BIG, _PREV, _BIG, _PREV],
                          out_specs=[_BIG] * 3, out_shape=[shp, shp, shp], compiler_params=_params(("arbitrary",)),
                          name=f"attn_fwd_{g}")(q, k, k, v, v)


def _natural(ref, d, scr, ts):
    if d == 1:
        return ref[0]
    n = ts // d
    for r in range(d):
        v = ref[r]
        scr.at[0][pl.ds(r, n, stride=d), :] = v[:, 0:LANES]
        scr.at[1][pl.ds(r, n, stride=d), :] = v[:, LANES:2 * LANES]
    return jnp.concatenate([scr[0], scr[1]], axis=1)


def _attn_combine(oml, ts=512):
    S = oml[0][0].shape[0]

    def body(*refs):
        ins, (att_ref, out_ref, lse_ref, scr) = refs[:9], refs[9:]
        o, m, l = [[_natural(ins[3 * g + k], d, scr, ts) for g, d in enumerate(DILS)] for k in range(3)]
        mx = jnp.maximum(jnp.maximum(m[0], m[1]), m[2])
        w = [jnp.exp(m[g] - mx) for g in range(3)]
        den = w[0] * l[0] + w[1] * l[1] + w[2] * l[2]
        out = (w[0] * o[0] + w[1] * o[1] + w[2] * o[2]) / den
        out_ref[...] = out
        att_ref[...] = out.astype(MXU)
        lse_ref[...] = mx + jnp.log(den)

    in_specs = [pl.BlockSpec((d, ts // d, ATT_O), lambda i: (0, i, 0)) for d in DILS for _ in range(3)]
    args = [a.reshape(d, S // d, ATT_O) for d, grp in zip(DILS, oml) for a in grp]
    blk = pl.BlockSpec((ts, ATT_O), lambda i: (i, 0))
    return pl.pallas_call(body, grid=(S // ts,), in_specs=in_specs, out_specs=[blk, blk, blk],
                          out_shape=[jax.ShapeDtypeStruct((S, ATT_O), MXU), jax.ShapeDtypeStruct((S, ATT_O), F32),
                                     jax.ShapeDtypeStruct((S, ATT_O), F32)],
                          scratch_shapes=[pltpu.VMEM((2, ts, LANES), F32)], compiler_params=_params(("arbitrary",)),
                          name="attn_combine")(*args)


def _attn_bwd_prep(datt, o, lse, ts=512):
    S = datt.shape[0]

    def body(da0, da1, o_ref, l0, l1, *rest):
        outs, dl = rest[:9], rest[9]
        prod = jnp.concatenate([da0[...], da1[...]], axis=1) * o_ref[...]
        delta = jnp.zeros((ts, ATT_O), F32)
        for hm in _head_masks((ts, ATT_O)):
            delta = jnp.where(hm, jnp.sum(_only(hm, prod), axis=1, keepdims=True), delta)
        dl[0] = delta[:, 0:LANES]
        dl[1] = delta[:, LANES:2 * LANES]
        for g, d in enumerate(DILS):
            n = ts // d
            for r in range(d):
                rows = _rows_of(r, n, d)
                outs[g][r] = jnp.concatenate([da0[rows, :], da1[rows, :]], axis=1).astype(MXU)
                outs[3 + g][r] = jnp.concatenate([dl.at[0][rows, :], dl.at[1][rows, :]], axis=1)
                outs[6 + g][r] = jnp.concatenate([l0[rows, :], l1[rows, :]], axis=1)

    half = lambda j: pl.BlockSpec((ts, LANES), lambda i: (i, j))
    out_specs = [pl.BlockSpec((d, ts // d, ATT_O), lambda i: (0, i, 0)) for _ in range(3) for d in DILS]
    out_shape = [jax.ShapeDtypeStruct((d, S // d, ATT_O), dt) for dt in (MXU, F32, F32) for d in DILS]
    res = pl.pallas_call(body, grid=(S // ts,), in_specs=[half(0), half(1), pl.BlockSpec((ts, ATT_O), lambda i: (i, 0)), half(0), half(1)],
                         out_specs=out_specs, out_shape=out_shape, scratch_shapes=[pltpu.VMEM((2, ts, LANES), F32)],
                         compiler_params=_params(("arbitrary",)), name="attn_bwd_prep")(datt, datt, o, lse, lse)
    return [[res[k * 3 + g].reshape(S, ATT_O) for g in range(3)] for k in range(3)]


def _head_col(x, h):
    return x[:, h * 64:h * 64 + 1]


def _attn_bwd(g, q, k, v, do, delta, lse):
    S = q.shape[0]
    nb = S // QB
    nblk = nb // DILS[g]

    def body(k_ref, v_ref, qc_ref, qn_ref, doc_ref, don_ref, dlc_ref, dln_ref, lc_ref, ln_ref, dq_ref, dk_ref, dv_ref, dq_scr):
        hms, hmk = _head_masks((2 * QB, ATT_O)), _head_masks((QB, ATT_O))
        first_head = lax.broadcasted_iota(jnp.int32, (2 * QB, 2 * QB), 1) < QB
        first = pl.program_id(0) == 0

        @pl.when(first)
        def _():
            dq_scr[0:QB] = jnp.zeros((QB, ATT_O), F32)

        @pl.when(jnp.logical_not(first))
        def _():
            dq_scr[0:QB] = dq_scr[ASUB * QB:(ASUB + 1) * QB]

        dq_scr[QB:(ASUB + 1) * QB] = jnp.zeros((ASUB * QB, ATT_O), F32)

        def both(cur_ref, nxt_ref, j):
            return jnp.concatenate([_sub(cur_ref, j), nxt_ref[...] if j == ASUB - 1 else _sub(cur_ref, j + 1)], axis=0)

        for j in range(ASUB):
            ok = _band_mask_queries(((pl.program_id(0) * ASUB + j + 1) & (nblk - 1)) > 0)
            q2, do2, dl2, lse2 = both(qc_ref, qn_ref, j), both(doc_ref, don_ref, j), both(dlc_ref, dln_ref, j), both(lc_ref, ln_ref, j)
            kv, vv = _sub(k_ref, j), _sub(v_ref, j)
            dk = jnp.zeros((QB, ATT_O), F32)
            dv = jnp.zeros((QB, ATT_O), F32)
            dq2 = jnp.zeros((2 * QB, ATT_O), F32)
            for h in range(0, 4, 2):
                both_heads = hms[h] | hms[h + 1]
                qp, dop = _only(both_heads, q2), _only(both_heads, do2)
                kp = jnp.concatenate([_only(hmk[h], kv), _only(hmk[h + 1], kv)], axis=0)
                vp = jnp.concatenate([_only(hmk[h], vv), _only(hmk[h + 1], vv)], axis=0)
                lse_p = jnp.where(first_head, _head_col(lse2, h), _head_col(lse2, h + 1))
                dl_p = jnp.where(first_head, _head_col(dl2, h), _head_col(dl2, h + 1))
                p = jnp.where(ok, jnp.exp(_dot_nt(qp, kp) * ATT_SCALE - lse_p), 0.0)
                ds = (p * (_dot_nt(dop, vp) - dl_p)).astype(MXU)
                dvp, dkp = _dot_tn(p.astype(MXU), dop), _dot_tn(ds, qp)
                dv = dv + _only(hmk[h], dvp[0:QB]) + _only(hmk[h + 1], dvp[QB:2 * QB])
                dk = dk + _only(hmk[h], dkp[0:QB]) + _only(hmk[h + 1], dkp[QB:2 * QB])
                dq2 = dq2 + _dot(ds, kp)
            dk_ref[j * QB:(j + 1) * QB] = dk * ATT_SCALE
            dv_ref[j * QB:(j + 1) * QB] = dv
            dq_scr[j * QB:(j + 2) * QB] += dq2
        dq_ref[...] = dq_scr[0:ASUB * QB] * ATT_SCALE

    nxt = pl.BlockSpec((QB, ATT_O), lambda b: (jnp.minimum((b + 1) * ASUB, nb - 1), 0))
    shp = jax.ShapeDtypeStruct((S, ATT_O), F32)
    return pl.pallas_call(body, grid=(nb // ASUB,), in_specs=[_BIG, _BIG, _BIG, nxt, _BIG, nxt, _BIG, nxt, _BIG, nxt], out_specs=[_BIG] * 3,
                          out_shape=[shp, shp, shp], scratch_shapes=[pltpu.VMEM(((ASUB + 1) * QB, ATT_O), F32)],
                          compiler_params=_params(("arbitrary",)), name=f"attn_bwd_{g}")(k, v, q, q, do, do, delta, delta, lse, lse)


def _merge_fwd(x0, u, a2, yb, att, wa, wb, wc, w_out, g_post, ts=512):
    S = x0.shape[0]

    def body(i, g, x_ref, gate_ref, a2_ref, yb_ref, att_ref, wa_ref, wb_ref, wc_ref, wo_ref, gp_ref, mg_ref, y_ref, xo_ref):
        gate = lambda n: jax.nn.sigmoid(gate_ref[:, n * D:(n + 1) * D].astype(F32))
        merged = gate(0) * _dot_nt(a2_ref[...], wa_ref[...])
        merged = merged + gate(1) * _dot_nt(yb_ref[...], wb_ref[...])
        merged = merged + gate(2) * _dot_nt(att_ref[...], wc_ref[...])
        mb = merged.astype(MXU)
        mg_ref[...] = mb
        y = _dot(mb, wo_ref[...])
        y_ref[...] = y
        xo_ref[...] = x_ref[...] + _rms(y, gp_ref[...])[0]

    ins = [("t", x0, D, 0), ("t", u, GATE_W, 0), ("t", a2, POOLW, 0), ("t", yb, POOLW, 0), ("t", att, ATT_O, 0),
           ("w", wa), ("w", wb), ("w", wc), ("w", w_out), ("w", g_post)]
    return _rows_call("merge_fwd", body, S, ts, ins, [("t", D, MXU), ("t", D, F32), ("t", D, F32)])


def _merge_bwd(dx, y1, u, a2, yb, att, merged, wa, wb, wc, w_out, g_post, ts=512):
    S = dx.shape[0]
    last = S // ts - 1

    def body(i, g, dx_ref, y_ref, gate_ref, a2_ref, yb_ref, att_ref, mg_ref, wa_ref, wb_ref, wc_ref, wo_ref, gp_ref,
             dgate_ref, da2_ref, dyb_ref, datt_ref, dgp_ref, dwo_ref, dwa_ref, dwb_ref, dwc_ref, acc_o, acc_a, acc_b, acc_c):
        @pl.when(g == 0)
        def _():
            for acc in (acc_o, acc_a, acc_b, acc_c):
                acc[...] = jnp.zeros_like(acc)

        dxv, y = dx_ref[...], y_ref[...]
        dy, r = _rms_bwd(dxv * gp_ref[...], y)
        _acc(dgp_ref, g, jnp.sum(dxv * (y * r), axis=0, keepdims=True))
        dyb16 = dy.astype(MXU)
        acc_o[...] += _dot_tn(mg_ref[...], dyb16)
        dm = _dot_nt(dyb16, wo_ref[...])
        for n, (src, w_ref, din_ref, acc) in enumerate(((a2_ref, wa_ref, da2_ref, acc_a), (yb_ref, wb_ref, dyb_ref, acc_b),
                                                       (att_ref, wc_ref, datt_ref, acc_c))):
            gt = jax.nn.sigmoid(gate_ref[:, n * D:(n + 1) * D].astype(F32))
            br = _dot_nt(src[...], w_ref[...])
            dgate_ref[:, n * D:(n + 1) * D] = (dm * br * gt * (1.0 - gt)).astype(dgate_ref.dtype)
            dbr = (dm * gt).astype(MXU)
            acc[...] += _dot_tn(dbr, src[...])
            din_ref[...] = _dot(dbr, w_ref[...])

        @pl.when(g == last)
        def _():
            for out, acc in ((dwo_ref, acc_o), (dwa_ref, acc_a), (dwb_ref, acc_b), (dwc_ref, acc_c)):
                out[...] = acc[...].astype(MXU)

    ins = [("t", dx, D, 0), ("t", y1, D, 0), ("t", u, GATE_W, 0), ("t", a2, POOLW, 0), ("t", yb, POOLW, 0), ("t", att, ATT_O, 0),
           ("t", merged, D, 0), ("w", wa), ("w", wb), ("w", wc), ("w", w_out), ("w", g_post)]
    wshapes = [(D, D), (D, POOLW), (D, POOLW), (D, ATT_O)]
    outs = [("c", IN_W, GATE_W, 0, MXU), ("t", POOLW, F32), ("t", POOLW, F32), ("t", ATT_O, F32), ("a", (1, D), F32)]
    outs += [("a", s, MXU) for s in wshapes]
    return _rows_call("merge_bwd", body, S, ts, ins, outs, scratch=[pltpu.VMEM(s, F32) for s in wshapes])


def _prenorm_bwd(name, dx_res, du, wt, x, g_pre, ts=256, lead=0):
    S = x.shape[0]
    N = du.shape[1]

    def body(i, g, dx_ref, du_ref, wt_ref, x_ref, g_ref, o_ref, dg_ref):
        if lead:
            dhv = _dot(du_ref[:, 0:lead], wt_ref[N - lead:N, :]) + _dot(du_ref[:, lead:N], wt_ref[0:N - lead, :])
        else:
            dhv = _dot(du_ref[...], wt_ref[...])
        xv = x_ref[...]
        dxn, r = _rms_bwd(dhv * g_ref[...], xv)
        o_ref[...] = dx_ref[...] + dxn
        _acc(dg_ref, g, jnp.sum(dhv * (xv * r), axis=0, keepdims=True))

    ins = [("t", dx_res, D, 0), ("t", du, N, 0), ("w", wt), ("t", x, D, 0), ("w", g_pre)]
    return _rows_call(name, body, S, ts, ins, [("t", D, F32), ("a", (1, D), F32)])


def _mem_heads(qm, kv_ref):
    out = []
    for h in range(4):
        q = qm[:, h * 128:(h + 1) * 128].astype(MXU)
        k = kv_ref[:, h * 128:(h + 1) * 128]
        v = kv_ref[:, MEM_W + h * 128:MEM_W + (h + 1) * 128]
        sc = _dot_nt(q, k) * MEM_SCALE
        e = jnp.exp(sc - jnp.max(sc, axis=1, keepdims=True))
        out.append((e / jnp.sum(e, axis=1, keepdims=True), q, k, v))
    return out


def _mem_fwd(x1, kv, g_pre, w_mq, w_mo, g_post, ts=512):
    S = x1.shape[0]

    def body(i, g, x_ref, kv_ref, gq_ref, wq_ref, wo_ref, gp_ref, om_ref, y_ref, xo_ref):
        x = x_ref[...]
        hb = _rms(x, gq_ref[...])[0].astype(MXU)
        qm = _dot(hb, wq_ref[...])
        om = jnp.concatenate([_dot(p.astype(MXU), v) for p, _, _, v in _mem_heads(qm, kv_ref)], axis=1).astype(MXU)
        om_ref[...] = om
        y = _dot_nt(om, wo_ref[...])
        y_ref[...] = y
        xo_ref[...] = x + _rms(y, gp_ref[...])[0]

    ins = [("t", x1, D, 0), ("w", kv), ("w", g_pre), ("w", w_mq), ("w", w_mo), ("w", g_post)]
    return _rows_call("mem_fwd", body, S, ts, ins, [("t", MEM_W, MXU), ("t", D, F32), ("t", D, F32)])


def _mem_bwd(dx2, ym, x1, om, kv, g_pre, w_mq, w_mo, g_post, ts=512):
    S = x1.shape[0]
    last = S // ts - 1

    def body(i, g, dx_ref, y_ref, x_ref, om_ref, kv_ref, gq_ref, wq_ref, wo_ref, gp_ref, dxo_ref, dgp_ref, dgq_ref, dkv_ref,
             dwo_ref, dwq_ref, acc_o, acc_q):
        dxv, y, x = dx_ref[...], y_ref[...], x_ref[...]
        dy, r = _rms_bwd(dxv * gp_ref[...], y)
        _acc(dgp_ref, g, jnp.sum(dxv * (y * r), axis=0, keepdims=True))
        dyb = dy.astype(MXU)
        dom = _dot(dyb, wo_ref[...])
        h, r1 = _rms(x, gq_ref[...])
        hb = h.astype(MXU)
        qm = _dot(hb, wq_ref[...])
        dqs = []

        @pl.when(g == 0)
        def _():
            dkv_ref[...] = jnp.zeros_like(dkv_ref)
            acc_o[...] = jnp.zeros_like(acc_o)
            acc_q[...] = jnp.zeros_like(acc_q)

        acc_o[...] += _dot_tn(dyb, om_ref[...])

        for hh, (p, q, k, v) in enumerate(_mem_heads(qm, kv_ref)):
            doh = dom[:, hh * 128:(hh + 1) * 128].astype(MXU)
            dp = _dot_nt(doh, v)
            dsc = (p * (dp - jnp.sum(dp * p, axis=1, keepdims=True)) * MEM_SCALE).astype(MXU)
            dqs.append(_dot(dsc, k))
            dkv_ref[:, hh * 128:(hh + 1) * 128] += _dot_tn(dsc, q)
            dkv_ref[:, MEM_W + hh * 128:MEM_W + (hh + 1) * 128] += _dot_tn(p.astype(MXU), doh)
        dq = jnp.concatenate(dqs, axis=1).astype(MXU)
        acc_q[...] += _dot_tn(hb, dq)
        dh = _dot_nt(dq, wq_ref[...])
        _acc(dgq_ref, g, jnp.sum(dh * (x * r1), axis=0, keepdims=True))
        dxo_ref[...] = dxv + _rms_bwd(dh * gq_ref[...], x)[0]

        @pl.when(g == last)
        def _():
            dwo_ref[...] = acc_o[...].astype(MXU)
            dwq_ref[...] = acc_q[...].astype(MXU)

    ins = [("t", dx2, D, 0), ("t", ym, D, 0), ("t", x1, D, 0), ("t", om, MEM_W, 0), ("w", kv), ("w", g_pre), ("w", w_mq), ("w", w_mo),
           ("w", g_post)]
    outs = [("t", D, F32), ("a", (1, D), F32), ("a", (1, D), F32), ("a", (256, D), F32), ("a", (D, MEM_W), MXU), ("a", (D, MEM_W), MXU)]
    return _rows_call("mem_bwd", body, S, ts, ins, outs, scratch=[pltpu.VMEM((D, MEM_W), F32), pltpu.VMEM((D, MEM_W), F32)])


def _gain_grad(name, dn, x):
    n = x.shape[0]

    def body(i, g, dn_ref, x_ref, o_ref):
        xv = x_ref[...]
        r = lax.rsqrt(jnp.mean(xv * xv, axis=-1, keepdims=True) + EPS)
        o_ref[...] = jnp.sum(dn_ref[...] * (xv * r), axis=0, keepdims=True)

    return _rows_call(name, body, n, n, [("t", dn, D, 0), ("t", x, D, 0)], [("a", (1, D), F32)])[0]


def _ffn_fwd(x2, u3, conv_f, w_down, g_post, ts=512):
    S = x2.shape[0]

    def body(i, g, x_ref, ua_ref, ub_ref, cw_ref, wd_ref, gp_ref, act_ref, y_ref, xo_ref, c_ref, cu):
        @pl.when(g == 0)
        def _():
            cu[...] = jnp.zeros_like(cu)

        ua = ua_ref[...].astype(F32)
        c, _, _ = _conv3(ua, cu[...], cw_ref[...])
        c_ref[...] = c.astype(MXU)
        act = (c * jax.nn.sigmoid(c) * ub_ref[...].astype(F32)).astype(MXU)
        act_ref[...] = act
        y = _dot(act, wd_ref[...])
        y_ref[...] = y
        xo_ref[...] = x_ref[...] + _rms(y, gp_ref[...])[0]
        cu[...] = ua[ts - 8:]

    ins = [("t", x2, D, 0), ("t", u3, D_FF, 0), ("t", u3, D_FF, 1), ("w", conv_f), ("w", w_down), ("w", g_post)]
    return _rows_call("ffn_fwd", body, S, ts, ins, [("t", D_FF, MXU), ("t", D, F32), ("t", D, F32), ("t", D_FF, MXU)],
                      scratch=[pltpu.VMEM((8, D_FF), F32)])


def _ffn_bwd(dx3, y3, u3, c, conv_f, w_down, g_post, ts=256):
    S = dx3.shape[0]

    def body(i, g, dx_ref, y_ref, ua_ref, ub_ref, c_ref, cw_ref, wd_ref, gp_ref, dy_ref, du_ref, dgp_ref, dcw_ref, cdc):
        @pl.when(g == 0)
        def _():
            cdc[...] = jnp.zeros_like(cdc)

        dxv, y = dx_ref[...], y_ref[...]
        dy, r = _rms_bwd(dxv * gp_ref[...], y)
        _acc(dgp_ref, g, jnp.sum(dxv * (y * r), axis=0, keepdims=True))
        dyb = dy.astype(MXU)
        dy_ref[...] = dyb
        dact = _dot_nt(dyb, wd_ref[...])
        ua, c, w = ua_ref[...].astype(F32), c_ref[...].astype(F32), cw_ref[...]
        sg = jax.nn.sigmoid(c)
        du_ref[:, D_FF:2 * D_FF] = (dact * (c * sg)).astype(du_ref.dtype)
        dc = dact * ub_ref[...].astype(F32) * (sg * (1.0 + c * (1.0 - sg)))
        dua, dc1, dc2 = _conv3_t(dc, cdc[...], w, shifted=True)
        du_ref[:, 0:D_FF] = dua.astype(du_ref.dtype)
        dw = jnp.concatenate([jnp.sum(ua * dc2, axis=0, keepdims=True), jnp.sum(ua * dc1, axis=0, keepdims=True),
                              jnp.sum(ua * dc, axis=0, keepdims=True)], axis=0)
        _acc(dcw_ref, g, dw)
        cdc[...] = dc[:8]

    ins = [("t", dx3, D, 0), ("t", y3, D, 0), ("t", u3, D_FF, 0), ("t", u3, D_FF, 1), ("t", c, D_FF, 0), ("w", conv_f),
           ("w", w_down), ("w", g_post)]
    outs = [("t", D, MXU), ("t", 2 * D_FF, MXU), ("a", (1, D), F32), ("a", (3, D_FF), F32)]
    return _rows_call("ffn_bwd", body, S, ts, ins, outs, scratch=[pltpu.VMEM((8, D_FF), F32)], reverse=True)


def _loss_head(x, target, ts=512):
    S = x.shape[0]

    def body(i, g, x_ref, t_ref, dx_ref, acc_ref):
        diff = x_ref[...] - t_ref[...]
        dx_ref[...] = diff * (1.0 / D)
        col = jnp.sum(diff * diff, axis=0, keepdims=True)
        part = col[:, 0:LANES]
        for j in range(1, D // LANES):
            part = part + col[:, j * LANES:(j + 1) * LANES]
        row = lax.broadcasted_iota(jnp.int32, (8, LANES), 0)
        _acc(acc_ref, g, jnp.where(row == 0, jnp.broadcast_to(part, (8, LANES)), 0.0))

    return _rows_call("loss_head", body, S, ts, [("t", x, D, 0), ("t", target, D, 0)], [("t", D, F32), ("a", (8, LANES), F32)])


_OPERAND_NAME = dict(w_in='w_in', w_branch_a='wa', w_branch_b='wb', w_branch_c='wc', w_out='w_out', w_mq='w_mq', w_mkv='w_mkv',
                     w_mo='w_mo', w_up='w_up', w_down='w_down')


def _big_operands(big):
    return {_OPERAND_NAME[n]: a for n, a in big.items()}


def _layer_weights(big, small, l):
    pool_w = small['pool_w'][l].astype(MXU)
    wblk = jnp.zeros((POOLW, POOLW), MXU)
    for g in range(4):
        wblk = lax.dynamic_update_slice(wblk, pool_w[g], (g * 96, g * 96))
    vec = lambda n: small[n][l].reshape(1, -1)
    return dict(
        _big_operands(big),
        wblk=wblk, pool_scale=vec('pool_scale'), conv_b=small['conv_b_w'][l], conv_f=small['conv_ffn_w'][l],
        g_mix_pre=vec('norm_mix_pre'), g_mix_post=vec('norm_mix_post'), g_mem_pre=vec('norm_mem_pre'),
        g_mem_post=vec('norm_mem_post'), g_memkv=vec('norm_memkv'), g_ffn_pre=vec('norm_ffn_pre'), g_ffn_post=vec('norm_ffn_post'))


def _layer_fwd(x0, mem, W, ctab, stab):
    sv = _layer_fwd_mix(x0, W, ctab, stab)
    return _layer_fwd_late(mem, W, sv), sv


def _layer_fwd_mix(x0, W, ctab, stab):
    return _layer_fwd_merge(W, _layer_fwd_branches(x0, W, ctab, stab))


def _layer_fwd_branches(x0, W, ctab, stab):
    sv = dict(x0=x0)
    sv['u'], sv['h1'] = _norm_mm("in_proj", x0, W['g_mix_pre'], W['w_in'], ts=2048, tn=IN_TILE, wt=True, rot=IN_ROT, out_dtype=MXU)
    sv['a2'], sv['yb'] = _poolconv_fwd(sv['u'], W['wblk'], W['pool_scale'], W['conv_b'])
    sv['qkv'] = q3, k3, v3 = _rope_perm(sv['u'], ctab, stab)
    sv['att'], sv['o'], sv['lse'] = _attn_combine([_attn_fwd(g, q3[g], k3[g], v3[g]) for g in range(3)])
    return sv


def _layer_fwd_merge(W, sv):
    sv['merged'], sv['y1'], sv['x1'] = _merge_fwd(sv['x0'], sv['u'], sv['a2'], sv['yb'], sv['att'], W['wa'], W['wb'], W['wc'],
                                                  W['w_out'], W['g_mix_post'])
    return sv


def _layer_fwd_late(mem, W, sv):
    sv['kv'], sv['memn'] = _norm_mm("mem_kv", mem, W['g_memkv'], W['w_mkv'], ts=256, tn=D, out_dtype=MXU)
    sv['om'], sv['ym'], sv['x2'] = _mem_fwd(sv['x1'], sv['kv'], W['g_mem_pre'], W['w_mq'], W['w_mo'], W['g_mem_post'])
    sv['u3'], sv['h3'] = _norm_mm("up_proj", sv['x2'], W['g_ffn_pre'], W['w_up'], ts=2048, tn=1408, wt=True, out_dtype=MXU)
    sv['act'], sv['y3'], x3, sv['c3'] = _ffn_fwd(sv['x2'], sv['u3'], W['conv_f'], W['w_down'], W['g_ffn_post'])
    return x3


def _layer_bwd(dx3, mem, W, sv, ctab, stab):
    dx1, g = _layer_bwd_late(dx3, mem, W, sv)
    dx0, g_mix = _layer_bwd_mix(dx1, W, sv, ctab, stab)
    return dx0, {**g, **g_mix}


def _layer_bwd_late(dx3, mem, W, sv):
    g = {}
    dy3, du3, g['norm_ffn_post'], g['conv_ffn_w'] = _ffn_bwd(dx3, sv['y3'], sv['u3'], sv['c3'], W['conv_f'], W['w_down'], W['g_ffn_post'])
    g['w_down'] = _mm_tn("dw_down", sv['act'], dy3, cap_k=256)
    g['w_up'] = _mm_tn("dw_up", du3, sv['h3'])
    dx2, g['norm_ffn_pre'] = _prenorm_bwd("ffn_pre_bwd", dx3, du3, W['w_up'], sv['x2'], W['g_ffn_pre'], ts=512)
    dx1, g['norm_mem_post'], g['norm_mem_pre'], dkv, g['w_mo'], g['w_mq'] = _mem_bwd(
        dx2, sv['ym'], sv['x1'], sv['om'], sv['kv'], W['g_mem_pre'], W['w_mq'], W['w_mo'], W['g_mem_post'])
    dkvb = dkv.astype(MXU)
    g['w_mkv'] = _mm_tn("dw_mkv", sv['memn'], dkvb)
    g['norm_memkv'] = _gain_grad("memkv_gain", _mm_nt("d_memn", dkvb, W['w_mkv'], ts=256, tn=512), mem)
    return dx1, g


def _layer_bwd_mix(dx1, W, sv, ctab, stab):
    du, g = _layer_bwd_mixers(dx1, W, sv, ctab, stab)
    g['w_in'] = _dw_in(du, sv)
    dx0, g['norm_mix_pre'] = _mix_pre_bwd(dx1, du, W, sv)
    return dx0, g


def _dw_in(du, sv):
    return _mm_tn("dw_in", du, sv['h1'], cap_k=IN_TILE, rot=IN_ROT)


def _mix_pre_bwd(dx1, du, W, sv):
    return _prenorm_bwd("mix_pre_bwd", dx1, du, W['w_in'], sv['x0'], W['g_mix_pre'], lead=GATE_W)


def _layer_bwd_mixers(dx1, W, sv, ctab, stab):
    parts, g = _layer_bwd_merge(dx1, W, sv)
    du, g_br = _layer_bwd_branches(parts, W, sv, ctab, stab)
    return du, {**g, **g_br}


def _layer_bwd_merge(dx1, W, sv):
    g = {}
    du, da2, dyb, datt, g['norm_mix_post'], g['w_out'], g['w_branch_a'], g['w_branch_b'], g['w_branch_c'] = _merge_bwd(
        dx1, sv['y1'], sv['u'], sv['a2'], sv['yb'], sv['att'], sv['merged'], W['wa'], W['wb'], W['wc'], W['w_out'], W['g_mix_post'])
    return (du, da2, dyb, datt), g


def _layer_bwd_branches(parts, W, sv, ctab, stab):
    du, da2, dyb, datt = parts
    g = {}
    du, g['pool_scale'], dwblk, g['conv_b_w'] = _poolconv_bwd(sv['u'], da2, dyb, du, W['wblk'], W['pool_scale'], W['conv_b'])
    g['pool_w'] = jnp.stack([dwblk[k * 96:(k + 1) * 96, k * 96:(k + 1) * 96] for k in range(4)])
    q3, k3, v3 = sv['qkv']
    do3, dl3, lse3 = _attn_bwd_prep(datt, sv['o'], sv['lse'])
    dqkv3 = [_attn_bwd(i, q3[i], k3[i], v3[i], do3[i], dl3[i], lse3[i]) for i in range(3)]
    du = _rope_unperm_bwd([[t[which] for t in dqkv3] for which in range(3)], du, ctab, stab)
    return du, g


def _local_step(x, mem, positions, target, big, small):
    ctab, stab = _rope_tables(positions)
    Ws = [_layer_weights(big[l], small, l) for l in range(DEPTH)]
    saved = []
    for l in range(DEPTH):
        x, sv = _layer_fwd(x, mem, Ws[l], ctab, stab)
        saved.append(sv)
    dx, acc = _loss_head(x, target)
    loss = jnp.sum(acc) * (0.5 / D)
    grads = [None] * DEPTH
    for l in reversed(range(DEPTH)):
        dx, grads[l] = _layer_bwd(dx, mem, Ws[l], saved[l], ctab, stab)
    return loss, dx, grads


_HBM = pl.BlockSpec(memory_space=pl.ANY)
MESH_ID = pl.DeviceIdType.MESH


def _all_gather(name, xs):
    n = len(xs)

    def body(*refs):
        x_refs, out_refs = refs[:n], refs[n:2 * n]
        send_sems, recv_sems, local_sems = refs[2 * n:]
        x, y, c = lax.axis_index("x"), lax.axis_index("y"), lax.axis_index("c")
        me, sibling = (x, y, c), (x, y, 1 - c)
        chips = [(1 - x, y), (x, 1 - y), (1 - x, 1 - y)]

        def slot(a, p):
            return out_refs[a].at[4 * p[0] + 2 * p[1] + p[2]]

        def copy(a, k, block, to, src=None):
            return pltpu.make_async_remote_copy(src_ref=slot(a, block) if src is None else src, dst_ref=slot(a, block),
                                                send_sem=send_sems.at[a, k], recv_sem=recv_sems.at[a, k], device_id=to,
                                                device_id_type=MESH_ID)

        started = []
        for a in range(n):
            mine = pltpu.make_async_copy(x_refs[a], slot(a, me), local_sems.at[a])
            mine.start()
            started.append(mine)
        first = []
        for a in range(n):
            first.append(copy(a, 0, me, sibling, src=x_refs[a]))
            first += [copy(a, 1 + j, me, (*chip, c), src=x_refs[a]) for j, chip in enumerate(chips)]
        for cp in first:
            cp.start()
        passed = []
        for j, chip in enumerate(chips):
            for a in range(n):
                copy(a, 1 + j, (*chip, c), me).wait_recv()
                fw = copy(a, 4 + j, (*chip, c), sibling)
                fw.start()
                passed.append(fw)
        for a in range(n):
            copy(a, 0, sibling, me).wait_recv()
            for j, chip in enumerate(chips):
                copy(a, 4 + j, (*chip, 1 - c), me).wait_recv()
        for cp in first + passed:
            cp.wait_send()
        for mine in started:
            mine.wait()

    return pl.pallas_call(
        body, out_shape=[jax.ShapeDtypeStruct((N_DEV,) + x.shape, x.dtype) for x in xs], in_specs=[_HBM] * n, out_specs=[_HBM] * n,
        scratch_shapes=[pltpu.SemaphoreType.DMA((n, 7)), pltpu.SemaphoreType.DMA((n, 7)), pltpu.SemaphoreType.DMA((n,))],
        name=name)(*xs)


_SEM =pl.BlockSpec(memory_space=pltpu.SEMAPHORE)
_IN_HBM = pl.BlockSpec(memory_space=pltpu.HBM)
_SIDE_EFFECT = pltpu.SideEffectType.DATAFLOW_SIDE_EFFECTING


def _push_copies(src_refs, land_refs, send_sems, recv_sems, per_peer):
    x, y, c = lax.axis_index("x"), lax.axis_index("y"), lax.axis_index("c")
    me = 4 * x + 2 * y + c
    copies = []
    for r in range(1, N_DEV):
        px, py, pc = x ^ ((r >> 2) & 1), y ^ ((r >> 1) & 1), c ^ (r & 1)
        for a, (s, d) in enumerate(zip(src_refs, land_refs)):
            k = a * (N_DEV - 1) + r - 1
            copies.append(pltpu.make_async_remote_copy(src_ref=s.at[4 * px + 2 * py + pc] if per_peer else s, dst_ref=d.at[me],
                                                       send_sem=send_sems.at[k], recv_sem=recv_sems.at[k],
                                                       device_id=(px, py, pc), device_id_type=MESH_ID))
    return copies


def _push_start(name, srcs, per_peer, after):
    n = len(srcs)
    lands = [lax.empty((N_DEV,) + (s.shape[1:] if per_peer else s.shape), s.dtype) for s in srcs]

    def body(*refs):
        for cp in _push_copies(refs[:n], refs[n:2 * n], refs[2 * n + 1], refs[2 * n + 2], per_peer):
            cp.start()
        refs[-1][...] = jnp.zeros_like(refs[-1])

    hbm = [pltpu.HBM(a.shape, a.dtype) for a in (*srcs, *lands)]
    sems = pltpu.SemaphoreType.DMA((n * (N_DEV - 1),))
    out = pl.pallas_call(
        body, name=name, out_shape=(sems, sems, *hbm, jax.ShapeDtypeStruct((8, LANES), F32)),
        in_specs=[_IN_HBM] * (2 * n) + [pl.BlockSpec(memory_space=pl.ANY)],
        out_specs=(_SEM, _SEM, *[_IN_HBM] * (2 * n), pl.BlockSpec(memory_space=pltpu.VMEM)),
        input_output_aliases={a: 2 + a for a in range(2 * n)},
        compiler_params=pltpu.CompilerParams(has_side_effects=_SIDE_EFFECT),
    )(*[pltpu.with_memory_space_constraint(a, pltpu.HBM) for a in (*srcs, *lands)], after)
    return out[0], out[1], out[2:2 + n], out[2 + n:2 + 2 * n], out[-1]


def _push_wait(name, started, per_peer, after):
    send_sems, recv_sems, srcs, lands, _ = started
    n = len(srcs)

    def body(*refs):
        for cp in _push_copies(refs[:n], refs[n:2 * n], refs[2 * n], refs[2 * n + 1], per_peer):
            cp.wait_send()
            cp.wait_recv()

    out = pl.pallas_call(
        body, name=name, out_shape=[pltpu.HBM(a.shape, a.dtype) for a in (*srcs, *lands)],
        in_specs=[_IN_HBM] * (2 * n) + [_SEM, _SEM, pl.BlockSpec(memory_space=pl.ANY)], out_specs=[_IN_HBM] * (2 * n),
        input_output_aliases={a: a for a in range(2 * n)},
        compiler_params=pltpu.CompilerParams(has_side_effects=_SIDE_EFFECT),
    )(*srcs, *lands, send_sems, recv_sems, after)
    if per_peer:
        return list(zip(out[:n], out[n:]))
    return _with_own(out[n:], out[:n], _my_slot())


def _my_slot():
    return 4 * lax.axis_index("x") + 2 * lax.axis_index("y") + lax.axis_index("c")


def _row_tile(rows, cols, budget):
    if rows * cols * 4 <= budget or rows % 16:
        return rows
    best = 16
    for t in range(16, rows + 1, 16):
        if rows % t == 0 and t * cols * 4 <= budget:
            best = t
    return best


def _slot_total(r_ref, own_ref):
    me = _my_slot()
    g = jnp.where(me == 0, own_ref[...], r_ref[0]).astype(F32)
    for k in range(1, N_DEV):
        g = g + jnp.where(me == k, own_ref[...], r_ref[k]).astype(F32)
    return g


def _sum_slots(name, pushed):
    src, recv = pushed
    _, R, C = recv.shape
    tr = _row_tile(R, C, 1 << 20)

    def body(r_ref, own_ref, o_ref):
        o_ref[...] = _slot_total(r_ref, own_ref)

    return pl.pallas_call(body, grid=(R // tr,),
                          in_specs=[pl.BlockSpec((N_DEV, tr, C), lambda i: (0, i, 0)), pl.BlockSpec((None, tr, C), lambda i: (_my_slot(), i, 0))],
                          out_specs=pl.BlockSpec((tr, C), lambda i: (i, 0)), out_shape=jax.ShapeDtypeStruct((R, C), F32),
                          compiler_params=_params(("arbitrary",)), name=name)(recv, src)


def _adamw_step(gv, w_ref, m_ref, v_ref, d_ref, mo_ref, vo_ref):
    mn = ADAM_B1 * m_ref[...] + (1.0 - ADAM_B1) * gv
    vn = ADAM_B2 * v_ref[...] + (1.0 - ADAM_B2) * (gv * gv)
    mo_ref[...] = mn
    vo_ref[...] = vn
    c1 = 1.0 - ADAM_B1 ** ADAM_STEP
    c2 = 1.0 - ADAM_B2 ** ADAM_STEP
    d_ref[...] = -ADAM_LR * ((mn / c1) / (jnp.sqrt(vn / c2) + ADAM_EPS) + ADAM_WD * w_ref[...])


def _adamw(name, g, w, m, v):
    shape = w.shape
    R, C = shape[-2], shape[-1]
    view = (-1, R, C)
    L = w.size // (R * C)
    tr = _row_tile(R, C, 1 << 20)

    def body(g_ref, w_ref, m_ref, v_ref, d_ref, mo_ref, vo_ref):
        _adamw_step(g_ref[...], w_ref, m_ref, v_ref, d_ref, mo_ref, vo_ref)

    blk = pl.BlockSpec((None, tr, C), lambda l, i: (l, i, 0))
    shp = jax.ShapeDtypeStruct((L, R, C), F32)
    outs = pl.pallas_call(body, grid=(L, R // tr), in_specs=[blk, blk, blk, blk], out_specs=[blk, blk, blk], out_shape=[shp, shp, shp],
                          compiler_params=_params(("arbitrary", "arbitrary")), name=name)(*[a.reshape(view) for a in (g, w, m, v)])
    return [o.reshape(shape) for o in outs]


def _sum_adamw(name, pushed, w, m, v):
    L, R, C = w.shape
    tr = _row_tile(R, C, 1 << 20)
    n_i = R // tr

    def body(*refs):
        shares, (w_ref, m_ref, v_ref, g_ref, d_ref, mo_ref, vo_ref) = refs[:2 * L], refs[2 * L:]
        for k in range(L):
            @pl.when(pl.program_id(0) == k)
            def _(k=k):
                g_ref[...] = _slot_total(shares[2 * k], shares[2 * k + 1])
        _adamw_step(g_ref[...], w_ref, m_ref, v_ref, d_ref, mo_ref, vo_ref)

    def during(k):
        return lambda l, i: jnp.where(l == k, i, jnp.where(l < k, 0, n_i - 1))

    in_specs, operands = [], []
    for k, (src, recv) in enumerate(pushed):
        in_specs += [pl.BlockSpec((N_DEV, tr, C), lambda l, i, at=during(k): (0, at(l, i), 0)),
                     pl.BlockSpec((None, tr, C), lambda l, i, at=during(k): (_my_slot(), at(l, i), 0))]
        operands += [recv, src]
    blk = pl.BlockSpec((None, tr, C), lambda l, i: (l, i, 0))
    shp = jax.ShapeDtypeStruct((L, R, C), F32)
    return pl.pallas_call(body, grid=(L, n_i), in_specs=in_specs + [blk, blk, blk], out_specs=[blk] * 4, out_shape=[shp] * 4,
                          compiler_params=_params(("arbitrary", "arbitrary")), name=name)(*operands, w, m, v)


def _pad_flat(a, n):
    a = a.reshape(-1)
    return jnp.pad(a, (0, n - a.shape[0]))


def _seg(n):
    return -(-n // FLAT_ALIGN) * FLAT_ALIGN


def _to_blocks(full, axis):
    shp = full.shape
    return jnp.moveaxis(full.reshape(shp[:axis] + (N_DEV, shp[axis] // N_DEV) + shp[axis + 1:]), axis, 0)


def _from_blocks(blocks, axis):
    b = jnp.moveaxis(blocks, 0, axis)
    shp = b.shape
    return b.reshape(shp[:axis] + (shp[axis] * shp[axis + 1],) + shp[axis + 2:])


def _as_rows(shard, n):
    return shard.T if SHARD_AXIS[n] == 2 else shard


def _with_own(lands, own, me):
    return [lax.dynamic_update_slice(land, o[None], (me, 0, 0)) for land, o in zip(lands, own)]


def kernel(x, mem, positions, norm_mix_pre, norm_mix_post, w_in, pool_w, pool_scale, conv_b_w, w_branch_a, w_branch_b, w_branch_c, w_out, norm_mem_pre, norm_mem_post, norm_memkv, w_mq, w_mkv, w_mo, norm_ffn_pre, norm_ffn_post, w_up, conv_ffn_w, w_down, loss_target, m_norm_mix_pre, m_norm_mix_post, m_w_in, m_pool_w, m_pool_scale, m_conv_b_w, m_w_branch_a, m_w_branch_b, m_w_branch_c, m_w_out, m_norm_mem_pre, m_norm_mem_post, m_norm_memkv, m_w_mq, m_w_mkv, m_w_mo, m_norm_ffn_pre, m_norm_ffn_post, m_w_up, m_conv_ffn_w, m_w_down, v_norm_mix_pre, v_norm_mix_post, v_w_in, v_pool_w, v_pool_scale, v_conv_b_w, v_w_branch_a, v_w_branch_b, v_w_branch_c, v_w_out, v_norm_mem_pre, v_norm_mem_post, v_norm_memkv, v_w_mq, v_w_mkv, v_w_mo, v_norm_ffn_pre, v_norm_ffn_post, v_w_up, v_conv_ffn_w, v_w_down):
    w = dict(norm_mix_pre=norm_mix_pre, norm_mix_post=norm_mix_post, w_in=w_in, pool_w=pool_w, pool_scale=pool_scale, conv_b_w=conv_b_w, w_branch_a=w_branch_a, w_branch_b=w_branch_b, w_branch_c=w_branch_c, w_out=w_out, norm_mem_pre=norm_mem_pre, norm_mem_post=norm_mem_post, norm_memkv=norm_memkv, w_mq=w_mq, w_mkv=w_mkv, w_mo=w_mo, norm_ffn_pre=norm_ffn_pre, norm_ffn_post=norm_ffn_post, w_up=w_up, conv_ffn_w=conv_ffn_w, w_down=w_down)
    m = dict(norm_mix_pre=m_norm_mix_pre, norm_mix_post=m_norm_mix_post, w_in=m_w_in, pool_w=m_pool_w, pool_scale=m_pool_scale, conv_b_w=m_conv_b_w, w_branch_a=m_w_branch_a, w_branch_b=m_w_branch_b, w_branch_c=m_w_branch_c, w_out=m_w_out, norm_mem_pre=m_norm_mem_pre, norm_mem_post=m_norm_mem_post, norm_memkv=m_norm_memkv, w_mq=m_w_mq, w_mkv=m_w_mkv, w_mo=m_w_mo, norm_ffn_pre=m_norm_ffn_pre, norm_ffn_post=m_norm_ffn_post, w_up=m_w_up, conv_ffn_w=m_conv_ffn_w, w_down=m_w_down)
    v = dict(norm_mix_pre=v_norm_mix_pre, norm_mix_post=v_norm_mix_post, w_in=v_w_in, pool_w=v_pool_w, pool_scale=v_pool_scale, conv_b_w=v_conv_b_w, w_branch_a=v_w_branch_a, w_branch_b=v_w_branch_b, w_branch_c=v_w_branch_c, w_out=v_w_out, norm_mem_pre=v_norm_mem_pre, norm_mem_post=v_norm_mem_post, norm_memkv=v_norm_memkv, w_mq=v_w_mq, w_mkv=v_w_mkv, w_mo=v_w_mo, norm_ffn_pre=v_norm_ffn_pre, norm_ffn_post=v_norm_ffn_post, w_up=v_w_up, conv_ffn_w=v_conv_ffn_w, w_down=v_w_down)

    mix_big = [n for n in BIG if n not in LATE_BIG]
    block = lambda names, l: [_as_rows(w[n][l], n).astype(MXU) for n in names]
    conv = jnp.concatenate([_pad_flat(w[n], _seg(w[n].size)) for n in F32_GATHERED]).reshape(-1, LANES)
    groups = dict(m=MERGE_BIG, b=LATE_BIG, a=mix_big)
    got0 = _all_gather("weights_all_gather_0", block(['w_in'], 0) + [conv])
    conv_all = got0[-1].reshape(N_DEV, -1)
    small, off = {n: w[n] for n in WEIGHTS if n not in SHARD_AXIS}, 0
    for n in F32_GATHERED:
        small[n] = _from_blocks(conv_all[:, off:off + w[n].size].reshape((N_DEV,) + w[n].shape), 2)
        off += _seg(w[n].size)
    whole = lambda names, got: {n: o.reshape(-1, o.shape[-1]) for n, o in zip(names, got)}
    pushes, after = {}, got0[0]
    for tag, l in (('m', 0), ('b', 0), ('a', 1)):
        pushes[tag, l] = _push_start(f"weights_push_start_{l}{tag}", block(groups[tag], l), False, after)
        after = pushes[tag, l][4]

    def arrived(tag, l, done):
        return _big_operands(whole(groups[tag], _push_wait(f"weights_push_wait_{l}{tag}", pushes[tag, l], False, done)))

    ctab, stab = _rope_tables(positions[0])
    W0 = _layer_weights(whole(['w_in'], got0), small, 0)
    sv0 = _layer_fwd_branches(x[0], dict(W0, g_mix_pre=W0['g_mix_pre'] + after[0, 0]), ctab, stab)
    W0.update(arrived('m', 0, sv0['att']))
    sv0 = _layer_fwd_merge(W0, sv0)
    W0.update(arrived('b', 0, sv0['x1']))
    x1 = _layer_fwd_late(mem[0], W0, sv0)
    pushes['b', 1] = _push_start("weights_push_start_1b", block(groups['b'], 1), False, x1)
    W1 = _layer_weights({}, small, 1)
    W1.update(arrived('a', 1, pushes['b', 1][4]))
    sv1 = _layer_fwd_mix(x1, W1, ctab, stab)
    W1.update(arrived('b', 1, sv1['x1']))
    x2 = _layer_fwd_late(mem[0], W1, sv1)
    dx, acc = _loss_head(x2, loss_target[0])
    loss = lax.psum(jnp.sum(acc) * (0.5 / D), MESH_AXES)
    grads = [None] * DEPTH
    dx, grads[1] = _layer_bwd(dx, mem[0], W1, sv1, ctab, stab)
    sent = [None, [grads[1][n].reshape(N_DEV, -1, grads[1][n].shape[-1]) for n in BIG]]
    push_g = _push_start("grads_push_start_1", sent[1], True, dx)
    dx, g_late = _layer_bwd_late(dx, mem[0], dict(W0, g_ffn_post=W0['g_ffn_post'] + push_g[4][0, 0]), sv0)
    sent_late = [g_late[n].reshape(N_DEV, -1, g_late[n].shape[-1]) for n in LATE_BIG]
    push_l = _push_start("grads_push_start_0", sent_late, True, dx)
    parts, g_mix = _layer_bwd_merge(dx, dict(W0, g_mix_post=W0['g_mix_post'] + push_l[4][0, 0]), sv0)
    sent_merge = [g_mix[n].reshape(N_DEV, -1, g_mix[n].shape[-1]) for n in MERGE_BIG]
    push_m = _push_start("grads_push_start_0m", sent_merge, True, parts[0])
    du, g_br = _layer_bwd_branches(parts, dict(W0, pool_scale=W0['pool_scale'] + push_m[4][0, 0]), sv0, ctab, stab)
    g_mix.update(g_br)
    g_mix['w_in'] = _dw_in(du, sv0)
    sent_in = [g_mix['w_in'].reshape(N_DEV, -1, D)]
    push_i = _push_start("grads_push_start_in", sent_in, True, du)
    dx, g_mix['norm_mix_pre'] = _mix_pre_bwd(dx, du, dict(W0, g_mix_pre=W0['g_mix_pre'] + push_i[4][0, 0]), sv0)
    grads[0] = {**g_late, **g_mix}
    recv1 = _push_wait("grads_push_wait_1", push_g, True, dx)
    recv_late = _push_wait("grads_push_wait_0", push_l, True, dx)
    recv_merge = _push_wait("grads_push_wait_0m", push_m, True, dx)

    misc_names = [n for n in WEIGHTS if n not in BIG]
    stacked = {n: jnp.stack([grads[l][n].reshape(small[n].shape[1:]) for l in range(DEPTH)]) for n in misc_names}
    rows = [(_to_blocks(stacked[n], 2) if n in SHARD_AXIS else jnp.broadcast_to(stacked[n][None], (N_DEV,) + stacked[n].shape))
            for n in misc_names]
    segs = [_seg(w[n].size) for n in misc_names]
    misc = jnp.concatenate([jnp.pad(r.reshape(N_DEV, -1), ((0, 0), (0, s - r[0].size))) for r, s in zip(rows, segs)],
                           axis=1).reshape(N_DEV, -1, LANES)
    push_x = _push_start("grads_push_start_small", [misc], True, dx)
    g_out, shares = {}, {}
    for l, names, recv in ((1, BIG, recv1), (0, LATE_BIG, recv_late), (0, MERGE_BIG, recv_merge)):
        for n, r in zip(names, recv):
            shares[n, l] = r

    swap = lambda a: jnp.swapaxes(a, 1, 2)

    def update(n):
        if n not in BIG:
            return [g_out[n], *_adamw(f"adamw_{n}", g_out[n], w[n], m[n], v[n])]
        of_layers = [shares[n, l] for l in range(DEPTH)]
        if SHARD_AXIS[n] == 1:
            return _sum_adamw(f"adamw_{n}", of_layers, w[n], m[n], v[n])
        if w[n].shape[2] % LANES:
            return [swap(a) for a in _sum_adamw(f"adamw_{n}", of_layers, swap(w[n]), swap(m[n]), swap(v[n]))]
        g = swap(jnp.stack([_sum_slots(f"sum_{n}_{l}", s) for l, s in enumerate(of_layers)]))
        return [g, *_adamw(f"adamw_{n}", g, w[n], m[n], v[n])]

    done = {n: update(n) for n in BIG if n != 'w_in'}
    shares['w_in', 0] = _push_wait("grads_push_wait_in", push_i, True, done[BIG[-1]][1])[0]
    done['w_in'] = update('w_in')
    after_w_in = swap(done['w_in'][1])
    misc_sum = _sum_slots("sum_misc", _push_wait("grads_push_wait_small", push_x, True, after_w_in)[0]).reshape(-1)
    off = 0
    for n, s in zip(misc_names, segs):
        g_out[n] = misc_sum[off:off + w[n].size].reshape(w[n].shape)
        done[n] = update(n)
        off += s
    return (loss, dx[None], *[done[n][k] for k in range(4) for n in WEIGHTS])
```

```python
import jax
import jax.numpy as jnp
from jax import lax
from jax.experimental import pallas as pl
from jax.experimental.pallas import tpu as pltpu

F32 = jnp.float32
MXU = jnp.bfloat16

D = 1024
DEPTH = 2
POOLW = 384
ATT_W = 768
ATT_O = 256
GATE_W = 3 * D
IN_W = 6912
IN_TILE = 768
IN_ROT = (IN_W - GATE_W) // IN_TILE
MEM_W = 512
D_FF = 2816
EPS = 1e-6
ROPE_THETA = 500000.0
QB = 128
DILS = (1, 4, 16)
NEG = -1e30
MEM_SCALE = 128 ** -0.5
ATT_SCALE = 0.125

ADAM_LR, ADAM_B1, ADAM_B2, ADAM_EPS, ADAM_WD, ADAM_STEP = 0.001, 0.9, 0.999, 1e-08, 0.01, 10

N_DEV = 8
MESH_AXES = ("x", "y", "c")
LANES = 128
FLAT_ALIGN = 2048
UPDATE_BLOCK_BYTES = 1 << 20

WEIGHTS = ['norm_mix_pre', 'norm_mix_post', 'w_in', 'pool_w', 'pool_scale', 'conv_b_w', 'w_branch_a', 'w_branch_b',
           'w_branch_c', 'w_out', 'norm_mem_pre', 'norm_mem_post', 'norm_memkv', 'w_mq', 'w_mkv', 'w_mo',
           'norm_ffn_pre', 'norm_ffn_post', 'w_up', 'conv_ffn_w', 'w_down']
SHARD_AXIS = {'w_in': 2, 'conv_b_w': 2, 'w_branch_a': 2, 'w_branch_b': 2, 'w_branch_c': 2, 'w_out': 1, 'w_mq': 1,
              'w_mkv': 1, 'w_mo': 2, 'w_up': 2, 'conv_ffn_w': 2, 'w_down': 1}
F32_GATHERED = ('conv_b_w', 'conv_ffn_w')
BIG = [n for n in WEIGHTS if n in SHARD_AXIS and n not in F32_GATHERED]
LATE_BIG = ['w_mq', 'w_mkv', 'w_mo', 'w_up', 'w_down']
MERGE_BIG = ['w_branch_a', 'w_branch_b', 'w_branch_c', 'w_out']


VMEM_LIMIT_MB = 60


def _params(sem):
    return pltpu.CompilerParams(dimension_semantics=sem, vmem_limit_bytes=VMEM_LIMIT_MB << 20)


def _dot(a, b, prec=None):
    return lax.dot_general(a, b, (((1,), (0,)), ((), ())), preferred_element_type=F32, precision=prec)


def _dot_nt(a, b, prec=None):
    return lax.dot_general(a, b, (((1,), (1,)), ((), ())), preferred_element_type=F32, precision=prec)


def _dot_tn(a, b, prec=None):
    return lax.dot_general(a, b, (((0,), (0,)), ((), ())), preferred_element_type=F32, precision=prec)


def _tile(n, cap):
    if n <= cap:
        return n
    best = None
    for t in range(LANES, cap + 1, LANES):
        if n % t == 0:
            best = t
    assert best is not None, (n, cap)
    return best


def _rms(x, g):
    r = lax.rsqrt(jnp.mean(x * x, axis=-1, keepdims=True) + EPS)
    return x * r * g, r


def _rms_bwd(w, y):
    r = lax.rsqrt(jnp.mean(y * y, axis=-1, keepdims=True) + EPS)
    return r * w - y * (r * r * r) * jnp.mean(w * y, axis=-1, keepdims=True), r


def _rows_call(name, body, n_rows, ts, ins, outs, scratch=(), reverse=False, aliases=None):
    nt = n_rows // ts
    assert nt * ts == n_rows

    def tile_of(g):
        return (nt - 1 - g) if reverse else g

    in_specs, args = [], []
    for op in ins:
        if op[0] == "t":
            _, a, cw, cb = op
            in_specs.append(pl.BlockSpec((ts, cw), lambda g, cb=cb: (tile_of(g), cb)))
        elif op[0] == "h":
            _, a, hr, cw, cb = op
            in_specs.append(pl.BlockSpec((hr, cw), lambda g, cb=cb, k=ts // hr: (jnp.maximum(tile_of(g) * k - 1, 0), cb)))
        elif op[0] == "x":
            _, a = op
            in_specs.append(pl.BlockSpec(memory_space=pl.ANY))
        else:
            _, a = op
            in_specs.append(pl.BlockSpec(a.shape, lambda g, n=a.ndim: (0,) * n))
        args.append(a)
    out_specs, out_shape = [], []
    for op in outs:
        if op[0] == "t":
            _, cols, dt = op
            out_specs.append(pl.BlockSpec((ts, cols), lambda g: (tile_of(g), 0)))
            out_shape.append(jax.ShapeDtypeStruct((n_rows, cols), dt))
        elif op[0] == "c":
            _, total, cols, cb, dt = op
            out_specs.append(pl.BlockSpec((ts, cols), lambda g, cb=cb: (tile_of(g), cb)))
            out_shape.append(jax.ShapeDtypeStruct((n_rows, total), dt))
        else:
            _, shp, dt = op
            out_specs.append(pl.BlockSpec(shp, lambda g, n=len(shp): (0,) * n))
            out_shape.append(jax.ShapeDtypeStruct(shp, dt))

    def kern(*refs):
        g = pl.program_id(0)
        body(tile_of(g), g, *refs)

    return pl.pallas_call(kern, grid=(nt,), in_specs=in_specs, out_specs=out_specs, out_shape=out_shape,
                          scratch_shapes=list(scratch), input_output_aliases=aliases or {},
                          compiler_params=_params(("arbitrary",)), name=name)(*args)


def _acc(ref, g, val):
    @pl.when(g == 0)
    def _():
        ref[...] = val

    @pl.when(g != 0)
    def _():
        ref[...] += val


def _norm_mm(name, x, g, w, ts, tn, out_dtype=F32, wt=False, rot=0):
    S, K = x.shape
    N = w.shape[0] if wt else w.shape[1]
    assert wt or not rot

    def body(x_ref, g_ref, w_ref, o_ref, h_ref, hs):
        @pl.when(pl.program_id(1) == 0)
        def _():
            h, _ = _rms(x_ref[...], g_ref[...])
            hs[...] = h.astype(MXU)
            h_ref[...] = h.astype(MXU)

        o_ref[...] = (_dot_nt if wt else _dot)(hs[...], w_ref[...]).astype(out_dtype)

    w_spec = pl.BlockSpec((tn, K), lambda i, j: ((j + rot) % (N // tn), 0)) if wt else pl.BlockSpec((K, tn), lambda i, j: (0, j))
    return pl.pallas_call(
        body, grid=(S // ts, N // tn),
        in_specs=[pl.BlockSpec((ts, K), lambda i, j: (i, 0)), pl.BlockSpec((1, K), lambda i, j: (0, 0)), w_spec],
        out_specs=[pl.BlockSpec((ts, tn), lambda i, j: (i, j)), pl.BlockSpec((ts, K), lambda i, j: (i, 0))],
        out_shape=[jax.ShapeDtypeStruct((S, N), out_dtype), jax.ShapeDtypeStruct((S, K), MXU)],
        scratch_shapes=[pltpu.VMEM((ts, K), MXU)],
        compiler_params=_params(("arbitrary", "arbitrary")), name=name)(x, g, w)


def _mm_nt(name, a, b, ts, tn, out_dtype=F32):
    M, K = a.shape
    N = b.shape[0]

    def body(a_ref, b_ref, o_ref):
        o_ref[...] = _dot_nt(a_ref[...], b_ref[...]).astype(out_dtype)

    return pl.pallas_call(
        body, grid=(M // ts, N // tn),
        in_specs=[pl.BlockSpec((ts, K), lambda i, j: (i, 0)), pl.BlockSpec((tn, K), lambda i, j: (j, 0))],
        out_specs=pl.BlockSpec((ts, tn), lambda i, j: (i, j)), out_shape=jax.ShapeDtypeStruct((M, N), out_dtype),
        compiler_params=_params(("arbitrary", "arbitrary")), name=name)(a, b)


def _mm_tn(name, a, b, cap_k=512, cap_n=1024, out_dtype=MXU, rot=0):
    S, K = a.shape
    N = b.shape[1]
    tk, tn = _tile(K, cap_k), _tile(N, cap_n)

    def body(a_ref, b_ref, o_ref):
        o_ref[...] = _dot_tn(a_ref[...], b_ref[...]).astype(out_dtype)

    return pl.pallas_call(
        body, grid=(K // tk, N // tn),
        in_specs=[pl.BlockSpec((S, tk), lambda i, j: (0, i)), pl.BlockSpec((S, tn), lambda i, j: (0, j))],
        out_specs=pl.BlockSpec((tk, tn), lambda i, j: ((i + rot) % (K // tk), j)), out_shape=jax.ShapeDtypeStruct((K, N), out_dtype),
        compiler_params=_params(("arbitrary", "arbitrary")), name=name)(a, b)


def _pool_cols(shape):
    col = lax.broadcasted_iota(jnp.int32, shape, 1)
    return col < 96, col < 192, col < 288


def _pool_select(s2, s4, s8, s16):
    c1, c2, c3 = _pool_cols(s2.shape)
    return jnp.where(c1, s2, jnp.where(c2, s4, jnp.where(c3, s8, s16)))


def _pool_cnt(t0, ts):
    c1, c2, c3 = _pool_cols((ts, POOLW))
    win = jnp.where(c1, 2, jnp.where(c2, 4, jnp.where(c3, 8, 16)))
    t = t0 + lax.broadcasted_iota(jnp.int32, (ts, POOLW), 0)
    return jnp.minimum(t + 1, win).astype(F32)


def _pooled(a, prev, t0):
    ts = a.shape[0]
    ext = jnp.concatenate([prev, a], axis=0)
    s2 = ext + pltpu.roll(ext, 1, axis=0)
    s4 = s2 + pltpu.roll(s2, 2, axis=0)
    s8 = s4 + pltpu.roll(s4, 4, axis=0)
    s16 = s8 + pltpu.roll(s8, 8, axis=0)
    sums = _pool_select(s2, s4, s8, s16)[16:]
    return sums / _pool_cnt(t0, ts) - a


def _conv3(z, prev8, w):
    ext = jnp.concatenate([prev8, z], axis=0)
    z1 = pltpu.roll(ext, 1, axis=0)[8:]
    z2 = pltpu.roll(ext, 2, axis=0)[8:]
    return w[0:1] * z2 + w[1:2] * z1 + w[2:3] * z, z1, z2


def _conv3_t(dc, next8, w, shifted=False):
    ts = dc.shape[0]
    ext = jnp.concatenate([dc, next8], axis=0)
    n = ts + 8
    u1 = pltpu.roll(ext, n - 1, axis=0)[:ts]
    u2 = pltpu.roll(ext, n - 2, axis=0)[:ts]
    out = w[2:3] * dc + w[1:2] * u1 + w[0:1] * u2
    return (out, u1, u2) if shifted else out


def _poolconv_fwd(u, wblk, pool_scale, conv_b, ts=512):
    S = u.shape[0]

    def body(i, g, a_ref, bx_ref, bb_ref, bc_ref, wblk_ref, ps_ref, cw_ref, a2_ref, yb_ref, ca, cz):
        @pl.when(g == 0)
        def _():
            ca[...] = jnp.zeros_like(ca)
            cz[...] = jnp.zeros_like(cz)

        a = a_ref[...].astype(F32)
        p = _pooled(a, ca[...], i * ts)
        mixed = _dot(p.astype(MXU), wblk_ref[...])
        a2_ref[...] = (mixed * ps_ref[...]).astype(MXU)
        z = bc_ref[...].astype(F32) * bx_ref[...].astype(F32)
        conv, _, _ = _conv3(z, cz[...], cw_ref[...])
        yb_ref[...] = (bb_ref[...].astype(F32) * conv).astype(MXU)
        ca[...] = a[ts - 16:]
        cz[...] = z[ts - 8:]

    ins = [("t", u, POOLW, 8), ("t", u, POOLW, 9), ("t", u, POOLW, 10), ("t", u, POOLW, 11), ("w", wblk), ("w", pool_scale),
           ("w", conv_b)]
    return _rows_call("poolconv_fwd", body, S, ts, ins, [("t", POOLW, MXU), ("t", POOLW, MXU)],
                      scratch=[pltpu.VMEM((16, POOLW), F32), pltpu.VMEM((8, POOLW), F32)])


def _poolconv_bwd(u, d_a2, d_yb, du, wblk, pool_scale, conv_b, ts=512):
    S = u.shape[0]

    def body(i, g, a_ref, bx_ref, bb_ref, bc_ref, ap_ref, bxp_ref, bcp_ref, da2_ref, dyb_ref, wblk_ref, ps_ref, cw_ref, _,
             o_ref, dps_ref, dwb_ref, dcw_ref, ce, cdz):
        @pl.when(g == 0)
        def _():
            ce[...] = jnp.zeros_like(ce)
            cdz[...] = jnp.zeros_like(cdz)

        first = (i > 0).astype(F32)
        a = a_ref[...].astype(F32)
        p = _pooled(a, ap_ref[...].astype(F32) * first, i * ts)
        pb = p.astype(MXU)
        mixed = _dot(pb, wblk_ref[...])
        da2 = da2_ref[...]
        dmixed = (da2 * ps_ref[...]).astype(MXU)
        dp = _dot_nt(dmixed, wblk_ref[...])
        _acc(dps_ref, g, jnp.sum(da2 * mixed, axis=0, keepdims=True))
        _acc(dwb_ref, g, _dot_tn(pb, dmixed))
        e = dp / _pool_cnt(i * ts, ts)
        ext = jnp.concatenate([e, ce[...]], axis=0)
        n = ts + 16
        f2 = ext + pltpu.roll(ext, n - 1, axis=0)
        f4 = f2 + pltpu.roll(f2, n - 2, axis=0)
        f8 = f4 + pltpu.roll(f4, n - 4, axis=0)
        f16 = f8 + pltpu.roll(f8, n - 8, axis=0)
        o_ref[:, 0:POOLW] = (_pool_select(f2, f4, f8, f16)[:ts] - dp).astype(o_ref.dtype)
        ce[...] = e[:16]

        bx, bb, bc = bx_ref[...].astype(F32), bb_ref[...].astype(F32), bc_ref[...].astype(F32)
        z = bc * bx
        w = cw_ref[...]
        conv, z1, z2 = _conv3(z, (bxp_ref[...].astype(F32) * bcp_ref[...].astype(F32))[8:16] * first, w)
        dyb = dyb_ref[...]
        dconv = dyb * bb
        dz = _conv3_t(dconv, cdz[...], w)
        o_ref[:, POOLW:2 * POOLW] = (dz * bc).astype(o_ref.dtype)
        o_ref[:, 2 * POOLW:3 * POOLW] = (dyb * conv).astype(o_ref.dtype)
        o_ref[:, 3 * POOLW:4 * POOLW] = (dz * bx).astype(o_ref.dtype)
        dw = jnp.concatenate([jnp.sum(dconv * z2, axis=0, keepdims=True), jnp.sum(dconv * z1, axis=0, keepdims=True),
                              jnp.sum(dconv * z, axis=0, keepdims=True)], axis=0)
        _acc(dcw_ref, g, dw)
        cdz[...] = dconv[:8]

    ins = [("t", u, POOLW, 8), ("t", u, POOLW, 9), ("t", u, POOLW, 10), ("t", u, POOLW, 11),
           ("h", u, 16, POOLW, 8), ("h", u, 16, POOLW, 9), ("h", u, 16, POOLW, 11),
           ("t", d_a2, POOLW, 0), ("t", d_yb, POOLW, 0), ("w", wblk), ("w", pool_scale), ("w", conv_b), ("x", du)]
    outs = [("c", IN_W, 4 * POOLW, GATE_W // (4 * POOLW), MXU), ("a", (1, POOLW), F32), ("a", (POOLW, POOLW), F32), ("a", (3, POOLW), F32)]
    return _rows_call("poolconv_bwd", body, S, ts, ins, outs, aliases={len(ins) - 1: 0},
                      scratch=[pltpu.VMEM((16, POOLW), F32), pltpu.VMEM((8, POOLW), F32)], reverse=True)


def _rope_tables(positions):
    inv = ROPE_THETA ** (-jnp.arange(0, 16, 2, dtype=F32) / 16)
    lane_inv = jnp.tile(jnp.concatenate([inv, inv, jnp.zeros(48, F32)]), 2)
    lane_sign = jnp.tile(jnp.concatenate([-jnp.ones(8, F32), jnp.ones(8, F32), jnp.zeros(48, F32)]), 2)
    ang = positions.astype(F32)[:, None] * lane_inv
    return jnp.where(lane_sign == 0, 1.0, jnp.cos(ang)), jnp.sin(ang) * lane_sign


def _partner(x):
    lane = lax.broadcasted_iota(jnp.int32, x.shape, 1) % 64
    return jnp.where(lane < 8, pltpu.roll(x, LANES - 8, axis=1), jnp.where(lane < 16, pltpu.roll(x, 8, axis=1), 0.0))


def _rope(x, c, s):
    return x * c + _partner(x) * s


def _rope_t(x, c, s):
    return x * c + _partner(x * s)


def _rows_of(r, n, d):
    return pl.ds(r, n, stride=d) if d > 1 else pl.ds(0, n)


def _head_masks(shape):
    lane = lax.broadcasted_iota(jnp.int32, shape, 1) // 64
    return [lane == h for h in range(4)]


def _only(mask, x):
    return jnp.where(mask, x, jnp.zeros_like(x))


def _rope_perm(u, ctab, stab, ts=512):
    S = u.shape[0]
    nch = ATT_W // LANES

    def body(*refs):
        chunks, (c_ref, s_ref), outs, scr = refs[:3 * nch], refs[3 * nch:3 * nch + 2], refs[3 * nch + 2:-1], refs[-1]
        for k in range(3 * nch):
            scr[k] = chunks[k][...].astype(F32)
        for g, d in enumerate(DILS):
            n = ts // d
            for r in range(d):
                rows = _rows_of(r, n, d)
                c, s = c_ref[rows, :], s_ref[rows, :]
                for which in range(3):
                    parts = [scr.at[which * nch + j][rows, :] for j in (2 * g, 2 * g + 1)]
                    if which < 2:
                        parts = [_rope(x, c, s) for x in parts]
                    outs[which * 3 + g][r] = jnp.concatenate(parts, axis=1).astype(MXU)

    base = (IN_W - 3 * ATT_W) // LANES
    in_specs = [pl.BlockSpec((ts, LANES), lambda i, cb=base + k: (i, cb)) for k in range(3 * nch)]
    in_specs += [pl.BlockSpec((ts, LANES), lambda i: (i, 0))] * 2
    out_specs = [pl.BlockSpec((d, ts // d, ATT_O), lambda i: (0, i, 0)) for _ in range(3) for d in DILS]
    out_shape = [jax.ShapeDtypeStruct((d, S // d, ATT_O), MXU) for _ in range(3) for d in DILS]
    res = pl.pallas_call(body, grid=(S // ts,), in_specs=in_specs, out_specs=out_specs, out_shape=out_shape,
                         scratch_shapes=[pltpu.VMEM((3 * nch, ts, LANES), F32)],
                         compiler_params=_params(("arbitrary",)), name="rope_perm")(*([u] * (3 * nch)), ctab, stab)
    return [[res[which * 3 + g].reshape(S, ATT_O) for g in range(3)] for which in range(3)]


def _rope_unperm_bwd(dqkv, du, ctab, stab, ts=512):
    S = dqkv[0][0].shape[0]
    nch = ATT_W // LANES

    def body(*refs):
        ins, (c_ref, s_ref, _, o_ref, scr) = refs[:9], refs[9:]
        for g, d in enumerate(DILS):
            n = ts // d
            for r in range(d):
                rows = _rows_of(r, n, d)
                c, s = c_ref[rows, :], s_ref[rows, :]
                for which in range(3):
                    v = ins[which * 3 + g][r]
                    for half in range(2):
                        x = v[:, half * LANES:(half + 1) * LANES]
                        scr.at[which * nch + 2 * g + half][rows, :] = _rope_t(x, c, s) if which < 2 else x
        for j in range(3 * nch):
            o_ref[:, j * LANES:(j + 1) * LANES] = scr[j].astype(o_ref.dtype)

    in_specs = [pl.BlockSpec((d, ts // d, ATT_O), lambda i: (0, i, 0)) for _ in range(3) for d in DILS]
    in_specs += [pl.BlockSpec((ts, LANES), lambda i: (i, 0))] * 2 + [pl.BlockSpec(memory_space=pl.ANY)]
    args = [dqkv[which][g].reshape(d, S // d, ATT_O) for which in range(3) for g, d in enumerate(DILS)]
    last = (IN_W - 3 * ATT_W) // (3 * ATT_W)
    return pl.pallas_call(body, grid=(S // ts,), in_specs=in_specs, out_specs=pl.BlockSpec((ts, 3 * ATT_W), lambda i: (i, last)),
                          out_shape=jax.ShapeDtypeStruct((S, IN_W), MXU), scratch_shapes=[pltpu.VMEM((3 * nch, ts, LANES), F32)],
                          input_output_aliases={len(in_specs) - 1: 0},
                          compiler_params=_params(("arbitrary",)), name="rope_unperm_bwd")(*args, ctab, stab, du)


def _band_mask_keys(has_prev):
    r = lax.broadcasted_iota(jnp.int32, (QB, 2 * QB), 0)
    c = lax.broadcasted_iota(jnp.int32, (QB, 2 * QB), 1)
    return ((c < QB) & (c >= r) & has_prev) | ((c >= QB) & (c - QB <= r))


def _band_mask_queries(has_next):
    r = lax.broadcasted_iota(jnp.int32, (2 * QB, 2 * QB), 0)
    c = lax.broadcasted_iota(jnp.int32, (2 * QB, 2 * QB), 1) & (QB - 1)
    return ((r < QB) & (c <= r)) | ((r >= QB) & (c >= r - QB) & has_next)


ASUB = 4
_BIG = pl.BlockSpec((ASUB * QB, ATT_O), lambda b: (b, 0))
_PREV = pl.BlockSpec((QB, ATT_O), lambda b: (jnp.maximum(b * ASUB - 1, 0), 0))


def _sub(ref, j):
    return ref[j * QB:(j + 1) * QB]


def _attn_fwd(g, q, k, v):
    S = q.shape[0]
    nb = S // QB
    nblk = nb // DILS[g]

    def body(q_ref, kc_ref, kp_ref, vc_ref, vp_ref, o_ref, m_ref, l_ref):
        hm_kv, hm_o = _head_masks((2 * QB, ATT_O)), _head_masks((QB, ATT_O))
        for j in range(ASUB):
            ok = _band_mask_keys(((pl.program_id(0) * ASUB + j) & (nblk - 1)) > 0)
            k2 = jnp.concatenate([kp_ref[...] if j == 0 else _sub(kc_ref, j - 1), _sub(kc_ref, j)], axis=0)
            v2 = jnp.concatenate([vp_ref[...] if j == 0 else _sub(vc_ref, j - 1), _sub(vc_ref, j)], axis=0)
            qv = _sub(q_ref, j)
            o_acc = jnp.zeros((QB, ATT_O), F32)
            m_acc = jnp.zeros((QB, ATT_O), F32)
            l_acc = jnp.zeros((QB, ATT_O), F32)
            for h in range(4):
                s = jnp.where(ok, _dot_nt(qv, _only(hm_kv[h], k2)) * ATT_SCALE, NEG)
                m = jnp.max(s, axis=1, keepdims=True)
                p = jnp.exp(s - m)
                o_acc = o_acc + _dot(p.astype(MXU), _only(hm_kv[h], v2))
                m_acc = jnp.where(hm_o[h], m, m_acc)
                l_acc = jnp.where(hm_o[h], jnp.sum(p, axis=1, keepdims=True), l_acc)
            o_ref[j * QB:(j + 1) * QB] = o_acc
            m_ref[j * QB:(j + 1) * QB] = m_acc
            l_ref[j * QB:(j + 1) * QB] = l_acc

    shp = jax.ShapeDtypeStruct((S, ATT_O), F32)
    return pl.pallas_call(body, grid=(nb // ASUB,), in_specs=[_BIG, _BIG, _PREV, _BIG, _PREV],
                          out_specs=[_BIG] * 3, out_shape=[shp, shp, shp], compiler_params=_params(("arbitrary",)),
                          name=f"attn_fwd_{g}")(q, k, k, v, v)


def _natural(ref, d, scr, ts):
    if d == 1:
        return ref[0]
    n = ts // d
    for r in range(d):
        v = ref[r]
        scr.at[0][pl.ds(r, n, stride=d), :] = v[:, 0:LANES]
        scr.at[1][pl.ds(r, n, stride=d), :] = v[:, LANES:2 * LANES]
    return jnp.concatenate([scr[0], scr[1]], axis=1)


def _attn_combine(oml, ts=512):
    S = oml[0][0].shape[0]

    def body(*refs):
        ins, (att_ref, out_ref, lse_ref, scr) = refs[:9], refs[9:]
        o, m, l = [[_natural(ins[3 * g + k], d, scr, ts) for g, d in enumerate(DILS)] for k in range(3)]
        mx = jnp.maximum(jnp.maximum(m[0], m[1]), m[2])
        w = [jnp.exp(m[g] - mx) for g in range(3)]
        den = w[0] * l[0] + w[1] * l[1] + w[2] * l[2]
        out = (w[0] * o[0] + w[1] * o[1] + w[2] * o[2]) / den
        out_ref[...] = out
        att_ref[...] = out.astype(MXU)
        lse_ref[...] = mx + jnp.log(den)

    in_specs = [pl.BlockSpec((d, ts // d, ATT_O), lambda i: (0, i, 0)) for d in DILS for _ in range(3)]
    args = [a.reshape(d, S // d, ATT_O) for d, grp in zip(DILS, oml) for a in grp]
    blk = pl.BlockSpec((ts, ATT_O), lambda i: (i, 0))
    return pl.pallas_call(body, grid=(S // ts,), in_specs=in_specs, out_specs=[blk, blk, blk],
                          out_shape=[jax.ShapeDtypeStruct((S, ATT_O), MXU), jax.ShapeDtypeStruct((S, ATT_O), F32),
                                     jax.ShapeDtypeStruct((S, ATT_O), F32)],
                          scratch_shapes=[pltpu.VMEM((2, ts, LANES), F32)], compiler_params=_params(("arbitrary",)),
                          name="attn_combine")(*args)


def _attn_bwd_prep(datt, o, lse, ts=512):
    S = datt.shape[0]

    def body(da0, da1, o_ref, l0, l1, *rest):
        outs, dl = rest[:9], rest[9]
        prod = jnp.concatenate([da0[...], da1[...]], axis=1) * o_ref[...]
        delta = jnp.zeros((ts, ATT_O), F32)
        for hm in _head_masks((ts, ATT_O)):
            delta = jnp.where(hm, jnp.sum(_only(hm, prod), axis=1, keepdims=True), delta)
        dl[0] = delta[:, 0:LANES]
        dl[1] = delta[:, LANES:2 * LANES]
        for g, d in enumerate(DILS):
            n = ts // d
            for r in range(d):
                rows = _rows_of(r, n, d)
                outs[g][r] = jnp.concatenate([da0[rows, :], da1[rows, :]], axis=1).astype(MXU)
                outs[3 + g][r] = jnp.concatenate([dl.at[0][rows, :], dl.at[1][rows, :]], axis=1)
                outs[6 + g][r] = jnp.concatenate([l0[rows, :], l1[rows, :]], axis=1)

    half = lambda j: pl.BlockSpec((ts, LANES), lambda i: (i, j))
    out_specs = [pl.BlockSpec((d, ts // d, ATT_O), lambda i: (0, i, 0)) for _ in range(3) for d in DILS]
    out_shape = [jax.ShapeDtypeStruct((d, S // d, ATT_O), dt) for dt in (MXU, F32, F32) for d in DILS]
    res = pl.pallas_call(body, grid=(S // ts,), in_specs=[half(0), half(1), pl.BlockSpec((ts, ATT_O), lambda i: (i, 0)), half(0), half(1)],
                         out_specs=out_specs, out_shape=out_shape, scratch_shapes=[pltpu.VMEM((2, ts, LANES), F32)],
                         compiler_params=_params(("arbitrary",)), name="attn_bwd_prep")(datt, datt, o, lse, lse)
    return [[res[k * 3 + g].reshape(S, ATT_O) for g in range(3)] for k in range(3)]


def _head_col(x, h):
    return x[:, h * 64:h * 64 + 1]


def _attn_bwd(g, q, k, v, do, delta, lse):
    S = q.shape[0]
    nb = S // QB
    nblk = nb // DILS[g]

    def body(k_ref, v_ref, qc_ref, qn_ref, doc_ref, don_ref, dlc_ref, dln_ref, lc_ref, ln_ref, dq_ref, dk_ref, dv_ref, dq_scr):
        hms, hmk = _head_masks((2 * QB, ATT_O)), _head_masks((QB, ATT_O))
        first_head = lax.broadcasted_iota(jnp.int32, (2 * QB, 2 * QB), 1) < QB
        first = pl.program_id(0) == 0

        @pl.when(first)
        def _():
            dq_scr[0:QB] = jnp.zeros((QB, ATT_O), F32)

        @pl.when(jnp.logical_not(first))
        def _():
            dq_scr[0:QB] = dq_scr[ASUB * QB:(ASUB + 1) * QB]

        dq_scr[QB:(ASUB + 1) * QB] = jnp.zeros((ASUB * QB, ATT_O), F32)

        def both(cur_ref, nxt_ref, j):
            return jnp.concatenate([_sub(cur_ref, j), nxt_ref[...] if j == ASUB - 1 else _sub(cur_ref, j + 1)], axis=0)

        for j in range(ASUB):
            ok = _band_mask_queries(((pl.program_id(0) * ASUB + j + 1) & (nblk - 1)) > 0)
            q2, do2, dl2, lse2 = both(qc_ref, qn_ref, j), both(doc_ref, don_ref, j), both(dlc_ref, dln_ref, j), both(lc_ref, ln_ref, j)
            kv, vv = _sub(k_ref, j), _sub(v_ref, j)
            dk = jnp.zeros((QB, ATT_O), F32)
            dv = jnp.zeros((QB, ATT_O), F32)
            dq2 = jnp.zeros((2 * QB, ATT_O), F32)
            for h in range(0, 4, 2):
                both_heads = hms[h] | hms[h + 1]
                qp, dop = _only(both_heads, q2), _only(both_heads, do2)
                kp = jnp.concatenate([_only(hmk[h], kv), _only(hmk[h + 1], kv)], axis=0)
                vp = jnp.concatenate([_only(hmk[h], vv), _only(hmk[h + 1], vv)], axis=0)
                lse_p = jnp.where(first_head, _head_col(lse2, h), _head_col(lse2, h + 1))
                dl_p = jnp.where(first_head, _head_col(dl2, h), _head_col(dl2, h + 1))
                p = jnp.where(ok, jnp.exp(_dot_nt(qp, kp) * ATT_SCALE - lse_p), 0.0)
                ds = (p * (_dot_nt(dop, vp) - dl_p)).astype(MXU)
                dvp, dkp = _dot_tn(p.astype(MXU), dop), _dot_tn(ds, qp)
                dv = dv + _only(hmk[h], dvp[0:QB]) + _only(hmk[h + 1], dvp[QB:2 * QB])
                dk = dk + _only(hmk[h], dkp[0:QB]) + _only(hmk[h + 1], dkp[QB:2 * QB])
                dq2 = dq2 + _dot(ds, kp)
            dk_ref[j * QB:(j + 1) * QB] = dk * ATT_SCALE
            dv_ref[j * QB:(j + 1) * QB] = dv
            dq_scr[j * QB:(j + 2) * QB] += dq2
        dq_ref[...] = dq_scr[0:ASUB * QB] * ATT_SCALE

    nxt = pl.BlockSpec((QB, ATT_O), lambda b: (jnp.minimum((b + 1) * ASUB, nb - 1), 0))
    shp = jax.ShapeDtypeStruct((S, ATT_O), F32)
    return pl.pallas_call(body, grid=(nb // ASUB,), in_specs=[_BIG, _BIG, _BIG, nxt, _BIG, nxt, _BIG, nxt, _BIG, nxt], out_specs=[_BIG] * 3,
                          out_shape=[shp, shp, shp], scratch_shapes=[pltpu.VMEM(((ASUB + 1) * QB, ATT_O), F32)],
                          compiler_params=_params(("arbitrary",)), name=f"attn_bwd_{g}")(k, v, q, q, do, do, delta, delta, lse, lse)


def _merge_fwd(x0, u, a2, yb, att, wa, wb, wc, w_out, g_post, ts=512):
    S = x0.shape[0]

    def body(i, g, x_ref, gate_ref, a2_ref, yb_ref, att_ref, wa_ref, wb_ref, wc_ref, wo_ref, gp_ref, mg_ref, y_ref, xo_ref):
        gate = lambda n: jax.nn.sigmoid(gate_ref[:, n * D:(n + 1) * D].astype(F32))
        merged = gate(0) * _dot_nt(a2_ref[...], wa_ref[...])
        merged = merged + gate(1) * _dot_nt(yb_ref[...], wb_ref[...])
        merged = merged + gate(2) * _dot_nt(att_ref[...], wc_ref[...])
        mb = merged.astype(MXU)
        mg_ref[...] = mb
        y = _dot(mb, wo_ref[...])
        y_ref[...] = y
        xo_ref[...] = x_ref[...] + _rms(y, gp_ref[...])[0]

    ins = [("t", x0, D, 0), ("t", u, GATE_W, 0), ("t", a2, POOLW, 0), ("t", yb, POOLW, 0), ("t", att, ATT_O, 0),
           ("w", wa), ("w", wb), ("w", wc), ("w", w_out), ("w", g_post)]
    return _rows_call("merge_fwd", body, S, ts, ins, [("t", D, MXU), ("t", D, F32), ("t", D, F32)])


def _merge_bwd(dx, y1, u, a2, yb, att, merged, wa, wb, wc, w_out, g_post, ts=512):
    S = dx.shape[0]
    last = S // ts - 1

    def body(i, g, dx_ref, y_ref, gate_ref, a2_ref, yb_ref, att_ref, mg_ref, wa_ref, wb_ref, wc_ref, wo_ref, gp_ref,
             dgate_ref, da2_ref, dyb_ref, datt_ref, dgp_ref, dwo_ref, dwa_ref, dwb_ref, dwc_ref, acc_o, acc_a, acc_b, acc_c):
        @pl.when(g == 0)
        def _():
            for acc in (acc_o, acc_a, acc_b, acc_c):
                acc[...] = jnp.zeros_like(acc)

        dxv, y = dx_ref[...], y_ref[...]
        dy, r = _rms_bwd(dxv * gp_ref[...], y)
        _acc(dgp_ref, g, jnp.sum(dxv * (y * r), axis=0, keepdims=True))
        dyb16 = dy.astype(MXU)
        acc_o[...] += _dot_tn(mg_ref[...], dyb16)
        dm = _dot_nt(dyb16, wo_ref[...])
        for n, (src, w_ref, din_ref, acc) in enumerate(((a2_ref, wa_ref, da2_ref, acc_a), (yb_ref, wb_ref, dyb_ref, acc_b),
                                                       (att_ref, wc_ref, datt_ref, acc_c))):
            gt = jax.nn.sigmoid(gate_ref[:, n * D:(n + 1) * D].astype(F32))
            br = _dot_nt(src[...], w_ref[...])
            dgate_ref[:, n * D:(n + 1) * D] = (dm * br * gt * (1.0 - gt)).astype(dgate_ref.dtype)
            dbr = (dm * gt).astype(MXU)
            acc[...] += _dot_tn(dbr, src[...])
            din_ref[...] = _dot(dbr, w_ref[...])

        @pl.when(g == last)
        def _():
            for out, acc in ((dwo_ref, acc_o), (dwa_ref, acc_a), (dwb_ref, acc_b), (dwc_ref, acc_c)):
                out[...] = acc[...].astype(MXU)

    ins = [("t", dx, D, 0), ("t", y1, D, 0), ("t", u, GATE_W, 0), ("t", a2, POOLW, 0), ("t", yb, POOLW, 0), ("t", att, ATT_O, 0),
           ("t", merged, D, 0), ("w", wa), ("w", wb), ("w", wc), ("w", w_out), ("w", g_post)]
    wshapes = [(D, D), (D, POOLW), (D, POOLW), (D, ATT_O)]
    outs = [("c", IN_W, GATE_W, 0, MXU), ("t", POOLW, F32), ("t", POOLW, F32), ("t", ATT_O, F32), ("a", (1, D), F32)]
    outs += [("a", s, MXU) for s in wshapes]
    return _rows_call("merge_bwd", body, S, ts, ins, outs, scratch=[pltpu.VMEM(s, F32) for s in wshapes])


def _prenorm_bwd(name, dx_res, du, wt, x, g_pre, ts=256, lead=0):
    S = x.shape[0]
    N = du.shape[1]

    def body(i, g, dx_ref, du_ref, wt_ref, x_ref, g_ref, o_ref, dg_ref):
        if lead:
            dhv = _dot(du_ref[:, 0:lead], wt_ref[N - lead:N, :]) + _dot(du_ref[:, lead:N], wt_ref[0:N - lead, :])
        else:
            dhv = _dot(du_ref[...], wt_ref[...])
        xv = x_ref[...]
        dxn, r = _rms_bwd(dhv * g_ref[...], xv)
        o_ref[...] = dx_ref[...] + dxn
        _acc(dg_ref, g, jnp.sum(dhv * (xv * r), axis=0, keepdims=True))

    ins = [("t", dx_res, D, 0), ("t", du, N, 0), ("w", wt), ("t", x, D, 0), ("w", g_pre)]
    return _rows_call(name, body, S, ts, ins, [("t", D, F32), ("a", (1, D), F32)])


def _mem_heads(qm, kv_ref):
    out = []
    for h in range(4):
        q = qm[:, h * 128:(h + 1) * 128].astype(MXU)
        k = kv_ref[:, h * 128:(h + 1) * 128]
        v = kv_ref[:, MEM_W + h * 128:MEM_W + (h + 1) * 128]
        sc = _dot_nt(q, k) * MEM_SCALE
        e = jnp.exp(sc - jnp.max(sc, axis=1, keepdims=True))
        out.append((e / jnp.sum(e, axis=1, keepdims=True), q, k, v))
    return out


def _mem_fwd(x1, kv, g_pre, w_mq, w_mo, g_post, ts=512):
    S = x1.shape[0]

    def body(i, g, x_ref, kv_ref, gq_ref, wq_ref, wo_ref, gp_ref, om_ref, y_ref, xo_ref):
        x = x_ref[...]
        hb = _rms(x, gq_ref[...])[0].astype(MXU)
        qm = _dot(hb, wq_ref[...])
        om = jnp.concatenate([_dot(p.astype(MXU), v) for p, _, _, v in _mem_heads(qm, kv_ref)], axis=1).astype(MXU)
        om_ref[...] = om
        y = _dot_nt(om, wo_ref[...])
        y_ref[...] = y
        xo_ref[...] = x + _rms(y, gp_ref[...])[0]

    ins = [("t", x1, D, 0), ("w", kv), ("w", g_pre), ("w", w_mq), ("w", w_mo), ("w", g_post)]
    return _rows_call("mem_fwd", body, S, ts, ins, [("t", MEM_W, MXU), ("t", D, F32), ("t", D, F32)])


def _mem_bwd(dx2, ym, x1, om, kv, g_pre, w_mq, w_mo, g_post, ts=512):
    S = x1.shape[0]
    last = S // ts - 1

    def body(i, g, dx_ref, y_ref, x_ref, om_ref, kv_ref, gq_ref, wq_ref, wo_ref, gp_ref, dxo_ref, dgp_ref, dgq_ref, dkv_ref,
             dwo_ref, dwq_ref, acc_o, acc_q):
        dxv, y, x = dx_ref[...], y_ref[...], x_ref[...]
        dy, r = _rms_bwd(dxv * gp_ref[...], y)
        _acc(dgp_ref, g, jnp.sum(dxv * (y * r), axis=0, keepdims=True))
        dyb = dy.astype(MXU)
        dom = _dot(dyb, wo_ref[...])
        h, r1 = _rms(x, gq_ref[...])
        hb = h.astype(MXU)
        qm = _dot(hb, wq_ref[...])
        dqs = []

        @pl.when(g == 0)
        def _():
            dkv_ref[...] = jnp.zeros_like(dkv_ref)
            acc_o[...] = jnp.zeros_like(acc_o)
            acc_q[...] = jnp.zeros_like(acc_q)

        acc_o[...] += _dot_tn(dyb, om_ref[...])

        for hh, (p, q, k, v) in enumerate(_mem_heads(qm, kv_ref)):
            doh = dom[:, hh * 128:(hh + 1) * 128].astype(MXU)
            dp = _dot_nt(doh, v)
            dsc = (p * (dp - jnp.sum(dp * p, axis=1, keepdims=True)) * MEM_SCALE).astype(MXU)
            dqs.append(_dot(dsc, k))
            dkv_ref[:, hh * 128:(hh + 1) * 128] += _dot_tn(dsc, q)
            dkv_ref[:, MEM_W + hh * 128:MEM_W + (hh + 1) * 128] += _dot_tn(p.astype(MXU), doh)
        dq = jnp.concatenate(dqs, axis=1).astype(MXU)
        acc_q[...] += _dot_tn(hb, dq)
        dh = _dot_nt(dq, wq_ref[...])
        _acc(dgq_ref, g, jnp.sum(dh * (x * r1), axis=0, keepdims=True))
        dxo_ref[...] = dxv + _rms_bwd(dh * gq_ref[...], x)[0]

        @pl.when(g == last)
        def _():
            dwo_ref[...] = acc_o[...].astype(MXU)
            dwq_ref[...] = acc_q[...].astype(MXU)

    ins = [("t", dx2, D, 0), ("t", ym, D, 0), ("t", x1, D, 0), ("t", om, MEM_W, 0), ("w", kv), ("w", g_pre), ("w", w_mq), ("w", w_mo),
           ("w", g_post)]
    outs = [("t", D, F32), ("a", (1, D), F32), ("a", (1, D), F32), ("a", (256, D), F32), ("a", (D, MEM_W), MXU), ("a", (D, MEM_W), MXU)]
    return _rows_call("mem_bwd", body, S, ts, ins, outs, scratch=[pltpu.VMEM((D, MEM_W), F32), pltpu.VMEM((D, MEM_W), F32)])


def _gain_grad(name, dn, x):
    n = x.shape[0]

    def body(i, g, dn_ref, x_ref, o_ref):
        xv = x_ref[...]
        r = lax.rsqrt(jnp.mean(xv * xv, axis=-1, keepdims=True) + EPS)
        o_ref[...] = jnp.sum(dn_ref[...] * (xv * r), axis=0, keepdims=True)

    return _rows_call(name, body, n, n, [("t", dn, D, 0), ("t", x, D, 0)], [("a", (1, D), F32)])[0]


def _ffn_fwd(x2, u3, conv_f, w_down, g_post, ts=512):
    S = x2.shape[0]

    def body(i, g, x_ref, ua_ref, ub_ref, cw_ref, wd_ref, gp_ref, act_ref, y_ref, xo_ref, c_ref, cu):
        @pl.when(g == 0)
        def _():
            cu[...] = jnp.zeros_like(cu)

        ua = ua_ref[...].astype(F32)
        c, _, _ = _conv3(ua, cu[...], cw_ref[...])
        c_ref[...] = c.astype(MXU)
        act = (c * jax.nn.sigmoid(c) * ub_ref[...].astype(F32)).astype(MXU)
        act_ref[...] = act
        y = _dot(act, wd_ref[...])
        y_ref[...] = y
        xo_ref[...] = x_ref[...] + _rms(y, gp_ref[...])[0]
        cu[...] = ua[ts - 8:]

    ins = [("t", x2, D, 0), ("t", u3, D_FF, 0), ("t", u3, D_FF, 1), ("w", conv_f), ("w", w_down), ("w", g_post)]
    return _rows_call("ffn_fwd", body, S, ts, ins, [("t", D_FF, MXU), ("t", D, F32), ("t", D, F32), ("t", D_FF, MXU)],
                      scratch=[pltpu.VMEM((8, D_FF), F32)])


def _ffn_bwd(dx3, y3, u3, c, conv_f, w_down, g_post, ts=256):
    S = dx3.shape[0]

    def body(i, g, dx_ref, y_ref, ua_ref, ub_ref, c_ref, cw_ref, wd_ref, gp_ref, dy_ref, du_ref, dgp_ref, dcw_ref, cdc):
        @pl.when(g == 0)
        def _():
            cdc[...] = jnp.zeros_like(cdc)

        dxv, y = dx_ref[...], y_ref[...]
        dy, r = _rms_bwd(dxv * gp_ref[...], y)
        _acc(dgp_ref, g, jnp.sum(dxv * (y * r), axis=0, keepdims=True))
        dyb = dy.astype(MXU)
        dy_ref[...] = dyb
        dact = _dot_nt(dyb, wd_ref[...])
        ua, c, w = ua_ref[...].astype(F32), c_ref[...].astype(F32), cw_ref[...]
        sg = jax.nn.sigmoid(c)
        du_ref[:, D_FF:2 * D_FF] = (dact * (c * sg)).astype(du_ref.dtype)
        dc = dact * ub_ref[...].astype(F32) * (sg * (1.0 + c * (1.0 - sg)))
        dua, dc1, dc2 = _conv3_t(dc, cdc[...], w, shifted=True)
        du_ref[:, 0:D_FF] = dua.astype(du_ref.dtype)
        dw = jnp.concatenate([jnp.sum(ua * dc2, axis=0, keepdims=True), jnp.sum(ua * dc1, axis=0, keepdims=True),
                              jnp.sum(ua * dc, axis=0, keepdims=True)], axis=0)
        _acc(dcw_ref, g, dw)
        cdc[...] = dc[:8]

    ins = [("t", dx3, D, 0), ("t", y3, D, 0), ("t", u3, D_FF, 0), ("t", u3, D_FF, 1), ("t", c, D_FF, 0), ("w", conv_f),
           ("w", w_down), ("w", g_post)]
    outs = [("t", D, MXU), ("t", 2 * D_FF, MXU), ("a", (1, D), F32), ("a", (3, D_FF), F32)]
    return _rows_call("ffn_bwd", body, S, ts, ins, outs, scratch=[pltpu.VMEM((8, D_FF), F32)], reverse=True)


def _loss_head(x, target, ts=512):
    S = x.shape[0]

    def body(i, g, x_ref, t_ref, dx_ref, acc_ref):
        diff = x_ref[...] - t_ref[...]
        dx_ref[...] = diff * (1.0 / D)
        col = jnp.sum(diff * diff, axis=0, keepdims=True)
        part = col[:, 0:LANES]
        for j in range(1, D // LANES):
            part = part + col[:, j * LANES:(j + 1) * LANES]
        row = lax.broadcasted_iota(jnp.int32, (8, LANES), 0)
        _acc(acc_ref, g, jnp.where(row == 0, jnp.broadcast_to(part, (8, LANES)), 0.0))

    return _rows_call("loss_head", body, S, ts, [("t", x, D, 0), ("t", target, D, 0)], [("t", D, F32), ("a", (8, LANES), F32)])


_OPERAND_NAME = dict(w_in='w_in', w_branch_a='wa', w_branch_b='wb', w_branch_c='wc', w_out='w_out', w_mq='w_mq', w_mkv='w_mkv',
                     w_mo='w_mo', w_up='w_up', w_down='w_down')


def _big_operands(big):
    return {_OPERAND_NAME[n]: a for n, a in big.items()}


def _layer_weights(big, small, l):
    pool_w = small['pool_w'][l].astype(MXU)
    wblk = jnp.zeros((POOLW, POOLW), MXU)
    for g in range(4):
        wblk = lax.dynamic_update_slice(wblk, pool_w[g], (g * 96, g * 96))
    vec = lambda n: small[n][l].reshape(1, -1)
    return dict(
        _big_operands(big),
        wblk=wblk, pool_scale=vec('pool_scale'), conv_b=small['conv_b_w'][l], conv_f=small['conv_ffn_w'][l],
        g_mix_pre=vec('norm_mix_pre'), g_mix_post=vec('norm_mix_post'), g_mem_pre=vec('norm_mem_pre'),
        g_mem_post=vec('norm_mem_post'), g_memkv=vec('norm_memkv'), g_ffn_pre=vec('norm_ffn_pre'), g_ffn_post=vec('norm_ffn_post'))


def _layer_fwd(x0, mem, W, ctab, stab):
    sv = _layer_fwd_mix(x0, W, ctab, stab)
    return _layer_fwd_late(mem, W, sv), sv


def _layer_fwd_mix(x0, W, ctab, stab):
    return _layer_fwd_merge(W, _layer_fwd_branches(x0, W, ctab, stab))


def _layer_fwd_branches(x0, W, ctab, stab):
    sv = dict(x0=x0)
    sv['u'], sv['h1'] = _norm_mm("in_proj", x0, W['g_mix_pre'], W['w_in'], ts=2048, tn=IN_TILE, wt=True, rot=IN_ROT, out_dtype=MXU)
    sv['a2'], sv['yb'] = _poolconv_fwd(sv['u'], W['wblk'], W['pool_scale'], W['conv_b'])
    sv['qkv'] = q3, k3, v3 = _rope_perm(sv['u'], ctab, stab)
    sv['att'], sv['o'], sv['lse'] = _attn_combine([_attn_fwd(g, q3[g], k3[g], v3[g]) for g in range(3)])
    return sv


def _layer_fwd_merge(W, sv):
    sv['merged'], sv['y1'], sv['x1'] = _merge_fwd(sv['x0'], sv['u'], sv['a2'], sv['yb'], sv['att'], W['wa'], W['wb'], W['wc'],
                                                  W['w_out'], W['g_mix_post'])
    return sv


def _layer_fwd_late(mem, W, sv):
    sv['kv'], sv['memn'] = _norm_mm("mem_kv", mem, W['g_memkv'], W['w_mkv'], ts=256, tn=D, out_dtype=MXU)
    sv['om'], sv['ym'], sv['x2'] = _mem_fwd(sv['x1'], sv['kv'], W['g_mem_pre'], W['w_mq'], W['w_mo'], W['g_mem_post'])
    sv['u3'], sv['h3'] = _norm_mm("up_proj", sv['x2'], W['g_ffn_pre'], W['w_up'], ts=2048, tn=1408, wt=True, out_dtype=MXU)
    sv['act'], sv['y3'], x3, sv['c3'] = _ffn_fwd(sv['x2'], sv['u3'], W['conv_f'], W['w_down'], W['g_ffn_post'])
    return x3


def _layer_bwd(dx3, mem, W, sv, ctab, stab):
    dx1, g = _layer_bwd_late(dx3, mem, W, sv)
    dx0, g_mix = _layer_bwd_mix(dx1, W, sv, ctab, stab)
    return dx0, {**g, **g_mix}


def _layer_bwd_late(dx3, mem, W, sv):
    g = {}
    dy3, du3, g['norm_ffn_post'], g['conv_ffn_w'] = _ffn_bwd(dx3, sv['y3'], sv['u3'], sv['c3'], W['conv_f'], W['w_down'], W['g_ffn_post'])
    g['w_down'] = _mm_tn("dw_down", sv['act'], dy3, cap_k=256)
    g['w_up'] = _mm_tn("dw_up", du3, sv['h3'])
    dx2, g['norm_ffn_pre'] = _prenorm_bwd("ffn_pre_bwd", dx3, du3, W['w_up'], sv['x2'], W['g_ffn_pre'], ts=512)
    dx1, g['norm_mem_post'], g['norm_mem_pre'], dkv, g['w_mo'], g['w_mq'] = _mem_bwd(
        dx2, sv['ym'], sv['x1'], sv['om'], sv['kv'], W['g_mem_pre'], W['w_mq'], W['w_mo'], W['g_mem_post'])
    dkvb = dkv.astype(MXU)
    g['w_mkv'] = _mm_tn("dw_mkv", sv['memn'], dkvb)
    g['norm_memkv'] = _gain_grad("memkv_gain", _mm_nt("d_memn", dkvb, W['w_mkv'], ts=256, tn=512), mem)
    return dx1, g


def _layer_bwd_mix(dx1, W, sv, ctab, stab):
    du, g = _layer_bwd_mixers(dx1, W, sv, ctab, stab)
    g['w_in'] = _dw_in(du, sv)
    dx0, g['norm_mix_pre'] = _mix_pre_bwd(dx1, du, W, sv)
    return dx0, g


def _dw_in(du, sv):
    return _mm_tn("dw_in", du, sv['h1'], cap_k=IN_TILE, rot=IN_ROT)


def _mix_pre_bwd(dx1, du, W, sv):
    return _prenorm_bwd("mix_pre_bwd", dx1, du, W['w_in'], sv['x0'], W['g_mix_pre'], lead=GATE_W)


def _layer_bwd_mixers(dx1, W, sv, ctab, stab):
    parts, g = _layer_bwd_merge(dx1, W, sv)
    du, g_br = _layer_bwd_branches(parts, W, sv, ctab, stab)
    return du, {**g, **g_br}


def _layer_bwd_merge(dx1, W, sv):
    g = {}
    du, da2, dyb, datt, g['norm_mix_post'], g['w_out'], g['w_branch_a'], g['w_branch_b'], g['w_branch_c'] = _merge_bwd(
        dx1, sv['y1'], sv['u'], sv['a2'], sv['yb'], sv['att'], sv['merged'], W['wa'], W['wb'], W['wc'], W['w_out'], W['g_mix_post'])
    return (du, da2, dyb, datt), g


def _layer_bwd_branches(parts, W, sv, ctab, stab):
    du, da2, dyb, datt = parts
    g = {}
    du, g['pool_scale'], dwblk, g['conv_b_w'] = _poolconv_bwd(sv['u'], da2, dyb, du, W['wblk'], W['pool_scale'], W['conv_b'])
    g['pool_w'] = jnp.stack([dwblk[k * 96:(k + 1) * 96, k * 96:(k + 1) * 96] for k in range(4)])
    q3, k3, v3 = sv['qkv']
    do3, dl3, lse3 = _attn_bwd_prep(datt, sv['o'], sv['lse'])
    dqkv3 = [_attn_bwd(i, q3[i], k3[i], v3[i], do3[i], dl3[i], lse3[i]) for i in range(3)]
    du = _rope_unperm_bwd([[t[which] for t in dqkv3] for which in range(3)], du, ctab, stab)
    return du, g


def _local_step(x, mem, positions, target, big, small):
    ctab, stab = _rope_tables(positions)
    Ws = [_layer_weights(big[l], small, l) for l in range(DEPTH)]
    saved = []
    for l in range(DEPTH):
        x, sv = _layer_fwd(x, mem, Ws[l], ctab, stab)
        saved.append(sv)
    dx, acc = _loss_head(x, target)
    loss = jnp.sum(acc) * (0.5 / D)
    grads = [None] * DEPTH
    for l in reversed(range(DEPTH)):
        dx, grads[l] = _layer_bwd(dx, mem, Ws[l], saved[l], ctab, stab)
    return loss, dx, grads


_HBM = pl.BlockSpec(memory_space=pl.ANY)
MESH_ID = pl.DeviceIdType.MESH


def _all_gather(name, xs):
    n = len(xs)

    def body(*refs):
        x_refs, out_refs = refs[:n], refs[n:2 * n]
        send_sems, recv_sems, local_sems = refs[2 * n:]
        x, y, c = lax.axis_index("x"), lax.axis_index("y"), lax.axis_index("c")
        me, sibling = (x, y, c), (x, y, 1 - c)
        chips = [(1 - x, y), (x, 1 - y), (1 - x, 1 - y)]

        def slot(a, p):
            return out_refs[a].at[4 * p[0] + 2 * p[1] + p[2]]

        def copy(a, k, block, to, src=None):
            return pltpu.make_async_remote_copy(src_ref=slot(a, block) if src is None else src, dst_ref=slot(a, block),
                                                send_sem=send_sems.at[a, k], recv_sem=recv_sems.at[a, k], device_id=to,
                                                device_id_type=MESH_ID)

        started = []
        for a in range(n):
            mine = pltpu.make_async_copy(x_refs[a], slot(a, me), local_sems.at[a])
            mine.start()
            started.append(mine)
        first = []
        for a in range(n):
            first.append(copy(a, 0, me, sibling, src=x_refs[a]))
            first += [copy(a, 1 + j, me, (*chip, c), src=x_refs[a]) for j, chip in enumerate(chips)]
        for cp in first:
            cp.start()
        passed = []
        for j, chip in enumerate(chips):
            for a in range(n):
                copy(a, 1 + j, (*chip, c), me).wait_recv()
                fw = copy(a, 4 + j, (*chip, c), sibling)
                fw.start()
                passed.append(fw)
        for a in range(n):
            copy(a, 0, sibling, me).wait_recv()
            for j, chip in enumerate(chips):
                copy(a, 4 + j, (*chip, 1 - c), me).wait_recv()
        for cp in first + passed:
            cp.wait_send()
        for mine in started:
            mine.wait()

    return pl.pallas_call(
        body, out_shape=[jax.ShapeDtypeStruct((N_DEV,) + x.shape, x.dtype) for x in xs], in_specs=[_HBM] * n, out_specs=[_HBM] * n,
        scratch_shapes=[pltpu.SemaphoreType.DMA((n, 7)), pltpu.SemaphoreType.DMA((n, 7)), pltpu.SemaphoreType.DMA((n,))],
        name=name)(*xs)


_SEM =pl.BlockSpec(memory_space=pltpu.SEMAPHORE)
_IN_HBM = pl.BlockSpec(memory_space=pltpu.HBM)
_SIDE_EFFECT = pltpu.SideEffectType.DATAFLOW_SIDE_EFFECTING


def _push_copies(src_refs, land_refs, send_sems, recv_sems, per_peer):
    x, y, c = lax.axis_index("x"), lax.axis_index("y"), lax.axis_index("c")
    me = 4 * x + 2 * y + c
    copies = []
    for r in range(1, N_DEV):
        px, py, pc = x ^ ((r >> 2) & 1), y ^ ((r >> 1) & 1), c ^ (r & 1)
        for a, (s, d) in enumerate(zip(src_refs, land_refs)):
            k = a * (N_DEV - 1) + r - 1
            copies.append(pltpu.make_async_remote_copy(src_ref=s.at[4 * px + 2 * py + pc] if per_peer else s, dst_ref=d.at[me],
                                                       send_sem=send_sems.at[k], recv_sem=recv_sems.at[k],
                                                       device_id=(px, py, pc), device_id_type=MESH_ID))
    return copies


def _push_start(name, srcs, per_peer, after):
    n = len(srcs)
    lands = [lax.empty((N_DEV,) + (s.shape[1:] if per_peer else s.shape), s.dtype) for s in srcs]

    def body(*refs):
        for cp in _push_copies(refs[:n], refs[n:2 * n], refs[2 * n + 1], refs[2 * n + 2], per_peer):
            cp.start()
        refs[-1][...] = jnp.zeros_like(refs[-1])

    hbm = [pltpu.HBM(a.shape, a.dtype) for a in (*srcs, *lands)]
    sems = pltpu.SemaphoreType.DMA((n * (N_DEV - 1),))
    out = pl.pallas_call(
        body, name=name, out_shape=(sems, sems, *hbm, jax.ShapeDtypeStruct((8, LANES), F32)),
        in_specs=[_IN_HBM] * (2 * n) + [pl.BlockSpec(memory_space=pl.ANY)],
        out_specs=(_SEM, _SEM, *[_IN_HBM] * (2 * n), pl.BlockSpec(memory_space=pltpu.VMEM)),
        input_output_aliases={a: 2 + a for a in range(2 * n)},
        compiler_params=pltpu.CompilerParams(has_side_effects=_SIDE_EFFECT),
    )(*[pltpu.with_memory_space_constraint(a, pltpu.HBM) for a in (*srcs, *lands)], after)
    return out[0], out[1], out[2:2 + n], out[2 + n:2 + 2 * n], out[-1]


def _push_wait(name, started, per_peer, after):
    send_sems, recv_sems, srcs, lands, _ = started
    n = len(srcs)

    def body(*refs):
        for cp in _push_copies(refs[:n], refs[n:2 * n], refs[2 * n], refs[2 * n + 1], per_peer):
            cp.wait_send()
            cp.wait_recv()

    out = pl.pallas_call(
        body, name=name, out_shape=[pltpu.HBM(a.shape, a.dtype) for a in (*srcs, *lands)],
        in_specs=[_IN_HBM] * (2 * n) + [_SEM, _SEM, pl.BlockSpec(memory_space=pl.ANY)], out_specs=[_IN_HBM] * (2 * n),
        input_output_aliases={a: a for a in range(2 * n)},
        compiler_params=pltpu.CompilerParams(has_side_effects=_SIDE_EFFECT),
    )(*srcs, *lands, send_sems, recv_sems, after)
    if per_peer:
        return list(zip(out[:n], out[n:]))
    return _with_own(out[n:], out[:n], _my_slot())


def _my_slot():
    return 4 * lax.axis_index("x") + 2 * lax.axis_index("y") + lax.axis_index("c")


def _row_tile(rows, cols, budget):
    if rows * cols * 4 <= budget or rows % 16:
        return rows
    best = 16
    for t in range(16, rows + 1, 16):
        if rows % t == 0 and t * cols * 4 <= budget:
            best = t
    return best


def _slot_total(r_ref, own_ref):
    me = _my_slot()
    g = jnp.where(me == 0, own_ref[...], r_ref[0]).astype(F32)
    for k in range(1, N_DEV):
        g = g + jnp.where(me == k, own_ref[...], r_ref[k]).astype(F32)
    return g


def _sum_slots(name, pushed):
    src, recv = pushed
    _, R, C = recv.shape
    tr = _row_tile(R, C, UPDATE_BLOCK_BYTES)

    def body(r_ref, own_ref, o_ref):
        o_ref[...] = _slot_total(r_ref, own_ref)

    return pl.pallas_call(body, grid=(R // tr,),
                          in_specs=[pl.BlockSpec((N_DEV, tr, C), lambda i: (0, i, 0)), pl.BlockSpec((None, tr, C), lambda i: (_my_slot(), i, 0))],
                          out_specs=pl.BlockSpec((tr, C), lambda i: (i, 0)), out_shape=jax.ShapeDtypeStruct((R, C), F32),
                          compiler_params=_params(("arbitrary",)), name=name)(recv, src)


def _adamw_step(gv, w_ref, m_ref, v_ref, d_ref, mo_ref, vo_ref):
    mn = ADAM_B1 * m_ref[...] + (1.0 - ADAM_B1) * gv
    vn = ADAM_B2 * v_ref[...] + (1.0 - ADAM_B2) * (gv * gv)
    mo_ref[...] = mn
    vo_ref[...] = vn
    c1 = 1.0 - ADAM_B1 ** ADAM_STEP
    c2 = 1.0 - ADAM_B2 ** ADAM_STEP
    d_ref[...] = -ADAM_LR * ((mn / c1) / (jnp.sqrt(vn / c2) + ADAM_EPS) + ADAM_WD * w_ref[...])


def _adamw(name, g, w, m, v):
    shape = w.shape
    R, C = shape[-2], shape[-1]
    view = (-1, R, C)
    L = w.size // (R * C)
    tr = _row_tile(R, C, UPDATE_BLOCK_BYTES)

    def body(g_ref, w_ref, m_ref, v_ref, d_ref, mo_ref, vo_ref):
        _adamw_step(g_ref[...], w_ref, m_ref, v_ref, d_ref, mo_ref, vo_ref)

    blk = pl.BlockSpec((None, tr, C), lambda l, i: (l, i, 0))
    shp = jax.ShapeDtypeStruct((L, R, C), F32)
    outs = pl.pallas_call(body, grid=(L, R // tr), in_specs=[blk, blk, blk, blk], out_specs=[blk, blk, blk], out_shape=[shp, shp, shp],
                          compiler_params=_params(("arbitrary", "arbitrary")), name=name)(*[a.reshape(view) for a in (g, w, m, v)])
    return [o.reshape(shape) for o in outs]


def _sum_adamw(name, pushed, w, m, v):
    L, R, C = w.shape
    tr = _row_tile(R, C, UPDATE_BLOCK_BYTES)
    n_i = R // tr

    def body(*refs):
        shares, (w_ref, m_ref, v_ref, g_ref, d_ref, mo_ref, vo_ref) = refs[:2 * L], refs[2 * L:]
        for k in range(L):
            @pl.when(pl.program_id(0) == k)
            def _(k=k):
                g_ref[...] = _slot_total(shares[2 * k], shares[2 * k + 1])
        _adamw_step(g_ref[...], w_ref, m_ref, v_ref, d_ref, mo_ref, vo_ref)

    def during(k):
        return lambda l, i: jnp.where(l == k, i, jnp.where(l < k, 0, n_i - 1))

    in_specs, operands = [], []
    for k, (src, recv) in enumerate(pushed):
        in_specs += [pl.BlockSpec((N_DEV, tr, C), lambda l, i, at=during(k): (0, at(l, i), 0)),
                     pl.BlockSpec((None, tr, C), lambda l, i, at=during(k): (_my_slot(), at(l, i), 0))]
        operands += [recv, src]
    blk = pl.BlockSpec((None, tr, C), lambda l, i: (l, i, 0))
    shp = jax.ShapeDtypeStruct((L, R, C), F32)
    return pl.pallas_call(body, grid=(L, n_i), in_specs=in_specs + [blk, blk, blk], out_specs=[blk] * 4, out_shape=[shp] * 4,
                          compiler_params=_params(("arbitrary", "arbitrary")), name=name)(*operands, w, m, v)


def _pad_flat(a, n):
    a = a.reshape(-1)
    return jnp.pad(a, (0, n - a.shape[0]))


def _seg(n):
    return -(-n // FLAT_ALIGN) * FLAT_ALIGN


def _to_blocks(full, axis):
    shp = full.shape
    return jnp.moveaxis(full.reshape(shp[:axis] + (N_DEV, shp[axis] // N_DEV) + shp[axis + 1:]), axis, 0)


def _from_blocks(blocks, axis):
    b = jnp.moveaxis(blocks, 0, axis)
    shp = b.shape
    return b.reshape(shp[:axis] + (shp[axis] * shp[axis + 1],) + shp[axis + 2:])


def _as_rows(shard, n):
    return shard.T if SHARD_AXIS[n] == 2 else shard


def _with_own(lands, own, me):
    return [lax.dynamic_update_slice(land, o[None], (me, 0, 0)) for land, o in zip(lands, own)]


def kernel(x, mem, positions, norm_mix_pre, norm_mix_post, w_in, pool_w, pool_scale, conv_b_w, w_branch_a, w_branch_b, w_branch_c, w_out, norm_mem_pre, norm_mem_post, norm_memkv, w_mq, w_mkv, w_mo, norm_ffn_pre, norm_ffn_post, w_up, conv_ffn_w, w_down, loss_target, m_norm_mix_pre, m_norm_mix_post, m_w_in, m_pool_w, m_pool_scale, m_conv_b_w, m_w_branch_a, m_w_branch_b, m_w_branch_c, m_w_out, m_norm_mem_pre, m_norm_mem_post, m_norm_memkv, m_w_mq, m_w_mkv, m_w_mo, m_norm_ffn_pre, m_norm_ffn_post, m_w_up, m_conv_ffn_w, m_w_down, v_norm_mix_pre, v_norm_mix_post, v_w_in, v_pool_w, v_pool_scale, v_conv_b_w, v_w_branch_a, v_w_branch_b, v_w_branch_c, v_w_out, v_norm_mem_pre, v_norm_mem_post, v_norm_memkv, v_w_mq, v_w_mkv, v_w_mo, v_norm_ffn_pre, v_norm_ffn_post, v_w_up, v_conv_ffn_w, v_w_down):
    w = dict(norm_mix_pre=norm_mix_pre, norm_mix_post=norm_mix_post, w_in=w_in, pool_w=pool_w, pool_scale=pool_scale, conv_b_w=conv_b_w, w_branch_a=w_branch_a, w_branch_b=w_branch_b, w_branch_c=w_branch_c, w_out=w_out, norm_mem_pre=norm_mem_pre, norm_mem_post=norm_mem_post, norm_memkv=norm_memkv, w_mq=w_mq, w_mkv=w_mkv, w_mo=w_mo, norm_ffn_pre=norm_ffn_pre, norm_ffn_post=norm_ffn_post, w_up=w_up, conv_ffn_w=conv_ffn_w, w_down=w_down)
    m = dict(norm_mix_pre=m_norm_mix_pre, norm_mix_post=m_norm_mix_post, w_in=m_w_in, pool_w=m_pool_w, pool_scale=m_pool_scale, conv_b_w=m_conv_b_w, w_branch_a=m_w_branch_a, w_branch_b=m_w_branch_b, w_branch_c=m_w_branch_c, w_out=m_w_out, norm_mem_pre=m_norm_mem_pre, norm_mem_post=m_norm_mem_post, norm_memkv=m_norm_memkv, w_mq=m_w_mq, w_mkv=m_w_mkv, w_mo=m_w_mo, norm_ffn_pre=m_norm_ffn_pre, norm_ffn_post=m_norm_ffn_post, w_up=m_w_up, conv_ffn_w=m_conv_ffn_w, w_down=m_w_down)
    v = dict(norm_mix_pre=v_norm_mix_pre, norm_mix_post=v_norm_mix_post, w_in=v_w_in, pool_w=v_pool_w, pool_scale=v_pool_scale, conv_b_w=v_conv_b_w, w_branch_a=v_w_branch_a, w_branch_b=v_w_branch_b, w_branch_c=v_w_branch_c, w_out=v_w_out, norm_mem_pre=v_norm_mem_pre, norm_mem_post=v_norm_mem_post, norm_memkv=v_norm_memkv, w_mq=v_w_mq, w_mkv=v_w_mkv, w_mo=v_w_mo, norm_ffn_pre=v_norm_ffn_pre, norm_ffn_post=v_norm_ffn_post, w_up=v_w_up, conv_ffn_w=v_conv_ffn_w, w_down=v_w_down)

    mix_big = [n for n in BIG if n not in LATE_BIG]
    block = lambda names, l: [_as_rows(w[n][l], n).astype(MXU) for n in names]
    conv = jnp.concatenate([_pad_flat(w[n], _seg(w[n].size)) for n in F32_GATHERED]).reshape(-1, LANES)
    groups = dict(m=MERGE_BIG, b=LATE_BIG, a=mix_big)
    got0 = _all_gather("weights_all_gather_0", block(['w_in'], 0) + [conv])
    conv_all = got0[-1].reshape(N_DEV, -1)
    small, off = {n: w[n] for n in WEIGHTS if n not in SHARD_AXIS}, 0
    for n in F32_GATHERED:
        small[n] = _from_blocks(conv_all[:, off:off + w[n].size].reshape((N_DEV,) + w[n].shape), 2)
        off += _seg(w[n].size)
    whole = lambda names, got: {n: o.reshape(-1, o.shape[-1]) for n, o in zip(names, got)}
    pushes, after = {}, got0[0]
    for tag, l in (('m', 0), ('b', 0), ('a', 1)):
        pushes[tag, l] = _push_start(f"weights_push_start_{l}{tag}", block(groups[tag], l), False, after)
        after = pushes[tag, l][4]

    def arrived(tag, l, done):
        return _big_operands(whole(groups[tag], _push_wait(f"weights_push_wait_{l}{tag}", pushes[tag, l], False, done)))

    ctab, stab = _rope_tables(positions[0])
    W0 = _layer_weights(whole(['w_in'], got0), small, 0)
    sv0 = _layer_fwd_branches(x[0], dict(W0, g_mix_pre=W0['g_mix_pre'] + after[0, 0]), ctab, stab)
    W0.update(arrived('m', 0, sv0['att']))
    sv0 = _layer_fwd_merge(W0, sv0)
    W0.update(arrived('b', 0, sv0['x1']))
    x1 = _layer_fwd_late(mem[0], W0, sv0)
    pushes['b', 1] = _push_start("weights_push_start_1b", block(groups['b'], 1), False, x1)
    W1 = _layer_weights({}, small, 1)
    W1.update(arrived('a', 1, pushes['b', 1][4]))
    sv1 = _layer_fwd_mix(x1, W1, ctab, stab)
    W1.update(arrived('b', 1, sv1['x1']))
    x2 = _layer_fwd_late(mem[0], W1, sv1)
    dx, acc = _loss_head(x2, loss_target[0])
    loss_share = jnp.sum(acc) * (0.5 / D)
    grads = [None] * DEPTH
    dx, grads[1] = _layer_bwd(dx, mem[0], W1, sv1, ctab, stab)
    sent = [None, [grads[1][n].reshape(N_DEV, -1, grads[1][n].shape[-1]) for n in BIG]]
    push_g = _push_start("grads_push_start_1", sent[1], True, dx)
    dx, g_late = _layer_bwd_late(dx, mem[0], dict(W0, g_ffn_post=W0['g_ffn_post'] + push_g[4][0, 0]), sv0)
    sent_late = [g_late[n].reshape(N_DEV, -1, g_late[n].shape[-1]) for n in LATE_BIG]
    push_l = _push_start("grads_push_start_0", sent_late, True, dx)
    parts, g_mix = _layer_bwd_merge(dx, dict(W0, g_mix_post=W0['g_mix_post'] + push_l[4][0, 0]), sv0)
    sent_merge = [g_mix[n].reshape(N_DEV, -1, g_mix[n].shape[-1]) for n in MERGE_BIG]
    push_m = _push_start("grads_push_start_0m", sent_merge, True, parts[0])
    du, g_br = _layer_bwd_branches(parts, dict(W0, pool_scale=W0['pool_scale'] + push_m[4][0, 0]), sv0, ctab, stab)
    g_mix.update(g_br)
    g_mix['w_in'] = _dw_in(du, sv0)
    sent_in = [g_mix['w_in'].reshape(N_DEV, -1, D)]
    push_i = _push_start("grads_push_start_in", sent_in, True, du)
    dx, g_mix['norm_mix_pre'] = _mix_pre_bwd(dx, du, dict(W0, g_mix_pre=W0['g_mix_pre'] + push_i[4][0, 0]), sv0)
    grads[0] = {**g_late, **g_mix}
    recv1 = _push_wait("grads_push_wait_1", push_g, True, dx)
    recv_late = _push_wait("grads_push_wait_0", push_l, True, dx)
    recv_merge = _push_wait("grads_push_wait_0m", push_m, True, dx)

    misc_names = [n for n in WEIGHTS if n not in BIG]
    stacked = {n: jnp.stack([grads[l][n].reshape(small[n].shape[1:]) for l in range(DEPTH)]) for n in misc_names}
    rows = [(_to_blocks(stacked[n], 2) if n in SHARD_AXIS else jnp.broadcast_to(stacked[n][None], (N_DEV,) + stacked[n].shape))
            for n in misc_names]
    segs = [_seg(w[n].size) for n in misc_names]
    misc = jnp.concatenate([jnp.pad(r.reshape(N_DEV, -1), ((0, 0), (0, s - r[0].size))) for r, s in zip(rows, segs)]
                           + [jnp.broadcast_to(loss_share, (N_DEV, FLAT_ALIGN))], axis=1).reshape(N_DEV, -1, LANES)
    push_x = _push_start("grads_push_start_small", [misc], True, dx)
    g_out, shares = {}, {}
    for l, names, recv in ((1, BIG, recv1), (0, LATE_BIG, recv_late), (0, MERGE_BIG, recv_merge)):
        for n, r in zip(names, recv):
            shares[n, l] = r

    swap = lambda a: jnp.swapaxes(a, 1, 2)

    def update(n):
        if n not in BIG:
            return [g_out[n], *_adamw(f"adamw_{n}", g_out[n], w[n], m[n], v[n])]
        of_layers = [shares[n, l] for l in range(DEPTH)]
        if SHARD_AXIS[n] == 1:
            return _sum_adamw(f"adamw_{n}", of_layers, w[n], m[n], v[n])
        if w[n].shape[2] % LANES:
            return [swap(a) for a in _sum_adamw(f"adamw_{n}", of_layers, swap(w[n]), swap(m[n]), swap(v[n]))]
        g = swap(jnp.stack([_sum_slots(f"sum_{n}_{l}", s) for l, s in enumerate(of_layers)]))
        return [g, *_adamw(f"adamw_{n}", g, w[n], m[n], v[n])]

    done = {n: update(n) for n in BIG if n != 'w_in'}
    shares['w_in', 0] = _push_wait("grads_push_wait_in", push_i, True, done[BIG[-1]][1])[0]
    done['w_in'] = update('w_in')
    after_w_in = swap(done['w_in'][1])
    misc_sum = _sum_slots("sum_misc", _push_wait("grads_push_wait_small", push_x, True, after_w_in)[0]).reshape(-1)
    off = 0
    for n, s in zip(misc_names, segs):
        g_out[n] = misc_sum[off:off + w[n].size].reshape(w[n].shape)
        done[n] = update(n)
        off += s
    return (misc_sum[off], dx[None], *[done[n][k] for k in range(4) for n in WEIGHTS])
```

```python
import jax
import jax.numpy as jnp
from jax import lax
from jax.experimental import pallas as pl
from jax.experimental.pallas import tpu as pltpu

F32 = jnp.float32
MXU = jnp.bfloat16

D = 1024
DEPTH = 2
POOLW = 384
ATT_W = 768
ATT_O = 256
GATE_W = 3 * D
IN_W = 6912
IN_TILE = 768
IN_ROT = (IN_W - GATE_W) // IN_TILE
MEM_W = 512
D_FF = 2816
EPS = 1e-6
ROPE_THETA = 500000.0
QB = 128
DILS = (1, 4, 16)
NEG = -1e30
MEM_SCALE = 128 ** -0.5
ATT_SCALE = 0.125

ADAM_LR, ADAM_B1, ADAM_B2, ADAM_EPS, ADAM_WD, ADAM_STEP = 0.001, 0.9, 0.999, 1e-08, 0.01, 10

N_DEV = 8
MESH_AXES = ("x", "y", "c")
LANES = 128
FLAT_ALIGN = 2048
UPDATE_BLOCK_BYTES = 1 << 20

WEIGHTS = ['norm_mix_pre', 'norm_mix_post', 'w_in', 'pool_w', 'pool_scale', 'conv_b_w', 'w_branch_a', 'w_branch_b',
           'w_branch_c', 'w_out', 'norm_mem_pre', 'norm_mem_post', 'norm_memkv', 'w_mq', 'w_mkv', 'w_mo',
           'norm_ffn_pre', 'norm_ffn_post', 'w_up', 'conv_ffn_w', 'w_down']
SHARD_AXIS = {'w_in': 2, 'conv_b_w': 2, 'w_branch_a': 2, 'w_branch_b': 2, 'w_branch_c': 2, 'w_out': 1, 'w_mq': 1,
              'w_mkv': 1, 'w_mo': 2, 'w_up': 2, 'conv_ffn_w': 2, 'w_down': 1}
F32_GATHERED = ('conv_b_w', 'conv_ffn_w')
BIG = [n for n in WEIGHTS if n in SHARD_AXIS and n not in F32_GATHERED]
LATE_BIG = ['w_mq', 'w_mkv', 'w_mo', 'w_up', 'w_down']
MERGE_BIG = ['w_branch_a', 'w_branch_b', 'w_branch_c', 'w_out']


VMEM_LIMIT_MB = 60


def _params(sem):
    return pltpu.CompilerParams(dimension_semantics=sem, vmem_limit_bytes=VMEM_LIMIT_MB << 20)


def _dot(a, b, prec=None):
    return lax.dot_general(a, b, (((1,), (0,)), ((), ())), preferred_element_type=F32, precision=prec)


def _dot_nt(a, b, prec=None):
    return lax.dot_general(a, b, (((1,), (1,)), ((), ())), preferred_element_type=F32, precision=prec)


def _dot_tn(a, b, prec=None):
    return lax.dot_general(a, b, (((0,), (0,)), ((), ())), preferred_element_type=F32, precision=prec)


def _tile(n, cap):
    if n <= cap:
        return n
    best = None
    for t in range(LANES, cap + 1, LANES):
        if n % t == 0:
            best = t
    assert best is not None, (n, cap)
    return best


def _rms(x, g):
    r = lax.rsqrt(jnp.mean(x * x, axis=-1, keepdims=True) + EPS)
    return x * r * g, r


def _rms_bwd(w, y):
    r = lax.rsqrt(jnp.mean(y * y, axis=-1, keepdims=True) + EPS)
    return r * w - y * (r * r * r) * jnp.mean(w * y, axis=-1, keepdims=True), r


def _rows_call(name, body, n_rows, ts, ins, outs, scratch=(), reverse=False, aliases=None):
    nt = n_rows // ts
    assert nt * ts == n_rows

    def tile_of(g):
        return (nt - 1 - g) if reverse else g

    in_specs, args = [], []
    for op in ins:
        if op[0] == "t":
            _, a, cw, cb = op
            in_specs.append(pl.BlockSpec((ts, cw), lambda g, cb=cb: (tile_of(g), cb)))
        elif op[0] == "h":
            _, a, hr, cw, cb = op
            in_specs.append(pl.BlockSpec((hr, cw), lambda g, cb=cb, k=ts // hr: (jnp.maximum(tile_of(g) * k - 1, 0), cb)))
        elif op[0] == "x":
            _, a = op
            in_specs.append(pl.BlockSpec(memory_space=pl.ANY))
        else:
            _, a = op
            in_specs.append(pl.BlockSpec(a.shape, lambda g, n=a.ndim: (0,) * n))
        args.append(a)
    out_specs, out_shape = [], []
    for op in outs:
        if op[0] == "t":
            _, cols, dt = op
            out_specs.append(pl.BlockSpec((ts, cols), lambda g: (tile_of(g), 0)))
            out_shape.append(jax.ShapeDtypeStruct((n_rows, cols), dt))
        elif op[0] == "c":
            _, total, cols, cb, dt = op
            out_specs.append(pl.BlockSpec((ts, cols), lambda g, cb=cb: (tile_of(g), cb)))
            out_shape.append(jax.ShapeDtypeStruct((n_rows, total), dt))
        else:
            _, shp, dt = op
            out_specs.append(pl.BlockSpec(shp, lambda g, n=len(shp): (0,) * n))
            out_shape.append(jax.ShapeDtypeStruct(shp, dt))

    def kern(*refs):
        g = pl.program_id(0)
        body(tile_of(g), g, *refs)

    return pl.pallas_call(kern, grid=(nt,), in_specs=in_specs, out_specs=out_specs, out_shape=out_shape,
                          scratch_shapes=list(scratch), input_output_aliases=aliases or {},
                          compiler_params=_params(("arbitrary",)), name=name)(*args)


def _acc(ref, g, val):
    @pl.when(g == 0)
    def _():
        ref[...] = val

    @pl.when(g != 0)
    def _():
        ref[...] += val


def _norm_mm(name, x, g, w, ts, tn, out_dtype=F32, wt=False, rot=0):
    S, K = x.shape
    N = w.shape[0] if wt else w.shape[1]
    assert wt or not rot

    def body(x_ref, g_ref, w_ref, o_ref, h_ref, hs):
        @pl.when(pl.program_id(1) == 0)
        def _():
            h, _ = _rms(x_ref[...], g_ref[...])
            hs[...] = h.astype(MXU)
            h_ref[...] = h.astype(MXU)

        o_ref[...] = (_dot_nt if wt else _dot)(hs[...], w_ref[...]).astype(out_dtype)

    w_spec = pl.BlockSpec((tn, K), lambda i, j: ((j + rot) % (N // tn), 0)) if wt else pl.BlockSpec((K, tn), lambda i, j: (0, j))
    return pl.pallas_call(
        body, grid=(S // ts, N // tn),
        in_specs=[pl.BlockSpec((ts, K), lambda i, j: (i, 0)), pl.BlockSpec((1, K), lambda i, j: (0, 0)), w_spec],
        out_specs=[pl.BlockSpec((ts, tn), lambda i, j: (i, j)), pl.BlockSpec((ts, K), lambda i, j: (i, 0))],
        out_shape=[jax.ShapeDtypeStruct((S, N), out_dtype), jax.ShapeDtypeStruct((S, K), MXU)],
        scratch_shapes=[pltpu.VMEM((ts, K), MXU)],
        compiler_params=_params(("arbitrary", "arbitrary")), name=name)(x, g, w)


def _mm_nt(name, a, b, ts, tn, out_dtype=F32):
    M, K = a.shape
    N = b.shape[0]

    def body(a_ref, b_ref, o_ref):
        o_ref[...] = _dot_nt(a_ref[...], b_ref[...]).astype(out_dtype)

    return pl.pallas_call(
        body, grid=(M // ts, N // tn),
        in_specs=[pl.BlockSpec((ts, K), lambda i, j: (i, 0)), pl.BlockSpec((tn, K), lambda i, j: (j, 0))],
        out_specs=pl.BlockSpec((ts, tn), lambda i, j: (i, j)), out_shape=jax.ShapeDtypeStruct((M, N), out_dtype),
        compiler_params=_params(("arbitrary", "arbitrary")), name=name)(a, b)


def _mm_tn(name, a, b, cap_k=512, cap_n=1024, out_dtype=MXU, rot=0):
    S, K = a.shape
    N = b.shape[1]
    tk, tn = _tile(K, cap_k), _tile(N, cap_n)

    def body(a_ref, b_ref, o_ref):
        o_ref[...] = _dot_tn(a_ref[...], b_ref[...]).astype(out_dtype)

    return pl.pallas_call(
        body, grid=(K // tk, N // tn),
        in_specs=[pl.BlockSpec((S, tk), lambda i, j: (0, i)), pl.BlockSpec((S, tn), lambda i, j: (0, j))],
        out_specs=pl.BlockSpec((tk, tn), lambda i, j: ((i + rot) % (K // tk), j)), out_shape=jax.ShapeDtypeStruct((K, N), out_dtype),
        compiler_params=_params(("arbitrary", "arbitrary")), name=name)(a, b)


def _pool_cols(shape):
    col = lax.broadcasted_iota(jnp.int32, shape, 1)
    return col < 96, col < 192, col < 288


def _pool_select(s2, s4, s8, s16):
    c1, c2, c3 = _pool_cols(s2.shape)
    return jnp.where(c1, s2, jnp.where(c2, s4, jnp.where(c3, s8, s16)))


def _pool_cnt(t0, ts):
    c1, c2, c3 = _pool_cols((ts, POOLW))
    win = jnp.where(c1, 2, jnp.where(c2, 4, jnp.where(c3, 8, 16)))
    t = t0 + lax.broadcasted_iota(jnp.int32, (ts, POOLW), 0)
    return jnp.minimum(t + 1, win).astype(F32)


def _pooled(a, prev, t0):
    ts = a.shape[0]
    ext = jnp.concatenate([prev, a], axis=0)
    s2 = ext + pltpu.roll(ext, 1, axis=0)
    s4 = s2 + pltpu.roll(s2, 2, axis=0)
    s8 = s4 + pltpu.roll(s4, 4, axis=0)
    s16 = s8 + pltpu.roll(s8, 8, axis=0)
    sums = _pool_select(s2, s4, s8, s16)[16:]
    return sums / _pool_cnt(t0, ts) - a


def _conv3(z, prev8, w):
    ext = jnp.concatenate([prev8, z], axis=0)
    z1 = pltpu.roll(ext, 1, axis=0)[8:]
    z2 = pltpu.roll(ext, 2, axis=0)[8:]
    return w[0:1] * z2 + w[1:2] * z1 + w[2:3] * z, z1, z2


def _conv3_t(dc, next8, w, shifted=False):
    ts = dc.shape[0]
    ext = jnp.concatenate([dc, next8], axis=0)
    n = ts + 8
    u1 = pltpu.roll(ext, n - 1, axis=0)[:ts]
    u2 = pltpu.roll(ext, n - 2, axis=0)[:ts]
    out = w[2:3] * dc + w[1:2] * u1 + w[0:1] * u2
    return (out, u1, u2) if shifted else out


def _poolconv_fwd(u, wblk, pool_scale, conv_b, ts=512):
    S = u.shape[0]

    def body(i, g, a_ref, bx_ref, bb_ref, bc_ref, wblk_ref, ps_ref, cw_ref, a2_ref, yb_ref, ca, cz):
        @pl.when(g == 0)
        def _():
            ca[...] = jnp.zeros_like(ca)
            cz[...] = jnp.zeros_like(cz)

        a = a_ref[...].astype(F32)
        p = _pooled(a, ca[...], i * ts)
        mixed = _dot(p.astype(MXU), wblk_ref[...])
        a2_ref[...] = (mixed * ps_ref[...]).astype(MXU)
        z = bc_ref[...].astype(F32) * bx_ref[...].astype(F32)
        conv, _, _ = _conv3(z, cz[...], cw_ref[...])
        yb_ref[...] = (bb_ref[...].astype(F32) * conv).astype(MXU)
        ca[...] = a[ts - 16:]
        cz[...] = z[ts - 8:]

    ins = [("t", u, POOLW, 8), ("t", u, POOLW, 9), ("t", u, POOLW, 10), ("t", u, POOLW, 11), ("w", wblk), ("w", pool_scale),
           ("w", conv_b)]
    return _rows_call("poolconv_fwd", body, S, ts, ins, [("t", POOLW, MXU), ("t", POOLW, MXU)],
                      scratch=[pltpu.VMEM((16, POOLW), F32), pltpu.VMEM((8, POOLW), F32)])


def _poolconv_bwd(u, d_a2, d_yb, du, wblk, pool_scale, conv_b, ts=512):
    S = u.shape[0]

    def body(i, g, a_ref, bx_ref, bb_ref, bc_ref, ap_ref, bxp_ref, bcp_ref, da2_ref, dyb_ref, wblk_ref, ps_ref, cw_ref, _,
             o_ref, dps_ref, dwb_ref, dcw_ref, ce, cdz):
        @pl.when(g == 0)
        def _():
            ce[...] = jnp.zeros_like(ce)
            cdz[...] = jnp.zeros_like(cdz)

        first = (i > 0).astype(F32)
        a = a_ref[...].astype(F32)
        p = _pooled(a, ap_ref[...].astype(F32) * first, i * ts)
        pb = p.astype(MXU)
        mixed = _dot(pb, wblk_ref[...])
        da2 = da2_ref[...]
        dmixed = (da2 * ps_ref[...]).astype(MXU)
        dp = _dot_nt(dmixed, wblk_ref[...])
        _acc(dps_ref, g, jnp.sum(da2 * mixed, axis=0, keepdims=True))
        _acc(dwb_ref, g, _dot_tn(pb, dmixed))
        e = dp / _pool_cnt(i * ts, ts)
        ext = jnp.concatenate([e, ce[...]], axis=0)
        n = ts + 16
        f2 = ext + pltpu.roll(ext, n - 1, axis=0)
        f4 = f2 + pltpu.roll(f2, n - 2, axis=0)
        f8 = f4 + pltpu.roll(f4, n - 4, axis=0)
        f16 = f8 + pltpu.roll(f8, n - 8, axis=0)
        o_ref[:, 0:POOLW] = (_pool_select(f2, f4, f8, f16)[:ts] - dp).astype(o_ref.dtype)
        ce[...] = e[:16]

        bx, bb, bc = bx_ref[...].astype(F32), bb_ref[...].astype(F32), bc_ref[...].astype(F32)
        z = bc * bx
        w = cw_ref[...]
        conv, z1, z2 = _conv3(z, (bxp_ref[...].astype(F32) * bcp_ref[...].astype(F32))[8:16] * first, w)
        dyb = dyb_ref[...]
        dconv = dyb * bb
        dz = _conv3_t(dconv, cdz[...], w)
        o_ref[:, POOLW:2 * POOLW] = (dz * bc).astype(o_ref.dtype)
        o_ref[:, 2 * POOLW:3 * POOLW] = (dyb * conv).astype(o_ref.dtype)
        o_ref[:, 3 * POOLW:4 * POOLW] = (dz * bx).astype(o_ref.dtype)
        dw = jnp.concatenate([jnp.sum(dconv * z2, axis=0, keepdims=True), jnp.sum(dconv * z1, axis=0, keepdims=True),
                              jnp.sum(dconv * z, axis=0, keepdims=True)], axis=0)
        _acc(dcw_ref, g, dw)
        cdz[...] = dconv[:8]

    ins = [("t", u, POOLW, 8), ("t", u, POOLW, 9), ("t", u, POOLW, 10), ("t", u, POOLW, 11),
           ("h", u, 16, POOLW, 8), ("h", u, 16, POOLW, 9), ("h", u, 16, POOLW, 11),
           ("t", d_a2, POOLW, 0), ("t", d_yb, POOLW, 0), ("w", wblk), ("w", pool_scale), ("w", conv_b), ("x", du)]
    outs = [("c", IN_W, 4 * POOLW, GATE_W // (4 * POOLW), MXU), ("a", (1, POOLW), F32), ("a", (POOLW, POOLW), F32), ("a", (3, POOLW), F32)]
    return _rows_call("poolconv_bwd", body, S, ts, ins, outs, aliases={len(ins) - 1: 0},
                      scratch=[pltpu.VMEM((16, POOLW), F32), pltpu.VMEM((8, POOLW), F32)], reverse=True)


def _rope_tables(positions):
    inv = ROPE_THETA ** (-jnp.arange(0, 16, 2, dtype=F32) / 16)
    lane_inv = jnp.tile(jnp.concatenate([inv, inv, jnp.zeros(48, F32)]), 2)
    lane_sign = jnp.tile(jnp.concatenate([-jnp.ones(8, F32), jnp.ones(8, F32), jnp.zeros(48, F32)]), 2)
    ang = positions.astype(F32)[:, None] * lane_inv
    return jnp.where(lane_sign == 0, 1.0, jnp.cos(ang)), jnp.sin(ang) * lane_sign


def _partner(x):
    lane = lax.broadcasted_iota(jnp.int32, x.shape, 1) % 64
    return jnp.where(lane < 8, pltpu.roll(x, LANES - 8, axis=1), jnp.where(lane < 16, pltpu.roll(x, 8, axis=1), 0.0))


def _rope(x, c, s):
    return x * c + _partner(x) * s


def _rope_t(x, c, s):
    return x * c + _partner(x * s)


def _rows_of(r, n, d):
    return pl.ds(r, n, stride=d) if d > 1 else pl.ds(0, n)


def _head_masks(shape):
    lane = lax.broadcasted_iota(jnp.int32, shape, 1) // 64
    return [lane == h for h in range(4)]


def _only(mask, x):
    return jnp.where(mask, x, jnp.zeros_like(x))


def _rope_perm(u, ctab, stab, ts=512):
    S = u.shape[0]
    nch = ATT_W // LANES

    def body(*refs):
        chunks, (c_ref, s_ref), outs, scr = refs[:3 * nch], refs[3 * nch:3 * nch + 2], refs[3 * nch + 2:-1], refs[-1]
        for k in range(3 * nch):
            scr[k] = chunks[k][...].astype(F32)
        for g, d in enumerate(DILS):
            n = ts // d
            for r in range(d):
                rows = _rows_of(r, n, d)
                c, s = c_ref[rows, :], s_ref[rows, :]
                for which in range(3):
                    parts = [scr.at[which * nch + j][rows, :] for j in (2 * g, 2 * g + 1)]
                    if which < 2:
                        parts = [_rope(x, c, s) for x in parts]
                    outs[which * 3 + g][r] = jnp.concatenate(parts, axis=1).astype(MXU)

    base = (IN_W - 3 * ATT_W) // LANES
    in_specs = [pl.BlockSpec((ts, LANES), lambda i, cb=base + k: (i, cb)) for k in range(3 * nch)]
    in_specs += [pl.BlockSpec((ts, LANES), lambda i: (i, 0))] * 2
    out_specs = [pl.BlockSpec((d, ts // d, ATT_O), lambda i: (0, i, 0)) for _ in range(3) for d in DILS]
    out_shape = [jax.ShapeDtypeStruct((d, S // d, ATT_O), MXU) for _ in range(3) for d in DILS]
    res = pl.pallas_call(body, grid=(S // ts,), in_specs=in_specs, out_specs=out_specs, out_shape=out_shape,
                         scratch_shapes=[pltpu.VMEM((3 * nch, ts, LANES), F32)],
                         compiler_params=_params(("arbitrary",)), name="rope_perm")(*([u] * (3 * nch)), ctab, stab)
    return [[res[which * 3 + g].reshape(S, ATT_O) for g in range(3)] for which in range(3)]


def _rope_unperm_bwd(dqkv, du, ctab, stab, ts=512):
    S = dqkv[0][0].shape[0]
    nch = ATT_W // LANES

    def body(*refs):
        ins, (c_ref, s_ref, _, o_ref, scr) = refs[:9], refs[9:]
        for g, d in enumerate(DILS):
            n = ts // d
            for r in range(d):
                rows = _rows_of(r, n, d)
                c, s = c_ref[rows, :], s_ref[rows, :]
                for which in range(3):
                    v = ins[which * 3 + g][r]
                    for half in range(2):
                        x = v[:, half * LANES:(half + 1) * LANES]
                        scr.at[which * nch + 2 * g + half][rows, :] = _rope_t(x, c, s) if which < 2 else x
        for j in range(3 * nch):
            o_ref[:, j * LANES:(j + 1) * LANES] = scr[j].astype(o_ref.dtype)

    in_specs = [pl.BlockSpec((d, ts // d, ATT_O), lambda i: (0, i, 0)) for _ in range(3) for d in DILS]
    in_specs += [pl.BlockSpec((ts, LANES), lambda i: (i, 0))] * 2 + [pl.BlockSpec(memory_space=pl.ANY)]
    args = [dqkv[which][g].reshape(d, S // d, ATT_O) for which in range(3) for g, d in enumerate(DILS)]
    last = (IN_W - 3 * ATT_W) // (3 * ATT_W)
    return pl.pallas_call(body, grid=(S // ts,), in_specs=in_specs, out_specs=pl.BlockSpec((ts, 3 * ATT_W), lambda i: (i, last)),
                          out_shape=jax.ShapeDtypeStruct((S, IN_W), MXU), scratch_shapes=[pltpu.VMEM((3 * nch, ts, LANES), F32)],
                          input_output_aliases={len(in_specs) - 1: 0},
                          compiler_params=_params(("arbitrary",)), name="rope_unperm_bwd")(*args, ctab, stab, du)


def _band_mask_keys(has_prev):
    r = lax.broadcasted_iota(jnp.int32, (QB, 2 * QB), 0)
    c = lax.broadcasted_iota(jnp.int32, (QB, 2 * QB), 1)
    return ((c < QB) & (c >= r) & has_prev) | ((c >= QB) & (c - QB <= r))


def _band_mask_queries(has_next):
    r = lax.broadcasted_iota(jnp.int32, (2 * QB, 2 * QB), 0)
    c = lax.broadcasted_iota(jnp.int32, (2 * QB, 2 * QB), 1) & (QB - 1)
    return ((r < QB) & (c <= r)) | ((r >= QB) & (c >= r - QB) & has_next)


ASUB = 4
_BIG = pl.BlockSpec((ASUB * QB, ATT_O), lambda b: (b, 0))
_PREV = pl.BlockSpec((QB, ATT_O), lambda b: (jnp.maximum(b * ASUB - 1, 0), 0))


def _sub(ref, j):
    return ref[j * QB:(j + 1) * QB]


def _attn_fwd(g, q, k, v):
    S = q.shape[0]
    nb = S // QB
    nblk = nb // DILS[g]

    def body(q_ref, kc_ref, kp_ref, vc_ref, vp_ref, o_ref, m_ref, l_ref):
        hm_kv, hm_o = _head_masks((2 * QB, ATT_O)), _head_masks((QB, ATT_O))
        for j in range(ASUB):
            ok = _band_mask_keys(((pl.program_id(0) * ASUB + j) & (nblk - 1)) > 0)
            k2 = jnp.concatenate([kp_ref[...] if j == 0 else _sub(kc_ref, j - 1), _sub(kc_ref, j)], axis=0)
            v2 = jnp.concatenate([vp_ref[...] if j == 0 else _sub(vc_ref, j - 1), _sub(vc_ref, j)], axis=0)
            qv = _sub(q_ref, j)
            o_acc = jnp.zeros((QB, ATT_O), F32)
            m_acc = jnp.zeros((QB, ATT_O), F32)
            l_acc = jnp.zeros((QB, ATT_O), F32)
            for h in range(4):
                s = jnp.where(ok, _dot_nt(qv, _only(hm_kv[h], k2)) * ATT_SCALE, NEG)
                m = jnp.max(s, axis=1, keepdims=True)
                p = jnp.exp(s - m)
                o_acc = o_acc + _dot(p.astype(MXU), _only(hm_kv[h], v2))
                m_acc = jnp.where(hm_o[h], m, m_acc)
                l_acc = jnp.where(hm_o[h], jnp.sum(p, axis=1, keepdims=True), l_acc)
            o_ref[j * QB:(j + 1) * QB] = o_acc
            m_ref[j * QB:(j + 1) * QB] = m_acc
            l_ref[j * QB:(j + 1) * QB] = l_acc

    shp = jax.ShapeDtypeStruct((S, ATT_O), F32)
    return pl.pallas_call(body, grid=(nb // ASUB,), in_specs=[_BIG, _BIG, _PREV, _BIG, _PREV],
                          out_specs=[_BIG] * 3, out_shape=[shp, shp, shp], compiler_params=_params(("arbitrary",)),
                          name=f"attn_fwd_{g}")(q, k, k, v, v)


def _natural(ref, d, scr, ts):
    if d == 1:
        return ref[0]
    n = ts // d
    for r in range(d):
        v = ref[r]
        scr.at[0][pl.ds(r, n, stride=d), :] = v[:, 0:LANES]
        scr.at[1][pl.ds(r, n, stride=d), :] = v[:, LANES:2 * LANES]
    return jnp.concatenate([scr[0], scr[1]], axis=1)


def _attn_combine(oml, ts=512):
    S = oml[0][0].shape[0]

    def body(*refs):
        ins, (att_ref, out_ref, lse_ref, scr) = refs[:9], refs[9:]
        o, m, l = [[_natural(ins[3 * g + k], d, scr, ts) for g, d in enumerate(DILS)] for k in range(3)]
        mx = jnp.maximum(jnp.maximum(m[0], m[1]), m[2])
        w = [jnp.exp(m[g] - mx) for g in range(3)]
        den = w[0] * l[0] + w[1] * l[1] + w[2] * l[2]
        out = (w[0] * o[0] + w[1] * o[1] + w[2] * o[2]) / den
        out_ref[...] = out
        att_ref[...] = out.astype(MXU)
        lse_ref[...] = mx + jnp.log(den)

    in_specs = [pl.BlockSpec((d, ts // d, ATT_O), lambda i: (0, i, 0)) for d in DILS for _ in range(3)]
    args = [a.reshape(d, S // d, ATT_O) for d, grp in zip(DILS, oml) for a in grp]
    blk = pl.BlockSpec((ts, ATT_O), lambda i: (i, 0))
    return pl.pallas_call(body, grid=(S // ts,), in_specs=in_specs, out_specs=[blk, blk, blk],
                          out_shape=[jax.ShapeDtypeStruct((S, ATT_O), MXU), jax.ShapeDtypeStruct((S, ATT_O), F32),
                                     jax.ShapeDtypeStruct((S, ATT_O), F32)],
                          scratch_shapes=[pltpu.VMEM((2, ts, LANES), F32)], compiler_params=_params(("arbitrary",)),
                          name="attn_combine")(*args)


def _attn_bwd_prep(datt, o, lse, ts=512):
    S = datt.shape[0]

    def body(da0, da1, o_ref, l0, l1, *rest):
        outs, dl = rest[:9], rest[9]
        prod = jnp.concatenate([da0[...], da1[...]], axis=1) * o_ref[...]
        delta = jnp.zeros((ts, ATT_O), F32)
        for hm in _head_masks((ts, ATT_O)):
            delta = jnp.where(hm, jnp.sum(_only(hm, prod), axis=1, keepdims=True), delta)
        dl[0] = delta[:, 0:LANES]
        dl[1] = delta[:, LANES:2 * LANES]
        for g, d in enumerate(DILS):
            n = ts // d
            for r in range(d):
                rows = _rows_of(r, n, d)
                outs[g][r] = jnp.concatenate([da0[rows, :], da1[rows, :]], axis=1).astype(MXU)
                outs[3 + g][r] = jnp.concatenate([dl.at[0][rows, :], dl.at[1][rows, :]], axis=1)
                outs[6 + g][r] = jnp.concatenate([l0[rows, :], l1[rows, :]], axis=1)

    half = lambda j: pl.BlockSpec((ts, LANES), lambda i: (i, j))
    out_specs = [pl.BlockSpec((d, ts // d, ATT_O), lambda i: (0, i, 0)) for _ in range(3) for d in DILS]
    out_shape = [jax.ShapeDtypeStruct((d, S // d, ATT_O), dt) for dt in (MXU, F32, F32) for d in DILS]
    res = pl.pallas_call(body, grid=(S // ts,), in_specs=[half(0), half(1), pl.BlockSpec((ts, ATT_O), lambda i: (i, 0)), half(0), half(1)],
                         out_specs=out_specs, out_shape=out_shape, scratch_shapes=[pltpu.VMEM((2, ts, LANES), F32)],
                         compiler_params=_params(("arbitrary",)), name="attn_bwd_prep")(datt, datt, o, lse, lse)
    return [[res[k * 3 + g].reshape(S, ATT_O) for g in range(3)] for k in range(3)]


def _head_col(x, h):
    return x[:, h * 64:h * 64 + 1]


def _attn_bwd(g, q, k, v, do, delta, lse):
    S = q.shape[0]
    nb = S // QB
    nblk = nb // DILS[g]

    def body(k_ref, v_ref, qc_ref, qn_ref, doc_ref, don_ref, dlc_ref, dln_ref, lc_ref, ln_ref, dq_ref, dk_ref, dv_ref, dq_scr):
        hms, hmk = _head_masks((2 * QB, ATT_O)), _head_masks((QB, ATT_O))
        first_head = lax.broadcasted_iota(jnp.int32, (2 * QB, 2 * QB), 1) < QB
        first = pl.program_id(0) == 0

        @pl.when(first)
        def _():
            dq_scr[0:QB] = jnp.zeros((QB, ATT_O), F32)

        @pl.when(jnp.logical_not(first))
        def _():
            dq_scr[0:QB] = dq_scr[ASUB * QB:(ASUB + 1) * QB]

        dq_scr[QB:(ASUB + 1) * QB] = jnp.zeros((ASUB * QB, ATT_O), F32)

        def both(cur_ref, nxt_ref, j):
            return jnp.concatenate([_sub(cur_ref, j), nxt_ref[...] if j == ASUB - 1 else _sub(cur_ref, j + 1)], axis=0)

        for j in range(ASUB):
            ok = _band_mask_queries(((pl.program_id(0) * ASUB + j + 1) & (nblk - 1)) > 0)
            q2, do2, dl2, lse2 = both(qc_ref, qn_ref, j), both(doc_ref, don_ref, j), both(dlc_ref, dln_ref, j), both(lc_ref, ln_ref, j)
            kv, vv = _sub(k_ref, j), _sub(v_ref, j)
            dk = jnp.zeros((QB, ATT_O), F32)
            dv = jnp.zeros((QB, ATT_O), F32)
            dq2 = jnp.zeros((2 * QB, ATT_O), F32)
            for h in range(0, 4, 2):
                both_heads = hms[h] | hms[h + 1]
                qp, dop = _only(both_heads, q2), _only(both_heads, do2)
                kp = jnp.concatenate([_only(hmk[h], kv), _only(hmk[h + 1], kv)], axis=0)
                vp = jnp.concatenate([_only(hmk[h], vv), _only(hmk[h + 1], vv)], axis=0)
                lse_p = jnp.where(first_head, _head_col(lse2, h), _head_col(lse2, h + 1))
                dl_p = jnp.where(first_head, _head_col(dl2, h), _head_col(dl2, h + 1))
                p = jnp.where(ok, jnp.exp(_dot_nt(qp, kp) * ATT_SCALE - lse_p), 0.0)
                ds = (p * (_dot_nt(dop, vp) - dl_p)).astype(MXU)
                dvp, dkp = _dot_tn(p.astype(MXU), dop), _dot_tn(ds, qp)
                dv = dv + _only(hmk[h], dvp[0:QB]) + _only(hmk[h + 1], dvp[QB:2 * QB])
                dk = dk + _only(hmk[h], dkp[0:QB]) + _only(hmk[h + 1], dkp[QB:2 * QB])
                dq2 = dq2 + _dot(ds, kp)
            dk_ref[j * QB:(j + 1) * QB] = dk * ATT_SCALE
            dv_ref[j * QB:(j + 1) * QB] = dv
            dq_scr[j * QB:(j + 2) * QB] += dq2
        dq_ref[...] = dq_scr[0:ASUB * QB] * ATT_SCALE

    nxt = pl.BlockSpec((QB, ATT_O), lambda b: (jnp.minimum((b + 1) * ASUB, nb - 1), 0))
    shp = jax.ShapeDtypeStruct((S, ATT_O), F32)
    return pl.pallas_call(body, grid=(nb // ASUB,), in_specs=[_BIG, _BIG, _BIG, nxt, _BIG, nxt, _BIG, nxt, _BIG, nxt], out_specs=[_BIG] * 3,
                          out_shape=[shp, shp, shp], scratch_shapes=[pltpu.VMEM(((ASUB + 1) * QB, ATT_O), F32)],
                          compiler_params=_params(("arbitrary",)), name=f"attn_bwd_{g}")(k, v, q, q, do, do, delta, delta, lse, lse)


def _merge_fwd(x0, u, a2, yb, att, wa, wb, wc, w_out, g_post, ts=512):
    S = x0.shape[0]

    def body(i, g, x_ref, gate_ref, a2_ref, yb_ref, att_ref, wa_ref, wb_ref, wc_ref, wo_ref, gp_ref, mg_ref, y_ref, xo_ref):
        gate = lambda n: jax.nn.sigmoid(gate_ref[:, n * D:(n + 1) * D].astype(F32))
        merged = gate(0) * _dot_nt(a2_ref[...], wa_ref[...])
        merged = merged + gate(1) * _dot_nt(yb_ref[...], wb_ref[...])
        merged = merged + gate(2) * _dot_nt(att_ref[...], wc_ref[...])
        mb = merged.astype(MXU)
        mg_ref[...] = mb
        y = _dot(mb, wo_ref[...])
        y_ref[...] = y
        xo_ref[...] = x_ref[...] + _rms(y, gp_ref[...])[0]

    ins = [("t", x0, D, 0), ("t", u, GATE_W, 0), ("t", a2, POOLW, 0), ("t", yb, POOLW, 0), ("t", att, ATT_O, 0),
           ("w", wa), ("w", wb), ("w", wc), ("w", w_out), ("w", g_post)]
    return _rows_call("merge_fwd", body, S, ts, ins, [("t", D, MXU), ("t", D, F32), ("t", D, F32)])


def _merge_bwd(dx, y1, u, a2, yb, att, merged, wa, wb, wc, w_out, g_post, ts=512):
    S = dx.shape[0]
    last = S // ts - 1

    def body(i, g, dx_ref, y_ref, gate_ref, a2_ref, yb_ref, att_ref, mg_ref, wa_ref, wb_ref, wc_ref, wo_ref, gp_ref,
             dgate_ref, da2_ref, dyb_ref, datt_ref, dgp_ref, dwo_ref, dwa_ref, dwb_ref, dwc_ref, acc_o, acc_a, acc_b, acc_c):
        @pl.when(g == 0)
        def _():
            for acc in (acc_o, acc_a, acc_b, acc_c):
                acc[...] = jnp.zeros_like(acc)

        dxv, y = dx_ref[...], y_ref[...]
        dy, r = _rms_bwd(dxv * gp_ref[...], y)
        _acc(dgp_ref, g, jnp.sum(dxv * (y * r), axis=0, keepdims=True))
        dyb16 = dy.astype(MXU)
        acc_o[...] += _dot_tn(mg_ref[...], dyb16)
        dm = _dot_nt(dyb16, wo_ref[...])
        for n, (src, w_ref, din_ref, acc) in enumerate(((a2_ref, wa_ref, da2_ref, acc_a), (yb_ref, wb_ref, dyb_ref, acc_b),
                                                       (att_ref, wc_ref, datt_ref, acc_c))):
            gt = jax.nn.sigmoid(gate_ref[:, n * D:(n + 1) * D].astype(F32))
            br = _dot_nt(src[...], w_ref[...])
            dgate_ref[:, n * D:(n + 1) * D] = (dm * br * gt * (1.0 - gt)).astype(dgate_ref.dtype)
            dbr = (dm * gt).astype(MXU)
            acc[...] += _dot_tn(dbr, src[...])
            din_ref[...] = _dot(dbr, w_ref[...])

        @pl.when(g == last)
        def _():
            for out, acc in ((dwo_ref, acc_o), (dwa_ref, acc_a), (dwb_ref, acc_b), (dwc_ref, acc_c)):
                out[...] = acc[...].astype(MXU)

    ins = [("t", dx, D, 0), ("t", y1, D, 0), ("t", u, GATE_W, 0), ("t", a2, POOLW, 0), ("t", yb, POOLW, 0), ("t", att, ATT_O, 0),
           ("t", merged, D, 0), ("w", wa), ("w", wb), ("w", wc), ("w", w_out), ("w", g_post)]
    wshapes = [(D, D), (D, POOLW), (D, POOLW), (D, ATT_O)]
    outs = [("c", IN_W, GATE_W, 0, MXU), ("t", POOLW, F32), ("t", POOLW, F32), ("t", ATT_O, F32), ("a", (1, D), F32)]
    outs += [("a", s, MXU) for s in wshapes]
    return _rows_call("merge_bwd", body, S, ts, ins, outs, scratch=[pltpu.VMEM(s, F32) for s in wshapes])


def _prenorm_bwd(name, dx_res, du, wt, x, g_pre, ts=256, lead=0):
    S = x.shape[0]
    N = du.shape[1]

    def body(i, g, dx_ref, du_ref, wt_ref, x_ref, g_ref, o_ref, dg_ref):
        if lead:
            dhv = _dot(du_ref[:, 0:lead], wt_ref[N - lead:N, :]) + _dot(du_ref[:, lead:N], wt_ref[0:N - lead, :])
        else:
            dhv = _dot(du_ref[...], wt_ref[...])
        xv = x_ref[...]
        dxn, r = _rms_bwd(dhv * g_ref[...], xv)
        o_ref[...] = dx_ref[...] + dxn
        _acc(dg_ref, g, jnp.sum(dhv * (xv * r), axis=0, keepdims=True))

    ins = [("t", dx_res, D, 0), ("t", du, N, 0), ("w", wt), ("t", x, D, 0), ("w", g_pre)]
    return _rows_call(name, body, S, ts, ins, [("t", D, F32), ("a", (1, D), F32)])


def _mem_heads(qm, kv_ref):
    out = []
    for h in range(4):
        q = qm[:, h * 128:(h + 1) * 128].astype(MXU)
        k = kv_ref[:, h * 128:(h + 1) * 128]
        v = kv_ref[:, MEM_W + h * 128:MEM_W + (h + 1) * 128]
        sc = _dot_nt(q, k) * MEM_SCALE
        e = jnp.exp(sc - jnp.max(sc, axis=1, keepdims=True))
        out.append((e / jnp.sum(e, axis=1, keepdims=True), q, k, v))
    return out


def _mem_fwd(x1, kv, g_pre, w_mq, w_mo, g_post, ts=512):
    S = x1.shape[0]

    def body(i, g, x_ref, kv_ref, gq_ref, wq_ref, wo_ref, gp_ref, om_ref, y_ref, xo_ref):
        x = x_ref[...]
        hb = _rms(x, gq_ref[...])[0].astype(MXU)
        qm = _dot(hb, wq_ref[...])
        om = jnp.concatenate([_dot(p.astype(MXU), v) for p, _, _, v in _mem_heads(qm, kv_ref)], axis=1).astype(MXU)
        om_ref[...] = om
        y = _dot_nt(om, wo_ref[...])
        y_ref[...] = y
        xo_ref[...] = x + _rms(y, gp_ref[...])[0]

    ins = [("t", x1, D, 0), ("w", kv), ("w", g_pre), ("w", w_mq), ("w", w_mo), ("w", g_post)]
    return _rows_call("mem_fwd", body, S, ts, ins, [("t", MEM_W, MXU), ("t", D, F32), ("t", D, F32)])


def _mem_bwd(dx2, ym, x1, om, kv, g_pre, w_mq, w_mo, g_post, ts=512):
    S = x1.shape[0]
    last = S // ts - 1

    def body(i, g, dx_ref, y_ref, x_ref, om_ref, kv_ref, gq_ref, wq_ref, wo_ref, gp_ref, dxo_ref, dgp_ref, dgq_ref, dkv_ref,
             dwo_ref, dwq_ref, acc_o, acc_q):
        dxv, y, x = dx_ref[...], y_ref[...], x_ref[...]
        dy, r = _rms_bwd(dxv * gp_ref[...], y)
        _acc(dgp_ref, g, jnp.sum(dxv * (y * r), axis=0, keepdims=True))
        dyb = dy.astype(MXU)
        dom = _dot(dyb, wo_ref[...])
        h, r1 = _rms(x, gq_ref[...])
        hb = h.astype(MXU)
        qm = _dot(hb, wq_ref[...])
        dqs = []

        @pl.when(g == 0)
        def _():
            dkv_ref[...] = jnp.zeros_like(dkv_ref)
            acc_o[...] = jnp.zeros_like(acc_o)
            acc_q[...] = jnp.zeros_like(acc_q)

        acc_o[...] += _dot_tn(dyb, om_ref[...])

        for hh, (p, q, k, v) in enumerate(_mem_heads(qm, kv_ref)):
            doh = dom[:, hh * 128:(hh + 1) * 128].astype(MXU)
            dp = _dot_nt(doh, v)
            dsc = (p * (dp - jnp.sum(dp * p, axis=1, keepdims=True)) * MEM_SCALE).astype(MXU)
            dqs.append(_dot(dsc, k))
            dkv_ref[:, hh * 128:(hh + 1) * 128] += _dot_tn(dsc, q)
            dkv_ref[:, MEM_W + hh * 128:MEM_W + (hh + 1) * 128] += _dot_tn(p.astype(MXU), doh)
        dq = jnp.concatenate(dqs, axis=1).astype(MXU)
        acc_q[...] += _dot_tn(hb, dq)
        dh = _dot_nt(dq, wq_ref[...])
        _acc(dgq_ref, g, jnp.sum(dh * (x * r1), axis=0, keepdims=True))
        dxo_ref[...] = dxv + _rms_bwd(dh * gq_ref[...], x)[0]

        @pl.when(g == last)
        def _():
            dwo_ref[...] = acc_o[...].astype(MXU)
            dwq_ref[...] = acc_q[...].astype(MXU)

    ins = [("t", dx2, D, 0), ("t", ym, D, 0), ("t", x1, D, 0), ("t", om, MEM_W, 0), ("w", kv), ("w", g_pre), ("w", w_mq), ("w", w_mo),
           ("w", g_post)]
    outs = [("t", D, F32), ("a", (1, D), F32), ("a", (1, D), F32), ("a", (256, D), F32), ("a", (D, MEM_W), MXU), ("a", (D, MEM_W), MXU)]
    return _rows_call("mem_bwd", body, S, ts, ins, outs, scratch=[pltpu.VMEM((D, MEM_W), F32), pltpu.VMEM((D, MEM_W), F32)])


def _gain_grad(name, dn, x):
    n = x.shape[0]

    def body(i, g, dn_ref, x_ref, o_ref):
        xv = x_ref[...]
        r = lax.rsqrt(jnp.mean(xv * xv, axis=-1, keepdims=True) + EPS)
        o_ref[...] = jnp.sum(dn_ref[...] * (xv * r), axis=0, keepdims=True)

    return _rows_call(name, body, n, n, [("t", dn, D, 0), ("t", x, D, 0)], [("a", (1, D), F32)])[0]


def _ffn_fwd(x2, u3, conv_f, w_down, g_post, ts=512):
    S = x2.shape[0]

    def body(i, g, x_ref, ua_ref, ub_ref, cw_ref, wd_ref, gp_ref, act_ref, y_ref, xo_ref, c_ref, cu):
        @pl.when(g == 0)
        def _():
            cu[...] = jnp.zeros_like(cu)

        ua = ua_ref[...].astype(F32)
        c, _, _ = _conv3(ua, cu[...], cw_ref[...])
        c_ref[...] = c.astype(MXU)
        act = (c * jax.nn.sigmoid(c) * ub_ref[...].astype(F32)).astype(MXU)
        act_ref[...] = act
        y = _dot(act, wd_ref[...])
        y_ref[...] = y
        xo_ref[...] = x_ref[...] + _rms(y, gp_ref[...])[0]
        cu[...] = ua[ts - 8:]

    ins = [("t", x2, D, 0), ("t", u3, D_FF, 0), ("t", u3, D_FF, 1), ("w", conv_f), ("w", w_down), ("w", g_post)]
    return _rows_call("ffn_fwd", body, S, ts, ins, [("t", D_FF, MXU), ("t", D, F32), ("t", D, F32), ("t", D_FF, MXU)],
                      scratch=[pltpu.VMEM((8, D_FF), F32)])


def _ffn_bwd(dx3, y3, u3, c, conv_f, w_down, g_post, ts=256):
    S = dx3.shape[0]

    def body(i, g, dx_ref, y_ref, ua_ref, ub_ref, c_ref, cw_ref, wd_ref, gp_ref, dy_ref, du_ref, dgp_ref, dcw_ref, cdc):
        @pl.when(g == 0)
        def _():
            cdc[...] = jnp.zeros_like(cdc)

        dxv, y = dx_ref[...], y_ref[...]
        dy, r = _rms_bwd(dxv * gp_ref[...], y)
        _acc(dgp_ref, g, jnp.sum(dxv * (y * r), axis=0, keepdims=True))
        dyb = dy.astype(MXU)
        dy_ref[...] = dyb
        dact = _dot_nt(dyb, wd_ref[...])
        ua, c, w = ua_ref[...].astype(F32), c_ref[...].astype(F32), cw_ref[...]
        sg = jax.nn.sigmoid(c)
        du_ref[:, D_FF:2 * D_FF] = (dact * (c * sg)).astype(du_ref.dtype)
        dc = dact * ub_ref[...].astype(F32) * (sg * (1.0 + c * (1.0 - sg)))
        dua, dc1, dc2 = _conv3_t(dc, cdc[...], w, shifted=True)
        du_ref[:, 0:D_FF] = dua.astype(du_ref.dtype)
        dw = jnp.concatenate([jnp.sum(ua * dc2, axis=0, keepdims=True), jnp.sum(ua * dc1, axis=0, keepdims=True),
                              jnp.sum(ua * dc, axis=0, keepdims=True)], axis=0)
        _acc(dcw_ref, g, dw)
        cdc[...] = dc[:8]

    ins = [("t", dx3, D, 0), ("t", y3, D, 0), ("t", u3, D_FF, 0), ("t", u3, D_FF, 1), ("t", c, D_FF, 0), ("w", conv_f),
           ("w", w_down), ("w", g_post)]
    outs = [("t", D, MXU), ("t", 2 * D_FF, MXU), ("a", (1, D), F32), ("a", (3, D_FF), F32)]
    return _rows_call("ffn_bwd", body, S, ts, ins, outs, scratch=[pltpu.VMEM((8, D_FF), F32)], reverse=True)


def _loss_head(x, target, ts=512):
    S = x.shape[0]

    def body(i, g, x_ref, t_ref, dx_ref, acc_ref):
        diff = x_ref[...] - t_ref[...]
        dx_ref[...] = diff * (1.0 / D)
        col = jnp.sum(diff * diff, axis=0, keepdims=True)
        part = col[:, 0:LANES]
        for j in range(1, D // LANES):
            part = part + col[:, j * LANES:(j + 1) * LANES]
        row = lax.broadcasted_iota(jnp.int32, (8, LANES), 0)
        _acc(acc_ref, g, jnp.where(row == 0, jnp.broadcast_to(part, (8, LANES)), 0.0))

    return _rows_call("loss_head", body, S, ts, [("t", x, D, 0), ("t", target, D, 0)], [("t", D, F32), ("a", (8, LANES), F32)])


_OPERAND_NAME = dict(w_in='w_in', w_branch_a='wa', w_branch_b='wb', w_branch_c='wc', w_out='w_out', w_mq='w_mq', w_mkv='w_mkv',
                     w_mo='w_mo', w_up='w_up', w_down='w_down')


def _big_operands(big):
    return {_OPERAND_NAME[n]: a for n, a in big.items()}


def _layer_weights(big, small, l):
    pool_w = small['pool_w'][l].astype(MXU)
    wblk = jnp.zeros((POOLW, POOLW), MXU)
    for g in range(4):
        wblk = lax.dynamic_update_slice(wblk, pool_w[g], (g * 96, g * 96))
    vec = lambda n: small[n][l].reshape(1, -1)
    return dict(
        _big_operands(big),
        wblk=wblk, pool_scale=vec('pool_scale'), conv_b=small['conv_b_w'][l], conv_f=small['conv_ffn_w'][l],
        g_mix_pre=vec('norm_mix_pre'), g_mix_post=vec('norm_mix_post'), g_mem_pre=vec('norm_mem_pre'),
        g_mem_post=vec('norm_mem_post'), g_memkv=vec('norm_memkv'), g_ffn_pre=vec('norm_ffn_pre'), g_ffn_post=vec('norm_ffn_post'))


def _layer_fwd(x0, mem, W, ctab, stab):
    sv = _layer_fwd_mix(x0, W, ctab, stab)
    return _layer_fwd_late(mem, W, sv), sv


def _layer_fwd_mix(x0, W, ctab, stab):
    return _layer_fwd_merge(W, _layer_fwd_branches(x0, W, ctab, stab))


def _layer_fwd_branches(x0, W, ctab, stab):
    sv = dict(x0=x0)
    sv['u'], sv['h1'] = _norm_mm("in_proj", x0, W['g_mix_pre'], W['w_in'], ts=2048, tn=IN_TILE, wt=True, rot=IN_ROT, out_dtype=MXU)
    sv['a2'], sv['yb'] = _poolconv_fwd(sv['u'], W['wblk'], W['pool_scale'], W['conv_b'])
    sv['qkv'] = q3, k3, v3 = _rope_perm(sv['u'], ctab, stab)
    sv['att'], sv['o'], sv['lse'] = _attn_combine([_attn_fwd(g, q3[g], k3[g], v3[g]) for g in range(3)])
    return sv


def _layer_fwd_merge(W, sv):
    sv['merged'], sv['y1'], sv['x1'] = _merge_fwd(sv['x0'], sv['u'], sv['a2'], sv['yb'], sv['att'], W['wa'], W['wb'], W['wc'],
                                                  W['w_out'], W['g_mix_post'])
    return sv


def _layer_fwd_late(mem, W, sv):
    sv['kv'], sv['memn'] = _norm_mm("mem_kv", mem, W['g_memkv'], W['w_mkv'], ts=256, tn=D, out_dtype=MXU)
    sv['om'], sv['ym'], sv['x2'] = _mem_fwd(sv['x1'], sv['kv'], W['g_mem_pre'], W['w_mq'], W['w_mo'], W['g_mem_post'])
    sv['u3'], sv['h3'] = _norm_mm("up_proj", sv['x2'], W['g_ffn_pre'], W['w_up'], ts=2048, tn=1408, wt=True, out_dtype=MXU)
    sv['act'], sv['y3'], x3, sv['c3'] = _ffn_fwd(sv['x2'], sv['u3'], W['conv_f'], W['w_down'], W['g_ffn_post'])
    return x3


def _layer_bwd(dx3, mem, W, sv, ctab, stab):
    dx1, g = _layer_bwd_late(dx3, mem, W, sv)
    dx0, g_mix = _layer_bwd_mix(dx1, W, sv, ctab, stab)
    return dx0, {**g, **g_mix}


def _layer_bwd_late(dx3, mem, W, sv):
    g = {}
    dy3, du3, g['norm_ffn_post'], g['conv_ffn_w'] = _ffn_bwd(dx3, sv['y3'], sv['u3'], sv['c3'], W['conv_f'], W['w_down'], W['g_ffn_post'])
    g['w_down'] = _mm_tn("dw_down", sv['act'], dy3, cap_k=256)
    g['w_up'] = _mm_tn("dw_up", du3, sv['h3'])
    dx2, g['norm_ffn_pre'] = _prenorm_bwd("ffn_pre_bwd", dx3, du3, W['w_up'], sv['x2'], W['g_ffn_pre'], ts=512)
    dx1, g['norm_mem_post'], g['norm_mem_pre'], dkv, g['w_mo'], g['w_mq'] = _mem_bwd(
        dx2, sv['ym'], sv['x1'], sv['om'], sv['kv'], W['g_mem_pre'], W['w_mq'], W['w_mo'], W['g_mem_post'])
    dkvb = dkv.astype(MXU)
    g['w_mkv'] = _mm_tn("dw_mkv", sv['memn'], dkvb)
    g['norm_memkv'] = _gain_grad("memkv_gain", _mm_nt("d_memn", dkvb, W['w_mkv'], ts=256, tn=512), mem)
    return dx1, g


def _layer_bwd_mix(dx1, W, sv, ctab, stab):
    du, g = _layer_bwd_mixers(dx1, W, sv, ctab, stab)
    g['w_in'] = _dw_in(du, sv)
    dx0, g['norm_mix_pre'] = _mix_pre_bwd(dx1, du, W, sv)
    return dx0, g


def _dw_in(du, sv):
    return _mm_tn("dw_in", du, sv['h1'], cap_k=IN_TILE, rot=IN_ROT)


def _mix_pre_bwd(dx1, du, W, sv):
    return _prenorm_bwd("mix_pre_bwd", dx1, du, W['w_in'], sv['x0'], W['g_mix_pre'], lead=GATE_W)


def _layer_bwd_mixers(dx1, W, sv, ctab, stab):
    parts, g = _layer_bwd_merge(dx1, W, sv)
    du, g_br = _layer_bwd_branches(parts, W, sv, ctab, stab)
    return du, {**g, **g_br}


def _layer_bwd_merge(dx1, W, sv):
    g = {}
    du, da2, dyb, datt, g['norm_mix_post'], g['w_out'], g['w_branch_a'], g['w_branch_b'], g['w_branch_c'] = _merge_bwd(
        dx1, sv['y1'], sv['u'], sv['a2'], sv['yb'], sv['att'], sv['merged'], W['wa'], W['wb'], W['wc'], W['w_out'], W['g_mix_post'])
    return (du, da2, dyb, datt), g


def _layer_bwd_branches(parts, W, sv, ctab, stab):
    du, da2, dyb, datt = parts
    g = {}
    du, g['pool_scale'], dwblk, g['conv_b_w'] = _poolconv_bwd(sv['u'], da2, dyb, du, W['wblk'], W['pool_scale'], W['conv_b'])
    g['pool_w'] = jnp.stack([dwblk[k * 96:(k + 1) * 96, k * 96:(k + 1) * 96] for k in range(4)])
    q3, k3, v3 = sv['qkv']
    do3, dl3, lse3 = _attn_bwd_prep(datt, sv['o'], sv['lse'])
    dqkv3 = [_attn_bwd(i, q3[i], k3[i], v3[i], do3[i], dl3[i], lse3[i]) for i in range(3)]
    du = _rope_unperm_bwd([[t[which] for t in dqkv3] for which in range(3)], du, ctab, stab)
    return du, g


def _local_step(x, mem, positions, target, big, small):
    ctab, stab = _rope_tables(positions)
    Ws = [_layer_weights(big[l], small, l) for l in range(DEPTH)]
    saved = []
    for l in range(DEPTH):
        x, sv = _layer_fwd(x, mem, Ws[l], ctab, stab)
        saved.append(sv)
    dx, acc = _loss_head(x, target)
    loss = jnp.sum(acc) * (0.5 / D)
    grads = [None] * DEPTH
    for l in reversed(range(DEPTH)):
        dx, grads[l] = _layer_bwd(dx, mem, Ws[l], saved[l], ctab, stab)
    return loss, dx, grads


_HBM = pl.BlockSpec(memory_space=pl.ANY)
MESH_ID = pl.DeviceIdType.MESH


def _all_gather(name, xs):
    n = len(xs)

    def body(*refs):
        x_refs, out_refs = refs[:n], refs[n:2 * n]
        send_sems, recv_sems, local_sems = refs[2 * n:]
        x, y, c = lax.axis_index("x"), lax.axis_index("y"), lax.axis_index("c")
        me, sibling = (x, y, c), (x, y, 1 - c)
        chips = [(1 - x, y), (x, 1 - y), (1 - x, 1 - y)]

        def slot(a, p):
            return out_refs[a].at[4 * p[0] + 2 * p[1] + p[2]]

        def copy(a, k, block, to, src=None):
            return pltpu.make_async_remote_copy(src_ref=slot(a, block) if src is None else src, dst_ref=slot(a, block),
                                                send_sem=send_sems.at[a, k], recv_sem=recv_sems.at[a, k], device_id=to,
                                                device_id_type=MESH_ID)

        started = []
        for a in range(n):
            mine = pltpu.make_async_copy(x_refs[a], slot(a, me), local_sems.at[a])
            mine.start()
            started.append(mine)
        first = []
        for a in range(n):
            first.append(copy(a, 0, me, sibling, src=x_refs[a]))
            first += [copy(a, 1 + j, me, (*chip, c), src=x_refs[a]) for j, chip in enumerate(chips)]
        for cp in first:
            cp.start()
        passed = []
        for j, chip in enumerate(chips):
            for a in range(n):
                copy(a, 1 + j, (*chip, c), me).wait_recv()
                fw = copy(a, 4 + j, (*chip, c), sibling)
                fw.start()
                passed.append(fw)
        for a in range(n):
            copy(a, 0, sibling, me).wait_recv()
            for j, chip in enumerate(chips):
                copy(a, 4 + j, (*chip, 1 - c), me).wait_recv()
        for cp in first + passed:
            cp.wait_send()
        for mine in started:
            mine.wait()

    return pl.pallas_call(
        body, out_shape=[jax.ShapeDtypeStruct((N_DEV,) + x.shape, x.dtype) for x in xs], in_specs=[_HBM] * n, out_specs=[_HBM] * n,
        scratch_shapes=[pltpu.SemaphoreType.DMA((n, 7)), pltpu.SemaphoreType.DMA((n, 7)), pltpu.SemaphoreType.DMA((n,))],
        name=name)(*xs)


_SEM =pl.BlockSpec(memory_space=pltpu.SEMAPHORE)
_IN_HBM = pl.BlockSpec(memory_space=pltpu.HBM)
_SIDE_EFFECT = pltpu.SideEffectType.DATAFLOW_SIDE_EFFECTING


def _push_copies(src_refs, land_refs, send_sems, recv_sems, per_peer, first=0):
    x, y, c = lax.axis_index("x"), lax.axis_index("y"), lax.axis_index("c")
    me = 4 * x + 2 * y + c
    copies = []
    for r in range(1, N_DEV):
        px, py, pc = x ^ ((r >> 2) & 1), y ^ ((r >> 1) & 1), c ^ (r & 1)
        for a, (s, d) in enumerate(zip(src_refs, land_refs)):
            k = (first + a) * (N_DEV - 1) + r - 1
            copies.append(pltpu.make_async_remote_copy(src_ref=s.at[4 * px + 2 * py + pc] if per_peer else s, dst_ref=d.at[me],
                                                       send_sem=send_sems.at[k], recv_sem=recv_sems.at[k],
                                                       device_id=(px, py, pc), device_id_type=MESH_ID))
    return copies


def _push_start(name, srcs, per_peer, after):
    n = len(srcs)
    lands = [lax.empty((N_DEV,) + (s.shape[1:] if per_peer else s.shape), s.dtype) for s in srcs]

    def body(*refs):
        for cp in _push_copies(refs[:n], refs[n:2 * n], refs[2 * n + 1], refs[2 * n + 2], per_peer):
            cp.start()
        refs[-1][...] = jnp.zeros_like(refs[-1])

    hbm = [pltpu.HBM(a.shape, a.dtype) for a in (*srcs, *lands)]
    sems = pltpu.SemaphoreType.DMA((n * (N_DEV - 1),))
    out = pl.pallas_call(
        body, name=name, out_shape=(sems, sems, *hbm, jax.ShapeDtypeStruct((8, LANES), F32)),
        in_specs=[_IN_HBM] * (2 * n) + [pl.BlockSpec(memory_space=pl.ANY)],
        out_specs=(_SEM, _SEM, *[_IN_HBM] * (2 * n), pl.BlockSpec(memory_space=pltpu.VMEM)),
        input_output_aliases={a: 2 + a for a in range(2 * n)},
        compiler_params=pltpu.CompilerParams(has_side_effects=_SIDE_EFFECT),
    )(*[pltpu.with_memory_space_constraint(a, pltpu.HBM) for a in (*srcs, *lands)], after)
    return out[0], out[1], out[2:2 + n], out[2 + n:2 + 2 * n], out[-1]


def _push_wait(name, started, per_peer, after):
    send_sems, recv_sems, srcs, lands, _, *first = started
    n = len(srcs)

    def body(*refs):
        for cp in _push_copies(refs[:n], refs[n:2 * n], refs[2 * n], refs[2 * n + 1], per_peer, *first):
            cp.wait_send()
            cp.wait_recv()

    out = pl.pallas_call(
        body, name=name, out_shape=[pltpu.HBM(a.shape, a.dtype) for a in (*srcs, *lands)],
        in_specs=[_IN_HBM] * (2 * n) + [_SEM, _SEM, pl.BlockSpec(memory_space=pl.ANY)], out_specs=[_IN_HBM] * (2 * n),
        input_output_aliases={a: a for a in range(2 * n)},
        compiler_params=pltpu.CompilerParams(has_side_effects=_SIDE_EFFECT),
    )(*srcs, *lands, send_sems, recv_sems, after)
    if per_peer:
        return list(zip(out[:n], out[n:]))
    return _with_own(out[n:], out[:n], _my_slot())


def _my_slot():
    return 4 * lax.axis_index("x") + 2 * lax.axis_index("y") + lax.axis_index("c")


def _row_tile(rows, cols, budget):
    if rows * cols * 4 <= budget or rows % 16:
        return rows
    best = 16
    for t in range(16, rows + 1, 16):
        if rows % t == 0 and t * cols * 4 <= budget:
            best = t
    return best


def _slot_total(r_ref, own_ref):
    me = _my_slot()
    g = jnp.where(me == 0, own_ref[...], r_ref[0]).astype(F32)
    for k in range(1, N_DEV):
        g = g + jnp.where(me == k, own_ref[...], r_ref[k]).astype(F32)
    return g


def _sum_slots(name, pushed):
    src, recv = pushed
    _, R, C = recv.shape
    tr = _row_tile(R, C, UPDATE_BLOCK_BYTES)

    def body(r_ref, own_ref, o_ref):
        o_ref[...] = _slot_total(r_ref, own_ref)

    return pl.pallas_call(body, grid=(R // tr,),
                          in_specs=[pl.BlockSpec((N_DEV, tr, C), lambda i: (0, i, 0)), pl.BlockSpec((None, tr, C), lambda i: (_my_slot(), i, 0))],
                          out_specs=pl.BlockSpec((tr, C), lambda i: (i, 0)), out_shape=jax.ShapeDtypeStruct((R, C), F32),
                          compiler_params=_params(("arbitrary",)), name=name)(recv, src)


def _adamw_step(gv, w_ref, m_ref, v_ref, d_ref, mo_ref, vo_ref):
    mn = ADAM_B1 * m_ref[...] + (1.0 - ADAM_B1) * gv
    vn = ADAM_B2 * v_ref[...] + (1.0 - ADAM_B2) * (gv * gv)
    mo_ref[...] = mn
    vo_ref[...] = vn
    c1 = 1.0 - ADAM_B1 ** ADAM_STEP
    c2 = 1.0 - ADAM_B2 ** ADAM_STEP
    d_ref[...] = -ADAM_LR * ((mn / c1) / (jnp.sqrt(vn / c2) + ADAM_EPS) + ADAM_WD * w_ref[...])


def _adamw(name, g, w, m, v):
    shape = w.shape
    R, C = shape[-2], shape[-1]
    view = (-1, R, C)
    L = w.size // (R * C)
    tr = _row_tile(R, C, UPDATE_BLOCK_BYTES)

    def body(g_ref, w_ref, m_ref, v_ref, d_ref, mo_ref, vo_ref):
        _adamw_step(g_ref[...], w_ref, m_ref, v_ref, d_ref, mo_ref, vo_ref)

    blk = pl.BlockSpec((None, tr, C), lambda l, i: (l, i, 0))
    shp = jax.ShapeDtypeStruct((L, R, C), F32)
    outs = pl.pallas_call(body, grid=(L, R // tr), in_specs=[blk, blk, blk, blk], out_specs=[blk, blk, blk], out_shape=[shp, shp, shp],
                          compiler_params=_params(("arbitrary", "arbitrary")), name=name)(*[a.reshape(view) for a in (g, w, m, v)])
    return [o.reshape(shape) for o in outs]


def _sum_adamw(name, pushed, w, m, v):
    L, R, C = w.shape
    tr = _row_tile(R, C, UPDATE_BLOCK_BYTES)
    n_i = R // tr

    def body(*refs):
        shares, (w_ref, m_ref, v_ref, g_ref, d_ref, mo_ref, vo_ref) = refs[:2 * L], refs[2 * L:]
        for k in range(L):
            @pl.when(pl.program_id(0) == k)
            def _(k=k):
                g_ref[...] = _slot_total(shares[2 * k], shares[2 * k + 1])
        _adamw_step(g_ref[...], w_ref, m_ref, v_ref, d_ref, mo_ref, vo_ref)

    def during(k):
        return lambda l, i: jnp.where(l == k, i, jnp.where(l < k, 0, n_i - 1))

    in_specs, operands = [], []
    for k, (src, recv) in enumerate(pushed):
        in_specs += [pl.BlockSpec((N_DEV, tr, C), lambda l, i, at=during(k): (0, at(l, i), 0)),
                     pl.BlockSpec((None, tr, C), lambda l, i, at=during(k): (_my_slot(), at(l, i), 0))]
        operands += [recv, src]
    blk = pl.BlockSpec((None, tr, C), lambda l, i: (l, i, 0))
    shp = jax.ShapeDtypeStruct((L, R, C), F32)
    return pl.pallas_call(body, grid=(L, n_i), in_specs=in_specs + [blk, blk, blk], out_specs=[blk] * 4, out_shape=[shp] * 4,
                          compiler_params=_params(("arbitrary", "arbitrary")), name=name)(*operands, w, m, v)


def _pad_flat(a, n):
    a = a.reshape(-1)
    return jnp.pad(a, (0, n - a.shape[0]))


def _seg(n):
    return -(-n // FLAT_ALIGN) * FLAT_ALIGN


def _to_blocks(full, axis):
    shp = full.shape
    return jnp.moveaxis(full.reshape(shp[:axis] + (N_DEV, shp[axis] // N_DEV) + shp[axis + 1:]), axis, 0)


def _from_blocks(blocks, axis):
    b = jnp.moveaxis(blocks, 0, axis)
    shp = b.shape
    return b.reshape(shp[:axis] + (shp[axis] * shp[axis + 1],) + shp[axis + 2:])


def _as_rows(shard, n):
    return shard.T if SHARD_AXIS[n] == 2 else shard


def _with_own(lands, own, me):
    return [lax.dynamic_update_slice(land, o[None], (me, 0, 0)) for land, o in zip(lands, own)]


def kernel(x, mem, positions, norm_mix_pre, norm_mix_post, w_in, pool_w, pool_scale, conv_b_w, w_branch_a, w_branch_b, w_branch_c, w_out, norm_mem_pre, norm_mem_post, norm_memkv, w_mq, w_mkv, w_mo, norm_ffn_pre, norm_ffn_post, w_up, conv_ffn_w, w_down, loss_target, m_norm_mix_pre, m_norm_mix_post, m_w_in, m_pool_w, m_pool_scale, m_conv_b_w, m_w_branch_a, m_w_branch_b, m_w_branch_c, m_w_out, m_norm_mem_pre, m_norm_mem_post, m_norm_memkv, m_w_mq, m_w_mkv, m_w_mo, m_norm_ffn_pre, m_norm_ffn_post, m_w_up, m_conv_ffn_w, m_w_down, v_norm_mix_pre, v_norm_mix_post, v_w_in, v_pool_w, v_pool_scale, v_conv_b_w, v_w_branch_a, v_w_branch_b, v_w_branch_c, v_w_out, v_norm_mem_pre, v_norm_mem_post, v_norm_memkv, v_w_mq, v_w_mkv, v_w_mo, v_norm_ffn_pre, v_norm_ffn_post, v_w_up, v_conv_ffn_w, v_w_down):
    w = dict(norm_mix_pre=norm_mix_pre, norm_mix_post=norm_mix_post, w_in=w_in, pool_w=pool_w, pool_scale=pool_scale, conv_b_w=conv_b_w, w_branch_a=w_branch_a, w_branch_b=w_branch_b, w_branch_c=w_branch_c, w_out=w_out, norm_mem_pre=norm_mem_pre, norm_mem_post=norm_mem_post, norm_memkv=norm_memkv, w_mq=w_mq, w_mkv=w_mkv, w_mo=w_mo, norm_ffn_pre=norm_ffn_pre, norm_ffn_post=norm_ffn_post, w_up=w_up, conv_ffn_w=conv_ffn_w, w_down=w_down)
    m = dict(norm_mix_pre=m_norm_mix_pre, norm_mix_post=m_norm_mix_post, w_in=m_w_in, pool_w=m_pool_w, pool_scale=m_pool_scale, conv_b_w=m_conv_b_w, w_branch_a=m_w_branch_a, w_branch_b=m_w_branch_b, w_branch_c=m_w_branch_c, w_out=m_w_out, norm_mem_pre=m_norm_mem_pre, norm_mem_post=m_norm_mem_post, norm_memkv=m_norm_memkv, w_mq=m_w_mq, w_mkv=m_w_mkv, w_mo=m_w_mo, norm_ffn_pre=m_norm_ffn_pre, norm_ffn_post=m_norm_ffn_post, w_up=m_w_up, conv_ffn_w=m_conv_ffn_w, w_down=m_w_down)
    v = dict(norm_mix_pre=v_norm_mix_pre, norm_mix_post=v_norm_mix_post, w_in=v_w_in, pool_w=v_pool_w, pool_scale=v_pool_scale, conv_b_w=v_conv_b_w, w_branch_a=v_w_branch_a, w_branch_b=v_w_branch_b, w_branch_c=v_w_branch_c, w_out=v_w_out, norm_mem_pre=v_norm_mem_pre, norm_mem_post=v_norm_mem_post, norm_memkv=v_norm_memkv, w_mq=v_w_mq, w_mkv=v_w_mkv, w_mo=v_w_mo, norm_ffn_pre=v_norm_ffn_pre, norm_ffn_post=v_norm_ffn_post, w_up=v_w_up, conv_ffn_w=v_conv_ffn_w, w_down=v_w_down)

    mix_big = [n for n in BIG if n not in LATE_BIG]
    block = lambda names, l: [_as_rows(w[n][l], n).astype(MXU) for n in names]
    conv = jnp.concatenate([_pad_flat(w[n], _seg(w[n].size)) for n in F32_GATHERED]).reshape(-1, LANES)
    groups = dict(m=MERGE_BIG, b=LATE_BIG, a=mix_big)
    got0 = _all_gather("weights_all_gather_0", block(['w_in'], 0) + [conv])
    conv_all = got0[-1].reshape(N_DEV, -1)
    small, off = {n: w[n] for n in WEIGHTS if n not in SHARD_AXIS}, 0
    for n in F32_GATHERED:
        small[n] = _from_blocks(conv_all[:, off:off + w[n].size].reshape((N_DEV,) + w[n].shape), 2)
        off += _seg(w[n].size)
    whole = lambda names, got: {n: o.reshape(-1, o.shape[-1]) for n, o in zip(names, got)}
    early = (('m', 0), ('b', 0), ('a', 1))
    sent = _push_start("weights_push_start_0", [b for tag, l in early for b in block(groups[tag], l)], False, got0[0])
    pushes, after, lo = {}, sent[4], 0
    for tag, l in early:
        hi = lo + len(groups[tag])
        pushes[tag, l] = (sent[0], sent[1], sent[2][lo:hi], sent[3][lo:hi], after, lo)
        lo = hi

    def arrived(tag, l, done):
        return _big_operands(whole(groups[tag], _push_wait(f"weights_push_wait_{l}{tag}", pushes[tag, l], False, done)))

    ctab, stab = _rope_tables(positions[0])
    W0 = _layer_weights(whole(['w_in'], got0), small, 0)
    sv0 = _layer_fwd_branches(x[0], dict(W0, g_mix_pre=W0['g_mix_pre'] + after[0, 0]), ctab, stab)
    W0.update(arrived('m', 0, sv0['att']))
    sv0 = _layer_fwd_merge(W0, sv0)
    W0.update(arrived('b', 0, sv0['x1']))
    x1 = _layer_fwd_late(mem[0], W0, sv0)
    pushes['b', 1] = _push_start("weights_push_start_1b", block(groups['b'], 1), False, x1)
    W1 = _layer_weights({}, small, 1)
    W1.update(arrived('a', 1, pushes['b', 1][4]))
    sv1 = _layer_fwd_mix(x1, W1, ctab, stab)
    W1.update(arrived('b', 1, sv1['x1']))
    x2 = _layer_fwd_late(mem[0], W1, sv1)
    dx, acc = _loss_head(x2, loss_target[0])
    loss_share = jnp.sum(acc) * (0.5 / D)
    grads = [None] * DEPTH
    dx, grads[1] = _layer_bwd(dx, mem[0], W1, sv1, ctab, stab)
    sent = [None, [grads[1][n].reshape(N_DEV, -1, grads[1][n].shape[-1]) for n in BIG]]
    push_g = _push_start("grads_push_start_1", sent[1], True, dx)
    dx, g_late = _layer_bwd_late(dx, mem[0], dict(W0, g_ffn_post=W0['g_ffn_post'] + push_g[4][0, 0]), sv0)
    sent_late = [g_late[n].reshape(N_DEV, -1, g_late[n].shape[-1]) for n in LATE_BIG]
    push_l = _push_start("grads_push_start_0", sent_late, True, dx)
    parts, g_mix = _layer_bwd_merge(dx, dict(W0, g_mix_post=W0['g_mix_post'] + push_l[4][0, 0]), sv0)
    sent_merge = [g_mix[n].reshape(N_DEV, -1, g_mix[n].shape[-1]) for n in MERGE_BIG]
    push_m = _push_start("grads_push_start_0m", sent_merge, True, parts[0])
    du, g_br = _layer_bwd_branches(parts, dict(W0, pool_scale=W0['pool_scale'] + push_m[4][0, 0]), sv0, ctab, stab)
    g_mix.update(g_br)
    g_mix['w_in'] = _dw_in(du, sv0)
    sent_in = [g_mix['w_in'].reshape(N_DEV, -1, D)]
    push_i = _push_start("grads_push_start_in", sent_in, True, du)
    dx, g_mix['norm_mix_pre'] = _mix_pre_bwd(dx, du, dict(W0, g_mix_pre=W0['g_mix_pre'] + push_i[4][0, 0]), sv0)
    grads[0] = {**g_late, **g_mix}
    recv1 = _push_wait("grads_push_wait_1", push_g, True, dx)
    recv_late = _push_wait("grads_push_wait_0", push_l, True, dx)
    recv_merge = _push_wait("grads_push_wait_0m", push_m, True, dx)

    misc_names = [n for n in WEIGHTS if n not in BIG]
    stacked = {n: jnp.stack([grads[l][n].reshape(small[n].shape[1:]) for l in range(DEPTH)]) for n in misc_names}
    rows = [(_to_blocks(stacked[n], 2) if n in SHARD_AXIS else jnp.broadcast_to(stacked[n][None], (N_DEV,) + stacked[n].shape))
            for n in misc_names]
    segs = [_seg(w[n].size) for n in misc_names]
    misc = jnp.concatenate([jnp.pad(r.reshape(N_DEV, -1), ((0, 0), (0, s - r[0].size))) for r, s in zip(rows, segs)]
                           + [jnp.broadcast_to(loss_share, (N_DEV, FLAT_ALIGN))], axis=1).reshape(N_DEV, -1, LANES)
    push_x = _push_start("grads_push_start_small", [misc], True, dx)
    g_out, shares = {}, {}
    for l, names, recv in ((1, BIG, recv1), (0, LATE_BIG, recv_late), (0, MERGE_BIG, recv_merge)):
        for n, r in zip(names, recv):
            shares[n, l] = r

    swap = lambda a: jnp.swapaxes(a, 1, 2)

    def update(n):
        if n not in BIG:
            return [g_out[n], *_adamw(f"adamw_{n}", g_out[n], w[n], m[n], v[n])]
        of_layers = [shares[n, l] for l in range(DEPTH)]
        if SHARD_AXIS[n] == 1:
            return _sum_adamw(f"adamw_{n}", of_layers, w[n], m[n], v[n])
        if w[n].shape[2] % LANES:
            return [swap(a) for a in _sum_adamw(f"adamw_{n}", of_layers, swap(w[n]), swap(m[n]), swap(v[n]))]
        g = swap(jnp.stack([_sum_slots(f"sum_{n}_{l}", s) for l, s in enumerate(of_layers)]))
        return [g, *_adamw(f"adamw_{n}", g, w[n], m[n], v[n])]

    done = {n: update(n) for n in BIG if n != 'w_in'}
    shares['w_in', 0] = _push_wait("grads_push_wait_in", push_i, True, done[BIG[-1]][1])[0]
    done['w_in'] = update('w_in')
    after_w_in = swap(done['w_in'][1])
    misc_sum = _sum_slots("sum_misc", _push_wait("grads_push_wait_small", push_x, True, after_w_in)[0]).reshape(-1)
    off = 0
    for n, s in zip(misc_names, segs):
        g_out[n] = misc_sum[off:off + w[n].size].reshape(w[n].shape)
        done[n] = update(n)
        off += s
    return (misc_sum[off], dx[None], *[done[n][k] for k in range(4) for n in WEIGHTS])
```
